```python
import jax, jax.numpy as jnp
from jax import lax
import numpy as np

D_MODEL = 1024
BATCH = 8
SEQ = 2048
DEPTH = 4

CONV_WIDTH = D_MODEL // 2
CONV_HEADS = 8
CONV_K = 3
POOL_WIDTH = D_MODEL // 2
POOL_WINDOWS = (2, 4, 8, 16)
POOL_GROUPS = len(POOL_WINDOWS)
POOL_GROUP_DIM = POOL_WIDTH // POOL_GROUPS
MIX_WIDTH = CONV_WIDTH + POOL_WIDTH
IN_COLS = 4 * CONV_WIDTH + 2 * POOL_WIDTH
NORM_EPS = 1e-6

kernel_name = "hybrid_shortconv_pool_parallel_adaln"


def rms_norm(x, g):
    xf = x.astype(jnp.float32)
    y = xf * lax.rsqrt(jnp.mean(xf * xf, axis=-1, keepdims=True) + NORM_EPS)
    return (y * g.astype(jnp.float32)).astype(x.dtype)


def causal_depthwise_conv(z, w):
    C = z.shape[-1]
    return lax.conv_general_dilated(
        z, w[:, None, :].astype(z.dtype),
        window_strides=(1,), padding=[(CONV_K - 1, 0)],
        dimension_numbers=("NWC", "WIO", "NWC"),
        feature_group_count=C)


def causal_multiscale_pool(p, w_pool, pool_scale):
    B, T, _ = p.shape
    pf = p.astype(jnp.float32)
    pos = jnp.arange(1, T + 1, dtype=jnp.float32)[None, :, None]
    outs = []
    for g, w in enumerate(POOL_WINDOWS):
        pg = pf[..., g * POOL_GROUP_DIM:(g + 1) * POOL_GROUP_DIM]
        s = jnp.cumsum(pg, axis=1)
        lag = jnp.pad(s, ((0, 0), (w, 0), (0, 0)))[:, :T]
        mean = (s - lag) / jnp.minimum(pos, float(w))
        outs.append(mean - pg)
    pooled = jnp.stack(outs, axis=2).astype(p.dtype)
    mixed = jnp.einsum("btgc,gcd->btgd", pooled, w_pool)
    return mixed.reshape(B, T, POOL_WIDTH) * pool_scale


def _fwd_setup_inputs(seed: int = 0) -> dict:
    key = jax.random.key(seed)
    ks = jax.random.split(key, 12)
    f32 = jnp.float32
    x = jax.random.normal(ks[0], (BATCH, SEQ, D_MODEL), f32)
    c = jax.random.normal(ks[1], (BATCH, D_MODEL), f32)
    w_ada = jax.random.normal(ks[2], (DEPTH, D_MODEL, 3 * D_MODEL), f32) * D_MODEL ** -0.5
    b_ada = 0.01 * jax.random.normal(ks[3], (DEPTH, 3 * D_MODEL), f32)
    g_pre = 1.0 + 0.02 * jax.random.normal(ks[4], (DEPTH, D_MODEL), f32)
    w_in = jax.random.normal(ks[5], (DEPTH, D_MODEL, IN_COLS), f32) * D_MODEL ** -0.5
    w_conv = jax.random.normal(ks[6], (DEPTH, CONV_K, CONV_WIDTH), f32) * CONV_K ** -0.5
    w_pool = jax.random.normal(ks[7], (DEPTH, POOL_GROUPS, POOL_GROUP_DIM, POOL_GROUP_DIM), f32) * POOL_GROUP_DIM ** -0.5
    pool_scale = 1.0 + 0.1 * jax.random.normal(ks[8], (DEPTH, POOL_WIDTH), f32)
    w_out = jax.random.normal(ks[9], (DEPTH, MIX_WIDTH, D_MODEL), f32) * MIX_WIDTH ** -0.5
    g_post = 1.0 + 0.02 * jax.random.normal(ks[10], (DEPTH, D_MODEL), f32)
    return {"x": x, "c": c, "w_ada": w_ada, "b_ada": b_ada, "g_pre": g_pre, "w_in": w_in,
            "w_conv": w_conv, "w_pool": w_pool, "pool_scale": pool_scale,
            "w_out": w_out, "g_post": g_post}


def _fwd_reference(x, c, w_ada, b_ada, g_pre, w_in, w_conv, w_pool, pool_scale, w_out, g_post):
    B, T, D = x.shape
    c_act = jax.nn.silu(c)
    cw = CONV_WIDTH
    for l in range(DEPTH):
        mod = c_act @ w_ada[l] + b_ada[l]
        shift, scale, gate = jnp.split(mod, 3, axis=-1)
        h = rms_norm(x, g_pre[l]) * (1.0 + scale[:, None, :]) + shift[:, None, :]
        proj = h @ w_in[l]
        u_a = proj[..., 0 * cw:1 * cw]
        b_a = proj[..., 1 * cw:2 * cw]
        c_a = proj[..., 2 * cw:3 * cw]
        gate_a = proj[..., 3 * cw:4 * cw]
        o = 4 * cw
        u_p = proj[..., o:o + POOL_WIDTH]
        gate_p = proj[..., o + POOL_WIDTH:o + 2 * POOL_WIDTH]
        y_a = b_a * causal_depthwise_conv(c_a * u_a, w_conv[l]) * jax.nn.silu(gate_a)
        y_p = causal_multiscale_pool(u_p, w_pool[l], pool_scale[l]) * jax.nn.silu(gate_p)
        y = jnp.concatenate([y_a, y_p], axis=-1) @ w_out[l]
        x = x + gate[:, None, :] * rms_norm(y, g_post[l])
    return x


import jax as _jax
import jax.numpy as _jnp

TWIN_FORMAT = 'train_step'
FWD_PARAMS = ['x', 'c', 'w_ada', 'b_ada', 'g_pre', 'w_in', 'w_conv', 'w_pool', 'pool_scale', 'w_out', 'g_post']
TWIN_WEIGHTS = ['w_ada', 'b_ada', 'g_pre', 'w_in', 'w_conv', 'w_pool', 'pool_scale', 'w_out', 'g_post']
TWIN_DIFF_INPUT = 'x'
TWIN_INPUTS = ['x', 'c', 'w_ada', 'b_ada', 'g_pre', 'w_in', 'w_conv', 'w_pool', 'pool_scale', 'w_out', 'g_post', 'loss_target', 'm_w_ada', 'm_b_ada', 'm_g_pre', 'm_w_in', 'm_w_conv', 'm_w_pool', 'm_pool_scale', 'm_w_out', 'm_g_post', 'v_w_ada', 'v_b_ada', 'v_g_pre', 'v_w_in', 'v_w_conv', 'v_w_pool', 'v_pool_scale', 'v_w_out', 'v_g_post']
TWIN_OUTPUTS = ['loss', 'grad_x', 'grad_w_ada', 'grad_b_ada', 'grad_g_pre', 'grad_w_in', 'grad_w_conv', 'grad_w_pool', 'grad_pool_scale', 'grad_w_out', 'grad_g_post', 'delta_w_ada', 'delta_b_ada', 'delta_g_pre', 'delta_w_in', 'delta_w_conv', 'delta_w_pool', 'delta_pool_scale', 'delta_w_out', 'delta_g_post', 'new_m_w_ada', 'new_m_b_ada', 'new_m_g_pre', 'new_m_w_in', 'new_m_w_conv', 'new_m_w_pool', 'new_m_pool_scale', 'new_m_w_out', 'new_m_g_post', 'new_v_w_ada', 'new_v_b_ada', 'new_v_g_pre', 'new_v_w_in', 'new_v_w_conv', 'new_v_w_pool', 'new_v_pool_scale', 'new_v_w_out', 'new_v_g_post']
TWIN_LEAF_KINDS = {'loss': 'loss', 'grad_x': 'grad_x', 'grad_w_ada': 'grad_w', 'grad_b_ada': 'grad_w', 'grad_g_pre': 'grad_w', 'grad_w_in': 'grad_w', 'grad_w_conv': 'grad_w', 'grad_w_pool': 'grad_w', 'grad_pool_scale': 'grad_w', 'grad_w_out': 'grad_w', 'grad_g_post': 'grad_w', 'delta_w_ada': 'delta_w', 'delta_b_ada': 'delta_w', 'delta_g_pre': 'delta_w', 'delta_w_in': 'delta_w', 'delta_w_conv': 'delta_w', 'delta_w_pool': 'delta_w', 'delta_pool_scale': 'delta_w', 'delta_w_out': 'delta_w', 'delta_g_post': 'delta_w', 'new_m_w_ada': 'new_m', 'new_m_b_ada': 'new_m', 'new_m_g_pre': 'new_m', 'new_m_w_in': 'new_m', 'new_m_w_conv': 'new_m', 'new_m_w_pool': 'new_m', 'new_m_pool_scale': 'new_m', 'new_m_w_out': 'new_m', 'new_m_g_post': 'new_m', 'new_v_w_ada': 'new_v', 'new_v_b_ada': 'new_v', 'new_v_g_pre': 'new_v', 'new_v_w_in': 'new_v', 'new_v_w_conv': 'new_v', 'new_v_w_pool': 'new_v', 'new_v_pool_scale': 'new_v', 'new_v_w_out': 'new_v', 'new_v_g_post': 'new_v'}


def _forward(args):
    return _fwd_reference(*[args[k] for k in FWD_PARAMS])


def _output_shape():
    out = _jax.eval_shape(lambda: _forward(_fwd_setup_inputs(0)))
    return out.shape, out.dtype

N_MICROBATCH = 1
ADAM_LR = 0.001
ADAM_B1 = 0.9
ADAM_B2 = 0.999
ADAM_EPS = 1e-08
ADAM_WD = 0.01
ADAM_STEP = 10
PER_EXAMPLE_BATCH_AXIS = {'x': 0, 'c': 0, 'loss_target': 0}
SHARED_INPUTS = []
_WEIGHT_DTYPES = {'w_ada': _jnp.float32, 'b_ada': _jnp.float32, 'g_pre': _jnp.float32, 'w_in': _jnp.float32, 'w_conv': _jnp.float32, 'w_pool': _jnp.float32, 'pool_scale': _jnp.float32, 'w_out': _jnp.float32, 'g_post': _jnp.float32}
MOMENT_SCALE = {'w_ada': 1.714570e+00, 'b_ada': 3.170460e+00, 'g_pre': 4.307999e-01, 'w_in': 2.796189e-01, 'w_conv': 3.400671e-01, 'w_pool': 1.566136e-01, 'pool_scale': 1.808928e-01, 'w_out': 3.067608e-01, 'g_post': 7.402807e+00}


def _to_microbatches(a, axis):
    t = _jnp.moveaxis(a, axis, 0)
    t = t.reshape((N_MICROBATCH, t.shape[0] // N_MICROBATCH) + t.shape[1:])
    return _jnp.moveaxis(t, 1, axis + 1)


def setup_inputs(seed: int = 0) -> dict:
    inp = _fwd_setup_inputs(seed)
    key = _jax.random.fold_in(_jax.random.key(seed), 7919)
    shape, _ = _output_shape()
    out = dict(inp)
    out["loss_target"] = _jax.random.normal(_jax.random.fold_in(key, 0), shape, _jnp.float32)
    for i, name in enumerate(TWIN_WEIGHTS):
        w = inp[name].astype(_jnp.float32)
        if MOMENT_SCALE is None:
            s = _jnp.sqrt(_jnp.mean(_jnp.square(w)) + 1e-30)
        else:
            s = MOMENT_SCALE[name]
        km, kv = _jax.random.split(_jax.random.fold_in(key, i + 1))
        out[name] = w
        out["m_" + name] = s * _jax.random.normal(km, w.shape, _jnp.float32)
        out["v_" + name] = (s * s) * _jax.random.uniform(kv, w.shape, _jnp.float32, 0.5, 1.5)
    if N_MICROBATCH > 1:
        for name, axis in PER_EXAMPLE_BATCH_AXIS.items():
            out[name] = _to_microbatches(out[name], axis)
    return {'x': out['x'], 'c': out['c'], 'w_ada': out['w_ada'], 'b_ada': out['b_ada'], 'g_pre': out['g_pre'], 'w_in': out['w_in'], 'w_conv': out['w_conv'], 'w_pool': out['w_pool'], 'pool_scale': out['pool_scale'], 'w_out': out['w_out'], 'g_post': out['g_post'], 'loss_target': out['loss_target'], 'm_w_ada': out['m_w_ada'], 'm_b_ada': out['m_b_ada'], 'm_g_pre': out['m_g_pre'], 'm_w_in': out['m_w_in'], 'm_w_conv': out['m_w_conv'], 'm_w_pool': out['m_w_pool'], 'm_pool_scale': out['m_pool_scale'], 'm_w_out': out['m_w_out'], 'm_g_post': out['m_g_post'], 'v_w_ada': out['v_w_ada'], 'v_b_ada': out['v_b_ada'], 'v_g_pre': out['v_g_pre'], 'v_w_in': out['v_w_in'], 'v_w_conv': out['v_w_conv'], 'v_w_pool': out['v_w_pool'], 'v_pool_scale': out['v_pool_scale'], 'v_w_out': out['v_w_out'], 'v_g_post': out['v_g_post']}


def _loss(weights, diff, rest, loss_target):
    with _jax.named_scope("forward"):
        args = {**rest, TWIN_DIFF_INPUT: diff, **{k: w.astype(_WEIGHT_DTYPES[k]) for k, w in weights.items()}}
        y = _forward(args)
    with _jax.named_scope("loss_head"):
        err = _jnp.square(y.astype(_jnp.float32) - loss_target)
        return 0.5 * _jnp.sum(_jnp.mean(err, axis=-1)) if err.ndim else 0.5 * err


def _adamw(w, g, m, v):
    m = ADAM_B1 * m + (1.0 - ADAM_B1) * g
    v = ADAM_B2 * v + (1.0 - ADAM_B2) * _jnp.square(g)
    m_hat = m / (1.0 - ADAM_B1 ** ADAM_STEP)
    v_hat = v / (1.0 - ADAM_B2 ** ADAM_STEP)
    delta = -ADAM_LR * (m_hat / (_jnp.sqrt(v_hat) + ADAM_EPS) + ADAM_WD * w)
    return delta, m, v


def reference(x, c, w_ada, b_ada, g_pre, w_in, w_conv, w_pool, pool_scale, w_out, g_post, loss_target, m_w_ada, m_b_ada, m_g_pre, m_w_in, m_w_conv, m_w_pool, m_pool_scale, m_w_out, m_g_post, v_w_ada, v_b_ada, v_g_pre, v_w_in, v_w_conv, v_w_pool, v_pool_scale, v_w_out, v_g_post):
    given = dict(x=x, c=c, w_ada=w_ada, b_ada=b_ada, g_pre=g_pre, w_in=w_in, w_conv=w_conv, w_pool=w_pool, pool_scale=pool_scale, w_out=w_out, g_post=g_post, loss_target=loss_target, m_w_ada=m_w_ada, m_b_ada=m_b_ada, m_g_pre=m_g_pre, m_w_in=m_w_in, m_w_conv=m_w_conv, m_w_pool=m_w_pool, m_pool_scale=m_pool_scale, m_w_out=m_w_out, m_g_post=m_g_post, v_w_ada=v_w_ada, v_b_ada=v_b_ada, v_g_pre=v_g_pre, v_w_in=v_w_in, v_w_conv=v_w_conv, v_w_pool=v_w_pool, v_pool_scale=v_pool_scale, v_w_out=v_w_out, v_g_post=v_g_post)
    weights = {n: given[n] for n in TWIN_WEIGHTS}
    shared = {n: given[n] for n in SHARED_INPUTS}
    per_example = {n: given[n] for n in ['x', 'c']}
    grad_fn = _jax.value_and_grad(_loss, argnums=(0, 1))

    def one_microbatch(ex, loss_target):
        ex = dict(ex)
        diff = ex.pop(TWIN_DIFF_INPUT)
        return grad_fn(weights, diff, {**shared, **ex}, loss_target)

    if N_MICROBATCH == 1:
        loss, (grad_w, grad_x) = one_microbatch(per_example, given["loss_target"])
    else:
        def body(carry, xs):
            loss_sum, grad_sum = carry
            l_k, (gw_k, gx_k) = one_microbatch(xs[0], xs[1])
            with _jax.named_scope("update"):
                return (loss_sum + l_k, _jax.tree.map(_jnp.add, grad_sum, gw_k)), gx_k

        init = (_jnp.zeros((), _jnp.float32), _jax.tree.map(_jnp.zeros_like, weights))
        (loss, grad_w), grad_x = _jax.lax.scan(body, init, (per_example, given["loss_target"]))
    with _jax.named_scope("update"):
        delta_w, new_m, new_v = {}, {}, {}
        for n in TWIN_WEIGHTS:
            delta_w[n], new_m[n], new_v[n] = _adamw(weights[n], grad_w[n], given["m_" + n], given["v_" + n])
    return (loss, grad_x, *[grad_w[n] for n in TWIN_WEIGHTS], *[delta_w[n] for n in TWIN_WEIGHTS],
            *[new_m[n] for n in TWIN_WEIGHTS], *[new_v[n] for n in TWIN_WEIGHTS])
```

```python
import jax
import jax.numpy as jnp
from jax import lax
from jax.experimental import pallas as pl
from jax.experimental.pallas import tpu as pltpu

F32 = jnp.float32
BF16 = jnp.bfloat16

D_MODEL = 1024
DEPTH = 4
CONV_W = 512
POOL_W = 512
POOL_WINDOWS = (2, 4, 8, 16)
GROUP_D = 128
IN_COLS = 4 * CONV_W + 2 * POOL_W
NORM_EPS = 1e-6

ADAM_LR = 0.001
ADAM_B1 = 0.9
ADAM_B2 = 0.999
ADAM_EPS = 1e-08
ADAM_WD = 0.01
ADAM_STEP = 10

N_DEV = 8
N_CHIP = 4
MESH = pl.DeviceIdType.MESH
W_IN_SHARD = IN_COLS // N_DEV
W_OUT_SHARD = D_MODEL // N_DEV
POOL_ROWS = len(POOL_WINDOWS) * GROUP_D
POOL_SHARD = POOL_ROWS // N_DEV

SUBLANES = 8
LANES = 128
VMEM_LIMIT_BYTES = 56 * 1024 * 1024
ROW_TILE = 256
POOL_HALO = 16
CONV_HALO = SUBLANES

SLAB_COLS = 3 * D_MODEL + D_MODEL + D_MODEL + POOL_W + 3 * CONV_W


def _params(**kw):
    return pltpu.CompilerParams(vmem_limit_bytes=VMEM_LIMIT_BYTES, **kw)


def _sigmoid(v):
    return 1.0 / (1.0 + jnp.exp(-v))


def _dot(a, b):
    return jnp.dot(a, b, preferred_element_type=F32)


def _dot_tn(a, b):
    return lax.dot_general(a, b, (((0,), (0,)), ((), ())), preferred_element_type=F32)


def _dot_nt(a, b):
    return lax.dot_general(a, b, (((1,), (1,)), ((), ())), preferred_element_type=F32)


def _rows_from_before(v, k):
    return pltpu.roll(v, k, 0)


def _rows_from_after(v, k):
    return pltpu.roll(v, v.shape[0] - k, 0)


def _window_counts(t0, rows):
    return (lax.broadcasted_iota(jnp.int32, (rows, 1), 0) + (t0 + 1)).astype(F32)


def _split_proj(p32):
    cw = CONV_W
    return (p32[:, 0 * cw:1 * cw], p32[:, 1 * cw:2 * cw], p32[:, 2 * cw:3 * cw], p32[:, 3 * cw:4 * cw],
            p32[:, 4 * cw:4 * cw + POOL_W], p32[:, 4 * cw + POOL_W:])


def _mixer_forward(p32, z_halo, up_halo, cps, wpool_ref, t0):
    tm = p32.shape[0]
    u_a, b_a, c_a, g_a, u_p, g_p = _split_proj(p32)
    w0, w1, w2, ps = cps[0:1, :], cps[1:2, :], cps[2:3, :], cps[3:4, :]
    z = c_a * u_a
    zcat = jnp.concatenate([z_halo, z], axis=0)
    z1 = _rows_from_before(zcat, 1)[CONV_HALO:]
    z2 = _rows_from_before(zcat, 2)[CONV_HALO:]
    conv = w0 * z2 + w1 * z1 + w2 * z
    sig_a = _sigmoid(g_a)
    silu_a = g_a * sig_a
    y_a = b_a * conv * silu_a

    pcat = jnp.concatenate([up_halo, u_p], axis=0)
    counts = _window_counts(t0, tm)
    pooled, mixed, inv = [], [], []
    for g, w in enumerate(POOL_WINDOWS):
        cols = slice(g * GROUP_D, (g + 1) * GROUP_D)
        s = pcat[:, cols]
        step = 1
        while step < w:
            s = s + _rows_from_before(s, step)
            step *= 2
        inv_g = 1.0 / jnp.minimum(counts, float(w))
        pooled_g = s[POOL_HALO:] * inv_g - u_p[:, cols]
        pooled.append(pooled_g)
        inv.append(inv_g)
        mixed.append(_dot(pooled_g.astype(BF16), wpool_ref[g]))
    mixed = jnp.concatenate(mixed, axis=1)
    sig_p = _sigmoid(g_p)
    silu_p = g_p * sig_p
    y_p = mixed * ps * silu_p
    ycat = jnp.concatenate([y_a, y_p], axis=1).astype(BF16)
    return dict(u_a=u_a, b_a=b_a, c_a=c_a, g_a=g_a, u_p=u_p, g_p=g_p, z=z, z1=z1, z2=z2, conv=conv, sig_a=sig_a,
                silu_a=silu_a, pooled=pooled, inv=inv, mixed=mixed, sig_p=sig_p, silu_p=silu_p, ycat=ycat,
                w0=w0, w1=w1, w2=w2, ps=ps)


def _layer_spec(shape, layer):
    nd = len(shape)
    return pl.BlockSpec((None,) + tuple(shape[1:]), lambda i, _l=layer, _n=nd: (_l,) + (0,) * (_n - 1))


def _forward_layer(layer, x, vec, cps, wpool, win_full, wout_full):
    t_len = x.shape[0]
    n_tiles = t_len // ROW_TILE

    def body(x_ref, vec_ref, cps_ref, wpool_ref, win_ref, wout_ref, xo_ref, proj_ref, y_ref, zc_ref, pc_ref):
        i = pl.program_id(0)

        @pl.when(i == 0)
        def _():
            zc_ref[...] = jnp.zeros_like(zc_ref)
            pc_ref[...] = jnp.zeros_like(pc_ref)

        x_t = x_ref[...]
        shift, scale, gate = vec_ref[0:1, :], vec_ref[1:2, :], vec_ref[2:3, :]
        g_pre, g_post = vec_ref[3:4, :], vec_ref[4:5, :]
        rx = lax.rsqrt(jnp.mean(x_t * x_t, axis=-1, keepdims=True) + NORM_EPS)
        h = (x_t * rx) * g_pre * (1.0 + scale) + shift
        proj_b = _dot(h.astype(BF16), win_ref[...]).astype(BF16)
        proj_ref[...] = proj_b
        mx = _mixer_forward(proj_b.astype(F32), zc_ref[...], pc_ref[...], cps_ref[...], wpool_ref, i * ROW_TILE)
        zc_ref[...] = mx["z"][ROW_TILE - CONV_HALO:]
        pc_ref[...] = mx["u_p"][ROW_TILE - POOL_HALO:]
        y_b = _dot(mx["ycat"], wout_ref[...]).astype(BF16)
        y_ref[...] = y_b
        y_t = y_b.astype(F32)
        ry = lax.rsqrt(jnp.mean(y_t * y_t, axis=-1, keepdims=True) + NORM_EPS)
        xo_ref[...] = x_t + gate * (y_t * ry * g_post)

    row = lambda cols: pl.BlockSpec((ROW_TILE, cols), lambda i: (i, 0))
    return pl.pallas_call(
        body, name=f"forward_layer_{layer}", grid=(n_tiles,),
        in_specs=[row(D_MODEL), _layer_spec(vec.shape, layer), _layer_spec(cps.shape, layer),
                  _layer_spec(wpool.shape, layer), _layer_spec(win_full.shape, layer),
                  _layer_spec(wout_full.shape, layer)],
        out_specs=[row(D_MODEL), row(IN_COLS), row(D_MODEL)],
        out_shape=[jax.ShapeDtypeStruct((t_len, D_MODEL), F32), jax.ShapeDtypeStruct((t_len, IN_COLS), BF16),
                   jax.ShapeDtypeStruct((t_len, D_MODEL), BF16)],
        scratch_shapes=[pltpu.VMEM((CONV_HALO, CONV_W), F32), pltpu.VMEM((POOL_HALO, POOL_W), F32)],
        compiler_params=_params(dimension_semantics=("arbitrary",)),
    )(x, vec, cps, wpool, win_full, wout_full)


def _loss_head(x_final, target):
    t_len = x_final.shape[0]
    n_tiles = t_len // ROW_TILE

    def body(x_ref, t_ref, dx_ref, loss_ref):
        @pl.when(pl.program_id(0) == 0)
        def _():
            loss_ref[...] = jnp.zeros_like(loss_ref)

        err = x_ref[...] - t_ref[...]
        dx_ref[...] = err * (1.0 / D_MODEL)
        loss_ref[...] += jnp.sum(err * err) * (0.5 / D_MODEL)

    row = pl.BlockSpec((ROW_TILE, D_MODEL), lambda i: (i, 0))
    return pl.pallas_call(
        body, name="loss_head", grid=(n_tiles,), in_specs=[row, row],
        out_specs=[row, pl.BlockSpec((SUBLANES, LANES), lambda i: (0, 0))],
        out_shape=[jax.ShapeDtypeStruct((t_len, D_MODEL), F32), jax.ShapeDtypeStruct((SUBLANES, LANES), F32)],
        compiler_params=_params(dimension_semantics=("arbitrary",)),
    )(x_final, target)


def _backward_mixers(layer, dxo, y, proj, vec, cps, wpool, wout_full, gwout_all, gwpool_all):
    t_len = dxo.shape[0]
    n_tiles = t_len // ROW_TILE
    halo_per_tile = ROW_TILE // POOL_HALO

    def body(dxo_ref, y_ref, proj_ref, projh_ref, vec_ref, cps_ref, wpool_ref, wout_ref, *rest):
        if gwout_all is None:
            (dproj_ref, gwout_ref, gwpool_ref, dcps_ref, dvec_ref,
             gwout_acc, gwpool_acc, dcc_ref, qc_ref) = rest
        else:
            (_, _, dproj_ref, gwout_ref, gwpool_ref, dcps_ref, dvec_ref,
             gwout_acc, gwpool_acc, dcc_ref, qc_ref) = rest
        i = pl.program_id(0)
        tile = n_tiles - 1 - i

        @pl.when(i == 0)
        def _():
            gwout_acc[...] = jnp.zeros_like(gwout_acc)
            gwpool_acc[...] = jnp.zeros_like(gwpool_acc)
            dcps_ref[...] = jnp.zeros_like(dcps_ref)
            dvec_ref[...] = jnp.zeros_like(dvec_ref)
            dcc_ref[...] = jnp.zeros_like(dcc_ref)
            qc_ref[...] = jnp.zeros_like(qc_ref)

        gate, g_post = vec_ref[2:3, :], vec_ref[4:5, :]
        dxo_t = dxo_ref[...]
        y_t = y_ref[...].astype(F32)
        ry = lax.rsqrt(jnp.mean(y_t * y_t, axis=-1, keepdims=True) + NORM_EPS)
        yh = y_t * ry
        dxy = dxo_t * yh
        dvec_ref[0:1, :] += jnp.sum(dxy * g_post, axis=0, keepdims=True)
        dvec_ref[1:2, :] += jnp.sum(dxy * gate, axis=0, keepdims=True)
        dyh = dxo_t * (gate * g_post)
        dy_b = (ry * (dyh - yh * jnp.mean(dyh * yh, axis=-1, keepdims=True))).astype(BF16)

        halo = jnp.where(tile > 0, projh_ref[...].astype(F32), 0.0)
        hu_a, _, hc_a, _, hu_p, _ = _split_proj(halo)
        z_halo = (hc_a * hu_a)[POOL_HALO - CONV_HALO:]
        mx = _mixer_forward(proj_ref[...].astype(F32), z_halo, hu_p, cps_ref[...], wpool_ref, tile * ROW_TILE)

        gwout_acc[...] += _dot_tn(mx["ycat"], dy_b)
        dycat = _dot_nt(dy_b, wout_ref[...])
        dy_a, dy_p = dycat[:, :CONV_W], dycat[:, CONV_W:]

        t_a = dy_a * mx["silu_a"]
        db_a = t_a * mx["conv"]
        dconv = t_a * mx["b_a"]
        dg_a = dy_a * mx["b_a"] * mx["conv"] * (mx["sig_a"] * (1.0 + mx["g_a"] * (1.0 - mx["sig_a"])))
        dccat = jnp.concatenate([dconv, dcc_ref[...]], axis=0)
        dc1 = _rows_from_after(dccat, 1)[:ROW_TILE]
        dc2 = _rows_from_after(dccat, 2)[:ROW_TILE]
        dz = mx["w2"] * dconv + mx["w1"] * dc1 + mx["w0"] * dc2
        dcc_ref[...] = dconv[:CONV_HALO]
        dcps_ref[0:1, :] += jnp.sum(dconv * mx["z2"], axis=0, keepdims=True)
        dcps_ref[1:2, :] += jnp.sum(dconv * mx["z1"], axis=0, keepdims=True)
        dcps_ref[2:3, :] += jnp.sum(dconv * mx["z"], axis=0, keepdims=True)
        du_a = dz * mx["c_a"]
        dc_a = dz * mx["u_a"]

        t_p = dy_p * mx["silu_p"]
        dcps_ref[3:4, :] += jnp.sum(t_p * mx["mixed"], axis=0, keepdims=True)
        dmixed = (t_p * mx["ps"]).astype(BF16)
        dg_p = dy_p * mx["mixed"] * mx["ps"] * (mx["sig_p"] * (1.0 + mx["g_p"] * (1.0 - mx["sig_p"])))
        du_p, q_head = [], []
        for g, w in enumerate(POOL_WINDOWS):
            cols = slice(g * GROUP_D, (g + 1) * GROUP_D)
            dm_g = dmixed[:, cols]
            dpooled_g = _dot_nt(dm_g, wpool_ref[g])
            gwpool_acc[g] += _dot_tn(mx["pooled"][g].astype(BF16), dm_g)
            q_g = dpooled_g * mx["inv"][g]
            q_head.append(q_g[:POOL_HALO])
            s = jnp.concatenate([q_g, qc_ref[:, cols]], axis=0)
            step = 1
            while step < w:
                s = s + _rows_from_after(s, step)
                step *= 2
            du_p.append(s[:ROW_TILE] - dpooled_g)
        qc_ref[...] = jnp.concatenate(q_head, axis=1)
        dproj_ref[...] = jnp.concatenate([du_a, db_a, dc_a, dg_a] + du_p + [dg_p], axis=1).astype(BF16)

        @pl.when(i == n_tiles - 1)
        def _():
            gwout_ref[...] = gwout_acc[...].astype(BF16)
            gwpool_ref[...] = gwpool_acc[...].astype(BF16)

    rev = lambda cols: pl.BlockSpec((ROW_TILE, cols), lambda i: (n_tiles - 1 - i, 0))
    halo_spec = pl.BlockSpec(
        (POOL_HALO, IN_COLS), lambda i: (jnp.maximum((n_tiles - 1 - i) * halo_per_tile - 1, 0), 0))
    const = lambda shape: pl.BlockSpec(shape, lambda i, _n=len(shape): (0,) * _n)
    gwout_shape = (DEPTH, D_MODEL, D_MODEL)
    gwpool_shape = (DEPTH, len(POOL_WINDOWS), GROUP_D, GROUP_D)
    in_specs = [rev(D_MODEL), rev(D_MODEL), rev(IN_COLS), halo_spec, _layer_spec(vec.shape, layer),
                _layer_spec(cps.shape, layer), _layer_spec(wpool.shape, layer), _layer_spec(wout_full.shape, layer)]
    args = [dxo, y, proj, proj, vec, cps, wpool, wout_full]
    aliases = {}
    if gwout_all is not None:
        in_specs += [pl.BlockSpec(memory_space=pl.ANY), pl.BlockSpec(memory_space=pl.ANY)]
        args += [gwout_all, gwpool_all]
        aliases = {len(args) - 2: 1, len(args) - 1: 2}
    return pl.pallas_call(
        body, name=f"backward_mixers_{layer}", grid=(n_tiles,), in_specs=in_specs,
        out_specs=[rev(IN_COLS), _layer_spec(gwout_shape, layer), _layer_spec(gwpool_shape, layer),
                   const((SUBLANES, CONV_W)), const((SUBLANES, D_MODEL))],
        out_shape=[jax.ShapeDtypeStruct((t_len, IN_COLS), BF16), jax.ShapeDtypeStruct(gwout_shape, BF16),
                   jax.ShapeDtypeStruct(gwpool_shape, BF16), jax.ShapeDtypeStruct((SUBLANES, CONV_W), F32),
                   jax.ShapeDtypeStruct((SUBLANES, D_MODEL), F32)],
        scratch_shapes=[pltpu.VMEM((D_MODEL, D_MODEL), F32), pltpu.VMEM(gwpool_shape[1:], F32),
                        pltpu.VMEM((CONV_HALO, CONV_W), F32), pltpu.VMEM((POOL_HALO, POOL_W), F32)],
        input_output_aliases=aliases,
        compiler_params=_params(dimension_semantics=("arbitrary",)),
    )(*args)


def _backward_input_proj(layer, x, dproj, dxo, vec, win_full, gwin_all):
    t_len = x.shape[0]
    n_tiles = t_len // ROW_TILE

    def body(x_ref, dproj_ref, dxo_ref, vec_ref, win_ref, *rest):
        dx_ref, gwin_ref, dvec_ref, gwin_acc = rest[-4:]
        i = pl.program_id(0)

        @pl.when(i == 0)
        def _():
            gwin_acc[...] = jnp.zeros_like(gwin_acc)
            dvec_ref[...] = jnp.zeros_like(dvec_ref)

        shift, scale, g_pre = vec_ref[0:1, :], vec_ref[1:2, :], vec_ref[3:4, :]
        x_t = x_ref[...]
        rx = lax.rsqrt(jnp.mean(x_t * x_t, axis=-1, keepdims=True) + NORM_EPS)
        xn = x_t * rx
        mod_scale = 1.0 + scale
        h_b = (xn * g_pre * mod_scale + shift).astype(BF16)
        dproj_t = dproj_ref[...]
        gwin_acc[...] += _dot_tn(h_b, dproj_t)
        dh = _dot_nt(dproj_t, win_ref[...])
        dvec_ref[0:1, :] += jnp.sum(dh, axis=0, keepdims=True)
        dhx = dh * xn
        dvec_ref[1:2, :] += jnp.sum(dhx * g_pre, axis=0, keepdims=True)
        dvec_ref[2:3, :] += jnp.sum(dhx * mod_scale, axis=0, keepdims=True)
        dxn = dh * (g_pre * mod_scale)
        dx_ref[...] = dxo_ref[...] + rx * (dxn - xn * jnp.mean(dxn * xn, axis=-1, keepdims=True))

        @pl.when(i == n_tiles - 1)
        def _():
            gwin_ref[...] = gwin_acc[...].astype(BF16)

    row = lambda cols: pl.BlockSpec((ROW_TILE, cols), lambda i: (i, 0))
    gwin_shape = (DEPTH, D_MODEL, IN_COLS)
    in_specs = [row(D_MODEL), row(IN_COLS), row(D_MODEL), _layer_spec(vec.shape, layer),
                _layer_spec(win_full.shape, layer)]
    args = [x, dproj, dxo, vec, win_full]
    aliases = {}
    if gwin_all is not None:
        in_specs.append(pl.BlockSpec(memory_space=pl.ANY))
        args.append(gwin_all)
        aliases = {len(args) - 1: 1}
    return pl.pallas_call(
        body, name=f"backward_input_proj_{layer}", grid=(n_tiles,), in_specs=in_specs,
        out_specs=[row(D_MODEL), _layer_spec(gwin_shape, layer),
                   pl.BlockSpec((SUBLANES, D_MODEL), lambda i: (0, 0))],
        out_shape=[jax.ShapeDtypeStruct((t_len, D_MODEL), F32), jax.ShapeDtypeStruct(gwin_shape, BF16),
                   jax.ShapeDtypeStruct((SUBLANES, D_MODEL), F32)],
        scratch_shapes=[pltpu.VMEM((D_MODEL, IN_COLS), F32)],
        input_output_aliases=aliases,
        compiler_params=_params(dimension_semantics=("arbitrary",)),
    )(*args)


def _mesh_position():
    return lax.axis_index("x"), lax.axis_index("y"), lax.axis_index("c")


def _block_id(x, y, c):
    return 4 * x + 2 * y + c


def _all_gather_small(v, name):
    def body(v_ref, out_ref, send_sems, recv_sems):
        x, y, c = _mesh_position()
        me = _block_id(x, y, c)
        out_ref[me] = v_ref[...]
        sends = []
        for k in range(1, N_DEV):
            px, py, pc = x ^ ((k >> 2) & 1), y ^ ((k >> 1) & 1), c ^ (k & 1)
            send = pltpu.make_async_remote_copy(
                src_ref=v_ref, dst_ref=out_ref.at[me], send_sem=send_sems.at[k - 1], recv_sem=recv_sems.at[k - 1],
                device_id=(px, py, pc), device_id_type=MESH)
            send.start()
            sends.append((send, _block_id(px, py, pc)))
        for k, (send, peer) in enumerate(sends):
            pltpu.make_async_remote_copy(
                src_ref=v_ref, dst_ref=out_ref.at[peer], send_sem=send_sems.at[k], recv_sem=recv_sems.at[k],
                device_id=(x, y, c), device_id_type=MESH).wait_recv()
        for send, _ in sends:
            send.wait_send()

    vmem = pl.BlockSpec(memory_space=pltpu.VMEM)
    return pl.pallas_call(
        body, name=name, in_specs=[vmem], out_specs=vmem,
        out_shape=jax.ShapeDtypeStruct((N_DEV,) + v.shape, v.dtype),
        scratch_shapes=[pltpu.SemaphoreType.DMA((N_DEV - 1,)), pltpu.SemaphoreType.DMA((N_DEV - 1,))],
        compiler_params=_params(),
    )(v)


def _win_block(ref, blk):
    return ref.at[:, :, pl.ds(pl.multiple_of(blk * W_IN_SHARD, LANES), W_IN_SHARD)]


def _wout_block(ref, blk):
    return ref.at[:, pl.ds(pl.multiple_of(blk * W_OUT_SHARD, W_OUT_SHARD), W_OUT_SHARD), :]


def _wpool_block(ref, blk):
    return ref.at[:, pl.ds(pl.multiple_of(blk * POOL_SHARD, POOL_SHARD), POOL_SHARD), :]


def _all_gather_weights(win_shard, wout_shard):
    blocks = (_win_block, _wout_block)
    n_arr = len(blocks)

    def body(win_sh, wout_sh, win_ref, wout_ref, send_sems, recv_sems, local_sems):
        x, y, c = _mesh_position()
        me = _block_id(x, y, c)
        sibling = (x, y, 1 - c)
        chips = [(x ^ 1, y), (x, y ^ 1), (x ^ 1, y ^ 1)]
        shards, fulls = (win_sh, wout_sh), (win_ref, wout_ref)

        def copy(a, k, blk, to, src=None):
            window = blocks[a](fulls[a], blk)
            return pltpu.make_async_remote_copy(
                src_ref=window if src is None else src, dst_ref=window, send_sem=send_sems.at[a, k],
                recv_sem=recv_sems.at[a, k], device_id=to, device_id_type=MESH)

        started = []
        for a in range(n_arr):
            own = pltpu.make_async_copy(shards[a], blocks[a](fulls[a], me), local_sems.at[a])
            own.start()
            started.append(own)
        sends = []
        for a in range(n_arr):
            sends.append(copy(a, 0, me, sibling, src=shards[a]))
            sends += [copy(a, 1 + j, me, (*chip, c), src=shards[a]) for j, chip in enumerate(chips)]
        for s in sends:
            s.start()
        for j, chip in enumerate(chips):
            for a in range(n_arr):
                blk = _block_id(*chip, c)
                copy(a, 1 + j, blk, (x, y, c)).wait_recv()
                fwd = copy(a, 4 + j, blk, sibling)
                fwd.start()
                sends.append(fwd)
        for a in range(n_arr):
            copy(a, 0, _block_id(x, y, 1 - c), (x, y, c)).wait_recv()
            for j, chip in enumerate(chips):
                copy(a, 4 + j, _block_id(*chip, 1 - c), (x, y, c)).wait_recv()
        for s in sends:
            s.wait_send()
        for own in started:
            own.wait()

    hbm = pl.BlockSpec(memory_space=pl.ANY)
    return pl.pallas_call(
        body, name="all_gather_weights", in_specs=[hbm, hbm], out_specs=[hbm, hbm],
        out_shape=[jax.ShapeDtypeStruct((DEPTH, D_MODEL, IN_COLS), BF16),
                   jax.ShapeDtypeStruct((DEPTH, D_MODEL, D_MODEL), BF16)],
        scratch_shapes=[pltpu.SemaphoreType.DMA((n_arr, N_DEV - 1)), pltpu.SemaphoreType.DMA((n_arr, N_DEV - 1)),
                        pltpu.SemaphoreType.DMA((n_arr,))],
        compiler_params=_params(),
    )(win_shard, wout_shard)


_GRAD_BLOCKS = (_win_block, _wout_block, _wpool_block)
_GRAD_BLOCK_SHAPES = ((DEPTH, D_MODEL, W_IN_SHARD), (DEPTH, W_OUT_SHARD, D_MODEL), (DEPTH, POOL_SHARD, GROUP_D))


def _exchange_with_sibling(grads):
    n_arr = len(grads)

    def body(*refs):
        srcs, dsts = refs[:n_arr], refs[n_arr:2 * n_arr]
        send_sems, recv_sems = refs[2 * n_arr:]
        x, y, c = _mesh_position()
        sends = []
        for a in range(n_arr):
            for q in range(N_CHIP):
                send = pltpu.make_async_remote_copy(
                    src_ref=_GRAD_BLOCKS[a](srcs[a], 2 * q + (1 - c)), dst_ref=dsts[a].at[q],
                    send_sem=send_sems.at[a, q], recv_sem=recv_sems.at[a, q],
                    device_id=(x, y, 1 - c), device_id_type=MESH)
                send.start()
                sends.append(send)
        for s in sends:
            s.wait_recv()
        for s in sends:
            s.wait_send()

    hbm = pl.BlockSpec(memory_space=pl.ANY)
    return pl.pallas_call(
        body, name="grad_exchange_sibling", in_specs=[hbm] * n_arr, out_specs=[hbm] * n_arr,
        out_shape=[jax.ShapeDtypeStruct((N_CHIP,) + s, BF16) for s in _GRAD_BLOCK_SHAPES],
        scratch_shapes=[pltpu.SemaphoreType.DMA((n_arr, N_CHIP)), pltpu.SemaphoreType.DMA((n_arr, N_CHIP))],
        compiler_params=_params(),
    )(*grads)


def _add_sibling_blocks(grads, received, core):
    n_arr = len(grads)

    def body(core_ref, *refs):
        mine, theirs, outs = refs[:n_arr], refs[n_arr:2 * n_arr], refs[2 * n_arr:]
        for a in range(n_arr):
            outs[a][...] = (mine[a][...].astype(F32) + theirs[a][...].astype(F32)).astype(BF16)

    own_specs = [
        pl.BlockSpec((None, D_MODEL, W_IN_SHARD), lambda q, l, core_ref: (l, 0, 2 * q + core_ref[0])),
        pl.BlockSpec((None, W_OUT_SHARD, D_MODEL), lambda q, l, core_ref: (l, 2 * q + core_ref[0], 0)),
        pl.BlockSpec((None, POOL_SHARD, GROUP_D), lambda q, l, core_ref: (l, 2 * q + core_ref[0], 0)),
    ]
    blk_specs = [pl.BlockSpec((None, None) + s[1:], lambda q, l, core_ref: (q, l, 0, 0)) for s in _GRAD_BLOCK_SHAPES]
    return pl.pallas_call(
        body, name="grad_add_sibling",
        grid_spec=pltpu.PrefetchScalarGridSpec(
            num_scalar_prefetch=1, grid=(N_CHIP, DEPTH), in_specs=own_specs + blk_specs, out_specs=blk_specs),
        out_shape=[jax.ShapeDtypeStruct((N_CHIP,) + s, BF16) for s in _GRAD_BLOCK_SHAPES],
        compiler_params=_params(dimension_semantics=("arbitrary", "arbitrary")),
    )(core, *grads, *received)


def _exchange_with_chips(partials):
    n_arr = len(partials)
    n_peer = N_CHIP - 1

    def body(*refs):
        srcs, dsts = refs[:n_arr], refs[n_arr:2 * n_arr]
        send_sems, recv_sems = refs[2 * n_arr:]
        x, y, c = _mesh_position()
        chips = [(x ^ 1, y), (x, y ^ 1), (x ^ 1, y ^ 1)]
        sends = []
        for a in range(n_arr):
            for j, (qx, qy) in enumerate(chips):
                send = pltpu.make_async_remote_copy(
                    src_ref=srcs[a].at[2 * qx + qy], dst_ref=dsts[a].at[j],
                    send_sem=send_sems.at[a, j], recv_sem=recv_sems.at[a, j],
                    device_id=(qx, qy, c), device_id_type=MESH)
                send.start()
                sends.append(send)
        for s in sends:
            s.wait_recv()
        for s in sends:
            s.wait_send()

    hbm = pl.BlockSpec(memory_space=pl.ANY)
    return pl.pallas_call(
        body, name="grad_exchange_chips", in_specs=[hbm] * n_arr, out_specs=[hbm] * n_arr,
        out_shape=[jax.ShapeDtypeStruct((n_peer,) + s, BF16) for s in _GRAD_BLOCK_SHAPES],
        scratch_shapes=[pltpu.SemaphoreType.DMA((n_arr, n_peer)), pltpu.SemaphoreType.DMA((n_arr, n_peer))],
        compiler_params=_params(),
    )(*partials)


def _modulation_columns(c_all, w_ada):
    def body(c_ref, w_ref, cact_ref, out_ref):
        c_t = c_ref[...]
        c_act = c_t * _sigmoid(c_t)
        cact_ref[...] = c_act
        out_ref[...] = jnp.dot(c_act, w_ref[...], preferred_element_type=F32, precision=lax.Precision.HIGHEST)

    return pl.pallas_call(
        body, name="modulation_columns", grid=(DEPTH,),
        in_specs=[pl.BlockSpec((N_DEV, D_MODEL), lambda l: (0, 0)),
                  pl.BlockSpec((None, D_MODEL, W_IN_SHARD), lambda l: (l, 0, 0))],
        out_specs=[pl.BlockSpec((N_DEV, D_MODEL), lambda l: (0, 0)),
                   pl.BlockSpec((N_DEV, W_IN_SHARD), lambda l: (0, l))],
        out_shape=[jax.ShapeDtypeStruct((N_DEV, D_MODEL), F32),
                   jax.ShapeDtypeStruct((N_DEV, DEPTH * W_IN_SHARD), F32)],
        compiler_params=_params(dimension_semantics=("arbitrary",)),
    )(c_all, w_ada)


def _adamw(w, g, m, v):
    m_new = ADAM_B1 * m + (1.0 - ADAM_B1) * g
    v_new = ADAM_B2 * v + (1.0 - ADAM_B2) * (g * g)
    m_hat = m_new / (1.0 - ADAM_B1 ** ADAM_STEP)
    v_hat = v_new / (1.0 - ADAM_B2 ** ADAM_STEP)
    delta = -ADAM_LR * (m_hat / (jnp.sqrt(v_hat) + ADAM_EPS) + ADAM_WD * w)
    return delta, m_new, v_new


def _adamw_w_ada(w, m, v, c_act_t, dmod_cols):
    def body(w_ref, m_ref, v_ref, ct_ref, dm_ref, g_ref, d_ref, mo_ref, vo_ref):
        g = ct_ref[:, 0:1] * dm_ref[0:1, :]
        for b in range(1, N_DEV):
            g = g + ct_ref[:, b:b + 1] * dm_ref[b:b + 1, :]
        g_ref[...] = g
        d_ref[...], mo_ref[...], vo_ref[...] = _adamw(w_ref[...], g, m_ref[...], v_ref[...])

    big = pl.BlockSpec((None, D_MODEL, W_IN_SHARD), lambda l: (l, 0, 0))
    return pl.pallas_call(
        body, name="adamw_w_ada", grid=(DEPTH,),
        in_specs=[big, big, big, pl.BlockSpec((D_MODEL, N_DEV), lambda l: (0, 0)),
                  pl.BlockSpec((None, N_DEV, W_IN_SHARD), lambda l: (l, 0, 0))],
        out_specs=[big] * 4, out_shape=[jax.ShapeDtypeStruct(w.shape, F32)] * 4,
        compiler_params=_params(dimension_semantics=("arbitrary",)),
    )(w, m, v, c_act_t, dmod_cols)


def _adamw_reduced(name, w, m, v, partial, received, chip, row_tile):
    depth, rows, cols = w.shape

    def body(chip_ref, w_ref, m_ref, v_ref, own_ref, recv_ref, g_ref, d_ref, mo_ref, vo_ref):
        g = own_ref[...].astype(F32)
        for j in range(N_CHIP - 1):
            g = g + recv_ref[j].astype(F32)
        g_ref[...] = g
        d_ref[...], mo_ref[...], vo_ref[...] = _adamw(w_ref[...], g, m_ref[...], v_ref[...])

    blk = pl.BlockSpec((None, row_tile, cols), lambda l, r, chip_ref: (l, r, 0))
    own = pl.BlockSpec((None, None, row_tile, cols), lambda l, r, chip_ref: (chip_ref[0], l, r, 0))
    recv = pl.BlockSpec((N_CHIP - 1, None, row_tile, cols), lambda l, r, chip_ref: (0, l, r, 0))
    return pl.pallas_call(
        body, name=name,
        grid_spec=pltpu.PrefetchScalarGridSpec(
            num_scalar_prefetch=1, grid=(depth, rows // row_tile), in_specs=[blk, blk, blk, own, recv],
            out_specs=[blk] * 4),
        out_shape=[jax.ShapeDtypeStruct(w.shape, F32)] * 4,
        compiler_params=_params(dimension_semantics=("arbitrary", "arbitrary")),
    )(chip, w, m, v, partial, received)


def _adamw_small(params):
    n = len(params)

    def body(*refs):
        ins, outs = refs[:4 * n], refs[4 * n:]
        for p in range(n):
            w_ref, g_ref, m_ref, v_ref = ins[4 * p:4 * p + 4]
            d_ref, mo_ref, vo_ref = outs[3 * p:3 * p + 3]
            d_ref[...], mo_ref[...], vo_ref[...] = _adamw(w_ref[...], g_ref[...], m_ref[...], v_ref[...])

    vmem = pl.BlockSpec(memory_space=pltpu.VMEM)
    flat = [a for group in params for a in group]
    out_shape = [jax.ShapeDtypeStruct(group[0].shape, F32) for group in params for _ in range(3)]
    outs = pl.pallas_call(
        body, name="adamw_small", in_specs=[vmem] * len(flat), out_specs=[vmem] * len(out_shape),
        out_shape=out_shape, compiler_params=_params(),
    )(*flat)
    return [tuple(outs[3 * p:3 * p + 3]) for p in range(n)]


def _sum_sources(slabs):
    def body(s_ref, o_ref):
        acc = s_ref[0]
        for b in range(1, N_DEV):
            acc = acc + s_ref[b]
        o_ref[...] = acc

    vmem = pl.BlockSpec(memory_space=pltpu.VMEM)
    return pl.pallas_call(
        body, name="sum_small_grads", in_specs=[vmem], out_specs=vmem,
        out_shape=jax.ShapeDtypeStruct(slabs.shape[1:], F32), compiler_params=_params(),
    )(slabs)


def _to_bf16(a, name):
    def body(a_ref, o_ref):
        o_ref[...] = a_ref[...].astype(BF16)

    spec = pl.BlockSpec((None,) + a.shape[1:], lambda l: (l, 0, 0))
    return pl.pallas_call(
        body, name=name, grid=(a.shape[0],), in_specs=[spec], out_specs=spec,
        out_shape=jax.ShapeDtypeStruct(a.shape, BF16), compiler_params=_params(dimension_semantics=("arbitrary",)),
    )(a)


def kernel(x, c, w_ada, b_ada, g_pre, w_in, w_conv, w_pool, pool_scale, w_out, g_post, loss_target, m_w_ada, m_b_ada, m_g_pre, m_w_in, m_w_conv, m_w_pool, m_pool_scale, m_w_out, m_g_post, v_w_ada, v_b_ada, v_g_pre, v_w_in, v_w_conv, v_w_pool, v_pool_scale, v_w_out, v_g_post):
    mx, my, mc = _mesh_position()
    me = _block_id(mx, my, mc)
    chip = (2 * mx + my).astype(jnp.int32).reshape(1)
    core = mc.astype(jnp.int32).reshape(1)
    x0 = x[0]
    target = loss_target[0]
    conv_shard = w_conv.shape[-1]

    c_all = _all_gather_small(c, "all_gather_c")[:, 0, :]
    c_act, pieces = _modulation_columns(c_all, w_ada)
    mod_all = _all_gather_small(pieces, "all_gather_modulation")
    mod_mine = lax.dynamic_index_in_dim(mod_all, me, axis=1, keepdims=False)
    mod = mod_mine.reshape(N_DEV, DEPTH, W_IN_SHARD).transpose(1, 0, 2).reshape(DEPTH, 3 * D_MODEL) + b_ada
    zeros_d = jnp.zeros((DEPTH, 3, D_MODEL), F32)
    vec = jnp.concatenate([mod.reshape(DEPTH, 3, D_MODEL), g_pre[:, None], g_post[:, None], zeros_d], axis=1)

    w_conv_all = _all_gather_small(w_conv.reshape(DEPTH * 3, conv_shard), "all_gather_w_conv")
    w_conv_full = w_conv_all.transpose(1, 0, 2).reshape(DEPTH, 3, CONV_W)
    cps = jnp.concatenate([w_conv_full, pool_scale[:, None], jnp.zeros((DEPTH, 4, CONV_W), F32)], axis=1)
    wpool_b = _to_bf16(w_pool.reshape(DEPTH, POOL_ROWS, GROUP_D), "cast_w_pool").reshape(w_pool.shape)

    win_full, wout_full = _all_gather_weights(_to_bf16(w_in, "cast_w_in"), _to_bf16(w_out, "cast_w_out"))

    xs, projs, ys = [x0], [], []
    for l in range(DEPTH):
        x_next, proj, y = _forward_layer(l, xs[-1], vec, cps, wpool_b, win_full, wout_full)
        xs.append(x_next)
        projs.append(proj)
        ys.append(y)
    dx, loss_tile = _loss_head(xs[DEPTH], target)
    loss = lax.psum(loss_tile[0, 0], ("x", "y", "c"))

    gwin_all = gwout_all = gwpool_all = None
    slab_rows = [None] * DEPTH
    for l in reversed(range(DEPTH)):
        dproj, gwout_all, gwpool_all, dcps, dvec1 = _backward_mixers(
            l, dx, ys[l], projs[l], vec, cps, wpool_b, wout_full, gwout_all, gwpool_all)
        dx, gwin_all, dvec2 = _backward_input_proj(l, xs[l], dproj, dx, vec, win_full, gwin_all)
        slab_rows[l] = jnp.concatenate(
            [dvec2[0], dvec2[1], dvec1[0], dvec2[2], dvec1[1], dcps[3], dcps[0], dcps[1], dcps[2]])
    grad_x = dx[None]

    grads = (gwin_all, gwout_all, gwpool_all.reshape(DEPTH, POOL_ROWS, GROUP_D))
    from_sibling = _exchange_with_sibling(grads)
    partials = _add_sibling_blocks(grads, from_sibling, core)
    from_chips = _exchange_with_chips(partials)
    g_w_in, d_w_in, nm_w_in, nv_w_in = _adamw_reduced(
        "adamw_w_in", w_in, m_w_in, v_w_in, partials[0], from_chips[0], chip, ROW_TILE)
    g_w_out, d_w_out, nm_w_out, nv_w_out = _adamw_reduced(
        "adamw_w_out", w_out, m_w_out, v_w_out, partials[1], from_chips[1], chip, W_OUT_SHARD)
    pool_shard_shape = (DEPTH, POOL_SHARD, GROUP_D)
    zero_shard = jnp.zeros(pool_shard_shape, F32)
    g_pool_shard = _adamw_reduced(
        "reduce_w_pool", zero_shard, zero_shard, zero_shard, partials[2], from_chips[2], chip, POOL_SHARD)[0]
    g_pool_all = _all_gather_small(g_pool_shard.reshape(DEPTH * POOL_SHARD, GROUP_D), "all_gather_grad_w_pool")
    g_w_pool = g_pool_all.reshape(N_DEV, DEPTH, POOL_SHARD, GROUP_D).transpose(1, 0, 2, 3).reshape(w_pool.shape)

    slab = jnp.stack(slab_rows)
    slabs = _all_gather_small(slab, "all_gather_small_grads")
    total = _sum_sources(slabs)
    o = 3 * D_MODEL
    g_b_ada = total[:, :o]
    g_g_pre = total[:, o:o + D_MODEL]
    g_g_post = total[:, o + D_MODEL:o + 2 * D_MODEL]
    g_pool_scale = total[:, o + 2 * D_MODEL:o + 2 * D_MODEL + POOL_W]
    g_conv_full = total[:, o + 2 * D_MODEL + POOL_W:].reshape(DEPTH, 3, CONV_W)
    g_w_conv = lax.dynamic_slice_in_dim(g_conv_full, me * conv_shard, conv_shard, axis=2)

    dmod_all = slabs[:, :, :o].reshape(N_DEV, DEPTH, N_DEV, W_IN_SHARD)
    dmod_cols = lax.dynamic_index_in_dim(dmod_all, me, axis=2, keepdims=False).transpose(1, 0, 2)
    g_w_ada, d_w_ada, nm_w_ada, nv_w_ada = _adamw_w_ada(w_ada, m_w_ada, v_w_ada, c_act.T, dmod_cols)

    flat2 = lambda a: a.reshape(-1, a.shape[-1])
    small = _adamw_small([
        (b_ada, g_b_ada, m_b_ada, v_b_ada),
        (g_pre, g_g_pre, m_g_pre, v_g_pre),
        (flat2(w_conv), flat2(g_w_conv), flat2(m_w_conv), flat2(v_w_conv)),
        (flat2(w_pool), flat2(g_w_pool), flat2(m_w_pool), flat2(v_w_pool)),
        (pool_scale, g_pool_scale, m_pool_scale, v_pool_scale),
        (g_post, g_g_post, m_g_post, v_g_post),
    ])
    (d_b_ada, nm_b_ada, nv_b_ada), (d_g_pre, nm_g_pre, nv_g_pre), conv_upd, pool_upd, \
        (d_ps, nm_ps, nv_ps), (d_g_post, nm_g_post, nv_g_post) = small
    d_w_conv, nm_w_conv, nv_w_conv = (a.reshape(w_conv.shape) for a in conv_upd)
    d_w_pool, nm_w_pool, nv_w_pool = (a.reshape(w_pool.shape) for a in pool_upd)

    return (loss, grad_x,
            g_w_ada, g_b_ada, g_g_pre, g_w_in, g_w_conv, g_w_pool, g_pool_scale, g_w_out, g_g_post,
            d_w_ada, d_b_ada, d_g_pre, d_w_in, d_w_conv, d_w_pool, d_ps, d_w_out, d_g_post,
            nm_w_ada, nm_b_ada, nm_g_pre, nm_w_in, nm_w_conv, nm_w_pool, nm_ps, nm_w_out, nm_g_post,
            nv_w_ada, nv_b_ada, nv_g_pre, nv_w_in, nv_w_conv, nv_w_pool, nv_ps, nv_w_out, nv_g_post)
```

```python
import jax
import jax.numpy as jnp
from jax import lax
from jax.experimental import pallas as pl
from jax.experimental.pallas import tpu as pltpu

F32 = jnp.float32
BF16 = jnp.bfloat16

D_MODEL = 1024
DEPTH = 4
CONV_W = 512
POOL_W = 512
POOL_WINDOWS = (2, 4, 8, 16)
GROUP_D = 128
IN_COLS = 4 * CONV_W + 2 * POOL_W
NORM_EPS = 1e-6

ADAM_LR = 0.001
ADAM_B1 = 0.9
ADAM_B2 = 0.999
ADAM_EPS = 1e-08
ADAM_WD = 0.01
ADAM_STEP = 10

N_DEV = 8
N_CHIP = 4
N_OTHER_CHIPS = N_CHIP - 1
MESH = pl.DeviceIdType.MESH
W_IN_SHARD = IN_COLS // N_DEV
W_OUT_SHARD = D_MODEL // N_DEV
POOL_ROWS = len(POOL_WINDOWS) * GROUP_D
POOL_SHARD = POOL_ROWS // N_DEV

SUBLANES = 8
LANES = 128
VMEM_LIMIT_BYTES = 56 * 1024 * 1024
ROW_TILE = 256
POOL_HALO = 16
CONV_HALO = SUBLANES

SLAB_COLS = 3 * D_MODEL + D_MODEL + D_MODEL + POOL_W + 3 * CONV_W
SLAB_ROWS = SUBLANES

HBM = pl.BlockSpec(memory_space=pl.ANY)


def _params(**kw):
    return pltpu.CompilerParams(vmem_limit_bytes=VMEM_LIMIT_BYTES, **kw)


def _sigmoid(v):
    return 1.0 / (1.0 + jnp.exp(-v))


def _dot(a, b):
    return jnp.dot(a, b, preferred_element_type=F32)


def _dot_tn(a, b):
    return lax.dot_general(a, b, (((0,), (0,)), ((), ())), preferred_element_type=F32)


def _dot_nt(a, b):
    return lax.dot_general(a, b, (((1,), (1,)), ((), ())), preferred_element_type=F32)


def _rows_from_before(v, k):
    return pltpu.roll(v, k, 0)


def _rows_from_after(v, k):
    return pltpu.roll(v, v.shape[0] - k, 0)


def _window_counts(t0, rows):
    return (lax.broadcasted_iota(jnp.int32, (rows, 1), 0) + (t0 + 1)).astype(F32)


def _split_proj(p32):
    cw = CONV_W
    return (p32[:, 0 * cw:1 * cw], p32[:, 1 * cw:2 * cw], p32[:, 2 * cw:3 * cw], p32[:, 3 * cw:4 * cw],
            p32[:, 4 * cw:4 * cw + POOL_W], p32[:, 4 * cw + POOL_W:])


def _mixer_forward(p32, z_halo, up_halo, cps, wpool_ref, t0):
    tm = p32.shape[0]
    u_a, b_a, c_a, g_a, u_p, g_p = _split_proj(p32)
    w0, w1, w2, ps = cps[0:1, :], cps[1:2, :], cps[2:3, :], cps[3:4, :]
    z = c_a * u_a
    zcat = jnp.concatenate([z_halo, z], axis=0)
    z1 = _rows_from_before(zcat, 1)[CONV_HALO:]
    z2 = _rows_from_before(zcat, 2)[CONV_HALO:]
    conv = w0 * z2 + w1 * z1 + w2 * z
    sig_a = _sigmoid(g_a)
    silu_a = g_a * sig_a
    y_a = b_a * conv * silu_a

    pcat = jnp.concatenate([up_halo, u_p], axis=0)
    counts = _window_counts(t0, tm)
    pooled, mixed, inv = [], [], []
    for g, w in enumerate(POOL_WINDOWS):
        cols = slice(g * GROUP_D, (g + 1) * GROUP_D)
        s = pcat[:, cols]
        step = 1
        while step < w:
            s = s + _rows_from_before(s, step)
            step *= 2
        inv_g = 1.0 / jnp.minimum(counts, float(w))
        pooled_g = s[POOL_HALO:] * inv_g - u_p[:, cols]
        pooled.append(pooled_g)
        inv.append(inv_g)
        mixed.append(_dot(pooled_g.astype(BF16), wpool_ref[g]))
    mixed = jnp.concatenate(mixed, axis=1)
    sig_p = _sigmoid(g_p)
    silu_p = g_p * sig_p
    y_p = mixed * ps * silu_p
    ycat = jnp.concatenate([y_a, y_p], axis=1).astype(BF16)
    return dict(u_a=u_a, b_a=b_a, c_a=c_a, g_a=g_a, u_p=u_p, g_p=g_p, z=z, z1=z1, z2=z2, conv=conv, sig_a=sig_a,
                silu_a=silu_a, pooled=pooled, inv=inv, mixed=mixed, sig_p=sig_p, silu_p=silu_p, ycat=ycat,
                w0=w0, w1=w1, w2=w2, ps=ps)


def _layer_spec(shape, layer):
    nd = len(shape)
    return pl.BlockSpec((None,) + tuple(shape[1:]), lambda i, _l=layer, _n=nd: (_l,) + (0,) * (_n - 1))


def _whole_spec(shape):
    return pl.BlockSpec(tuple(shape), lambda i, _n=len(shape): (0,) * _n)


def _mesh_position():
    return lax.axis_index("x"), lax.axis_index("y"), lax.axis_index("c")


def _block_id(x, y, c):
    return 4 * x + 2 * y + c


def _other_chips(x, y):
    return [(x ^ 1, y), (x, y ^ 1), (x ^ 1, y ^ 1)]


def _col_block(ref, blk):
    return ref.at[:, pl.ds(pl.multiple_of(blk * W_IN_SHARD, LANES), W_IN_SHARD)]


def _row_block(rows):
    def block(ref, blk):
        return ref.at[pl.ds(pl.multiple_of(blk * rows, rows), rows), :]
    return block


_BLOCK_OF = (_col_block, _row_block(W_OUT_SHARD), _row_block(POOL_SHARD))
_BLOCK_SHAPES = ((D_MODEL, W_IN_SHARD), (W_OUT_SHARD, D_MODEL), (POOL_SHARD, GROUP_D))


class _Exchange:
    def __init__(self, inputs, out_shapes, aliases, sem_shapes, make):
        self.inputs, self.out_shapes, self.aliases, self.sem_shapes, self.make = (
            list(inputs), list(out_shapes), dict(aliases), list(sem_shapes), make)


def _run_exchange(exchange, name):
    n_in, n_out = len(exchange.inputs), len(exchange.out_shapes)

    def body(*refs):
        start, finish = exchange.make(refs[:n_in], refs[n_in:n_in + n_out], refs[n_in + n_out:])
        start()
        finish()

    return pl.pallas_call(
        body, name=name, in_specs=[HBM] * n_in, out_specs=[HBM] * n_out, out_shape=exchange.out_shapes,
        scratch_shapes=exchange.sem_shapes, input_output_aliases=exchange.aliases, compiler_params=_params(),
    )(*exchange.inputs)


def _gather_weights_exchange(layer, win_shards, wout_shards):
    n_arr = 2

    def make(in_refs, out_refs, sems):
        send_sems, recv_sems, local_sems = sems
        x, y, c = _mesh_position()
        me = _block_id(x, y, c)
        sibling = (x, y, 1 - c)
        chips = _other_chips(x, y)
        shards = [in_refs[a].at[layer] for a in range(n_arr)]

        def copy(a, k, blk, to, src=None):
            window = _BLOCK_OF[a](out_refs[a], blk)
            return pltpu.make_async_remote_copy(
                src_ref=window if src is None else src, dst_ref=window, send_sem=send_sems.at[a, k],
                recv_sem=recv_sems.at[a, k], device_id=to, device_id_type=MESH)

        own = [pltpu.make_async_copy(shards[a], _BLOCK_OF[a](out_refs[a], me), local_sems.at[a])
               for a in range(n_arr)]
        first = []
        for a in range(n_arr):
            first.append(copy(a, 0, me, sibling, src=shards[a]))
            first += [copy(a, 1 + j, me, (*chip, c), src=shards[a]) for j, chip in enumerate(chips)]

        def start():
            for cp in own + first:
                cp.start()

        def finish():
            passed = []
            for j, chip in enumerate(chips):
                blk = _block_id(*chip, c)
                for a in range(n_arr):
                    copy(a, 1 + j, blk, (x, y, c)).wait_recv()
                    fwd = copy(a, 4 + j, blk, sibling)
                    fwd.start()
                    passed.append(fwd)
            for a in range(n_arr):
                copy(a, 0, _block_id(x, y, 1 - c), (x, y, c)).wait_recv()
                for j, chip in enumerate(chips):
                    copy(a, 4 + j, _block_id(*chip, 1 - c), (x, y, c)).wait_recv()
            for cp in first + passed:
                cp.wait_send()
            for cp in own:
                cp.wait()

        return start, finish

    return _Exchange(
        [win_shards, wout_shards],
        [jax.ShapeDtypeStruct((D_MODEL, IN_COLS), BF16), jax.ShapeDtypeStruct((D_MODEL, D_MODEL), BF16)], {},
        [pltpu.SemaphoreType.DMA((n_arr, N_DEV - 1)), pltpu.SemaphoreType.DMA((n_arr, N_DEV - 1)),
         pltpu.SemaphoreType.DMA((n_arr,))], make)


def _sibling_exchange(grads):
    n_arr = len(grads)

    def make(in_refs, out_refs, sems):
        send_sems, recv_sems = sems
        x, y, c = _mesh_position()
        copies = [pltpu.make_async_remote_copy(
            src_ref=_BLOCK_OF[a](in_refs[a], 2 * q + (1 - c)), dst_ref=out_refs[a].at[q],
            send_sem=send_sems.at[a, q], recv_sem=recv_sems.at[a, q], device_id=(x, y, 1 - c), device_id_type=MESH)
            for a in range(n_arr) for q in range(N_CHIP)]

        def start():
            for cp in copies:
                cp.start()

        def finish():
            for cp in copies:
                cp.wait_recv()
            for cp in copies:
                cp.wait_send()

        return start, finish

    return _Exchange(
        grads, [jax.ShapeDtypeStruct((N_CHIP,) + s, BF16) for s in _BLOCK_SHAPES], {},
        [pltpu.SemaphoreType.DMA((n_arr, N_CHIP)), pltpu.SemaphoreType.DMA((n_arr, N_CHIP))], make)


def _chips_exchange(layer, partials, received):
    n_arr = len(partials)

    def make(in_refs, out_refs, sems):
        send_sems, recv_sems = sems
        x, y, c = _mesh_position()
        copies = [pltpu.make_async_remote_copy(
            src_ref=in_refs[a].at[2 * qx + qy, layer], dst_ref=out_refs[a].at[j, layer],
            send_sem=send_sems.at[a, j], recv_sem=recv_sems.at[a, j], device_id=(qx, qy, c), device_id_type=MESH)
            for a in range(n_arr) for j, (qx, qy) in enumerate(_other_chips(x, y))]

        def start():
            for cp in copies:
                cp.start()

        def finish():
            for cp in copies:
                cp.wait_recv()
            for cp in copies:
                cp.wait_send()

        return start, finish

    inputs = list(partials)
    aliases = {}
    if received is not None:
        inputs += list(received)
        aliases = {n_arr + a: a for a in range(n_arr)}
    return _Exchange(
        inputs, [jax.ShapeDtypeStruct((N_OTHER_CHIPS, DEPTH) + s, BF16) for s in _BLOCK_SHAPES], aliases,
        [pltpu.SemaphoreType.DMA((n_arr, N_OTHER_CHIPS)), pltpu.SemaphoreType.DMA((n_arr, N_OTHER_CHIPS))], make)


def _host(exchange, args, in_specs, out_shape, out_specs, scratch):
    n_own = (len(args), len(out_shape), len(scratch))
    if exchange is None:
        return {}, lambda refs: (refs, None)
    n_ex = (len(exchange.inputs), len(exchange.out_shapes), len(exchange.sem_shapes))
    aliases = {n_own[0] + i: n_own[1] + o for i, o in exchange.aliases.items()}
    args += exchange.inputs
    in_specs += [HBM] * n_ex[0]
    out_shape += exchange.out_shapes
    out_specs += [HBM] * n_ex[1]
    scratch += exchange.sem_shapes

    def split(refs):
        own, theirs, at = [], [], 0
        for mine, ex in zip(n_own, n_ex):
            own += refs[at:at + mine]
            theirs.append(refs[at + mine:at + mine + ex])
            at += mine + ex
        return own, exchange.make(*theirs)

    return aliases, split


def _all_gather_small(v, name):
    def body(v_ref, out_ref, send_sems, recv_sems):
        x, y, c = _mesh_position()
        me = _block_id(x, y, c)
        out_ref[me] = v_ref[...]
        sends = []
        for k in range(1, N_DEV):
            px, py, pc = x ^ ((k >> 2) & 1), y ^ ((k >> 1) & 1), c ^ (k & 1)
            send = pltpu.make_async_remote_copy(
                src_ref=v_ref, dst_ref=out_ref.at[me], send_sem=send_sems.at[k - 1], recv_sem=recv_sems.at[k - 1],
                device_id=(px, py, pc), device_id_type=MESH)
            send.start()
            sends.append((send, _block_id(px, py, pc)))
        for k, (send, peer) in enumerate(sends):
            pltpu.make_async_remote_copy(
                src_ref=v_ref, dst_ref=out_ref.at[peer], send_sem=send_sems.at[k], recv_sem=recv_sems.at[k],
                device_id=(x, y, c), device_id_type=MESH).wait_recv()
        for send, _ in sends:
            send.wait_send()

    vmem = pl.BlockSpec(memory_space=pltpu.VMEM)
    return pl.pallas_call(
        body, name=name, in_specs=[vmem], out_specs=vmem,
        out_shape=jax.ShapeDtypeStruct((N_DEV,) + v.shape, v.dtype),
        scratch_shapes=[pltpu.SemaphoreType.DMA((N_DEV - 1,)), pltpu.SemaphoreType.DMA((N_DEV - 1,))],
        compiler_params=_params(),
    )(v)


def _forward_layer(layer, x, vec, cps, wpool, win, wout, exchange):
    t_len = x.shape[0]
    n_tiles = t_len // ROW_TILE
    row = lambda cols: pl.BlockSpec((ROW_TILE, cols), lambda i: (i, 0))
    args = [x, vec, cps, wpool, win, wout]
    in_specs = [row(D_MODEL), _layer_spec(vec.shape, layer), _layer_spec(cps.shape, layer),
                _layer_spec(wpool.shape, layer), _whole_spec(win.shape), _whole_spec(wout.shape)]
    out_shape = [jax.ShapeDtypeStruct((t_len, D_MODEL), F32), jax.ShapeDtypeStruct((t_len, IN_COLS), BF16),
                 jax.ShapeDtypeStruct((t_len, D_MODEL), BF16)]
    out_specs = [row(D_MODEL), row(IN_COLS), row(D_MODEL)]
    scratch = [pltpu.VMEM((CONV_HALO, CONV_W), F32), pltpu.VMEM((POOL_HALO, POOL_W), F32)]
    aliases, split = _host(exchange, args, in_specs, out_shape, out_specs, scratch)

    def body(*refs):
        (x_ref, vec_ref, cps_ref, wpool_ref, win_ref, wout_ref, xo_ref, proj_ref, y_ref, zc_ref, pc_ref), hosted = (
            split(refs))
        i = pl.program_id(0)

        @pl.when(i == 0)
        def _():
            if hosted is not None:
                hosted[0]()
            zc_ref[...] = jnp.zeros_like(zc_ref)
            pc_ref[...] = jnp.zeros_like(pc_ref)

        x_t = x_ref[...]
        shift, scale, gate = vec_ref[0:1, :], vec_ref[1:2, :], vec_ref[2:3, :]
        g_pre, g_post = vec_ref[3:4, :], vec_ref[4:5, :]
        rx = lax.rsqrt(jnp.mean(x_t * x_t, axis=-1, keepdims=True) + NORM_EPS)
        h = (x_t * rx) * g_pre * (1.0 + scale) + shift
        proj_b = _dot(h.astype(BF16), win_ref[...]).astype(BF16)
        proj_ref[...] = proj_b
        mx = _mixer_forward(proj_b.astype(F32), zc_ref[...], pc_ref[...], cps_ref[...], wpool_ref, i * ROW_TILE)
        zc_ref[...] = mx["z"][ROW_TILE - CONV_HALO:]
        pc_ref[...] = mx["u_p"][ROW_TILE - POOL_HALO:]
        y_b = _dot(mx["ycat"], wout_ref[...]).astype(BF16)
        y_ref[...] = y_b
        y_t = y_b.astype(F32)
        ry = lax.rsqrt(jnp.mean(y_t * y_t, axis=-1, keepdims=True) + NORM_EPS)
        xo_ref[...] = x_t + gate * (y_t * ry * g_post)

        if hosted is not None:
            pl.when(i == n_tiles - 1)(hosted[1])

    return pl.pallas_call(
        body, name=f"forward_layer_{layer}", grid=(n_tiles,), in_specs=in_specs, out_specs=out_specs,
        out_shape=out_shape, scratch_shapes=scratch, input_output_aliases=aliases,
        compiler_params=_params(dimension_semantics=("arbitrary",)),
    )(*args)


def _loss_head(x_final, target):
    t_len = x_final.shape[0]
    n_tiles = t_len // ROW_TILE

    def body(x_ref, t_ref, dx_ref, loss_ref):
        @pl.when(pl.program_id(0) == 0)
        def _():
            loss_ref[...] = jnp.zeros_like(loss_ref)

        err = x_ref[...] - t_ref[...]
        dx_ref[...] = err * (1.0 / D_MODEL)
        loss_ref[...] += jnp.sum(err * err) * (0.5 / D_MODEL)

    row = pl.BlockSpec((ROW_TILE, D_MODEL), lambda i: (i, 0))
    return pl.pallas_call(
        body, name="loss_head", grid=(n_tiles,), in_specs=[row, row],
        out_specs=[row, pl.BlockSpec((SUBLANES, LANES), lambda i: (0, 0))],
        out_shape=[jax.ShapeDtypeStruct((t_len, D_MODEL), F32), jax.ShapeDtypeStruct((SUBLANES, LANES), F32)],
        compiler_params=_params(dimension_semantics=("arbitrary",)),
    )(x_final, target)


def _backward_mixers(layer, dxo, y, proj, vec, cps, wpool, wout, exchange):
    t_len = dxo.shape[0]
    n_tiles = t_len // ROW_TILE
    halo_per_tile = ROW_TILE // POOL_HALO
    rev = lambda cols: pl.BlockSpec((ROW_TILE, cols), lambda i: (n_tiles - 1 - i, 0))
    halo_spec = pl.BlockSpec(
        (POOL_HALO, IN_COLS), lambda i: (jnp.maximum((n_tiles - 1 - i) * halo_per_tile - 1, 0), 0))
    gwpool_shape = (len(POOL_WINDOWS), GROUP_D, GROUP_D)
    args = [dxo, y, proj, proj, vec, cps, wpool, wout]
    in_specs = [rev(D_MODEL), rev(D_MODEL), rev(IN_COLS), halo_spec, _layer_spec(vec.shape, layer),
                _layer_spec(cps.shape, layer), _layer_spec(wpool.shape, layer), _whole_spec(wout.shape)]
    out_shape = [jax.ShapeDtypeStruct((t_len, IN_COLS), BF16), jax.ShapeDtypeStruct((D_MODEL, D_MODEL), BF16),
                 jax.ShapeDtypeStruct(gwpool_shape, BF16), jax.ShapeDtypeStruct((SUBLANES, CONV_W), F32),
                 jax.ShapeDtypeStruct((SUBLANES, D_MODEL), F32)]
    out_specs = [rev(IN_COLS), _whole_spec((D_MODEL, D_MODEL)), _whole_spec(gwpool_shape),
                 _whole_spec((SUBLANES, CONV_W)), _whole_spec((SUBLANES, D_MODEL))]
    scratch = [pltpu.VMEM((D_MODEL, D_MODEL), F32), pltpu.VMEM(gwpool_shape, F32),
               pltpu.VMEM((CONV_HALO, CONV_W), F32), pltpu.VMEM((POOL_HALO, POOL_W), F32)]
    aliases, split = _host(exchange, args, in_specs, out_shape, out_specs, scratch)

    def body(*refs):
        (dxo_ref, y_ref, proj_ref, projh_ref, vec_ref, cps_ref, wpool_ref, wout_ref, dproj_ref, gwout_ref, gwpool_ref,
         dcps_ref, dvec_ref, gwout_acc, gwpool_acc, dcc_ref, qc_ref), hosted = split(refs)
        i = pl.program_id(0)
        tile = n_tiles - 1 - i

        @pl.when(i == 0)
        def _():
            if hosted is not None:
                hosted[0]()
            gwout_acc[...] = jnp.zeros_like(gwout_acc)
            gwpool_acc[...] = jnp.zeros_like(gwpool_acc)
            dcps_ref[...] = jnp.zeros_like(dcps_ref)
            dvec_ref[...] = jnp.zeros_like(dvec_ref)
            dcc_ref[...] = jnp.zeros_like(dcc_ref)
            qc_ref[...] = jnp.zeros_like(qc_ref)

        gate, g_post = vec_ref[2:3, :], vec_ref[4:5, :]
        dxo_t = dxo_ref[...]
        y_t = y_ref[...].astype(F32)
        ry = lax.rsqrt(jnp.mean(y_t * y_t, axis=-1, keepdims=True) + NORM_EPS)
        yh = y_t * ry
        dxy = dxo_t * yh
        dvec_ref[0:1, :] += jnp.sum(dxy * g_post, axis=0, keepdims=True)
        dvec_ref[1:2, :] += jnp.sum(dxy * gate, axis=0, keepdims=True)
        dyh = dxo_t * (gate * g_post)
        dy_b = (ry * (dyh - yh * jnp.mean(dyh * yh, axis=-1, keepdims=True))).astype(BF16)

        halo = jnp.where(tile > 0, projh_ref[...].astype(F32), 0.0)
        hu_a, _, hc_a, _, hu_p, _ = _split_proj(halo)
        z_halo = (hc_a * hu_a)[POOL_HALO - CONV_HALO:]
        mx = _mixer_forward(proj_ref[...].astype(F32), z_halo, hu_p, cps_ref[...], wpool_ref, tile * ROW_TILE)

        gwout_acc[...] += _dot_tn(mx["ycat"], dy_b)
        dycat = _dot_nt(dy_b, wout_ref[...])
        dy_a, dy_p = dycat[:, :CONV_W], dycat[:, CONV_W:]

        t_a = dy_a * mx["silu_a"]
        db_a = t_a * mx["conv"]
        dconv = t_a * mx["b_a"]
        dg_a = dy_a * mx["b_a"] * mx["conv"] * (mx["sig_a"] * (1.0 + mx["g_a"] * (1.0 - mx["sig_a"])))
        dccat = jnp.concatenate([dconv, dcc_ref[...]], axis=0)
        dc1 = _rows_from_after(dccat, 1)[:ROW_TILE]
        dc2 = _rows_from_after(dccat, 2)[:ROW_TILE]
        dz = mx["w2"] * dconv + mx["w1"] * dc1 + mx["w0"] * dc2
        dcc_ref[...] = dconv[:CONV_HALO]
        dcps_ref[0:1, :] += jnp.sum(dconv * mx["z2"], axis=0, keepdims=True)
        dcps_ref[1:2, :] += jnp.sum(dconv * mx["z1"], axis=0, keepdims=True)
        dcps_ref[2:3, :] += jnp.sum(dconv * mx["z"], axis=0, keepdims=True)
        du_a = dz * mx["c_a"]
        dc_a = dz * mx["u_a"]

        t_p = dy_p * mx["silu_p"]
        dcps_ref[3:4, :] += jnp.sum(t_p * mx["mixed"], axis=0, keepdims=True)
        dmixed = (t_p * mx["ps"]).astype(BF16)
        dg_p = dy_p * mx["mixed"] * mx["ps"] * (mx["sig_p"] * (1.0 + mx["g_p"] * (1.0 - mx["sig_p"])))
        du_p, q_head = [], []
        for g, w in enumerate(POOL_WINDOWS):
            cols = slice(g * GROUP_D, (g + 1) * GROUP_D)
            dm_g = dmixed[:, cols]
            dpooled_g = _dot_nt(dm_g, wpool_ref[g])
            gwpool_acc[g] += _dot_tn(mx["pooled"][g].astype(BF16), dm_g)
            q_g = dpooled_g * mx["inv"][g]
            q_head.append(q_g[:POOL_HALO])
            s = jnp.concatenate([q_g, qc_ref[:, cols]], axis=0)
            step = 1
            while step < w:
                s = s + _rows_from_after(s, step)
                step *= 2
            du_p.append(s[:ROW_TILE] - dpooled_g)
        qc_ref[...] = jnp.concatenate(q_head, axis=1)
        dproj_ref[...] = jnp.concatenate([du_a, db_a, dc_a, dg_a] + du_p + [dg_p], axis=1).astype(BF16)

        @pl.when(i == n_tiles - 1)
        def _():
            gwout_ref[...] = gwout_acc[...].astype(BF16)
            gwpool_ref[...] = gwpool_acc[...].astype(BF16)
            if hosted is not None:
                hosted[1]()

    return pl.pallas_call(
        body, name=f"backward_mixers_{layer}", grid=(n_tiles,), in_specs=in_specs, out_specs=out_specs,
        out_shape=out_shape, scratch_shapes=scratch, input_output_aliases=aliases,
        compiler_params=_params(dimension_semantics=("arbitrary",)),
    )(*args)


def _backward_input_proj(layer, x, dproj, dxo, vec, win, exchange):
    t_len = x.shape[0]
    n_tiles = t_len // ROW_TILE
    row = lambda cols: pl.BlockSpec((ROW_TILE, cols), lambda i: (i, 0))
    args = [x, dproj, dxo, vec, win]
    in_specs = [row(D_MODEL), row(IN_COLS), row(D_MODEL), _layer_spec(vec.shape, layer), _whole_spec(win.shape)]
    out_shape = [jax.ShapeDtypeStruct((t_len, D_MODEL), F32), jax.ShapeDtypeStruct((D_MODEL, IN_COLS), BF16),
                 jax.ShapeDtypeStruct((SUBLANES, D_MODEL), F32)]
    out_specs = [row(D_MODEL), _whole_spec((D_MODEL, IN_COLS)), _whole_spec((SUBLANES, D_MODEL))]
    scratch = [pltpu.VMEM((D_MODEL, IN_COLS), F32)]
    aliases, split = _host(exchange, args, in_specs, out_shape, out_specs, scratch)

    def body(*refs):
        (x_ref, dproj_ref, dxo_ref, vec_ref, win_ref, dx_ref, gwin_ref, dvec_ref, gwin_acc), hosted = split(refs)
        i = pl.program_id(0)

        @pl.when(i == 0)
        def _():
            if hosted is not None:
                hosted[0]()
            gwin_acc[...] = jnp.zeros_like(gwin_acc)
            dvec_ref[...] = jnp.zeros_like(dvec_ref)

        shift, scale, g_pre = vec_ref[0:1, :], vec_ref[1:2, :], vec_ref[3:4, :]
        x_t = x_ref[...]
        rx = lax.rsqrt(jnp.mean(x_t * x_t, axis=-1, keepdims=True) + NORM_EPS)
        xn = x_t * rx
        mod_scale = 1.0 + scale
        h_b = (xn * g_pre * mod_scale + shift).astype(BF16)
        dproj_t = dproj_ref[...]
        gwin_acc[...] += _dot_tn(h_b, dproj_t)
        dh = _dot_nt(dproj_t, win_ref[...])
        dvec_ref[0:1, :] += jnp.sum(dh, axis=0, keepdims=True)
        dhx = dh * xn
        dvec_ref[1:2, :] += jnp.sum(dhx * g_pre, axis=0, keepdims=True)
        dvec_ref[2:3, :] += jnp.sum(dhx * mod_scale, axis=0, keepdims=True)
        dxn = dh * (g_pre * mod_scale)
        dx_ref[...] = dxo_ref[...] + rx * (dxn - xn * jnp.mean(dxn * xn, axis=-1, keepdims=True))

        @pl.when(i == n_tiles - 1)
        def _():
            gwin_ref[...] = gwin_acc[...].astype(BF16)
            if hosted is not None:
                hosted[1]()

    return pl.pallas_call(
        body, name=f"backward_input_proj_{layer}", grid=(n_tiles,), in_specs=in_specs, out_specs=out_specs,
        out_shape=out_shape, scratch_shapes=scratch, input_output_aliases=aliases,
        compiler_params=_params(dimension_semantics=("arbitrary",)),
    )(*args)


def _add_sibling_blocks(layer, grads, received, core, partials):
    n_arr = len(grads)

    def body(core_ref, *refs):
        mine, theirs, outs = refs[:n_arr], refs[n_arr:2 * n_arr], refs[-n_arr:]
        for a in range(n_arr):
            outs[a][...] = (mine[a][...].astype(F32) + theirs[a][...].astype(F32)).astype(BF16)

    own_specs = [
        pl.BlockSpec((D_MODEL, W_IN_SHARD), lambda q, core_ref: (0, 2 * q + core_ref[0])),
        pl.BlockSpec((W_OUT_SHARD, D_MODEL), lambda q, core_ref: (2 * q + core_ref[0], 0)),
        pl.BlockSpec((POOL_SHARD, GROUP_D), lambda q, core_ref: (2 * q + core_ref[0], 0)),
    ]
    recv_specs = [pl.BlockSpec((None,) + s, lambda q, core_ref: (q, 0, 0)) for s in _BLOCK_SHAPES]
    out_specs = [pl.BlockSpec((None, None) + s, lambda q, core_ref: (q, layer, 0, 0)) for s in _BLOCK_SHAPES]
    args = [core, *grads, *received]
    in_specs = own_specs + recv_specs
    aliases = {}
    if partials is not None:
        aliases = {len(args) + a: a for a in range(n_arr)}
        args += list(partials)
        in_specs += [HBM] * n_arr
    return pl.pallas_call(
        body, name=f"grad_add_sibling_{layer}",
        grid_spec=pltpu.PrefetchScalarGridSpec(
            num_scalar_prefetch=1, grid=(N_CHIP,), in_specs=in_specs, out_specs=out_specs),
        out_shape=[jax.ShapeDtypeStruct((N_CHIP, DEPTH) + s, BF16) for s in _BLOCK_SHAPES],
        input_output_aliases=aliases,
        compiler_params=_params(dimension_semantics=("arbitrary",)),
    )(*args)


def _modulation_columns(c_all, w_ada):
    def body(c_ref, w_ref, cact_ref, out_ref):
        c_t = c_ref[...]
        c_act = c_t * _sigmoid(c_t)
        cact_ref[...] = c_act
        out_ref[...] = jnp.dot(c_act, w_ref[...], preferred_element_type=F32, precision=lax.Precision.HIGHEST)

    return pl.pallas_call(
        body, name="modulation_columns", grid=(DEPTH,),
        in_specs=[pl.BlockSpec((N_DEV, D_MODEL), lambda l: (0, 0)),
                  pl.BlockSpec((None, D_MODEL, W_IN_SHARD), lambda l: (l, 0, 0))],
        out_specs=[pl.BlockSpec((N_DEV, D_MODEL), lambda l: (0, 0)),
                   pl.BlockSpec((N_DEV, W_IN_SHARD), lambda l: (0, l))],
        out_shape=[jax.ShapeDtypeStruct((N_DEV, D_MODEL), F32),
                   jax.ShapeDtypeStruct((N_DEV, DEPTH * W_IN_SHARD), F32)],
        compiler_params=_params(dimension_semantics=("arbitrary",)),
    )(c_all, w_ada)


def _adamw(w, g, m, v):
    m_new = ADAM_B1 * m + (1.0 - ADAM_B1) * g
    v_new = ADAM_B2 * v + (1.0 - ADAM_B2) * (g * g)
    m_hat = m_new / (1.0 - ADAM_B1 ** ADAM_STEP)
    v_hat = v_new / (1.0 - ADAM_B2 ** ADAM_STEP)
    delta = -ADAM_LR * (m_hat / (jnp.sqrt(v_hat) + ADAM_EPS) + ADAM_WD * w)
    return delta, m_new, v_new


def _adamw_w_ada(w, m, v, c_act_t, dmod_cols):
    def body(w_ref, m_ref, v_ref, ct_ref, dm_ref, g_ref, d_ref, mo_ref, vo_ref):
        g = ct_ref[:, 0:1] * dm_ref[0:1, :]
        for b in range(1, N_DEV):
            g = g + ct_ref[:, b:b + 1] * dm_ref[b:b + 1, :]
        g_ref[...] = g
        d_ref[...], mo_ref[...], vo_ref[...] = _adamw(w_ref[...], g, m_ref[...], v_ref[...])

    big = pl.BlockSpec((None, D_MODEL, W_IN_SHARD), lambda l: (l, 0, 0))
    return pl.pallas_call(
        body, name="adamw_w_ada", grid=(DEPTH,),
        in_specs=[big, big, big, pl.BlockSpec((D_MODEL, N_DEV), lambda l: (0, 0)),
                  pl.BlockSpec((None, N_DEV, W_IN_SHARD), lambda l: (l, 0, 0))],
        out_specs=[big] * 4, out_shape=[jax.ShapeDtypeStruct(w.shape, F32)] * 4,
        compiler_params=_params(dimension_semantics=("arbitrary",)),
    )(w, m, v, c_act_t, dmod_cols)


def _sum_chip_partials(own_ref, recv_ref):
    g = own_ref[...].astype(F32)
    for j in range(N_OTHER_CHIPS):
        g = g + recv_ref[j].astype(F32)
    return g


def _partial_specs(row_tile, cols):
    own = pl.BlockSpec((None, None, row_tile, cols), lambda l, r, chip_ref: (chip_ref[0], l, r, 0))
    recv = pl.BlockSpec((N_OTHER_CHIPS, None, row_tile, cols), lambda l, r, chip_ref: (0, l, r, 0))
    return own, recv


def _adamw_reduced(name, w, m, v, partial, received, chip, row_tile):
    depth, rows, cols = w.shape

    def body(chip_ref, w_ref, m_ref, v_ref, own_ref, recv_ref, g_ref, d_ref, mo_ref, vo_ref):
        g = _sum_chip_partials(own_ref, recv_ref)
        g_ref[...] = g
        d_ref[...], mo_ref[...], vo_ref[...] = _adamw(w_ref[...], g, m_ref[...], v_ref[...])

    blk = pl.BlockSpec((None, row_tile, cols), lambda l, r, chip_ref: (l, r, 0))
    return pl.pallas_call(
        body, name=name,
        grid_spec=pltpu.PrefetchScalarGridSpec(
            num_scalar_prefetch=1, grid=(depth, rows // row_tile),
            in_specs=[blk, blk, blk, *_partial_specs(row_tile, cols)], out_specs=[blk] * 4),
        out_shape=[jax.ShapeDtypeStruct(w.shape, F32)] * 4,
        compiler_params=_params(dimension_semantics=("arbitrary", "arbitrary")),
    )(chip, w, m, v, partial, received)


def _reduce_w_pool(partial, received, chip):
    def body(chip_ref, own_ref, recv_ref, g_ref):
        g_ref[...] = _sum_chip_partials(own_ref, recv_ref)

    return pl.pallas_call(
        body, name="reduce_w_pool",
        grid_spec=pltpu.PrefetchScalarGridSpec(
            num_scalar_prefetch=1, grid=(DEPTH, 1), in_specs=list(_partial_specs(POOL_SHARD, GROUP_D)),
            out_specs=pl.BlockSpec((POOL_SHARD, GROUP_D), lambda l, r, chip_ref: (l, 0))),
        out_shape=jax.ShapeDtypeStruct((DEPTH * POOL_SHARD, GROUP_D), F32),
        compiler_params=_params(dimension_semantics=("arbitrary", "arbitrary")),
    )(chip, partial, received)


def _adamw_small(params):
    n = len(params)

    def body(*refs):
        ins, outs = refs[:4 * n], refs[4 * n:]
        for p in range(n):
            w_ref, g_ref, m_ref, v_ref = ins[4 * p:4 * p + 4]
            d_ref, mo_ref, vo_ref = outs[3 * p:3 * p + 3]
            d_ref[...], mo_ref[...], vo_ref[...] = _adamw(w_ref[...], g_ref[...], m_ref[...], v_ref[...])

    vmem = pl.BlockSpec(memory_space=pltpu.VMEM)
    flat = [a for group in params for a in group]
    out_shape = [jax.ShapeDtypeStruct(group[0].shape, F32) for group in params for _ in range(3)]
    outs = pl.pallas_call(
        body, name="adamw_small", in_specs=[vmem] * len(flat), out_specs=[vmem] * len(out_shape),
        out_shape=out_shape, compiler_params=_params(),
    )(*flat)
    return [tuple(outs[3 * p:3 * p + 3]) for p in range(n)]


def _sum_sources(slabs):
    def body(s_ref, o_ref):
        acc = s_ref[0]
        for b in range(1, N_DEV):
            acc = acc + s_ref[b]
        o_ref[...] = acc

    vmem = pl.BlockSpec(memory_space=pltpu.VMEM)
    return pl.pallas_call(
        body, name="sum_small_grads", in_specs=[vmem], out_specs=vmem,
        out_shape=jax.ShapeDtypeStruct(slabs.shape[1:], F32), compiler_params=_params(),
    )(slabs)


def _to_bf16(a, name):
    def body(a_ref, o_ref):
        o_ref[...] = a_ref[...].astype(BF16)

    spec = pl.BlockSpec((None,) + a.shape[1:], lambda l: (l, 0, 0))
    return pl.pallas_call(
        body, name=name, grid=(a.shape[0],), in_specs=[spec], out_specs=spec,
        out_shape=jax.ShapeDtypeStruct(a.shape, BF16), compiler_params=_params(dimension_semantics=("arbitrary",)),
    )(a)


def kernel(x, c, w_ada, b_ada, g_pre, w_in, w_conv, w_pool, pool_scale, w_out, g_post, loss_target, m_w_ada, m_b_ada, m_g_pre, m_w_in, m_w_conv, m_w_pool, m_pool_scale, m_w_out, m_g_post, v_w_ada, v_b_ada, v_g_pre, v_w_in, v_w_conv, v_w_pool, v_pool_scale, v_w_out, v_g_post):
    mx, my, mc = _mesh_position()
    me = _block_id(mx, my, mc)
    chip = (2 * mx + my).astype(jnp.int32).reshape(1)
    core = mc.astype(jnp.int32).reshape(1)
    x0 = x[0]
    target = loss_target[0]
    conv_shard = w_conv.shape[-1]

    c_all = _all_gather_small(c, "all_gather_c")[:, 0, :]
    c_act, pieces = _modulation_columns(c_all, w_ada)
    mod_all = _all_gather_small(pieces, "all_gather_modulation")
    mod_mine = lax.dynamic_index_in_dim(mod_all, me, axis=1, keepdims=False)
    mod = mod_mine.reshape(N_DEV, DEPTH, W_IN_SHARD).transpose(1, 0, 2).reshape(DEPTH, 3 * D_MODEL) + b_ada
    zeros_d = jnp.zeros((DEPTH, 3, D_MODEL), F32)
    vec = jnp.concatenate([mod.reshape(DEPTH, 3, D_MODEL), g_pre[:, None], g_post[:, None], zeros_d], axis=1)

    w_conv_all = _all_gather_small(w_conv.reshape(DEPTH * 3, conv_shard), "all_gather_w_conv")
    w_conv_full = w_conv_all.transpose(1, 0, 2).reshape(DEPTH, 3, CONV_W)
    cps = jnp.concatenate([w_conv_full, pool_scale[:, None], jnp.zeros((DEPTH, 4, CONV_W), F32)], axis=1)
    wpool_b = _to_bf16(w_pool.reshape(DEPTH, POOL_ROWS, GROUP_D), "cast_w_pool").reshape(w_pool.shape)

    win_shards, wout_shards = _to_bf16(w_in, "cast_w_in"), _to_bf16(w_out, "cast_w_out")
    wins, wouts = [[a] for a in _run_exchange(
        _gather_weights_exchange(0, win_shards, wout_shards), "all_gather_weights_0")]

    xs, projs, ys = [x0], [], []
    for l in range(DEPTH):
        gather = _gather_weights_exchange(l + 1, win_shards, wout_shards) if l + 1 < DEPTH else None
        x_next, proj, y, *gathered = _forward_layer(l, xs[-1], vec, cps, wpool_b, wins[l], wouts[l], gather)
        xs.append(x_next)
        projs.append(proj)
        ys.append(y)
        if gather is not None:
            wins.append(gathered[0])
            wouts.append(gathered[1])
    dx, loss_tile = _loss_head(xs[DEPTH], target)

    slab_rows = [None] * DEPTH
    pending = partials = from_chips = None
    for l in reversed(range(DEPTH)):
        to_sibling = _sibling_exchange(pending) if pending is not None else None
        dproj, gwout, gwpool, dcps, dvec1, *from_sibling = _backward_mixers(
            l, dx, ys[l], projs[l], vec, cps, wpool_b, wouts[l], to_sibling)
        to_chips = None
        if pending is not None:
            partials = _add_sibling_blocks(l + 1, pending, from_sibling, core, partials)
            to_chips = _chips_exchange(l + 1, partials, from_chips)
        dx, gwin, dvec2, *received = _backward_input_proj(l, xs[l], dproj, dx, vec, wins[l], to_chips)
        if to_chips is not None:
            from_chips = received
        pending = [gwin, gwout, gwpool.reshape(POOL_ROWS, GROUP_D)]
        slab_rows[l] = jnp.concatenate(
            [dvec2[0], dvec2[1], dvec1[0], dvec2[2], dvec1[1], dcps[3], dcps[0], dcps[1], dcps[2]])
    grad_x = dx[None]
    from_sibling = _run_exchange(_sibling_exchange(pending), "grad_exchange_sibling_0")
    partials = _add_sibling_blocks(0, pending, from_sibling, core, partials)
    from_chips = _run_exchange(_chips_exchange(0, partials, from_chips), "grad_exchange_chips_0")

    g_w_in, d_w_in, nm_w_in, nv_w_in = _adamw_reduced(
        "adamw_w_in", w_in, m_w_in, v_w_in, partials[0], from_chips[0], chip, ROW_TILE)
    g_w_out, d_w_out, nm_w_out, nv_w_out = _adamw_reduced(
        "adamw_w_out", w_out, m_w_out, v_w_out, partials[1], from_chips[1], chip, W_OUT_SHARD)
    g_pool_all = _all_gather_small(_reduce_w_pool(partials[2], from_chips[2], chip), "all_gather_grad_w_pool")
    g_w_pool = g_pool_all.reshape(N_DEV, DEPTH, POOL_SHARD, GROUP_D).transpose(1, 0, 2, 3).reshape(w_pool.shape)

    loss_row = jnp.pad(loss_tile[0], (0, SLAB_COLS - LANES))
    slab = jnp.stack(slab_rows + [loss_row] + [jnp.zeros((SLAB_COLS,), F32)] * (SLAB_ROWS - DEPTH - 1))
    slabs = _all_gather_small(slab, "all_gather_small_grads")
    total = _sum_sources(slabs)
    loss = total[DEPTH, 0]
    o = 3 * D_MODEL
    g_b_ada = total[:DEPTH, :o]
    g_g_pre = total[:DEPTH, o:o + D_MODEL]
    g_g_post = total[:DEPTH, o + D_MODEL:o + 2 * D_MODEL]
    g_pool_scale = total[:DEPTH, o + 2 * D_MODEL:o + 2 * D_MODEL + POOL_W]
    g_conv_full = total[:DEPTH, o + 2 * D_MODEL + POOL_W:].reshape(DEPTH, 3, CONV_W)
    g_w_conv = lax.dynamic_slice_in_dim(g_conv_full, me * conv_shard, conv_shard, axis=2)

    dmod_all = slabs[:, :DEPTH, :o].reshape(N_DEV, DEPTH, N_DEV, W_IN_SHARD)
    dmod_cols = lax.dynamic_index_in_dim(dmod_all, me, axis=2, keepdims=False).transpose(1, 0, 2)
    g_w_ada, d_w_ada, nm_w_ada, nv_w_ada = _adamw_w_ada(w_ada, m_w_ada, v_w_ada, c_act.T, dmod_cols)

    flat2 = lambda a: a.reshape(-1, a.shape[-1])
    small = _adamw_small([
        (b_ada, g_b_ada, m_b_ada, v_b_ada),
        (g_pre, g_g_pre, m_g_pre, v_g_pre),
        (flat2(w_conv), flat2(g_w_conv), flat2(m_w_conv), flat2(v_w_conv)),
        (flat2(w_pool), flat2(g_w_pool), flat2(m_w_pool), flat2(v_w_pool)),
        (pool_scale, g_pool_scale, m_pool_scale, v_pool_scale),
        (g_post, g_g_post, m_g_post, v_g_post),
    ])
    (d_b_ada, nm_b_ada, nv_b_ada), (d_g_pre, nm_g_pre, nv_g_pre), conv_upd, pool_upd, \
        (d_ps, nm_ps, nv_ps), (d_g_post, nm_g_post, nv_g_post) = small
    d_w_conv, nm_w_conv, nv_w_conv = (a.reshape(w_conv.shape) for a in conv_upd)
    d_w_pool, nm_w_pool, nv_w_pool = (a.reshape(w_pool.shape) for a in pool_upd)

    return (loss, grad_x,
            g_w_ada, g_b_ada, g_g_pre, g_w_in, g_w_conv, g_w_pool, g_pool_scale, g_w_out, g_g_post,
            d_w_ada, d_b_ada, d_g_pre, d_w_in, d_w_conv, d_w_pool, d_ps, d_w_out, d_g_post,
            nm_w_ada, nm_b_ada, nm_g_pre, nm_w_in, nm_w_conv, nm_w_pool, nm_ps, nm_w_out, nm_g_post,
            nv_w_ada, nv_b_ada, nv_g_pre, nv_w_in, nv_w_conv, nv_w_pool, nv_ps, nv_w_out, nv_g_post)
```

```python
import jax
import jax.numpy as jnp
from jax import lax
from jax.experimental import pallas as pl
from jax.experimental.pallas import tpu as pltpu

F32 = jnp.float32
BF16 = jnp.bfloat16

D_MODEL = 1024
DEPTH = 4
CONV_W = 512
POOL_W = 512
POOL_WINDOWS = (2, 4, 8, 16)
GROUP_D = 128
IN_COLS = 4 * CONV_W + 2 * POOL_W
NORM_EPS = 1e-6

ADAM_LR = 0.001
ADAM_B1 = 0.9
ADAM_B2 = 0.999
ADAM_EPS = 1e-08
ADAM_WD = 0.01
ADAM_STEP = 10

N_DEV = 8
N_CHIP = 4
N_OTHER_CHIPS = N_CHIP - 1
MESH = pl.DeviceIdType.MESH
W_IN_SHARD = IN_COLS // N_DEV
W_OUT_SHARD = D_MODEL // N_DEV
POOL_ROWS = len(POOL_WINDOWS) * GROUP_D
POOL_SHARD = POOL_ROWS // N_DEV

SUBLANES = 8
LANES = 128
VMEM_LIMIT_BYTES = 56 * 1024 * 1024
ROW_TILE = 256
POOL_HALO = 16
CONV_HALO = SUBLANES

SLAB_COLS = 3 * D_MODEL + D_MODEL + D_MODEL + POOL_W + 3 * CONV_W
SLAB_ROWS = SUBLANES

HBM = pl.BlockSpec(memory_space=pl.ANY)


def _params(**kw):
    return pltpu.CompilerParams(vmem_limit_bytes=VMEM_LIMIT_BYTES, **kw)


def _sigmoid(v):
    return 1.0 / (1.0 + jnp.exp(-v))


def _dot(a, b):
    return jnp.dot(a, b, preferred_element_type=F32)


def _dot_tn(a, b):
    return lax.dot_general(a, b, (((0,), (0,)), ((), ())), preferred_element_type=F32)


def _dot_nt(a, b):
    return lax.dot_general(a, b, (((1,), (1,)), ((), ())), preferred_element_type=F32)


def _rows_from_before(v, k):
    return pltpu.roll(v, k, 0)


def _rows_from_after(v, k):
    return pltpu.roll(v, v.shape[0] - k, 0)


def _window_counts(t0, rows):
    return (lax.broadcasted_iota(jnp.int32, (rows, 1), 0) + (t0 + 1)).astype(F32)


def _split_proj(p32):
    cw = CONV_W
    return (p32[:, 0 * cw:1 * cw], p32[:, 1 * cw:2 * cw], p32[:, 2 * cw:3 * cw], p32[:, 3 * cw:4 * cw],
            p32[:, 4 * cw:4 * cw + POOL_W], p32[:, 4 * cw + POOL_W:])


def _mixer_forward(p32, z_halo, up_halo, cps, wpool_ref, t0):
    tm = p32.shape[0]
    u_a, b_a, c_a, g_a, u_p, g_p = _split_proj(p32)
    w0, w1, w2, ps = cps[0:1, :], cps[1:2, :], cps[2:3, :], cps[3:4, :]
    z = c_a * u_a
    zcat = jnp.concatenate([z_halo, z], axis=0)
    z1 = _rows_from_before(zcat, 1)[CONV_HALO:]
    z2 = _rows_from_before(zcat, 2)[CONV_HALO:]
    conv = w0 * z2 + w1 * z1 + w2 * z
    sig_a = _sigmoid(g_a)
    silu_a = g_a * sig_a
    y_a = b_a * conv * silu_a

    pcat = jnp.concatenate([up_halo, u_p], axis=0)
    counts = _window_counts(t0, tm)
    pooled, mixed, inv = [], [], []
    for g, w in enumerate(POOL_WINDOWS):
        cols = slice(g * GROUP_D, (g + 1) * GROUP_D)
        s = pcat[:, cols]
        step = 1
        while step < w:
            s = s + _rows_from_before(s, step)
            step *= 2
        inv_g = 1.0 / jnp.minimum(counts, float(w))
        pooled_g = s[POOL_HALO:] * inv_g - u_p[:, cols]
        pooled.append(pooled_g)
        inv.append(inv_g)
        mixed.append(_dot(pooled_g.astype(BF16), wpool_ref[g]))
    mixed = jnp.concatenate(mixed, axis=1)
    sig_p = _sigmoid(g_p)
    silu_p = g_p * sig_p
    y_p = mixed * ps * silu_p
    ycat = jnp.concatenate([y_a, y_p], axis=1).astype(BF16)
    return dict(u_a=u_a, b_a=b_a, c_a=c_a, g_a=g_a, u_p=u_p, g_p=g_p, z=z, z1=z1, z2=z2, conv=conv, sig_a=sig_a,
                silu_a=silu_a, pooled=pooled, inv=inv, mixed=mixed, sig_p=sig_p, silu_p=silu_p, ycat=ycat,
                w0=w0, w1=w1, w2=w2, ps=ps)


def _layer_spec(shape, layer):
    nd = len(shape)
    return pl.BlockSpec((None,) + tuple(shape[1:]), lambda i, _l=layer, _n=nd: (_l,) + (0,) * (_n - 1))


def _whole_spec(shape):
    return pl.BlockSpec(tuple(shape), lambda i, _n=len(shape): (0,) * _n)


def _mesh_position():
    return lax.axis_index("x"), lax.axis_index("y"), lax.axis_index("c")


def _block_id(x, y, c):
    return 4 * x + 2 * y + c


def _other_chips(x, y):
    return [(x ^ 1, y), (x, y ^ 1), (x ^ 1, y ^ 1)]


def _col_block(ref, blk):
    return ref.at[:, pl.ds(pl.multiple_of(blk * W_IN_SHARD, LANES), W_IN_SHARD)]


def _row_block(rows):
    def block(ref, blk):
        return ref.at[pl.ds(pl.multiple_of(blk * rows, rows), rows), :]
    return block


_BLOCK_OF = (_col_block, _row_block(W_OUT_SHARD), _row_block(POOL_SHARD))
_BLOCK_SHAPES = ((D_MODEL, W_IN_SHARD), (W_OUT_SHARD, D_MODEL), (POOL_SHARD, GROUP_D))


class _Exchange:
    def __init__(self, inputs, out_shapes, aliases, sem_shapes, make):
        self.inputs, self.out_shapes, self.aliases, self.sem_shapes, self.make = (
            list(inputs), list(out_shapes), dict(aliases), list(sem_shapes), make)


def _run_exchange(exchange, name):
    n_in, n_out = len(exchange.inputs), len(exchange.out_shapes)

    def body(*refs):
        start, finish = exchange.make(refs[:n_in], refs[n_in:n_in + n_out], refs[n_in + n_out:])
        start()
        finish()

    return pl.pallas_call(
        body, name=name, in_specs=[HBM] * n_in, out_specs=[HBM] * n_out, out_shape=exchange.out_shapes,
        scratch_shapes=exchange.sem_shapes, input_output_aliases=exchange.aliases, compiler_params=_params(),
    )(*exchange.inputs)


def _gather_weights_exchange(layer, win_shards, wout_shards):
    n_arr = 2

    def make(in_refs, out_refs, sems):
        send_sems, recv_sems, local_sems = sems
        x, y, c = _mesh_position()
        me = _block_id(x, y, c)
        sibling = (x, y, 1 - c)
        chips = _other_chips(x, y)
        shards = [in_refs[a].at[layer] for a in range(n_arr)]

        def copy(a, k, blk, to, src=None):
            window = _BLOCK_OF[a](out_refs[a], blk)
            return pltpu.make_async_remote_copy(
                src_ref=window if src is None else src, dst_ref=window, send_sem=send_sems.at[a, k],
                recv_sem=recv_sems.at[a, k], device_id=to, device_id_type=MESH)

        own = [pltpu.make_async_copy(shards[a], _BLOCK_OF[a](out_refs[a], me), local_sems.at[a])
               for a in range(n_arr)]
        first = []
        for a in range(n_arr):
            first.append(copy(a, 0, me, sibling, src=shards[a]))
            first += [copy(a, 1 + j, me, (*chip, c), src=shards[a]) for j, chip in enumerate(chips)]

        def start():
            for cp in own + first:
                cp.start()

        def finish():
            passed = []
            for j, chip in enumerate(chips):
                blk = _block_id(*chip, c)
                for a in range(n_arr):
                    copy(a, 1 + j, blk, (x, y, c)).wait_recv()
                    fwd = copy(a, 4 + j, blk, sibling)
                    fwd.start()
                    passed.append(fwd)
            for a in range(n_arr):
                copy(a, 0, _block_id(x, y, 1 - c), (x, y, c)).wait_recv()
                for j, chip in enumerate(chips):
                    copy(a, 4 + j, _block_id(*chip, 1 - c), (x, y, c)).wait_recv()
            for cp in first + passed:
                cp.wait_send()
            for cp in own:
                cp.wait()

        return start, finish

    return _Exchange(
        [win_shards, wout_shards],
        [jax.ShapeDtypeStruct((D_MODEL, IN_COLS), BF16), jax.ShapeDtypeStruct((D_MODEL, D_MODEL), BF16)], {},
        [pltpu.SemaphoreType.DMA((n_arr, N_DEV - 1)), pltpu.SemaphoreType.DMA((n_arr, N_DEV - 1)),
         pltpu.SemaphoreType.DMA((n_arr,))], make)


W_IN_KIND = (0,)
W_OUT_KINDS = (1, 2)


def _merge_exchanges(exchanges):
    exchanges = [e for e in exchanges if e is not None]
    if not exchanges:
        return None
    inputs, out_shapes, sem_shapes, aliases = [], [], [], {}
    for e in exchanges:
        aliases.update({len(inputs) + i: len(out_shapes) + o for i, o in e.aliases.items()})
        inputs += e.inputs
        out_shapes += e.out_shapes
        sem_shapes += e.sem_shapes

    def make(in_refs, out_refs, sems):
        made, at = [], [0, 0, 0]
        for e in exchanges:
            n = (len(e.inputs), len(e.out_shapes), len(e.sem_shapes))
            made.append(e.make(in_refs[at[0]:at[0] + n[0]], out_refs[at[1]:at[1] + n[1]], sems[at[2]:at[2] + n[2]]))
            at = [a + b for a, b in zip(at, n)]

        def start():
            for s, _ in made:
                s()

        def finish():
            for _, f in made:
                f()

        return start, finish

    return _Exchange(inputs, out_shapes, aliases, sem_shapes, make)


def _sibling_exchange(grads, kinds):
    n_arr = len(grads)

    def make(in_refs, out_refs, sems):
        send_sems, recv_sems = sems
        x, y, c = _mesh_position()
        copies = [pltpu.make_async_remote_copy(
            src_ref=_BLOCK_OF[kinds[a]](in_refs[a], 2 * q + (1 - c)), dst_ref=out_refs[a].at[q],
            send_sem=send_sems.at[a, q], recv_sem=recv_sems.at[a, q], device_id=(x, y, 1 - c), device_id_type=MESH)
            for a in range(n_arr) for q in range(N_CHIP)]

        def start():
            for cp in copies:
                cp.start()

        def finish():
            for cp in copies:
                cp.wait_recv()
            for cp in copies:
                cp.wait_send()

        return start, finish

    return _Exchange(
        grads, [jax.ShapeDtypeStruct((N_CHIP,) + _BLOCK_SHAPES[k], BF16) for k in kinds], {},
        [pltpu.SemaphoreType.DMA((n_arr, N_CHIP)), pltpu.SemaphoreType.DMA((n_arr, N_CHIP))], make)


def _chips_exchange(layer, partials, received, kinds):
    n_arr = len(partials)

    def make(in_refs, out_refs, sems):
        send_sems, recv_sems = sems
        x, y, c = _mesh_position()
        copies = [pltpu.make_async_remote_copy(
            src_ref=in_refs[a].at[2 * qx + qy, layer], dst_ref=out_refs[a].at[j, layer],
            send_sem=send_sems.at[a, j], recv_sem=recv_sems.at[a, j], device_id=(qx, qy, c), device_id_type=MESH)
            for a in range(n_arr) for j, (qx, qy) in enumerate(_other_chips(x, y))]

        def start():
            for cp in copies:
                cp.start()

        def finish():
            for cp in copies:
                cp.wait_recv()
            for cp in copies:
                cp.wait_send()

        return start, finish

    inputs = list(partials)
    aliases = {}
    if received is not None:
        inputs += list(received)
        aliases = {n_arr + a: a for a in range(n_arr)}
    return _Exchange(
        inputs, [jax.ShapeDtypeStruct((N_OTHER_CHIPS, DEPTH) + _BLOCK_SHAPES[k], BF16) for k in kinds], aliases,
        [pltpu.SemaphoreType.DMA((n_arr, N_OTHER_CHIPS)), pltpu.SemaphoreType.DMA((n_arr, N_OTHER_CHIPS))], make)


def _host(exchange, args, in_specs, out_shape, out_specs, scratch):
    n_own = (len(args), len(out_shape), len(scratch))
    if exchange is None:
        return {}, lambda refs: (refs, None)
    n_ex = (len(exchange.inputs), len(exchange.out_shapes), len(exchange.sem_shapes))
    aliases = {n_own[0] + i: n_own[1] + o for i, o in exchange.aliases.items()}
    args += exchange.inputs
    in_specs += [HBM] * n_ex[0]
    out_shape += exchange.out_shapes
    out_specs += [HBM] * n_ex[1]
    scratch += exchange.sem_shapes

    def split(refs):
        own, theirs, at = [], [], 0
        for mine, ex in zip(n_own, n_ex):
            own += refs[at:at + mine]
            theirs.append(refs[at + mine:at + mine + ex])
            at += mine + ex
        return own, exchange.make(*theirs)

    return aliases, split


def _all_gather_small(v, name, exchange=None):
    vmem = pl.BlockSpec(memory_space=pltpu.VMEM)
    args, in_specs = [v], [vmem]
    out_shape, out_specs = [jax.ShapeDtypeStruct((N_DEV,) + v.shape, v.dtype)], [vmem]
    scratch = [pltpu.SemaphoreType.DMA((N_DEV - 1,)), pltpu.SemaphoreType.DMA((N_DEV - 1,))]
    aliases, split = _host(exchange, args, in_specs, out_shape, out_specs, scratch)

    def body(*refs):
        (v_ref, out_ref, send_sems, recv_sems), hosted = split(refs)
        if hosted is not None:
            hosted[0]()
        x, y, c = _mesh_position()
        me = _block_id(x, y, c)
        out_ref[me] = v_ref[...]
        sends = []
        for k in range(1, N_DEV):
            px, py, pc = x ^ ((k >> 2) & 1), y ^ ((k >> 1) & 1), c ^ (k & 1)
            send = pltpu.make_async_remote_copy(
                src_ref=v_ref, dst_ref=out_ref.at[me], send_sem=send_sems.at[k - 1], recv_sem=recv_sems.at[k - 1],
                device_id=(px, py, pc), device_id_type=MESH)
            send.start()
            sends.append((send, _block_id(px, py, pc)))
        for k, (send, peer) in enumerate(sends):
            pltpu.make_async_remote_copy(
                src_ref=v_ref, dst_ref=out_ref.at[peer], send_sem=send_sems.at[k], recv_sem=recv_sems.at[k],
                device_id=(x, y, c), device_id_type=MESH).wait_recv()
        for send, _ in sends:
            send.wait_send()
        if hosted is not None:
            hosted[1]()

    outs = pl.pallas_call(
        body, name=name, in_specs=in_specs, out_specs=out_specs, out_shape=out_shape, scratch_shapes=scratch,
        input_output_aliases=aliases, compiler_params=_params(),
    )(*args)
    return outs[0] if exchange is None else outs


def _forward_layer(layer, x, vec, cps, wpool, win, wout, exchange):
    t_len = x.shape[0]
    n_tiles = t_len // ROW_TILE
    row = lambda cols: pl.BlockSpec((ROW_TILE, cols), lambda i: (i, 0))
    args = [x, vec, cps, wpool, win, wout]
    in_specs = [row(D_MODEL), _layer_spec(vec.shape, layer), _layer_spec(cps.shape, layer),
                _layer_spec(wpool.shape, layer), _whole_spec(win.shape), _whole_spec(wout.shape)]
    out_shape = [jax.ShapeDtypeStruct((t_len, D_MODEL), F32), jax.ShapeDtypeStruct((t_len, IN_COLS), BF16),
                 jax.ShapeDtypeStruct((t_len, D_MODEL), BF16)]
    out_specs = [row(D_MODEL), row(IN_COLS), row(D_MODEL)]
    scratch = [pltpu.VMEM((CONV_HALO, CONV_W), F32), pltpu.VMEM((POOL_HALO, POOL_W), F32)]
    aliases, split = _host(exchange, args, in_specs, out_shape, out_specs, scratch)

    def body(*refs):
        (x_ref, vec_ref, cps_ref, wpool_ref, win_ref, wout_ref, xo_ref, proj_ref, y_ref, zc_ref, pc_ref), hosted = (
            split(refs))
        i = pl.program_id(0)

        @pl.when(i == 0)
        def _():
            if hosted is not None:
                hosted[0]()
            zc_ref[...] = jnp.zeros_like(zc_ref)
            pc_ref[...] = jnp.zeros_like(pc_ref)

        x_t = x_ref[...]
        shift, scale, gate = vec_ref[0:1, :], vec_ref[1:2, :], vec_ref[2:3, :]
        g_pre, g_post = vec_ref[3:4, :], vec_ref[4:5, :]
        rx = lax.rsqrt(jnp.mean(x_t * x_t, axis=-1, keepdims=True) + NORM_EPS)
        h = (x_t * rx) * g_pre * (1.0 + scale) + shift
        proj_b = _dot(h.astype(BF16), win_ref[...]).astype(BF16)
        proj_ref[...] = proj_b
        mx = _mixer_forward(proj_b.astype(F32), zc_ref[...], pc_ref[...], cps_ref[...], wpool_ref, i * ROW_TILE)
        zc_ref[...] = mx["z"][ROW_TILE - CONV_HALO:]
        pc_ref[...] = mx["u_p"][ROW_TILE - POOL_HALO:]
        y_b = _dot(mx["ycat"], wout_ref[...]).astype(BF16)
        y_ref[...] = y_b
        y_t = y_b.astype(F32)
        ry = lax.rsqrt(jnp.mean(y_t * y_t, axis=-1, keepdims=True) + NORM_EPS)
        xo_ref[...] = x_t + gate * (y_t * ry * g_post)

        if hosted is not None:
            pl.when(i == n_tiles - 1)(hosted[1])

    return pl.pallas_call(
        body, name=f"forward_layer_{layer}", grid=(n_tiles,), in_specs=in_specs, out_specs=out_specs,
        out_shape=out_shape, scratch_shapes=scratch, input_output_aliases=aliases,
        compiler_params=_params(dimension_semantics=("arbitrary",)),
    )(*args)


def _loss_head(x_final, target):
    t_len = x_final.shape[0]
    n_tiles = t_len // ROW_TILE

    def body(x_ref, t_ref, dx_ref, loss_ref):
        @pl.when(pl.program_id(0) == 0)
        def _():
            loss_ref[...] = jnp.zeros_like(loss_ref)

        err = x_ref[...] - t_ref[...]
        dx_ref[...] = err * (1.0 / D_MODEL)
        loss_ref[...] += jnp.sum(err * err) * (0.5 / D_MODEL)

    row = pl.BlockSpec((ROW_TILE, D_MODEL), lambda i: (i, 0))
    return pl.pallas_call(
        body, name="loss_head", grid=(n_tiles,), in_specs=[row, row],
        out_specs=[row, pl.BlockSpec((SUBLANES, LANES), lambda i: (0, 0))],
        out_shape=[jax.ShapeDtypeStruct((t_len, D_MODEL), F32), jax.ShapeDtypeStruct((SUBLANES, LANES), F32)],
        compiler_params=_params(dimension_semantics=("arbitrary",)),
    )(x_final, target)


def _backward_mixers(layer, dxo, y, proj, vec, cps, wpool, wout, exchange):
    t_len = dxo.shape[0]
    n_tiles = t_len // ROW_TILE
    halo_per_tile = ROW_TILE // POOL_HALO
    rev = lambda cols: pl.BlockSpec((ROW_TILE, cols), lambda i: (n_tiles - 1 - i, 0))
    halo_spec = pl.BlockSpec(
        (POOL_HALO, IN_COLS), lambda i: (jnp.maximum((n_tiles - 1 - i) * halo_per_tile - 1, 0), 0))
    gwpool_shape = (len(POOL_WINDOWS), GROUP_D, GROUP_D)
    args = [dxo, y, proj, proj, vec, cps, wpool, wout]
    in_specs = [rev(D_MODEL), rev(D_MODEL), rev(IN_COLS), halo_spec, _layer_spec(vec.shape, layer),
                _layer_spec(cps.shape, layer), _layer_spec(wpool.shape, layer), _whole_spec(wout.shape)]
    out_shape = [jax.ShapeDtypeStruct((t_len, IN_COLS), BF16), jax.ShapeDtypeStruct((D_MODEL, D_MODEL), BF16),
                 jax.ShapeDtypeStruct(gwpool_shape, BF16), jax.ShapeDtypeStruct((SUBLANES, CONV_W), F32),
                 jax.ShapeDtypeStruct((SUBLANES, D_MODEL), F32)]
    out_specs = [rev(IN_COLS), _whole_spec((D_MODEL, D_MODEL)), _whole_spec(gwpool_shape),
                 _whole_spec((SUBLANES, CONV_W)), _whole_spec((SUBLANES, D_MODEL))]
    scratch = [pltpu.VMEM((D_MODEL, D_MODEL), F32), pltpu.VMEM(gwpool_shape, F32),
               pltpu.VMEM((CONV_HALO, CONV_W), F32), pltpu.VMEM((POOL_HALO, POOL_W), F32)]
    aliases, split = _host(exchange, args, in_specs, out_shape, out_specs, scratch)

    def body(*refs):
        (dxo_ref, y_ref, proj_ref, projh_ref, vec_ref, cps_ref, wpool_ref, wout_ref, dproj_ref, gwout_ref, gwpool_ref,
         dcps_ref, dvec_ref, gwout_acc, gwpool_acc, dcc_ref, qc_ref), hosted = split(refs)
        i = pl.program_id(0)
        tile = n_tiles - 1 - i

        @pl.when(i == 0)
        def _():
            if hosted is not None:
                hosted[0]()
            gwout_acc[...] = jnp.zeros_like(gwout_acc)
            gwpool_acc[...] = jnp.zeros_like(gwpool_acc)
            dcps_ref[...] = jnp.zeros_like(dcps_ref)
            dvec_ref[...] = jnp.zeros_like(dvec_ref)
            dcc_ref[...] = jnp.zeros_like(dcc_ref)
            qc_ref[...] = jnp.zeros_like(qc_ref)

        gate, g_post = vec_ref[2:3, :], vec_ref[4:5, :]
        dxo_t = dxo_ref[...]
        y_t = y_ref[...].astype(F32)
        ry = lax.rsqrt(jnp.mean(y_t * y_t, axis=-1, keepdims=True) + NORM_EPS)
        yh = y_t * ry
        dxy = dxo_t * yh
        dvec_ref[0:1, :] += jnp.sum(dxy * g_post, axis=0, keepdims=True)
        dvec_ref[1:2, :] += jnp.sum(dxy * gate, axis=0, keepdims=True)
        dyh = dxo_t * (gate * g_post)
        dy_b = (ry * (dyh - yh * jnp.mean(dyh * yh, axis=-1, keepdims=True))).astype(BF16)

        halo = jnp.where(tile > 0, projh_ref[...].astype(F32), 0.0)
        hu_a, _, hc_a, _, hu_p, _ = _split_proj(halo)
        z_halo = (hc_a * hu_a)[POOL_HALO - CONV_HALO:]
        mx = _mixer_forward(proj_ref[...].astype(F32), z_halo, hu_p, cps_ref[...], wpool_ref, tile * ROW_TILE)

        gwout_acc[...] += _dot_tn(mx["ycat"], dy_b)
        dycat = _dot_nt(dy_b, wout_ref[...])
        dy_a, dy_p = dycat[:, :CONV_W], dycat[:, CONV_W:]

        t_a = dy_a * mx["silu_a"]
        db_a = t_a * mx["conv"]
        dconv = t_a * mx["b_a"]
        dg_a = dy_a * mx["b_a"] * mx["conv"] * (mx["sig_a"] * (1.0 + mx["g_a"] * (1.0 - mx["sig_a"])))
        dccat = jnp.concatenate([dconv, dcc_ref[...]], axis=0)
        dc1 = _rows_from_after(dccat, 1)[:ROW_TILE]
        dc2 = _rows_from_after(dccat, 2)[:ROW_TILE]
        dz = mx["w2"] * dconv + mx["w1"] * dc1 + mx["w0"] * dc2
        dcc_ref[...] = dconv[:CONV_HALO]
        dcps_ref[0:1, :] += jnp.sum(dconv * mx["z2"], axis=0, keepdims=True)
        dcps_ref[1:2, :] += jnp.sum(dconv * mx["z1"], axis=0, keepdims=True)
        dcps_ref[2:3, :] += jnp.sum(dconv * mx["z"], axis=0, keepdims=True)
        du_a = dz * mx["c_a"]
        dc_a = dz * mx["u_a"]

        t_p = dy_p * mx["silu_p"]
        dcps_ref[3:4, :] += jnp.sum(t_p * mx["mixed"], axis=0, keepdims=True)
        dmixed = (t_p * mx["ps"]).astype(BF16)
        dg_p = dy_p * mx["mixed"] * mx["ps"] * (mx["sig_p"] * (1.0 + mx["g_p"] * (1.0 - mx["sig_p"])))
        du_p, q_head = [], []
        for g, w in enumerate(POOL_WINDOWS):
            cols = slice(g * GROUP_D, (g + 1) * GROUP_D)
            dm_g = dmixed[:, cols]
            dpooled_g = _dot_nt(dm_g, wpool_ref[g])
            gwpool_acc[g] += _dot_tn(mx["pooled"][g].astype(BF16), dm_g)
            q_g = dpooled_g * mx["inv"][g]
            q_head.append(q_g[:POOL_HALO])
            s = jnp.concatenate([q_g, qc_ref[:, cols]], axis=0)
            step = 1
            while step < w:
                s = s + _rows_from_after(s, step)
                step *= 2
            du_p.append(s[:ROW_TILE] - dpooled_g)
        qc_ref[...] = jnp.concatenate(q_head, axis=1)
        dproj_ref[...] = jnp.concatenate([du_a, db_a, dc_a, dg_a] + du_p + [dg_p], axis=1).astype(BF16)

        @pl.when(i == n_tiles - 1)
        def _():
            gwout_ref[...] = gwout_acc[...].astype(BF16)
            gwpool_ref[...] = gwpool_acc[...].astype(BF16)
            if hosted is not None:
                hosted[1]()

    return pl.pallas_call(
        body, name=f"backward_mixers_{layer}", grid=(n_tiles,), in_specs=in_specs, out_specs=out_specs,
        out_shape=out_shape, scratch_shapes=scratch, input_output_aliases=aliases,
        compiler_params=_params(dimension_semantics=("arbitrary",)),
    )(*args)


def _backward_input_proj(layer, x, dproj, dxo, vec, win, exchange):
    t_len = x.shape[0]
    n_tiles = t_len // ROW_TILE
    row = lambda cols: pl.BlockSpec((ROW_TILE, cols), lambda i: (i, 0))
    args = [x, dproj, dxo, vec, win]
    in_specs = [row(D_MODEL), row(IN_COLS), row(D_MODEL), _layer_spec(vec.shape, layer), _whole_spec(win.shape)]
    out_shape = [jax.ShapeDtypeStruct((t_len, D_MODEL), F32), jax.ShapeDtypeStruct((D_MODEL, IN_COLS), BF16),
                 jax.ShapeDtypeStruct((SUBLANES, D_MODEL), F32)]
    out_specs = [row(D_MODEL), _whole_spec((D_MODEL, IN_COLS)), _whole_spec((SUBLANES, D_MODEL))]
    scratch = [pltpu.VMEM((D_MODEL, IN_COLS), F32)]
    aliases, split = _host(exchange, args, in_specs, out_shape, out_specs, scratch)

    def body(*refs):
        (x_ref, dproj_ref, dxo_ref, vec_ref, win_ref, dx_ref, gwin_ref, dvec_ref, gwin_acc), hosted = split(refs)
        i = pl.program_id(0)

        @pl.when(i == 0)
        def _():
            if hosted is not None:
                hosted[0]()
            gwin_acc[...] = jnp.zeros_like(gwin_acc)
            dvec_ref[...] = jnp.zeros_like(dvec_ref)

        shift, scale, g_pre = vec_ref[0:1, :], vec_ref[1:2, :], vec_ref[3:4, :]
        x_t = x_ref[...]
        rx = lax.rsqrt(jnp.mean(x_t * x_t, axis=-1, keepdims=True) + NORM_EPS)
        xn = x_t * rx
        mod_scale = 1.0 + scale
        h_b = (xn * g_pre * mod_scale + shift).astype(BF16)
        dproj_t = dproj_ref[...]
        gwin_acc[...] += _dot_tn(h_b, dproj_t)
        dh = _dot_nt(dproj_t, win_ref[...])
        dvec_ref[0:1, :] += jnp.sum(dh, axis=0, keepdims=True)
        dhx = dh * xn
        dvec_ref[1:2, :] += jnp.sum(dhx * g_pre, axis=0, keepdims=True)
        dvec_ref[2:3, :] += jnp.sum(dhx * mod_scale, axis=0, keepdims=True)
        dxn = dh * (g_pre * mod_scale)
        dx_ref[...] = dxo_ref[...] + rx * (dxn - xn * jnp.mean(dxn * xn, axis=-1, keepdims=True))

        @pl.when(i == n_tiles - 1)
        def _():
            gwin_ref[...] = gwin_acc[...].astype(BF16)
            if hosted is not None:
                hosted[1]()

    return pl.pallas_call(
        body, name=f"backward_input_proj_{layer}", grid=(n_tiles,), in_specs=in_specs, out_specs=out_specs,
        out_shape=out_shape, scratch_shapes=scratch, input_output_aliases=aliases,
        compiler_params=_params(dimension_semantics=("arbitrary",)),
    )(*args)


def _add_sibling_blocks(name, layer, grads, received, core, partials, kinds):
    n_arr = len(grads)

    def body(core_ref, *refs):
        mine, theirs, outs = refs[:n_arr], refs[n_arr:2 * n_arr], refs[-n_arr:]
        for a in range(n_arr):
            outs[a][...] = (mine[a][...].astype(F32) + theirs[a][...].astype(F32)).astype(BF16)

    own_of_kind = [
        pl.BlockSpec((D_MODEL, W_IN_SHARD), lambda q, core_ref: (0, 2 * q + core_ref[0])),
        pl.BlockSpec((W_OUT_SHARD, D_MODEL), lambda q, core_ref: (2 * q + core_ref[0], 0)),
        pl.BlockSpec((POOL_SHARD, GROUP_D), lambda q, core_ref: (2 * q + core_ref[0], 0)),
    ]
    shapes = [_BLOCK_SHAPES[k] for k in kinds]
    recv_specs = [pl.BlockSpec((None,) + s, lambda q, core_ref: (q, 0, 0)) for s in shapes]
    out_specs = [pl.BlockSpec((None, None) + s, lambda q, core_ref: (q, layer, 0, 0)) for s in shapes]
    args = [core, *grads, *received]
    in_specs = [own_of_kind[k] for k in kinds] + recv_specs
    aliases = {}
    if partials is not None:
        aliases = {len(args) + a: a for a in range(n_arr)}
        args += list(partials)
        in_specs += [HBM] * n_arr
    return pl.pallas_call(
        body, name=name,
        grid_spec=pltpu.PrefetchScalarGridSpec(
            num_scalar_prefetch=1, grid=(N_CHIP,), in_specs=in_specs, out_specs=out_specs),
        out_shape=[jax.ShapeDtypeStruct((N_CHIP, DEPTH) + s, BF16) for s in shapes],
        input_output_aliases=aliases,
        compiler_params=_params(dimension_semantics=("arbitrary",)),
    )(*args)


def _modulation_columns(c_all, w_ada):
    def body(c_ref, w_ref, cact_ref, out_ref):
        c_t = c_ref[...]
        c_act = c_t * _sigmoid(c_t)
        cact_ref[...] = c_act
        out_ref[...] = jnp.dot(c_act, w_ref[...], preferred_element_type=F32, precision=lax.Precision.HIGHEST)

    return pl.pallas_call(
        body, name="modulation_columns", grid=(DEPTH,),
        in_specs=[pl.BlockSpec((N_DEV, D_MODEL), lambda l: (0, 0)),
                  pl.BlockSpec((None, D_MODEL, W_IN_SHARD), lambda l: (l, 0, 0))],
        out_specs=[pl.BlockSpec((N_DEV, D_MODEL), lambda l: (0, 0)),
                   pl.BlockSpec((N_DEV, W_IN_SHARD), lambda l: (0, l))],
        out_shape=[jax.ShapeDtypeStruct((N_DEV, D_MODEL), F32),
                   jax.ShapeDtypeStruct((N_DEV, DEPTH * W_IN_SHARD), F32)],
        compiler_params=_params(dimension_semantics=("arbitrary",)),
    )(c_all, w_ada)


def _adamw(w, g, m, v):
    m_new = ADAM_B1 * m + (1.0 - ADAM_B1) * g
    v_new = ADAM_B2 * v + (1.0 - ADAM_B2) * (g * g)
    m_hat = m_new / (1.0 - ADAM_B1 ** ADAM_STEP)
    v_hat = v_new / (1.0 - ADAM_B2 ** ADAM_STEP)
    delta = -ADAM_LR * (m_hat / (jnp.sqrt(v_hat) + ADAM_EPS) + ADAM_WD * w)
    return delta, m_new, v_new


def _adamw_w_ada(w, m, v, c_act_t, dmod_cols, exchange):
    big = pl.BlockSpec((None, D_MODEL, W_IN_SHARD), lambda l: (l, 0, 0))
    args = [w, m, v, c_act_t, dmod_cols]
    in_specs = [big, big, big, pl.BlockSpec((D_MODEL, N_DEV), lambda l: (0, 0)),
                pl.BlockSpec((None, N_DEV, W_IN_SHARD), lambda l: (l, 0, 0))]
    out_shape, out_specs, scratch = [jax.ShapeDtypeStruct(w.shape, F32)] * 4, [big] * 4, []
    aliases, split = _host(exchange, args, in_specs, out_shape, out_specs, scratch)

    def body(*refs):
        (w_ref, m_ref, v_ref, ct_ref, dm_ref, g_ref, d_ref, mo_ref, vo_ref), hosted = split(refs)
        if hosted is not None:
            pl.when(pl.program_id(0) == 0)(hosted[0])
        g = ct_ref[:, 0:1] * dm_ref[0:1, :]
        for b in range(1, N_DEV):
            g = g + ct_ref[:, b:b + 1] * dm_ref[b:b + 1, :]
        g_ref[...] = g
        d_ref[...], mo_ref[...], vo_ref[...] = _adamw(w_ref[...], g, m_ref[...], v_ref[...])
        if hosted is not None:
            pl.when(pl.program_id(0) == DEPTH - 1)(hosted[1])

    return pl.pallas_call(
        body, name="adamw_w_ada", grid=(DEPTH,), in_specs=in_specs, out_specs=out_specs, out_shape=out_shape,
        scratch_shapes=scratch, input_output_aliases=aliases,
        compiler_params=_params(dimension_semantics=("arbitrary",)),
    )(*args)


def _sum_chip_partials(own_ref, recv_ref):
    g = own_ref[...].astype(F32)
    for j in range(N_OTHER_CHIPS):
        g = g + recv_ref[j].astype(F32)
    return g


def _partial_specs(row_tile, cols):
    own = pl.BlockSpec((None, None, row_tile, cols), lambda l, r, chip_ref: (chip_ref[0], l, r, 0))
    recv = pl.BlockSpec((N_OTHER_CHIPS, None, row_tile, cols), lambda l, r, chip_ref: (0, l, r, 0))
    return own, recv


def _adamw_reduced(name, w, m, v, partial, received, chip, row_tile):
    depth, rows, cols = w.shape

    def body(chip_ref, w_ref, m_ref, v_ref, own_ref, recv_ref, g_ref, d_ref, mo_ref, vo_ref):
        g = _sum_chip_partials(own_ref, recv_ref)
        g_ref[...] = g
        d_ref[...], mo_ref[...], vo_ref[...] = _adamw(w_ref[...], g, m_ref[...], v_ref[...])

    blk = pl.BlockSpec((None, row_tile, cols), lambda l, r, chip_ref: (l, r, 0))
    return pl.pallas_call(
        body, name=name,
        grid_spec=pltpu.PrefetchScalarGridSpec(
            num_scalar_prefetch=1, grid=(depth, rows // row_tile),
            in_specs=[blk, blk, blk, *_partial_specs(row_tile, cols)], out_specs=[blk] * 4),
        out_shape=[jax.ShapeDtypeStruct(w.shape, F32)] * 4,
        compiler_params=_params(dimension_semantics=("arbitrary", "arbitrary")),
    )(chip, w, m, v, partial, received)


def _reduce_w_pool(partial, received, chip):
    def body(chip_ref, own_ref, recv_ref, g_ref):
        g_ref[...] = _sum_chip_partials(own_ref, recv_ref)

    return pl.pallas_call(
        body, name="reduce_w_pool",
        grid_spec=pltpu.PrefetchScalarGridSpec(
            num_scalar_prefetch=1, grid=(DEPTH, 1), in_specs=list(_partial_specs(POOL_SHARD, GROUP_D)),
            out_specs=pl.BlockSpec((POOL_SHARD, GROUP_D), lambda l, r, chip_ref: (l, 0))),
        out_shape=jax.ShapeDtypeStruct((DEPTH * POOL_SHARD, GROUP_D), F32),
        compiler_params=_params(dimension_semantics=("arbitrary", "arbitrary")),
    )(chip, partial, received)


def _adamw_small(params):
    n = len(params)

    def body(*refs):
        ins, outs = refs[:4 * n], refs[4 * n:]
        for p in range(n):
            w_ref, g_ref, m_ref, v_ref = ins[4 * p:4 * p + 4]
            d_ref, mo_ref, vo_ref = outs[3 * p:3 * p + 3]
            d_ref[...], mo_ref[...], vo_ref[...] = _adamw(w_ref[...], g_ref[...], m_ref[...], v_ref[...])

    vmem = pl.BlockSpec(memory_space=pltpu.VMEM)
    flat = [a for group in params for a in group]
    out_shape = [jax.ShapeDtypeStruct(group[0].shape, F32) for group in params for _ in range(3)]
    outs = pl.pallas_call(
        body, name="adamw_small", in_specs=[vmem] * len(flat), out_specs=[vmem] * len(out_shape),
        out_shape=out_shape, compiler_params=_params(),
    )(*flat)
    return [tuple(outs[3 * p:3 * p + 3]) for p in range(n)]


def _sum_sources(slabs):
    def body(s_ref, o_ref):
        acc = s_ref[0]
        for b in range(1, N_DEV):
            acc = acc + s_ref[b]
        o_ref[...] = acc

    vmem = pl.BlockSpec(memory_space=pltpu.VMEM)
    return pl.pallas_call(
        body, name="sum_small_grads", in_specs=[vmem], out_specs=vmem,
        out_shape=jax.ShapeDtypeStruct(slabs.shape[1:], F32), compiler_params=_params(),
    )(slabs)


def _to_bf16(a, name):
    def body(a_ref, o_ref):
        o_ref[...] = a_ref[...].astype(BF16)

    spec = pl.BlockSpec((None,) + a.shape[1:], lambda l: (l, 0, 0))
    return pl.pallas_call(
        body, name=name, grid=(a.shape[0],), in_specs=[spec], out_specs=spec,
        out_shape=jax.ShapeDtypeStruct(a.shape, BF16), compiler_params=_params(dimension_semantics=("arbitrary",)),
    )(a)


def kernel(x, c, w_ada, b_ada, g_pre, w_in, w_conv, w_pool, pool_scale, w_out, g_post, loss_target, m_w_ada, m_b_ada, m_g_pre, m_w_in, m_w_conv, m_w_pool, m_pool_scale, m_w_out, m_g_post, v_w_ada, v_b_ada, v_g_pre, v_w_in, v_w_conv, v_w_pool, v_pool_scale, v_w_out, v_g_post):
    mx, my, mc = _mesh_position()
    me = _block_id(mx, my, mc)
    chip = (2 * mx + my).astype(jnp.int32).reshape(1)
    core = mc.astype(jnp.int32).reshape(1)
    x0 = x[0]
    target = loss_target[0]
    conv_shard = w_conv.shape[-1]

    c_all = _all_gather_small(c, "all_gather_c")[:, 0, :]
    c_act, pieces = _modulation_columns(c_all, w_ada)
    mod_all = _all_gather_small(pieces, "all_gather_modulation")
    mod_mine = lax.dynamic_index_in_dim(mod_all, me, axis=1, keepdims=False)
    mod = mod_mine.reshape(N_DEV, DEPTH, W_IN_SHARD).transpose(1, 0, 2).reshape(DEPTH, 3 * D_MODEL) + b_ada
    zeros_d = jnp.zeros((DEPTH, 3, D_MODEL), F32)
    vec = jnp.concatenate([mod.reshape(DEPTH, 3, D_MODEL), g_pre[:, None], g_post[:, None], zeros_d], axis=1)

    w_conv_all = _all_gather_small(w_conv.reshape(DEPTH * 3, conv_shard), "all_gather_w_conv")
    w_conv_full = w_conv_all.transpose(1, 0, 2).reshape(DEPTH, 3, CONV_W)
    cps = jnp.concatenate([w_conv_full, pool_scale[:, None], jnp.zeros((DEPTH, 4, CONV_W), F32)], axis=1)
    wpool_b = _to_bf16(w_pool.reshape(DEPTH, POOL_ROWS, GROUP_D), "cast_w_pool").reshape(w_pool.shape)

    win_shards, wout_shards = _to_bf16(w_in, "cast_w_in"), _to_bf16(w_out, "cast_w_out")
    wins, wouts = [[a] for a in _run_exchange(
        _gather_weights_exchange(0, win_shards, wout_shards), "all_gather_weights_0")]

    xs, projs, ys = [x0], [], []
    for l in range(DEPTH):
        gather = _gather_weights_exchange(l + 1, win_shards, wout_shards) if l + 1 < DEPTH else None
        x_next, proj, y, *gathered = _forward_layer(l, xs[-1], vec, cps, wpool_b, wins[l], wouts[l], gather)
        xs.append(x_next)
        projs.append(proj)
        ys.append(y)
        if gather is not None:
            wins.append(gathered[0])
            wouts.append(gathered[1])
    dx, loss_tile = _loss_head(xs[DEPTH], target)

    slab_rows = [None] * DEPTH
    gwin_above = part_w = recv_w = part_o = recv_o = None
    for l in reversed(range(DEPTH)):
        hosted = None
        if gwin_above is not None:
            hosted = _merge_exchanges([_sibling_exchange([gwin_above], W_IN_KIND),
                                       _chips_exchange(l + 1, part_o, recv_o, W_OUT_KINDS)])
        dproj, gwout, gwpool, dcps, dvec1, *res = _backward_mixers(
            l, dx, ys[l], projs[l], vec, cps, wpool_b, wouts[l], hosted)
        chips_w = None
        if gwin_above is not None:
            recv_o = res[1:]
            part_w = _add_sibling_blocks(
                f"grad_add_w_in_{l + 1}", l + 1, [gwin_above], res[:1], core, part_w, W_IN_KIND)
            chips_w = _chips_exchange(l + 1, part_w, recv_w, W_IN_KIND)
        grads_o = [gwout, gwpool.reshape(POOL_ROWS, GROUP_D)]
        hosted = _merge_exchanges([chips_w, _sibling_exchange(grads_o, W_OUT_KINDS)])
        dx, gwin_above, dvec2, *res = _backward_input_proj(l, xs[l], dproj, dx, vec, wins[l], hosted)
        if chips_w is not None:
            recv_w, res = res[:1], res[1:]
        part_o = _add_sibling_blocks(f"grad_add_w_out_{l}", l, grads_o, res, core, part_o, W_OUT_KINDS)
        slab_rows[l] = jnp.concatenate(
            [dvec2[0], dvec2[1], dvec1[0], dvec2[2], dvec1[1], dcps[3], dcps[0], dcps[1], dcps[2]])
    grad_x = dx[None]

    loss_row = jnp.pad(loss_tile[0], (0, SLAB_COLS - LANES))
    slab = jnp.stack(slab_rows + [loss_row] + [jnp.zeros((SLAB_COLS,), F32)] * (SLAB_ROWS - DEPTH - 1))
    hosted = _merge_exchanges([_sibling_exchange([gwin_above], W_IN_KIND),
                               _chips_exchange(0, part_o, recv_o, W_OUT_KINDS)])
    slabs, *res = _all_gather_small(slab, "all_gather_small_grads", hosted)
    recv_o = res[1:]
    part_w = _add_sibling_blocks("grad_add_w_in_0", 0, [gwin_above], res[:1], core, part_w, W_IN_KIND)
    total = _sum_sources(slabs)
    loss = total[DEPTH, 0]
    o = 3 * D_MODEL
    g_b_ada = total[:DEPTH, :o]
    g_g_pre = total[:DEPTH, o:o + D_MODEL]
    g_g_post = total[:DEPTH, o + D_MODEL:o + 2 * D_MODEL]
    g_pool_scale = total[:DEPTH, o + 2 * D_MODEL:o + 2 * D_MODEL + POOL_W]
    g_conv_full = total[:DEPTH, o + 2 * D_MODEL + POOL_W:].reshape(DEPTH, 3, CONV_W)
    g_w_conv = lax.dynamic_slice_in_dim(g_conv_full, me * conv_shard, conv_shard, axis=2)

    dmod_all = slabs[:, :DEPTH, :o].reshape(N_DEV, DEPTH, N_DEV, W_IN_SHARD)
    dmod_cols = lax.dynamic_index_in_dim(dmod_all, me, axis=2, keepdims=False).transpose(1, 0, 2)
    g_w_ada, d_w_ada, nm_w_ada, nv_w_ada, *recv_w = _adamw_w_ada(
        w_ada, m_w_ada, v_w_ada, c_act.T, dmod_cols, _chips_exchange(0, part_w, recv_w, W_IN_KIND))

    g_w_in, d_w_in, nm_w_in, nv_w_in = _adamw_reduced(
        "adamw_w_in", w_in, m_w_in, v_w_in, part_w[0], recv_w[0], chip, ROW_TILE)
    g_w_out, d_w_out, nm_w_out, nv_w_out = _adamw_reduced(
        "adamw_w_out", w_out, m_w_out, v_w_out, part_o[0], recv_o[0], chip, W_OUT_SHARD)
    g_pool_all = _all_gather_small(_reduce_w_pool(part_o[1], recv_o[1], chip), "all_gather_grad_w_pool")
    g_w_pool = g_pool_all.reshape(N_DEV, DEPTH, POOL_SHARD, GROUP_D).transpose(1, 0, 2, 3).reshape(w_pool.shape)

    flat2 = lambda a: a.reshape(-1, a.shape[-1])
    small = _adamw_small([
        (b_ada, g_b_ada, m_b_ada, v_b_ada),
        (g_pre, g_g_pre, m_g_pre, v_g_pre),
        (flat2(w_conv), flat2(g_w_conv), flat2(m_w_conv), flat2(v_w_conv)),
        (flat2(w_pool), flat2(g_w_pool), flat2(m_w_pool), flat2(v_w_pool)),
        (pool_scale, g_pool_scale, m_pool_scale, v_pool_scale),
        (g_post, g_g_post, m_g_post, v_g_post),
    ])
    (d_b_ada, nm_b_ada, nv_b_ada), (d_g_pre, nm_g_pre, nv_g_pre), conv_upd, pool_upd, \
        (d_ps, nm_ps, nv_ps), (d_g_post, nm_g_post, nv_g_post) = small
    d_w_conv, nm_w_conv, nv_w_conv = (a.reshape(w_conv.shape) for a in conv_upd)
    d_w_pool, nm_w_pool, nv_w_pool = (a.reshape(w_pool.shape) for a in pool_upd)

    return (loss, grad_x,
            g_w_ada, g_b_ada, g_g_pre, g_w_in, g_w_conv, g_w_pool, g_pool_scale, g_w_out, g_g_post,
            d_w_ada, d_b_ada, d_g_pre, d_w_in, d_w_conv, d_w_pool, d_ps, d_w_out, d_g_post,
            nm_w_ada, nm_b_ada, nm_g_pre, nm_w_in, nm_w_conv, nm_w_pool, nm_ps, nm_w_out, nm_g_post,
            nv_w_ada, nv_b_ada, nv_g_pre, nv_w_in, nv_w_conv, nv_w_pool, nv_ps, nv_w_out, nv_g_post)
```

```python
import jax
import jax.numpy as jnp
from jax import lax
from jax.experimental import pallas as pl
from jax.experimental.pallas import tpu as pltpu

F32 = jnp.float32
BF16 = jnp.bfloat16

D_MODEL = 1024
DEPTH = 4
CONV_W = 512
POOL_W = 512
POOL_WINDOWS = (2, 4, 8, 16)
GROUP_D = 128
IN_COLS = 4 * CONV_W + 2 * POOL_W
NORM_EPS = 1e-6

ADAM_LR = 0.001
ADAM_B1 = 0.9
ADAM_B2 = 0.999
ADAM_EPS = 1e-08
ADAM_WD = 0.01
ADAM_STEP = 10

N_DEV = 8
N_CHIP = 4
N_OTHER_CHIPS = N_CHIP - 1
MESH = pl.DeviceIdType.MESH
W_IN_SHARD = IN_COLS // N_DEV
W_OUT_SHARD = D_MODEL // N_DEV
POOL_ROWS = len(POOL_WINDOWS) * GROUP_D
POOL_SHARD = POOL_ROWS // N_DEV

SUBLANES = 8
LANES = 128
VMEM_LIMIT_BYTES = 56 * 1024 * 1024
ROW_TILE = 256
POOL_HALO = 16
CONV_HALO = SUBLANES

SLAB_COLS = 3 * D_MODEL + D_MODEL + D_MODEL + POOL_W + 3 * CONV_W
SLAB_ROWS = SUBLANES

HBM = pl.BlockSpec(memory_space=pl.ANY)


def _params(**kw):
    return pltpu.CompilerParams(vmem_limit_bytes=VMEM_LIMIT_BYTES, **kw)


def _sigmoid(v):
    return 1.0 / (1.0 + jnp.exp(-v))


def _dot(a, b):
    return jnp.dot(a, b, preferred_element_type=F32)


def _dot_tn(a, b):
    return lax.dot_general(a, b, (((0,), (0,)), ((), ())), preferred_element_type=F32)


def _dot_nt(a, b):
    return lax.dot_general(a, b, (((1,), (1,)), ((), ())), preferred_element_type=F32)


def _rows_from_before(v, k):
    return pltpu.roll(v, k, 0)


def _rows_from_after(v, k):
    return pltpu.roll(v, v.shape[0] - k, 0)


def _window_counts(t0, rows):
    return (lax.broadcasted_iota(jnp.int32, (rows, 1), 0) + (t0 + 1)).astype(F32)


def _split_proj(p32):
    cw = CONV_W
    return (p32[:, 0 * cw:1 * cw], p32[:, 1 * cw:2 * cw], p32[:, 2 * cw:3 * cw], p32[:, 3 * cw:4 * cw],
            p32[:, 4 * cw:4 * cw + POOL_W], p32[:, 4 * cw + POOL_W:])


def _mixer_forward(p32, z_halo, up_halo, cps, wpool_ref, t0):
    tm = p32.shape[0]
    u_a, b_a, c_a, g_a, u_p, g_p = _split_proj(p32)
    w0, w1, w2, ps = cps[0:1, :], cps[1:2, :], cps[2:3, :], cps[3:4, :]
    z = c_a * u_a
    zcat = jnp.concatenate([z_halo, z], axis=0)
    z1 = _rows_from_before(zcat, 1)[CONV_HALO:]
    z2 = _rows_from_before(zcat, 2)[CONV_HALO:]
    conv = w0 * z2 + w1 * z1 + w2 * z
    sig_a = _sigmoid(g_a)
    silu_a = g_a * sig_a
    y_a = b_a * conv * silu_a

    pcat = jnp.concatenate([up_halo, u_p], axis=0)
    counts = _window_counts(t0, tm)
    pooled, mixed, inv = [], [], []
    for g, w in enumerate(POOL_WINDOWS):
        cols = slice(g * GROUP_D, (g + 1) * GROUP_D)
        s = pcat[:, cols]
        step = 1
        while step < w:
            s = s + _rows_from_before(s, step)
            step *= 2
        inv_g = 1.0 / jnp.minimum(counts, float(w))
        pooled_g = s[POOL_HALO:] * inv_g - u_p[:, cols]
        pooled.append(pooled_g)
        inv.append(inv_g)
        mixed.append(_dot(pooled_g.astype(BF16), wpool_ref[g]))
    mixed = jnp.concatenate(mixed, axis=1)
    sig_p = _sigmoid(g_p)
    silu_p = g_p * sig_p
    y_p = mixed * ps * silu_p
    ycat = jnp.concatenate([y_a, y_p], axis=1).astype(BF16)
    return dict(u_a=u_a, b_a=b_a, c_a=c_a, g_a=g_a, u_p=u_p, g_p=g_p, z=z, z1=z1, z2=z2, conv=conv, sig_a=sig_a,
                silu_a=silu_a, pooled=pooled, inv=inv, mixed=mixed, sig_p=sig_p, silu_p=silu_p, ycat=ycat,
                w0=w0, w1=w1, w2=w2, ps=ps)


def _layer_spec(shape, layer):
    nd = len(shape)
    return pl.BlockSpec((None,) + tuple(shape[1:]), lambda i, _l=layer, _n=nd: (_l,) + (0,) * (_n - 1))


def _whole_spec(shape):
    return pl.BlockSpec(tuple(shape), lambda i, _n=len(shape): (0,) * _n)


def _mesh_position():
    return lax.axis_index("x"), lax.axis_index("y"), lax.axis_index("c")


def _block_id(x, y, c):
    return 4 * x + 2 * y + c


def _other_chips(x, y):
    return [(x ^ 1, y), (x, y ^ 1), (x ^ 1, y ^ 1)]


def _col_block(ref, blk):
    return ref.at[:, pl.ds(pl.multiple_of(blk * W_IN_SHARD, LANES), W_IN_SHARD)]


def _row_block(rows):
    def block(ref, blk):
        return ref.at[pl.ds(pl.multiple_of(blk * rows, rows), rows), :]
    return block


_BLOCK_OF = (_col_block, _row_block(W_OUT_SHARD), _row_block(POOL_SHARD))
_BLOCK_SHAPES = ((D_MODEL, W_IN_SHARD), (W_OUT_SHARD, D_MODEL), (POOL_SHARD, GROUP_D))


class _Exchange:
    def __init__(self, inputs, out_shapes, aliases, sem_shapes, make):
        self.inputs, self.out_shapes, self.aliases, self.sem_shapes, self.make = (
            list(inputs), list(out_shapes), dict(aliases), list(sem_shapes), make)


def _run_exchange(exchange, name):
    n_in, n_out = len(exchange.inputs), len(exchange.out_shapes)

    def body(*refs):
        start, finish = exchange.make(refs[:n_in], refs[n_in:n_in + n_out], refs[n_in + n_out:])
        start()
        finish()

    return pl.pallas_call(
        body, name=name, in_specs=[HBM] * n_in, out_specs=[HBM] * n_out, out_shape=exchange.out_shapes,
        scratch_shapes=exchange.sem_shapes, input_output_aliases=exchange.aliases, compiler_params=_params(),
    )(*exchange.inputs)


_SEM = pl.BlockSpec(memory_space=pltpu.SEMAPHORE)
_DATAFLOW = pltpu.SideEffectType.DATAFLOW_SIDE_EFFECTING


def _start_exchange(exchange, name):
    n_in, n_out, n_sem = len(exchange.inputs), len(exchange.out_shapes), len(exchange.sem_shapes)
    sources = [i for i in range(n_in) if i not in exchange.aliases]
    aliases = {i: n_sem + k for k, i in enumerate(sources)}
    aliases.update({i: n_sem + len(sources) + o for i, o in exchange.aliases.items()})

    def body(*refs):
        in_refs = refs[:n_in]
        sems = refs[n_in:n_in + n_sem]
        out_refs = refs[n_in + n_sem + len(sources):n_in + n_sem + len(sources) + n_out]
        exchange.make(in_refs, out_refs, sems)[0]()
        refs[-1][...] = jnp.zeros_like(refs[-1])

    outs = pl.pallas_call(
        body, name=name, in_specs=[HBM] * n_in,
        out_specs=[_SEM] * n_sem + [HBM] * (len(sources) + n_out) + [pl.BlockSpec(memory_space=pltpu.VMEM)],
        out_shape=(exchange.sem_shapes + [pltpu.HBM(exchange.inputs[i].shape, exchange.inputs[i].dtype) for i in sources]
                   + [pltpu.HBM(s.shape, s.dtype) for s in exchange.out_shapes]
                   + [jax.ShapeDtypeStruct((SUBLANES, LANES), F32)]),
        input_output_aliases=aliases, compiler_params=_params(has_side_effects=_DATAFLOW),
    )(*exchange.inputs)
    return outs[:n_sem], outs[n_sem:n_sem + len(sources)], outs[n_sem + len(sources):-1], outs[-1]


def _finish_exchange(exchange, name, sems, sources, landing, after):
    n_src, n_out, n_sem = len(sources), len(landing), len(sems)
    n_in = len(exchange.inputs)
    source_at = [i for i in range(n_in) if i not in exchange.aliases]

    def body(*refs):
        src_refs, out_refs = refs[:n_src], refs[n_src:n_src + n_out]
        sem_refs = refs[n_src + n_out:n_src + n_out + n_sem]
        in_refs = [None] * n_in
        for k, i in enumerate(source_at):
            in_refs[i] = src_refs[k]
        for i, o in exchange.aliases.items():
            in_refs[i] = out_refs[o]
        exchange.make(in_refs, out_refs, sem_refs)[1]()

    arrays = list(sources) + list(landing)
    outs = pl.pallas_call(
        body, name=name, in_specs=[HBM] * len(arrays) + [_SEM] * n_sem + [HBM] * len(after),
        out_specs=[HBM] * len(arrays), out_shape=[pltpu.HBM(a.shape, a.dtype) for a in arrays],
        input_output_aliases={i: i for i in range(len(arrays))}, compiler_params=_params(has_side_effects=_DATAFLOW),
    )(*arrays, *sems, *after)
    return outs[:n_src], outs[n_src:]


def _gather_weights_exchange(layer, win_shards, wout_shards):
    n_arr = 2

    def make(in_refs, out_refs, sems):
        send_sems, recv_sems, local_sems = sems
        x, y, c = _mesh_position()
        me = _block_id(x, y, c)
        sibling = (x, y, 1 - c)
        chips = _other_chips(x, y)
        shards = [in_refs[a].at[layer] for a in range(n_arr)]

        def copy(a, k, blk, to, src=None):
            window = _BLOCK_OF[a](out_refs[a], blk)
            return pltpu.make_async_remote_copy(
                src_ref=window if src is None else src, dst_ref=window, send_sem=send_sems.at[a, k],
                recv_sem=recv_sems.at[a, k], device_id=to, device_id_type=MESH)

        own = [pltpu.make_async_copy(shards[a], _BLOCK_OF[a](out_refs[a], me), local_sems.at[a])
               for a in range(n_arr)]
        first = []
        for a in range(n_arr):
            first.append(copy(a, 0, me, sibling, src=shards[a]))
            first += [copy(a, 1 + j, me, (*chip, c), src=shards[a]) for j, chip in enumerate(chips)]

        def start():
            for cp in own + first:
                cp.start()

        def finish():
            passed = []
            for j, chip in enumerate(chips):
                blk = _block_id(*chip, c)
                for a in range(n_arr):
                    copy(a, 1 + j, blk, (x, y, c)).wait_recv()
                    fwd = copy(a, 4 + j, blk, sibling)
                    fwd.start()
                    passed.append(fwd)
            for a in range(n_arr):
                copy(a, 0, _block_id(x, y, 1 - c), (x, y, c)).wait_recv()
                for j, chip in enumerate(chips):
                    copy(a, 4 + j, _block_id(*chip, 1 - c), (x, y, c)).wait_recv()
            for cp in first + passed:
                cp.wait_send()
            for cp in own:
                cp.wait()

        return start, finish

    return _Exchange(
        [win_shards, wout_shards],
        [jax.ShapeDtypeStruct((D_MODEL, IN_COLS), BF16), jax.ShapeDtypeStruct((D_MODEL, D_MODEL), BF16)], {},
        [pltpu.SemaphoreType.DMA((n_arr, N_DEV - 1)), pltpu.SemaphoreType.DMA((n_arr, N_DEV - 1)),
         pltpu.SemaphoreType.DMA((n_arr,))], make)


W_IN_KIND = (0,)
W_OUT_KINDS = (1, 2)


def _merge_exchanges(exchanges):
    exchanges = [e for e in exchanges if e is not None]
    if not exchanges:
        return None
    inputs, out_shapes, sem_shapes, aliases = [], [], [], {}
    for e in exchanges:
        aliases.update({len(inputs) + i: len(out_shapes) + o for i, o in e.aliases.items()})
        inputs += e.inputs
        out_shapes += e.out_shapes
        sem_shapes += e.sem_shapes

    def make(in_refs, out_refs, sems):
        made, at = [], [0, 0, 0]
        for e in exchanges:
            n = (len(e.inputs), len(e.out_shapes), len(e.sem_shapes))
            made.append(e.make(in_refs[at[0]:at[0] + n[0]], out_refs[at[1]:at[1] + n[1]], sems[at[2]:at[2] + n[2]]))
            at = [a + b for a, b in zip(at, n)]

        def start():
            for s, _ in made:
                s()

        def finish():
            for _, f in made:
                f()

        return start, finish

    return _Exchange(inputs, out_shapes, aliases, sem_shapes, make)


def _sibling_exchange(grads, kinds):
    n_arr = len(grads)

    def make(in_refs, out_refs, sems):
        send_sems, recv_sems = sems
        x, y, c = _mesh_position()
        copies = [pltpu.make_async_remote_copy(
            src_ref=_BLOCK_OF[kinds[a]](in_refs[a], 2 * q + (1 - c)), dst_ref=out_refs[a].at[q],
            send_sem=send_sems.at[a, q], recv_sem=recv_sems.at[a, q], device_id=(x, y, 1 - c), device_id_type=MESH)
            for a in range(n_arr) for q in range(N_CHIP)]

        def start():
            for cp in copies:
                cp.start()

        def finish():
            for cp in copies:
                cp.wait_recv()
            for cp in copies:
                cp.wait_send()

        return start, finish

    return _Exchange(
        grads, [jax.ShapeDtypeStruct((N_CHIP,) + _BLOCK_SHAPES[k], BF16) for k in kinds], {},
        [pltpu.SemaphoreType.DMA((n_arr, N_CHIP)), pltpu.SemaphoreType.DMA((n_arr, N_CHIP))], make)


def _chips_exchange(layer, partials, received, kinds):
    n_arr = len(partials)

    def make(in_refs, out_refs, sems):
        send_sems, recv_sems = sems
        x, y, c = _mesh_position()
        copies = [pltpu.make_async_remote_copy(
            src_ref=in_refs[a].at[2 * qx + qy, layer], dst_ref=out_refs[a].at[j, layer],
            send_sem=send_sems.at[a * N_OTHER_CHIPS + j], recv_sem=recv_sems.at[a * N_OTHER_CHIPS + j],
            device_id=(qx, qy, c), device_id_type=MESH)
            for a in range(n_arr) for j, (qx, qy) in enumerate(_other_chips(x, y))]

        def start():
            for cp in copies:
                cp.start()

        def finish():
            for cp in copies:
                cp.wait_recv()
            for cp in copies:
                cp.wait_send()

        return start, finish

    inputs = list(partials)
    aliases = {}
    if received is not None:
        inputs += list(received)
        aliases = {n_arr + a: a for a in range(n_arr)}
    return _Exchange(
        inputs, [jax.ShapeDtypeStruct((N_OTHER_CHIPS, DEPTH) + _BLOCK_SHAPES[k], BF16) for k in kinds], aliases,
        [pltpu.SemaphoreType.DMA((n_arr * N_OTHER_CHIPS,)), pltpu.SemaphoreType.DMA((n_arr * N_OTHER_CHIPS,))], make)


def _host(exchange, args, in_specs, out_shape, out_specs, scratch):
    n_own = (len(args), len(out_shape), len(scratch))
    if exchange is None:
        return {}, lambda refs: (refs, None)
    n_ex = (len(exchange.inputs), len(exchange.out_shapes), len(exchange.sem_shapes))
    aliases = {n_own[0] + i: n_own[1] + o for i, o in exchange.aliases.items()}
    args += exchange.inputs
    in_specs += [HBM] * n_ex[0]
    out_shape += exchange.out_shapes
    out_specs += [HBM] * n_ex[1]
    scratch += exchange.sem_shapes

    def split(refs):
        own, theirs, at = [], [], 0
        for mine, ex in zip(n_own, n_ex):
            own += refs[at:at + mine]
            theirs.append(refs[at + mine:at + mine + ex])
            at += mine + ex
        return own, exchange.make(*theirs)

    return aliases, split


def _all_gather_small(v, name, exchange=None):
    vmem = pl.BlockSpec(memory_space=pltpu.VMEM)
    args, in_specs = [v], [vmem]
    out_shape, out_specs = [jax.ShapeDtypeStruct((N_DEV,) + v.shape, v.dtype)], [vmem]
    scratch = [pltpu.SemaphoreType.DMA((N_DEV - 1,)), pltpu.SemaphoreType.DMA((N_DEV - 1,))]
    aliases, split = _host(exchange, args, in_specs, out_shape, out_specs, scratch)

    def body(*refs):
        (v_ref, out_ref, send_sems, recv_sems), hosted = split(refs)
        if hosted is not None:
            hosted[0]()
        x, y, c = _mesh_position()
        me = _block_id(x, y, c)
        out_ref[me] = v_ref[...]
        sends = []
        for k in range(1, N_DEV):
            px, py, pc = x ^ ((k >> 2) & 1), y ^ ((k >> 1) & 1), c ^ (k & 1)
            send = pltpu.make_async_remote_copy(
                src_ref=v_ref, dst_ref=out_ref.at[me], send_sem=send_sems.at[k - 1], recv_sem=recv_sems.at[k - 1],
                device_id=(px, py, pc), device_id_type=MESH)
            send.start()
            sends.append((send, _block_id(px, py, pc)))
        for k, (send, peer) in enumerate(sends):
            pltpu.make_async_remote_copy(
                src_ref=v_ref, dst_ref=out_ref.at[peer], send_sem=send_sems.at[k], recv_sem=recv_sems.at[k],
                device_id=(x, y, c), device_id_type=MESH).wait_recv()
        for send, _ in sends:
            send.wait_send()
        if hosted is not None:
            hosted[1]()

    outs = pl.pallas_call(
        body, name=name, in_specs=in_specs, out_specs=out_specs, out_shape=out_shape, scratch_shapes=scratch,
        input_output_aliases=aliases, compiler_params=_params(),
    )(*args)
    return outs[0] if exchange is None else outs


def _forward_layer(layer, x, vec, cps, wpool, win, wout, exchange):
    t_len = x.shape[0]
    n_tiles = t_len // ROW_TILE
    row = lambda cols: pl.BlockSpec((ROW_TILE, cols), lambda i: (i, 0))
    args = [x, vec, cps, wpool, win, wout]
    in_specs = [row(D_MODEL), _layer_spec(vec.shape, layer), _layer_spec(cps.shape, layer),
                _layer_spec(wpool.shape, layer), _whole_spec(win.shape), _whole_spec(wout.shape)]
    out_shape = [jax.ShapeDtypeStruct((t_len, D_MODEL), F32), jax.ShapeDtypeStruct((t_len, IN_COLS), BF16),
                 jax.ShapeDtypeStruct((t_len, D_MODEL), BF16)]
    out_specs = [row(D_MODEL), row(IN_COLS), row(D_MODEL)]
    scratch = [pltpu.VMEM((CONV_HALO, CONV_W), F32), pltpu.VMEM((POOL_HALO, POOL_W), F32)]
    aliases, split = _host(exchange, args, in_specs, out_shape, out_specs, scratch)

    def body(*refs):
        (x_ref, vec_ref, cps_ref, wpool_ref, win_ref, wout_ref, xo_ref, proj_ref, y_ref, zc_ref, pc_ref), hosted = (
            split(refs))
        i = pl.program_id(0)

        @pl.when(i == 0)
        def _():
            if hosted is not None:
                hosted[0]()
            zc_ref[...] = jnp.zeros_like(zc_ref)
            pc_ref[...] = jnp.zeros_like(pc_ref)

        x_t = x_ref[...]
        shift, scale, gate = vec_ref[0:1, :], vec_ref[1:2, :], vec_ref[2:3, :]
        g_pre, g_post = vec_ref[3:4, :], vec_ref[4:5, :]
        rx = lax.rsqrt(jnp.mean(x_t * x_t, axis=-1, keepdims=True) + NORM_EPS)
        h = (x_t * rx) * g_pre * (1.0 + scale) + shift
        proj_b = _dot(h.astype(BF16), win_ref[...]).astype(BF16)
        proj_ref[...] = proj_b
        mx = _mixer_forward(proj_b.astype(F32), zc_ref[...], pc_ref[...], cps_ref[...], wpool_ref, i * ROW_TILE)
        zc_ref[...] = mx["z"][ROW_TILE - CONV_HALO:]
        pc_ref[...] = mx["u_p"][ROW_TILE - POOL_HALO:]
        y_b = _dot(mx["ycat"], wout_ref[...]).astype(BF16)
        y_ref[...] = y_b
        y_t = y_b.astype(F32)
        ry = lax.rsqrt(jnp.mean(y_t * y_t, axis=-1, keepdims=True) + NORM_EPS)
        xo_ref[...] = x_t + gate * (y_t * ry * g_post)

        if hosted is not None:
            pl.when(i == n_tiles - 1)(hosted[1])

    return pl.pallas_call(
        body, name=f"forward_layer_{layer}", grid=(n_tiles,), in_specs=in_specs, out_specs=out_specs,
        out_shape=out_shape, scratch_shapes=scratch, input_output_aliases=aliases,
        compiler_params=_params(dimension_semantics=("arbitrary",)),
    )(*args)


def _loss_head(x_final, target):
    t_len = x_final.shape[0]
    n_tiles = t_len // ROW_TILE

    def body(x_ref, t_ref, dx_ref, loss_ref):
        @pl.when(pl.program_id(0) == 0)
        def _():
            loss_ref[...] = jnp.zeros_like(loss_ref)

        err = x_ref[...] - t_ref[...]
        dx_ref[...] = err * (1.0 / D_MODEL)
        loss_ref[...] += jnp.sum(err * err) * (0.5 / D_MODEL)

    row = pl.BlockSpec((ROW_TILE, D_MODEL), lambda i: (i, 0))
    return pl.pallas_call(
        body, name="loss_head", grid=(n_tiles,), in_specs=[row, row],
        out_specs=[row, pl.BlockSpec((SUBLANES, LANES), lambda i: (0, 0))],
        out_shape=[jax.ShapeDtypeStruct((t_len, D_MODEL), F32), jax.ShapeDtypeStruct((SUBLANES, LANES), F32)],
        compiler_params=_params(dimension_semantics=("arbitrary",)),
    )(x_final, target)


def _backward_mixers(layer, dxo, y, proj, vec, cps, wpool, wout, exchange):
    t_len = dxo.shape[0]
    n_tiles = t_len // ROW_TILE
    halo_per_tile = ROW_TILE // POOL_HALO
    rev = lambda cols: pl.BlockSpec((ROW_TILE, cols), lambda i: (n_tiles - 1 - i, 0))
    halo_spec = pl.BlockSpec(
        (POOL_HALO, IN_COLS), lambda i: (jnp.maximum((n_tiles - 1 - i) * halo_per_tile - 1, 0), 0))
    gwpool_shape = (len(POOL_WINDOWS), GROUP_D, GROUP_D)
    args = [dxo, y, proj, proj, vec, cps, wpool, wout]
    in_specs = [rev(D_MODEL), rev(D_MODEL), rev(IN_COLS), halo_spec, _layer_spec(vec.shape, layer),
                _layer_spec(cps.shape, layer), _layer_spec(wpool.shape, layer), _whole_spec(wout.shape)]
    out_shape = [jax.ShapeDtypeStruct((t_len, IN_COLS), BF16), jax.ShapeDtypeStruct((D_MODEL, D_MODEL), BF16),
                 jax.ShapeDtypeStruct(gwpool_shape, BF16), jax.ShapeDtypeStruct((SUBLANES, CONV_W), F32),
                 jax.ShapeDtypeStruct((SUBLANES, D_MODEL), F32)]
    out_specs = [rev(IN_COLS), _whole_spec((D_MODEL, D_MODEL)), _whole_spec(gwpool_shape),
                 _whole_spec((SUBLANES, CONV_W)), _whole_spec((SUBLANES, D_MODEL))]
    scratch = [pltpu.VMEM((D_MODEL, D_MODEL), F32), pltpu.VMEM(gwpool_shape, F32),
               pltpu.VMEM((CONV_HALO, CONV_W), F32), pltpu.VMEM((POOL_HALO, POOL_W), F32)]
    aliases, split = _host(exchange, args, in_specs, out_shape, out_specs, scratch)

    def body(*refs):
        (dxo_ref, y_ref, proj_ref, projh_ref, vec_ref, cps_ref, wpool_ref, wout_ref, dproj_ref, gwout_ref, gwpool_ref,
         dcps_ref, dvec_ref, gwout_acc, gwpool_acc, dcc_ref, qc_ref), hosted = split(refs)
        i = pl.program_id(0)
        tile = n_tiles - 1 - i

        @pl.when(i == 0)
        def _():
            if hosted is not None:
                hosted[0]()
            gwout_acc[...] = jnp.zeros_like(gwout_acc)
            gwpool_acc[...] = jnp.zeros_like(gwpool_acc)
            dcps_ref[...] = jnp.zeros_like(dcps_ref)
            dvec_ref[...] = jnp.zeros_like(dvec_ref)
            dcc_ref[...] = jnp.zeros_like(dcc_ref)
            qc_ref[...] = jnp.zeros_like(qc_ref)

        gate, g_post = vec_ref[2:3, :], vec_ref[4:5, :]
        dxo_t = dxo_ref[...]
        y_t = y_ref[...].astype(F32)
        ry = lax.rsqrt(jnp.mean(y_t * y_t, axis=-1, keepdims=True) + NORM_EPS)
        yh = y_t * ry
        dxy = dxo_t * yh
        dvec_ref[0:1, :] += jnp.sum(dxy * g_post, axis=0, keepdims=True)
        dvec_ref[1:2, :] += jnp.sum(dxy * gate, axis=0, keepdims=True)
        dyh = dxo_t * (gate * g_post)
        dy_b = (ry * (dyh - yh * jnp.mean(dyh * yh, axis=-1, keepdims=True))).astype(BF16)

        halo = jnp.where(tile > 0, projh_ref[...].astype(F32), 0.0)
        hu_a, _, hc_a, _, hu_p, _ = _split_proj(halo)
        z_halo = (hc_a * hu_a)[POOL_HALO - CONV_HALO:]
        mx = _mixer_forward(proj_ref[...].astype(F32), z_halo, hu_p, cps_ref[...], wpool_ref, tile * ROW_TILE)

        gwout_acc[...] += _dot_tn(mx["ycat"], dy_b)
        dycat = _dot_nt(dy_b, wout_ref[...])
        dy_a, dy_p = dycat[:, :CONV_W], dycat[:, CONV_W:]

        t_a = dy_a * mx["silu_a"]
        db_a = t_a * mx["conv"]
        dconv = t_a * mx["b_a"]
        dg_a = dy_a * mx["b_a"] * mx["conv"] * (mx["sig_a"] * (1.0 + mx["g_a"] * (1.0 - mx["sig_a"])))
        dccat = jnp.concatenate([dconv, dcc_ref[...]], axis=0)
        dc1 = _rows_from_after(dccat, 1)[:ROW_TILE]
        dc2 = _rows_from_after(dccat, 2)[:ROW_TILE]
        dz = mx["w2"] * dconv + mx["w1"] * dc1 + mx["w0"] * dc2
        dcc_ref[...] = dconv[:CONV_HALO]
        dcps_ref[0:1, :] += jnp.sum(dconv * mx["z2"], axis=0, keepdims=True)
        dcps_ref[1:2, :] += jnp.sum(dconv * mx["z1"], axis=0, keepdims=True)
        dcps_ref[2:3, :] += jnp.sum(dconv * mx["z"], axis=0, keepdims=True)
        du_a = dz * mx["c_a"]
        dc_a = dz * mx["u_a"]

        t_p = dy_p * mx["silu_p"]
        dcps_ref[3:4, :] += jnp.sum(t_p * mx["mixed"], axis=0, keepdims=True)
        dmixed = (t_p * mx["ps"]).astype(BF16)
        dg_p = dy_p * mx["mixed"] * mx["ps"] * (mx["sig_p"] * (1.0 + mx["g_p"] * (1.0 - mx["sig_p"])))
        du_p, q_head = [], []
        for g, w in enumerate(POOL_WINDOWS):
            cols = slice(g * GROUP_D, (g + 1) * GROUP_D)
            dm_g = dmixed[:, cols]
            dpooled_g = _dot_nt(dm_g, wpool_ref[g])
            gwpool_acc[g] += _dot_tn(mx["pooled"][g].astype(BF16), dm_g)
            q_g = dpooled_g * mx["inv"][g]
            q_head.append(q_g[:POOL_HALO])
            s = jnp.concatenate([q_g, qc_ref[:, cols]], axis=0)
            step = 1
            while step < w:
                s = s + _rows_from_after(s, step)
                step *= 2
            du_p.append(s[:ROW_TILE] - dpooled_g)
        qc_ref[...] = jnp.concatenate(q_head, axis=1)
        dproj_ref[...] = jnp.concatenate([du_a, db_a, dc_a, dg_a] + du_p + [dg_p], axis=1).astype(BF16)

        @pl.when(i == n_tiles - 1)
        def _():
            gwout_ref[...] = gwout_acc[...].astype(BF16)
            gwpool_ref[...] = gwpool_acc[...].astype(BF16)
            if hosted is not None:
                hosted[1]()

    return pl.pallas_call(
        body, name=f"backward_mixers_{layer}", grid=(n_tiles,), in_specs=in_specs, out_specs=out_specs,
        out_shape=out_shape, scratch_shapes=scratch, input_output_aliases=aliases,
        compiler_params=_params(dimension_semantics=("arbitrary",)),
    )(*args)


def _backward_input_proj(layer, x, dproj, dxo, vec, win, exchange):
    t_len = x.shape[0]
    n_tiles = t_len // ROW_TILE
    row = lambda cols: pl.BlockSpec((ROW_TILE, cols), lambda i: (i, 0))
    args = [x, dproj, dxo, vec, win]
    in_specs = [row(D_MODEL), row(IN_COLS), row(D_MODEL), _layer_spec(vec.shape, layer), _whole_spec(win.shape)]
    out_shape = [jax.ShapeDtypeStruct((t_len, D_MODEL), F32), jax.ShapeDtypeStruct((D_MODEL, IN_COLS), BF16),
                 jax.ShapeDtypeStruct((SUBLANES, D_MODEL), F32)]
    out_specs = [row(D_MODEL), _whole_spec((D_MODEL, IN_COLS)), _whole_spec((SUBLANES, D_MODEL))]
    scratch = [pltpu.VMEM((D_MODEL, IN_COLS), F32)]
    aliases, split = _host(exchange, args, in_specs, out_shape, out_specs, scratch)

    def body(*refs):
        (x_ref, dproj_ref, dxo_ref, vec_ref, win_ref, dx_ref, gwin_ref, dvec_ref, gwin_acc), hosted = split(refs)
        i = pl.program_id(0)

        @pl.when(i == 0)
        def _():
            if hosted is not None:
                hosted[0]()
            gwin_acc[...] = jnp.zeros_like(gwin_acc)
            dvec_ref[...] = jnp.zeros_like(dvec_ref)

        shift, scale, g_pre = vec_ref[0:1, :], vec_ref[1:2, :], vec_ref[3:4, :]
        x_t = x_ref[...]
        rx = lax.rsqrt(jnp.mean(x_t * x_t, axis=-1, keepdims=True) + NORM_EPS)
        xn = x_t * rx
        mod_scale = 1.0 + scale
        h_b = (xn * g_pre * mod_scale + shift).astype(BF16)
        dproj_t = dproj_ref[...]
        gwin_acc[...] += _dot_tn(h_b, dproj_t)
        dh = _dot_nt(dproj_t, win_ref[...])
        dvec_ref[0:1, :] += jnp.sum(dh, axis=0, keepdims=True)
        dhx = dh * xn
        dvec_ref[1:2, :] += jnp.sum(dhx * g_pre, axis=0, keepdims=True)
        dvec_ref[2:3, :] += jnp.sum(dhx * mod_scale, axis=0, keepdims=True)
        dxn = dh * (g_pre * mod_scale)
        dx_ref[...] = dxo_ref[...] + rx * (dxn - xn * jnp.mean(dxn * xn, axis=-1, keepdims=True))

        @pl.when(i == n_tiles - 1)
        def _():
            gwin_ref[...] = gwin_acc[...].astype(BF16)
            if hosted is not None:
                hosted[1]()

    return pl.pallas_call(
        body, name=f"backward_input_proj_{layer}", grid=(n_tiles,), in_specs=in_specs, out_specs=out_specs,
        out_shape=out_shape, scratch_shapes=scratch, input_output_aliases=aliases,
        compiler_params=_params(dimension_semantics=("arbitrary",)),
    )(*args)


def _add_sibling_blocks(name, layer, grads, received, core, partials, kinds):
    n_arr = len(grads)

    def body(core_ref, *refs):
        mine, theirs, outs = refs[:n_arr], refs[n_arr:2 * n_arr], refs[-n_arr:]
        for a in range(n_arr):
            outs[a][...] = (mine[a][...].astype(F32) + theirs[a][...].astype(F32)).astype(BF16)

    own_of_kind = [
        pl.BlockSpec((D_MODEL, W_IN_SHARD), lambda q, core_ref: (0, 2 * q + core_ref[0])),
        pl.BlockSpec((W_OUT_SHARD, D_MODEL), lambda q, core_ref: (2 * q + core_ref[0], 0)),
        pl.BlockSpec((POOL_SHARD, GROUP_D), lambda q, core_ref: (2 * q + core_ref[0], 0)),
    ]
    shapes = [_BLOCK_SHAPES[k] for k in kinds]
    recv_specs = [pl.BlockSpec((None,) + s, lambda q, core_ref: (q, 0, 0)) for s in shapes]
    out_specs = [pl.BlockSpec((None, None) + s, lambda q, core_ref: (q, layer, 0, 0)) for s in shapes]
    args = [core, *grads, *received]
    in_specs = [own_of_kind[k] for k in kinds] + recv_specs
    aliases = {}
    if partials is not None:
        aliases = {len(args) + a: a for a in range(n_arr)}
        args += list(partials)
        in_specs += [HBM] * n_arr
    return pl.pallas_call(
        body, name=name,
        grid_spec=pltpu.PrefetchScalarGridSpec(
            num_scalar_prefetch=1, grid=(N_CHIP,), in_specs=in_specs, out_specs=out_specs),
        out_shape=[jax.ShapeDtypeStruct((N_CHIP, DEPTH) + s, BF16) for s in shapes],
        input_output_aliases=aliases,
        compiler_params=_params(dimension_semantics=("arbitrary",)),
    )(*args)


def _modulation_columns(c_all, w_ada):
    def body(c_ref, w_ref, cact_ref, out_ref):
        c_t = c_ref[...]
        c_act = c_t * _sigmoid(c_t)
        cact_ref[...] = c_act
        out_ref[...] = jnp.dot(c_act, w_ref[...], preferred_element_type=F32, precision=lax.Precision.HIGHEST)

    return pl.pallas_call(
        body, name="modulation_columns", grid=(DEPTH,),
        in_specs=[pl.BlockSpec((N_DEV, D_MODEL), lambda l: (0, 0)),
                  pl.BlockSpec((None, D_MODEL, W_IN_SHARD), lambda l: (l, 0, 0))],
        out_specs=[pl.BlockSpec((N_DEV, D_MODEL), lambda l: (0, 0)),
                   pl.BlockSpec((N_DEV, W_IN_SHARD), lambda l: (0, l))],
        out_shape=[jax.ShapeDtypeStruct((N_DEV, D_MODEL), F32),
                   jax.ShapeDtypeStruct((N_DEV, DEPTH * W_IN_SHARD), F32)],
        compiler_params=_params(dimension_semantics=("arbitrary",)),
    )(c_all, w_ada)


def _adamw(w, g, m, v):
    m_new = ADAM_B1 * m + (1.0 - ADAM_B1) * g
    v_new = ADAM_B2 * v + (1.0 - ADAM_B2) * (g * g)
    m_hat = m_new / (1.0 - ADAM_B1 ** ADAM_STEP)
    v_hat = v_new / (1.0 - ADAM_B2 ** ADAM_STEP)
    delta = -ADAM_LR * (m_hat / (jnp.sqrt(v_hat) + ADAM_EPS) + ADAM_WD * w)
    return delta, m_new, v_new


def _adamw_w_ada(w, m, v, c_act_t, dmod_cols, exchange):
    big = pl.BlockSpec((None, D_MODEL, W_IN_SHARD), lambda l: (l, 0, 0))
    args = [w, m, v, c_act_t, dmod_cols]
    in_specs = [big, big, big, pl.BlockSpec((D_MODEL, N_DEV), lambda l: (0, 0)),
                pl.BlockSpec((None, N_DEV, W_IN_SHARD), lambda l: (l, 0, 0))]
    out_shape, out_specs, scratch = [jax.ShapeDtypeStruct(w.shape, F32)] * 4, [big] * 4, []
    aliases, split = _host(exchange, args, in_specs, out_shape, out_specs, scratch)

    def body(*refs):
        (w_ref, m_ref, v_ref, ct_ref, dm_ref, g_ref, d_ref, mo_ref, vo_ref), hosted = split(refs)
        if hosted is not None:
            pl.when(pl.program_id(0) == 0)(hosted[0])
        g = ct_ref[:, 0:1] * dm_ref[0:1, :]
        for b in range(1, N_DEV):
            g = g + ct_ref[:, b:b + 1] * dm_ref[b:b + 1, :]
        g_ref[...] = g
        d_ref[...], mo_ref[...], vo_ref[...] = _adamw(w_ref[...], g, m_ref[...], v_ref[...])
        if hosted is not None:
            pl.when(pl.program_id(0) == DEPTH - 1)(hosted[1])

    return pl.pallas_call(
        body, name="adamw_w_ada", grid=(DEPTH,), in_specs=in_specs, out_specs=out_specs, out_shape=out_shape,
        scratch_shapes=scratch, input_output_aliases=aliases,
        compiler_params=_params(dimension_semantics=("arbitrary",)),
    )(*args)


def _sum_chip_partials(own_ref, recv_ref):
    g = own_ref[...].astype(F32)
    for j in range(N_OTHER_CHIPS):
        g = g + recv_ref[j].astype(F32)
    return g


def _partial_specs(row_tile, cols):
    own = pl.BlockSpec((None, None, row_tile, cols), lambda l, r, chip_ref: (chip_ref[0], l, r, 0))
    recv = pl.BlockSpec((N_OTHER_CHIPS, None, row_tile, cols), lambda l, r, chip_ref: (0, l, r, 0))
    return own, recv


def _adamw_reduced(name, w, m, v, partial, received, chip, row_tile):
    depth, rows, cols = w.shape

    def body(chip_ref, w_ref, m_ref, v_ref, own_ref, recv_ref, g_ref, d_ref, mo_ref, vo_ref):
        g = _sum_chip_partials(own_ref, recv_ref)
        g_ref[...] = g
        d_ref[...], mo_ref[...], vo_ref[...] = _adamw(w_ref[...], g, m_ref[...], v_ref[...])

    blk = pl.BlockSpec((None, row_tile, cols), lambda l, r, chip_ref: (l, r, 0))
    return pl.pallas_call(
        body, name=name,
        grid_spec=pltpu.PrefetchScalarGridSpec(
            num_scalar_prefetch=1, grid=(depth, rows // row_tile),
            in_specs=[blk, blk, blk, *_partial_specs(row_tile, cols)], out_specs=[blk] * 4),
        out_shape=[jax.ShapeDtypeStruct(w.shape, F32)] * 4,
        compiler_params=_params(dimension_semantics=("arbitrary", "arbitrary")),
    )(chip, w, m, v, partial, received)


def _reduce_w_pool(partial, received, chip):
    def body(chip_ref, own_ref, recv_ref, g_ref):
        g_ref[...] = _sum_chip_partials(own_ref, recv_ref)

    return pl.pallas_call(
        body, name="reduce_w_pool",
        grid_spec=pltpu.PrefetchScalarGridSpec(
            num_scalar_prefetch=1, grid=(DEPTH, 1), in_specs=list(_partial_specs(POOL_SHARD, GROUP_D)),
            out_specs=pl.BlockSpec((POOL_SHARD, GROUP_D), lambda l, r, chip_ref: (l, 0))),
        out_shape=jax.ShapeDtypeStruct((DEPTH * POOL_SHARD, GROUP_D), F32),
        compiler_params=_params(dimension_semantics=("arbitrary", "arbitrary")),
    )(chip, partial, received)


def _adamw_small(params):
    n = len(params)

    def body(*refs):
        ins, outs = refs[:4 * n], refs[4 * n:]
        for p in range(n):
            w_ref, g_ref, m_ref, v_ref = ins[4 * p:4 * p + 4]
            d_ref, mo_ref, vo_ref = outs[3 * p:3 * p + 3]
            d_ref[...], mo_ref[...], vo_ref[...] = _adamw(w_ref[...], g_ref[...], m_ref[...], v_ref[...])

    vmem = pl.BlockSpec(memory_space=pltpu.VMEM)
    flat = [a for group in params for a in group]
    out_shape = [jax.ShapeDtypeStruct(group[0].shape, F32) for group in params for _ in range(3)]
    outs = pl.pallas_call(
        body, name="adamw_small", in_specs=[vmem] * len(flat), out_specs=[vmem] * len(out_shape),
        out_shape=out_shape, compiler_params=_params(),
    )(*flat)
    return [tuple(outs[3 * p:3 * p + 3]) for p in range(n)]


def _sum_sources(slabs):
    def body(s_ref, o_ref):
        acc = s_ref[0]
        for b in range(1, N_DEV):
            acc = acc + s_ref[b]
        o_ref[...] = acc

    vmem = pl.BlockSpec(memory_space=pltpu.VMEM)
    return pl.pallas_call(
        body, name="sum_small_grads", in_specs=[vmem], out_specs=vmem,
        out_shape=jax.ShapeDtypeStruct(slabs.shape[1:], F32), compiler_params=_params(),
    )(slabs)


def _to_bf16(a, name):
    def body(a_ref, o_ref):
        o_ref[...] = a_ref[...].astype(BF16)

    spec = pl.BlockSpec((None,) + a.shape[1:], lambda l: (l, 0, 0))
    return pl.pallas_call(
        body, name=name, grid=(a.shape[0],), in_specs=[spec], out_specs=spec,
        out_shape=jax.ShapeDtypeStruct(a.shape, BF16), compiler_params=_params(dimension_semantics=("arbitrary",)),
    )(a)


def kernel(x, c, w_ada, b_ada, g_pre, w_in, w_conv, w_pool, pool_scale, w_out, g_post, loss_target, m_w_ada, m_b_ada, m_g_pre, m_w_in, m_w_conv, m_w_pool, m_pool_scale, m_w_out, m_g_post, v_w_ada, v_b_ada, v_g_pre, v_w_in, v_w_conv, v_w_pool, v_pool_scale, v_w_out, v_g_post):
    mx, my, mc = _mesh_position()
    me = _block_id(mx, my, mc)
    chip = (2 * mx + my).astype(jnp.int32).reshape(1)
    core = mc.astype(jnp.int32).reshape(1)
    x0 = x[0]
    target = loss_target[0]
    conv_shard = w_conv.shape[-1]

    c_all = _all_gather_small(c, "all_gather_c")[:, 0, :]
    c_act, pieces = _modulation_columns(c_all, w_ada)
    mod_all = _all_gather_small(pieces, "all_gather_modulation")
    mod_mine = lax.dynamic_index_in_dim(mod_all, me, axis=1, keepdims=False)
    mod = mod_mine.reshape(N_DEV, DEPTH, W_IN_SHARD).transpose(1, 0, 2).reshape(DEPTH, 3 * D_MODEL) + b_ada
    zeros_d = jnp.zeros((DEPTH, 3, D_MODEL), F32)
    vec = jnp.concatenate([mod.reshape(DEPTH, 3, D_MODEL), g_pre[:, None], g_post[:, None], zeros_d], axis=1)

    w_conv_all = _all_gather_small(w_conv.reshape(DEPTH * 3, conv_shard), "all_gather_w_conv")
    w_conv_full = w_conv_all.transpose(1, 0, 2).reshape(DEPTH, 3, CONV_W)
    cps = jnp.concatenate([w_conv_full, pool_scale[:, None], jnp.zeros((DEPTH, 4, CONV_W), F32)], axis=1)
    wpool_b = _to_bf16(w_pool.reshape(DEPTH, POOL_ROWS, GROUP_D), "cast_w_pool").reshape(w_pool.shape)

    win_shards, wout_shards = _to_bf16(w_in, "cast_w_in"), _to_bf16(w_out, "cast_w_out")
    wins, wouts = [[a] for a in _run_exchange(
        _gather_weights_exchange(0, win_shards, wout_shards), "all_gather_weights_0")]

    xs, projs, ys = [x0], [], []
    for l in range(DEPTH):
        gather = _gather_weights_exchange(l + 1, win_shards, wout_shards) if l + 1 < DEPTH else None
        x_next, proj, y, *gathered = _forward_layer(l, xs[-1], vec, cps, wpool_b, wins[l], wouts[l], gather)
        xs.append(x_next)
        projs.append(proj)
        ys.append(y)
        if gather is not None:
            wins.append(gathered[0])
            wouts.append(gathered[1])
    dx, loss_tile = _loss_head(xs[DEPTH], target)

    slab_rows = [None] * DEPTH
    gwin_above = part_w = recv_w = part_o = recv_o = None
    for l in reversed(range(DEPTH)):
        hosted = None
        if gwin_above is not None:
            hosted = _merge_exchanges([_sibling_exchange([gwin_above], W_IN_KIND),
                                       _chips_exchange(l + 1, part_o, recv_o, W_OUT_KINDS)])
        dproj, gwout, gwpool, dcps, dvec1, *res = _backward_mixers(
            l, dx, ys[l], projs[l], vec, cps, wpool_b, wouts[l], hosted)
        chips_w = None
        if gwin_above is not None:
            recv_o = res[1:]
            part_w = _add_sibling_blocks(
                f"grad_add_w_in_{l + 1}", l + 1, [gwin_above], res[:1], core, part_w, W_IN_KIND)
            chips_w = _chips_exchange(l + 1, part_w, recv_w, W_IN_KIND)
        grads_o = [gwout, gwpool.reshape(POOL_ROWS, GROUP_D)]
        hosted = _merge_exchanges([chips_w, _sibling_exchange(grads_o, W_OUT_KINDS)])
        dx, gwin_above, dvec2, *res = _backward_input_proj(l, xs[l], dproj, dx, vec, wins[l], hosted)
        if chips_w is not None:
            recv_w, res = res[:1], res[1:]
        part_o = _add_sibling_blocks(f"grad_add_w_out_{l}", l, grads_o, res, core, part_o, W_OUT_KINDS)
        slab_rows[l] = jnp.concatenate(
            [dvec2[0], dvec2[1], dvec1[0], dvec2[2], dvec1[1], dcps[3], dcps[0], dcps[1], dcps[2]])
    grad_x = dx[None]

    loss_row = jnp.pad(loss_tile[0], (0, SLAB_COLS - LANES))
    slab = jnp.stack(slab_rows + [loss_row] + [jnp.zeros((SLAB_COLS,), F32)] * (SLAB_ROWS - DEPTH - 1))
    hosted = _merge_exchanges([_sibling_exchange([gwin_above], W_IN_KIND),
                               _chips_exchange(0, part_o, recv_o, W_OUT_KINDS)])
    slabs, *res = _all_gather_small(slab, "all_gather_small_grads", hosted)
    recv_o = res[1:]
    part_w = _add_sibling_blocks("grad_add_w_in_0", 0, [gwin_above], res[:1], core, part_w, W_IN_KIND)
    total = _sum_sources(slabs)
    loss = total[DEPTH, 0]
    o = 3 * D_MODEL
    g_b_ada = total[:DEPTH, :o]
    g_g_pre = total[:DEPTH, o:o + D_MODEL]
    g_g_post = total[:DEPTH, o + D_MODEL:o + 2 * D_MODEL]
    g_pool_scale = total[:DEPTH, o + 2 * D_MODEL:o + 2 * D_MODEL + POOL_W]
    g_conv_full = total[:DEPTH, o + 2 * D_MODEL + POOL_W:].reshape(DEPTH, 3, CONV_W)
    g_w_conv = lax.dynamic_slice_in_dim(g_conv_full, me * conv_shard, conv_shard, axis=2)

    chips_w = _chips_exchange(0, part_w, recv_w, W_IN_KIND)
    sems, in_flight, landing, token = _start_exchange(chips_w, "grad_chips_w_in_0_start")

    dmod_all = slabs[:, :DEPTH, :o].reshape(N_DEV, DEPTH, N_DEV, W_IN_SHARD)
    dmod_cols = lax.dynamic_index_in_dim(dmod_all, me, axis=2, keepdims=False).transpose(1, 0, 2) + token[0, 0]
    g_w_ada, d_w_ada, nm_w_ada, nv_w_ada = _adamw_w_ada(w_ada, m_w_ada, v_w_ada, c_act.T, dmod_cols, None)

    g_w_out, d_w_out, nm_w_out, nv_w_out = _adamw_reduced(
        "adamw_w_out", w_out, m_w_out, v_w_out, part_o[0], recv_o[0], chip, W_OUT_SHARD)
    g_pool_all = _all_gather_small(_reduce_w_pool(part_o[1], recv_o[1], chip), "all_gather_grad_w_pool")
    g_w_pool = g_pool_all.reshape(N_DEV, DEPTH, POOL_SHARD, GROUP_D).transpose(1, 0, 2, 3).reshape(w_pool.shape)
    part_w, recv_w = _finish_exchange(
        chips_w, "grad_chips_w_in_0_finish", sems, in_flight, landing, [nv_w_ada, nv_w_out, g_pool_all])
    g_w_in, d_w_in, nm_w_in, nv_w_in = _adamw_reduced(
        "adamw_w_in", w_in, m_w_in, v_w_in, part_w[0], recv_w[0], chip, ROW_TILE)

    flat2 = lambda a: a.reshape(-1, a.shape[-1])
    small = _adamw_small([
        (b_ada, g_b_ada, m_b_ada, v_b_ada),
        (g_pre, g_g_pre, m_g_pre, v_g_pre),
        (flat2(w_conv), flat2(g_w_conv), flat2(m_w_conv), flat2(v_w_conv)),
        (flat2(w_pool), flat2(g_w_pool), flat2(m_w_pool), flat2(v_w_pool)),
        (pool_scale, g_pool_scale, m_pool_scale, v_pool_scale),
        (g_post, g_g_post, m_g_post, v_g_post),
    ])
    (d_b_ada, nm_b_ada, nv_b_ada), (d_g_pre, nm_g_pre, nv_g_pre), conv_upd, pool_upd, \
        (d_ps, nm_ps, nv_ps), (d_g_post, nm_g_post, nv_g_post) = small
    d_w_conv, nm_w_conv, nv_w_conv = (a.reshape(w_conv.shape) for a in conv_upd)
    d_w_pool, nm_w_pool, nv_w_pool = (a.reshape(w_pool.shape) for a in pool_upd)

    return (loss, grad_x,
            g_w_ada, g_b_ada, g_g_pre, g_w_in, g_w_conv, g_w_pool, g_pool_scale, g_w_out, g_g_post,
            d_w_ada, d_b_ada, d_g_pre, d_w_in, d_w_conv, d_w_pool, d_ps, d_w_out, d_g_post,
            nm_w_ada, nm_b_ada, nm_g_pre, nm_w_in, nm_w_conv, nm_w_pool, nm_ps, nm_w_out, nm_g_post,
            nv_w_ada, nv_b_ada, nv_g_pre, nv_w_in, nv_w_conv, nv_w_pool, nv_ps, nv_w_out, nv_g_post)
```

```python
import jax
import jax.numpy as jnp
from jax import lax
from jax.experimental import pallas as pl
from jax.experimental.pallas import tpu as pltpu

F32 = jnp.float32
BF16 = jnp.bfloat16

D_MODEL = 1024
DEPTH = 4
CONV_W = 512
POOL_W = 512
POOL_WINDOWS = (2, 4, 8, 16)
GROUP_D = 128
IN_COLS = 4 * CONV_W + 2 * POOL_W
NORM_EPS = 1e-6

ADAM_LR = 0.001
ADAM_B1 = 0.9
ADAM_B2 = 0.999
ADAM_EPS = 1e-08
ADAM_WD = 0.01
ADAM_STEP = 10

N_DEV = 8
N_CHIP = 4
N_OTHER_CHIPS = N_CHIP - 1
MESH = pl.DeviceIdType.MESH
W_IN_SHARD = IN_COLS // N_DEV
W_OUT_SHARD = D_MODEL // N_DEV
POOL_ROWS = len(POOL_WINDOWS) * GROUP_D
POOL_SHARD = POOL_ROWS // N_DEV

SUBLANES = 8
LANES = 128
VMEM_LIMIT_BYTES = 56 * 1024 * 1024
ROW_TILE = 256
POOL_HALO = 16
CONV_HALO = SUBLANES

SLAB_COLS = 3 * D_MODEL + D_MODEL + D_MODEL + POOL_W + 3 * CONV_W
SLAB_ROWS = SUBLANES

HBM = pl.BlockSpec(memory_space=pl.ANY)


def _params(**kw):
    return pltpu.CompilerParams(vmem_limit_bytes=VMEM_LIMIT_BYTES, **kw)


def _sigmoid(v):
    return 1.0 / (1.0 + jnp.exp(-v))


def _dot(a, b):
    return jnp.dot(a, b, preferred_element_type=F32)


def _dot_tn(a, b):
    return lax.dot_general(a, b, (((0,), (0,)), ((), ())), preferred_element_type=F32)


def _dot_nt(a, b):
    return lax.dot_general(a, b, (((1,), (1,)), ((), ())), preferred_element_type=F32)


def _rows_from_before(v, k):
    return pltpu.roll(v, k, 0)


def _rows_from_after(v, k):
    return pltpu.roll(v, v.shape[0] - k, 0)


def _window_counts(t0, rows):
    return (lax.broadcasted_iota(jnp.int32, (rows, 1), 0) + (t0 + 1)).astype(F32)


def _split_proj(p32):
    cw = CONV_W
    return (p32[:, 0 * cw:1 * cw], p32[:, 1 * cw:2 * cw], p32[:, 2 * cw:3 * cw], p32[:, 3 * cw:4 * cw],
            p32[:, 4 * cw:4 * cw + POOL_W], p32[:, 4 * cw + POOL_W:])


def _mixer_forward(p32, z_halo, up_halo, cps, wpool_ref, t0):
    tm = p32.shape[0]
    u_a, b_a, c_a, g_a, u_p, g_p = _split_proj(p32)
    w0, w1, w2, ps = cps[0:1, :], cps[1:2, :], cps[2:3, :], cps[3:4, :]
    z = c_a * u_a
    zcat = jnp.concatenate([z_halo, z], axis=0)
    z1 = _rows_from_before(zcat, 1)[CONV_HALO:]
    z2 = _rows_from_before(zcat, 2)[CONV_HALO:]
    conv = w0 * z2 + w1 * z1 + w2 * z
    sig_a = _sigmoid(g_a)
    silu_a = g_a * sig_a
    y_a = b_a * conv * silu_a

    pcat = jnp.concatenate([up_halo, u_p], axis=0)
    counts = _window_counts(t0, tm)
    pooled, mixed, inv = [], [], []
    for g, w in enumerate(POOL_WINDOWS):
        cols = slice(g * GROUP_D, (g + 1) * GROUP_D)
        s = pcat[:, cols]
        step = 1
        while step < w:
            s = s + _rows_from_before(s, step)
            step *= 2
        inv_g = 1.0 / jnp.minimum(counts, float(w))
        pooled_g = s[POOL_HALO:] * inv_g - u_p[:, cols]
        pooled.append(pooled_g)
        inv.append(inv_g)
        mixed.append(_dot(pooled_g.astype(BF16), wpool_ref[g]))
    mixed = jnp.concatenate(mixed, axis=1)
    sig_p = _sigmoid(g_p)
    silu_p = g_p * sig_p
    y_p = mixed * ps * silu_p
    ycat = jnp.concatenate([y_a, y_p], axis=1).astype(BF16)
    return dict(u_a=u_a, b_a=b_a, c_a=c_a, g_a=g_a, u_p=u_p, g_p=g_p, z=z, z1=z1, z2=z2, conv=conv, sig_a=sig_a,
                silu_a=silu_a, pooled=pooled, inv=inv, mixed=mixed, sig_p=sig_p, silu_p=silu_p, ycat=ycat,
                w0=w0, w1=w1, w2=w2, ps=ps)


def _layer_spec(shape, layer):
    nd = len(shape)
    return pl.BlockSpec((None,) + tuple(shape[1:]), lambda i, _l=layer, _n=nd: (_l,) + (0,) * (_n - 1))


def _whole_spec(shape):
    return pl.BlockSpec(tuple(shape), lambda i, _n=len(shape): (0,) * _n)


def _mesh_position():
    return lax.axis_index("x"), lax.axis_index("y"), lax.axis_index("c")


def _block_id(x, y, c):
    return 4 * x + 2 * y + c


def _other_chips(x, y):
    return [(x ^ 1, y), (x, y ^ 1), (x ^ 1, y ^ 1)]


def _col_block(ref, blk):
    return ref.at[:, pl.ds(pl.multiple_of(blk * W_IN_SHARD, LANES), W_IN_SHARD)]


def _row_block(rows):
    def block(ref, blk):
        return ref.at[pl.ds(pl.multiple_of(blk * rows, rows), rows), :]
    return block


_BLOCK_OF = (_col_block, _row_block(W_OUT_SHARD), _row_block(POOL_SHARD))
_BLOCK_SHAPES = ((D_MODEL, W_IN_SHARD), (W_OUT_SHARD, D_MODEL), (POOL_SHARD, GROUP_D))


class _Exchange:
    def __init__(self, inputs, out_shapes, aliases, sem_shapes, make):
        self.inputs, self.out_shapes, self.aliases, self.sem_shapes, self.make = (
            list(inputs), list(out_shapes), dict(aliases), list(sem_shapes), make)


def _run_exchange(exchange, name):
    n_in, n_out = len(exchange.inputs), len(exchange.out_shapes)

    def body(*refs):
        start, finish = exchange.make(refs[:n_in], refs[n_in:n_in + n_out], refs[n_in + n_out:])
        start()
        finish()

    return pl.pallas_call(
        body, name=name, in_specs=[HBM] * n_in, out_specs=[HBM] * n_out, out_shape=exchange.out_shapes,
        scratch_shapes=exchange.sem_shapes, input_output_aliases=exchange.aliases, compiler_params=_params(),
    )(*exchange.inputs)


_SEM = pl.BlockSpec(memory_space=pltpu.SEMAPHORE)
_DATAFLOW = pltpu.SideEffectType.DATAFLOW_SIDE_EFFECTING


def _start_exchange(exchange, name):
    n_in, n_out, n_sem = len(exchange.inputs), len(exchange.out_shapes), len(exchange.sem_shapes)
    sources = [i for i in range(n_in) if i not in exchange.aliases]
    aliases = {i: n_sem + k for k, i in enumerate(sources)}
    aliases.update({i: n_sem + len(sources) + o for i, o in exchange.aliases.items()})

    def body(*refs):
        in_refs = refs[:n_in]
        sems = refs[n_in:n_in + n_sem]
        out_refs = refs[n_in + n_sem + len(sources):n_in + n_sem + len(sources) + n_out]
        exchange.make(in_refs, out_refs, sems)[0]()
        refs[-1][...] = jnp.zeros_like(refs[-1])

    outs = pl.pallas_call(
        body, name=name, in_specs=[HBM] * n_in,
        out_specs=[_SEM] * n_sem + [HBM] * (len(sources) + n_out) + [pl.BlockSpec(memory_space=pltpu.VMEM)],
        out_shape=(exchange.sem_shapes + [pltpu.HBM(exchange.inputs[i].shape, exchange.inputs[i].dtype) for i in sources]
                   + [pltpu.HBM(s.shape, s.dtype) for s in exchange.out_shapes]
                   + [jax.ShapeDtypeStruct((SUBLANES, LANES), F32)]),
        input_output_aliases=aliases, compiler_params=_params(has_side_effects=_DATAFLOW),
    )(*exchange.inputs)
    return outs[:n_sem], outs[n_sem:n_sem + len(sources)], outs[n_sem + len(sources):-1], outs[-1]


def _finish_exchange(exchange, name, sems, sources, landing, after):
    n_src, n_out, n_sem = len(sources), len(landing), len(sems)
    n_in = len(exchange.inputs)
    source_at = [i for i in range(n_in) if i not in exchange.aliases]

    def body(*refs):
        src_refs, out_refs = refs[:n_src], refs[n_src:n_src + n_out]
        sem_refs = refs[n_src + n_out:n_src + n_out + n_sem]
        in_refs = [None] * n_in
        for k, i in enumerate(source_at):
            in_refs[i] = src_refs[k]
        for i, o in exchange.aliases.items():
            in_refs[i] = out_refs[o]
        exchange.make(in_refs, out_refs, sem_refs)[1]()

    arrays = list(sources) + list(landing)
    outs = pl.pallas_call(
        body, name=name, in_specs=[HBM] * len(arrays) + [_SEM] * n_sem + [HBM] * len(after),
        out_specs=[HBM] * len(arrays), out_shape=[pltpu.HBM(a.shape, a.dtype) for a in arrays],
        input_output_aliases={i: i for i in range(len(arrays))}, compiler_params=_params(has_side_effects=_DATAFLOW),
    )(*arrays, *sems, *after)
    return outs[:n_src], outs[n_src:]


N_GATHERED = 2
FIRST_COPIES = 1 + N_OTHER_CHIPS
_GATHERED_SHAPES = ((D_MODEL, IN_COLS), (D_MODEL, D_MODEL))


def _first_sem(layer, a, k):
    return (layer * N_GATHERED + a) * FIRST_COPIES + k


def _gather_copy(window_of, full_ref, blk, send_sem, recv_sem, to, src=None):
    window = window_of(full_ref, blk)
    return pltpu.make_async_remote_copy(
        src_ref=window if src is None else src, dst_ref=window, send_sem=send_sem, recv_sem=recv_sem,
        device_id=to, device_id_type=MESH)


def _first_copies(layer, shard_refs, full_refs, send_sems, recv_sems, local_sems):
    x, y, c = _mesh_position()
    me = _block_id(x, y, c)
    own, remote = [], []
    for a in range(N_GATHERED):
        shard = shard_refs[a].at[layer]
        own.append(pltpu.make_async_copy(
            shard, _BLOCK_OF[a](full_refs[a], me), local_sems.at[layer * N_GATHERED + a]))
        targets = [(x, y, 1 - c)] + [(*chip, c) for chip in _other_chips(x, y)]
        remote += [_gather_copy(_BLOCK_OF[a], full_refs[a], me, send_sems.at[_first_sem(layer, a, k)],
                                recv_sems.at[_first_sem(layer, a, k)], to, src=shard)
                   for k, to in enumerate(targets)]
    return own, remote


def _gather_weights_start(win_shards, wout_shards):
    n_first = DEPTH * N_GATHERED * FIRST_COPIES
    sem_shapes = [pltpu.SemaphoreType.DMA((n_first,)), pltpu.SemaphoreType.DMA((n_first,)),
                  pltpu.SemaphoreType.DMA((DEPTH * N_GATHERED,))]
    shards = [win_shards, wout_shards]

    def body(win_sh, wout_sh, send_sems, recv_sems, local_sems, win_thru, wout_thru, *rest):
        landing, token = rest[:-1], rest[-1]
        for layer in range(DEPTH):
            own, remote = _first_copies(layer, (win_sh, wout_sh), landing[N_GATHERED * layer:N_GATHERED * (layer + 1)],
                                        send_sems, recv_sems, local_sems)
            for cp in own + remote:
                cp.start()
        token[...] = jnp.zeros_like(token)

    outs = pl.pallas_call(
        body, name="all_gather_weights_start", in_specs=[HBM] * 2,
        out_specs=[_SEM] * 3 + [HBM] * (2 + DEPTH * N_GATHERED) + [pl.BlockSpec(memory_space=pltpu.VMEM)],
        out_shape=(sem_shapes + [pltpu.HBM(s.shape, s.dtype) for s in shards]
                   + [pltpu.HBM(s, BF16) for _ in range(DEPTH) for s in _GATHERED_SHAPES]
                   + [jax.ShapeDtypeStruct((SUBLANES, LANES), F32)]),
        input_output_aliases={0: 3, 1: 4}, compiler_params=_params(has_side_effects=_DATAFLOW),
    )(*shards)
    landing = outs[5:-1]
    return outs[:3], outs[3:5], [landing[N_GATHERED * l:N_GATHERED * (l + 1)] for l in range(DEPTH)], outs[-1]


def _passed_on_copies(full_refs, send_sems, recv_sems, core_of_block):
    x, y, c = _mesh_position()
    return [_gather_copy(_BLOCK_OF[a], full_refs[a], _block_id(*chip, core_of_block),
                         send_sems.at[a * N_OTHER_CHIPS + j], recv_sems.at[a * N_OTHER_CHIPS + j], (x, y, 1 - c))
            for a in range(N_GATHERED) for j, chip in enumerate(_other_chips(x, y))]


def _gather_weights_pass_on(layer, first_recv_sems, landing, after):
    n = N_GATHERED * N_OTHER_CHIPS

    def body(win_ref, wout_ref, first_recv, *rest):
        send_sems, recv_sems = rest[len(after):len(after) + 2]
        x, y, c = _mesh_position()
        full_refs = (win_ref, wout_ref)
        passed = _passed_on_copies(full_refs, send_sems, recv_sems, c)
        for a in range(N_GATHERED):
            for j, chip in enumerate(_other_chips(x, y)):
                sem = _first_sem(layer, a, 1 + j)
                _gather_copy(_BLOCK_OF[a], full_refs[a], _block_id(*chip, c), first_recv.at[sem], first_recv.at[sem],
                             (x, y, c)).wait_recv()
                passed[a * N_OTHER_CHIPS + j].start()

    outs = pl.pallas_call(
        body, name=f"all_gather_weights_pass_on_{layer}", in_specs=[HBM] * N_GATHERED + [_SEM] + [HBM] * len(after),
        out_specs=[_SEM] * 2 + [HBM] * N_GATHERED,
        out_shape=[pltpu.SemaphoreType.DMA((n,)), pltpu.SemaphoreType.DMA((n,))]
        + [pltpu.HBM(a.shape, a.dtype) for a in landing],
        input_output_aliases={a: 2 + a for a in range(N_GATHERED)},
        compiler_params=_params(has_side_effects=_DATAFLOW),
    )(*landing, first_recv_sems, *after)
    return outs[:2], outs[2:]


def _gather_weights_finish(layer, first_sems, passed_sems, shards, landing):
    def body(win_ref, wout_ref, first_send, first_recv, local_sems, passed_send, passed_recv, win_sh, wout_sh, *thru):
        x, y, c = _mesh_position()
        full_refs = (win_ref, wout_ref)
        own, remote = _first_copies(layer, (win_sh, wout_sh), full_refs, first_send, first_recv, local_sems)
        for a in range(N_GATHERED):
            sem = _first_sem(layer, a, 0)
            _gather_copy(_BLOCK_OF[a], full_refs[a], _block_id(x, y, 1 - c), first_recv.at[sem], first_recv.at[sem],
                         (x, y, c)).wait_recv()
        for cp in _passed_on_copies(full_refs, passed_send, passed_recv, 1 - c):
            cp.wait_recv()
        for cp in remote + _passed_on_copies(full_refs, passed_send, passed_recv, c):
            cp.wait_send()
        for cp in own:
            cp.wait()

    return pl.pallas_call(
        body, name=f"all_gather_weights_finish_{layer}", in_specs=[HBM] * N_GATHERED + [_SEM] * 5 + [HBM] * 2,
        out_specs=[HBM] * N_GATHERED, out_shape=[pltpu.HBM(a.shape, a.dtype) for a in landing],
        input_output_aliases={a: a for a in range(N_GATHERED)}, compiler_params=_params(has_side_effects=_DATAFLOW),
    )(*landing, *first_sems, *passed_sems, *shards)


W_IN_KIND = (0,)
W_OUT_KINDS = (1, 2)


def _merge_exchanges(exchanges):
    exchanges = [e for e in exchanges if e is not None]
    if not exchanges:
        return None
    inputs, out_shapes, sem_shapes, aliases = [], [], [], {}
    for e in exchanges:
        aliases.update({len(inputs) + i: len(out_shapes) + o for i, o in e.aliases.items()})
        inputs += e.inputs
        out_shapes += e.out_shapes
        sem_shapes += e.sem_shapes

    def make(in_refs, out_refs, sems):
        made, at = [], [0, 0, 0]
        for e in exchanges:
            n = (len(e.inputs), len(e.out_shapes), len(e.sem_shapes))
            made.append(e.make(in_refs[at[0]:at[0] + n[0]], out_refs[at[1]:at[1] + n[1]], sems[at[2]:at[2] + n[2]]))
            at = [a + b for a, b in zip(at, n)]

        def start():
            for s, _ in made:
                s()

        def finish():
            for _, f in made:
                f()

        return start, finish

    return _Exchange(inputs, out_shapes, aliases, sem_shapes, make)


def _sibling_exchange(grads, kinds):
    n_arr = len(grads)

    def make(in_refs, out_refs, sems):
        send_sems, recv_sems = sems
        x, y, c = _mesh_position()
        copies = [pltpu.make_async_remote_copy(
            src_ref=_BLOCK_OF[kinds[a]](in_refs[a], 2 * q + (1 - c)), dst_ref=out_refs[a].at[q],
            send_sem=send_sems.at[a, q], recv_sem=recv_sems.at[a, q], device_id=(x, y, 1 - c), device_id_type=MESH)
            for a in range(n_arr) for q in range(N_CHIP)]

        def start():
            for cp in copies:
                cp.start()

        def finish():
            for cp in copies:
                cp.wait_recv()
            for cp in copies:
                cp.wait_send()

        return start, finish

    return _Exchange(
        grads, [jax.ShapeDtypeStruct((N_CHIP,) + _BLOCK_SHAPES[k], BF16) for k in kinds], {},
        [pltpu.SemaphoreType.DMA((n_arr, N_CHIP)), pltpu.SemaphoreType.DMA((n_arr, N_CHIP))], make)


def _chips_exchange(layer, partials, received, kinds):
    n_arr = len(partials)

    def make(in_refs, out_refs, sems):
        send_sems, recv_sems = sems
        x, y, c = _mesh_position()
        copies = [pltpu.make_async_remote_copy(
            src_ref=in_refs[a].at[2 * qx + qy, layer], dst_ref=out_refs[a].at[j, layer],
            send_sem=send_sems.at[a * N_OTHER_CHIPS + j], recv_sem=recv_sems.at[a * N_OTHER_CHIPS + j],
            device_id=(qx, qy, c), device_id_type=MESH)
            for a in range(n_arr) for j, (qx, qy) in enumerate(_other_chips(x, y))]

        def start():
            for cp in copies:
                cp.start()

        def finish():
            for cp in copies:
                cp.wait_recv()
            for cp in copies:
                cp.wait_send()

        return start, finish

    inputs = list(partials)
    aliases = {}
    if received is not None:
        inputs += list(received)
        aliases = {n_arr + a: a for a in range(n_arr)}
    return _Exchange(
        inputs, [jax.ShapeDtypeStruct((N_OTHER_CHIPS, DEPTH) + _BLOCK_SHAPES[k], BF16) for k in kinds], aliases,
        [pltpu.SemaphoreType.DMA((n_arr * N_OTHER_CHIPS,)), pltpu.SemaphoreType.DMA((n_arr * N_OTHER_CHIPS,))], make)


def _host(exchange, args, in_specs, out_shape, out_specs, scratch):
    n_own = (len(args), len(out_shape), len(scratch))
    if exchange is None:
        return {}, lambda refs: (refs, None)
    n_ex = (len(exchange.inputs), len(exchange.out_shapes), len(exchange.sem_shapes))
    aliases = {n_own[0] + i: n_own[1] + o for i, o in exchange.aliases.items()}
    args += exchange.inputs
    in_specs += [HBM] * n_ex[0]
    out_shape += exchange.out_shapes
    out_specs += [HBM] * n_ex[1]
    scratch += exchange.sem_shapes

    def split(refs):
        own, theirs, at = [], [], 0
        for mine, ex in zip(n_own, n_ex):
            own += refs[at:at + mine]
            theirs.append(refs[at + mine:at + mine + ex])
            at += mine + ex
        return own, exchange.make(*theirs)

    return aliases, split


def _all_gather_small(v, name, exchange=None):
    vmem = pl.BlockSpec(memory_space=pltpu.VMEM)
    args, in_specs = [v], [vmem]
    out_shape, out_specs = [jax.ShapeDtypeStruct((N_DEV,) + v.shape, v.dtype)], [vmem]
    scratch = [pltpu.SemaphoreType.DMA((N_DEV - 1,)), pltpu.SemaphoreType.DMA((N_DEV - 1,))]
    aliases, split = _host(exchange, args, in_specs, out_shape, out_specs, scratch)

    def body(*refs):
        (v_ref, out_ref, send_sems, recv_sems), hosted = split(refs)
        if hosted is not None:
            hosted[0]()
        x, y, c = _mesh_position()
        me = _block_id(x, y, c)
        out_ref[me] = v_ref[...]
        sends = []
        for k in range(1, N_DEV):
            px, py, pc = x ^ ((k >> 2) & 1), y ^ ((k >> 1) & 1), c ^ (k & 1)
            send = pltpu.make_async_remote_copy(
                src_ref=v_ref, dst_ref=out_ref.at[me], send_sem=send_sems.at[k - 1], recv_sem=recv_sems.at[k - 1],
                device_id=(px, py, pc), device_id_type=MESH)
            send.start()
            sends.append((send, _block_id(px, py, pc)))
        for k, (send, peer) in enumerate(sends):
            pltpu.make_async_remote_copy(
                src_ref=v_ref, dst_ref=out_ref.at[peer], send_sem=send_sems.at[k], recv_sem=recv_sems.at[k],
                device_id=(x, y, c), device_id_type=MESH).wait_recv()
        for send, _ in sends:
            send.wait_send()
        if hosted is not None:
            hosted[1]()

    outs = pl.pallas_call(
        body, name=name, in_specs=in_specs, out_specs=out_specs, out_shape=out_shape, scratch_shapes=scratch,
        input_output_aliases=aliases, compiler_params=_params(),
    )(*args)
    return outs[0] if exchange is None else outs


def _forward_layer(layer, x, vec, cps, wpool, win, wout, exchange):
    t_len = x.shape[0]
    n_tiles = t_len // ROW_TILE
    row = lambda cols: pl.BlockSpec((ROW_TILE, cols), lambda i: (i, 0))
    args = [x, vec, cps, wpool, win, wout]
    in_specs = [row(D_MODEL), _layer_spec(vec.shape, layer), _layer_spec(cps.shape, layer),
                _layer_spec(wpool.shape, layer), _whole_spec(win.shape), _whole_spec(wout.shape)]
    out_shape = [jax.ShapeDtypeStruct((t_len, D_MODEL), F32), jax.ShapeDtypeStruct((t_len, IN_COLS), BF16),
                 jax.ShapeDtypeStruct((t_len, D_MODEL), BF16)]
    out_specs = [row(D_MODEL), row(IN_COLS), row(D_MODEL)]
    scratch = [pltpu.VMEM((CONV_HALO, CONV_W), F32), pltpu.VMEM((POOL_HALO, POOL_W), F32)]
    aliases, split = _host(exchange, args, in_specs, out_shape, out_specs, scratch)

    def body(*refs):
        (x_ref, vec_ref, cps_ref, wpool_ref, win_ref, wout_ref, xo_ref, proj_ref, y_ref, zc_ref, pc_ref), hosted = (
            split(refs))
        i = pl.program_id(0)

        @pl.when(i == 0)
        def _():
            if hosted is not None:
                hosted[0]()
            zc_ref[...] = jnp.zeros_like(zc_ref)
            pc_ref[...] = jnp.zeros_like(pc_ref)

        x_t = x_ref[...]
        shift, scale, gate = vec_ref[0:1, :], vec_ref[1:2, :], vec_ref[2:3, :]
        g_pre, g_post = vec_ref[3:4, :], vec_ref[4:5, :]
        rx = lax.rsqrt(jnp.mean(x_t * x_t, axis=-1, keepdims=True) + NORM_EPS)
        h = (x_t * rx) * g_pre * (1.0 + scale) + shift
        proj_b = _dot(h.astype(BF16), win_ref[...]).astype(BF16)
        proj_ref[...] = proj_b
        mx = _mixer_forward(proj_b.astype(F32), zc_ref[...], pc_ref[...], cps_ref[...], wpool_ref, i * ROW_TILE)
        zc_ref[...] = mx["z"][ROW_TILE - CONV_HALO:]
        pc_ref[...] = mx["u_p"][ROW_TILE - POOL_HALO:]
        y_b = _dot(mx["ycat"], wout_ref[...]).astype(BF16)
        y_ref[...] = y_b
        y_t = y_b.astype(F32)
        ry = lax.rsqrt(jnp.mean(y_t * y_t, axis=-1, keepdims=True) + NORM_EPS)
        xo_ref[...] = x_t + gate * (y_t * ry * g_post)

        if hosted is not None:
            pl.when(i == n_tiles - 1)(hosted[1])

    return pl.pallas_call(
        body, name=f"forward_layer_{layer}", grid=(n_tiles,), in_specs=in_specs, out_specs=out_specs,
        out_shape=out_shape, scratch_shapes=scratch, input_output_aliases=aliases,
        compiler_params=_params(dimension_semantics=("arbitrary",)),
    )(*args)


def _loss_head(x_final, target):
    t_len = x_final.shape[0]
    n_tiles = t_len // ROW_TILE

    def body(x_ref, t_ref, dx_ref, loss_ref):
        @pl.when(pl.program_id(0) == 0)
        def _():
            loss_ref[...] = jnp.zeros_like(loss_ref)

        err = x_ref[...] - t_ref[...]
        dx_ref[...] = err * (1.0 / D_MODEL)
        loss_ref[...] += jnp.sum(err * err) * (0.5 / D_MODEL)

    row = pl.BlockSpec((ROW_TILE, D_MODEL), lambda i: (i, 0))
    return pl.pallas_call(
        body, name="loss_head", grid=(n_tiles,), in_specs=[row, row],
        out_specs=[row, pl.BlockSpec((SUBLANES, LANES), lambda i: (0, 0))],
        out_shape=[jax.ShapeDtypeStruct((t_len, D_MODEL), F32), jax.ShapeDtypeStruct((SUBLANES, LANES), F32)],
        compiler_params=_params(dimension_semantics=("arbitrary",)),
    )(x_final, target)


def _backward_mixers(layer, dxo, y, proj, vec, cps, wpool, wout, exchange):
    t_len = dxo.shape[0]
    n_tiles = t_len // ROW_TILE
    halo_per_tile = ROW_TILE // POOL_HALO
    rev = lambda cols: pl.BlockSpec((ROW_TILE, cols), lambda i: (n_tiles - 1 - i, 0))
    halo_spec = pl.BlockSpec(
        (POOL_HALO, IN_COLS), lambda i: (jnp.maximum((n_tiles - 1 - i) * halo_per_tile - 1, 0), 0))
    gwpool_shape = (len(POOL_WINDOWS), GROUP_D, GROUP_D)
    args = [dxo, y, proj, proj, vec, cps, wpool, wout]
    in_specs = [rev(D_MODEL), rev(D_MODEL), rev(IN_COLS), halo_spec, _layer_spec(vec.shape, layer),
                _layer_spec(cps.shape, layer), _layer_spec(wpool.shape, layer), _whole_spec(wout.shape)]
    out_shape = [jax.ShapeDtypeStruct((t_len, IN_COLS), BF16), jax.ShapeDtypeStruct((D_MODEL, D_MODEL), BF16),
                 jax.ShapeDtypeStruct(gwpool_shape, BF16), jax.ShapeDtypeStruct((SUBLANES, CONV_W), F32),
                 jax.ShapeDtypeStruct((SUBLANES, D_MODEL), F32)]
    out_specs = [rev(IN_COLS), _whole_spec((D_MODEL, D_MODEL)), _whole_spec(gwpool_shape),
                 _whole_spec((SUBLANES, CONV_W)), _whole_spec((SUBLANES, D_MODEL))]
    scratch = [pltpu.VMEM((D_MODEL, D_MODEL), F32), pltpu.VMEM(gwpool_shape, F32),
               pltpu.VMEM((CONV_HALO, CONV_W), F32), pltpu.VMEM((POOL_HALO, POOL_W), F32)]
    aliases, split = _host(exchange, args, in_specs, out_shape, out_specs, scratch)

    def body(*refs):
        (dxo_ref, y_ref, proj_ref, projh_ref, vec_ref, cps_ref, wpool_ref, wout_ref, dproj_ref, gwout_ref, gwpool_ref,
         dcps_ref, dvec_ref, gwout_acc, gwpool_acc, dcc_ref, qc_ref), hosted = split(refs)
        i = pl.program_id(0)
        tile = n_tiles - 1 - i

        @pl.when(i == 0)
        def _():
            if hosted is not None:
                hosted[0]()
            gwout_acc[...] = jnp.zeros_like(gwout_acc)
            gwpool_acc[...] = jnp.zeros_like(gwpool_acc)
            dcps_ref[...] = jnp.zeros_like(dcps_ref)
            dvec_ref[...] = jnp.zeros_like(dvec_ref)
            dcc_ref[...] = jnp.zeros_like(dcc_ref)
            qc_ref[...] = jnp.zeros_like(qc_ref)

        gate, g_post = vec_ref[2:3, :], vec_ref[4:5, :]
        dxo_t = dxo_ref[...]
        y_t = y_ref[...].astype(F32)
        ry = lax.rsqrt(jnp.mean(y_t * y_t, axis=-1, keepdims=True) + NORM_EPS)
        yh = y_t * ry
        dxy = dxo_t * yh
        dvec_ref[0:1, :] += jnp.sum(dxy * g_post, axis=0, keepdims=True)
        dvec_ref[1:2, :] += jnp.sum(dxy * gate, axis=0, keepdims=True)
        dyh = dxo_t * (gate * g_post)
        dy_b = (ry * (dyh - yh * jnp.mean(dyh * yh, axis=-1, keepdims=True))).astype(BF16)

        halo = jnp.where(tile > 0, projh_ref[...].astype(F32), 0.0)
        hu_a, _, hc_a, _, hu_p, _ = _split_proj(halo)
        z_halo = (hc_a * hu_a)[POOL_HALO - CONV_HALO:]
        mx = _mixer_forward(proj_ref[...].astype(F32), z_halo, hu_p, cps_ref[...], wpool_ref, tile * ROW_TILE)

        gwout_acc[...] += _dot_tn(mx["ycat"], dy_b)
        dycat = _dot_nt(dy_b, wout_ref[...])
        dy_a, dy_p = dycat[:, :CONV_W], dycat[:, CONV_W:]

        t_a = dy_a * mx["silu_a"]
        db_a = t_a * mx["conv"]
        dconv = t_a * mx["b_a"]
        dg_a = dy_a * mx["b_a"] * mx["conv"] * (mx["sig_a"] * (1.0 + mx["g_a"] * (1.0 - mx["sig_a"])))
        dccat = jnp.concatenate([dconv, dcc_ref[...]], axis=0)
        dc1 = _rows_from_after(dccat, 1)[:ROW_TILE]
        dc2 = _rows_from_after(dccat, 2)[:ROW_TILE]
        dz = mx["w2"] * dconv + mx["w1"] * dc1 + mx["w0"] * dc2
        dcc_ref[...] = dconv[:CONV_HALO]
        dcps_ref[0:1, :] += jnp.sum(dconv * mx["z2"], axis=0, keepdims=True)
        dcps_ref[1:2, :] += jnp.sum(dconv * mx["z1"], axis=0, keepdims=True)
        dcps_ref[2:3, :] += jnp.sum(dconv * mx["z"], axis=0, keepdims=True)
        du_a = dz * mx["c_a"]
        dc_a = dz * mx["u_a"]

        t_p = dy_p * mx["silu_p"]
        dcps_ref[3:4, :] += jnp.sum(t_p * mx["mixed"], axis=0, keepdims=True)
        dmixed = (t_p * mx["ps"]).astype(BF16)
        dg_p = dy_p * mx["mixed"] * mx["ps"] * (mx["sig_p"] * (1.0 + mx["g_p"] * (1.0 - mx["sig_p"])))
        du_p, q_head = [], []
        for g, w in enumerate(POOL_WINDOWS):
            cols = slice(g * GROUP_D, (g + 1) * GROUP_D)
            dm_g = dmixed[:, cols]
            dpooled_g = _dot_nt(dm_g, wpool_ref[g])
            gwpool_acc[g] += _dot_tn(mx["pooled"][g].astype(BF16), dm_g)
            q_g = dpooled_g * mx["inv"][g]
            q_head.append(q_g[:POOL_HALO])
            s = jnp.concatenate([q_g, qc_ref[:, cols]], axis=0)
            step = 1
            while step < w:
                s = s + _rows_from_after(s, step)
                step *= 2
            du_p.append(s[:ROW_TILE] - dpooled_g)
        qc_ref[...] = jnp.concatenate(q_head, axis=1)
        dproj_ref[...] = jnp.concatenate([du_a, db_a, dc_a, dg_a] + du_p + [dg_p], axis=1).astype(BF16)

        @pl.when(i == n_tiles - 1)
        def _():
            gwout_ref[...] = gwout_acc[...].astype(BF16)
            gwpool_ref[...] = gwpool_acc[...].astype(BF16)
            if hosted is not None:
                hosted[1]()

    return pl.pallas_call(
        body, name=f"backward_mixers_{layer}", grid=(n_tiles,), in_specs=in_specs, out_specs=out_specs,
        out_shape=out_shape, scratch_shapes=scratch, input_output_aliases=aliases,
        compiler_params=_params(dimension_semantics=("arbitrary",)),
    )(*args)


def _backward_input_proj(layer, x, dproj, dxo, vec, win, exchange):
    t_len = x.shape[0]
    n_tiles = t_len // ROW_TILE
    row = lambda cols: pl.BlockSpec((ROW_TILE, cols), lambda i: (i, 0))
    args = [x, dproj, dxo, vec, win]
    in_specs = [row(D_MODEL), row(IN_COLS), row(D_MODEL), _layer_spec(vec.shape, layer), _whole_spec(win.shape)]
    out_shape = [jax.ShapeDtypeStruct((t_len, D_MODEL), F32), jax.ShapeDtypeStruct((D_MODEL, IN_COLS), BF16),
                 jax.ShapeDtypeStruct((SUBLANES, D_MODEL), F32)]
    out_specs = [row(D_MODEL), _whole_spec((D_MODEL, IN_COLS)), _whole_spec((SUBLANES, D_MODEL))]
    scratch = [pltpu.VMEM((D_MODEL, IN_COLS), F32)]
    aliases, split = _host(exchange, args, in_specs, out_shape, out_specs, scratch)

    def body(*refs):
        (x_ref, dproj_ref, dxo_ref, vec_ref, win_ref, dx_ref, gwin_ref, dvec_ref, gwin_acc), hosted = split(refs)
        i = pl.program_id(0)

        @pl.when(i == 0)
        def _():
            if hosted is not None:
                hosted[0]()
            gwin_acc[...] = jnp.zeros_like(gwin_acc)
            dvec_ref[...] = jnp.zeros_like(dvec_ref)

        shift, scale, g_pre = vec_ref[0:1, :], vec_ref[1:2, :], vec_ref[3:4, :]
        x_t = x_ref[...]
        rx = lax.rsqrt(jnp.mean(x_t * x_t, axis=-1, keepdims=True) + NORM_EPS)
        xn = x_t * rx
        mod_scale = 1.0 + scale
        h_b = (xn * g_pre * mod_scale + shift).astype(BF16)
        dproj_t = dproj_ref[...]
        gwin_acc[...] += _dot_tn(h_b, dproj_t)
        dh = _dot_nt(dproj_t, win_ref[...])
        dvec_ref[0:1, :] += jnp.sum(dh, axis=0, keepdims=True)
        dhx = dh * xn
        dvec_ref[1:2, :] += jnp.sum(dhx * g_pre, axis=0, keepdims=True)
        dvec_ref[2:3, :] += jnp.sum(dhx * mod_scale, axis=0, keepdims=True)
        dxn = dh * (g_pre * mod_scale)
        dx_ref[...] = dxo_ref[...] + rx * (dxn - xn * jnp.mean(dxn * xn, axis=-1, keepdims=True))

        @pl.when(i == n_tiles - 1)
        def _():
            gwin_ref[...] = gwin_acc[...].astype(BF16)
            if hosted is not None:
                hosted[1]()

    return pl.pallas_call(
        body, name=f"backward_input_proj_{layer}", grid=(n_tiles,), in_specs=in_specs, out_specs=out_specs,
        out_shape=out_shape, scratch_shapes=scratch, input_output_aliases=aliases,
        compiler_params=_params(dimension_semantics=("arbitrary",)),
    )(*args)


def _add_sibling_blocks(name, layer, grads, received, core, partials, kinds):
    n_arr = len(grads)

    def body(core_ref, *refs):
        mine, theirs, outs = refs[:n_arr], refs[n_arr:2 * n_arr], refs[-n_arr:]
        for a in range(n_arr):
            outs[a][...] = (mine[a][...].astype(F32) + theirs[a][...].astype(F32)).astype(BF16)

    own_of_kind = [
        pl.BlockSpec((D_MODEL, W_IN_SHARD), lambda q, core_ref: (0, 2 * q + core_ref[0])),
        pl.BlockSpec((W_OUT_SHARD, D_MODEL), lambda q, core_ref: (2 * q + core_ref[0], 0)),
        pl.BlockSpec((POOL_SHARD, GROUP_D), lambda q, core_ref: (2 * q + core_ref[0], 0)),
    ]
    shapes = [_BLOCK_SHAPES[k] for k in kinds]
    recv_specs = [pl.BlockSpec((None,) + s, lambda q, core_ref: (q, 0, 0)) for s in shapes]
    out_specs = [pl.BlockSpec((None, None) + s, lambda q, core_ref: (q, layer, 0, 0)) for s in shapes]
    args = [core, *grads, *received]
    in_specs = [own_of_kind[k] for k in kinds] + recv_specs
    aliases = {}
    if partials is not None:
        aliases = {len(args) + a: a for a in range(n_arr)}
        args += list(partials)
        in_specs += [HBM] * n_arr
    return pl.pallas_call(
        body, name=name,
        grid_spec=pltpu.PrefetchScalarGridSpec(
            num_scalar_prefetch=1, grid=(N_CHIP,), in_specs=in_specs, out_specs=out_specs),
        out_shape=[jax.ShapeDtypeStruct((N_CHIP, DEPTH) + s, BF16) for s in shapes],
        input_output_aliases=aliases,
        compiler_params=_params(dimension_semantics=("arbitrary",)),
    )(*args)


def _modulation_columns(c_all, w_ada):
    def body(c_ref, w_ref, cact_ref, out_ref):
        c_t = c_ref[...]
        c_act = c_t * _sigmoid(c_t)
        cact_ref[...] = c_act
        out_ref[...] = jnp.dot(c_act, w_ref[...], preferred_element_type=F32, precision=lax.Precision.HIGHEST)

    return pl.pallas_call(
        body, name="modulation_columns", grid=(DEPTH,),
        in_specs=[pl.BlockSpec((N_DEV, D_MODEL), lambda l: (0, 0)),
                  pl.BlockSpec((None, D_MODEL, W_IN_SHARD), lambda l: (l, 0, 0))],
        out_specs=[pl.BlockSpec((N_DEV, D_MODEL), lambda l: (0, 0)),
                   pl.BlockSpec((N_DEV, W_IN_SHARD), lambda l: (0, l))],
        out_shape=[jax.ShapeDtypeStruct((N_DEV, D_MODEL), F32),
                   jax.ShapeDtypeStruct((N_DEV, DEPTH * W_IN_SHARD), F32)],
        compiler_params=_params(dimension_semantics=("arbitrary",)),
    )(c_all, w_ada)


def _adamw(w, g, m, v):
    m_new = ADAM_B1 * m + (1.0 - ADAM_B1) * g
    v_new = ADAM_B2 * v + (1.0 - ADAM_B2) * (g * g)
    m_hat = m_new / (1.0 - ADAM_B1 ** ADAM_STEP)
    v_hat = v_new / (1.0 - ADAM_B2 ** ADAM_STEP)
    delta = -ADAM_LR * (m_hat / (jnp.sqrt(v_hat) + ADAM_EPS) + ADAM_WD * w)
    return delta, m_new, v_new


def _adamw_w_ada(w, m, v, c_act_t, dmod_cols, exchange):
    big = pl.BlockSpec((None, D_MODEL, W_IN_SHARD), lambda l: (l, 0, 0))
    args = [w, m, v, c_act_t, dmod_cols]
    in_specs = [big, big, big, pl.BlockSpec((D_MODEL, N_DEV), lambda l: (0, 0)),
                pl.BlockSpec((None, N_DEV, W_IN_SHARD), lambda l: (l, 0, 0))]
    out_shape, out_specs, scratch = [jax.ShapeDtypeStruct(w.shape, F32)] * 4, [big] * 4, []
    aliases, split = _host(exchange, args, in_specs, out_shape, out_specs, scratch)

    def body(*refs):
        (w_ref, m_ref, v_ref, ct_ref, dm_ref, g_ref, d_ref, mo_ref, vo_ref), hosted = split(refs)
        if hosted is not None:
            pl.when(pl.program_id(0) == 0)(hosted[0])
        g = ct_ref[:, 0:1] * dm_ref[0:1, :]
        for b in range(1, N_DEV):
            g = g + ct_ref[:, b:b + 1] * dm_ref[b:b + 1, :]
        g_ref[...] = g
        d_ref[...], mo_ref[...], vo_ref[...] = _adamw(w_ref[...], g, m_ref[...], v_ref[...])
        if hosted is not None:
            pl.when(pl.program_id(0) == DEPTH - 1)(hosted[1])

    return pl.pallas_call(
        body, name="adamw_w_ada", grid=(DEPTH,), in_specs=in_specs, out_specs=out_specs, out_shape=out_shape,
        scratch_shapes=scratch, input_output_aliases=aliases,
        compiler_params=_params(dimension_semantics=("arbitrary",)),
    )(*args)


def _sum_chip_partials(own_ref, recv_ref):
    g = own_ref[...].astype(F32)
    for j in range(N_OTHER_CHIPS):
        g = g + recv_ref[j].astype(F32)
    return g


def _partial_specs(row_tile, cols):
    own = pl.BlockSpec((None, None, row_tile, cols), lambda l, r, chip_ref: (chip_ref[0], l, r, 0))
    recv = pl.BlockSpec((N_OTHER_CHIPS, None, row_tile, cols), lambda l, r, chip_ref: (0, l, r, 0))
    return own, recv


def _adamw_reduced(name, w, m, v, partial, received, chip, row_tile):
    depth, rows, cols = w.shape

    def body(chip_ref, w_ref, m_ref, v_ref, own_ref, recv_ref, g_ref, d_ref, mo_ref, vo_ref):
        g = _sum_chip_partials(own_ref, recv_ref)
        g_ref[...] = g
        d_ref[...], mo_ref[...], vo_ref[...] = _adamw(w_ref[...], g, m_ref[...], v_ref[...])

    blk = pl.BlockSpec((None, row_tile, cols), lambda l, r, chip_ref: (l, r, 0))
    return pl.pallas_call(
        body, name=name,
        grid_spec=pltpu.PrefetchScalarGridSpec(
            num_scalar_prefetch=1, grid=(depth, rows // row_tile),
            in_specs=[blk, blk, blk, *_partial_specs(row_tile, cols)], out_specs=[blk] * 4),
        out_shape=[jax.ShapeDtypeStruct(w.shape, F32)] * 4,
        compiler_params=_params(dimension_semantics=("arbitrary", "arbitrary")),
    )(chip, w, m, v, partial, received)


def _reduce_w_pool(partial, received, chip):
    def body(chip_ref, own_ref, recv_ref, g_ref):
        g_ref[...] = _sum_chip_partials(own_ref, recv_ref)

    return pl.pallas_call(
        body, name="reduce_w_pool",
        grid_spec=pltpu.PrefetchScalarGridSpec(
            num_scalar_prefetch=1, grid=(DEPTH, 1), in_specs=list(_partial_specs(POOL_SHARD, GROUP_D)),
            out_specs=pl.BlockSpec((POOL_SHARD, GROUP_D), lambda l, r, chip_ref: (l, 0))),
        out_shape=jax.ShapeDtypeStruct((DEPTH * POOL_SHARD, GROUP_D), F32),
        compiler_params=_params(dimension_semantics=("arbitrary", "arbitrary")),
    )(chip, partial, received)


def _adamw_small(params):
    n = len(params)

    def body(*refs):
        ins, outs = refs[:4 * n], refs[4 * n:]
        for p in range(n):
            w_ref, g_ref, m_ref, v_ref = ins[4 * p:4 * p + 4]
            d_ref, mo_ref, vo_ref = outs[3 * p:3 * p + 3]
            d_ref[...], mo_ref[...], vo_ref[...] = _adamw(w_ref[...], g_ref[...], m_ref[...], v_ref[...])

    vmem = pl.BlockSpec(memory_space=pltpu.VMEM)
    flat = [a for group in params for a in group]
    out_shape = [jax.ShapeDtypeStruct(group[0].shape, F32) for group in params for _ in range(3)]
    outs = pl.pallas_call(
        body, name="adamw_small", in_specs=[vmem] * len(flat), out_specs=[vmem] * len(out_shape),
        out_shape=out_shape, compiler_params=_params(),
    )(*flat)
    return [tuple(outs[3 * p:3 * p + 3]) for p in range(n)]


def _sum_sources(slabs):
    def body(s_ref, o_ref):
        acc = s_ref[0]
        for b in range(1, N_DEV):
            acc = acc + s_ref[b]
        o_ref[...] = acc

    vmem = pl.BlockSpec(memory_space=pltpu.VMEM)
    return pl.pallas_call(
        body, name="sum_small_grads", in_specs=[vmem], out_specs=vmem,
        out_shape=jax.ShapeDtypeStruct(slabs.shape[1:], F32), compiler_params=_params(),
    )(slabs)


def _to_bf16(a, name):
    def body(a_ref, o_ref):
        o_ref[...] = a_ref[...].astype(BF16)

    spec = pl.BlockSpec((None,) + a.shape[1:], lambda l: (l, 0, 0))
    return pl.pallas_call(
        body, name=name, grid=(a.shape[0],), in_specs=[spec], out_specs=spec,
        out_shape=jax.ShapeDtypeStruct(a.shape, BF16), compiler_params=_params(dimension_semantics=("arbitrary",)),
    )(a)


def kernel(x, c, w_ada, b_ada, g_pre, w_in, w_conv, w_pool, pool_scale, w_out, g_post, loss_target, m_w_ada, m_b_ada, m_g_pre, m_w_in, m_w_conv, m_w_pool, m_pool_scale, m_w_out, m_g_post, v_w_ada, v_b_ada, v_g_pre, v_w_in, v_w_conv, v_w_pool, v_pool_scale, v_w_out, v_g_post):
    mx, my, mc = _mesh_position()
    me = _block_id(mx, my, mc)
    chip = (2 * mx + my).astype(jnp.int32).reshape(1)
    core = mc.astype(jnp.int32).reshape(1)
    x0 = x[0]
    target = loss_target[0]
    conv_shard = w_conv.shape[-1]

    first_sems, shards, landing, token = _gather_weights_start(
        _to_bf16(w_in, "cast_w_in"), _to_bf16(w_out, "cast_w_out"))

    c_all = _all_gather_small(c + token[0, 0], "all_gather_c")[:, 0, :]
    c_act, pieces = _modulation_columns(c_all, w_ada)
    mod_all = _all_gather_small(pieces, "all_gather_modulation")
    mod_mine = lax.dynamic_index_in_dim(mod_all, me, axis=1, keepdims=False)
    mod = mod_mine.reshape(N_DEV, DEPTH, W_IN_SHARD).transpose(1, 0, 2).reshape(DEPTH, 3 * D_MODEL) + b_ada
    zeros_d = jnp.zeros((DEPTH, 3, D_MODEL), F32)
    vec = jnp.concatenate([mod.reshape(DEPTH, 3, D_MODEL), g_pre[:, None], g_post[:, None], zeros_d], axis=1)

    w_conv_all = _all_gather_small(w_conv.reshape(DEPTH * 3, conv_shard), "all_gather_w_conv")
    w_conv_full = w_conv_all.transpose(1, 0, 2).reshape(DEPTH, 3, CONV_W)
    cps = jnp.concatenate([w_conv_full, pool_scale[:, None], jnp.zeros((DEPTH, 4, CONV_W), F32)], axis=1)
    wpool_b = _to_bf16(w_pool.reshape(DEPTH, POOL_ROWS, GROUP_D), "cast_w_pool").reshape(w_pool.shape)

    xs, projs, ys, wins, wouts = [x0], [], [], [], []
    for l in range(DEPTH):
        passed_sems, zones = _gather_weights_pass_on(
            l, first_sems[1], landing[l], [vec, cps, wpool_b] if l == 0 else [xs[-1]])
        win, wout = _gather_weights_finish(l, first_sems, passed_sems, shards, zones)
        x_next, proj, y = _forward_layer(l, xs[-1], vec, cps, wpool_b, win, wout, None)
        xs.append(x_next)
        projs.append(proj)
        ys.append(y)
        wins.append(win)
        wouts.append(wout)
    dx, loss_tile = _loss_head(xs[DEPTH], target)

    slab_rows = [None] * DEPTH
    gwin_above = part_w = recv_w = part_o = recv_o = None
    for l in reversed(range(DEPTH)):
        hosted = None
        if gwin_above is not None:
            hosted = _merge_exchanges([_sibling_exchange([gwin_above], W_IN_KIND),
                                       _chips_exchange(l + 1, part_o, recv_o, W_OUT_KINDS)])
        dproj, gwout, gwpool, dcps, dvec1, *res = _backward_mixers(
            l, dx, ys[l], projs[l], vec, cps, wpool_b, wouts[l], hosted)
        chips_w = None
        if gwin_above is not None:
            recv_o = res[1:]
            part_w = _add_sibling_blocks(
                f"grad_add_w_in_{l + 1}", l + 1, [gwin_above], res[:1], core, part_w, W_IN_KIND)
            chips_w = _chips_exchange(l + 1, part_w, recv_w, W_IN_KIND)
        grads_o = [gwout, gwpool.reshape(POOL_ROWS, GROUP_D)]
        hosted = _merge_exchanges([chips_w, _sibling_exchange(grads_o, W_OUT_KINDS)])
        dx, gwin_above, dvec2, *res = _backward_input_proj(l, xs[l], dproj, dx, vec, wins[l], hosted)
        if chips_w is not None:
            recv_w, res = res[:1], res[1:]
        part_o = _add_sibling_blocks(f"grad_add_w_out_{l}", l, grads_o, res, core, part_o, W_OUT_KINDS)
        slab_rows[l] = jnp.concatenate(
            [dvec2[0], dvec2[1], dvec1[0], dvec2[2], dvec1[1], dcps[3], dcps[0], dcps[1], dcps[2]])
    grad_x = dx[None]

    loss_row = jnp.pad(loss_tile[0], (0, SLAB_COLS - LANES))
    slab = jnp.stack(slab_rows + [loss_row] + [jnp.zeros((SLAB_COLS,), F32)] * (SLAB_ROWS - DEPTH - 1))
    hosted = _merge_exchanges([_sibling_exchange([gwin_above], W_IN_KIND),
                               _chips_exchange(0, part_o, recv_o, W_OUT_KINDS)])
    slabs, *res = _all_gather_small(slab, "all_gather_small_grads", hosted)
    recv_o = res[1:]
    part_w = _add_sibling_blocks("grad_add_w_in_0", 0, [gwin_above], res[:1], core, part_w, W_IN_KIND)
    total = _sum_sources(slabs)
    loss = total[DEPTH, 0]
    o = 3 * D_MODEL
    g_b_ada = total[:DEPTH, :o]
    g_g_pre = total[:DEPTH, o:o + D_MODEL]
    g_g_post = total[:DEPTH, o + D_MODEL:o + 2 * D_MODEL]
    g_pool_scale = total[:DEPTH, o + 2 * D_MODEL:o + 2 * D_MODEL + POOL_W]
    g_conv_full = total[:DEPTH, o + 2 * D_MODEL + POOL_W:].reshape(DEPTH, 3, CONV_W)
    g_w_conv = lax.dynamic_slice_in_dim(g_conv_full, me * conv_shard, conv_shard, axis=2)

    chips_w = _chips_exchange(0, part_w, recv_w, W_IN_KIND)
    sems, in_flight, landing, token = _start_exchange(chips_w, "grad_chips_w_in_0_start")

    dmod_all = slabs[:, :DEPTH, :o].reshape(N_DEV, DEPTH, N_DEV, W_IN_SHARD)
    dmod_cols = lax.dynamic_index_in_dim(dmod_all, me, axis=2, keepdims=False).transpose(1, 0, 2) + token[0, 0]
    g_w_ada, d_w_ada, nm_w_ada, nv_w_ada = _adamw_w_ada(w_ada, m_w_ada, v_w_ada, c_act.T, dmod_cols, None)

    g_w_out, d_w_out, nm_w_out, nv_w_out = _adamw_reduced(
        "adamw_w_out", w_out, m_w_out, v_w_out, part_o[0], recv_o[0], chip, W_OUT_SHARD)
    g_pool_all = _all_gather_small(_reduce_w_pool(part_o[1], recv_o[1], chip), "all_gather_grad_w_pool")
    g_w_pool = g_pool_all.reshape(N_DEV, DEPTH, POOL_SHARD, GROUP_D).transpose(1, 0, 2, 3).reshape(w_pool.shape)
    part_w, recv_w = _finish_exchange(
        chips_w, "grad_chips_w_in_0_finish", sems, in_flight, landing, [nv_w_ada, nv_w_out, g_pool_all])
    g_w_in, d_w_in, nm_w_in, nv_w_in = _adamw_reduced(
        "adamw_w_in", w_in, m_w_in, v_w_in, part_w[0], recv_w[0], chip, ROW_TILE)

    flat2 = lambda a: a.reshape(-1, a.shape[-1])
    small = _adamw_small([
        (b_ada, g_b_ada, m_b_ada, v_b_ada),
        (g_pre, g_g_pre, m_g_pre, v_g_pre),
        (flat2(w_conv), flat2(g_w_conv), flat2(m_w_conv), flat2(v_w_conv)),
        (flat2(w_pool), flat2(g_w_pool), flat2(m_w_pool), flat2(v_w_pool)),
        (pool_scale, g_pool_scale, m_pool_scale, v_pool_scale),
        (g_post, g_g_post, m_g_post, v_g_post),
    ])
    (d_b_ada, nm_b_ada, nv_b_ada), (d_g_pre, nm_g_pre, nv_g_pre), conv_upd, pool_upd, \
        (d_ps, nm_ps, nv_ps), (d_g_post, nm_g_post, nv_g_post) = small
    d_w_conv, nm_w_conv, nv_w_conv = (a.reshape(w_conv.shape) for a in conv_upd)
    d_w_pool, nm_w_pool, nv_w_pool = (a.reshape(w_pool.shape) for a in pool_upd)

    return (loss, grad_x,
            g_w_ada, g_b_ada, g_g_pre, g_w_in, g_w_conv, g_w_pool, g_pool_scale, g_w_out, g_g_post,
            d_w_ada, d_b_ada, d_g_pre, d_w_in, d_w_conv, d_w_pool, d_ps, d_w_out, d_g_post,
            nm_w_ada, nm_b_ada, nm_g_pre, nm_w_in, nm_w_conv, nm_w_pool, nm_ps, nm_w_out, nm_g_post,
            nv_w_ada, nv_b_ada, nv_g_pre, nv_w_in, nv_w_conv, nv_w_pool, nv_ps, nv_w_out, nv_g_post)
```

```python
import jax
import jax.numpy as jnp
from jax import lax
from jax.experimental import pallas as pl
from jax.experimental.pallas import tpu as pltpu

F32 = jnp.float32
BF16 = jnp.bfloat16

D_MODEL = 1024
DEPTH = 4
CONV_W = 512
POOL_W = 512
POOL_WINDOWS = (2, 4, 8, 16)
GROUP_D = 128
IN_COLS = 4 * CONV_W + 2 * POOL_W
NORM_EPS = 1e-6

ADAM_LR = 0.001
ADAM_B1 = 0.9
ADAM_B2 = 0.999
ADAM_EPS = 1e-08
ADAM_WD = 0.01
ADAM_STEP = 10

N_DEV = 8
N_CHIP = 4
N_OTHER_CHIPS = N_CHIP - 1
MESH = pl.DeviceIdType.MESH
W_IN_SHARD = IN_COLS // N_DEV
W_OUT_SHARD = D_MODEL // N_DEV
POOL_ROWS = len(POOL_WINDOWS) * GROUP_D
POOL_SHARD = POOL_ROWS // N_DEV

SUBLANES = 8
LANES = 128
VMEM_LIMIT_BYTES = 56 * 1024 * 1024
ROW_TILE = 256
POOL_HALO = 16
CONV_HALO = SUBLANES

SLAB_COLS = 3 * D_MODEL + D_MODEL + D_MODEL + POOL_W + 3 * CONV_W
SLAB_ROWS = SUBLANES

HBM = pl.BlockSpec(memory_space=pl.ANY)


def _params(**kw):
    return pltpu.CompilerParams(vmem_limit_bytes=VMEM_LIMIT_BYTES, **kw)


def _sigmoid(v):
    return 1.0 / (1.0 + jnp.exp(-v))


def _dot(a, b):
    return jnp.dot(a, b, preferred_element_type=F32)


def _dot_tn(a, b):
    return lax.dot_general(a, b, (((0,), (0,)), ((), ())), preferred_element_type=F32)


def _dot_nt(a, b):
    return lax.dot_general(a, b, (((1,), (1,)), ((), ())), preferred_element_type=F32)


def _rows_from_before(v, k):
    return pltpu.roll(v, k, 0)


def _rows_from_after(v, k):
    return pltpu.roll(v, v.shape[0] - k, 0)


def _window_counts(t0, rows):
    return (lax.broadcasted_iota(jnp.int32, (rows, 1), 0) + (t0 + 1)).astype(F32)


def _split_proj(p32):
    cw = CONV_W
    return (p32[:, 0 * cw:1 * cw], p32[:, 1 * cw:2 * cw], p32[:, 2 * cw:3 * cw], p32[:, 3 * cw:4 * cw],
            p32[:, 4 * cw:4 * cw + POOL_W], p32[:, 4 * cw + POOL_W:])


def _mixer_forward(p32, z_halo, up_halo, cps, wpool_ref, t0):
    tm = p32.shape[0]
    u_a, b_a, c_a, g_a, u_p, g_p = _split_proj(p32)
    w0, w1, w2, ps = cps[0:1, :], cps[1:2, :], cps[2:3, :], cps[3:4, :]
    z = c_a * u_a
    zcat = jnp.concatenate([z_halo, z], axis=0)
    z1 = _rows_from_before(zcat, 1)[CONV_HALO:]
    z2 = _rows_from_before(zcat, 2)[CONV_HALO:]
    conv = w0 * z2 + w1 * z1 + w2 * z
    sig_a = _sigmoid(g_a)
    silu_a = g_a * sig_a
    y_a = b_a * conv * silu_a

    pcat = jnp.concatenate([up_halo, u_p], axis=0)
    counts = _window_counts(t0, tm)
    pooled, mixed, inv = [], [], []
    for g, w in enumerate(POOL_WINDOWS):
        cols = slice(g * GROUP_D, (g + 1) * GROUP_D)
        s = pcat[:, cols]
        step = 1
        while step < w:
            s = s + _rows_from_before(s, step)
            step *= 2
        inv_g = 1.0 / jnp.minimum(counts, float(w))
        pooled_g = s[POOL_HALO:] * inv_g - u_p[:, cols]
        pooled.append(pooled_g)
        inv.append(inv_g)
        mixed.append(_dot(pooled_g.astype(BF16), wpool_ref[g]))
    mixed = jnp.concatenate(mixed, axis=1)
    sig_p = _sigmoid(g_p)
    silu_p = g_p * sig_p
    y_p = mixed * ps * silu_p
    ycat = jnp.concatenate([y_a, y_p], axis=1).astype(BF16)
    return dict(u_a=u_a, b_a=b_a, c_a=c_a, g_a=g_a, u_p=u_p, g_p=g_p, z=z, z1=z1, z2=z2, conv=conv, sig_a=sig_a,
                silu_a=silu_a, pooled=pooled, inv=inv, mixed=mixed, sig_p=sig_p, silu_p=silu_p, ycat=ycat,
                w0=w0, w1=w1, w2=w2, ps=ps)


def _layer_spec(shape, layer):
    nd = len(shape)
    return pl.BlockSpec((None,) + tuple(shape[1:]), lambda i, _l=layer, _n=nd: (_l,) + (0,) * (_n - 1))


def _whole_spec(shape):
    return pl.BlockSpec(tuple(shape), lambda i, _n=len(shape): (0,) * _n)


def _mesh_position():
    return lax.axis_index("x"), lax.axis_index("y"), lax.axis_index("c")


def _block_id(x, y, c):
    return 4 * x + 2 * y + c


def _other_chips(x, y):
    return [(x ^ 1, y), (x, y ^ 1), (x ^ 1, y ^ 1)]


def _col_block(ref, blk):
    return ref.at[:, pl.ds(pl.multiple_of(blk * W_IN_SHARD, LANES), W_IN_SHARD)]


def _row_block(rows):
    def block(ref, blk):
        return ref.at[pl.ds(pl.multiple_of(blk * rows, rows), rows), :]
    return block


_BLOCK_OF = (_col_block, _row_block(W_OUT_SHARD), _row_block(POOL_SHARD))
_BLOCK_SHAPES = ((D_MODEL, W_IN_SHARD), (W_OUT_SHARD, D_MODEL), (POOL_SHARD, GROUP_D))


class _Exchange:
    def __init__(self, inputs, out_shapes, aliases, sem_shapes, make):
        self.inputs, self.out_shapes, self.aliases, self.sem_shapes, self.make = (
            list(inputs), list(out_shapes), dict(aliases), list(sem_shapes), make)


def _run_exchange(exchange, name):
    n_in, n_out = len(exchange.inputs), len(exchange.out_shapes)

    def body(*refs):
        start, finish = exchange.make(refs[:n_in], refs[n_in:n_in + n_out], refs[n_in + n_out:])
        start()
        finish()

    return pl.pallas_call(
        body, name=name, in_specs=[HBM] * n_in, out_specs=[HBM] * n_out, out_shape=exchange.out_shapes,
        scratch_shapes=exchange.sem_shapes, input_output_aliases=exchange.aliases, compiler_params=_params(),
    )(*exchange.inputs)


_SEM = pl.BlockSpec(memory_space=pltpu.SEMAPHORE)
_DATAFLOW = pltpu.SideEffectType.DATAFLOW_SIDE_EFFECTING


def _start_exchange(exchange, name):
    n_in, n_out, n_sem = len(exchange.inputs), len(exchange.out_shapes), len(exchange.sem_shapes)
    sources = [i for i in range(n_in) if i not in exchange.aliases]
    aliases = {i: n_sem + k for k, i in enumerate(sources)}
    aliases.update({i: n_sem + len(sources) + o for i, o in exchange.aliases.items()})

    def body(*refs):
        in_refs = refs[:n_in]
        sems = refs[n_in:n_in + n_sem]
        out_refs = refs[n_in + n_sem + len(sources):n_in + n_sem + len(sources) + n_out]
        exchange.make(in_refs, out_refs, sems)[0]()
        refs[-1][...] = jnp.zeros_like(refs[-1])

    outs = pl.pallas_call(
        body, name=name, in_specs=[HBM] * n_in,
        out_specs=[_SEM] * n_sem + [HBM] * (len(sources) + n_out) + [pl.BlockSpec(memory_space=pltpu.VMEM)],
        out_shape=(exchange.sem_shapes + [pltpu.HBM(exchange.inputs[i].shape, exchange.inputs[i].dtype) for i in sources]
                   + [pltpu.HBM(s.shape, s.dtype) for s in exchange.out_shapes]
                   + [jax.ShapeDtypeStruct((SUBLANES, LANES), F32)]),
        input_output_aliases=aliases, compiler_params=_params(has_side_effects=_DATAFLOW),
    )(*exchange.inputs)
    return outs[:n_sem], outs[n_sem:n_sem + len(sources)], outs[n_sem + len(sources):-1], outs[-1]


def _finish_exchange(exchange, name, sems, sources, landing, after):
    n_src, n_out, n_sem = len(sources), len(landing), len(sems)
    n_in = len(exchange.inputs)
    source_at = [i for i in range(n_in) if i not in exchange.aliases]

    def body(*refs):
        src_refs, out_refs = refs[:n_src], refs[n_src:n_src + n_out]
        sem_refs = refs[n_src + n_out:n_src + n_out + n_sem]
        in_refs = [None] * n_in
        for k, i in enumerate(source_at):
            in_refs[i] = src_refs[k]
        for i, o in exchange.aliases.items():
            in_refs[i] = out_refs[o]
        exchange.make(in_refs, out_refs, sem_refs)[1]()

    arrays = list(sources) + list(landing)
    outs = pl.pallas_call(
        body, name=name, in_specs=[HBM] * len(arrays) + [_SEM] * n_sem + [HBM] * len(after),
        out_specs=[HBM] * len(arrays), out_shape=[pltpu.HBM(a.shape, a.dtype) for a in arrays],
        input_output_aliases={i: i for i in range(len(arrays))}, compiler_params=_params(has_side_effects=_DATAFLOW),
    )(*arrays, *sems, *after)
    return outs[:n_src], outs[n_src:]


N_GATHERED = 2
FIRST_COPIES = 1 + N_OTHER_CHIPS
_GATHERED_SHAPES = ((D_MODEL, IN_COLS), (D_MODEL, D_MODEL))


def _first_sem(layer, a, k):
    return (layer * N_GATHERED + a) * FIRST_COPIES + k


def _gather_copy(window_of, full_ref, blk, send_sem, recv_sem, to, src=None):
    window = window_of(full_ref, blk)
    return pltpu.make_async_remote_copy(
        src_ref=window if src is None else src, dst_ref=window, send_sem=send_sem, recv_sem=recv_sem,
        device_id=to, device_id_type=MESH)


def _first_copies(layer, shard_refs, full_refs, send_sems, recv_sems, local_sems):
    x, y, c = _mesh_position()
    me = _block_id(x, y, c)
    own, remote = [], []
    for a in range(N_GATHERED):
        shard = shard_refs[a].at[layer]
        own.append(pltpu.make_async_copy(
            shard, _BLOCK_OF[a](full_refs[a], me), local_sems.at[layer * N_GATHERED + a]))
        targets = [(x, y, 1 - c)] + [(*chip, c) for chip in _other_chips(x, y)]
        remote += [_gather_copy(_BLOCK_OF[a], full_refs[a], me, send_sems.at[_first_sem(layer, a, k)],
                                recv_sems.at[_first_sem(layer, a, k)], to, src=shard)
                   for k, to in enumerate(targets)]
    return own, remote


def _gather_weights_start(win_shards, wout_shards, after):
    n_first = DEPTH * N_GATHERED * FIRST_COPIES
    sem_shapes = [pltpu.SemaphoreType.DMA((n_first,)), pltpu.SemaphoreType.DMA((n_first,)),
                  pltpu.SemaphoreType.DMA((DEPTH * N_GATHERED,))]
    shards = [win_shards, wout_shards]

    def body(win_sh, wout_sh, *rest):
        send_sems, recv_sems, local_sems, win_thru, wout_thru, *landing = rest[len(after):]
        for layer in range(DEPTH):
            own, remote = _first_copies(layer, (win_sh, wout_sh), landing[N_GATHERED * layer:N_GATHERED * (layer + 1)],
                                        send_sems, recv_sems, local_sems)
            for cp in own + remote:
                cp.start()

    outs = pl.pallas_call(
        body, name="all_gather_weights_start", in_specs=[HBM] * (2 + len(after)),
        out_specs=[_SEM] * 3 + [HBM] * (2 + DEPTH * N_GATHERED),
        out_shape=(sem_shapes + [pltpu.HBM(s.shape, s.dtype) for s in shards]
                   + [pltpu.HBM(s, BF16) for _ in range(DEPTH) for s in _GATHERED_SHAPES]),
        input_output_aliases={0: 3, 1: 4}, compiler_params=_params(has_side_effects=_DATAFLOW),
    )(*shards, *after)
    landing = outs[5:]
    return outs[:3], outs[3:5], [landing[N_GATHERED * l:N_GATHERED * (l + 1)] for l in range(DEPTH)]


def _passed_on_copies(full_refs, send_sems, recv_sems, core_of_block):
    x, y, c = _mesh_position()
    return [_gather_copy(_BLOCK_OF[a], full_refs[a], _block_id(*chip, core_of_block),
                         send_sems.at[a * N_OTHER_CHIPS + j], recv_sems.at[a * N_OTHER_CHIPS + j], (x, y, 1 - c))
            for a in range(N_GATHERED) for j, chip in enumerate(_other_chips(x, y))]


def _gather_weights_pass_on(layer, first_recv_sems, landing, after):
    n = N_GATHERED * N_OTHER_CHIPS

    def body(win_ref, wout_ref, first_recv, *rest):
        send_sems, recv_sems = rest[len(after):len(after) + 2]
        x, y, c = _mesh_position()
        full_refs = (win_ref, wout_ref)
        passed = _passed_on_copies(full_refs, send_sems, recv_sems, c)
        for a in range(N_GATHERED):
            for j, chip in enumerate(_other_chips(x, y)):
                sem = _first_sem(layer, a, 1 + j)
                _gather_copy(_BLOCK_OF[a], full_refs[a], _block_id(*chip, c), first_recv.at[sem], first_recv.at[sem],
                             (x, y, c)).wait_recv()
                passed[a * N_OTHER_CHIPS + j].start()

    outs = pl.pallas_call(
        body, name=f"all_gather_weights_pass_on_{layer}", in_specs=[HBM] * N_GATHERED + [_SEM] + [HBM] * len(after),
        out_specs=[_SEM] * 2 + [HBM] * N_GATHERED,
        out_shape=[pltpu.SemaphoreType.DMA((n,)), pltpu.SemaphoreType.DMA((n,))]
        + [pltpu.HBM(a.shape, a.dtype) for a in landing],
        input_output_aliases={a: 2 + a for a in range(N_GATHERED)},
        compiler_params=_params(has_side_effects=_DATAFLOW),
    )(*landing, first_recv_sems, *after)
    return outs[:2], outs[2:]


def _gather_weights_finish(layer, first_sems, passed_sems, shards, landing):
    def body(win_ref, wout_ref, first_send, first_recv, local_sems, passed_send, passed_recv, win_sh, wout_sh, *thru):
        x, y, c = _mesh_position()
        full_refs = (win_ref, wout_ref)
        own, remote = _first_copies(layer, (win_sh, wout_sh), full_refs, first_send, first_recv, local_sems)
        for a in range(N_GATHERED):
            sem = _first_sem(layer, a, 0)
            _gather_copy(_BLOCK_OF[a], full_refs[a], _block_id(x, y, 1 - c), first_recv.at[sem], first_recv.at[sem],
                         (x, y, c)).wait_recv()
        for cp in _passed_on_copies(full_refs, passed_send, passed_recv, 1 - c):
            cp.wait_recv()
        for cp in remote + _passed_on_copies(full_refs, passed_send, passed_recv, c):
            cp.wait_send()
        for cp in own:
            cp.wait()

    return pl.pallas_call(
        body, name=f"all_gather_weights_finish_{layer}", in_specs=[HBM] * N_GATHERED + [_SEM] * 5 + [HBM] * 2,
        out_specs=[HBM] * N_GATHERED, out_shape=[pltpu.HBM(a.shape, a.dtype) for a in landing],
        input_output_aliases={a: a for a in range(N_GATHERED)}, compiler_params=_params(has_side_effects=_DATAFLOW),
    )(*landing, *first_sems, *passed_sems, *shards)


W_IN_KIND = (0,)
W_OUT_KINDS = (1, 2)


def _merge_exchanges(exchanges):
    exchanges = [e for e in exchanges if e is not None]
    if not exchanges:
        return None
    inputs, out_shapes, sem_shapes, aliases = [], [], [], {}
    for e in exchanges:
        aliases.update({len(inputs) + i: len(out_shapes) + o for i, o in e.aliases.items()})
        inputs += e.inputs
        out_shapes += e.out_shapes
        sem_shapes += e.sem_shapes

    def make(in_refs, out_refs, sems):
        made, at = [], [0, 0, 0]
        for e in exchanges:
            n = (len(e.inputs), len(e.out_shapes), len(e.sem_shapes))
            made.append(e.make(in_refs[at[0]:at[0] + n[0]], out_refs[at[1]:at[1] + n[1]], sems[at[2]:at[2] + n[2]]))
            at = [a + b for a, b in zip(at, n)]

        def start():
            for s, _ in made:
                s()

        def finish():
            for _, f in made:
                f()

        return start, finish

    return _Exchange(inputs, out_shapes, aliases, sem_shapes, make)


def _sibling_exchange(grads, kinds):
    n_arr = len(grads)

    def make(in_refs, out_refs, sems):
        send_sems, recv_sems = sems
        x, y, c = _mesh_position()
        copies = [pltpu.make_async_remote_copy(
            src_ref=_BLOCK_OF[kinds[a]](in_refs[a], 2 * q + (1 - c)), dst_ref=out_refs[a].at[q],
            send_sem=send_sems.at[a, q], recv_sem=recv_sems.at[a, q], device_id=(x, y, 1 - c), device_id_type=MESH)
            for a in range(n_arr) for q in range(N_CHIP)]

        def start():
            for cp in copies:
                cp.start()

        def finish():
            for cp in copies:
                cp.wait_recv()
            for cp in copies:
                cp.wait_send()

        return start, finish

    return _Exchange(
        grads, [jax.ShapeDtypeStruct((N_CHIP,) + _BLOCK_SHAPES[k], BF16) for k in kinds], {},
        [pltpu.SemaphoreType.DMA((n_arr, N_CHIP)), pltpu.SemaphoreType.DMA((n_arr, N_CHIP))], make)


def _chips_exchange(layer, partials, received, kinds):
    n_arr = len(partials)

    def make(in_refs, out_refs, sems):
        send_sems, recv_sems = sems
        x, y, c = _mesh_position()
        copies = [pltpu.make_async_remote_copy(
            src_ref=in_refs[a].at[2 * qx + qy, layer], dst_ref=out_refs[a].at[j, layer],
            send_sem=send_sems.at[a * N_OTHER_CHIPS + j], recv_sem=recv_sems.at[a * N_OTHER_CHIPS + j],
            device_id=(qx, qy, c), device_id_type=MESH)
            for a in range(n_arr) for j, (qx, qy) in enumerate(_other_chips(x, y))]

        def start():
            for cp in copies:
                cp.start()

        def finish():
            for cp in copies:
                cp.wait_recv()
            for cp in copies:
                cp.wait_send()

        return start, finish

    inputs = list(partials)
    aliases = {}
    if received is not None:
        inputs += list(received)
        aliases = {n_arr + a: a for a in range(n_arr)}
    return _Exchange(
        inputs, [jax.ShapeDtypeStruct((N_OTHER_CHIPS, DEPTH) + _BLOCK_SHAPES[k], BF16) for k in kinds], aliases,
        [pltpu.SemaphoreType.DMA((n_arr * N_OTHER_CHIPS,)), pltpu.SemaphoreType.DMA((n_arr * N_OTHER_CHIPS,))], make)


def _host(exchange, args, in_specs, out_shape, out_specs, scratch):
    n_own = (len(args), len(out_shape), len(scratch))
    if exchange is None:
        return {}, lambda refs: (refs, None)
    n_ex = (len(exchange.inputs), len(exchange.out_shapes), len(exchange.sem_shapes))
    aliases = {n_own[0] + i: n_own[1] + o for i, o in exchange.aliases.items()}
    args += exchange.inputs
    in_specs += [HBM] * n_ex[0]
    out_shape += exchange.out_shapes
    out_specs += [HBM] * n_ex[1]
    scratch += exchange.sem_shapes

    def split(refs):
        own, theirs, at = [], [], 0
        for mine, ex in zip(n_own, n_ex):
            own += refs[at:at + mine]
            theirs.append(refs[at + mine:at + mine + ex])
            at += mine + ex
        return own, exchange.make(*theirs)

    return aliases, split


def _all_gather_small(v, name, exchange=None):
    vmem = pl.BlockSpec(memory_space=pltpu.VMEM)
    args, in_specs = [v], [vmem]
    out_shape, out_specs = [jax.ShapeDtypeStruct((N_DEV,) + v.shape, v.dtype)], [vmem]
    scratch = [pltpu.SemaphoreType.DMA((N_DEV - 1,)), pltpu.SemaphoreType.DMA((N_DEV - 1,))]
    aliases, split = _host(exchange, args, in_specs, out_shape, out_specs, scratch)

    def body(*refs):
        (v_ref, out_ref, send_sems, recv_sems), hosted = split(refs)
        if hosted is not None:
            hosted[0]()
        x, y, c = _mesh_position()
        me = _block_id(x, y, c)
        out_ref[me] = v_ref[...]
        sends = []
        for k in range(1, N_DEV):
            px, py, pc = x ^ ((k >> 2) & 1), y ^ ((k >> 1) & 1), c ^ (k & 1)
            send = pltpu.make_async_remote_copy(
                src_ref=v_ref, dst_ref=out_ref.at[me], send_sem=send_sems.at[k - 1], recv_sem=recv_sems.at[k - 1],
                device_id=(px, py, pc), device_id_type=MESH)
            send.start()
            sends.append((send, _block_id(px, py, pc)))
        for k, (send, peer) in enumerate(sends):
            pltpu.make_async_remote_copy(
                src_ref=v_ref, dst_ref=out_ref.at[peer], send_sem=send_sems.at[k], recv_sem=recv_sems.at[k],
                device_id=(x, y, c), device_id_type=MESH).wait_recv()
        for send, _ in sends:
            send.wait_send()
        if hosted is not None:
            hosted[1]()

    outs = pl.pallas_call(
        body, name=name, in_specs=in_specs, out_specs=out_specs, out_shape=out_shape, scratch_shapes=scratch,
        input_output_aliases=aliases, compiler_params=_params(),
    )(*args)
    return outs[0] if exchange is None else outs


def _forward_layer(layer, x, vec, cps, wpool, win, wout, exchange):
    t_len = x.shape[0]
    n_tiles = t_len // ROW_TILE
    row = lambda cols: pl.BlockSpec((ROW_TILE, cols), lambda i: (i, 0))
    args = [x, vec, cps, wpool, win, wout]
    in_specs = [row(D_MODEL), _layer_spec(vec.shape, layer), _layer_spec(cps.shape, layer),
                _layer_spec(wpool.shape, layer), _whole_spec(win.shape), _whole_spec(wout.shape)]
    out_shape = [jax.ShapeDtypeStruct((t_len, D_MODEL), F32), jax.ShapeDtypeStruct((t_len, IN_COLS), BF16),
                 jax.ShapeDtypeStruct((t_len, D_MODEL), BF16)]
    out_specs = [row(D_MODEL), row(IN_COLS), row(D_MODEL)]
    scratch = [pltpu.VMEM((CONV_HALO, CONV_W), F32), pltpu.VMEM((POOL_HALO, POOL_W), F32)]
    aliases, split = _host(exchange, args, in_specs, out_shape, out_specs, scratch)

    def body(*refs):
        (x_ref, vec_ref, cps_ref, wpool_ref, win_ref, wout_ref, xo_ref, proj_ref, y_ref, zc_ref, pc_ref), hosted = (
            split(refs))
        i = pl.program_id(0)

        @pl.when(i == 0)
        def _():
            if hosted is not None:
                hosted[0]()
            zc_ref[...] = jnp.zeros_like(zc_ref)
            pc_ref[...] = jnp.zeros_like(pc_ref)

        x_t = x_ref[...]
        shift, scale, gate = vec_ref[0:1, :], vec_ref[1:2, :], vec_ref[2:3, :]
        g_pre, g_post = vec_ref[3:4, :], vec_ref[4:5, :]
        rx = lax.rsqrt(jnp.mean(x_t * x_t, axis=-1, keepdims=True) + NORM_EPS)
        h = (x_t * rx) * g_pre * (1.0 + scale) + shift
        proj_b = _dot(h.astype(BF16), win_ref[...]).astype(BF16)
        proj_ref[...] = proj_b
        mx = _mixer_forward(proj_b.astype(F32), zc_ref[...], pc_ref[...], cps_ref[...], wpool_ref, i * ROW_TILE)
        zc_ref[...] = mx["z"][ROW_TILE - CONV_HALO:]
        pc_ref[...] = mx["u_p"][ROW_TILE - POOL_HALO:]
        y_b = _dot(mx["ycat"], wout_ref[...]).astype(BF16)
        y_ref[...] = y_b
        y_t = y_b.astype(F32)
        ry = lax.rsqrt(jnp.mean(y_t * y_t, axis=-1, keepdims=True) + NORM_EPS)
        xo_ref[...] = x_t + gate * (y_t * ry * g_post)

        if hosted is not None:
            pl.when(i == n_tiles - 1)(hosted[1])

    return pl.pallas_call(
        body, name=f"forward_layer_{layer}", grid=(n_tiles,), in_specs=in_specs, out_specs=out_specs,
        out_shape=out_shape, scratch_shapes=scratch, input_output_aliases=aliases,
        compiler_params=_params(dimension_semantics=("arbitrary",)),
    )(*args)


def _loss_head(x_final, target):
    t_len = x_final.shape[0]
    n_tiles = t_len // ROW_TILE

    def body(x_ref, t_ref, dx_ref, loss_ref):
        @pl.when(pl.program_id(0) == 0)
        def _():
            loss_ref[...] = jnp.zeros_like(loss_ref)

        err = x_ref[...] - t_ref[...]
        dx_ref[...] = err * (1.0 / D_MODEL)
        loss_ref[...] += jnp.sum(err * err) * (0.5 / D_MODEL)

    row = pl.BlockSpec((ROW_TILE, D_MODEL), lambda i: (i, 0))
    return pl.pallas_call(
        body, name="loss_head", grid=(n_tiles,), in_specs=[row, row],
        out_specs=[row, pl.BlockSpec((SUBLANES, LANES), lambda i: (0, 0))],
        out_shape=[jax.ShapeDtypeStruct((t_len, D_MODEL), F32), jax.ShapeDtypeStruct((SUBLANES, LANES), F32)],
        compiler_params=_params(dimension_semantics=("arbitrary",)),
    )(x_final, target)


def _backward_mixers(layer, dxo, y, proj, vec, cps, wpool, wout, exchange):
    t_len = dxo.shape[0]
    n_tiles = t_len // ROW_TILE
    halo_per_tile = ROW_TILE // POOL_HALO
    rev = lambda cols: pl.BlockSpec((ROW_TILE, cols), lambda i: (n_tiles - 1 - i, 0))
    halo_spec = pl.BlockSpec(
        (POOL_HALO, IN_COLS), lambda i: (jnp.maximum((n_tiles - 1 - i) * halo_per_tile - 1, 0), 0))
    gwpool_shape = (len(POOL_WINDOWS), GROUP_D, GROUP_D)
    args = [dxo, y, proj, proj, vec, cps, wpool, wout]
    in_specs = [rev(D_MODEL), rev(D_MODEL), rev(IN_COLS), halo_spec, _layer_spec(vec.shape, layer),
                _layer_spec(cps.shape, layer), _layer_spec(wpool.shape, layer), _whole_spec(wout.shape)]
    out_shape = [jax.ShapeDtypeStruct((t_len, IN_COLS), BF16), jax.ShapeDtypeStruct((D_MODEL, D_MODEL), BF16),
                 jax.ShapeDtypeStruct(gwpool_shape, BF16), jax.ShapeDtypeStruct((SUBLANES, CONV_W), F32),
                 jax.ShapeDtypeStruct((SUBLANES, D_MODEL), F32)]
    out_specs = [rev(IN_COLS), _whole_spec((D_MODEL, D_MODEL)), _whole_spec(gwpool_shape),
                 _whole_spec((SUBLANES, CONV_W)), _whole_spec((SUBLANES, D_MODEL))]
    scratch = [pltpu.VMEM((D_MODEL, D_MODEL), F32), pltpu.VMEM(gwpool_shape, F32),
               pltpu.VMEM((CONV_HALO, CONV_W), F32), pltpu.VMEM((POOL_HALO, POOL_W), F32)]
    aliases, split = _host(exchange, args, in_specs, out_shape, out_specs, scratch)

    def body(*refs):
        (dxo_ref, y_ref, proj_ref, projh_ref, vec_ref, cps_ref, wpool_ref, wout_ref, dproj_ref, gwout_ref, gwpool_ref,
         dcps_ref, dvec_ref, gwout_acc, gwpool_acc, dcc_ref, qc_ref), hosted = split(refs)
        i = pl.program_id(0)
        tile = n_tiles - 1 - i

        @pl.when(i == 0)
        def _():
            if hosted is not None:
                hosted[0]()
            gwout_acc[...] = jnp.zeros_like(gwout_acc)
            gwpool_acc[...] = jnp.zeros_like(gwpool_acc)
            dcps_ref[...] = jnp.zeros_like(dcps_ref)
            dvec_ref[...] = jnp.zeros_like(dvec_ref)
            dcc_ref[...] = jnp.zeros_like(dcc_ref)
            qc_ref[...] = jnp.zeros_like(qc_ref)

        gate, g_post = vec_ref[2:3, :], vec_ref[4:5, :]
        dxo_t = dxo_ref[...]
        y_t = y_ref[...].astype(F32)
        ry = lax.rsqrt(jnp.mean(y_t * y_t, axis=-1, keepdims=True) + NORM_EPS)
        yh = y_t * ry
        dxy = dxo_t * yh
        dvec_ref[0:1, :] += jnp.sum(dxy * g_post, axis=0, keepdims=True)
        dvec_ref[1:2, :] += jnp.sum(dxy * gate, axis=0, keepdims=True)
        dyh = dxo_t * (gate * g_post)
        dy_b = (ry * (dyh - yh * jnp.mean(dyh * yh, axis=-1, keepdims=True))).astype(BF16)

        halo = jnp.where(tile > 0, projh_ref[...].astype(F32), 0.0)
        hu_a, _, hc_a, _, hu_p, _ = _split_proj(halo)
        z_halo = (hc_a * hu_a)[POOL_HALO - CONV_HALO:]
        mx = _mixer_forward(proj_ref[...].astype(F32), z_halo, hu_p, cps_ref[...], wpool_ref, tile * ROW_TILE)

        gwout_acc[...] += _dot_tn(mx["ycat"], dy_b)
        dycat = _dot_nt(dy_b, wout_ref[...])
        dy_a, dy_p = dycat[:, :CONV_W], dycat[:, CONV_W:]

        t_a = dy_a * mx["silu_a"]
        db_a = t_a * mx["conv"]
        dconv = t_a * mx["b_a"]
        dg_a = dy_a * mx["b_a"] * mx["conv"] * (mx["sig_a"] * (1.0 + mx["g_a"] * (1.0 - mx["sig_a"])))
        dccat = jnp.concatenate([dconv, dcc_ref[...]], axis=0)
        dc1 = _rows_from_after(dccat, 1)[:ROW_TILE]
        dc2 = _rows_from_after(dccat, 2)[:ROW_TILE]
        dz = mx["w2"] * dconv + mx["w1"] * dc1 + mx["w0"] * dc2
        dcc_ref[...] = dconv[:CONV_HALO]
        dcps_ref[0:1, :] += jnp.sum(dconv * mx["z2"], axis=0, keepdims=True)
        dcps_ref[1:2, :] += jnp.sum(dconv * mx["z1"], axis=0, keepdims=True)
        dcps_ref[2:3, :] += jnp.sum(dconv * mx["z"], axis=0, keepdims=True)
        du_a = dz * mx["c_a"]
        dc_a = dz * mx["u_a"]

        t_p = dy_p * mx["silu_p"]
        dcps_ref[3:4, :] += jnp.sum(t_p * mx["mixed"], axis=0, keepdims=True)
        dmixed = (t_p * mx["ps"]).astype(BF16)
        dg_p = dy_p * mx["mixed"] * mx["ps"] * (mx["sig_p"] * (1.0 + mx["g_p"] * (1.0 - mx["sig_p"])))
        du_p, q_head = [], []
        for g, w in enumerate(POOL_WINDOWS):
            cols = slice(g * GROUP_D, (g + 1) * GROUP_D)
            dm_g = dmixed[:, cols]
            dpooled_g = _dot_nt(dm_g, wpool_ref[g])
            gwpool_acc[g] += _dot_tn(mx["pooled"][g].astype(BF16), dm_g)
            q_g = dpooled_g * mx["inv"][g]
            q_head.append(q_g[:POOL_HALO])
            s = jnp.concatenate([q_g, qc_ref[:, cols]], axis=0)
            step = 1
            while step < w:
                s = s + _rows_from_after(s, step)
                step *= 2
            du_p.append(s[:ROW_TILE] - dpooled_g)
        qc_ref[...] = jnp.concatenate(q_head, axis=1)
        dproj_ref[...] = jnp.concatenate([du_a, db_a, dc_a, dg_a] + du_p + [dg_p], axis=1).astype(BF16)

        @pl.when(i == n_tiles - 1)
        def _():
            gwout_ref[...] = gwout_acc[...].astype(BF16)
            gwpool_ref[...] = gwpool_acc[...].astype(BF16)
            if hosted is not None:
                hosted[1]()

    return pl.pallas_call(
        body, name=f"backward_mixers_{layer}", grid=(n_tiles,), in_specs=in_specs, out_specs=out_specs,
        out_shape=out_shape, scratch_shapes=scratch, input_output_aliases=aliases,
        compiler_params=_params(dimension_semantics=("arbitrary",)),
    )(*args)


def _backward_input_proj(layer, x, dproj, dxo, vec, win, exchange):
    t_len = x.shape[0]
    n_tiles = t_len // ROW_TILE
    row = lambda cols: pl.BlockSpec((ROW_TILE, cols), lambda i: (i, 0))
    args = [x, dproj, dxo, vec, win]
    in_specs = [row(D_MODEL), row(IN_COLS), row(D_MODEL), _layer_spec(vec.shape, layer), _whole_spec(win.shape)]
    out_shape = [jax.ShapeDtypeStruct((t_len, D_MODEL), F32), jax.ShapeDtypeStruct((D_MODEL, IN_COLS), BF16),
                 jax.ShapeDtypeStruct((SUBLANES, D_MODEL), F32)]
    out_specs = [row(D_MODEL), _whole_spec((D_MODEL, IN_COLS)), _whole_spec((SUBLANES, D_MODEL))]
    scratch = [pltpu.VMEM((D_MODEL, IN_COLS), F32)]
    aliases, split = _host(exchange, args, in_specs, out_shape, out_specs, scratch)

    def body(*refs):
        (x_ref, dproj_ref, dxo_ref, vec_ref, win_ref, dx_ref, gwin_ref, dvec_ref, gwin_acc), hosted = split(refs)
        i = pl.program_id(0)

        @pl.when(i == 0)
        def _():
            if hosted is not None:
                hosted[0]()
            gwin_acc[...] = jnp.zeros_like(gwin_acc)
            dvec_ref[...] = jnp.zeros_like(dvec_ref)

        shift, scale, g_pre = vec_ref[0:1, :], vec_ref[1:2, :], vec_ref[3:4, :]
        x_t = x_ref[...]
        rx = lax.rsqrt(jnp.mean(x_t * x_t, axis=-1, keepdims=True) + NORM_EPS)
        xn = x_t * rx
        mod_scale = 1.0 + scale
        h_b = (xn * g_pre * mod_scale + shift).astype(BF16)
        dproj_t = dproj_ref[...]
        gwin_acc[...] += _dot_tn(h_b, dproj_t)
        dh = _dot_nt(dproj_t, win_ref[...])
        dvec_ref[0:1, :] += jnp.sum(dh, axis=0, keepdims=True)
        dhx = dh * xn
        dvec_ref[1:2, :] += jnp.sum(dhx * g_pre, axis=0, keepdims=True)
        dvec_ref[2:3, :] += jnp.sum(dhx * mod_scale, axis=0, keepdims=True)
        dxn = dh * (g_pre * mod_scale)
        dx_ref[...] = dxo_ref[...] + rx * (dxn - xn * jnp.mean(dxn * xn, axis=-1, keepdims=True))

        @pl.when(i == n_tiles - 1)
        def _():
            gwin_ref[...] = gwin_acc[...].astype(BF16)
            if hosted is not None:
                hosted[1]()

    return pl.pallas_call(
        body, name=f"backward_input_proj_{layer}", grid=(n_tiles,), in_specs=in_specs, out_specs=out_specs,
        out_shape=out_shape, scratch_shapes=scratch, input_output_aliases=aliases,
        compiler_params=_params(dimension_semantics=("arbitrary",)),
    )(*args)


def _add_sibling_blocks(name, layer, grads, received, core, partials, kinds):
    n_arr = len(grads)

    def body(core_ref, *refs):
        mine, theirs, outs = refs[:n_arr], refs[n_arr:2 * n_arr], refs[-n_arr:]
        for a in range(n_arr):
            outs[a][...] = (mine[a][...].astype(F32) + theirs[a][...].astype(F32)).astype(BF16)

    own_of_kind = [
        pl.BlockSpec((D_MODEL, W_IN_SHARD), lambda q, core_ref: (0, 2 * q + core_ref[0])),
        pl.BlockSpec((W_OUT_SHARD, D_MODEL), lambda q, core_ref: (2 * q + core_ref[0], 0)),
        pl.BlockSpec((POOL_SHARD, GROUP_D), lambda q, core_ref: (2 * q + core_ref[0], 0)),
    ]
    shapes = [_BLOCK_SHAPES[k] for k in kinds]
    recv_specs = [pl.BlockSpec((None,) + s, lambda q, core_ref: (q, 0, 0)) for s in shapes]
    out_specs = [pl.BlockSpec((None, None) + s, lambda q, core_ref: (q, layer, 0, 0)) for s in shapes]
    args = [core, *grads, *received]
    in_specs = [own_of_kind[k] for k in kinds] + recv_specs
    aliases = {}
    if partials is not None:
        aliases = {len(args) + a: a for a in range(n_arr)}
        args += list(partials)
        in_specs += [HBM] * n_arr
    return pl.pallas_call(
        body, name=name,
        grid_spec=pltpu.PrefetchScalarGridSpec(
            num_scalar_prefetch=1, grid=(N_CHIP,), in_specs=in_specs, out_specs=out_specs),
        out_shape=[jax.ShapeDtypeStruct((N_CHIP, DEPTH) + s, BF16) for s in shapes],
        input_output_aliases=aliases,
        compiler_params=_params(dimension_semantics=("arbitrary",)),
    )(*args)


def _modulation_columns(c_all, w_ada):
    def body(c_ref, w_ref, cact_ref, out_ref):
        c_t = c_ref[...]
        c_act = c_t * _sigmoid(c_t)
        cact_ref[...] = c_act
        out_ref[...] = jnp.dot(c_act, w_ref[...], preferred_element_type=F32, precision=lax.Precision.HIGHEST)

    return pl.pallas_call(
        body, name="modulation_columns", grid=(DEPTH,),
        in_specs=[pl.BlockSpec((N_DEV, D_MODEL), lambda l: (0, 0)),
                  pl.BlockSpec((None, D_MODEL, W_IN_SHARD), lambda l: (l, 0, 0))],
        out_specs=[pl.BlockSpec((N_DEV, D_MODEL), lambda l: (0, 0)),
                   pl.BlockSpec((N_DEV, W_IN_SHARD), lambda l: (0, l))],
        out_shape=[jax.ShapeDtypeStruct((N_DEV, D_MODEL), F32),
                   jax.ShapeDtypeStruct((N_DEV, DEPTH * W_IN_SHARD), F32)],
        compiler_params=_params(dimension_semantics=("arbitrary",)),
    )(c_all, w_ada)


def _adamw(w, g, m, v):
    m_new = ADAM_B1 * m + (1.0 - ADAM_B1) * g
    v_new = ADAM_B2 * v + (1.0 - ADAM_B2) * (g * g)
    m_hat = m_new / (1.0 - ADAM_B1 ** ADAM_STEP)
    v_hat = v_new / (1.0 - ADAM_B2 ** ADAM_STEP)
    delta = -ADAM_LR * (m_hat / (jnp.sqrt(v_hat) + ADAM_EPS) + ADAM_WD * w)
    return delta, m_new, v_new


def _adamw_w_ada(w, m, v, c_act_t, dmod_cols, exchange):
    big = pl.BlockSpec((None, D_MODEL, W_IN_SHARD), lambda l: (l, 0, 0))
    args = [w, m, v, c_act_t, dmod_cols]
    in_specs = [big, big, big, pl.BlockSpec((D_MODEL, N_DEV), lambda l: (0, 0)),
                pl.BlockSpec((None, N_DEV, W_IN_SHARD), lambda l: (l, 0, 0))]
    out_shape, out_specs, scratch = [jax.ShapeDtypeStruct(w.shape, F32)] * 4, [big] * 4, []
    aliases, split = _host(exchange, args, in_specs, out_shape, out_specs, scratch)

    def body(*refs):
        (w_ref, m_ref, v_ref, ct_ref, dm_ref, g_ref, d_ref, mo_ref, vo_ref), hosted = split(refs)
        if hosted is not None:
            pl.when(pl.program_id(0) == 0)(hosted[0])
        g = ct_ref[:, 0:1] * dm_ref[0:1, :]
        for b in range(1, N_DEV):
            g = g + ct_ref[:, b:b + 1] * dm_ref[b:b + 1, :]
        g_ref[...] = g
        d_ref[...], mo_ref[...], vo_ref[...] = _adamw(w_ref[...], g, m_ref[...], v_ref[...])
        if hosted is not None:
            pl.when(pl.program_id(0) == DEPTH - 1)(hosted[1])

    return pl.pallas_call(
        body, name="adamw_w_ada", grid=(DEPTH,), in_specs=in_specs, out_specs=out_specs, out_shape=out_shape,
        scratch_shapes=scratch, input_output_aliases=aliases,
        compiler_params=_params(dimension_semantics=("arbitrary",)),
    )(*args)


def _sum_chip_partials(own_ref, recv_ref):
    g = own_ref[...].astype(F32)
    for j in range(N_OTHER_CHIPS):
        g = g + recv_ref[j].astype(F32)
    return g


def _partial_specs(row_tile, cols):
    own = pl.BlockSpec((None, None, row_tile, cols), lambda l, r, chip_ref: (chip_ref[0], l, r, 0))
    recv = pl.BlockSpec((N_OTHER_CHIPS, None, row_tile, cols), lambda l, r, chip_ref: (0, l, r, 0))
    return own, recv


def _adamw_reduced(name, w, m, v, partial, received, chip, row_tile):
    depth, rows, cols = w.shape

    def body(chip_ref, w_ref, m_ref, v_ref, own_ref, recv_ref, g_ref, d_ref, mo_ref, vo_ref):
        g = _sum_chip_partials(own_ref, recv_ref)
        g_ref[...] = g
        d_ref[...], mo_ref[...], vo_ref[...] = _adamw(w_ref[...], g, m_ref[...], v_ref[...])

    blk = pl.BlockSpec((None, row_tile, cols), lambda l, r, chip_ref: (l, r, 0))
    return pl.pallas_call(
        body, name=name,
        grid_spec=pltpu.PrefetchScalarGridSpec(
            num_scalar_prefetch=1, grid=(depth, rows // row_tile),
            in_specs=[blk, blk, blk, *_partial_specs(row_tile, cols)], out_specs=[blk] * 4),
        out_shape=[jax.ShapeDtypeStruct(w.shape, F32)] * 4,
        compiler_params=_params(dimension_semantics=("arbitrary", "arbitrary")),
    )(chip, w, m, v, partial, received)


def _reduce_w_pool(partial, received, chip):
    def body(chip_ref, own_ref, recv_ref, g_ref):
        g_ref[...] = _sum_chip_partials(own_ref, recv_ref)

    return pl.pallas_call(
        body, name="reduce_w_pool",
        grid_spec=pltpu.PrefetchScalarGridSpec(
            num_scalar_prefetch=1, grid=(DEPTH, 1), in_specs=list(_partial_specs(POOL_SHARD, GROUP_D)),
            out_specs=pl.BlockSpec((POOL_SHARD, GROUP_D), lambda l, r, chip_ref: (l, 0))),
        out_shape=jax.ShapeDtypeStruct((DEPTH * POOL_SHARD, GROUP_D), F32),
        compiler_params=_params(dimension_semantics=("arbitrary", "arbitrary")),
    )(chip, partial, received)


def _adamw_small(params):
    n = len(params)

    def body(*refs):
        ins, outs = refs[:4 * n], refs[4 * n:]
        for p in range(n):
            w_ref, g_ref, m_ref, v_ref = ins[4 * p:4 * p + 4]
            d_ref, mo_ref, vo_ref = outs[3 * p:3 * p + 3]
            d_ref[...], mo_ref[...], vo_ref[...] = _adamw(w_ref[...], g_ref[...], m_ref[...], v_ref[...])

    vmem = pl.BlockSpec(memory_space=pltpu.VMEM)
    flat = [a for group in params for a in group]
    out_shape = [jax.ShapeDtypeStruct(group[0].shape, F32) for group in params for _ in range(3)]
    outs = pl.pallas_call(
        body, name="adamw_small", in_specs=[vmem] * len(flat), out_specs=[vmem] * len(out_shape),
        out_shape=out_shape, compiler_params=_params(),
    )(*flat)
    return [tuple(outs[3 * p:3 * p + 3]) for p in range(n)]


def _sum_sources(slabs):
    def body(s_ref, o_ref):
        acc = s_ref[0]
        for b in range(1, N_DEV):
            acc = acc + s_ref[b]
        o_ref[...] = acc

    vmem = pl.BlockSpec(memory_space=pltpu.VMEM)
    return pl.pallas_call(
        body, name="sum_small_grads", in_specs=[vmem], out_specs=vmem,
        out_shape=jax.ShapeDtypeStruct(slabs.shape[1:], F32), compiler_params=_params(),
    )(slabs)


def _to_bf16(a, name):
    def body(a_ref, o_ref):
        o_ref[...] = a_ref[...].astype(BF16)

    spec = pl.BlockSpec((None,) + a.shape[1:], lambda l: (l, 0, 0))
    return pl.pallas_call(
        body, name=name, grid=(a.shape[0],), in_specs=[spec], out_specs=spec,
        out_shape=jax.ShapeDtypeStruct(a.shape, BF16), compiler_params=_params(dimension_semantics=("arbitrary",)),
    )(a)


def kernel(x, c, w_ada, b_ada, g_pre, w_in, w_conv, w_pool, pool_scale, w_out, g_post, loss_target, m_w_ada, m_b_ada, m_g_pre, m_w_in, m_w_conv, m_w_pool, m_pool_scale, m_w_out, m_g_post, v_w_ada, v_b_ada, v_g_pre, v_w_in, v_w_conv, v_w_pool, v_pool_scale, v_w_out, v_g_post):
    mx, my, mc = _mesh_position()
    me = _block_id(mx, my, mc)
    chip = (2 * mx + my).astype(jnp.int32).reshape(1)
    core = mc.astype(jnp.int32).reshape(1)
    x0 = x[0]
    target = loss_target[0]
    conv_shard = w_conv.shape[-1]

    own_small = jnp.concatenate([c, w_conv.reshape(1, DEPTH * 3 * conv_shard)], axis=1)
    all_small = _all_gather_small(own_small, "all_gather_c_w_conv")[:, 0, :]
    c_all = all_small[:, :D_MODEL]
    w_conv_full = all_small[:, D_MODEL:].reshape(N_DEV, DEPTH, 3, conv_shard).transpose(1, 2, 0, 3).reshape(
        DEPTH, 3, CONV_W)
    cps = jnp.concatenate([w_conv_full, pool_scale[:, None], jnp.zeros((DEPTH, 4, CONV_W), F32)], axis=1)

    c_act, pieces = _modulation_columns(c_all, w_ada)
    mod_all = _all_gather_small(pieces, "all_gather_modulation")
    mod_mine = lax.dynamic_index_in_dim(mod_all, me, axis=1, keepdims=False)
    mod = mod_mine.reshape(N_DEV, DEPTH, W_IN_SHARD).transpose(1, 0, 2).reshape(DEPTH, 3 * D_MODEL) + b_ada
    zeros_d = jnp.zeros((DEPTH, 3, D_MODEL), F32)
    vec = jnp.concatenate([mod.reshape(DEPTH, 3, D_MODEL), g_pre[:, None], g_post[:, None], zeros_d], axis=1)

    first_sems, shards, landing = _gather_weights_start(
        _to_bf16(w_in, "cast_w_in"), _to_bf16(w_out, "cast_w_out"), [mod_all, all_small])
    wpool_b = _to_bf16(w_pool.reshape(DEPTH, POOL_ROWS, GROUP_D), "cast_w_pool").reshape(w_pool.shape)

    xs, projs, ys, wins, wouts = [x0], [], [], [], []
    for l in range(DEPTH):
        passed_sems, zones = _gather_weights_pass_on(
            l, first_sems[1], landing[l], [vec, cps, wpool_b] if l == 0 else [xs[-1]])
        win, wout = _gather_weights_finish(l, first_sems, passed_sems, shards, zones)
        x_next, proj, y = _forward_layer(l, xs[-1], vec, cps, wpool_b, win, wout, None)
        xs.append(x_next)
        projs.append(proj)
        ys.append(y)
        wins.append(win)
        wouts.append(wout)
    dx, loss_tile = _loss_head(xs[DEPTH], target)

    slab_rows = [None] * DEPTH
    gwin_above = part_w = recv_w = part_o = recv_o = None
    for l in reversed(range(DEPTH)):
        hosted = None
        if gwin_above is not None:
            hosted = _merge_exchanges([_sibling_exchange([gwin_above], W_IN_KIND),
                                       _chips_exchange(l + 1, part_o, recv_o, W_OUT_KINDS)])
        dproj, gwout, gwpool, dcps, dvec1, *res = _backward_mixers(
            l, dx, ys[l], projs[l], vec, cps, wpool_b, wouts[l], hosted)
        chips_w = None
        if gwin_above is not None:
            recv_o = res[1:]
            part_w = _add_sibling_blocks(
                f"grad_add_w_in_{l + 1}", l + 1, [gwin_above], res[:1], core, part_w, W_IN_KIND)
            chips_w = _chips_exchange(l + 1, part_w, recv_w, W_IN_KIND)
        grads_o = [gwout, gwpool.reshape(POOL_ROWS, GROUP_D)]
        hosted = _merge_exchanges([chips_w, _sibling_exchange(grads_o, W_OUT_KINDS)])
        dx, gwin_above, dvec2, *res = _backward_input_proj(l, xs[l], dproj, dx, vec, wins[l], hosted)
        if chips_w is not None:
            recv_w, res = res[:1], res[1:]
        part_o = _add_sibling_blocks(f"grad_add_w_out_{l}", l, grads_o, res, core, part_o, W_OUT_KINDS)
        slab_rows[l] = jnp.concatenate(
            [dvec2[0], dvec2[1], dvec1[0], dvec2[2], dvec1[1], dcps[3], dcps[0], dcps[1], dcps[2]])
    grad_x = dx[None]

    loss_row = jnp.pad(loss_tile[0], (0, SLAB_COLS - LANES))
    slab = jnp.stack(slab_rows + [loss_row] + [jnp.zeros((SLAB_COLS,), F32)] * (SLAB_ROWS - DEPTH - 1))
    hosted = _merge_exchanges([_sibling_exchange([gwin_above], W_IN_KIND),
                               _chips_exchange(0, part_o, recv_o, W_OUT_KINDS)])
    slabs, *res = _all_gather_small(slab, "all_gather_small_grads", hosted)
    recv_o = res[1:]
    part_w = _add_sibling_blocks("grad_add_w_in_0", 0, [gwin_above], res[:1], core, part_w, W_IN_KIND)
    total = _sum_sources(slabs)
    loss = total[DEPTH, 0]
    o = 3 * D_MODEL
    g_b_ada = total[:DEPTH, :o]
    g_g_pre = total[:DEPTH, o:o + D_MODEL]
    g_g_post = total[:DEPTH, o + D_MODEL:o + 2 * D_MODEL]
    g_pool_scale = total[:DEPTH, o + 2 * D_MODEL:o + 2 * D_MODEL + POOL_W]
    g_conv_full = total[:DEPTH, o + 2 * D_MODEL + POOL_W:].reshape(DEPTH, 3, CONV_W)
    g_w_conv = lax.dynamic_slice_in_dim(g_conv_full, me * conv_shard, conv_shard, axis=2)

    chips_w = _chips_exchange(0, part_w, recv_w, W_IN_KIND)
    sems, in_flight, landing, token = _start_exchange(chips_w, "grad_chips_w_in_0_start")

    dmod_all = slabs[:, :DEPTH, :o].reshape(N_DEV, DEPTH, N_DEV, W_IN_SHARD)
    dmod_cols = lax.dynamic_index_in_dim(dmod_all, me, axis=2, keepdims=False).transpose(1, 0, 2) + token[0, 0]
    g_w_ada, d_w_ada, nm_w_ada, nv_w_ada = _adamw_w_ada(w_ada, m_w_ada, v_w_ada, c_act.T, dmod_cols, None)

    g_w_out, d_w_out, nm_w_out, nv_w_out = _adamw_reduced(
        "adamw_w_out", w_out, m_w_out, v_w_out, part_o[0], recv_o[0], chip, W_OUT_SHARD)
    g_pool_all = _all_gather_small(_reduce_w_pool(part_o[1], recv_o[1], chip), "all_gather_grad_w_pool")
    g_w_pool = g_pool_all.reshape(N_DEV, DEPTH, POOL_SHARD, GROUP_D).transpose(1, 0, 2, 3).reshape(w_pool.shape)
    part_w, recv_w = _finish_exchange(
        chips_w, "grad_chips_w_in_0_finish", sems, in_flight, landing, [nv_w_ada, nv_w_out, g_pool_all])
    g_w_in, d_w_in, nm_w_in, nv_w_in = _adamw_reduced(
        "adamw_w_in", w_in, m_w_in, v_w_in, part_w[0], recv_w[0], chip, ROW_TILE)

    flat2 = lambda a: a.reshape(-1, a.shape[-1])
    small = _adamw_small([
        (b_ada, g_b_ada, m_b_ada, v_b_ada),
        (g_pre, g_g_pre, m_g_pre, v_g_pre),
        (flat2(w_conv), flat2(g_w_conv), flat2(m_w_conv), flat2(v_w_conv)),
        (flat2(w_pool), flat2(g_w_pool), flat2(m_w_pool), flat2(v_w_pool)),
        (pool_scale, g_pool_scale, m_pool_scale, v_pool_scale),
        (g_post, g_g_post, m_g_post, v_g_post),
    ])
    (d_b_ada, nm_b_ada, nv_b_ada), (d_g_pre, nm_g_pre, nv_g_pre), conv_upd, pool_upd, \
        (d_ps, nm_ps, nv_ps), (d_g_post, nm_g_post, nv_g_post) = small
    d_w_conv, nm_w_conv, nv_w_conv = (a.reshape(w_conv.shape) for a in conv_upd)
    d_w_pool, nm_w_pool, nv_w_pool = (a.reshape(w_pool.shape) for a in pool_upd)

    return (loss, grad_x,
            g_w_ada, g_b_ada, g_g_pre, g_w_in, g_w_conv, g_w_pool, g_pool_scale, g_w_out, g_g_post,
            d_w_ada, d_b_ada, d_g_pre, d_w_in, d_w_conv, d_w_pool, d_ps, d_w_out, d_g_post,
            nm_w_ada, nm_b_ada, nm_g_pre, nm_w_in, nm_w_conv, nm_w_pool, nm_ps, nm_w_out, nm_g_post,
            nv_w_ada, nv_b_ada, nv_g_pre, nv_w_in, nv_w_conv, nv_w_pool, nv_ps, nv_w_out, nv_g_post)
```

```python
import jax
import jax.numpy as jnp
from jax import lax
from jax.experimental import pallas as pl
from jax.experimental.pallas import tpu as pltpu

F32 = jnp.float32
BF16 = jnp.bfloat16

D_MODEL = 1024
DEPTH = 4
CONV_W = 512
POOL_W = 512
POOL_WINDOWS = (2, 4, 8, 16)
GROUP_D = 128
IN_COLS = 4 * CONV_W + 2 * POOL_W
NORM_EPS = 1e-6

ADAM_LR = 0.001
ADAM_B1 = 0.9
ADAM_B2 = 0.999
ADAM_EPS = 1e-08
ADAM_WD = 0.01
ADAM_STEP = 10

N_DEV = 8
N_CHIP = 4
N_OTHER_CHIPS = N_CHIP - 1
MESH = pl.DeviceIdType.MESH
W_IN_SHARD = IN_COLS // N_DEV
W_OUT_SHARD = D_MODEL // N_DEV
POOL_ROWS = len(POOL_WINDOWS) * GROUP_D
POOL_SHARD = POOL_ROWS // N_DEV

SUBLANES = 8
LANES = 128
VMEM_LIMIT_BYTES = 56 * 1024 * 1024
ROW_TILE = 512
POOL_HALO = 16
CONV_HALO = SUBLANES

SLAB_COLS = 3 * D_MODEL + D_MODEL + D_MODEL + POOL_W + 3 * CONV_W
SLAB_ROWS = SUBLANES

HBM = pl.BlockSpec(memory_space=pl.ANY)


def _params(**kw):
    return pltpu.CompilerParams(vmem_limit_bytes=VMEM_LIMIT_BYTES, **kw)


def _sigmoid(v):
    return 1.0 / (1.0 + jnp.exp(-v))


def _dot(a, b):
    return jnp.dot(a, b, preferred_element_type=F32)


def _dot_tn(a, b):
    return lax.dot_general(a, b, (((0,), (0,)), ((), ())), preferred_element_type=F32)


def _dot_nt(a, b):
    return lax.dot_general(a, b, (((1,), (1,)), ((), ())), preferred_element_type=F32)


def _rows_from_before(v, k):
    return pltpu.roll(v, k, 0)


def _rows_from_after(v, k):
    return pltpu.roll(v, v.shape[0] - k, 0)


def _window_counts(t0, rows):
    return (lax.broadcasted_iota(jnp.int32, (rows, 1), 0) + (t0 + 1)).astype(F32)


def _split_proj(p32):
    cw = CONV_W
    return (p32[:, 0 * cw:1 * cw], p32[:, 1 * cw:2 * cw], p32[:, 2 * cw:3 * cw], p32[:, 3 * cw:4 * cw],
            p32[:, 4 * cw:4 * cw + POOL_W], p32[:, 4 * cw + POOL_W:])


def _mixer_forward(p32, z_halo, up_halo, cps, wpool_ref, t0):
    tm = p32.shape[0]
    u_a, b_a, c_a, g_a, u_p, g_p = _split_proj(p32)
    w0, w1, w2, ps = cps[0:1, :], cps[1:2, :], cps[2:3, :], cps[3:4, :]
    z = c_a * u_a
    zcat = jnp.concatenate([z_halo, z], axis=0)
    z1 = _rows_from_before(zcat, 1)[CONV_HALO:]
    z2 = _rows_from_before(zcat, 2)[CONV_HALO:]
    conv = w0 * z2 + w1 * z1 + w2 * z
    sig_a = _sigmoid(g_a)
    silu_a = g_a * sig_a
    y_a = b_a * conv * silu_a

    pcat = jnp.concatenate([up_halo, u_p], axis=0)
    counts = _window_counts(t0, tm)
    pooled, mixed, inv = [], [], []
    for g, w in enumerate(POOL_WINDOWS):
        cols = slice(g * GROUP_D, (g + 1) * GROUP_D)
        s = pcat[:, cols]
        step = 1
        while step < w:
            s = s + _rows_from_before(s, step)
            step *= 2
        inv_g = 1.0 / jnp.minimum(counts, float(w))
        pooled_g = s[POOL_HALO:] * inv_g - u_p[:, cols]
        pooled.append(pooled_g)
        inv.append(inv_g)
        mixed.append(_dot(pooled_g.astype(BF16), wpool_ref[g]))
    mixed = jnp.concatenate(mixed, axis=1)
    sig_p = _sigmoid(g_p)
    silu_p = g_p * sig_p
    y_p = mixed * ps * silu_p
    ycat = jnp.concatenate([y_a, y_p], axis=1).astype(BF16)
    return dict(u_a=u_a, b_a=b_a, c_a=c_a, g_a=g_a, u_p=u_p, g_p=g_p, z=z, z1=z1, z2=z2, conv=conv, sig_a=sig_a,
                silu_a=silu_a, pooled=pooled, inv=inv, mixed=mixed, sig_p=sig_p, silu_p=silu_p, ycat=ycat,
                w0=w0, w1=w1, w2=w2, ps=ps)


def _layer_spec(shape, layer):
    nd = len(shape)
    return pl.BlockSpec((None,) + tuple(shape[1:]), lambda i, _l=layer, _n=nd: (_l,) + (0,) * (_n - 1))


def _whole_spec(shape):
    return pl.BlockSpec(tuple(shape), lambda i, _n=len(shape): (0,) * _n)


def _mesh_position():
    return lax.axis_index("x"), lax.axis_index("y"), lax.axis_index("c")


def _block_id(x, y, c):
    return 4 * x + 2 * y + c


def _other_chips(x, y):
    return [(x ^ 1, y), (x, y ^ 1), (x ^ 1, y ^ 1)]


def _col_block(ref, blk):
    return ref.at[:, pl.ds(pl.multiple_of(blk * W_IN_SHARD, LANES), W_IN_SHARD)]


def _row_block(rows):
    def block(ref, blk):
        return ref.at[pl.ds(pl.multiple_of(blk * rows, rows), rows), :]
    return block


_BLOCK_OF = (_col_block, _row_block(W_OUT_SHARD), _row_block(POOL_SHARD))
_BLOCK_SHAPES = ((D_MODEL, W_IN_SHARD), (W_OUT_SHARD, D_MODEL), (POOL_SHARD, GROUP_D))


class _Exchange:
    def __init__(self, inputs, out_shapes, aliases, sem_shapes, make):
        self.inputs, self.out_shapes, self.aliases, self.sem_shapes, self.make = (
            list(inputs), list(out_shapes), dict(aliases), list(sem_shapes), make)


def _run_exchange(exchange, name):
    n_in, n_out = len(exchange.inputs), len(exchange.out_shapes)

    def body(*refs):
        start, finish = exchange.make(refs[:n_in], refs[n_in:n_in + n_out], refs[n_in + n_out:])
        start()
        finish()

    return pl.pallas_call(
        body, name=name, in_specs=[HBM] * n_in, out_specs=[HBM] * n_out, out_shape=exchange.out_shapes,
        scratch_shapes=exchange.sem_shapes, input_output_aliases=exchange.aliases, compiler_params=_params(),
    )(*exchange.inputs)


_SEM = pl.BlockSpec(memory_space=pltpu.SEMAPHORE)
_DATAFLOW = pltpu.SideEffectType.DATAFLOW_SIDE_EFFECTING


def _start_exchange(exchange, name):
    n_in, n_out, n_sem = len(exchange.inputs), len(exchange.out_shapes), len(exchange.sem_shapes)
    sources = [i for i in range(n_in) if i not in exchange.aliases]
    aliases = {i: n_sem + k for k, i in enumerate(sources)}
    aliases.update({i: n_sem + len(sources) + o for i, o in exchange.aliases.items()})

    def body(*refs):
        in_refs = refs[:n_in]
        sems = refs[n_in:n_in + n_sem]
        out_refs = refs[n_in + n_sem + len(sources):n_in + n_sem + len(sources) + n_out]
        exchange.make(in_refs, out_refs, sems)[0]()
        refs[-1][...] = jnp.zeros_like(refs[-1])

    outs = pl.pallas_call(
        body, name=name, in_specs=[HBM] * n_in,
        out_specs=[_SEM] * n_sem + [HBM] * (len(sources) + n_out) + [pl.BlockSpec(memory_space=pltpu.VMEM)],
        out_shape=(exchange.sem_shapes + [pltpu.HBM(exchange.inputs[i].shape, exchange.inputs[i].dtype) for i in sources]
                   + [pltpu.HBM(s.shape, s.dtype) for s in exchange.out_shapes]
                   + [jax.ShapeDtypeStruct((SUBLANES, LANES), F32)]),
        input_output_aliases=aliases, compiler_params=_params(has_side_effects=_DATAFLOW),
    )(*exchange.inputs)
    return outs[:n_sem], outs[n_sem:n_sem + len(sources)], outs[n_sem + len(sources):-1], outs[-1]


def _finish_exchange(exchange, name, sems, sources, landing, after):
    n_src, n_out, n_sem = len(sources), len(landing), len(sems)
    n_in = len(exchange.inputs)
    source_at = [i for i in range(n_in) if i not in exchange.aliases]

    def body(*refs):
        src_refs, out_refs = refs[:n_src], refs[n_src:n_src + n_out]
        sem_refs = refs[n_src + n_out:n_src + n_out + n_sem]
        in_refs = [None] * n_in
        for k, i in enumerate(source_at):
            in_refs[i] = src_refs[k]
        for i, o in exchange.aliases.items():
            in_refs[i] = out_refs[o]
        exchange.make(in_refs, out_refs, sem_refs)[1]()

    arrays = list(sources) + list(landing)
    outs = pl.pallas_call(
        body, name=name, in_specs=[HBM] * len(arrays) + [_SEM] * n_sem + [HBM] * len(after),
        out_specs=[HBM] * len(arrays), out_shape=[pltpu.HBM(a.shape, a.dtype) for a in arrays],
        input_output_aliases={i: i for i in range(len(arrays))}, compiler_params=_params(has_side_effects=_DATAFLOW),
    )(*arrays, *sems, *after)
    return outs[:n_src], outs[n_src:]


N_GATHERED = 2
FIRST_COPIES = 1 + N_OTHER_CHIPS
_GATHERED_SHAPES = ((D_MODEL, IN_COLS), (D_MODEL, D_MODEL))


def _first_sem(layer, a, k):
    return (layer * N_GATHERED + a) * FIRST_COPIES + k


def _gather_copy(window_of, full_ref, blk, send_sem, recv_sem, to, src=None):
    window = window_of(full_ref, blk)
    return pltpu.make_async_remote_copy(
        src_ref=window if src is None else src, dst_ref=window, send_sem=send_sem, recv_sem=recv_sem,
        device_id=to, device_id_type=MESH)


def _first_copies(layer, shard_refs, full_refs, send_sems, recv_sems, local_sems):
    x, y, c = _mesh_position()
    me = _block_id(x, y, c)
    own, remote = [], []
    for a in range(N_GATHERED):
        shard = shard_refs[a].at[layer]
        own.append(pltpu.make_async_copy(
            shard, _BLOCK_OF[a](full_refs[a], me), local_sems.at[layer * N_GATHERED + a]))
        targets = [(x, y, 1 - c)] + [(*chip, c) for chip in _other_chips(x, y)]
        remote += [_gather_copy(_BLOCK_OF[a], full_refs[a], me, send_sems.at[_first_sem(layer, a, k)],
                                recv_sems.at[_first_sem(layer, a, k)], to, src=shard)
                   for k, to in enumerate(targets)]
    return own, remote


def _gather_weights_start(win_shards, wout_shards, after):
    n_first = DEPTH * N_GATHERED * FIRST_COPIES
    sem_shapes = [pltpu.SemaphoreType.DMA((n_first,)), pltpu.SemaphoreType.DMA((n_first,)),
                  pltpu.SemaphoreType.DMA((DEPTH * N_GATHERED,))]
    shards = [win_shards, wout_shards]

    def body(win_sh, wout_sh, *rest):
        send_sems, recv_sems, local_sems, win_thru, wout_thru, *landing = rest[len(after):]
        for layer in range(DEPTH):
            own, remote = _first_copies(layer, (win_sh, wout_sh), landing[N_GATHERED * layer:N_GATHERED * (layer + 1)],
                                        send_sems, recv_sems, local_sems)
            for cp in own + remote:
                cp.start()

    outs = pl.pallas_call(
        body, name="all_gather_weights_start", in_specs=[HBM] * (2 + len(after)),
        out_specs=[_SEM] * 3 + [HBM] * (2 + DEPTH * N_GATHERED),
        out_shape=(sem_shapes + [pltpu.HBM(s.shape, s.dtype) for s in shards]
                   + [pltpu.HBM(s, BF16) for _ in range(DEPTH) for s in _GATHERED_SHAPES]),
        input_output_aliases={0: 3, 1: 4}, compiler_params=_params(has_side_effects=_DATAFLOW),
    )(*shards, *after)
    landing = outs[5:]
    return outs[:3], outs[3:5], [landing[N_GATHERED * l:N_GATHERED * (l + 1)] for l in range(DEPTH)]


def _passed_on_copies(full_refs, send_sems, recv_sems, core_of_block):
    x, y, c = _mesh_position()
    return [_gather_copy(_BLOCK_OF[a], full_refs[a], _block_id(*chip, core_of_block),
                         send_sems.at[a * N_OTHER_CHIPS + j], recv_sems.at[a * N_OTHER_CHIPS + j], (x, y, 1 - c))
            for a in range(N_GATHERED) for j, chip in enumerate(_other_chips(x, y))]


def _gather_weights_pass_on(layer, first_recv_sems, landing, after):
    n = N_GATHERED * N_OTHER_CHIPS

    def body(win_ref, wout_ref, first_recv, *rest):
        send_sems, recv_sems = rest[len(after):len(after) + 2]
        x, y, c = _mesh_position()
        full_refs = (win_ref, wout_ref)
        passed = _passed_on_copies(full_refs, send_sems, recv_sems, c)
        for a in range(N_GATHERED):
            for j, chip in enumerate(_other_chips(x, y)):
                sem = _first_sem(layer, a, 1 + j)
                _gather_copy(_BLOCK_OF[a], full_refs[a], _block_id(*chip, c), first_recv.at[sem], first_recv.at[sem],
                             (x, y, c)).wait_recv()
                passed[a * N_OTHER_CHIPS + j].start()

    outs = pl.pallas_call(
        body, name=f"all_gather_weights_pass_on_{layer}", in_specs=[HBM] * N_GATHERED + [_SEM] + [HBM] * len(after),
        out_specs=[_SEM] * 2 + [HBM] * N_GATHERED,
        out_shape=[pltpu.SemaphoreType.DMA((n,)), pltpu.SemaphoreType.DMA((n,))]
        + [pltpu.HBM(a.shape, a.dtype) for a in landing],
        input_output_aliases={a: 2 + a for a in range(N_GATHERED)},
        compiler_params=_params(has_side_effects=_DATAFLOW),
    )(*landing, first_recv_sems, *after)
    return outs[:2], outs[2:]


def _gather_weights_finish(layer, first_sems, passed_sems, shards, landing):
    def body(win_ref, wout_ref, first_send, first_recv, local_sems, passed_send, passed_recv, win_sh, wout_sh, *thru):
        x, y, c = _mesh_position()
        full_refs = (win_ref, wout_ref)
        own, remote = _first_copies(layer, (win_sh, wout_sh), full_refs, first_send, first_recv, local_sems)
        for a in range(N_GATHERED):
            sem = _first_sem(layer, a, 0)
            _gather_copy(_BLOCK_OF[a], full_refs[a], _block_id(x, y, 1 - c), first_recv.at[sem], first_recv.at[sem],
                         (x, y, c)).wait_recv()
        for cp in _passed_on_copies(full_refs, passed_send, passed_recv, 1 - c):
            cp.wait_recv()
        for cp in remote + _passed_on_copies(full_refs, passed_send, passed_recv, c):
            cp.wait_send()
        for cp in own:
            cp.wait()

    return pl.pallas_call(
        body, name=f"all_gather_weights_finish_{layer}", in_specs=[HBM] * N_GATHERED + [_SEM] * 5 + [HBM] * 2,
        out_specs=[HBM] * N_GATHERED, out_shape=[pltpu.HBM(a.shape, a.dtype) for a in landing],
        input_output_aliases={a: a for a in range(N_GATHERED)}, compiler_params=_params(has_side_effects=_DATAFLOW),
    )(*landing, *first_sems, *passed_sems, *shards)


W_IN_KIND = (0,)
W_OUT_KINDS = (1, 2)


def _merge_exchanges(exchanges):
    exchanges = [e for e in exchanges if e is not None]
    if not exchanges:
        return None
    inputs, out_shapes, sem_shapes, aliases = [], [], [], {}
    for e in exchanges:
        aliases.update({len(inputs) + i: len(out_shapes) + o for i, o in e.aliases.items()})
        inputs += e.inputs
        out_shapes += e.out_shapes
        sem_shapes += e.sem_shapes

    def make(in_refs, out_refs, sems):
        made, at = [], [0, 0, 0]
        for e in exchanges:
            n = (len(e.inputs), len(e.out_shapes), len(e.sem_shapes))
            made.append(e.make(in_refs[at[0]:at[0] + n[0]], out_refs[at[1]:at[1] + n[1]], sems[at[2]:at[2] + n[2]]))
            at = [a + b for a, b in zip(at, n)]

        def start():
            for s, _ in made:
                s()

        def finish():
            for _, f in made:
                f()

        return start, finish

    return _Exchange(inputs, out_shapes, aliases, sem_shapes, make)


def _sibling_exchange(grads, kinds):
    n_arr = len(grads)

    def make(in_refs, out_refs, sems):
        send_sems, recv_sems = sems
        x, y, c = _mesh_position()
        copies = [pltpu.make_async_remote_copy(
            src_ref=_BLOCK_OF[kinds[a]](in_refs[a], 2 * q + (1 - c)), dst_ref=out_refs[a].at[q],
            send_sem=send_sems.at[a, q], recv_sem=recv_sems.at[a, q], device_id=(x, y, 1 - c), device_id_type=MESH)
            for a in range(n_arr) for q in range(N_CHIP)]

        def start():
            for cp in copies:
                cp.start()

        def finish():
            for cp in copies:
                cp.wait_recv()
            for cp in copies:
                cp.wait_send()

        return start, finish

    return _Exchange(
        grads, [jax.ShapeDtypeStruct((N_CHIP,) + _BLOCK_SHAPES[k], BF16) for k in kinds], {},
        [pltpu.SemaphoreType.DMA((n_arr, N_CHIP)), pltpu.SemaphoreType.DMA((n_arr, N_CHIP))], make)


def _chips_exchange(layer, partials, received, kinds):
    n_arr = len(partials)

    def make(in_refs, out_refs, sems):
        send_sems, recv_sems = sems
        x, y, c = _mesh_position()
        copies = [pltpu.make_async_remote_copy(
            src_ref=in_refs[a].at[2 * qx + qy, layer], dst_ref=out_refs[a].at[j, layer],
            send_sem=send_sems.at[a * N_OTHER_CHIPS + j], recv_sem=recv_sems.at[a * N_OTHER_CHIPS + j],
            device_id=(qx, qy, c), device_id_type=MESH)
            for a in range(n_arr) for j, (qx, qy) in enumerate(_other_chips(x, y))]

        def start():
            for cp in copies:
                cp.start()

        def finish():
            for cp in copies:
                cp.wait_recv()
            for cp in copies:
                cp.wait_send()

        return start, finish

    inputs = list(partials)
    aliases = {}
    if received is not None:
        inputs += list(received)
        aliases = {n_arr + a: a for a in range(n_arr)}
    return _Exchange(
        inputs, [jax.ShapeDtypeStruct((N_OTHER_CHIPS, DEPTH) + _BLOCK_SHAPES[k], BF16) for k in kinds], aliases,
        [pltpu.SemaphoreType.DMA((n_arr * N_OTHER_CHIPS,)), pltpu.SemaphoreType.DMA((n_arr * N_OTHER_CHIPS,))], make)


def _host(exchange, args, in_specs, out_shape, out_specs, scratch):
    n_own = (len(args), len(out_shape), len(scratch))
    if exchange is None:
        return {}, lambda refs: (refs, None)
    n_ex = (len(exchange.inputs), len(exchange.out_shapes), len(exchange.sem_shapes))
    aliases = {n_own[0] + i: n_own[1] + o for i, o in exchange.aliases.items()}
    args += exchange.inputs
    in_specs += [HBM] * n_ex[0]
    out_shape += exchange.out_shapes
    out_specs += [HBM] * n_ex[1]
    scratch += exchange.sem_shapes

    def split(refs):
        own, theirs, at = [], [], 0
        for mine, ex in zip(n_own, n_ex):
            own += refs[at:at + mine]
            theirs.append(refs[at + mine:at + mine + ex])
            at += mine + ex
        return own, exchange.make(*theirs)

    return aliases, split


def _all_gather_small(v, name, exchange=None):
    vmem = pl.BlockSpec(memory_space=pltpu.VMEM)
    args, in_specs = [v], [vmem]
    out_shape, out_specs = [jax.ShapeDtypeStruct((N_DEV,) + v.shape, v.dtype)], [vmem]
    scratch = [pltpu.SemaphoreType.DMA((N_DEV - 1,)), pltpu.SemaphoreType.DMA((N_DEV - 1,))]
    aliases, split = _host(exchange, args, in_specs, out_shape, out_specs, scratch)

    def body(*refs):
        (v_ref, out_ref, send_sems, recv_sems), hosted = split(refs)
        if hosted is not None:
            hosted[0]()
        x, y, c = _mesh_position()
        me = _block_id(x, y, c)
        out_ref[me] = v_ref[...]
        sends = []
        for k in range(1, N_DEV):
            px, py, pc = x ^ ((k >> 2) & 1), y ^ ((k >> 1) & 1), c ^ (k & 1)
            send = pltpu.make_async_remote_copy(
                src_ref=v_ref, dst_ref=out_ref.at[me], send_sem=send_sems.at[k - 1], recv_sem=recv_sems.at[k - 1],
                device_id=(px, py, pc), device_id_type=MESH)
            send.start()
            sends.append((send, _block_id(px, py, pc)))
        for k, (send, peer) in enumerate(sends):
            pltpu.make_async_remote_copy(
                src_ref=v_ref, dst_ref=out_ref.at[peer], send_sem=send_sems.at[k], recv_sem=recv_sems.at[k],
                device_id=(x, y, c), device_id_type=MESH).wait_recv()
        for send, _ in sends:
            send.wait_send()
        if hosted is not None:
            hosted[1]()

    outs = pl.pallas_call(
        body, name=name, in_specs=in_specs, out_specs=out_specs, out_shape=out_shape, scratch_shapes=scratch,
        input_output_aliases=aliases, compiler_params=_params(),
    )(*args)
    return outs[0] if exchange is None else outs


def _forward_layer(layer, x, vec, cps, wpool, win, wout, exchange):
    t_len = x.shape[0]
    n_tiles = t_len // ROW_TILE
    row = lambda cols: pl.BlockSpec((ROW_TILE, cols), lambda i: (i, 0))
    args = [x, vec, cps, wpool, win, wout]
    in_specs = [row(D_MODEL), _layer_spec(vec.shape, layer), _layer_spec(cps.shape, layer),
                _layer_spec(wpool.shape, layer), _whole_spec(win.shape), _whole_spec(wout.shape)]
    out_shape = [jax.ShapeDtypeStruct((t_len, D_MODEL), F32), jax.ShapeDtypeStruct((t_len, IN_COLS), BF16),
                 jax.ShapeDtypeStruct((t_len, D_MODEL), BF16)]
    out_specs = [row(D_MODEL), row(IN_COLS), row(D_MODEL)]
    scratch = [pltpu.VMEM((CONV_HALO, CONV_W), F32), pltpu.VMEM((POOL_HALO, POOL_W), F32)]
    aliases, split = _host(exchange, args, in_specs, out_shape, out_specs, scratch)

    def body(*refs):
        (x_ref, vec_ref, cps_ref, wpool_ref, win_ref, wout_ref, xo_ref, proj_ref, y_ref, zc_ref, pc_ref), hosted = (
            split(refs))
        i = pl.program_id(0)

        @pl.when(i == 0)
        def _():
            if hosted is not None:
                hosted[0]()
            zc_ref[...] = jnp.zeros_like(zc_ref)
            pc_ref[...] = jnp.zeros_like(pc_ref)

        x_t = x_ref[...]
        shift, scale, gate = vec_ref[0:1, :], vec_ref[1:2, :], vec_ref[2:3, :]
        g_pre, g_post = vec_ref[3:4, :], vec_ref[4:5, :]
        rx = lax.rsqrt(jnp.mean(x_t * x_t, axis=-1, keepdims=True) + NORM_EPS)
        h = (x_t * rx) * g_pre * (1.0 + scale) + shift
        proj_b = _dot(h.astype(BF16), win_ref[...]).astype(BF16)
        proj_ref[...] = proj_b
        mx = _mixer_forward(proj_b.astype(F32), zc_ref[...], pc_ref[...], cps_ref[...], wpool_ref, i * ROW_TILE)
        zc_ref[...] = mx["z"][ROW_TILE - CONV_HALO:]
        pc_ref[...] = mx["u_p"][ROW_TILE - POOL_HALO:]
        y_b = _dot(mx["ycat"], wout_ref[...]).astype(BF16)
        y_ref[...] = y_b
        y_t = y_b.astype(F32)
        ry = lax.rsqrt(jnp.mean(y_t * y_t, axis=-1, keepdims=True) + NORM_EPS)
        xo_ref[...] = x_t + gate * (y_t * ry * g_post)

        if hosted is not None:
            pl.when(i == n_tiles - 1)(hosted[1])

    return pl.pallas_call(
        body, name=f"forward_layer_{layer}", grid=(n_tiles,), in_specs=in_specs, out_specs=out_specs,
        out_shape=out_shape, scratch_shapes=scratch, input_output_aliases=aliases,
        compiler_params=_params(dimension_semantics=("arbitrary",)),
    )(*args)


def _loss_head(x_final, target):
    t_len = x_final.shape[0]
    n_tiles = t_len // ROW_TILE

    def body(x_ref, t_ref, dx_ref, loss_ref):
        @pl.when(pl.program_id(0) == 0)
        def _():
            loss_ref[...] = jnp.zeros_like(loss_ref)

        err = x_ref[...] - t_ref[...]
        dx_ref[...] = err * (1.0 / D_MODEL)
        loss_ref[...] += jnp.sum(err * err) * (0.5 / D_MODEL)

    row = pl.BlockSpec((ROW_TILE, D_MODEL), lambda i: (i, 0))
    return pl.pallas_call(
        body, name="loss_head", grid=(n_tiles,), in_specs=[row, row],
        out_specs=[row, pl.BlockSpec((SUBLANES, LANES), lambda i: (0, 0))],
        out_shape=[jax.ShapeDtypeStruct((t_len, D_MODEL), F32), jax.ShapeDtypeStruct((SUBLANES, LANES), F32)],
        compiler_params=_params(dimension_semantics=("arbitrary",)),
    )(x_final, target)


def _backward_mixers(layer, dxo, y, proj, vec, cps, wpool, wout, exchange):
    t_len = dxo.shape[0]
    n_tiles = t_len // ROW_TILE
    halo_per_tile = ROW_TILE // POOL_HALO
    rev = lambda cols: pl.BlockSpec((ROW_TILE, cols), lambda i: (n_tiles - 1 - i, 0))
    halo_spec = pl.BlockSpec(
        (POOL_HALO, IN_COLS), lambda i: (jnp.maximum((n_tiles - 1 - i) * halo_per_tile - 1, 0), 0))
    gwpool_shape = (len(POOL_WINDOWS), GROUP_D, GROUP_D)
    args = [dxo, y, proj, proj, vec, cps, wpool, wout]
    in_specs = [rev(D_MODEL), rev(D_MODEL), rev(IN_COLS), halo_spec, _layer_spec(vec.shape, layer),
                _layer_spec(cps.shape, layer), _layer_spec(wpool.shape, layer), _whole_spec(wout.shape)]
    out_shape = [jax.ShapeDtypeStruct((t_len, IN_COLS), BF16), jax.ShapeDtypeStruct((D_MODEL, D_MODEL), BF16),
                 jax.ShapeDtypeStruct(gwpool_shape, BF16), jax.ShapeDtypeStruct((SUBLANES, CONV_W), F32),
                 jax.ShapeDtypeStruct((SUBLANES, D_MODEL), F32)]
    out_specs = [rev(IN_COLS), _whole_spec((D_MODEL, D_MODEL)), _whole_spec(gwpool_shape),
                 _whole_spec((SUBLANES, CONV_W)), _whole_spec((SUBLANES, D_MODEL))]
    scratch = [pltpu.VMEM((D_MODEL, D_MODEL), F32), pltpu.VMEM(gwpool_shape, F32),
               pltpu.VMEM((CONV_HALO, CONV_W), F32), pltpu.VMEM((POOL_HALO, POOL_W), F32)]
    aliases, split = _host(exchange, args, in_specs, out_shape, out_specs, scratch)

    def body(*refs):
        (dxo_ref, y_ref, proj_ref, projh_ref, vec_ref, cps_ref, wpool_ref, wout_ref, dproj_ref, gwout_ref, gwpool_ref,
         dcps_ref, dvec_ref, gwout_acc, gwpool_acc, dcc_ref, qc_ref), hosted = split(refs)
        i = pl.program_id(0)
        tile = n_tiles - 1 - i

        @pl.when(i == 0)
        def _():
            if hosted is not None:
                hosted[0]()
            gwout_acc[...] = jnp.zeros_like(gwout_acc)
            gwpool_acc[...] = jnp.zeros_like(gwpool_acc)
            dcps_ref[...] = jnp.zeros_like(dcps_ref)
            dvec_ref[...] = jnp.zeros_like(dvec_ref)
            dcc_ref[...] = jnp.zeros_like(dcc_ref)
            qc_ref[...] = jnp.zeros_like(qc_ref)

        gate, g_post = vec_ref[2:3, :], vec_ref[4:5, :]
        dxo_t = dxo_ref[...]
        y_t = y_ref[...].astype(F32)
        ry = lax.rsqrt(jnp.mean(y_t * y_t, axis=-1, keepdims=True) + NORM_EPS)
        yh = y_t * ry
        dxy = dxo_t * yh
        dvec_ref[0:1, :] += jnp.sum(dxy * g_post, axis=0, keepdims=True)
        dvec_ref[1:2, :] += jnp.sum(dxy * gate, axis=0, keepdims=True)
        dyh = dxo_t * (gate * g_post)
        dy_b = (ry * (dyh - yh * jnp.mean(dyh * yh, axis=-1, keepdims=True))).astype(BF16)

        halo = jnp.where(tile > 0, projh_ref[...].astype(F32), 0.0)
        hu_a, _, hc_a, _, hu_p, _ = _split_proj(halo)
        z_halo = (hc_a * hu_a)[POOL_HALO - CONV_HALO:]
        mx = _mixer_forward(proj_ref[...].astype(F32), z_halo, hu_p, cps_ref[...], wpool_ref, tile * ROW_TILE)

        gwout_acc[...] += _dot_tn(mx["ycat"], dy_b)
        dycat = _dot_nt(dy_b, wout_ref[...])
        dy_a, dy_p = dycat[:, :CONV_W], dycat[:, CONV_W:]

        t_a = dy_a * mx["silu_a"]
        db_a = t_a * mx["conv"]
        dconv = t_a * mx["b_a"]
        dg_a = dy_a * mx["b_a"] * mx["conv"] * (mx["sig_a"] * (1.0 + mx["g_a"] * (1.0 - mx["sig_a"])))
        dccat = jnp.concatenate([dconv, dcc_ref[...]], axis=0)
        dc1 = _rows_from_after(dccat, 1)[:ROW_TILE]
        dc2 = _rows_from_after(dccat, 2)[:ROW_TILE]
        dz = mx["w2"] * dconv + mx["w1"] * dc1 + mx["w0"] * dc2
        dcc_ref[...] = dconv[:CONV_HALO]
        dcps_ref[0:1, :] += jnp.sum(dconv * mx["z2"], axis=0, keepdims=True)
        dcps_ref[1:2, :] += jnp.sum(dconv * mx["z1"], axis=0, keepdims=True)
        dcps_ref[2:3, :] += jnp.sum(dconv * mx["z"], axis=0, keepdims=True)
        du_a = dz * mx["c_a"]
        dc_a = dz * mx["u_a"]

        t_p = dy_p * mx["silu_p"]
        dcps_ref[3:4, :] += jnp.sum(t_p * mx["mixed"], axis=0, keepdims=True)
        dmixed = (t_p * mx["ps"]).astype(BF16)
        dg_p = dy_p * mx["mixed"] * mx["ps"] * (mx["sig_p"] * (1.0 + mx["g_p"] * (1.0 - mx["sig_p"])))
        du_p, q_head = [], []
        for g, w in enumerate(POOL_WINDOWS):
            cols = slice(g * GROUP_D, (g + 1) * GROUP_D)
            dm_g = dmixed[:, cols]
            dpooled_g = _dot_nt(dm_g, wpool_ref[g])
            gwpool_acc[g] += _dot_tn(mx["pooled"][g].astype(BF16), dm_g)
            q_g = dpooled_g * mx["inv"][g]
            q_head.append(q_g[:POOL_HALO])
            s = jnp.concatenate([q_g, qc_ref[:, cols]], axis=0)
            step = 1
            while step < w:
                s = s + _rows_from_after(s, step)
                step *= 2
            du_p.append(s[:ROW_TILE] - dpooled_g)
        qc_ref[...] = jnp.concatenate(q_head, axis=1)
        dproj_ref[...] = jnp.concatenate([du_a, db_a, dc_a, dg_a] + du_p + [dg_p], axis=1).astype(BF16)

        @pl.when(i == n_tiles - 1)
        def _():
            gwout_ref[...] = gwout_acc[...].astype(BF16)
            gwpool_ref[...] = gwpool_acc[...].astype(BF16)
            if hosted is not None:
                hosted[1]()

    return pl.pallas_call(
        body, name=f"backward_mixers_{layer}", grid=(n_tiles,), in_specs=in_specs, out_specs=out_specs,
        out_shape=out_shape, scratch_shapes=scratch, input_output_aliases=aliases,
        compiler_params=_params(dimension_semantics=("arbitrary",)),
    )(*args)


def _backward_input_proj(layer, x, dproj, dxo, vec, win, exchange):
    t_len = x.shape[0]
    n_tiles = t_len // ROW_TILE
    row = lambda cols: pl.BlockSpec((ROW_TILE, cols), lambda i: (i, 0))
    args = [x, dproj, dxo, vec, win]
    in_specs = [row(D_MODEL), row(IN_COLS), row(D_MODEL), _layer_spec(vec.shape, layer), _whole_spec(win.shape)]
    out_shape = [jax.ShapeDtypeStruct((t_len, D_MODEL), F32), jax.ShapeDtypeStruct((D_MODEL, IN_COLS), BF16),
                 jax.ShapeDtypeStruct((SUBLANES, D_MODEL), F32)]
    out_specs = [row(D_MODEL), _whole_spec((D_MODEL, IN_COLS)), _whole_spec((SUBLANES, D_MODEL))]
    scratch = [pltpu.VMEM((D_MODEL, IN_COLS), F32)]
    aliases, split = _host(exchange, args, in_specs, out_shape, out_specs, scratch)

    def body(*refs):
        (x_ref, dproj_ref, dxo_ref, vec_ref, win_ref, dx_ref, gwin_ref, dvec_ref, gwin_acc), hosted = split(refs)
        i = pl.program_id(0)

        @pl.when(i == 0)
        def _():
            if hosted is not None:
                hosted[0]()
            gwin_acc[...] = jnp.zeros_like(gwin_acc)
            dvec_ref[...] = jnp.zeros_like(dvec_ref)

        shift, scale, g_pre = vec_ref[0:1, :], vec_ref[1:2, :], vec_ref[3:4, :]
        x_t = x_ref[...]
        rx = lax.rsqrt(jnp.mean(x_t * x_t, axis=-1, keepdims=True) + NORM_EPS)
        xn = x_t * rx
        mod_scale = 1.0 + scale
        h_b = (xn * g_pre * mod_scale + shift).astype(BF16)
        dproj_t = dproj_ref[...]
        gwin_acc[...] += _dot_tn(h_b, dproj_t)
        dh = _dot_nt(dproj_t, win_ref[...])
        dvec_ref[0:1, :] += jnp.sum(dh, axis=0, keepdims=True)
        dhx = dh * xn
        dvec_ref[1:2, :] += jnp.sum(dhx * g_pre, axis=0, keepdims=True)
        dvec_ref[2:3, :] += jnp.sum(dhx * mod_scale, axis=0, keepdims=True)
        dxn = dh * (g_pre * mod_scale)
        dx_ref[...] = dxo_ref[...] + rx * (dxn - xn * jnp.mean(dxn * xn, axis=-1, keepdims=True))

        @pl.when(i == n_tiles - 1)
        def _():
            gwin_ref[...] = gwin_acc[...].astype(BF16)
            if hosted is not None:
                hosted[1]()

    return pl.pallas_call(
        body, name=f"backward_input_proj_{layer}", grid=(n_tiles,), in_specs=in_specs, out_specs=out_specs,
        out_shape=out_shape, scratch_shapes=scratch, input_output_aliases=aliases,
        compiler_params=_params(dimension_semantics=("arbitrary",)),
    )(*args)


def _add_sibling_blocks(name, layer, grads, received, core, partials, kinds):
    n_arr = len(grads)

    def body(core_ref, *refs):
        mine, theirs, outs = refs[:n_arr], refs[n_arr:2 * n_arr], refs[-n_arr:]
        for a in range(n_arr):
            outs[a][...] = (mine[a][...].astype(F32) + theirs[a][...].astype(F32)).astype(BF16)

    own_of_kind = [
        pl.BlockSpec((D_MODEL, W_IN_SHARD), lambda q, core_ref: (0, 2 * q + core_ref[0])),
        pl.BlockSpec((W_OUT_SHARD, D_MODEL), lambda q, core_ref: (2 * q + core_ref[0], 0)),
        pl.BlockSpec((POOL_SHARD, GROUP_D), lambda q, core_ref: (2 * q + core_ref[0], 0)),
    ]
    shapes = [_BLOCK_SHAPES[k] for k in kinds]
    recv_specs = [pl.BlockSpec((None,) + s, lambda q, core_ref: (q, 0, 0)) for s in shapes]
    out_specs = [pl.BlockSpec((None, None) + s, lambda q, core_ref: (q, layer, 0, 0)) for s in shapes]
    args = [core, *grads, *received]
    in_specs = [own_of_kind[k] for k in kinds] + recv_specs
    aliases = {}
    if partials is not None:
        aliases = {len(args) + a: a for a in range(n_arr)}
        args += list(partials)
        in_specs += [HBM] * n_arr
    return pl.pallas_call(
        body, name=name,
        grid_spec=pltpu.PrefetchScalarGridSpec(
            num_scalar_prefetch=1, grid=(N_CHIP,), in_specs=in_specs, out_specs=out_specs),
        out_shape=[jax.ShapeDtypeStruct((N_CHIP, DEPTH) + s, BF16) for s in shapes],
        input_output_aliases=aliases,
        compiler_params=_params(dimension_semantics=("arbitrary",)),
    )(*args)


def _modulation_columns(c_all, w_ada):
    def body(c_ref, w_ref, cact_ref, out_ref):
        c_t = c_ref[...]
        c_act = c_t * _sigmoid(c_t)
        cact_ref[...] = c_act
        out_ref[...] = jnp.dot(c_act, w_ref[...], preferred_element_type=F32, precision=lax.Precision.HIGHEST)

    return pl.pallas_call(
        body, name="modulation_columns", grid=(DEPTH,),
        in_specs=[pl.BlockSpec((N_DEV, D_MODEL), lambda l: (0, 0)),
                  pl.BlockSpec((None, D_MODEL, W_IN_SHARD), lambda l: (l, 0, 0))],
        out_specs=[pl.BlockSpec((N_DEV, D_MODEL), lambda l: (0, 0)),
                   pl.BlockSpec((N_DEV, W_IN_SHARD), lambda l: (0, l))],
        out_shape=[jax.ShapeDtypeStruct((N_DEV, D_MODEL), F32),
                   jax.ShapeDtypeStruct((N_DEV, DEPTH * W_IN_SHARD), F32)],
        compiler_params=_params(dimension_semantics=("arbitrary",)),
    )(c_all, w_ada)


def _adamw(w, g, m, v):
    m_new = ADAM_B1 * m + (1.0 - ADAM_B1) * g
    v_new = ADAM_B2 * v + (1.0 - ADAM_B2) * (g * g)
    m_hat = m_new / (1.0 - ADAM_B1 ** ADAM_STEP)
    v_hat = v_new / (1.0 - ADAM_B2 ** ADAM_STEP)
    delta = -ADAM_LR * (m_hat / (jnp.sqrt(v_hat) + ADAM_EPS) + ADAM_WD * w)
    return delta, m_new, v_new


def _adamw_w_ada(w, m, v, c_act_t, dmod_cols, exchange):
    big = pl.BlockSpec((None, D_MODEL, W_IN_SHARD), lambda l: (l, 0, 0))
    args = [w, m, v, c_act_t, dmod_cols]
    in_specs = [big, big, big, pl.BlockSpec((D_MODEL, N_DEV), lambda l: (0, 0)),
                pl.BlockSpec((None, N_DEV, W_IN_SHARD), lambda l: (l, 0, 0))]
    out_shape, out_specs, scratch = [jax.ShapeDtypeStruct(w.shape, F32)] * 4, [big] * 4, []
    aliases, split = _host(exchange, args, in_specs, out_shape, out_specs, scratch)

    def body(*refs):
        (w_ref, m_ref, v_ref, ct_ref, dm_ref, g_ref, d_ref, mo_ref, vo_ref), hosted = split(refs)
        if hosted is not None:
            pl.when(pl.program_id(0) == 0)(hosted[0])
        g = ct_ref[:, 0:1] * dm_ref[0:1, :]
        for b in range(1, N_DEV):
            g = g + ct_ref[:, b:b + 1] * dm_ref[b:b + 1, :]
        g_ref[...] = g
        d_ref[...], mo_ref[...], vo_ref[...] = _adamw(w_ref[...], g, m_ref[...], v_ref[...])
        if hosted is not None:
            pl.when(pl.program_id(0) == DEPTH - 1)(hosted[1])

    return pl.pallas_call(
        body, name="adamw_w_ada", grid=(DEPTH,), in_specs=in_specs, out_specs=out_specs, out_shape=out_shape,
        scratch_shapes=scratch, input_output_aliases=aliases,
        compiler_params=_params(dimension_semantics=("arbitrary",)),
    )(*args)


def _sum_chip_partials(own_ref, recv_ref):
    g = own_ref[...].astype(F32)
    for j in range(N_OTHER_CHIPS):
        g = g + recv_ref[j].astype(F32)
    return g


def _partial_specs(row_tile, cols):
    own = pl.BlockSpec((None, None, row_tile, cols), lambda l, r, chip_ref: (chip_ref[0], l, r, 0))
    recv = pl.BlockSpec((N_OTHER_CHIPS, None, row_tile, cols), lambda l, r, chip_ref: (0, l, r, 0))
    return own, recv


def _adamw_reduced(name, w, m, v, partial, received, chip, row_tile):
    depth, rows, cols = w.shape

    def body(chip_ref, w_ref, m_ref, v_ref, own_ref, recv_ref, g_ref, d_ref, mo_ref, vo_ref):
        g = _sum_chip_partials(own_ref, recv_ref)
        g_ref[...] = g
        d_ref[...], mo_ref[...], vo_ref[...] = _adamw(w_ref[...], g, m_ref[...], v_ref[...])

    blk = pl.BlockSpec((None, row_tile, cols), lambda l, r, chip_ref: (l, r, 0))
    return pl.pallas_call(
        body, name=name,
        grid_spec=pltpu.PrefetchScalarGridSpec(
            num_scalar_prefetch=1, grid=(depth, rows // row_tile),
            in_specs=[blk, blk, blk, *_partial_specs(row_tile, cols)], out_specs=[blk] * 4),
        out_shape=[jax.ShapeDtypeStruct(w.shape, F32)] * 4,
        compiler_params=_params(dimension_semantics=("arbitrary", "arbitrary")),
    )(chip, w, m, v, partial, received)


def _reduce_w_pool(partial, received, chip):
    def body(chip_ref, own_ref, recv_ref, g_ref):
        g_ref[...] = _sum_chip_partials(own_ref, recv_ref)

    return pl.pallas_call(
        body, name="reduce_w_pool",
        grid_spec=pltpu.PrefetchScalarGridSpec(
            num_scalar_prefetch=1, grid=(DEPTH, 1), in_specs=list(_partial_specs(POOL_SHARD, GROUP_D)),
            out_specs=pl.BlockSpec((POOL_SHARD, GROUP_D), lambda l, r, chip_ref: (l, 0))),
        out_shape=jax.ShapeDtypeStruct((DEPTH * POOL_SHARD, GROUP_D), F32),
        compiler_params=_params(dimension_semantics=("arbitrary", "arbitrary")),
    )(chip, partial, received)


def _adamw_small(params):
    n = len(params)

    def body(*refs):
        ins, outs = refs[:4 * n], refs[4 * n:]
        for p in range(n):
            w_ref, g_ref, m_ref, v_ref = ins[4 * p:4 * p + 4]
            d_ref, mo_ref, vo_ref = outs[3 * p:3 * p + 3]
            d_ref[...], mo_ref[...], vo_ref[...] = _adamw(w_ref[...], g_ref[...], m_ref[...], v_ref[...])

    vmem = pl.BlockSpec(memory_space=pltpu.VMEM)
    flat = [a for group in params for a in group]
    out_shape = [jax.ShapeDtypeStruct(group[0].shape, F32) for group in params for _ in range(3)]
    outs = pl.pallas_call(
        body, name="adamw_small", in_specs=[vmem] * len(flat), out_specs=[vmem] * len(out_shape),
        out_shape=out_shape, compiler_params=_params(),
    )(*flat)
    return [tuple(outs[3 * p:3 * p + 3]) for p in range(n)]


def _sum_sources(slabs):
    def body(s_ref, o_ref):
        acc = s_ref[0]
        for b in range(1, N_DEV):
            acc = acc + s_ref[b]
        o_ref[...] = acc

    vmem = pl.BlockSpec(memory_space=pltpu.VMEM)
    return pl.pallas_call(
        body, name="sum_small_grads", in_specs=[vmem], out_specs=vmem,
        out_shape=jax.ShapeDtypeStruct(slabs.shape[1:], F32), compiler_params=_params(),
    )(slabs)


def _to_bf16(a, name):
    def body(a_ref, o_ref):
        o_ref[...] = a_ref[...].astype(BF16)

    spec = pl.BlockSpec((None,) + a.shape[1:], lambda l: (l, 0, 0))
    return pl.pallas_call(
        body, name=name, grid=(a.shape[0],), in_specs=[spec], out_specs=spec,
        out_shape=jax.ShapeDtypeStruct(a.shape, BF16), compiler_params=_params(dimension_semantics=("arbitrary",)),
    )(a)


def kernel(x, c, w_ada, b_ada, g_pre, w_in, w_conv, w_pool, pool_scale, w_out, g_post, loss_target, m_w_ada, m_b_ada, m_g_pre, m_w_in, m_w_conv, m_w_pool, m_pool_scale, m_w_out, m_g_post, v_w_ada, v_b_ada, v_g_pre, v_w_in, v_w_conv, v_w_pool, v_pool_scale, v_w_out, v_g_post):
    mx, my, mc = _mesh_position()
    me = _block_id(mx, my, mc)
    chip = (2 * mx + my).astype(jnp.int32).reshape(1)
    core = mc.astype(jnp.int32).reshape(1)
    x0 = x[0]
    target = loss_target[0]
    conv_shard = w_conv.shape[-1]

    own_small = jnp.concatenate([c, w_conv.reshape(1, DEPTH * 3 * conv_shard)], axis=1)
    all_small = _all_gather_small(own_small, "all_gather_c_w_conv")[:, 0, :]
    c_all = all_small[:, :D_MODEL]
    w_conv_full = all_small[:, D_MODEL:].reshape(N_DEV, DEPTH, 3, conv_shard).transpose(1, 2, 0, 3).reshape(
        DEPTH, 3, CONV_W)
    cps = jnp.concatenate([w_conv_full, pool_scale[:, None], jnp.zeros((DEPTH, 4, CONV_W), F32)], axis=1)

    c_act, pieces = _modulation_columns(c_all, w_ada)
    mod_all = _all_gather_small(pieces, "all_gather_modulation")
    mod_mine = lax.dynamic_index_in_dim(mod_all, me, axis=1, keepdims=False)
    mod = mod_mine.reshape(N_DEV, DEPTH, W_IN_SHARD).transpose(1, 0, 2).reshape(DEPTH, 3 * D_MODEL) + b_ada
    zeros_d = jnp.zeros((DEPTH, 3, D_MODEL), F32)
    vec = jnp.concatenate([mod.reshape(DEPTH, 3, D_MODEL), g_pre[:, None], g_post[:, None], zeros_d], axis=1)

    first_sems, shards, landing = _gather_weights_start(
        _to_bf16(w_in, "cast_w_in"), _to_bf16(w_out, "cast_w_out"), [mod_all, all_small])
    wpool_b = _to_bf16(w_pool.reshape(DEPTH, POOL_ROWS, GROUP_D), "cast_w_pool").reshape(w_pool.shape)

    xs, projs, ys, wins, wouts = [x0], [], [], [], []
    for l in range(DEPTH):
        passed_sems, zones = _gather_weights_pass_on(
            l, first_sems[1], landing[l], [vec, cps, wpool_b] if l == 0 else [xs[-1]])
        win, wout = _gather_weights_finish(l, first_sems, passed_sems, shards, zones)
        x_next, proj, y = _forward_layer(l, xs[-1], vec, cps, wpool_b, win, wout, None)
        xs.append(x_next)
        projs.append(proj)
        ys.append(y)
        wins.append(win)
        wouts.append(wout)
    dx, loss_tile = _loss_head(xs[DEPTH], target)

    slab_rows = [None] * DEPTH
    gwin_above = part_w = recv_w = part_o = recv_o = None
    for l in reversed(range(DEPTH)):
        hosted = None
        if gwin_above is not None:
            hosted = _merge_exchanges([_sibling_exchange([gwin_above], W_IN_KIND),
                                       _chips_exchange(l + 1, part_o, recv_o, W_OUT_KINDS)])
        dproj, gwout, gwpool, dcps, dvec1, *res = _backward_mixers(
            l, dx, ys[l], projs[l], vec, cps, wpool_b, wouts[l], hosted)
        chips_w = None
        if gwin_above is not None:
            recv_o = res[1:]
            part_w = _add_sibling_blocks(
                f"grad_add_w_in_{l + 1}", l + 1, [gwin_above], res[:1], core, part_w, W_IN_KIND)
            chips_w = _chips_exchange(l + 1, part_w, recv_w, W_IN_KIND)
        grads_o = [gwout, gwpool.reshape(POOL_ROWS, GROUP_D)]
        hosted = _merge_exchanges([chips_w, _sibling_exchange(grads_o, W_OUT_KINDS)])
        dx, gwin_above, dvec2, *res = _backward_input_proj(l, xs[l], dproj, dx, vec, wins[l], hosted)
        if chips_w is not None:
            recv_w, res = res[:1], res[1:]
        part_o = _add_sibling_blocks(f"grad_add_w_out_{l}", l, grads_o, res, core, part_o, W_OUT_KINDS)
        slab_rows[l] = jnp.concatenate(
            [dvec2[0], dvec2[1], dvec1[0], dvec2[2], dvec1[1], dcps[3], dcps[0], dcps[1], dcps[2]])
    grad_x = dx[None]

    loss_row = jnp.pad(loss_tile[0], (0, SLAB_COLS - LANES))
    slab = jnp.stack(slab_rows + [loss_row] + [jnp.zeros((SLAB_COLS,), F32)] * (SLAB_ROWS - DEPTH - 1))
    hosted = _merge_exchanges([_sibling_exchange([gwin_above], W_IN_KIND),
                               _chips_exchange(0, part_o, recv_o, W_OUT_KINDS)])
    slabs, *res = _all_gather_small(slab, "all_gather_small_grads", hosted)
    recv_o = res[1:]
    part_w = _add_sibling_blocks("grad_add_w_in_0", 0, [gwin_above], res[:1], core, part_w, W_IN_KIND)
    total = _sum_sources(slabs)
    loss = total[DEPTH, 0]
    o = 3 * D_MODEL
    g_b_ada = total[:DEPTH, :o]
    g_g_pre = total[:DEPTH, o:o + D_MODEL]
    g_g_post = total[:DEPTH, o + D_MODEL:o + 2 * D_MODEL]
    g_pool_scale = total[:DEPTH, o + 2 * D_MODEL:o + 2 * D_MODEL + POOL_W]
    g_conv_full = total[:DEPTH, o + 2 * D_MODEL + POOL_W:].reshape(DEPTH, 3, CONV_W)
    g_w_conv = lax.dynamic_slice_in_dim(g_conv_full, me * conv_shard, conv_shard, axis=2)

    chips_w = _chips_exchange(0, part_w, recv_w, W_IN_KIND)
    sems, in_flight, landing, token = _start_exchange(chips_w, "grad_chips_w_in_0_start")

    dmod_all = slabs[:, :DEPTH, :o].reshape(N_DEV, DEPTH, N_DEV, W_IN_SHARD)
    dmod_cols = lax.dynamic_index_in_dim(dmod_all, me, axis=2, keepdims=False).transpose(1, 0, 2) + token[0, 0]
    g_w_ada, d_w_ada, nm_w_ada, nv_w_ada = _adamw_w_ada(w_ada, m_w_ada, v_w_ada, c_act.T, dmod_cols, None)

    g_w_out, d_w_out, nm_w_out, nv_w_out = _adamw_reduced(
        "adamw_w_out", w_out, m_w_out, v_w_out, part_o[0], recv_o[0], chip, W_OUT_SHARD)
    g_pool_all = _all_gather_small(_reduce_w_pool(part_o[1], recv_o[1], chip), "all_gather_grad_w_pool")
    g_w_pool = g_pool_all.reshape(N_DEV, DEPTH, POOL_SHARD, GROUP_D).transpose(1, 0, 2, 3).reshape(w_pool.shape)
    part_w, recv_w = _finish_exchange(
        chips_w, "grad_chips_w_in_0_finish", sems, in_flight, landing, [nv_w_ada, nv_w_out, g_pool_all])
    g_w_in, d_w_in, nm_w_in, nv_w_in = _adamw_reduced(
        "adamw_w_in", w_in, m_w_in, v_w_in, part_w[0], recv_w[0], chip, ROW_TILE)

    flat2 = lambda a: a.reshape(-1, a.shape[-1])
    small = _adamw_small([
        (b_ada, g_b_ada, m_b_ada, v_b_ada),
        (g_pre, g_g_pre, m_g_pre, v_g_pre),
        (flat2(w_conv), flat2(g_w_conv), flat2(m_w_conv), flat2(v_w_conv)),
        (flat2(w_pool), flat2(g_w_pool), flat2(m_w_pool), flat2(v_w_pool)),
        (pool_scale, g_pool_scale, m_pool_scale, v_pool_scale),
        (g_post, g_g_post, m_g_post, v_g_post),
    ])
    (d_b_ada, nm_b_ada, nv_b_ada), (d_g_pre, nm_g_pre, nv_g_pre), conv_upd, pool_upd, \
        (d_ps, nm_ps, nv_ps), (d_g_post, nm_g_post, nv_g_post) = small
    d_w_conv, nm_w_conv, nv_w_conv = (a.reshape(w_conv.shape) for a in conv_upd)
    d_w_pool, nm_w_pool, nv_w_pool = (a.reshape(w_pool.shape) for a in pool_upd)

    return (loss, grad_x,
            g_w_ada, g_b_ada, g_g_pre, g_w_in, g_w_conv, g_w_pool, g_pool_scale, g_w_out, g_g_post,
            d_w_ada, d_b_ada, d_g_pre, d_w_in, d_w_conv, d_w_pool, d_ps, d_w_out, d_g_post,
            nm_w_ada, nm_b_ada, nm_g_pre, nm_w_in, nm_w_conv, nm_w_pool, nm_ps, nm_w_out, nm_g_post,
            nv_w_ada, nv_b_ada, nv_g_pre, nv_w_in, nv_w_conv, nv_w_pool, nv_ps, nv_w_out, nv_g_post)
```

```python
import jax
import jax.numpy as jnp
from jax import lax
from jax.experimental import pallas as pl
from jax.experimental.pallas import tpu as pltpu

F32 = jnp.float32
BF16 = jnp.bfloat16

D_MODEL = 1024
DEPTH = 4
CONV_W = 512
POOL_W = 512
POOL_WINDOWS = (2, 4, 8, 16)
GROUP_D = 128
IN_COLS = 4 * CONV_W + 2 * POOL_W
NORM_EPS = 1e-6

ADAM_LR = 0.001
ADAM_B1 = 0.9
ADAM_B2 = 0.999
ADAM_EPS = 1e-08
ADAM_WD = 0.01
ADAM_STEP = 10

N_DEV = 8
N_CHIP = 4
N_OTHER_CHIPS = N_CHIP - 1
MESH = pl.DeviceIdType.MESH
W_IN_SHARD = IN_COLS // N_DEV
W_OUT_SHARD = D_MODEL // N_DEV
POOL_ROWS = len(POOL_WINDOWS) * GROUP_D
POOL_SHARD = POOL_ROWS // N_DEV

SUBLANES = 8
LANES = 128
VMEM_LIMIT_BYTES = 56 * 1024 * 1024
ROW_TILE = 512
BWD_TILE = 256
POOL_HALO = 16
CONV_HALO = SUBLANES

SLAB_COLS = 3 * D_MODEL + D_MODEL + D_MODEL + POOL_W + 3 * CONV_W
SLAB_ROWS = SUBLANES

HBM = pl.BlockSpec(memory_space=pl.ANY)


def _params(**kw):
    return pltpu.CompilerParams(vmem_limit_bytes=VMEM_LIMIT_BYTES, **kw)


def _sigmoid(v):
    return 1.0 / (1.0 + jnp.exp(-v))


def _dot(a, b):
    return jnp.dot(a, b, preferred_element_type=F32)


def _dot_tn(a, b):
    return lax.dot_general(a, b, (((0,), (0,)), ((), ())), preferred_element_type=F32)


def _dot_nt(a, b):
    return lax.dot_general(a, b, (((1,), (1,)), ((), ())), preferred_element_type=F32)


def _rows_from_before(v, k):
    return pltpu.roll(v, k, 0)


def _rows_from_after(v, k):
    return pltpu.roll(v, v.shape[0] - k, 0)


def _window_counts(t0, rows):
    return (lax.broadcasted_iota(jnp.int32, (rows, 1), 0) + (t0 + 1)).astype(F32)


def _split_proj(p32):
    cw = CONV_W
    return (p32[:, 0 * cw:1 * cw], p32[:, 1 * cw:2 * cw], p32[:, 2 * cw:3 * cw], p32[:, 3 * cw:4 * cw],
            p32[:, 4 * cw:4 * cw + POOL_W], p32[:, 4 * cw + POOL_W:])


def _mixer_forward(p32, z_halo, up_halo, cps, wpool_ref, t0):
    tm = p32.shape[0]
    u_a, b_a, c_a, g_a, u_p, g_p = _split_proj(p32)
    w0, w1, w2, ps = cps[0:1, :], cps[1:2, :], cps[2:3, :], cps[3:4, :]
    z = c_a * u_a
    zcat = jnp.concatenate([z_halo, z], axis=0)
    z1 = _rows_from_before(zcat, 1)[CONV_HALO:]
    z2 = _rows_from_before(zcat, 2)[CONV_HALO:]
    conv = w0 * z2 + w1 * z1 + w2 * z
    sig_a = _sigmoid(g_a)
    silu_a = g_a * sig_a
    y_a = b_a * conv * silu_a

    pcat = jnp.concatenate([up_halo, u_p], axis=0)
    counts = _window_counts(t0, tm)
    pooled, mixed, inv = [], [], []
    for g, w in enumerate(POOL_WINDOWS):
        cols = slice(g * GROUP_D, (g + 1) * GROUP_D)
        s = pcat[:, cols]
        step = 1
        while step < w:
            s = s + _rows_from_before(s, step)
            step *= 2
        inv_g = 1.0 / jnp.minimum(counts, float(w))
        pooled_g = s[POOL_HALO:] * inv_g - u_p[:, cols]
        pooled.append(pooled_g)
        inv.append(inv_g)
        mixed.append(_dot(pooled_g.astype(BF16), wpool_ref[g]))
    mixed = jnp.concatenate(mixed, axis=1)
    sig_p = _sigmoid(g_p)
    silu_p = g_p * sig_p
    y_p = mixed * ps * silu_p
    ycat = jnp.concatenate([y_a, y_p], axis=1)
    return dict(u_a=u_a, b_a=b_a, c_a=c_a, g_a=g_a, u_p=u_p, g_p=g_p, z=z, z1=z1, z2=z2, conv=conv, sig_a=sig_a,
                silu_a=silu_a, pooled=pooled, inv=inv, mixed=mixed, sig_p=sig_p, silu_p=silu_p, ycat=ycat,
                w0=w0, w1=w1, w2=w2, ps=ps)


def _layer_spec(shape, layer):
    nd = len(shape)
    return pl.BlockSpec((None,) + tuple(shape[1:]), lambda i, _l=layer, _n=nd: (_l,) + (0,) * (_n - 1))


def _whole_spec(shape):
    return pl.BlockSpec(tuple(shape), lambda i, _n=len(shape): (0,) * _n)


def _mesh_position():
    return lax.axis_index("x"), lax.axis_index("y"), lax.axis_index("c")


def _block_id(x, y, c):
    return 4 * x + 2 * y + c


def _other_chips(x, y):
    return [(x ^ 1, y), (x, y ^ 1), (x ^ 1, y ^ 1)]


def _col_block(ref, blk):
    return ref.at[:, pl.ds(pl.multiple_of(blk * W_IN_SHARD, LANES), W_IN_SHARD)]


def _row_block(rows):
    def block(ref, blk):
        return ref.at[pl.ds(pl.multiple_of(blk * rows, rows), rows), :]
    return block


_BLOCK_OF = (_col_block, _row_block(W_OUT_SHARD), _row_block(POOL_SHARD))
_BLOCK_SHAPES = ((D_MODEL, W_IN_SHARD), (W_OUT_SHARD, D_MODEL), (POOL_SHARD, GROUP_D))


class _Exchange:
    def __init__(self, inputs, out_shapes, aliases, sem_shapes, make):
        self.inputs, self.out_shapes, self.aliases, self.sem_shapes, self.make = (
            list(inputs), list(out_shapes), dict(aliases), list(sem_shapes), make)


_SEM =pl.BlockSpec(memory_space=pltpu.SEMAPHORE)
_DATAFLOW = pltpu.SideEffectType.DATAFLOW_SIDE_EFFECTING


def _start_exchange(exchange, name):
    n_in, n_out, n_sem = len(exchange.inputs), len(exchange.out_shapes), len(exchange.sem_shapes)
    sources = [i for i in range(n_in) if i not in exchange.aliases]
    aliases = {i: n_sem + k for k, i in enumerate(sources)}
    aliases.update({i: n_sem + len(sources) + o for i, o in exchange.aliases.items()})

    def body(*refs):
        in_refs = refs[:n_in]
        sems = refs[n_in:n_in + n_sem]
        out_refs = refs[n_in + n_sem + len(sources):n_in + n_sem + len(sources) + n_out]
        exchange.make(in_refs, out_refs, sems)[0]()
        refs[-1][...] = jnp.zeros_like(refs[-1])

    outs = pl.pallas_call(
        body, name=name, in_specs=[HBM] * n_in,
        out_specs=[_SEM] * n_sem + [HBM] * (len(sources) + n_out) + [pl.BlockSpec(memory_space=pltpu.VMEM)],
        out_shape=(exchange.sem_shapes + [pltpu.HBM(exchange.inputs[i].shape, exchange.inputs[i].dtype) for i in sources]
                   + [pltpu.HBM(s.shape, s.dtype) for s in exchange.out_shapes]
                   + [jax.ShapeDtypeStruct((SUBLANES, LANES), F32)]),
        input_output_aliases=aliases, compiler_params=_params(has_side_effects=_DATAFLOW),
    )(*exchange.inputs)
    return outs[:n_sem], outs[n_sem:n_sem + len(sources)], outs[n_sem + len(sources):-1], outs[-1]


def _finish_exchange(exchange, name, sems, sources, landing, after):
    n_src, n_out, n_sem = len(sources), len(landing), len(sems)
    n_in = len(exchange.inputs)
    source_at = [i for i in range(n_in) if i not in exchange.aliases]

    def body(*refs):
        src_refs, out_refs = refs[:n_src], refs[n_src:n_src + n_out]
        sem_refs = refs[n_src + n_out:n_src + n_out + n_sem]
        in_refs = [None] * n_in
        for k, i in enumerate(source_at):
            in_refs[i] = src_refs[k]
        for i, o in exchange.aliases.items():
            in_refs[i] = out_refs[o]
        exchange.make(in_refs, out_refs, sem_refs)[1]()

    arrays = list(sources) + list(landing)
    outs = pl.pallas_call(
        body, name=name, in_specs=[HBM] * len(arrays) + [_SEM] * n_sem + [HBM] * len(after),
        out_specs=[HBM] * len(arrays), out_shape=[pltpu.HBM(a.shape, a.dtype) for a in arrays],
        input_output_aliases={i: i for i in range(len(arrays))}, compiler_params=_params(has_side_effects=_DATAFLOW),
    )(*arrays, *sems, *after)
    return outs[:n_src], outs[n_src:]


N_GATHERED = 2
FIRST_COPIES = 1 + N_OTHER_CHIPS
_GATHERED_SHAPES = ((D_MODEL, IN_COLS), (D_MODEL, D_MODEL))


def _first_sem(layer, a, k):
    return (layer * N_GATHERED + a) * FIRST_COPIES + k


def _gather_copy(window_of, full_ref, blk, send_sem, recv_sem, to, src=None):
    window = window_of(full_ref, blk)
    return pltpu.make_async_remote_copy(
        src_ref=window if src is None else src, dst_ref=window, send_sem=send_sem, recv_sem=recv_sem,
        device_id=to, device_id_type=MESH)


def _first_copies(layer, shard_refs, full_refs, send_sems, recv_sems, local_sems):
    x, y, c = _mesh_position()
    me = _block_id(x, y, c)
    own, remote = [], []
    for a in range(N_GATHERED):
        shard = shard_refs[a].at[layer]
        own.append(pltpu.make_async_copy(
            shard, _BLOCK_OF[a](full_refs[a], me), local_sems.at[layer * N_GATHERED + a]))
        targets = [(x, y, 1 - c)] + [(*chip, c) for chip in _other_chips(x, y)]
        remote += [_gather_copy(_BLOCK_OF[a], full_refs[a], me, send_sems.at[_first_sem(layer, a, k)],
                                recv_sems.at[_first_sem(layer, a, k)], to, src=shard)
                   for k, to in enumerate(targets)]
    return own, remote


def _gather_weights_start(win_shards, wout_shards, after):
    n_first = DEPTH * N_GATHERED * FIRST_COPIES
    sem_shapes = [pltpu.SemaphoreType.DMA((n_first,)), pltpu.SemaphoreType.DMA((n_first,)),
                  pltpu.SemaphoreType.DMA((DEPTH * N_GATHERED,))]
    shards = [win_shards, wout_shards]

    def body(win_sh, wout_sh, *rest):
        send_sems, recv_sems, local_sems, win_thru, wout_thru, *landing = rest[len(after):]
        for layer in range(DEPTH):
            own, remote = _first_copies(layer, (win_sh, wout_sh), landing[N_GATHERED * layer:N_GATHERED * (layer + 1)],
                                        send_sems, recv_sems, local_sems)
            for cp in own + remote:
                cp.start()

    outs = pl.pallas_call(
        body, name="all_gather_weights_start", in_specs=[HBM] * (2 + len(after)),
        out_specs=[_SEM] * 3 + [HBM] * (2 + DEPTH * N_GATHERED),
        out_shape=(sem_shapes + [pltpu.HBM(s.shape, s.dtype) for s in shards]
                   + [pltpu.HBM(s, BF16) for _ in range(DEPTH) for s in _GATHERED_SHAPES]),
        input_output_aliases={0: 3, 1: 4}, compiler_params=_params(has_side_effects=_DATAFLOW),
    )(*shards, *after)
    landing = outs[5:]
    return outs[:3], outs[3:5], [landing[N_GATHERED * l:N_GATHERED * (l + 1)] for l in range(DEPTH)]


def _passed_on_copies(full_refs, send_sems, recv_sems, core_of_block):
    x, y, c = _mesh_position()
    return [_gather_copy(_BLOCK_OF[a], full_refs[a], _block_id(*chip, core_of_block),
                         send_sems.at[a * N_OTHER_CHIPS + j], recv_sems.at[a * N_OTHER_CHIPS + j], (x, y, 1 - c))
            for a in range(N_GATHERED) for j, chip in enumerate(_other_chips(x, y))]


def _gather_weights_pass_on(layer, first_recv_sems, landing, after):
    n = N_GATHERED * N_OTHER_CHIPS

    def body(win_ref, wout_ref, first_recv, *rest):
        send_sems, recv_sems = rest[len(after):len(after) + 2]
        x, y, c = _mesh_position()
        full_refs = (win_ref, wout_ref)
        passed = _passed_on_copies(full_refs, send_sems, recv_sems, c)
        for a in range(N_GATHERED):
            for j, chip in enumerate(_other_chips(x, y)):
                sem = _first_sem(layer, a, 1 + j)
                _gather_copy(_BLOCK_OF[a], full_refs[a], _block_id(*chip, c), first_recv.at[sem], first_recv.at[sem],
                             (x, y, c)).wait_recv()
                passed[a * N_OTHER_CHIPS + j].start()

    outs = pl.pallas_call(
        body, name=f"all_gather_weights_pass_on_{layer}", in_specs=[HBM] * N_GATHERED + [_SEM] + [HBM] * len(after),
        out_specs=[_SEM] * 2 + [HBM] * N_GATHERED,
        out_shape=[pltpu.SemaphoreType.DMA((n,)), pltpu.SemaphoreType.DMA((n,))]
        + [pltpu.HBM(a.shape, a.dtype) for a in landing],
        input_output_aliases={a: 2 + a for a in range(N_GATHERED)},
        compiler_params=_params(has_side_effects=_DATAFLOW),
    )(*landing, first_recv_sems, *after)
    return outs[:2], outs[2:]


def _gather_weights_finish(layer, first_sems, passed_sems, shards, landing):
    def body(win_ref, wout_ref, first_send, first_recv, local_sems, passed_send, passed_recv, win_sh, wout_sh, *thru):
        x, y, c = _mesh_position()
        full_refs = (win_ref, wout_ref)
        own, remote = _first_copies(layer, (win_sh, wout_sh), full_refs, first_send, first_recv, local_sems)
        for a in range(N_GATHERED):
            sem = _first_sem(layer, a, 0)
            _gather_copy(_BLOCK_OF[a], full_refs[a], _block_id(x, y, 1 - c), first_recv.at[sem], first_recv.at[sem],
                         (x, y, c)).wait_recv()
        for cp in _passed_on_copies(full_refs, passed_send, passed_recv, 1 - c):
            cp.wait_recv()
        for cp in remote + _passed_on_copies(full_refs, passed_send, passed_recv, c):
            cp.wait_send()
        for cp in own:
            cp.wait()

    return pl.pallas_call(
        body, name=f"all_gather_weights_finish_{layer}", in_specs=[HBM] * N_GATHERED + [_SEM] * 5 + [HBM] * 2,
        out_specs=[HBM] * N_GATHERED, out_shape=[pltpu.HBM(a.shape, a.dtype) for a in landing],
        input_output_aliases={a: a for a in range(N_GATHERED)}, compiler_params=_params(has_side_effects=_DATAFLOW),
    )(*landing, *first_sems, *passed_sems, *shards)


ALL_KINDS = (0, 1, 2)


def _sibling_exchange(grads, kinds):
    n_arr = len(grads)

    def make(in_refs, out_refs, sems):
        send_sems, recv_sems = sems
        x, y, c = _mesh_position()
        copies = [pltpu.make_async_remote_copy(
            src_ref=_BLOCK_OF[kinds[a]](in_refs[a], 2 * q + (1 - c)), dst_ref=out_refs[a].at[q],
            send_sem=send_sems.at[a, q], recv_sem=recv_sems.at[a, q], device_id=(x, y, 1 - c), device_id_type=MESH)
            for a in range(n_arr) for q in range(N_CHIP)]

        def start():
            for cp in copies:
                cp.start()

        def finish():
            for cp in copies:
                cp.wait_recv()
            for cp in copies:
                cp.wait_send()

        return start, finish

    return _Exchange(
        grads, [jax.ShapeDtypeStruct((N_CHIP,) + _BLOCK_SHAPES[k], BF16) for k in kinds], {},
        [pltpu.SemaphoreType.DMA((n_arr, N_CHIP)), pltpu.SemaphoreType.DMA((n_arr, N_CHIP))], make)


def _chips_exchange(layer, partials, received, kinds):
    n_arr = len(partials)

    def make(in_refs, out_refs, sems):
        send_sems, recv_sems = sems
        x, y, c = _mesh_position()
        copies = [pltpu.make_async_remote_copy(
            src_ref=in_refs[a].at[2 * qx + qy, layer], dst_ref=out_refs[a].at[j, layer],
            send_sem=send_sems.at[a * N_OTHER_CHIPS + j], recv_sem=recv_sems.at[a * N_OTHER_CHIPS + j],
            device_id=(qx, qy, c), device_id_type=MESH)
            for a in range(n_arr) for j, (qx, qy) in enumerate(_other_chips(x, y))]

        def start():
            for cp in copies:
                cp.start()

        def finish():
            for cp in copies:
                cp.wait_recv()
            for cp in copies:
                cp.wait_send()

        return start, finish

    inputs = list(partials)
    aliases = {}
    if received is not None:
        inputs += list(received)
        aliases = {n_arr + a: a for a in range(n_arr)}
    return _Exchange(
        inputs, [jax.ShapeDtypeStruct((N_OTHER_CHIPS, DEPTH) + _BLOCK_SHAPES[k], BF16) for k in kinds], aliases,
        [pltpu.SemaphoreType.DMA((n_arr * N_OTHER_CHIPS,)), pltpu.SemaphoreType.DMA((n_arr * N_OTHER_CHIPS,))], make)


def _host(exchange, args, in_specs, out_shape, out_specs, scratch):
    n_own = (len(args), len(out_shape), len(scratch))
    if exchange is None:
        return {}, lambda refs: (refs, None)
    n_ex = (len(exchange.inputs), len(exchange.out_shapes), len(exchange.sem_shapes))
    aliases = {n_own[0] + i: n_own[1] + o for i, o in exchange.aliases.items()}
    args += exchange.inputs
    in_specs += [HBM] * n_ex[0]
    out_shape += exchange.out_shapes
    out_specs += [HBM] * n_ex[1]
    scratch += exchange.sem_shapes

    def split(refs):
        own, theirs, at = [], [], 0
        for mine, ex in zip(n_own, n_ex):
            own += refs[at:at + mine]
            theirs.append(refs[at + mine:at + mine + ex])
            at += mine + ex
        return own, exchange.make(*theirs)

    return aliases, split


def _all_gather_small(v, name, exchange=None):
    vmem = pl.BlockSpec(memory_space=pltpu.VMEM)
    args, in_specs = [v], [vmem]
    out_shape, out_specs = [jax.ShapeDtypeStruct((N_DEV,) + v.shape, v.dtype)], [vmem]
    scratch = [pltpu.SemaphoreType.DMA((N_DEV - 1,)), pltpu.SemaphoreType.DMA((N_DEV - 1,))]
    aliases, split = _host(exchange, args, in_specs, out_shape, out_specs, scratch)

    def body(*refs):
        (v_ref, out_ref, send_sems, recv_sems), hosted = split(refs)
        if hosted is not None:
            hosted[0]()
        x, y, c = _mesh_position()
        me = _block_id(x, y, c)
        out_ref[me] = v_ref[...]
        sends = []
        for k in range(1, N_DEV):
            px, py, pc = x ^ ((k >> 2) & 1), y ^ ((k >> 1) & 1), c ^ (k & 1)
            send = pltpu.make_async_remote_copy(
                src_ref=v_ref, dst_ref=out_ref.at[me], send_sem=send_sems.at[k - 1], recv_sem=recv_sems.at[k - 1],
                device_id=(px, py, pc), device_id_type=MESH)
            send.start()
            sends.append((send, _block_id(px, py, pc)))
        for k, (send, peer) in enumerate(sends):
            pltpu.make_async_remote_copy(
                src_ref=v_ref, dst_ref=out_ref.at[peer], send_sem=send_sems.at[k], recv_sem=recv_sems.at[k],
                device_id=(x, y, c), device_id_type=MESH).wait_recv()
        for send, _ in sends:
            send.wait_send()
        if hosted is not None:
            hosted[1]()

    outs = pl.pallas_call(
        body, name=name, in_specs=in_specs, out_specs=out_specs, out_shape=out_shape, scratch_shapes=scratch,
        input_output_aliases=aliases, compiler_params=_params(),
    )(*args)
    return outs[0] if exchange is None else outs


def _forward_layer(layer, x, vec, cps, wpool, win, wout, exchange):
    t_len = x.shape[0]
    n_tiles = t_len // ROW_TILE
    row = lambda cols: pl.BlockSpec((ROW_TILE, cols), lambda i: (i, 0))
    args = [x, vec, cps, wpool, win, wout]
    in_specs = [row(D_MODEL), _layer_spec(vec.shape, layer), _layer_spec(cps.shape, layer),
                _layer_spec(wpool.shape, layer), _whole_spec(win.shape), _whole_spec(wout.shape)]
    out_shape = [jax.ShapeDtypeStruct((t_len, D_MODEL), F32), jax.ShapeDtypeStruct((t_len, IN_COLS), BF16),
                 jax.ShapeDtypeStruct((t_len, D_MODEL), BF16)]
    out_specs = [row(D_MODEL), row(IN_COLS), row(D_MODEL)]
    scratch = [pltpu.VMEM((CONV_HALO, CONV_W), F32), pltpu.VMEM((POOL_HALO, POOL_W), F32)]
    aliases, split = _host(exchange, args, in_specs, out_shape, out_specs, scratch)

    def body(*refs):
        (x_ref, vec_ref, cps_ref, wpool_ref, win_ref, wout_ref, xo_ref, proj_ref, y_ref, zc_ref, pc_ref), hosted = (
            split(refs))
        i = pl.program_id(0)

        @pl.when(i == 0)
        def _():
            if hosted is not None:
                hosted[0]()
            zc_ref[...] = jnp.zeros_like(zc_ref)
            pc_ref[...] = jnp.zeros_like(pc_ref)

        x_t = x_ref[...]
        shift, scale, gate = vec_ref[0:1, :], vec_ref[1:2, :], vec_ref[2:3, :]
        g_pre, g_post = vec_ref[3:4, :], vec_ref[4:5, :]
        rx = lax.rsqrt(jnp.mean(x_t * x_t, axis=-1, keepdims=True) + NORM_EPS)
        h = (x_t * rx) * g_pre * (1.0 + scale) + shift
        proj_b = _dot(h.astype(BF16), win_ref[...]).astype(BF16)
        proj_ref[...] = proj_b
        mx = _mixer_forward(proj_b.astype(F32), zc_ref[...], pc_ref[...], cps_ref[...], wpool_ref, i * ROW_TILE)
        zc_ref[...] = mx["z"][ROW_TILE - CONV_HALO:]
        pc_ref[...] = mx["u_p"][ROW_TILE - POOL_HALO:]
        y_b = _dot(mx["ycat"].astype(BF16), wout_ref[...]).astype(BF16)
        y_ref[...] = y_b
        y_t = y_b.astype(F32)
        ry = lax.rsqrt(jnp.mean(y_t * y_t, axis=-1, keepdims=True) + NORM_EPS)
        xo_ref[...] = x_t + gate * (y_t * ry * g_post)

        if hosted is not None:
            pl.when(i == n_tiles - 1)(hosted[1])

    return pl.pallas_call(
        body, name=f"forward_layer_{layer}", grid=(n_tiles,), in_specs=in_specs, out_specs=out_specs,
        out_shape=out_shape, scratch_shapes=scratch, input_output_aliases=aliases,
        compiler_params=_params(dimension_semantics=("arbitrary",)),
    )(*args)


def _loss_head(x_final, target):
    t_len = x_final.shape[0]
    n_tiles = t_len // ROW_TILE

    def body(x_ref, t_ref, dx_ref, loss_ref):
        @pl.when(pl.program_id(0) == 0)
        def _():
            loss_ref[...] = jnp.zeros_like(loss_ref)

        err = x_ref[...] - t_ref[...]
        dx_ref[...] = err * (1.0 / D_MODEL)
        loss_ref[...] += jnp.sum(err * err) * (0.5 / D_MODEL)

    row = pl.BlockSpec((ROW_TILE, D_MODEL), lambda i: (i, 0))
    return pl.pallas_call(
        body, name="loss_head", grid=(n_tiles,), in_specs=[row, row],
        out_specs=[row, pl.BlockSpec((SUBLANES, LANES), lambda i: (0, 0))],
        out_shape=[jax.ShapeDtypeStruct((t_len, D_MODEL), F32), jax.ShapeDtypeStruct((SUBLANES, LANES), F32)],
        compiler_params=_params(dimension_semantics=("arbitrary",)),
    )(x_final, target)


def _backward_layer(layer, dxo, y, proj, x, vec, cps, wpool, wout, win, exchange):
    t_len = dxo.shape[0]
    n_tiles = t_len // BWD_TILE
    halo_per_tile = BWD_TILE // POOL_HALO
    rev = lambda cols: pl.BlockSpec((BWD_TILE, cols), lambda i: (n_tiles - 1 - i, 0))
    rev_t = pl.BlockSpec((D_MODEL, BWD_TILE), lambda i: (0, n_tiles - 1 - i))
    halo_spec = pl.BlockSpec(
        (POOL_HALO, IN_COLS), lambda i: (jnp.maximum((n_tiles - 1 - i) * halo_per_tile - 1, 0), 0))
    gwpool_shape = (len(POOL_WINDOWS), GROUP_D, GROUP_D)
    args = [dxo, y, proj, proj, x, vec, cps, wpool, wout, win]
    in_specs = [rev(D_MODEL), rev(D_MODEL), rev(IN_COLS), halo_spec, rev(D_MODEL), _layer_spec(vec.shape, layer),
                _layer_spec(cps.shape, layer), _layer_spec(wpool.shape, layer), _whole_spec(wout.shape),
                _whole_spec(win.shape)]
    out_shape = [jax.ShapeDtypeStruct((t_len, D_MODEL), F32), jax.ShapeDtypeStruct((t_len, IN_COLS), BF16),
                 jax.ShapeDtypeStruct((D_MODEL, t_len), BF16), jax.ShapeDtypeStruct((D_MODEL, t_len), BF16),
                 jax.ShapeDtypeStruct((t_len, D_MODEL), BF16), jax.ShapeDtypeStruct(gwpool_shape, BF16),
                 jax.ShapeDtypeStruct((SUBLANES, CONV_W), F32), jax.ShapeDtypeStruct((SUBLANES, D_MODEL), F32)]
    out_specs = [rev(D_MODEL), rev(IN_COLS), rev_t, rev_t, rev(D_MODEL), _whole_spec(gwpool_shape),
                 _whole_spec((SUBLANES, CONV_W)), _whole_spec((SUBLANES, D_MODEL))]
    scratch = [pltpu.VMEM(gwpool_shape, F32), pltpu.VMEM((CONV_HALO, CONV_W), F32),
               pltpu.VMEM((POOL_HALO, POOL_W), F32)]
    aliases, split = _host(exchange, args, in_specs, out_shape, out_specs, scratch)

    def body(*refs):
        (dxo_ref, y_ref, proj_ref, projh_ref, x_ref, vec_ref, cps_ref, wpool_ref, wout_ref, win_ref,
         dx_ref, dproj_ref, ht_ref, ycatt_ref, dy_ref, gwpool_ref, dcps_ref, dvec_ref,
         gwpool_acc, dcc_ref, qc_ref), hosted = split(refs)
        i = pl.program_id(0)
        tile = n_tiles - 1 - i

        @pl.when(i == 0)
        def _():
            if hosted is not None:
                hosted[0]()
            gwpool_acc[...] = jnp.zeros_like(gwpool_acc)
            dcps_ref[...] = jnp.zeros_like(dcps_ref)
            dvec_ref[...] = jnp.zeros_like(dvec_ref)
            dcc_ref[...] = jnp.zeros_like(dcc_ref)
            qc_ref[...] = jnp.zeros_like(qc_ref)

        shift, scale, gate = vec_ref[0:1, :], vec_ref[1:2, :], vec_ref[2:3, :]
        g_pre, g_post = vec_ref[3:4, :], vec_ref[4:5, :]

        dxo_t = dxo_ref[...]
        y_t = y_ref[...].astype(F32)
        ry = lax.rsqrt(jnp.mean(y_t * y_t, axis=-1, keepdims=True) + NORM_EPS)
        yh = y_t * ry
        dxy = dxo_t * yh
        dvec_ref[2:3, :] += jnp.sum(dxy * g_post, axis=0, keepdims=True)
        dvec_ref[4:5, :] += jnp.sum(dxy * gate, axis=0, keepdims=True)
        dyh = dxo_t * (gate * g_post)
        dy_b = (ry * (dyh - yh * jnp.mean(dyh * yh, axis=-1, keepdims=True))).astype(BF16)
        dy_ref[...] = dy_b

        halo = jnp.where(tile > 0, projh_ref[...].astype(F32), 0.0)
        hu_a, _, hc_a, _, hu_p, _ = _split_proj(halo)
        z_halo = (hc_a * hu_a)[POOL_HALO - CONV_HALO:]
        mx = _mixer_forward(proj_ref[...].astype(F32), z_halo, hu_p, cps_ref[...], wpool_ref, tile * BWD_TILE)
        ycatt_ref[...] = mx["ycat"].T.astype(BF16)

        dycat = _dot_nt(dy_b, wout_ref[...])
        dy_a, dy_p = dycat[:, :CONV_W], dycat[:, CONV_W:]

        t_a = dy_a * mx["silu_a"]
        db_a = t_a * mx["conv"]
        dconv = t_a * mx["b_a"]
        dg_a = dy_a * mx["b_a"] * mx["conv"] * (mx["sig_a"] * (1.0 + mx["g_a"] * (1.0 - mx["sig_a"])))
        dccat = jnp.concatenate([dconv, dcc_ref[...]], axis=0)
        dc1 = _rows_from_after(dccat, 1)[:BWD_TILE]
        dc2 = _rows_from_after(dccat, 2)[:BWD_TILE]
        dz = mx["w2"] * dconv + mx["w1"] * dc1 + mx["w0"] * dc2
        dcc_ref[...] = dconv[:CONV_HALO]
        dcps_ref[0:1, :] += jnp.sum(dconv * mx["z2"], axis=0, keepdims=True)
        dcps_ref[1:2, :] += jnp.sum(dconv * mx["z1"], axis=0, keepdims=True)
        dcps_ref[2:3, :] += jnp.sum(dconv * mx["z"], axis=0, keepdims=True)
        du_a = dz * mx["c_a"]
        dc_a = dz * mx["u_a"]

        t_p = dy_p * mx["silu_p"]
        dcps_ref[3:4, :] += jnp.sum(t_p * mx["mixed"], axis=0, keepdims=True)
        dmixed = (t_p * mx["ps"]).astype(BF16)
        dg_p = dy_p * mx["mixed"] * mx["ps"] * (mx["sig_p"] * (1.0 + mx["g_p"] * (1.0 - mx["sig_p"])))
        du_p, q_head = [], []
        for g, w in enumerate(POOL_WINDOWS):
            cols = slice(g * GROUP_D, (g + 1) * GROUP_D)
            dm_g = dmixed[:, cols]
            dpooled_g = _dot_nt(dm_g, wpool_ref[g])
            gwpool_acc[g] += _dot_tn(mx["pooled"][g].astype(BF16), dm_g)
            q_g = dpooled_g * mx["inv"][g]
            q_head.append(q_g[:POOL_HALO])
            s = jnp.concatenate([q_g, qc_ref[:, cols]], axis=0)
            step = 1
            while step < w:
                s = s + _rows_from_after(s, step)
                step *= 2
            du_p.append(s[:BWD_TILE] - dpooled_g)
        qc_ref[...] = jnp.concatenate(q_head, axis=1)
        dproj_b = jnp.concatenate([du_a, db_a, dc_a, dg_a] + du_p + [dg_p], axis=1).astype(BF16)
        dproj_ref[...] = dproj_b

        x_t = x_ref[...]
        rx = lax.rsqrt(jnp.mean(x_t * x_t, axis=-1, keepdims=True) + NORM_EPS)
        xn = x_t * rx
        mod_scale = 1.0 + scale
        ht_ref[...] = (xn * g_pre * mod_scale + shift).T.astype(BF16)
        dh = _dot_nt(dproj_b, win_ref[...])
        dvec_ref[0:1, :] += jnp.sum(dh, axis=0, keepdims=True)
        dhx = dh * xn
        dvec_ref[1:2, :] += jnp.sum(dhx * g_pre, axis=0, keepdims=True)
        dvec_ref[3:4, :] += jnp.sum(dhx * mod_scale, axis=0, keepdims=True)
        dxn = dh * (g_pre * mod_scale)
        dx_ref[...] = dxo_t + rx * (dxn - xn * jnp.mean(dxn * xn, axis=-1, keepdims=True))

        @pl.when(i == n_tiles - 1)
        def _():
            gwpool_ref[...] = gwpool_acc[...].astype(BF16)
            if hosted is not None:
                hosted[1]()

    return pl.pallas_call(
        body, name=f"backward_layer_{layer}", grid=(n_tiles,), in_specs=in_specs, out_specs=out_specs,
        out_shape=out_shape, scratch_shapes=scratch, input_output_aliases=aliases,
        compiler_params=_params(dimension_semantics=("arbitrary",)),
    )(*args)


def _weight_grads(layer, h_t, dproj, ycat_t, dy):
    t_len = dy.shape[0]
    n_in, n_out = IN_COLS // W_IN_SHARD, D_MODEL // W_OUT_SHARD

    def body(ht_ref, dproj_ref, ycatt_ref, dy_ref, gwin_ref, gwout_ref):
        s = pl.program_id(0)

        @pl.when(s < n_in)
        def _():
            gwin_ref[...] = _dot(ht_ref[...], dproj_ref[...]).astype(BF16)

        @pl.when(s >= n_in)
        def _():
            gwout_ref[...] = _dot(ycatt_ref[...], dy_ref[...]).astype(BF16)

    return pl.pallas_call(
        body, name=f"weight_grads_{layer}", grid=(n_in + n_out,),
        in_specs=[_whole_spec(h_t.shape),
                  pl.BlockSpec((t_len, W_IN_SHARD), lambda s: (0, jnp.minimum(s, n_in - 1))),
                  pl.BlockSpec((W_OUT_SHARD, t_len), lambda s: (jnp.maximum(s - n_in, 0), 0)),
                  _whole_spec(dy.shape)],
        out_specs=[pl.BlockSpec((D_MODEL, W_IN_SHARD), lambda s: (0, jnp.minimum(s, n_in - 1))),
                   pl.BlockSpec((W_OUT_SHARD, D_MODEL), lambda s: (jnp.maximum(s - n_in, 0), 0))],
        out_shape=[jax.ShapeDtypeStruct((D_MODEL, IN_COLS), BF16), jax.ShapeDtypeStruct((D_MODEL, D_MODEL), BF16)],
        compiler_params=_params(dimension_semantics=("arbitrary",)),
    )(h_t, dproj, ycat_t, dy)


def _add_sibling_blocks(name, layer, grads, received, core, partials, kinds):
    n_arr = len(grads)

    def body(core_ref, *refs):
        mine, theirs, outs = refs[:n_arr], refs[n_arr:2 * n_arr], refs[-n_arr:]
        for a in range(n_arr):
            outs[a][...] = (mine[a][...].astype(F32) + theirs[a][...].astype(F32)).astype(BF16)

    own_of_kind = [
        pl.BlockSpec((D_MODEL, W_IN_SHARD), lambda q, core_ref: (0, 2 * q + core_ref[0])),
        pl.BlockSpec((W_OUT_SHARD, D_MODEL), lambda q, core_ref: (2 * q + core_ref[0], 0)),
        pl.BlockSpec((POOL_SHARD, GROUP_D), lambda q, core_ref: (2 * q + core_ref[0], 0)),
    ]
    shapes = [_BLOCK_SHAPES[k] for k in kinds]
    recv_specs = [pl.BlockSpec((None,) + s, lambda q, core_ref: (q, 0, 0)) for s in shapes]
    out_specs = [pl.BlockSpec((None, None) + s, lambda q, core_ref: (q, layer, 0, 0)) for s in shapes]
    args = [core, *grads, *received]
    in_specs = [own_of_kind[k] for k in kinds] + recv_specs
    aliases = {}
    if partials is not None:
        aliases = {len(args) + a: a for a in range(n_arr)}
        args += list(partials)
        in_specs += [HBM] * n_arr
    return pl.pallas_call(
        body, name=name,
        grid_spec=pltpu.PrefetchScalarGridSpec(
            num_scalar_prefetch=1, grid=(N_CHIP,), in_specs=in_specs, out_specs=out_specs),
        out_shape=[jax.ShapeDtypeStruct((N_CHIP, DEPTH) + s, BF16) for s in shapes],
        input_output_aliases=aliases,
        compiler_params=_params(dimension_semantics=("arbitrary",)),
    )(*args)


def _modulation_columns(c_all, w_ada):
    def body(c_ref, w_ref, cact_ref, out_ref):
        c_t = c_ref[...]
        c_act = c_t * _sigmoid(c_t)
        cact_ref[...] = c_act
        out_ref[...] = jnp.dot(c_act, w_ref[...], preferred_element_type=F32, precision=lax.Precision.HIGHEST)

    return pl.pallas_call(
        body, name="modulation_columns", grid=(DEPTH,),
        in_specs=[pl.BlockSpec((N_DEV, D_MODEL), lambda l: (0, 0)),
                  pl.BlockSpec((None, D_MODEL, W_IN_SHARD), lambda l: (l, 0, 0))],
        out_specs=[pl.BlockSpec((N_DEV, D_MODEL), lambda l: (0, 0)),
                   pl.BlockSpec((N_DEV, W_IN_SHARD), lambda l: (0, l))],
        out_shape=[jax.ShapeDtypeStruct((N_DEV, D_MODEL), F32),
                   jax.ShapeDtypeStruct((N_DEV, DEPTH * W_IN_SHARD), F32)],
        compiler_params=_params(dimension_semantics=("arbitrary",)),
    )(c_all, w_ada)


def _adamw(w, g, m, v):
    m_new = ADAM_B1 * m + (1.0 - ADAM_B1) * g
    v_new = ADAM_B2 * v + (1.0 - ADAM_B2) * (g * g)
    m_hat = m_new / (1.0 - ADAM_B1 ** ADAM_STEP)
    v_hat = v_new / (1.0 - ADAM_B2 ** ADAM_STEP)
    delta = -ADAM_LR * (m_hat / (jnp.sqrt(v_hat) + ADAM_EPS) + ADAM_WD * w)
    return delta, m_new, v_new


def _adamw_w_ada(w, m, v, c_act_t, dmod_cols, exchange):
    big = pl.BlockSpec((None, D_MODEL, W_IN_SHARD), lambda l: (l, 0, 0))
    args = [w, m, v, c_act_t, dmod_cols]
    in_specs = [big, big, big, pl.BlockSpec((D_MODEL, N_DEV), lambda l: (0, 0)),
                pl.BlockSpec((None, N_DEV, W_IN_SHARD), lambda l: (l, 0, 0))]
    out_shape, out_specs, scratch = [jax.ShapeDtypeStruct(w.shape, F32)] * 4, [big] * 4, []
    aliases, split = _host(exchange, args, in_specs, out_shape, out_specs, scratch)

    def body(*refs):
        (w_ref, m_ref, v_ref, ct_ref, dm_ref, g_ref, d_ref, mo_ref, vo_ref), hosted = split(refs)
        if hosted is not None:
            pl.when(pl.program_id(0) == 0)(hosted[0])
        g = ct_ref[:, 0:1] * dm_ref[0:1, :]
        for b in range(1, N_DEV):
            g = g + ct_ref[:, b:b + 1] * dm_ref[b:b + 1, :]
        g_ref[...] = g
        d_ref[...], mo_ref[...], vo_ref[...] = _adamw(w_ref[...], g, m_ref[...], v_ref[...])
        if hosted is not None:
            pl.when(pl.program_id(0) == DEPTH - 1)(hosted[1])

    return pl.pallas_call(
        body, name="adamw_w_ada", grid=(DEPTH,), in_specs=in_specs, out_specs=out_specs, out_shape=out_shape,
        scratch_shapes=scratch, input_output_aliases=aliases,
        compiler_params=_params(dimension_semantics=("arbitrary",)),
    )(*args)


def _sum_chip_partials(own_ref, recv_ref):
    g = own_ref[...].astype(F32)
    for j in range(N_OTHER_CHIPS):
        g = g + recv_ref[j].astype(F32)
    return g


def _partial_specs(row_tile, cols):
    own = pl.BlockSpec((None, None, row_tile, cols), lambda l, r, chip_ref: (chip_ref[0], l, r, 0))
    recv = pl.BlockSpec((N_OTHER_CHIPS, None, row_tile, cols), lambda l, r, chip_ref: (0, l, r, 0))
    return own, recv


def _adamw_reduced(name, w, m, v, partial, received, chip, row_tile):
    depth, rows, cols = w.shape

    def body(chip_ref, w_ref, m_ref, v_ref, own_ref, recv_ref, g_ref, d_ref, mo_ref, vo_ref):
        g = _sum_chip_partials(own_ref, recv_ref)
        g_ref[...] = g
        d_ref[...], mo_ref[...], vo_ref[...] = _adamw(w_ref[...], g, m_ref[...], v_ref[...])

    blk = pl.BlockSpec((None, row_tile, cols), lambda l, r, chip_ref: (l, r, 0))
    return pl.pallas_call(
        body, name=name,
        grid_spec=pltpu.PrefetchScalarGridSpec(
            num_scalar_prefetch=1, grid=(depth, rows // row_tile),
            in_specs=[blk, blk, blk, *_partial_specs(row_tile, cols)], out_specs=[blk] * 4),
        out_shape=[jax.ShapeDtypeStruct(w.shape, F32)] * 4,
        compiler_params=_params(dimension_semantics=("arbitrary", "arbitrary")),
    )(chip, w, m, v, partial, received)


def _reduce_w_pool(partial, received, chip):
    def body(chip_ref, own_ref, recv_ref, g_ref):
        g_ref[...] = _sum_chip_partials(own_ref, recv_ref)

    return pl.pallas_call(
        body, name="reduce_w_pool",
        grid_spec=pltpu.PrefetchScalarGridSpec(
            num_scalar_prefetch=1, grid=(DEPTH, 1), in_specs=list(_partial_specs(POOL_SHARD, GROUP_D)),
            out_specs=pl.BlockSpec((POOL_SHARD, GROUP_D), lambda l, r, chip_ref: (l, 0))),
        out_shape=jax.ShapeDtypeStruct((DEPTH * POOL_SHARD, GROUP_D), F32),
        compiler_params=_params(dimension_semantics=("arbitrary", "arbitrary")),
    )(chip, partial, received)


def _adamw_small(params):
    n = len(params)

    def body(*refs):
        ins, outs = refs[:4 * n], refs[4 * n:]
        for p in range(n):
            w_ref, g_ref, m_ref, v_ref = ins[4 * p:4 * p + 4]
            d_ref, mo_ref, vo_ref = outs[3 * p:3 * p + 3]
            d_ref[...], mo_ref[...], vo_ref[...] = _adamw(w_ref[...], g_ref[...], m_ref[...], v_ref[...])

    vmem = pl.BlockSpec(memory_space=pltpu.VMEM)
    flat = [a for group in params for a in group]
    out_shape = [jax.ShapeDtypeStruct(group[0].shape, F32) for group in params for _ in range(3)]
    outs = pl.pallas_call(
        body, name="adamw_small", in_specs=[vmem] * len(flat), out_specs=[vmem] * len(out_shape),
        out_shape=out_shape, compiler_params=_params(),
    )(*flat)
    return [tuple(outs[3 * p:3 * p + 3]) for p in range(n)]


def _sum_sources(slabs):
    def body(s_ref, o_ref):
        acc = s_ref[0]
        for b in range(1, N_DEV):
            acc = acc + s_ref[b]
        o_ref[...] = acc

    vmem = pl.BlockSpec(memory_space=pltpu.VMEM)
    return pl.pallas_call(
        body, name="sum_small_grads", in_specs=[vmem], out_specs=vmem,
        out_shape=jax.ShapeDtypeStruct(slabs.shape[1:], F32), compiler_params=_params(),
    )(slabs)


def _to_bf16(a, name):
    def body(a_ref, o_ref):
        o_ref[...] = a_ref[...].astype(BF16)

    spec = pl.BlockSpec((None,) + a.shape[1:], lambda l: (l, 0, 0))
    return pl.pallas_call(
        body, name=name, grid=(a.shape[0],), in_specs=[spec], out_specs=spec,
        out_shape=jax.ShapeDtypeStruct(a.shape, BF16), compiler_params=_params(dimension_semantics=("arbitrary",)),
    )(a)


def kernel(x, c, w_ada, b_ada, g_pre, w_in, w_conv, w_pool, pool_scale, w_out, g_post, loss_target, m_w_ada, m_b_ada, m_g_pre, m_w_in, m_w_conv, m_w_pool, m_pool_scale, m_w_out, m_g_post, v_w_ada, v_b_ada, v_g_pre, v_w_in, v_w_conv, v_w_pool, v_pool_scale, v_w_out, v_g_post):
    mx, my, mc = _mesh_position()
    me = _block_id(mx, my, mc)
    chip = (2 * mx + my).astype(jnp.int32).reshape(1)
    core = mc.astype(jnp.int32).reshape(1)
    x0 = x[0]
    target = loss_target[0]
    conv_shard = w_conv.shape[-1]

    own_small = jnp.concatenate([c, w_conv.reshape(1, DEPTH * 3 * conv_shard)], axis=1)
    all_small = _all_gather_small(own_small, "all_gather_c_w_conv")[:, 0, :]
    c_all = all_small[:, :D_MODEL]
    w_conv_full = all_small[:, D_MODEL:].reshape(N_DEV, DEPTH, 3, conv_shard).transpose(1, 2, 0, 3).reshape(
        DEPTH, 3, CONV_W)
    cps = jnp.concatenate([w_conv_full, pool_scale[:, None], jnp.zeros((DEPTH, 4, CONV_W), F32)], axis=1)

    c_act, pieces = _modulation_columns(c_all, w_ada)
    mod_all = _all_gather_small(pieces, "all_gather_modulation")
    mod_mine = lax.dynamic_index_in_dim(mod_all, me, axis=1, keepdims=False)
    mod = mod_mine.reshape(N_DEV, DEPTH, W_IN_SHARD).transpose(1, 0, 2).reshape(DEPTH, 3 * D_MODEL) + b_ada
    zeros_d = jnp.zeros((DEPTH, 3, D_MODEL), F32)
    vec = jnp.concatenate([mod.reshape(DEPTH, 3, D_MODEL), g_pre[:, None], g_post[:, None], zeros_d], axis=1)

    first_sems, shards, landing = _gather_weights_start(
        _to_bf16(w_in, "cast_w_in"), _to_bf16(w_out, "cast_w_out"), [mod_all, all_small])
    wpool_b = _to_bf16(w_pool.reshape(DEPTH, POOL_ROWS, GROUP_D), "cast_w_pool").reshape(w_pool.shape)

    xs, projs, ys, wins, wouts = [x0], [], [], [], []
    for l in range(DEPTH):
        passed_sems, zones = _gather_weights_pass_on(
            l, first_sems[1], landing[l], [vec, cps, wpool_b] if l == 0 else [xs[-1]])
        win, wout = _gather_weights_finish(l, first_sems, passed_sems, shards, zones)
        x_next, proj, y = _forward_layer(l, xs[-1], vec, cps, wpool_b, win, wout, None)
        xs.append(x_next)
        projs.append(proj)
        ys.append(y)
        wins.append(win)
        wouts.append(wout)
    dx, loss_tile = _loss_head(xs[DEPTH], target)

    slab_rows = [None] * DEPTH
    grads_above = partials = received = None
    in_flight = []
    for l in reversed(range(DEPTH)):
        hosted = _sibling_exchange(grads_above, ALL_KINDS) if grads_above is not None else None
        dx, dproj, h_t, ycat_t, dy, gwpool, dcps, dvec, *from_sibling = _backward_layer(
            l, dx, ys[l], projs[l], xs[l], vec, cps, wpool_b, wouts[l], wins[l], hosted)
        if grads_above is not None:
            partials = _add_sibling_blocks(
                f"grad_add_sibling_{l + 1}", l + 1, grads_above, from_sibling, core, partials, ALL_KINDS)
            chips = _chips_exchange(l + 1, partials, received, ALL_KINDS)
            sems, partials, received, _ = _start_exchange(chips, f"grad_chips_start_{l + 1}")
            in_flight.append((chips, sems, l + 1))
        gwin, gwout = _weight_grads(l, h_t, dproj, ycat_t, dy)
        grads_above = [gwin, gwout, gwpool.reshape(POOL_ROWS, GROUP_D)]
        slab_rows[l] = jnp.concatenate(
            [dvec[0], dvec[1], dvec[2], dvec[3], dvec[4], dcps[3], dcps[0], dcps[1], dcps[2]])
    grad_x = dx[None]

    loss_row = jnp.pad(loss_tile[0], (0, SLAB_COLS - LANES))
    slab = jnp.stack(slab_rows + [loss_row] + [jnp.zeros((SLAB_COLS,), F32)] * (SLAB_ROWS - DEPTH - 1))
    slabs, *from_sibling = _all_gather_small(
        slab, "all_gather_small_grads", _sibling_exchange(grads_above, ALL_KINDS))
    partials = _add_sibling_blocks("grad_add_sibling_0", 0, grads_above, from_sibling, core, partials, ALL_KINDS)
    chips = _chips_exchange(0, partials, received, ALL_KINDS)
    sems, partials, received, token = _start_exchange(chips, "grad_chips_start_0")
    in_flight.append((chips, sems, 0))

    total = _sum_sources(slabs)
    loss = total[DEPTH, 0]
    o = 3 * D_MODEL
    g_b_ada = total[:DEPTH, :o]
    g_g_pre = total[:DEPTH, o:o + D_MODEL]
    g_g_post = total[:DEPTH, o + D_MODEL:o + 2 * D_MODEL]
    g_pool_scale = total[:DEPTH, o + 2 * D_MODEL:o + 2 * D_MODEL + POOL_W]
    g_conv_full = total[:DEPTH, o + 2 * D_MODEL + POOL_W:].reshape(DEPTH, 3, CONV_W)
    g_w_conv = lax.dynamic_slice_in_dim(g_conv_full, me * conv_shard, conv_shard, axis=2)

    dmod_all = slabs[:, :DEPTH, :o].reshape(N_DEV, DEPTH, N_DEV, W_IN_SHARD)
    dmod_cols = lax.dynamic_index_in_dim(dmod_all, me, axis=2, keepdims=False).transpose(1, 0, 2) + token[0, 0]
    g_w_ada, d_w_ada, nm_w_ada, nv_w_ada = _adamw_w_ada(w_ada, m_w_ada, v_w_ada, c_act.T, dmod_cols, None)

    after = [nv_w_ada]
    for chips, sems, l in in_flight:
        partials, received = _finish_exchange(chips, f"grad_chips_finish_{l}", sems, partials, received, after)
        after = []
    g_w_in, d_w_in, nm_w_in, nv_w_in = _adamw_reduced(
        "adamw_w_in", w_in, m_w_in, v_w_in, partials[0], received[0], chip, ROW_TILE)
    g_w_out, d_w_out, nm_w_out, nv_w_out = _adamw_reduced(
        "adamw_w_out", w_out, m_w_out, v_w_out, partials[1], received[1], chip, W_OUT_SHARD)
    g_pool_all = _all_gather_small(_reduce_w_pool(partials[2], received[2], chip), "all_gather_grad_w_pool")
    g_w_pool = g_pool_all.reshape(N_DEV, DEPTH, POOL_SHARD, GROUP_D).transpose(1, 0, 2, 3).reshape(w_pool.shape)

    flat2 = lambda a: a.reshape(-1, a.shape[-1])
    small = _adamw_small([
        (b_ada, g_b_ada, m_b_ada, v_b_ada),
        (g_pre, g_g_pre, m_g_pre, v_g_pre),
        (flat2(w_conv), flat2(g_w_conv), flat2(m_w_conv), flat2(v_w_conv)),
        (flat2(w_pool), flat2(g_w_pool), flat2(m_w_pool), flat2(v_w_pool)),
        (pool_scale, g_pool_scale, m_pool_scale, v_pool_scale),
        (g_post, g_g_post, m_g_post, v_g_post),
    ])
    (d_b_ada, nm_b_ada, nv_b_ada), (d_g_pre, nm_g_pre, nv_g_pre), conv_upd, pool_upd, \
        (d_ps, nm_ps, nv_ps), (d_g_post, nm_g_post, nv_g_post) = small
    d_w_conv, nm_w_conv, nv_w_conv = (a.reshape(w_conv.shape) for a in conv_upd)
    d_w_pool, nm_w_pool, nv_w_pool = (a.reshape(w_pool.shape) for a in pool_upd)

    return (loss, grad_x,
            g_w_ada, g_b_ada, g_g_pre, g_w_in, g_w_conv, g_w_pool, g_pool_scale, g_w_out, g_g_post,
            d_w_ada, d_b_ada, d_g_pre, d_w_in, d_w_conv, d_w_pool, d_ps, d_w_out, d_g_post,
            nm_w_ada, nm_b_ada, nm_g_pre, nm_w_in, nm_w_conv, nm_w_pool, nm_ps, nm_w_out, nm_g_post,
            nv_w_ada, nv_b_ada, nv_g_pre, nv_w_in, nv_w_conv, nv_w_pool, nv_ps, nv_w_out, nv_g_post)
```

```python
import jax
import jax.numpy as jnp
from jax import lax
from jax.experimental import pallas as pl
from jax.experimental.pallas import tpu as pltpu

F32 = jnp.float32
BF16 = jnp.bfloat16

D_MODEL = 1024
DEPTH = 4
CONV_W = 512
POOL_W = 512
POOL_WINDOWS = (2, 4, 8, 16)
GROUP_D = 128
IN_COLS = 4 * CONV_W + 2 * POOL_W
NORM_EPS = 1e-6

ADAM_LR = 0.001
ADAM_B1 = 0.9
ADAM_B2 = 0.999
ADAM_EPS = 1e-08
ADAM_WD = 0.01
ADAM_STEP = 10

N_DEV = 8
N_CHIP = 4
N_OTHER_CHIPS = N_CHIP - 1
MESH = pl.DeviceIdType.MESH
W_IN_SHARD = IN_COLS // N_DEV
W_OUT_SHARD = D_MODEL // N_DEV
POOL_ROWS = len(POOL_WINDOWS) * GROUP_D
POOL_SHARD = POOL_ROWS // N_DEV

SUBLANES = 8
LANES = 128
VMEM_LIMIT_BYTES = 56 * 1024 * 1024
ROW_TILE = 512
BWD_TILE = 256
GWIN_COLS = 768
GWOUT_COLS = 512
POOL_HALO = 16
CONV_HALO = SUBLANES

SLAB_COLS = 3 * D_MODEL + D_MODEL + D_MODEL + POOL_W + 3 * CONV_W
SLAB_ROWS = SUBLANES

HBM = pl.BlockSpec(memory_space=pl.ANY)


def _params(**kw):
    return pltpu.CompilerParams(vmem_limit_bytes=VMEM_LIMIT_BYTES, **kw)


def _sigmoid(v):
    return 1.0 / (1.0 + jnp.exp(-v))


def _dot(a, b):
    return jnp.dot(a, b, preferred_element_type=F32)


def _dot_tn(a, b):
    return lax.dot_general(a, b, (((0,), (0,)), ((), ())), preferred_element_type=F32)


def _dot_nt(a, b):
    return lax.dot_general(a, b, (((1,), (1,)), ((), ())), preferred_element_type=F32)


def _rows_from_before(v, k):
    return pltpu.roll(v, k, 0)


def _rows_from_after(v, k):
    return pltpu.roll(v, v.shape[0] - k, 0)


def _window_counts(t0, rows):
    return (lax.broadcasted_iota(jnp.int32, (rows, 1), 0) + (t0 + 1)).astype(F32)


def _split_proj(p32):
    cw = CONV_W
    return (p32[:, 0 * cw:1 * cw], p32[:, 1 * cw:2 * cw], p32[:, 2 * cw:3 * cw], p32[:, 3 * cw:4 * cw],
            p32[:, 4 * cw:4 * cw + POOL_W], p32[:, 4 * cw + POOL_W:])


def _mixer_forward(p32, z_halo, up_halo, cps, wpool_ref, t0):
    tm = p32.shape[0]
    u_a, b_a, c_a, g_a, u_p, g_p = _split_proj(p32)
    w0, w1, w2, ps = cps[0:1, :], cps[1:2, :], cps[2:3, :], cps[3:4, :]
    z = c_a * u_a
    zcat = jnp.concatenate([z_halo, z], axis=0)
    z1 = _rows_from_before(zcat, 1)[CONV_HALO:]
    z2 = _rows_from_before(zcat, 2)[CONV_HALO:]
    conv = w0 * z2 + w1 * z1 + w2 * z
    sig_a = _sigmoid(g_a)
    silu_a = g_a * sig_a
    y_a = b_a * conv * silu_a

    pcat = jnp.concatenate([up_halo, u_p], axis=0)
    counts = _window_counts(t0, tm)
    pooled, mixed, inv = [], [], []
    for g, w in enumerate(POOL_WINDOWS):
        cols = slice(g * GROUP_D, (g + 1) * GROUP_D)
        s = pcat[:, cols]
        step = 1
        while step < w:
            s = s + _rows_from_before(s, step)
            step *= 2
        inv_g = 1.0 / jnp.minimum(counts, float(w))
        pooled_g = s[POOL_HALO:] * inv_g - u_p[:, cols]
        pooled.append(pooled_g)
        inv.append(inv_g)
        mixed.append(_dot(pooled_g.astype(BF16), wpool_ref[g]))
    mixed = jnp.concatenate(mixed, axis=1)
    sig_p = _sigmoid(g_p)
    silu_p = g_p * sig_p
    y_p = mixed * ps * silu_p
    ycat = jnp.concatenate([y_a, y_p], axis=1)
    return dict(u_a=u_a, b_a=b_a, c_a=c_a, g_a=g_a, u_p=u_p, g_p=g_p, z=z, z1=z1, z2=z2, conv=conv, sig_a=sig_a,
                silu_a=silu_a, pooled=pooled, inv=inv, mixed=mixed, sig_p=sig_p, silu_p=silu_p, ycat=ycat,
                w0=w0, w1=w1, w2=w2, ps=ps)


def _layer_spec(shape, layer):
    nd = len(shape)
    return pl.BlockSpec((None,) + tuple(shape[1:]), lambda i, _l=layer, _n=nd: (_l,) + (0,) * (_n - 1))


def _whole_spec(shape):
    return pl.BlockSpec(tuple(shape), lambda i, _n=len(shape): (0,) * _n)


def _mesh_position():
    return lax.axis_index("x"), lax.axis_index("y"), lax.axis_index("c")


def _block_id(x, y, c):
    return 4 * x + 2 * y + c


def _other_chips(x, y):
    return [(x ^ 1, y), (x, y ^ 1), (x ^ 1, y ^ 1)]


def _col_block(ref, blk):
    return ref.at[:, pl.ds(pl.multiple_of(blk * W_IN_SHARD, LANES), W_IN_SHARD)]


def _row_block(rows):
    def block(ref, blk):
        return ref.at[pl.ds(pl.multiple_of(blk * rows, rows), rows), :]
    return block


_BLOCK_OF = (_col_block, _row_block(W_OUT_SHARD), _row_block(POOL_SHARD))
_BLOCK_SHAPES = ((D_MODEL, W_IN_SHARD), (W_OUT_SHARD, D_MODEL), (POOL_SHARD, GROUP_D))


class _Exchange:
    def __init__(self, inputs, out_shapes, aliases, sem_shapes, make):
        self.inputs, self.out_shapes, self.aliases, self.sem_shapes, self.make = (
            list(inputs), list(out_shapes), dict(aliases), list(sem_shapes), make)


_SEM =pl.BlockSpec(memory_space=pltpu.SEMAPHORE)
_DATAFLOW = pltpu.SideEffectType.DATAFLOW_SIDE_EFFECTING


def _start_exchange(exchange, name):
    n_in, n_out, n_sem = len(exchange.inputs), len(exchange.out_shapes), len(exchange.sem_shapes)
    sources = [i for i in range(n_in) if i not in exchange.aliases]
    aliases = {i: n_sem + k for k, i in enumerate(sources)}
    aliases.update({i: n_sem + len(sources) + o for i, o in exchange.aliases.items()})

    def body(*refs):
        in_refs = refs[:n_in]
        sems = refs[n_in:n_in + n_sem]
        out_refs = refs[n_in + n_sem + len(sources):n_in + n_sem + len(sources) + n_out]
        exchange.make(in_refs, out_refs, sems)[0]()
        refs[-1][...] = jnp.zeros_like(refs[-1])

    outs = pl.pallas_call(
        body, name=name, in_specs=[HBM] * n_in,
        out_specs=[_SEM] * n_sem + [HBM] * (len(sources) + n_out) + [pl.BlockSpec(memory_space=pltpu.VMEM)],
        out_shape=(exchange.sem_shapes + [pltpu.HBM(exchange.inputs[i].shape, exchange.inputs[i].dtype) for i in sources]
                   + [pltpu.HBM(s.shape, s.dtype) for s in exchange.out_shapes]
                   + [jax.ShapeDtypeStruct((SUBLANES, LANES), F32)]),
        input_output_aliases=aliases, compiler_params=_params(has_side_effects=_DATAFLOW),
    )(*exchange.inputs)
    return outs[:n_sem], outs[n_sem:n_sem + len(sources)], outs[n_sem + len(sources):-1], outs[-1]


def _finish_exchange(exchange, name, sems, sources, landing, after):
    n_src, n_out, n_sem = len(sources), len(landing), len(sems)
    n_in = len(exchange.inputs)
    source_at = [i for i in range(n_in) if i not in exchange.aliases]

    def body(*refs):
        src_refs, out_refs = refs[:n_src], refs[n_src:n_src + n_out]
        sem_refs = refs[n_src + n_out:n_src + n_out + n_sem]
        in_refs = [None] * n_in
        for k, i in enumerate(source_at):
            in_refs[i] = src_refs[k]
        for i, o in exchange.aliases.items():
            in_refs[i] = out_refs[o]
        exchange.make(in_refs, out_refs, sem_refs)[1]()

    arrays = list(sources) + list(landing)
    outs = pl.pallas_call(
        body, name=name, in_specs=[HBM] * len(arrays) + [_SEM] * n_sem + [HBM] * len(after),
        out_specs=[HBM] * len(arrays), out_shape=[pltpu.HBM(a.shape, a.dtype) for a in arrays],
        input_output_aliases={i: i for i in range(len(arrays))}, compiler_params=_params(has_side_effects=_DATAFLOW),
    )(*arrays, *sems, *after)
    return outs[:n_src], outs[n_src:]


N_GATHERED = 2
FIRST_COPIES = 1 + N_OTHER_CHIPS
_GATHERED_SHAPES = ((D_MODEL, IN_COLS), (D_MODEL, D_MODEL))


def _first_sem(layer, a, k):
    return (layer * N_GATHERED + a) * FIRST_COPIES + k


def _gather_copy(window_of, full_ref, blk, send_sem, recv_sem, to, src=None):
    window = window_of(full_ref, blk)
    return pltpu.make_async_remote_copy(
        src_ref=window if src is None else src, dst_ref=window, send_sem=send_sem, recv_sem=recv_sem,
        device_id=to, device_id_type=MESH)


def _first_copies(layer, shard_refs, full_refs, send_sems, recv_sems, local_sems):
    x, y, c = _mesh_position()
    me = _block_id(x, y, c)
    own, remote = [], []
    for a in range(N_GATHERED):
        shard = shard_refs[a].at[layer]
        own.append(pltpu.make_async_copy(
            shard, _BLOCK_OF[a](full_refs[a], me), local_sems.at[layer * N_GATHERED + a]))
        targets = [(x, y, 1 - c)] + [(*chip, c) for chip in _other_chips(x, y)]
        remote += [_gather_copy(_BLOCK_OF[a], full_refs[a], me, send_sems.at[_first_sem(layer, a, k)],
                                recv_sems.at[_first_sem(layer, a, k)], to, src=shard)
                   for k, to in enumerate(targets)]
    return own, remote


def _gather_weights_start(win_shards, wout_shards, after):
    n_first = DEPTH * N_GATHERED * FIRST_COPIES
    sem_shapes = [pltpu.SemaphoreType.DMA((n_first,)), pltpu.SemaphoreType.DMA((n_first,)),
                  pltpu.SemaphoreType.DMA((DEPTH * N_GATHERED,))]
    shards = [win_shards, wout_shards]

    def body(win_sh, wout_sh, *rest):
        send_sems, recv_sems, local_sems, win_thru, wout_thru, *landing = rest[len(after):]
        for layer in range(DEPTH):
            own, remote = _first_copies(layer, (win_sh, wout_sh), landing[N_GATHERED * layer:N_GATHERED * (layer + 1)],
                                        send_sems, recv_sems, local_sems)
            for cp in own + remote:
                cp.start()

    outs = pl.pallas_call(
        body, name="all_gather_weights_start", in_specs=[HBM] * (2 + len(after)),
        out_specs=[_SEM] * 3 + [HBM] * (2 + DEPTH * N_GATHERED),
        out_shape=(sem_shapes + [pltpu.HBM(s.shape, s.dtype) for s in shards]
                   + [pltpu.HBM(s, BF16) for _ in range(DEPTH) for s in _GATHERED_SHAPES]),
        input_output_aliases={0: 3, 1: 4}, compiler_params=_params(has_side_effects=_DATAFLOW),
    )(*shards, *after)
    landing = outs[5:]
    return outs[:3], outs[3:5], [landing[N_GATHERED * l:N_GATHERED * (l + 1)] for l in range(DEPTH)]


def _passed_on_copies(full_refs, send_sems, recv_sems, core_of_block):
    x, y, c = _mesh_position()
    return [_gather_copy(_BLOCK_OF[a], full_refs[a], _block_id(*chip, core_of_block),
                         send_sems.at[a * N_OTHER_CHIPS + j], recv_sems.at[a * N_OTHER_CHIPS + j], (x, y, 1 - c))
            for a in range(N_GATHERED) for j, chip in enumerate(_other_chips(x, y))]


def _gather_weights_pass_on(layer, first_recv_sems, landing, after):
    n = N_GATHERED * N_OTHER_CHIPS

    def body(win_ref, wout_ref, first_recv, *rest):
        send_sems, recv_sems = rest[len(after):len(after) + 2]
        x, y, c = _mesh_position()
        full_refs = (win_ref, wout_ref)
        passed = _passed_on_copies(full_refs, send_sems, recv_sems, c)
        for a in range(N_GATHERED):
            for j, chip in enumerate(_other_chips(x, y)):
                sem = _first_sem(layer, a, 1 + j)
                _gather_copy(_BLOCK_OF[a], full_refs[a], _block_id(*chip, c), first_recv.at[sem], first_recv.at[sem],
                             (x, y, c)).wait_recv()
                passed[a * N_OTHER_CHIPS + j].start()

    outs = pl.pallas_call(
        body, name=f"all_gather_weights_pass_on_{layer}", in_specs=[HBM] * N_GATHERED + [_SEM] + [HBM] * len(after),
        out_specs=[_SEM] * 2 + [HBM] * N_GATHERED,
        out_shape=[pltpu.SemaphoreType.DMA((n,)), pltpu.SemaphoreType.DMA((n,))]
        + [pltpu.HBM(a.shape, a.dtype) for a in landing],
        input_output_aliases={a: 2 + a for a in range(N_GATHERED)},
        compiler_params=_params(has_side_effects=_DATAFLOW),
    )(*landing, first_recv_sems, *after)
    return outs[:2], outs[2:]


def _gather_weights_finish(layer, first_sems, passed_sems, shards, landing):
    def body(win_ref, wout_ref, first_send, first_recv, local_sems, passed_send, passed_recv, win_sh, wout_sh, *thru):
        x, y, c = _mesh_position()
        full_refs = (win_ref, wout_ref)
        own, remote = _first_copies(layer, (win_sh, wout_sh), full_refs, first_send, first_recv, local_sems)
        for a in range(N_GATHERED):
            sem = _first_sem(layer, a, 0)
            _gather_copy(_BLOCK_OF[a], full_refs[a], _block_id(x, y, 1 - c), first_recv.at[sem], first_recv.at[sem],
                         (x, y, c)).wait_recv()
        for cp in _passed_on_copies(full_refs, passed_send, passed_recv, 1 - c):
            cp.wait_recv()
        for cp in remote + _passed_on_copies(full_refs, passed_send, passed_recv, c):
            cp.wait_send()
        for cp in own:
            cp.wait()

    return pl.pallas_call(
        body, name=f"all_gather_weights_finish_{layer}", in_specs=[HBM] * N_GATHERED + [_SEM] * 5 + [HBM] * 2,
        out_specs=[HBM] * N_GATHERED, out_shape=[pltpu.HBM(a.shape, a.dtype) for a in landing],
        input_output_aliases={a: a for a in range(N_GATHERED)}, compiler_params=_params(has_side_effects=_DATAFLOW),
    )(*landing, *first_sems, *passed_sems, *shards)


ALL_KINDS = (0, 1, 2)


def _sibling_exchange(grads, kinds):
    n_arr = len(grads)

    def make(in_refs, out_refs, sems):
        send_sems, recv_sems = sems
        x, y, c = _mesh_position()
        copies = [pltpu.make_async_remote_copy(
            src_ref=_BLOCK_OF[kinds[a]](in_refs[a], 2 * q + (1 - c)), dst_ref=out_refs[a].at[q],
            send_sem=send_sems.at[a, q], recv_sem=recv_sems.at[a, q], device_id=(x, y, 1 - c), device_id_type=MESH)
            for a in range(n_arr) for q in range(N_CHIP)]

        def start():
            for cp in copies:
                cp.start()

        def finish():
            for cp in copies:
                cp.wait_recv()
            for cp in copies:
                cp.wait_send()

        return start, finish

    return _Exchange(
        grads, [jax.ShapeDtypeStruct((N_CHIP,) + _BLOCK_SHAPES[k], BF16) for k in kinds], {},
        [pltpu.SemaphoreType.DMA((n_arr, N_CHIP)), pltpu.SemaphoreType.DMA((n_arr, N_CHIP))], make)


def _chips_exchange(layer, partials, received, kinds):
    n_arr = len(partials)

    def make(in_refs, out_refs, sems):
        send_sems, recv_sems = sems
        x, y, c = _mesh_position()
        copies = [pltpu.make_async_remote_copy(
            src_ref=in_refs[a].at[2 * qx + qy, layer], dst_ref=out_refs[a].at[j, layer],
            send_sem=send_sems.at[a * N_OTHER_CHIPS + j], recv_sem=recv_sems.at[a * N_OTHER_CHIPS + j],
            device_id=(qx, qy, c), device_id_type=MESH)
            for a in range(n_arr) for j, (qx, qy) in enumerate(_other_chips(x, y))]

        def start():
            for cp in copies:
                cp.start()

        def finish():
            for cp in copies:
                cp.wait_recv()
            for cp in copies:
                cp.wait_send()

        return start, finish

    inputs = list(partials)
    aliases = {}
    if received is not None:
        inputs += list(received)
        aliases = {n_arr + a: a for a in range(n_arr)}
    return _Exchange(
        inputs, [jax.ShapeDtypeStruct((N_OTHER_CHIPS, DEPTH) + _BLOCK_SHAPES[k], BF16) for k in kinds], aliases,
        [pltpu.SemaphoreType.DMA((n_arr * N_OTHER_CHIPS,)), pltpu.SemaphoreType.DMA((n_arr * N_OTHER_CHIPS,))], make)


def _all_gather_exchange(v):
    def make(in_refs, out_refs, sems):
        send_sems, recv_sems, local_sem = sems
        x, y, c = _mesh_position()
        me = _block_id(x, y, c)
        own = pltpu.make_async_copy(in_refs[0], out_refs[0].at[me], local_sem.at[0])
        sends, arrivals = [], []
        for k in range(1, N_DEV):
            px, py, pc = x ^ ((k >> 2) & 1), y ^ ((k >> 1) & 1), c ^ (k & 1)
            sends.append(pltpu.make_async_remote_copy(
                src_ref=in_refs[0], dst_ref=out_refs[0].at[me], send_sem=send_sems.at[k - 1],
                recv_sem=recv_sems.at[k - 1], device_id=(px, py, pc), device_id_type=MESH))
            arrivals.append(pltpu.make_async_remote_copy(
                src_ref=in_refs[0], dst_ref=out_refs[0].at[_block_id(px, py, pc)], send_sem=send_sems.at[k - 1],
                recv_sem=recv_sems.at[k - 1], device_id=(x, y, c), device_id_type=MESH))

        def start():
            for cp in [own] + sends:
                cp.start()

        def finish():
            for cp in arrivals:
                cp.wait_recv()
            for cp in sends:
                cp.wait_send()
            own.wait()

        return start, finish

    return _Exchange(
        [v], [jax.ShapeDtypeStruct((N_DEV,) + v.shape, v.dtype)], {},
        [pltpu.SemaphoreType.DMA((N_DEV - 1,)), pltpu.SemaphoreType.DMA((N_DEV - 1,)),
         pltpu.SemaphoreType.DMA((1,))], make)


def _host(exchange, args, in_specs, out_shape, out_specs, scratch):
    n_own = (len(args), len(out_shape), len(scratch))
    if exchange is None:
        return {}, lambda refs: (refs, None)
    n_ex = (len(exchange.inputs), len(exchange.out_shapes), len(exchange.sem_shapes))
    aliases = {n_own[0] + i: n_own[1] + o for i, o in exchange.aliases.items()}
    args += exchange.inputs
    in_specs += [HBM] * n_ex[0]
    out_shape += exchange.out_shapes
    out_specs += [HBM] * n_ex[1]
    scratch += exchange.sem_shapes

    def split(refs):
        own, theirs, at = [], [], 0
        for mine, ex in zip(n_own, n_ex):
            own += refs[at:at + mine]
            theirs.append(refs[at + mine:at + mine + ex])
            at += mine + ex
        return own, exchange.make(*theirs)

    return aliases, split


def _all_gather_small(v, name, exchange=None):
    vmem = pl.BlockSpec(memory_space=pltpu.VMEM)
    args, in_specs = [v], [vmem]
    out_shape, out_specs = [jax.ShapeDtypeStruct((N_DEV,) + v.shape, v.dtype)], [vmem]
    scratch = [pltpu.SemaphoreType.DMA((N_DEV - 1,)), pltpu.SemaphoreType.DMA((N_DEV - 1,))]
    aliases, split = _host(exchange, args, in_specs, out_shape, out_specs, scratch)

    def body(*refs):
        (v_ref, out_ref, send_sems, recv_sems), hosted = split(refs)
        if hosted is not None:
            hosted[0]()
        x, y, c = _mesh_position()
        me = _block_id(x, y, c)
        out_ref[me] = v_ref[...]
        sends = []
        for k in range(1, N_DEV):
            px, py, pc = x ^ ((k >> 2) & 1), y ^ ((k >> 1) & 1), c ^ (k & 1)
            send = pltpu.make_async_remote_copy(
                src_ref=v_ref, dst_ref=out_ref.at[me], send_sem=send_sems.at[k - 1], recv_sem=recv_sems.at[k - 1],
                device_id=(px, py, pc), device_id_type=MESH)
            send.start()
            sends.append((send, _block_id(px, py, pc)))
        for k, (send, peer) in enumerate(sends):
            pltpu.make_async_remote_copy(
                src_ref=v_ref, dst_ref=out_ref.at[peer], send_sem=send_sems.at[k], recv_sem=recv_sems.at[k],
                device_id=(x, y, c), device_id_type=MESH).wait_recv()
        for send, _ in sends:
            send.wait_send()
        if hosted is not None:
            hosted[1]()

    outs = pl.pallas_call(
        body, name=name, in_specs=in_specs, out_specs=out_specs, out_shape=out_shape, scratch_shapes=scratch,
        input_output_aliases=aliases, compiler_params=_params(),
    )(*args)
    return outs[0] if exchange is None else outs


def _forward_layer(layer, x, vec, cps, wpool, win, wout, exchange):
    t_len = x.shape[0]
    n_tiles = t_len // ROW_TILE
    row = lambda cols: pl.BlockSpec((ROW_TILE, cols), lambda i: (i, 0))
    args = [x, vec, cps, wpool, win, wout]
    in_specs = [row(D_MODEL), _layer_spec(vec.shape, layer), _layer_spec(cps.shape, layer),
                _layer_spec(wpool.shape, layer), _whole_spec(win.shape), _whole_spec(wout.shape)]
    out_shape = [jax.ShapeDtypeStruct((t_len, D_MODEL), F32), jax.ShapeDtypeStruct((t_len, IN_COLS), BF16),
                 jax.ShapeDtypeStruct((t_len, D_MODEL), BF16)]
    out_specs = [row(D_MODEL), row(IN_COLS), row(D_MODEL)]
    scratch = [pltpu.VMEM((CONV_HALO, CONV_W), F32), pltpu.VMEM((POOL_HALO, POOL_W), F32)]
    aliases, split = _host(exchange, args, in_specs, out_shape, out_specs, scratch)

    def body(*refs):
        (x_ref, vec_ref, cps_ref, wpool_ref, win_ref, wout_ref, xo_ref, proj_ref, y_ref, zc_ref, pc_ref), hosted = (
            split(refs))
        i = pl.program_id(0)

        @pl.when(i == 0)
        def _():
            if hosted is not None:
                hosted[0]()
            zc_ref[...] = jnp.zeros_like(zc_ref)
            pc_ref[...] = jnp.zeros_like(pc_ref)

        x_t = x_ref[...]
        shift, scale, gate = vec_ref[0:1, :], vec_ref[1:2, :], vec_ref[2:3, :]
        g_pre, g_post = vec_ref[3:4, :], vec_ref[4:5, :]
        rx = lax.rsqrt(jnp.mean(x_t * x_t, axis=-1, keepdims=True) + NORM_EPS)
        h = (x_t * rx) * g_pre * (1.0 + scale) + shift
        proj_b = _dot(h.astype(BF16), win_ref[...]).astype(BF16)
        proj_ref[...] = proj_b
        mx = _mixer_forward(proj_b.astype(F32), zc_ref[...], pc_ref[...], cps_ref[...], wpool_ref, i * ROW_TILE)
        zc_ref[...] = mx["z"][ROW_TILE - CONV_HALO:]
        pc_ref[...] = mx["u_p"][ROW_TILE - POOL_HALO:]
        y_b = _dot(mx["ycat"].astype(BF16), wout_ref[...]).astype(BF16)
        y_ref[...] = y_b
        y_t = y_b.astype(F32)
        ry = lax.rsqrt(jnp.mean(y_t * y_t, axis=-1, keepdims=True) + NORM_EPS)
        xo_ref[...] = x_t + gate * (y_t * ry * g_post)

        if hosted is not None:
            pl.when(i == n_tiles - 1)(hosted[1])

    return pl.pallas_call(
        body, name=f"forward_layer_{layer}", grid=(n_tiles,), in_specs=in_specs, out_specs=out_specs,
        out_shape=out_shape, scratch_shapes=scratch, input_output_aliases=aliases,
        compiler_params=_params(dimension_semantics=("arbitrary",)),
    )(*args)


def _loss_head(x_final, target):
    t_len = x_final.shape[0]
    n_tiles = t_len // ROW_TILE

    def body(x_ref, t_ref, dx_ref, loss_ref):
        @pl.when(pl.program_id(0) == 0)
        def _():
            loss_ref[...] = jnp.zeros_like(loss_ref)

        err = x_ref[...] - t_ref[...]
        dx_ref[...] = err * (1.0 / D_MODEL)
        loss_ref[...] += jnp.sum(err * err) * (0.5 / D_MODEL)

    row = pl.BlockSpec((ROW_TILE, D_MODEL), lambda i: (i, 0))
    return pl.pallas_call(
        body, name="loss_head", grid=(n_tiles,), in_specs=[row, row],
        out_specs=[row, pl.BlockSpec((SUBLANES, LANES), lambda i: (0, 0))],
        out_shape=[jax.ShapeDtypeStruct((t_len, D_MODEL), F32), jax.ShapeDtypeStruct((SUBLANES, LANES), F32)],
        compiler_params=_params(dimension_semantics=("arbitrary",)),
    )(x_final, target)


def _backward_layer(layer, dxo, y, proj, x, vec, cps, wpool, wout, win, exchange):
    t_len = dxo.shape[0]
    n_tiles = t_len // BWD_TILE
    halo_per_tile = BWD_TILE // POOL_HALO
    rev = lambda cols: pl.BlockSpec((BWD_TILE, cols), lambda i: (n_tiles - 1 - i, 0))
    rev_t = pl.BlockSpec((D_MODEL, BWD_TILE), lambda i: (0, n_tiles - 1 - i))
    halo_spec = pl.BlockSpec(
        (POOL_HALO, IN_COLS), lambda i: (jnp.maximum((n_tiles - 1 - i) * halo_per_tile - 1, 0), 0))
    gwpool_shape = (len(POOL_WINDOWS), GROUP_D, GROUP_D)
    args = [dxo, y, proj, proj, x, vec, cps, wpool, wout, win]
    in_specs = [rev(D_MODEL), rev(D_MODEL), rev(IN_COLS), halo_spec, rev(D_MODEL), _layer_spec(vec.shape, layer),
                _layer_spec(cps.shape, layer), _layer_spec(wpool.shape, layer), _whole_spec(wout.shape),
                _whole_spec(win.shape)]
    out_shape = [jax.ShapeDtypeStruct((t_len, D_MODEL), F32), jax.ShapeDtypeStruct((t_len, IN_COLS), BF16),
                 jax.ShapeDtypeStruct((D_MODEL, t_len), BF16), jax.ShapeDtypeStruct((D_MODEL, t_len), BF16),
                 jax.ShapeDtypeStruct((t_len, D_MODEL), BF16), jax.ShapeDtypeStruct(gwpool_shape, BF16),
                 jax.ShapeDtypeStruct((SUBLANES, CONV_W), F32), jax.ShapeDtypeStruct((SUBLANES, D_MODEL), F32)]
    out_specs = [rev(D_MODEL), rev(IN_COLS), rev_t, rev_t, rev(D_MODEL), _whole_spec(gwpool_shape),
                 _whole_spec((SUBLANES, CONV_W)), _whole_spec((SUBLANES, D_MODEL))]
    scratch = [pltpu.VMEM(gwpool_shape, F32), pltpu.VMEM((CONV_HALO, CONV_W), F32),
               pltpu.VMEM((POOL_HALO, POOL_W), F32)]
    aliases, split = _host(exchange, args, in_specs, out_shape, out_specs, scratch)

    def body(*refs):
        (dxo_ref, y_ref, proj_ref, projh_ref, x_ref, vec_ref, cps_ref, wpool_ref, wout_ref, win_ref,
         dx_ref, dproj_ref, ht_ref, ycatt_ref, dy_ref, gwpool_ref, dcps_ref, dvec_ref,
         gwpool_acc, dcc_ref, qc_ref), hosted = split(refs)
        i = pl.program_id(0)
        tile = n_tiles - 1 - i

        @pl.when(i == 0)
        def _():
            if hosted is not None:
                hosted[0]()
            gwpool_acc[...] = jnp.zeros_like(gwpool_acc)
            dcps_ref[...] = jnp.zeros_like(dcps_ref)
            dvec_ref[...] = jnp.zeros_like(dvec_ref)
            dcc_ref[...] = jnp.zeros_like(dcc_ref)
            qc_ref[...] = jnp.zeros_like(qc_ref)

        shift, scale, gate = vec_ref[0:1, :], vec_ref[1:2, :], vec_ref[2:3, :]
        g_pre, g_post = vec_ref[3:4, :], vec_ref[4:5, :]

        dxo_t = dxo_ref[...]
        y_t = y_ref[...].astype(F32)
        ry = lax.rsqrt(jnp.mean(y_t * y_t, axis=-1, keepdims=True) + NORM_EPS)
        yh = y_t * ry
        dxy = dxo_t * yh
        dvec_ref[2:3, :] += jnp.sum(dxy * g_post, axis=0, keepdims=True)
        dvec_ref[4:5, :] += jnp.sum(dxy * gate, axis=0, keepdims=True)
        dyh = dxo_t * (gate * g_post)
        dy_b = (ry * (dyh - yh * jnp.mean(dyh * yh, axis=-1, keepdims=True))).astype(BF16)
        dy_ref[...] = dy_b

        halo = jnp.where(tile > 0, projh_ref[...].astype(F32), 0.0)
        hu_a, _, hc_a, _, hu_p, _ = _split_proj(halo)
        z_halo = (hc_a * hu_a)[POOL_HALO - CONV_HALO:]
        mx = _mixer_forward(proj_ref[...].astype(F32), z_halo, hu_p, cps_ref[...], wpool_ref, tile * BWD_TILE)
        ycatt_ref[...] = mx["ycat"].T.astype(BF16)

        dycat = _dot_nt(dy_b, wout_ref[...])
        dy_a, dy_p = dycat[:, :CONV_W], dycat[:, CONV_W:]

        t_a = dy_a * mx["silu_a"]
        db_a = t_a * mx["conv"]
        dconv = t_a * mx["b_a"]
        dg_a = dy_a * mx["b_a"] * mx["conv"] * (mx["sig_a"] * (1.0 + mx["g_a"] * (1.0 - mx["sig_a"])))
        dccat = jnp.concatenate([dconv, dcc_ref[...]], axis=0)
        dc1 = _rows_from_after(dccat, 1)[:BWD_TILE]
        dc2 = _rows_from_after(dccat, 2)[:BWD_TILE]
        dz = mx["w2"] * dconv + mx["w1"] * dc1 + mx["w0"] * dc2
        dcc_ref[...] = dconv[:CONV_HALO]
        dcps_ref[0:1, :] += jnp.sum(dconv * mx["z2"], axis=0, keepdims=True)
        dcps_ref[1:2, :] += jnp.sum(dconv * mx["z1"], axis=0, keepdims=True)
        dcps_ref[2:3, :] += jnp.sum(dconv * mx["z"], axis=0, keepdims=True)
        du_a = dz * mx["c_a"]
        dc_a = dz * mx["u_a"]

        t_p = dy_p * mx["silu_p"]
        dcps_ref[3:4, :] += jnp.sum(t_p * mx["mixed"], axis=0, keepdims=True)
        dmixed = (t_p * mx["ps"]).astype(BF16)
        dg_p = dy_p * mx["mixed"] * mx["ps"] * (mx["sig_p"] * (1.0 + mx["g_p"] * (1.0 - mx["sig_p"])))
        du_p, q_head = [], []
        for g, w in enumerate(POOL_WINDOWS):
            cols = slice(g * GROUP_D, (g + 1) * GROUP_D)
            dm_g = dmixed[:, cols]
            dpooled_g = _dot_nt(dm_g, wpool_ref[g])
            gwpool_acc[g] += _dot_tn(mx["pooled"][g].astype(BF16), dm_g)
            q_g = dpooled_g * mx["inv"][g]
            q_head.append(q_g[:POOL_HALO])
            s = jnp.concatenate([q_g, qc_ref[:, cols]], axis=0)
            step = 1
            while step < w:
                s = s + _rows_from_after(s, step)
                step *= 2
            du_p.append(s[:BWD_TILE] - dpooled_g)
        qc_ref[...] = jnp.concatenate(q_head, axis=1)
        dproj_b = jnp.concatenate([du_a, db_a, dc_a, dg_a] + du_p + [dg_p], axis=1).astype(BF16)
        dproj_ref[...] = dproj_b

        x_t = x_ref[...]
        rx = lax.rsqrt(jnp.mean(x_t * x_t, axis=-1, keepdims=True) + NORM_EPS)
        xn = x_t * rx
        mod_scale = 1.0 + scale
        ht_ref[...] = (xn * g_pre * mod_scale + shift).T.astype(BF16)
        dh = _dot_nt(dproj_b, win_ref[...])
        dvec_ref[0:1, :] += jnp.sum(dh, axis=0, keepdims=True)
        dhx = dh * xn
        dvec_ref[1:2, :] += jnp.sum(dhx * g_pre, axis=0, keepdims=True)
        dvec_ref[3:4, :] += jnp.sum(dhx * mod_scale, axis=0, keepdims=True)
        dxn = dh * (g_pre * mod_scale)
        dx_ref[...] = dxo_t + rx * (dxn - xn * jnp.mean(dxn * xn, axis=-1, keepdims=True))

        @pl.when(i == n_tiles - 1)
        def _():
            gwpool_ref[...] = gwpool_acc[...].astype(BF16)
            if hosted is not None:
                hosted[1]()

    return pl.pallas_call(
        body, name=f"backward_layer_{layer}", grid=(n_tiles,), in_specs=in_specs, out_specs=out_specs,
        out_shape=out_shape, scratch_shapes=scratch, input_output_aliases=aliases,
        compiler_params=_params(dimension_semantics=("arbitrary",)),
    )(*args)


def _weight_grads(layer, h_t, dproj, ycat_t, dy, exchange):
    t_len = dy.shape[0]
    n_in, n_out = IN_COLS // GWIN_COLS, D_MODEL // GWOUT_COLS
    args = [h_t, dproj, ycat_t, dy]
    in_specs = [_whole_spec(h_t.shape),
                pl.BlockSpec((t_len, GWIN_COLS), lambda s: (0, jnp.minimum(s, n_in - 1))),
                _whole_spec(ycat_t.shape),
                pl.BlockSpec((t_len, GWOUT_COLS), lambda s: (0, jnp.maximum(s - n_in, 0)))]
    out_shape = [jax.ShapeDtypeStruct((D_MODEL, IN_COLS), BF16), jax.ShapeDtypeStruct((D_MODEL, D_MODEL), BF16)]
    out_specs = [pl.BlockSpec((D_MODEL, GWIN_COLS), lambda s: (0, jnp.minimum(s, n_in - 1))),
                 pl.BlockSpec((D_MODEL, GWOUT_COLS), lambda s: (0, jnp.maximum(s - n_in, 0)))]
    scratch = []
    aliases, split = _host(exchange, args, in_specs, out_shape, out_specs, scratch)

    def body(*refs):
        (ht_ref, dproj_ref, ycatt_ref, dy_ref, gwin_ref, gwout_ref), hosted = split(refs)
        s = pl.program_id(0)
        if hosted is not None:
            pl.when(s == 0)(hosted[0])

        @pl.when(s < n_in)
        def _():
            gwin_ref[...] = _dot(ht_ref[...], dproj_ref[...]).astype(BF16)

        @pl.when(s >= n_in)
        def _():
            gwout_ref[...] = _dot(ycatt_ref[...], dy_ref[...]).astype(BF16)

        if hosted is not None:
            pl.when(s == n_in + n_out - 1)(hosted[1])

    return pl.pallas_call(
        body, name=f"weight_grads_{layer}", grid=(n_in + n_out,), in_specs=in_specs, out_specs=out_specs,
        out_shape=out_shape, scratch_shapes=scratch, input_output_aliases=aliases,
        compiler_params=_params(dimension_semantics=("arbitrary",)),
    )(*args)


def _add_sibling_blocks(name, layer, grads, received, core, partials, kinds):
    n_arr = len(grads)

    def body(core_ref, *refs):
        mine, theirs, outs = refs[:n_arr], refs[n_arr:2 * n_arr], refs[-n_arr:]
        for a in range(n_arr):
            outs[a][...] = (mine[a][...].astype(F32) + theirs[a][...].astype(F32)).astype(BF16)

    own_of_kind = [
        pl.BlockSpec((D_MODEL, W_IN_SHARD), lambda q, core_ref: (0, 2 * q + core_ref[0])),
        pl.BlockSpec((W_OUT_SHARD, D_MODEL), lambda q, core_ref: (2 * q + core_ref[0], 0)),
        pl.BlockSpec((POOL_SHARD, GROUP_D), lambda q, core_ref: (2 * q + core_ref[0], 0)),
    ]
    shapes = [_BLOCK_SHAPES[k] for k in kinds]
    recv_specs = [pl.BlockSpec((None,) + s, lambda q, core_ref: (q, 0, 0)) for s in shapes]
    out_specs = [pl.BlockSpec((None, None) + s, lambda q, core_ref: (q, layer, 0, 0)) for s in shapes]
    args = [core, *grads, *received]
    in_specs = [own_of_kind[k] for k in kinds] + recv_specs
    aliases = {}
    if partials is not None:
        aliases = {len(args) + a: a for a in range(n_arr)}
        args += list(partials)
        in_specs += [HBM] * n_arr
    return pl.pallas_call(
        body, name=name,
        grid_spec=pltpu.PrefetchScalarGridSpec(
            num_scalar_prefetch=1, grid=(N_CHIP,), in_specs=in_specs, out_specs=out_specs),
        out_shape=[jax.ShapeDtypeStruct((N_CHIP, DEPTH) + s, BF16) for s in shapes],
        input_output_aliases=aliases,
        compiler_params=_params(dimension_semantics=("arbitrary",)),
    )(*args)


def _modulation_columns(c_all, w_ada):
    def body(c_ref, w_ref, cact_ref, out_ref):
        c_t = c_ref[...]
        c_act = c_t * _sigmoid(c_t)
        cact_ref[...] = c_act
        out_ref[...] = jnp.dot(c_act, w_ref[...], preferred_element_type=F32, precision=lax.Precision.HIGHEST)

    return pl.pallas_call(
        body, name="modulation_columns", grid=(DEPTH,),
        in_specs=[pl.BlockSpec((N_DEV, D_MODEL), lambda l: (0, 0)),
                  pl.BlockSpec((None, D_MODEL, W_IN_SHARD), lambda l: (l, 0, 0))],
        out_specs=[pl.BlockSpec((N_DEV, D_MODEL), lambda l: (0, 0)),
                   pl.BlockSpec((N_DEV, W_IN_SHARD), lambda l: (0, l))],
        out_shape=[jax.ShapeDtypeStruct((N_DEV, D_MODEL), F32),
                   jax.ShapeDtypeStruct((N_DEV, DEPTH * W_IN_SHARD), F32)],
        compiler_params=_params(dimension_semantics=("arbitrary",)),
    )(c_all, w_ada)


def _adamw(w, g, m, v):
    m_new = ADAM_B1 * m + (1.0 - ADAM_B1) * g
    v_new = ADAM_B2 * v + (1.0 - ADAM_B2) * (g * g)
    m_hat = m_new / (1.0 - ADAM_B1 ** ADAM_STEP)
    v_hat = v_new / (1.0 - ADAM_B2 ** ADAM_STEP)
    delta = -ADAM_LR * (m_hat / (jnp.sqrt(v_hat) + ADAM_EPS) + ADAM_WD * w)
    return delta, m_new, v_new


def _adamw_w_ada(w, m, v, c_act_t, dmod_cols, exchange):
    big = pl.BlockSpec((None, D_MODEL, W_IN_SHARD), lambda l: (l, 0, 0))
    args = [w, m, v, c_act_t, dmod_cols]
    in_specs = [big, big, big, pl.BlockSpec((D_MODEL, N_DEV), lambda l: (0, 0)),
                pl.BlockSpec((None, N_DEV, W_IN_SHARD), lambda l: (l, 0, 0))]
    out_shape, out_specs, scratch = [jax.ShapeDtypeStruct(w.shape, F32)] * 4, [big] * 4, []
    aliases, split = _host(exchange, args, in_specs, out_shape, out_specs, scratch)

    def body(*refs):
        (w_ref, m_ref, v_ref, ct_ref, dm_ref, g_ref, d_ref, mo_ref, vo_ref), hosted = split(refs)
        if hosted is not None:
            pl.when(pl.program_id(0) == 0)(hosted[0])
        g = ct_ref[:, 0:1] * dm_ref[0:1, :]
        for b in range(1, N_DEV):
            g = g + ct_ref[:, b:b + 1] * dm_ref[b:b + 1, :]
        g_ref[...] = g
        d_ref[...], mo_ref[...], vo_ref[...] = _adamw(w_ref[...], g, m_ref[...], v_ref[...])
        if hosted is not None:
            pl.when(pl.program_id(0) == DEPTH - 1)(hosted[1])

    return pl.pallas_call(
        body, name="adamw_w_ada", grid=(DEPTH,), in_specs=in_specs, out_specs=out_specs, out_shape=out_shape,
        scratch_shapes=scratch, input_output_aliases=aliases,
        compiler_params=_params(dimension_semantics=("arbitrary",)),
    )(*args)


def _sum_chip_partials(own_ref, recv_ref):
    g = own_ref[...].astype(F32)
    for j in range(N_OTHER_CHIPS):
        g = g + recv_ref[j].astype(F32)
    return g


def _partial_specs(row_tile, cols, first_layer=0):
    own = pl.BlockSpec((None, None, row_tile, cols), lambda l, r, chip_ref: (chip_ref[0], first_layer + l, r, 0))
    recv = pl.BlockSpec((N_OTHER_CHIPS, None, row_tile, cols), lambda l, r, chip_ref: (0, first_layer + l, r, 0))
    return own, recv


def _adamw_reduced(name, w, m, v, partial, received, chip, row_tile, layers, continued):
    depth, rows, cols = w.shape
    first, stop = layers

    def body(chip_ref, w_ref, m_ref, v_ref, own_ref, recv_ref, *rest):
        g_ref, d_ref, mo_ref, vo_ref = rest[-4:]
        g = _sum_chip_partials(own_ref, recv_ref)
        g_ref[...] = g
        d_ref[...], mo_ref[...], vo_ref[...] = _adamw(w_ref[...], g, m_ref[...], v_ref[...])

    blk = pl.BlockSpec((None, row_tile, cols), lambda l, r, chip_ref: (first + l, r, 0))
    args = [chip, w, m, v, partial, received]
    in_specs = [blk, blk, blk, *_partial_specs(row_tile, cols, first)]
    aliases = {}
    if continued is not None:
        aliases = {len(args) + k: k for k in range(4)}
        args += list(continued)
        in_specs += [HBM] * 4
    return pl.pallas_call(
        body, name=name,
        grid_spec=pltpu.PrefetchScalarGridSpec(
            num_scalar_prefetch=1, grid=(stop - first, rows // row_tile), in_specs=in_specs, out_specs=[blk] * 4),
        out_shape=[jax.ShapeDtypeStruct(w.shape, F32)] * 4, input_output_aliases=aliases,
        compiler_params=_params(dimension_semantics=("arbitrary", "arbitrary")),
    )(*args)


def _reduce_w_pool(partial, received, chip):
    def body(chip_ref, own_ref, recv_ref, g_ref):
        g_ref[...] = _sum_chip_partials(own_ref, recv_ref)

    return pl.pallas_call(
        body, name="reduce_w_pool",
        grid_spec=pltpu.PrefetchScalarGridSpec(
            num_scalar_prefetch=1, grid=(DEPTH, 1), in_specs=list(_partial_specs(POOL_SHARD, GROUP_D)),
            out_specs=pl.BlockSpec((POOL_SHARD, GROUP_D), lambda l, r, chip_ref: (l, 0))),
        out_shape=jax.ShapeDtypeStruct((DEPTH * POOL_SHARD, GROUP_D), F32),
        compiler_params=_params(dimension_semantics=("arbitrary", "arbitrary")),
    )(chip, partial, received)


def _adamw_small(params):
    n = len(params)

    def body(*refs):
        ins, outs = refs[:4 * n], refs[4 * n:]
        for p in range(n):
            w_ref, g_ref, m_ref, v_ref = ins[4 * p:4 * p + 4]
            d_ref, mo_ref, vo_ref = outs[3 * p:3 * p + 3]
            d_ref[...], mo_ref[...], vo_ref[...] = _adamw(w_ref[...], g_ref[...], m_ref[...], v_ref[...])

    vmem = pl.BlockSpec(memory_space=pltpu.VMEM)
    flat = [a for group in params for a in group]
    out_shape = [jax.ShapeDtypeStruct(group[0].shape, F32) for group in params for _ in range(3)]
    outs = pl.pallas_call(
        body, name="adamw_small", in_specs=[vmem] * len(flat), out_specs=[vmem] * len(out_shape),
        out_shape=out_shape, compiler_params=_params(),
    )(*flat)
    return [tuple(outs[3 * p:3 * p + 3]) for p in range(n)]


def _sum_sources(slabs):
    def body(s_ref, o_ref):
        acc = s_ref[0]
        for b in range(1, N_DEV):
            acc = acc + s_ref[b]
        o_ref[...] = acc

    vmem = pl.BlockSpec(memory_space=pltpu.VMEM)
    return pl.pallas_call(
        body, name="sum_small_grads", in_specs=[vmem], out_specs=vmem,
        out_shape=jax.ShapeDtypeStruct(slabs.shape[1:], F32), compiler_params=_params(),
    )(slabs)


def _to_bf16(a, name):
    def body(a_ref, o_ref):
        o_ref[...] = a_ref[...].astype(BF16)

    spec = pl.BlockSpec((None,) + a.shape[1:], lambda l: (l, 0, 0))
    return pl.pallas_call(
        body, name=name, grid=(a.shape[0],), in_specs=[spec], out_specs=spec,
        out_shape=jax.ShapeDtypeStruct(a.shape, BF16), compiler_params=_params(dimension_semantics=("arbitrary",)),
    )(a)


def kernel(x, c, w_ada, b_ada, g_pre, w_in, w_conv, w_pool, pool_scale, w_out, g_post, loss_target, m_w_ada, m_b_ada, m_g_pre, m_w_in, m_w_conv, m_w_pool, m_pool_scale, m_w_out, m_g_post, v_w_ada, v_b_ada, v_g_pre, v_w_in, v_w_conv, v_w_pool, v_pool_scale, v_w_out, v_g_post):
    mx, my, mc = _mesh_position()
    me = _block_id(mx, my, mc)
    chip = (2 * mx + my).astype(jnp.int32).reshape(1)
    core = mc.astype(jnp.int32).reshape(1)
    x0 = x[0]
    target = loss_target[0]
    conv_shard = w_conv.shape[-1]

    own_small = jnp.concatenate([c, w_conv.reshape(1, DEPTH * 3 * conv_shard)], axis=1)
    all_small = _all_gather_small(own_small, "all_gather_c_w_conv")[:, 0, :]
    c_all = all_small[:, :D_MODEL]
    w_conv_full = all_small[:, D_MODEL:].reshape(N_DEV, DEPTH, 3, conv_shard).transpose(1, 2, 0, 3).reshape(
        DEPTH, 3, CONV_W)
    cps = jnp.concatenate([w_conv_full, pool_scale[:, None], jnp.zeros((DEPTH, 4, CONV_W), F32)], axis=1)

    c_act, pieces = _modulation_columns(c_all, w_ada)
    mod_all = _all_gather_small(pieces, "all_gather_modulation")
    mod_mine = lax.dynamic_index_in_dim(mod_all, me, axis=1, keepdims=False)
    mod = mod_mine.reshape(N_DEV, DEPTH, W_IN_SHARD).transpose(1, 0, 2).reshape(DEPTH, 3 * D_MODEL) + b_ada
    zeros_d = jnp.zeros((DEPTH, 3, D_MODEL), F32)
    vec = jnp.concatenate([mod.reshape(DEPTH, 3, D_MODEL), g_pre[:, None], g_post[:, None], zeros_d], axis=1)

    first_sems, shards, landing = _gather_weights_start(
        _to_bf16(w_in, "cast_w_in"), _to_bf16(w_out, "cast_w_out"), [mod_all, all_small])
    wpool_b = _to_bf16(w_pool.reshape(DEPTH, POOL_ROWS, GROUP_D), "cast_w_pool").reshape(w_pool.shape)

    xs, projs, ys, wins, wouts = [x0], [], [], [], []
    for l in range(DEPTH):
        passed_sems, zones = _gather_weights_pass_on(
            l, first_sems[1], landing[l], [vec, cps, wpool_b] if l == 0 else [xs[-1]])
        win, wout = _gather_weights_finish(l, first_sems, passed_sems, shards, zones)
        x_next, proj, y = _forward_layer(l, xs[-1], vec, cps, wpool_b, win, wout, None)
        xs.append(x_next)
        projs.append(proj)
        ys.append(y)
        wins.append(win)
        wouts.append(wout)
    dx, loss_tile = _loss_head(xs[DEPTH], target)

    slab_rows = [None] * DEPTH
    grads_above = partials = received = None
    in_flight = []
    for l in reversed(range(DEPTH)):
        dx, dproj, h_t, ycat_t, dy, gwpool, dcps, dvec = _backward_layer(
            l, dx, ys[l], projs[l], xs[l], vec, cps, wpool_b, wouts[l], wins[l], None)
        hosted = _sibling_exchange(grads_above, ALL_KINDS) if grads_above is not None else None
        gwin, gwout, *from_sibling = _weight_grads(l, h_t, dproj, ycat_t, dy, hosted)
        if grads_above is not None:
            partials = _add_sibling_blocks(
                f"grad_add_sibling_{l + 1}", l + 1, grads_above, from_sibling, core, partials, ALL_KINDS)
            chips = _chips_exchange(l + 1, partials, received, ALL_KINDS)
            sems, partials, received, _ = _start_exchange(chips, f"grad_chips_start_{l + 1}")
            in_flight.append((chips, sems, l + 1))
        grads_above = [gwin, gwout, gwpool.reshape(POOL_ROWS, GROUP_D)]
        slab_rows[l] = jnp.concatenate(
            [dvec[0], dvec[1], dvec[2], dvec[3], dvec[4], dcps[3], dcps[0], dcps[1], dcps[2]])
    grad_x = dx[None]

    loss_row = jnp.pad(loss_tile[0], (0, SLAB_COLS - LANES))
    slab = jnp.stack(slab_rows + [loss_row] + [jnp.zeros((SLAB_COLS,), F32)] * (SLAB_ROWS - DEPTH - 1))
    slabs, *from_sibling = _all_gather_small(
        slab, "all_gather_small_grads", _sibling_exchange(grads_above, ALL_KINDS))
    partials = _add_sibling_blocks("grad_add_sibling_0", 0, grads_above, from_sibling, core, partials, ALL_KINDS)
    chips_0 = _chips_exchange(0, partials, received, ALL_KINDS)
    sems_0, partials, received, token = _start_exchange(chips_0, "grad_chips_start_0")

    total = _sum_sources(slabs)
    loss = total[DEPTH, 0]
    o = 3 * D_MODEL
    g_b_ada = total[:DEPTH, :o]
    g_g_pre = total[:DEPTH, o:o + D_MODEL]
    g_g_post = total[:DEPTH, o + D_MODEL:o + 2 * D_MODEL]
    g_pool_scale = total[:DEPTH, o + 2 * D_MODEL:o + 2 * D_MODEL + POOL_W]
    g_conv_full = total[:DEPTH, o + 2 * D_MODEL + POOL_W:].reshape(DEPTH, 3, CONV_W)
    g_w_conv = lax.dynamic_slice_in_dim(g_conv_full, me * conv_shard, conv_shard, axis=2)

    after = [token]
    for chips, sems, l in in_flight:
        partials, received = _finish_exchange(chips, f"grad_chips_finish_{l}", sems, partials, received, after)
        after = []
    upper = (1, DEPTH)
    w_in_upper = _adamw_reduced(
        "adamw_w_in_upper", w_in, m_w_in, v_w_in, partials[0], received[0], chip, ROW_TILE, upper, None)
    w_out_upper = _adamw_reduced(
        "adamw_w_out_upper", w_out, m_w_out, v_w_out, partials[1], received[1], chip, W_OUT_SHARD, upper, None)
    dmod_all = slabs[:, :DEPTH, :o].reshape(N_DEV, DEPTH, N_DEV, W_IN_SHARD)
    dmod_cols = lax.dynamic_index_in_dim(dmod_all, me, axis=2, keepdims=False).transpose(1, 0, 2) + token[0, 0]
    g_w_ada, d_w_ada, nm_w_ada, nv_w_ada = _adamw_w_ada(w_ada, m_w_ada, v_w_ada, c_act.T, dmod_cols, None)

    partials, received = _finish_exchange(
        chips_0, "grad_chips_finish_0", sems_0, partials, received, [nv_w_ada, w_in_upper[3], w_out_upper[3]])
    gather_pool = _all_gather_exchange(_reduce_w_pool(partials[2], received[2], chip))
    sems_p, pool_rows, pool_landing, token_p = _start_exchange(gather_pool, "all_gather_grad_w_pool_start")
    g_w_in, d_w_in, nm_w_in, nv_w_in = _adamw_reduced(
        "adamw_w_in_0", w_in, m_w_in, v_w_in, partials[0], received[0], chip, ROW_TILE, (0, 1), w_in_upper)
    g_w_out, d_w_out, nm_w_out, nv_w_out = _adamw_reduced(
        "adamw_w_out_0", w_out, m_w_out, v_w_out, partials[1], received[1], chip, W_OUT_SHARD, (0, 1), w_out_upper)
    _, (g_pool_all,) = _finish_exchange(
        gather_pool, "all_gather_grad_w_pool_finish", sems_p, pool_rows, pool_landing, [nv_w_in, nv_w_out])
    g_w_pool = g_pool_all.reshape(N_DEV, DEPTH, POOL_SHARD, GROUP_D).transpose(1, 0, 2, 3).reshape(w_pool.shape)

    flat2 = lambda a: a.reshape(-1, a.shape[-1])
    small = _adamw_small([
        (b_ada, g_b_ada, m_b_ada, v_b_ada),
        (g_pre, g_g_pre, m_g_pre, v_g_pre),
        (flat2(w_conv), flat2(g_w_conv), flat2(m_w_conv), flat2(v_w_conv)),
        (flat2(w_pool), flat2(g_w_pool), flat2(m_w_pool), flat2(v_w_pool)),
        (pool_scale, g_pool_scale, m_pool_scale, v_pool_scale),
        (g_post, g_g_post, m_g_post, v_g_post),
    ])
    (d_b_ada, nm_b_ada, nv_b_ada), (d_g_pre, nm_g_pre, nv_g_pre), conv_upd, pool_upd, \
        (d_ps, nm_ps, nv_ps), (d_g_post, nm_g_post, nv_g_post) = small
    d_w_conv, nm_w_conv, nv_w_conv = (a.reshape(w_conv.shape) for a in conv_upd)
    d_w_pool, nm_w_pool, nv_w_pool = (a.reshape(w_pool.shape) for a in pool_upd)

    return (loss, grad_x,
            g_w_ada, g_b_ada, g_g_pre, g_w_in, g_w_conv, g_w_pool, g_pool_scale, g_w_out, g_g_post,
            d_w_ada, d_b_ada, d_g_pre, d_w_in, d_w_conv, d_w_pool, d_ps, d_w_out, d_g_post,
            nm_w_ada, nm_b_ada, nm_g_pre, nm_w_in, nm_w_conv, nm_w_pool, nm_ps, nm_w_out, nm_g_post,
            nv_w_ada, nv_b_ada, nv_g_pre, nv_w_in, nv_w_conv, nv_w_pool, nv_ps, nv_w_out, nv_g_post)
```

```python
import jax
import jax.numpy as jnp
from jax import lax
from jax.experimental import pallas as pl
from jax.experimental.pallas import tpu as pltpu

F32 = jnp.float32
BF16 = jnp.bfloat16

D_MODEL = 1024
DEPTH = 4
CONV_W = 512
POOL_W = 512
POOL_WINDOWS = (2, 4, 8, 16)
GROUP_D = 128
IN_COLS = 4 * CONV_W + 2 * POOL_W
NORM_EPS = 1e-6

ADAM_LR = 0.001
ADAM_B1 = 0.9
ADAM_B2 = 0.999
ADAM_EPS = 1e-08
ADAM_WD = 0.01
ADAM_STEP = 10

N_DEV = 8
N_CHIP = 4
N_OTHER_CHIPS = N_CHIP - 1
MESH = pl.DeviceIdType.MESH
W_IN_SHARD = IN_COLS // N_DEV
W_OUT_SHARD = D_MODEL // N_DEV
POOL_ROWS = len(POOL_WINDOWS) * GROUP_D
POOL_SHARD = POOL_ROWS // N_DEV

SUBLANES = 8
LANES = 128
VMEM_LIMIT_BYTES = 56 * 1024 * 1024
ROW_TILE = 512
BWD_TILE = 256
GWIN_COLS = 768
GWOUT_COLS = 512
POOL_HALO = 16
CONV_HALO = SUBLANES

SLAB_COLS = 3 * D_MODEL + D_MODEL + D_MODEL + POOL_W + 3 * CONV_W
SLAB_ROWS = SUBLANES

HBM = pl.BlockSpec(memory_space=pl.ANY)


def _params(**kw):
    return pltpu.CompilerParams(vmem_limit_bytes=VMEM_LIMIT_BYTES, **kw)


def _sigmoid(v):
    return 1.0 / (1.0 + jnp.exp(-v))


def _dot(a, b):
    return jnp.dot(a, b, preferred_element_type=F32)


def _dot_tn(a, b):
    return lax.dot_general(a, b, (((0,), (0,)), ((), ())), preferred_element_type=F32)


def _dot_nt(a, b):
    return lax.dot_general(a, b, (((1,), (1,)), ((), ())), preferred_element_type=F32)


def _rows_from_before(v, k):
    return pltpu.roll(v, k, 0)


def _rows_from_after(v, k):
    return pltpu.roll(v, v.shape[0] - k, 0)


def _window_counts(t0, rows):
    return (lax.broadcasted_iota(jnp.int32, (rows, 1), 0) + (t0 + 1)).astype(F32)


def _split_proj(p32):
    cw = CONV_W
    return (p32[:, 0 * cw:1 * cw], p32[:, 1 * cw:2 * cw], p32[:, 2 * cw:3 * cw], p32[:, 3 * cw:4 * cw],
            p32[:, 4 * cw:4 * cw + POOL_W], p32[:, 4 * cw + POOL_W:])


def _mixer_forward(p32, z_halo, up_halo, cps, wpool_ref, t0):
    tm = p32.shape[0]
    u_a, b_a, c_a, g_a, u_p, g_p = _split_proj(p32)
    w0, w1, w2, ps = cps[0:1, :], cps[1:2, :], cps[2:3, :], cps[3:4, :]
    z = c_a * u_a
    zcat = jnp.concatenate([z_halo, z], axis=0)
    z1 = _rows_from_before(zcat, 1)[CONV_HALO:]
    z2 = _rows_from_before(zcat, 2)[CONV_HALO:]
    conv = w0 * z2 + w1 * z1 + w2 * z
    sig_a = _sigmoid(g_a)
    silu_a = g_a * sig_a
    y_a = b_a * conv * silu_a

    pcat = jnp.concatenate([up_halo, u_p], axis=0)
    counts = _window_counts(t0, tm)
    pooled, mixed, inv = [], [], []
    for g, w in enumerate(POOL_WINDOWS):
        cols = slice(g * GROUP_D, (g + 1) * GROUP_D)
        s = pcat[:, cols]
        step = 1
        while step < w:
            s = s + _rows_from_before(s, step)
            step *= 2
        inv_g = 1.0 / jnp.minimum(counts, float(w))
        pooled_g = s[POOL_HALO:] * inv_g - u_p[:, cols]
        pooled.append(pooled_g)
        inv.append(inv_g)
        mixed.append(_dot(pooled_g.astype(BF16), wpool_ref[g]))
    mixed = jnp.concatenate(mixed, axis=1)
    sig_p = _sigmoid(g_p)
    silu_p = g_p * sig_p
    y_p = mixed * ps * silu_p
    ycat = jnp.concatenate([y_a, y_p], axis=1)
    return dict(u_a=u_a, b_a=b_a, c_a=c_a, g_a=g_a, u_p=u_p, g_p=g_p, z=z, z1=z1, z2=z2, conv=conv, sig_a=sig_a,
                silu_a=silu_a, pooled=pooled, inv=inv, mixed=mixed, sig_p=sig_p, silu_p=silu_p, ycat=ycat,
                w0=w0, w1=w1, w2=w2, ps=ps)


def _layer_spec(shape, layer):
    nd = len(shape)
    return pl.BlockSpec((None,) + tuple(shape[1:]), lambda i, _l=layer, _n=nd: (_l,) + (0,) * (_n - 1))


def _whole_spec(shape):
    return pl.BlockSpec(tuple(shape), lambda i, _n=len(shape): (0,) * _n)


def _mesh_position():
    return lax.axis_index("x"), lax.axis_index("y"), lax.axis_index("c")


def _block_id(x, y, c):
    return 4 * x + 2 * y + c


def _other_chips(x, y):
    return [(x ^ 1, y), (x, y ^ 1), (x ^ 1, y ^ 1)]


def _col_block(ref, blk):
    return ref.at[:, pl.ds(pl.multiple_of(blk * W_IN_SHARD, LANES), W_IN_SHARD)]


def _row_block(rows):
    def block(ref, blk):
        return ref.at[pl.ds(pl.multiple_of(blk * rows, rows), rows), :]
    return block


_BLOCK_OF = (_col_block, _row_block(W_OUT_SHARD), _row_block(POOL_SHARD))
_BLOCK_SHAPES = ((D_MODEL, W_IN_SHARD), (W_OUT_SHARD, D_MODEL), (POOL_SHARD, GROUP_D))


class _Exchange:
    def __init__(self, inputs, out_shapes, aliases, sem_shapes, make):
        self.inputs, self.out_shapes, self.aliases, self.sem_shapes, self.make = (
            list(inputs), list(out_shapes), dict(aliases), list(sem_shapes), make)


def _run_exchange(exchange, name):
    n_in, n_out = len(exchange.inputs), len(exchange.out_shapes)

    def body(*refs):
        start, finish = exchange.make(refs[:n_in], refs[n_in:n_in + n_out], refs[n_in + n_out:])
        start()
        finish()

    return pl.pallas_call(
        body, name=name, in_specs=[HBM] * n_in, out_specs=[HBM] * n_out, out_shape=exchange.out_shapes,
        scratch_shapes=exchange.sem_shapes, input_output_aliases=exchange.aliases, compiler_params=_params(),
    )(*exchange.inputs)


_SEM = pl.BlockSpec(memory_space=pltpu.SEMAPHORE)
_DATAFLOW = pltpu.SideEffectType.DATAFLOW_SIDE_EFFECTING


def _start_exchange(exchange, name, after=()):
    n_in, n_out, n_sem = len(exchange.inputs), len(exchange.out_shapes), len(exchange.sem_shapes)
    sources = [i for i in range(n_in) if i not in exchange.aliases]
    aliases = {i: n_sem + k for k, i in enumerate(sources)}
    aliases.update({i: n_sem + len(sources) + o for i, o in exchange.aliases.items()})

    def body(*refs):
        in_refs = refs[:n_in]
        outs = refs[n_in + len(after):]
        sems = outs[:n_sem]
        out_refs = outs[n_sem + len(sources):n_sem + len(sources) + n_out]
        exchange.make(in_refs, out_refs, sems)[0]()
        refs[-1][...] = jnp.zeros_like(refs[-1])

    outs = pl.pallas_call(
        body, name=name, in_specs=[HBM] * (n_in + len(after)),
        out_specs=[_SEM] * n_sem + [HBM] * (len(sources) + n_out) + [pl.BlockSpec(memory_space=pltpu.VMEM)],
        out_shape=(exchange.sem_shapes + [pltpu.HBM(exchange.inputs[i].shape, exchange.inputs[i].dtype) for i in sources]
                   + [pltpu.HBM(s.shape, s.dtype) for s in exchange.out_shapes]
                   + [jax.ShapeDtypeStruct((SUBLANES, LANES), F32)]),
        input_output_aliases=aliases, compiler_params=_params(has_side_effects=_DATAFLOW),
    )(*exchange.inputs, *after)
    return outs[:n_sem], outs[n_sem:n_sem + len(sources)], outs[n_sem + len(sources):-1], outs[-1]


def _finish_exchange(exchange, name, sems, sources, landing, after):
    n_src, n_out, n_sem = len(sources), len(landing), len(sems)
    n_in = len(exchange.inputs)
    source_at = [i for i in range(n_in) if i not in exchange.aliases]

    def body(*refs):
        src_refs, out_refs = refs[:n_src], refs[n_src:n_src + n_out]
        sem_refs = refs[n_src + n_out:n_src + n_out + n_sem]
        in_refs = [None] * n_in
        for k, i in enumerate(source_at):
            in_refs[i] = src_refs[k]
        for i, o in exchange.aliases.items():
            in_refs[i] = out_refs[o]
        exchange.make(in_refs, out_refs, sem_refs)[1]()

    arrays = list(sources) + list(landing)
    outs = pl.pallas_call(
        body, name=name, in_specs=[HBM] * len(arrays) + [_SEM] * n_sem + [HBM] * len(after),
        out_specs=[HBM] * len(arrays), out_shape=[pltpu.HBM(a.shape, a.dtype) for a in arrays],
        input_output_aliases={i: i for i in range(len(arrays))}, compiler_params=_params(has_side_effects=_DATAFLOW),
    )(*arrays, *sems, *after)
    return outs[:n_src], outs[n_src:]


N_GATHERED = 2
FIRST_COPIES = 1 + N_OTHER_CHIPS
_GATHERED_SHAPES = ((D_MODEL, IN_COLS), (D_MODEL, D_MODEL))


def _first_sem(layer, a, k):
    return (layer * N_GATHERED + a) * FIRST_COPIES + k


def _gather_copy(window_of, full_ref, blk, send_sem, recv_sem, to, src=None):
    window = window_of(full_ref, blk)
    return pltpu.make_async_remote_copy(
        src_ref=window if src is None else src, dst_ref=window, send_sem=send_sem, recv_sem=recv_sem,
        device_id=to, device_id_type=MESH)


def _first_copies(layer, shard_refs, full_refs, send_sems, recv_sems, local_sems):
    x, y, c = _mesh_position()
    me = _block_id(x, y, c)
    own, remote = [], []
    for a in range(N_GATHERED):
        shard = shard_refs[a].at[layer]
        own.append(pltpu.make_async_copy(
            shard, _BLOCK_OF[a](full_refs[a], me), local_sems.at[layer * N_GATHERED + a]))
        targets = [(x, y, 1 - c)] + [(*chip, c) for chip in _other_chips(x, y)]
        remote += [_gather_copy(_BLOCK_OF[a], full_refs[a], me, send_sems.at[_first_sem(layer, a, k)],
                                recv_sems.at[_first_sem(layer, a, k)], to, src=shard)
                   for k, to in enumerate(targets)]
    return own, remote


def _gather_weights_start(win_shards, wout_shards, after):
    n_first = DEPTH * N_GATHERED * FIRST_COPIES
    sem_shapes = [pltpu.SemaphoreType.DMA((n_first,)), pltpu.SemaphoreType.DMA((n_first,)),
                  pltpu.SemaphoreType.DMA((DEPTH * N_GATHERED,))]
    shards = [win_shards, wout_shards]

    def body(win_sh, wout_sh, *rest):
        send_sems, recv_sems, local_sems, win_thru, wout_thru, *landing = rest[len(after):]
        for layer in range(DEPTH):
            own, remote = _first_copies(layer, (win_sh, wout_sh), landing[N_GATHERED * layer:N_GATHERED * (layer + 1)],
                                        send_sems, recv_sems, local_sems)
            for cp in own + remote:
                cp.start()

    outs = pl.pallas_call(
        body, name="all_gather_weights_start", in_specs=[HBM] * (2 + len(after)),
        out_specs=[_SEM] * 3 + [HBM] * (2 + DEPTH * N_GATHERED),
        out_shape=(sem_shapes + [pltpu.HBM(s.shape, s.dtype) for s in shards]
                   + [pltpu.HBM(s, BF16) for _ in range(DEPTH) for s in _GATHERED_SHAPES]),
        input_output_aliases={0: 3, 1: 4}, compiler_params=_params(has_side_effects=_DATAFLOW),
    )(*shards, *after)
    landing = outs[5:]
    return outs[:3], outs[3:5], [landing[N_GATHERED * l:N_GATHERED * (l + 1)] for l in range(DEPTH)]


def _passed_on_copies(full_refs, send_sems, recv_sems, core_of_block):
    x, y, c = _mesh_position()
    return [_gather_copy(_BLOCK_OF[a], full_refs[a], _block_id(*chip, core_of_block),
                         send_sems.at[a * N_OTHER_CHIPS + j], recv_sems.at[a * N_OTHER_CHIPS + j], (x, y, 1 - c))
            for a in range(N_GATHERED) for j, chip in enumerate(_other_chips(x, y))]


def _gather_weights_pass_on(layer, first_recv_sems, landing, after):
    n = N_GATHERED * N_OTHER_CHIPS

    def body(win_ref, wout_ref, first_recv, *rest):
        send_sems, recv_sems = rest[len(after):len(after) + 2]
        x, y, c = _mesh_position()
        full_refs = (win_ref, wout_ref)
        passed = _passed_on_copies(full_refs, send_sems, recv_sems, c)
        for a in range(N_GATHERED):
            for j, chip in enumerate(_other_chips(x, y)):
                sem = _first_sem(layer, a, 1 + j)
                _gather_copy(_BLOCK_OF[a], full_refs[a], _block_id(*chip, c), first_recv.at[sem], first_recv.at[sem],
                             (x, y, c)).wait_recv()
                passed[a * N_OTHER_CHIPS + j].start()

    outs = pl.pallas_call(
        body, name=f"all_gather_weights_pass_on_{layer}", in_specs=[HBM] * N_GATHERED + [_SEM] + [HBM] * len(after),
        out_specs=[_SEM] * 2 + [HBM] * N_GATHERED,
        out_shape=[pltpu.SemaphoreType.DMA((n,)), pltpu.SemaphoreType.DMA((n,))]
        + [pltpu.HBM(a.shape, a.dtype) for a in landing],
        input_output_aliases={a: 2 + a for a in range(N_GATHERED)},
        compiler_params=_params(has_side_effects=_DATAFLOW),
    )(*landing, first_recv_sems, *after)
    return outs[:2], outs[2:]


def _gather_weights_finish(layer, first_sems, passed_sems, shards, landing):
    def body(win_ref, wout_ref, first_send, first_recv, local_sems, passed_send, passed_recv, win_sh, wout_sh, *thru):
        x, y, c = _mesh_position()
        full_refs = (win_ref, wout_ref)
        own, remote = _first_copies(layer, (win_sh, wout_sh), full_refs, first_send, first_recv, local_sems)
        for a in range(N_GATHERED):
            sem = _first_sem(layer, a, 0)
            _gather_copy(_BLOCK_OF[a], full_refs[a], _block_id(x, y, 1 - c), first_recv.at[sem], first_recv.at[sem],
                         (x, y, c)).wait_recv()
        for cp in _passed_on_copies(full_refs, passed_send, passed_recv, 1 - c):
            cp.wait_recv()
        for cp in remote + _passed_on_copies(full_refs, passed_send, passed_recv, c):
            cp.wait_send()
        for cp in own:
            cp.wait()

    return pl.pallas_call(
        body, name=f"all_gather_weights_finish_{layer}", in_specs=[HBM] * N_GATHERED + [_SEM] * 5 + [HBM] * 2,
        out_specs=[HBM] * N_GATHERED, out_shape=[pltpu.HBM(a.shape, a.dtype) for a in landing],
        input_output_aliases={a: a for a in range(N_GATHERED)}, compiler_params=_params(has_side_effects=_DATAFLOW),
    )(*landing, *first_sems, *passed_sems, *shards)


ALL_KINDS = (0, 1, 2)


def _sibling_exchange(grads, kinds):
    n_arr = len(grads)

    def make(in_refs, out_refs, sems):
        send_sems, recv_sems = sems
        x, y, c = _mesh_position()
        copies = [pltpu.make_async_remote_copy(
            src_ref=_BLOCK_OF[kinds[a]](in_refs[a], 2 * q + (1 - c)), dst_ref=out_refs[a].at[q],
            send_sem=send_sems.at[a, q], recv_sem=recv_sems.at[a, q], device_id=(x, y, 1 - c), device_id_type=MESH)
            for a in range(n_arr) for q in range(N_CHIP)]

        def start():
            for cp in copies:
                cp.start()

        def finish():
            for cp in copies:
                cp.wait_recv()
            for cp in copies:
                cp.wait_send()

        return start, finish

    return _Exchange(
        grads, [jax.ShapeDtypeStruct((N_CHIP,) + _BLOCK_SHAPES[k], BF16) for k in kinds], {},
        [pltpu.SemaphoreType.DMA((n_arr, N_CHIP)), pltpu.SemaphoreType.DMA((n_arr, N_CHIP))], make)


def _chips_exchange(layer, partials, received, kinds):
    n_arr = len(partials)

    def make(in_refs, out_refs, sems):
        send_sems, recv_sems = sems
        x, y, c = _mesh_position()
        copies = [pltpu.make_async_remote_copy(
            src_ref=in_refs[a].at[2 * qx + qy, layer], dst_ref=out_refs[a].at[j, layer],
            send_sem=send_sems.at[a * N_OTHER_CHIPS + j], recv_sem=recv_sems.at[a * N_OTHER_CHIPS + j],
            device_id=(qx, qy, c), device_id_type=MESH)
            for a in range(n_arr) for j, (qx, qy) in enumerate(_other_chips(x, y))]

        def start():
            for cp in copies:
                cp.start()

        def finish():
            for cp in copies:
                cp.wait_recv()
            for cp in copies:
                cp.wait_send()

        return start, finish

    inputs = list(partials)
    aliases = {}
    if received is not None:
        inputs += list(received)
        aliases = {n_arr + a: a for a in range(n_arr)}
    return _Exchange(
        inputs, [jax.ShapeDtypeStruct((N_OTHER_CHIPS, DEPTH) + _BLOCK_SHAPES[k], BF16) for k in kinds], aliases,
        [pltpu.SemaphoreType.DMA((n_arr * N_OTHER_CHIPS,)), pltpu.SemaphoreType.DMA((n_arr * N_OTHER_CHIPS,))], make)


def _all_gather_exchange(v):
    def make(in_refs, out_refs, sems):
        send_sems, recv_sems, local_sem = sems
        x, y, c = _mesh_position()
        me = _block_id(x, y, c)
        own = pltpu.make_async_copy(in_refs[0], out_refs[0].at[me], local_sem.at[0])
        sends, arrivals = [], []
        for k in range(1, N_DEV):
            px, py, pc = x ^ ((k >> 2) & 1), y ^ ((k >> 1) & 1), c ^ (k & 1)
            sends.append(pltpu.make_async_remote_copy(
                src_ref=in_refs[0], dst_ref=out_refs[0].at[me], send_sem=send_sems.at[k - 1],
                recv_sem=recv_sems.at[k - 1], device_id=(px, py, pc), device_id_type=MESH))
            arrivals.append(pltpu.make_async_remote_copy(
                src_ref=in_refs[0], dst_ref=out_refs[0].at[_block_id(px, py, pc)], send_sem=send_sems.at[k - 1],
                recv_sem=recv_sems.at[k - 1], device_id=(x, y, c), device_id_type=MESH))

        def start():
            for cp in [own] + sends:
                cp.start()

        def finish():
            for cp in arrivals:
                cp.wait_recv()
            for cp in sends:
                cp.wait_send()
            own.wait()

        return start, finish

    return _Exchange(
        [v], [jax.ShapeDtypeStruct((N_DEV,) + v.shape, v.dtype)], {},
        [pltpu.SemaphoreType.DMA((N_DEV - 1,)), pltpu.SemaphoreType.DMA((N_DEV - 1,)),
         pltpu.SemaphoreType.DMA((1,))], make)


def _host(exchange, args, in_specs, out_shape, out_specs, scratch):
    n_own = (len(args), len(out_shape), len(scratch))
    if exchange is None:
        return {}, lambda refs: (refs, None)
    n_ex = (len(exchange.inputs), len(exchange.out_shapes), len(exchange.sem_shapes))
    aliases = {n_own[0] + i: n_own[1] + o for i, o in exchange.aliases.items()}
    args += exchange.inputs
    in_specs += [HBM] * n_ex[0]
    out_shape += exchange.out_shapes
    out_specs += [HBM] * n_ex[1]
    scratch += exchange.sem_shapes

    def split(refs):
        own, theirs, at = [], [], 0
        for mine, ex in zip(n_own, n_ex):
            own += refs[at:at + mine]
            theirs.append(refs[at + mine:at + mine + ex])
            at += mine + ex
        return own, exchange.make(*theirs)

    return aliases, split


def _all_gather_small(v, name, exchange=None):
    vmem = pl.BlockSpec(memory_space=pltpu.VMEM)
    args, in_specs = [v], [vmem]
    out_shape, out_specs = [jax.ShapeDtypeStruct((N_DEV,) + v.shape, v.dtype)], [vmem]
    scratch = [pltpu.SemaphoreType.DMA((N_DEV - 1,)), pltpu.SemaphoreType.DMA((N_DEV - 1,))]
    aliases, split = _host(exchange, args, in_specs, out_shape, out_specs, scratch)

    def body(*refs):
        (v_ref, out_ref, send_sems, recv_sems), hosted = split(refs)
        if hosted is not None:
            hosted[0]()
        x, y, c = _mesh_position()
        me = _block_id(x, y, c)
        out_ref[me] = v_ref[...]
        sends = []
        for k in range(1, N_DEV):
            px, py, pc = x ^ ((k >> 2) & 1), y ^ ((k >> 1) & 1), c ^ (k & 1)
            send = pltpu.make_async_remote_copy(
                src_ref=v_ref, dst_ref=out_ref.at[me], send_sem=send_sems.at[k - 1], recv_sem=recv_sems.at[k - 1],
                device_id=(px, py, pc), device_id_type=MESH)
            send.start()
            sends.append((send, _block_id(px, py, pc)))
        for k, (send, peer) in enumerate(sends):
            pltpu.make_async_remote_copy(
                src_ref=v_ref, dst_ref=out_ref.at[peer], send_sem=send_sems.at[k], recv_sem=recv_sems.at[k],
                device_id=(x, y, c), device_id_type=MESH).wait_recv()
        for send, _ in sends:
            send.wait_send()
        if hosted is not None:
            hosted[1]()

    outs = pl.pallas_call(
        body, name=name, in_specs=in_specs, out_specs=out_specs, out_shape=out_shape, scratch_shapes=scratch,
        input_output_aliases=aliases, compiler_params=_params(),
    )(*args)
    return outs[0] if exchange is None else outs


def _forward_layer(layer, x, vec, cps, wpool, win, wout, exchange):
    t_len = x.shape[0]
    n_tiles = t_len // ROW_TILE
    row = lambda cols: pl.BlockSpec((ROW_TILE, cols), lambda i: (i, 0))
    args = [x, vec, cps, wpool, win, wout]
    in_specs = [row(D_MODEL), _layer_spec(vec.shape, layer), _layer_spec(cps.shape, layer),
                _layer_spec(wpool.shape, layer), _whole_spec(win.shape), _whole_spec(wout.shape)]
    out_shape = [jax.ShapeDtypeStruct((t_len, D_MODEL), F32), jax.ShapeDtypeStruct((t_len, IN_COLS), BF16),
                 jax.ShapeDtypeStruct((t_len, D_MODEL), BF16)]
    out_specs = [row(D_MODEL), row(IN_COLS), row(D_MODEL)]
    scratch = [pltpu.VMEM((CONV_HALO, CONV_W), F32), pltpu.VMEM((POOL_HALO, POOL_W), F32)]
    aliases, split = _host(exchange, args, in_specs, out_shape, out_specs, scratch)

    def body(*refs):
        (x_ref, vec_ref, cps_ref, wpool_ref, win_ref, wout_ref, xo_ref, proj_ref, y_ref, zc_ref, pc_ref), hosted = (
            split(refs))
        i = pl.program_id(0)

        @pl.when(i == 0)
        def _():
            if hosted is not None:
                hosted[0]()
            zc_ref[...] = jnp.zeros_like(zc_ref)
            pc_ref[...] = jnp.zeros_like(pc_ref)

        x_t = x_ref[...]
        shift, scale, gate = vec_ref[0:1, :], vec_ref[1:2, :], vec_ref[2:3, :]
        g_pre, g_post = vec_ref[3:4, :], vec_ref[4:5, :]
        rx = lax.rsqrt(jnp.mean(x_t * x_t, axis=-1, keepdims=True) + NORM_EPS)
        h = (x_t * rx) * g_pre * (1.0 + scale) + shift
        proj_b = _dot(h.astype(BF16), win_ref[...]).astype(BF16)
        proj_ref[...] = proj_b
        mx = _mixer_forward(proj_b.astype(F32), zc_ref[...], pc_ref[...], cps_ref[...], wpool_ref, i * ROW_TILE)
        zc_ref[...] = mx["z"][ROW_TILE - CONV_HALO:]
        pc_ref[...] = mx["u_p"][ROW_TILE - POOL_HALO:]
        y_b = _dot(mx["ycat"].astype(BF16), wout_ref[...]).astype(BF16)
        y_ref[...] = y_b
        y_t = y_b.astype(F32)
        ry = lax.rsqrt(jnp.mean(y_t * y_t, axis=-1, keepdims=True) + NORM_EPS)
        xo_ref[...] = x_t + gate * (y_t * ry * g_post)

        if hosted is not None:
            pl.when(i == n_tiles - 1)(hosted[1])

    return pl.pallas_call(
        body, name=f"forward_layer_{layer}", grid=(n_tiles,), in_specs=in_specs, out_specs=out_specs,
        out_shape=out_shape, scratch_shapes=scratch, input_output_aliases=aliases,
        compiler_params=_params(dimension_semantics=("arbitrary",)),
    )(*args)


def _loss_head(x_final, target):
    t_len = x_final.shape[0]
    n_tiles = t_len // ROW_TILE

    def body(x_ref, t_ref, dx_ref, loss_ref):
        @pl.when(pl.program_id(0) == 0)
        def _():
            loss_ref[...] = jnp.zeros_like(loss_ref)

        err = x_ref[...] - t_ref[...]
        dx_ref[...] = err * (1.0 / D_MODEL)
        loss_ref[...] += jnp.sum(err * err) * (0.5 / D_MODEL)

    row = pl.BlockSpec((ROW_TILE, D_MODEL), lambda i: (i, 0))
    return pl.pallas_call(
        body, name="loss_head", grid=(n_tiles,), in_specs=[row, row],
        out_specs=[row, pl.BlockSpec((SUBLANES, LANES), lambda i: (0, 0))],
        out_shape=[jax.ShapeDtypeStruct((t_len, D_MODEL), F32), jax.ShapeDtypeStruct((SUBLANES, LANES), F32)],
        compiler_params=_params(dimension_semantics=("arbitrary",)),
    )(x_final, target)


def _backward_layer(layer, dxo, y, proj, x, vec, cps, wpool, wout, win, exchange):
    t_len = dxo.shape[0]
    n_tiles = t_len // BWD_TILE
    halo_per_tile = BWD_TILE // POOL_HALO
    rev = lambda cols: pl.BlockSpec((BWD_TILE, cols), lambda i: (n_tiles - 1 - i, 0))
    rev_t = pl.BlockSpec((D_MODEL, BWD_TILE), lambda i: (0, n_tiles - 1 - i))
    halo_spec = pl.BlockSpec(
        (POOL_HALO, IN_COLS), lambda i: (jnp.maximum((n_tiles - 1 - i) * halo_per_tile - 1, 0), 0))
    gwpool_shape = (len(POOL_WINDOWS), GROUP_D, GROUP_D)
    args = [dxo, y, proj, proj, x, vec, cps, wpool, wout, win]
    in_specs = [rev(D_MODEL), rev(D_MODEL), rev(IN_COLS), halo_spec, rev(D_MODEL), _layer_spec(vec.shape, layer),
                _layer_spec(cps.shape, layer), _layer_spec(wpool.shape, layer), _whole_spec(wout.shape),
                _whole_spec(win.shape)]
    out_shape = [jax.ShapeDtypeStruct((t_len, D_MODEL), F32), jax.ShapeDtypeStruct((t_len, IN_COLS), BF16),
                 jax.ShapeDtypeStruct((D_MODEL, t_len), BF16), jax.ShapeDtypeStruct((D_MODEL, t_len), BF16),
                 jax.ShapeDtypeStruct((t_len, D_MODEL), BF16), jax.ShapeDtypeStruct(gwpool_shape, BF16),
                 jax.ShapeDtypeStruct((SUBLANES, CONV_W), F32), jax.ShapeDtypeStruct((SUBLANES, D_MODEL), F32)]
    out_specs = [rev(D_MODEL), rev(IN_COLS), rev_t, rev_t, rev(D_MODEL), _whole_spec(gwpool_shape),
                 _whole_spec((SUBLANES, CONV_W)), _whole_spec((SUBLANES, D_MODEL))]
    scratch = [pltpu.VMEM(gwpool_shape, F32), pltpu.VMEM((CONV_HALO, CONV_W), F32),
               pltpu.VMEM((POOL_HALO, POOL_W), F32)]
    aliases, split = _host(exchange, args, in_specs, out_shape, out_specs, scratch)

    def body(*refs):
        (dxo_ref, y_ref, proj_ref, projh_ref, x_ref, vec_ref, cps_ref, wpool_ref, wout_ref, win_ref,
         dx_ref, dproj_ref, ht_ref, ycatt_ref, dy_ref, gwpool_ref, dcps_ref, dvec_ref,
         gwpool_acc, dcc_ref, qc_ref), hosted = split(refs)
        i = pl.program_id(0)
        tile = n_tiles - 1 - i

        @pl.when(i == 0)
        def _():
            if hosted is not None:
                hosted[0]()
            gwpool_acc[...] = jnp.zeros_like(gwpool_acc)
            dcps_ref[...] = jnp.zeros_like(dcps_ref)
            dvec_ref[...] = jnp.zeros_like(dvec_ref)
            dcc_ref[...] = jnp.zeros_like(dcc_ref)
            qc_ref[...] = jnp.zeros_like(qc_ref)

        shift, scale, gate = vec_ref[0:1, :], vec_ref[1:2, :], vec_ref[2:3, :]
        g_pre, g_post = vec_ref[3:4, :], vec_ref[4:5, :]

        dxo_t = dxo_ref[...]
        y_t = y_ref[...].astype(F32)
        ry = lax.rsqrt(jnp.mean(y_t * y_t, axis=-1, keepdims=True) + NORM_EPS)
        yh = y_t * ry
        dxy = dxo_t * yh
        dvec_ref[2:3, :] += jnp.sum(dxy * g_post, axis=0, keepdims=True)
        dvec_ref[4:5, :] += jnp.sum(dxy * gate, axis=0, keepdims=True)
        dyh = dxo_t * (gate * g_post)
        dy_b = (ry * (dyh - yh * jnp.mean(dyh * yh, axis=-1, keepdims=True))).astype(BF16)
        dy_ref[...] = dy_b

        halo = jnp.where(tile > 0, projh_ref[...].astype(F32), 0.0)
        hu_a, _, hc_a, _, hu_p, _ = _split_proj(halo)
        z_halo = (hc_a * hu_a)[POOL_HALO - CONV_HALO:]
        mx = _mixer_forward(proj_ref[...].astype(F32), z_halo, hu_p, cps_ref[...], wpool_ref, tile * BWD_TILE)
        ycatt_ref[...] = mx["ycat"].T.astype(BF16)

        dycat = _dot_nt(dy_b, wout_ref[...])
        dy_a, dy_p = dycat[:, :CONV_W], dycat[:, CONV_W:]

        t_a = dy_a * mx["silu_a"]
        db_a = t_a * mx["conv"]
        dconv = t_a * mx["b_a"]
        dg_a = dy_a * mx["b_a"] * mx["conv"] * (mx["sig_a"] * (1.0 + mx["g_a"] * (1.0 - mx["sig_a"])))
        dccat = jnp.concatenate([dconv, dcc_ref[...]], axis=0)
        dc1 = _rows_from_after(dccat, 1)[:BWD_TILE]
        dc2 = _rows_from_after(dccat, 2)[:BWD_TILE]
        dz = mx["w2"] * dconv + mx["w1"] * dc1 + mx["w0"] * dc2
        dcc_ref[...] = dconv[:CONV_HALO]
        dcps_ref[0:1, :] += jnp.sum(dconv * mx["z2"], axis=0, keepdims=True)
        dcps_ref[1:2, :] += jnp.sum(dconv * mx["z1"], axis=0, keepdims=True)
        dcps_ref[2:3, :] += jnp.sum(dconv * mx["z"], axis=0, keepdims=True)
        du_a = dz * mx["c_a"]
        dc_a = dz * mx["u_a"]

        t_p = dy_p * mx["silu_p"]
        dcps_ref[3:4, :] += jnp.sum(t_p * mx["mixed"], axis=0, keepdims=True)
        dmixed = (t_p * mx["ps"]).astype(BF16)
        dg_p = dy_p * mx["mixed"] * mx["ps"] * (mx["sig_p"] * (1.0 + mx["g_p"] * (1.0 - mx["sig_p"])))
        du_p, q_head = [], []
        for g, w in enumerate(POOL_WINDOWS):
            cols = slice(g * GROUP_D, (g + 1) * GROUP_D)
            dm_g = dmixed[:, cols]
            dpooled_g = _dot_nt(dm_g, wpool_ref[g])
            gwpool_acc[g] += _dot_tn(mx["pooled"][g].astype(BF16), dm_g)
            q_g = dpooled_g * mx["inv"][g]
            q_head.append(q_g[:POOL_HALO])
            s = jnp.concatenate([q_g, qc_ref[:, cols]], axis=0)
            step = 1
            while step < w:
                s = s + _rows_from_after(s, step)
                step *= 2
            du_p.append(s[:BWD_TILE] - dpooled_g)
        qc_ref[...] = jnp.concatenate(q_head, axis=1)
        dproj_b = jnp.concatenate([du_a, db_a, dc_a, dg_a] + du_p + [dg_p], axis=1).astype(BF16)
        dproj_ref[...] = dproj_b

        x_t = x_ref[...]
        rx = lax.rsqrt(jnp.mean(x_t * x_t, axis=-1, keepdims=True) + NORM_EPS)
        xn = x_t * rx
        mod_scale = 1.0 + scale
        ht_ref[...] = (xn * g_pre * mod_scale + shift).T.astype(BF16)
        dh = _dot_nt(dproj_b, win_ref[...])
        dvec_ref[0:1, :] += jnp.sum(dh, axis=0, keepdims=True)
        dhx = dh * xn
        dvec_ref[1:2, :] += jnp.sum(dhx * g_pre, axis=0, keepdims=True)
        dvec_ref[3:4, :] += jnp.sum(dhx * mod_scale, axis=0, keepdims=True)
        dxn = dh * (g_pre * mod_scale)
        dx_ref[...] = dxo_t + rx * (dxn - xn * jnp.mean(dxn * xn, axis=-1, keepdims=True))

        @pl.when(i == n_tiles - 1)
        def _():
            gwpool_ref[...] = gwpool_acc[...].astype(BF16)
            if hosted is not None:
                hosted[1]()

    return pl.pallas_call(
        body, name=f"backward_layer_{layer}", grid=(n_tiles,), in_specs=in_specs, out_specs=out_specs,
        out_shape=out_shape, scratch_shapes=scratch, input_output_aliases=aliases,
        compiler_params=_params(dimension_semantics=("arbitrary",)),
    )(*args)


def _weight_grads(layer, h_t, dproj, ycat_t, dy, exchange):
    t_len = dy.shape[0]
    n_in, n_out = IN_COLS // GWIN_COLS, D_MODEL // GWOUT_COLS
    args = [h_t, dproj, ycat_t, dy]
    in_specs = [_whole_spec(h_t.shape),
                pl.BlockSpec((t_len, GWIN_COLS), lambda s: (0, jnp.minimum(s, n_in - 1))),
                _whole_spec(ycat_t.shape),
                pl.BlockSpec((t_len, GWOUT_COLS), lambda s: (0, jnp.maximum(s - n_in, 0)))]
    out_shape = [jax.ShapeDtypeStruct((D_MODEL, IN_COLS), BF16), jax.ShapeDtypeStruct((D_MODEL, D_MODEL), BF16)]
    out_specs = [pl.BlockSpec((D_MODEL, GWIN_COLS), lambda s: (0, jnp.minimum(s, n_in - 1))),
                 pl.BlockSpec((D_MODEL, GWOUT_COLS), lambda s: (0, jnp.maximum(s - n_in, 0)))]
    scratch = []
    aliases, split = _host(exchange, args, in_specs, out_shape, out_specs, scratch)

    def body(*refs):
        (ht_ref, dproj_ref, ycatt_ref, dy_ref, gwin_ref, gwout_ref), hosted = split(refs)
        s = pl.program_id(0)
        if hosted is not None:
            pl.when(s == 0)(hosted[0])

        @pl.when(s < n_in)
        def _():
            gwin_ref[...] = _dot(ht_ref[...], dproj_ref[...]).astype(BF16)

        @pl.when(s >= n_in)
        def _():
            gwout_ref[...] = _dot(ycatt_ref[...], dy_ref[...]).astype(BF16)

        if hosted is not None:
            pl.when(s == n_in + n_out - 1)(hosted[1])

    return pl.pallas_call(
        body, name=f"weight_grads_{layer}", grid=(n_in + n_out,), in_specs=in_specs, out_specs=out_specs,
        out_shape=out_shape, scratch_shapes=scratch, input_output_aliases=aliases,
        compiler_params=_params(dimension_semantics=("arbitrary",)),
    )(*args)


def _add_sibling_blocks(name, layer, grads, received, core, partials, kinds):
    n_arr = len(grads)

    def body(core_ref, *refs):
        mine, theirs, outs = refs[:n_arr], refs[n_arr:2 * n_arr], refs[-n_arr:]
        for a in range(n_arr):
            outs[a][...] = (mine[a][...].astype(F32) + theirs[a][...].astype(F32)).astype(BF16)

    own_of_kind = [
        pl.BlockSpec((D_MODEL, W_IN_SHARD), lambda q, core_ref: (0, 2 * q + core_ref[0])),
        pl.BlockSpec((W_OUT_SHARD, D_MODEL), lambda q, core_ref: (2 * q + core_ref[0], 0)),
        pl.BlockSpec((POOL_SHARD, GROUP_D), lambda q, core_ref: (2 * q + core_ref[0], 0)),
    ]
    shapes = [_BLOCK_SHAPES[k] for k in kinds]
    recv_specs = [pl.BlockSpec((None,) + s, lambda q, core_ref: (q, 0, 0)) for s in shapes]
    out_specs = [pl.BlockSpec((None, None) + s, lambda q, core_ref: (q, layer, 0, 0)) for s in shapes]
    args = [core, *grads, *received]
    in_specs = [own_of_kind[k] for k in kinds] + recv_specs
    aliases = {}
    if partials is not None:
        aliases = {len(args) + a: a for a in range(n_arr)}
        args += list(partials)
        in_specs += [HBM] * n_arr
    return pl.pallas_call(
        body, name=name,
        grid_spec=pltpu.PrefetchScalarGridSpec(
            num_scalar_prefetch=1, grid=(N_CHIP,), in_specs=in_specs, out_specs=out_specs),
        out_shape=[jax.ShapeDtypeStruct((N_CHIP, DEPTH) + s, BF16) for s in shapes],
        input_output_aliases=aliases,
        compiler_params=_params(dimension_semantics=("arbitrary",)),
    )(*args)


def _modulation_columns(c_all, w_ada):
    def body(c_ref, w_ref, cact_ref, out_ref):
        c_t = c_ref[...]
        c_act = c_t * _sigmoid(c_t)
        cact_ref[...] = c_act
        out_ref[...] = jnp.dot(c_act, w_ref[...], preferred_element_type=F32, precision=lax.Precision.HIGHEST)

    return pl.pallas_call(
        body, name="modulation_columns", grid=(DEPTH,),
        in_specs=[pl.BlockSpec((N_DEV, D_MODEL), lambda l: (0, 0)),
                  pl.BlockSpec((None, D_MODEL, W_IN_SHARD), lambda l: (l, 0, 0))],
        out_specs=[pl.BlockSpec((N_DEV, D_MODEL), lambda l: (0, 0)),
                   pl.BlockSpec((N_DEV, W_IN_SHARD), lambda l: (0, l))],
        out_shape=[jax.ShapeDtypeStruct((N_DEV, D_MODEL), F32),
                   jax.ShapeDtypeStruct((N_DEV, DEPTH * W_IN_SHARD), F32)],
        compiler_params=_params(dimension_semantics=("arbitrary",)),
    )(c_all, w_ada)


def _adamw(w, g, m, v):
    m_new = ADAM_B1 * m + (1.0 - ADAM_B1) * g
    v_new = ADAM_B2 * v + (1.0 - ADAM_B2) * (g * g)
    m_hat = m_new / (1.0 - ADAM_B1 ** ADAM_STEP)
    v_hat = v_new / (1.0 - ADAM_B2 ** ADAM_STEP)
    delta = -ADAM_LR * (m_hat / (jnp.sqrt(v_hat) + ADAM_EPS) + ADAM_WD * w)
    return delta, m_new, v_new


def _adamw_w_ada(w, m, v, c_act_t, dmod_cols, exchange):
    big = pl.BlockSpec((None, D_MODEL, W_IN_SHARD), lambda l: (l, 0, 0))
    args = [w, m, v, c_act_t, dmod_cols]
    in_specs = [big, big, big, pl.BlockSpec((D_MODEL, N_DEV), lambda l: (0, 0)),
                pl.BlockSpec((None, N_DEV, W_IN_SHARD), lambda l: (l, 0, 0))]
    out_shape, out_specs, scratch = [jax.ShapeDtypeStruct(w.shape, F32)] * 4, [big] * 4, []
    aliases, split = _host(exchange, args, in_specs, out_shape, out_specs, scratch)

    def body(*refs):
        (w_ref, m_ref, v_ref, ct_ref, dm_ref, g_ref, d_ref, mo_ref, vo_ref), hosted = split(refs)
        if hosted is not None:
            pl.when(pl.program_id(0) == 0)(hosted[0])
        g = ct_ref[:, 0:1] * dm_ref[0:1, :]
        for b in range(1, N_DEV):
            g = g + ct_ref[:, b:b + 1] * dm_ref[b:b + 1, :]
        g_ref[...] = g
        d_ref[...], mo_ref[...], vo_ref[...] = _adamw(w_ref[...], g, m_ref[...], v_ref[...])
        if hosted is not None:
            pl.when(pl.program_id(0) == DEPTH - 1)(hosted[1])

    return pl.pallas_call(
        body, name="adamw_w_ada", grid=(DEPTH,), in_specs=in_specs, out_specs=out_specs, out_shape=out_shape,
        scratch_shapes=scratch, input_output_aliases=aliases,
        compiler_params=_params(dimension_semantics=("arbitrary",)),
    )(*args)


def _sum_chip_partials(own_ref, recv_ref):
    g = own_ref[...].astype(F32)
    for j in range(N_OTHER_CHIPS):
        g = g + recv_ref[j].astype(F32)
    return g


def _partial_specs(row_tile, cols, first_layer=0):
    own = pl.BlockSpec((None, None, row_tile, cols), lambda l, r, chip_ref: (chip_ref[0], first_layer + l, r, 0))
    recv = pl.BlockSpec((N_OTHER_CHIPS, None, row_tile, cols), lambda l, r, chip_ref: (0, first_layer + l, r, 0))
    return own, recv


def _adamw_reduced(name, w, m, v, partial, received, chip, row_tile, layers, continued):
    depth, rows, cols = w.shape
    first, stop = layers

    def body(chip_ref, w_ref, m_ref, v_ref, own_ref, recv_ref, *rest):
        g_ref, d_ref, mo_ref, vo_ref = rest[-4:]
        g = _sum_chip_partials(own_ref, recv_ref)
        g_ref[...] = g
        d_ref[...], mo_ref[...], vo_ref[...] = _adamw(w_ref[...], g, m_ref[...], v_ref[...])

    blk = pl.BlockSpec((None, row_tile, cols), lambda l, r, chip_ref: (first + l, r, 0))
    args = [chip, w, m, v, partial, received]
    in_specs = [blk, blk, blk, *_partial_specs(row_tile, cols, first)]
    aliases = {}
    if continued is not None:
        aliases = {len(args) + k: k for k in range(4)}
        args += list(continued)
        in_specs += [HBM] * 4
    return pl.pallas_call(
        body, name=name,
        grid_spec=pltpu.PrefetchScalarGridSpec(
            num_scalar_prefetch=1, grid=(stop - first, rows // row_tile), in_specs=in_specs, out_specs=[blk] * 4),
        out_shape=[jax.ShapeDtypeStruct(w.shape, F32)] * 4, input_output_aliases=aliases,
        compiler_params=_params(dimension_semantics=("arbitrary", "arbitrary")),
    )(*args)


def _reduce_w_pool(partial, received, chip):
    def body(chip_ref, own_ref, recv_ref, g_ref):
        g_ref[...] = _sum_chip_partials(own_ref, recv_ref)

    return pl.pallas_call(
        body, name="reduce_w_pool",
        grid_spec=pltpu.PrefetchScalarGridSpec(
            num_scalar_prefetch=1, grid=(DEPTH, 1), in_specs=list(_partial_specs(POOL_SHARD, GROUP_D)),
            out_specs=pl.BlockSpec((POOL_SHARD, GROUP_D), lambda l, r, chip_ref: (l, 0))),
        out_shape=jax.ShapeDtypeStruct((DEPTH * POOL_SHARD, GROUP_D), F32),
        compiler_params=_params(dimension_semantics=("arbitrary", "arbitrary")),
    )(chip, partial, received)


def _adamw_small(params):
    n = len(params)

    def body(*refs):
        ins, outs = refs[:4 * n], refs[4 * n:]
        for p in range(n):
            w_ref, g_ref, m_ref, v_ref = ins[4 * p:4 * p + 4]
            d_ref, mo_ref, vo_ref = outs[3 * p:3 * p + 3]
            d_ref[...], mo_ref[...], vo_ref[...] = _adamw(w_ref[...], g_ref[...], m_ref[...], v_ref[...])

    vmem = pl.BlockSpec(memory_space=pltpu.VMEM)
    flat = [a for group in params for a in group]
    out_shape = [jax.ShapeDtypeStruct(group[0].shape, F32) for group in params for _ in range(3)]
    outs = pl.pallas_call(
        body, name="adamw_small", in_specs=[vmem] * len(flat), out_specs=[vmem] * len(out_shape),
        out_shape=out_shape, compiler_params=_params(),
    )(*flat)
    return [tuple(outs[3 * p:3 * p + 3]) for p in range(n)]


def _sum_sources(slabs):
    def body(s_ref, o_ref):
        acc = s_ref[0]
        for b in range(1, N_DEV):
            acc = acc + s_ref[b]
        o_ref[...] = acc

    vmem = pl.BlockSpec(memory_space=pltpu.VMEM)
    return pl.pallas_call(
        body, name="sum_small_grads", in_specs=[vmem], out_specs=vmem,
        out_shape=jax.ShapeDtypeStruct(slabs.shape[1:], F32), compiler_params=_params(),
    )(slabs)


def _to_bf16(a, name):
    def body(a_ref, o_ref):
        o_ref[...] = a_ref[...].astype(BF16)

    spec = pl.BlockSpec((None,) + a.shape[1:], lambda l: (l, 0, 0))
    return pl.pallas_call(
        body, name=name, grid=(a.shape[0],), in_specs=[spec], out_specs=spec,
        out_shape=jax.ShapeDtypeStruct(a.shape, BF16), compiler_params=_params(dimension_semantics=("arbitrary",)),
    )(a)


def kernel(x, c, w_ada, b_ada, g_pre, w_in, w_conv, w_pool, pool_scale, w_out, g_post, loss_target, m_w_ada, m_b_ada, m_g_pre, m_w_in, m_w_conv, m_w_pool, m_pool_scale, m_w_out, m_g_post, v_w_ada, v_b_ada, v_g_pre, v_w_in, v_w_conv, v_w_pool, v_pool_scale, v_w_out, v_g_post):
    mx, my, mc = _mesh_position()
    me = _block_id(mx, my, mc)
    chip = (2 * mx + my).astype(jnp.int32).reshape(1)
    core = mc.astype(jnp.int32).reshape(1)
    x0 = x[0]
    target = loss_target[0]
    conv_shard = w_conv.shape[-1]

    own_small = jnp.concatenate([c, w_conv.reshape(1, DEPTH * 3 * conv_shard)], axis=1)
    all_small = _all_gather_small(own_small, "all_gather_c_w_conv")[:, 0, :]
    c_all = all_small[:, :D_MODEL]
    w_conv_full = all_small[:, D_MODEL:].reshape(N_DEV, DEPTH, 3, conv_shard).transpose(1, 2, 0, 3).reshape(
        DEPTH, 3, CONV_W)
    cps = jnp.concatenate([w_conv_full, pool_scale[:, None], jnp.zeros((DEPTH, 4, CONV_W), F32)], axis=1)

    c_act, pieces = _modulation_columns(c_all, w_ada)
    mod_all = _all_gather_small(pieces, "all_gather_modulation")
    mod_mine = lax.dynamic_index_in_dim(mod_all, me, axis=1, keepdims=False)
    mod = mod_mine.reshape(N_DEV, DEPTH, W_IN_SHARD).transpose(1, 0, 2).reshape(DEPTH, 3 * D_MODEL) + b_ada
    zeros_d = jnp.zeros((DEPTH, 3, D_MODEL), F32)
    vec = jnp.concatenate([mod.reshape(DEPTH, 3, D_MODEL), g_pre[:, None], g_post[:, None], zeros_d], axis=1)

    first_sems, shards, landing = _gather_weights_start(
        _to_bf16(w_in, "cast_w_in"), _to_bf16(w_out, "cast_w_out"), [mod_all, all_small])
    wpool_b = _to_bf16(w_pool.reshape(DEPTH, POOL_ROWS, GROUP_D), "cast_w_pool").reshape(w_pool.shape)

    xs, projs, ys, wins, wouts = [x0], [], [], [], []
    for l in range(DEPTH):
        passed_sems, zones = _gather_weights_pass_on(
            l, first_sems[1], landing[l], [vec, cps, wpool_b] if l == 0 else [xs[-1]])
        win, wout = _gather_weights_finish(l, first_sems, passed_sems, shards, zones)
        x_next, proj, y = _forward_layer(l, xs[-1], vec, cps, wpool_b, win, wout, None)
        xs.append(x_next)
        projs.append(proj)
        ys.append(y)
        wins.append(win)
        wouts.append(wout)
    dx, loss_tile = _loss_head(xs[DEPTH], target)

    slab_rows = [None] * DEPTH
    partials = received = None
    in_flight = []

    def scatter(layer, grads, from_sibling, after):
        nonlocal partials, received
        partials = _add_sibling_blocks(
            f"grad_add_sibling_{layer}", layer, grads, from_sibling, core, partials, ALL_KINDS)
        chips = _chips_exchange(layer, partials, received, ALL_KINDS)
        sems, partials, received, token = _start_exchange(chips, f"grad_chips_start_{layer}", after)
        in_flight.append((chips, sems, layer))
        return token

    grads_above = None
    for l in reversed(range(DEPTH)):
        dx, dproj, h_t, ycat_t, dy, gwpool, dcps, dvec = _backward_layer(
            l, dx, ys[l], projs[l], xs[l], vec, cps, wpool_b, wouts[l], wins[l], None)
        slab_rows[l] = jnp.concatenate(
            [dvec[0], dvec[1], dvec[2], dvec[3], dvec[4], dcps[3], dcps[0], dcps[1], dcps[2]])
        if l == 0:
            loss_row = jnp.pad(loss_tile[0], (0, SLAB_COLS - LANES))
            slab = jnp.stack(slab_rows + [loss_row] + [jnp.zeros((SLAB_COLS,), F32)] * (SLAB_ROWS - DEPTH - 1))
            slabs = _all_gather_small(slab, "all_gather_small_grads")
        hosted = _sibling_exchange(grads_above, ALL_KINDS) if grads_above is not None else None
        gwin, gwout, *from_sibling = _weight_grads(l, h_t, dproj, ycat_t, dy, hosted)
        if grads_above is not None:
            scatter(l + 1, grads_above, from_sibling, [])
        grads_above = [gwin, gwout, gwpool.reshape(POOL_ROWS, GROUP_D)]
        if l <= 1:
            from_sibling = _run_exchange(_sibling_exchange(grads_above, ALL_KINDS), f"grad_exchange_sibling_{l}")
            token = scatter(l, grads_above, from_sibling, [slabs] if l == 0 else [])
            grads_above = None
    grad_x = dx[None]
    chips_0, sems_0, _ = in_flight.pop()

    total = _sum_sources(slabs)
    loss = total[DEPTH, 0]
    o = 3 * D_MODEL
    g_b_ada = total[:DEPTH, :o]
    g_g_pre = total[:DEPTH, o:o + D_MODEL]
    g_g_post = total[:DEPTH, o + D_MODEL:o + 2 * D_MODEL]
    g_pool_scale = total[:DEPTH, o + 2 * D_MODEL:o + 2 * D_MODEL + POOL_W]
    g_conv_full = total[:DEPTH, o + 2 * D_MODEL + POOL_W:].reshape(DEPTH, 3, CONV_W)
    g_w_conv = lax.dynamic_slice_in_dim(g_conv_full, me * conv_shard, conv_shard, axis=2)

    after = [token]
    for chips, sems, l in in_flight:
        partials, received = _finish_exchange(chips, f"grad_chips_finish_{l}", sems, partials, received, after)
        after = []
    upper = (1, DEPTH)
    w_in_upper = _adamw_reduced(
        "adamw_w_in_upper", w_in, m_w_in, v_w_in, partials[0], received[0], chip, ROW_TILE, upper, None)
    w_out_upper = _adamw_reduced(
        "adamw_w_out_upper", w_out, m_w_out, v_w_out, partials[1], received[1], chip, W_OUT_SHARD, upper, None)
    dmod_all = slabs[:, :DEPTH, :o].reshape(N_DEV, DEPTH, N_DEV, W_IN_SHARD)
    dmod_cols = lax.dynamic_index_in_dim(dmod_all, me, axis=2, keepdims=False).transpose(1, 0, 2) + token[0, 0]
    g_w_ada, d_w_ada, nm_w_ada, nv_w_ada = _adamw_w_ada(w_ada, m_w_ada, v_w_ada, c_act.T, dmod_cols, None)

    partials, received = _finish_exchange(
        chips_0, "grad_chips_finish_0", sems_0, partials, received, [nv_w_ada, w_in_upper[3], w_out_upper[3]])
    gather_pool = _all_gather_exchange(_reduce_w_pool(partials[2], received[2], chip))
    sems_p, pool_rows, pool_landing, token_p = _start_exchange(gather_pool, "all_gather_grad_w_pool_start")
    g_w_in, d_w_in, nm_w_in, nv_w_in = _adamw_reduced(
        "adamw_w_in_0", w_in, m_w_in, v_w_in, partials[0], received[0], chip, ROW_TILE, (0, 1), w_in_upper)
    g_w_out, d_w_out, nm_w_out, nv_w_out = _adamw_reduced(
        "adamw_w_out_0", w_out, m_w_out, v_w_out, partials[1], received[1], chip, W_OUT_SHARD, (0, 1), w_out_upper)
    _, (g_pool_all,) = _finish_exchange(
        gather_pool, "all_gather_grad_w_pool_finish", sems_p, pool_rows, pool_landing, [nv_w_in, nv_w_out])
    g_w_pool = g_pool_all.reshape(N_DEV, DEPTH, POOL_SHARD, GROUP_D).transpose(1, 0, 2, 3).reshape(w_pool.shape)

    flat2 = lambda a: a.reshape(-1, a.shape[-1])
    small = _adamw_small([
        (b_ada, g_b_ada, m_b_ada, v_b_ada),
        (g_pre, g_g_pre, m_g_pre, v_g_pre),
        (flat2(w_conv), flat2(g_w_conv), flat2(m_w_conv), flat2(v_w_conv)),
        (flat2(w_pool), flat2(g_w_pool), flat2(m_w_pool), flat2(v_w_pool)),
        (pool_scale, g_pool_scale, m_pool_scale, v_pool_scale),
        (g_post, g_g_post, m_g_post, v_g_post),
    ])
    (d_b_ada, nm_b_ada, nv_b_ada), (d_g_pre, nm_g_pre, nv_g_pre), conv_upd, pool_upd, \
        (d_ps, nm_ps, nv_ps), (d_g_post, nm_g_post, nv_g_post) = small
    d_w_conv, nm_w_conv, nv_w_conv = (a.reshape(w_conv.shape) for a in conv_upd)
    d_w_pool, nm_w_pool, nv_w_pool = (a.reshape(w_pool.shape) for a in pool_upd)

    return (loss, grad_x,
            g_w_ada, g_b_ada, g_g_pre, g_w_in, g_w_conv, g_w_pool, g_pool_scale, g_w_out, g_g_post,
            d_w_ada, d_b_ada, d_g_pre, d_w_in, d_w_conv, d_w_pool, d_ps, d_w_out, d_g_post,
            nm_w_ada, nm_b_ada, nm_g_pre, nm_w_in, nm_w_conv, nm_w_pool, nm_ps, nm_w_out, nm_g_post,
            nv_w_ada, nv_b_ada, nv_g_pre, nv_w_in, nv_w_conv, nv_w_pool, nv_ps, nv_w_out, nv_g_post)
```

```python
import jax
import jax.numpy as jnp
from jax import lax
from jax.experimental import pallas as pl
from jax.experimental.pallas import tpu as pltpu

F32 = jnp.float32
BF16 = jnp.bfloat16

D_MODEL = 1024
DEPTH = 4
CONV_W = 512
POOL_W = 512
POOL_WINDOWS = (2, 4, 8, 16)
GROUP_D = 128
IN_COLS = 4 * CONV_W + 2 * POOL_W
NORM_EPS = 1e-6

ADAM_LR = 0.001
ADAM_B1 = 0.9
ADAM_B2 = 0.999
ADAM_EPS = 1e-08
ADAM_WD = 0.01
ADAM_STEP = 10

N_DEV = 8
N_CHIP = 4
N_OTHER_CHIPS = N_CHIP - 1
MESH = pl.DeviceIdType.MESH
W_IN_SHARD = IN_COLS // N_DEV
W_OUT_SHARD = D_MODEL // N_DEV
POOL_ROWS = len(POOL_WINDOWS) * GROUP_D
POOL_SHARD = POOL_ROWS // N_DEV

SUBLANES = 8
LANES = 128
VMEM_LIMIT_BYTES = 56 * 1024 * 1024
ROW_TILE = 512
BWD_TILE = 256
GWIN_COLS = 768
GWOUT_COLS = 512
POOL_HALO = 16
CONV_HALO = SUBLANES

SLAB_COLS = 3 * D_MODEL + D_MODEL + D_MODEL + POOL_W + 3 * CONV_W

HBM = pl.BlockSpec(memory_space=pl.ANY)


def _params(**kw):
    return pltpu.CompilerParams(vmem_limit_bytes=VMEM_LIMIT_BYTES, **kw)


def _sigmoid(v):
    return 1.0 / (1.0 + jnp.exp(-v))


def _dot(a, b):
    return jnp.dot(a, b, preferred_element_type=F32)


def _dot_tn(a, b):
    return lax.dot_general(a, b, (((0,), (0,)), ((), ())), preferred_element_type=F32)


def _dot_nt(a, b):
    return lax.dot_general(a, b, (((1,), (1,)), ((), ())), preferred_element_type=F32)


def _rows_from_before(v, k):
    return pltpu.roll(v, k, 0)


def _rows_from_after(v, k):
    return pltpu.roll(v, v.shape[0] - k, 0)


def _window_counts(t0, rows):
    return (lax.broadcasted_iota(jnp.int32, (rows, 1), 0) + (t0 + 1)).astype(F32)


def _split_proj(p32):
    cw = CONV_W
    return (p32[:, 0 * cw:1 * cw], p32[:, 1 * cw:2 * cw], p32[:, 2 * cw:3 * cw], p32[:, 3 * cw:4 * cw],
            p32[:, 4 * cw:4 * cw + POOL_W], p32[:, 4 * cw + POOL_W:])


def _mixer_forward(p32, z_halo, up_halo, cps, wpool_ref, t0):
    tm = p32.shape[0]
    u_a, b_a, c_a, g_a, u_p, g_p = _split_proj(p32)
    w0, w1, w2, ps = cps[0:1, :], cps[1:2, :], cps[2:3, :], cps[3:4, :]
    z = c_a * u_a
    zcat = jnp.concatenate([z_halo, z], axis=0)
    z1 = _rows_from_before(zcat, 1)[CONV_HALO:]
    z2 = _rows_from_before(zcat, 2)[CONV_HALO:]
    conv = w0 * z2 + w1 * z1 + w2 * z
    sig_a = _sigmoid(g_a)
    silu_a = g_a * sig_a
    y_a = b_a * conv * silu_a

    pcat = jnp.concatenate([up_halo, u_p], axis=0)
    counts = _window_counts(t0, tm)
    pooled, mixed, inv = [], [], []
    for g, w in enumerate(POOL_WINDOWS):
        cols = slice(g * GROUP_D, (g + 1) * GROUP_D)
        s = pcat[:, cols]
        step = 1
        while step < w:
            s = s + _rows_from_before(s, step)
            step *= 2
        inv_g = 1.0 / jnp.minimum(counts, float(w))
        pooled_g = s[POOL_HALO:] * inv_g - u_p[:, cols]
        pooled.append(pooled_g)
        inv.append(inv_g)
        mixed.append(_dot(pooled_g.astype(BF16), wpool_ref[g]))
    mixed = jnp.concatenate(mixed, axis=1)
    sig_p = _sigmoid(g_p)
    silu_p = g_p * sig_p
    y_p = mixed * ps * silu_p
    ycat = jnp.concatenate([y_a, y_p], axis=1)
    return dict(u_a=u_a, b_a=b_a, c_a=c_a, g_a=g_a, u_p=u_p, g_p=g_p, z=z, z1=z1, z2=z2, conv=conv, sig_a=sig_a,
                silu_a=silu_a, pooled=pooled, inv=inv, mixed=mixed, sig_p=sig_p, silu_p=silu_p, ycat=ycat,
                w0=w0, w1=w1, w2=w2, ps=ps)


def _layer_spec(shape, layer):
    nd = len(shape)
    return pl.BlockSpec((None,) + tuple(shape[1:]), lambda i, _l=layer, _n=nd: (_l,) + (0,) * (_n - 1))


def _whole_spec(shape):
    return pl.BlockSpec(tuple(shape), lambda i, _n=len(shape): (0,) * _n)


def _mesh_position():
    return lax.axis_index("x"), lax.axis_index("y"), lax.axis_index("c")


def _block_id(x, y, c):
    return 4 * x + 2 * y + c


def _other_chips(x, y):
    return [(x ^ 1, y), (x, y ^ 1), (x ^ 1, y ^ 1)]


def _col_block(ref, blk):
    return ref.at[:, pl.ds(pl.multiple_of(blk * W_IN_SHARD, LANES), W_IN_SHARD)]


def _row_block(rows):
    def block(ref, blk):
        return ref.at[pl.ds(pl.multiple_of(blk * rows, rows), rows), :]
    return block


_BLOCK_OF = (_col_block, _row_block(W_OUT_SHARD), _row_block(POOL_SHARD))
_BLOCK_SHAPES = ((D_MODEL, W_IN_SHARD), (W_OUT_SHARD, D_MODEL), (POOL_SHARD, GROUP_D))


class _Exchange:
    def __init__(self, inputs, out_shapes, aliases, sem_shapes, make):
        self.inputs, self.out_shapes, self.aliases, self.sem_shapes, self.make = (
            list(inputs), list(out_shapes), dict(aliases), list(sem_shapes), make)


def _run_exchange(exchange, name):
    n_in, n_out = len(exchange.inputs), len(exchange.out_shapes)

    def body(*refs):
        start, finish = exchange.make(refs[:n_in], refs[n_in:n_in + n_out], refs[n_in + n_out:])
        start()
        finish()

    return pl.pallas_call(
        body, name=name, in_specs=[HBM] * n_in, out_specs=[HBM] * n_out, out_shape=exchange.out_shapes,
        scratch_shapes=exchange.sem_shapes, input_output_aliases=exchange.aliases, compiler_params=_params(),
    )(*exchange.inputs)


_SEM = pl.BlockSpec(memory_space=pltpu.SEMAPHORE)
_DATAFLOW = pltpu.SideEffectType.DATAFLOW_SIDE_EFFECTING


def _start_exchange(exchange, name, after=()):
    n_in, n_out, n_sem = len(exchange.inputs), len(exchange.out_shapes), len(exchange.sem_shapes)
    sources = [i for i in range(n_in) if i not in exchange.aliases]
    aliases = {i: n_sem + k for k, i in enumerate(sources)}
    aliases.update({i: n_sem + len(sources) + o for i, o in exchange.aliases.items()})

    def body(*refs):
        in_refs = refs[:n_in]
        outs = refs[n_in + len(after):]
        sems = outs[:n_sem]
        out_refs = outs[n_sem + len(sources):n_sem + len(sources) + n_out]
        exchange.make(in_refs, out_refs, sems)[0]()
        refs[-1][...] = jnp.zeros_like(refs[-1])

    outs = pl.pallas_call(
        body, name=name, in_specs=[HBM] * (n_in + len(after)),
        out_specs=[_SEM] * n_sem + [HBM] * (len(sources) + n_out) + [pl.BlockSpec(memory_space=pltpu.VMEM)],
        out_shape=(exchange.sem_shapes + [pltpu.HBM(exchange.inputs[i].shape, exchange.inputs[i].dtype) for i in sources]
                   + [pltpu.HBM(s.shape, s.dtype) for s in exchange.out_shapes]
                   + [jax.ShapeDtypeStruct((SUBLANES, LANES), F32)]),
        input_output_aliases=aliases, compiler_params=_params(has_side_effects=_DATAFLOW),
    )(*exchange.inputs, *after)
    return outs[:n_sem], outs[n_sem:n_sem + len(sources)], outs[n_sem + len(sources):-1], outs[-1]


def _finish_exchange(exchange, name, sems, sources, landing, after):
    n_src, n_out, n_sem = len(sources), len(landing), len(sems)
    n_in = len(exchange.inputs)
    source_at = [i for i in range(n_in) if i not in exchange.aliases]

    def body(*refs):
        src_refs, out_refs = refs[:n_src], refs[n_src:n_src + n_out]
        sem_refs = refs[n_src + n_out:n_src + n_out + n_sem]
        in_refs = [None] * n_in
        for k, i in enumerate(source_at):
            in_refs[i] = src_refs[k]
        for i, o in exchange.aliases.items():
            in_refs[i] = out_refs[o]
        exchange.make(in_refs, out_refs, sem_refs)[1]()

    arrays = list(sources) + list(landing)
    outs = pl.pallas_call(
        body, name=name, in_specs=[HBM] * len(arrays) + [_SEM] * n_sem + [HBM] * len(after),
        out_specs=[HBM] * len(arrays), out_shape=[pltpu.HBM(a.shape, a.dtype) for a in arrays],
        input_output_aliases={i: i for i in range(len(arrays))}, compiler_params=_params(has_side_effects=_DATAFLOW),
    )(*arrays, *sems, *after)
    return outs[:n_src], outs[n_src:]


N_GATHERED = 2
FIRST_COPIES = 1 + N_OTHER_CHIPS
_GATHERED_SHAPES = ((D_MODEL, IN_COLS), (D_MODEL, D_MODEL))


def _first_sem(layer, a, k):
    return (layer * N_GATHERED + a) * FIRST_COPIES + k


def _gather_copy(window_of, full_ref, blk, send_sem, recv_sem, to, src=None):
    window = window_of(full_ref, blk)
    return pltpu.make_async_remote_copy(
        src_ref=window if src is None else src, dst_ref=window, send_sem=send_sem, recv_sem=recv_sem,
        device_id=to, device_id_type=MESH)


def _first_copies(layer, shard_refs, full_refs, send_sems, recv_sems, local_sems):
    x, y, c = _mesh_position()
    me = _block_id(x, y, c)
    own, remote = [], []
    for a in range(N_GATHERED):
        shard = shard_refs[a].at[layer]
        own.append(pltpu.make_async_copy(
            shard, _BLOCK_OF[a](full_refs[a], me), local_sems.at[layer * N_GATHERED + a]))
        targets = [(x, y, 1 - c)] + [(*chip, c) for chip in _other_chips(x, y)]
        remote += [_gather_copy(_BLOCK_OF[a], full_refs[a], me, send_sems.at[_first_sem(layer, a, k)],
                                recv_sems.at[_first_sem(layer, a, k)], to, src=shard)
                   for k, to in enumerate(targets)]
    return own, remote


def _gather_weights_start(first_layer, win_shards, wout_shards, after):
    n_layers = win_shards.shape[0]
    n_first = n_layers * N_GATHERED * FIRST_COPIES
    sem_shapes = [pltpu.SemaphoreType.DMA((n_first,)), pltpu.SemaphoreType.DMA((n_first,)),
                  pltpu.SemaphoreType.DMA((n_layers * N_GATHERED,))]
    shards = [win_shards, wout_shards]

    def body(win_sh, wout_sh, *rest):
        send_sems, recv_sems, local_sems, win_thru, wout_thru, *landing = rest[len(after):]
        for layer in range(n_layers):
            own, remote = _first_copies(layer, (win_sh, wout_sh), landing[N_GATHERED * layer:N_GATHERED * (layer + 1)],
                                        send_sems, recv_sems, local_sems)
            for cp in own + remote:
                cp.start()

    outs = pl.pallas_call(
        body, name=f"all_gather_weights_start_{first_layer}", in_specs=[HBM] * (2 + len(after)),
        out_specs=[_SEM] * 3 + [HBM] * (2 + n_layers * N_GATHERED),
        out_shape=(sem_shapes + [pltpu.HBM(s.shape, s.dtype) for s in shards]
                   + [pltpu.HBM(s, BF16) for _ in range(n_layers) for s in _GATHERED_SHAPES]),
        input_output_aliases={0: 3, 1: 4}, compiler_params=_params(has_side_effects=_DATAFLOW),
    )(*shards, *after)
    landing = outs[5:]
    return outs[:3], outs[3:5], [landing[N_GATHERED * l:N_GATHERED * (l + 1)] for l in range(n_layers)]


def _passed_on_copies(full_refs, send_sems, recv_sems, core_of_block):
    x, y, c = _mesh_position()
    return [_gather_copy(_BLOCK_OF[a], full_refs[a], _block_id(*chip, core_of_block),
                         send_sems.at[a * N_OTHER_CHIPS + j], recv_sems.at[a * N_OTHER_CHIPS + j], (x, y, 1 - c))
            for a in range(N_GATHERED) for j, chip in enumerate(_other_chips(x, y))]


def _gather_weights_pass_on(layer, index, first_recv_sems, landing, after):
    n = N_GATHERED * N_OTHER_CHIPS

    def body(win_ref, wout_ref, first_recv, *rest):
        send_sems, recv_sems = rest[len(after):len(after) + 2]
        x, y, c = _mesh_position()
        full_refs = (win_ref, wout_ref)
        passed = _passed_on_copies(full_refs, send_sems, recv_sems, c)
        for a in range(N_GATHERED):
            for j, chip in enumerate(_other_chips(x, y)):
                sem = _first_sem(index, a, 1 + j)
                _gather_copy(_BLOCK_OF[a], full_refs[a], _block_id(*chip, c), first_recv.at[sem], first_recv.at[sem],
                             (x, y, c)).wait_recv()
                passed[a * N_OTHER_CHIPS + j].start()

    outs = pl.pallas_call(
        body, name=f"all_gather_weights_pass_on_{layer}", in_specs=[HBM] * N_GATHERED + [_SEM] + [HBM] * len(after),
        out_specs=[_SEM] * 2 + [HBM] * N_GATHERED,
        out_shape=[pltpu.SemaphoreType.DMA((n,)), pltpu.SemaphoreType.DMA((n,))]
        + [pltpu.HBM(a.shape, a.dtype) for a in landing],
        input_output_aliases={a: 2 + a for a in range(N_GATHERED)},
        compiler_params=_params(has_side_effects=_DATAFLOW),
    )(*landing, first_recv_sems, *after)
    return outs[:2], outs[2:]


def _gather_weights_finish(layer, index, first_sems, passed_sems, shards, landing):
    def body(win_ref, wout_ref, first_send, first_recv, local_sems, passed_send, passed_recv, win_sh, wout_sh, *thru):
        x, y, c = _mesh_position()
        full_refs = (win_ref, wout_ref)
        own, remote = _first_copies(index, (win_sh, wout_sh), full_refs, first_send, first_recv, local_sems)
        for a in range(N_GATHERED):
            sem = _first_sem(index, a, 0)
            _gather_copy(_BLOCK_OF[a], full_refs[a], _block_id(x, y, 1 - c), first_recv.at[sem], first_recv.at[sem],
                         (x, y, c)).wait_recv()
        for cp in _passed_on_copies(full_refs, passed_send, passed_recv, 1 - c):
            cp.wait_recv()
        for cp in remote + _passed_on_copies(full_refs, passed_send, passed_recv, c):
            cp.wait_send()
        for cp in own:
            cp.wait()

    return pl.pallas_call(
        body, name=f"all_gather_weights_finish_{layer}", in_specs=[HBM] * N_GATHERED + [_SEM] * 5 + [HBM] * 2,
        out_specs=[HBM] * N_GATHERED, out_shape=[pltpu.HBM(a.shape, a.dtype) for a in landing],
        input_output_aliases={a: a for a in range(N_GATHERED)}, compiler_params=_params(has_side_effects=_DATAFLOW),
    )(*landing, *first_sems, *passed_sems, *shards)


ALL_KINDS = (0, 1, 2)


def _sibling_exchange(grads, kinds):
    n_arr = len(grads)

    def make(in_refs, out_refs, sems):
        send_sems, recv_sems = sems
        x, y, c = _mesh_position()
        copies = [pltpu.make_async_remote_copy(
            src_ref=_BLOCK_OF[kinds[a]](in_refs[a], 2 * q + (1 - c)), dst_ref=out_refs[a].at[q],
            send_sem=send_sems.at[a, q], recv_sem=recv_sems.at[a, q], device_id=(x, y, 1 - c), device_id_type=MESH)
            for a in range(n_arr) for q in range(N_CHIP)]

        def start():
            for cp in copies:
                cp.start()

        def finish():
            for cp in copies:
                cp.wait_recv()
            for cp in copies:
                cp.wait_send()

        return start, finish

    return _Exchange(
        grads, [jax.ShapeDtypeStruct((N_CHIP,) + _BLOCK_SHAPES[k], BF16) for k in kinds], {},
        [pltpu.SemaphoreType.DMA((n_arr, N_CHIP)), pltpu.SemaphoreType.DMA((n_arr, N_CHIP))], make)


def _chips_exchange(layer, partials, received, kinds):
    n_arr = len(partials)

    def make(in_refs, out_refs, sems):
        send_sems, recv_sems = sems
        x, y, c = _mesh_position()
        copies = [pltpu.make_async_remote_copy(
            src_ref=in_refs[a].at[2 * qx + qy, layer], dst_ref=out_refs[a].at[j, layer],
            send_sem=send_sems.at[a * N_OTHER_CHIPS + j], recv_sem=recv_sems.at[a * N_OTHER_CHIPS + j],
            device_id=(qx, qy, c), device_id_type=MESH)
            for a in range(n_arr) for j, (qx, qy) in enumerate(_other_chips(x, y))]

        def start():
            for cp in copies:
                cp.start()

        def finish():
            for cp in copies:
                cp.wait_recv()
            for cp in copies:
                cp.wait_send()

        return start, finish

    inputs = list(partials)
    aliases = {}
    if received is not None:
        inputs += list(received)
        aliases = {n_arr + a: a for a in range(n_arr)}
    return _Exchange(
        inputs, [jax.ShapeDtypeStruct((N_OTHER_CHIPS, DEPTH) + _BLOCK_SHAPES[k], BF16) for k in kinds], aliases,
        [pltpu.SemaphoreType.DMA((n_arr * N_OTHER_CHIPS,)), pltpu.SemaphoreType.DMA((n_arr * N_OTHER_CHIPS,))], make)


def _all_gather_exchange(v):
    def make(in_refs, out_refs, sems):
        send_sems, recv_sems, local_sem = sems
        x, y, c = _mesh_position()
        me = _block_id(x, y, c)
        own = pltpu.make_async_copy(in_refs[0], out_refs[0].at[me], local_sem.at[0])
        sends, arrivals = [], []
        for k in range(1, N_DEV):
            px, py, pc = x ^ ((k >> 2) & 1), y ^ ((k >> 1) & 1), c ^ (k & 1)
            sends.append(pltpu.make_async_remote_copy(
                src_ref=in_refs[0], dst_ref=out_refs[0].at[me], send_sem=send_sems.at[k - 1],
                recv_sem=recv_sems.at[k - 1], device_id=(px, py, pc), device_id_type=MESH))
            arrivals.append(pltpu.make_async_remote_copy(
                src_ref=in_refs[0], dst_ref=out_refs[0].at[_block_id(px, py, pc)], send_sem=send_sems.at[k - 1],
                recv_sem=recv_sems.at[k - 1], device_id=(x, y, c), device_id_type=MESH))

        def start():
            for cp in [own] + sends:
                cp.start()

        def finish():
            for cp in arrivals:
                cp.wait_recv()
            for cp in sends:
                cp.wait_send()
            own.wait()

        return start, finish

    return _Exchange(
        [v], [jax.ShapeDtypeStruct((N_DEV,) + v.shape, v.dtype)], {},
        [pltpu.SemaphoreType.DMA((N_DEV - 1,)), pltpu.SemaphoreType.DMA((N_DEV - 1,)),
         pltpu.SemaphoreType.DMA((1,))], make)


def _host(exchange, args, in_specs, out_shape, out_specs, scratch):
    n_own = (len(args), len(out_shape), len(scratch))
    if exchange is None:
        return {}, lambda refs: (refs, None)
    n_ex = (len(exchange.inputs), len(exchange.out_shapes), len(exchange.sem_shapes))
    aliases = {n_own[0] + i: n_own[1] + o for i, o in exchange.aliases.items()}
    args += exchange.inputs
    in_specs += [HBM] * n_ex[0]
    out_shape += exchange.out_shapes
    out_specs += [HBM] * n_ex[1]
    scratch += exchange.sem_shapes

    def split(refs):
        own, theirs, at = [], [], 0
        for mine, ex in zip(n_own, n_ex):
            own += refs[at:at + mine]
            theirs.append(refs[at + mine:at + mine + ex])
            at += mine + ex
        return own, exchange.make(*theirs)

    return aliases, split


def _all_gather_small(v, name, exchange=None):
    vmem = pl.BlockSpec(memory_space=pltpu.VMEM)
    args, in_specs = [v], [vmem]
    out_shape, out_specs = [jax.ShapeDtypeStruct((N_DEV,) + v.shape, v.dtype)], [vmem]
    scratch = [pltpu.SemaphoreType.DMA((N_DEV - 1,)), pltpu.SemaphoreType.DMA((N_DEV - 1,))]
    aliases, split = _host(exchange, args, in_specs, out_shape, out_specs, scratch)

    def body(*refs):
        (v_ref, out_ref, send_sems, recv_sems), hosted = split(refs)
        if hosted is not None:
            hosted[0]()
        x, y, c = _mesh_position()
        me = _block_id(x, y, c)
        out_ref[me] = v_ref[...]
        sends = []
        for k in range(1, N_DEV):
            px, py, pc = x ^ ((k >> 2) & 1), y ^ ((k >> 1) & 1), c ^ (k & 1)
            send = pltpu.make_async_remote_copy(
                src_ref=v_ref, dst_ref=out_ref.at[me], send_sem=send_sems.at[k - 1], recv_sem=recv_sems.at[k - 1],
                device_id=(px, py, pc), device_id_type=MESH)
            send.start()
            sends.append((send, _block_id(px, py, pc)))
        for k, (send, peer) in enumerate(sends):
            pltpu.make_async_remote_copy(
                src_ref=v_ref, dst_ref=out_ref.at[peer], send_sem=send_sems.at[k], recv_sem=recv_sems.at[k],
                device_id=(x, y, c), device_id_type=MESH).wait_recv()
        for send, _ in sends:
            send.wait_send()
        if hosted is not None:
            hosted[1]()

    outs = pl.pallas_call(
        body, name=name, in_specs=in_specs, out_specs=out_specs, out_shape=out_shape, scratch_shapes=scratch,
        input_output_aliases=aliases, compiler_params=_params(),
    )(*args)
    return outs[0] if exchange is None else outs


def _forward_layer(layer, x, vec, cps, wpool, win, wout, exchange):
    t_len = x.shape[0]
    n_tiles = t_len // ROW_TILE
    row = lambda cols: pl.BlockSpec((ROW_TILE, cols), lambda i: (i, 0))
    args = [x, vec, cps, wpool, win, wout]
    in_specs = [row(D_MODEL), _layer_spec(vec.shape, layer), _layer_spec(cps.shape, layer),
                _layer_spec(wpool.shape, layer), _whole_spec(win.shape), _whole_spec(wout.shape)]
    out_shape = [jax.ShapeDtypeStruct((t_len, D_MODEL), F32), jax.ShapeDtypeStruct((t_len, IN_COLS), BF16),
                 jax.ShapeDtypeStruct((t_len, D_MODEL), BF16)]
    out_specs = [row(D_MODEL), row(IN_COLS), row(D_MODEL)]
    scratch = [pltpu.VMEM((CONV_HALO, CONV_W), F32), pltpu.VMEM((POOL_HALO, POOL_W), F32)]
    aliases, split = _host(exchange, args, in_specs, out_shape, out_specs, scratch)

    def body(*refs):
        (x_ref, vec_ref, cps_ref, wpool_ref, win_ref, wout_ref, xo_ref, proj_ref, y_ref, zc_ref, pc_ref), hosted = (
            split(refs))
        i = pl.program_id(0)

        @pl.when(i == 0)
        def _():
            if hosted is not None:
                hosted[0]()
            zc_ref[...] = jnp.zeros_like(zc_ref)
            pc_ref[...] = jnp.zeros_like(pc_ref)

        x_t = x_ref[...]
        shift, scale, gate = vec_ref[0:1, :], vec_ref[1:2, :], vec_ref[2:3, :]
        g_pre, g_post = vec_ref[3:4, :], vec_ref[4:5, :]
        rx = lax.rsqrt(jnp.mean(x_t * x_t, axis=-1, keepdims=True) + NORM_EPS)
        h = (x_t * rx) * g_pre * (1.0 + scale) + shift
        proj_b = _dot(h.astype(BF16), win_ref[...]).astype(BF16)
        proj_ref[...] = proj_b
        mx = _mixer_forward(proj_b.astype(F32), zc_ref[...], pc_ref[...], cps_ref[...], wpool_ref, i * ROW_TILE)
        zc_ref[...] = mx["z"][ROW_TILE - CONV_HALO:]
        pc_ref[...] = mx["u_p"][ROW_TILE - POOL_HALO:]
        y_b = _dot(mx["ycat"].astype(BF16), wout_ref[...]).astype(BF16)
        y_ref[...] = y_b
        y_t = y_b.astype(F32)
        ry = lax.rsqrt(jnp.mean(y_t * y_t, axis=-1, keepdims=True) + NORM_EPS)
        xo_ref[...] = x_t + gate * (y_t * ry * g_post)

        if hosted is not None:
            pl.when(i == n_tiles - 1)(hosted[1])

    return pl.pallas_call(
        body, name=f"forward_layer_{layer}", grid=(n_tiles,), in_specs=in_specs, out_specs=out_specs,
        out_shape=out_shape, scratch_shapes=scratch, input_output_aliases=aliases,
        compiler_params=_params(dimension_semantics=("arbitrary",)),
    )(*args)


def _loss_head(x_final, target):
    t_len = x_final.shape[0]
    n_tiles = t_len // ROW_TILE

    def body(x_ref, t_ref, dx_ref, loss_ref):
        @pl.when(pl.program_id(0) == 0)
        def _():
            loss_ref[...] = jnp.zeros_like(loss_ref)

        err = x_ref[...] - t_ref[...]
        dx_ref[...] = err * (1.0 / D_MODEL)
        loss_ref[...] += jnp.sum(err * err) * (0.5 / D_MODEL)

    row = pl.BlockSpec((ROW_TILE, D_MODEL), lambda i: (i, 0))
    return pl.pallas_call(
        body, name="loss_head", grid=(n_tiles,), in_specs=[row, row],
        out_specs=[row, pl.BlockSpec((SUBLANES, LANES), lambda i: (0, 0))],
        out_shape=[jax.ShapeDtypeStruct((t_len, D_MODEL), F32), jax.ShapeDtypeStruct((SUBLANES, LANES), F32)],
        compiler_params=_params(dimension_semantics=("arbitrary",)),
    )(x_final, target)


def _backward_layer(layer, dxo, y, proj, x, vec, cps, wpool, wout, win, exchange):
    t_len = dxo.shape[0]
    n_tiles = t_len // BWD_TILE
    halo_per_tile = BWD_TILE // POOL_HALO
    rev = lambda cols: pl.BlockSpec((BWD_TILE, cols), lambda i: (n_tiles - 1 - i, 0))
    rev_t = pl.BlockSpec((D_MODEL, BWD_TILE), lambda i: (0, n_tiles - 1 - i))
    halo_spec = pl.BlockSpec(
        (POOL_HALO, IN_COLS), lambda i: (jnp.maximum((n_tiles - 1 - i) * halo_per_tile - 1, 0), 0))
    gwpool_shape = (len(POOL_WINDOWS), GROUP_D, GROUP_D)
    args = [dxo, y, proj, proj, x, vec, cps, wpool, wout, win]
    in_specs = [rev(D_MODEL), rev(D_MODEL), rev(IN_COLS), halo_spec, rev(D_MODEL), _layer_spec(vec.shape, layer),
                _layer_spec(cps.shape, layer), _layer_spec(wpool.shape, layer), _whole_spec(wout.shape),
                _whole_spec(win.shape)]
    out_shape = [jax.ShapeDtypeStruct((t_len, D_MODEL), F32), jax.ShapeDtypeStruct((t_len, IN_COLS), BF16),
                 jax.ShapeDtypeStruct((D_MODEL, t_len), BF16), jax.ShapeDtypeStruct((D_MODEL, t_len), BF16),
                 jax.ShapeDtypeStruct((t_len, D_MODEL), BF16), jax.ShapeDtypeStruct(gwpool_shape, BF16),
                 jax.ShapeDtypeStruct((SUBLANES, CONV_W), F32), jax.ShapeDtypeStruct((SUBLANES, D_MODEL), F32)]
    out_specs = [rev(D_MODEL), rev(IN_COLS), rev_t, rev_t, rev(D_MODEL), _whole_spec(gwpool_shape),
                 _whole_spec((SUBLANES, CONV_W)), _whole_spec((SUBLANES, D_MODEL))]
    scratch = [pltpu.VMEM(gwpool_shape, F32), pltpu.VMEM((CONV_HALO, CONV_W), F32),
               pltpu.VMEM((POOL_HALO, POOL_W), F32)]
    aliases, split = _host(exchange, args, in_specs, out_shape, out_specs, scratch)

    def body(*refs):
        (dxo_ref, y_ref, proj_ref, projh_ref, x_ref, vec_ref, cps_ref, wpool_ref, wout_ref, win_ref,
         dx_ref, dproj_ref, ht_ref, ycatt_ref, dy_ref, gwpool_ref, dcps_ref, dvec_ref,
         gwpool_acc, dcc_ref, qc_ref), hosted = split(refs)
        i = pl.program_id(0)
        tile = n_tiles - 1 - i

        @pl.when(i == 0)
        def _():
            if hosted is not None:
                hosted[0]()
            gwpool_acc[...] = jnp.zeros_like(gwpool_acc)
            dcps_ref[...] = jnp.zeros_like(dcps_ref)
            dvec_ref[...] = jnp.zeros_like(dvec_ref)
            dcc_ref[...] = jnp.zeros_like(dcc_ref)
            qc_ref[...] = jnp.zeros_like(qc_ref)

        shift, scale, gate = vec_ref[0:1, :], vec_ref[1:2, :], vec_ref[2:3, :]
        g_pre, g_post = vec_ref[3:4, :], vec_ref[4:5, :]

        dxo_t = dxo_ref[...]
        y_t = y_ref[...].astype(F32)
        ry = lax.rsqrt(jnp.mean(y_t * y_t, axis=-1, keepdims=True) + NORM_EPS)
        yh = y_t * ry
        dvec_ref[2:3, :] += jnp.sum(dxo_t * yh, axis=0, keepdims=True)
        dyh = dxo_t * (gate * g_post)
        dy_b = (ry * (dyh - yh * jnp.mean(dyh * yh, axis=-1, keepdims=True))).astype(BF16)
        dy_ref[...] = dy_b

        halo = jnp.where(tile > 0, projh_ref[...].astype(F32), 0.0)
        hu_a, _, hc_a, _, hu_p, _ = _split_proj(halo)
        z_halo = (hc_a * hu_a)[POOL_HALO - CONV_HALO:]
        mx = _mixer_forward(proj_ref[...].astype(F32), z_halo, hu_p, cps_ref[...], wpool_ref, tile * BWD_TILE)
        ycatt_ref[...] = mx["ycat"].T.astype(BF16)

        dycat = _dot_nt(dy_b, wout_ref[...])
        dy_a, dy_p = dycat[:, :CONV_W], dycat[:, CONV_W:]

        t_a = dy_a * mx["silu_a"]
        db_a = t_a * mx["conv"]
        dconv = t_a * mx["b_a"]
        dg_a = dy_a * mx["b_a"] * mx["conv"] * (mx["sig_a"] * (1.0 + mx["g_a"] * (1.0 - mx["sig_a"])))
        dccat = jnp.concatenate([dconv, dcc_ref[...]], axis=0)
        dc1 = _rows_from_after(dccat, 1)[:BWD_TILE]
        dc2 = _rows_from_after(dccat, 2)[:BWD_TILE]
        dz = mx["w2"] * dconv + mx["w1"] * dc1 + mx["w0"] * dc2
        dcc_ref[...] = dconv[:CONV_HALO]
        dcps_ref[0:1, :] += jnp.sum(dconv * mx["z2"], axis=0, keepdims=True)
        dcps_ref[1:2, :] += jnp.sum(dconv * mx["z1"], axis=0, keepdims=True)
        dcps_ref[2:3, :] += jnp.sum(dconv * mx["z"], axis=0, keepdims=True)
        du_a = dz * mx["c_a"]
        dc_a = dz * mx["u_a"]

        t_p = dy_p * mx["silu_p"]
        dcps_ref[3:4, :] += jnp.sum(t_p * mx["mixed"], axis=0, keepdims=True)
        dmixed = (t_p * mx["ps"]).astype(BF16)
        dg_p = dy_p * mx["mixed"] * mx["ps"] * (mx["sig_p"] * (1.0 + mx["g_p"] * (1.0 - mx["sig_p"])))
        du_p, q_head = [], []
        for g, w in enumerate(POOL_WINDOWS):
            cols = slice(g * GROUP_D, (g + 1) * GROUP_D)
            dm_g = dmixed[:, cols]
            dpooled_g = _dot_nt(dm_g, wpool_ref[g])
            gwpool_acc[g] += _dot_tn(mx["pooled"][g].astype(BF16), dm_g)
            q_g = dpooled_g * mx["inv"][g]
            q_head.append(q_g[:POOL_HALO])
            s = jnp.concatenate([q_g, qc_ref[:, cols]], axis=0)
            step = 1
            while step < w:
                s = s + _rows_from_after(s, step)
                step *= 2
            du_p.append(s[:BWD_TILE] - dpooled_g)
        qc_ref[...] = jnp.concatenate(q_head, axis=1)
        dproj_b = jnp.concatenate([du_a, db_a, dc_a, dg_a] + du_p + [dg_p], axis=1).astype(BF16)
        dproj_ref[...] = dproj_b

        x_t = x_ref[...]
        rx = lax.rsqrt(jnp.mean(x_t * x_t, axis=-1, keepdims=True) + NORM_EPS)
        xn = x_t * rx
        mod_scale = 1.0 + scale
        ht_ref[...] = (xn * g_pre * mod_scale + shift).T.astype(BF16)
        dh = _dot_nt(dproj_b, win_ref[...])
        dvec_ref[0:1, :] += jnp.sum(dh, axis=0, keepdims=True)
        dvec_ref[1:2, :] += jnp.sum(dh * xn, axis=0, keepdims=True)
        dxn = dh * (g_pre * mod_scale)
        dx_ref[...] = dxo_t + rx * (dxn - xn * jnp.mean(dxn * xn, axis=-1, keepdims=True))

        @pl.when(i == n_tiles - 1)
        def _():
            gwpool_ref[...] = gwpool_acc[...].astype(BF16)
            sum_dh_xn, sum_dxo_yh = dvec_ref[1:2, :], dvec_ref[2:3, :]
            dvec_ref[1:2, :] = sum_dh_xn * g_pre
            dvec_ref[3:4, :] = sum_dh_xn * mod_scale
            dvec_ref[2:3, :] = sum_dxo_yh * g_post
            dvec_ref[4:5, :] = sum_dxo_yh * gate
            if hosted is not None:
                hosted[1]()

    return pl.pallas_call(
        body, name=f"backward_layer_{layer}", grid=(n_tiles,), in_specs=in_specs, out_specs=out_specs,
        out_shape=out_shape, scratch_shapes=scratch, input_output_aliases=aliases,
        compiler_params=_params(dimension_semantics=("arbitrary",)),
    )(*args)


def _weight_grads(layer, h_t, dproj, ycat_t, dy, exchange):
    t_len = dy.shape[0]
    n_in, n_out = IN_COLS // GWIN_COLS, D_MODEL // GWOUT_COLS
    args = [h_t, dproj, ycat_t, dy]
    in_specs = [_whole_spec(h_t.shape),
                pl.BlockSpec((t_len, GWIN_COLS), lambda s: (0, jnp.minimum(s, n_in - 1))),
                _whole_spec(ycat_t.shape),
                pl.BlockSpec((t_len, GWOUT_COLS), lambda s: (0, jnp.maximum(s - n_in, 0)))]
    out_shape = [jax.ShapeDtypeStruct((D_MODEL, IN_COLS), BF16), jax.ShapeDtypeStruct((D_MODEL, D_MODEL), BF16)]
    out_specs = [pl.BlockSpec((D_MODEL, GWIN_COLS), lambda s: (0, jnp.minimum(s, n_in - 1))),
                 pl.BlockSpec((D_MODEL, GWOUT_COLS), lambda s: (0, jnp.maximum(s - n_in, 0)))]
    scratch = []
    aliases, split = _host(exchange, args, in_specs, out_shape, out_specs, scratch)

    def body(*refs):
        (ht_ref, dproj_ref, ycatt_ref, dy_ref, gwin_ref, gwout_ref), hosted = split(refs)
        s = pl.program_id(0)
        if hosted is not None:
            pl.when(s == 0)(hosted[0])

        @pl.when(s < n_in)
        def _():
            gwin_ref[...] = _dot(ht_ref[...], dproj_ref[...]).astype(BF16)

        @pl.when(s >= n_in)
        def _():
            gwout_ref[...] = _dot(ycatt_ref[...], dy_ref[...]).astype(BF16)

        if hosted is not None:
            pl.when(s == n_in + n_out - 1)(hosted[1])

    return pl.pallas_call(
        body, name=f"weight_grads_{layer}", grid=(n_in + n_out,), in_specs=in_specs, out_specs=out_specs,
        out_shape=out_shape, scratch_shapes=scratch, input_output_aliases=aliases,
        compiler_params=_params(dimension_semantics=("arbitrary",)),
    )(*args)


def _add_sibling_blocks(name, layer, grads, received, core, partials, kinds):
    n_arr = len(grads)

    def body(core_ref, *refs):
        mine, theirs, outs = refs[:n_arr], refs[n_arr:2 * n_arr], refs[-n_arr:]
        for a in range(n_arr):
            outs[a][...] = (mine[a][...].astype(F32) + theirs[a][...].astype(F32)).astype(BF16)

    own_of_kind = [
        pl.BlockSpec((D_MODEL, W_IN_SHARD), lambda q, core_ref: (0, 2 * q + core_ref[0])),
        pl.BlockSpec((W_OUT_SHARD, D_MODEL), lambda q, core_ref: (2 * q + core_ref[0], 0)),
        pl.BlockSpec((POOL_SHARD, GROUP_D), lambda q, core_ref: (2 * q + core_ref[0], 0)),
    ]
    shapes = [_BLOCK_SHAPES[k] for k in kinds]
    recv_specs = [pl.BlockSpec((None,) + s, lambda q, core_ref: (q, 0, 0)) for s in shapes]
    out_specs = [pl.BlockSpec((None, None) + s, lambda q, core_ref: (q, layer, 0, 0)) for s in shapes]
    args = [core, *grads, *received]
    in_specs = [own_of_kind[k] for k in kinds] + recv_specs
    aliases = {}
    if partials is not None:
        aliases = {len(args) + a: a for a in range(n_arr)}
        args += list(partials)
        in_specs += [HBM] * n_arr
    return pl.pallas_call(
        body, name=name,
        grid_spec=pltpu.PrefetchScalarGridSpec(
            num_scalar_prefetch=1, grid=(N_CHIP,), in_specs=in_specs, out_specs=out_specs),
        out_shape=[jax.ShapeDtypeStruct((N_CHIP, DEPTH) + s, BF16) for s in shapes],
        input_output_aliases=aliases,
        compiler_params=_params(dimension_semantics=("arbitrary",)),
    )(*args)


def _modulation_columns(c_all, w_ada):
    def body(c_ref, w_ref, cact_ref, out_ref):
        c_t = c_ref[...]
        c_act = c_t * _sigmoid(c_t)
        cact_ref[...] = c_act
        out_ref[...] = jnp.dot(c_act, w_ref[...], preferred_element_type=F32, precision=lax.Precision.HIGHEST)

    return pl.pallas_call(
        body, name="modulation_columns", grid=(DEPTH,),
        in_specs=[pl.BlockSpec((N_DEV, D_MODEL), lambda l: (0, 0)),
                  pl.BlockSpec((None, D_MODEL, W_IN_SHARD), lambda l: (l, 0, 0))],
        out_specs=[pl.BlockSpec((N_DEV, D_MODEL), lambda l: (0, 0)),
                   pl.BlockSpec((N_DEV, W_IN_SHARD), lambda l: (0, l))],
        out_shape=[jax.ShapeDtypeStruct((N_DEV, D_MODEL), F32),
                   jax.ShapeDtypeStruct((N_DEV, DEPTH * W_IN_SHARD), F32)],
        compiler_params=_params(dimension_semantics=("arbitrary",)),
    )(c_all, w_ada)


def _adamw(w, g, m, v):
    m_new = ADAM_B1 * m + (1.0 - ADAM_B1) * g
    v_new = ADAM_B2 * v + (1.0 - ADAM_B2) * (g * g)
    m_hat = m_new / (1.0 - ADAM_B1 ** ADAM_STEP)
    v_hat = v_new / (1.0 - ADAM_B2 ** ADAM_STEP)
    delta = -ADAM_LR * (m_hat / (jnp.sqrt(v_hat) + ADAM_EPS) + ADAM_WD * w)
    return delta, m_new, v_new


def _adamw_w_ada(w, m, v, c_act_t, dmod_cols, exchange):
    big = pl.BlockSpec((None, D_MODEL, W_IN_SHARD), lambda l: (l, 0, 0))
    args = [w, m, v, c_act_t, dmod_cols]
    in_specs = [big, big, big, pl.BlockSpec((D_MODEL, N_DEV), lambda l: (0, 0)),
                pl.BlockSpec((None, N_DEV, W_IN_SHARD), lambda l: (l, 0, 0))]
    out_shape, out_specs, scratch = [jax.ShapeDtypeStruct(w.shape, F32)] * 4, [big] * 4, []
    aliases, split = _host(exchange, args, in_specs, out_shape, out_specs, scratch)

    def body(*refs):
        (w_ref, m_ref, v_ref, ct_ref, dm_ref, g_ref, d_ref, mo_ref, vo_ref), hosted = split(refs)
        if hosted is not None:
            pl.when(pl.program_id(0) == 0)(hosted[0])
        g = ct_ref[:, 0:1] * dm_ref[0:1, :]
        for b in range(1, N_DEV):
            g = g + ct_ref[:, b:b + 1] * dm_ref[b:b + 1, :]
        g_ref[...] = g
        d_ref[...], mo_ref[...], vo_ref[...] = _adamw(w_ref[...], g, m_ref[...], v_ref[...])
        if hosted is not None:
            pl.when(pl.program_id(0) == DEPTH - 1)(hosted[1])

    return pl.pallas_call(
        body, name="adamw_w_ada", grid=(DEPTH,), in_specs=in_specs, out_specs=out_specs, out_shape=out_shape,
        scratch_shapes=scratch, input_output_aliases=aliases,
        compiler_params=_params(dimension_semantics=("arbitrary",)),
    )(*args)


def _sum_chip_partials(own_ref, recv_ref):
    g = own_ref[...].astype(F32)
    for j in range(N_OTHER_CHIPS):
        g = g + recv_ref[j].astype(F32)
    return g


def _partial_specs(row_tile, cols, first_layer=0):
    own = pl.BlockSpec((None, None, row_tile, cols), lambda l, r, chip_ref: (chip_ref[0], first_layer + l, r, 0))
    recv = pl.BlockSpec((N_OTHER_CHIPS, None, row_tile, cols), lambda l, r, chip_ref: (0, first_layer + l, r, 0))
    return own, recv


def _adamw_reduced(name, w, m, v, partial, received, chip, row_tile, layers, continued):
    depth, rows, cols = w.shape
    first, stop = layers

    def body(chip_ref, w_ref, m_ref, v_ref, own_ref, recv_ref, *rest):
        g_ref, d_ref, mo_ref, vo_ref = rest[-4:]
        g = _sum_chip_partials(own_ref, recv_ref)
        g_ref[...] = g
        d_ref[...], mo_ref[...], vo_ref[...] = _adamw(w_ref[...], g, m_ref[...], v_ref[...])

    blk = pl.BlockSpec((None, row_tile, cols), lambda l, r, chip_ref: (first + l, r, 0))
    args = [chip, w, m, v, partial, received]
    in_specs = [blk, blk, blk, *_partial_specs(row_tile, cols, first)]
    aliases = {}
    if continued is not None:
        aliases = {len(args) + k: k for k in range(4)}
        args += list(continued)
        in_specs += [HBM] * 4
    return pl.pallas_call(
        body, name=name,
        grid_spec=pltpu.PrefetchScalarGridSpec(
            num_scalar_prefetch=1, grid=(stop - first, rows // row_tile), in_specs=in_specs, out_specs=[blk] * 4),
        out_shape=[jax.ShapeDtypeStruct(w.shape, F32)] * 4, input_output_aliases=aliases,
        compiler_params=_params(dimension_semantics=("arbitrary", "arbitrary")),
    )(*args)


def _reduce_w_pool(partial, received, chip):
    def body(chip_ref, own_ref, recv_ref, g_ref):
        g_ref[...] = _sum_chip_partials(own_ref, recv_ref)

    return pl.pallas_call(
        body, name="reduce_w_pool",
        grid_spec=pltpu.PrefetchScalarGridSpec(
            num_scalar_prefetch=1, grid=(DEPTH, 1), in_specs=list(_partial_specs(POOL_SHARD, GROUP_D)),
            out_specs=pl.BlockSpec((POOL_SHARD, GROUP_D), lambda l, r, chip_ref: (l, 0))),
        out_shape=jax.ShapeDtypeStruct((DEPTH * POOL_SHARD, GROUP_D), F32),
        compiler_params=_params(dimension_semantics=("arbitrary", "arbitrary")),
    )(chip, partial, received)


def _adamw_small(params):
    n = len(params)

    def body(*refs):
        ins, outs = refs[:4 * n], refs[4 * n:]
        for p in range(n):
            w_ref, g_ref, m_ref, v_ref = ins[4 * p:4 * p + 4]
            d_ref, mo_ref, vo_ref = outs[3 * p:3 * p + 3]
            d_ref[...], mo_ref[...], vo_ref[...] = _adamw(w_ref[...], g_ref[...], m_ref[...], v_ref[...])

    vmem = pl.BlockSpec(memory_space=pltpu.VMEM)
    flat = [a for group in params for a in group]
    out_shape = [jax.ShapeDtypeStruct(group[0].shape, F32) for group in params for _ in range(3)]
    outs = pl.pallas_call(
        body, name="adamw_small", in_specs=[vmem] * len(flat), out_specs=[vmem] * len(out_shape),
        out_shape=out_shape, compiler_params=_params(),
    )(*flat)
    return [tuple(outs[3 * p:3 * p + 3]) for p in range(n)]


def _sum_sources(slabs):
    def body(s_ref, o_ref):
        acc = s_ref[0]
        for b in range(1, N_DEV):
            acc = acc + s_ref[b]
        o_ref[...] = acc

    vmem = pl.BlockSpec(memory_space=pltpu.VMEM)
    return pl.pallas_call(
        body, name="sum_small_grads", in_specs=[vmem], out_specs=vmem,
        out_shape=jax.ShapeDtypeStruct(slabs.shape[1:], F32), compiler_params=_params(),
    )(slabs)


def _to_bf16(a, name, layers=None):
    first, stop = layers or (0, a.shape[0])

    def body(a_ref, o_ref):
        o_ref[...] = a_ref[...].astype(BF16)

    block = (None,) + a.shape[1:]
    return pl.pallas_call(
        body, name=name, grid=(stop - first,), in_specs=[pl.BlockSpec(block, lambda l: (first + l, 0, 0))],
        out_specs=pl.BlockSpec(block, lambda l: (l, 0, 0)),
        out_shape=jax.ShapeDtypeStruct((stop - first,) + a.shape[1:], BF16),
        compiler_params=_params(dimension_semantics=("arbitrary",)),
    )(a)


def kernel(x, c, w_ada, b_ada, g_pre, w_in, w_conv, w_pool, pool_scale, w_out, g_post, loss_target, m_w_ada, m_b_ada, m_g_pre, m_w_in, m_w_conv, m_w_pool, m_pool_scale, m_w_out, m_g_post, v_w_ada, v_b_ada, v_g_pre, v_w_in, v_w_conv, v_w_pool, v_pool_scale, v_w_out, v_g_post):
    mx, my, mc = _mesh_position()
    me = _block_id(mx, my, mc)
    chip = (2 * mx + my).astype(jnp.int32).reshape(1)
    core = mc.astype(jnp.int32).reshape(1)
    x0 = x[0]
    target = loss_target[0]
    conv_shard = w_conv.shape[-1]

    own_small = jnp.concatenate([c, w_conv.reshape(1, DEPTH * 3 * conv_shard)], axis=1)
    all_small = _all_gather_small(own_small, "all_gather_c_w_conv")[:, 0, :]
    c_all = all_small[:, :D_MODEL]
    w_conv_full = all_small[:, D_MODEL:].reshape(N_DEV, DEPTH, 3, conv_shard).transpose(1, 2, 0, 3).reshape(
        DEPTH, 3, CONV_W)
    cps = jnp.concatenate([w_conv_full, pool_scale[:, None], jnp.zeros((DEPTH, 4, CONV_W), F32)], axis=1)

    c_act, pieces = _modulation_columns(c_all, w_ada)
    mod_all = _all_gather_small(pieces, "all_gather_modulation")
    mod_mine = lax.dynamic_index_in_dim(mod_all, me, axis=1, keepdims=False)
    mod = mod_mine.reshape(N_DEV, DEPTH, W_IN_SHARD).transpose(1, 0, 2).reshape(DEPTH, 3 * D_MODEL) + b_ada
    zeros_d = jnp.zeros((DEPTH, 3, D_MODEL), F32)
    vec = jnp.concatenate([mod.reshape(DEPTH, 3, D_MODEL), g_pre[:, None], g_post[:, None], zeros_d], axis=1)

    gathers = []
    after = [mod_all, all_small]
    for group in ((0, 1), (1, DEPTH)):
        first_sems, shards, landing = _gather_weights_start(
            group[0], _to_bf16(w_in, f"cast_w_in_{group[0]}", group), _to_bf16(w_out, f"cast_w_out_{group[0]}", group),
            after)
        gathers += [(first_sems, shards, landing[k], k) for k in range(group[1] - group[0])]
        after = [shards[0]]
    wpool_b = _to_bf16(w_pool.reshape(DEPTH, POOL_ROWS, GROUP_D), "cast_w_pool").reshape(w_pool.shape)

    xs, projs, ys, wins, wouts = [x0], [], [], [], []
    for l in range(DEPTH):
        first_sems, shards, zones, index = gathers[l]
        passed_sems, zones = _gather_weights_pass_on(
            l, index, first_sems[1], zones, [vec, cps, wpool_b] if l == 0 else [xs[-1]])
        win, wout = _gather_weights_finish(l, index, first_sems, passed_sems, shards, zones)
        x_next, proj, y = _forward_layer(l, xs[-1], vec, cps, wpool_b, win, wout, None)
        xs.append(x_next)
        projs.append(proj)
        ys.append(y)
        wins.append(win)
        wouts.append(wout)
    dx, loss_tile = _loss_head(xs[DEPTH], target)

    slab_rows = [None] * DEPTH
    partials = received = None
    in_flight = []

    def scatter(layer, grads, from_sibling, after):
        nonlocal partials, received
        partials = _add_sibling_blocks(
            f"grad_add_sibling_{layer}", layer, grads, from_sibling, core, partials, ALL_KINDS)
        chips = _chips_exchange(layer, partials, received, ALL_KINDS)
        sems, partials, received, token = _start_exchange(chips, f"grad_chips_start_{layer}", after)
        in_flight.append((chips, sems, layer))
        return token

    grads_above = None
    for l in reversed(range(DEPTH)):
        dx, dproj, h_t, ycat_t, dy, gwpool, dcps, dvec = _backward_layer(
            l, dx, ys[l], projs[l], xs[l], vec, cps, wpool_b, wouts[l], wins[l], None)
        slab_rows[l] = jnp.concatenate(
            [dvec[0], dvec[1], dvec[2], dvec[3], dvec[4], dcps[3], dcps[0], dcps[1], dcps[2],
             loss_tile[0] if l == 0 else jnp.zeros((LANES,), F32)])
        if l == 0:
            slabs = _all_gather_small(jnp.stack(slab_rows), "all_gather_small_grads")
        hosted = _sibling_exchange(grads_above, ALL_KINDS) if grads_above is not None else None
        gwin, gwout, *from_sibling = _weight_grads(l, h_t, dproj, ycat_t, dy, hosted)
        if grads_above is not None:
            scatter(l + 1, grads_above, from_sibling, [])
        grads_above = [gwin, gwout, gwpool.reshape(POOL_ROWS, GROUP_D)]
        if l <= 1:
            from_sibling = _run_exchange(_sibling_exchange(grads_above, ALL_KINDS), f"grad_exchange_sibling_{l}")
            token = scatter(l, grads_above, from_sibling, [slabs] if l == 0 else [])
            grads_above = None
    grad_x = dx[None]
    chips_0, sems_0, _ = in_flight.pop()

    total = _sum_sources(slabs)
    loss = total[0, SLAB_COLS]
    o = 3 * D_MODEL
    g_b_ada = total[:, :o]
    g_g_pre = total[:, o:o + D_MODEL]
    g_g_post = total[:, o + D_MODEL:o + 2 * D_MODEL]
    g_pool_scale = total[:, o + 2 * D_MODEL:o + 2 * D_MODEL + POOL_W]
    g_conv_full = total[:, o + 2 * D_MODEL + POOL_W:SLAB_COLS].reshape(DEPTH, 3, CONV_W)
    g_w_conv = lax.dynamic_slice_in_dim(g_conv_full, me * conv_shard, conv_shard, axis=2)

    after = [token]
    for chips, sems, l in in_flight:
        partials, received = _finish_exchange(chips, f"grad_chips_finish_{l}", sems, partials, received, after)
        after = []
    upper = (1, DEPTH)
    w_in_upper = _adamw_reduced(
        "adamw_w_in_upper", w_in, m_w_in, v_w_in, partials[0], received[0], chip, ROW_TILE, upper, None)
    w_out_upper = _adamw_reduced(
        "adamw_w_out_upper", w_out, m_w_out, v_w_out, partials[1], received[1], chip, W_OUT_SHARD, upper, None)
    dmod_all = slabs[:, :, :o].reshape(N_DEV, DEPTH, N_DEV, W_IN_SHARD)
    dmod_cols = lax.dynamic_index_in_dim(dmod_all, me, axis=2, keepdims=False).transpose(1, 0, 2) + token[0, 0]
    g_w_ada, d_w_ada, nm_w_ada, nv_w_ada = _adamw_w_ada(w_ada, m_w_ada, v_w_ada, c_act.T, dmod_cols, None)

    partials, received = _finish_exchange(
        chips_0, "grad_chips_finish_0", sems_0, partials, received, [nv_w_ada, w_in_upper[3], w_out_upper[3]])
    gather_pool = _all_gather_exchange(_reduce_w_pool(partials[2], received[2], chip))
    sems_p, pool_rows, pool_landing, token_p = _start_exchange(gather_pool, "all_gather_grad_w_pool_start")
    g_w_in, d_w_in, nm_w_in, nv_w_in = _adamw_reduced(
        "adamw_w_in_0", w_in, m_w_in, v_w_in, partials[0], received[0], chip, ROW_TILE, (0, 1), w_in_upper)
    g_w_out, d_w_out, nm_w_out, nv_w_out = _adamw_reduced(
        "adamw_w_out_0", w_out, m_w_out, v_w_out, partials[1], received[1], chip, W_OUT_SHARD, (0, 1), w_out_upper)
    _, (g_pool_all,) = _finish_exchange(
        gather_pool, "all_gather_grad_w_pool_finish", sems_p, pool_rows, pool_landing, [nv_w_in, nv_w_out])
    g_w_pool = g_pool_all.reshape(N_DEV, DEPTH, POOL_SHARD, GROUP_D).transpose(1, 0, 2, 3).reshape(w_pool.shape)

    flat2 = lambda a: a.reshape(-1, a.shape[-1])
    small = _adamw_small([
        (b_ada, g_b_ada, m_b_ada, v_b_ada),
        (g_pre, g_g_pre, m_g_pre, v_g_pre),
        (flat2(w_conv), flat2(g_w_conv), flat2(m_w_conv), flat2(v_w_conv)),
        (flat2(w_pool), flat2(g_w_pool), flat2(m_w_pool), flat2(v_w_pool)),
        (pool_scale, g_pool_scale, m_pool_scale, v_pool_scale),
        (g_post, g_g_post, m_g_post, v_g_post),
    ])
    (d_b_ada, nm_b_ada, nv_b_ada), (d_g_pre, nm_g_pre, nv_g_pre), conv_upd, pool_upd, \
        (d_ps, nm_ps, nv_ps), (d_g_post, nm_g_post, nv_g_post) = small
    d_w_conv, nm_w_conv, nv_w_conv = (a.reshape(w_conv.shape) for a in conv_upd)
    d_w_pool, nm_w_pool, nv_w_pool = (a.reshape(w_pool.shape) for a in pool_upd)

    return (loss, grad_x,
            g_w_ada, g_b_ada, g_g_pre, g_w_in, g_w_conv, g_w_pool, g_pool_scale, g_w_out, g_g_post,
            d_w_ada, d_b_ada, d_g_pre, d_w_in, d_w_conv, d_w_pool, d_ps, d_w_out, d_g_post,
            nm_w_ada, nm_b_ada, nm_g_pre, nm_w_in, nm_w_conv, nm_w_pool, nm_ps, nm_w_out, nm_g_post,
            nv_w_ada, nv_b_ada, nv_g_pre, nv_w_in, nv_w_conv, nv_w_pool, nv_ps, nv_w_out, nv_g_post)
```

```python
import jax
import jax.numpy as jnp
from jax import lax
from jax.experimental import pallas as pl
from jax.experimental.pallas import tpu as pltpu

F32 = jnp.float32
BF16 = jnp.bfloat16

D_MODEL = 1024
DEPTH = 4
CONV_W = 512
POOL_W = 512
POOL_WINDOWS = (2, 4, 8, 16)
GROUP_D = 128
IN_COLS = 4 * CONV_W + 2 * POOL_W
NORM_EPS = 1e-6

ADAM_LR = 0.001
ADAM_B1 = 0.9
ADAM_B2 = 0.999
ADAM_EPS = 1e-08
ADAM_WD = 0.01
ADAM_STEP = 10

N_DEV = 8
N_CHIP = 4
N_OTHER_CHIPS = N_CHIP - 1
MESH = pl.DeviceIdType.MESH
W_IN_SHARD = IN_COLS // N_DEV
W_OUT_SHARD = D_MODEL // N_DEV
POOL_ROWS = len(POOL_WINDOWS) * GROUP_D
POOL_SHARD = POOL_ROWS // N_DEV

SUBLANES = 8
LANES = 128
VMEM_LIMIT_BYTES = 56 * 1024 * 1024
ROW_TILE = 512
BWD_TILE = 256
GWIN_COLS = 768
GWOUT_COLS = 512
POOL_HALO = 16
CONV_HALO = SUBLANES

SLAB_COLS = 3 * D_MODEL + D_MODEL + D_MODEL + POOL_W + 3 * CONV_W

HBM = pl.BlockSpec(memory_space=pl.ANY)


def _params(**kw):
    return pltpu.CompilerParams(vmem_limit_bytes=VMEM_LIMIT_BYTES, **kw)


def _sigmoid(v):
    return 1.0 / (1.0 + jnp.exp(-v))


def _dot(a, b):
    return jnp.dot(a, b, preferred_element_type=F32)


def _dot_tn(a, b):
    return lax.dot_general(a, b, (((0,), (0,)), ((), ())), preferred_element_type=F32)


def _dot_nt(a, b):
    return lax.dot_general(a, b, (((1,), (1,)), ((), ())), preferred_element_type=F32)


def _rows_from_before(v, k):
    return pltpu.roll(v, k, 0)


def _rows_from_after(v, k):
    return pltpu.roll(v, v.shape[0] - k, 0)


def _window_counts(t0, rows):
    return (lax.broadcasted_iota(jnp.int32, (rows, 1), 0) + (t0 + 1)).astype(F32)


def _split_proj(p32):
    cw = CONV_W
    return (p32[:, 0 * cw:1 * cw], p32[:, 1 * cw:2 * cw], p32[:, 2 * cw:3 * cw], p32[:, 3 * cw:4 * cw],
            p32[:, 4 * cw:4 * cw + POOL_W], p32[:, 4 * cw + POOL_W:])


def _mixer_forward(p32, z_halo, up_halo, cps, wpool_ref, t0):
    tm = p32.shape[0]
    u_a, b_a, c_a, g_a, u_p, g_p = _split_proj(p32)
    w0, w1, w2, ps = cps[0:1, :], cps[1:2, :], cps[2:3, :], cps[3:4, :]
    z = c_a * u_a
    zcat = jnp.concatenate([z_halo, z], axis=0)
    z1 = _rows_from_before(zcat, 1)[CONV_HALO:]
    z2 = _rows_from_before(zcat, 2)[CONV_HALO:]
    conv = w0 * z2 + w1 * z1 + w2 * z
    sig_a = _sigmoid(g_a)
    silu_a = g_a * sig_a
    y_a = b_a * conv * silu_a

    pcat = jnp.concatenate([up_halo, u_p], axis=0)
    counts = _window_counts(t0, tm)
    pooled, mixed, inv = [], [], []
    for g, w in enumerate(POOL_WINDOWS):
        cols = slice(g * GROUP_D, (g + 1) * GROUP_D)
        s = pcat[:, cols]
        step = 1
        while step < w:
            s = s + _rows_from_before(s, step)
            step *= 2
        inv_g = 1.0 / jnp.minimum(counts, float(w))
        pooled_g = s[POOL_HALO:] * inv_g - u_p[:, cols]
        pooled.append(pooled_g)
        inv.append(inv_g)
        mixed.append(_dot(pooled_g.astype(BF16), wpool_ref[g]))
    mixed = jnp.concatenate(mixed, axis=1)
    sig_p = _sigmoid(g_p)
    silu_p = g_p * sig_p
    y_p = mixed * ps * silu_p
    ycat = jnp.concatenate([y_a, y_p], axis=1)
    return dict(u_a=u_a, b_a=b_a, c_a=c_a, g_a=g_a, u_p=u_p, g_p=g_p, z=z, z1=z1, z2=z2, conv=conv, sig_a=sig_a,
                silu_a=silu_a, pooled=pooled, inv=inv, mixed=mixed, sig_p=sig_p, silu_p=silu_p, ycat=ycat,
                w0=w0, w1=w1, w2=w2, ps=ps)


def _layer_spec(shape, layer):
    nd = len(shape)
    return pl.BlockSpec((None,) + tuple(shape[1:]), lambda i, _l=layer, _n=nd: (_l,) + (0,) * (_n - 1))


def _whole_spec(shape):
    return pl.BlockSpec(tuple(shape), lambda i, _n=len(shape): (0,) * _n)


def _mesh_position():
    return lax.axis_index("x"), lax.axis_index("y"), lax.axis_index("c")


def _block_id(x, y, c):
    return 4 * x + 2 * y + c


def _other_chips(x, y):
    return [(x ^ 1, y), (x, y ^ 1), (x ^ 1, y ^ 1)]


def _col_block(ref, blk):
    return ref.at[:, pl.ds(pl.multiple_of(blk * W_IN_SHARD, LANES), W_IN_SHARD)]


def _row_block(rows):
    def block(ref, blk):
        return ref.at[pl.ds(pl.multiple_of(blk * rows, rows), rows), :]
    return block


_BLOCK_OF = (_col_block, _row_block(W_OUT_SHARD), _row_block(POOL_SHARD))
_BLOCK_SHAPES = ((D_MODEL, W_IN_SHARD), (W_OUT_SHARD, D_MODEL), (POOL_SHARD, GROUP_D))


class _Exchange:
    def __init__(self, inputs, out_shapes, aliases, sem_shapes, make):
        self.inputs, self.out_shapes, self.aliases, self.sem_shapes, self.make = (
            list(inputs), list(out_shapes), dict(aliases), list(sem_shapes), make)


def _run_exchange(exchange, name):
    n_in, n_out = len(exchange.inputs), len(exchange.out_shapes)

    def body(*refs):
        start, finish = exchange.make(refs[:n_in], refs[n_in:n_in + n_out], refs[n_in + n_out:])
        start()
        finish()

    return pl.pallas_call(
        body, name=name, in_specs=[HBM] * n_in, out_specs=[HBM] * n_out, out_shape=exchange.out_shapes,
        scratch_shapes=exchange.sem_shapes, input_output_aliases=exchange.aliases, compiler_params=_params(),
    )(*exchange.inputs)


_SEM = pl.BlockSpec(memory_space=pltpu.SEMAPHORE)
_DATAFLOW = pltpu.SideEffectType.DATAFLOW_SIDE_EFFECTING


def _start_exchange(exchange, name, after=()):
    n_in, n_out, n_sem = len(exchange.inputs), len(exchange.out_shapes), len(exchange.sem_shapes)
    sources = [i for i in range(n_in) if i not in exchange.aliases]
    aliases = {i: n_sem + k for k, i in enumerate(sources)}
    aliases.update({i: n_sem + len(sources) + o for i, o in exchange.aliases.items()})

    def body(*refs):
        in_refs = refs[:n_in]
        outs = refs[n_in + len(after):]
        sems = outs[:n_sem]
        out_refs = outs[n_sem + len(sources):n_sem + len(sources) + n_out]
        exchange.make(in_refs, out_refs, sems)[0]()
        refs[-1][...] = jnp.zeros_like(refs[-1])

    outs = pl.pallas_call(
        body, name=name, in_specs=[HBM] * (n_in + len(after)),
        out_specs=[_SEM] * n_sem + [HBM] * (len(sources) + n_out) + [pl.BlockSpec(memory_space=pltpu.VMEM)],
        out_shape=(exchange.sem_shapes + [pltpu.HBM(exchange.inputs[i].shape, exchange.inputs[i].dtype) for i in sources]
                   + [pltpu.HBM(s.shape, s.dtype) for s in exchange.out_shapes]
                   + [jax.ShapeDtypeStruct((SUBLANES, LANES), F32)]),
        input_output_aliases=aliases, compiler_params=_params(has_side_effects=_DATAFLOW),
    )(*exchange.inputs, *after)
    return outs[:n_sem], outs[n_sem:n_sem + len(sources)], outs[n_sem + len(sources):-1], outs[-1]


def _finish_exchange(exchange, name, sems, sources, landing, after):
    n_src, n_out, n_sem = len(sources), len(landing), len(sems)
    n_in = len(exchange.inputs)
    source_at = [i for i in range(n_in) if i not in exchange.aliases]

    def body(*refs):
        src_refs, out_refs = refs[:n_src], refs[n_src:n_src + n_out]
        sem_refs = refs[n_src + n_out:n_src + n_out + n_sem]
        in_refs = [None] * n_in
        for k, i in enumerate(source_at):
            in_refs[i] = src_refs[k]
        for i, o in exchange.aliases.items():
            in_refs[i] = out_refs[o]
        exchange.make(in_refs, out_refs, sem_refs)[1]()

    arrays = list(sources) + list(landing)
    outs = pl.pallas_call(
        body, name=name, in_specs=[HBM] * len(arrays) + [_SEM] * n_sem + [HBM] * len(after),
        out_specs=[HBM] * len(arrays), out_shape=[pltpu.HBM(a.shape, a.dtype) for a in arrays],
        input_output_aliases={i: i for i in range(len(arrays))}, compiler_params=_params(has_side_effects=_DATAFLOW),
    )(*arrays, *sems, *after)
    return outs[:n_src], outs[n_src:]


N_GATHERED = 2
FIRST_COPIES = 1 + N_OTHER_CHIPS
_GATHERED_SHAPES = ((D_MODEL, IN_COLS), (D_MODEL, D_MODEL))


def _first_sem(layer, a, k):
    return (layer * N_GATHERED + a) * FIRST_COPIES + k


def _gather_copy(window_of, full_ref, blk, send_sem, recv_sem, to, src=None):
    window = window_of(full_ref, blk)
    return pltpu.make_async_remote_copy(
        src_ref=window if src is None else src, dst_ref=window, send_sem=send_sem, recv_sem=recv_sem,
        device_id=to, device_id_type=MESH)


def _first_copies(layer, shard_refs, full_refs, send_sems, recv_sems, local_sems):
    x, y, c = _mesh_position()
    me = _block_id(x, y, c)
    own, remote = [], []
    for a in range(N_GATHERED):
        shard = shard_refs[a].at[layer]
        own.append(pltpu.make_async_copy(
            shard, _BLOCK_OF[a](full_refs[a], me), local_sems.at[layer * N_GATHERED + a]))
        targets = [(x, y, 1 - c)] + [(*chip, c) for chip in _other_chips(x, y)]
        remote += [_gather_copy(_BLOCK_OF[a], full_refs[a], me, send_sems.at[_first_sem(layer, a, k)],
                                recv_sems.at[_first_sem(layer, a, k)], to, src=shard)
                   for k, to in enumerate(targets)]
    return own, remote


def _gather_weights_start(first_layer, win_shards, wout_shards, after):
    n_layers = win_shards.shape[0]
    n_first = n_layers * N_GATHERED * FIRST_COPIES
    sem_shapes = [pltpu.SemaphoreType.DMA((n_first,)), pltpu.SemaphoreType.DMA((n_first,)),
                  pltpu.SemaphoreType.DMA((n_layers * N_GATHERED,))]
    shards = [win_shards, wout_shards]

    def body(win_sh, wout_sh, *rest):
        send_sems, recv_sems, local_sems, win_thru, wout_thru, *landing = rest[len(after):]
        for layer in range(n_layers):
            own, remote = _first_copies(layer, (win_sh, wout_sh), landing[N_GATHERED * layer:N_GATHERED * (layer + 1)],
                                        send_sems, recv_sems, local_sems)
            for cp in own + remote:
                cp.start()

    outs = pl.pallas_call(
        body, name=f"all_gather_weights_start_{first_layer}", in_specs=[HBM] * (2 + len(after)),
        out_specs=[_SEM] * 3 + [HBM] * (2 + n_layers * N_GATHERED),
        out_shape=(sem_shapes + [pltpu.HBM(s.shape, s.dtype) for s in shards]
                   + [pltpu.HBM(s, BF16) for _ in range(n_layers) for s in _GATHERED_SHAPES]),
        input_output_aliases={0: 3, 1: 4}, compiler_params=_params(has_side_effects=_DATAFLOW),
    )(*shards, *after)
    landing = outs[5:]
    return outs[:3], outs[3:5], [landing[N_GATHERED * l:N_GATHERED * (l + 1)] for l in range(n_layers)]


def _passed_on_copies(full_refs, send_sems, recv_sems, core_of_block):
    x, y, c = _mesh_position()
    return [_gather_copy(_BLOCK_OF[a], full_refs[a], _block_id(*chip, core_of_block),
                         send_sems.at[a * N_OTHER_CHIPS + j], recv_sems.at[a * N_OTHER_CHIPS + j], (x, y, 1 - c))
            for a in range(N_GATHERED) for j, chip in enumerate(_other_chips(x, y))]


def _gather_weights_pass_on(layer, index, first_recv_sems, landing, after):
    n = N_GATHERED * N_OTHER_CHIPS

    def body(win_ref, wout_ref, first_recv, *rest):
        send_sems, recv_sems = rest[len(after):len(after) + 2]
        x, y, c = _mesh_position()
        full_refs = (win_ref, wout_ref)
        passed = _passed_on_copies(full_refs, send_sems, recv_sems, c)
        for a in range(N_GATHERED):
            for j, chip in enumerate(_other_chips(x, y)):
                sem = _first_sem(index, a, 1 + j)
                _gather_copy(_BLOCK_OF[a], full_refs[a], _block_id(*chip, c), first_recv.at[sem], first_recv.at[sem],
                             (x, y, c)).wait_recv()
                passed[a * N_OTHER_CHIPS + j].start()

    outs = pl.pallas_call(
        body, name=f"all_gather_weights_pass_on_{layer}", in_specs=[HBM] * N_GATHERED + [_SEM] + [HBM] * len(after),
        out_specs=[_SEM] * 2 + [HBM] * N_GATHERED,
        out_shape=[pltpu.SemaphoreType.DMA((n,)), pltpu.SemaphoreType.DMA((n,))]
        + [pltpu.HBM(a.shape, a.dtype) for a in landing],
        input_output_aliases={a: 2 + a for a in range(N_GATHERED)},
        compiler_params=_params(has_side_effects=_DATAFLOW),
    )(*landing, first_recv_sems, *after)
    return outs[:2], outs[2:]


def _gather_weights_finish(layer, index, first_sems, passed_sems, shards, landing):
    def body(win_ref, wout_ref, first_send, first_recv, local_sems, passed_send, passed_recv, win_sh, wout_sh, *thru):
        x, y, c = _mesh_position()
        full_refs = (win_ref, wout_ref)
        own, remote = _first_copies(index, (win_sh, wout_sh), full_refs, first_send, first_recv, local_sems)
        for a in range(N_GATHERED):
            sem = _first_sem(index, a, 0)
            _gather_copy(_BLOCK_OF[a], full_refs[a], _block_id(x, y, 1 - c), first_recv.at[sem], first_recv.at[sem],
                         (x, y, c)).wait_recv()
        for cp in _passed_on_copies(full_refs, passed_send, passed_recv, 1 - c):
            cp.wait_recv()
        for cp in remote + _passed_on_copies(full_refs, passed_send, passed_recv, c):
            cp.wait_send()
        for cp in own:
            cp.wait()

    return pl.pallas_call(
        body, name=f"all_gather_weights_finish_{layer}", in_specs=[HBM] * N_GATHERED + [_SEM] * 5 + [HBM] * 2,
        out_specs=[HBM] * N_GATHERED, out_shape=[pltpu.HBM(a.shape, a.dtype) for a in landing],
        input_output_aliases={a: a for a in range(N_GATHERED)}, compiler_params=_params(has_side_effects=_DATAFLOW),
    )(*landing, *first_sems, *passed_sems, *shards)


ALL_KINDS = (0, 1, 2)


def _sibling_exchange(grads, kinds):
    n_arr = len(grads)

    def make(in_refs, out_refs, sems):
        send_sems, recv_sems = sems
        x, y, c = _mesh_position()
        copies = [pltpu.make_async_remote_copy(
            src_ref=_BLOCK_OF[kinds[a]](in_refs[a], 2 * q + (1 - c)), dst_ref=out_refs[a].at[q],
            send_sem=send_sems.at[a, q], recv_sem=recv_sems.at[a, q], device_id=(x, y, 1 - c), device_id_type=MESH)
            for a in range(n_arr) for q in range(N_CHIP)]

        def start():
            for cp in copies:
                cp.start()

        def finish():
            for cp in copies:
                cp.wait_recv()
            for cp in copies:
                cp.wait_send()

        return start, finish

    return _Exchange(
        grads, [jax.ShapeDtypeStruct((N_CHIP,) + _BLOCK_SHAPES[k], BF16) for k in kinds], {},
        [pltpu.SemaphoreType.DMA((n_arr, N_CHIP)), pltpu.SemaphoreType.DMA((n_arr, N_CHIP))], make)


def _chips_exchange(layer, partials, received, kinds):
    n_arr = len(partials)

    def make(in_refs, out_refs, sems):
        send_sems, recv_sems = sems
        x, y, c = _mesh_position()
        copies = [pltpu.make_async_remote_copy(
            src_ref=in_refs[a].at[2 * qx + qy, layer], dst_ref=out_refs[a].at[j, layer],
            send_sem=send_sems.at[a * N_OTHER_CHIPS + j], recv_sem=recv_sems.at[a * N_OTHER_CHIPS + j],
            device_id=(qx, qy, c), device_id_type=MESH)
            for a in range(n_arr) for j, (qx, qy) in enumerate(_other_chips(x, y))]

        def start():
            for cp in copies:
                cp.start()

        def finish():
            for cp in copies:
                cp.wait_recv()
            for cp in copies:
                cp.wait_send()

        return start, finish

    inputs = list(partials)
    aliases = {}
    if received is not None:
        inputs += list(received)
        aliases = {n_arr + a: a for a in range(n_arr)}
    return _Exchange(
        inputs, [jax.ShapeDtypeStruct((N_OTHER_CHIPS, DEPTH) + _BLOCK_SHAPES[k], BF16) for k in kinds], aliases,
        [pltpu.SemaphoreType.DMA((n_arr * N_OTHER_CHIPS,)), pltpu.SemaphoreType.DMA((n_arr * N_OTHER_CHIPS,))], make)


def _all_gather_exchange(v):
    def make(in_refs, out_refs, sems):
        send_sems, recv_sems, local_sem = sems
        x, y, c = _mesh_position()
        me = _block_id(x, y, c)
        own = pltpu.make_async_copy(in_refs[0], out_refs[0].at[me], local_sem.at[0])
        sends, arrivals = [], []
        for k in range(1, N_DEV):
            px, py, pc = x ^ ((k >> 2) & 1), y ^ ((k >> 1) & 1), c ^ (k & 1)
            sends.append(pltpu.make_async_remote_copy(
                src_ref=in_refs[0], dst_ref=out_refs[0].at[me], send_sem=send_sems.at[k - 1],
                recv_sem=recv_sems.at[k - 1], device_id=(px, py, pc), device_id_type=MESH))
            arrivals.append(pltpu.make_async_remote_copy(
                src_ref=in_refs[0], dst_ref=out_refs[0].at[_block_id(px, py, pc)], send_sem=send_sems.at[k - 1],
                recv_sem=recv_sems.at[k - 1], device_id=(x, y, c), device_id_type=MESH))

        def start():
            for cp in [own] + sends:
                cp.start()

        def finish():
            for cp in arrivals:
                cp.wait_recv()
            for cp in sends:
                cp.wait_send()
            own.wait()

        return start, finish

    return _Exchange(
        [v], [jax.ShapeDtypeStruct((N_DEV,) + v.shape, v.dtype)], {},
        [pltpu.SemaphoreType.DMA((N_DEV - 1,)), pltpu.SemaphoreType.DMA((N_DEV - 1,)),
         pltpu.SemaphoreType.DMA((1,))], make)


def _host(exchange, args, in_specs, out_shape, out_specs, scratch):
    n_own = (len(args), len(out_shape), len(scratch))
    if exchange is None:
        return {}, lambda refs: (refs, None)
    n_ex = (len(exchange.inputs), len(exchange.out_shapes), len(exchange.sem_shapes))
    aliases = {n_own[0] + i: n_own[1] + o for i, o in exchange.aliases.items()}
    args += exchange.inputs
    in_specs += [HBM] * n_ex[0]
    out_shape += exchange.out_shapes
    out_specs += [HBM] * n_ex[1]
    scratch += exchange.sem_shapes

    def split(refs):
        own, theirs, at = [], [], 0
        for mine, ex in zip(n_own, n_ex):
            own += refs[at:at + mine]
            theirs.append(refs[at + mine:at + mine + ex])
            at += mine + ex
        return own, exchange.make(*theirs)

    return aliases, split


def _all_gather_small(v, name, exchange=None):
    vmem = pl.BlockSpec(memory_space=pltpu.VMEM)
    args, in_specs = [v], [vmem]
    out_shape, out_specs = [jax.ShapeDtypeStruct((N_DEV,) + v.shape, v.dtype)], [vmem]
    scratch = [pltpu.SemaphoreType.DMA((N_DEV - 1,)), pltpu.SemaphoreType.DMA((N_DEV - 1,))]
    aliases, split = _host(exchange, args, in_specs, out_shape, out_specs, scratch)

    def body(*refs):
        (v_ref, out_ref, send_sems, recv_sems), hosted = split(refs)
        if hosted is not None:
            hosted[0]()
        x, y, c = _mesh_position()
        me = _block_id(x, y, c)
        out_ref[me] = v_ref[...]
        sends = []
        for k in range(1, N_DEV):
            px, py, pc = x ^ ((k >> 2) & 1), y ^ ((k >> 1) & 1), c ^ (k & 1)
            send = pltpu.make_async_remote_copy(
                src_ref=v_ref, dst_ref=out_ref.at[me], send_sem=send_sems.at[k - 1], recv_sem=recv_sems.at[k - 1],
                device_id=(px, py, pc), device_id_type=MESH)
            send.start()
            sends.append((send, _block_id(px, py, pc)))
        for k, (send, peer) in enumerate(sends):
            pltpu.make_async_remote_copy(
                src_ref=v_ref, dst_ref=out_ref.at[peer], send_sem=send_sems.at[k], recv_sem=recv_sems.at[k],
                device_id=(x, y, c), device_id_type=MESH).wait_recv()
        for send, _ in sends:
            send.wait_send()
        if hosted is not None:
            hosted[1]()

    outs = pl.pallas_call(
        body, name=name, in_specs=in_specs, out_specs=out_specs, out_shape=out_shape, scratch_shapes=scratch,
        input_output_aliases=aliases, compiler_params=_params(),
    )(*args)
    return outs[0] if exchange is None else outs


def _forward_layer(layer, x, vec, cps, wpool, win, wout, exchange):
    t_len = x.shape[0]
    n_tiles = t_len // ROW_TILE
    row = lambda cols: pl.BlockSpec((ROW_TILE, cols), lambda i: (i, 0))
    args = [x, vec, cps, wpool, win, wout]
    in_specs = [row(D_MODEL), _layer_spec(vec.shape, layer), _layer_spec(cps.shape, layer),
                _layer_spec(wpool.shape, layer), _whole_spec(win.shape), _whole_spec(wout.shape)]
    out_shape = [jax.ShapeDtypeStruct((t_len, D_MODEL), F32), jax.ShapeDtypeStruct((t_len, IN_COLS), BF16),
                 jax.ShapeDtypeStruct((t_len, D_MODEL), BF16)]
    out_specs = [row(D_MODEL), row(IN_COLS), row(D_MODEL)]
    scratch = [pltpu.VMEM((CONV_HALO, CONV_W), F32), pltpu.VMEM((POOL_HALO, POOL_W), F32)]
    aliases, split = _host(exchange, args, in_specs, out_shape, out_specs, scratch)

    def body(*refs):
        (x_ref, vec_ref, cps_ref, wpool_ref, win_ref, wout_ref, xo_ref, proj_ref, y_ref, zc_ref, pc_ref), hosted = (
            split(refs))
        i = pl.program_id(0)

        @pl.when(i == 0)
        def _():
            if hosted is not None:
                hosted[0]()
            zc_ref[...] = jnp.zeros_like(zc_ref)
            pc_ref[...] = jnp.zeros_like(pc_ref)

        x_t = x_ref[...]
        shift, scale, gate = vec_ref[0:1, :], vec_ref[1:2, :], vec_ref[2:3, :]
        g_pre, g_post = vec_ref[3:4, :], vec_ref[4:5, :]
        rx = lax.rsqrt(jnp.mean(x_t * x_t, axis=-1, keepdims=True) + NORM_EPS)
        h = (x_t * rx) * g_pre * (1.0 + scale) + shift
        proj_b = _dot(h.astype(BF16), win_ref[...]).astype(BF16)
        proj_ref[...] = proj_b
        mx = _mixer_forward(proj_b.astype(F32), zc_ref[...], pc_ref[...], cps_ref[...], wpool_ref, i * ROW_TILE)
        zc_ref[...] = mx["z"][ROW_TILE - CONV_HALO:]
        pc_ref[...] = mx["u_p"][ROW_TILE - POOL_HALO:]
        y_b = _dot(mx["ycat"].astype(BF16), wout_ref[...]).astype(BF16)
        y_ref[...] = y_b
        y_t = y_b.astype(F32)
        ry = lax.rsqrt(jnp.mean(y_t * y_t, axis=-1, keepdims=True) + NORM_EPS)
        xo_ref[...] = x_t + gate * (y_t * ry * g_post)

        if hosted is not None:
            pl.when(i == n_tiles - 1)(hosted[1])

    return pl.pallas_call(
        body, name=f"forward_layer_{layer}", grid=(n_tiles,), in_specs=in_specs, out_specs=out_specs,
        out_shape=out_shape, scratch_shapes=scratch, input_output_aliases=aliases,
        compiler_params=_params(dimension_semantics=("arbitrary",)),
    )(*args)


def _loss_head(x_final, target):
    t_len = x_final.shape[0]
    n_tiles = t_len // ROW_TILE

    def body(x_ref, t_ref, dx_ref, loss_ref):
        @pl.when(pl.program_id(0) == 0)
        def _():
            loss_ref[...] = jnp.zeros_like(loss_ref)

        err = x_ref[...] - t_ref[...]
        dx_ref[...] = err * (1.0 / D_MODEL)
        loss_ref[...] += jnp.sum(err * err) * (0.5 / D_MODEL)

    row = pl.BlockSpec((ROW_TILE, D_MODEL), lambda i: (i, 0))
    return pl.pallas_call(
        body, name="loss_head", grid=(n_tiles,), in_specs=[row, row],
        out_specs=[row, pl.BlockSpec((SUBLANES, LANES), lambda i: (0, 0))],
        out_shape=[jax.ShapeDtypeStruct((t_len, D_MODEL), F32), jax.ShapeDtypeStruct((SUBLANES, LANES), F32)],
        compiler_params=_params(dimension_semantics=("arbitrary",)),
    )(x_final, target)


def _backward_layer(layer, dxo, y, proj, x, vec, cps, wpool, wout, win, exchange):
    t_len = dxo.shape[0]
    n_tiles = t_len // BWD_TILE
    halo_per_tile = BWD_TILE // POOL_HALO
    rev = lambda cols: pl.BlockSpec((BWD_TILE, cols), lambda i: (n_tiles - 1 - i, 0))
    rev_t = pl.BlockSpec((D_MODEL, BWD_TILE), lambda i: (0, n_tiles - 1 - i))
    halo_spec = pl.BlockSpec(
        (POOL_HALO, IN_COLS), lambda i: (jnp.maximum((n_tiles - 1 - i) * halo_per_tile - 1, 0), 0))
    gwpool_shape = (len(POOL_WINDOWS), GROUP_D, GROUP_D)
    args = [dxo, y, proj, proj, x, vec, cps, wpool, wout, win]
    in_specs = [rev(D_MODEL), rev(D_MODEL), rev(IN_COLS), halo_spec, rev(D_MODEL), _layer_spec(vec.shape, layer),
                _layer_spec(cps.shape, layer), _layer_spec(wpool.shape, layer), _whole_spec(wout.shape),
                _whole_spec(win.shape)]
    out_shape = [jax.ShapeDtypeStruct((t_len, D_MODEL), F32), jax.ShapeDtypeStruct((t_len, IN_COLS), BF16),
                 jax.ShapeDtypeStruct((D_MODEL, t_len), BF16), jax.ShapeDtypeStruct((D_MODEL, t_len), BF16),
                 jax.ShapeDtypeStruct((t_len, D_MODEL), BF16), jax.ShapeDtypeStruct(gwpool_shape, BF16),
                 jax.ShapeDtypeStruct((SUBLANES, CONV_W), F32), jax.ShapeDtypeStruct((SUBLANES, D_MODEL), F32)]
    out_specs = [rev(D_MODEL), rev(IN_COLS), rev_t, rev_t, rev(D_MODEL), _whole_spec(gwpool_shape),
                 _whole_spec((SUBLANES, CONV_W)), _whole_spec((SUBLANES, D_MODEL))]
    scratch = [pltpu.VMEM(gwpool_shape, F32), pltpu.VMEM((CONV_HALO, CONV_W), F32),
               pltpu.VMEM((POOL_HALO, POOL_W), F32)]
    aliases, split = _host(exchange, args, in_specs, out_shape, out_specs, scratch)

    def body(*refs):
        (dxo_ref, y_ref, proj_ref, projh_ref, x_ref, vec_ref, cps_ref, wpool_ref, wout_ref, win_ref,
         dx_ref, dproj_ref, ht_ref, ycatt_ref, dy_ref, gwpool_ref, dcps_ref, dvec_ref,
         gwpool_acc, dcc_ref, qc_ref), hosted = split(refs)
        i = pl.program_id(0)
        tile = n_tiles - 1 - i

        @pl.when(i == 0)
        def _():
            if hosted is not None:
                hosted[0]()
            gwpool_acc[...] = jnp.zeros_like(gwpool_acc)
            dcps_ref[...] = jnp.zeros_like(dcps_ref)
            dvec_ref[...] = jnp.zeros_like(dvec_ref)
            dcc_ref[...] = jnp.zeros_like(dcc_ref)
            qc_ref[...] = jnp.zeros_like(qc_ref)

        shift, scale, gate = vec_ref[0:1, :], vec_ref[1:2, :], vec_ref[2:3, :]
        g_pre, g_post = vec_ref[3:4, :], vec_ref[4:5, :]

        dxo_t = dxo_ref[...]
        y_t = y_ref[...].astype(F32)
        ry = lax.rsqrt(jnp.mean(y_t * y_t, axis=-1, keepdims=True) + NORM_EPS)
        yh = y_t * ry
        dvec_ref[2:3, :] += jnp.sum(dxo_t * yh, axis=0, keepdims=True)
        dyh = dxo_t * (gate * g_post)
        dy_b = (ry * (dyh - yh * jnp.mean(dyh * yh, axis=-1, keepdims=True))).astype(BF16)
        dy_ref[...] = dy_b

        halo = jnp.where(tile > 0, projh_ref[...].astype(F32), 0.0)
        hu_a, _, hc_a, _, hu_p, _ = _split_proj(halo)
        z_halo = (hc_a * hu_a)[POOL_HALO - CONV_HALO:]
        mx = _mixer_forward(proj_ref[...].astype(F32), z_halo, hu_p, cps_ref[...], wpool_ref, tile * BWD_TILE)
        ycatt_ref[...] = mx["ycat"].T.astype(BF16)

        dycat = _dot_nt(dy_b, wout_ref[...])
        dy_a, dy_p = dycat[:, :CONV_W], dycat[:, CONV_W:]

        t_a = dy_a * mx["silu_a"]
        db_a = t_a * mx["conv"]
        dconv = t_a * mx["b_a"]
        dg_a = dy_a * mx["b_a"] * mx["conv"] * (mx["sig_a"] * (1.0 + mx["g_a"] * (1.0 - mx["sig_a"])))
        dccat = jnp.concatenate([dconv, dcc_ref[...]], axis=0)
        dc1 = _rows_from_after(dccat, 1)[:BWD_TILE]
        dc2 = _rows_from_after(dccat, 2)[:BWD_TILE]
        dz = mx["w2"] * dconv + mx["w1"] * dc1 + mx["w0"] * dc2
        dcc_ref[...] = dconv[:CONV_HALO]
        dcps_ref[0:1, :] += jnp.sum(dconv * mx["z2"], axis=0, keepdims=True)
        dcps_ref[1:2, :] += jnp.sum(dconv * mx["z1"], axis=0, keepdims=True)
        dcps_ref[2:3, :] += jnp.sum(dconv * mx["z"], axis=0, keepdims=True)
        du_a = dz * mx["c_a"]
        dc_a = dz * mx["u_a"]

        t_p = dy_p * mx["silu_p"]
        dcps_ref[3:4, :] += jnp.sum(t_p * mx["mixed"], axis=0, keepdims=True)
        dmixed = (t_p * mx["ps"]).astype(BF16)
        dg_p = dy_p * mx["mixed"] * mx["ps"] * (mx["sig_p"] * (1.0 + mx["g_p"] * (1.0 - mx["sig_p"])))
        du_p, q_head = [], []
        for g, w in enumerate(POOL_WINDOWS):
            cols = slice(g * GROUP_D, (g + 1) * GROUP_D)
            dm_g = dmixed[:, cols]
            dpooled_g = _dot_nt(dm_g, wpool_ref[g])
            gwpool_acc[g] += _dot_tn(mx["pooled"][g].astype(BF16), dm_g)
            q_g = dpooled_g * mx["inv"][g]
            q_head.append(q_g[:POOL_HALO])
            s = jnp.concatenate([q_g, qc_ref[:, cols]], axis=0)
            step = 1
            while step < w:
                s = s + _rows_from_after(s, step)
                step *= 2
            du_p.append(s[:BWD_TILE] - dpooled_g)
        qc_ref[...] = jnp.concatenate(q_head, axis=1)
        dproj_b = jnp.concatenate([du_a, db_a, dc_a, dg_a] + du_p + [dg_p], axis=1).astype(BF16)
        dproj_ref[...] = dproj_b

        x_t = x_ref[...]
        rx = lax.rsqrt(jnp.mean(x_t * x_t, axis=-1, keepdims=True) + NORM_EPS)
        xn = x_t * rx
        mod_scale = 1.0 + scale
        ht_ref[...] = (xn * g_pre * mod_scale + shift).T.astype(BF16)
        dh = _dot_nt(dproj_b, win_ref[...])
        dvec_ref[0:1, :] += jnp.sum(dh, axis=0, keepdims=True)
        dvec_ref[1:2, :] += jnp.sum(dh * xn, axis=0, keepdims=True)
        dxn = dh * (g_pre * mod_scale)
        dx_ref[...] = dxo_t + rx * (dxn - xn * jnp.mean(dxn * xn, axis=-1, keepdims=True))

        @pl.when(i == n_tiles - 1)
        def _():
            gwpool_ref[...] = gwpool_acc[...].astype(BF16)
            sum_dh_xn, sum_dxo_yh = dvec_ref[1:2, :], dvec_ref[2:3, :]
            dvec_ref[1:2, :] = sum_dh_xn * g_pre
            dvec_ref[3:4, :] = sum_dh_xn * mod_scale
            dvec_ref[2:3, :] = sum_dxo_yh * g_post
            dvec_ref[4:5, :] = sum_dxo_yh * gate
            if hosted is not None:
                hosted[1]()

    return pl.pallas_call(
        body, name=f"backward_layer_{layer}", grid=(n_tiles,), in_specs=in_specs, out_specs=out_specs,
        out_shape=out_shape, scratch_shapes=scratch, input_output_aliases=aliases,
        compiler_params=_params(dimension_semantics=("arbitrary",)),
    )(*args)


def _weight_grads(layer, h_t, dproj, ycat_t, dy, exchange):
    t_len = dy.shape[0]
    n_in, n_out = IN_COLS // GWIN_COLS, D_MODEL // GWOUT_COLS
    args = [h_t, dproj, ycat_t, dy]
    in_specs = [_whole_spec(h_t.shape),
                pl.BlockSpec((t_len, GWIN_COLS), lambda s: (0, jnp.minimum(s, n_in - 1))),
                _whole_spec(ycat_t.shape),
                pl.BlockSpec((t_len, GWOUT_COLS), lambda s: (0, jnp.maximum(s - n_in, 0)))]
    out_shape = [jax.ShapeDtypeStruct((D_MODEL, IN_COLS), BF16), jax.ShapeDtypeStruct((D_MODEL, D_MODEL), BF16)]
    out_specs = [pl.BlockSpec((D_MODEL, GWIN_COLS), lambda s: (0, jnp.minimum(s, n_in - 1))),
                 pl.BlockSpec((D_MODEL, GWOUT_COLS), lambda s: (0, jnp.maximum(s - n_in, 0)))]
    scratch = []
    aliases, split = _host(exchange, args, in_specs, out_shape, out_specs, scratch)

    def body(*refs):
        (ht_ref, dproj_ref, ycatt_ref, dy_ref, gwin_ref, gwout_ref), hosted = split(refs)
        s = pl.program_id(0)
        if hosted is not None:
            pl.when(s == 0)(hosted[0])

        @pl.when(s < n_in)
        def _():
            gwin_ref[...] = _dot(ht_ref[...], dproj_ref[...]).astype(BF16)

        @pl.when(s >= n_in)
        def _():
            gwout_ref[...] = _dot(ycatt_ref[...], dy_ref[...]).astype(BF16)

        if hosted is not None:
            pl.when(s == n_in + n_out - 1)(hosted[1])

    return pl.pallas_call(
        body, name=f"weight_grads_{layer}", grid=(n_in + n_out,), in_specs=in_specs, out_specs=out_specs,
        out_shape=out_shape, scratch_shapes=scratch, input_output_aliases=aliases,
        compiler_params=_params(dimension_semantics=("arbitrary",)),
    )(*args)


def _add_sibling_blocks(name, layer, grads, received, core, partials, kinds):
    n_arr = len(grads)

    def body(core_ref, *refs):
        mine, theirs, outs = refs[:n_arr], refs[n_arr:2 * n_arr], refs[-n_arr:]
        for a in range(n_arr):
            outs[a][...] = (mine[a][...].astype(F32) + theirs[a][...].astype(F32)).astype(BF16)

    own_of_kind = [
        pl.BlockSpec((D_MODEL, W_IN_SHARD), lambda q, core_ref: (0, 2 * q + core_ref[0])),
        pl.BlockSpec((W_OUT_SHARD, D_MODEL), lambda q, core_ref: (2 * q + core_ref[0], 0)),
        pl.BlockSpec((POOL_SHARD, GROUP_D), lambda q, core_ref: (2 * q + core_ref[0], 0)),
    ]
    shapes = [_BLOCK_SHAPES[k] for k in kinds]
    recv_specs = [pl.BlockSpec((None,) + s, lambda q, core_ref: (q, 0, 0)) for s in shapes]
    out_specs = [pl.BlockSpec((None, None) + s, lambda q, core_ref: (q, layer, 0, 0)) for s in shapes]
    args = [core, *grads, *received]
    in_specs = [own_of_kind[k] for k in kinds] + recv_specs
    aliases = {}
    if partials is not None:
        aliases = {len(args) + a: a for a in range(n_arr)}
        args += list(partials)
        in_specs += [HBM] * n_arr
    return pl.pallas_call(
        body, name=name,
        grid_spec=pltpu.PrefetchScalarGridSpec(
            num_scalar_prefetch=1, grid=(N_CHIP,), in_specs=in_specs, out_specs=out_specs),
        out_shape=[jax.ShapeDtypeStruct((N_CHIP, DEPTH) + s, BF16) for s in shapes],
        input_output_aliases=aliases,
        compiler_params=_params(dimension_semantics=("arbitrary",)),
    )(*args)


def _modulation_columns(c_all, w_ada):
    def body(c_ref, w_ref, cact_ref, out_ref):
        c_t = c_ref[...]
        c_act = c_t * _sigmoid(c_t)
        cact_ref[...] = c_act
        out_ref[...] = jnp.dot(c_act, w_ref[...], preferred_element_type=F32, precision=lax.Precision.HIGHEST)

    return pl.pallas_call(
        body, name="modulation_columns", grid=(DEPTH,),
        in_specs=[pl.BlockSpec((N_DEV, D_MODEL), lambda l: (0, 0)),
                  pl.BlockSpec((None, D_MODEL, W_IN_SHARD), lambda l: (l, 0, 0))],
        out_specs=[pl.BlockSpec((N_DEV, D_MODEL), lambda l: (0, 0)),
                   pl.BlockSpec((N_DEV, W_IN_SHARD), lambda l: (0, l))],
        out_shape=[jax.ShapeDtypeStruct((N_DEV, D_MODEL), F32),
                   jax.ShapeDtypeStruct((N_DEV, DEPTH * W_IN_SHARD), F32)],
        compiler_params=_params(dimension_semantics=("arbitrary",)),
    )(c_all, w_ada)


def _adamw(w, g, m, v):
    m_new = ADAM_B1 * m + (1.0 - ADAM_B1) * g
    v_new = ADAM_B2 * v + (1.0 - ADAM_B2) * (g * g)
    m_hat = m_new / (1.0 - ADAM_B1 ** ADAM_STEP)
    v_hat = v_new / (1.0 - ADAM_B2 ** ADAM_STEP)
    delta = -ADAM_LR * (m_hat / (jnp.sqrt(v_hat) + ADAM_EPS) + ADAM_WD * w)
    return delta, m_new, v_new


def _adamw_w_ada(w, m, v, c_act_t, dmod_cols, exchange):
    big = pl.BlockSpec((None, D_MODEL, W_IN_SHARD), lambda l: (l, 0, 0))
    args = [w, m, v, c_act_t, dmod_cols]
    in_specs = [big, big, big, pl.BlockSpec((D_MODEL, N_DEV), lambda l: (0, 0)),
                pl.BlockSpec((None, N_DEV, W_IN_SHARD), lambda l: (l, 0, 0))]
    out_shape, out_specs, scratch = [jax.ShapeDtypeStruct(w.shape, F32)] * 4, [big] * 4, []
    aliases, split = _host(exchange, args, in_specs, out_shape, out_specs, scratch)

    def body(*refs):
        (w_ref, m_ref, v_ref, ct_ref, dm_ref, g_ref, d_ref, mo_ref, vo_ref), hosted = split(refs)
        if hosted is not None:
            pl.when(pl.program_id(0) == 0)(hosted[0])
        g = ct_ref[:, 0:1] * dm_ref[0:1, :]
        for b in range(1, N_DEV):
            g = g + ct_ref[:, b:b + 1] * dm_ref[b:b + 1, :]
        g_ref[...] = g
        d_ref[...], mo_ref[...], vo_ref[...] = _adamw(w_ref[...], g, m_ref[...], v_ref[...])
        if hosted is not None:
            pl.when(pl.program_id(0) == DEPTH - 1)(hosted[1])

    return pl.pallas_call(
        body, name="adamw_w_ada", grid=(DEPTH,), in_specs=in_specs, out_specs=out_specs, out_shape=out_shape,
        scratch_shapes=scratch, input_output_aliases=aliases,
        compiler_params=_params(dimension_semantics=("arbitrary",)),
    )(*args)


def _sum_chip_partials(own_ref, recv_ref):
    g = own_ref[...].astype(F32)
    for j in range(N_OTHER_CHIPS):
        g = g + recv_ref[j].astype(F32)
    return g


def _partial_specs(row_tile, cols, first_layer=0):
    own = pl.BlockSpec((None, None, row_tile, cols), lambda l, r, chip_ref: (chip_ref[0], first_layer + l, r, 0))
    recv = pl.BlockSpec((N_OTHER_CHIPS, None, row_tile, cols), lambda l, r, chip_ref: (0, first_layer + l, r, 0))
    return own, recv


def _adamw_reduced(name, w, m, v, partial, received, chip, row_tile, layers, continued):
    depth, rows, cols = w.shape
    first, stop = layers

    def body(chip_ref, w_ref, m_ref, v_ref, own_ref, recv_ref, *rest):
        g_ref, d_ref, mo_ref, vo_ref = rest[-4:]
        g = _sum_chip_partials(own_ref, recv_ref)
        g_ref[...] = g
        d_ref[...], mo_ref[...], vo_ref[...] = _adamw(w_ref[...], g, m_ref[...], v_ref[...])

    blk = pl.BlockSpec((None, row_tile, cols), lambda l, r, chip_ref: (first + l, r, 0))
    args = [chip, w, m, v, partial, received]
    in_specs = [blk, blk, blk, *_partial_specs(row_tile, cols, first)]
    aliases = {}
    if continued is not None:
        aliases = {len(args) + k: k for k in range(4)}
        args += list(continued)
        in_specs += [HBM] * 4
    return pl.pallas_call(
        body, name=name,
        grid_spec=pltpu.PrefetchScalarGridSpec(
            num_scalar_prefetch=1, grid=(stop - first, rows // row_tile), in_specs=in_specs, out_specs=[blk] * 4),
        out_shape=[jax.ShapeDtypeStruct(w.shape, F32)] * 4, input_output_aliases=aliases,
        compiler_params=_params(dimension_semantics=("arbitrary", "arbitrary")),
    )(*args)


def _reduce_w_pool(partial, received, chip):
    def body(chip_ref, own_ref, recv_ref, g_ref):
        g_ref[...] = _sum_chip_partials(own_ref, recv_ref)

    return pl.pallas_call(
        body, name="reduce_w_pool",
        grid_spec=pltpu.PrefetchScalarGridSpec(
            num_scalar_prefetch=1, grid=(DEPTH, 1), in_specs=list(_partial_specs(POOL_SHARD, GROUP_D)),
            out_specs=pl.BlockSpec((POOL_SHARD, GROUP_D), lambda l, r, chip_ref: (l, 0))),
        out_shape=jax.ShapeDtypeStruct((DEPTH * POOL_SHARD, GROUP_D), F32),
        compiler_params=_params(dimension_semantics=("arbitrary", "arbitrary")),
    )(chip, partial, received)


def _adamw_small(params):
    n = len(params)

    def body(*refs):
        ins, outs = refs[:4 * n], refs[4 * n:]
        for p in range(n):
            w_ref, g_ref, m_ref, v_ref = ins[4 * p:4 * p + 4]
            d_ref, mo_ref, vo_ref = outs[3 * p:3 * p + 3]
            d_ref[...], mo_ref[...], vo_ref[...] = _adamw(w_ref[...], g_ref[...], m_ref[...], v_ref[...])

    vmem = pl.BlockSpec(memory_space=pltpu.VMEM)
    flat = [a for group in params for a in group]
    out_shape = [jax.ShapeDtypeStruct(group[0].shape, F32) for group in params for _ in range(3)]
    outs = pl.pallas_call(
        body, name="adamw_small", in_specs=[vmem] * len(flat), out_specs=[vmem] * len(out_shape),
        out_shape=out_shape, compiler_params=_params(),
    )(*flat)
    return [tuple(outs[3 * p:3 * p + 3]) for p in range(n)]


def _sum_sources(slabs):
    def body(s_ref, o_ref):
        acc = s_ref[0]
        for b in range(1, N_DEV):
            acc = acc + s_ref[b]
        o_ref[...] = acc

    vmem = pl.BlockSpec(memory_space=pltpu.VMEM)
    return pl.pallas_call(
        body, name="sum_small_grads", in_specs=[vmem], out_specs=vmem,
        out_shape=jax.ShapeDtypeStruct(slabs.shape[1:], F32), compiler_params=_params(),
    )(slabs)


def _to_bf16(a, name, layers=None):
    first, stop = layers or (0, a.shape[0])

    def body(a_ref, o_ref):
        o_ref[...] = a_ref[...].astype(BF16)

    block = (None,) + a.shape[1:]
    return pl.pallas_call(
        body, name=name, grid=(stop - first,), in_specs=[pl.BlockSpec(block, lambda l: (first + l, 0, 0))],
        out_specs=pl.BlockSpec(block, lambda l: (l, 0, 0)),
        out_shape=jax.ShapeDtypeStruct((stop - first,) + a.shape[1:], BF16),
        compiler_params=_params(dimension_semantics=("arbitrary",)),
    )(a)


def kernel(x, c, w_ada, b_ada, g_pre, w_in, w_conv, w_pool, pool_scale, w_out, g_post, loss_target, m_w_ada, m_b_ada, m_g_pre, m_w_in, m_w_conv, m_w_pool, m_pool_scale, m_w_out, m_g_post, v_w_ada, v_b_ada, v_g_pre, v_w_in, v_w_conv, v_w_pool, v_pool_scale, v_w_out, v_g_post):
    mx, my, mc = _mesh_position()
    me = _block_id(mx, my, mc)
    chip = (2 * mx + my).astype(jnp.int32).reshape(1)
    core = mc.astype(jnp.int32).reshape(1)
    x0 = x[0]
    target = loss_target[0]
    conv_shard = w_conv.shape[-1]

    own_small = jnp.concatenate([c, w_conv.reshape(1, DEPTH * 3 * conv_shard)], axis=1)
    all_small = _all_gather_small(own_small, "all_gather_c_w_conv")[:, 0, :]
    c_all = all_small[:, :D_MODEL]
    w_conv_full = all_small[:, D_MODEL:].reshape(N_DEV, DEPTH, 3, conv_shard).transpose(1, 2, 0, 3).reshape(
        DEPTH, 3, CONV_W)
    cps = jnp.concatenate([w_conv_full, pool_scale[:, None], jnp.zeros((DEPTH, 4, CONV_W), F32)], axis=1)

    c_act, pieces = _modulation_columns(c_all, w_ada)
    mod_all = _all_gather_small(pieces, "all_gather_modulation")
    mod_mine = lax.dynamic_index_in_dim(mod_all, me, axis=1, keepdims=False)
    mod = mod_mine.reshape(N_DEV, DEPTH, W_IN_SHARD).transpose(1, 0, 2).reshape(DEPTH, 3 * D_MODEL) + b_ada
    zeros_d = jnp.zeros((DEPTH, 3, D_MODEL), F32)
    vec = jnp.concatenate([mod.reshape(DEPTH, 3, D_MODEL), g_pre[:, None], g_post[:, None], zeros_d], axis=1)

    gathers = []
    after = [mod_all, all_small]
    for group in ((0, 1), (1, DEPTH)):
        first_sems, shards, landing = _gather_weights_start(
            group[0], _to_bf16(w_in, f"cast_w_in_{group[0]}", group), _to_bf16(w_out, f"cast_w_out_{group[0]}", group),
            after)
        gathers += [(first_sems, shards, landing[k], k) for k in range(group[1] - group[0])]
        after = [shards[0]]
    wpool_b = _to_bf16(w_pool.reshape(DEPTH, POOL_ROWS, GROUP_D), "cast_w_pool").reshape(w_pool.shape)

    xs, projs, ys, wins, wouts = [x0], [], [], [], []
    for l in range(DEPTH):
        first_sems, shards, zones, index = gathers[l]
        passed_sems, zones = _gather_weights_pass_on(
            l, index, first_sems[1], zones, [vec, cps, wpool_b, gathers[-1][1][0]] if l == 0 else [xs[-1]])
        win, wout = _gather_weights_finish(l, index, first_sems, passed_sems, shards, zones)
        x_next, proj, y = _forward_layer(l, xs[-1], vec, cps, wpool_b, win, wout, None)
        xs.append(x_next)
        projs.append(proj)
        ys.append(y)
        wins.append(win)
        wouts.append(wout)
    dx, loss_tile = _loss_head(xs[DEPTH], target)

    slab_rows = [None] * DEPTH
    partials = received = None
    in_flight = []

    def scatter(layer, grads, from_sibling, after):
        nonlocal partials, received
        partials = _add_sibling_blocks(
            f"grad_add_sibling_{layer}", layer, grads, from_sibling, core, partials, ALL_KINDS)
        chips = _chips_exchange(layer, partials, received, ALL_KINDS)
        sems, partials, received, token = _start_exchange(chips, f"grad_chips_start_{layer}", after)
        in_flight.append((chips, sems, layer))
        return token

    grads_above = None
    for l in reversed(range(DEPTH)):
        dx, dproj, h_t, ycat_t, dy, gwpool, dcps, dvec = _backward_layer(
            l, dx, ys[l], projs[l], xs[l], vec, cps, wpool_b, wouts[l], wins[l], None)
        slab_rows[l] = jnp.concatenate(
            [dvec[0], dvec[1], dvec[2], dvec[3], dvec[4], dcps[3], dcps[0], dcps[1], dcps[2],
             loss_tile[0] if l == 0 else jnp.zeros((LANES,), F32)])
        if l == 0:
            slabs = _all_gather_small(jnp.stack(slab_rows), "all_gather_small_grads")
        hosted = _sibling_exchange(grads_above, ALL_KINDS) if grads_above is not None else None
        gwin, gwout, *from_sibling = _weight_grads(l, h_t, dproj, ycat_t, dy, hosted)
        if grads_above is not None:
            scatter(l + 1, grads_above, from_sibling, [])
        grads_above = [gwin, gwout, gwpool.reshape(POOL_ROWS, GROUP_D)]
        if l <= 1:
            from_sibling = _run_exchange(_sibling_exchange(grads_above, ALL_KINDS), f"grad_exchange_sibling_{l}")
            token = scatter(l, grads_above, from_sibling, [slabs] if l == 0 else [])
            grads_above = None
    grad_x = dx[None]
    chips_0, sems_0, _ = in_flight.pop()

    total = _sum_sources(slabs)
    loss = total[0, SLAB_COLS]
    o = 3 * D_MODEL
    g_b_ada = total[:, :o]
    g_g_pre = total[:, o:o + D_MODEL]
    g_g_post = total[:, o + D_MODEL:o + 2 * D_MODEL]
    g_pool_scale = total[:, o + 2 * D_MODEL:o + 2 * D_MODEL + POOL_W]
    g_conv_full = total[:, o + 2 * D_MODEL + POOL_W:SLAB_COLS].reshape(DEPTH, 3, CONV_W)
    g_w_conv = lax.dynamic_slice_in_dim(g_conv_full, me * conv_shard, conv_shard, axis=2)

    after = [token]
    for chips, sems, l in in_flight:
        partials, received = _finish_exchange(chips, f"grad_chips_finish_{l}", sems, partials, received, after)
        after = []
    upper = (1, DEPTH)
    w_in_upper = _adamw_reduced(
        "adamw_w_in_upper", w_in, m_w_in, v_w_in, partials[0], received[0], chip, ROW_TILE, upper, None)
    w_out_upper = _adamw_reduced(
        "adamw_w_out_upper", w_out, m_w_out, v_w_out, partials[1], received[1], chip, W_OUT_SHARD, upper, None)
    dmod_all = slabs[:, :, :o].reshape(N_DEV, DEPTH, N_DEV, W_IN_SHARD)
    dmod_cols = lax.dynamic_index_in_dim(dmod_all, me, axis=2, keepdims=False).transpose(1, 0, 2) + token[0, 0]
    g_w_ada, d_w_ada, nm_w_ada, nv_w_ada = _adamw_w_ada(w_ada, m_w_ada, v_w_ada, c_act.T, dmod_cols, None)

    partials, received = _finish_exchange(
        chips_0, "grad_chips_finish_0", sems_0, partials, received, [nv_w_ada, w_in_upper[3], w_out_upper[3]])
    gather_pool = _all_gather_exchange(_reduce_w_pool(partials[2], received[2], chip))
    sems_p, pool_rows, pool_landing, token_p = _start_exchange(gather_pool, "all_gather_grad_w_pool_start")
    g_w_in, d_w_in, nm_w_in, nv_w_in = _adamw_reduced(
        "adamw_w_in_0", w_in, m_w_in, v_w_in, partials[0], received[0], chip, ROW_TILE, (0, 1), w_in_upper)
    g_w_out, d_w_out, nm_w_out, nv_w_out = _adamw_reduced(
        "adamw_w_out_0", w_out, m_w_out, v_w_out, partials[1], received[1], chip, W_OUT_SHARD, (0, 1), w_out_upper)
    _, (g_pool_all,) = _finish_exchange(
        gather_pool, "all_gather_grad_w_pool_finish", sems_p, pool_rows, pool_landing, [nv_w_in, nv_w_out])
    g_w_pool = g_pool_all.reshape(N_DEV, DEPTH, POOL_SHARD, GROUP_D).transpose(1, 0, 2, 3).reshape(w_pool.shape)

    flat2 = lambda a: a.reshape(-1, a.shape[-1])
    small = _adamw_small([
        (b_ada, g_b_ada, m_b_ada, v_b_ada),
        (g_pre, g_g_pre, m_g_pre, v_g_pre),
        (flat2(w_conv), flat2(g_w_conv), flat2(m_w_conv), flat2(v_w_conv)),
        (flat2(w_pool), flat2(g_w_pool), flat2(m_w_pool), flat2(v_w_pool)),
        (pool_scale, g_pool_scale, m_pool_scale, v_pool_scale),
        (g_post, g_g_post, m_g_post, v_g_post),
    ])
    (d_b_ada, nm_b_ada, nv_b_ada), (d_g_pre, nm_g_pre, nv_g_pre), conv_upd, pool_upd, \
        (d_ps, nm_ps, nv_ps), (d_g_post, nm_g_post, nv_g_post) = small
    d_w_conv, nm_w_conv, nv_w_conv = (a.reshape(w_conv.shape) for a in conv_upd)
    d_w_pool, nm_w_pool, nv_w_pool = (a.reshape(w_pool.shape) for a in pool_upd)

    return (loss, grad_x,
            g_w_ada, g_b_ada, g_g_pre, g_w_in, g_w_conv, g_w_pool, g_pool_scale, g_w_out, g_g_post,
            d_w_ada, d_b_ada, d_g_pre, d_w_in, d_w_conv, d_w_pool, d_ps, d_w_out, d_g_post,
            nm_w_ada, nm_b_ada, nm_g_pre, nm_w_in, nm_w_conv, nm_w_pool, nm_ps, nm_w_out, nm_g_post,
            nv_w_ada, nv_b_ada, nv_g_pre, nv_w_in, nv_w_conv, nv_w_pool, nv_ps, nv_w_out, nv_g_post)
```

```python
import jax
import jax.numpy as jnp
from jax import lax
from jax.experimental import pallas as pl
from jax.experimental.pallas import tpu as pltpu

F32 = jnp.float32
BF16 = jnp.bfloat16

D_MODEL = 1024
DEPTH = 4
CONV_W = 512
POOL_W = 512
POOL_WINDOWS = (2, 4, 8, 16)
GROUP_D = 128
IN_COLS = 4 * CONV_W + 2 * POOL_W
NORM_EPS = 1e-6

ADAM_LR = 0.001
ADAM_B1 = 0.9
ADAM_B2 = 0.999
ADAM_EPS = 1e-08
ADAM_WD = 0.01
ADAM_STEP = 10

N_DEV = 8
N_CHIP = 4
N_OTHER_CHIPS = N_CHIP - 1
MESH = pl.DeviceIdType.MESH
W_IN_SHARD = IN_COLS // N_DEV
W_OUT_SHARD = D_MODEL // N_DEV
POOL_ROWS = len(POOL_WINDOWS) * GROUP_D
POOL_SHARD = POOL_ROWS // N_DEV

SUBLANES = 8
LANES = 128
VMEM_LIMIT_BYTES = 56 * 1024 * 1024
ROW_TILE = 512
BWD_TILE = 256
GWIN_COLS = 768
GWOUT_COLS = 512
POOL_HALO = 16
CONV_HALO = SUBLANES

SLAB_COLS = 3 * D_MODEL + D_MODEL + D_MODEL + POOL_W + 3 * CONV_W

HBM = pl.BlockSpec(memory_space=pl.ANY)


def _params(**kw):
    return pltpu.CompilerParams(vmem_limit_bytes=VMEM_LIMIT_BYTES, **kw)


def _sigmoid(v):
    return 1.0 / (1.0 + jnp.exp(-v))


def _dot(a, b):
    return jnp.dot(a, b, preferred_element_type=F32)


def _dot_tn(a, b):
    return lax.dot_general(a, b, (((0,), (0,)), ((), ())), preferred_element_type=F32)


def _dot_nt(a, b):
    return lax.dot_general(a, b, (((1,), (1,)), ((), ())), preferred_element_type=F32)


def _rows_from_before(v, k):
    return pltpu.roll(v, k, 0)


def _rows_from_after(v, k):
    return pltpu.roll(v, v.shape[0] - k, 0)


def _window_counts(t0, rows):
    return (lax.broadcasted_iota(jnp.int32, (rows, 1), 0) + (t0 + 1)).astype(F32)


def _split_proj(p32):
    cw = CONV_W
    return (p32[:, 0 * cw:1 * cw], p32[:, 1 * cw:2 * cw], p32[:, 2 * cw:3 * cw], p32[:, 3 * cw:4 * cw],
            p32[:, 4 * cw:4 * cw + POOL_W], p32[:, 4 * cw + POOL_W:])


def _layer_spec(shape, layer):
    nd = len(shape)
    return pl.BlockSpec((None,) + tuple(shape[1:]), lambda i, _l=layer, _n=nd: (_l,) + (0,) * (_n - 1))


def _whole_spec(shape):
    return pl.BlockSpec(tuple(shape), lambda i, _n=len(shape): (0,) * _n)


def _mesh_position():
    return lax.axis_index("x"), lax.axis_index("y"), lax.axis_index("c")


def _block_id(x, y, c):
    return 4 * x + 2 * y + c


def _other_chips(x, y):
    return [(x ^ 1, y), (x, y ^ 1), (x ^ 1, y ^ 1)]


def _col_block(ref, blk):
    return ref.at[:, pl.ds(pl.multiple_of(blk * W_IN_SHARD, LANES), W_IN_SHARD)]


def _row_block(rows):
    def block(ref, blk):
        return ref.at[pl.ds(pl.multiple_of(blk * rows, rows), rows), :]
    return block


_BLOCK_OF = (_col_block, _row_block(W_OUT_SHARD), _row_block(POOL_SHARD))
_BLOCK_SHAPES = ((D_MODEL, W_IN_SHARD), (W_OUT_SHARD, D_MODEL), (POOL_SHARD, GROUP_D))


class _Exchange:
    def __init__(self, inputs, out_shapes, aliases, sem_shapes, make):
        self.inputs, self.out_shapes, self.aliases, self.sem_shapes, self.make = (
            list(inputs), list(out_shapes), dict(aliases), list(sem_shapes), make)


def _run_exchange(exchange, name):
    n_in, n_out = len(exchange.inputs), len(exchange.out_shapes)

    def body(*refs):
        start, finish = exchange.make(refs[:n_in], refs[n_in:n_in + n_out], refs[n_in + n_out:])
        start()
        finish()

    return pl.pallas_call(
        body, name=name, in_specs=[HBM] * n_in, out_specs=[HBM] * n_out, out_shape=exchange.out_shapes,
        scratch_shapes=exchange.sem_shapes, input_output_aliases=exchange.aliases, compiler_params=_params(),
    )(*exchange.inputs)


_SEM = pl.BlockSpec(memory_space=pltpu.SEMAPHORE)
_DATAFLOW = pltpu.SideEffectType.DATAFLOW_SIDE_EFFECTING


def _start_exchange(exchange, name, after=()):
    n_in, n_out, n_sem = len(exchange.inputs), len(exchange.out_shapes), len(exchange.sem_shapes)
    sources = [i for i in range(n_in) if i not in exchange.aliases]
    aliases = {i: n_sem + k for k, i in enumerate(sources)}
    aliases.update({i: n_sem + len(sources) + o for i, o in exchange.aliases.items()})

    def body(*refs):
        in_refs = refs[:n_in]
        outs = refs[n_in + len(after):]
        sems = outs[:n_sem]
        out_refs = outs[n_sem + len(sources):n_sem + len(sources) + n_out]
        exchange.make(in_refs, out_refs, sems)[0]()
        refs[-1][...] = jnp.zeros_like(refs[-1])

    outs = pl.pallas_call(
        body, name=name, in_specs=[HBM] * (n_in + len(after)),
        out_specs=[_SEM] * n_sem + [HBM] * (len(sources) + n_out) + [pl.BlockSpec(memory_space=pltpu.VMEM)],
        out_shape=(exchange.sem_shapes + [pltpu.HBM(exchange.inputs[i].shape, exchange.inputs[i].dtype) for i in sources]
                   + [pltpu.HBM(s.shape, s.dtype) for s in exchange.out_shapes]
                   + [jax.ShapeDtypeStruct((SUBLANES, LANES), F32)]),
        input_output_aliases=aliases, compiler_params=_params(has_side_effects=_DATAFLOW),
    )(*exchange.inputs, *after)
    return outs[:n_sem], outs[n_sem:n_sem + len(sources)], outs[n_sem + len(sources):-1], outs[-1]


def _finish_exchange(exchange, name, sems, sources, landing, after):
    n_src, n_out, n_sem = len(sources), len(landing), len(sems)
    n_in = len(exchange.inputs)
    source_at = [i for i in range(n_in) if i not in exchange.aliases]

    def body(*refs):
        src_refs, out_refs = refs[:n_src], refs[n_src:n_src + n_out]
        sem_refs = refs[n_src + n_out:n_src + n_out + n_sem]
        in_refs = [None] * n_in
        for k, i in enumerate(source_at):
            in_refs[i] = src_refs[k]
        for i, o in exchange.aliases.items():
            in_refs[i] = out_refs[o]
        exchange.make(in_refs, out_refs, sem_refs)[1]()

    arrays = list(sources) + list(landing)
    outs = pl.pallas_call(
        body, name=name, in_specs=[HBM] * len(arrays) + [_SEM] * n_sem + [HBM] * len(after),
        out_specs=[HBM] * len(arrays), out_shape=[pltpu.HBM(a.shape, a.dtype) for a in arrays],
        input_output_aliases={i: i for i in range(len(arrays))}, compiler_params=_params(has_side_effects=_DATAFLOW),
    )(*arrays, *sems, *after)
    return outs[:n_src], outs[n_src:]


N_GATHERED = 2
FIRST_COPIES = 1 + N_OTHER_CHIPS
_GATHERED_SHAPES = ((D_MODEL, IN_COLS), (D_MODEL, D_MODEL))


def _first_sem(layer, a, k):
    return (layer * N_GATHERED + a) * FIRST_COPIES + k


def _gather_copy(window_of, full_ref, blk, send_sem, recv_sem, to, src=None):
    window = window_of(full_ref, blk)
    return pltpu.make_async_remote_copy(
        src_ref=window if src is None else src, dst_ref=window, send_sem=send_sem, recv_sem=recv_sem,
        device_id=to, device_id_type=MESH)


def _first_copies(layer, shard_refs, full_refs, send_sems, recv_sems, local_sems):
    x, y, c = _mesh_position()
    me = _block_id(x, y, c)
    own, remote = [], []
    for a in range(N_GATHERED):
        shard = shard_refs[a].at[layer]
        own.append(pltpu.make_async_copy(
            shard, _BLOCK_OF[a](full_refs[a], me), local_sems.at[layer * N_GATHERED + a]))
        targets = [(x, y, 1 - c)] + [(*chip, c) for chip in _other_chips(x, y)]
        remote += [_gather_copy(_BLOCK_OF[a], full_refs[a], me, send_sems.at[_first_sem(layer, a, k)],
                                recv_sems.at[_first_sem(layer, a, k)], to, src=shard)
                   for k, to in enumerate(targets)]
    return own, remote


def _gather_weights_start(first_layer, win_shards, wout_shards, after):
    n_layers = win_shards.shape[0]
    n_first = n_layers * N_GATHERED * FIRST_COPIES
    sem_shapes = [pltpu.SemaphoreType.DMA((n_first,)), pltpu.SemaphoreType.DMA((n_first,)),
                  pltpu.SemaphoreType.DMA((n_layers * N_GATHERED,))]
    shards = [win_shards, wout_shards]

    def body(win_sh, wout_sh, *rest):
        send_sems, recv_sems, local_sems, win_thru, wout_thru, *landing = rest[len(after):]
        for layer in range(n_layers):
            own, remote = _first_copies(layer, (win_sh, wout_sh), landing[N_GATHERED * layer:N_GATHERED * (layer + 1)],
                                        send_sems, recv_sems, local_sems)
            for cp in own + remote:
                cp.start()

    outs = pl.pallas_call(
        body, name=f"all_gather_weights_start_{first_layer}", in_specs=[HBM] * (2 + len(after)),
        out_specs=[_SEM] * 3 + [HBM] * (2 + n_layers * N_GATHERED),
        out_shape=(sem_shapes + [pltpu.HBM(s.shape, s.dtype) for s in shards]
                   + [pltpu.HBM(s, BF16) for _ in range(n_layers) for s in _GATHERED_SHAPES]),
        input_output_aliases={0: 3, 1: 4}, compiler_params=_params(has_side_effects=_DATAFLOW),
    )(*shards, *after)
    landing = outs[5:]
    return outs[:3], outs[3:5], [landing[N_GATHERED * l:N_GATHERED * (l + 1)] for l in range(n_layers)]


def _passed_on_copies(full_refs, send_sems, recv_sems, core_of_block):
    x, y, c = _mesh_position()
    return [_gather_copy(_BLOCK_OF[a], full_refs[a], _block_id(*chip, core_of_block),
                         send_sems.at[a * N_OTHER_CHIPS + j], recv_sems.at[a * N_OTHER_CHIPS + j], (x, y, 1 - c))
            for a in range(N_GATHERED) for j, chip in enumerate(_other_chips(x, y))]


def _gather_weights_pass_on(layer, index, first_recv_sems, landing, after):
    n = N_GATHERED * N_OTHER_CHIPS

    def body(win_ref, wout_ref, first_recv, *rest):
        send_sems, recv_sems = rest[len(after):len(after) + 2]
        x, y, c = _mesh_position()
        full_refs = (win_ref, wout_ref)
        passed = _passed_on_copies(full_refs, send_sems, recv_sems, c)
        for a in range(N_GATHERED):
            for j, chip in enumerate(_other_chips(x, y)):
                sem = _first_sem(index, a, 1 + j)
                _gather_copy(_BLOCK_OF[a], full_refs[a], _block_id(*chip, c), first_recv.at[sem], first_recv.at[sem],
                             (x, y, c)).wait_recv()
                passed[a * N_OTHER_CHIPS + j].start()

    outs = pl.pallas_call(
        body, name=f"all_gather_weights_pass_on_{layer}", in_specs=[HBM] * N_GATHERED + [_SEM] + [HBM] * len(after),
        out_specs=[_SEM] * 2 + [HBM] * N_GATHERED,
        out_shape=[pltpu.SemaphoreType.DMA((n,)), pltpu.SemaphoreType.DMA((n,))]
        + [pltpu.HBM(a.shape, a.dtype) for a in landing],
        input_output_aliases={a: 2 + a for a in range(N_GATHERED)},
        compiler_params=_params(has_side_effects=_DATAFLOW),
    )(*landing, first_recv_sems, *after)
    return outs[:2], outs[2:]


def _gather_weights_finish(layer, index, first_sems, passed_sems, shards, landing):
    def body(win_ref, wout_ref, first_send, first_recv, local_sems, passed_send, passed_recv, win_sh, wout_sh, *thru):
        x, y, c = _mesh_position()
        full_refs = (win_ref, wout_ref)
        own, remote = _first_copies(index, (win_sh, wout_sh), full_refs, first_send, first_recv, local_sems)
        for a in range(N_GATHERED):
            sem = _first_sem(index, a, 0)
            _gather_copy(_BLOCK_OF[a], full_refs[a], _block_id(x, y, 1 - c), first_recv.at[sem], first_recv.at[sem],
                         (x, y, c)).wait_recv()
        for cp in _passed_on_copies(full_refs, passed_send, passed_recv, 1 - c):
            cp.wait_recv()
        for cp in remote + _passed_on_copies(full_refs, passed_send, passed_recv, c):
            cp.wait_send()
        for cp in own:
            cp.wait()

    return pl.pallas_call(
        body, name=f"all_gather_weights_finish_{layer}", in_specs=[HBM] * N_GATHERED + [_SEM] * 5 + [HBM] * 2,
        out_specs=[HBM] * N_GATHERED, out_shape=[pltpu.HBM(a.shape, a.dtype) for a in landing],
        input_output_aliases={a: a for a in range(N_GATHERED)}, compiler_params=_params(has_side_effects=_DATAFLOW),
    )(*landing, *first_sems, *passed_sems, *shards)


ALL_KINDS = (0, 1, 2)


def _sibling_exchange(grads, kinds):
    n_arr = len(grads)

    def make(in_refs, out_refs, sems):
        send_sems, recv_sems = sems
        x, y, c = _mesh_position()
        copies = [pltpu.make_async_remote_copy(
            src_ref=_BLOCK_OF[kinds[a]](in_refs[a], 2 * q + (1 - c)), dst_ref=out_refs[a].at[q],
            send_sem=send_sems.at[a, q], recv_sem=recv_sems.at[a, q], device_id=(x, y, 1 - c), device_id_type=MESH)
            for a in range(n_arr) for q in range(N_CHIP)]

        def start():
            for cp in copies:
                cp.start()

        def finish():
            for cp in copies:
                cp.wait_recv()
            for cp in copies:
                cp.wait_send()

        return start, finish

    return _Exchange(
        grads, [jax.ShapeDtypeStruct((N_CHIP,) + _BLOCK_SHAPES[k], BF16) for k in kinds], {},
        [pltpu.SemaphoreType.DMA((n_arr, N_CHIP)), pltpu.SemaphoreType.DMA((n_arr, N_CHIP))], make)


def _chips_exchange(layer, partials, received, kinds):
    n_arr = len(partials)

    def make(in_refs, out_refs, sems):
        send_sems, recv_sems = sems
        x, y, c = _mesh_position()
        copies = [pltpu.make_async_remote_copy(
            src_ref=in_refs[a].at[2 * qx + qy, layer], dst_ref=out_refs[a].at[j, layer],
            send_sem=send_sems.at[a * N_OTHER_CHIPS + j], recv_sem=recv_sems.at[a * N_OTHER_CHIPS + j],
            device_id=(qx, qy, c), device_id_type=MESH)
            for a in range(n_arr) for j, (qx, qy) in enumerate(_other_chips(x, y))]

        def start():
            for cp in copies:
                cp.start()

        def finish():
            for cp in copies:
                cp.wait_recv()
            for cp in copies:
                cp.wait_send()

        return start, finish

    inputs = list(partials)
    aliases = {}
    if received is not None:
        inputs += list(received)
        aliases = {n_arr + a: a for a in range(n_arr)}
    return _Exchange(
        inputs, [jax.ShapeDtypeStruct((N_OTHER_CHIPS, DEPTH) + _BLOCK_SHAPES[k], BF16) for k in kinds], aliases,
        [pltpu.SemaphoreType.DMA((n_arr * N_OTHER_CHIPS,)), pltpu.SemaphoreType.DMA((n_arr * N_OTHER_CHIPS,))], make)


def _all_gather_exchange(v):
    def make(in_refs, out_refs, sems):
        send_sems, recv_sems, local_sem = sems
        x, y, c = _mesh_position()
        me = _block_id(x, y, c)
        own = pltpu.make_async_copy(in_refs[0], out_refs[0].at[me], local_sem.at[0])
        sends, arrivals = [], []
        for k in range(1, N_DEV):
            px, py, pc = x ^ ((k >> 2) & 1), y ^ ((k >> 1) & 1), c ^ (k & 1)
            sends.append(pltpu.make_async_remote_copy(
                src_ref=in_refs[0], dst_ref=out_refs[0].at[me], send_sem=send_sems.at[k - 1],
                recv_sem=recv_sems.at[k - 1], device_id=(px, py, pc), device_id_type=MESH))
            arrivals.append(pltpu.make_async_remote_copy(
                src_ref=in_refs[0], dst_ref=out_refs[0].at[_block_id(px, py, pc)], send_sem=send_sems.at[k - 1],
                recv_sem=recv_sems.at[k - 1], device_id=(x, y, c), device_id_type=MESH))

        def start():
            for cp in [own] + sends:
                cp.start()

        def finish():
            for cp in arrivals:
                cp.wait_recv()
            for cp in sends:
                cp.wait_send()
            own.wait()

        return start, finish

    return _Exchange(
        [v], [jax.ShapeDtypeStruct((N_DEV,) + v.shape, v.dtype)], {},
        [pltpu.SemaphoreType.DMA((N_DEV - 1,)), pltpu.SemaphoreType.DMA((N_DEV - 1,)),
         pltpu.SemaphoreType.DMA((1,))], make)


def _host(exchange, args, in_specs, out_shape, out_specs, scratch):
    n_own = (len(args), len(out_shape), len(scratch))
    if exchange is None:
        return {}, lambda refs: (refs, None)
    n_ex = (len(exchange.inputs), len(exchange.out_shapes), len(exchange.sem_shapes))
    aliases = {n_own[0] + i: n_own[1] + o for i, o in exchange.aliases.items()}
    args += exchange.inputs
    in_specs += [HBM] * n_ex[0]
    out_shape += exchange.out_shapes
    out_specs += [HBM] * n_ex[1]
    scratch += exchange.sem_shapes

    def split(refs):
        own, theirs, at = [], [], 0
        for mine, ex in zip(n_own, n_ex):
            own += refs[at:at + mine]
            theirs.append(refs[at + mine:at + mine + ex])
            at += mine + ex
        return own, exchange.make(*theirs)

    return aliases, split


def _all_gather_small(v, name, exchange=None):
    vmem = pl.BlockSpec(memory_space=pltpu.VMEM)
    args, in_specs = [v], [vmem]
    out_shape, out_specs = [jax.ShapeDtypeStruct((N_DEV,) + v.shape, v.dtype)], [vmem]
    scratch = [pltpu.SemaphoreType.DMA((N_DEV - 1,)), pltpu.SemaphoreType.DMA((N_DEV - 1,))]
    aliases, split = _host(exchange, args, in_specs, out_shape, out_specs, scratch)

    def body(*refs):
        (v_ref, out_ref, send_sems, recv_sems), hosted = split(refs)
        if hosted is not None:
            hosted[0]()
        x, y, c = _mesh_position()
        me = _block_id(x, y, c)
        out_ref[me] = v_ref[...]
        sends = []
        for k in range(1, N_DEV):
            px, py, pc = x ^ ((k >> 2) & 1), y ^ ((k >> 1) & 1), c ^ (k & 1)
            send = pltpu.make_async_remote_copy(
                src_ref=v_ref, dst_ref=out_ref.at[me], send_sem=send_sems.at[k - 1], recv_sem=recv_sems.at[k - 1],
                device_id=(px, py, pc), device_id_type=MESH)
            send.start()
            sends.append((send, _block_id(px, py, pc)))
        for k, (send, peer) in enumerate(sends):
            pltpu.make_async_remote_copy(
                src_ref=v_ref, dst_ref=out_ref.at[peer], send_sem=send_sems.at[k], recv_sem=recv_sems.at[k],
                device_id=(x, y, c), device_id_type=MESH).wait_recv()
        for send, _ in sends:
            send.wait_send()
        if hosted is not None:
            hosted[1]()

    outs = pl.pallas_call(
        body, name=name, in_specs=in_specs, out_specs=out_specs, out_shape=out_shape, scratch_shapes=scratch,
        input_output_aliases=aliases, compiler_params=_params(),
    )(*args)
    return outs[0] if exchange is None else outs


def _forward_layer(layer, x, vec, cps, wpool, win, wout):
    t_len = x.shape[0]
    n_tiles = t_len // ROW_TILE
    row = lambda cols: pl.BlockSpec((ROW_TILE, cols), lambda i: (i, 0))
    col_t = pl.BlockSpec((D_MODEL, ROW_TILE), lambda i: (0, i))
    widths = (D_MODEL, 2 * CONV_W, 3 * CONV_W, 4 * POOL_W)

    def body(x_ref, vec_ref, cps_ref, wpool_ref, win_ref, wout_ref,
             xo_ref, y_ref, ht_ref, ycatt_ref, uc_ref, fa_ref, fp_ref, zc_ref, pc_ref):
        i = pl.program_id(0)

        @pl.when(i == 0)
        def _():
            zc_ref[...] = jnp.zeros_like(zc_ref)
            pc_ref[...] = jnp.zeros_like(pc_ref)

        x_t = x_ref[...]
        shift, scale, gate = vec_ref[0:1, :], vec_ref[1:2, :], vec_ref[2:3, :]
        g_pre, g_post = vec_ref[3:4, :], vec_ref[4:5, :]
        w0, w1, w2, ps = cps_ref[0:1, :], cps_ref[1:2, :], cps_ref[2:3, :], cps_ref[3:4, :]
        rx = lax.rsqrt(jnp.mean(x_t * x_t, axis=-1, keepdims=True) + NORM_EPS)
        h = (x_t * rx) * g_pre * (1.0 + scale) + shift
        ht_ref[...] = h.T.astype(BF16)
        proj = _dot(h.astype(BF16), win_ref[...]).astype(BF16).astype(F32)
        u_a, b_a, c_a, g_a, u_p, g_p = _split_proj(proj)
        uc_ref[...] = jnp.concatenate([u_a, c_a], axis=1).astype(BF16)

        z = c_a * u_a
        zcat = jnp.concatenate([zc_ref[...], z], axis=0)
        zc_ref[...] = z[ROW_TILE - CONV_HALO:]
        conv = (w0 * _rows_from_before(zcat, 2)[CONV_HALO:] + w1 * _rows_from_before(zcat, 1)[CONV_HALO:] + w2 * z)
        sig_a = _sigmoid(g_a)
        silu_a = g_a * sig_a
        b_conv = b_a * conv
        y_a = b_conv * silu_a
        fa_ref[...] = jnp.concatenate(
            [silu_a * conv, silu_a * b_a, b_conv * (sig_a * (1.0 + g_a * (1.0 - sig_a)))], axis=1).astype(BF16)

        pcat = jnp.concatenate([pc_ref[...], u_p], axis=0)
        pc_ref[...] = u_p[ROW_TILE - POOL_HALO:]
        counts = _window_counts(i * ROW_TILE, ROW_TILE)
        pooled, mixed = [], []
        for g, w in enumerate(POOL_WINDOWS):
            cols = slice(g * GROUP_D, (g + 1) * GROUP_D)
            s = pcat[:, cols]
            step = 1
            while step < w:
                s = s + _rows_from_before(s, step)
                step *= 2
            pooled_g = (s[POOL_HALO:] * (1.0 / jnp.minimum(counts, float(w))) - u_p[:, cols]).astype(BF16)
            pooled.append(pooled_g)
            mixed.append(_dot(pooled_g, wpool_ref[g]))
        mixed = jnp.concatenate(mixed, axis=1)
        sig_p = _sigmoid(g_p)
        silu_p = g_p * sig_p
        mixed_ps = mixed * ps
        y_p = mixed_ps * silu_p
        fp_ref[...] = jnp.concatenate(
            [(ps * silu_p).astype(BF16), (mixed_ps * (sig_p * (1.0 + g_p * (1.0 - sig_p)))).astype(BF16),
             (silu_p * mixed).astype(BF16)] + pooled, axis=1)

        ycat = jnp.concatenate([y_a, y_p], axis=1)
        ycatt_ref[...] = ycat.T.astype(BF16)
        y_b = _dot(ycat.astype(BF16), wout_ref[...]).astype(BF16)
        y_ref[...] = y_b
        y_t = y_b.astype(F32)
        ry = lax.rsqrt(jnp.mean(y_t * y_t, axis=-1, keepdims=True) + NORM_EPS)
        xo_ref[...] = x_t + gate * (y_t * ry * g_post)

    return pl.pallas_call(
        body, name=f"forward_layer_{layer}", grid=(n_tiles,),
        in_specs=[row(D_MODEL), _layer_spec(vec.shape, layer), _layer_spec(cps.shape, layer),
                  _layer_spec(wpool.shape, layer), _whole_spec(win.shape), _whole_spec(wout.shape)],
        out_specs=[row(D_MODEL), row(D_MODEL), col_t, col_t] + [row(w) for w in widths[1:]],
        out_shape=[jax.ShapeDtypeStruct((t_len, D_MODEL), F32), jax.ShapeDtypeStruct((t_len, D_MODEL), BF16),
                   jax.ShapeDtypeStruct((D_MODEL, t_len), BF16), jax.ShapeDtypeStruct((D_MODEL, t_len), BF16)]
        + [jax.ShapeDtypeStruct((t_len, w), BF16) for w in widths[1:]],
        scratch_shapes=[pltpu.VMEM((CONV_HALO, CONV_W), F32), pltpu.VMEM((POOL_HALO, POOL_W), F32)],
        compiler_params=_params(dimension_semantics=("arbitrary",)),
    )(x, vec, cps, wpool, win, wout)


def _loss_head(x_final, target):
    t_len = x_final.shape[0]
    n_tiles = t_len // ROW_TILE

    def body(x_ref, t_ref, dx_ref, loss_ref):
        @pl.when(pl.program_id(0) == 0)
        def _():
            loss_ref[...] = jnp.zeros_like(loss_ref)

        err = x_ref[...] - t_ref[...]
        dx_ref[...] = err * (1.0 / D_MODEL)
        loss_ref[...] += jnp.sum(err * err) * (0.5 / D_MODEL)

    row = pl.BlockSpec((ROW_TILE, D_MODEL), lambda i: (i, 0))
    return pl.pallas_call(
        body, name="loss_head", grid=(n_tiles,), in_specs=[row, row],
        out_specs=[row, pl.BlockSpec((SUBLANES, LANES), lambda i: (0, 0))],
        out_shape=[jax.ShapeDtypeStruct((t_len, D_MODEL), F32), jax.ShapeDtypeStruct((SUBLANES, LANES), F32)],
        compiler_params=_params(dimension_semantics=("arbitrary",)),
    )(x_final, target)


def _backward_layer(layer, dxo, y, x, uc, fa, fp, vec, cps, wpool, wout, win):
    t_len = dxo.shape[0]
    n_tiles = t_len // BWD_TILE
    halo_per_tile = BWD_TILE // POOL_HALO
    rev = lambda cols: pl.BlockSpec((BWD_TILE, cols), lambda i: (n_tiles - 1 - i, 0))
    halo_spec = pl.BlockSpec(
        (POOL_HALO, 2 * CONV_W), lambda i: (jnp.maximum((n_tiles - 1 - i) * halo_per_tile - 1, 0), 0))
    gwpool_shape = (len(POOL_WINDOWS), GROUP_D, GROUP_D)

    def body(dxo_ref, y_ref, x_ref, uc_ref, uch_ref, fa_ref, fp_ref, vec_ref, cps_ref, wpool_ref, wout_ref, win_ref,
             dx_ref, dproj_ref, dy_ref, gwpool_ref, dcps_ref, dvec_ref, gwpool_acc, dcc_ref, qc_ref):
        i = pl.program_id(0)
        tile = n_tiles - 1 - i

        @pl.when(i == 0)
        def _():
            gwpool_acc[...] = jnp.zeros_like(gwpool_acc)
            dcps_ref[...] = jnp.zeros_like(dcps_ref)
            dvec_ref[...] = jnp.zeros_like(dvec_ref)
            dcc_ref[...] = jnp.zeros_like(dcc_ref)
            qc_ref[...] = jnp.zeros_like(qc_ref)

        shift, scale, gate = vec_ref[0:1, :], vec_ref[1:2, :], vec_ref[2:3, :]
        g_pre, g_post = vec_ref[3:4, :], vec_ref[4:5, :]
        w0, w1, w2 = cps_ref[0:1, :], cps_ref[1:2, :], cps_ref[2:3, :]

        dxo_t = dxo_ref[...]
        y_t = y_ref[...].astype(F32)
        ry = lax.rsqrt(jnp.mean(y_t * y_t, axis=-1, keepdims=True) + NORM_EPS)
        yh = y_t * ry
        dvec_ref[2:3, :] += jnp.sum(dxo_t * yh, axis=0, keepdims=True)
        dyh = dxo_t * (gate * g_post)
        dy_b = (ry * (dyh - yh * jnp.mean(dyh * yh, axis=-1, keepdims=True))).astype(BF16)
        dy_ref[...] = dy_b
        dycat = _dot_nt(dy_b, wout_ref[...])
        dy_a, dy_p = dycat[:, :CONV_W], dycat[:, CONV_W:]

        fa_t = fa_ref[...].astype(F32)
        db_a = dy_a * fa_t[:, :CONV_W]
        dconv = dy_a * fa_t[:, CONV_W:2 * CONV_W]
        dg_a = dy_a * fa_t[:, 2 * CONV_W:]
        uc_t = uc_ref[...].astype(F32)
        u_a, c_a = uc_t[:, :CONV_W], uc_t[:, CONV_W:]
        halo = jnp.where(tile > 0, uch_ref[...].astype(F32), 0.0)[POOL_HALO - CONV_HALO:]
        z = c_a * u_a
        zcat = jnp.concatenate([halo[:, CONV_W:] * halo[:, :CONV_W], z], axis=0)
        z1 = _rows_from_before(zcat, 1)[CONV_HALO:]
        z2 = _rows_from_before(zcat, 2)[CONV_HALO:]
        dccat = jnp.concatenate([dconv, dcc_ref[...]], axis=0)
        dc1 = _rows_from_after(dccat, 1)[:BWD_TILE]
        dc2 = _rows_from_after(dccat, 2)[:BWD_TILE]
        dz = w2 * dconv + w1 * dc1 + w0 * dc2
        dcc_ref[...] = dconv[:CONV_HALO]
        dcps_ref[0:1, :] += jnp.sum(dconv * z2, axis=0, keepdims=True)
        dcps_ref[1:2, :] += jnp.sum(dconv * z1, axis=0, keepdims=True)
        dcps_ref[2:3, :] += jnp.sum(dconv * z, axis=0, keepdims=True)
        du_a = dz * c_a
        dc_a = dz * u_a

        dmixed = (dy_p * fp_ref[:, :POOL_W].astype(F32)).astype(BF16)
        dg_p = dy_p * fp_ref[:, POOL_W:2 * POOL_W].astype(F32)
        dcps_ref[3:4, :] += jnp.sum(dy_p * fp_ref[:, 2 * POOL_W:3 * POOL_W].astype(F32), axis=0, keepdims=True)
        counts = _window_counts(tile * BWD_TILE, BWD_TILE)
        du_p, q_head = [], []
        for g, w in enumerate(POOL_WINDOWS):
            cols = slice(g * GROUP_D, (g + 1) * GROUP_D)
            dm_g = dmixed[:, cols]
            dpooled_g = _dot_nt(dm_g, wpool_ref[g])
            gwpool_acc[g] += _dot_tn(fp_ref[:, 3 * POOL_W + g * GROUP_D:3 * POOL_W + (g + 1) * GROUP_D], dm_g)
            q_g = dpooled_g * (1.0 / jnp.minimum(counts, float(w)))
            q_head.append(q_g[:POOL_HALO])
            s = jnp.concatenate([q_g, qc_ref[:, cols]], axis=0)
            step = 1
            while step < w:
                s = s + _rows_from_after(s, step)
                step *= 2
            du_p.append(s[:BWD_TILE] - dpooled_g)
        qc_ref[...] = jnp.concatenate(q_head, axis=1)
        dproj_b = jnp.concatenate([du_a, db_a, dc_a, dg_a] + du_p + [dg_p], axis=1).astype(BF16)
        dproj_ref[...] = dproj_b

        x_t = x_ref[...]
        rx = lax.rsqrt(jnp.mean(x_t * x_t, axis=-1, keepdims=True) + NORM_EPS)
        xn = x_t * rx
        mod_scale = 1.0 + scale
        dh = _dot_nt(dproj_b, win_ref[...])
        dvec_ref[0:1, :] += jnp.sum(dh, axis=0, keepdims=True)
        dvec_ref[1:2, :] += jnp.sum(dh * xn, axis=0, keepdims=True)
        dxn = dh * (g_pre * mod_scale)
        dx_ref[...] = dxo_t + rx * (dxn - xn * jnp.mean(dxn * xn, axis=-1, keepdims=True))

        @pl.when(i == n_tiles - 1)
        def _():
            gwpool_ref[...] = gwpool_acc[...].astype(BF16)
            sum_dh_xn, sum_dxo_yh = dvec_ref[1:2, :], dvec_ref[2:3, :]
            dvec_ref[1:2, :] = sum_dh_xn * g_pre
            dvec_ref[3:4, :] = sum_dh_xn * mod_scale
            dvec_ref[2:3, :] = sum_dxo_yh * g_post
            dvec_ref[4:5, :] = sum_dxo_yh * gate

    return pl.pallas_call(
        body, name=f"backward_layer_{layer}", grid=(n_tiles,),
        in_specs=[rev(D_MODEL), rev(D_MODEL), rev(D_MODEL), rev(2 * CONV_W), halo_spec, rev(3 * CONV_W),
                  rev(4 * POOL_W), _layer_spec(vec.shape, layer), _layer_spec(cps.shape, layer),
                  _layer_spec(wpool.shape, layer), _whole_spec(wout.shape), _whole_spec(win.shape)],
        out_specs=[rev(D_MODEL), rev(IN_COLS), rev(D_MODEL), _whole_spec(gwpool_shape),
                   _whole_spec((SUBLANES, CONV_W)), _whole_spec((SUBLANES, D_MODEL))],
        out_shape=[jax.ShapeDtypeStruct((t_len, D_MODEL), F32), jax.ShapeDtypeStruct((t_len, IN_COLS), BF16),
                   jax.ShapeDtypeStruct((t_len, D_MODEL), BF16), jax.ShapeDtypeStruct(gwpool_shape, BF16),
                   jax.ShapeDtypeStruct((SUBLANES, CONV_W), F32), jax.ShapeDtypeStruct((SUBLANES, D_MODEL), F32)],
        scratch_shapes=[pltpu.VMEM(gwpool_shape, F32), pltpu.VMEM((CONV_HALO, CONV_W), F32),
                        pltpu.VMEM((POOL_HALO, POOL_W), F32)],
        compiler_params=_params(dimension_semantics=("arbitrary",)),
    )(dxo, y, x, uc, uc, fa, fp, vec, cps, wpool, wout, win)


def _weight_grads(layer, h_t, dproj, ycat_t, dy, exchange):
    t_len = dy.shape[0]
    n_in, n_out = IN_COLS // GWIN_COLS, D_MODEL // GWOUT_COLS
    args = [h_t, dproj, ycat_t, dy]
    in_specs = [_whole_spec(h_t.shape),
                pl.BlockSpec((t_len, GWIN_COLS), lambda s: (0, jnp.minimum(s, n_in - 1))),
                _whole_spec(ycat_t.shape),
                pl.BlockSpec((t_len, GWOUT_COLS), lambda s: (0, jnp.maximum(s - n_in, 0)))]
    out_shape = [jax.ShapeDtypeStruct((D_MODEL, IN_COLS), BF16), jax.ShapeDtypeStruct((D_MODEL, D_MODEL), BF16)]
    out_specs = [pl.BlockSpec((D_MODEL, GWIN_COLS), lambda s: (0, jnp.minimum(s, n_in - 1))),
                 pl.BlockSpec((D_MODEL, GWOUT_COLS), lambda s: (0, jnp.maximum(s - n_in, 0)))]
    scratch = []
    aliases, split = _host(exchange, args, in_specs, out_shape, out_specs, scratch)

    def body(*refs):
        (ht_ref, dproj_ref, ycatt_ref, dy_ref, gwin_ref, gwout_ref), hosted = split(refs)
        s = pl.program_id(0)
        if hosted is not None:
            pl.when(s == 0)(hosted[0])

        @pl.when(s < n_in)
        def _():
            gwin_ref[...] = _dot(ht_ref[...], dproj_ref[...]).astype(BF16)

        @pl.when(s >= n_in)
        def _():
            gwout_ref[...] = _dot(ycatt_ref[...], dy_ref[...]).astype(BF16)

        if hosted is not None:
            pl.when(s == n_in + n_out - 1)(hosted[1])

    return pl.pallas_call(
        body, name=f"weight_grads_{layer}", grid=(n_in + n_out,), in_specs=in_specs, out_specs=out_specs,
        out_shape=out_shape, scratch_shapes=scratch, input_output_aliases=aliases,
        compiler_params=_params(dimension_semantics=("arbitrary",)),
    )(*args)


def _add_sibling_blocks(name, layer, grads, received, core, partials, kinds):
    n_arr = len(grads)

    def body(core_ref, *refs):
        mine, theirs, outs = refs[:n_arr], refs[n_arr:2 * n_arr], refs[-n_arr:]
        for a in range(n_arr):
            outs[a][...] = (mine[a][...].astype(F32) + theirs[a][...].astype(F32)).astype(BF16)

    own_of_kind = [
        pl.BlockSpec((D_MODEL, W_IN_SHARD), lambda q, core_ref: (0, 2 * q + core_ref[0])),
        pl.BlockSpec((W_OUT_SHARD, D_MODEL), lambda q, core_ref: (2 * q + core_ref[0], 0)),
        pl.BlockSpec((POOL_SHARD, GROUP_D), lambda q, core_ref: (2 * q + core_ref[0], 0)),
    ]
    shapes = [_BLOCK_SHAPES[k] for k in kinds]
    recv_specs = [pl.BlockSpec((None,) + s, lambda q, core_ref: (q, 0, 0)) for s in shapes]
    out_specs = [pl.BlockSpec((None, None) + s, lambda q, core_ref: (q, layer, 0, 0)) for s in shapes]
    args = [core, *grads, *received]
    in_specs = [own_of_kind[k] for k in kinds] + recv_specs
    aliases = {}
    if partials is not None:
        aliases = {len(args) + a: a for a in range(n_arr)}
        args += list(partials)
        in_specs += [HBM] * n_arr
    return pl.pallas_call(
        body, name=name,
        grid_spec=pltpu.PrefetchScalarGridSpec(
            num_scalar_prefetch=1, grid=(N_CHIP,), in_specs=in_specs, out_specs=out_specs),
        out_shape=[jax.ShapeDtypeStruct((N_CHIP, DEPTH) + s, BF16) for s in shapes],
        input_output_aliases=aliases,
        compiler_params=_params(dimension_semantics=("arbitrary",)),
    )(*args)


def _modulation_columns(c_all, w_ada):
    def body(c_ref, w_ref, cact_ref, out_ref):
        c_t = c_ref[...]
        c_act = c_t * _sigmoid(c_t)
        cact_ref[...] = c_act
        out_ref[...] = jnp.dot(c_act, w_ref[...], preferred_element_type=F32, precision=lax.Precision.HIGHEST)

    return pl.pallas_call(
        body, name="modulation_columns", grid=(DEPTH,),
        in_specs=[pl.BlockSpec((N_DEV, D_MODEL), lambda l: (0, 0)),
                  pl.BlockSpec((None, D_MODEL, W_IN_SHARD), lambda l: (l, 0, 0))],
        out_specs=[pl.BlockSpec((N_DEV, D_MODEL), lambda l: (0, 0)),
                   pl.BlockSpec((N_DEV, W_IN_SHARD), lambda l: (0, l))],
        out_shape=[jax.ShapeDtypeStruct((N_DEV, D_MODEL), F32),
                   jax.ShapeDtypeStruct((N_DEV, DEPTH * W_IN_SHARD), F32)],
        compiler_params=_params(dimension_semantics=("arbitrary",)),
    )(c_all, w_ada)


def _adamw(w, g, m, v):
    m_new = ADAM_B1 * m + (1.0 - ADAM_B1) * g
    v_new = ADAM_B2 * v + (1.0 - ADAM_B2) * (g * g)
    m_hat = m_new / (1.0 - ADAM_B1 ** ADAM_STEP)
    v_hat = v_new / (1.0 - ADAM_B2 ** ADAM_STEP)
    delta = -ADAM_LR * (m_hat / (jnp.sqrt(v_hat) + ADAM_EPS) + ADAM_WD * w)
    return delta, m_new, v_new


def _adamw_w_ada(w, m, v, c_act_t, dmod_cols, exchange):
    big = pl.BlockSpec((None, D_MODEL, W_IN_SHARD), lambda l: (l, 0, 0))
    args = [w, m, v, c_act_t, dmod_cols]
    in_specs = [big, big, big, pl.BlockSpec((D_MODEL, N_DEV), lambda l: (0, 0)),
                pl.BlockSpec((None, N_DEV, W_IN_SHARD), lambda l: (l, 0, 0))]
    out_shape, out_specs, scratch = [jax.ShapeDtypeStruct(w.shape, F32)] * 4, [big] * 4, []
    aliases, split = _host(exchange, args, in_specs, out_shape, out_specs, scratch)

    def body(*refs):
        (w_ref, m_ref, v_ref, ct_ref, dm_ref, g_ref, d_ref, mo_ref, vo_ref), hosted = split(refs)
        if hosted is not None:
            pl.when(pl.program_id(0) == 0)(hosted[0])
        g = ct_ref[:, 0:1] * dm_ref[0:1, :]
        for b in range(1, N_DEV):
            g = g + ct_ref[:, b:b + 1] * dm_ref[b:b + 1, :]
        g_ref[...] = g
        d_ref[...], mo_ref[...], vo_ref[...] = _adamw(w_ref[...], g, m_ref[...], v_ref[...])
        if hosted is not None:
            pl.when(pl.program_id(0) == DEPTH - 1)(hosted[1])

    return pl.pallas_call(
        body, name="adamw_w_ada", grid=(DEPTH,), in_specs=in_specs, out_specs=out_specs, out_shape=out_shape,
        scratch_shapes=scratch, input_output_aliases=aliases,
        compiler_params=_params(dimension_semantics=("arbitrary",)),
    )(*args)


def _sum_chip_partials(own_ref, recv_ref):
    g = own_ref[...].astype(F32)
    for j in range(N_OTHER_CHIPS):
        g = g + recv_ref[j].astype(F32)
    return g


def _partial_specs(row_tile, cols, first_layer=0):
    own = pl.BlockSpec((None, None, row_tile, cols), lambda l, r, chip_ref: (chip_ref[0], first_layer + l, r, 0))
    recv = pl.BlockSpec((N_OTHER_CHIPS, None, row_tile, cols), lambda l, r, chip_ref: (0, first_layer + l, r, 0))
    return own, recv


def _adamw_reduced(name, w, m, v, partial, received, chip, row_tile, layers, continued):
    depth, rows, cols = w.shape
    first, stop = layers

    def body(chip_ref, w_ref, m_ref, v_ref, own_ref, recv_ref, *rest):
        g_ref, d_ref, mo_ref, vo_ref = rest[-4:]
        g = _sum_chip_partials(own_ref, recv_ref)
        g_ref[...] = g
        d_ref[...], mo_ref[...], vo_ref[...] = _adamw(w_ref[...], g, m_ref[...], v_ref[...])

    blk = pl.BlockSpec((None, row_tile, cols), lambda l, r, chip_ref: (first + l, r, 0))
    args = [chip, w, m, v, partial, received]
    in_specs = [blk, blk, blk, *_partial_specs(row_tile, cols, first)]
    aliases = {}
    if continued is not None:
        aliases = {len(args) + k: k for k in range(4)}
        args += list(continued)
        in_specs += [HBM] * 4
    return pl.pallas_call(
        body, name=name,
        grid_spec=pltpu.PrefetchScalarGridSpec(
            num_scalar_prefetch=1, grid=(stop - first, rows // row_tile), in_specs=in_specs, out_specs=[blk] * 4),
        out_shape=[jax.ShapeDtypeStruct(w.shape, F32)] * 4, input_output_aliases=aliases,
        compiler_params=_params(dimension_semantics=("arbitrary", "arbitrary")),
    )(*args)


def _reduce_w_pool(partial, received, chip):
    def body(chip_ref, own_ref, recv_ref, g_ref):
        g_ref[...] = _sum_chip_partials(own_ref, recv_ref)

    return pl.pallas_call(
        body, name="reduce_w_pool",
        grid_spec=pltpu.PrefetchScalarGridSpec(
            num_scalar_prefetch=1, grid=(DEPTH, 1), in_specs=list(_partial_specs(POOL_SHARD, GROUP_D)),
            out_specs=pl.BlockSpec((POOL_SHARD, GROUP_D), lambda l, r, chip_ref: (l, 0))),
        out_shape=jax.ShapeDtypeStruct((DEPTH * POOL_SHARD, GROUP_D), F32),
        compiler_params=_params(dimension_semantics=("arbitrary", "arbitrary")),
    )(chip, partial, received)


def _adamw_small(params):
    n = len(params)

    def body(*refs):
        ins, outs = refs[:4 * n], refs[4 * n:]
        for p in range(n):
            w_ref, g_ref, m_ref, v_ref = ins[4 * p:4 * p + 4]
            d_ref, mo_ref, vo_ref = outs[3 * p:3 * p + 3]
            d_ref[...], mo_ref[...], vo_ref[...] = _adamw(w_ref[...], g_ref[...], m_ref[...], v_ref[...])

    vmem = pl.BlockSpec(memory_space=pltpu.VMEM)
    flat = [a for group in params for a in group]
    out_shape = [jax.ShapeDtypeStruct(group[0].shape, F32) for group in params for _ in range(3)]
    outs = pl.pallas_call(
        body, name="adamw_small", in_specs=[vmem] * len(flat), out_specs=[vmem] * len(out_shape),
        out_shape=out_shape, compiler_params=_params(),
    )(*flat)
    return [tuple(outs[3 * p:3 * p + 3]) for p in range(n)]


def _sum_sources(slabs):
    def body(s_ref, o_ref):
        acc = s_ref[0]
        for b in range(1, N_DEV):
            acc = acc + s_ref[b]
        o_ref[...] = acc

    vmem = pl.BlockSpec(memory_space=pltpu.VMEM)
    return pl.pallas_call(
        body, name="sum_small_grads", in_specs=[vmem], out_specs=vmem,
        out_shape=jax.ShapeDtypeStruct(slabs.shape[1:], F32), compiler_params=_params(),
    )(slabs)


def _to_bf16(a, name, layers=None):
    first, stop = layers or (0, a.shape[0])

    def body(a_ref, o_ref):
        o_ref[...] = a_ref[...].astype(BF16)

    block = (None,) + a.shape[1:]
    return pl.pallas_call(
        body, name=name, grid=(stop - first,), in_specs=[pl.BlockSpec(block, lambda l: (first + l, 0, 0))],
        out_specs=pl.BlockSpec(block, lambda l: (l, 0, 0)),
        out_shape=jax.ShapeDtypeStruct((stop - first,) + a.shape[1:], BF16),
        compiler_params=_params(dimension_semantics=("arbitrary",)),
    )(a)


def kernel(x, c, w_ada, b_ada, g_pre, w_in, w_conv, w_pool, pool_scale, w_out, g_post, loss_target, m_w_ada, m_b_ada, m_g_pre, m_w_in, m_w_conv, m_w_pool, m_pool_scale, m_w_out, m_g_post, v_w_ada, v_b_ada, v_g_pre, v_w_in, v_w_conv, v_w_pool, v_pool_scale, v_w_out, v_g_post):
    mx, my, mc = _mesh_position()
    me = _block_id(mx, my, mc)
    chip = (2 * mx + my).astype(jnp.int32).reshape(1)
    core = mc.astype(jnp.int32).reshape(1)
    x0 = x[0]
    target = loss_target[0]
    conv_shard = w_conv.shape[-1]

    own_small = jnp.concatenate([c, w_conv.reshape(1, DEPTH * 3 * conv_shard)], axis=1)
    all_small = _all_gather_small(own_small, "all_gather_c_w_conv")[:, 0, :]
    c_all = all_small[:, :D_MODEL]
    w_conv_full = all_small[:, D_MODEL:].reshape(N_DEV, DEPTH, 3, conv_shard).transpose(1, 2, 0, 3).reshape(
        DEPTH, 3, CONV_W)
    cps = jnp.concatenate([w_conv_full, pool_scale[:, None], jnp.zeros((DEPTH, 4, CONV_W), F32)], axis=1)

    c_act, pieces = _modulation_columns(c_all, w_ada)
    mod_all = _all_gather_small(pieces, "all_gather_modulation")
    mod_mine = lax.dynamic_index_in_dim(mod_all, me, axis=1, keepdims=False)
    mod = mod_mine.reshape(N_DEV, DEPTH, W_IN_SHARD).transpose(1, 0, 2).reshape(DEPTH, 3 * D_MODEL) + b_ada
    zeros_d = jnp.zeros((DEPTH, 3, D_MODEL), F32)
    vec = jnp.concatenate([mod.reshape(DEPTH, 3, D_MODEL), g_pre[:, None], g_post[:, None], zeros_d], axis=1)

    gathers = []
    after = [mod_all, all_small]
    for group in ((0, 1), (1, DEPTH)):
        first_sems, shards, landing = _gather_weights_start(
            group[0], _to_bf16(w_in, f"cast_w_in_{group[0]}", group), _to_bf16(w_out, f"cast_w_out_{group[0]}", group),
            after)
        gathers += [(first_sems, shards, landing[k], k) for k in range(group[1] - group[0])]
        after = [shards[0]]
    wpool_b = _to_bf16(w_pool.reshape(DEPTH, POOL_ROWS, GROUP_D), "cast_w_pool").reshape(w_pool.shape)

    xs, kept, wins, wouts = [x0], [], [], []
    for l in range(DEPTH):
        first_sems, shards, zones, index = gathers[l]
        passed_sems, zones = _gather_weights_pass_on(
            l, index, first_sems[1], zones, [vec, cps, wpool_b, gathers[-1][1][0]] if l == 0 else [xs[-1]])
        win, wout = _gather_weights_finish(l, index, first_sems, passed_sems, shards, zones)
        x_next, *for_backward = _forward_layer(l, xs[-1], vec, cps, wpool_b, win, wout)
        xs.append(x_next)
        kept.append(for_backward)
        wins.append(win)
        wouts.append(wout)
    dx, loss_tile = _loss_head(xs[DEPTH], target)

    slab_rows = [None] * DEPTH
    partials = received = None
    in_flight = []

    def scatter(layer, grads, from_sibling, after):
        nonlocal partials, received
        partials = _add_sibling_blocks(
            f"grad_add_sibling_{layer}", layer, grads, from_sibling, core, partials, ALL_KINDS)
        chips = _chips_exchange(layer, partials, received, ALL_KINDS)
        sems, partials, received, token = _start_exchange(chips, f"grad_chips_start_{layer}", after)
        in_flight.append((chips, sems, layer))
        return token

    grads_above = None
    for l in reversed(range(DEPTH)):
        y, h_t, ycat_t, uc, fa, fp = kept[l]
        dx, dproj, dy, gwpool, dcps, dvec = _backward_layer(
            l, dx, y, xs[l], uc, fa, fp, vec, cps, wpool_b, wouts[l], wins[l])
        slab_rows[l] = jnp.concatenate(
            [dvec[0], dvec[1], dvec[2], dvec[3], dvec[4], dcps[3], dcps[0], dcps[1], dcps[2],
             loss_tile[0] if l == 0 else jnp.zeros((LANES,), F32)])
        if l == 0:
            slabs = _all_gather_small(jnp.stack(slab_rows), "all_gather_small_grads")
        hosted = _sibling_exchange(grads_above, ALL_KINDS) if grads_above is not None else None
        gwin, gwout, *from_sibling = _weight_grads(l, h_t, dproj, ycat_t, dy, hosted)
        if grads_above is not None:
            scatter(l + 1, grads_above, from_sibling, [])
        grads_above = [gwin, gwout, gwpool.reshape(POOL_ROWS, GROUP_D)]
        if l <= 1:
            from_sibling = _run_exchange(_sibling_exchange(grads_above, ALL_KINDS), f"grad_exchange_sibling_{l}")
            token = scatter(l, grads_above, from_sibling, [slabs] if l == 0 else [])
            grads_above = None
    grad_x = dx[None]
    chips_0, sems_0, _ = in_flight.pop()

    total = _sum_sources(slabs)
    loss = total[0, SLAB_COLS]
    o = 3 * D_MODEL
    g_b_ada = total[:, :o]
    g_g_pre = total[:, o:o + D_MODEL]
    g_g_post = total[:, o + D_MODEL:o + 2 * D_MODEL]
    g_pool_scale = total[:, o + 2 * D_MODEL:o + 2 * D_MODEL + POOL_W]
    g_conv_full = total[:, o + 2 * D_MODEL + POOL_W:SLAB_COLS].reshape(DEPTH, 3, CONV_W)
    g_w_conv = lax.dynamic_slice_in_dim(g_conv_full, me * conv_shard, conv_shard, axis=2)

    after = [token]
    for chips, sems, l in in_flight:
        partials, received = _finish_exchange(chips, f"grad_chips_finish_{l}", sems, partials, received, after)
        after = []
    upper = (1, DEPTH)
    w_in_upper = _adamw_reduced(
        "adamw_w_in_upper", w_in, m_w_in, v_w_in, partials[0], received[0], chip, ROW_TILE, upper, None)
    w_out_upper = _adamw_reduced(
        "adamw_w_out_upper", w_out, m_w_out, v_w_out, partials[1], received[1], chip, W_OUT_SHARD, upper, None)
    dmod_all = slabs[:, :, :o].reshape(N_DEV, DEPTH, N_DEV, W_IN_SHARD)
    dmod_cols = lax.dynamic_index_in_dim(dmod_all, me, axis=2, keepdims=False).transpose(1, 0, 2) + token[0, 0]
    g_w_ada, d_w_ada, nm_w_ada, nv_w_ada = _adamw_w_ada(w_ada, m_w_ada, v_w_ada, c_act.T, dmod_cols, None)

    partials, received = _finish_exchange(
        chips_0, "grad_chips_finish_0", sems_0, partials, received, [nv_w_ada, w_in_upper[3], w_out_upper[3]])
    gather_pool = _all_gather_exchange(_reduce_w_pool(partials[2], received[2], chip))
    sems_p, pool_rows, pool_landing, token_p = _start_exchange(gather_pool, "all_gather_grad_w_pool_start")
    g_w_in, d_w_in, nm_w_in, nv_w_in = _adamw_reduced(
        "adamw_w_in_0", w_in, m_w_in, v_w_in, partials[0], received[0], chip, ROW_TILE, (0, 1), w_in_upper)
    g_w_out, d_w_out, nm_w_out, nv_w_out = _adamw_reduced(
        "adamw_w_out_0", w_out, m_w_out, v_w_out, partials[1], received[1], chip, W_OUT_SHARD, (0, 1), w_out_upper)
    _, (g_pool_all,) = _finish_exchange(
        gather_pool, "all_gather_grad_w_pool_finish", sems_p, pool_rows, pool_landing, [nv_w_in, nv_w_out])
    g_w_pool = g_pool_all.reshape(N_DEV, DEPTH, POOL_SHARD, GROUP_D).transpose(1, 0, 2, 3).reshape(w_pool.shape)

    flat2 = lambda a: a.reshape(-1, a.shape[-1])
    small = _adamw_small([
        (b_ada, g_b_ada, m_b_ada, v_b_ada),
        (g_pre, g_g_pre, m_g_pre, v_g_pre),
        (flat2(w_conv), flat2(g_w_conv), flat2(m_w_conv), flat2(v_w_conv)),
        (flat2(w_pool), flat2(g_w_pool), flat2(m_w_pool), flat2(v_w_pool)),
        (pool_scale, g_pool_scale, m_pool_scale, v_pool_scale),
        (g_post, g_g_post, m_g_post, v_g_post),
    ])
    (d_b_ada, nm_b_ada, nv_b_ada), (d_g_pre, nm_g_pre, nv_g_pre), conv_upd, pool_upd, \
        (d_ps, nm_ps, nv_ps), (d_g_post, nm_g_post, nv_g_post) = small
    d_w_conv, nm_w_conv, nv_w_conv = (a.reshape(w_conv.shape) for a in conv_upd)
    d_w_pool, nm_w_pool, nv_w_pool = (a.reshape(w_pool.shape) for a in pool_upd)

    return (loss, grad_x,
            g_w_ada, g_b_ada, g_g_pre, g_w_in, g_w_conv, g_w_pool, g_pool_scale, g_w_out, g_g_post,
            d_w_ada, d_b_ada, d_g_pre, d_w_in, d_w_conv, d_w_pool, d_ps, d_w_out, d_g_post,
            nm_w_ada, nm_b_ada, nm_g_pre, nm_w_in, nm_w_conv, nm_w_pool, nm_ps, nm_w_out, nm_g_post,
            nv_w_ada, nv_b_ada, nv_g_pre, nv_w_in, nv_w_conv, nv_w_pool, nv_ps, nv_w_out, nv_g_post)
```

```python
import jax
import jax.numpy as jnp
from jax import lax
from jax.experimental import pallas as pl
from jax.experimental.pallas import tpu as pltpu

F32 = jnp.float32
BF16 = jnp.bfloat16

D_MODEL = 1024
DEPTH = 4
CONV_W = 512
POOL_W = 512
POOL_WINDOWS = (2, 4, 8, 16)
GROUP_D = 128
IN_COLS = 4 * CONV_W + 2 * POOL_W
NORM_EPS = 1e-6

ADAM_LR = 0.001
ADAM_B1 = 0.9
ADAM_B2 = 0.999
ADAM_EPS = 1e-08
ADAM_WD = 0.01
ADAM_STEP = 10

N_DEV = 8
N_CHIP = 4
N_OTHER_CHIPS = N_CHIP - 1
MESH = pl.DeviceIdType.MESH
W_IN_SHARD = IN_COLS // N_DEV
W_OUT_SHARD = D_MODEL // N_DEV
POOL_ROWS = len(POOL_WINDOWS) * GROUP_D
POOL_SHARD = POOL_ROWS // N_DEV

SUBLANES = 8
LANES = 128
VMEM_LIMIT_BYTES = 56 * 1024 * 1024
ROW_TILE = 512
BWD_TILE = 256
GWIN_COLS = 768
GWOUT_COLS = 512
POOL_HALO = 16
CONV_HALO = SUBLANES

SLAB_COLS = 3 * D_MODEL + D_MODEL + D_MODEL + POOL_W + 3 * CONV_W

HBM = pl.BlockSpec(memory_space=pl.ANY)


def _params(**kw):
    return pltpu.CompilerParams(vmem_limit_bytes=VMEM_LIMIT_BYTES, **kw)


def _sigmoid(v):
    return 1.0 / (1.0 + jnp.exp(-v))


def _dot(a, b):
    return jnp.dot(a, b, preferred_element_type=F32)


def _dot_tn(a, b):
    return lax.dot_general(a, b, (((0,), (0,)), ((), ())), preferred_element_type=F32)


def _dot_nt(a, b):
    return lax.dot_general(a, b, (((1,), (1,)), ((), ())), preferred_element_type=F32)


def _rows_from_before(v, k):
    return pltpu.roll(v, k, 0)


def _rows_from_after(v, k):
    return pltpu.roll(v, v.shape[0] - k, 0)


def _window_counts(t0, rows):
    return (lax.broadcasted_iota(jnp.int32, (rows, 1), 0) + (t0 + 1)).astype(F32)


def _split_proj(p32):
    cw = CONV_W
    return (p32[:, 0 * cw:1 * cw], p32[:, 1 * cw:2 * cw], p32[:, 2 * cw:3 * cw], p32[:, 3 * cw:4 * cw],
            p32[:, 4 * cw:4 * cw + POOL_W], p32[:, 4 * cw + POOL_W:])


def _layer_spec(shape, layer):
    nd = len(shape)
    return pl.BlockSpec((None,) + tuple(shape[1:]), lambda i, _l=layer, _n=nd: (_l,) + (0,) * (_n - 1))


def _whole_spec(shape):
    return pl.BlockSpec(tuple(shape), lambda i, _n=len(shape): (0,) * _n)


def _mesh_position():
    return lax.axis_index("x"), lax.axis_index("y"), lax.axis_index("c")


def _block_id(x, y, c):
    return 4 * x + 2 * y + c


def _other_chips(x, y):
    return [(x ^ 1, y), (x, y ^ 1), (x ^ 1, y ^ 1)]


def _col_block(ref, blk):
    return ref.at[:, pl.ds(pl.multiple_of(blk * W_IN_SHARD, LANES), W_IN_SHARD)]


def _row_block(rows):
    def block(ref, blk):
        return ref.at[pl.ds(pl.multiple_of(blk * rows, rows), rows), :]
    return block


_BLOCK_OF = (_col_block, _row_block(W_OUT_SHARD), _row_block(POOL_SHARD))
_BLOCK_SHAPES = ((D_MODEL, W_IN_SHARD), (W_OUT_SHARD, D_MODEL), (POOL_SHARD, GROUP_D))


class _Exchange:
    def __init__(self, inputs, out_shapes, aliases, sem_shapes, make):
        self.inputs, self.out_shapes, self.aliases, self.sem_shapes, self.make = (
            list(inputs), list(out_shapes), dict(aliases), list(sem_shapes), make)


def _run_exchange(exchange, name):
    n_in, n_out = len(exchange.inputs), len(exchange.out_shapes)

    def body(*refs):
        start, finish = exchange.make(refs[:n_in], refs[n_in:n_in + n_out], refs[n_in + n_out:])
        start()
        finish()

    return pl.pallas_call(
        body, name=name, in_specs=[HBM] * n_in, out_specs=[HBM] * n_out, out_shape=exchange.out_shapes,
        scratch_shapes=exchange.sem_shapes, input_output_aliases=exchange.aliases, compiler_params=_params(),
    )(*exchange.inputs)


_SEM = pl.BlockSpec(memory_space=pltpu.SEMAPHORE)
_DATAFLOW = pltpu.SideEffectType.DATAFLOW_SIDE_EFFECTING


def _start_exchange(exchange, name, after=()):
    n_in, n_out, n_sem = len(exchange.inputs), len(exchange.out_shapes), len(exchange.sem_shapes)
    sources = [i for i in range(n_in) if i not in exchange.aliases]
    aliases = {i: n_sem + k for k, i in enumerate(sources)}
    aliases.update({i: n_sem + len(sources) + o for i, o in exchange.aliases.items()})

    def body(*refs):
        in_refs = refs[:n_in]
        outs = refs[n_in + len(after):]
        sems = outs[:n_sem]
        out_refs = outs[n_sem + len(sources):n_sem + len(sources) + n_out]
        exchange.make(in_refs, out_refs, sems)[0]()
        refs[-1][...] = jnp.zeros_like(refs[-1])

    outs = pl.pallas_call(
        body, name=name, in_specs=[HBM] * (n_in + len(after)),
        out_specs=[_SEM] * n_sem + [HBM] * (len(sources) + n_out) + [pl.BlockSpec(memory_space=pltpu.VMEM)],
        out_shape=(exchange.sem_shapes + [pltpu.HBM(exchange.inputs[i].shape, exchange.inputs[i].dtype) for i in sources]
                   + [pltpu.HBM(s.shape, s.dtype) for s in exchange.out_shapes]
                   + [jax.ShapeDtypeStruct((SUBLANES, LANES), F32)]),
        input_output_aliases=aliases, compiler_params=_params(has_side_effects=_DATAFLOW),
    )(*exchange.inputs, *after)
    return outs[:n_sem], outs[n_sem:n_sem + len(sources)], outs[n_sem + len(sources):-1], outs[-1]


def _finish_exchange(exchange, name, sems, sources, landing, after):
    n_src, n_out, n_sem = len(sources), len(landing), len(sems)
    n_in = len(exchange.inputs)
    source_at = [i for i in range(n_in) if i not in exchange.aliases]

    def body(*refs):
        src_refs, out_refs = refs[:n_src], refs[n_src:n_src + n_out]
        sem_refs = refs[n_src + n_out:n_src + n_out + n_sem]
        in_refs = [None] * n_in
        for k, i in enumerate(source_at):
            in_refs[i] = src_refs[k]
        for i, o in exchange.aliases.items():
            in_refs[i] = out_refs[o]
        exchange.make(in_refs, out_refs, sem_refs)[1]()

    arrays = list(sources) + list(landing)
    outs = pl.pallas_call(
        body, name=name, in_specs=[HBM] * len(arrays) + [_SEM] * n_sem + [HBM] * len(after),
        out_specs=[HBM] * len(arrays), out_shape=[pltpu.HBM(a.shape, a.dtype) for a in arrays],
        input_output_aliases={i: i for i in range(len(arrays))}, compiler_params=_params(has_side_effects=_DATAFLOW),
    )(*arrays, *sems, *after)
    return outs[:n_src], outs[n_src:]


N_GATHERED = 2
FIRST_COPIES = 1 + N_OTHER_CHIPS
_GATHERED_SHAPES = ((D_MODEL, IN_COLS), (D_MODEL, D_MODEL))


def _first_sem(layer, a, k):
    return (layer * N_GATHERED + a) * FIRST_COPIES + k


def _gather_copy(window_of, full_ref, blk, send_sem, recv_sem, to, src=None):
    window = window_of(full_ref, blk)
    return pltpu.make_async_remote_copy(
        src_ref=window if src is None else src, dst_ref=window, send_sem=send_sem, recv_sem=recv_sem,
        device_id=to, device_id_type=MESH)


def _first_copies(layer, shard_refs, full_refs, send_sems, recv_sems, local_sems):
    x, y, c = _mesh_position()
    me = _block_id(x, y, c)
    own, remote = [], []
    for a in range(N_GATHERED):
        shard = shard_refs[a].at[layer]
        own.append(pltpu.make_async_copy(
            shard, _BLOCK_OF[a](full_refs[a], me), local_sems.at[layer * N_GATHERED + a]))
        targets = [(x, y, 1 - c)] + [(*chip, c) for chip in _other_chips(x, y)]
        remote += [_gather_copy(_BLOCK_OF[a], full_refs[a], me, send_sems.at[_first_sem(layer, a, k)],
                                recv_sems.at[_first_sem(layer, a, k)], to, src=shard)
                   for k, to in enumerate(targets)]
    return own, remote


def _gather_weights_start(first_layer, win_shards, wout_shards, after):
    n_layers = win_shards.shape[0]
    n_first = n_layers * N_GATHERED * FIRST_COPIES
    sem_shapes = [pltpu.SemaphoreType.DMA((n_first,)), pltpu.SemaphoreType.DMA((n_first,)),
                  pltpu.SemaphoreType.DMA((n_layers * N_GATHERED,))]
    shards = [win_shards, wout_shards]

    def body(win_sh, wout_sh, *rest):
        send_sems, recv_sems, local_sems, win_thru, wout_thru, *landing = rest[len(after):]
        for layer in range(n_layers):
            own, remote = _first_copies(layer, (win_sh, wout_sh), landing[N_GATHERED * layer:N_GATHERED * (layer + 1)],
                                        send_sems, recv_sems, local_sems)
            for cp in own + remote:
                cp.start()

    outs = pl.pallas_call(
        body, name=f"all_gather_weights_start_{first_layer}", in_specs=[HBM] * (2 + len(after)),
        out_specs=[_SEM] * 3 + [HBM] * (2 + n_layers * N_GATHERED),
        out_shape=(sem_shapes + [pltpu.HBM(s.shape, s.dtype) for s in shards]
                   + [pltpu.HBM(s, BF16) for _ in range(n_layers) for s in _GATHERED_SHAPES]),
        input_output_aliases={0: 3, 1: 4}, compiler_params=_params(has_side_effects=_DATAFLOW),
    )(*shards, *after)
    landing = outs[5:]
    return outs[:3], outs[3:5], [landing[N_GATHERED * l:N_GATHERED * (l + 1)] for l in range(n_layers)]


def _passed_on_copies(full_refs, send_sems, recv_sems, core_of_block):
    x, y, c = _mesh_position()
    return [_gather_copy(_BLOCK_OF[a], full_refs[a], _block_id(*chip, core_of_block),
                         send_sems.at[a * N_OTHER_CHIPS + j], recv_sems.at[a * N_OTHER_CHIPS + j], (x, y, 1 - c))
            for a in range(N_GATHERED) for j, chip in enumerate(_other_chips(x, y))]


def _gather_weights_pass_on(layer, index, first_recv_sems, landing, after):
    n = N_GATHERED * N_OTHER_CHIPS

    def body(win_ref, wout_ref, first_recv, *rest):
        send_sems, recv_sems = rest[len(after):len(after) + 2]
        x, y, c = _mesh_position()
        full_refs = (win_ref, wout_ref)
        passed = _passed_on_copies(full_refs, send_sems, recv_sems, c)
        for a in range(N_GATHERED):
            for j, chip in enumerate(_other_chips(x, y)):
                sem = _first_sem(index, a, 1 + j)
                _gather_copy(_BLOCK_OF[a], full_refs[a], _block_id(*chip, c), first_recv.at[sem], first_recv.at[sem],
                             (x, y, c)).wait_recv()
                passed[a * N_OTHER_CHIPS + j].start()

    outs = pl.pallas_call(
        body, name=f"all_gather_weights_pass_on_{layer}", in_specs=[HBM] * N_GATHERED + [_SEM] + [HBM] * len(after),
        out_specs=[_SEM] * 2 + [HBM] * N_GATHERED,
        out_shape=[pltpu.SemaphoreType.DMA((n,)), pltpu.SemaphoreType.DMA((n,))]
        + [pltpu.HBM(a.shape, a.dtype) for a in landing],
        input_output_aliases={a: 2 + a for a in range(N_GATHERED)},
        compiler_params=_params(has_side_effects=_DATAFLOW),
    )(*landing, first_recv_sems, *after)
    return outs[:2], outs[2:]


def _gather_weights_finish(layer, index, first_sems, passed_sems, shards, landing):
    def body(win_ref, wout_ref, first_send, first_recv, local_sems, passed_send, passed_recv, win_sh, wout_sh, *thru):
        x, y, c = _mesh_position()
        full_refs = (win_ref, wout_ref)
        own, remote = _first_copies(index, (win_sh, wout_sh), full_refs, first_send, first_recv, local_sems)
        for a in range(N_GATHERED):
            sem = _first_sem(index, a, 0)
            _gather_copy(_BLOCK_OF[a], full_refs[a], _block_id(x, y, 1 - c), first_recv.at[sem], first_recv.at[sem],
                         (x, y, c)).wait_recv()
        for cp in _passed_on_copies(full_refs, passed_send, passed_recv, 1 - c):
            cp.wait_recv()
        for cp in remote + _passed_on_copies(full_refs, passed_send, passed_recv, c):
            cp.wait_send()
        for cp in own:
            cp.wait()

    return pl.pallas_call(
        body, name=f"all_gather_weights_finish_{layer}", in_specs=[HBM] * N_GATHERED + [_SEM] * 5 + [HBM] * 2,
        out_specs=[HBM] * N_GATHERED, out_shape=[pltpu.HBM(a.shape, a.dtype) for a in landing],
        input_output_aliases={a: a for a in range(N_GATHERED)}, compiler_params=_params(has_side_effects=_DATAFLOW),
    )(*landing, *first_sems, *passed_sems, *shards)


ALL_KINDS = (0, 1, 2)


def _sibling_exchange(grads, kinds):
    n_arr = len(grads)

    def make(in_refs, out_refs, sems):
        send_sems, recv_sems = sems
        x, y, c = _mesh_position()
        copies = [pltpu.make_async_remote_copy(
            src_ref=_BLOCK_OF[kinds[a]](in_refs[a], 2 * q + (1 - c)), dst_ref=out_refs[a].at[q],
            send_sem=send_sems.at[a, q], recv_sem=recv_sems.at[a, q], device_id=(x, y, 1 - c), device_id_type=MESH)
            for a in range(n_arr) for q in range(N_CHIP)]

        def start():
            for cp in copies:
                cp.start()

        def finish():
            for cp in copies:
                cp.wait_recv()
            for cp in copies:
                cp.wait_send()

        return start, finish

    return _Exchange(
        grads, [jax.ShapeDtypeStruct((N_CHIP,) + _BLOCK_SHAPES[k], BF16) for k in kinds], {},
        [pltpu.SemaphoreType.DMA((n_arr, N_CHIP)), pltpu.SemaphoreType.DMA((n_arr, N_CHIP))], make)


def _chips_exchange(layer, partials, received, kinds):
    n_arr = len(partials)

    def make(in_refs, out_refs, sems):
        send_sems, recv_sems = sems
        x, y, c = _mesh_position()
        copies = [pltpu.make_async_remote_copy(
            src_ref=in_refs[a].at[2 * qx + qy, layer], dst_ref=out_refs[a].at[j, layer],
            send_sem=send_sems.at[a * N_OTHER_CHIPS + j], recv_sem=recv_sems.at[a * N_OTHER_CHIPS + j],
            device_id=(qx, qy, c), device_id_type=MESH)
            for a in range(n_arr) for j, (qx, qy) in enumerate(_other_chips(x, y))]

        def start():
            for cp in copies:
                cp.start()

        def finish():
            for cp in copies:
                cp.wait_recv()
            for cp in copies:
                cp.wait_send()

        return start, finish

    inputs = list(partials)
    aliases = {}
    if received is not None:
        inputs += list(received)
        aliases = {n_arr + a: a for a in range(n_arr)}
    return _Exchange(
        inputs, [jax.ShapeDtypeStruct((N_OTHER_CHIPS, DEPTH) + _BLOCK_SHAPES[k], BF16) for k in kinds], aliases,
        [pltpu.SemaphoreType.DMA((n_arr * N_OTHER_CHIPS,)), pltpu.SemaphoreType.DMA((n_arr * N_OTHER_CHIPS,))], make)


def _all_gather_exchange(v):
    def make(in_refs, out_refs, sems):
        send_sems, recv_sems, local_sem = sems
        x, y, c = _mesh_position()
        me = _block_id(x, y, c)
        own = pltpu.make_async_copy(in_refs[0], out_refs[0].at[me], local_sem.at[0])
        sends, arrivals = [], []
        for k in range(1, N_DEV):
            px, py, pc = x ^ ((k >> 2) & 1), y ^ ((k >> 1) & 1), c ^ (k & 1)
            sends.append(pltpu.make_async_remote_copy(
                src_ref=in_refs[0], dst_ref=out_refs[0].at[me], send_sem=send_sems.at[k - 1],
                recv_sem=recv_sems.at[k - 1], device_id=(px, py, pc), device_id_type=MESH))
            arrivals.append(pltpu.make_async_remote_copy(
                src_ref=in_refs[0], dst_ref=out_refs[0].at[_block_id(px, py, pc)], send_sem=send_sems.at[k - 1],
                recv_sem=recv_sems.at[k - 1], device_id=(x, y, c), device_id_type=MESH))

        def start():
            for cp in [own] + sends:
                cp.start()

        def finish():
            for cp in arrivals:
                cp.wait_recv()
            for cp in sends:
                cp.wait_send()
            own.wait()

        return start, finish

    return _Exchange(
        [v], [jax.ShapeDtypeStruct((N_DEV,) + v.shape, v.dtype)], {},
        [pltpu.SemaphoreType.DMA((N_DEV - 1,)), pltpu.SemaphoreType.DMA((N_DEV - 1,)),
         pltpu.SemaphoreType.DMA((1,))], make)


def _host(exchange, args, in_specs, out_shape, out_specs, scratch):
    n_own = (len(args), len(out_shape), len(scratch))
    if exchange is None:
        return {}, lambda refs: (refs, None)
    n_ex = (len(exchange.inputs), len(exchange.out_shapes), len(exchange.sem_shapes))
    aliases = {n_own[0] + i: n_own[1] + o for i, o in exchange.aliases.items()}
    args += exchange.inputs
    in_specs += [HBM] * n_ex[0]
    out_shape += exchange.out_shapes
    out_specs += [HBM] * n_ex[1]
    scratch += exchange.sem_shapes

    def split(refs):
        own, theirs, at = [], [], 0
        for mine, ex in zip(n_own, n_ex):
            own += refs[at:at + mine]
            theirs.append(refs[at + mine:at + mine + ex])
            at += mine + ex
        return own, exchange.make(*theirs)

    return aliases, split


def _all_gather_small(v, name, after=()):
    vmem = pl.BlockSpec(memory_space=pltpu.VMEM)

    def body(v_ref, *rest):
        out_ref, send_sems, recv_sems = rest[len(after):]
        x, y, c = _mesh_position()
        me = _block_id(x, y, c)
        out_ref[me] = v_ref[...]
        sends = []
        for k in range(1, N_DEV):
            px, py, pc = x ^ ((k >> 2) & 1), y ^ ((k >> 1) & 1), c ^ (k & 1)
            send = pltpu.make_async_remote_copy(
                src_ref=v_ref, dst_ref=out_ref.at[me], send_sem=send_sems.at[k - 1], recv_sem=recv_sems.at[k - 1],
                device_id=(px, py, pc), device_id_type=MESH)
            send.start()
            sends.append((send, _block_id(px, py, pc)))
        for k, (send, peer) in enumerate(sends):
            pltpu.make_async_remote_copy(
                src_ref=v_ref, dst_ref=out_ref.at[peer], send_sem=send_sems.at[k], recv_sem=recv_sems.at[k],
                device_id=(x, y, c), device_id_type=MESH).wait_recv()
        for send, _ in sends:
            send.wait_send()

    return pl.pallas_call(
        body, name=name, in_specs=[vmem] + [HBM] * len(after), out_specs=vmem,
        out_shape=jax.ShapeDtypeStruct((N_DEV,) + v.shape, v.dtype),
        scratch_shapes=[pltpu.SemaphoreType.DMA((N_DEV - 1,)), pltpu.SemaphoreType.DMA((N_DEV - 1,))],
        compiler_params=_params(),
    )(v, *after)


def _forward_layer(layer, x, vec, cps, wpool, win, wout):
    t_len = x.shape[0]
    n_tiles = t_len // ROW_TILE
    row = lambda cols: pl.BlockSpec((ROW_TILE, cols), lambda i: (i, 0))
    col_t = pl.BlockSpec((D_MODEL, ROW_TILE), lambda i: (0, i))
    widths = (D_MODEL, 2 * CONV_W, 3 * CONV_W, 4 * POOL_W)

    def body(x_ref, vec_ref, cps_ref, wpool_ref, win_ref, wout_ref,
             xo_ref, y_ref, ht_ref, ycatt_ref, uc_ref, fa_ref, fp_ref, zc_ref, pc_ref):
        i = pl.program_id(0)

        @pl.when(i == 0)
        def _():
            zc_ref[...] = jnp.zeros_like(zc_ref)
            pc_ref[...] = jnp.zeros_like(pc_ref)

        x_t = x_ref[...]
        shift, scale, gate = vec_ref[0:1, :], vec_ref[1:2, :], vec_ref[2:3, :]
        g_pre, g_post = vec_ref[3:4, :], vec_ref[4:5, :]
        w0, w1, w2, ps = cps_ref[0:1, :], cps_ref[1:2, :], cps_ref[2:3, :], cps_ref[3:4, :]
        rx = lax.rsqrt(jnp.mean(x_t * x_t, axis=-1, keepdims=True) + NORM_EPS)
        h = (x_t * rx) * g_pre * (1.0 + scale) + shift
        ht_ref[...] = h.T.astype(BF16)
        proj = _dot(h.astype(BF16), win_ref[...]).astype(BF16).astype(F32)
        u_a, b_a, c_a, g_a, u_p, g_p = _split_proj(proj)
        uc_ref[...] = jnp.concatenate([u_a, c_a], axis=1).astype(BF16)

        z = c_a * u_a
        zcat = jnp.concatenate([zc_ref[...], z], axis=0)
        zc_ref[...] = z[ROW_TILE - CONV_HALO:]
        conv = (w0 * _rows_from_before(zcat, 2)[CONV_HALO:] + w1 * _rows_from_before(zcat, 1)[CONV_HALO:] + w2 * z)
        sig_a = _sigmoid(g_a)
        silu_a = g_a * sig_a
        b_conv = b_a * conv
        y_a = b_conv * silu_a
        fa_ref[...] = jnp.concatenate(
            [silu_a * conv, silu_a * b_a, b_conv * (sig_a * (1.0 + g_a * (1.0 - sig_a)))], axis=1).astype(BF16)

        pcat = jnp.concatenate([pc_ref[...], u_p], axis=0)
        pc_ref[...] = u_p[ROW_TILE - POOL_HALO:]
        counts = _window_counts(i * ROW_TILE, ROW_TILE)
        pooled, mixed = [], []
        for g, w in enumerate(POOL_WINDOWS):
            cols = slice(g * GROUP_D, (g + 1) * GROUP_D)
            s = pcat[:, cols]
            step = 1
            while step < w:
                s = s + _rows_from_before(s, step)
                step *= 2
            pooled_g = (s[POOL_HALO:] * (1.0 / jnp.minimum(counts, float(w))) - u_p[:, cols]).astype(BF16)
            pooled.append(pooled_g)
            mixed.append(_dot(pooled_g, wpool_ref[g]))
        mixed = jnp.concatenate(mixed, axis=1)
        sig_p = _sigmoid(g_p)
        silu_p = g_p * sig_p
        mixed_ps = mixed * ps
        y_p = mixed_ps * silu_p
        fp_ref[...] = jnp.concatenate(
            [(ps * silu_p).astype(BF16), (mixed_ps * (sig_p * (1.0 + g_p * (1.0 - sig_p)))).astype(BF16),
             (silu_p * mixed).astype(BF16)] + pooled, axis=1)

        ycat = jnp.concatenate([y_a, y_p], axis=1)
        ycatt_ref[...] = ycat.T.astype(BF16)
        y_b = _dot(ycat.astype(BF16), wout_ref[...]).astype(BF16)
        y_ref[...] = y_b
        y_t = y_b.astype(F32)
        ry = lax.rsqrt(jnp.mean(y_t * y_t, axis=-1, keepdims=True) + NORM_EPS)
        xo_ref[...] = x_t + gate * (y_t * ry * g_post)

    return pl.pallas_call(
        body, name=f"forward_layer_{layer}", grid=(n_tiles,),
        in_specs=[row(D_MODEL), _layer_spec(vec.shape, layer), _layer_spec(cps.shape, layer),
                  _layer_spec(wpool.shape, layer), _whole_spec(win.shape), _whole_spec(wout.shape)],
        out_specs=[row(D_MODEL), row(D_MODEL), col_t, col_t] + [row(w) for w in widths[1:]],
        out_shape=[jax.ShapeDtypeStruct((t_len, D_MODEL), F32), jax.ShapeDtypeStruct((t_len, D_MODEL), BF16),
                   jax.ShapeDtypeStruct((D_MODEL, t_len), BF16), jax.ShapeDtypeStruct((D_MODEL, t_len), BF16)]
        + [jax.ShapeDtypeStruct((t_len, w), BF16) for w in widths[1:]],
        scratch_shapes=[pltpu.VMEM((CONV_HALO, CONV_W), F32), pltpu.VMEM((POOL_HALO, POOL_W), F32)],
        compiler_params=_params(dimension_semantics=("arbitrary",)),
    )(x, vec, cps, wpool, win, wout)


def _loss_head(x_final, target):
    t_len = x_final.shape[0]
    n_tiles = t_len // ROW_TILE

    def body(x_ref, t_ref, dx_ref, loss_ref):
        @pl.when(pl.program_id(0) == 0)
        def _():
            loss_ref[...] = jnp.zeros_like(loss_ref)

        err = x_ref[...] - t_ref[...]
        dx_ref[...] = err * (1.0 / D_MODEL)
        loss_ref[...] += jnp.sum(err * err) * (0.5 / D_MODEL)

    row = pl.BlockSpec((ROW_TILE, D_MODEL), lambda i: (i, 0))
    return pl.pallas_call(
        body, name="loss_head", grid=(n_tiles,), in_specs=[row, row],
        out_specs=[row, pl.BlockSpec((SUBLANES, LANES), lambda i: (0, 0))],
        out_shape=[jax.ShapeDtypeStruct((t_len, D_MODEL), F32), jax.ShapeDtypeStruct((SUBLANES, LANES), F32)],
        compiler_params=_params(dimension_semantics=("arbitrary",)),
    )(x_final, target)


def _backward_layer(layer, dxo, y, x, uc, fa, fp, vec, cps, wpool, wout, win):
    t_len = dxo.shape[0]
    n_tiles = t_len // BWD_TILE
    halo_per_tile = BWD_TILE // POOL_HALO
    rev = lambda cols: pl.BlockSpec((BWD_TILE, cols), lambda i: (n_tiles - 1 - i, 0))
    halo_spec = pl.BlockSpec(
        (POOL_HALO, 2 * CONV_W), lambda i: (jnp.maximum((n_tiles - 1 - i) * halo_per_tile - 1, 0), 0))
    gwpool_shape = (len(POOL_WINDOWS), GROUP_D, GROUP_D)

    def body(dxo_ref, y_ref, x_ref, uc_ref, uch_ref, fa_ref, fp_ref, vec_ref, cps_ref, wpool_ref, wout_ref, win_ref,
             dx_ref, dproj_ref, dy_ref, gwpool_ref, dcps_ref, dvec_ref, gwpool_acc, dcc_ref, qc_ref):
        i = pl.program_id(0)
        tile = n_tiles - 1 - i

        @pl.when(i == 0)
        def _():
            gwpool_acc[...] = jnp.zeros_like(gwpool_acc)
            dcps_ref[...] = jnp.zeros_like(dcps_ref)
            dvec_ref[...] = jnp.zeros_like(dvec_ref)
            dcc_ref[...] = jnp.zeros_like(dcc_ref)
            qc_ref[...] = jnp.zeros_like(qc_ref)

        shift, scale, gate = vec_ref[0:1, :], vec_ref[1:2, :], vec_ref[2:3, :]
        g_pre, g_post = vec_ref[3:4, :], vec_ref[4:5, :]
        w0, w1, w2 = cps_ref[0:1, :], cps_ref[1:2, :], cps_ref[2:3, :]

        dxo_t = dxo_ref[...]
        y_t = y_ref[...].astype(F32)
        ry = lax.rsqrt(jnp.mean(y_t * y_t, axis=-1, keepdims=True) + NORM_EPS)
        yh = y_t * ry
        dvec_ref[2:3, :] += jnp.sum(dxo_t * yh, axis=0, keepdims=True)
        dyh = dxo_t * (gate * g_post)
        dy_b = (ry * (dyh - yh * jnp.mean(dyh * yh, axis=-1, keepdims=True))).astype(BF16)
        dy_ref[...] = dy_b
        dycat = _dot_nt(dy_b, wout_ref[...])
        dy_a, dy_p = dycat[:, :CONV_W], dycat[:, CONV_W:]

        fa_t = fa_ref[...].astype(F32)
        db_a = dy_a * fa_t[:, :CONV_W]
        dconv = dy_a * fa_t[:, CONV_W:2 * CONV_W]
        dg_a = dy_a * fa_t[:, 2 * CONV_W:]
        uc_t = uc_ref[...].astype(F32)
        u_a, c_a = uc_t[:, :CONV_W], uc_t[:, CONV_W:]
        halo = jnp.where(tile > 0, uch_ref[...].astype(F32), 0.0)[POOL_HALO - CONV_HALO:]
        z = c_a * u_a
        zcat = jnp.concatenate([halo[:, CONV_W:] * halo[:, :CONV_W], z], axis=0)
        z1 = _rows_from_before(zcat, 1)[CONV_HALO:]
        z2 = _rows_from_before(zcat, 2)[CONV_HALO:]
        dccat = jnp.concatenate([dconv, dcc_ref[...]], axis=0)
        dc1 = _rows_from_after(dccat, 1)[:BWD_TILE]
        dc2 = _rows_from_after(dccat, 2)[:BWD_TILE]
        dz = w2 * dconv + w1 * dc1 + w0 * dc2
        dcc_ref[...] = dconv[:CONV_HALO]
        dcps_ref[0:1, :] += jnp.sum(dconv * z2, axis=0, keepdims=True)
        dcps_ref[1:2, :] += jnp.sum(dconv * z1, axis=0, keepdims=True)
        dcps_ref[2:3, :] += jnp.sum(dconv * z, axis=0, keepdims=True)
        du_a = dz * c_a
        dc_a = dz * u_a

        dmixed = (dy_p * fp_ref[:, :POOL_W].astype(F32)).astype(BF16)
        dg_p = dy_p * fp_ref[:, POOL_W:2 * POOL_W].astype(F32)
        dcps_ref[3:4, :] += jnp.sum(dy_p * fp_ref[:, 2 * POOL_W:3 * POOL_W].astype(F32), axis=0, keepdims=True)
        counts = _window_counts(tile * BWD_TILE, BWD_TILE)
        du_p, q_head = [], []
        for g, w in enumerate(POOL_WINDOWS):
            cols = slice(g * GROUP_D, (g + 1) * GROUP_D)
            dm_g = dmixed[:, cols]
            dpooled_g = _dot_nt(dm_g, wpool_ref[g])
            gwpool_acc[g] += _dot_tn(fp_ref[:, 3 * POOL_W + g * GROUP_D:3 * POOL_W + (g + 1) * GROUP_D], dm_g)
            q_g = dpooled_g * (1.0 / jnp.minimum(counts, float(w)))
            q_head.append(q_g[:POOL_HALO])
            s = jnp.concatenate([q_g, qc_ref[:, cols]], axis=0)
            step = 1
            while step < w:
                s = s + _rows_from_after(s, step)
                step *= 2
            du_p.append(s[:BWD_TILE] - dpooled_g)
        qc_ref[...] = jnp.concatenate(q_head, axis=1)
        dproj_b = jnp.concatenate([du_a, db_a, dc_a, dg_a] + du_p + [dg_p], axis=1).astype(BF16)
        dproj_ref[...] = dproj_b

        x_t = x_ref[...]
        rx = lax.rsqrt(jnp.mean(x_t * x_t, axis=-1, keepdims=True) + NORM_EPS)
        xn = x_t * rx
        mod_scale = 1.0 + scale
        dh = _dot_nt(dproj_b, win_ref[...])
        dvec_ref[0:1, :] += jnp.sum(dh, axis=0, keepdims=True)
        dvec_ref[1:2, :] += jnp.sum(dh * xn, axis=0, keepdims=True)
        dxn = dh * (g_pre * mod_scale)
        dx_ref[...] = dxo_t + rx * (dxn - xn * jnp.mean(dxn * xn, axis=-1, keepdims=True))

        @pl.when(i == n_tiles - 1)
        def _():
            gwpool_ref[...] = gwpool_acc[...].astype(BF16)
            sum_dh_xn, sum_dxo_yh = dvec_ref[1:2, :], dvec_ref[2:3, :]
            dvec_ref[1:2, :] = sum_dh_xn * g_pre
            dvec_ref[3:4, :] = sum_dh_xn * mod_scale
            dvec_ref[2:3, :] = sum_dxo_yh * g_post
            dvec_ref[4:5, :] = sum_dxo_yh * gate

    return pl.pallas_call(
        body, name=f"backward_layer_{layer}", grid=(n_tiles,),
        in_specs=[rev(D_MODEL), rev(D_MODEL), rev(D_MODEL), rev(2 * CONV_W), halo_spec, rev(3 * CONV_W),
                  rev(4 * POOL_W), _layer_spec(vec.shape, layer), _layer_spec(cps.shape, layer),
                  _layer_spec(wpool.shape, layer), _whole_spec(wout.shape), _whole_spec(win.shape)],
        out_specs=[rev(D_MODEL), rev(IN_COLS), rev(D_MODEL), _whole_spec(gwpool_shape),
                   _whole_spec((SUBLANES, CONV_W)), _whole_spec((SUBLANES, D_MODEL))],
        out_shape=[jax.ShapeDtypeStruct((t_len, D_MODEL), F32), jax.ShapeDtypeStruct((t_len, IN_COLS), BF16),
                   jax.ShapeDtypeStruct((t_len, D_MODEL), BF16), jax.ShapeDtypeStruct(gwpool_shape, BF16),
                   jax.ShapeDtypeStruct((SUBLANES, CONV_W), F32), jax.ShapeDtypeStruct((SUBLANES, D_MODEL), F32)],
        scratch_shapes=[pltpu.VMEM(gwpool_shape, F32), pltpu.VMEM((CONV_HALO, CONV_W), F32),
                        pltpu.VMEM((POOL_HALO, POOL_W), F32)],
        compiler_params=_params(dimension_semantics=("arbitrary",)),
    )(dxo, y, x, uc, uc, fa, fp, vec, cps, wpool, wout, win)


def _weight_grads(layer, h_t, dproj, ycat_t, dy, exchange):
    t_len = dy.shape[0]
    n_in, n_out = IN_COLS // GWIN_COLS, D_MODEL // GWOUT_COLS
    args = [h_t, dproj, ycat_t, dy]
    in_specs = [_whole_spec(h_t.shape),
                pl.BlockSpec((t_len, GWIN_COLS), lambda s: (0, jnp.minimum(s, n_in - 1))),
                _whole_spec(ycat_t.shape),
                pl.BlockSpec((t_len, GWOUT_COLS), lambda s: (0, jnp.maximum(s - n_in, 0)))]
    out_shape = [jax.ShapeDtypeStruct((D_MODEL, IN_COLS), BF16), jax.ShapeDtypeStruct((D_MODEL, D_MODEL), BF16)]
    out_specs = [pl.BlockSpec((D_MODEL, GWIN_COLS), lambda s: (0, jnp.minimum(s, n_in - 1))),
                 pl.BlockSpec((D_MODEL, GWOUT_COLS), lambda s: (0, jnp.maximum(s - n_in, 0)))]
    scratch = []
    aliases, split = _host(exchange, args, in_specs, out_shape, out_specs, scratch)

    def body(*refs):
        (ht_ref, dproj_ref, ycatt_ref, dy_ref, gwin_ref, gwout_ref), hosted = split(refs)
        s = pl.program_id(0)
        if hosted is not None:
            pl.when(s == 0)(hosted[0])

        @pl.when(s < n_in)
        def _():
            gwin_ref[...] = _dot(ht_ref[...], dproj_ref[...]).astype(BF16)

        @pl.when(s >= n_in)
        def _():
            gwout_ref[...] = _dot(ycatt_ref[...], dy_ref[...]).astype(BF16)

        if hosted is not None:
            pl.when(s == n_in + n_out - 1)(hosted[1])

    return pl.pallas_call(
        body, name=f"weight_grads_{layer}", grid=(n_in + n_out,), in_specs=in_specs, out_specs=out_specs,
        out_shape=out_shape, scratch_shapes=scratch, input_output_aliases=aliases,
        compiler_params=_params(dimension_semantics=("arbitrary",)),
    )(*args)


def _add_sibling_blocks(name, layer, grads, received, core, partials, kinds):
    n_arr = len(grads)

    def body(core_ref, *refs):
        mine, theirs, outs = refs[:n_arr], refs[n_arr:2 * n_arr], refs[-n_arr:]
        for a in range(n_arr):
            outs[a][...] = (mine[a][...].astype(F32) + theirs[a][...].astype(F32)).astype(BF16)

    own_of_kind = [
        pl.BlockSpec((D_MODEL, W_IN_SHARD), lambda q, core_ref: (0, 2 * q + core_ref[0])),
        pl.BlockSpec((W_OUT_SHARD, D_MODEL), lambda q, core_ref: (2 * q + core_ref[0], 0)),
        pl.BlockSpec((POOL_SHARD, GROUP_D), lambda q, core_ref: (2 * q + core_ref[0], 0)),
    ]
    shapes = [_BLOCK_SHAPES[k] for k in kinds]
    recv_specs = [pl.BlockSpec((None,) + s, lambda q, core_ref: (q, 0, 0)) for s in shapes]
    out_specs = [pl.BlockSpec((None, None) + s, lambda q, core_ref: (q, layer, 0, 0)) for s in shapes]
    args = [core, *grads, *received]
    in_specs = [own_of_kind[k] for k in kinds] + recv_specs
    aliases = {}
    if partials is not None:
        aliases = {len(args) + a: a for a in range(n_arr)}
        args += list(partials)
        in_specs += [HBM] * n_arr
    return pl.pallas_call(
        body, name=name,
        grid_spec=pltpu.PrefetchScalarGridSpec(
            num_scalar_prefetch=1, grid=(N_CHIP,), in_specs=in_specs, out_specs=out_specs),
        out_shape=[jax.ShapeDtypeStruct((N_CHIP, DEPTH) + s, BF16) for s in shapes],
        input_output_aliases=aliases,
        compiler_params=_params(dimension_semantics=("arbitrary",)),
    )(*args)


def _modulation_columns(c_all, w_ada):
    def body(c_ref, w_ref, cact_ref, out_ref):
        c_t = c_ref[...]
        c_act = c_t * _sigmoid(c_t)
        cact_ref[...] = c_act
        out_ref[...] = jnp.dot(c_act, w_ref[...], preferred_element_type=F32, precision=lax.Precision.HIGHEST)

    return pl.pallas_call(
        body, name="modulation_columns", grid=(DEPTH,),
        in_specs=[pl.BlockSpec((N_DEV, D_MODEL), lambda l: (0, 0)),
                  pl.BlockSpec((None, D_MODEL, W_IN_SHARD), lambda l: (l, 0, 0))],
        out_specs=[pl.BlockSpec((N_DEV, D_MODEL), lambda l: (0, 0)),
                   pl.BlockSpec((N_DEV, W_IN_SHARD), lambda l: (0, l))],
        out_shape=[jax.ShapeDtypeStruct((N_DEV, D_MODEL), F32),
                   jax.ShapeDtypeStruct((N_DEV, DEPTH * W_IN_SHARD), F32)],
        compiler_params=_params(dimension_semantics=("arbitrary",)),
    )(c_all, w_ada)


def _adamw(w, g, m, v):
    m_new = ADAM_B1 * m + (1.0 - ADAM_B1) * g
    v_new = ADAM_B2 * v + (1.0 - ADAM_B2) * (g * g)
    m_hat = m_new / (1.0 - ADAM_B1 ** ADAM_STEP)
    v_hat = v_new / (1.0 - ADAM_B2 ** ADAM_STEP)
    delta = -ADAM_LR * (m_hat / (jnp.sqrt(v_hat) + ADAM_EPS) + ADAM_WD * w)
    return delta, m_new, v_new


def _adamw_w_ada(w, m, v, c_act_t, dmod_cols, exchange):
    big = pl.BlockSpec((None, D_MODEL, W_IN_SHARD), lambda l: (l, 0, 0))
    args = [w, m, v, c_act_t, dmod_cols]
    in_specs = [big, big, big, pl.BlockSpec((D_MODEL, N_DEV), lambda l: (0, 0)),
                pl.BlockSpec((None, N_DEV, W_IN_SHARD), lambda l: (l, 0, 0))]
    out_shape, out_specs, scratch = [jax.ShapeDtypeStruct(w.shape, F32)] * 4, [big] * 4, []
    aliases, split = _host(exchange, args, in_specs, out_shape, out_specs, scratch)

    def body(*refs):
        (w_ref, m_ref, v_ref, ct_ref, dm_ref, g_ref, d_ref, mo_ref, vo_ref), hosted = split(refs)
        if hosted is not None:
            pl.when(pl.program_id(0) == 0)(hosted[0])
        g = ct_ref[:, 0:1] * dm_ref[0:1, :]
        for b in range(1, N_DEV):
            g = g + ct_ref[:, b:b + 1] * dm_ref[b:b + 1, :]
        g_ref[...] = g
        d_ref[...], mo_ref[...], vo_ref[...] = _adamw(w_ref[...], g, m_ref[...], v_ref[...])
        if hosted is not None:
            pl.when(pl.program_id(0) == DEPTH - 1)(hosted[1])

    return pl.pallas_call(
        body, name="adamw_w_ada", grid=(DEPTH,), in_specs=in_specs, out_specs=out_specs, out_shape=out_shape,
        scratch_shapes=scratch, input_output_aliases=aliases,
        compiler_params=_params(dimension_semantics=("arbitrary",)),
    )(*args)


def _sum_chip_partials(own_ref, recv_ref):
    g = own_ref[...].astype(F32)
    for j in range(N_OTHER_CHIPS):
        g = g + recv_ref[j].astype(F32)
    return g


def _partial_specs(row_tile, cols, first_layer=0):
    own = pl.BlockSpec((None, None, row_tile, cols), lambda l, r, chip_ref: (chip_ref[0], first_layer + l, r, 0))
    recv = pl.BlockSpec((N_OTHER_CHIPS, None, row_tile, cols), lambda l, r, chip_ref: (0, first_layer + l, r, 0))
    return own, recv


def _adamw_reduced(name, w, m, v, partial, received, chip, row_tile, layers, continued):
    depth, rows, cols = w.shape
    first, stop = layers

    def body(chip_ref, w_ref, m_ref, v_ref, own_ref, recv_ref, *rest):
        g_ref, d_ref, mo_ref, vo_ref = rest[-4:]
        g = _sum_chip_partials(own_ref, recv_ref)
        g_ref[...] = g
        d_ref[...], mo_ref[...], vo_ref[...] = _adamw(w_ref[...], g, m_ref[...], v_ref[...])

    blk = pl.BlockSpec((None, row_tile, cols), lambda l, r, chip_ref: (first + l, r, 0))
    args = [chip, w, m, v, partial, received]
    in_specs = [blk, blk, blk, *_partial_specs(row_tile, cols, first)]
    aliases = {}
    if continued is not None:
        aliases = {len(args) + k: k for k in range(4)}
        args += list(continued)
        in_specs += [HBM] * 4
    return pl.pallas_call(
        body, name=name,
        grid_spec=pltpu.PrefetchScalarGridSpec(
            num_scalar_prefetch=1, grid=(stop - first, rows // row_tile), in_specs=in_specs, out_specs=[blk] * 4),
        out_shape=[jax.ShapeDtypeStruct(w.shape, F32)] * 4, input_output_aliases=aliases,
        compiler_params=_params(dimension_semantics=("arbitrary", "arbitrary")),
    )(*args)


def _reduce_w_pool(partial, received, chip):
    def body(chip_ref, own_ref, recv_ref, g_ref):
        g_ref[...] = _sum_chip_partials(own_ref, recv_ref)

    return pl.pallas_call(
        body, name="reduce_w_pool",
        grid_spec=pltpu.PrefetchScalarGridSpec(
            num_scalar_prefetch=1, grid=(DEPTH, 1), in_specs=list(_partial_specs(POOL_SHARD, GROUP_D)),
            out_specs=pl.BlockSpec((POOL_SHARD, GROUP_D), lambda l, r, chip_ref: (l, 0))),
        out_shape=jax.ShapeDtypeStruct((DEPTH * POOL_SHARD, GROUP_D), F32),
        compiler_params=_params(dimension_semantics=("arbitrary", "arbitrary")),
    )(chip, partial, received)


def _adamw_small(params):
    n = len(params)

    def body(*refs):
        ins, outs = refs[:4 * n], refs[4 * n:]
        for p in range(n):
            w_ref, g_ref, m_ref, v_ref = ins[4 * p:4 * p + 4]
            d_ref, mo_ref, vo_ref = outs[3 * p:3 * p + 3]
            d_ref[...], mo_ref[...], vo_ref[...] = _adamw(w_ref[...], g_ref[...], m_ref[...], v_ref[...])

    vmem = pl.BlockSpec(memory_space=pltpu.VMEM)
    flat = [a for group in params for a in group]
    out_shape = [jax.ShapeDtypeStruct(group[0].shape, F32) for group in params for _ in range(3)]
    outs = pl.pallas_call(
        body, name="adamw_small", in_specs=[vmem] * len(flat), out_specs=[vmem] * len(out_shape),
        out_shape=out_shape, compiler_params=_params(),
    )(*flat)
    return [tuple(outs[3 * p:3 * p + 3]) for p in range(n)]


def _sum_sources(slabs):
    def body(s_ref, o_ref):
        acc = s_ref[0]
        for b in range(1, N_DEV):
            acc = acc + s_ref[b]
        o_ref[...] = acc

    vmem = pl.BlockSpec(memory_space=pltpu.VMEM)
    return pl.pallas_call(
        body, name="sum_small_grads", in_specs=[vmem], out_specs=vmem,
        out_shape=jax.ShapeDtypeStruct(slabs.shape[1:], F32), compiler_params=_params(),
    )(slabs)


def _to_bf16(a, name, layers=None):
    first, stop = layers or (0, a.shape[0])

    def body(a_ref, o_ref):
        o_ref[...] = a_ref[...].astype(BF16)

    block = (None,) + a.shape[1:]
    return pl.pallas_call(
        body, name=name, grid=(stop - first,), in_specs=[pl.BlockSpec(block, lambda l: (first + l, 0, 0))],
        out_specs=pl.BlockSpec(block, lambda l: (l, 0, 0)),
        out_shape=jax.ShapeDtypeStruct((stop - first,) + a.shape[1:], BF16),
        compiler_params=_params(dimension_semantics=("arbitrary",)),
    )(a)


def kernel(x, c, w_ada, b_ada, g_pre, w_in, w_conv, w_pool, pool_scale, w_out, g_post, loss_target, m_w_ada, m_b_ada, m_g_pre, m_w_in, m_w_conv, m_w_pool, m_pool_scale, m_w_out, m_g_post, v_w_ada, v_b_ada, v_g_pre, v_w_in, v_w_conv, v_w_pool, v_pool_scale, v_w_out, v_g_post):
    mx, my, mc = _mesh_position()
    me = _block_id(mx, my, mc)
    chip = (2 * mx + my).astype(jnp.int32).reshape(1)
    core = mc.astype(jnp.int32).reshape(1)
    x0 = x[0]
    target = loss_target[0]
    conv_shard = w_conv.shape[-1]

    own_small = jnp.concatenate([c, w_conv.reshape(1, DEPTH * 3 * conv_shard)], axis=1)
    all_small = _all_gather_small(own_small, "all_gather_c_w_conv")[:, 0, :]
    gathers = [_gather_weights_start(
        0, _to_bf16(w_in, "cast_w_in_0", (0, 1)), _to_bf16(w_out, "cast_w_out_0", (0, 1)), [all_small])]
    c_all = all_small[:, :D_MODEL]
    w_conv_full = all_small[:, D_MODEL:].reshape(N_DEV, DEPTH, 3, conv_shard).transpose(1, 2, 0, 3).reshape(
        DEPTH, 3, CONV_W)
    cps = jnp.concatenate([w_conv_full, pool_scale[:, None], jnp.zeros((DEPTH, 4, CONV_W), F32)], axis=1)

    c_act, pieces = _modulation_columns(c_all, w_ada)
    mod_all = _all_gather_small(pieces, "all_gather_modulation", [gathers[0][1][0]])
    mod_mine = lax.dynamic_index_in_dim(mod_all, me, axis=1, keepdims=False)
    mod = mod_mine.reshape(N_DEV, DEPTH, W_IN_SHARD).transpose(1, 0, 2).reshape(DEPTH, 3 * D_MODEL) + b_ada
    zeros_d = jnp.zeros((DEPTH, 3, D_MODEL), F32)
    vec = jnp.concatenate([mod.reshape(DEPTH, 3, D_MODEL), g_pre[:, None], g_post[:, None], zeros_d], axis=1)

    upper = (1, DEPTH)
    gathers.append(_gather_weights_start(
        1, _to_bf16(w_in, "cast_w_in_1", upper), _to_bf16(w_out, "cast_w_out_1", upper), [mod_all]))
    gathers = [(first_sems, shards, landing[k], k) for first_sems, shards, landing in gathers
               for k in range(len(landing))]
    wpool_b = _to_bf16(w_pool.reshape(DEPTH, POOL_ROWS, GROUP_D), "cast_w_pool").reshape(w_pool.shape)

    xs, kept, wins, wouts = [x0], [], [], []
    for l in range(DEPTH):
        first_sems, shards, zones, index = gathers[l]
        passed_sems, zones = _gather_weights_pass_on(
            l, index, first_sems[1], zones, [vec, cps, wpool_b, gathers[-1][1][0]] if l == 0 else [xs[-1]])
        win, wout = _gather_weights_finish(l, index, first_sems, passed_sems, shards, zones)
        x_next, *for_backward = _forward_layer(l, xs[-1], vec, cps, wpool_b, win, wout)
        xs.append(x_next)
        kept.append(for_backward)
        wins.append(win)
        wouts.append(wout)
    dx, loss_tile = _loss_head(xs[DEPTH], target)

    slab_rows = [None] * DEPTH
    partials = received = None
    in_flight = []

    def scatter(layer, grads, from_sibling, after):
        nonlocal partials, received
        partials = _add_sibling_blocks(
            f"grad_add_sibling_{layer}", layer, grads, from_sibling, core, partials, ALL_KINDS)
        chips = _chips_exchange(layer, partials, received, ALL_KINDS)
        sems, partials, received, token = _start_exchange(chips, f"grad_chips_start_{layer}", after)
        in_flight.append((chips, sems, layer))
        return token

    grads_above = None
    for l in reversed(range(DEPTH)):
        y, h_t, ycat_t, uc, fa, fp = kept[l]
        dx, dproj, dy, gwpool, dcps, dvec = _backward_layer(
            l, dx, y, xs[l], uc, fa, fp, vec, cps, wpool_b, wouts[l], wins[l])
        slab_rows[l] = jnp.concatenate(
            [dvec[0], dvec[1], dvec[2], dvec[3], dvec[4], dcps[3], dcps[0], dcps[1], dcps[2],
             loss_tile[0] if l == 0 else jnp.zeros((LANES,), F32)])
        if l == 0:
            slabs = _all_gather_small(jnp.stack(slab_rows), "all_gather_small_grads")
        hosted = _sibling_exchange(grads_above, ALL_KINDS) if grads_above is not None else None
        gwin, gwout, *from_sibling = _weight_grads(l, h_t, dproj, ycat_t, dy, hosted)
        if grads_above is not None:
            scatter(l + 1, grads_above, from_sibling, [])
        grads_above = [gwin, gwout, gwpool.reshape(POOL_ROWS, GROUP_D)]
        if l <= 1:
            from_sibling = _run_exchange(_sibling_exchange(grads_above, ALL_KINDS), f"grad_exchange_sibling_{l}")
            token = scatter(l, grads_above, from_sibling, [slabs] if l == 0 else [])
            grads_above = None
    grad_x = dx[None]
    chips_0, sems_0, _ = in_flight.pop()

    total = _sum_sources(slabs)
    loss = total[0, SLAB_COLS]
    o = 3 * D_MODEL
    g_b_ada = total[:, :o]
    g_g_pre = total[:, o:o + D_MODEL]
    g_g_post = total[:, o + D_MODEL:o + 2 * D_MODEL]
    g_pool_scale = total[:, o + 2 * D_MODEL:o + 2 * D_MODEL + POOL_W]
    g_conv_full = total[:, o + 2 * D_MODEL + POOL_W:SLAB_COLS].reshape(DEPTH, 3, CONV_W)
    g_w_conv = lax.dynamic_slice_in_dim(g_conv_full, me * conv_shard, conv_shard, axis=2)

    after = [token]
    for chips, sems, l in in_flight:
        partials, received = _finish_exchange(chips, f"grad_chips_finish_{l}", sems, partials, received, after)
        after = []
    upper = (1, DEPTH)
    w_in_upper = _adamw_reduced(
        "adamw_w_in_upper", w_in, m_w_in, v_w_in, partials[0], received[0], chip, ROW_TILE, upper, None)
    w_out_upper = _adamw_reduced(
        "adamw_w_out_upper", w_out, m_w_out, v_w_out, partials[1], received[1], chip, W_OUT_SHARD, upper, None)
    dmod_all = slabs[:, :, :o].reshape(N_DEV, DEPTH, N_DEV, W_IN_SHARD)
    dmod_cols = lax.dynamic_index_in_dim(dmod_all, me, axis=2, keepdims=False).transpose(1, 0, 2) + token[0, 0]
    g_w_ada, d_w_ada, nm_w_ada, nv_w_ada = _adamw_w_ada(w_ada, m_w_ada, v_w_ada, c_act.T, dmod_cols, None)

    partials, received = _finish_exchange(
        chips_0, "grad_chips_finish_0", sems_0, partials, received, [nv_w_ada, w_in_upper[3], w_out_upper[3]])
    gather_pool = _all_gather_exchange(_reduce_w_pool(partials[2], received[2], chip))
    sems_p, pool_rows, pool_landing, token_p = _start_exchange(gather_pool, "all_gather_grad_w_pool_start")
    g_w_in, d_w_in, nm_w_in, nv_w_in = _adamw_reduced(
        "adamw_w_in_0", w_in, m_w_in, v_w_in, partials[0], received[0], chip, ROW_TILE, (0, 1), w_in_upper)
    g_w_out, d_w_out, nm_w_out, nv_w_out = _adamw_reduced(
        "adamw_w_out_0", w_out, m_w_out, v_w_out, partials[1], received[1], chip, W_OUT_SHARD, (0, 1), w_out_upper)
    _, (g_pool_all,) = _finish_exchange(
        gather_pool, "all_gather_grad_w_pool_finish", sems_p, pool_rows, pool_landing, [nv_w_in, nv_w_out])
    g_w_pool = g_pool_all.reshape(N_DEV, DEPTH, POOL_SHARD, GROUP_D).transpose(1, 0, 2, 3).reshape(w_pool.shape)

    flat2 = lambda a: a.reshape(-1, a.shape[-1])
    small = _adamw_small([
        (b_ada, g_b_ada, m_b_ada, v_b_ada),
        (g_pre, g_g_pre, m_g_pre, v_g_pre),
        (flat2(w_conv), flat2(g_w_conv), flat2(m_w_conv), flat2(v_w_conv)),
        (flat2(w_pool), flat2(g_w_pool), flat2(m_w_pool), flat2(v_w_pool)),
        (pool_scale, g_pool_scale, m_pool_scale, v_pool_scale),
        (g_post, g_g_post, m_g_post, v_g_post),
    ])
    (d_b_ada, nm_b_ada, nv_b_ada), (d_g_pre, nm_g_pre, nv_g_pre), conv_upd, pool_upd, \
        (d_ps, nm_ps, nv_ps), (d_g_post, nm_g_post, nv_g_post) = small
    d_w_conv, nm_w_conv, nv_w_conv = (a.reshape(w_conv.shape) for a in conv_upd)
    d_w_pool, nm_w_pool, nv_w_pool = (a.reshape(w_pool.shape) for a in pool_upd)

    return (loss, grad_x,
            g_w_ada, g_b_ada, g_g_pre, g_w_in, g_w_conv, g_w_pool, g_pool_scale, g_w_out, g_g_post,
            d_w_ada, d_b_ada, d_g_pre, d_w_in, d_w_conv, d_w_pool, d_ps, d_w_out, d_g_post,
            nm_w_ada, nm_b_ada, nm_g_pre, nm_w_in, nm_w_conv, nm_w_pool, nm_ps, nm_w_out, nm_g_post,
            nv_w_ada, nv_b_ada, nv_g_pre, nv_w_in, nv_w_conv, nv_w_pool, nv_ps, nv_w_out, nv_g_post)
```

```python
import jax
import jax.numpy as jnp
from jax import lax
from jax.experimental import pallas as pl
from jax.experimental.pallas import tpu as pltpu

F32 = jnp.float32
BF16 = jnp.bfloat16

D_MODEL = 1024
DEPTH = 4
CONV_W = 512
POOL_W = 512
POOL_WINDOWS = (2, 4, 8, 16)
GROUP_D = 128
IN_COLS = 4 * CONV_W + 2 * POOL_W
NORM_EPS = 1e-6

ADAM_LR = 0.001
ADAM_B1 = 0.9
ADAM_B2 = 0.999
ADAM_EPS = 1e-08
ADAM_WD = 0.01
ADAM_STEP = 10

N_DEV = 8
N_CHIP = 4
N_OTHER_CHIPS = N_CHIP - 1
MESH = pl.DeviceIdType.MESH
W_IN_SHARD = IN_COLS // N_DEV
W_OUT_SHARD = D_MODEL // N_DEV
POOL_ROWS = len(POOL_WINDOWS) * GROUP_D
POOL_SHARD = POOL_ROWS // N_DEV

SUBLANES = 8
LANES = 128
VMEM_LIMIT_BYTES = 56 * 1024 * 1024
ROW_TILE = 512
BWD_TILE = 256
GWIN_COLS = 768
GWOUT_COLS = 512
POOL_HALO = 16
CONV_HALO = SUBLANES

SLAB_COLS = 3 * D_MODEL + D_MODEL + D_MODEL + POOL_W + 3 * CONV_W

HBM = pl.BlockSpec(memory_space=pl.ANY)


def _params(**kw):
    return pltpu.CompilerParams(vmem_limit_bytes=VMEM_LIMIT_BYTES, **kw)


def _sigmoid(v):
    return 1.0 / (1.0 + jnp.exp(-v))


def _dot(a, b):
    return jnp.dot(a, b, preferred_element_type=F32)


def _dot_tn(a, b):
    return lax.dot_general(a, b, (((0,), (0,)), ((), ())), preferred_element_type=F32)


def _dot_nt(a, b):
    return lax.dot_general(a, b, (((1,), (1,)), ((), ())), preferred_element_type=F32)


def _rows_from_before(v, k):
    return pltpu.roll(v, k, 0)


def _rows_from_after(v, k):
    return pltpu.roll(v, v.shape[0] - k, 0)


def _window_counts(t0, rows):
    return (lax.broadcasted_iota(jnp.int32, (rows, 1), 0) + (t0 + 1)).astype(F32)


def _split_proj(p32):
    cw = CONV_W
    return (p32[:, 0 * cw:1 * cw], p32[:, 1 * cw:2 * cw], p32[:, 2 * cw:3 * cw], p32[:, 3 * cw:4 * cw],
            p32[:, 4 * cw:4 * cw + POOL_W], p32[:, 4 * cw + POOL_W:])


def _layer_spec(shape, layer):
    nd = len(shape)
    return pl.BlockSpec((None,) + tuple(shape[1:]), lambda i, _l=layer, _n=nd: (_l,) + (0,) * (_n - 1))


def _whole_spec(shape):
    return pl.BlockSpec(tuple(shape), lambda i, _n=len(shape): (0,) * _n)


def _mesh_position():
    return lax.axis_index("x"), lax.axis_index("y"), lax.axis_index("c")


def _block_id(x, y, c):
    return 4 * x + 2 * y + c


def _other_chips(x, y):
    return [(x ^ 1, y), (x, y ^ 1), (x ^ 1, y ^ 1)]


def _col_block(ref, blk):
    return ref.at[:, pl.ds(pl.multiple_of(blk * W_IN_SHARD, LANES), W_IN_SHARD)]


def _row_block(rows):
    def block(ref, blk):
        return ref.at[pl.ds(pl.multiple_of(blk * rows, rows), rows), :]
    return block


_BLOCK_OF = (_col_block, _row_block(W_OUT_SHARD), _row_block(POOL_SHARD))
_BLOCK_SHAPES = ((D_MODEL, W_IN_SHARD), (W_OUT_SHARD, D_MODEL), (POOL_SHARD, GROUP_D))


class _Exchange:
    def __init__(self, inputs, out_shapes, aliases, sem_shapes, make):
        self.inputs, self.out_shapes, self.aliases, self.sem_shapes, self.make = (
            list(inputs), list(out_shapes), dict(aliases), list(sem_shapes), make)


def _run_exchange(exchange, name):
    n_in, n_out = len(exchange.inputs), len(exchange.out_shapes)

    def body(*refs):
        start, finish = exchange.make(refs[:n_in], refs[n_in:n_in + n_out], refs[n_in + n_out:])
        start()
        finish()

    return pl.pallas_call(
        body, name=name, in_specs=[HBM] * n_in, out_specs=[HBM] * n_out, out_shape=exchange.out_shapes,
        scratch_shapes=exchange.sem_shapes, input_output_aliases=exchange.aliases, compiler_params=_params(),
    )(*exchange.inputs)


_SEM = pl.BlockSpec(memory_space=pltpu.SEMAPHORE)
_DATAFLOW = pltpu.SideEffectType.DATAFLOW_SIDE_EFFECTING


def _start_exchange(exchange, name, after=()):
    n_in, n_out, n_sem = len(exchange.inputs), len(exchange.out_shapes), len(exchange.sem_shapes)
    sources = [i for i in range(n_in) if i not in exchange.aliases]
    aliases = {i: n_sem + k for k, i in enumerate(sources)}
    aliases.update({i: n_sem + len(sources) + o for i, o in exchange.aliases.items()})

    def body(*refs):
        in_refs = refs[:n_in]
        outs = refs[n_in + len(after):]
        sems = outs[:n_sem]
        out_refs = outs[n_sem + len(sources):n_sem + len(sources) + n_out]
        exchange.make(in_refs, out_refs, sems)[0]()
        refs[-1][...] = jnp.zeros_like(refs[-1])

    outs = pl.pallas_call(
        body, name=name, in_specs=[HBM] * (n_in + len(after)),
        out_specs=[_SEM] * n_sem + [HBM] * (len(sources) + n_out) + [pl.BlockSpec(memory_space=pltpu.VMEM)],
        out_shape=(exchange.sem_shapes + [pltpu.HBM(exchange.inputs[i].shape, exchange.inputs[i].dtype) for i in sources]
                   + [pltpu.HBM(s.shape, s.dtype) for s in exchange.out_shapes]
                   + [jax.ShapeDtypeStruct((SUBLANES, LANES), F32)]),
        input_output_aliases=aliases, compiler_params=_params(has_side_effects=_DATAFLOW),
    )(*exchange.inputs, *after)
    return outs[:n_sem], outs[n_sem:n_sem + len(sources)], outs[n_sem + len(sources):-1], outs[-1]


def _finish_exchange(exchange, name, sems, sources, landing, after):
    n_src, n_out, n_sem = len(sources), len(landing), len(sems)
    n_in = len(exchange.inputs)
    source_at = [i for i in range(n_in) if i not in exchange.aliases]

    def body(*refs):
        src_refs, out_refs = refs[:n_src], refs[n_src:n_src + n_out]
        sem_refs = refs[n_src + n_out:n_src + n_out + n_sem]
        in_refs = [None] * n_in
        for k, i in enumerate(source_at):
            in_refs[i] = src_refs[k]
        for i, o in exchange.aliases.items():
            in_refs[i] = out_refs[o]
        exchange.make(in_refs, out_refs, sem_refs)[1]()

    arrays = list(sources) + list(landing)
    outs = pl.pallas_call(
        body, name=name, in_specs=[HBM] * len(arrays) + [_SEM] * n_sem + [HBM] * len(after),
        out_specs=[HBM] * len(arrays), out_shape=[pltpu.HBM(a.shape, a.dtype) for a in arrays],
        input_output_aliases={i: i for i in range(len(arrays))}, compiler_params=_params(has_side_effects=_DATAFLOW),
    )(*arrays, *sems, *after)
    return outs[:n_src], outs[n_src:]


N_GATHERED = 2
FIRST_COPIES = 1 + N_OTHER_CHIPS
_GATHERED_SHAPES = ((D_MODEL, IN_COLS), (D_MODEL, D_MODEL))


def _first_sem(layer, a, k):
    return (layer * N_GATHERED + a) * FIRST_COPIES + k


def _gather_copy(window_of, full_ref, blk, send_sem, recv_sem, to, src=None):
    window = window_of(full_ref, blk)
    return pltpu.make_async_remote_copy(
        src_ref=window if src is None else src, dst_ref=window, send_sem=send_sem, recv_sem=recv_sem,
        device_id=to, device_id_type=MESH)


def _first_copies(layer, shard_refs, full_refs, send_sems, recv_sems, local_sems):
    x, y, c = _mesh_position()
    me = _block_id(x, y, c)
    own, remote = [], []
    for a in range(N_GATHERED):
        shard = shard_refs[a].at[layer]
        own.append(pltpu.make_async_copy(
            shard, _BLOCK_OF[a](full_refs[a], me), local_sems.at[layer * N_GATHERED + a]))
        targets = [(x, y, 1 - c)] + [(*chip, c) for chip in _other_chips(x, y)]
        remote += [_gather_copy(_BLOCK_OF[a], full_refs[a], me, send_sems.at[_first_sem(layer, a, k)],
                                recv_sems.at[_first_sem(layer, a, k)], to, src=shard)
                   for k, to in enumerate(targets)]
    return own, remote


def _gather_weights_start(first_layer, win_shards, wout_shards, after):
    n_layers = win_shards.shape[0]
    n_first = n_layers * N_GATHERED * FIRST_COPIES
    sem_shapes = [pltpu.SemaphoreType.DMA((n_first,)), pltpu.SemaphoreType.DMA((n_first,)),
                  pltpu.SemaphoreType.DMA((n_layers * N_GATHERED,))]
    shards = [win_shards, wout_shards]

    def body(win_sh, wout_sh, *rest):
        send_sems, recv_sems, local_sems, win_thru, wout_thru, *landing = rest[len(after):]
        for layer in range(n_layers):
            own, remote = _first_copies(layer, (win_sh, wout_sh), landing[N_GATHERED * layer:N_GATHERED * (layer + 1)],
                                        send_sems, recv_sems, local_sems)
            for cp in own + remote:
                cp.start()

    outs = pl.pallas_call(
        body, name=f"all_gather_weights_start_{first_layer}", in_specs=[HBM] * (2 + len(after)),
        out_specs=[_SEM] * 3 + [HBM] * (2 + n_layers * N_GATHERED),
        out_shape=(sem_shapes + [pltpu.HBM(s.shape, s.dtype) for s in shards]
                   + [pltpu.HBM(s, BF16) for _ in range(n_layers) for s in _GATHERED_SHAPES]),
        input_output_aliases={0: 3, 1: 4}, compiler_params=_params(has_side_effects=_DATAFLOW),
    )(*shards, *after)
    landing = outs[5:]
    return outs[:3], outs[3:5], [landing[N_GATHERED * l:N_GATHERED * (l + 1)] for l in range(n_layers)]


def _passed_on_copies(full_refs, send_sems, recv_sems, core_of_block):
    x, y, c = _mesh_position()
    return [_gather_copy(_BLOCK_OF[a], full_refs[a], _block_id(*chip, core_of_block),
                         send_sems.at[a * N_OTHER_CHIPS + j], recv_sems.at[a * N_OTHER_CHIPS + j], (x, y, 1 - c))
            for a in range(N_GATHERED) for j, chip in enumerate(_other_chips(x, y))]


def _gather_weights_pass_on(layer, index, first_recv_sems, landing, after):
    n = N_GATHERED * N_OTHER_CHIPS

    def body(win_ref, wout_ref, first_recv, *rest):
        send_sems, recv_sems = rest[len(after):len(after) + 2]
        x, y, c = _mesh_position()
        full_refs = (win_ref, wout_ref)
        passed = _passed_on_copies(full_refs, send_sems, recv_sems, c)
        for a in range(N_GATHERED):
            for j, chip in enumerate(_other_chips(x, y)):
                sem = _first_sem(index, a, 1 + j)
                _gather_copy(_BLOCK_OF[a], full_refs[a], _block_id(*chip, c), first_recv.at[sem], first_recv.at[sem],
                             (x, y, c)).wait_recv()
                passed[a * N_OTHER_CHIPS + j].start()

    outs = pl.pallas_call(
        body, name=f"all_gather_weights_pass_on_{layer}", in_specs=[HBM] * N_GATHERED + [_SEM] + [HBM] * len(after),
        out_specs=[_SEM] * 2 + [HBM] * N_GATHERED,
        out_shape=[pltpu.SemaphoreType.DMA((n,)), pltpu.SemaphoreType.DMA((n,))]
        + [pltpu.HBM(a.shape, a.dtype) for a in landing],
        input_output_aliases={a: 2 + a for a in range(N_GATHERED)},
        compiler_params=_params(has_side_effects=_DATAFLOW),
    )(*landing, first_recv_sems, *after)
    return outs[:2], outs[2:]


def _gather_weights_finish(layer, index, first_sems, passed_sems, shards, landing):
    def body(win_ref, wout_ref, first_send, first_recv, local_sems, passed_send, passed_recv, win_sh, wout_sh, *thru):
        x, y, c = _mesh_position()
        full_refs = (win_ref, wout_ref)
        own, remote = _first_copies(index, (win_sh, wout_sh), full_refs, first_send, first_recv, local_sems)
        for a in range(N_GATHERED):
            sem = _first_sem(index, a, 0)
            _gather_copy(_BLOCK_OF[a], full_refs[a], _block_id(x, y, 1 - c), first_recv.at[sem], first_recv.at[sem],
                         (x, y, c)).wait_recv()
        for cp in _passed_on_copies(full_refs, passed_send, passed_recv, 1 - c):
            cp.wait_recv()
        for cp in remote + _passed_on_copies(full_refs, passed_send, passed_recv, c):
            cp.wait_send()
        for cp in own:
            cp.wait()

    return pl.pallas_call(
        body, name=f"all_gather_weights_finish_{layer}", in_specs=[HBM] * N_GATHERED + [_SEM] * 5 + [HBM] * 2,
        out_specs=[HBM] * N_GATHERED, out_shape=[pltpu.HBM(a.shape, a.dtype) for a in landing],
        input_output_aliases={a: a for a in range(N_GATHERED)}, compiler_params=_params(has_side_effects=_DATAFLOW),
    )(*landing, *first_sems, *passed_sems, *shards)


ALL_KINDS = (0, 1, 2)


def _sibling_exchange(grads, kinds):
    n_arr = len(grads)

    def make(in_refs, out_refs, sems):
        send_sems, recv_sems = sems
        x, y, c = _mesh_position()
        copies = [pltpu.make_async_remote_copy(
            src_ref=_BLOCK_OF[kinds[a]](in_refs[a], 2 * q + (1 - c)), dst_ref=out_refs[a].at[q],
            send_sem=send_sems.at[a, q], recv_sem=recv_sems.at[a, q], device_id=(x, y, 1 - c), device_id_type=MESH)
            for a in range(n_arr) for q in range(N_CHIP)]

        def start():
            for cp in copies:
                cp.start()

        def finish():
            for cp in copies:
                cp.wait_recv()
            for cp in copies:
                cp.wait_send()

        return start, finish

    return _Exchange(
        grads, [jax.ShapeDtypeStruct((N_CHIP,) + _BLOCK_SHAPES[k], BF16) for k in kinds], {},
        [pltpu.SemaphoreType.DMA((n_arr, N_CHIP)), pltpu.SemaphoreType.DMA((n_arr, N_CHIP))], make)


def _chips_exchange(layer, partials, received, kinds):
    n_arr = len(partials)

    def make(in_refs, out_refs, sems):
        send_sems, recv_sems = sems
        x, y, c = _mesh_position()
        copies = [pltpu.make_async_remote_copy(
            src_ref=in_refs[a].at[2 * qx + qy, layer], dst_ref=out_refs[a].at[j, layer],
            send_sem=send_sems.at[a * N_OTHER_CHIPS + j], recv_sem=recv_sems.at[a * N_OTHER_CHIPS + j],
            device_id=(qx, qy, c), device_id_type=MESH)
            for a in range(n_arr) for j, (qx, qy) in enumerate(_other_chips(x, y))]

        def start():
            for cp in copies:
                cp.start()

        def finish():
            for cp in copies:
                cp.wait_recv()
            for cp in copies:
                cp.wait_send()

        return start, finish

    inputs = list(partials)
    aliases = {}
    if received is not None:
        inputs += list(received)
        aliases = {n_arr + a: a for a in range(n_arr)}
    return _Exchange(
        inputs, [jax.ShapeDtypeStruct((N_OTHER_CHIPS, DEPTH) + _BLOCK_SHAPES[k], BF16) for k in kinds], aliases,
        [pltpu.SemaphoreType.DMA((n_arr * N_OTHER_CHIPS,)), pltpu.SemaphoreType.DMA((n_arr * N_OTHER_CHIPS,))], make)


def _all_gather_exchange(v):
    def make(in_refs, out_refs, sems):
        send_sems, recv_sems, local_sem = sems
        x, y, c = _mesh_position()
        me = _block_id(x, y, c)
        own = pltpu.make_async_copy(in_refs[0], out_refs[0].at[me], local_sem.at[0])
        sends, arrivals = [], []
        for k in range(1, N_DEV):
            px, py, pc = x ^ ((k >> 2) & 1), y ^ ((k >> 1) & 1), c ^ (k & 1)
            sends.append(pltpu.make_async_remote_copy(
                src_ref=in_refs[0], dst_ref=out_refs[0].at[me], send_sem=send_sems.at[k - 1],
                recv_sem=recv_sems.at[k - 1], device_id=(px, py, pc), device_id_type=MESH))
            arrivals.append(pltpu.make_async_remote_copy(
                src_ref=in_refs[0], dst_ref=out_refs[0].at[_block_id(px, py, pc)], send_sem=send_sems.at[k - 1],
                recv_sem=recv_sems.at[k - 1], device_id=(x, y, c), device_id_type=MESH))

        def start():
            for cp in [own] + sends:
                cp.start()

        def finish():
            for cp in arrivals:
                cp.wait_recv()
            for cp in sends:
                cp.wait_send()
            own.wait()

        return start, finish

    return _Exchange(
        [v], [jax.ShapeDtypeStruct((N_DEV,) + v.shape, v.dtype)], {},
        [pltpu.SemaphoreType.DMA((N_DEV - 1,)), pltpu.SemaphoreType.DMA((N_DEV - 1,)),
         pltpu.SemaphoreType.DMA((1,))], make)


def _host(exchange, args, in_specs, out_shape, out_specs, scratch):
    n_own = (len(args), len(out_shape), len(scratch))
    if exchange is None:
        return {}, lambda refs: (refs, None)
    n_ex = (len(exchange.inputs), len(exchange.out_shapes), len(exchange.sem_shapes))
    aliases = {n_own[0] + i: n_own[1] + o for i, o in exchange.aliases.items()}
    args += exchange.inputs
    in_specs += [HBM] * n_ex[0]
    out_shape += exchange.out_shapes
    out_specs += [HBM] * n_ex[1]
    scratch += exchange.sem_shapes

    def split(refs):
        own, theirs, at = [], [], 0
        for mine, ex in zip(n_own, n_ex):
            own += refs[at:at + mine]
            theirs.append(refs[at + mine:at + mine + ex])
            at += mine + ex
        return own, exchange.make(*theirs)

    return aliases, split


def _all_gather_small(v, name, after=()):
    vmem = pl.BlockSpec(memory_space=pltpu.VMEM)

    def body(v_ref, *rest):
        out_ref, send_sems, recv_sems = rest[len(after):]
        x, y, c = _mesh_position()
        me = _block_id(x, y, c)
        out_ref[me] = v_ref[...]
        sends = []
        for k in range(1, N_DEV):
            px, py, pc = x ^ ((k >> 2) & 1), y ^ ((k >> 1) & 1), c ^ (k & 1)
            send = pltpu.make_async_remote_copy(
                src_ref=v_ref, dst_ref=out_ref.at[me], send_sem=send_sems.at[k - 1], recv_sem=recv_sems.at[k - 1],
                device_id=(px, py, pc), device_id_type=MESH)
            send.start()
            sends.append((send, _block_id(px, py, pc)))
        for k, (send, peer) in enumerate(sends):
            pltpu.make_async_remote_copy(
                src_ref=v_ref, dst_ref=out_ref.at[peer], send_sem=send_sems.at[k], recv_sem=recv_sems.at[k],
                device_id=(x, y, c), device_id_type=MESH).wait_recv()
        for send, _ in sends:
            send.wait_send()

    return pl.pallas_call(
        body, name=name, in_specs=[vmem] + [HBM] * len(after), out_specs=vmem,
        out_shape=jax.ShapeDtypeStruct((N_DEV,) + v.shape, v.dtype),
        scratch_shapes=[pltpu.SemaphoreType.DMA((N_DEV - 1,)), pltpu.SemaphoreType.DMA((N_DEV - 1,))],
        compiler_params=_params(),
    )(v, *after)


def _forward_layer(layer, x, vec, cps, wpool, win, wout):
    t_len = x.shape[0]
    n_tiles = t_len // ROW_TILE
    row = lambda cols: pl.BlockSpec((ROW_TILE, cols), lambda i: (i, 0))
    col_t = pl.BlockSpec((D_MODEL, ROW_TILE), lambda i: (0, i))
    widths = (D_MODEL, 2 * CONV_W, 3 * CONV_W, 4 * POOL_W)

    def body(x_ref, vec_ref, cps_ref, wpool_ref, win_ref, wout_ref,
             xo_ref, y_ref, ht_ref, ycatt_ref, uc_ref, fa_ref, fp_ref, zc_ref, pc_ref):
        i = pl.program_id(0)

        @pl.when(i == 0)
        def _():
            zc_ref[...] = jnp.zeros_like(zc_ref)
            pc_ref[...] = jnp.zeros_like(pc_ref)

        x_t = x_ref[...]
        shift, scale, gate = vec_ref[0:1, :], vec_ref[1:2, :], vec_ref[2:3, :]
        g_pre, g_post = vec_ref[3:4, :], vec_ref[4:5, :]
        w0, w1, w2, ps = cps_ref[0:1, :], cps_ref[1:2, :], cps_ref[2:3, :], cps_ref[3:4, :]
        rx = lax.rsqrt(jnp.mean(x_t * x_t, axis=-1, keepdims=True) + NORM_EPS)
        h = (x_t * rx) * g_pre * (1.0 + scale) + shift
        ht_ref[...] = h.T.astype(BF16)
        proj = _dot(h.astype(BF16), win_ref[...]).astype(BF16).astype(F32)
        u_a, b_a, c_a, g_a, u_p, g_p = _split_proj(proj)
        uc_ref[...] = jnp.concatenate([u_a, c_a], axis=1).astype(BF16)

        z = c_a * u_a
        zcat = jnp.concatenate([zc_ref[...], z], axis=0)
        zc_ref[...] = z[ROW_TILE - CONV_HALO:]
        conv = (w0 * _rows_from_before(zcat, 2)[CONV_HALO:] + w1 * _rows_from_before(zcat, 1)[CONV_HALO:] + w2 * z)
        sig_a = _sigmoid(g_a)
        silu_a = g_a * sig_a
        b_conv = b_a * conv
        y_a = b_conv * silu_a
        fa_ref[...] = jnp.concatenate(
            [silu_a * conv, silu_a * b_a, b_conv * (sig_a * (1.0 + g_a * (1.0 - sig_a)))], axis=1).astype(BF16)

        pcat = jnp.concatenate([pc_ref[...], u_p], axis=0)
        pc_ref[...] = u_p[ROW_TILE - POOL_HALO:]
        counts = _window_counts(i * ROW_TILE, ROW_TILE)
        pooled, mixed = [], []
        for g, w in enumerate(POOL_WINDOWS):
            cols = slice(g * GROUP_D, (g + 1) * GROUP_D)
            s = pcat[:, cols]
            step = 1
            while step < w:
                s = s + _rows_from_before(s, step)
                step *= 2
            pooled_g = (s[POOL_HALO:] * (1.0 / jnp.minimum(counts, float(w))) - u_p[:, cols]).astype(BF16)
            pooled.append(pooled_g)
            mixed.append(_dot(pooled_g, wpool_ref[g]))
        mixed = jnp.concatenate(mixed, axis=1)
        sig_p = _sigmoid(g_p)
        silu_p = g_p * sig_p
        mixed_ps = mixed * ps
        y_p = mixed_ps * silu_p
        fp_ref[...] = jnp.concatenate(
            [(ps * silu_p).astype(BF16), (mixed_ps * (sig_p * (1.0 + g_p * (1.0 - sig_p)))).astype(BF16),
             (silu_p * mixed).astype(BF16)] + pooled, axis=1)

        ycat = jnp.concatenate([y_a, y_p], axis=1)
        ycatt_ref[...] = ycat.T.astype(BF16)
        y_b = _dot(ycat.astype(BF16), wout_ref[...]).astype(BF16)
        y_ref[...] = y_b
        y_t = y_b.astype(F32)
        ry = lax.rsqrt(jnp.mean(y_t * y_t, axis=-1, keepdims=True) + NORM_EPS)
        xo_ref[...] = x_t + gate * (y_t * ry * g_post)

    return pl.pallas_call(
        body, name=f"forward_layer_{layer}", grid=(n_tiles,),
        in_specs=[row(D_MODEL), _layer_spec(vec.shape, layer), _layer_spec(cps.shape, layer),
                  _layer_spec(wpool.shape, layer), _whole_spec(win.shape), _whole_spec(wout.shape)],
        out_specs=[row(D_MODEL), row(D_MODEL), col_t, col_t] + [row(w) for w in widths[1:]],
        out_shape=[jax.ShapeDtypeStruct((t_len, D_MODEL), F32), jax.ShapeDtypeStruct((t_len, D_MODEL), BF16),
                   jax.ShapeDtypeStruct((D_MODEL, t_len), BF16), jax.ShapeDtypeStruct((D_MODEL, t_len), BF16)]
        + [jax.ShapeDtypeStruct((t_len, w), BF16) for w in widths[1:]],
        scratch_shapes=[pltpu.VMEM((CONV_HALO, CONV_W), F32), pltpu.VMEM((POOL_HALO, POOL_W), F32)],
        compiler_params=_params(dimension_semantics=("arbitrary",)),
    )(x, vec, cps, wpool, win, wout)


def _loss_head(x_final, target):
    t_len = x_final.shape[0]
    n_tiles = t_len // ROW_TILE

    def body(x_ref, t_ref, dx_ref, loss_ref):
        @pl.when(pl.program_id(0) == 0)
        def _():
            loss_ref[...] = jnp.zeros_like(loss_ref)

        err = x_ref[...] - t_ref[...]
        dx_ref[...] = err * (1.0 / D_MODEL)
        loss_ref[...] += jnp.sum(err * err) * (0.5 / D_MODEL)

    row = pl.BlockSpec((ROW_TILE, D_MODEL), lambda i: (i, 0))
    return pl.pallas_call(
        body, name="loss_head", grid=(n_tiles,), in_specs=[row, row],
        out_specs=[row, pl.BlockSpec((SUBLANES, LANES), lambda i: (0, 0))],
        out_shape=[jax.ShapeDtypeStruct((t_len, D_MODEL), F32), jax.ShapeDtypeStruct((SUBLANES, LANES), F32)],
        compiler_params=_params(dimension_semantics=("arbitrary",)),
    )(x_final, target)


def _backward_layer(layer, dxo, y, x, uc, fa, fp, vec, cps, wpool, wout, win):
    t_len = dxo.shape[0]
    n_tiles = t_len // BWD_TILE
    halo_per_tile = BWD_TILE // POOL_HALO
    rev = lambda cols: pl.BlockSpec((BWD_TILE, cols), lambda i: (n_tiles - 1 - i, 0))
    halo_spec = pl.BlockSpec(
        (POOL_HALO, 2 * CONV_W), lambda i: (jnp.maximum((n_tiles - 1 - i) * halo_per_tile - 1, 0), 0))
    gwpool_shape = (len(POOL_WINDOWS), GROUP_D, GROUP_D)

    def body(dxo_ref, y_ref, x_ref, uc_ref, uch_ref, fa_ref, fp_ref, vec_ref, cps_ref, wpool_ref, wout_ref, win_ref,
             dx_ref, dproj_ref, dy_ref, gwpool_ref, dcps_ref, dvec_ref, gwpool_acc, dcc_ref, qc_ref):
        i = pl.program_id(0)
        tile = n_tiles - 1 - i

        @pl.when(i == 0)
        def _():
            gwpool_acc[...] = jnp.zeros_like(gwpool_acc)
            dcps_ref[...] = jnp.zeros_like(dcps_ref)
            dvec_ref[...] = jnp.zeros_like(dvec_ref)
            dcc_ref[...] = jnp.zeros_like(dcc_ref)
            qc_ref[...] = jnp.zeros_like(qc_ref)

        shift, scale, gate = vec_ref[0:1, :], vec_ref[1:2, :], vec_ref[2:3, :]
        g_pre, g_post = vec_ref[3:4, :], vec_ref[4:5, :]
        w0, w1, w2 = cps_ref[0:1, :], cps_ref[1:2, :], cps_ref[2:3, :]

        dxo_t = dxo_ref[...]
        y_t = y_ref[...].astype(F32)
        ry = lax.rsqrt(jnp.mean(y_t * y_t, axis=-1, keepdims=True) + NORM_EPS)
        yh = y_t * ry
        dvec_ref[2:3, :] += jnp.sum(dxo_t * yh, axis=0, keepdims=True)
        dyh = dxo_t * (gate * g_post)
        dy_b = (ry * (dyh - yh * jnp.mean(dyh * yh, axis=-1, keepdims=True))).astype(BF16)
        dy_ref[...] = dy_b
        dycat = _dot_nt(dy_b, wout_ref[...])
        dy_a, dy_p = dycat[:, :CONV_W], dycat[:, CONV_W:]

        fa_t = fa_ref[...].astype(F32)
        db_a = dy_a * fa_t[:, :CONV_W]
        dconv = dy_a * fa_t[:, CONV_W:2 * CONV_W]
        dg_a = dy_a * fa_t[:, 2 * CONV_W:]
        uc_t = uc_ref[...].astype(F32)
        u_a, c_a = uc_t[:, :CONV_W], uc_t[:, CONV_W:]
        halo = jnp.where(tile > 0, uch_ref[...].astype(F32), 0.0)[POOL_HALO - CONV_HALO:]
        z = c_a * u_a
        zcat = jnp.concatenate([halo[:, CONV_W:] * halo[:, :CONV_W], z], axis=0)
        z1 = _rows_from_before(zcat, 1)[CONV_HALO:]
        z2 = _rows_from_before(zcat, 2)[CONV_HALO:]
        dccat = jnp.concatenate([dconv, dcc_ref[...]], axis=0)
        dc1 = _rows_from_after(dccat, 1)[:BWD_TILE]
        dc2 = _rows_from_after(dccat, 2)[:BWD_TILE]
        dz = w2 * dconv + w1 * dc1 + w0 * dc2
        dcc_ref[...] = dconv[:CONV_HALO]
        dcps_ref[0:1, :] += jnp.sum(dconv * z2, axis=0, keepdims=True)
        dcps_ref[1:2, :] += jnp.sum(dconv * z1, axis=0, keepdims=True)
        dcps_ref[2:3, :] += jnp.sum(dconv * z, axis=0, keepdims=True)
        du_a = dz * c_a
        dc_a = dz * u_a

        dmixed = (dy_p * fp_ref[:, :POOL_W].astype(F32)).astype(BF16)
        dg_p = dy_p * fp_ref[:, POOL_W:2 * POOL_W].astype(F32)
        dcps_ref[3:4, :] += jnp.sum(dy_p * fp_ref[:, 2 * POOL_W:3 * POOL_W].astype(F32), axis=0, keepdims=True)
        counts = _window_counts(tile * BWD_TILE, BWD_TILE)
        du_p, q_head = [], []
        for g, w in enumerate(POOL_WINDOWS):
            cols = slice(g * GROUP_D, (g + 1) * GROUP_D)
            dm_g = dmixed[:, cols]
            dpooled_g = _dot_nt(dm_g, wpool_ref[g])
            gwpool_acc[g] += _dot_tn(fp_ref[:, 3 * POOL_W + g * GROUP_D:3 * POOL_W + (g + 1) * GROUP_D], dm_g)
            q_g = dpooled_g * (1.0 / jnp.minimum(counts, float(w)))
            q_head.append(q_g[:POOL_HALO])
            s = jnp.concatenate([q_g, qc_ref[:, cols]], axis=0)
            step = 1
            while step < w:
                s = s + _rows_from_after(s, step)
                step *= 2
            du_p.append(s[:BWD_TILE] - dpooled_g)
        qc_ref[...] = jnp.concatenate(q_head, axis=1)
        dproj_b = jnp.concatenate([du_a, db_a, dc_a, dg_a] + du_p + [dg_p], axis=1).astype(BF16)
        dproj_ref[...] = dproj_b

        x_t = x_ref[...]
        rx = lax.rsqrt(jnp.mean(x_t * x_t, axis=-1, keepdims=True) + NORM_EPS)
        xn = x_t * rx
        mod_scale = 1.0 + scale
        dh = _dot_nt(dproj_b, win_ref[...])
        dvec_ref[0:1, :] += jnp.sum(dh, axis=0, keepdims=True)
        dvec_ref[1:2, :] += jnp.sum(dh * xn, axis=0, keepdims=True)
        dxn = dh * (g_pre * mod_scale)
        dx_ref[...] = dxo_t + rx * (dxn - xn * jnp.mean(dxn * xn, axis=-1, keepdims=True))

        @pl.when(i == n_tiles - 1)
        def _():
            gwpool_ref[...] = gwpool_acc[...].astype(BF16)
            sum_dh_xn, sum_dxo_yh = dvec_ref[1:2, :], dvec_ref[2:3, :]
            dvec_ref[1:2, :] = sum_dh_xn * g_pre
            dvec_ref[3:4, :] = sum_dh_xn * mod_scale
            dvec_ref[2:3, :] = sum_dxo_yh * g_post
            dvec_ref[4:5, :] = sum_dxo_yh * gate

    return pl.pallas_call(
        body, name=f"backward_layer_{layer}", grid=(n_tiles,),
        in_specs=[rev(D_MODEL), rev(D_MODEL), rev(D_MODEL), rev(2 * CONV_W), halo_spec, rev(3 * CONV_W),
                  rev(4 * POOL_W), _layer_spec(vec.shape, layer), _layer_spec(cps.shape, layer),
                  _layer_spec(wpool.shape, layer), _whole_spec(wout.shape), _whole_spec(win.shape)],
        out_specs=[rev(D_MODEL), rev(IN_COLS), rev(D_MODEL), _whole_spec(gwpool_shape),
                   _whole_spec((SUBLANES, CONV_W)), _whole_spec((SUBLANES, D_MODEL))],
        out_shape=[jax.ShapeDtypeStruct((t_len, D_MODEL), F32), jax.ShapeDtypeStruct((t_len, IN_COLS), BF16),
                   jax.ShapeDtypeStruct((t_len, D_MODEL), BF16), jax.ShapeDtypeStruct(gwpool_shape, BF16),
                   jax.ShapeDtypeStruct((SUBLANES, CONV_W), F32), jax.ShapeDtypeStruct((SUBLANES, D_MODEL), F32)],
        scratch_shapes=[pltpu.VMEM(gwpool_shape, F32), pltpu.VMEM((CONV_HALO, CONV_W), F32),
                        pltpu.VMEM((POOL_HALO, POOL_W), F32)],
        compiler_params=_params(dimension_semantics=("arbitrary",)),
    )(dxo, y, x, uc, uc, fa, fp, vec, cps, wpool, wout, win)


def _weight_grads(layer, h_t, dproj, ycat_t, dy, exchange):
    t_len = dy.shape[0]
    n_in, n_out = IN_COLS // GWIN_COLS, D_MODEL // GWOUT_COLS
    args = [h_t, dproj, ycat_t, dy]
    in_specs = [_whole_spec(h_t.shape),
                pl.BlockSpec((t_len, GWIN_COLS), lambda s: (0, jnp.minimum(s, n_in - 1))),
                _whole_spec(ycat_t.shape),
                pl.BlockSpec((t_len, GWOUT_COLS), lambda s: (0, jnp.maximum(s - n_in, 0)))]
    out_shape = [jax.ShapeDtypeStruct((D_MODEL, IN_COLS), BF16), jax.ShapeDtypeStruct((D_MODEL, D_MODEL), BF16)]
    out_specs = [pl.BlockSpec((D_MODEL, GWIN_COLS), lambda s: (0, jnp.minimum(s, n_in - 1))),
                 pl.BlockSpec((D_MODEL, GWOUT_COLS), lambda s: (0, jnp.maximum(s - n_in, 0)))]
    scratch = []
    aliases, split = _host(exchange, args, in_specs, out_shape, out_specs, scratch)

    def body(*refs):
        (ht_ref, dproj_ref, ycatt_ref, dy_ref, gwin_ref, gwout_ref), hosted = split(refs)
        s = pl.program_id(0)
        if hosted is not None:
            pl.when(s == 0)(hosted[0])

        @pl.when(s < n_in)
        def _():
            gwin_ref[...] = _dot(ht_ref[...], dproj_ref[...]).astype(BF16)

        @pl.when(s >= n_in)
        def _():
            gwout_ref[...] = _dot(ycatt_ref[...], dy_ref[...]).astype(BF16)

        if hosted is not None:
            pl.when(s == n_in + n_out - 1)(hosted[1])

    return pl.pallas_call(
        body, name=f"weight_grads_{layer}", grid=(n_in + n_out,), in_specs=in_specs, out_specs=out_specs,
        out_shape=out_shape, scratch_shapes=scratch, input_output_aliases=aliases,
        compiler_params=_params(dimension_semantics=("arbitrary",)),
    )(*args)


def _add_sibling_blocks(name, layer, grads, received, core, partials, kinds):
    n_arr = len(grads)

    def body(core_ref, *refs):
        mine, theirs, outs = refs[:n_arr], refs[n_arr:2 * n_arr], refs[-n_arr:]
        for a in range(n_arr):
            outs[a][...] = (mine[a][...].astype(F32) + theirs[a][...].astype(F32)).astype(BF16)

    own_of_kind = [
        pl.BlockSpec((D_MODEL, W_IN_SHARD), lambda q, core_ref: (0, 2 * q + core_ref[0])),
        pl.BlockSpec((W_OUT_SHARD, D_MODEL), lambda q, core_ref: (2 * q + core_ref[0], 0)),
        pl.BlockSpec((POOL_SHARD, GROUP_D), lambda q, core_ref: (2 * q + core_ref[0], 0)),
    ]
    shapes = [_BLOCK_SHAPES[k] for k in kinds]
    recv_specs = [pl.BlockSpec((None,) + s, lambda q, core_ref: (q, 0, 0)) for s in shapes]
    out_specs = [pl.BlockSpec((None, None) + s, lambda q, core_ref: (q, layer, 0, 0)) for s in shapes]
    args = [core, *grads, *received]
    in_specs = [own_of_kind[k] for k in kinds] + recv_specs
    aliases = {}
    if partials is not None:
        aliases = {len(args) + a: a for a in range(n_arr)}
        args += list(partials)
        in_specs += [HBM] * n_arr
    return pl.pallas_call(
        body, name=name,
        grid_spec=pltpu.PrefetchScalarGridSpec(
            num_scalar_prefetch=1, grid=(N_CHIP,), in_specs=in_specs, out_specs=out_specs),
        out_shape=[jax.ShapeDtypeStruct((N_CHIP, DEPTH) + s, BF16) for s in shapes],
        input_output_aliases=aliases,
        compiler_params=_params(dimension_semantics=("arbitrary",)),
    )(*args)


def _modulation_columns(c_all, w_ada):
    def body(c_ref, w_ref, cact_ref, out_ref):
        c_t = c_ref[...]
        c_act = c_t * _sigmoid(c_t)
        cact_ref[...] = c_act
        out_ref[...] = jnp.dot(c_act, w_ref[...], preferred_element_type=F32, precision=lax.Precision.HIGHEST)

    return pl.pallas_call(
        body, name="modulation_columns", grid=(DEPTH,),
        in_specs=[pl.BlockSpec((N_DEV, D_MODEL), lambda l: (0, 0)),
                  pl.BlockSpec((None, D_MODEL, W_IN_SHARD), lambda l: (l, 0, 0))],
        out_specs=[pl.BlockSpec((N_DEV, D_MODEL), lambda l: (0, 0)),
                   pl.BlockSpec((N_DEV, W_IN_SHARD), lambda l: (0, l))],
        out_shape=[jax.ShapeDtypeStruct((N_DEV, D_MODEL), F32),
                   jax.ShapeDtypeStruct((N_DEV, DEPTH * W_IN_SHARD), F32)],
        compiler_params=_params(dimension_semantics=("arbitrary",)),
    )(c_all, w_ada)


def _adamw(w, g, m, v):
    m_new = ADAM_B1 * m + (1.0 - ADAM_B1) * g
    v_new = ADAM_B2 * v + (1.0 - ADAM_B2) * (g * g)
    m_hat = m_new / (1.0 - ADAM_B1 ** ADAM_STEP)
    v_hat = v_new / (1.0 - ADAM_B2 ** ADAM_STEP)
    delta = -ADAM_LR * (m_hat / (jnp.sqrt(v_hat) + ADAM_EPS) + ADAM_WD * w)
    return delta, m_new, v_new


def _adamw_w_ada(w, m, v, c_act_t, dmod_cols, exchange):
    big = pl.BlockSpec((None, D_MODEL, W_IN_SHARD), lambda l: (l, 0, 0))
    args = [w, m, v, c_act_t, dmod_cols]
    in_specs = [big, big, big, pl.BlockSpec((D_MODEL, N_DEV), lambda l: (0, 0)),
                pl.BlockSpec((None, N_DEV, W_IN_SHARD), lambda l: (l, 0, 0))]
    out_shape, out_specs, scratch = [jax.ShapeDtypeStruct(w.shape, F32)] * 4, [big] * 4, []
    aliases, split = _host(exchange, args, in_specs, out_shape, out_specs, scratch)

    def body(*refs):
        (w_ref, m_ref, v_ref, ct_ref, dm_ref, g_ref, d_ref, mo_ref, vo_ref), hosted = split(refs)
        if hosted is not None:
            pl.when(pl.program_id(0) == 0)(hosted[0])
        g = ct_ref[:, 0:1] * dm_ref[0:1, :]
        for b in range(1, N_DEV):
            g = g + ct_ref[:, b:b + 1] * dm_ref[b:b + 1, :]
        g_ref[...] = g
        d_ref[...], mo_ref[...], vo_ref[...] = _adamw(w_ref[...], g, m_ref[...], v_ref[...])
        if hosted is not None:
            pl.when(pl.program_id(0) == DEPTH - 1)(hosted[1])

    return pl.pallas_call(
        body, name="adamw_w_ada", grid=(DEPTH,), in_specs=in_specs, out_specs=out_specs, out_shape=out_shape,
        scratch_shapes=scratch, input_output_aliases=aliases,
        compiler_params=_params(dimension_semantics=("arbitrary",)),
    )(*args)


def _sum_chip_partials(own_ref, recv_ref):
    g = own_ref[...].astype(F32)
    for j in range(N_OTHER_CHIPS):
        g = g + recv_ref[j].astype(F32)
    return g


def _partial_specs(row_tile, cols, first_layer=0):
    own = pl.BlockSpec((None, None, row_tile, cols), lambda l, r, chip_ref: (chip_ref[0], first_layer + l, r, 0))
    recv = pl.BlockSpec((N_OTHER_CHIPS, None, row_tile, cols), lambda l, r, chip_ref: (0, first_layer + l, r, 0))
    return own, recv


def _adamw_reduced(name, w, m, v, partial, received, chip, row_tile, layers, continued):
    depth, rows, cols = w.shape
    first, stop = layers

    def body(chip_ref, w_ref, m_ref, v_ref, own_ref, recv_ref, *rest):
        g_ref, d_ref, mo_ref, vo_ref = rest[-4:]
        g = _sum_chip_partials(own_ref, recv_ref)
        g_ref[...] = g
        d_ref[...], mo_ref[...], vo_ref[...] = _adamw(w_ref[...], g, m_ref[...], v_ref[...])

    blk = pl.BlockSpec((None, row_tile, cols), lambda l, r, chip_ref: (first + l, r, 0))
    args = [chip, w, m, v, partial, received]
    in_specs = [blk, blk, blk, *_partial_specs(row_tile, cols, first)]
    aliases = {}
    if continued is not None:
        aliases = {len(args) + k: k for k in range(4)}
        args += list(continued)
        in_specs += [HBM] * 4
    return pl.pallas_call(
        body, name=name,
        grid_spec=pltpu.PrefetchScalarGridSpec(
            num_scalar_prefetch=1, grid=(stop - first, rows // row_tile), in_specs=in_specs, out_specs=[blk] * 4),
        out_shape=[jax.ShapeDtypeStruct(w.shape, F32)] * 4, input_output_aliases=aliases,
        compiler_params=_params(dimension_semantics=("arbitrary", "arbitrary")),
    )(*args)


def _reduce_w_pool(partial, received, chip):
    def body(chip_ref, own_ref, recv_ref, g_ref):
        g_ref[...] = _sum_chip_partials(own_ref, recv_ref)

    return pl.pallas_call(
        body, name="reduce_w_pool",
        grid_spec=pltpu.PrefetchScalarGridSpec(
            num_scalar_prefetch=1, grid=(DEPTH, 1), in_specs=list(_partial_specs(POOL_SHARD, GROUP_D)),
            out_specs=pl.BlockSpec((POOL_SHARD, GROUP_D), lambda l, r, chip_ref: (l, 0))),
        out_shape=jax.ShapeDtypeStruct((DEPTH * POOL_SHARD, GROUP_D), F32),
        compiler_params=_params(dimension_semantics=("arbitrary", "arbitrary")),
    )(chip, partial, received)


def _adamw_small(params):
    n = len(params)

    def body(*refs):
        ins, outs = refs[:4 * n], refs[4 * n:]
        for p in range(n):
            w_ref, g_ref, m_ref, v_ref = ins[4 * p:4 * p + 4]
            d_ref, mo_ref, vo_ref = outs[3 * p:3 * p + 3]
            d_ref[...], mo_ref[...], vo_ref[...] = _adamw(w_ref[...], g_ref[...], m_ref[...], v_ref[...])

    vmem = pl.BlockSpec(memory_space=pltpu.VMEM)
    flat = [a for group in params for a in group]
    out_shape = [jax.ShapeDtypeStruct(group[0].shape, F32) for group in params for _ in range(3)]
    outs = pl.pallas_call(
        body, name="adamw_small", in_specs=[vmem] * len(flat), out_specs=[vmem] * len(out_shape),
        out_shape=out_shape, compiler_params=_params(),
    )(*flat)
    return [tuple(outs[3 * p:3 * p + 3]) for p in range(n)]


def _sum_sources(slabs):
    def body(s_ref, o_ref):
        acc = s_ref[0]
        for b in range(1, N_DEV):
            acc = acc + s_ref[b]
        o_ref[...] = acc

    vmem = pl.BlockSpec(memory_space=pltpu.VMEM)
    return pl.pallas_call(
        body, name="sum_small_grads", in_specs=[vmem], out_specs=vmem,
        out_shape=jax.ShapeDtypeStruct(slabs.shape[1:], F32), compiler_params=_params(),
    )(slabs)


def _to_bf16(a, name, layers=None):
    first, stop = layers or (0, a.shape[0])

    def body(a_ref, o_ref):
        o_ref[...] = a_ref[...].astype(BF16)

    block = (None,) + a.shape[1:]
    return pl.pallas_call(
        body, name=name, grid=(stop - first,), in_specs=[pl.BlockSpec(block, lambda l: (first + l, 0, 0))],
        out_specs=pl.BlockSpec(block, lambda l: (l, 0, 0)),
        out_shape=jax.ShapeDtypeStruct((stop - first,) + a.shape[1:], BF16),
        compiler_params=_params(dimension_semantics=("arbitrary",)),
    )(a)


def kernel(x, c, w_ada, b_ada, g_pre, w_in, w_conv, w_pool, pool_scale, w_out, g_post, loss_target, m_w_ada, m_b_ada, m_g_pre, m_w_in, m_w_conv, m_w_pool, m_pool_scale, m_w_out, m_g_post, v_w_ada, v_b_ada, v_g_pre, v_w_in, v_w_conv, v_w_pool, v_pool_scale, v_w_out, v_g_post):
    mx, my, mc = _mesh_position()
    me = _block_id(mx, my, mc)
    chip = (2 * mx + my).astype(jnp.int32).reshape(1)
    core = mc.astype(jnp.int32).reshape(1)
    x0 = x[0]
    target = loss_target[0]
    conv_shard = w_conv.shape[-1]

    own_small = jnp.concatenate([c, w_conv.reshape(1, DEPTH * 3 * conv_shard)], axis=1)
    all_small = _all_gather_small(own_small, "all_gather_c_w_conv")[:, 0, :]
    gathers = [_gather_weights_start(
        0, _to_bf16(w_in, "cast_w_in_0", (0, 1)), _to_bf16(w_out, "cast_w_out_0", (0, 1)), [all_small])]
    c_all = all_small[:, :D_MODEL]
    w_conv_full = all_small[:, D_MODEL:].reshape(N_DEV, DEPTH, 3, conv_shard).transpose(1, 2, 0, 3).reshape(
        DEPTH, 3, CONV_W)
    cps = jnp.concatenate([w_conv_full, pool_scale[:, None], jnp.zeros((DEPTH, 4, CONV_W), F32)], axis=1)

    c_act, pieces = _modulation_columns(c_all, w_ada)
    upper = (1, DEPTH)
    upper_shards = [_to_bf16(w_in, "cast_w_in_1", upper), _to_bf16(w_out, "cast_w_out_1", upper)]
    wpool_b = _to_bf16(w_pool.reshape(DEPTH, POOL_ROWS, GROUP_D), "cast_w_pool").reshape(w_pool.shape)
    mod_all = _all_gather_small(
        pieces, "all_gather_modulation", [gathers[0][1][0], *upper_shards, wpool_b])
    mod_mine = lax.dynamic_index_in_dim(mod_all, me, axis=1, keepdims=False)
    mod = mod_mine.reshape(N_DEV, DEPTH, W_IN_SHARD).transpose(1, 0, 2).reshape(DEPTH, 3 * D_MODEL) + b_ada
    zeros_d = jnp.zeros((DEPTH, 3, D_MODEL), F32)
    vec = jnp.concatenate([mod.reshape(DEPTH, 3, D_MODEL), g_pre[:, None], g_post[:, None], zeros_d], axis=1)

    gathers.append(_gather_weights_start(1, *upper_shards, [mod_all]))
    gathers = [(first_sems, shards, landing[k], k) for first_sems, shards, landing in gathers
               for k in range(len(landing))]

    xs, kept, wins, wouts = [x0], [], [], []
    for l in range(DEPTH):
        first_sems, shards, zones, index = gathers[l]
        passed_sems, zones = _gather_weights_pass_on(
            l, index, first_sems[1], zones, [vec, cps, wpool_b, gathers[-1][1][0]] if l == 0 else [xs[-1]])
        win, wout = _gather_weights_finish(l, index, first_sems, passed_sems, shards, zones)
        x_next, *for_backward = _forward_layer(l, xs[-1], vec, cps, wpool_b, win, wout)
        xs.append(x_next)
        kept.append(for_backward)
        wins.append(win)
        wouts.append(wout)
    dx, loss_tile = _loss_head(xs[DEPTH], target)

    slab_rows = [None] * DEPTH
    partials = received = None
    in_flight = []

    def scatter(layer, grads, from_sibling, after):
        nonlocal partials, received
        partials = _add_sibling_blocks(
            f"grad_add_sibling_{layer}", layer, grads, from_sibling, core, partials, ALL_KINDS)
        chips = _chips_exchange(layer, partials, received, ALL_KINDS)
        sems, partials, received, token = _start_exchange(chips, f"grad_chips_start_{layer}", after)
        in_flight.append((chips, sems, layer))
        return token

    grads_above = None
    for l in reversed(range(DEPTH)):
        y, h_t, ycat_t, uc, fa, fp = kept[l]
        dx, dproj, dy, gwpool, dcps, dvec = _backward_layer(
            l, dx, y, xs[l], uc, fa, fp, vec, cps, wpool_b, wouts[l], wins[l])
        slab_rows[l] = jnp.concatenate(
            [dvec[0], dvec[1], dvec[2], dvec[3], dvec[4], dcps[3], dcps[0], dcps[1], dcps[2],
             loss_tile[0] if l == 0 else jnp.zeros((LANES,), F32)])
        if l == 0:
            slabs = _all_gather_small(jnp.stack(slab_rows), "all_gather_small_grads")
        hosted = _sibling_exchange(grads_above, ALL_KINDS) if grads_above is not None else None
        gwin, gwout, *from_sibling = _weight_grads(l, h_t, dproj, ycat_t, dy, hosted)
        if grads_above is not None:
            scatter(l + 1, grads_above, from_sibling, [])
        grads_above = [gwin, gwout, gwpool.reshape(POOL_ROWS, GROUP_D)]
        if l <= 1:
            from_sibling = _run_exchange(_sibling_exchange(grads_above, ALL_KINDS), f"grad_exchange_sibling_{l}")
            token = scatter(l, grads_above, from_sibling, [slabs] if l == 0 else [])
            grads_above = None
    grad_x = dx[None]
    chips_0, sems_0, _ = in_flight.pop()

    total = _sum_sources(slabs)
    loss = total[0, SLAB_COLS]
    o = 3 * D_MODEL
    g_b_ada = total[:, :o]
    g_g_pre = total[:, o:o + D_MODEL]
    g_g_post = total[:, o + D_MODEL:o + 2 * D_MODEL]
    g_pool_scale = total[:, o + 2 * D_MODEL:o + 2 * D_MODEL + POOL_W]
    g_conv_full = total[:, o + 2 * D_MODEL + POOL_W:SLAB_COLS].reshape(DEPTH, 3, CONV_W)
    g_w_conv = lax.dynamic_slice_in_dim(g_conv_full, me * conv_shard, conv_shard, axis=2)

    after = [token]
    for chips, sems, l in in_flight:
        partials, received = _finish_exchange(chips, f"grad_chips_finish_{l}", sems, partials, received, after)
        after = []
    upper = (1, DEPTH)
    w_in_upper = _adamw_reduced(
        "adamw_w_in_upper", w_in, m_w_in, v_w_in, partials[0], received[0], chip, ROW_TILE, upper, None)
    w_out_upper = _adamw_reduced(
        "adamw_w_out_upper", w_out, m_w_out, v_w_out, partials[1], received[1], chip, W_OUT_SHARD, upper, None)
    dmod_all = slabs[:, :, :o].reshape(N_DEV, DEPTH, N_DEV, W_IN_SHARD)
    dmod_cols = lax.dynamic_index_in_dim(dmod_all, me, axis=2, keepdims=False).transpose(1, 0, 2) + token[0, 0]
    g_w_ada, d_w_ada, nm_w_ada, nv_w_ada = _adamw_w_ada(w_ada, m_w_ada, v_w_ada, c_act.T, dmod_cols, None)

    partials, received = _finish_exchange(
        chips_0, "grad_chips_finish_0", sems_0, partials, received, [nv_w_ada, w_in_upper[3], w_out_upper[3]])
    gather_pool = _all_gather_exchange(_reduce_w_pool(partials[2], received[2], chip))
    sems_p, pool_rows, pool_landing, token_p = _start_exchange(gather_pool, "all_gather_grad_w_pool_start")
    g_w_in, d_w_in, nm_w_in, nv_w_in = _adamw_reduced(
        "adamw_w_in_0", w_in, m_w_in, v_w_in, partials[0], received[0], chip, ROW_TILE, (0, 1), w_in_upper)
    g_w_out, d_w_out, nm_w_out, nv_w_out = _adamw_reduced(
        "adamw_w_out_0", w_out, m_w_out, v_w_out, partials[1], received[1], chip, W_OUT_SHARD, (0, 1), w_out_upper)
    _, (g_pool_all,) = _finish_exchange(
        gather_pool, "all_gather_grad_w_pool_finish", sems_p, pool_rows, pool_landing, [nv_w_in, nv_w_out])
    g_w_pool = g_pool_all.reshape(N_DEV, DEPTH, POOL_SHARD, GROUP_D).transpose(1, 0, 2, 3).reshape(w_pool.shape)

    flat2 = lambda a: a.reshape(-1, a.shape[-1])
    small = _adamw_small([
        (b_ada, g_b_ada, m_b_ada, v_b_ada),
        (g_pre, g_g_pre, m_g_pre, v_g_pre),
        (flat2(w_conv), flat2(g_w_conv), flat2(m_w_conv), flat2(v_w_conv)),
        (flat2(w_pool), flat2(g_w_pool), flat2(m_w_pool), flat2(v_w_pool)),
        (pool_scale, g_pool_scale, m_pool_scale, v_pool_scale),
        (g_post, g_g_post, m_g_post, v_g_post),
    ])
    (d_b_ada, nm_b_ada, nv_b_ada), (d_g_pre, nm_g_pre, nv_g_pre), conv_upd, pool_upd, \
        (d_ps, nm_ps, nv_ps), (d_g_post, nm_g_post, nv_g_post) = small
    d_w_conv, nm_w_conv, nv_w_conv = (a.reshape(w_conv.shape) for a in conv_upd)
    d_w_pool, nm_w_pool, nv_w_pool = (a.reshape(w_pool.shape) for a in pool_upd)

    return (loss, grad_x,
            g_w_ada, g_b_ada, g_g_pre, g_w_in, g_w_conv, g_w_pool, g_pool_scale, g_w_out, g_g_post,
            d_w_ada, d_b_ada, d_g_pre, d_w_in, d_w_conv, d_w_pool, d_ps, d_w_out, d_g_post,
            nm_w_ada, nm_b_ada, nm_g_pre, nm_w_in, nm_w_conv, nm_w_pool, nm_ps, nm_w_out, nm_g_post,
            nv_w_ada, nv_b_ada, nv_g_pre, nv_w_in, nv_w_conv, nv_w_pool, nv_ps, nv_w_out, nv_g_post)
```

```python
import jax
import jax.numpy as jnp
from jax import lax
from jax.experimental import pallas as pl
from jax.experimental.pallas import tpu as pltpu

F32 = jnp.float32
BF16 = jnp.bfloat16

D_MODEL = 1024
DEPTH = 4
CONV_W = 512
POOL_W = 512
POOL_WINDOWS = (2, 4, 8, 16)
GROUP_D = 128
IN_COLS = 4 * CONV_W + 2 * POOL_W
NORM_EPS = 1e-6

ADAM_LR = 0.001
ADAM_B1 = 0.9
ADAM_B2 = 0.999
ADAM_EPS = 1e-08
ADAM_WD = 0.01
ADAM_STEP = 10

N_DEV = 8
N_CHIP = 4
N_OTHER_CHIPS = N_CHIP - 1
MESH = pl.DeviceIdType.MESH
W_IN_SHARD = IN_COLS // N_DEV
W_OUT_SHARD = D_MODEL // N_DEV
POOL_ROWS = len(POOL_WINDOWS) * GROUP_D
POOL_SHARD = POOL_ROWS // N_DEV

SUBLANES = 8
LANES = 128
VMEM_LIMIT_BYTES = 56 * 1024 * 1024
ROW_TILE = 512
BWD_TILE = 256
GWIN_COLS = 768
GWOUT_COLS = 512
POOL_HALO = 16
CONV_HALO = SUBLANES

SLAB_COLS = 3 * D_MODEL + D_MODEL + D_MODEL + POOL_W + 3 * CONV_W

HBM = pl.BlockSpec(memory_space=pl.ANY)


def _params(**kw):
    return pltpu.CompilerParams(vmem_limit_bytes=VMEM_LIMIT_BYTES, **kw)


def _sigmoid(v):
    return 1.0 / (1.0 + jnp.exp(-v))


def _dot(a, b):
    return jnp.dot(a, b, preferred_element_type=F32)


def _dot_tn(a, b):
    return lax.dot_general(a, b, (((0,), (0,)), ((), ())), preferred_element_type=F32)


def _dot_nt(a, b):
    return lax.dot_general(a, b, (((1,), (1,)), ((), ())), preferred_element_type=F32)


def _rows_from_before(v, k):
    return pltpu.roll(v, k, 0)


def _rows_from_after(v, k):
    return pltpu.roll(v, v.shape[0] - k, 0)


def _window_counts(t0, rows):
    return (lax.broadcasted_iota(jnp.int32, (rows, 1), 0) + (t0 + 1)).astype(F32)


def _split_proj(p32):
    cw = CONV_W
    return (p32[:, 0 * cw:1 * cw], p32[:, 1 * cw:2 * cw], p32[:, 2 * cw:3 * cw], p32[:, 3 * cw:4 * cw],
            p32[:, 4 * cw:4 * cw + POOL_W], p32[:, 4 * cw + POOL_W:])


def _layer_spec(shape, layer):
    nd = len(shape)
    return pl.BlockSpec((None,) + tuple(shape[1:]), lambda i, _l=layer, _n=nd: (_l,) + (0,) * (_n - 1))


def _whole_spec(shape):
    return pl.BlockSpec(tuple(shape), lambda i, _n=len(shape): (0,) * _n)


def _mesh_position():
    return lax.axis_index("x"), lax.axis_index("y"), lax.axis_index("c")


def _block_id(x, y, c):
    return 4 * x + 2 * y + c


def _other_chips(x, y):
    return [(x ^ 1, y), (x, y ^ 1), (x ^ 1, y ^ 1)]


def _col_block(ref, blk):
    return ref.at[:, pl.ds(pl.multiple_of(blk * W_IN_SHARD, LANES), W_IN_SHARD)]


def _row_block(rows):
    def block(ref, blk):
        return ref.at[pl.ds(pl.multiple_of(blk * rows, rows), rows), :]
    return block


_BLOCK_OF = (_col_block, _row_block(W_OUT_SHARD), _row_block(POOL_SHARD))
_BLOCK_SHAPES = ((D_MODEL, W_IN_SHARD), (W_OUT_SHARD, D_MODEL), (POOL_SHARD, GROUP_D))


class _Exchange:
    def __init__(self, inputs, out_shapes, aliases, sem_shapes, make):
        self.inputs, self.out_shapes, self.aliases, self.sem_shapes, self.make = (
            list(inputs), list(out_shapes), dict(aliases), list(sem_shapes), make)


def _run_exchange(exchange, name):
    n_in, n_out = len(exchange.inputs), len(exchange.out_shapes)

    def body(*refs):
        start, finish = exchange.make(refs[:n_in], refs[n_in:n_in + n_out], refs[n_in + n_out:])
        start()
        finish()

    return pl.pallas_call(
        body, name=name, in_specs=[HBM] * n_in, out_specs=[HBM] * n_out, out_shape=exchange.out_shapes,
        scratch_shapes=exchange.sem_shapes, input_output_aliases=exchange.aliases, compiler_params=_params(),
    )(*exchange.inputs)


_SEM = pl.BlockSpec(memory_space=pltpu.SEMAPHORE)
_DATAFLOW = pltpu.SideEffectType.DATAFLOW_SIDE_EFFECTING


def _start_exchange(exchange, name, after=()):
    n_in, n_out, n_sem = len(exchange.inputs), len(exchange.out_shapes), len(exchange.sem_shapes)
    sources = [i for i in range(n_in) if i not in exchange.aliases]
    aliases = {i: n_sem + k for k, i in enumerate(sources)}
    aliases.update({i: n_sem + len(sources) + o for i, o in exchange.aliases.items()})

    def body(*refs):
        in_refs = refs[:n_in]
        outs = refs[n_in + len(after):]
        sems = outs[:n_sem]
        out_refs = outs[n_sem + len(sources):n_sem + len(sources) + n_out]
        exchange.make(in_refs, out_refs, sems)[0]()
        refs[-1][...] = jnp.zeros_like(refs[-1])

    outs = pl.pallas_call(
        body, name=name, in_specs=[HBM] * (n_in + len(after)),
        out_specs=[_SEM] * n_sem + [HBM] * (len(sources) + n_out) + [pl.BlockSpec(memory_space=pltpu.VMEM)],
        out_shape=(exchange.sem_shapes + [pltpu.HBM(exchange.inputs[i].shape, exchange.inputs[i].dtype) for i in sources]
                   + [pltpu.HBM(s.shape, s.dtype) for s in exchange.out_shapes]
                   + [jax.ShapeDtypeStruct((SUBLANES, LANES), F32)]),
        input_output_aliases=aliases, compiler_params=_params(has_side_effects=_DATAFLOW),
    )(*exchange.inputs, *after)
    return outs[:n_sem], outs[n_sem:n_sem + len(sources)], outs[n_sem + len(sources):-1], outs[-1]


def _finish_exchange(exchange, name, sems, sources, landing, after):
    n_src, n_out, n_sem = len(sources), len(landing), len(sems)
    n_in = len(exchange.inputs)
    source_at = [i for i in range(n_in) if i not in exchange.aliases]

    def body(*refs):
        src_refs, out_refs = refs[:n_src], refs[n_src:n_src + n_out]
        sem_refs = refs[n_src + n_out:n_src + n_out + n_sem]
        in_refs = [None] * n_in
        for k, i in enumerate(source_at):
            in_refs[i] = src_refs[k]
        for i, o in exchange.aliases.items():
            in_refs[i] = out_refs[o]
        exchange.make(in_refs, out_refs, sem_refs)[1]()

    arrays = list(sources) + list(landing)
    outs = pl.pallas_call(
        body, name=name, in_specs=[HBM] * len(arrays) + [_SEM] * n_sem + [HBM] * len(after),
        out_specs=[HBM] * len(arrays), out_shape=[pltpu.HBM(a.shape, a.dtype) for a in arrays],
        input_output_aliases={i: i for i in range(len(arrays))}, compiler_params=_params(has_side_effects=_DATAFLOW),
    )(*arrays, *sems, *after)
    return outs[:n_src], outs[n_src:]


N_GATHERED = 2
FIRST_COPIES = 1 + N_OTHER_CHIPS
_GATHERED_SHAPES = ((D_MODEL, IN_COLS), (D_MODEL, D_MODEL))


def _first_sem(layer, a, k):
    return (layer * N_GATHERED + a) * FIRST_COPIES + k


def _gather_copy(window_of, full_ref, blk, send_sem, recv_sem, to, src=None):
    window = window_of(full_ref, blk)
    return pltpu.make_async_remote_copy(
        src_ref=window if src is None else src, dst_ref=window, send_sem=send_sem, recv_sem=recv_sem,
        device_id=to, device_id_type=MESH)


def _first_copies(layer, shard_refs, full_refs, send_sems, recv_sems, local_sems):
    x, y, c = _mesh_position()
    me = _block_id(x, y, c)
    own, remote = [], []
    for a in range(N_GATHERED):
        shard = shard_refs[a].at[layer]
        own.append(pltpu.make_async_copy(
            shard, _BLOCK_OF[a](full_refs[a], me), local_sems.at[layer * N_GATHERED + a]))
        targets = [(x, y, 1 - c)] + [(*chip, c) for chip in _other_chips(x, y)]
        remote += [_gather_copy(_BLOCK_OF[a], full_refs[a], me, send_sems.at[_first_sem(layer, a, k)],
                                recv_sems.at[_first_sem(layer, a, k)], to, src=shard)
                   for k, to in enumerate(targets)]
    return own, remote


def _gather_weights_start(first_layer, win_shards, wout_shards, after):
    n_layers = win_shards.shape[0]
    n_first = n_layers * N_GATHERED * FIRST_COPIES
    sem_shapes = [pltpu.SemaphoreType.DMA((n_first,)), pltpu.SemaphoreType.DMA((n_first,)),
                  pltpu.SemaphoreType.DMA((n_layers * N_GATHERED,))]
    shards = [win_shards, wout_shards]

    def body(win_sh, wout_sh, *rest):
        send_sems, recv_sems, local_sems, win_thru, wout_thru, *landing = rest[len(after):]
        for layer in range(n_layers):
            own, remote = _first_copies(layer, (win_sh, wout_sh), landing[N_GATHERED * layer:N_GATHERED * (layer + 1)],
                                        send_sems, recv_sems, local_sems)
            for cp in own + remote:
                cp.start()

    outs = pl.pallas_call(
        body, name=f"all_gather_weights_start_{first_layer}", in_specs=[HBM] * (2 + len(after)),
        out_specs=[_SEM] * 3 + [HBM] * (2 + n_layers * N_GATHERED),
        out_shape=(sem_shapes + [pltpu.HBM(s.shape, s.dtype) for s in shards]
                   + [pltpu.HBM(s, BF16) for _ in range(n_layers) for s in _GATHERED_SHAPES]),
        input_output_aliases={0: 3, 1: 4}, compiler_params=_params(has_side_effects=_DATAFLOW),
    )(*shards, *after)
    landing = outs[5:]
    return outs[:3], outs[3:5], [landing[N_GATHERED * l:N_GATHERED * (l + 1)] for l in range(n_layers)]


def _passed_on_copies(full_refs, send_sems, recv_sems, core_of_block):
    x, y, c = _mesh_position()
    return [_gather_copy(_BLOCK_OF[a], full_refs[a], _block_id(*chip, core_of_block),
                         send_sems.at[a * N_OTHER_CHIPS + j], recv_sems.at[a * N_OTHER_CHIPS + j], (x, y, 1 - c))
            for a in range(N_GATHERED) for j, chip in enumerate(_other_chips(x, y))]


def _gather_weights_pass_on(layer, index, first_recv_sems, landing, after):
    n = N_GATHERED * N_OTHER_CHIPS

    def body(win_ref, wout_ref, first_recv, *rest):
        send_sems, recv_sems = rest[len(after):len(after) + 2]
        x, y, c = _mesh_position()
        full_refs = (win_ref, wout_ref)
        passed = _passed_on_copies(full_refs, send_sems, recv_sems, c)
        for a in range(N_GATHERED):
            for j, chip in enumerate(_other_chips(x, y)):
                sem = _first_sem(index, a, 1 + j)
                _gather_copy(_BLOCK_OF[a], full_refs[a], _block_id(*chip, c), first_recv.at[sem], first_recv.at[sem],
                             (x, y, c)).wait_recv()
                passed[a * N_OTHER_CHIPS + j].start()

    outs = pl.pallas_call(
        body, name=f"all_gather_weights_pass_on_{layer}", in_specs=[HBM] * N_GATHERED + [_SEM] + [HBM] * len(after),
        out_specs=[_SEM] * 2 + [HBM] * N_GATHERED,
        out_shape=[pltpu.SemaphoreType.DMA((n,)), pltpu.SemaphoreType.DMA((n,))]
        + [pltpu.HBM(a.shape, a.dtype) for a in landing],
        input_output_aliases={a: 2 + a for a in range(N_GATHERED)},
        compiler_params=_params(has_side_effects=_DATAFLOW),
    )(*landing, first_recv_sems, *after)
    return outs[:2], outs[2:]


def _gather_weights_finish(layer, index, first_sems, passed_sems, shards, landing):
    def body(win_ref, wout_ref, first_send, first_recv, local_sems, passed_send, passed_recv, win_sh, wout_sh, *thru):
        x, y, c = _mesh_position()
        full_refs = (win_ref, wout_ref)
        own, remote = _first_copies(index, (win_sh, wout_sh), full_refs, first_send, first_recv, local_sems)
        for a in range(N_GATHERED):
            sem = _first_sem(index, a, 0)
            _gather_copy(_BLOCK_OF[a], full_refs[a], _block_id(x, y, 1 - c), first_recv.at[sem], first_recv.at[sem],
                         (x, y, c)).wait_recv()
        for cp in _passed_on_copies(full_refs, passed_send, passed_recv, 1 - c):
            cp.wait_recv()
        for cp in remote + _passed_on_copies(full_refs, passed_send, passed_recv, c):
            cp.wait_send()
        for cp in own:
            cp.wait()

    return pl.pallas_call(
        body, name=f"all_gather_weights_finish_{layer}", in_specs=[HBM] * N_GATHERED + [_SEM] * 5 + [HBM] * 2,
        out_specs=[HBM] * N_GATHERED, out_shape=[pltpu.HBM(a.shape, a.dtype) for a in landing],
        input_output_aliases={a: a for a in range(N_GATHERED)}, compiler_params=_params(has_side_effects=_DATAFLOW),
    )(*landing, *first_sems, *passed_sems, *shards)


ALL_KINDS = (0, 1, 2)


def _sibling_exchange(grads, kinds):
    n_arr = len(grads)

    def make(in_refs, out_refs, sems):
        send_sems, recv_sems = sems
        x, y, c = _mesh_position()
        copies = [pltpu.make_async_remote_copy(
            src_ref=_BLOCK_OF[kinds[a]](in_refs[a], 2 * q + (1 - c)), dst_ref=out_refs[a].at[q],
            send_sem=send_sems.at[a, q], recv_sem=recv_sems.at[a, q], device_id=(x, y, 1 - c), device_id_type=MESH)
            for a in range(n_arr) for q in range(N_CHIP)]

        def start():
            for cp in copies:
                cp.start()

        def finish():
            for cp in copies:
                cp.wait_recv()
            for cp in copies:
                cp.wait_send()

        return start, finish

    return _Exchange(
        grads, [jax.ShapeDtypeStruct((N_CHIP,) + _BLOCK_SHAPES[k], BF16) for k in kinds], {},
        [pltpu.SemaphoreType.DMA((n_arr, N_CHIP)), pltpu.SemaphoreType.DMA((n_arr, N_CHIP))], make)


def _chips_exchange(layer, partials, received, kinds):
    n_arr = len(partials)

    def make(in_refs, out_refs, sems):
        send_sems, recv_sems = sems
        x, y, c = _mesh_position()
        copies = [pltpu.make_async_remote_copy(
            src_ref=in_refs[a].at[2 * qx + qy, layer], dst_ref=out_refs[a].at[j, layer],
            send_sem=send_sems.at[a * N_OTHER_CHIPS + j], recv_sem=recv_sems.at[a * N_OTHER_CHIPS + j],
            device_id=(qx, qy, c), device_id_type=MESH)
            for a in range(n_arr) for j, (qx, qy) in enumerate(_other_chips(x, y))]

        def start():
            for cp in copies:
                cp.start()

        def finish():
            for cp in copies:
                cp.wait_recv()
            for cp in copies:
                cp.wait_send()

        return start, finish

    inputs = list(partials)
    aliases = {}
    if received is not None:
        inputs += list(received)
        aliases = {n_arr + a: a for a in range(n_arr)}
    return _Exchange(
        inputs, [jax.ShapeDtypeStruct((N_OTHER_CHIPS, DEPTH) + _BLOCK_SHAPES[k], BF16) for k in kinds], aliases,
        [pltpu.SemaphoreType.DMA((n_arr * N_OTHER_CHIPS,)), pltpu.SemaphoreType.DMA((n_arr * N_OTHER_CHIPS,))], make)


def _all_gather_exchange(v):
    def make(in_refs, out_refs, sems):
        send_sems, recv_sems, local_sem = sems
        x, y, c = _mesh_position()
        me = _block_id(x, y, c)
        own = pltpu.make_async_copy(in_refs[0], out_refs[0].at[me], local_sem.at[0])
        sends, arrivals = [], []
        for k in range(1, N_DEV):
            px, py, pc = x ^ ((k >> 2) & 1), y ^ ((k >> 1) & 1), c ^ (k & 1)
            sends.append(pltpu.make_async_remote_copy(
                src_ref=in_refs[0], dst_ref=out_refs[0].at[me], send_sem=send_sems.at[k - 1],
                recv_sem=recv_sems.at[k - 1], device_id=(px, py, pc), device_id_type=MESH))
            arrivals.append(pltpu.make_async_remote_copy(
                src_ref=in_refs[0], dst_ref=out_refs[0].at[_block_id(px, py, pc)], send_sem=send_sems.at[k - 1],
                recv_sem=recv_sems.at[k - 1], device_id=(x, y, c), device_id_type=MESH))

        def start():
            for cp in [own] + sends:
                cp.start()

        def finish():
            for cp in arrivals:
                cp.wait_recv()
            for cp in sends:
                cp.wait_send()
            own.wait()

        return start, finish

    return _Exchange(
        [v], [jax.ShapeDtypeStruct((N_DEV,) + v.shape, v.dtype)], {},
        [pltpu.SemaphoreType.DMA((N_DEV - 1,)), pltpu.SemaphoreType.DMA((N_DEV - 1,)),
         pltpu.SemaphoreType.DMA((1,))], make)


def _host(exchange, args, in_specs, out_shape, out_specs, scratch):
    n_own = (len(args), len(out_shape), len(scratch))
    if exchange is None:
        return {}, lambda refs: (refs, None)
    n_ex = (len(exchange.inputs), len(exchange.out_shapes), len(exchange.sem_shapes))
    aliases = {n_own[0] + i: n_own[1] + o for i, o in exchange.aliases.items()}
    args += exchange.inputs
    in_specs += [HBM] * n_ex[0]
    out_shape += exchange.out_shapes
    out_specs += [HBM] * n_ex[1]
    scratch += exchange.sem_shapes

    def split(refs):
        own, theirs, at = [], [], 0
        for mine, ex in zip(n_own, n_ex):
            own += refs[at:at + mine]
            theirs.append(refs[at + mine:at + mine + ex])
            at += mine + ex
        return own, exchange.make(*theirs)

    return aliases, split


def _all_gather_small(v, name, after=()):
    vmem = pl.BlockSpec(memory_space=pltpu.VMEM)

    def body(v_ref, *rest):
        out_ref, send_sems, recv_sems = rest[len(after):]
        x, y, c = _mesh_position()
        me = _block_id(x, y, c)
        out_ref[me] = v_ref[...]
        sends = []
        for k in range(1, N_DEV):
            px, py, pc = x ^ ((k >> 2) & 1), y ^ ((k >> 1) & 1), c ^ (k & 1)
            send = pltpu.make_async_remote_copy(
                src_ref=v_ref, dst_ref=out_ref.at[me], send_sem=send_sems.at[k - 1], recv_sem=recv_sems.at[k - 1],
                device_id=(px, py, pc), device_id_type=MESH)
            send.start()
            sends.append((send, _block_id(px, py, pc)))
        for k, (send, peer) in enumerate(sends):
            pltpu.make_async_remote_copy(
                src_ref=v_ref, dst_ref=out_ref.at[peer], send_sem=send_sems.at[k], recv_sem=recv_sems.at[k],
                device_id=(x, y, c), device_id_type=MESH).wait_recv()
        for send, _ in sends:
            send.wait_send()

    return pl.pallas_call(
        body, name=name, in_specs=[vmem] + [HBM] * len(after), out_specs=vmem,
        out_shape=jax.ShapeDtypeStruct((N_DEV,) + v.shape, v.dtype),
        scratch_shapes=[pltpu.SemaphoreType.DMA((N_DEV - 1,)), pltpu.SemaphoreType.DMA((N_DEV - 1,))],
        compiler_params=_params(),
    )(v, *after)


def _forward_layer(layer, x, vec, cps, wpool, win, wout):
    t_len = x.shape[0]
    n_tiles = t_len // ROW_TILE
    row = lambda cols: pl.BlockSpec((ROW_TILE, cols), lambda i: (i, 0))
    col_t = pl.BlockSpec((D_MODEL, ROW_TILE), lambda i: (0, i))
    widths = (D_MODEL, 2 * CONV_W, 3 * CONV_W, 4 * POOL_W)

    def body(x_ref, vec_ref, cps_ref, wpool_ref, win_ref, wout_ref,
             xo_ref, y_ref, ht_ref, ycatt_ref, uc_ref, fa_ref, fp_ref, zc_ref, pc_ref):
        i = pl.program_id(0)

        @pl.when(i == 0)
        def _():
            zc_ref[...] = jnp.zeros_like(zc_ref)
            pc_ref[...] = jnp.zeros_like(pc_ref)

        x_t = x_ref[...]
        shift, scale, gate = vec_ref[0:1, :], vec_ref[1:2, :], vec_ref[2:3, :]
        g_pre, g_post = vec_ref[3:4, :], vec_ref[4:5, :]
        w0, w1, w2, ps = cps_ref[0:1, :], cps_ref[1:2, :], cps_ref[2:3, :], cps_ref[3:4, :]
        rx = lax.rsqrt(jnp.mean(x_t * x_t, axis=-1, keepdims=True) + NORM_EPS)
        h = (x_t * rx) * g_pre * (1.0 + scale) + shift
        ht_ref[...] = h.T.astype(BF16)
        proj = _dot(h.astype(BF16), win_ref[...]).astype(BF16).astype(F32)
        u_a, b_a, c_a, g_a, u_p, g_p = _split_proj(proj)
        uc_ref[...] = jnp.concatenate([u_a, c_a], axis=1).astype(BF16)

        z = c_a * u_a
        zcat = jnp.concatenate([zc_ref[...], z], axis=0)
        zc_ref[...] = z[ROW_TILE - CONV_HALO:]
        conv = (w0 * _rows_from_before(zcat, 2)[CONV_HALO:] + w1 * _rows_from_before(zcat, 1)[CONV_HALO:] + w2 * z)
        sig_a = _sigmoid(g_a)
        silu_a = g_a * sig_a
        b_conv = b_a * conv
        y_a = b_conv * silu_a
        fa_ref[...] = jnp.concatenate(
            [silu_a * conv, silu_a * b_a, b_conv * (sig_a * (1.0 + g_a * (1.0 - sig_a)))], axis=1).astype(BF16)

        pcat = jnp.concatenate([pc_ref[...], u_p], axis=0)
        pc_ref[...] = u_p[ROW_TILE - POOL_HALO:]
        counts = _window_counts(i * ROW_TILE, ROW_TILE)
        pooled, mixed = [], []
        for g, w in enumerate(POOL_WINDOWS):
            cols = slice(g * GROUP_D, (g + 1) * GROUP_D)
            s = pcat[:, cols]
            step = 1
            while step < w:
                s = s + _rows_from_before(s, step)
                step *= 2
            pooled_g = (s[POOL_HALO:] * (1.0 / jnp.minimum(counts, float(w))) - u_p[:, cols]).astype(BF16)
            pooled.append(pooled_g)
            mixed.append(_dot(pooled_g, wpool_ref[g]))
        mixed = jnp.concatenate(mixed, axis=1)
        sig_p = _sigmoid(g_p)
        silu_p = g_p * sig_p
        mixed_ps = mixed * ps
        y_p = mixed_ps * silu_p
        fp_ref[...] = jnp.concatenate(
            [(ps * silu_p).astype(BF16), (mixed_ps * (sig_p * (1.0 + g_p * (1.0 - sig_p)))).astype(BF16),
             (silu_p * mixed).astype(BF16)] + pooled, axis=1)

        ycat = jnp.concatenate([y_a, y_p], axis=1)
        ycatt_ref[...] = ycat.T.astype(BF16)
        y_b = _dot(ycat.astype(BF16), wout_ref[...]).astype(BF16)
        y_ref[...] = y_b
        y_t = y_b.astype(F32)
        ry = lax.rsqrt(jnp.mean(y_t * y_t, axis=-1, keepdims=True) + NORM_EPS)
        xo_ref[...] = x_t + gate * (y_t * ry * g_post)

    return pl.pallas_call(
        body, name=f"forward_layer_{layer}", grid=(n_tiles,),
        in_specs=[row(D_MODEL), _layer_spec(vec.shape, layer), _layer_spec(cps.shape, layer),
                  _layer_spec(wpool.shape, layer), _whole_spec(win.shape), _whole_spec(wout.shape)],
        out_specs=[row(D_MODEL), row(D_MODEL), col_t, col_t] + [row(w) for w in widths[1:]],
        out_shape=[jax.ShapeDtypeStruct((t_len, D_MODEL), F32), jax.ShapeDtypeStruct((t_len, D_MODEL), BF16),
                   jax.ShapeDtypeStruct((D_MODEL, t_len), BF16), jax.ShapeDtypeStruct((D_MODEL, t_len), BF16)]
        + [jax.ShapeDtypeStruct((t_len, w), BF16) for w in widths[1:]],
        scratch_shapes=[pltpu.VMEM((CONV_HALO, CONV_W), F32), pltpu.VMEM((POOL_HALO, POOL_W), F32)],
        compiler_params=_params(dimension_semantics=("arbitrary",)),
    )(x, vec, cps, wpool, win, wout)


def _loss_head(x_final, target):
    t_len = x_final.shape[0]
    n_tiles = t_len // ROW_TILE

    def body(x_ref, t_ref, dx_ref, loss_ref):
        @pl.when(pl.program_id(0) == 0)
        def _():
            loss_ref[...] = jnp.zeros_like(loss_ref)

        err = x_ref[...] - t_ref[...]
        dx_ref[...] = err * (1.0 / D_MODEL)
        loss_ref[...] += jnp.sum(err * err) * (0.5 / D_MODEL)

    row = pl.BlockSpec((ROW_TILE, D_MODEL), lambda i: (i, 0))
    return pl.pallas_call(
        body, name="loss_head", grid=(n_tiles,), in_specs=[row, row],
        out_specs=[row, pl.BlockSpec((SUBLANES, LANES), lambda i: (0, 0))],
        out_shape=[jax.ShapeDtypeStruct((t_len, D_MODEL), F32), jax.ShapeDtypeStruct((SUBLANES, LANES), F32)],
        compiler_params=_params(dimension_semantics=("arbitrary",)),
    )(x_final, target)


def _backward_layer(layer, dxo, y, x, uc, fa, fp, vec, cps, wpool, wout, win):
    t_len = dxo.shape[0]
    n_tiles = t_len // BWD_TILE
    halo_per_tile = BWD_TILE // POOL_HALO
    rev = lambda cols: pl.BlockSpec((BWD_TILE, cols), lambda i: (n_tiles - 1 - i, 0))
    halo_spec = pl.BlockSpec(
        (POOL_HALO, 2 * CONV_W), lambda i: (jnp.maximum((n_tiles - 1 - i) * halo_per_tile - 1, 0), 0))
    gwpool_shape = (len(POOL_WINDOWS), GROUP_D, GROUP_D)

    def body(dxo_ref, y_ref, x_ref, uc_ref, uch_ref, fa_ref, fp_ref, vec_ref, cps_ref, wpool_ref, wout_ref, win_ref,
             dx_ref, dproj_ref, dy_ref, gwpool_ref, dcps_ref, dvec_ref, gwpool_acc, dcc_ref, qc_ref):
        i = pl.program_id(0)
        tile = n_tiles - 1 - i

        @pl.when(i == 0)
        def _():
            gwpool_acc[...] = jnp.zeros_like(gwpool_acc)
            dcps_ref[...] = jnp.zeros_like(dcps_ref)
            dvec_ref[...] = jnp.zeros_like(dvec_ref)
            dcc_ref[...] = jnp.zeros_like(dcc_ref)
            qc_ref[...] = jnp.zeros_like(qc_ref)

        shift, scale, gate = vec_ref[0:1, :], vec_ref[1:2, :], vec_ref[2:3, :]
        g_pre, g_post = vec_ref[3:4, :], vec_ref[4:5, :]
        w0, w1, w2 = cps_ref[0:1, :], cps_ref[1:2, :], cps_ref[2:3, :]

        dxo_t = dxo_ref[...]
        y_t = y_ref[...].astype(F32)
        ry = lax.rsqrt(jnp.mean(y_t * y_t, axis=-1, keepdims=True) + NORM_EPS)
        yh = y_t * ry
        dvec_ref[2:3, :] += jnp.sum(dxo_t * yh, axis=0, keepdims=True)
        dyh = dxo_t * (gate * g_post)
        dy_b = (ry * (dyh - yh * jnp.mean(dyh * yh, axis=-1, keepdims=True))).astype(BF16)
        dy_ref[...] = dy_b
        dycat = _dot_nt(dy_b, wout_ref[...])
        dy_a, dy_p = dycat[:, :CONV_W], dycat[:, CONV_W:]

        fa_t = fa_ref[...].astype(F32)
        db_a = dy_a * fa_t[:, :CONV_W]
        dconv = dy_a * fa_t[:, CONV_W:2 * CONV_W]
        dg_a = dy_a * fa_t[:, 2 * CONV_W:]
        uc_t = uc_ref[...].astype(F32)
        u_a, c_a = uc_t[:, :CONV_W], uc_t[:, CONV_W:]
        halo = jnp.where(tile > 0, uch_ref[...].astype(F32), 0.0)[POOL_HALO - CONV_HALO:]
        z = c_a * u_a
        zcat = jnp.concatenate([halo[:, CONV_W:] * halo[:, :CONV_W], z], axis=0)
        z1 = _rows_from_before(zcat, 1)[CONV_HALO:]
        z2 = _rows_from_before(zcat, 2)[CONV_HALO:]
        dccat = jnp.concatenate([dconv, dcc_ref[...]], axis=0)
        dc1 = _rows_from_after(dccat, 1)[:BWD_TILE]
        dc2 = _rows_from_after(dccat, 2)[:BWD_TILE]
        dz = w2 * dconv + w1 * dc1 + w0 * dc2
        dcc_ref[...] = dconv[:CONV_HALO]
        dcps_ref[0:1, :] += jnp.sum(dconv * z2, axis=0, keepdims=True)
        dcps_ref[1:2, :] += jnp.sum(dconv * z1, axis=0, keepdims=True)
        dcps_ref[2:3, :] += jnp.sum(dconv * z, axis=0, keepdims=True)
        du_a = dz * c_a
        dc_a = dz * u_a

        dmixed = (dy_p * fp_ref[:, :POOL_W].astype(F32)).astype(BF16)
        dg_p = dy_p * fp_ref[:, POOL_W:2 * POOL_W].astype(F32)
        dcps_ref[3:4, :] += jnp.sum(dy_p * fp_ref[:, 2 * POOL_W:3 * POOL_W].astype(F32), axis=0, keepdims=True)
        counts = _window_counts(tile * BWD_TILE, BWD_TILE)
        du_p, q_head = [], []
        for g, w in enumerate(POOL_WINDOWS):
            cols = slice(g * GROUP_D, (g + 1) * GROUP_D)
            dm_g = dmixed[:, cols]
            dpooled_g = _dot_nt(dm_g, wpool_ref[g])
            gwpool_acc[g] += _dot_tn(fp_ref[:, 3 * POOL_W + g * GROUP_D:3 * POOL_W + (g + 1) * GROUP_D], dm_g)
            q_g = dpooled_g * (1.0 / jnp.minimum(counts, float(w)))
            q_head.append(q_g[:POOL_HALO])
            s = jnp.concatenate([q_g, qc_ref[:, cols]], axis=0)
            step = 1
            while step < w:
                s = s + _rows_from_after(s, step)
                step *= 2
            du_p.append(s[:BWD_TILE] - dpooled_g)
        qc_ref[...] = jnp.concatenate(q_head, axis=1)
        dproj_b = jnp.concatenate([du_a, db_a, dc_a, dg_a] + du_p + [dg_p], axis=1).astype(BF16)
        dproj_ref[...] = dproj_b

        x_t = x_ref[...]
        rx = lax.rsqrt(jnp.mean(x_t * x_t, axis=-1, keepdims=True) + NORM_EPS)
        xn = x_t * rx
        mod_scale = 1.0 + scale
        dh = _dot_nt(dproj_b, win_ref[...])
        dvec_ref[0:1, :] += jnp.sum(dh, axis=0, keepdims=True)
        dvec_ref[1:2, :] += jnp.sum(dh * xn, axis=0, keepdims=True)
        dxn = dh * (g_pre * mod_scale)
        dx_ref[...] = dxo_t + rx * (dxn - xn * jnp.mean(dxn * xn, axis=-1, keepdims=True))

        @pl.when(i == n_tiles - 1)
        def _():
            gwpool_ref[...] = gwpool_acc[...].astype(BF16)
            sum_dh_xn, sum_dxo_yh = dvec_ref[1:2, :], dvec_ref[2:3, :]
            dvec_ref[1:2, :] = sum_dh_xn * g_pre
            dvec_ref[3:4, :] = sum_dh_xn * mod_scale
            dvec_ref[2:3, :] = sum_dxo_yh * g_post
            dvec_ref[4:5, :] = sum_dxo_yh * gate

    return pl.pallas_call(
        body, name=f"backward_layer_{layer}", grid=(n_tiles,),
        in_specs=[rev(D_MODEL), rev(D_MODEL), rev(D_MODEL), rev(2 * CONV_W), halo_spec, rev(3 * CONV_W),
                  rev(4 * POOL_W), _layer_spec(vec.shape, layer), _layer_spec(cps.shape, layer),
                  _layer_spec(wpool.shape, layer), _whole_spec(wout.shape), _whole_spec(win.shape)],
        out_specs=[rev(D_MODEL), rev(IN_COLS), rev(D_MODEL), _whole_spec(gwpool_shape),
                   _whole_spec((SUBLANES, CONV_W)), _whole_spec((SUBLANES, D_MODEL))],
        out_shape=[jax.ShapeDtypeStruct((t_len, D_MODEL), F32), jax.ShapeDtypeStruct((t_len, IN_COLS), BF16),
                   jax.ShapeDtypeStruct((t_len, D_MODEL), BF16), jax.ShapeDtypeStruct(gwpool_shape, BF16),
                   jax.ShapeDtypeStruct((SUBLANES, CONV_W), F32), jax.ShapeDtypeStruct((SUBLANES, D_MODEL), F32)],
        scratch_shapes=[pltpu.VMEM(gwpool_shape, F32), pltpu.VMEM((CONV_HALO, CONV_W), F32),
                        pltpu.VMEM((POOL_HALO, POOL_W), F32)],
        compiler_params=_params(dimension_semantics=("arbitrary",)),
    )(dxo, y, x, uc, uc, fa, fp, vec, cps, wpool, wout, win)


def _weight_grads(layer, h_t, dproj, ycat_t, dy, exchange, after=()):
    t_len = dy.shape[0]
    n_in, n_out = IN_COLS // GWIN_COLS, D_MODEL // GWOUT_COLS
    args = [h_t, dproj, ycat_t, dy, *after]
    in_specs = [_whole_spec(h_t.shape),
                pl.BlockSpec((t_len, GWIN_COLS), lambda s: (0, jnp.minimum(s, n_in - 1))),
                _whole_spec(ycat_t.shape),
                pl.BlockSpec((t_len, GWOUT_COLS), lambda s: (0, jnp.maximum(s - n_in, 0)))] + [HBM] * len(after)
    out_shape = [jax.ShapeDtypeStruct((D_MODEL, IN_COLS), BF16), jax.ShapeDtypeStruct((D_MODEL, D_MODEL), BF16)]
    out_specs = [pl.BlockSpec((D_MODEL, GWIN_COLS), lambda s: (0, jnp.minimum(s, n_in - 1))),
                 pl.BlockSpec((D_MODEL, GWOUT_COLS), lambda s: (0, jnp.maximum(s - n_in, 0)))]
    scratch = []
    aliases, split = _host(exchange, args, in_specs, out_shape, out_specs, scratch)

    def body(*refs):
        (ht_ref, dproj_ref, ycatt_ref, dy_ref, *_, gwin_ref, gwout_ref), hosted = split(refs)
        s = pl.program_id(0)
        if hosted is not None:
            pl.when(s == 0)(hosted[0])

        @pl.when(s < n_in)
        def _():
            gwin_ref[...] = _dot(ht_ref[...], dproj_ref[...]).astype(BF16)

        @pl.when(s >= n_in)
        def _():
            gwout_ref[...] = _dot(ycatt_ref[...], dy_ref[...]).astype(BF16)

        if hosted is not None:
            pl.when(s == n_in + n_out - 1)(hosted[1])

    return pl.pallas_call(
        body, name=f"weight_grads_{layer}", grid=(n_in + n_out,), in_specs=in_specs, out_specs=out_specs,
        out_shape=out_shape, scratch_shapes=scratch, input_output_aliases=aliases,
        compiler_params=_params(dimension_semantics=("arbitrary",)),
    )(*args)


def _add_sibling_blocks(name, layer, grads, received, core, partials, kinds):
    n_arr = len(grads)

    def body(core_ref, *refs):
        mine, theirs, outs = refs[:n_arr], refs[n_arr:2 * n_arr], refs[-n_arr:]
        for a in range(n_arr):
            outs[a][...] = (mine[a][...].astype(F32) + theirs[a][...].astype(F32)).astype(BF16)

    own_of_kind = [
        pl.BlockSpec((D_MODEL, W_IN_SHARD), lambda q, core_ref: (0, 2 * q + core_ref[0])),
        pl.BlockSpec((W_OUT_SHARD, D_MODEL), lambda q, core_ref: (2 * q + core_ref[0], 0)),
        pl.BlockSpec((POOL_SHARD, GROUP_D), lambda q, core_ref: (2 * q + core_ref[0], 0)),
    ]
    shapes = [_BLOCK_SHAPES[k] for k in kinds]
    recv_specs = [pl.BlockSpec((None,) + s, lambda q, core_ref: (q, 0, 0)) for s in shapes]
    out_specs = [pl.BlockSpec((None, None) + s, lambda q, core_ref: (q, layer, 0, 0)) for s in shapes]
    args = [core, *grads, *received]
    in_specs = [own_of_kind[k] for k in kinds] + recv_specs
    aliases = {}
    if partials is not None:
        aliases = {len(args) + a: a for a in range(n_arr)}
        args += list(partials)
        in_specs += [HBM] * n_arr
    return pl.pallas_call(
        body, name=name,
        grid_spec=pltpu.PrefetchScalarGridSpec(
            num_scalar_prefetch=1, grid=(N_CHIP,), in_specs=in_specs, out_specs=out_specs),
        out_shape=[jax.ShapeDtypeStruct((N_CHIP, DEPTH) + s, BF16) for s in shapes],
        input_output_aliases=aliases,
        compiler_params=_params(dimension_semantics=("arbitrary",)),
    )(*args)


def _modulation_columns(c_all, w_ada):
    def body(c_ref, w_ref, cact_ref, out_ref):
        c_t = c_ref[...]
        c_act = c_t * _sigmoid(c_t)
        cact_ref[...] = c_act
        out_ref[...] = jnp.dot(c_act, w_ref[...], preferred_element_type=F32, precision=lax.Precision.HIGHEST)

    return pl.pallas_call(
        body, name="modulation_columns", grid=(DEPTH,),
        in_specs=[pl.BlockSpec((N_DEV, D_MODEL), lambda l: (0, 0)),
                  pl.BlockSpec((None, D_MODEL, W_IN_SHARD), lambda l: (l, 0, 0))],
        out_specs=[pl.BlockSpec((N_DEV, D_MODEL), lambda l: (0, 0)),
                   pl.BlockSpec((N_DEV, W_IN_SHARD), lambda l: (0, l))],
        out_shape=[jax.ShapeDtypeStruct((N_DEV, D_MODEL), F32),
                   jax.ShapeDtypeStruct((N_DEV, DEPTH * W_IN_SHARD), F32)],
        compiler_params=_params(dimension_semantics=("arbitrary",)),
    )(c_all, w_ada)


def _adamw(w, g, m, v):
    m_new = ADAM_B1 * m + (1.0 - ADAM_B1) * g
    v_new = ADAM_B2 * v + (1.0 - ADAM_B2) * (g * g)
    m_hat = m_new / (1.0 - ADAM_B1 ** ADAM_STEP)
    v_hat = v_new / (1.0 - ADAM_B2 ** ADAM_STEP)
    delta = -ADAM_LR * (m_hat / (jnp.sqrt(v_hat) + ADAM_EPS) + ADAM_WD * w)
    return delta, m_new, v_new


def _adamw_w_ada(w, m, v, c_act_t, dmod_cols, exchange):
    big = pl.BlockSpec((None, D_MODEL, W_IN_SHARD), lambda l: (l, 0, 0))
    args = [w, m, v, c_act_t, dmod_cols]
    in_specs = [big, big, big, pl.BlockSpec((D_MODEL, N_DEV), lambda l: (0, 0)),
                pl.BlockSpec((None, N_DEV, W_IN_SHARD), lambda l: (l, 0, 0))]
    out_shape, out_specs, scratch = [jax.ShapeDtypeStruct(w.shape, F32)] * 4, [big] * 4, []
    aliases, split = _host(exchange, args, in_specs, out_shape, out_specs, scratch)

    def body(*refs):
        (w_ref, m_ref, v_ref, ct_ref, dm_ref, g_ref, d_ref, mo_ref, vo_ref), hosted = split(refs)
        if hosted is not None:
            pl.when(pl.program_id(0) == 0)(hosted[0])
        g = ct_ref[:, 0:1] * dm_ref[0:1, :]
        for b in range(1, N_DEV):
            g = g + ct_ref[:, b:b + 1] * dm_ref[b:b + 1, :]
        g_ref[...] = g
        d_ref[...], mo_ref[...], vo_ref[...] = _adamw(w_ref[...], g, m_ref[...], v_ref[...])
        if hosted is not None:
            pl.when(pl.program_id(0) == DEPTH - 1)(hosted[1])

    return pl.pallas_call(
        body, name="adamw_w_ada", grid=(DEPTH,), in_specs=in_specs, out_specs=out_specs, out_shape=out_shape,
        scratch_shapes=scratch, input_output_aliases=aliases,
        compiler_params=_params(dimension_semantics=("arbitrary",)),
    )(*args)


def _sum_chip_partials(own_ref, recv_ref):
    g = own_ref[...].astype(F32)
    for j in range(N_OTHER_CHIPS):
        g = g + recv_ref[j].astype(F32)
    return g


def _partial_specs(row_tile, cols, first_layer=0):
    own = pl.BlockSpec((None, None, row_tile, cols), lambda l, r, chip_ref: (chip_ref[0], first_layer + l, r, 0))
    recv = pl.BlockSpec((N_OTHER_CHIPS, None, row_tile, cols), lambda l, r, chip_ref: (0, first_layer + l, r, 0))
    return own, recv


def _adamw_reduced(name, w, m, v, partial, received, chip, row_tile, layers, continued):
    depth, rows, cols = w.shape
    first, stop = layers

    def body(chip_ref, w_ref, m_ref, v_ref, own_ref, recv_ref, *rest):
        g_ref, d_ref, mo_ref, vo_ref = rest[-4:]
        g = _sum_chip_partials(own_ref, recv_ref)
        g_ref[...] = g
        d_ref[...], mo_ref[...], vo_ref[...] = _adamw(w_ref[...], g, m_ref[...], v_ref[...])

    blk = pl.BlockSpec((None, row_tile, cols), lambda l, r, chip_ref: (first + l, r, 0))
    args = [chip, w, m, v, partial, received]
    in_specs = [blk, blk, blk, *_partial_specs(row_tile, cols, first)]
    aliases = {}
    if continued is not None:
        aliases = {len(args) + k: k for k in range(4)}
        args += list(continued)
        in_specs += [HBM] * 4
    return pl.pallas_call(
        body, name=name,
        grid_spec=pltpu.PrefetchScalarGridSpec(
            num_scalar_prefetch=1, grid=(stop - first, rows // row_tile), in_specs=in_specs, out_specs=[blk] * 4),
        out_shape=[jax.ShapeDtypeStruct(w.shape, F32)] * 4, input_output_aliases=aliases,
        compiler_params=_params(dimension_semantics=("arbitrary", "arbitrary")),
    )(*args)


def _reduce_w_pool(partial, received, chip):
    def body(chip_ref, own_ref, recv_ref, g_ref):
        g_ref[...] = _sum_chip_partials(own_ref, recv_ref)

    return pl.pallas_call(
        body, name="reduce_w_pool",
        grid_spec=pltpu.PrefetchScalarGridSpec(
            num_scalar_prefetch=1, grid=(DEPTH, 1), in_specs=list(_partial_specs(POOL_SHARD, GROUP_D)),
            out_specs=pl.BlockSpec((POOL_SHARD, GROUP_D), lambda l, r, chip_ref: (l, 0))),
        out_shape=jax.ShapeDtypeStruct((DEPTH * POOL_SHARD, GROUP_D), F32),
        compiler_params=_params(dimension_semantics=("arbitrary", "arbitrary")),
    )(chip, partial, received)


def _adamw_small(params):
    n = len(params)

    def body(*refs):
        ins, outs = refs[:4 * n], refs[4 * n:]
        for p in range(n):
            w_ref, g_ref, m_ref, v_ref = ins[4 * p:4 * p + 4]
            d_ref, mo_ref, vo_ref = outs[3 * p:3 * p + 3]
            d_ref[...], mo_ref[...], vo_ref[...] = _adamw(w_ref[...], g_ref[...], m_ref[...], v_ref[...])

    vmem = pl.BlockSpec(memory_space=pltpu.VMEM)
    flat = [a for group in params for a in group]
    out_shape = [jax.ShapeDtypeStruct(group[0].shape, F32) for group in params for _ in range(3)]
    outs = pl.pallas_call(
        body, name="adamw_small", in_specs=[vmem] * len(flat), out_specs=[vmem] * len(out_shape),
        out_shape=out_shape, compiler_params=_params(),
    )(*flat)
    return [tuple(outs[3 * p:3 * p + 3]) for p in range(n)]


def _sum_sources(slabs):
    def body(s_ref, o_ref):
        acc = s_ref[0]
        for b in range(1, N_DEV):
            acc = acc + s_ref[b]
        o_ref[...] = acc

    vmem = pl.BlockSpec(memory_space=pltpu.VMEM)
    return pl.pallas_call(
        body, name="sum_small_grads", in_specs=[vmem], out_specs=vmem,
        out_shape=jax.ShapeDtypeStruct(slabs.shape[1:], F32), compiler_params=_params(),
    )(slabs)


def _to_bf16(a, name, layers=None):
    first, stop = layers or (0, a.shape[0])

    def body(a_ref, o_ref):
        o_ref[...] = a_ref[...].astype(BF16)

    block = (None,) + a.shape[1:]
    return pl.pallas_call(
        body, name=name, grid=(stop - first,), in_specs=[pl.BlockSpec(block, lambda l: (first + l, 0, 0))],
        out_specs=pl.BlockSpec(block, lambda l: (l, 0, 0)),
        out_shape=jax.ShapeDtypeStruct((stop - first,) + a.shape[1:], BF16),
        compiler_params=_params(dimension_semantics=("arbitrary",)),
    )(a)


def kernel(x, c, w_ada, b_ada, g_pre, w_in, w_conv, w_pool, pool_scale, w_out, g_post, loss_target, m_w_ada, m_b_ada, m_g_pre, m_w_in, m_w_conv, m_w_pool, m_pool_scale, m_w_out, m_g_post, v_w_ada, v_b_ada, v_g_pre, v_w_in, v_w_conv, v_w_pool, v_pool_scale, v_w_out, v_g_post):
    mx, my, mc = _mesh_position()
    me = _block_id(mx, my, mc)
    chip = (2 * mx + my).astype(jnp.int32).reshape(1)
    core = mc.astype(jnp.int32).reshape(1)
    x0 = x[0]
    target = loss_target[0]
    conv_shard = w_conv.shape[-1]

    own_small = jnp.concatenate([c, w_conv.reshape(1, DEPTH * 3 * conv_shard)], axis=1)
    all_small = _all_gather_small(own_small, "all_gather_c_w_conv")[:, 0, :]
    gathers = [_gather_weights_start(
        0, _to_bf16(w_in, "cast_w_in_0", (0, 1)), _to_bf16(w_out, "cast_w_out_0", (0, 1)), [all_small])]
    c_all = all_small[:, :D_MODEL]
    w_conv_full = all_small[:, D_MODEL:].reshape(N_DEV, DEPTH, 3, conv_shard).transpose(1, 2, 0, 3).reshape(
        DEPTH, 3, CONV_W)
    cps = jnp.concatenate([w_conv_full, pool_scale[:, None], jnp.zeros((DEPTH, 4, CONV_W), F32)], axis=1)

    c_act, pieces = _modulation_columns(c_all, w_ada)
    upper = (1, DEPTH)
    upper_shards = [_to_bf16(w_in, "cast_w_in_1", upper), _to_bf16(w_out, "cast_w_out_1", upper)]
    wpool_b = _to_bf16(w_pool.reshape(DEPTH, POOL_ROWS, GROUP_D), "cast_w_pool").reshape(w_pool.shape)
    mod_all = _all_gather_small(
        pieces, "all_gather_modulation", [gathers[0][1][0], *upper_shards, wpool_b])
    mod_mine = lax.dynamic_index_in_dim(mod_all, me, axis=1, keepdims=False)
    mod = mod_mine.reshape(N_DEV, DEPTH, W_IN_SHARD).transpose(1, 0, 2).reshape(DEPTH, 3 * D_MODEL) + b_ada
    zeros_d = jnp.zeros((DEPTH, 3, D_MODEL), F32)
    vec = jnp.concatenate([mod.reshape(DEPTH, 3, D_MODEL), g_pre[:, None], g_post[:, None], zeros_d], axis=1)

    gathers.append(_gather_weights_start(1, *upper_shards, [mod_all]))
    gathers = [(first_sems, shards, landing[k], k) for first_sems, shards, landing in gathers
               for k in range(len(landing))]

    xs, kept, wins, wouts = [x0], [], [], []
    for l in range(DEPTH):
        first_sems, shards, zones, index = gathers[l]
        passed_sems, zones = _gather_weights_pass_on(
            l, index, first_sems[1], zones, [vec, cps, wpool_b, gathers[-1][1][0]] if l == 0 else [xs[-1]])
        win, wout = _gather_weights_finish(l, index, first_sems, passed_sems, shards, zones)
        x_next, *for_backward = _forward_layer(l, xs[-1], vec, cps, wpool_b, win, wout)
        xs.append(x_next)
        kept.append(for_backward)
        wins.append(win)
        wouts.append(wout)
    dx, loss_tile = _loss_head(xs[DEPTH], target)

    slab_rows = [None] * DEPTH
    partials = received = None
    in_flight = []

    def scatter(layer, grads, from_sibling, after):
        nonlocal partials, received
        partials = _add_sibling_blocks(
            f"grad_add_sibling_{layer}", layer, grads, from_sibling, core, partials, ALL_KINDS)
        chips = _chips_exchange(layer, partials, received, ALL_KINDS)
        sems, partials, received, token = _start_exchange(chips, f"grad_chips_start_{layer}", after)
        in_flight.append((chips, sems, layer))
        return token

    grads_above = None
    for l in reversed(range(DEPTH)):
        y, h_t, ycat_t, uc, fa, fp = kept[l]
        dx, dproj, dy, gwpool, dcps, dvec = _backward_layer(
            l, dx, y, xs[l], uc, fa, fp, vec, cps, wpool_b, wouts[l], wins[l])
        slab_rows[l] = jnp.concatenate(
            [dvec[0], dvec[1], dvec[2], dvec[3], dvec[4], dcps[3], dcps[0], dcps[1], dcps[2],
             loss_tile[0] if l == 0 else jnp.zeros((LANES,), F32)])
        after = []
        if l == 0:
            gather_small = _all_gather_exchange(jnp.stack(slab_rows))
            sems_s, slab, slabs, token = _start_exchange(gather_small, "all_gather_small_grads_start")
            after = [token]
        hosted = _sibling_exchange(grads_above, ALL_KINDS) if grads_above is not None else None
        gwin, gwout, *from_sibling = _weight_grads(l, h_t, dproj, ycat_t, dy, hosted, after)
        if l == 0:
            _, (slabs,) = _finish_exchange(
                gather_small, "all_gather_small_grads_finish", sems_s, slab, slabs, [gwin])
        if grads_above is not None:
            scatter(l + 1, grads_above, from_sibling, [])
        grads_above = [gwin, gwout, gwpool.reshape(POOL_ROWS, GROUP_D)]
        if l <= 1:
            from_sibling = _run_exchange(_sibling_exchange(grads_above, ALL_KINDS), f"grad_exchange_sibling_{l}")
            token = scatter(l, grads_above, from_sibling, [slabs] if l == 0 else [])
            grads_above = None
    grad_x = dx[None]
    chips_0, sems_0, _ = in_flight.pop()

    total = _sum_sources(slabs)
    loss = total[0, SLAB_COLS]
    o = 3 * D_MODEL
    g_b_ada = total[:, :o]
    g_g_pre = total[:, o:o + D_MODEL]
    g_g_post = total[:, o + D_MODEL:o + 2 * D_MODEL]
    g_pool_scale = total[:, o + 2 * D_MODEL:o + 2 * D_MODEL + POOL_W]
    g_conv_full = total[:, o + 2 * D_MODEL + POOL_W:SLAB_COLS].reshape(DEPTH, 3, CONV_W)
    g_w_conv = lax.dynamic_slice_in_dim(g_conv_full, me * conv_shard, conv_shard, axis=2)

    after = [token]
    for chips, sems, l in in_flight:
        partials, received = _finish_exchange(chips, f"grad_chips_finish_{l}", sems, partials, received, after)
        after = []
    upper = (1, DEPTH)
    w_in_upper = _adamw_reduced(
        "adamw_w_in_upper", w_in, m_w_in, v_w_in, partials[0], received[0], chip, ROW_TILE, upper, None)
    w_out_upper = _adamw_reduced(
        "adamw_w_out_upper", w_out, m_w_out, v_w_out, partials[1], received[1], chip, W_OUT_SHARD, upper, None)
    dmod_all = slabs[:, :, :o].reshape(N_DEV, DEPTH, N_DEV, W_IN_SHARD)
    dmod_cols = lax.dynamic_index_in_dim(dmod_all, me, axis=2, keepdims=False).transpose(1, 0, 2) + token[0, 0]
    g_w_ada, d_w_ada, nm_w_ada, nv_w_ada = _adamw_w_ada(w_ada, m_w_ada, v_w_ada, c_act.T, dmod_cols, None)

    partials, received = _finish_exchange(
        chips_0, "grad_chips_finish_0", sems_0, partials, received, [nv_w_ada, w_in_upper[3], w_out_upper[3]])
    gather_pool = _all_gather_exchange(_reduce_w_pool(partials[2], received[2], chip))
    sems_p, pool_rows, pool_landing, token_p = _start_exchange(gather_pool, "all_gather_grad_w_pool_start")
    g_w_in, d_w_in, nm_w_in, nv_w_in = _adamw_reduced(
        "adamw_w_in_0", w_in, m_w_in, v_w_in, partials[0], received[0], chip, ROW_TILE, (0, 1), w_in_upper)
    g_w_out, d_w_out, nm_w_out, nv_w_out = _adamw_reduced(
        "adamw_w_out_0", w_out, m_w_out, v_w_out, partials[1], received[1], chip, W_OUT_SHARD, (0, 1), w_out_upper)
    _, (g_pool_all,) = _finish_exchange(
        gather_pool, "all_gather_grad_w_pool_finish", sems_p, pool_rows, pool_landing, [nv_w_in, nv_w_out])
    g_w_pool = g_pool_all.reshape(N_DEV, DEPTH, POOL_SHARD, GROUP_D).transpose(1, 0, 2, 3).reshape(w_pool.shape)

    flat2 = lambda a: a.reshape(-1, a.shape[-1])
    small = _adamw_small([
        (b_ada, g_b_ada, m_b_ada, v_b_ada),
        (g_pre, g_g_pre, m_g_pre, v_g_pre),
        (flat2(w_conv), flat2(g_w_conv), flat2(m_w_conv), flat2(v_w_conv)),
        (flat2(w_pool), flat2(g_w_pool), flat2(m_w_pool), flat2(v_w_pool)),
        (pool_scale, g_pool_scale, m_pool_scale, v_pool_scale),
        (g_post, g_g_post, m_g_post, v_g_post),
    ])
    (d_b_ada, nm_b_ada, nv_b_ada), (d_g_pre, nm_g_pre, nv_g_pre), conv_upd, pool_upd, \
        (d_ps, nm_ps, nv_ps), (d_g_post, nm_g_post, nv_g_post) = small
    d_w_conv, nm_w_conv, nv_w_conv = (a.reshape(w_conv.shape) for a in conv_upd)
    d_w_pool, nm_w_pool, nv_w_pool = (a.reshape(w_pool.shape) for a in pool_upd)

    return (loss, grad_x,
            g_w_ada, g_b_ada, g_g_pre, g_w_in, g_w_conv, g_w_pool, g_pool_scale, g_w_out, g_g_post,
            d_w_ada, d_b_ada, d_g_pre, d_w_in, d_w_conv, d_w_pool, d_ps, d_w_out, d_g_post,
            nm_w_ada, nm_b_ada, nm_g_pre, nm_w_in, nm_w_conv, nm_w_pool, nm_ps, nm_w_out, nm_g_post,
            nv_w_ada, nv_b_ada, nv_g_pre, nv_w_in, nv_w_conv, nv_w_pool, nv_ps, nv_w_out, nv_g_post)
```

```python
import jax
import jax.numpy as jnp
from jax import lax
from jax.experimental import pallas as pl
from jax.experimental.pallas import tpu as pltpu

F32 = jnp.float32
BF16 = jnp.bfloat16

D_MODEL = 1024
DEPTH = 4
CONV_W = 512
POOL_W = 512
POOL_WINDOWS = (2, 4, 8, 16)
GROUP_D = 128
IN_COLS = 4 * CONV_W + 2 * POOL_W
NORM_EPS = 1e-6

ADAM_LR = 0.001
ADAM_B1 = 0.9
ADAM_B2 = 0.999
ADAM_EPS = 1e-08
ADAM_WD = 0.01
ADAM_STEP = 10

N_DEV = 8
N_CHIP = 4
N_OTHER_CHIPS = N_CHIP - 1
MESH = pl.DeviceIdType.MESH
W_IN_SHARD = IN_COLS // N_DEV
W_OUT_SHARD = D_MODEL // N_DEV
POOL_ROWS = len(POOL_WINDOWS) * GROUP_D
POOL_SHARD = POOL_ROWS // N_DEV

SUBLANES = 8
LANES = 128
VMEM_LIMIT_BYTES = 56 * 1024 * 1024
ROW_TILE = 512
BWD_TILE = 256
GWIN_COLS = 768
GWOUT_COLS = 512
POOL_HALO = 16
CONV_HALO = SUBLANES

SLAB_COLS = 3 * D_MODEL + D_MODEL + D_MODEL + POOL_W + 3 * CONV_W

HBM = pl.BlockSpec(memory_space=pl.ANY)


def _params(**kw):
    return pltpu.CompilerParams(vmem_limit_bytes=VMEM_LIMIT_BYTES, **kw)


def _sigmoid(v):
    return 1.0 / (1.0 + jnp.exp(-v))


def _dot(a, b):
    return jnp.dot(a, b, preferred_element_type=F32)


def _dot_tn(a, b):
    return lax.dot_general(a, b, (((0,), (0,)), ((), ())), preferred_element_type=F32)


def _dot_nt(a, b):
    return lax.dot_general(a, b, (((1,), (1,)), ((), ())), preferred_element_type=F32)


def _rows_from_before(v, k):
    return pltpu.roll(v, k, 0)


def _rows_from_after(v, k):
    return pltpu.roll(v, v.shape[0] - k, 0)


def _window_counts(t0, rows):
    return (lax.broadcasted_iota(jnp.int32, (rows, 1), 0) + (t0 + 1)).astype(F32)


def _split_proj(p32):
    cw = CONV_W
    return (p32[:, 0 * cw:1 * cw], p32[:, 1 * cw:2 * cw], p32[:, 2 * cw:3 * cw], p32[:, 3 * cw:4 * cw],
            p32[:, 4 * cw:4 * cw + POOL_W], p32[:, 4 * cw + POOL_W:])


def _layer_spec(shape, layer):
    nd = len(shape)
    return pl.BlockSpec((None,) + tuple(shape[1:]), lambda i, _l=layer, _n=nd: (_l,) + (0,) * (_n - 1))


def _whole_spec(shape):
    return pl.BlockSpec(tuple(shape), lambda i, _n=len(shape): (0,) * _n, pipeline_mode=pl.Buffered(1))


def _mesh_position():
    return lax.axis_index("x"), lax.axis_index("y"), lax.axis_index("c")


def _block_id(x, y, c):
    return 4 * x + 2 * y + c


def _other_chips(x, y):
    return [(x ^ 1, y), (x, y ^ 1), (x ^ 1, y ^ 1)]


def _col_block(ref, blk):
    return ref.at[:, pl.ds(pl.multiple_of(blk * W_IN_SHARD, LANES), W_IN_SHARD)]


def _row_block(rows):
    def block(ref, blk):
        return ref.at[pl.ds(pl.multiple_of(blk * rows, rows), rows), :]
    return block


_BLOCK_OF = (_col_block, _row_block(W_OUT_SHARD), _row_block(POOL_SHARD))
_BLOCK_SHAPES = ((D_MODEL, W_IN_SHARD), (W_OUT_SHARD, D_MODEL), (POOL_SHARD, GROUP_D))


class _Exchange:
    def __init__(self, inputs, out_shapes, aliases, sem_shapes, make):
        self.inputs, self.out_shapes, self.aliases, self.sem_shapes, self.make = (
            list(inputs), list(out_shapes), dict(aliases), list(sem_shapes), make)


def _run_exchange(exchange, name):
    n_in, n_out = len(exchange.inputs), len(exchange.out_shapes)

    def body(*refs):
        start, finish = exchange.make(refs[:n_in], refs[n_in:n_in + n_out], refs[n_in + n_out:])
        start()
        finish()

    return pl.pallas_call(
        body, name=name, in_specs=[HBM] * n_in, out_specs=[HBM] * n_out, out_shape=exchange.out_shapes,
        scratch_shapes=exchange.sem_shapes, input_output_aliases=exchange.aliases, compiler_params=_params(),
    )(*exchange.inputs)


_SEM = pl.BlockSpec(memory_space=pltpu.SEMAPHORE)
_DATAFLOW = pltpu.SideEffectType.DATAFLOW_SIDE_EFFECTING


def _start_exchange(exchange, name, after=()):
    n_in, n_out, n_sem = len(exchange.inputs), len(exchange.out_shapes), len(exchange.sem_shapes)
    sources = [i for i in range(n_in) if i not in exchange.aliases]
    aliases = {i: n_sem + k for k, i in enumerate(sources)}
    aliases.update({i: n_sem + len(sources) + o for i, o in exchange.aliases.items()})

    def body(*refs):
        in_refs = refs[:n_in]
        outs = refs[n_in + len(after):]
        sems = outs[:n_sem]
        out_refs = outs[n_sem + len(sources):n_sem + len(sources) + n_out]
        exchange.make(in_refs, out_refs, sems)[0]()
        refs[-1][...] = jnp.zeros_like(refs[-1])

    outs = pl.pallas_call(
        body, name=name, in_specs=[HBM] * (n_in + len(after)),
        out_specs=[_SEM] * n_sem + [HBM] * (len(sources) + n_out) + [pl.BlockSpec(memory_space=pltpu.VMEM)],
        out_shape=(exchange.sem_shapes + [pltpu.HBM(exchange.inputs[i].shape, exchange.inputs[i].dtype) for i in sources]
                   + [pltpu.HBM(s.shape, s.dtype) for s in exchange.out_shapes]
                   + [jax.ShapeDtypeStruct((SUBLANES, LANES), F32)]),
        input_output_aliases=aliases, compiler_params=_params(has_side_effects=_DATAFLOW),
    )(*exchange.inputs, *after)
    return outs[:n_sem], outs[n_sem:n_sem + len(sources)], outs[n_sem + len(sources):-1], outs[-1]


def _finish_exchange(exchange, name, sems, sources, landing, after):
    n_src, n_out, n_sem = len(sources), len(landing), len(sems)
    n_in = len(exchange.inputs)
    source_at = [i for i in range(n_in) if i not in exchange.aliases]

    def body(*refs):
        src_refs, out_refs = refs[:n_src], refs[n_src:n_src + n_out]
        sem_refs = refs[n_src + n_out:n_src + n_out + n_sem]
        in_refs = [None] * n_in
        for k, i in enumerate(source_at):
            in_refs[i] = src_refs[k]
        for i, o in exchange.aliases.items():
            in_refs[i] = out_refs[o]
        exchange.make(in_refs, out_refs, sem_refs)[1]()

    arrays = list(sources) + list(landing)
    outs = pl.pallas_call(
        body, name=name, in_specs=[HBM] * len(arrays) + [_SEM] * n_sem + [HBM] * len(after),
        out_specs=[HBM] * len(arrays), out_shape=[pltpu.HBM(a.shape, a.dtype) for a in arrays],
        input_output_aliases={i: i for i in range(len(arrays))}, compiler_params=_params(has_side_effects=_DATAFLOW),
    )(*arrays, *sems, *after)
    return outs[:n_src], outs[n_src:]


N_GATHERED = 2
FIRST_COPIES = 1 + N_OTHER_CHIPS
_GATHERED_SHAPES = ((D_MODEL, IN_COLS), (D_MODEL, D_MODEL))


def _first_sem(layer, a, k):
    return (layer * N_GATHERED + a) * FIRST_COPIES + k


def _gather_copy(window_of, full_ref, blk, send_sem, recv_sem, to, src=None):
    window = window_of(full_ref, blk)
    return pltpu.make_async_remote_copy(
        src_ref=window if src is None else src, dst_ref=window, send_sem=send_sem, recv_sem=recv_sem,
        device_id=to, device_id_type=MESH)


def _first_copies(layer, shard_refs, full_refs, send_sems, recv_sems, local_sems):
    x, y, c = _mesh_position()
    me = _block_id(x, y, c)
    own, remote = [], []
    for a in range(N_GATHERED):
        shard = shard_refs[a].at[layer]
        own.append(pltpu.make_async_copy(
            shard, _BLOCK_OF[a](full_refs[a], me), local_sems.at[layer * N_GATHERED + a]))
        targets = [(x, y, 1 - c)] + [(*chip, c) for chip in _other_chips(x, y)]
        remote += [_gather_copy(_BLOCK_OF[a], full_refs[a], me, send_sems.at[_first_sem(layer, a, k)],
                                recv_sems.at[_first_sem(layer, a, k)], to, src=shard)
                   for k, to in enumerate(targets)]
    return own, remote


def _gather_weights_start(first_layer, win_shards, wout_shards, after):
    n_layers = win_shards.shape[0]
    n_first = n_layers * N_GATHERED * FIRST_COPIES
    sem_shapes = [pltpu.SemaphoreType.DMA((n_first,)), pltpu.SemaphoreType.DMA((n_first,)),
                  pltpu.SemaphoreType.DMA((n_layers * N_GATHERED,))]
    shards = [win_shards, wout_shards]

    def body(win_sh, wout_sh, *rest):
        send_sems, recv_sems, local_sems, win_thru, wout_thru, *landing = rest[len(after):]
        for layer in range(n_layers):
            own, remote = _first_copies(layer, (win_sh, wout_sh), landing[N_GATHERED * layer:N_GATHERED * (layer + 1)],
                                        send_sems, recv_sems, local_sems)
            for cp in own + remote:
                cp.start()

    outs = pl.pallas_call(
        body, name=f"all_gather_weights_start_{first_layer}", in_specs=[HBM] * (2 + len(after)),
        out_specs=[_SEM] * 3 + [HBM] * (2 + n_layers * N_GATHERED),
        out_shape=(sem_shapes + [pltpu.HBM(s.shape, s.dtype) for s in shards]
                   + [pltpu.HBM(s, BF16) for _ in range(n_layers) for s in _GATHERED_SHAPES]),
        input_output_aliases={0: 3, 1: 4}, compiler_params=_params(has_side_effects=_DATAFLOW),
    )(*shards, *after)
    landing = outs[5:]
    return outs[:3], outs[3:5], [landing[N_GATHERED * l:N_GATHERED * (l + 1)] for l in range(n_layers)]


def _passed_on_copies(full_refs, send_sems, recv_sems, core_of_block):
    x, y, c = _mesh_position()
    return [_gather_copy(_BLOCK_OF[a], full_refs[a], _block_id(*chip, core_of_block),
                         send_sems.at[a * N_OTHER_CHIPS + j], recv_sems.at[a * N_OTHER_CHIPS + j], (x, y, 1 - c))
            for a in range(N_GATHERED) for j, chip in enumerate(_other_chips(x, y))]


def _gather_weights_pass_on(layer, index, first_recv_sems, landing, after):
    n = N_GATHERED * N_OTHER_CHIPS

    def body(win_ref, wout_ref, first_recv, *rest):
        send_sems, recv_sems = rest[len(after):len(after) + 2]
        x, y, c = _mesh_position()
        full_refs = (win_ref, wout_ref)
        passed = _passed_on_copies(full_refs, send_sems, recv_sems, c)
        for a in range(N_GATHERED):
            for j, chip in enumerate(_other_chips(x, y)):
                sem = _first_sem(index, a, 1 + j)
                _gather_copy(_BLOCK_OF[a], full_refs[a], _block_id(*chip, c), first_recv.at[sem], first_recv.at[sem],
                             (x, y, c)).wait_recv()
                passed[a * N_OTHER_CHIPS + j].start()

    outs = pl.pallas_call(
        body, name=f"all_gather_weights_pass_on_{layer}", in_specs=[HBM] * N_GATHERED + [_SEM] + [HBM] * len(after),
        out_specs=[_SEM] * 2 + [HBM] * N_GATHERED,
        out_shape=[pltpu.SemaphoreType.DMA((n,)), pltpu.SemaphoreType.DMA((n,))]
        + [pltpu.HBM(a.shape, a.dtype) for a in landing],
        input_output_aliases={a: 2 + a for a in range(N_GATHERED)},
        compiler_params=_params(has_side_effects=_DATAFLOW),
    )(*landing, first_recv_sems, *after)
    return outs[:2], outs[2:]


def _gather_weights_finish(layer, index, first_sems, passed_sems, shards, landing):
    def body(win_ref, wout_ref, first_send, first_recv, local_sems, passed_send, passed_recv, win_sh, wout_sh, *thru):
        x, y, c = _mesh_position()
        full_refs = (win_ref, wout_ref)
        own, remote = _first_copies(index, (win_sh, wout_sh), full_refs, first_send, first_recv, local_sems)
        for a in range(N_GATHERED):
            sem = _first_sem(index, a, 0)
            _gather_copy(_BLOCK_OF[a], full_refs[a], _block_id(x, y, 1 - c), first_recv.at[sem], first_recv.at[sem],
                         (x, y, c)).wait_recv()
        for cp in _passed_on_copies(full_refs, passed_send, passed_recv, 1 - c):
            cp.wait_recv()
        for cp in remote + _passed_on_copies(full_refs, passed_send, passed_recv, c):
            cp.wait_send()
        for cp in own:
            cp.wait()

    return pl.pallas_call(
        body, name=f"all_gather_weights_finish_{layer}", in_specs=[HBM] * N_GATHERED + [_SEM] * 5 + [HBM] * 2,
        out_specs=[HBM] * N_GATHERED, out_shape=[pltpu.HBM(a.shape, a.dtype) for a in landing],
        input_output_aliases={a: a for a in range(N_GATHERED)}, compiler_params=_params(has_side_effects=_DATAFLOW),
    )(*landing, *first_sems, *passed_sems, *shards)


ALL_KINDS = (0, 1, 2)


def _sibling_exchange(grads, kinds):
    n_arr = len(grads)

    def make(in_refs, out_refs, sems):
        send_sems, recv_sems = sems
        x, y, c = _mesh_position()
        copies = [pltpu.make_async_remote_copy(
            src_ref=_BLOCK_OF[kinds[a]](in_refs[a], 2 * q + (1 - c)), dst_ref=out_refs[a].at[q],
            send_sem=send_sems.at[a, q], recv_sem=recv_sems.at[a, q], device_id=(x, y, 1 - c), device_id_type=MESH)
            for a in range(n_arr) for q in range(N_CHIP)]

        def start():
            for cp in copies:
                cp.start()

        def finish():
            for cp in copies:
                cp.wait_recv()
            for cp in copies:
                cp.wait_send()

        return start, finish

    return _Exchange(
        grads, [jax.ShapeDtypeStruct((N_CHIP,) + _BLOCK_SHAPES[k], BF16) for k in kinds], {},
        [pltpu.SemaphoreType.DMA((n_arr, N_CHIP)), pltpu.SemaphoreType.DMA((n_arr, N_CHIP))], make)


def _chips_exchange(layer, partials, received, kinds):
    n_arr = len(partials)

    def make(in_refs, out_refs, sems):
        send_sems, recv_sems = sems
        x, y, c = _mesh_position()
        copies = [pltpu.make_async_remote_copy(
            src_ref=in_refs[a].at[2 * qx + qy, layer], dst_ref=out_refs[a].at[j, layer],
            send_sem=send_sems.at[a * N_OTHER_CHIPS + j], recv_sem=recv_sems.at[a * N_OTHER_CHIPS + j],
            device_id=(qx, qy, c), device_id_type=MESH)
            for a in range(n_arr) for j, (qx, qy) in enumerate(_other_chips(x, y))]

        def start():
            for cp in copies:
                cp.start()

        def finish():
            for cp in copies:
                cp.wait_recv()
            for cp in copies:
                cp.wait_send()

        return start, finish

    inputs = list(partials)
    aliases = {}
    if received is not None:
        inputs += list(received)
        aliases = {n_arr + a: a for a in range(n_arr)}
    return _Exchange(
        inputs, [jax.ShapeDtypeStruct((N_OTHER_CHIPS, DEPTH) + _BLOCK_SHAPES[k], BF16) for k in kinds], aliases,
        [pltpu.SemaphoreType.DMA((n_arr * N_OTHER_CHIPS,)), pltpu.SemaphoreType.DMA((n_arr * N_OTHER_CHIPS,))], make)


def _all_gather_exchange(v):
    def make(in_refs, out_refs, sems):
        send_sems, recv_sems, local_sem = sems
        x, y, c = _mesh_position()
        me = _block_id(x, y, c)
        own = pltpu.make_async_copy(in_refs[0], out_refs[0].at[me], local_sem.at[0])
        sends, arrivals = [], []
        for k in range(1, N_DEV):
            px, py, pc = x ^ ((k >> 2) & 1), y ^ ((k >> 1) & 1), c ^ (k & 1)
            sends.append(pltpu.make_async_remote_copy(
                src_ref=in_refs[0], dst_ref=out_refs[0].at[me], send_sem=send_sems.at[k - 1],
                recv_sem=recv_sems.at[k - 1], device_id=(px, py, pc), device_id_type=MESH))
            arrivals.append(pltpu.make_async_remote_copy(
                src_ref=in_refs[0], dst_ref=out_refs[0].at[_block_id(px, py, pc)], send_sem=send_sems.at[k - 1],
                recv_sem=recv_sems.at[k - 1], device_id=(x, y, c), device_id_type=MESH))

        def start():
            for cp in [own] + sends:
                cp.start()

        def finish():
            for cp in arrivals:
                cp.wait_recv()
            for cp in sends:
                cp.wait_send()
            own.wait()

        return start, finish

    return _Exchange(
        [v], [jax.ShapeDtypeStruct((N_DEV,) + v.shape, v.dtype)], {},
        [pltpu.SemaphoreType.DMA((N_DEV - 1,)), pltpu.SemaphoreType.DMA((N_DEV - 1,)),
         pltpu.SemaphoreType.DMA((1,))], make)


def _host(exchange, args, in_specs, out_shape, out_specs, scratch):
    n_own = (len(args), len(out_shape), len(scratch))
    if exchange is None:
        return {}, lambda refs: (refs, None)
    n_ex = (len(exchange.inputs), len(exchange.out_shapes), len(exchange.sem_shapes))
    aliases = {n_own[0] + i: n_own[1] + o for i, o in exchange.aliases.items()}
    args += exchange.inputs
    in_specs += [HBM] * n_ex[0]
    out_shape += exchange.out_shapes
    out_specs += [HBM] * n_ex[1]
    scratch += exchange.sem_shapes

    def split(refs):
        own, theirs, at = [], [], 0
        for mine, ex in zip(n_own, n_ex):
            own += refs[at:at + mine]
            theirs.append(refs[at + mine:at + mine + ex])
            at += mine + ex
        return own, exchange.make(*theirs)

    return aliases, split


def _all_gather_small(v, name, after=()):
    vmem = pl.BlockSpec(memory_space=pltpu.VMEM)

    def body(v_ref, *rest):
        out_ref, send_sems, recv_sems = rest[len(after):]
        x, y, c = _mesh_position()
        me = _block_id(x, y, c)
        out_ref[me] = v_ref[...]
        sends = []
        for k in range(1, N_DEV):
            px, py, pc = x ^ ((k >> 2) & 1), y ^ ((k >> 1) & 1), c ^ (k & 1)
            send = pltpu.make_async_remote_copy(
                src_ref=v_ref, dst_ref=out_ref.at[me], send_sem=send_sems.at[k - 1], recv_sem=recv_sems.at[k - 1],
                device_id=(px, py, pc), device_id_type=MESH)
            send.start()
            sends.append((send, _block_id(px, py, pc)))
        for k, (send, peer) in enumerate(sends):
            pltpu.make_async_remote_copy(
                src_ref=v_ref, dst_ref=out_ref.at[peer], send_sem=send_sems.at[k], recv_sem=recv_sems.at[k],
                device_id=(x, y, c), device_id_type=MESH).wait_recv()
        for send, _ in sends:
            send.wait_send()

    return pl.pallas_call(
        body, name=name, in_specs=[vmem] + [HBM] * len(after), out_specs=vmem,
        out_shape=jax.ShapeDtypeStruct((N_DEV,) + v.shape, v.dtype),
        scratch_shapes=[pltpu.SemaphoreType.DMA((N_DEV - 1,)), pltpu.SemaphoreType.DMA((N_DEV - 1,))],
        compiler_params=_params(),
    )(v, *after)


def _forward_layer(layer, x, vec, cps, wpool, win, wout, target=None):
    t_len = x.shape[0]
    n_tiles = t_len // ROW_TILE
    row = lambda cols: pl.BlockSpec((ROW_TILE, cols), lambda i: (i, 0))
    col_t = pl.BlockSpec((D_MODEL, ROW_TILE), lambda i: (0, i))
    widths = (D_MODEL, 2 * CONV_W, 3 * CONV_W, 4 * POOL_W)
    head = target is not None

    def body(x_ref, vec_ref, cps_ref, wpool_ref, win_ref, wout_ref, *rest):
        target_ref = rest[0] if head else None
        xo_ref, y_ref, ht_ref, ycatt_ref, uc_ref, fa_ref, fp_ref = rest[head:head + 7]
        loss_ref = rest[head + 7] if head else None
        zc_ref, pc_ref = rest[-2:]
        i = pl.program_id(0)

        @pl.when(i == 0)
        def _():
            zc_ref[...] = jnp.zeros_like(zc_ref)
            pc_ref[...] = jnp.zeros_like(pc_ref)
            if head:
                loss_ref[...] = jnp.zeros_like(loss_ref)

        x_t = x_ref[...]
        shift, scale, gate = vec_ref[0:1, :], vec_ref[1:2, :], vec_ref[2:3, :]
        g_pre, g_post = vec_ref[3:4, :], vec_ref[4:5, :]
        w0, w1, w2, ps = cps_ref[0:1, :], cps_ref[1:2, :], cps_ref[2:3, :], cps_ref[3:4, :]
        rx = lax.rsqrt(jnp.mean(x_t * x_t, axis=-1, keepdims=True) + NORM_EPS)
        h = (x_t * rx) * g_pre * (1.0 + scale) + shift
        ht_ref[...] = h.T.astype(BF16)
        proj = _dot(h.astype(BF16), win_ref[...]).astype(BF16).astype(F32)
        u_a, b_a, c_a, g_a, u_p, g_p = _split_proj(proj)
        uc_ref[...] = jnp.concatenate([u_a, c_a], axis=1).astype(BF16)

        z = c_a * u_a
        zcat = jnp.concatenate([zc_ref[...], z], axis=0)
        zc_ref[...] = z[ROW_TILE - CONV_HALO:]
        conv = (w0 * _rows_from_before(zcat, 2)[CONV_HALO:] + w1 * _rows_from_before(zcat, 1)[CONV_HALO:] + w2 * z)
        sig_a = _sigmoid(g_a)
        silu_a = g_a * sig_a
        b_conv = b_a * conv
        y_a = b_conv * silu_a
        fa_ref[...] = jnp.concatenate(
            [silu_a * conv, silu_a * b_a, b_conv * (sig_a * (1.0 + g_a * (1.0 - sig_a)))], axis=1).astype(BF16)

        pcat = jnp.concatenate([pc_ref[...], u_p], axis=0)
        pc_ref[...] = u_p[ROW_TILE - POOL_HALO:]
        counts = _window_counts(i * ROW_TILE, ROW_TILE)
        pooled, mixed = [], []
        for g, w in enumerate(POOL_WINDOWS):
            cols = slice(g * GROUP_D, (g + 1) * GROUP_D)
            s = pcat[:, cols]
            step = 1
            while step < w:
                s = s + _rows_from_before(s, step)
                step *= 2
            pooled_g = (s[POOL_HALO:] * (1.0 / jnp.minimum(counts, float(w))) - u_p[:, cols]).astype(BF16)
            pooled.append(pooled_g)
            mixed.append(_dot(pooled_g, wpool_ref[g]))
        mixed = jnp.concatenate(mixed, axis=1)
        sig_p = _sigmoid(g_p)
        silu_p = g_p * sig_p
        mixed_ps = mixed * ps
        y_p = mixed_ps * silu_p
        fp_ref[...] = jnp.concatenate(
            [(ps * silu_p).astype(BF16), (mixed_ps * (sig_p * (1.0 + g_p * (1.0 - sig_p)))).astype(BF16),
             (silu_p * mixed).astype(BF16)] + pooled, axis=1)

        ycat = jnp.concatenate([y_a, y_p], axis=1)
        ycatt_ref[...] = ycat.T.astype(BF16)
        y_b = _dot(ycat.astype(BF16), wout_ref[...]).astype(BF16)
        y_ref[...] = y_b
        y_t = y_b.astype(F32)
        ry = lax.rsqrt(jnp.mean(y_t * y_t, axis=-1, keepdims=True) + NORM_EPS)
        x_next = x_t + gate * (y_t * ry * g_post)
        if head:
            err = x_next - target_ref[...]
            xo_ref[...] = err * (1.0 / D_MODEL)
            loss_ref[...] += jnp.sum(err * err) * (0.5 / D_MODEL)
        else:
            xo_ref[...] = x_next

    tile = (SUBLANES, LANES)
    return pl.pallas_call(
        body, name=f"forward_layer_{layer}", grid=(n_tiles,),
        in_specs=[row(D_MODEL), _layer_spec(vec.shape, layer), _layer_spec(cps.shape, layer),
                  _layer_spec(wpool.shape, layer), _whole_spec(win.shape), _whole_spec(wout.shape)]
        + [row(D_MODEL)] * head,
        out_specs=[row(D_MODEL), row(D_MODEL), col_t, col_t] + [row(w) for w in widths[1:]] + [_whole_spec(tile)] * head,
        out_shape=[jax.ShapeDtypeStruct((t_len, D_MODEL), F32), jax.ShapeDtypeStruct((t_len, D_MODEL), BF16),
                   jax.ShapeDtypeStruct((D_MODEL, t_len), BF16), jax.ShapeDtypeStruct((D_MODEL, t_len), BF16)]
        + [jax.ShapeDtypeStruct((t_len, w), BF16) for w in widths[1:]] + [jax.ShapeDtypeStruct(tile, F32)] * head,
        scratch_shapes=[pltpu.VMEM((CONV_HALO, CONV_W), F32), pltpu.VMEM((POOL_HALO, POOL_W), F32)],
        compiler_params=_params(dimension_semantics=("arbitrary",)),
    )(x, vec, cps, wpool, win, wout, *([target] * head))


def _backward_layer(layer, dxo, y, x, uc, fa, fp, vec, cps, wpool, wout, win):
    t_len = dxo.shape[0]
    n_tiles = t_len // BWD_TILE
    halo_per_tile = BWD_TILE // POOL_HALO
    rev = lambda cols: pl.BlockSpec((BWD_TILE, cols), lambda i: (n_tiles - 1 - i, 0))
    halo_spec = pl.BlockSpec(
        (POOL_HALO, 2 * CONV_W), lambda i: (jnp.maximum((n_tiles - 1 - i) * halo_per_tile - 1, 0), 0))
    gwpool_shape = (len(POOL_WINDOWS), GROUP_D, GROUP_D)

    def body(dxo_ref, y_ref, x_ref, uc_ref, uch_ref, fa_ref, fp_ref, vec_ref, cps_ref, wpool_ref, wout_ref, win_ref,
             dx_ref, dproj_ref, dy_ref, gwpool_ref, dcps_ref, dvec_ref, gwpool_acc, dcc_ref, qc_ref):
        i = pl.program_id(0)
        tile = n_tiles - 1 - i

        @pl.when(i == 0)
        def _():
            gwpool_acc[...] = jnp.zeros_like(gwpool_acc)
            dcps_ref[...] = jnp.zeros_like(dcps_ref)
            dvec_ref[...] = jnp.zeros_like(dvec_ref)
            dcc_ref[...] = jnp.zeros_like(dcc_ref)
            qc_ref[...] = jnp.zeros_like(qc_ref)

        shift, scale, gate = vec_ref[0:1, :], vec_ref[1:2, :], vec_ref[2:3, :]
        g_pre, g_post = vec_ref[3:4, :], vec_ref[4:5, :]
        w0, w1, w2 = cps_ref[0:1, :], cps_ref[1:2, :], cps_ref[2:3, :]

        dxo_t = dxo_ref[...]
        y_t = y_ref[...].astype(F32)
        ry = lax.rsqrt(jnp.mean(y_t * y_t, axis=-1, keepdims=True) + NORM_EPS)
        yh = y_t * ry
        dvec_ref[2:3, :] += jnp.sum(dxo_t * yh, axis=0, keepdims=True)
        dyh = dxo_t * (gate * g_post)
        dy_b = (ry * (dyh - yh * jnp.mean(dyh * yh, axis=-1, keepdims=True))).astype(BF16)
        dy_ref[...] = dy_b
        dycat = _dot_nt(dy_b, wout_ref[...])
        dy_a, dy_p = dycat[:, :CONV_W], dycat[:, CONV_W:]

        fa_t = fa_ref[...].astype(F32)
        db_a = dy_a * fa_t[:, :CONV_W]
        dconv = dy_a * fa_t[:, CONV_W:2 * CONV_W]
        dg_a = dy_a * fa_t[:, 2 * CONV_W:]
        uc_t = uc_ref[...].astype(F32)
        u_a, c_a = uc_t[:, :CONV_W], uc_t[:, CONV_W:]
        halo = jnp.where(tile > 0, uch_ref[...].astype(F32), 0.0)[POOL_HALO - CONV_HALO:]
        z = c_a * u_a
        zcat = jnp.concatenate([halo[:, CONV_W:] * halo[:, :CONV_W], z], axis=0)
        z1 = _rows_from_before(zcat, 1)[CONV_HALO:]
        z2 = _rows_from_before(zcat, 2)[CONV_HALO:]
        dccat = jnp.concatenate([dconv, dcc_ref[...]], axis=0)
        dc1 = _rows_from_after(dccat, 1)[:BWD_TILE]
        dc2 = _rows_from_after(dccat, 2)[:BWD_TILE]
        dz = w2 * dconv + w1 * dc1 + w0 * dc2
        dcc_ref[...] = dconv[:CONV_HALO]
        dcps_ref[0:1, :] += jnp.sum(dconv * z2, axis=0, keepdims=True)
        dcps_ref[1:2, :] += jnp.sum(dconv * z1, axis=0, keepdims=True)
        dcps_ref[2:3, :] += jnp.sum(dconv * z, axis=0, keepdims=True)
        du_a = dz * c_a
        dc_a = dz * u_a

        dmixed = (dy_p * fp_ref[:, :POOL_W].astype(F32)).astype(BF16)
        dg_p = dy_p * fp_ref[:, POOL_W:2 * POOL_W].astype(F32)
        dcps_ref[3:4, :] += jnp.sum(dy_p * fp_ref[:, 2 * POOL_W:3 * POOL_W].astype(F32), axis=0, keepdims=True)
        counts = _window_counts(tile * BWD_TILE, BWD_TILE)
        du_p, q_head = [], []
        for g, w in enumerate(POOL_WINDOWS):
            cols = slice(g * GROUP_D, (g + 1) * GROUP_D)
            dm_g = dmixed[:, cols]
            dpooled_g = _dot_nt(dm_g, wpool_ref[g])
            gwpool_acc[g] += _dot_tn(fp_ref[:, 3 * POOL_W + g * GROUP_D:3 * POOL_W + (g + 1) * GROUP_D], dm_g)
            q_g = dpooled_g * (1.0 / jnp.minimum(counts, float(w)))
            q_head.append(q_g[:POOL_HALO])
            s = jnp.concatenate([q_g, qc_ref[:, cols]], axis=0)
            step = 1
            while step < w:
                s = s + _rows_from_after(s, step)
                step *= 2
            du_p.append(s[:BWD_TILE] - dpooled_g)
        qc_ref[...] = jnp.concatenate(q_head, axis=1)
        dproj_b = jnp.concatenate([du_a, db_a, dc_a, dg_a] + du_p + [dg_p], axis=1).astype(BF16)
        dproj_ref[...] = dproj_b

        x_t = x_ref[...]
        rx = lax.rsqrt(jnp.mean(x_t * x_t, axis=-1, keepdims=True) + NORM_EPS)
        xn = x_t * rx
        mod_scale = 1.0 + scale
        dh = _dot_nt(dproj_b, win_ref[...])
        dvec_ref[0:1, :] += jnp.sum(dh, axis=0, keepdims=True)
        dvec_ref[1:2, :] += jnp.sum(dh * xn, axis=0, keepdims=True)
        dxn = dh * (g_pre * mod_scale)
        dx_ref[...] = dxo_t + rx * (dxn - xn * jnp.mean(dxn * xn, axis=-1, keepdims=True))

        @pl.when(i == n_tiles - 1)
        def _():
            gwpool_ref[...] = gwpool_acc[...].astype(BF16)
            sum_dh_xn, sum_dxo_yh = dvec_ref[1:2, :], dvec_ref[2:3, :]
            dvec_ref[1:2, :] = sum_dh_xn * g_pre
            dvec_ref[3:4, :] = sum_dh_xn * mod_scale
            dvec_ref[2:3, :] = sum_dxo_yh * g_post
            dvec_ref[4:5, :] = sum_dxo_yh * gate

    return pl.pallas_call(
        body, name=f"backward_layer_{layer}", grid=(n_tiles,),
        in_specs=[rev(D_MODEL), rev(D_MODEL), rev(D_MODEL), rev(2 * CONV_W), halo_spec, rev(3 * CONV_W),
                  rev(4 * POOL_W), _layer_spec(vec.shape, layer), _layer_spec(cps.shape, layer),
                  _layer_spec(wpool.shape, layer), _whole_spec(wout.shape), _whole_spec(win.shape)],
        out_specs=[rev(D_MODEL), rev(IN_COLS), rev(D_MODEL), _whole_spec(gwpool_shape),
                   _whole_spec((SUBLANES, CONV_W)), _whole_spec((SUBLANES, D_MODEL))],
        out_shape=[jax.ShapeDtypeStruct((t_len, D_MODEL), F32), jax.ShapeDtypeStruct((t_len, IN_COLS), BF16),
                   jax.ShapeDtypeStruct((t_len, D_MODEL), BF16), jax.ShapeDtypeStruct(gwpool_shape, BF16),
                   jax.ShapeDtypeStruct((SUBLANES, CONV_W), F32), jax.ShapeDtypeStruct((SUBLANES, D_MODEL), F32)],
        scratch_shapes=[pltpu.VMEM(gwpool_shape, F32), pltpu.VMEM((CONV_HALO, CONV_W), F32),
                        pltpu.VMEM((POOL_HALO, POOL_W), F32)],
        compiler_params=_params(dimension_semantics=("arbitrary",)),
    )(dxo, y, x, uc, uc, fa, fp, vec, cps, wpool, wout, win)


def _weight_grads(layer, h_t, dproj, ycat_t, dy, exchange, after=()):
    t_len = dy.shape[0]
    n_in, n_out = IN_COLS // GWIN_COLS, D_MODEL // GWOUT_COLS
    args = [h_t, dproj, ycat_t, dy, *after]
    in_specs = [_whole_spec(h_t.shape),
                pl.BlockSpec((t_len, GWIN_COLS), lambda s: (0, jnp.minimum(s, n_in - 1))),
                _whole_spec(ycat_t.shape),
                pl.BlockSpec((t_len, GWOUT_COLS), lambda s: (0, jnp.maximum(s - n_in, 0)))] + [HBM] * len(after)
    out_shape = [jax.ShapeDtypeStruct((D_MODEL, IN_COLS), BF16), jax.ShapeDtypeStruct((D_MODEL, D_MODEL), BF16)]
    out_specs = [pl.BlockSpec((D_MODEL, GWIN_COLS), lambda s: (0, jnp.minimum(s, n_in - 1))),
                 pl.BlockSpec((D_MODEL, GWOUT_COLS), lambda s: (0, jnp.maximum(s - n_in, 0)))]
    scratch = []
    aliases, split = _host(exchange, args, in_specs, out_shape, out_specs, scratch)

    def body(*refs):
        (ht_ref, dproj_ref, ycatt_ref, dy_ref, *_, gwin_ref, gwout_ref), hosted = split(refs)
        s = pl.program_id(0)
        if hosted is not None:
            pl.when(s == 0)(hosted[0])

        @pl.when(s < n_in)
        def _():
            gwin_ref[...] = _dot(ht_ref[...], dproj_ref[...]).astype(BF16)

        @pl.when(s >= n_in)
        def _():
            gwout_ref[...] = _dot(ycatt_ref[...], dy_ref[...]).astype(BF16)

        if hosted is not None:
            pl.when(s == n_in + n_out - 1)(hosted[1])

    return pl.pallas_call(
        body, name=f"weight_grads_{layer}", grid=(n_in + n_out,), in_specs=in_specs, out_specs=out_specs,
        out_shape=out_shape, scratch_shapes=scratch, input_output_aliases=aliases,
        compiler_params=_params(dimension_semantics=("arbitrary",)),
    )(*args)


def _add_sibling_blocks(name, layer, grads, received, core, partials, kinds):
    n_arr = len(grads)

    def body(core_ref, *refs):
        mine, theirs, outs = refs[:n_arr], refs[n_arr:2 * n_arr], refs[-n_arr:]
        for a in range(n_arr):
            outs[a][...] = (mine[a][...].astype(F32) + theirs[a][...].astype(F32)).astype(BF16)

    own_of_kind = [
        pl.BlockSpec((D_MODEL, W_IN_SHARD), lambda q, core_ref: (0, 2 * q + core_ref[0])),
        pl.BlockSpec((W_OUT_SHARD, D_MODEL), lambda q, core_ref: (2 * q + core_ref[0], 0)),
        pl.BlockSpec((POOL_SHARD, GROUP_D), lambda q, core_ref: (2 * q + core_ref[0], 0)),
    ]
    shapes = [_BLOCK_SHAPES[k] for k in kinds]
    recv_specs = [pl.BlockSpec((None,) + s, lambda q, core_ref: (q, 0, 0)) for s in shapes]
    out_specs = [pl.BlockSpec((None, None) + s, lambda q, core_ref: (q, layer, 0, 0)) for s in shapes]
    args = [core, *grads, *received]
    in_specs = [own_of_kind[k] for k in kinds] + recv_specs
    aliases = {}
    if partials is not None:
        aliases = {len(args) + a: a for a in range(n_arr)}
        args += list(partials)
        in_specs += [HBM] * n_arr
    return pl.pallas_call(
        body, name=name,
        grid_spec=pltpu.PrefetchScalarGridSpec(
            num_scalar_prefetch=1, grid=(N_CHIP,), in_specs=in_specs, out_specs=out_specs),
        out_shape=[jax.ShapeDtypeStruct((N_CHIP, DEPTH) + s, BF16) for s in shapes],
        input_output_aliases=aliases,
        compiler_params=_params(dimension_semantics=("arbitrary",)),
    )(*args)


def _modulation_columns(c_all, w_ada):
    def body(c_ref, w_ref, cact_ref, out_ref):
        c_t = c_ref[...]
        c_act = c_t * _sigmoid(c_t)
        cact_ref[...] = c_act
        out_ref[...] = jnp.dot(c_act, w_ref[...], preferred_element_type=F32, precision=lax.Precision.HIGHEST)

    return pl.pallas_call(
        body, name="modulation_columns", grid=(DEPTH,),
        in_specs=[pl.BlockSpec((N_DEV, D_MODEL), lambda l: (0, 0)),
                  pl.BlockSpec((None, D_MODEL, W_IN_SHARD), lambda l: (l, 0, 0))],
        out_specs=[pl.BlockSpec((N_DEV, D_MODEL), lambda l: (0, 0)),
                   pl.BlockSpec((N_DEV, W_IN_SHARD), lambda l: (0, l))],
        out_shape=[jax.ShapeDtypeStruct((N_DEV, D_MODEL), F32),
                   jax.ShapeDtypeStruct((N_DEV, DEPTH * W_IN_SHARD), F32)],
        compiler_params=_params(dimension_semantics=("arbitrary",)),
    )(c_all, w_ada)


def _adamw(w, g, m, v):
    m_new = ADAM_B1 * m + (1.0 - ADAM_B1) * g
    v_new = ADAM_B2 * v + (1.0 - ADAM_B2) * (g * g)
    m_hat = m_new / (1.0 - ADAM_B1 ** ADAM_STEP)
    v_hat = v_new / (1.0 - ADAM_B2 ** ADAM_STEP)
    delta = -ADAM_LR * (m_hat / (jnp.sqrt(v_hat) + ADAM_EPS) + ADAM_WD * w)
    return delta, m_new, v_new


def _adamw_w_ada(w, m, v, c_act_t, dmod_cols, exchange):
    big = pl.BlockSpec((None, D_MODEL, W_IN_SHARD), lambda l: (l, 0, 0))
    args = [w, m, v, c_act_t, dmod_cols]
    in_specs = [big, big, big, pl.BlockSpec((D_MODEL, N_DEV), lambda l: (0, 0)),
                pl.BlockSpec((None, N_DEV, W_IN_SHARD), lambda l: (l, 0, 0))]
    out_shape, out_specs, scratch = [jax.ShapeDtypeStruct(w.shape, F32)] * 4, [big] * 4, []
    aliases, split = _host(exchange, args, in_specs, out_shape, out_specs, scratch)

    def body(*refs):
        (w_ref, m_ref, v_ref, ct_ref, dm_ref, g_ref, d_ref, mo_ref, vo_ref), hosted = split(refs)
        if hosted is not None:
            pl.when(pl.program_id(0) == 0)(hosted[0])
        g = ct_ref[:, 0:1] * dm_ref[0:1, :]
        for b in range(1, N_DEV):
            g = g + ct_ref[:, b:b + 1] * dm_ref[b:b + 1, :]
        g_ref[...] = g
        d_ref[...], mo_ref[...], vo_ref[...] = _adamw(w_ref[...], g, m_ref[...], v_ref[...])
        if hosted is not None:
            pl.when(pl.program_id(0) == DEPTH - 1)(hosted[1])

    return pl.pallas_call(
        body, name="adamw_w_ada", grid=(DEPTH,), in_specs=in_specs, out_specs=out_specs, out_shape=out_shape,
        scratch_shapes=scratch, input_output_aliases=aliases,
        compiler_params=_params(dimension_semantics=("arbitrary",)),
    )(*args)


def _sum_chip_partials(own_ref, recv_ref):
    g = own_ref[...].astype(F32)
    for j in range(N_OTHER_CHIPS):
        g = g + recv_ref[j].astype(F32)
    return g


def _partial_specs(row_tile, cols, first_layer=0):
    own = pl.BlockSpec((None, None, row_tile, cols), lambda l, r, chip_ref: (chip_ref[0], first_layer + l, r, 0))
    recv = pl.BlockSpec((N_OTHER_CHIPS, None, row_tile, cols), lambda l, r, chip_ref: (0, first_layer + l, r, 0))
    return own, recv


def _adamw_reduced(name, w, m, v, partial, received, chip, row_tile, layers, continued):
    depth, rows, cols = w.shape
    first, stop = layers

    def body(chip_ref, w_ref, m_ref, v_ref, own_ref, recv_ref, *rest):
        g_ref, d_ref, mo_ref, vo_ref = rest[-4:]
        g = _sum_chip_partials(own_ref, recv_ref)
        g_ref[...] = g
        d_ref[...], mo_ref[...], vo_ref[...] = _adamw(w_ref[...], g, m_ref[...], v_ref[...])

    blk = pl.BlockSpec((None, row_tile, cols), lambda l, r, chip_ref: (first + l, r, 0))
    args = [chip, w, m, v, partial, received]
    in_specs = [blk, blk, blk, *_partial_specs(row_tile, cols, first)]
    aliases = {}
    if continued is not None:
        aliases = {len(args) + k: k for k in range(4)}
        args += list(continued)
        in_specs += [HBM] * 4
    return pl.pallas_call(
        body, name=name,
        grid_spec=pltpu.PrefetchScalarGridSpec(
            num_scalar_prefetch=1, grid=(stop - first, rows // row_tile), in_specs=in_specs, out_specs=[blk] * 4),
        out_shape=[jax.ShapeDtypeStruct(w.shape, F32)] * 4, input_output_aliases=aliases,
        compiler_params=_params(dimension_semantics=("arbitrary", "arbitrary")),
    )(*args)


def _reduce_w_pool(partial, received, chip):
    def body(chip_ref, own_ref, recv_ref, g_ref):
        g_ref[...] = _sum_chip_partials(own_ref, recv_ref)

    return pl.pallas_call(
        body, name="reduce_w_pool",
        grid_spec=pltpu.PrefetchScalarGridSpec(
            num_scalar_prefetch=1, grid=(DEPTH, 1), in_specs=list(_partial_specs(POOL_SHARD, GROUP_D)),
            out_specs=pl.BlockSpec((POOL_SHARD, GROUP_D), lambda l, r, chip_ref: (l, 0))),
        out_shape=jax.ShapeDtypeStruct((DEPTH * POOL_SHARD, GROUP_D), F32),
        compiler_params=_params(dimension_semantics=("arbitrary", "arbitrary")),
    )(chip, partial, received)


def _adamw_small(params):
    n = len(params)

    def body(*refs):
        ins, outs = refs[:4 * n], refs[4 * n:]
        for p in range(n):
            w_ref, g_ref, m_ref, v_ref = ins[4 * p:4 * p + 4]
            d_ref, mo_ref, vo_ref = outs[3 * p:3 * p + 3]
            d_ref[...], mo_ref[...], vo_ref[...] = _adamw(w_ref[...], g_ref[...], m_ref[...], v_ref[...])

    vmem = pl.BlockSpec(memory_space=pltpu.VMEM)
    flat = [a for group in params for a in group]
    out_shape = [jax.ShapeDtypeStruct(group[0].shape, F32) for group in params for _ in range(3)]
    outs = pl.pallas_call(
        body, name="adamw_small", in_specs=[vmem] * len(flat), out_specs=[vmem] * len(out_shape),
        out_shape=out_shape, compiler_params=_params(),
    )(*flat)
    return [tuple(outs[3 * p:3 * p + 3]) for p in range(n)]


def _sum_sources(slabs):
    def body(s_ref, o_ref):
        acc = s_ref[0]
        for b in range(1, N_DEV):
            acc = acc + s_ref[b]
        o_ref[...] = acc

    vmem = pl.BlockSpec(memory_space=pltpu.VMEM)
    return pl.pallas_call(
        body, name="sum_small_grads", in_specs=[vmem], out_specs=vmem,
        out_shape=jax.ShapeDtypeStruct(slabs.shape[1:], F32), compiler_params=_params(),
    )(slabs)


def _to_bf16(a, name, layers=None):
    first, stop = layers or (0, a.shape[0])

    def body(a_ref, o_ref):
        o_ref[...] = a_ref[...].astype(BF16)

    block = (None,) + a.shape[1:]
    return pl.pallas_call(
        body, name=name, grid=(stop - first,), in_specs=[pl.BlockSpec(block, lambda l: (first + l, 0, 0))],
        out_specs=pl.BlockSpec(block, lambda l: (l, 0, 0)),
        out_shape=jax.ShapeDtypeStruct((stop - first,) + a.shape[1:], BF16),
        compiler_params=_params(dimension_semantics=("arbitrary",)),
    )(a)


def kernel(x, c, w_ada, b_ada, g_pre, w_in, w_conv, w_pool, pool_scale, w_out, g_post, loss_target, m_w_ada, m_b_ada, m_g_pre, m_w_in, m_w_conv, m_w_pool, m_pool_scale, m_w_out, m_g_post, v_w_ada, v_b_ada, v_g_pre, v_w_in, v_w_conv, v_w_pool, v_pool_scale, v_w_out, v_g_post):
    mx, my, mc = _mesh_position()
    me = _block_id(mx, my, mc)
    chip = (2 * mx + my).astype(jnp.int32).reshape(1)
    core = mc.astype(jnp.int32).reshape(1)
    x0 = x[0]
    target = loss_target[0]
    conv_shard = w_conv.shape[-1]

    own_small = jnp.concatenate([c, w_conv.reshape(1, DEPTH * 3 * conv_shard)], axis=1)
    all_small = _all_gather_small(own_small, "all_gather_c_w_conv")[:, 0, :]
    gathers = [_gather_weights_start(
        0, _to_bf16(w_in, "cast_w_in_0", (0, 1)), _to_bf16(w_out, "cast_w_out_0", (0, 1)), [all_small])]
    c_all = all_small[:, :D_MODEL]
    w_conv_full = all_small[:, D_MODEL:].reshape(N_DEV, DEPTH, 3, conv_shard).transpose(1, 2, 0, 3).reshape(
        DEPTH, 3, CONV_W)
    cps = jnp.concatenate([w_conv_full, pool_scale[:, None], jnp.zeros((DEPTH, 4, CONV_W), F32)], axis=1)

    c_act, pieces = _modulation_columns(c_all, w_ada)
    upper = (1, DEPTH)
    upper_shards = [_to_bf16(w_in, "cast_w_in_1", upper), _to_bf16(w_out, "cast_w_out_1", upper)]
    wpool_b = _to_bf16(w_pool.reshape(DEPTH, POOL_ROWS, GROUP_D), "cast_w_pool").reshape(w_pool.shape)
    mod_all = _all_gather_small(
        pieces, "all_gather_modulation", [gathers[0][1][0], *upper_shards, wpool_b])
    mod_mine = lax.dynamic_index_in_dim(mod_all, me, axis=1, keepdims=False)
    mod = mod_mine.reshape(N_DEV, DEPTH, W_IN_SHARD).transpose(1, 0, 2).reshape(DEPTH, 3 * D_MODEL) + b_ada
    zeros_d = jnp.zeros((DEPTH, 3, D_MODEL), F32)
    vec = jnp.concatenate([mod.reshape(DEPTH, 3, D_MODEL), g_pre[:, None], g_post[:, None], zeros_d], axis=1)

    gathers.append(_gather_weights_start(1, *upper_shards, [mod_all]))
    gathers = [(first_sems, shards, landing[k], k) for first_sems, shards, landing in gathers
               for k in range(len(landing))]

    xs, kept, wins, wouts = [x0], [], [], []
    for l in range(DEPTH):
        first_sems, shards, zones, index = gathers[l]
        passed_sems, zones = _gather_weights_pass_on(
            l, index, first_sems[1], zones, [vec, cps, wpool_b, gathers[-1][1][0]] if l == 0 else [xs[-1]])
        win, wout = _gather_weights_finish(l, index, first_sems, passed_sems, shards, zones)
        x_next, *for_backward = _forward_layer(
            l, xs[-1], vec, cps, wpool_b, win, wout, target if l == DEPTH - 1 else None)
        xs.append(x_next)
        kept.append(for_backward[:6])
        wins.append(win)
        wouts.append(wout)
    dx, loss_tile = xs[DEPTH], for_backward[6]

    slab_rows = [None] * DEPTH
    partials = received = None
    in_flight = []

    def scatter(layer, grads, from_sibling, after):
        nonlocal partials, received
        partials = _add_sibling_blocks(
            f"grad_add_sibling_{layer}", layer, grads, from_sibling, core, partials, ALL_KINDS)
        chips = _chips_exchange(layer, partials, received, ALL_KINDS)
        sems, partials, received, token = _start_exchange(chips, f"grad_chips_start_{layer}", after)
        in_flight.append((chips, sems, layer))
        return token

    grads_above = None
    for l in reversed(range(DEPTH)):
        y, h_t, ycat_t, uc, fa, fp = kept[l]
        dx, dproj, dy, gwpool, dcps, dvec = _backward_layer(
            l, dx, y, xs[l], uc, fa, fp, vec, cps, wpool_b, wouts[l], wins[l])
        slab_rows[l] = jnp.concatenate(
            [dvec[0], dvec[1], dvec[2], dvec[3], dvec[4], dcps[3], dcps[0], dcps[1], dcps[2],
             loss_tile[0] if l == 0 else jnp.zeros((LANES,), F32)])
        after = []
        if l == 0:
            gather_small = _all_gather_exchange(jnp.stack(slab_rows))
            sems_s, slab, slabs, token = _start_exchange(gather_small, "all_gather_small_grads_start")
            after = [token]
        hosted = _sibling_exchange(grads_above, ALL_KINDS) if grads_above is not None else None
        gwin, gwout, *from_sibling = _weight_grads(l, h_t, dproj, ycat_t, dy, hosted, after)
        if l == 0:
            _, (slabs,) = _finish_exchange(
                gather_small, "all_gather_small_grads_finish", sems_s, slab, slabs, [gwin])
        if grads_above is not None:
            scatter(l + 1, grads_above, from_sibling, [])
        grads_above = [gwin, gwout, gwpool.reshape(POOL_ROWS, GROUP_D)]
        if l <= 1:
            from_sibling = _run_exchange(_sibling_exchange(grads_above, ALL_KINDS), f"grad_exchange_sibling_{l}")
            token = scatter(l, grads_above, from_sibling, [slabs] if l == 0 else [])
            grads_above = None
    grad_x = dx[None]
    chips_0, sems_0, _ = in_flight.pop()

    total = _sum_sources(slabs)
    loss = total[0, SLAB_COLS]
    o = 3 * D_MODEL
    g_b_ada = total[:, :o]
    g_g_pre = total[:, o:o + D_MODEL]
    g_g_post = total[:, o + D_MODEL:o + 2 * D_MODEL]
    g_pool_scale = total[:, o + 2 * D_MODEL:o + 2 * D_MODEL + POOL_W]
    g_conv_full = total[:, o + 2 * D_MODEL + POOL_W:SLAB_COLS].reshape(DEPTH, 3, CONV_W)
    g_w_conv = lax.dynamic_slice_in_dim(g_conv_full, me * conv_shard, conv_shard, axis=2)

    after = [token]
    for chips, sems, l in in_flight:
        partials, received = _finish_exchange(chips, f"grad_chips_finish_{l}", sems, partials, received, after)
        after = []
    upper = (1, DEPTH)
    w_in_upper = _adamw_reduced(
        "adamw_w_in_upper", w_in, m_w_in, v_w_in, partials[0], received[0], chip, ROW_TILE, upper, None)
    w_out_upper = _adamw_reduced(
        "adamw_w_out_upper", w_out, m_w_out, v_w_out, partials[1], received[1], chip, W_OUT_SHARD, upper, None)
    dmod_all = slabs[:, :, :o].reshape(N_DEV, DEPTH, N_DEV, W_IN_SHARD)
    dmod_cols = lax.dynamic_index_in_dim(dmod_all, me, axis=2, keepdims=False).transpose(1, 0, 2) + token[0, 0]
    g_w_ada, d_w_ada, nm_w_ada, nv_w_ada = _adamw_w_ada(w_ada, m_w_ada, v_w_ada, c_act.T, dmod_cols, None)

    partials, received = _finish_exchange(
        chips_0, "grad_chips_finish_0", sems_0, partials, received, [nv_w_ada, w_in_upper[3], w_out_upper[3]])
    gather_pool = _all_gather_exchange(_reduce_w_pool(partials[2], received[2], chip))
    sems_p, pool_rows, pool_landing, token_p = _start_exchange(gather_pool, "all_gather_grad_w_pool_start")
    g_w_in, d_w_in, nm_w_in, nv_w_in = _adamw_reduced(
        "adamw_w_in_0", w_in, m_w_in, v_w_in, partials[0], received[0], chip, ROW_TILE, (0, 1), w_in_upper)
    g_w_out, d_w_out, nm_w_out, nv_w_out = _adamw_reduced(
        "adamw_w_out_0", w_out, m_w_out, v_w_out, partials[1], received[1], chip, W_OUT_SHARD, (0, 1), w_out_upper)
    _, (g_pool_all,) = _finish_exchange(
        gather_pool, "all_gather_grad_w_pool_finish", sems_p, pool_rows, pool_landing, [nv_w_in, nv_w_out])
    g_w_pool = g_pool_all.reshape(N_DEV, DEPTH, POOL_SHARD, GROUP_D).transpose(1, 0, 2, 3).reshape(w_pool.shape)

    flat2 = lambda a: a.reshape(-1, a.shape[-1])
    small = _adamw_small([
        (b_ada, g_b_ada, m_b_ada, v_b_ada),
        (g_pre, g_g_pre, m_g_pre, v_g_pre),
        (flat2(w_conv), flat2(g_w_conv), flat2(m_w_conv), flat2(v_w_conv)),
        (flat2(w_pool), flat2(g_w_pool), flat2(m_w_pool), flat2(v_w_pool)),
        (pool_scale, g_pool_scale, m_pool_scale, v_pool_scale),
        (g_post, g_g_post, m_g_post, v_g_post),
    ])
    (d_b_ada, nm_b_ada, nv_b_ada), (d_g_pre, nm_g_pre, nv_g_pre), conv_upd, pool_upd, \
        (d_ps, nm_ps, nv_ps), (d_g_post, nm_g_post, nv_g_post) = small
    d_w_conv, nm_w_conv, nv_w_conv = (a.reshape(w_conv.shape) for a in conv_upd)
    d_w_pool, nm_w_pool, nv_w_pool = (a.reshape(w_pool.shape) for a in pool_upd)

    return (loss, grad_x,
            g_w_ada, g_b_ada, g_g_pre, g_w_in, g_w_conv, g_w_pool, g_pool_scale, g_w_out, g_g_post,
            d_w_ada, d_b_ada, d_g_pre, d_w_in, d_w_conv, d_w_pool, d_ps, d_w_out, d_g_post,
            nm_w_ada, nm_b_ada, nm_g_pre, nm_w_in, nm_w_conv, nm_w_pool, nm_ps, nm_w_out, nm_g_post,
            nv_w_ada, nv_b_ada, nv_g_pre, nv_w_in, nv_w_conv, nv_w_pool, nv_ps, nv_w_out, nv_g_post)
```

```python
import jax
import jax.numpy as jnp
from jax import lax
from jax.experimental import pallas as pl
from jax.experimental.pallas import tpu as pltpu

F32 = jnp.float32
BF16 = jnp.bfloat16

D_MODEL = 1024
DEPTH = 4
CONV_W = 512
POOL_W = 512
POOL_WINDOWS = (2, 4, 8, 16)
GROUP_D = 128
IN_COLS = 4 * CONV_W + 2 * POOL_W
NORM_EPS = 1e-6

ADAM_LR = 0.001
ADAM_B1 = 0.9
ADAM_B2 = 0.999
ADAM_EPS = 1e-08
ADAM_WD = 0.01
ADAM_STEP = 10

N_DEV = 8
N_CHIP = 4
N_OTHER_CHIPS = N_CHIP - 1
MESH = pl.DeviceIdType.MESH
W_IN_SHARD = IN_COLS // N_DEV
W_OUT_SHARD = D_MODEL // N_DEV
POOL_ROWS = len(POOL_WINDOWS) * GROUP_D
POOL_SHARD = POOL_ROWS // N_DEV

SUBLANES = 8
LANES = 128
VMEM_LIMIT_BYTES = 56 * 1024 * 1024
ROW_TILE = 512
BWD_TILE = 512
GWIN_COLS = 768
GWOUT_COLS = 512
POOL_HALO = 16
CONV_HALO = SUBLANES

SLAB_COLS = 3 * D_MODEL + D_MODEL + D_MODEL + POOL_W + 3 * CONV_W

HBM = pl.BlockSpec(memory_space=pl.ANY)


def _params(**kw):
    return pltpu.CompilerParams(vmem_limit_bytes=VMEM_LIMIT_BYTES, **kw)


def _sigmoid(v):
    return 1.0 / (1.0 + jnp.exp(-v))


def _dot(a, b):
    return jnp.dot(a, b, preferred_element_type=F32)


def _dot_tn(a, b):
    return lax.dot_general(a, b, (((0,), (0,)), ((), ())), preferred_element_type=F32)


def _dot_nt(a, b):
    return lax.dot_general(a, b, (((1,), (1,)), ((), ())), preferred_element_type=F32)


def _rows_from_before(v, k):
    return pltpu.roll(v, k, 0)


def _rows_from_after(v, k):
    return pltpu.roll(v, v.shape[0] - k, 0)


def _window_counts(t0, rows):
    return (lax.broadcasted_iota(jnp.int32, (rows, 1), 0) + (t0 + 1)).astype(F32)


def _split_proj(p32):
    cw = CONV_W
    return (p32[:, 0 * cw:1 * cw], p32[:, 1 * cw:2 * cw], p32[:, 2 * cw:3 * cw], p32[:, 3 * cw:4 * cw],
            p32[:, 4 * cw:4 * cw + POOL_W], p32[:, 4 * cw + POOL_W:])


def _layer_spec(shape, layer):
    nd = len(shape)
    return pl.BlockSpec((None,) + tuple(shape[1:]), lambda i, _l=layer, _n=nd: (_l,) + (0,) * (_n - 1))


def _whole_spec(shape):
    return pl.BlockSpec(tuple(shape), lambda i, _n=len(shape): (0,) * _n, pipeline_mode=pl.Buffered(1))


def _mesh_position():
    return lax.axis_index("x"), lax.axis_index("y"), lax.axis_index("c")


def _block_id(x, y, c):
    return 4 * x + 2 * y + c


def _other_chips(x, y):
    return [(x ^ 1, y), (x, y ^ 1), (x ^ 1, y ^ 1)]


def _col_block(ref, blk):
    return ref.at[:, pl.ds(pl.multiple_of(blk * W_IN_SHARD, LANES), W_IN_SHARD)]


def _row_block(rows):
    def block(ref, blk):
        return ref.at[pl.ds(pl.multiple_of(blk * rows, rows), rows), :]
    return block


_BLOCK_OF = (_col_block, _row_block(W_OUT_SHARD), _row_block(POOL_SHARD))
_BLOCK_SHAPES = ((D_MODEL, W_IN_SHARD), (W_OUT_SHARD, D_MODEL), (POOL_SHARD, GROUP_D))


class _Exchange:
    def __init__(self, inputs, out_shapes, aliases, sem_shapes, make):
        self.inputs, self.out_shapes, self.aliases, self.sem_shapes, self.make = (
            list(inputs), list(out_shapes), dict(aliases), list(sem_shapes), make)


def _run_exchange(exchange, name):
    n_in, n_out = len(exchange.inputs), len(exchange.out_shapes)

    def body(*refs):
        start, finish = exchange.make(refs[:n_in], refs[n_in:n_in + n_out], refs[n_in + n_out:])
        start()
        finish()

    return pl.pallas_call(
        body, name=name, in_specs=[HBM] * n_in, out_specs=[HBM] * n_out, out_shape=exchange.out_shapes,
        scratch_shapes=exchange.sem_shapes, input_output_aliases=exchange.aliases, compiler_params=_params(),
    )(*exchange.inputs)


_SEM = pl.BlockSpec(memory_space=pltpu.SEMAPHORE)
_DATAFLOW = pltpu.SideEffectType.DATAFLOW_SIDE_EFFECTING


def _start_exchange(exchange, name, after=()):
    n_in, n_out, n_sem = len(exchange.inputs), len(exchange.out_shapes), len(exchange.sem_shapes)
    sources = [i for i in range(n_in) if i not in exchange.aliases]
    aliases = {i: n_sem + k for k, i in enumerate(sources)}
    aliases.update({i: n_sem + len(sources) + o for i, o in exchange.aliases.items()})

    def body(*refs):
        in_refs = refs[:n_in]
        outs = refs[n_in + len(after):]
        sems = outs[:n_sem]
        out_refs = outs[n_sem + len(sources):n_sem + len(sources) + n_out]
        exchange.make(in_refs, out_refs, sems)[0]()
        refs[-1][...] = jnp.zeros_like(refs[-1])

    outs = pl.pallas_call(
        body, name=name, in_specs=[HBM] * (n_in + len(after)),
        out_specs=[_SEM] * n_sem + [HBM] * (len(sources) + n_out) + [pl.BlockSpec(memory_space=pltpu.VMEM)],
        out_shape=(exchange.sem_shapes + [pltpu.HBM(exchange.inputs[i].shape, exchange.inputs[i].dtype) for i in sources]
                   + [pltpu.HBM(s.shape, s.dtype) for s in exchange.out_shapes]
                   + [jax.ShapeDtypeStruct((SUBLANES, LANES), F32)]),
        input_output_aliases=aliases, compiler_params=_params(has_side_effects=_DATAFLOW),
    )(*exchange.inputs, *after)
    return outs[:n_sem], outs[n_sem:n_sem + len(sources)], outs[n_sem + len(sources):-1], outs[-1]


def _finish_exchange(exchange, name, sems, sources, landing, after):
    n_src, n_out, n_sem = len(sources), len(landing), len(sems)
    n_in = len(exchange.inputs)
    source_at = [i for i in range(n_in) if i not in exchange.aliases]

    def body(*refs):
        src_refs, out_refs = refs[:n_src], refs[n_src:n_src + n_out]
        sem_refs = refs[n_src + n_out:n_src + n_out + n_sem]
        in_refs = [None] * n_in
        for k, i in enumerate(source_at):
            in_refs[i] = src_refs[k]
        for i, o in exchange.aliases.items():
            in_refs[i] = out_refs[o]
        exchange.make(in_refs, out_refs, sem_refs)[1]()

    arrays = list(sources) + list(landing)
    outs = pl.pallas_call(
        body, name=name, in_specs=[HBM] * len(arrays) + [_SEM] * n_sem + [HBM] * len(after),
        out_specs=[HBM] * len(arrays), out_shape=[pltpu.HBM(a.shape, a.dtype) for a in arrays],
        input_output_aliases={i: i for i in range(len(arrays))}, compiler_params=_params(has_side_effects=_DATAFLOW),
    )(*arrays, *sems, *after)
    return outs[:n_src], outs[n_src:]


N_GATHERED = 2
FIRST_COPIES = 1 + N_OTHER_CHIPS
_GATHERED_SHAPES = ((D_MODEL, IN_COLS), (D_MODEL, D_MODEL))


def _first_sem(layer, a, k):
    return (layer * N_GATHERED + a) * FIRST_COPIES + k


def _gather_copy(window_of, full_ref, blk, send_sem, recv_sem, to, src=None):
    window = window_of(full_ref, blk)
    return pltpu.make_async_remote_copy(
        src_ref=window if src is None else src, dst_ref=window, send_sem=send_sem, recv_sem=recv_sem,
        device_id=to, device_id_type=MESH)


def _first_copies(layer, shard_refs, full_refs, send_sems, recv_sems, local_sems):
    x, y, c = _mesh_position()
    me = _block_id(x, y, c)
    own, remote = [], []
    for a in range(N_GATHERED):
        shard = shard_refs[a].at[layer]
        own.append(pltpu.make_async_copy(
            shard, _BLOCK_OF[a](full_refs[a], me), local_sems.at[layer * N_GATHERED + a]))
        targets = [(x, y, 1 - c)] + [(*chip, c) for chip in _other_chips(x, y)]
        remote += [_gather_copy(_BLOCK_OF[a], full_refs[a], me, send_sems.at[_first_sem(layer, a, k)],
                                recv_sems.at[_first_sem(layer, a, k)], to, src=shard)
                   for k, to in enumerate(targets)]
    return own, remote


def _gather_weights_start(first_layer, win_shards, wout_shards, after):
    n_layers = win_shards.shape[0]
    n_first = n_layers * N_GATHERED * FIRST_COPIES
    sem_shapes = [pltpu.SemaphoreType.DMA((n_first,)), pltpu.SemaphoreType.DMA((n_first,)),
                  pltpu.SemaphoreType.DMA((n_layers * N_GATHERED,))]
    shards = [win_shards, wout_shards]

    def body(win_sh, wout_sh, *rest):
        send_sems, recv_sems, local_sems, win_thru, wout_thru, *landing = rest[len(after):]
        for layer in range(n_layers):
            own, remote = _first_copies(layer, (win_sh, wout_sh), landing[N_GATHERED * layer:N_GATHERED * (layer + 1)],
                                        send_sems, recv_sems, local_sems)
            for cp in own + remote:
                cp.start()

    outs = pl.pallas_call(
        body, name=f"all_gather_weights_start_{first_layer}", in_specs=[HBM] * (2 + len(after)),
        out_specs=[_SEM] * 3 + [HBM] * (2 + n_layers * N_GATHERED),
        out_shape=(sem_shapes + [pltpu.HBM(s.shape, s.dtype) for s in shards]
                   + [pltpu.HBM(s, BF16) for _ in range(n_layers) for s in _GATHERED_SHAPES]),
        input_output_aliases={0: 3, 1: 4}, compiler_params=_params(has_side_effects=_DATAFLOW),
    )(*shards, *after)
    landing = outs[5:]
    return outs[:3], outs[3:5], [landing[N_GATHERED * l:N_GATHERED * (l + 1)] for l in range(n_layers)]


def _passed_on_copies(full_refs, send_sems, recv_sems, core_of_block):
    x, y, c = _mesh_position()
    return [_gather_copy(_BLOCK_OF[a], full_refs[a], _block_id(*chip, core_of_block),
                         send_sems.at[a * N_OTHER_CHIPS + j], recv_sems.at[a * N_OTHER_CHIPS + j], (x, y, 1 - c))
            for a in range(N_GATHERED) for j, chip in enumerate(_other_chips(x, y))]


def _gather_weights_pass_on(layer, index, first_recv_sems, landing, after):
    n = N_GATHERED * N_OTHER_CHIPS

    def body(win_ref, wout_ref, first_recv, *rest):
        send_sems, recv_sems = rest[len(after):len(after) + 2]
        x, y, c = _mesh_position()
        full_refs = (win_ref, wout_ref)
        passed = _passed_on_copies(full_refs, send_sems, recv_sems, c)
        for a in range(N_GATHERED):
            for j, chip in enumerate(_other_chips(x, y)):
                sem = _first_sem(index, a, 1 + j)
                _gather_copy(_BLOCK_OF[a], full_refs[a], _block_id(*chip, c), first_recv.at[sem], first_recv.at[sem],
                             (x, y, c)).wait_recv()
                passed[a * N_OTHER_CHIPS + j].start()

    outs = pl.pallas_call(
        body, name=f"all_gather_weights_pass_on_{layer}", in_specs=[HBM] * N_GATHERED + [_SEM] + [HBM] * len(after),
        out_specs=[_SEM] * 2 + [HBM] * N_GATHERED,
        out_shape=[pltpu.SemaphoreType.DMA((n,)), pltpu.SemaphoreType.DMA((n,))]
        + [pltpu.HBM(a.shape, a.dtype) for a in landing],
        input_output_aliases={a: 2 + a for a in range(N_GATHERED)},
        compiler_params=_params(has_side_effects=_DATAFLOW),
    )(*landing, first_recv_sems, *after)
    return outs[:2], outs[2:]


def _gather_weights_finish(layer, index, first_sems, passed_sems, shards, landing):
    def body(win_ref, wout_ref, first_send, first_recv, local_sems, passed_send, passed_recv, win_sh, wout_sh, *thru):
        x, y, c = _mesh_position()
        full_refs = (win_ref, wout_ref)
        own, remote = _first_copies(index, (win_sh, wout_sh), full_refs, first_send, first_recv, local_sems)
        for a in range(N_GATHERED):
            sem = _first_sem(index, a, 0)
            _gather_copy(_BLOCK_OF[a], full_refs[a], _block_id(x, y, 1 - c), first_recv.at[sem], first_recv.at[sem],
                         (x, y, c)).wait_recv()
        for cp in _passed_on_copies(full_refs, passed_send, passed_recv, 1 - c):
            cp.wait_recv()
        for cp in remote + _passed_on_copies(full_refs, passed_send, passed_recv, c):
            cp.wait_send()
        for cp in own:
            cp.wait()

    return pl.pallas_call(
        body, name=f"all_gather_weights_finish_{layer}", in_specs=[HBM] * N_GATHERED + [_SEM] * 5 + [HBM] * 2,
        out_specs=[HBM] * N_GATHERED, out_shape=[pltpu.HBM(a.shape, a.dtype) for a in landing],
        input_output_aliases={a: a for a in range(N_GATHERED)}, compiler_params=_params(has_side_effects=_DATAFLOW),
    )(*landing, *first_sems, *passed_sems, *shards)


ALL_KINDS = (0, 1, 2)


def _sibling_exchange(grads, kinds):
    n_arr = len(grads)

    def make(in_refs, out_refs, sems):
        send_sems, recv_sems = sems
        x, y, c = _mesh_position()
        copies = [pltpu.make_async_remote_copy(
            src_ref=_BLOCK_OF[kinds[a]](in_refs[a], 2 * q + (1 - c)), dst_ref=out_refs[a].at[q],
            send_sem=send_sems.at[a, q], recv_sem=recv_sems.at[a, q], device_id=(x, y, 1 - c), device_id_type=MESH)
            for a in range(n_arr) for q in range(N_CHIP)]

        def start():
            for cp in copies:
                cp.start()

        def finish():
            for cp in copies:
                cp.wait_recv()
            for cp in copies:
                cp.wait_send()

        return start, finish

    return _Exchange(
        grads, [jax.ShapeDtypeStruct((N_CHIP,) + _BLOCK_SHAPES[k], BF16) for k in kinds], {},
        [pltpu.SemaphoreType.DMA((n_arr, N_CHIP)), pltpu.SemaphoreType.DMA((n_arr, N_CHIP))], make)


def _chips_exchange(layer, partials, received, kinds):
    n_arr = len(partials)

    def make(in_refs, out_refs, sems):
        send_sems, recv_sems = sems
        x, y, c = _mesh_position()
        copies = [pltpu.make_async_remote_copy(
            src_ref=in_refs[a].at[2 * qx + qy, layer], dst_ref=out_refs[a].at[j, layer],
            send_sem=send_sems.at[a * N_OTHER_CHIPS + j], recv_sem=recv_sems.at[a * N_OTHER_CHIPS + j],
            device_id=(qx, qy, c), device_id_type=MESH)
            for a in range(n_arr) for j, (qx, qy) in enumerate(_other_chips(x, y))]

        def start():
            for cp in copies:
                cp.start()

        def finish():
            for cp in copies:
                cp.wait_recv()
            for cp in copies:
                cp.wait_send()

        return start, finish

    inputs = list(partials)
    aliases = {}
    if received is not None:
        inputs += list(received)
        aliases = {n_arr + a: a for a in range(n_arr)}
    return _Exchange(
        inputs, [jax.ShapeDtypeStruct((N_OTHER_CHIPS, DEPTH) + _BLOCK_SHAPES[k], BF16) for k in kinds], aliases,
        [pltpu.SemaphoreType.DMA((n_arr * N_OTHER_CHIPS,)), pltpu.SemaphoreType.DMA((n_arr * N_OTHER_CHIPS,))], make)


def _all_gather_exchange(v):
    def make(in_refs, out_refs, sems):
        send_sems, recv_sems, local_sem = sems
        x, y, c = _mesh_position()
        me = _block_id(x, y, c)
        own = pltpu.make_async_copy(in_refs[0], out_refs[0].at[me], local_sem.at[0])
        sends, arrivals = [], []
        for k in range(1, N_DEV):
            px, py, pc = x ^ ((k >> 2) & 1), y ^ ((k >> 1) & 1), c ^ (k & 1)
            sends.append(pltpu.make_async_remote_copy(
                src_ref=in_refs[0], dst_ref=out_refs[0].at[me], send_sem=send_sems.at[k - 1],
                recv_sem=recv_sems.at[k - 1], device_id=(px, py, pc), device_id_type=MESH))
            arrivals.append(pltpu.make_async_remote_copy(
                src_ref=in_refs[0], dst_ref=out_refs[0].at[_block_id(px, py, pc)], send_sem=send_sems.at[k - 1],
                recv_sem=recv_sems.at[k - 1], device_id=(x, y, c), device_id_type=MESH))

        def start():
            for cp in [own] + sends:
                cp.start()

        def finish():
            for cp in arrivals:
                cp.wait_recv()
            for cp in sends:
                cp.wait_send()
            own.wait()

        return start, finish

    return _Exchange(
        [v], [jax.ShapeDtypeStruct((N_DEV,) + v.shape, v.dtype)], {},
        [pltpu.SemaphoreType.DMA((N_DEV - 1,)), pltpu.SemaphoreType.DMA((N_DEV - 1,)),
         pltpu.SemaphoreType.DMA((1,))], make)


def _host(exchange, args, in_specs, out_shape, out_specs, scratch):
    n_own = (len(args), len(out_shape), len(scratch))
    if exchange is None:
        return {}, lambda refs: (refs, None)
    n_ex = (len(exchange.inputs), len(exchange.out_shapes), len(exchange.sem_shapes))
    aliases = {n_own[0] + i: n_own[1] + o for i, o in exchange.aliases.items()}
    args += exchange.inputs
    in_specs += [HBM] * n_ex[0]
    out_shape += exchange.out_shapes
    out_specs += [HBM] * n_ex[1]
    scratch += exchange.sem_shapes

    def split(refs):
        own, theirs, at = [], [], 0
        for mine, ex in zip(n_own, n_ex):
            own += refs[at:at + mine]
            theirs.append(refs[at + mine:at + mine + ex])
            at += mine + ex
        return own, exchange.make(*theirs)

    return aliases, split


def _all_gather_small(v, name, after=()):
    vmem = pl.BlockSpec(memory_space=pltpu.VMEM)

    def body(v_ref, *rest):
        out_ref, send_sems, recv_sems = rest[len(after):]
        x, y, c = _mesh_position()
        me = _block_id(x, y, c)
        out_ref[me] = v_ref[...]
        sends = []
        for k in range(1, N_DEV):
            px, py, pc = x ^ ((k >> 2) & 1), y ^ ((k >> 1) & 1), c ^ (k & 1)
            send = pltpu.make_async_remote_copy(
                src_ref=v_ref, dst_ref=out_ref.at[me], send_sem=send_sems.at[k - 1], recv_sem=recv_sems.at[k - 1],
                device_id=(px, py, pc), device_id_type=MESH)
            send.start()
            sends.append((send, _block_id(px, py, pc)))
        for k, (send, peer) in enumerate(sends):
            pltpu.make_async_remote_copy(
                src_ref=v_ref, dst_ref=out_ref.at[peer], send_sem=send_sems.at[k], recv_sem=recv_sems.at[k],
                device_id=(x, y, c), device_id_type=MESH).wait_recv()
        for send, _ in sends:
            send.wait_send()

    return pl.pallas_call(
        body, name=name, in_specs=[vmem] + [HBM] * len(after), out_specs=vmem,
        out_shape=jax.ShapeDtypeStruct((N_DEV,) + v.shape, v.dtype),
        scratch_shapes=[pltpu.SemaphoreType.DMA((N_DEV - 1,)), pltpu.SemaphoreType.DMA((N_DEV - 1,))],
        compiler_params=_params(),
    )(v, *after)


def _forward_layer(layer, x, vec, cps, wpool, win, wout, target=None):
    t_len = x.shape[0]
    n_tiles = t_len // ROW_TILE
    row = lambda cols: pl.BlockSpec((ROW_TILE, cols), lambda i: (i, 0))
    col_t = pl.BlockSpec((D_MODEL, ROW_TILE), lambda i: (0, i))
    widths = (D_MODEL, 2 * CONV_W, 3 * CONV_W, 4 * POOL_W)
    head = target is not None

    def body(x_ref, vec_ref, cps_ref, wpool_ref, win_ref, wout_ref, *rest):
        target_ref = rest[0] if head else None
        xo_ref, y_ref, ht_ref, ycatt_ref, uc_ref, fa_ref, fp_ref = rest[head:head + 7]
        loss_ref = rest[head + 7] if head else None
        zc_ref, pc_ref = rest[-2:]
        i = pl.program_id(0)

        @pl.when(i == 0)
        def _():
            zc_ref[...] = jnp.zeros_like(zc_ref)
            pc_ref[...] = jnp.zeros_like(pc_ref)
            if head:
                loss_ref[...] = jnp.zeros_like(loss_ref)

        x_t = x_ref[...]
        shift, scale, gate = vec_ref[0:1, :], vec_ref[1:2, :], vec_ref[2:3, :]
        g_pre, g_post = vec_ref[3:4, :], vec_ref[4:5, :]
        w0, w1, w2, ps = cps_ref[0:1, :], cps_ref[1:2, :], cps_ref[2:3, :], cps_ref[3:4, :]
        rx = lax.rsqrt(jnp.mean(x_t * x_t, axis=-1, keepdims=True) + NORM_EPS)
        h = (x_t * rx) * g_pre * (1.0 + scale) + shift
        ht_ref[...] = h.T.astype(BF16)
        proj = _dot(h.astype(BF16), win_ref[...]).astype(BF16).astype(F32)
        u_a, b_a, c_a, g_a, u_p, g_p = _split_proj(proj)
        uc_ref[...] = jnp.concatenate([u_a, c_a], axis=1).astype(BF16)

        z = c_a * u_a
        zcat = jnp.concatenate([zc_ref[...], z], axis=0)
        zc_ref[...] = z[ROW_TILE - CONV_HALO:]
        conv = (w0 * _rows_from_before(zcat, 2)[CONV_HALO:] + w1 * _rows_from_before(zcat, 1)[CONV_HALO:] + w2 * z)
        sig_a = _sigmoid(g_a)
        silu_a = g_a * sig_a
        b_conv = b_a * conv
        y_a = b_conv * silu_a
        fa_ref[...] = jnp.concatenate(
            [silu_a * conv, silu_a * b_a, b_conv * (sig_a + silu_a * (1.0 - sig_a))], axis=1).astype(BF16)

        pcat = jnp.concatenate([pc_ref[...], u_p], axis=0)
        pc_ref[...] = u_p[ROW_TILE - POOL_HALO:]
        counts = _window_counts(i * ROW_TILE, ROW_TILE)
        pooled, mixed = [], []
        for g, w in enumerate(POOL_WINDOWS):
            cols = slice(g * GROUP_D, (g + 1) * GROUP_D)
            s = pcat[:, cols]
            step = 1
            while step < w:
                s = s + _rows_from_before(s, step)
                step *= 2
            pooled_g = (s[POOL_HALO:] * (1.0 / jnp.minimum(counts, float(w))) - u_p[:, cols]).astype(BF16)
            pooled.append(pooled_g)
            mixed.append(_dot(pooled_g, wpool_ref[g]))
        mixed = jnp.concatenate(mixed, axis=1)
        sig_p = _sigmoid(g_p)
        silu_p = g_p * sig_p
        mixed_ps = mixed * ps
        y_p = mixed_ps * silu_p
        fp_ref[...] = jnp.concatenate(
            [(ps * silu_p).astype(BF16), (mixed_ps * (sig_p + silu_p * (1.0 - sig_p))).astype(BF16),
             (silu_p * mixed).astype(BF16)] + pooled, axis=1)

        ycat = jnp.concatenate([y_a, y_p], axis=1)
        ycatt_ref[...] = ycat.T.astype(BF16)
        y_b = _dot(ycat.astype(BF16), wout_ref[...]).astype(BF16)
        y_ref[...] = y_b
        y_t = y_b.astype(F32)
        ry = lax.rsqrt(jnp.mean(y_t * y_t, axis=-1, keepdims=True) + NORM_EPS)
        x_next = x_t + gate * (y_t * ry * g_post)
        if head:
            err = x_next - target_ref[...]
            xo_ref[...] = err * (1.0 / D_MODEL)
            loss_ref[...] += jnp.sum(err * err) * (0.5 / D_MODEL)
        else:
            xo_ref[...] = x_next

    tile = (SUBLANES, LANES)
    return pl.pallas_call(
        body, name=f"forward_layer_{layer}", grid=(n_tiles,),
        in_specs=[row(D_MODEL), _layer_spec(vec.shape, layer), _layer_spec(cps.shape, layer),
                  _layer_spec(wpool.shape, layer), _whole_spec(win.shape), _whole_spec(wout.shape)]
        + [row(D_MODEL)] * head,
        out_specs=[row(D_MODEL), row(D_MODEL), col_t, col_t] + [row(w) for w in widths[1:]] + [_whole_spec(tile)] * head,
        out_shape=[jax.ShapeDtypeStruct((t_len, D_MODEL), F32), jax.ShapeDtypeStruct((t_len, D_MODEL), BF16),
                   jax.ShapeDtypeStruct((D_MODEL, t_len), BF16), jax.ShapeDtypeStruct((D_MODEL, t_len), BF16)]
        + [jax.ShapeDtypeStruct((t_len, w), BF16) for w in widths[1:]] + [jax.ShapeDtypeStruct(tile, F32)] * head,
        scratch_shapes=[pltpu.VMEM((CONV_HALO, CONV_W), F32), pltpu.VMEM((POOL_HALO, POOL_W), F32)],
        compiler_params=_params(dimension_semantics=("arbitrary",)),
    )(x, vec, cps, wpool, win, wout, *([target] * head))


def _backward_layer(layer, dxo, y, x, uc, fa, fp, vec, cps, wpool, wout, win):
    t_len = dxo.shape[0]
    n_tiles = t_len // BWD_TILE
    halo_per_tile = BWD_TILE // POOL_HALO
    rev = lambda cols: pl.BlockSpec((BWD_TILE, cols), lambda i: (n_tiles - 1 - i, 0))
    halo_spec = pl.BlockSpec(
        (POOL_HALO, 2 * CONV_W), lambda i: (jnp.maximum((n_tiles - 1 - i) * halo_per_tile - 1, 0), 0))
    gwpool_shape = (len(POOL_WINDOWS), GROUP_D, GROUP_D)

    def body(dxo_ref, y_ref, x_ref, uc_ref, uch_ref, fa_ref, fp_ref, vec_ref, cps_ref, wpool_ref, wout_ref, win_ref,
             dx_ref, dproj_ref, dy_ref, gwpool_ref, dcps_ref, dvec_ref, gwpool_acc, dcc_ref, qc_ref):
        i = pl.program_id(0)
        tile = n_tiles - 1 - i

        @pl.when(i == 0)
        def _():
            gwpool_acc[...] = jnp.zeros_like(gwpool_acc)
            dcps_ref[...] = jnp.zeros_like(dcps_ref)
            dvec_ref[...] = jnp.zeros_like(dvec_ref)
            dcc_ref[...] = jnp.zeros_like(dcc_ref)
            qc_ref[...] = jnp.zeros_like(qc_ref)

        shift, scale, gate = vec_ref[0:1, :], vec_ref[1:2, :], vec_ref[2:3, :]
        g_pre, g_post = vec_ref[3:4, :], vec_ref[4:5, :]
        w0, w1, w2 = cps_ref[0:1, :], cps_ref[1:2, :], cps_ref[2:3, :]

        dxo_t = dxo_ref[...]
        y_t = y_ref[...].astype(F32)
        ry = lax.rsqrt(jnp.mean(y_t * y_t, axis=-1, keepdims=True) + NORM_EPS)
        yh = y_t * ry
        dvec_ref[2:3, :] += jnp.sum(dxo_t * yh, axis=0, keepdims=True)
        dyh = dxo_t * (gate * g_post)
        dy_b = (ry * (dyh - yh * jnp.mean(dyh * yh, axis=-1, keepdims=True))).astype(BF16)
        dy_ref[...] = dy_b
        dycat = _dot_nt(dy_b, wout_ref[...])
        dy_a, dy_p = dycat[:, :CONV_W], dycat[:, CONV_W:]

        fa_t = fa_ref[...].astype(F32)
        db_a = dy_a * fa_t[:, :CONV_W]
        dconv = dy_a * fa_t[:, CONV_W:2 * CONV_W]
        dg_a = dy_a * fa_t[:, 2 * CONV_W:]
        uc_t = uc_ref[...].astype(F32)
        u_a, c_a = uc_t[:, :CONV_W], uc_t[:, CONV_W:]
        halo = jnp.where(tile > 0, uch_ref[...].astype(F32), 0.0)[POOL_HALO - CONV_HALO:]
        z = c_a * u_a
        zcat = jnp.concatenate([halo[:, CONV_W:] * halo[:, :CONV_W], z], axis=0)
        z1 = _rows_from_before(zcat, 1)[CONV_HALO:]
        z2 = _rows_from_before(zcat, 2)[CONV_HALO:]
        dccat = jnp.concatenate([dconv, dcc_ref[...]], axis=0)
        dc1 = _rows_from_after(dccat, 1)[:BWD_TILE]
        dc2 = _rows_from_after(dccat, 2)[:BWD_TILE]
        dz = w2 * dconv + w1 * dc1 + w0 * dc2
        dcc_ref[...] = dconv[:CONV_HALO]
        dcps_ref[0:1, :] += jnp.sum(dconv * z2, axis=0, keepdims=True)
        dcps_ref[1:2, :] += jnp.sum(dconv * z1, axis=0, keepdims=True)
        dcps_ref[2:3, :] += jnp.sum(dconv * z, axis=0, keepdims=True)
        du_a = dz * c_a
        dc_a = dz * u_a

        dmixed = (dy_p * fp_ref[:, :POOL_W].astype(F32)).astype(BF16)
        dg_p = dy_p * fp_ref[:, POOL_W:2 * POOL_W].astype(F32)
        dcps_ref[3:4, :] += jnp.sum(dy_p * fp_ref[:, 2 * POOL_W:3 * POOL_W].astype(F32), axis=0, keepdims=True)
        counts = _window_counts(tile * BWD_TILE, BWD_TILE)
        du_p, q_head = [], []
        for g, w in enumerate(POOL_WINDOWS):
            cols = slice(g * GROUP_D, (g + 1) * GROUP_D)
            dm_g = dmixed[:, cols]
            dpooled_g = _dot_nt(dm_g, wpool_ref[g])
            gwpool_acc[g] += _dot_tn(fp_ref[:, 3 * POOL_W + g * GROUP_D:3 * POOL_W + (g + 1) * GROUP_D], dm_g)
            q_g = dpooled_g * (1.0 / jnp.minimum(counts, float(w)))
            q_head.append(q_g[:POOL_HALO])
            s = jnp.concatenate([q_g, qc_ref[:, cols]], axis=0)
            step = 1
            while step < w:
                s = s + _rows_from_after(s, step)
                step *= 2
            du_p.append(s[:BWD_TILE] - dpooled_g)
        qc_ref[...] = jnp.concatenate(q_head, axis=1)
        dproj_b = jnp.concatenate([du_a, db_a, dc_a, dg_a] + du_p + [dg_p], axis=1).astype(BF16)
        dproj_ref[...] = dproj_b

        x_t = x_ref[...]
        rx = lax.rsqrt(jnp.mean(x_t * x_t, axis=-1, keepdims=True) + NORM_EPS)
        xn = x_t * rx
        mod_scale = 1.0 + scale
        dh = _dot_nt(dproj_b, win_ref[...])
        dvec_ref[0:1, :] += jnp.sum(dh, axis=0, keepdims=True)
        dvec_ref[1:2, :] += jnp.sum(dh * xn, axis=0, keepdims=True)
        dxn = dh * (g_pre * mod_scale)
        dx_ref[...] = dxo_t + rx * (dxn - xn * jnp.mean(dxn * xn, axis=-1, keepdims=True))

        @pl.when(i == n_tiles - 1)
        def _():
            gwpool_ref[...] = gwpool_acc[...].astype(BF16)
            sum_dh_xn, sum_dxo_yh = dvec_ref[1:2, :], dvec_ref[2:3, :]
            dvec_ref[1:2, :] = sum_dh_xn * g_pre
            dvec_ref[3:4, :] = sum_dh_xn * mod_scale
            dvec_ref[2:3, :] = sum_dxo_yh * g_post
            dvec_ref[4:5, :] = sum_dxo_yh * gate

    return pl.pallas_call(
        body, name=f"backward_layer_{layer}", grid=(n_tiles,),
        in_specs=[rev(D_MODEL), rev(D_MODEL), rev(D_MODEL), rev(2 * CONV_W), halo_spec, rev(3 * CONV_W),
                  rev(4 * POOL_W), _layer_spec(vec.shape, layer), _layer_spec(cps.shape, layer),
                  _layer_spec(wpool.shape, layer), _whole_spec(wout.shape), _whole_spec(win.shape)],
        out_specs=[rev(D_MODEL), rev(IN_COLS), rev(D_MODEL), _whole_spec(gwpool_shape),
                   _whole_spec((SUBLANES, CONV_W)), _whole_spec((SUBLANES, D_MODEL))],
        out_shape=[jax.ShapeDtypeStruct((t_len, D_MODEL), F32), jax.ShapeDtypeStruct((t_len, IN_COLS), BF16),
                   jax.ShapeDtypeStruct((t_len, D_MODEL), BF16), jax.ShapeDtypeStruct(gwpool_shape, BF16),
                   jax.ShapeDtypeStruct((SUBLANES, CONV_W), F32), jax.ShapeDtypeStruct((SUBLANES, D_MODEL), F32)],
        scratch_shapes=[pltpu.VMEM(gwpool_shape, F32), pltpu.VMEM((CONV_HALO, CONV_W), F32),
                        pltpu.VMEM((POOL_HALO, POOL_W), F32)],
        compiler_params=_params(dimension_semantics=("arbitrary",)),
    )(dxo, y, x, uc, uc, fa, fp, vec, cps, wpool, wout, win)


def _weight_grads(layer, h_t, dproj, ycat_t, dy, exchange, after=()):
    t_len = dy.shape[0]
    n_in, n_out = IN_COLS // GWIN_COLS, D_MODEL // GWOUT_COLS
    args = [h_t, dproj, ycat_t, dy, *after]
    in_specs = [_whole_spec(h_t.shape),
                pl.BlockSpec((t_len, GWIN_COLS), lambda s: (0, jnp.minimum(s, n_in - 1))),
                _whole_spec(ycat_t.shape),
                pl.BlockSpec((t_len, GWOUT_COLS), lambda s: (0, jnp.maximum(s - n_in, 0)))] + [HBM] * len(after)
    out_shape = [jax.ShapeDtypeStruct((D_MODEL, IN_COLS), BF16), jax.ShapeDtypeStruct((D_MODEL, D_MODEL), BF16)]
    out_specs = [pl.BlockSpec((D_MODEL, GWIN_COLS), lambda s: (0, jnp.minimum(s, n_in - 1))),
                 pl.BlockSpec((D_MODEL, GWOUT_COLS), lambda s: (0, jnp.maximum(s - n_in, 0)))]
    scratch = []
    aliases, split = _host(exchange, args, in_specs, out_shape, out_specs, scratch)

    def body(*refs):
        (ht_ref, dproj_ref, ycatt_ref, dy_ref, *_, gwin_ref, gwout_ref), hosted = split(refs)
        s = pl.program_id(0)
        if hosted is not None:
            pl.when(s == 0)(hosted[0])

        @pl.when(s < n_in)
        def _():
            gwin_ref[...] = _dot(ht_ref[...], dproj_ref[...]).astype(BF16)

        @pl.when(s >= n_in)
        def _():
            gwout_ref[...] = _dot(ycatt_ref[...], dy_ref[...]).astype(BF16)

        if hosted is not None:
            pl.when(s == n_in + n_out - 1)(hosted[1])

    return pl.pallas_call(
        body, name=f"weight_grads_{layer}", grid=(n_in + n_out,), in_specs=in_specs, out_specs=out_specs,
        out_shape=out_shape, scratch_shapes=scratch, input_output_aliases=aliases,
        compiler_params=_params(dimension_semantics=("arbitrary",)),
    )(*args)


def _add_sibling_blocks(name, layer, grads, received, core, partials, kinds):
    n_arr = len(grads)

    def body(core_ref, *refs):
        mine, theirs, outs = refs[:n_arr], refs[n_arr:2 * n_arr], refs[-n_arr:]
        for a in range(n_arr):
            outs[a][...] = (mine[a][...].astype(F32) + theirs[a][...].astype(F32)).astype(BF16)

    own_of_kind = [
        pl.BlockSpec((D_MODEL, W_IN_SHARD), lambda q, core_ref: (0, 2 * q + core_ref[0])),
        pl.BlockSpec((W_OUT_SHARD, D_MODEL), lambda q, core_ref: (2 * q + core_ref[0], 0)),
        pl.BlockSpec((POOL_SHARD, GROUP_D), lambda q, core_ref: (2 * q + core_ref[0], 0)),
    ]
    shapes = [_BLOCK_SHAPES[k] for k in kinds]
    recv_specs = [pl.BlockSpec((None,) + s, lambda q, core_ref: (q, 0, 0)) for s in shapes]
    out_specs = [pl.BlockSpec((None, None) + s, lambda q, core_ref: (q, layer, 0, 0)) for s in shapes]
    args = [core, *grads, *received]
    in_specs = [own_of_kind[k] for k in kinds] + recv_specs
    aliases = {}
    if partials is not None:
        aliases = {len(args) + a: a for a in range(n_arr)}
        args += list(partials)
        in_specs += [HBM] * n_arr
    return pl.pallas_call(
        body, name=name,
        grid_spec=pltpu.PrefetchScalarGridSpec(
            num_scalar_prefetch=1, grid=(N_CHIP,), in_specs=in_specs, out_specs=out_specs),
        out_shape=[jax.ShapeDtypeStruct((N_CHIP, DEPTH) + s, BF16) for s in shapes],
        input_output_aliases=aliases,
        compiler_params=_params(dimension_semantics=("arbitrary",)),
    )(*args)


def _modulation_columns(c_all, w_ada):
    def body(c_ref, w_ref, cact_ref, out_ref):
        c_t = c_ref[...]
        c_act = c_t * _sigmoid(c_t)
        cact_ref[...] = c_act
        out_ref[...] = jnp.dot(c_act, w_ref[...], preferred_element_type=F32, precision=lax.Precision.HIGHEST)

    return pl.pallas_call(
        body, name="modulation_columns", grid=(DEPTH,),
        in_specs=[pl.BlockSpec((N_DEV, D_MODEL), lambda l: (0, 0)),
                  pl.BlockSpec((None, D_MODEL, W_IN_SHARD), lambda l: (l, 0, 0))],
        out_specs=[pl.BlockSpec((N_DEV, D_MODEL), lambda l: (0, 0)),
                   pl.BlockSpec((N_DEV, W_IN_SHARD), lambda l: (0, l))],
        out_shape=[jax.ShapeDtypeStruct((N_DEV, D_MODEL), F32),
                   jax.ShapeDtypeStruct((N_DEV, DEPTH * W_IN_SHARD), F32)],
        compiler_params=_params(dimension_semantics=("arbitrary",)),
    )(c_all, w_ada)


def _adamw(w, g, m, v):
    m_new = ADAM_B1 * m + (1.0 - ADAM_B1) * g
    v_new = ADAM_B2 * v + (1.0 - ADAM_B2) * (g * g)
    m_hat = m_new / (1.0 - ADAM_B1 ** ADAM_STEP)
    v_hat = v_new / (1.0 - ADAM_B2 ** ADAM_STEP)
    delta = -ADAM_LR * (m_hat / (jnp.sqrt(v_hat) + ADAM_EPS) + ADAM_WD * w)
    return delta, m_new, v_new


def _adamw_w_ada(w, m, v, c_act_t, dmod_cols, exchange):
    big = pl.BlockSpec((None, D_MODEL, W_IN_SHARD), lambda l: (l, 0, 0))
    args = [w, m, v, c_act_t, dmod_cols]
    in_specs = [big, big, big, pl.BlockSpec((D_MODEL, N_DEV), lambda l: (0, 0)),
                pl.BlockSpec((None, N_DEV, W_IN_SHARD), lambda l: (l, 0, 0))]
    out_shape, out_specs, scratch = [jax.ShapeDtypeStruct(w.shape, F32)] * 4, [big] * 4, []
    aliases, split = _host(exchange, args, in_specs, out_shape, out_specs, scratch)

    def body(*refs):
        (w_ref, m_ref, v_ref, ct_ref, dm_ref, g_ref, d_ref, mo_ref, vo_ref), hosted = split(refs)
        if hosted is not None:
            pl.when(pl.program_id(0) == 0)(hosted[0])
        g = ct_ref[:, 0:1] * dm_ref[0:1, :]
        for b in range(1, N_DEV):
            g = g + ct_ref[:, b:b + 1] * dm_ref[b:b + 1, :]
        g_ref[...] = g
        d_ref[...], mo_ref[...], vo_ref[...] = _adamw(w_ref[...], g, m_ref[...], v_ref[...])
        if hosted is not None:
            pl.when(pl.program_id(0) == DEPTH - 1)(hosted[1])

    return pl.pallas_call(
        body, name="adamw_w_ada", grid=(DEPTH,), in_specs=in_specs, out_specs=out_specs, out_shape=out_shape,
        scratch_shapes=scratch, input_output_aliases=aliases,
        compiler_params=_params(dimension_semantics=("arbitrary",)),
    )(*args)


def _sum_chip_partials(own_ref, recv_ref):
    g = own_ref[...].astype(F32)
    for j in range(N_OTHER_CHIPS):
        g = g + recv_ref[j].astype(F32)
    return g


def _partial_specs(row_tile, cols, first_layer=0):
    own = pl.BlockSpec((None, None, row_tile, cols), lambda l, r, chip_ref: (chip_ref[0], first_layer + l, r, 0))
    recv = pl.BlockSpec((N_OTHER_CHIPS, None, row_tile, cols), lambda l, r, chip_ref: (0, first_layer + l, r, 0))
    return own, recv


def _adamw_reduced(name, w, m, v, partial, received, chip, row_tile, layers, continued):
    depth, rows, cols = w.shape
    first, stop = layers

    def body(chip_ref, w_ref, m_ref, v_ref, own_ref, recv_ref, *rest):
        g_ref, d_ref, mo_ref, vo_ref = rest[-4:]
        g = _sum_chip_partials(own_ref, recv_ref)
        g_ref[...] = g
        d_ref[...], mo_ref[...], vo_ref[...] = _adamw(w_ref[...], g, m_ref[...], v_ref[...])

    blk = pl.BlockSpec((None, row_tile, cols), lambda l, r, chip_ref: (first + l, r, 0))
    args = [chip, w, m, v, partial, received]
    in_specs = [blk, blk, blk, *_partial_specs(row_tile, cols, first)]
    aliases = {}
    if continued is not None:
        aliases = {len(args) + k: k for k in range(4)}
        args += list(continued)
        in_specs += [HBM] * 4
    return pl.pallas_call(
        body, name=name,
        grid_spec=pltpu.PrefetchScalarGridSpec(
            num_scalar_prefetch=1, grid=(stop - first, rows // row_tile), in_specs=in_specs, out_specs=[blk] * 4),
        out_shape=[jax.ShapeDtypeStruct(w.shape, F32)] * 4, input_output_aliases=aliases,
        compiler_params=_params(dimension_semantics=("arbitrary", "arbitrary")),
    )(*args)


def _reduce_w_pool(partial, received, chip):
    def body(chip_ref, own_ref, recv_ref, g_ref):
        g_ref[...] = _sum_chip_partials(own_ref, recv_ref)

    return pl.pallas_call(
        body, name="reduce_w_pool",
        grid_spec=pltpu.PrefetchScalarGridSpec(
            num_scalar_prefetch=1, grid=(DEPTH, 1), in_specs=list(_partial_specs(POOL_SHARD, GROUP_D)),
            out_specs=pl.BlockSpec((POOL_SHARD, GROUP_D), lambda l, r, chip_ref: (l, 0))),
        out_shape=jax.ShapeDtypeStruct((DEPTH * POOL_SHARD, GROUP_D), F32),
        compiler_params=_params(dimension_semantics=("arbitrary", "arbitrary")),
    )(chip, partial, received)


def _adamw_small(params):
    n = len(params)

    def body(*refs):
        ins, outs = refs[:4 * n], refs[4 * n:]
        for p in range(n):
            w_ref, g_ref, m_ref, v_ref = ins[4 * p:4 * p + 4]
            d_ref, mo_ref, vo_ref = outs[3 * p:3 * p + 3]
            d_ref[...], mo_ref[...], vo_ref[...] = _adamw(w_ref[...], g_ref[...], m_ref[...], v_ref[...])

    vmem = pl.BlockSpec(memory_space=pltpu.VMEM)
    flat = [a for group in params for a in group]
    out_shape = [jax.ShapeDtypeStruct(group[0].shape, F32) for group in params for _ in range(3)]
    outs = pl.pallas_call(
        body, name="adamw_small", in_specs=[vmem] * len(flat), out_specs=[vmem] * len(out_shape),
        out_shape=out_shape, compiler_params=_params(),
    )(*flat)
    return [tuple(outs[3 * p:3 * p + 3]) for p in range(n)]


def _sum_sources(slabs):
    def body(s_ref, o_ref):
        acc = s_ref[0]
        for b in range(1, N_DEV):
            acc = acc + s_ref[b]
        o_ref[...] = acc

    vmem = pl.BlockSpec(memory_space=pltpu.VMEM)
    return pl.pallas_call(
        body, name="sum_small_grads", in_specs=[vmem], out_specs=vmem,
        out_shape=jax.ShapeDtypeStruct(slabs.shape[1:], F32), compiler_params=_params(),
    )(slabs)


def _to_bf16(a, name, layers=None):
    first, stop = layers or (0, a.shape[0])

    def body(a_ref, o_ref):
        o_ref[...] = a_ref[...].astype(BF16)

    block = (None,) + a.shape[1:]
    return pl.pallas_call(
        body, name=name, grid=(stop - first,), in_specs=[pl.BlockSpec(block, lambda l: (first + l, 0, 0))],
        out_specs=pl.BlockSpec(block, lambda l: (l, 0, 0)),
        out_shape=jax.ShapeDtypeStruct((stop - first,) + a.shape[1:], BF16),
        compiler_params=_params(dimension_semantics=("arbitrary",)),
    )(a)


def kernel(x, c, w_ada, b_ada, g_pre, w_in, w_conv, w_pool, pool_scale, w_out, g_post, loss_target, m_w_ada, m_b_ada, m_g_pre, m_w_in, m_w_conv, m_w_pool, m_pool_scale, m_w_out, m_g_post, v_w_ada, v_b_ada, v_g_pre, v_w_in, v_w_conv, v_w_pool, v_pool_scale, v_w_out, v_g_post):
    mx, my, mc = _mesh_position()
    me = _block_id(mx, my, mc)
    chip = (2 * mx + my).astype(jnp.int32).reshape(1)
    core = mc.astype(jnp.int32).reshape(1)
    x0 = x[0]
    target = loss_target[0]
    conv_shard = w_conv.shape[-1]

    own_small = jnp.concatenate([c, w_conv.reshape(1, DEPTH * 3 * conv_shard)], axis=1)
    all_small = _all_gather_small(own_small, "all_gather_c_w_conv")[:, 0, :]
    gathers = [_gather_weights_start(
        0, _to_bf16(w_in, "cast_w_in_0", (0, 1)), _to_bf16(w_out, "cast_w_out_0", (0, 1)), [all_small])]
    c_all = all_small[:, :D_MODEL]
    w_conv_full = all_small[:, D_MODEL:].reshape(N_DEV, DEPTH, 3, conv_shard).transpose(1, 2, 0, 3).reshape(
        DEPTH, 3, CONV_W)
    cps = jnp.concatenate([w_conv_full, pool_scale[:, None], jnp.zeros((DEPTH, 4, CONV_W), F32)], axis=1)

    c_act, pieces = _modulation_columns(c_all, w_ada)
    upper = (1, DEPTH)
    upper_shards = [_to_bf16(w_in, "cast_w_in_1", upper), _to_bf16(w_out, "cast_w_out_1", upper)]
    wpool_b = _to_bf16(w_pool.reshape(DEPTH, POOL_ROWS, GROUP_D), "cast_w_pool").reshape(w_pool.shape)
    mod_all = _all_gather_small(
        pieces, "all_gather_modulation", [gathers[0][1][0], *upper_shards, wpool_b])
    mod_mine = lax.dynamic_index_in_dim(mod_all, me, axis=1, keepdims=False)
    mod = mod_mine.reshape(N_DEV, DEPTH, W_IN_SHARD).transpose(1, 0, 2).reshape(DEPTH, 3 * D_MODEL) + b_ada
    zeros_d = jnp.zeros((DEPTH, 3, D_MODEL), F32)
    vec = jnp.concatenate([mod.reshape(DEPTH, 3, D_MODEL), g_pre[:, None], g_post[:, None], zeros_d], axis=1)

    gathers.append(_gather_weights_start(1, *upper_shards, [mod_all]))
    gathers = [(first_sems, shards, landing[k], k) for first_sems, shards, landing in gathers
               for k in range(len(landing))]

    xs, kept, wins, wouts = [x0], [], [], []
    for l in range(DEPTH):
        first_sems, shards, zones, index = gathers[l]
        passed_sems, zones = _gather_weights_pass_on(
            l, index, first_sems[1], zones, [vec, cps, wpool_b, gathers[-1][1][0]] if l == 0 else [xs[-1]])
        win, wout = _gather_weights_finish(l, index, first_sems, passed_sems, shards, zones)
        x_next, *for_backward = _forward_layer(
            l, xs[-1], vec, cps, wpool_b, win, wout, target if l == DEPTH - 1 else None)
        xs.append(x_next)
        kept.append(for_backward[:6])
        wins.append(win)
        wouts.append(wout)
    dx, loss_tile = xs[DEPTH], for_backward[6]

    slab_rows = [None] * DEPTH
    partials = received = None
    in_flight = []

    def scatter(layer, grads, from_sibling, after):
        nonlocal partials, received
        partials = _add_sibling_blocks(
            f"grad_add_sibling_{layer}", layer, grads, from_sibling, core, partials, ALL_KINDS)
        chips = _chips_exchange(layer, partials, received, ALL_KINDS)
        sems, partials, received, token = _start_exchange(chips, f"grad_chips_start_{layer}", after)
        in_flight.append((chips, sems, layer))
        return token

    grads_above = None
    for l in reversed(range(DEPTH)):
        y, h_t, ycat_t, uc, fa, fp = kept[l]
        dx, dproj, dy, gwpool, dcps, dvec = _backward_layer(
            l, dx, y, xs[l], uc, fa, fp, vec, cps, wpool_b, wouts[l], wins[l])
        slab_rows[l] = jnp.concatenate(
            [dvec[0], dvec[1], dvec[2], dvec[3], dvec[4], dcps[3], dcps[0], dcps[1], dcps[2],
             loss_tile[0] if l == 0 else jnp.zeros((LANES,), F32)])
        after = []
        if l == 0:
            gather_small = _all_gather_exchange(jnp.stack(slab_rows))
            sems_s, slab, slabs, token = _start_exchange(gather_small, "all_gather_small_grads_start")
            after = [token]
        hosted = _sibling_exchange(grads_above, ALL_KINDS) if grads_above is not None else None
        gwin, gwout, *from_sibling = _weight_grads(l, h_t, dproj, ycat_t, dy, hosted, after)
        if l == 0:
            _, (slabs,) = _finish_exchange(
                gather_small, "all_gather_small_grads_finish", sems_s, slab, slabs, [gwin])
        if grads_above is not None:
            scatter(l + 1, grads_above, from_sibling, [])
        grads_above = [gwin, gwout, gwpool.reshape(POOL_ROWS, GROUP_D)]
        if l <= 1:
            from_sibling = _run_exchange(_sibling_exchange(grads_above, ALL_KINDS), f"grad_exchange_sibling_{l}")
            token = scatter(l, grads_above, from_sibling, [slabs] if l == 0 else [])
            grads_above = None
    grad_x = dx[None]
    chips_0, sems_0, _ = in_flight.pop()

    total = _sum_sources(slabs)
    loss = total[0, SLAB_COLS]
    o = 3 * D_MODEL
    g_b_ada = total[:, :o]
    g_g_pre = total[:, o:o + D_MODEL]
    g_g_post = total[:, o + D_MODEL:o + 2 * D_MODEL]
    g_pool_scale = total[:, o + 2 * D_MODEL:o + 2 * D_MODEL + POOL_W]
    g_conv_full = total[:, o + 2 * D_MODEL + POOL_W:SLAB_COLS].reshape(DEPTH, 3, CONV_W)
    g_w_conv = lax.dynamic_slice_in_dim(g_conv_full, me * conv_shard, conv_shard, axis=2)

    after = [token]
    for chips, sems, l in in_flight:
        partials, received = _finish_exchange(chips, f"grad_chips_finish_{l}", sems, partials, received, after)
        after = []
    upper = (1, DEPTH)
    w_in_upper = _adamw_reduced(
        "adamw_w_in_upper", w_in, m_w_in, v_w_in, partials[0], received[0], chip, ROW_TILE, upper, None)
    w_out_upper = _adamw_reduced(
        "adamw_w_out_upper", w_out, m_w_out, v_w_out, partials[1], received[1], chip, W_OUT_SHARD, upper, None)
    dmod_all = slabs[:, :, :o].reshape(N_DEV, DEPTH, N_DEV, W_IN_SHARD)
    dmod_cols = lax.dynamic_index_in_dim(dmod_all, me, axis=2, keepdims=False).transpose(1, 0, 2) + token[0, 0]
    g_w_ada, d_w_ada, nm_w_ada, nv_w_ada = _adamw_w_ada(w_ada, m_w_ada, v_w_ada, c_act.T, dmod_cols, None)

    partials, received = _finish_exchange(
        chips_0, "grad_chips_finish_0", sems_0, partials, received, [nv_w_ada, w_in_upper[3], w_out_upper[3]])
    gather_pool = _all_gather_exchange(_reduce_w_pool(partials[2], received[2], chip))
    sems_p, pool_rows, pool_landing, token_p = _start_exchange(gather_pool, "all_gather_grad_w_pool_start")
    g_w_in, d_w_in, nm_w_in, nv_w_in = _adamw_reduced(
        "adamw_w_in_0", w_in, m_w_in, v_w_in, partials[0], received[0], chip, ROW_TILE, (0, 1), w_in_upper)
    g_w_out, d_w_out, nm_w_out, nv_w_out = _adamw_reduced(
        "adamw_w_out_0", w_out, m_w_out, v_w_out, partials[1], received[1], chip, W_OUT_SHARD, (0, 1), w_out_upper)
    _, (g_pool_all,) = _finish_exchange(
        gather_pool, "all_gather_grad_w_pool_finish", sems_p, pool_rows, pool_landing, [nv_w_in, nv_w_out])
    g_w_pool = g_pool_all.reshape(N_DEV, DEPTH, POOL_SHARD, GROUP_D).transpose(1, 0, 2, 3).reshape(w_pool.shape)

    flat2 = lambda a: a.reshape(-1, a.shape[-1])
    small = _adamw_small([
        (b_ada, g_b_ada, m_b_ada, v_b_ada),
        (g_pre, g_g_pre, m_g_pre, v_g_pre),
        (flat2(w_conv), flat2(g_w_conv), flat2(m_w_conv), flat2(v_w_conv)),
        (flat2(w_pool), flat2(g_w_pool), flat2(m_w_pool), flat2(v_w_pool)),
        (pool_scale, g_pool_scale, m_pool_scale, v_pool_scale),
        (g_post, g_g_post, m_g_post, v_g_post),
    ])
    (d_b_ada, nm_b_ada, nv_b_ada), (d_g_pre, nm_g_pre, nv_g_pre), conv_upd, pool_upd, \
        (d_ps, nm_ps, nv_ps), (d_g_post, nm_g_post, nv_g_post) = small
    d_w_conv, nm_w_conv, nv_w_conv = (a.reshape(w_conv.shape) for a in conv_upd)
    d_w_pool, nm_w_pool, nv_w_pool = (a.reshape(w_pool.shape) for a in pool_upd)

    return (loss, grad_x,
            g_w_ada, g_b_ada, g_g_pre, g_w_in, g_w_conv, g_w_pool, g_pool_scale, g_w_out, g_g_post,
            d_w_ada, d_b_ada, d_g_pre, d_w_in, d_w_conv, d_w_pool, d_ps, d_w_out, d_g_post,
            nm_w_ada, nm_b_ada, nm_g_pre, nm_w_in, nm_w_conv, nm_w_pool, nm_ps, nm_w_out, nm_g_post,
            nv_w_ada, nv_b_ada, nv_g_pre, nv_w_in, nv_w_conv, nv_w_pool, nv_ps, nv_w_out, nv_g_post)
```

```python
import jax
import jax.numpy as jnp
from jax import lax
from jax.experimental import pallas as pl
from jax.experimental.pallas import tpu as pltpu

F32 = jnp.float32
BF16 = jnp.bfloat16

D_MODEL = 1024
DEPTH = 4
CONV_W = 512
POOL_W = 512
POOL_WINDOWS = (2, 4, 8, 16)
GROUP_D = 128
IN_COLS = 4 * CONV_W + 2 * POOL_W
NORM_EPS = 1e-6

ADAM_LR = 0.001
ADAM_B1 = 0.9
ADAM_B2 = 0.999
ADAM_EPS = 1e-08
ADAM_WD = 0.01
ADAM_STEP = 10

N_DEV = 8
N_CHIP = 4
N_OTHER_CHIPS = N_CHIP - 1
MESH = pl.DeviceIdType.MESH
W_IN_SHARD = IN_COLS // N_DEV
W_OUT_SHARD = D_MODEL // N_DEV
POOL_ROWS = len(POOL_WINDOWS) * GROUP_D
POOL_SHARD = POOL_ROWS // N_DEV

SUBLANES = 8
LANES = 128
VMEM_LIMIT_BYTES = 56 * 1024 * 1024
ROW_TILE = 512
BWD_TILE = 256
GWIN_COLS = 768
GWOUT_COLS = 512
POOL_HALO = 16
CONV_HALO = SUBLANES

SLAB_COLS = 3 * D_MODEL + D_MODEL + D_MODEL + POOL_W + 3 * CONV_W

HBM = pl.BlockSpec(memory_space=pl.ANY)


def _params(**kw):
    return pltpu.CompilerParams(vmem_limit_bytes=VMEM_LIMIT_BYTES, **kw)


def _sigmoid(v):
    return 1.0 / (1.0 + jnp.exp(-v))


def _dot(a, b):
    return jnp.dot(a, b, preferred_element_type=F32)


def _dot_tn(a, b):
    return lax.dot_general(a, b, (((0,), (0,)), ((), ())), preferred_element_type=F32)


def _dot_nt(a, b):
    return lax.dot_general(a, b, (((1,), (1,)), ((), ())), preferred_element_type=F32)


def _rows_from_before(v, k):
    return pltpu.roll(v, k, 0)


def _rows_from_after(v, k):
    return pltpu.roll(v, v.shape[0] - k, 0)


def _window_counts(t0, rows):
    return (lax.broadcasted_iota(jnp.int32, (rows, 1), 0) + (t0 + 1)).astype(F32)


def _split_proj(p32):
    cw = CONV_W
    return (p32[:, 0 * cw:1 * cw], p32[:, 1 * cw:2 * cw], p32[:, 2 * cw:3 * cw], p32[:, 3 * cw:4 * cw],
            p32[:, 4 * cw:4 * cw + POOL_W], p32[:, 4 * cw + POOL_W:])


def _layer_spec(shape, layer):
    nd = len(shape)
    return pl.BlockSpec((None,) + tuple(shape[1:]), lambda i, _l=layer, _n=nd: (_l,) + (0,) * (_n - 1))


def _whole_spec(shape):
    return pl.BlockSpec(tuple(shape), lambda i, _n=len(shape): (0,) * _n, pipeline_mode=pl.Buffered(1))


def _mesh_position():
    return lax.axis_index("x"), lax.axis_index("y"), lax.axis_index("c")


def _block_id(x, y, c):
    return 4 * x + 2 * y + c


def _other_chips(x, y):
    return [(x ^ 1, y), (x, y ^ 1), (x ^ 1, y ^ 1)]


def _col_block(ref, blk):
    return ref.at[:, pl.ds(pl.multiple_of(blk * W_IN_SHARD, LANES), W_IN_SHARD)]


def _row_block(rows):
    def block(ref, blk):
        return ref.at[pl.ds(pl.multiple_of(blk * rows, rows), rows), :]
    return block


_BLOCK_OF = (_col_block, _row_block(W_OUT_SHARD), _row_block(POOL_SHARD))
_BLOCK_SHAPES = ((D_MODEL, W_IN_SHARD), (W_OUT_SHARD, D_MODEL), (POOL_SHARD, GROUP_D))


class _Exchange:
    def __init__(self, inputs, out_shapes, aliases, sem_shapes, make):
        self.inputs, self.out_shapes, self.aliases, self.sem_shapes, self.make = (
            list(inputs), list(out_shapes), dict(aliases), list(sem_shapes), make)


def _run_exchange(exchange, name):
    n_in, n_out = len(exchange.inputs), len(exchange.out_shapes)

    def body(*refs):
        start, finish = exchange.make(refs[:n_in], refs[n_in:n_in + n_out], refs[n_in + n_out:])
        start()
        finish()

    return pl.pallas_call(
        body, name=name, in_specs=[HBM] * n_in, out_specs=[HBM] * n_out, out_shape=exchange.out_shapes,
        scratch_shapes=exchange.sem_shapes, input_output_aliases=exchange.aliases, compiler_params=_params(),
    )(*exchange.inputs)


_SEM = pl.BlockSpec(memory_space=pltpu.SEMAPHORE)
_DATAFLOW = pltpu.SideEffectType.DATAFLOW_SIDE_EFFECTING


def _start_exchange(exchange, name, after=()):
    n_in, n_out, n_sem = len(exchange.inputs), len(exchange.out_shapes), len(exchange.sem_shapes)
    sources = [i for i in range(n_in) if i not in exchange.aliases]
    aliases = {i: n_sem + k for k, i in enumerate(sources)}
    aliases.update({i: n_sem + len(sources) + o for i, o in exchange.aliases.items()})

    def body(*refs):
        in_refs = refs[:n_in]
        outs = refs[n_in + len(after):]
        sems = outs[:n_sem]
        out_refs = outs[n_sem + len(sources):n_sem + len(sources) + n_out]
        exchange.make(in_refs, out_refs, sems)[0]()
        refs[-1][...] = jnp.zeros_like(refs[-1])

    outs = pl.pallas_call(
        body, name=name, in_specs=[HBM] * (n_in + len(after)),
        out_specs=[_SEM] * n_sem + [HBM] * (len(sources) + n_out) + [pl.BlockSpec(memory_space=pltpu.VMEM)],
        out_shape=(exchange.sem_shapes + [pltpu.HBM(exchange.inputs[i].shape, exchange.inputs[i].dtype) for i in sources]
                   + [pltpu.HBM(s.shape, s.dtype) for s in exchange.out_shapes]
                   + [jax.ShapeDtypeStruct((SUBLANES, LANES), F32)]),
        input_output_aliases=aliases, compiler_params=_params(has_side_effects=_DATAFLOW),
    )(*exchange.inputs, *after)
    return outs[:n_sem], outs[n_sem:n_sem + len(sources)], outs[n_sem + len(sources):-1], outs[-1]


def _finish_exchange(exchange, name, sems, sources, landing, after):
    n_src, n_out, n_sem = len(sources), len(landing), len(sems)
    n_in = len(exchange.inputs)
    source_at = [i for i in range(n_in) if i not in exchange.aliases]

    def body(*refs):
        src_refs, out_refs = refs[:n_src], refs[n_src:n_src + n_out]
        sem_refs = refs[n_src + n_out:n_src + n_out + n_sem]
        in_refs = [None] * n_in
        for k, i in enumerate(source_at):
            in_refs[i] = src_refs[k]
        for i, o in exchange.aliases.items():
            in_refs[i] = out_refs[o]
        exchange.make(in_refs, out_refs, sem_refs)[1]()

    arrays = list(sources) + list(landing)
    outs = pl.pallas_call(
        body, name=name, in_specs=[HBM] * len(arrays) + [_SEM] * n_sem + [HBM] * len(after),
        out_specs=[HBM] * len(arrays), out_shape=[pltpu.HBM(a.shape, a.dtype) for a in arrays],
        input_output_aliases={i: i for i in range(len(arrays))}, compiler_params=_params(has_side_effects=_DATAFLOW),
    )(*arrays, *sems, *after)
    return outs[:n_src], outs[n_src:]


N_GATHERED = 2
FIRST_COPIES = 1 + N_OTHER_CHIPS
_GATHERED_SHAPES = ((D_MODEL, IN_COLS), (D_MODEL, D_MODEL))


def _first_sem(layer, a, k):
    return (layer * N_GATHERED + a) * FIRST_COPIES + k


def _gather_copy(window_of, full_ref, blk, send_sem, recv_sem, to, src=None):
    window = window_of(full_ref, blk)
    return pltpu.make_async_remote_copy(
        src_ref=window if src is None else src, dst_ref=window, send_sem=send_sem, recv_sem=recv_sem,
        device_id=to, device_id_type=MESH)


def _first_copies(layer, shard_refs, full_refs, send_sems, recv_sems, local_sems):
    x, y, c = _mesh_position()
    me = _block_id(x, y, c)
    own, remote = [], []
    for a in range(N_GATHERED):
        shard = shard_refs[a].at[layer]
        own.append(pltpu.make_async_copy(
            shard, _BLOCK_OF[a](full_refs[a], me), local_sems.at[layer * N_GATHERED + a]))
        targets = [(x, y, 1 - c)] + [(*chip, c) for chip in _other_chips(x, y)]
        remote += [_gather_copy(_BLOCK_OF[a], full_refs[a], me, send_sems.at[_first_sem(layer, a, k)],
                                recv_sems.at[_first_sem(layer, a, k)], to, src=shard)
                   for k, to in enumerate(targets)]
    return own, remote


def _gather_weights_start(first_layer, win_shards, wout_shards, after):
    n_layers = win_shards.shape[0]
    n_first = n_layers * N_GATHERED * FIRST_COPIES
    sem_shapes = [pltpu.SemaphoreType.DMA((n_first,)), pltpu.SemaphoreType.DMA((n_first,)),
                  pltpu.SemaphoreType.DMA((n_layers * N_GATHERED,))]
    shards = [win_shards, wout_shards]

    def body(win_sh, wout_sh, *rest):
        send_sems, recv_sems, local_sems, win_thru, wout_thru, *landing = rest[len(after):]
        for layer in range(n_layers):
            own, remote = _first_copies(layer, (win_sh, wout_sh), landing[N_GATHERED * layer:N_GATHERED * (layer + 1)],
                                        send_sems, recv_sems, local_sems)
            for cp in own + remote:
                cp.start()

    outs = pl.pallas_call(
        body, name=f"all_gather_weights_start_{first_layer}", in_specs=[HBM] * (2 + len(after)),
        out_specs=[_SEM] * 3 + [HBM] * (2 + n_layers * N_GATHERED),
        out_shape=(sem_shapes + [pltpu.HBM(s.shape, s.dtype) for s in shards]
                   + [pltpu.HBM(s, BF16) for _ in range(n_layers) for s in _GATHERED_SHAPES]),
        input_output_aliases={0: 3, 1: 4}, compiler_params=_params(has_side_effects=_DATAFLOW),
    )(*shards, *after)
    landing = outs[5:]
    return outs[:3], outs[3:5], [landing[N_GATHERED * l:N_GATHERED * (l + 1)] for l in range(n_layers)]


def _passed_on_copies(full_refs, send_sems, recv_sems, core_of_block):
    x, y, c = _mesh_position()
    return [_gather_copy(_BLOCK_OF[a], full_refs[a], _block_id(*chip, core_of_block),
                         send_sems.at[a * N_OTHER_CHIPS + j], recv_sems.at[a * N_OTHER_CHIPS + j], (x, y, 1 - c))
            for a in range(N_GATHERED) for j, chip in enumerate(_other_chips(x, y))]


def _gather_weights_pass_on(layer, index, first_recv_sems, landing, after):
    n = N_GATHERED * N_OTHER_CHIPS

    def body(win_ref, wout_ref, first_recv, *rest):
        send_sems, recv_sems = rest[len(after):len(after) + 2]
        x, y, c = _mesh_position()
        full_refs = (win_ref, wout_ref)
        passed = _passed_on_copies(full_refs, send_sems, recv_sems, c)
        for a in range(N_GATHERED):
            for j, chip in enumerate(_other_chips(x, y)):
                sem = _first_sem(index, a, 1 + j)
                _gather_copy(_BLOCK_OF[a], full_refs[a], _block_id(*chip, c), first_recv.at[sem], first_recv.at[sem],
                             (x, y, c)).wait_recv()
                passed[a * N_OTHER_CHIPS + j].start()

    outs = pl.pallas_call(
        body, name=f"all_gather_weights_pass_on_{layer}", in_specs=[HBM] * N_GATHERED + [_SEM] + [HBM] * len(after),
        out_specs=[_SEM] * 2 + [HBM] * N_GATHERED,
        out_shape=[pltpu.SemaphoreType.DMA((n,)), pltpu.SemaphoreType.DMA((n,))]
        + [pltpu.HBM(a.shape, a.dtype) for a in landing],
        input_output_aliases={a: 2 + a for a in range(N_GATHERED)},
        compiler_params=_params(has_side_effects=_DATAFLOW),
    )(*landing, first_recv_sems, *after)
    return outs[:2], outs[2:]


def _gather_weights_finish(layer, index, first_sems, passed_sems, shards, landing):
    def body(win_ref, wout_ref, first_send, first_recv, local_sems, passed_send, passed_recv, win_sh, wout_sh, *thru):
        x, y, c = _mesh_position()
        full_refs = (win_ref, wout_ref)
        own, remote = _first_copies(index, (win_sh, wout_sh), full_refs, first_send, first_recv, local_sems)
        for a in range(N_GATHERED):
            sem = _first_sem(index, a, 0)
            _gather_copy(_BLOCK_OF[a], full_refs[a], _block_id(x, y, 1 - c), first_recv.at[sem], first_recv.at[sem],
                         (x, y, c)).wait_recv()
        for cp in _passed_on_copies(full_refs, passed_send, passed_recv, 1 - c):
            cp.wait_recv()
        for cp in remote + _passed_on_copies(full_refs, passed_send, passed_recv, c):
            cp.wait_send()
        for cp in own:
            cp.wait()

    return pl.pallas_call(
        body, name=f"all_gather_weights_finish_{layer}", in_specs=[HBM] * N_GATHERED + [_SEM] * 5 + [HBM] * 2,
        out_specs=[HBM] * N_GATHERED, out_shape=[pltpu.HBM(a.shape, a.dtype) for a in landing],
        input_output_aliases={a: a for a in range(N_GATHERED)}, compiler_params=_params(has_side_effects=_DATAFLOW),
    )(*landing, *first_sems, *passed_sems, *shards)


ALL_KINDS = (0, 1, 2)


def _sibling_exchange(grads, kinds):
    n_arr = len(grads)

    def make(in_refs, out_refs, sems):
        send_sems, recv_sems = sems
        x, y, c = _mesh_position()
        copies = [pltpu.make_async_remote_copy(
            src_ref=_BLOCK_OF[kinds[a]](in_refs[a], 2 * q + (1 - c)), dst_ref=out_refs[a].at[q],
            send_sem=send_sems.at[a, q], recv_sem=recv_sems.at[a, q], device_id=(x, y, 1 - c), device_id_type=MESH)
            for a in range(n_arr) for q in range(N_CHIP)]

        def start():
            for cp in copies:
                cp.start()

        def finish():
            for cp in copies:
                cp.wait_recv()
            for cp in copies:
                cp.wait_send()

        return start, finish

    return _Exchange(
        grads, [jax.ShapeDtypeStruct((N_CHIP,) + _BLOCK_SHAPES[k], BF16) for k in kinds], {},
        [pltpu.SemaphoreType.DMA((n_arr, N_CHIP)), pltpu.SemaphoreType.DMA((n_arr, N_CHIP))], make)


def _chips_exchange(layer, partials, received, kinds):
    n_arr = len(partials)

    def make(in_refs, out_refs, sems):
        send_sems, recv_sems = sems
        x, y, c = _mesh_position()
        copies = [pltpu.make_async_remote_copy(
            src_ref=in_refs[a].at[2 * qx + qy, layer], dst_ref=out_refs[a].at[j, layer],
            send_sem=send_sems.at[a * N_OTHER_CHIPS + j], recv_sem=recv_sems.at[a * N_OTHER_CHIPS + j],
            device_id=(qx, qy, c), device_id_type=MESH)
            for a in range(n_arr) for j, (qx, qy) in enumerate(_other_chips(x, y))]

        def start():
            for cp in copies:
                cp.start()

        def finish():
            for cp in copies:
                cp.wait_recv()
            for cp in copies:
                cp.wait_send()

        return start, finish

    inputs = list(partials)
    aliases = {}
    if received is not None:
        inputs += list(received)
        aliases = {n_arr + a: a for a in range(n_arr)}
    return _Exchange(
        inputs, [jax.ShapeDtypeStruct((N_OTHER_CHIPS, DEPTH) + _BLOCK_SHAPES[k], BF16) for k in kinds], aliases,
        [pltpu.SemaphoreType.DMA((n_arr * N_OTHER_CHIPS,)), pltpu.SemaphoreType.DMA((n_arr * N_OTHER_CHIPS,))], make)


def _all_gather_exchange(v):
    def make(in_refs, out_refs, sems):
        send_sems, recv_sems, local_sem = sems
        x, y, c = _mesh_position()
        me = _block_id(x, y, c)
        own = pltpu.make_async_copy(in_refs[0], out_refs[0].at[me], local_sem.at[0])
        sends, arrivals = [], []
        for k in range(1, N_DEV):
            px, py, pc = x ^ ((k >> 2) & 1), y ^ ((k >> 1) & 1), c ^ (k & 1)
            sends.append(pltpu.make_async_remote_copy(
                src_ref=in_refs[0], dst_ref=out_refs[0].at[me], send_sem=send_sems.at[k - 1],
                recv_sem=recv_sems.at[k - 1], device_id=(px, py, pc), device_id_type=MESH))
            arrivals.append(pltpu.make_async_remote_copy(
                src_ref=in_refs[0], dst_ref=out_refs[0].at[_block_id(px, py, pc)], send_sem=send_sems.at[k - 1],
                recv_sem=recv_sems.at[k - 1], device_id=(x, y, c), device_id_type=MESH))

        def start():
            for cp in [own] + sends:
                cp.start()

        def finish():
            for cp in arrivals:
                cp.wait_recv()
            for cp in sends:
                cp.wait_send()
            own.wait()

        return start, finish

    return _Exchange(
        [v], [jax.ShapeDtypeStruct((N_DEV,) + v.shape, v.dtype)], {},
        [pltpu.SemaphoreType.DMA((N_DEV - 1,)), pltpu.SemaphoreType.DMA((N_DEV - 1,)),
         pltpu.SemaphoreType.DMA((1,))], make)


def _host(exchange, args, in_specs, out_shape, out_specs, scratch):
    n_own = (len(args), len(out_shape), len(scratch))
    if exchange is None:
        return {}, lambda refs: (refs, None)
    n_ex = (len(exchange.inputs), len(exchange.out_shapes), len(exchange.sem_shapes))
    aliases = {n_own[0] + i: n_own[1] + o for i, o in exchange.aliases.items()}
    args += exchange.inputs
    in_specs += [HBM] * n_ex[0]
    out_shape += exchange.out_shapes
    out_specs += [HBM] * n_ex[1]
    scratch += exchange.sem_shapes

    def split(refs):
        own, theirs, at = [], [], 0
        for mine, ex in zip(n_own, n_ex):
            own += refs[at:at + mine]
            theirs.append(refs[at + mine:at + mine + ex])
            at += mine + ex
        return own, exchange.make(*theirs)

    return aliases, split


def _all_gather_small(v, name, after=()):
    vmem = pl.BlockSpec(memory_space=pltpu.VMEM)

    def body(v_ref, *rest):
        out_ref, send_sems, recv_sems = rest[len(after):]
        x, y, c = _mesh_position()
        me = _block_id(x, y, c)
        out_ref[me] = v_ref[...]
        sends = []
        for k in range(1, N_DEV):
            px, py, pc = x ^ ((k >> 2) & 1), y ^ ((k >> 1) & 1), c ^ (k & 1)
            send = pltpu.make_async_remote_copy(
                src_ref=v_ref, dst_ref=out_ref.at[me], send_sem=send_sems.at[k - 1], recv_sem=recv_sems.at[k - 1],
                device_id=(px, py, pc), device_id_type=MESH)
            send.start()
            sends.append((send, _block_id(px, py, pc)))
        for k, (send, peer) in enumerate(sends):
            pltpu.make_async_remote_copy(
                src_ref=v_ref, dst_ref=out_ref.at[peer], send_sem=send_sems.at[k], recv_sem=recv_sems.at[k],
                device_id=(x, y, c), device_id_type=MESH).wait_recv()
        for send, _ in sends:
            send.wait_send()

    return pl.pallas_call(
        body, name=name, in_specs=[vmem] + [HBM] * len(after), out_specs=vmem,
        out_shape=jax.ShapeDtypeStruct((N_DEV,) + v.shape, v.dtype),
        scratch_shapes=[pltpu.SemaphoreType.DMA((N_DEV - 1,)), pltpu.SemaphoreType.DMA((N_DEV - 1,))],
        compiler_params=_params(),
    )(v, *after)


def _forward_layer(layer, x, vec, cps, wpool, win, wout, target=None):
    t_len = x.shape[0]
    n_tiles = t_len // ROW_TILE
    row = lambda cols: pl.BlockSpec((ROW_TILE, cols), lambda i: (i, 0))
    col_t = pl.BlockSpec((D_MODEL, ROW_TILE), lambda i: (0, i))
    widths = (D_MODEL, 2 * CONV_W, 3 * CONV_W, 4 * POOL_W)
    head = target is not None

    def body(x_ref, vec_ref, cps_ref, wpool_ref, win_ref, wout_ref, *rest):
        target_ref = rest[0] if head else None
        xo_ref, y_ref, ht_ref, ycatt_ref, uc_ref, fa_ref, fp_ref = rest[head:head + 7]
        loss_ref = rest[head + 7] if head else None
        zc_ref, pc_ref = rest[-2:]
        i = pl.program_id(0)

        @pl.when(i == 0)
        def _():
            zc_ref[...] = jnp.zeros_like(zc_ref)
            pc_ref[...] = jnp.zeros_like(pc_ref)
            if head:
                loss_ref[...] = jnp.zeros_like(loss_ref)

        x_t = x_ref[...]
        shift, scale, gate = vec_ref[0:1, :], vec_ref[1:2, :], vec_ref[2:3, :]
        g_pre, g_post = vec_ref[3:4, :], vec_ref[4:5, :]
        w0, w1, w2, ps = cps_ref[0:1, :], cps_ref[1:2, :], cps_ref[2:3, :], cps_ref[3:4, :]
        rx = lax.rsqrt(jnp.mean(x_t * x_t, axis=-1, keepdims=True) + NORM_EPS)
        h = (x_t * rx) * g_pre * (1.0 + scale) + shift
        ht_ref[...] = h.T.astype(BF16)
        proj = _dot(h.astype(BF16), win_ref[...]).astype(BF16).astype(F32)
        u_a, b_a, c_a, g_a, u_p, g_p = _split_proj(proj)
        uc_ref[...] = jnp.concatenate([u_a, c_a], axis=1).astype(BF16)

        z = c_a * u_a
        zcat = jnp.concatenate([zc_ref[...], z], axis=0)
        zc_ref[...] = z[ROW_TILE - CONV_HALO:]
        conv = (w0 * _rows_from_before(zcat, 2)[CONV_HALO:] + w1 * _rows_from_before(zcat, 1)[CONV_HALO:] + w2 * z)
        sig_a = _sigmoid(g_a)
        silu_a = g_a * sig_a
        b_conv = b_a * conv
        y_a = b_conv * silu_a
        fa_ref[...] = jnp.concatenate(
            [silu_a * conv, silu_a * b_a, b_conv * (sig_a + silu_a * (1.0 - sig_a))], axis=1).astype(BF16)

        pcat = jnp.concatenate([pc_ref[...], u_p], axis=0)
        pc_ref[...] = u_p[ROW_TILE - POOL_HALO:]
        counts = _window_counts(i * ROW_TILE, ROW_TILE)
        pooled, mixed = [], []
        for g, w in enumerate(POOL_WINDOWS):
            cols = slice(g * GROUP_D, (g + 1) * GROUP_D)
            s = pcat[:, cols]
            step = 1
            while step < w:
                s = s + _rows_from_before(s, step)
                step *= 2
            pooled_g = (s[POOL_HALO:] * (1.0 / jnp.minimum(counts, float(w))) - u_p[:, cols]).astype(BF16)
            pooled.append(pooled_g)
            mixed.append(_dot(pooled_g, wpool_ref[g]))
        mixed = jnp.concatenate(mixed, axis=1)
        sig_p = _sigmoid(g_p)
        silu_p = g_p * sig_p
        mixed_ps = mixed * ps
        y_p = mixed_ps * silu_p
        fp_ref[...] = jnp.concatenate(
            [(ps * silu_p).astype(BF16), (mixed_ps * (sig_p + silu_p * (1.0 - sig_p))).astype(BF16),
             (silu_p * mixed).astype(BF16)] + pooled, axis=1)

        ycat = jnp.concatenate([y_a, y_p], axis=1)
        ycatt_ref[...] = ycat.T.astype(BF16)
        y_b = _dot(ycat.astype(BF16), wout_ref[...]).astype(BF16)
        y_ref[...] = y_b
        y_t = y_b.astype(F32)
        ry = lax.rsqrt(jnp.mean(y_t * y_t, axis=-1, keepdims=True) + NORM_EPS)
        x_next = x_t + gate * (y_t * ry * g_post)
        if head:
            err = x_next - target_ref[...]
            xo_ref[...] = err * (1.0 / D_MODEL)
            loss_ref[...] += jnp.sum(err * err) * (0.5 / D_MODEL)
        else:
            xo_ref[...] = x_next

    tile = (SUBLANES, LANES)
    return pl.pallas_call(
        body, name=f"forward_layer_{layer}", grid=(n_tiles,),
        in_specs=[row(D_MODEL), _layer_spec(vec.shape, layer), _layer_spec(cps.shape, layer),
                  _layer_spec(wpool.shape, layer), _whole_spec(win.shape), _whole_spec(wout.shape)]
        + [row(D_MODEL)] * head,
        out_specs=[row(D_MODEL), row(D_MODEL), col_t, col_t] + [row(w) for w in widths[1:]] + [_whole_spec(tile)] * head,
        out_shape=[jax.ShapeDtypeStruct((t_len, D_MODEL), F32), jax.ShapeDtypeStruct((t_len, D_MODEL), BF16),
                   jax.ShapeDtypeStruct((D_MODEL, t_len), BF16), jax.ShapeDtypeStruct((D_MODEL, t_len), BF16)]
        + [jax.ShapeDtypeStruct((t_len, w), BF16) for w in widths[1:]] + [jax.ShapeDtypeStruct(tile, F32)] * head,
        scratch_shapes=[pltpu.VMEM((CONV_HALO, CONV_W), F32), pltpu.VMEM((POOL_HALO, POOL_W), F32)],
        compiler_params=_params(dimension_semantics=("arbitrary",)),
    )(x, vec, cps, wpool, win, wout, *([target] * head))


def _backward_layer(layer, dxo, y, x, uc, fa, fp, vec, cps, wpool, wout, win):
    t_len = dxo.shape[0]
    n_tiles = t_len // BWD_TILE
    halo_per_tile = BWD_TILE // POOL_HALO
    rev = lambda cols: pl.BlockSpec((BWD_TILE, cols), lambda i: (n_tiles - 1 - i, 0))
    halo_spec = pl.BlockSpec(
        (POOL_HALO, 2 * CONV_W), lambda i: (jnp.maximum((n_tiles - 1 - i) * halo_per_tile - 1, 0), 0))
    gwpool_shape = (len(POOL_WINDOWS), GROUP_D, GROUP_D)

    def body(dxo_ref, y_ref, x_ref, uc_ref, uch_ref, fa_ref, fp_ref, vec_ref, cps_ref, wpool_ref, wout_ref, win_hbm,
             dx_ref, dproj_ref, dy_ref, gwpool_ref, dcps_ref, dvec_ref, gwpool_acc, dcc_ref, qc_ref, win_ref, win_sem):
        i = pl.program_id(0)
        tile = n_tiles - 1 - i
        load_win = pltpu.make_async_copy(win_hbm, win_ref, win_sem)
        pl.when(i == 0)(load_win.start)

        @pl.when(i == 0)
        def _():
            gwpool_acc[...] = jnp.zeros_like(gwpool_acc)
            dcps_ref[...] = jnp.zeros_like(dcps_ref)
            dvec_ref[...] = jnp.zeros_like(dvec_ref)
            dcc_ref[...] = jnp.zeros_like(dcc_ref)
            qc_ref[...] = jnp.zeros_like(qc_ref)

        shift, scale, gate = vec_ref[0:1, :], vec_ref[1:2, :], vec_ref[2:3, :]
        g_pre, g_post = vec_ref[3:4, :], vec_ref[4:5, :]
        w0, w1, w2 = cps_ref[0:1, :], cps_ref[1:2, :], cps_ref[2:3, :]

        dxo_t = dxo_ref[...]
        y_t = y_ref[...].astype(F32)
        ry = lax.rsqrt(jnp.mean(y_t * y_t, axis=-1, keepdims=True) + NORM_EPS)
        yh = y_t * ry
        dvec_ref[2:3, :] += jnp.sum(dxo_t * yh, axis=0, keepdims=True)
        dyh = dxo_t * (gate * g_post)
        dy_b = (ry * (dyh - yh * jnp.mean(dyh * yh, axis=-1, keepdims=True))).astype(BF16)
        dy_ref[...] = dy_b
        dycat = _dot_nt(dy_b, wout_ref[...])
        dy_a, dy_p = dycat[:, :CONV_W], dycat[:, CONV_W:]

        fa_t = fa_ref[...].astype(F32)
        db_a = dy_a * fa_t[:, :CONV_W]
        dconv = dy_a * fa_t[:, CONV_W:2 * CONV_W]
        dg_a = dy_a * fa_t[:, 2 * CONV_W:]
        uc_t = uc_ref[...].astype(F32)
        u_a, c_a = uc_t[:, :CONV_W], uc_t[:, CONV_W:]
        halo = jnp.where(tile > 0, uch_ref[...].astype(F32), 0.0)[POOL_HALO - CONV_HALO:]
        z = c_a * u_a
        zcat = jnp.concatenate([halo[:, CONV_W:] * halo[:, :CONV_W], z], axis=0)
        z1 = _rows_from_before(zcat, 1)[CONV_HALO:]
        z2 = _rows_from_before(zcat, 2)[CONV_HALO:]
        dccat = jnp.concatenate([dconv, dcc_ref[...]], axis=0)
        dc1 = _rows_from_after(dccat, 1)[:BWD_TILE]
        dc2 = _rows_from_after(dccat, 2)[:BWD_TILE]
        dz = w2 * dconv + w1 * dc1 + w0 * dc2
        dcc_ref[...] = dconv[:CONV_HALO]
        dcps_ref[0:1, :] += jnp.sum(dconv * z2, axis=0, keepdims=True)
        dcps_ref[1:2, :] += jnp.sum(dconv * z1, axis=0, keepdims=True)
        dcps_ref[2:3, :] += jnp.sum(dconv * z, axis=0, keepdims=True)
        du_a = dz * c_a
        dc_a = dz * u_a

        dmixed = (dy_p * fp_ref[:, :POOL_W].astype(F32)).astype(BF16)
        dg_p = dy_p * fp_ref[:, POOL_W:2 * POOL_W].astype(F32)
        dcps_ref[3:4, :] += jnp.sum(dy_p * fp_ref[:, 2 * POOL_W:3 * POOL_W].astype(F32), axis=0, keepdims=True)
        counts = _window_counts(tile * BWD_TILE, BWD_TILE)
        du_p, q_head = [], []
        for g, w in enumerate(POOL_WINDOWS):
            cols = slice(g * GROUP_D, (g + 1) * GROUP_D)
            dm_g = dmixed[:, cols]
            dpooled_g = _dot_nt(dm_g, wpool_ref[g])
            gwpool_acc[g] += _dot_tn(fp_ref[:, 3 * POOL_W + g * GROUP_D:3 * POOL_W + (g + 1) * GROUP_D], dm_g)
            q_g = dpooled_g * (1.0 / jnp.minimum(counts, float(w)))
            q_head.append(q_g[:POOL_HALO])
            s = jnp.concatenate([q_g, qc_ref[:, cols]], axis=0)
            step = 1
            while step < w:
                s = s + _rows_from_after(s, step)
                step *= 2
            du_p.append(s[:BWD_TILE] - dpooled_g)
        qc_ref[...] = jnp.concatenate(q_head, axis=1)
        dproj_b = jnp.concatenate([du_a, db_a, dc_a, dg_a] + du_p + [dg_p], axis=1).astype(BF16)
        dproj_ref[...] = dproj_b

        x_t = x_ref[...]
        rx = lax.rsqrt(jnp.mean(x_t * x_t, axis=-1, keepdims=True) + NORM_EPS)
        xn = x_t * rx
        mod_scale = 1.0 + scale
        pl.when(i == 0)(load_win.wait)
        dh = _dot_nt(dproj_b, win_ref[...])
        dvec_ref[0:1, :] += jnp.sum(dh, axis=0, keepdims=True)
        dvec_ref[1:2, :] += jnp.sum(dh * xn, axis=0, keepdims=True)
        dxn = dh * (g_pre * mod_scale)
        dx_ref[...] = dxo_t + rx * (dxn - xn * jnp.mean(dxn * xn, axis=-1, keepdims=True))

        @pl.when(i == n_tiles - 1)
        def _():
            gwpool_ref[...] = gwpool_acc[...].astype(BF16)
            sum_dh_xn, sum_dxo_yh = dvec_ref[1:2, :], dvec_ref[2:3, :]
            dvec_ref[1:2, :] = sum_dh_xn * g_pre
            dvec_ref[3:4, :] = sum_dh_xn * mod_scale
            dvec_ref[2:3, :] = sum_dxo_yh * g_post
            dvec_ref[4:5, :] = sum_dxo_yh * gate

    return pl.pallas_call(
        body, name=f"backward_layer_{layer}", grid=(n_tiles,),
        in_specs=[rev(D_MODEL), rev(D_MODEL), rev(D_MODEL), rev(2 * CONV_W), halo_spec, rev(3 * CONV_W),
                  rev(4 * POOL_W), _layer_spec(vec.shape, layer), _layer_spec(cps.shape, layer),
                  _layer_spec(wpool.shape, layer), _whole_spec(wout.shape), HBM],
        out_specs=[rev(D_MODEL), rev(IN_COLS), rev(D_MODEL), _whole_spec(gwpool_shape),
                   _whole_spec((SUBLANES, CONV_W)), _whole_spec((SUBLANES, D_MODEL))],
        out_shape=[jax.ShapeDtypeStruct((t_len, D_MODEL), F32), jax.ShapeDtypeStruct((t_len, IN_COLS), BF16),
                   jax.ShapeDtypeStruct((t_len, D_MODEL), BF16), jax.ShapeDtypeStruct(gwpool_shape, BF16),
                   jax.ShapeDtypeStruct((SUBLANES, CONV_W), F32), jax.ShapeDtypeStruct((SUBLANES, D_MODEL), F32)],
        scratch_shapes=[pltpu.VMEM(gwpool_shape, F32), pltpu.VMEM((CONV_HALO, CONV_W), F32),
                        pltpu.VMEM((POOL_HALO, POOL_W), F32), pltpu.VMEM(win.shape, BF16), pltpu.SemaphoreType.DMA],
        compiler_params=_params(dimension_semantics=("arbitrary",)),
    )(dxo, y, x, uc, uc, fa, fp, vec, cps, wpool, wout, win)


def _weight_grads(layer, h_t, dproj, ycat_t, dy, exchange, after=()):
    t_len = dy.shape[0]
    n_in, n_out = IN_COLS // GWIN_COLS, D_MODEL // GWOUT_COLS
    args = [h_t, dproj, ycat_t, dy, *after]
    in_specs = [_whole_spec(h_t.shape),
                pl.BlockSpec((t_len, GWIN_COLS), lambda s: (0, jnp.minimum(s, n_in - 1))),
                HBM,
                pl.BlockSpec((t_len, GWOUT_COLS), lambda s: (0, jnp.maximum(s - n_in, 0)))] + [HBM] * len(after)
    out_shape = [jax.ShapeDtypeStruct((D_MODEL, IN_COLS), BF16), jax.ShapeDtypeStruct((D_MODEL, D_MODEL), BF16)]
    out_specs = [pl.BlockSpec((D_MODEL, GWIN_COLS), lambda s: (0, jnp.minimum(s, n_in - 1))),
                 pl.BlockSpec((D_MODEL, GWOUT_COLS), lambda s: (0, jnp.maximum(s - n_in, 0)))]
    scratch = [pltpu.VMEM(ycat_t.shape, BF16), pltpu.SemaphoreType.DMA]
    aliases, split = _host(exchange, args, in_specs, out_shape, out_specs, scratch)

    def body(*refs):
        (ht_ref, dproj_ref, ycatt_hbm, dy_ref, *_, gwin_ref, gwout_ref, ycatt_ref, ycatt_sem), hosted = split(refs)
        s = pl.program_id(0)
        load_ycatt = pltpu.make_async_copy(ycatt_hbm, ycatt_ref, ycatt_sem)
        pl.when(s == 0)(load_ycatt.start)
        if hosted is not None:
            pl.when(s == 0)(hosted[0])

        @pl.when(s < n_in)
        def _():
            gwin_ref[...] = _dot(ht_ref[...], dproj_ref[...]).astype(BF16)

        pl.when(s == n_in)(load_ycatt.wait)

        @pl.when(s >= n_in)
        def _():
            gwout_ref[...] = _dot(ycatt_ref[...], dy_ref[...]).astype(BF16)

        if hosted is not None:
            pl.when(s == n_in + n_out - 1)(hosted[1])

    return pl.pallas_call(
        body, name=f"weight_grads_{layer}", grid=(n_in + n_out,), in_specs=in_specs, out_specs=out_specs,
        out_shape=out_shape, scratch_shapes=scratch, input_output_aliases=aliases,
        compiler_params=_params(dimension_semantics=("arbitrary",)),
    )(*args)


def _add_sibling_blocks(name, layer, grads, received, core, partials, kinds):
    n_arr = len(grads)

    def body(core_ref, *refs):
        mine, theirs, outs = refs[:n_arr], refs[n_arr:2 * n_arr], refs[-n_arr:]
        for a in range(n_arr):
            outs[a][...] = (mine[a][...].astype(F32) + theirs[a][...].astype(F32)).astype(BF16)

    own_of_kind = [
        pl.BlockSpec((D_MODEL, W_IN_SHARD), lambda q, core_ref: (0, 2 * q + core_ref[0])),
        pl.BlockSpec((W_OUT_SHARD, D_MODEL), lambda q, core_ref: (2 * q + core_ref[0], 0)),
        pl.BlockSpec((POOL_SHARD, GROUP_D), lambda q, core_ref: (2 * q + core_ref[0], 0)),
    ]
    shapes = [_BLOCK_SHAPES[k] for k in kinds]
    recv_specs = [pl.BlockSpec((None,) + s, lambda q, core_ref: (q, 0, 0)) for s in shapes]
    out_specs = [pl.BlockSpec((None, None) + s, lambda q, core_ref: (q, layer, 0, 0)) for s in shapes]
    args = [core, *grads, *received]
    in_specs = [own_of_kind[k] for k in kinds] + recv_specs
    aliases = {}
    if partials is not None:
        aliases = {len(args) + a: a for a in range(n_arr)}
        args += list(partials)
        in_specs += [HBM] * n_arr
    return pl.pallas_call(
        body, name=name,
        grid_spec=pltpu.PrefetchScalarGridSpec(
            num_scalar_prefetch=1, grid=(N_CHIP,), in_specs=in_specs, out_specs=out_specs),
        out_shape=[jax.ShapeDtypeStruct((N_CHIP, DEPTH) + s, BF16) for s in shapes],
        input_output_aliases=aliases,
        compiler_params=_params(dimension_semantics=("arbitrary",)),
    )(*args)


def _modulation_columns(c_all, w_ada):
    def body(c_ref, w_ref, cact_ref, out_ref):
        c_t = c_ref[...]
        c_act = c_t * _sigmoid(c_t)
        cact_ref[...] = c_act
        out_ref[...] = jnp.dot(c_act, w_ref[...], preferred_element_type=F32, precision=lax.Precision.HIGHEST)

    return pl.pallas_call(
        body, name="modulation_columns", grid=(DEPTH,),
        in_specs=[pl.BlockSpec((N_DEV, D_MODEL), lambda l: (0, 0)),
                  pl.BlockSpec((None, D_MODEL, W_IN_SHARD), lambda l: (l, 0, 0))],
        out_specs=[pl.BlockSpec((N_DEV, D_MODEL), lambda l: (0, 0)),
                   pl.BlockSpec((N_DEV, W_IN_SHARD), lambda l: (0, l))],
        out_shape=[jax.ShapeDtypeStruct((N_DEV, D_MODEL), F32),
                   jax.ShapeDtypeStruct((N_DEV, DEPTH * W_IN_SHARD), F32)],
        compiler_params=_params(dimension_semantics=("arbitrary",)),
    )(c_all, w_ada)


def _adamw(w, g, m, v):
    m_new = ADAM_B1 * m + (1.0 - ADAM_B1) * g
    v_new = ADAM_B2 * v + (1.0 - ADAM_B2) * (g * g)
    m_hat = m_new / (1.0 - ADAM_B1 ** ADAM_STEP)
    v_hat = v_new / (1.0 - ADAM_B2 ** ADAM_STEP)
    delta = -ADAM_LR * (m_hat / (jnp.sqrt(v_hat) + ADAM_EPS) + ADAM_WD * w)
    return delta, m_new, v_new


def _adamw_w_ada(w, m, v, c_act_t, dmod_cols):
    rows = BWD_TILE

    def body(w_ref, m_ref, v_ref, ct_ref, dm_ref, g_ref, d_ref, mo_ref, vo_ref):
        g = ct_ref[:, 0:1] * dm_ref[0:1, :]
        for b in range(1, N_DEV):
            g = g + ct_ref[:, b:b + 1] * dm_ref[b:b + 1, :]
        g_ref[...] = g
        d_ref[...], mo_ref[...], vo_ref[...] = _adamw(w_ref[...], g, m_ref[...], v_ref[...])

    big = pl.BlockSpec((None, rows, W_IN_SHARD), lambda l, r: (l, r, 0))
    return pl.pallas_call(
        body, name="adamw_w_ada", grid=(DEPTH, D_MODEL // rows),
        in_specs=[big, big, big, pl.BlockSpec((rows, N_DEV), lambda l, r: (r, 0)),
                  pl.BlockSpec((None, N_DEV, W_IN_SHARD), lambda l, r: (l, 0, 0))],
        out_specs=[big] * 4, out_shape=[jax.ShapeDtypeStruct(w.shape, F32)] * 4,
        compiler_params=_params(dimension_semantics=("arbitrary", "arbitrary")),
    )(w, m, v, c_act_t, dmod_cols)


def _sum_chip_partials(own_ref, recv_ref):
    g = own_ref[...].astype(F32)
    for j in range(N_OTHER_CHIPS):
        g = g + recv_ref[j].astype(F32)
    return g


def _partial_specs(row_tile, cols, first_layer=0):
    own = pl.BlockSpec((None, None, row_tile, cols), lambda l, r, chip_ref: (chip_ref[0], first_layer + l, r, 0))
    recv = pl.BlockSpec((N_OTHER_CHIPS, None, row_tile, cols), lambda l, r, chip_ref: (0, first_layer + l, r, 0))
    return own, recv


def _adamw_reduced(name, w, m, v, partial, received, chip, row_tile, layers, continued):
    depth, rows, cols = w.shape
    first, stop = layers

    def body(chip_ref, w_ref, m_ref, v_ref, own_ref, recv_ref, *rest):
        g_ref, d_ref, mo_ref, vo_ref = rest[-4:]
        g = _sum_chip_partials(own_ref, recv_ref)
        g_ref[...] = g
        d_ref[...], mo_ref[...], vo_ref[...] = _adamw(w_ref[...], g, m_ref[...], v_ref[...])

    blk = pl.BlockSpec((None, row_tile, cols), lambda l, r, chip_ref: (first + l, r, 0))
    args = [chip, w, m, v, partial, received]
    in_specs = [blk, blk, blk, *_partial_specs(row_tile, cols, first)]
    aliases = {}
    if continued is not None:
        aliases = {len(args) + k: k for k in range(4)}
        args += list(continued)
        in_specs += [HBM] * 4
    return pl.pallas_call(
        body, name=name,
        grid_spec=pltpu.PrefetchScalarGridSpec(
            num_scalar_prefetch=1, grid=(stop - first, rows // row_tile), in_specs=in_specs, out_specs=[blk] * 4),
        out_shape=[jax.ShapeDtypeStruct(w.shape, F32)] * 4, input_output_aliases=aliases,
        compiler_params=_params(dimension_semantics=("arbitrary", "arbitrary")),
    )(*args)


def _reduce_w_pool(partial, received, chip):
    def body(chip_ref, own_ref, recv_ref, g_ref):
        g_ref[...] = _sum_chip_partials(own_ref, recv_ref)

    return pl.pallas_call(
        body, name="reduce_w_pool",
        grid_spec=pltpu.PrefetchScalarGridSpec(
            num_scalar_prefetch=1, grid=(DEPTH, 1), in_specs=list(_partial_specs(POOL_SHARD, GROUP_D)),
            out_specs=pl.BlockSpec((POOL_SHARD, GROUP_D), lambda l, r, chip_ref: (l, 0))),
        out_shape=jax.ShapeDtypeStruct((DEPTH * POOL_SHARD, GROUP_D), F32),
        compiler_params=_params(dimension_semantics=("arbitrary", "arbitrary")),
    )(chip, partial, received)


def _adamw_small(params):
    n = len(params)

    def body(*refs):
        ins, outs = refs[:4 * n], refs[4 * n:]
        for p in range(n):
            w_ref, g_ref, m_ref, v_ref = ins[4 * p:4 * p + 4]
            d_ref, mo_ref, vo_ref = outs[3 * p:3 * p + 3]
            d_ref[...], mo_ref[...], vo_ref[...] = _adamw(w_ref[...], g_ref[...], m_ref[...], v_ref[...])

    vmem = pl.BlockSpec(memory_space=pltpu.VMEM)
    flat = [a for group in params for a in group]
    out_shape = [jax.ShapeDtypeStruct(group[0].shape, F32) for group in params for _ in range(3)]
    outs = pl.pallas_call(
        body, name="adamw_small", in_specs=[vmem] * len(flat), out_specs=[vmem] * len(out_shape),
        out_shape=out_shape, compiler_params=_params(),
    )(*flat)
    return [tuple(outs[3 * p:3 * p + 3]) for p in range(n)]


def _sum_sources(slabs):
    def body(s_ref, o_ref):
        acc = s_ref[0]
        for b in range(1, N_DEV):
            acc = acc + s_ref[b]
        o_ref[...] = acc

    vmem = pl.BlockSpec(memory_space=pltpu.VMEM)
    return pl.pallas_call(
        body, name="sum_small_grads", in_specs=[vmem], out_specs=vmem,
        out_shape=jax.ShapeDtypeStruct(slabs.shape[1:], F32), compiler_params=_params(),
    )(slabs)


def _to_bf16(a, name, layers=None):
    first, stop = layers or (0, a.shape[0])

    def body(a_ref, o_ref):
        o_ref[...] = a_ref[...].astype(BF16)

    block = (None,) + a.shape[1:]
    return pl.pallas_call(
        body, name=name, grid=(stop - first,), in_specs=[pl.BlockSpec(block, lambda l: (first + l, 0, 0))],
        out_specs=pl.BlockSpec(block, lambda l: (l, 0, 0)),
        out_shape=jax.ShapeDtypeStruct((stop - first,) + a.shape[1:], BF16),
        compiler_params=_params(dimension_semantics=("arbitrary",)),
    )(a)


def kernel(x, c, w_ada, b_ada, g_pre, w_in, w_conv, w_pool, pool_scale, w_out, g_post, loss_target, m_w_ada, m_b_ada, m_g_pre, m_w_in, m_w_conv, m_w_pool, m_pool_scale, m_w_out, m_g_post, v_w_ada, v_b_ada, v_g_pre, v_w_in, v_w_conv, v_w_pool, v_pool_scale, v_w_out, v_g_post):
    mx, my, mc = _mesh_position()
    me = _block_id(mx, my, mc)
    chip = (2 * mx + my).astype(jnp.int32).reshape(1)
    core = mc.astype(jnp.int32).reshape(1)
    x0 = x[0]
    target = loss_target[0]
    conv_shard = w_conv.shape[-1]

    own_small = jnp.concatenate([c, w_conv.reshape(1, DEPTH * 3 * conv_shard)], axis=1)
    all_small = _all_gather_small(own_small, "all_gather_c_w_conv")[:, 0, :]
    gathers = [_gather_weights_start(
        0, _to_bf16(w_in, "cast_w_in_0", (0, 1)), _to_bf16(w_out, "cast_w_out_0", (0, 1)), [all_small])]
    c_all = all_small[:, :D_MODEL]
    w_conv_full = all_small[:, D_MODEL:].reshape(N_DEV, DEPTH, 3, conv_shard).transpose(1, 2, 0, 3).reshape(
        DEPTH, 3, CONV_W)
    cps = jnp.concatenate([w_conv_full, pool_scale[:, None], jnp.zeros((DEPTH, 4, CONV_W), F32)], axis=1)

    c_act, pieces = _modulation_columns(c_all, w_ada)
    upper = (1, DEPTH)
    upper_shards = [_to_bf16(w_in, "cast_w_in_1", upper), _to_bf16(w_out, "cast_w_out_1", upper)]
    wpool_b = _to_bf16(w_pool.reshape(DEPTH, POOL_ROWS, GROUP_D), "cast_w_pool").reshape(w_pool.shape)
    mod_all = _all_gather_small(
        pieces, "all_gather_modulation", [gathers[0][1][0], *upper_shards, wpool_b])
    mod_mine = lax.dynamic_index_in_dim(mod_all, me, axis=1, keepdims=False)
    mod = mod_mine.reshape(N_DEV, DEPTH, W_IN_SHARD).transpose(1, 0, 2).reshape(DEPTH, 3 * D_MODEL) + b_ada
    zeros_d = jnp.zeros((DEPTH, 3, D_MODEL), F32)
    vec = jnp.concatenate([mod.reshape(DEPTH, 3, D_MODEL), g_pre[:, None], g_post[:, None], zeros_d], axis=1)

    gathers.append(_gather_weights_start(1, *upper_shards, [mod_all]))
    gathers = [(first_sems, shards, landing[k], k) for first_sems, shards, landing in gathers
               for k in range(len(landing))]

    xs, kept, wins, wouts = [x0], [], [], []
    for l in range(DEPTH):
        first_sems, shards, zones, index = gathers[l]
        passed_sems, zones = _gather_weights_pass_on(
            l, index, first_sems[1], zones, [vec, cps, wpool_b, gathers[-1][1][0]] if l == 0 else [xs[-1]])
        win, wout = _gather_weights_finish(l, index, first_sems, passed_sems, shards, zones)
        x_next, *for_backward = _forward_layer(
            l, xs[-1], vec, cps, wpool_b, win, wout, target if l == DEPTH - 1 else None)
        xs.append(x_next)
        kept.append(for_backward[:6])
        wins.append(win)
        wouts.append(wout)
    dx, loss_tile = xs[DEPTH], for_backward[6]

    slab_rows = [None] * DEPTH
    partials = received = None
    in_flight = []

    def scatter(layer, grads, from_sibling, after):
        nonlocal partials, received
        partials = _add_sibling_blocks(
            f"grad_add_sibling_{layer}", layer, grads, from_sibling, core, partials, ALL_KINDS)
        chips = _chips_exchange(layer, partials, received, ALL_KINDS)
        sems, partials, received, token = _start_exchange(chips, f"grad_chips_start_{layer}", after)
        in_flight.append((chips, sems, layer))
        return token

    grads_above = None
    for l in reversed(range(DEPTH)):
        y, h_t, ycat_t, uc, fa, fp = kept[l]
        dx, dproj, dy, gwpool, dcps, dvec = _backward_layer(
            l, dx, y, xs[l], uc, fa, fp, vec, cps, wpool_b, wouts[l], wins[l])
        slab_rows[l] = jnp.concatenate(
            [dvec[0], dvec[1], dvec[2], dvec[3], dvec[4], dcps[3], dcps[0], dcps[1], dcps[2],
             loss_tile[0] if l == 0 else jnp.zeros((LANES,), F32)])
        after = []
        if l == 0:
            gather_small = _all_gather_exchange(jnp.stack(slab_rows))
            sems_s, slab, slabs, token = _start_exchange(gather_small, "all_gather_small_grads_start")
            after = [token]
        hosted = _sibling_exchange(grads_above, ALL_KINDS) if grads_above is not None else None
        gwin, gwout, *from_sibling = _weight_grads(l, h_t, dproj, ycat_t, dy, hosted, after)
        if l == 0:
            _, (slabs,) = _finish_exchange(
                gather_small, "all_gather_small_grads_finish", sems_s, slab, slabs, [gwin])
        if grads_above is not None:
            scatter(l + 1, grads_above, from_sibling, [])
        grads_above = [gwin, gwout, gwpool.reshape(POOL_ROWS, GROUP_D)]
        if l <= 1:
            from_sibling = _run_exchange(_sibling_exchange(grads_above, ALL_KINDS), f"grad_exchange_sibling_{l}")
            token = scatter(l, grads_above, from_sibling, [slabs] if l == 0 else [])
            grads_above = None
    grad_x = dx[None]
    chips_0, sems_0, _ = in_flight.pop()

    total = _sum_sources(slabs)
    loss = total[0, SLAB_COLS]
    o = 3 * D_MODEL
    g_b_ada = total[:, :o]
    g_g_pre = total[:, o:o + D_MODEL]
    g_g_post = total[:, o + D_MODEL:o + 2 * D_MODEL]
    g_pool_scale = total[:, o + 2 * D_MODEL:o + 2 * D_MODEL + POOL_W]
    g_conv_full = total[:, o + 2 * D_MODEL + POOL_W:SLAB_COLS].reshape(DEPTH, 3, CONV_W)
    g_w_conv = lax.dynamic_slice_in_dim(g_conv_full, me * conv_shard, conv_shard, axis=2)

    after = [token]
    for chips, sems, l in in_flight:
        partials, received = _finish_exchange(chips, f"grad_chips_finish_{l}", sems, partials, received, after)
        after = []
    upper = (1, DEPTH)
    w_in_upper = _adamw_reduced(
        "adamw_w_in_upper", w_in, m_w_in, v_w_in, partials[0], received[0], chip, ROW_TILE, upper, None)
    w_out_upper = _adamw_reduced(
        "adamw_w_out_upper", w_out, m_w_out, v_w_out, partials[1], received[1], chip, W_OUT_SHARD, upper, None)
    dmod_all = slabs[:, :, :o].reshape(N_DEV, DEPTH, N_DEV, W_IN_SHARD)
    dmod_cols = lax.dynamic_index_in_dim(dmod_all, me, axis=2, keepdims=False).transpose(1, 0, 2) + token[0, 0]
    g_w_ada, d_w_ada, nm_w_ada, nv_w_ada = _adamw_w_ada(w_ada, m_w_ada, v_w_ada, c_act.T, dmod_cols)

    partials, received = _finish_exchange(
        chips_0, "grad_chips_finish_0", sems_0, partials, received, [nv_w_ada, w_in_upper[3], w_out_upper[3]])
    gather_pool = _all_gather_exchange(_reduce_w_pool(partials[2], received[2], chip))
    sems_p, pool_rows, pool_landing, token_p = _start_exchange(gather_pool, "all_gather_grad_w_pool_start")
    g_w_in, d_w_in, nm_w_in, nv_w_in = _adamw_reduced(
        "adamw_w_in_0", w_in, m_w_in, v_w_in, partials[0], received[0], chip, ROW_TILE, (0, 1), w_in_upper)
    g_w_out, d_w_out, nm_w_out, nv_w_out = _adamw_reduced(
        "adamw_w_out_0", w_out, m_w_out, v_w_out, partials[1], received[1], chip, W_OUT_SHARD, (0, 1), w_out_upper)
    _, (g_pool_all,) = _finish_exchange(
        gather_pool, "all_gather_grad_w_pool_finish", sems_p, pool_rows, pool_landing, [nv_w_in, nv_w_out])
    g_w_pool = g_pool_all.reshape(N_DEV, DEPTH, POOL_SHARD, GROUP_D).transpose(1, 0, 2, 3).reshape(w_pool.shape)

    flat2 = lambda a: a.reshape(-1, a.shape[-1])
    small = _adamw_small([
        (b_ada, g_b_ada, m_b_ada, v_b_ada),
        (g_pre, g_g_pre, m_g_pre, v_g_pre),
        (flat2(w_conv), flat2(g_w_conv), flat2(m_w_conv), flat2(v_w_conv)),
        (flat2(w_pool), flat2(g_w_pool), flat2(m_w_pool), flat2(v_w_pool)),
        (pool_scale, g_pool_scale, m_pool_scale, v_pool_scale),
        (g_post, g_g_post, m_g_post, v_g_post),
    ])
    (d_b_ada, nm_b_ada, nv_b_ada), (d_g_pre, nm_g_pre, nv_g_pre), conv_upd, pool_upd, \
        (d_ps, nm_ps, nv_ps), (d_g_post, nm_g_post, nv_g_post) = small
    d_w_conv, nm_w_conv, nv_w_conv = (a.reshape(w_conv.shape) for a in conv_upd)
    d_w_pool, nm_w_pool, nv_w_pool = (a.reshape(w_pool.shape) for a in pool_upd)

    return (loss, grad_x,
            g_w_ada, g_b_ada, g_g_pre, g_w_in, g_w_conv, g_w_pool, g_pool_scale, g_w_out, g_g_post,
            d_w_ada, d_b_ada, d_g_pre, d_w_in, d_w_conv, d_w_pool, d_ps, d_w_out, d_g_post,
            nm_w_ada, nm_b_ada, nm_g_pre, nm_w_in, nm_w_conv, nm_w_pool, nm_ps, nm_w_out, nm_g_post,
            nv_w_ada, nv_b_ada, nv_g_pre, nv_w_in, nv_w_conv, nv_w_pool, nv_ps, nv_w_out, nv_g_post)
```

```python
import jax
import jax.numpy as jnp
from jax import lax
from jax.experimental import pallas as pl
from jax.experimental.pallas import tpu as pltpu

F32 = jnp.float32
BF16 = jnp.bfloat16

D_MODEL = 1024
DEPTH = 4
CONV_W = 512
POOL_W = 512
POOL_WINDOWS = (2, 4, 8, 16)
GROUP_D = 128
IN_COLS = 4 * CONV_W + 2 * POOL_W
NORM_EPS = 1e-6

ADAM_LR = 0.001
ADAM_B1 = 0.9
ADAM_B2 = 0.999
ADAM_EPS = 1e-08
ADAM_WD = 0.01
ADAM_STEP = 10

N_DEV = 8
N_CHIP = 4
N_OTHER_CHIPS = N_CHIP - 1
MESH = pl.DeviceIdType.MESH
W_IN_SHARD = IN_COLS // N_DEV
W_OUT_SHARD = D_MODEL // N_DEV
POOL_ROWS = len(POOL_WINDOWS) * GROUP_D
POOL_SHARD = POOL_ROWS // N_DEV

SUBLANES = 8
LANES = 128
VMEM_LIMIT_BYTES = 56 * 1024 * 1024
ROW_TILE = 512
BWD_TILE = 256
GWIN_COLS = 1024
GWOUT_COLS = 1024
POOL_HALO = 16
CONV_HALO = SUBLANES

SLAB_COLS = 3 * D_MODEL + D_MODEL + D_MODEL + POOL_W + 3 * CONV_W

HBM = pl.BlockSpec(memory_space=pl.ANY)


def _params(**kw):
    return pltpu.CompilerParams(vmem_limit_bytes=VMEM_LIMIT_BYTES, **kw)


def _sigmoid(v):
    return 1.0 / (1.0 + jnp.exp(-v))


def _dot(a, b):
    return jnp.dot(a, b, preferred_element_type=F32)


def _dot_tn(a, b):
    return lax.dot_general(a, b, (((0,), (0,)), ((), ())), preferred_element_type=F32)


def _dot_nt(a, b):
    return lax.dot_general(a, b, (((1,), (1,)), ((), ())), preferred_element_type=F32)


def _rows_from_before(v, k):
    return pltpu.roll(v, k, 0)


def _rows_from_after(v, k):
    return pltpu.roll(v, v.shape[0] - k, 0)


def _window_counts(t0, rows):
    return (lax.broadcasted_iota(jnp.int32, (rows, 1), 0) + (t0 + 1)).astype(F32)


def _split_proj(p32):
    cw = CONV_W
    return (p32[:, 0 * cw:1 * cw], p32[:, 1 * cw:2 * cw], p32[:, 2 * cw:3 * cw], p32[:, 3 * cw:4 * cw],
            p32[:, 4 * cw:4 * cw + POOL_W], p32[:, 4 * cw + POOL_W:])


def _layer_spec(shape, layer):
    nd = len(shape)
    return pl.BlockSpec((None,) + tuple(shape[1:]), lambda i, _l=layer, _n=nd: (_l,) + (0,) * (_n - 1))


def _whole_spec(shape):
    return pl.BlockSpec(tuple(shape), lambda i, _n=len(shape): (0,) * _n, pipeline_mode=pl.Buffered(1))


def _mesh_position():
    return lax.axis_index("x"), lax.axis_index("y"), lax.axis_index("c")


def _block_id(x, y, c):
    return 4 * x + 2 * y + c


def _other_chips(x, y):
    return [(x ^ 1, y), (x, y ^ 1), (x ^ 1, y ^ 1)]


def _col_block(ref, blk):
    return ref.at[:, pl.ds(pl.multiple_of(blk * W_IN_SHARD, LANES), W_IN_SHARD)]


def _row_block(rows):
    def block(ref, blk):
        return ref.at[pl.ds(pl.multiple_of(blk * rows, rows), rows), :]
    return block


_BLOCK_OF = (_col_block, _row_block(W_OUT_SHARD), _row_block(POOL_SHARD))
_BLOCK_SHAPES = ((D_MODEL, W_IN_SHARD), (W_OUT_SHARD, D_MODEL), (POOL_SHARD, GROUP_D))


class _Exchange:
    def __init__(self, inputs, out_shapes, aliases, sem_shapes, make):
        self.inputs, self.out_shapes, self.aliases, self.sem_shapes, self.make = (
            list(inputs), list(out_shapes), dict(aliases), list(sem_shapes), make)


def _run_exchange(exchange, name):
    n_in, n_out = len(exchange.inputs), len(exchange.out_shapes)

    def body(*refs):
        start, finish = exchange.make(refs[:n_in], refs[n_in:n_in + n_out], refs[n_in + n_out:])
        start()
        finish()

    return pl.pallas_call(
        body, name=name, in_specs=[HBM] * n_in, out_specs=[HBM] * n_out, out_shape=exchange.out_shapes,
        scratch_shapes=exchange.sem_shapes, input_output_aliases=exchange.aliases, compiler_params=_params(),
    )(*exchange.inputs)


_SEM = pl.BlockSpec(memory_space=pltpu.SEMAPHORE)
_DATAFLOW = pltpu.SideEffectType.DATAFLOW_SIDE_EFFECTING


def _start_exchange(exchange, name, after=()):
    n_in, n_out, n_sem = len(exchange.inputs), len(exchange.out_shapes), len(exchange.sem_shapes)
    sources = [i for i in range(n_in) if i not in exchange.aliases]
    aliases = {i: n_sem + k for k, i in enumerate(sources)}
    aliases.update({i: n_sem + len(sources) + o for i, o in exchange.aliases.items()})

    def body(*refs):
        in_refs = refs[:n_in]
        outs = refs[n_in + len(after):]
        sems = outs[:n_sem]
        out_refs = outs[n_sem + len(sources):n_sem + len(sources) + n_out]
        exchange.make(in_refs, out_refs, sems)[0]()
        refs[-1][...] = jnp.zeros_like(refs[-1])

    outs = pl.pallas_call(
        body, name=name, in_specs=[HBM] * (n_in + len(after)),
        out_specs=[_SEM] * n_sem + [HBM] * (len(sources) + n_out) + [pl.BlockSpec(memory_space=pltpu.VMEM)],
        out_shape=(exchange.sem_shapes + [pltpu.HBM(exchange.inputs[i].shape, exchange.inputs[i].dtype) for i in sources]
                   + [pltpu.HBM(s.shape, s.dtype) for s in exchange.out_shapes]
                   + [jax.ShapeDtypeStruct((SUBLANES, LANES), F32)]),
        input_output_aliases=aliases, compiler_params=_params(has_side_effects=_DATAFLOW),
    )(*exchange.inputs, *after)
    return outs[:n_sem], outs[n_sem:n_sem + len(sources)], outs[n_sem + len(sources):-1], outs[-1]


def _finish_exchange(exchange, name, sems, sources, landing, after):
    n_src, n_out, n_sem = len(sources), len(landing), len(sems)
    n_in = len(exchange.inputs)
    source_at = [i for i in range(n_in) if i not in exchange.aliases]

    def body(*refs):
        src_refs, out_refs = refs[:n_src], refs[n_src:n_src + n_out]
        sem_refs = refs[n_src + n_out:n_src + n_out + n_sem]
        in_refs = [None] * n_in
        for k, i in enumerate(source_at):
            in_refs[i] = src_refs[k]
        for i, o in exchange.aliases.items():
            in_refs[i] = out_refs[o]
        exchange.make(in_refs, out_refs, sem_refs)[1]()

    arrays = list(sources) + list(landing)
    outs = pl.pallas_call(
        body, name=name, in_specs=[HBM] * len(arrays) + [_SEM] * n_sem + [HBM] * len(after),
        out_specs=[HBM] * len(arrays), out_shape=[pltpu.HBM(a.shape, a.dtype) for a in arrays],
        input_output_aliases={i: i for i in range(len(arrays))}, compiler_params=_params(has_side_effects=_DATAFLOW),
    )(*arrays, *sems, *after)
    return outs[:n_src], outs[n_src:]


N_GATHERED = 2
FIRST_COPIES = 1 + N_OTHER_CHIPS
_GATHERED_SHAPES = ((D_MODEL, IN_COLS), (D_MODEL, D_MODEL))


def _first_sem(layer, a, k):
    return (layer * N_GATHERED + a) * FIRST_COPIES + k


def _gather_copy(window_of, full_ref, blk, send_sem, recv_sem, to, src=None):
    window = window_of(full_ref, blk)
    return pltpu.make_async_remote_copy(
        src_ref=window if src is None else src, dst_ref=window, send_sem=send_sem, recv_sem=recv_sem,
        device_id=to, device_id_type=MESH)


def _first_copies(layer, shard_refs, full_refs, send_sems, recv_sems, local_sems):
    x, y, c = _mesh_position()
    me = _block_id(x, y, c)
    own, remote = [], []
    for a in range(N_GATHERED):
        shard = shard_refs[a].at[layer]
        own.append(pltpu.make_async_copy(
            shard, _BLOCK_OF[a](full_refs[a], me), local_sems.at[layer * N_GATHERED + a]))
        targets = [(x, y, 1 - c)] + [(*chip, c) for chip in _other_chips(x, y)]
        remote += [_gather_copy(_BLOCK_OF[a], full_refs[a], me, send_sems.at[_first_sem(layer, a, k)],
                                recv_sems.at[_first_sem(layer, a, k)], to, src=shard)
                   for k, to in enumerate(targets)]
    return own, remote


def _gather_weights_start(first_layer, win_shards, wout_shards, after):
    n_layers = win_shards.shape[0]
    n_first = n_layers * N_GATHERED * FIRST_COPIES
    sem_shapes = [pltpu.SemaphoreType.DMA((n_first,)), pltpu.SemaphoreType.DMA((n_first,)),
                  pltpu.SemaphoreType.DMA((n_layers * N_GATHERED,))]
    shards = [win_shards, wout_shards]

    def body(win_sh, wout_sh, *rest):
        send_sems, recv_sems, local_sems, win_thru, wout_thru, *landing = rest[len(after):]
        for layer in range(n_layers):
            own, remote = _first_copies(layer, (win_sh, wout_sh), landing[N_GATHERED * layer:N_GATHERED * (layer + 1)],
                                        send_sems, recv_sems, local_sems)
            for cp in own + remote:
                cp.start()

    outs = pl.pallas_call(
        body, name=f"all_gather_weights_start_{first_layer}", in_specs=[HBM] * (2 + len(after)),
        out_specs=[_SEM] * 3 + [HBM] * (2 + n_layers * N_GATHERED),
        out_shape=(sem_shapes + [pltpu.HBM(s.shape, s.dtype) for s in shards]
                   + [pltpu.HBM(s, BF16) for _ in range(n_layers) for s in _GATHERED_SHAPES]),
        input_output_aliases={0: 3, 1: 4}, compiler_params=_params(has_side_effects=_DATAFLOW),
    )(*shards, *after)
    landing = outs[5:]
    return outs[:3], outs[3:5], [landing[N_GATHERED * l:N_GATHERED * (l + 1)] for l in range(n_layers)]


def _passed_on_copies(full_refs, send_sems, recv_sems, core_of_block):
    x, y, c = _mesh_position()
    return [_gather_copy(_BLOCK_OF[a], full_refs[a], _block_id(*chip, core_of_block),
                         send_sems.at[a * N_OTHER_CHIPS + j], recv_sems.at[a * N_OTHER_CHIPS + j], (x, y, 1 - c))
            for a in range(N_GATHERED) for j, chip in enumerate(_other_chips(x, y))]


def _gather_weights_pass_on(layer, index, first_recv_sems, landing, after):
    n = N_GATHERED * N_OTHER_CHIPS

    def body(win_ref, wout_ref, first_recv, *rest):
        send_sems, recv_sems = rest[len(after):len(after) + 2]
        x, y, c = _mesh_position()
        full_refs = (win_ref, wout_ref)
        passed = _passed_on_copies(full_refs, send_sems, recv_sems, c)
        for a in range(N_GATHERED):
            for j, chip in enumerate(_other_chips(x, y)):
                sem = _first_sem(index, a, 1 + j)
                _gather_copy(_BLOCK_OF[a], full_refs[a], _block_id(*chip, c), first_recv.at[sem], first_recv.at[sem],
                             (x, y, c)).wait_recv()
                passed[a * N_OTHER_CHIPS + j].start()

    outs = pl.pallas_call(
        body, name=f"all_gather_weights_pass_on_{layer}", in_specs=[HBM] * N_GATHERED + [_SEM] + [HBM] * len(after),
        out_specs=[_SEM] * 2 + [HBM] * N_GATHERED,
        out_shape=[pltpu.SemaphoreType.DMA((n,)), pltpu.SemaphoreType.DMA((n,))]
        + [pltpu.HBM(a.shape, a.dtype) for a in landing],
        input_output_aliases={a: 2 + a for a in range(N_GATHERED)},
        compiler_params=_params(has_side_effects=_DATAFLOW),
    )(*landing, first_recv_sems, *after)
    return outs[:2], outs[2:]


def _gather_weights_finish(layer, index, first_sems, passed_sems, shards, landing):
    def body(win_ref, wout_ref, first_send, first_recv, local_sems, passed_send, passed_recv, win_sh, wout_sh, *thru):
        x, y, c = _mesh_position()
        full_refs = (win_ref, wout_ref)
        own, remote = _first_copies(index, (win_sh, wout_sh), full_refs, first_send, first_recv, local_sems)
        for a in range(N_GATHERED):
            sem = _first_sem(index, a, 0)
            _gather_copy(_BLOCK_OF[a], full_refs[a], _block_id(x, y, 1 - c), first_recv.at[sem], first_recv.at[sem],
                         (x, y, c)).wait_recv()
        for cp in _passed_on_copies(full_refs, passed_send, passed_recv, 1 - c):
            cp.wait_recv()
        for cp in remote + _passed_on_copies(full_refs, passed_send, passed_recv, c):
            cp.wait_send()
        for cp in own:
            cp.wait()

    return pl.pallas_call(
        body, name=f"all_gather_weights_finish_{layer}", in_specs=[HBM] * N_GATHERED + [_SEM] * 5 + [HBM] * 2,
        out_specs=[HBM] * N_GATHERED, out_shape=[pltpu.HBM(a.shape, a.dtype) for a in landing],
        input_output_aliases={a: a for a in range(N_GATHERED)}, compiler_params=_params(has_side_effects=_DATAFLOW),
    )(*landing, *first_sems, *passed_sems, *shards)


ALL_KINDS = (0, 1, 2)


def _sibling_exchange(grads, kinds):
    n_arr = len(grads)

    def make(in_refs, out_refs, sems):
        send_sems, recv_sems = sems
        x, y, c = _mesh_position()
        copies = [pltpu.make_async_remote_copy(
            src_ref=_BLOCK_OF[kinds[a]](in_refs[a], 2 * q + (1 - c)), dst_ref=out_refs[a].at[q],
            send_sem=send_sems.at[a, q], recv_sem=recv_sems.at[a, q], device_id=(x, y, 1 - c), device_id_type=MESH)
            for a in range(n_arr) for q in range(N_CHIP)]

        def start():
            for cp in copies:
                cp.start()

        def finish():
            for cp in copies:
                cp.wait_recv()
            for cp in copies:
                cp.wait_send()

        return start, finish

    return _Exchange(
        grads, [jax.ShapeDtypeStruct((N_CHIP,) + _BLOCK_SHAPES[k], BF16) for k in kinds], {},
        [pltpu.SemaphoreType.DMA((n_arr, N_CHIP)), pltpu.SemaphoreType.DMA((n_arr, N_CHIP))], make)


def _chips_exchange(layer, partials, received, kinds):
    n_arr = len(partials)

    def make(in_refs, out_refs, sems):
        send_sems, recv_sems = sems
        x, y, c = _mesh_position()
        copies = [pltpu.make_async_remote_copy(
            src_ref=in_refs[a].at[2 * qx + qy, layer], dst_ref=out_refs[a].at[j, layer],
            send_sem=send_sems.at[a * N_OTHER_CHIPS + j], recv_sem=recv_sems.at[a * N_OTHER_CHIPS + j],
            device_id=(qx, qy, c), device_id_type=MESH)
            for a in range(n_arr) for j, (qx, qy) in enumerate(_other_chips(x, y))]

        def start():
            for cp in copies:
                cp.start()

        def finish():
            for cp in copies:
                cp.wait_recv()
            for cp in copies:
                cp.wait_send()

        return start, finish

    inputs = list(partials)
    aliases = {}
    if received is not None:
        inputs += list(received)
        aliases = {n_arr + a: a for a in range(n_arr)}
    return _Exchange(
        inputs, [jax.ShapeDtypeStruct((N_OTHER_CHIPS, DEPTH) + _BLOCK_SHAPES[k], BF16) for k in kinds], aliases,
        [pltpu.SemaphoreType.DMA((n_arr * N_OTHER_CHIPS,)), pltpu.SemaphoreType.DMA((n_arr * N_OTHER_CHIPS,))], make)


def _all_gather_exchange(v):
    def make(in_refs, out_refs, sems):
        send_sems, recv_sems, local_sem = sems
        x, y, c = _mesh_position()
        me = _block_id(x, y, c)
        own = pltpu.make_async_copy(in_refs[0], out_refs[0].at[me], local_sem.at[0])
        sends, arrivals = [], []
        for k in range(1, N_DEV):
            px, py, pc = x ^ ((k >> 2) & 1), y ^ ((k >> 1) & 1), c ^ (k & 1)
            sends.append(pltpu.make_async_remote_copy(
                src_ref=in_refs[0], dst_ref=out_refs[0].at[me], send_sem=send_sems.at[k - 1],
                recv_sem=recv_sems.at[k - 1], device_id=(px, py, pc), device_id_type=MESH))
            arrivals.append(pltpu.make_async_remote_copy(
                src_ref=in_refs[0], dst_ref=out_refs[0].at[_block_id(px, py, pc)], send_sem=send_sems.at[k - 1],
                recv_sem=recv_sems.at[k - 1], device_id=(x, y, c), device_id_type=MESH))

        def start():
            for cp in [own] + sends:
                cp.start()

        def finish():
            for cp in arrivals:
                cp.wait_recv()
            for cp in sends:
                cp.wait_send()
            own.wait()

        return start, finish

    return _Exchange(
        [v], [jax.ShapeDtypeStruct((N_DEV,) + v.shape, v.dtype)], {},
        [pltpu.SemaphoreType.DMA((N_DEV - 1,)), pltpu.SemaphoreType.DMA((N_DEV - 1,)),
         pltpu.SemaphoreType.DMA((1,))], make)


def _host(exchange, args, in_specs, out_shape, out_specs, scratch):
    n_own = (len(args), len(out_shape), len(scratch))
    if exchange is None:
        return {}, lambda refs: (refs, None)
    n_ex = (len(exchange.inputs), len(exchange.out_shapes), len(exchange.sem_shapes))
    aliases = {n_own[0] + i: n_own[1] + o for i, o in exchange.aliases.items()}
    args += exchange.inputs
    in_specs += [HBM] * n_ex[0]
    out_shape += exchange.out_shapes
    out_specs += [HBM] * n_ex[1]
    scratch += exchange.sem_shapes

    def split(refs):
        own, theirs, at = [], [], 0
        for mine, ex in zip(n_own, n_ex):
            own += refs[at:at + mine]
            theirs.append(refs[at + mine:at + mine + ex])
            at += mine + ex
        return own, exchange.make(*theirs)

    return aliases, split


def _all_gather_small(v, name, after=()):
    vmem = pl.BlockSpec(memory_space=pltpu.VMEM)

    def body(v_ref, *rest):
        out_ref, send_sems, recv_sems = rest[len(after):]
        x, y, c = _mesh_position()
        me = _block_id(x, y, c)
        out_ref[me] = v_ref[...]
        sends = []
        for k in range(1, N_DEV):
            px, py, pc = x ^ ((k >> 2) & 1), y ^ ((k >> 1) & 1), c ^ (k & 1)
            send = pltpu.make_async_remote_copy(
                src_ref=v_ref, dst_ref=out_ref.at[me], send_sem=send_sems.at[k - 1], recv_sem=recv_sems.at[k - 1],
                device_id=(px, py, pc), device_id_type=MESH)
            send.start()
            sends.append((send, _block_id(px, py, pc)))
        for k, (send, peer) in enumerate(sends):
            pltpu.make_async_remote_copy(
                src_ref=v_ref, dst_ref=out_ref.at[peer], send_sem=send_sems.at[k], recv_sem=recv_sems.at[k],
                device_id=(x, y, c), device_id_type=MESH).wait_recv()
        for send, _ in sends:
            send.wait_send()

    return pl.pallas_call(
        body, name=name, in_specs=[vmem] + [HBM] * len(after), out_specs=vmem,
        out_shape=jax.ShapeDtypeStruct((N_DEV,) + v.shape, v.dtype),
        scratch_shapes=[pltpu.SemaphoreType.DMA((N_DEV - 1,)), pltpu.SemaphoreType.DMA((N_DEV - 1,))],
        compiler_params=_params(),
    )(v, *after)


def _forward_layer(layer, x, vec, cps, wpool, win, wout, target=None):
    t_len = x.shape[0]
    n_tiles = t_len // ROW_TILE
    row = lambda cols: pl.BlockSpec((ROW_TILE, cols), lambda i: (i, 0))
    col_t = pl.BlockSpec((D_MODEL, ROW_TILE), lambda i: (0, i))
    widths = (D_MODEL, 2 * CONV_W, 3 * CONV_W, 4 * POOL_W)
    head = target is not None

    def body(x_ref, vec_ref, cps_ref, wpool_ref, win_ref, wout_ref, *rest):
        target_ref = rest[0] if head else None
        xo_ref, y_ref, ht_ref, ycatt_ref, uc_ref, fa_ref, fp_ref = rest[head:head + 7]
        loss_ref = rest[head + 7] if head else None
        zc_ref, pc_ref = rest[-2:]
        i = pl.program_id(0)

        @pl.when(i == 0)
        def _():
            zc_ref[...] = jnp.zeros_like(zc_ref)
            pc_ref[...] = jnp.zeros_like(pc_ref)
            if head:
                loss_ref[...] = jnp.zeros_like(loss_ref)

        x_t = x_ref[...]
        shift, scale, gate = vec_ref[0:1, :], vec_ref[1:2, :], vec_ref[2:3, :]
        g_pre, g_post = vec_ref[3:4, :], vec_ref[4:5, :]
        w0, w1, w2, ps = cps_ref[0:1, :], cps_ref[1:2, :], cps_ref[2:3, :], cps_ref[3:4, :]
        rx = lax.rsqrt(jnp.mean(x_t * x_t, axis=-1, keepdims=True) + NORM_EPS)
        h = (x_t * rx) * g_pre * (1.0 + scale) + shift
        ht_ref[...] = h.T.astype(BF16)
        proj = _dot(h.astype(BF16), win_ref[...]).astype(BF16).astype(F32)
        u_a, b_a, c_a, g_a, u_p, g_p = _split_proj(proj)
        uc_ref[...] = jnp.concatenate([u_a, c_a], axis=1).astype(BF16)

        z = c_a * u_a
        zcat = jnp.concatenate([zc_ref[...], z], axis=0)
        zc_ref[...] = z[ROW_TILE - CONV_HALO:]
        conv = (w0 * _rows_from_before(zcat, 2)[CONV_HALO:] + w1 * _rows_from_before(zcat, 1)[CONV_HALO:] + w2 * z)
        sig_a = _sigmoid(g_a)
        silu_a = g_a * sig_a
        b_conv = b_a * conv
        y_a = b_conv * silu_a
        fa_ref[...] = jnp.concatenate(
            [silu_a * conv, silu_a * b_a, b_conv * (sig_a + silu_a * (1.0 - sig_a))], axis=1).astype(BF16)

        pcat = jnp.concatenate([pc_ref[...], u_p], axis=0)
        pc_ref[...] = u_p[ROW_TILE - POOL_HALO:]
        counts = _window_counts(i * ROW_TILE, ROW_TILE)
        pooled, mixed = [], []
        for g, w in enumerate(POOL_WINDOWS):
            cols = slice(g * GROUP_D, (g + 1) * GROUP_D)
            s = pcat[:, cols]
            step = 1
            while step < w:
                s = s + _rows_from_before(s, step)
                step *= 2
            pooled_g = (s[POOL_HALO:] * (1.0 / jnp.minimum(counts, float(w))) - u_p[:, cols]).astype(BF16)
            pooled.append(pooled_g)
            mixed.append(_dot(pooled_g, wpool_ref[g]))
        mixed = jnp.concatenate(mixed, axis=1)
        sig_p = _sigmoid(g_p)
        silu_p = g_p * sig_p
        mixed_ps = mixed * ps
        y_p = mixed_ps * silu_p
        fp_ref[...] = jnp.concatenate(
            [(ps * silu_p).astype(BF16), (mixed_ps * (sig_p + silu_p * (1.0 - sig_p))).astype(BF16),
             (silu_p * mixed).astype(BF16)] + pooled, axis=1)

        ycat = jnp.concatenate([y_a, y_p], axis=1)
        ycatt_ref[...] = ycat.T.astype(BF16)
        y_b = _dot(ycat.astype(BF16), wout_ref[...]).astype(BF16)
        y_ref[...] = y_b
        y_t = y_b.astype(F32)
        ry = lax.rsqrt(jnp.mean(y_t * y_t, axis=-1, keepdims=True) + NORM_EPS)
        x_next = x_t + gate * (y_t * ry * g_post)
        if head:
            err = x_next - target_ref[...]
            xo_ref[...] = err * (1.0 / D_MODEL)
            loss_ref[...] += jnp.sum(err * err) * (0.5 / D_MODEL)
        else:
            xo_ref[...] = x_next

    tile = (SUBLANES, LANES)
    return pl.pallas_call(
        body, name=f"forward_layer_{layer}", grid=(n_tiles,),
        in_specs=[row(D_MODEL), _layer_spec(vec.shape, layer), _layer_spec(cps.shape, layer),
                  _layer_spec(wpool.shape, layer), _whole_spec(win.shape), _whole_spec(wout.shape)]
        + [row(D_MODEL)] * head,
        out_specs=[row(D_MODEL), row(D_MODEL), col_t, col_t] + [row(w) for w in widths[1:]] + [_whole_spec(tile)] * head,
        out_shape=[jax.ShapeDtypeStruct((t_len, D_MODEL), F32), jax.ShapeDtypeStruct((t_len, D_MODEL), BF16),
                   jax.ShapeDtypeStruct((D_MODEL, t_len), BF16), jax.ShapeDtypeStruct((D_MODEL, t_len), BF16)]
        + [jax.ShapeDtypeStruct((t_len, w), BF16) for w in widths[1:]] + [jax.ShapeDtypeStruct(tile, F32)] * head,
        scratch_shapes=[pltpu.VMEM((CONV_HALO, CONV_W), F32), pltpu.VMEM((POOL_HALO, POOL_W), F32)],
        compiler_params=_params(dimension_semantics=("arbitrary",)),
    )(x, vec, cps, wpool, win, wout, *([target] * head))


def _backward_layer(layer, dxo, y, x, uc, fa, fp, vec, cps, wpool, wout, win):
    t_len = dxo.shape[0]
    n_tiles = t_len // BWD_TILE
    halo_per_tile = BWD_TILE // POOL_HALO
    rev = lambda cols: pl.BlockSpec((BWD_TILE, cols), lambda i: (n_tiles - 1 - i, 0))
    halo_spec = pl.BlockSpec(
        (POOL_HALO, 2 * CONV_W), lambda i: (jnp.maximum((n_tiles - 1 - i) * halo_per_tile - 1, 0), 0))
    gwpool_shape = (len(POOL_WINDOWS), GROUP_D, GROUP_D)

    def body(dxo_ref, y_ref, x_ref, uc_ref, uch_ref, fa_ref, fp_ref, vec_ref, cps_ref, wpool_ref, wout_ref, win_ref,
             dx_ref, dproj_ref, dy_ref, gwpool_ref, dcps_ref, dvec_ref, gwpool_acc, dcc_ref, qc_ref):
        i = pl.program_id(0)
        tile = n_tiles - 1 - i

        @pl.when(i == 0)
        def _():
            gwpool_acc[...] = jnp.zeros_like(gwpool_acc)
            dcps_ref[...] = jnp.zeros_like(dcps_ref)
            dvec_ref[...] = jnp.zeros_like(dvec_ref)
            dcc_ref[...] = jnp.zeros_like(dcc_ref)
            qc_ref[...] = jnp.zeros_like(qc_ref)

        shift, scale, gate = vec_ref[0:1, :], vec_ref[1:2, :], vec_ref[2:3, :]
        g_pre, g_post = vec_ref[3:4, :], vec_ref[4:5, :]
        w0, w1, w2 = cps_ref[0:1, :], cps_ref[1:2, :], cps_ref[2:3, :]

        dxo_t = dxo_ref[...]
        y_t = y_ref[...].astype(F32)
        ry = lax.rsqrt(jnp.mean(y_t * y_t, axis=-1, keepdims=True) + NORM_EPS)
        yh = y_t * ry
        dvec_ref[2:3, :] += jnp.sum(dxo_t * yh, axis=0, keepdims=True)
        dyh = dxo_t * (gate * g_post)
        dy_b = (ry * (dyh - yh * jnp.mean(dyh * yh, axis=-1, keepdims=True))).astype(BF16)
        dy_ref[...] = dy_b
        dycat = _dot_nt(dy_b, wout_ref[...])
        dy_a, dy_p = dycat[:, :CONV_W], dycat[:, CONV_W:]

        fa_t = fa_ref[...].astype(F32)
        db_a = dy_a * fa_t[:, :CONV_W]
        dconv = dy_a * fa_t[:, CONV_W:2 * CONV_W]
        dg_a = dy_a * fa_t[:, 2 * CONV_W:]
        uc_t = uc_ref[...].astype(F32)
        u_a, c_a = uc_t[:, :CONV_W], uc_t[:, CONV_W:]
        halo = jnp.where(tile > 0, uch_ref[...].astype(F32), 0.0)[POOL_HALO - CONV_HALO:]
        z = c_a * u_a
        zcat = jnp.concatenate([halo[:, CONV_W:] * halo[:, :CONV_W], z], axis=0)
        z1 = _rows_from_before(zcat, 1)[CONV_HALO:]
        z2 = _rows_from_before(zcat, 2)[CONV_HALO:]
        dccat = jnp.concatenate([dconv, dcc_ref[...]], axis=0)
        dc1 = _rows_from_after(dccat, 1)[:BWD_TILE]
        dc2 = _rows_from_after(dccat, 2)[:BWD_TILE]
        dz = w2 * dconv + w1 * dc1 + w0 * dc2
        dcc_ref[...] = dconv[:CONV_HALO]
        dcps_ref[0:1, :] += jnp.sum(dconv * z2, axis=0, keepdims=True)
        dcps_ref[1:2, :] += jnp.sum(dconv * z1, axis=0, keepdims=True)
        dcps_ref[2:3, :] += jnp.sum(dconv * z, axis=0, keepdims=True)
        du_a = dz * c_a
        dc_a = dz * u_a

        dmixed = (dy_p * fp_ref[:, :POOL_W].astype(F32)).astype(BF16)
        dg_p = dy_p * fp_ref[:, POOL_W:2 * POOL_W].astype(F32)
        dcps_ref[3:4, :] += jnp.sum(dy_p * fp_ref[:, 2 * POOL_W:3 * POOL_W].astype(F32), axis=0, keepdims=True)
        counts = _window_counts(tile * BWD_TILE, BWD_TILE)
        du_p, q_head = [], []
        for g, w in enumerate(POOL_WINDOWS):
            cols = slice(g * GROUP_D, (g + 1) * GROUP_D)
            dm_g = dmixed[:, cols]
            dpooled_g = _dot_nt(dm_g, wpool_ref[g])
            gwpool_acc[g] += _dot_tn(fp_ref[:, 3 * POOL_W + g * GROUP_D:3 * POOL_W + (g + 1) * GROUP_D], dm_g)
            q_g = dpooled_g * (1.0 / jnp.minimum(counts, float(w)))
            q_head.append(q_g[:POOL_HALO])
            s = jnp.concatenate([q_g, qc_ref[:, cols]], axis=0)
            step = 1
            while step < w:
                s = s + _rows_from_after(s, step)
                step *= 2
            du_p.append(s[:BWD_TILE] - dpooled_g)
        qc_ref[...] = jnp.concatenate(q_head, axis=1)
        dproj_b = jnp.concatenate([du_a, db_a, dc_a, dg_a] + du_p + [dg_p], axis=1).astype(BF16)
        dproj_ref[...] = dproj_b

        x_t = x_ref[...]
        rx = lax.rsqrt(jnp.mean(x_t * x_t, axis=-1, keepdims=True) + NORM_EPS)
        xn = x_t * rx
        mod_scale = 1.0 + scale
        dh = _dot_nt(dproj_b, win_ref[...])
        dvec_ref[0:1, :] += jnp.sum(dh, axis=0, keepdims=True)
        dvec_ref[1:2, :] += jnp.sum(dh * xn, axis=0, keepdims=True)
        dxn = dh * (g_pre * mod_scale)
        dx_ref[...] = dxo_t + rx * (dxn - xn * jnp.mean(dxn * xn, axis=-1, keepdims=True))

        @pl.when(i == n_tiles - 1)
        def _():
            gwpool_ref[...] = gwpool_acc[...].astype(BF16)
            sum_dh_xn, sum_dxo_yh = dvec_ref[1:2, :], dvec_ref[2:3, :]
            dvec_ref[1:2, :] = sum_dh_xn * g_pre
            dvec_ref[3:4, :] = sum_dh_xn * mod_scale
            dvec_ref[2:3, :] = sum_dxo_yh * g_post
            dvec_ref[4:5, :] = sum_dxo_yh * gate

    return pl.pallas_call(
        body, name=f"backward_layer_{layer}", grid=(n_tiles,),
        in_specs=[rev(D_MODEL), rev(D_MODEL), rev(D_MODEL), rev(2 * CONV_W), halo_spec, rev(3 * CONV_W),
                  rev(4 * POOL_W), _layer_spec(vec.shape, layer), _layer_spec(cps.shape, layer),
                  _layer_spec(wpool.shape, layer), _whole_spec(wout.shape), _whole_spec(win.shape)],
        out_specs=[rev(D_MODEL), rev(IN_COLS), rev(D_MODEL), _whole_spec(gwpool_shape),
                   _whole_spec((SUBLANES, CONV_W)), _whole_spec((SUBLANES, D_MODEL))],
        out_shape=[jax.ShapeDtypeStruct((t_len, D_MODEL), F32), jax.ShapeDtypeStruct((t_len, IN_COLS), BF16),
                   jax.ShapeDtypeStruct((t_len, D_MODEL), BF16), jax.ShapeDtypeStruct(gwpool_shape, BF16),
                   jax.ShapeDtypeStruct((SUBLANES, CONV_W), F32), jax.ShapeDtypeStruct((SUBLANES, D_MODEL), F32)],
        scratch_shapes=[pltpu.VMEM(gwpool_shape, F32), pltpu.VMEM((CONV_HALO, CONV_W), F32),
                        pltpu.VMEM((POOL_HALO, POOL_W), F32)],
        compiler_params=_params(dimension_semantics=("arbitrary",)),
    )(dxo, y, x, uc, uc, fa, fp, vec, cps, wpool, wout, win)


def _weight_grads(layer, h_t, dproj, ycat_t, dy, exchange, after=()):
    t_len = dy.shape[0]
    n_in, n_out = IN_COLS // GWIN_COLS, D_MODEL // GWOUT_COLS
    args = [h_t, dproj, ycat_t, dy, *after]
    in_specs = [_whole_spec(h_t.shape),
                pl.BlockSpec((t_len, GWIN_COLS), lambda s: (0, jnp.minimum(s, n_in - 1))),
                _whole_spec(ycat_t.shape),
                pl.BlockSpec((t_len, GWOUT_COLS), lambda s: (0, jnp.maximum(s - n_in, 0)))] + [HBM] * len(after)
    out_shape = [jax.ShapeDtypeStruct((D_MODEL, IN_COLS), BF16), jax.ShapeDtypeStruct((D_MODEL, D_MODEL), BF16)]
    out_specs = [pl.BlockSpec((D_MODEL, GWIN_COLS), lambda s: (0, jnp.minimum(s, n_in - 1))),
                 pl.BlockSpec((D_MODEL, GWOUT_COLS), lambda s: (0, jnp.maximum(s - n_in, 0)))]
    scratch = []
    aliases, split = _host(exchange, args, in_specs, out_shape, out_specs, scratch)

    def body(*refs):
        (ht_ref, dproj_ref, ycatt_ref, dy_ref, *_, gwin_ref, gwout_ref), hosted = split(refs)
        s = pl.program_id(0)
        if hosted is not None:
            pl.when(s == 0)(hosted[0])

        @pl.when(s < n_in)
        def _():
            gwin_ref[...] = _dot(ht_ref[...], dproj_ref[...]).astype(BF16)

        @pl.when(s >= n_in)
        def _():
            gwout_ref[...] = _dot(ycatt_ref[...], dy_ref[...]).astype(BF16)

        if hosted is not None:
            pl.when(s == n_in + n_out - 1)(hosted[1])

    return pl.pallas_call(
        body, name=f"weight_grads_{layer}", grid=(n_in + n_out,), in_specs=in_specs, out_specs=out_specs,
        out_shape=out_shape, scratch_shapes=scratch, input_output_aliases=aliases,
        compiler_params=_params(dimension_semantics=("arbitrary",)),
    )(*args)


def _add_sibling_blocks(name, layer, grads, received, core, partials, kinds):
    n_arr = len(grads)

    def body(core_ref, *refs):
        mine, theirs, outs = refs[:n_arr], refs[n_arr:2 * n_arr], refs[-n_arr:]
        for a in range(n_arr):
            outs[a][...] = (mine[a][...].astype(F32) + theirs[a][...].astype(F32)).astype(BF16)

    own_of_kind = [
        pl.BlockSpec((D_MODEL, W_IN_SHARD), lambda q, core_ref: (0, 2 * q + core_ref[0])),
        pl.BlockSpec((W_OUT_SHARD, D_MODEL), lambda q, core_ref: (2 * q + core_ref[0], 0)),
        pl.BlockSpec((POOL_SHARD, GROUP_D), lambda q, core_ref: (2 * q + core_ref[0], 0)),
    ]
    shapes = [_BLOCK_SHAPES[k] for k in kinds]
    recv_specs = [pl.BlockSpec((None,) + s, lambda q, core_ref: (q, 0, 0)) for s in shapes]
    out_specs = [pl.BlockSpec((None, None) + s, lambda q, core_ref: (q, layer, 0, 0)) for s in shapes]
    args = [core, *grads, *received]
    in_specs = [own_of_kind[k] for k in kinds] + recv_specs
    aliases = {}
    if partials is not None:
        aliases = {len(args) + a: a for a in range(n_arr)}
        args += list(partials)
        in_specs += [HBM] * n_arr
    return pl.pallas_call(
        body, name=name,
        grid_spec=pltpu.PrefetchScalarGridSpec(
            num_scalar_prefetch=1, grid=(N_CHIP,), in_specs=in_specs, out_specs=out_specs),
        out_shape=[jax.ShapeDtypeStruct((N_CHIP, DEPTH) + s, BF16) for s in shapes],
        input_output_aliases=aliases,
        compiler_params=_params(dimension_semantics=("arbitrary",)),
    )(*args)


def _modulation_columns(c_all, w_ada):
    def body(c_ref, w_ref, cact_ref, out_ref):
        c_t = c_ref[...]
        c_act = c_t * _sigmoid(c_t)
        cact_ref[...] = c_act
        out_ref[...] = jnp.dot(c_act, w_ref[...], preferred_element_type=F32, precision=lax.Precision.HIGHEST)

    return pl.pallas_call(
        body, name="modulation_columns", grid=(DEPTH,),
        in_specs=[pl.BlockSpec((N_DEV, D_MODEL), lambda l: (0, 0)),
                  pl.BlockSpec((None, D_MODEL, W_IN_SHARD), lambda l: (l, 0, 0))],
        out_specs=[pl.BlockSpec((N_DEV, D_MODEL), lambda l: (0, 0)),
                   pl.BlockSpec((N_DEV, W_IN_SHARD), lambda l: (0, l))],
        out_shape=[jax.ShapeDtypeStruct((N_DEV, D_MODEL), F32),
                   jax.ShapeDtypeStruct((N_DEV, DEPTH * W_IN_SHARD), F32)],
        compiler_params=_params(dimension_semantics=("arbitrary",)),
    )(c_all, w_ada)


def _adamw(w, g, m, v):
    m_new = ADAM_B1 * m + (1.0 - ADAM_B1) * g
    v_new = ADAM_B2 * v + (1.0 - ADAM_B2) * (g * g)
    m_hat = m_new / (1.0 - ADAM_B1 ** ADAM_STEP)
    v_hat = v_new / (1.0 - ADAM_B2 ** ADAM_STEP)
    delta = -ADAM_LR * (m_hat / (jnp.sqrt(v_hat) + ADAM_EPS) + ADAM_WD * w)
    return delta, m_new, v_new


def _adamw_w_ada(w, m, v, c_act_t, dmod_cols):
    def body(w_ref, m_ref, v_ref, ct_ref, dm_ref, g_ref, d_ref, mo_ref, vo_ref):
        g = ct_ref[:, 0:1] * dm_ref[0:1, :]
        for b in range(1, N_DEV):
            g = g + ct_ref[:, b:b + 1] * dm_ref[b:b + 1, :]
        g_ref[...] = g
        d_ref[...], mo_ref[...], vo_ref[...] = _adamw(w_ref[...], g, m_ref[...], v_ref[...])

    big = pl.BlockSpec((None, D_MODEL, W_IN_SHARD), lambda l: (l, 0, 0))
    return pl.pallas_call(
        body, name="adamw_w_ada", grid=(DEPTH,),
        in_specs=[big, big, big, pl.BlockSpec((D_MODEL, N_DEV), lambda l: (0, 0)),
                  pl.BlockSpec((None, N_DEV, W_IN_SHARD), lambda l: (l, 0, 0))],
        out_specs=[big] * 4, out_shape=[jax.ShapeDtypeStruct(w.shape, F32)] * 4,
        compiler_params=_params(dimension_semantics=("arbitrary",)),
    )(w, m, v, c_act_t, dmod_cols)


def _sum_chip_partials(own_ref, recv_ref):
    g = own_ref[...].astype(F32)
    for j in range(N_OTHER_CHIPS):
        g = g + recv_ref[j].astype(F32)
    return g


def _partial_specs(row_tile, cols, first_layer=0):
    own = pl.BlockSpec((None, None, row_tile, cols), lambda l, r, chip_ref: (chip_ref[0], first_layer + l, r, 0))
    recv = pl.BlockSpec((N_OTHER_CHIPS, None, row_tile, cols), lambda l, r, chip_ref: (0, first_layer + l, r, 0))
    return own, recv


def _adamw_reduced(name, w, m, v, partial, received, chip, row_tile, layers, continued):
    depth, rows, cols = w.shape
    first, stop = layers

    def body(chip_ref, w_ref, m_ref, v_ref, own_ref, recv_ref, *rest):
        g_ref, d_ref, mo_ref, vo_ref = rest[-4:]
        g = _sum_chip_partials(own_ref, recv_ref)
        g_ref[...] = g
        d_ref[...], mo_ref[...], vo_ref[...] = _adamw(w_ref[...], g, m_ref[...], v_ref[...])

    blk = pl.BlockSpec((None, row_tile, cols), lambda l, r, chip_ref: (first + l, r, 0))
    args = [chip, w, m, v, partial, received]
    in_specs = [blk, blk, blk, *_partial_specs(row_tile, cols, first)]
    aliases = {}
    if continued is not None:
        aliases = {len(args) + k: k for k in range(4)}
        args += list(continued)
        in_specs += [HBM] * 4
    return pl.pallas_call(
        body, name=name,
        grid_spec=pltpu.PrefetchScalarGridSpec(
            num_scalar_prefetch=1, grid=(stop - first, rows // row_tile), in_specs=in_specs, out_specs=[blk] * 4),
        out_shape=[jax.ShapeDtypeStruct(w.shape, F32)] * 4, input_output_aliases=aliases,
        compiler_params=_params(dimension_semantics=("arbitrary", "arbitrary")),
    )(*args)


def _reduce_w_pool(partial, received, chip):
    def body(chip_ref, own_ref, recv_ref, g_ref):
        g_ref[...] = _sum_chip_partials(own_ref, recv_ref)

    return pl.pallas_call(
        body, name="reduce_w_pool",
        grid_spec=pltpu.PrefetchScalarGridSpec(
            num_scalar_prefetch=1, grid=(DEPTH, 1), in_specs=list(_partial_specs(POOL_SHARD, GROUP_D)),
            out_specs=pl.BlockSpec((POOL_SHARD, GROUP_D), lambda l, r, chip_ref: (l, 0))),
        out_shape=jax.ShapeDtypeStruct((DEPTH * POOL_SHARD, GROUP_D), F32),
        compiler_params=_params(dimension_semantics=("arbitrary", "arbitrary")),
    )(chip, partial, received)


def _adamw_small(params):
    n = len(params)

    def body(*refs):
        ins, outs = refs[:4 * n], refs[4 * n:]
        for p in range(n):
            w_ref, g_ref, m_ref, v_ref = ins[4 * p:4 * p + 4]
            d_ref, mo_ref, vo_ref = outs[3 * p:3 * p + 3]
            d_ref[...], mo_ref[...], vo_ref[...] = _adamw(w_ref[...], g_ref[...], m_ref[...], v_ref[...])

    vmem = pl.BlockSpec(memory_space=pltpu.VMEM)
    flat = [a for group in params for a in group]
    out_shape = [jax.ShapeDtypeStruct(group[0].shape, F32) for group in params for _ in range(3)]
    outs = pl.pallas_call(
        body, name="adamw_small", in_specs=[vmem] * len(flat), out_specs=[vmem] * len(out_shape),
        out_shape=out_shape, compiler_params=_params(),
    )(*flat)
    return [tuple(outs[3 * p:3 * p + 3]) for p in range(n)]


def _sum_sources(slabs):
    def body(s_ref, o_ref):
        acc = s_ref[0]
        for b in range(1, N_DEV):
            acc = acc + s_ref[b]
        o_ref[...] = acc

    vmem = pl.BlockSpec(memory_space=pltpu.VMEM)
    return pl.pallas_call(
        body, name="sum_small_grads", in_specs=[vmem], out_specs=vmem,
        out_shape=jax.ShapeDtypeStruct(slabs.shape[1:], F32), compiler_params=_params(),
    )(slabs)


def _to_bf16(a, name, layers=None):
    first, stop = layers or (0, a.shape[0])

    def body(a_ref, o_ref):
        o_ref[...] = a_ref[...].astype(BF16)

    block = (None,) + a.shape[1:]
    return pl.pallas_call(
        body, name=name, grid=(stop - first,), in_specs=[pl.BlockSpec(block, lambda l: (first + l, 0, 0))],
        out_specs=pl.BlockSpec(block, lambda l: (l, 0, 0)),
        out_shape=jax.ShapeDtypeStruct((stop - first,) + a.shape[1:], BF16),
        compiler_params=_params(dimension_semantics=("arbitrary",)),
    )(a)


def kernel(x, c, w_ada, b_ada, g_pre, w_in, w_conv, w_pool, pool_scale, w_out, g_post, loss_target, m_w_ada, m_b_ada, m_g_pre, m_w_in, m_w_conv, m_w_pool, m_pool_scale, m_w_out, m_g_post, v_w_ada, v_b_ada, v_g_pre, v_w_in, v_w_conv, v_w_pool, v_pool_scale, v_w_out, v_g_post):
    mx, my, mc = _mesh_position()
    me = _block_id(mx, my, mc)
    chip = (2 * mx + my).astype(jnp.int32).reshape(1)
    core = mc.astype(jnp.int32).reshape(1)
    x0 = x[0]
    target = loss_target[0]
    conv_shard = w_conv.shape[-1]

    own_small = jnp.concatenate([c, w_conv.reshape(1, DEPTH * 3 * conv_shard)], axis=1)
    all_small = _all_gather_small(own_small, "all_gather_c_w_conv")[:, 0, :]
    gathers = [_gather_weights_start(
        0, _to_bf16(w_in, "cast_w_in_0", (0, 1)), _to_bf16(w_out, "cast_w_out_0", (0, 1)), [all_small])]
    c_all = all_small[:, :D_MODEL]
    w_conv_full = all_small[:, D_MODEL:].reshape(N_DEV, DEPTH, 3, conv_shard).transpose(1, 2, 0, 3).reshape(
        DEPTH, 3, CONV_W)
    cps = jnp.concatenate([w_conv_full, pool_scale[:, None], jnp.zeros((DEPTH, 4, CONV_W), F32)], axis=1)

    c_act, pieces = _modulation_columns(c_all, w_ada)
    upper = (1, DEPTH)
    upper_shards = [_to_bf16(w_in, "cast_w_in_1", upper), _to_bf16(w_out, "cast_w_out_1", upper)]
    wpool_b = _to_bf16(w_pool.reshape(DEPTH, POOL_ROWS, GROUP_D), "cast_w_pool").reshape(w_pool.shape)
    mod_all = _all_gather_small(
        pieces, "all_gather_modulation", [gathers[0][1][0], *upper_shards, wpool_b])
    mod_mine = lax.dynamic_index_in_dim(mod_all, me, axis=1, keepdims=False)
    mod = mod_mine.reshape(N_DEV, DEPTH, W_IN_SHARD).transpose(1, 0, 2).reshape(DEPTH, 3 * D_MODEL) + b_ada
    zeros_d = jnp.zeros((DEPTH, 3, D_MODEL), F32)
    vec = jnp.concatenate([mod.reshape(DEPTH, 3, D_MODEL), g_pre[:, None], g_post[:, None], zeros_d], axis=1)

    gathers.append(_gather_weights_start(1, *upper_shards, [mod_all]))
    gathers = [(first_sems, shards, landing[k], k) for first_sems, shards, landing in gathers
               for k in range(len(landing))]

    xs, kept, wins, wouts = [x0], [], [], []
    for l in range(DEPTH):
        first_sems, shards, zones, index = gathers[l]
        passed_sems, zones = _gather_weights_pass_on(
            l, index, first_sems[1], zones, [vec, cps, wpool_b, gathers[-1][1][0]] if l == 0 else [xs[-1]])
        win, wout = _gather_weights_finish(l, index, first_sems, passed_sems, shards, zones)
        x_next, *for_backward = _forward_layer(
            l, xs[-1], vec, cps, wpool_b, win, wout, target if l == DEPTH - 1 else None)
        xs.append(x_next)
        kept.append(for_backward[:6])
        wins.append(win)
        wouts.append(wout)
    dx, loss_tile = xs[DEPTH], for_backward[6]

    slab_rows = [None] * DEPTH
    partials = received = None
    in_flight = []

    def scatter(layer, grads, from_sibling, after):
        nonlocal partials, received
        partials = _add_sibling_blocks(
            f"grad_add_sibling_{layer}", layer, grads, from_sibling, core, partials, ALL_KINDS)
        chips = _chips_exchange(layer, partials, received, ALL_KINDS)
        sems, partials, received, token = _start_exchange(chips, f"grad_chips_start_{layer}", after)
        in_flight.append((chips, sems, layer))
        return token

    grads_above = None
    for l in reversed(range(DEPTH)):
        y, h_t, ycat_t, uc, fa, fp = kept[l]
        dx, dproj, dy, gwpool, dcps, dvec = _backward_layer(
            l, dx, y, xs[l], uc, fa, fp, vec, cps, wpool_b, wouts[l], wins[l])
        slab_rows[l] = jnp.concatenate(
            [dvec[0], dvec[1], dvec[2], dvec[3], dvec[4], dcps[3], dcps[0], dcps[1], dcps[2],
             loss_tile[0] if l == 0 else jnp.zeros((LANES,), F32)])
        after = []
        if l == 0:
            gather_small = _all_gather_exchange(jnp.stack(slab_rows))
            sems_s, slab, slabs, token = _start_exchange(gather_small, "all_gather_small_grads_start")
            after = [token]
        hosted = _sibling_exchange(grads_above, ALL_KINDS) if grads_above is not None else None
        gwin, gwout, *from_sibling = _weight_grads(l, h_t, dproj, ycat_t, dy, hosted, after)
        if l == 0:
            _, (slabs,) = _finish_exchange(
                gather_small, "all_gather_small_grads_finish", sems_s, slab, slabs, [gwin])
        if grads_above is not None:
            scatter(l + 1, grads_above, from_sibling, [])
        grads_above = [gwin, gwout, gwpool.reshape(POOL_ROWS, GROUP_D)]
        if l <= 1:
            from_sibling = _run_exchange(_sibling_exchange(grads_above, ALL_KINDS), f"grad_exchange_sibling_{l}")
            token = scatter(l, grads_above, from_sibling, [slabs] if l == 0 else [])
            grads_above = None
    grad_x = dx[None]
    chips_0, sems_0, _ = in_flight.pop()

    total = _sum_sources(slabs)
    loss = total[0, SLAB_COLS]
    o = 3 * D_MODEL
    g_b_ada = total[:, :o]
    g_g_pre = total[:, o:o + D_MODEL]
    g_g_post = total[:, o + D_MODEL:o + 2 * D_MODEL]
    g_pool_scale = total[:, o + 2 * D_MODEL:o + 2 * D_MODEL + POOL_W]
    g_conv_full = total[:, o + 2 * D_MODEL + POOL_W:SLAB_COLS].reshape(DEPTH, 3, CONV_W)
    g_w_conv = lax.dynamic_slice_in_dim(g_conv_full, me * conv_shard, conv_shard, axis=2)

    after = [token]
    for chips, sems, l in in_flight:
        partials, received = _finish_exchange(chips, f"grad_chips_finish_{l}", sems, partials, received, after)
        after = []
    upper = (1, DEPTH)
    w_in_upper = _adamw_reduced(
        "adamw_w_in_upper", w_in, m_w_in, v_w_in, partials[0], received[0], chip, ROW_TILE, upper, None)
    w_out_upper = _adamw_reduced(
        "adamw_w_out_upper", w_out, m_w_out, v_w_out, partials[1], received[1], chip, W_OUT_SHARD, upper, None)
    dmod_all = slabs[:, :, :o].reshape(N_DEV, DEPTH, N_DEV, W_IN_SHARD)
    dmod_cols = lax.dynamic_index_in_dim(dmod_all, me, axis=2, keepdims=False).transpose(1, 0, 2) + token[0, 0]
    g_w_ada, d_w_ada, nm_w_ada, nv_w_ada = _adamw_w_ada(w_ada, m_w_ada, v_w_ada, c_act.T, dmod_cols)

    partials, received = _finish_exchange(
        chips_0, "grad_chips_finish_0", sems_0, partials, received, [nv_w_ada, w_in_upper[3], w_out_upper[3]])
    gather_pool = _all_gather_exchange(_reduce_w_pool(partials[2], received[2], chip))
    sems_p, pool_rows, pool_landing, token_p = _start_exchange(gather_pool, "all_gather_grad_w_pool_start")
    g_w_in, d_w_in, nm_w_in, nv_w_in = _adamw_reduced(
        "adamw_w_in_0", w_in, m_w_in, v_w_in, partials[0], received[0], chip, ROW_TILE, (0, 1), w_in_upper)
    g_w_out, d_w_out, nm_w_out, nv_w_out = _adamw_reduced(
        "adamw_w_out_0", w_out, m_w_out, v_w_out, partials[1], received[1], chip, W_OUT_SHARD, (0, 1), w_out_upper)
    _, (g_pool_all,) = _finish_exchange(
        gather_pool, "all_gather_grad_w_pool_finish", sems_p, pool_rows, pool_landing, [nv_w_in, nv_w_out])
    g_w_pool = g_pool_all.reshape(N_DEV, DEPTH, POOL_SHARD, GROUP_D).transpose(1, 0, 2, 3).reshape(w_pool.shape)

    flat2 = lambda a: a.reshape(-1, a.shape[-1])
    small = _adamw_small([
        (b_ada, g_b_ada, m_b_ada, v_b_ada),
        (g_pre, g_g_pre, m_g_pre, v_g_pre),
        (flat2(w_conv), flat2(g_w_conv), flat2(m_w_conv), flat2(v_w_conv)),
        (flat2(w_pool), flat2(g_w_pool), flat2(m_w_pool), flat2(v_w_pool)),
        (pool_scale, g_pool_scale, m_pool_scale, v_pool_scale),
        (g_post, g_g_post, m_g_post, v_g_post),
    ])
    (d_b_ada, nm_b_ada, nv_b_ada), (d_g_pre, nm_g_pre, nv_g_pre), conv_upd, pool_upd, \
        (d_ps, nm_ps, nv_ps), (d_g_post, nm_g_post, nv_g_post) = small
    d_w_conv, nm_w_conv, nv_w_conv = (a.reshape(w_conv.shape) for a in conv_upd)
    d_w_pool, nm_w_pool, nv_w_pool = (a.reshape(w_pool.shape) for a in pool_upd)

    return (loss, grad_x,
            g_w_ada, g_b_ada, g_g_pre, g_w_in, g_w_conv, g_w_pool, g_pool_scale, g_w_out, g_g_post,
            d_w_ada, d_b_ada, d_g_pre, d_w_in, d_w_conv, d_w_pool, d_ps, d_w_out, d_g_post,
            nm_w_ada, nm_b_ada, nm_g_pre, nm_w_in, nm_w_conv, nm_w_pool, nm_ps, nm_w_out, nm_g_post,
            nv_w_ada, nv_b_ada, nv_g_pre, nv_w_in, nv_w_conv, nv_w_pool, nv_ps, nv_w_out, nv_g_post)
```

```python
import jax
import jax.numpy as jnp
from jax import lax
from jax.experimental import pallas as pl
from jax.experimental.pallas import tpu as pltpu

F32 = jnp.float32
BF16 = jnp.bfloat16

D_MODEL = 1024
DEPTH = 4
CONV_W = 512
POOL_W = 512
POOL_WINDOWS = (2, 4, 8, 16)
GROUP_D = 128
IN_COLS = 4 * CONV_W + 2 * POOL_W
NORM_EPS = 1e-6

ADAM_LR = 0.001
ADAM_B1 = 0.9
ADAM_B2 = 0.999
ADAM_EPS = 1e-08
ADAM_WD = 0.01
ADAM_STEP = 10

N_DEV = 8
N_CHIP = 4
N_OTHER_CHIPS = N_CHIP - 1
MESH = pl.DeviceIdType.MESH
W_IN_SHARD = IN_COLS // N_DEV
W_OUT_SHARD = D_MODEL // N_DEV
POOL_ROWS = len(POOL_WINDOWS) * GROUP_D
POOL_SHARD = POOL_ROWS // N_DEV

SUBLANES = 8
LANES = 128
VMEM_LIMIT_BYTES = 56 * 1024 * 1024
ROW_TILE = 512
BWD_TILE = 256
GWIN_COLS = 768
GWOUT_COLS = 512
POOL_HALO = 16
CONV_HALO = SUBLANES

SLAB_COLS = 3 * D_MODEL + D_MODEL + D_MODEL + POOL_W + 3 * CONV_W

HBM = pl.BlockSpec(memory_space=pl.ANY)


def _params(**kw):
    return pltpu.CompilerParams(vmem_limit_bytes=VMEM_LIMIT_BYTES, **kw)


def _sigmoid(v):
    return 1.0 / (1.0 + jnp.exp(-v))


def _dot(a, b):
    return jnp.dot(a, b, preferred_element_type=F32)


def _dot_tn(a, b):
    return lax.dot_general(a, b, (((0,), (0,)), ((), ())), preferred_element_type=F32)


def _dot_nt(a, b):
    return lax.dot_general(a, b, (((1,), (1,)), ((), ())), preferred_element_type=F32)


def _rows_from_before(v, k):
    return pltpu.roll(v, k, 0)


def _rows_from_after(v, k):
    return pltpu.roll(v, v.shape[0] - k, 0)


def _window_counts(t0, rows):
    return (lax.broadcasted_iota(jnp.int32, (rows, 1), 0) + (t0 + 1)).astype(F32)


def _split_proj(p32):
    cw = CONV_W
    return (p32[:, 0 * cw:1 * cw], p32[:, 1 * cw:2 * cw], p32[:, 2 * cw:3 * cw], p32[:, 3 * cw:4 * cw],
            p32[:, 4 * cw:4 * cw + POOL_W], p32[:, 4 * cw + POOL_W:])


def _layer_spec(shape, layer):
    nd = len(shape)
    return pl.BlockSpec((None,) + tuple(shape[1:]), lambda i, _l=layer, _n=nd: (_l,) + (0,) * (_n - 1))


def _whole_spec(shape):
    return pl.BlockSpec(tuple(shape), lambda i, _n=len(shape): (0,) * _n, pipeline_mode=pl.Buffered(1))


def _mesh_position():
    return lax.axis_index("x"), lax.axis_index("y"), lax.axis_index("c")


def _block_id(x, y, c):
    return 4 * x + 2 * y + c


def _other_chips(x, y):
    return [(x ^ 1, y), (x, y ^ 1), (x ^ 1, y ^ 1)]


def _col_block(ref, blk):
    return ref.at[:, pl.ds(pl.multiple_of(blk * W_IN_SHARD, LANES), W_IN_SHARD)]


def _row_block(rows):
    def block(ref, blk):
        return ref.at[pl.ds(pl.multiple_of(blk * rows, rows), rows), :]
    return block


_BLOCK_OF = (_col_block, _row_block(W_OUT_SHARD), _row_block(POOL_SHARD))
_BLOCK_SHAPES = ((D_MODEL, W_IN_SHARD), (W_OUT_SHARD, D_MODEL), (POOL_SHARD, GROUP_D))


class _Exchange:
    def __init__(self, inputs, out_shapes, aliases, sem_shapes, make):
        self.inputs, self.out_shapes, self.aliases, self.sem_shapes, self.make = (
            list(inputs), list(out_shapes), dict(aliases), list(sem_shapes), make)


def _run_exchange(exchange, name):
    n_in, n_out = len(exchange.inputs), len(exchange.out_shapes)

    def body(*refs):
        start, finish = exchange.make(refs[:n_in], refs[n_in:n_in + n_out], refs[n_in + n_out:])
        start()
        finish()

    return pl.pallas_call(
        body, name=name, in_specs=[HBM] * n_in, out_specs=[HBM] * n_out, out_shape=exchange.out_shapes,
        scratch_shapes=exchange.sem_shapes, input_output_aliases=exchange.aliases, compiler_params=_params(),
    )(*exchange.inputs)


_SEM = pl.BlockSpec(memory_space=pltpu.SEMAPHORE)
_DATAFLOW = pltpu.SideEffectType.DATAFLOW_SIDE_EFFECTING


def _start_exchange(exchange, name, after=()):
    n_in, n_out, n_sem = len(exchange.inputs), len(exchange.out_shapes), len(exchange.sem_shapes)
    sources = [i for i in range(n_in) if i not in exchange.aliases]
    aliases = {i: n_sem + k for k, i in enumerate(sources)}
    aliases.update({i: n_sem + len(sources) + o for i, o in exchange.aliases.items()})

    def body(*refs):
        in_refs = refs[:n_in]
        outs = refs[n_in + len(after):]
        sems = outs[:n_sem]
        out_refs = outs[n_sem + len(sources):n_sem + len(sources) + n_out]
        exchange.make(in_refs, out_refs, sems)[0]()
        refs[-1][...] = jnp.zeros_like(refs[-1])

    outs = pl.pallas_call(
        body, name=name, in_specs=[HBM] * (n_in + len(after)),
        out_specs=[_SEM] * n_sem + [HBM] * (len(sources) + n_out) + [pl.BlockSpec(memory_space=pltpu.VMEM)],
        out_shape=(exchange.sem_shapes + [pltpu.HBM(exchange.inputs[i].shape, exchange.inputs[i].dtype) for i in sources]
                   + [pltpu.HBM(s.shape, s.dtype) for s in exchange.out_shapes]
                   + [jax.ShapeDtypeStruct((SUBLANES, LANES), F32)]),
        input_output_aliases=aliases, compiler_params=_params(has_side_effects=_DATAFLOW),
    )(*exchange.inputs, *after)
    return outs[:n_sem], outs[n_sem:n_sem + len(sources)], outs[n_sem + len(sources):-1], outs[-1]


def _finish_exchange(exchange, name, sems, sources, landing, after):
    n_src, n_out, n_sem = len(sources), len(landing), len(sems)
    n_in = len(exchange.inputs)
    source_at = [i for i in range(n_in) if i not in exchange.aliases]

    def body(*refs):
        src_refs, out_refs = refs[:n_src], refs[n_src:n_src + n_out]
        sem_refs = refs[n_src + n_out:n_src + n_out + n_sem]
        in_refs = [None] * n_in
        for k, i in enumerate(source_at):
            in_refs[i] = src_refs[k]
        for i, o in exchange.aliases.items():
            in_refs[i] = out_refs[o]
        exchange.make(in_refs, out_refs, sem_refs)[1]()

    arrays = list(sources) + list(landing)
    outs = pl.pallas_call(
        body, name=name, in_specs=[HBM] * len(arrays) + [_SEM] * n_sem + [HBM] * len(after),
        out_specs=[HBM] * len(arrays), out_shape=[pltpu.HBM(a.shape, a.dtype) for a in arrays],
        input_output_aliases={i: i for i in range(len(arrays))}, compiler_params=_params(has_side_effects=_DATAFLOW),
    )(*arrays, *sems, *after)
    return outs[:n_src], outs[n_src:]


N_GATHERED = 2
FIRST_COPIES = 1 + N_OTHER_CHIPS
_GATHERED_SHAPES = ((D_MODEL, IN_COLS), (D_MODEL, D_MODEL))


def _first_sem(layer, a, k):
    return (layer * N_GATHERED + a) * FIRST_COPIES + k


def _gather_copy(window_of, full_ref, blk, send_sem, recv_sem, to, src=None):
    window = window_of(full_ref, blk)
    return pltpu.make_async_remote_copy(
        src_ref=window if src is None else src, dst_ref=window, send_sem=send_sem, recv_sem=recv_sem,
        device_id=to, device_id_type=MESH)


def _first_copies(layer, shard_refs, full_refs, send_sems, recv_sems, local_sems):
    x, y, c = _mesh_position()
    me = _block_id(x, y, c)
    own, remote = [], []
    for a in range(N_GATHERED):
        shard = shard_refs[a].at[layer]
        own.append(pltpu.make_async_copy(
            shard, _BLOCK_OF[a](full_refs[a], me), local_sems.at[layer * N_GATHERED + a]))
        targets = [(x, y, 1 - c)] + [(*chip, c) for chip in _other_chips(x, y)]
        remote += [_gather_copy(_BLOCK_OF[a], full_refs[a], me, send_sems.at[_first_sem(layer, a, k)],
                                recv_sems.at[_first_sem(layer, a, k)], to, src=shard)
                   for k, to in enumerate(targets)]
    return own, remote


def _gather_weights_start(first_layer, win_shards, wout_shards, after):
    n_layers = win_shards.shape[0]
    n_first = n_layers * N_GATHERED * FIRST_COPIES
    sem_shapes = [pltpu.SemaphoreType.DMA((n_first,)), pltpu.SemaphoreType.DMA((n_first,)),
                  pltpu.SemaphoreType.DMA((n_layers * N_GATHERED,))]
    shards = [win_shards, wout_shards]

    def body(win_sh, wout_sh, *rest):
        send_sems, recv_sems, local_sems, win_thru, wout_thru, *landing = rest[len(after):]
        for layer in range(n_layers):
            own, remote = _first_copies(layer, (win_sh, wout_sh), landing[N_GATHERED * layer:N_GATHERED * (layer + 1)],
                                        send_sems, recv_sems, local_sems)
            for cp in own + remote:
                cp.start()

    outs = pl.pallas_call(
        body, name=f"all_gather_weights_start_{first_layer}", in_specs=[HBM] * (2 + len(after)),
        out_specs=[_SEM] * 3 + [HBM] * (2 + n_layers * N_GATHERED),
        out_shape=(sem_shapes + [pltpu.HBM(s.shape, s.dtype) for s in shards]
                   + [pltpu.HBM(s, BF16) for _ in range(n_layers) for s in _GATHERED_SHAPES]),
        input_output_aliases={0: 3, 1: 4}, compiler_params=_params(has_side_effects=_DATAFLOW),
    )(*shards, *after)
    landing = outs[5:]
    return outs[:3], outs[3:5], [landing[N_GATHERED * l:N_GATHERED * (l + 1)] for l in range(n_layers)]


def _passed_on_copies(full_refs, send_sems, recv_sems, core_of_block):
    x, y, c = _mesh_position()
    return [_gather_copy(_BLOCK_OF[a], full_refs[a], _block_id(*chip, core_of_block),
                         send_sems.at[a * N_OTHER_CHIPS + j], recv_sems.at[a * N_OTHER_CHIPS + j], (x, y, 1 - c))
            for a in range(N_GATHERED) for j, chip in enumerate(_other_chips(x, y))]


def _gather_weights_pass_on(layer, index, first_recv_sems, landing, after):
    n = N_GATHERED * N_OTHER_CHIPS

    def body(win_ref, wout_ref, first_recv, *rest):
        send_sems, recv_sems = rest[len(after):len(after) + 2]
        x, y, c = _mesh_position()
        full_refs = (win_ref, wout_ref)
        passed = _passed_on_copies(full_refs, send_sems, recv_sems, c)
        for a in range(N_GATHERED):
            for j, chip in enumerate(_other_chips(x, y)):
                sem = _first_sem(index, a, 1 + j)
                _gather_copy(_BLOCK_OF[a], full_refs[a], _block_id(*chip, c), first_recv.at[sem], first_recv.at[sem],
                             (x, y, c)).wait_recv()
                passed[a * N_OTHER_CHIPS + j].start()

    outs = pl.pallas_call(
        body, name=f"all_gather_weights_pass_on_{layer}", in_specs=[HBM] * N_GATHERED + [_SEM] + [HBM] * len(after),
        out_specs=[_SEM] * 2 + [HBM] * N_GATHERED,
        out_shape=[pltpu.SemaphoreType.DMA((n,)), pltpu.SemaphoreType.DMA((n,))]
        + [pltpu.HBM(a.shape, a.dtype) for a in landing],
        input_output_aliases={a: 2 + a for a in range(N_GATHERED)},
        compiler_params=_params(has_side_effects=_DATAFLOW),
    )(*landing, first_recv_sems, *after)
    return outs[:2], outs[2:]


def _gather_weights_finish(layer, index, first_sems, passed_sems, shards, landing):
    def body(win_ref, wout_ref, first_send, first_recv, local_sems, passed_send, passed_recv, win_sh, wout_sh, *thru):
        x, y, c = _mesh_position()
        full_refs = (win_ref, wout_ref)
        own, remote = _first_copies(index, (win_sh, wout_sh), full_refs, first_send, first_recv, local_sems)
        for a in range(N_GATHERED):
            sem = _first_sem(index, a, 0)
            _gather_copy(_BLOCK_OF[a], full_refs[a], _block_id(x, y, 1 - c), first_recv.at[sem], first_recv.at[sem],
                         (x, y, c)).wait_recv()
        for cp in _passed_on_copies(full_refs, passed_send, passed_recv, 1 - c):
            cp.wait_recv()
        for cp in remote + _passed_on_copies(full_refs, passed_send, passed_recv, c):
            cp.wait_send()
        for cp in own:
            cp.wait()

    return pl.pallas_call(
        body, name=f"all_gather_weights_finish_{layer}", in_specs=[HBM] * N_GATHERED + [_SEM] * 5 + [HBM] * 2,
        out_specs=[HBM] * N_GATHERED, out_shape=[pltpu.HBM(a.shape, a.dtype) for a in landing],
        input_output_aliases={a: a for a in range(N_GATHERED)}, compiler_params=_params(has_side_effects=_DATAFLOW),
    )(*landing, *first_sems, *passed_sems, *shards)


ALL_KINDS = (0, 1, 2)


def _sibling_exchange(grads, kinds):
    n_arr = len(grads)

    def make(in_refs, out_refs, sems):
        send_sems, recv_sems = sems
        x, y, c = _mesh_position()
        copies = [pltpu.make_async_remote_copy(
            src_ref=_BLOCK_OF[kinds[a]](in_refs[a], 2 * q + (1 - c)), dst_ref=out_refs[a].at[q],
            send_sem=send_sems.at[a * N_CHIP + q], recv_sem=recv_sems.at[a * N_CHIP + q],
            device_id=(x, y, 1 - c), device_id_type=MESH)
            for a in range(n_arr) for q in range(N_CHIP)]

        def start():
            for cp in copies:
                cp.start()

        def finish():
            for cp in copies:
                cp.wait_recv()
            for cp in copies:
                cp.wait_send()

        return start, finish

    return _Exchange(
        grads, [jax.ShapeDtypeStruct((N_CHIP,) + _BLOCK_SHAPES[k], BF16) for k in kinds], {},
        [pltpu.SemaphoreType.DMA((n_arr * N_CHIP,)), pltpu.SemaphoreType.DMA((n_arr * N_CHIP,))], make)


def _chips_exchange(layer, partials, received, kinds):
    n_arr = len(partials)

    def make(in_refs, out_refs, sems):
        send_sems, recv_sems = sems
        x, y, c = _mesh_position()
        copies = [pltpu.make_async_remote_copy(
            src_ref=in_refs[a].at[2 * qx + qy, layer], dst_ref=out_refs[a].at[j, layer],
            send_sem=send_sems.at[a * N_OTHER_CHIPS + j], recv_sem=recv_sems.at[a * N_OTHER_CHIPS + j],
            device_id=(qx, qy, c), device_id_type=MESH)
            for a in range(n_arr) for j, (qx, qy) in enumerate(_other_chips(x, y))]

        def start():
            for cp in copies:
                cp.start()

        def finish():
            for cp in copies:
                cp.wait_recv()
            for cp in copies:
                cp.wait_send()

        return start, finish

    inputs = list(partials)
    aliases = {}
    if received is not None:
        inputs += list(received)
        aliases = {n_arr + a: a for a in range(n_arr)}
    return _Exchange(
        inputs, [jax.ShapeDtypeStruct((N_OTHER_CHIPS, DEPTH) + _BLOCK_SHAPES[k], BF16) for k in kinds], aliases,
        [pltpu.SemaphoreType.DMA((n_arr * N_OTHER_CHIPS,)), pltpu.SemaphoreType.DMA((n_arr * N_OTHER_CHIPS,))], make)


def _all_gather_exchange(v):
    def make(in_refs, out_refs, sems):
        send_sems, recv_sems, local_sem = sems
        x, y, c = _mesh_position()
        me = _block_id(x, y, c)
        own = pltpu.make_async_copy(in_refs[0], out_refs[0].at[me], local_sem.at[0])
        sends, arrivals = [], []
        for k in range(1, N_DEV):
            px, py, pc = x ^ ((k >> 2) & 1), y ^ ((k >> 1) & 1), c ^ (k & 1)
            sends.append(pltpu.make_async_remote_copy(
                src_ref=in_refs[0], dst_ref=out_refs[0].at[me], send_sem=send_sems.at[k - 1],
                recv_sem=recv_sems.at[k - 1], device_id=(px, py, pc), device_id_type=MESH))
            arrivals.append(pltpu.make_async_remote_copy(
                src_ref=in_refs[0], dst_ref=out_refs[0].at[_block_id(px, py, pc)], send_sem=send_sems.at[k - 1],
                recv_sem=recv_sems.at[k - 1], device_id=(x, y, c), device_id_type=MESH))

        def start():
            for cp in [own] + sends:
                cp.start()

        def finish():
            for cp in arrivals:
                cp.wait_recv()
            for cp in sends:
                cp.wait_send()
            own.wait()

        return start, finish

    return _Exchange(
        [v], [jax.ShapeDtypeStruct((N_DEV,) + v.shape, v.dtype)], {},
        [pltpu.SemaphoreType.DMA((N_DEV - 1,)), pltpu.SemaphoreType.DMA((N_DEV - 1,)),
         pltpu.SemaphoreType.DMA((1,))], make)


def _host(exchange, args, in_specs, out_shape, out_specs, scratch):
    n_own = (len(args), len(out_shape), len(scratch))
    if exchange is None:
        return {}, lambda refs: (refs, None)
    n_ex = (len(exchange.inputs), len(exchange.out_shapes), len(exchange.sem_shapes))
    aliases = {n_own[0] + i: n_own[1] + o for i, o in exchange.aliases.items()}
    args += exchange.inputs
    in_specs += [HBM] * n_ex[0]
    out_shape += exchange.out_shapes
    out_specs += [HBM] * n_ex[1]
    scratch += exchange.sem_shapes

    def split(refs):
        own, theirs, at = [], [], 0
        for mine, ex in zip(n_own, n_ex):
            own += refs[at:at + mine]
            theirs.append(refs[at + mine:at + mine + ex])
            at += mine + ex
        return own, exchange.make(*theirs)

    return aliases, split


def _all_gather_small(v, name, after=()):
    vmem = pl.BlockSpec(memory_space=pltpu.VMEM)

    def body(v_ref, *rest):
        out_ref, send_sems, recv_sems = rest[len(after):]
        x, y, c = _mesh_position()
        me = _block_id(x, y, c)
        out_ref[me] = v_ref[...]
        sends = []
        for k in range(1, N_DEV):
            px, py, pc = x ^ ((k >> 2) & 1), y ^ ((k >> 1) & 1), c ^ (k & 1)
            send = pltpu.make_async_remote_copy(
                src_ref=v_ref, dst_ref=out_ref.at[me], send_sem=send_sems.at[k - 1], recv_sem=recv_sems.at[k - 1],
                device_id=(px, py, pc), device_id_type=MESH)
            send.start()
            sends.append((send, _block_id(px, py, pc)))
        for k, (send, peer) in enumerate(sends):
            pltpu.make_async_remote_copy(
                src_ref=v_ref, dst_ref=out_ref.at[peer], send_sem=send_sems.at[k], recv_sem=recv_sems.at[k],
                device_id=(x, y, c), device_id_type=MESH).wait_recv()
        for send, _ in sends:
            send.wait_send()

    return pl.pallas_call(
        body, name=name, in_specs=[vmem] + [HBM] * len(after), out_specs=vmem,
        out_shape=jax.ShapeDtypeStruct((N_DEV,) + v.shape, v.dtype),
        scratch_shapes=[pltpu.SemaphoreType.DMA((N_DEV - 1,)), pltpu.SemaphoreType.DMA((N_DEV - 1,))],
        compiler_params=_params(),
    )(v, *after)


def _forward_layer(layer, x, vec, cps, wpool, win, wout, target=None):
    t_len = x.shape[0]
    n_tiles = t_len // ROW_TILE
    row = lambda cols: pl.BlockSpec((ROW_TILE, cols), lambda i: (i, 0))
    col_t = pl.BlockSpec((D_MODEL, ROW_TILE), lambda i: (0, i))
    widths = (D_MODEL, 2 * CONV_W, 3 * CONV_W, 4 * POOL_W)
    head = target is not None

    def body(x_ref, vec_ref, cps_ref, wpool_ref, win_ref, wout_ref, *rest):
        target_ref = rest[0] if head else None
        xo_ref, y_ref, ht_ref, ycatt_ref, uc_ref, fa_ref, fp_ref = rest[head:head + 7]
        loss_ref = rest[head + 7] if head else None
        zc_ref, pc_ref = rest[-2:]
        i = pl.program_id(0)

        @pl.when(i == 0)
        def _():
            zc_ref[...] = jnp.zeros_like(zc_ref)
            pc_ref[...] = jnp.zeros_like(pc_ref)
            if head:
                loss_ref[...] = jnp.zeros_like(loss_ref)

        x_t = x_ref[...]
        shift, scale, gate = vec_ref[0:1, :], vec_ref[1:2, :], vec_ref[2:3, :]
        g_pre, g_post = vec_ref[3:4, :], vec_ref[4:5, :]
        w0, w1, w2, ps = cps_ref[0:1, :], cps_ref[1:2, :], cps_ref[2:3, :], cps_ref[3:4, :]
        rx = lax.rsqrt(jnp.mean(x_t * x_t, axis=-1, keepdims=True) + NORM_EPS)
        h = (x_t * rx) * g_pre * (1.0 + scale) + shift
        ht_ref[...] = h.T.astype(BF16)
        proj = _dot(h.astype(BF16), win_ref[...]).astype(BF16).astype(F32)
        u_a, b_a, c_a, g_a, u_p, g_p = _split_proj(proj)
        uc_ref[...] = jnp.concatenate([u_a, c_a], axis=1).astype(BF16)

        z = c_a * u_a
        zcat = jnp.concatenate([zc_ref[...], z], axis=0)
        zc_ref[...] = z[ROW_TILE - CONV_HALO:]
        conv = (w0 * _rows_from_before(zcat, 2)[CONV_HALO:] + w1 * _rows_from_before(zcat, 1)[CONV_HALO:] + w2 * z)
        sig_a = _sigmoid(g_a)
        silu_a = g_a * sig_a
        b_conv = b_a * conv
        y_a = b_conv * silu_a
        fa_ref[...] = jnp.concatenate(
            [silu_a * conv, silu_a * b_a, b_conv * (sig_a + silu_a * (1.0 - sig_a))], axis=1).astype(BF16)

        pcat = jnp.concatenate([pc_ref[...], u_p], axis=0)
        pc_ref[...] = u_p[ROW_TILE - POOL_HALO:]
        counts = _window_counts(i * ROW_TILE, ROW_TILE)
        pooled, mixed = [], []
        for g, w in enumerate(POOL_WINDOWS):
            cols = slice(g * GROUP_D, (g + 1) * GROUP_D)
            s = pcat[:, cols]
            step = 1
            while step < w:
                s = s + _rows_from_before(s, step)
                step *= 2
            pooled_g = (s[POOL_HALO:] * (1.0 / jnp.minimum(counts, float(w))) - u_p[:, cols]).astype(BF16)
            pooled.append(pooled_g)
            mixed.append(_dot(pooled_g, wpool_ref[g]))
        mixed = jnp.concatenate(mixed, axis=1)
        sig_p = _sigmoid(g_p)
        silu_p = g_p * sig_p
        mixed_ps = mixed * ps
        y_p = mixed_ps * silu_p
        fp_ref[...] = jnp.concatenate(
            [(ps * silu_p).astype(BF16), (mixed_ps * (sig_p + silu_p * (1.0 - sig_p))).astype(BF16),
             (silu_p * mixed).astype(BF16)] + pooled, axis=1)

        ycat = jnp.concatenate([y_a, y_p], axis=1)
        ycatt_ref[...] = ycat.T.astype(BF16)
        y_b = _dot(ycat.astype(BF16), wout_ref[...]).astype(BF16)
        y_ref[...] = y_b
        y_t = y_b.astype(F32)
        ry = lax.rsqrt(jnp.mean(y_t * y_t, axis=-1, keepdims=True) + NORM_EPS)
        x_next = x_t + gate * (y_t * ry * g_post)
        if head:
            err = x_next - target_ref[...]
            xo_ref[...] = err * (1.0 / D_MODEL)
            loss_ref[...] += jnp.sum(err * err) * (0.5 / D_MODEL)
        else:
            xo_ref[...] = x_next

    tile = (SUBLANES, LANES)
    return pl.pallas_call(
        body, name=f"forward_layer_{layer}", grid=(n_tiles,),
        in_specs=[row(D_MODEL), _layer_spec(vec.shape, layer), _layer_spec(cps.shape, layer),
                  _layer_spec(wpool.shape, layer), _whole_spec(win.shape), _whole_spec(wout.shape)]
        + [row(D_MODEL)] * head,
        out_specs=[row(D_MODEL), row(D_MODEL), col_t, col_t] + [row(w) for w in widths[1:]] + [_whole_spec(tile)] * head,
        out_shape=[jax.ShapeDtypeStruct((t_len, D_MODEL), F32), jax.ShapeDtypeStruct((t_len, D_MODEL), BF16),
                   jax.ShapeDtypeStruct((D_MODEL, t_len), BF16), jax.ShapeDtypeStruct((D_MODEL, t_len), BF16)]
        + [jax.ShapeDtypeStruct((t_len, w), BF16) for w in widths[1:]] + [jax.ShapeDtypeStruct(tile, F32)] * head,
        scratch_shapes=[pltpu.VMEM((CONV_HALO, CONV_W), F32), pltpu.VMEM((POOL_HALO, POOL_W), F32)],
        compiler_params=_params(dimension_semantics=("arbitrary",)),
    )(x, vec, cps, wpool, win, wout, *([target] * head))


def _backward_layer(layer, dxo, y, x, uc, fa, fp, vec, cps, wpool, wout, win, after=()):
    t_len = dxo.shape[0]
    n_tiles = t_len // BWD_TILE
    halo_per_tile = BWD_TILE // POOL_HALO
    rev = lambda cols: pl.BlockSpec((BWD_TILE, cols), lambda i: (n_tiles - 1 - i, 0))
    halo_spec = pl.BlockSpec(
        (POOL_HALO, 2 * CONV_W), lambda i: (jnp.maximum((n_tiles - 1 - i) * halo_per_tile - 1, 0), 0))
    gwpool_shape = (len(POOL_WINDOWS), GROUP_D, GROUP_D)

    def body(dxo_ref, y_ref, x_ref, uc_ref, uch_ref, fa_ref, fp_ref, vec_ref, cps_ref, wpool_ref, wout_ref, win_ref,
             *rest):
        dx_ref, dproj_ref, dy_ref, gwpool_ref, dcps_ref, dvec_ref, gwpool_acc, dcc_ref, qc_ref = rest[len(after):]
        i = pl.program_id(0)
        tile = n_tiles - 1 - i

        @pl.when(i == 0)
        def _():
            gwpool_acc[...] = jnp.zeros_like(gwpool_acc)
            dcps_ref[...] = jnp.zeros_like(dcps_ref)
            dvec_ref[...] = jnp.zeros_like(dvec_ref)
            dcc_ref[...] = jnp.zeros_like(dcc_ref)
            qc_ref[...] = jnp.zeros_like(qc_ref)

        shift, scale, gate = vec_ref[0:1, :], vec_ref[1:2, :], vec_ref[2:3, :]
        g_pre, g_post = vec_ref[3:4, :], vec_ref[4:5, :]
        w0, w1, w2 = cps_ref[0:1, :], cps_ref[1:2, :], cps_ref[2:3, :]

        dxo_t = dxo_ref[...]
        y_t = y_ref[...].astype(F32)
        ry = lax.rsqrt(jnp.mean(y_t * y_t, axis=-1, keepdims=True) + NORM_EPS)
        yh = y_t * ry
        dvec_ref[2:3, :] += jnp.sum(dxo_t * yh, axis=0, keepdims=True)
        dyh = dxo_t * (gate * g_post)
        dy_b = (ry * (dyh - yh * jnp.mean(dyh * yh, axis=-1, keepdims=True))).astype(BF16)
        dy_ref[...] = dy_b
        dycat = _dot_nt(dy_b, wout_ref[...])
        dy_a, dy_p = dycat[:, :CONV_W], dycat[:, CONV_W:]

        fa_t = fa_ref[...].astype(F32)
        db_a = dy_a * fa_t[:, :CONV_W]
        dconv = dy_a * fa_t[:, CONV_W:2 * CONV_W]
        dg_a = dy_a * fa_t[:, 2 * CONV_W:]
        uc_t = uc_ref[...].astype(F32)
        u_a, c_a = uc_t[:, :CONV_W], uc_t[:, CONV_W:]
        halo = jnp.where(tile > 0, uch_ref[...].astype(F32), 0.0)[POOL_HALO - CONV_HALO:]
        z = c_a * u_a
        zcat = jnp.concatenate([halo[:, CONV_W:] * halo[:, :CONV_W], z], axis=0)
        z1 = _rows_from_before(zcat, 1)[CONV_HALO:]
        z2 = _rows_from_before(zcat, 2)[CONV_HALO:]
        dccat = jnp.concatenate([dconv, dcc_ref[...]], axis=0)
        dc1 = _rows_from_after(dccat, 1)[:BWD_TILE]
        dc2 = _rows_from_after(dccat, 2)[:BWD_TILE]
        dz = w2 * dconv + w1 * dc1 + w0 * dc2
        dcc_ref[...] = dconv[:CONV_HALO]
        dcps_ref[0:1, :] += jnp.sum(dconv * z2, axis=0, keepdims=True)
        dcps_ref[1:2, :] += jnp.sum(dconv * z1, axis=0, keepdims=True)
        dcps_ref[2:3, :] += jnp.sum(dconv * z, axis=0, keepdims=True)
        du_a = dz * c_a
        dc_a = dz * u_a

        dmixed = (dy_p * fp_ref[:, :POOL_W].astype(F32)).astype(BF16)
        dg_p = dy_p * fp_ref[:, POOL_W:2 * POOL_W].astype(F32)
        dcps_ref[3:4, :] += jnp.sum(dy_p * fp_ref[:, 2 * POOL_W:3 * POOL_W].astype(F32), axis=0, keepdims=True)
        counts = _window_counts(tile * BWD_TILE, BWD_TILE)
        du_p, q_head = [], []
        for g, w in enumerate(POOL_WINDOWS):
            cols = slice(g * GROUP_D, (g + 1) * GROUP_D)
            dm_g = dmixed[:, cols]
            dpooled_g = _dot_nt(dm_g, wpool_ref[g])
            gwpool_acc[g] += _dot_tn(fp_ref[:, 3 * POOL_W + g * GROUP_D:3 * POOL_W + (g + 1) * GROUP_D], dm_g)
            q_g = dpooled_g * (1.0 / jnp.minimum(counts, float(w)))
            q_head.append(q_g[:POOL_HALO])
            s = jnp.concatenate([q_g, qc_ref[:, cols]], axis=0)
            step = 1
            while step < w:
                s = s + _rows_from_after(s, step)
                step *= 2
            du_p.append(s[:BWD_TILE] - dpooled_g)
        qc_ref[...] = jnp.concatenate(q_head, axis=1)
        dproj_b = jnp.concatenate([du_a, db_a, dc_a, dg_a] + du_p + [dg_p], axis=1).astype(BF16)
        dproj_ref[...] = dproj_b

        x_t = x_ref[...]
        rx = lax.rsqrt(jnp.mean(x_t * x_t, axis=-1, keepdims=True) + NORM_EPS)
        xn = x_t * rx
        mod_scale = 1.0 + scale
        dh = _dot_nt(dproj_b, win_ref[...])
        dvec_ref[0:1, :] += jnp.sum(dh, axis=0, keepdims=True)
        dvec_ref[1:2, :] += jnp.sum(dh * xn, axis=0, keepdims=True)
        dxn = dh * (g_pre * mod_scale)
        dx_ref[...] = dxo_t + rx * (dxn - xn * jnp.mean(dxn * xn, axis=-1, keepdims=True))

        @pl.when(i == n_tiles - 1)
        def _():
            gwpool_ref[...] = gwpool_acc[...].astype(BF16)
            sum_dh_xn, sum_dxo_yh = dvec_ref[1:2, :], dvec_ref[2:3, :]
            dvec_ref[1:2, :] = sum_dh_xn * g_pre
            dvec_ref[3:4, :] = sum_dh_xn * mod_scale
            dvec_ref[2:3, :] = sum_dxo_yh * g_post
            dvec_ref[4:5, :] = sum_dxo_yh * gate

    return pl.pallas_call(
        body, name=f"backward_layer_{layer}", grid=(n_tiles,),
        in_specs=[rev(D_MODEL), rev(D_MODEL), rev(D_MODEL), rev(2 * CONV_W), halo_spec, rev(3 * CONV_W),
                  rev(4 * POOL_W), _layer_spec(vec.shape, layer), _layer_spec(cps.shape, layer),
                  _layer_spec(wpool.shape, layer), _whole_spec(wout.shape), _whole_spec(win.shape)]
        + [HBM] * len(after),
        out_specs=[rev(D_MODEL), rev(IN_COLS), rev(D_MODEL), _whole_spec(gwpool_shape),
                   _whole_spec((SUBLANES, CONV_W)), _whole_spec((SUBLANES, D_MODEL))],
        out_shape=[jax.ShapeDtypeStruct((t_len, D_MODEL), F32), jax.ShapeDtypeStruct((t_len, IN_COLS), BF16),
                   jax.ShapeDtypeStruct((t_len, D_MODEL), BF16), jax.ShapeDtypeStruct(gwpool_shape, BF16),
                   jax.ShapeDtypeStruct((SUBLANES, CONV_W), F32), jax.ShapeDtypeStruct((SUBLANES, D_MODEL), F32)],
        scratch_shapes=[pltpu.VMEM(gwpool_shape, F32), pltpu.VMEM((CONV_HALO, CONV_W), F32),
                        pltpu.VMEM((POOL_HALO, POOL_W), F32)],
        compiler_params=_params(dimension_semantics=("arbitrary",)),
    )(dxo, y, x, uc, uc, fa, fp, vec, cps, wpool, wout, win, *after)


def _weight_grads(layer, h_t, dproj, ycat_t, dy, exchange, after=()):
    t_len = dy.shape[0]
    n_in, n_out = IN_COLS // GWIN_COLS, D_MODEL // GWOUT_COLS
    args = [h_t, dproj, ycat_t, dy, *after]
    in_specs = [_whole_spec(h_t.shape),
                pl.BlockSpec((t_len, GWIN_COLS), lambda s: (0, jnp.minimum(s, n_in - 1))),
                _whole_spec(ycat_t.shape),
                pl.BlockSpec((t_len, GWOUT_COLS), lambda s: (0, jnp.maximum(s - n_in, 0)))] + [HBM] * len(after)
    out_shape = [jax.ShapeDtypeStruct((D_MODEL, IN_COLS), BF16), jax.ShapeDtypeStruct((D_MODEL, D_MODEL), BF16)]
    out_specs = [pl.BlockSpec((D_MODEL, GWIN_COLS), lambda s: (0, jnp.minimum(s, n_in - 1))),
                 pl.BlockSpec((D_MODEL, GWOUT_COLS), lambda s: (0, jnp.maximum(s - n_in, 0)))]
    scratch = []
    aliases, split = _host(exchange, args, in_specs, out_shape, out_specs, scratch)

    def body(*refs):
        (ht_ref, dproj_ref, ycatt_ref, dy_ref, *_, gwin_ref, gwout_ref), hosted = split(refs)
        s = pl.program_id(0)
        if hosted is not None:
            pl.when(s == 0)(hosted[0])

        @pl.when(s < n_in)
        def _():
            gwin_ref[...] = _dot(ht_ref[...], dproj_ref[...]).astype(BF16)

        @pl.when(s >= n_in)
        def _():
            gwout_ref[...] = _dot(ycatt_ref[...], dy_ref[...]).astype(BF16)

        if hosted is not None:
            pl.when(s == n_in + n_out - 1)(hosted[1])

    return pl.pallas_call(
        body, name=f"weight_grads_{layer}", grid=(n_in + n_out,), in_specs=in_specs, out_specs=out_specs,
        out_shape=out_shape, scratch_shapes=scratch, input_output_aliases=aliases,
        compiler_params=_params(dimension_semantics=("arbitrary",)),
    )(*args)


def _add_sibling_blocks(name, layer, grads, received, core, partials, kinds):
    n_arr = len(grads)

    def body(core_ref, *refs):
        mine, theirs, outs = refs[:n_arr], refs[n_arr:2 * n_arr], refs[-n_arr:]
        for a in range(n_arr):
            outs[a][...] = (mine[a][...].astype(F32) + theirs[a][...].astype(F32)).astype(BF16)

    own_of_kind = [
        pl.BlockSpec((D_MODEL, W_IN_SHARD), lambda q, core_ref: (0, 2 * q + core_ref[0])),
        pl.BlockSpec((W_OUT_SHARD, D_MODEL), lambda q, core_ref: (2 * q + core_ref[0], 0)),
        pl.BlockSpec((POOL_SHARD, GROUP_D), lambda q, core_ref: (2 * q + core_ref[0], 0)),
    ]
    shapes = [_BLOCK_SHAPES[k] for k in kinds]
    recv_specs = [pl.BlockSpec((None,) + s, lambda q, core_ref: (q, 0, 0)) for s in shapes]
    out_specs = [pl.BlockSpec((None, None) + s, lambda q, core_ref: (q, layer, 0, 0)) for s in shapes]
    args = [core, *grads, *received]
    in_specs = [own_of_kind[k] for k in kinds] + recv_specs
    aliases = {}
    if partials is not None:
        aliases = {len(args) + a: a for a in range(n_arr)}
        args += list(partials)
        in_specs += [HBM] * n_arr
    return pl.pallas_call(
        body, name=name,
        grid_spec=pltpu.PrefetchScalarGridSpec(
            num_scalar_prefetch=1, grid=(N_CHIP,), in_specs=in_specs, out_specs=out_specs),
        out_shape=[jax.ShapeDtypeStruct((N_CHIP, DEPTH) + s, BF16) for s in shapes],
        input_output_aliases=aliases,
        compiler_params=_params(dimension_semantics=("arbitrary",)),
    )(*args)


def _modulation_columns(c_all, w_ada):
    def body(c_ref, w_ref, cact_ref, out_ref):
        c_t = c_ref[...]
        c_act = c_t * _sigmoid(c_t)
        cact_ref[...] = c_act
        out_ref[...] = jnp.dot(c_act, w_ref[...], preferred_element_type=F32, precision=lax.Precision.HIGHEST)

    return pl.pallas_call(
        body, name="modulation_columns", grid=(DEPTH,),
        in_specs=[pl.BlockSpec((N_DEV, D_MODEL), lambda l: (0, 0)),
                  pl.BlockSpec((None, D_MODEL, W_IN_SHARD), lambda l: (l, 0, 0))],
        out_specs=[pl.BlockSpec((N_DEV, D_MODEL), lambda l: (0, 0)),
                   pl.BlockSpec((N_DEV, W_IN_SHARD), lambda l: (0, l))],
        out_shape=[jax.ShapeDtypeStruct((N_DEV, D_MODEL), F32),
                   jax.ShapeDtypeStruct((N_DEV, DEPTH * W_IN_SHARD), F32)],
        compiler_params=_params(dimension_semantics=("arbitrary",)),
    )(c_all, w_ada)


def _adamw(w, g, m, v):
    m_new = ADAM_B1 * m + (1.0 - ADAM_B1) * g
    v_new = ADAM_B2 * v + (1.0 - ADAM_B2) * (g * g)
    m_hat = m_new / (1.0 - ADAM_B1 ** ADAM_STEP)
    v_hat = v_new / (1.0 - ADAM_B2 ** ADAM_STEP)
    delta = -ADAM_LR * (m_hat / (jnp.sqrt(v_hat) + ADAM_EPS) + ADAM_WD * w)
    return delta, m_new, v_new


def _adamw_w_ada(w, m, v, c_act_t, dmod_cols):
    def body(w_ref, m_ref, v_ref, ct_ref, dm_ref, g_ref, d_ref, mo_ref, vo_ref):
        g = ct_ref[:, 0:1] * dm_ref[0:1, :]
        for b in range(1, N_DEV):
            g = g + ct_ref[:, b:b + 1] * dm_ref[b:b + 1, :]
        g_ref[...] = g
        d_ref[...], mo_ref[...], vo_ref[...] = _adamw(w_ref[...], g, m_ref[...], v_ref[...])

    big = pl.BlockSpec((None, D_MODEL, W_IN_SHARD), lambda l: (l, 0, 0))
    return pl.pallas_call(
        body, name="adamw_w_ada", grid=(DEPTH,),
        in_specs=[big, big, big, pl.BlockSpec((D_MODEL, N_DEV), lambda l: (0, 0)),
                  pl.BlockSpec((None, N_DEV, W_IN_SHARD), lambda l: (l, 0, 0))],
        out_specs=[big] * 4, out_shape=[jax.ShapeDtypeStruct(w.shape, F32)] * 4,
        compiler_params=_params(dimension_semantics=("arbitrary",)),
    )(w, m, v, c_act_t, dmod_cols)


def _sum_chip_partials(own_ref, recv_ref):
    g = own_ref[...].astype(F32)
    for j in range(N_OTHER_CHIPS):
        g = g + recv_ref[j].astype(F32)
    return g


def _partial_specs(row_tile, cols, first_layer=0):
    own = pl.BlockSpec((None, None, row_tile, cols), lambda l, r, chip_ref: (chip_ref[0], first_layer + l, r, 0))
    recv = pl.BlockSpec((N_OTHER_CHIPS, None, row_tile, cols), lambda l, r, chip_ref: (0, first_layer + l, r, 0))
    return own, recv


def _adamw_reduced(name, w, m, v, partial, received, chip, row_tile, layers, continued):
    depth, rows, cols = w.shape
    first, stop = layers

    def body(chip_ref, w_ref, m_ref, v_ref, own_ref, recv_ref, *rest):
        g_ref, d_ref, mo_ref, vo_ref = rest[-4:]
        g = _sum_chip_partials(own_ref, recv_ref)
        g_ref[...] = g
        d_ref[...], mo_ref[...], vo_ref[...] = _adamw(w_ref[...], g, m_ref[...], v_ref[...])

    blk = pl.BlockSpec((None, row_tile, cols), lambda l, r, chip_ref: (first + l, r, 0))
    args = [chip, w, m, v, partial, received]
    in_specs = [blk, blk, blk, *_partial_specs(row_tile, cols, first)]
    aliases = {}
    if continued is not None:
        aliases = {len(args) + k: k for k in range(4)}
        args += list(continued)
        in_specs += [HBM] * 4
    return pl.pallas_call(
        body, name=name,
        grid_spec=pltpu.PrefetchScalarGridSpec(
            num_scalar_prefetch=1, grid=(stop - first, rows // row_tile), in_specs=in_specs, out_specs=[blk] * 4),
        out_shape=[jax.ShapeDtypeStruct(w.shape, F32)] * 4, input_output_aliases=aliases,
        compiler_params=_params(dimension_semantics=("arbitrary", "arbitrary")),
    )(*args)


def _reduce_w_pool(partial, received, chip):
    def body(chip_ref, own_ref, recv_ref, g_ref):
        g_ref[...] = _sum_chip_partials(own_ref, recv_ref)

    return pl.pallas_call(
        body, name="reduce_w_pool",
        grid_spec=pltpu.PrefetchScalarGridSpec(
            num_scalar_prefetch=1, grid=(DEPTH, 1), in_specs=list(_partial_specs(POOL_SHARD, GROUP_D)),
            out_specs=pl.BlockSpec((POOL_SHARD, GROUP_D), lambda l, r, chip_ref: (l, 0))),
        out_shape=jax.ShapeDtypeStruct((DEPTH * POOL_SHARD, GROUP_D), F32),
        compiler_params=_params(dimension_semantics=("arbitrary", "arbitrary")),
    )(chip, partial, received)


def _adamw_small(params):
    n = len(params)

    def body(*refs):
        ins, outs = refs[:4 * n], refs[4 * n:]
        for p in range(n):
            w_ref, g_ref, m_ref, v_ref = ins[4 * p:4 * p + 4]
            d_ref, mo_ref, vo_ref = outs[3 * p:3 * p + 3]
            d_ref[...], mo_ref[...], vo_ref[...] = _adamw(w_ref[...], g_ref[...], m_ref[...], v_ref[...])

    vmem = pl.BlockSpec(memory_space=pltpu.VMEM)
    flat = [a for group in params for a in group]
    out_shape = [jax.ShapeDtypeStruct(group[0].shape, F32) for group in params for _ in range(3)]
    outs = pl.pallas_call(
        body, name="adamw_small", in_specs=[vmem] * len(flat), out_specs=[vmem] * len(out_shape),
        out_shape=out_shape, compiler_params=_params(),
    )(*flat)
    return [tuple(outs[3 * p:3 * p + 3]) for p in range(n)]


def _sum_sources(slabs):
    def body(s_ref, o_ref):
        acc = s_ref[0]
        for b in range(1, N_DEV):
            acc = acc + s_ref[b]
        o_ref[...] = acc

    vmem = pl.BlockSpec(memory_space=pltpu.VMEM)
    return pl.pallas_call(
        body, name="sum_small_grads", in_specs=[vmem], out_specs=vmem,
        out_shape=jax.ShapeDtypeStruct(slabs.shape[1:], F32), compiler_params=_params(),
    )(slabs)


def _to_bf16(a, name, layers=None):
    first, stop = layers or (0, a.shape[0])

    def body(a_ref, o_ref):
        o_ref[...] = a_ref[...].astype(BF16)

    block = (None,) + a.shape[1:]
    return pl.pallas_call(
        body, name=name, grid=(stop - first,), in_specs=[pl.BlockSpec(block, lambda l: (first + l, 0, 0))],
        out_specs=pl.BlockSpec(block, lambda l: (l, 0, 0)),
        out_shape=jax.ShapeDtypeStruct((stop - first,) + a.shape[1:], BF16),
        compiler_params=_params(dimension_semantics=("arbitrary",)),
    )(a)


def kernel(x, c, w_ada, b_ada, g_pre, w_in, w_conv, w_pool, pool_scale, w_out, g_post, loss_target, m_w_ada, m_b_ada, m_g_pre, m_w_in, m_w_conv, m_w_pool, m_pool_scale, m_w_out, m_g_post, v_w_ada, v_b_ada, v_g_pre, v_w_in, v_w_conv, v_w_pool, v_pool_scale, v_w_out, v_g_post):
    mx, my, mc = _mesh_position()
    me = _block_id(mx, my, mc)
    chip = (2 * mx + my).astype(jnp.int32).reshape(1)
    core = mc.astype(jnp.int32).reshape(1)
    x0 = x[0]
    target = loss_target[0]
    conv_shard = w_conv.shape[-1]

    own_small = jnp.concatenate([c, w_conv.reshape(1, DEPTH * 3 * conv_shard)], axis=1)
    all_small = _all_gather_small(own_small, "all_gather_c_w_conv")[:, 0, :]
    gathers = [_gather_weights_start(
        0, _to_bf16(w_in, "cast_w_in_0", (0, 1)), _to_bf16(w_out, "cast_w_out_0", (0, 1)), [all_small])]
    c_all = all_small[:, :D_MODEL]
    w_conv_full = all_small[:, D_MODEL:].reshape(N_DEV, DEPTH, 3, conv_shard).transpose(1, 2, 0, 3).reshape(
        DEPTH, 3, CONV_W)
    cps = jnp.concatenate([w_conv_full, pool_scale[:, None], jnp.zeros((DEPTH, 4, CONV_W), F32)], axis=1)

    c_act, pieces = _modulation_columns(c_all, w_ada)
    upper = (1, DEPTH)
    upper_shards = [_to_bf16(w_in, "cast_w_in_1", upper), _to_bf16(w_out, "cast_w_out_1", upper)]
    wpool_b = _to_bf16(w_pool.reshape(DEPTH, POOL_ROWS, GROUP_D), "cast_w_pool").reshape(w_pool.shape)
    mod_all = _all_gather_small(
        pieces, "all_gather_modulation", [gathers[0][1][0], *upper_shards, wpool_b])
    mod_mine = lax.dynamic_index_in_dim(mod_all, me, axis=1, keepdims=False)
    mod = mod_mine.reshape(N_DEV, DEPTH, W_IN_SHARD).transpose(1, 0, 2).reshape(DEPTH, 3 * D_MODEL) + b_ada
    zeros_d = jnp.zeros((DEPTH, 3, D_MODEL), F32)
    vec = jnp.concatenate([mod.reshape(DEPTH, 3, D_MODEL), g_pre[:, None], g_post[:, None], zeros_d], axis=1)

    gathers.append(_gather_weights_start(1, *upper_shards, [mod_all]))
    gathers = [(first_sems, shards, landing[k], k) for first_sems, shards, landing in gathers
               for k in range(len(landing))]

    xs, kept, wins, wouts = [x0], [], [], []
    for l in range(DEPTH):
        first_sems, shards, zones, index = gathers[l]
        passed_sems, zones = _gather_weights_pass_on(
            l, index, first_sems[1], zones, [vec, cps, wpool_b, gathers[-1][1][0]] if l == 0 else [xs[-1]])
        win, wout = _gather_weights_finish(l, index, first_sems, passed_sems, shards, zones)
        x_next, *for_backward = _forward_layer(
            l, xs[-1], vec, cps, wpool_b, win, wout, target if l == DEPTH - 1 else None)
        xs.append(x_next)
        kept.append(for_backward[:6])
        wins.append(win)
        wouts.append(wout)
    dx, loss_tile = xs[DEPTH], for_backward[6]

    slab_rows = [None] * DEPTH
    partials = received = None
    in_flight = []

    def scatter(layer, grads, from_sibling, after):
        nonlocal partials, received
        partials = _add_sibling_blocks(
            f"grad_add_sibling_{layer}", layer, grads, from_sibling, core, partials, ALL_KINDS)
        chips = _chips_exchange(layer, partials, received, ALL_KINDS)
        sems, partials, received, token = _start_exchange(chips, f"grad_chips_start_{layer}", after)
        in_flight.append((chips, sems, layer))
        return token

    grads_above = early = None
    for l in reversed(range(DEPTH)):
        y, h_t, ycat_t, uc, fa, fp = kept[l]
        dx, dproj, dy, gwpool, dcps, dvec = _backward_layer(
            l, dx, y, xs[l], uc, fa, fp, vec, cps, wpool_b, wouts[l], wins[l], [early[4]] if early else [])
        if early:
            exchange, sems, grads, landing, _ = early
            _, from_sibling = _finish_exchange(exchange, "grad_exchange_sibling_1_finish", sems, grads, landing, [dx])
            scatter(1, grads, from_sibling, [])
            early = None
        slab_rows[l] = jnp.concatenate(
            [dvec[0], dvec[1], dvec[2], dvec[3], dvec[4], dcps[3], dcps[0], dcps[1], dcps[2],
             loss_tile[0] if l == 0 else jnp.zeros((LANES,), F32)])
        after = []
        if l == 0:
            gather_small = _all_gather_exchange(jnp.stack(slab_rows))
            sems_s, slab, slabs, token = _start_exchange(gather_small, "all_gather_small_grads_start")
            after = [token]
        hosted = _sibling_exchange(grads_above, ALL_KINDS) if grads_above is not None else None
        gwin, gwout, *from_sibling = _weight_grads(l, h_t, dproj, ycat_t, dy, hosted, after)
        if l == 0:
            _, (slabs,) = _finish_exchange(
                gather_small, "all_gather_small_grads_finish", sems_s, slab, slabs, [gwin])
        if grads_above is not None:
            scatter(l + 1, grads_above, from_sibling, [])
        grads_above = [gwin, gwout, gwpool.reshape(POOL_ROWS, GROUP_D)]
        if l == 1:
            exchange = _sibling_exchange(grads_above, ALL_KINDS)
            early = (exchange, *_start_exchange(exchange, "grad_exchange_sibling_1_start"))
            grads_above = None
        if l == 0:
            from_sibling = _run_exchange(_sibling_exchange(grads_above, ALL_KINDS), "grad_exchange_sibling_0")
            token = scatter(0, grads_above, from_sibling, [slabs])
            grads_above = None
    grad_x = dx[None]
    chips_0, sems_0, _ = in_flight.pop()

    total = _sum_sources(slabs)
    loss = total[0, SLAB_COLS]
    o = 3 * D_MODEL
    g_b_ada = total[:, :o]
    g_g_pre = total[:, o:o + D_MODEL]
    g_g_post = total[:, o + D_MODEL:o + 2 * D_MODEL]
    g_pool_scale = total[:, o + 2 * D_MODEL:o + 2 * D_MODEL + POOL_W]
    g_conv_full = total[:, o + 2 * D_MODEL + POOL_W:SLAB_COLS].reshape(DEPTH, 3, CONV_W)
    g_w_conv = lax.dynamic_slice_in_dim(g_conv_full, me * conv_shard, conv_shard, axis=2)

    after = [token]
    for chips, sems, l in in_flight:
        partials, received = _finish_exchange(chips, f"grad_chips_finish_{l}", sems, partials, received, after)
        after = []
    upper = (1, DEPTH)
    w_in_upper = _adamw_reduced(
        "adamw_w_in_upper", w_in, m_w_in, v_w_in, partials[0], received[0], chip, ROW_TILE, upper, None)
    w_out_upper = _adamw_reduced(
        "adamw_w_out_upper", w_out, m_w_out, v_w_out, partials[1], received[1], chip, W_OUT_SHARD, upper, None)
    dmod_all = slabs[:, :, :o].reshape(N_DEV, DEPTH, N_DEV, W_IN_SHARD)
    dmod_cols = lax.dynamic_index_in_dim(dmod_all, me, axis=2, keepdims=False).transpose(1, 0, 2) + token[0, 0]
    g_w_ada, d_w_ada, nm_w_ada, nv_w_ada = _adamw_w_ada(w_ada, m_w_ada, v_w_ada, c_act.T, dmod_cols)

    partials, received = _finish_exchange(
        chips_0, "grad_chips_finish_0", sems_0, partials, received, [nv_w_ada, w_in_upper[3], w_out_upper[3]])
    gather_pool = _all_gather_exchange(_reduce_w_pool(partials[2], received[2], chip))
    sems_p, pool_rows, pool_landing, token_p = _start_exchange(gather_pool, "all_gather_grad_w_pool_start")
    g_w_in, d_w_in, nm_w_in, nv_w_in = _adamw_reduced(
        "adamw_w_in_0", w_in, m_w_in, v_w_in, partials[0], received[0], chip, ROW_TILE, (0, 1), w_in_upper)
    g_w_out, d_w_out, nm_w_out, nv_w_out = _adamw_reduced(
        "adamw_w_out_0", w_out, m_w_out, v_w_out, partials[1], received[1], chip, W_OUT_SHARD, (0, 1), w_out_upper)
    _, (g_pool_all,) = _finish_exchange(
        gather_pool, "all_gather_grad_w_pool_finish", sems_p, pool_rows, pool_landing, [nv_w_in, nv_w_out])
    g_w_pool = g_pool_all.reshape(N_DEV, DEPTH, POOL_SHARD, GROUP_D).transpose(1, 0, 2, 3).reshape(w_pool.shape)

    flat2 = lambda a: a.reshape(-1, a.shape[-1])
    small = _adamw_small([
        (b_ada, g_b_ada, m_b_ada, v_b_ada),
        (g_pre, g_g_pre, m_g_pre, v_g_pre),
        (flat2(w_conv), flat2(g_w_conv), flat2(m_w_conv), flat2(v_w_conv)),
        (flat2(w_pool), flat2(g_w_pool), flat2(m_w_pool), flat2(v_w_pool)),
        (pool_scale, g_pool_scale, m_pool_scale, v_pool_scale),
        (g_post, g_g_post, m_g_post, v_g_post),
    ])
    (d_b_ada, nm_b_ada, nv_b_ada), (d_g_pre, nm_g_pre, nv_g_pre), conv_upd, pool_upd, \
        (d_ps, nm_ps, nv_ps), (d_g_post, nm_g_post, nv_g_post) = small
    d_w_conv, nm_w_conv, nv_w_conv = (a.reshape(w_conv.shape) for a in conv_upd)
    d_w_pool, nm_w_pool, nv_w_pool = (a.reshape(w_pool.shape) for a in pool_upd)

    return (loss, grad_x,
            g_w_ada, g_b_ada, g_g_pre, g_w_in, g_w_conv, g_w_pool, g_pool_scale, g_w_out, g_g_post,
            d_w_ada, d_b_ada, d_g_pre, d_w_in, d_w_conv, d_w_pool, d_ps, d_w_out, d_g_post,
            nm_w_ada, nm_b_ada, nm_g_pre, nm_w_in, nm_w_conv, nm_w_pool, nm_ps, nm_w_out, nm_g_post,
            nv_w_ada, nv_b_ada, nv_g_pre, nv_w_in, nv_w_conv, nv_w_pool, nv_ps, nv_w_out, nv_g_post)
```

```python
import jax
import jax.numpy as jnp
from jax import lax
from jax.experimental import pallas as pl
from jax.experimental.pallas import tpu as pltpu

F32 = jnp.float32
BF16 = jnp.bfloat16

D_MODEL = 1024
DEPTH = 4
CONV_W = 512
POOL_W = 512
POOL_WINDOWS = (2, 4, 8, 16)
GROUP_D = 128
IN_COLS = 4 * CONV_W + 2 * POOL_W
NORM_EPS = 1e-6

ADAM_LR = 0.001
ADAM_B1 = 0.9
ADAM_B2 = 0.999
ADAM_EPS = 1e-08
ADAM_WD = 0.01
ADAM_STEP = 10

N_DEV = 8
N_CHIP = 4
N_OTHER_CHIPS = N_CHIP - 1
MESH = pl.DeviceIdType.MESH
W_IN_SHARD = IN_COLS // N_DEV
W_OUT_SHARD = D_MODEL // N_DEV
POOL_ROWS = len(POOL_WINDOWS) * GROUP_D
POOL_SHARD = POOL_ROWS // N_DEV

SUBLANES = 8
LANES = 128
VMEM_LIMIT_BYTES = 56 * 1024 * 1024
ROW_TILE = 512
BWD_TILE = 256
GWIN_COLS = 768
GWOUT_COLS = 512
POOL_HALO = 16
CONV_HALO = SUBLANES

SLAB_COLS = 3 * D_MODEL + D_MODEL + D_MODEL + POOL_W + 3 * CONV_W

HBM = pl.BlockSpec(memory_space=pl.ANY)


def _params(**kw):
    return pltpu.CompilerParams(vmem_limit_bytes=VMEM_LIMIT_BYTES, **kw)


def _sigmoid(v):
    return 1.0 / (1.0 + jnp.exp(-v))


def _dot(a, b):
    return jnp.dot(a, b, preferred_element_type=F32)


def _dot_tn(a, b):
    return lax.dot_general(a, b, (((0,), (0,)), ((), ())), preferred_element_type=F32)


def _dot_nt(a, b):
    return lax.dot_general(a, b, (((1,), (1,)), ((), ())), preferred_element_type=F32)


def _rows_from_before(v, k):
    return pltpu.roll(v, k, 0)


def _rows_from_after(v, k):
    return pltpu.roll(v, v.shape[0] - k, 0)


def _window_counts(t0, rows):
    return (lax.broadcasted_iota(jnp.int32, (rows, 1), 0) + (t0 + 1)).astype(F32)


def _split_proj(p32):
    cw = CONV_W
    return (p32[:, 0 * cw:1 * cw], p32[:, 1 * cw:2 * cw], p32[:, 2 * cw:3 * cw], p32[:, 3 * cw:4 * cw],
            p32[:, 4 * cw:4 * cw + POOL_W], p32[:, 4 * cw + POOL_W:])


def _layer_spec(shape, layer):
    nd = len(shape)
    return pl.BlockSpec((None,) + tuple(shape[1:]), lambda i, _l=layer, _n=nd: (_l,) + (0,) * (_n - 1))


def _whole_spec(shape):
    return pl.BlockSpec(tuple(shape), lambda i, _n=len(shape): (0,) * _n, pipeline_mode=pl.Buffered(1))


def _mesh_position():
    return lax.axis_index("x"), lax.axis_index("y"), lax.axis_index("c")


def _block_id(x, y, c):
    return 4 * x + 2 * y + c


def _other_chips(x, y):
    return [(x ^ 1, y), (x, y ^ 1), (x ^ 1, y ^ 1)]


def _col_block(ref, blk):
    return ref.at[:, pl.ds(pl.multiple_of(blk * W_IN_SHARD, LANES), W_IN_SHARD)]


def _row_block(rows):
    def block(ref, blk):
        return ref.at[pl.ds(pl.multiple_of(blk * rows, rows), rows), :]
    return block


_BLOCK_OF = (_col_block, _row_block(W_OUT_SHARD), _row_block(POOL_SHARD))
_BLOCK_SHAPES = ((D_MODEL, W_IN_SHARD), (W_OUT_SHARD, D_MODEL), (POOL_SHARD, GROUP_D))


class _Exchange:
    def __init__(self, inputs, out_shapes, aliases, sem_shapes, make):
        self.inputs, self.out_shapes, self.aliases, self.sem_shapes, self.make = (
            list(inputs), list(out_shapes), dict(aliases), list(sem_shapes), make)


def _run_exchange(exchange, name):
    n_in, n_out = len(exchange.inputs), len(exchange.out_shapes)

    def body(*refs):
        start, finish = exchange.make(refs[:n_in], refs[n_in:n_in + n_out], refs[n_in + n_out:])
        start()
        finish()

    return pl.pallas_call(
        body, name=name, in_specs=[HBM] * n_in, out_specs=[HBM] * n_out, out_shape=exchange.out_shapes,
        scratch_shapes=exchange.sem_shapes, input_output_aliases=exchange.aliases, compiler_params=_params(),
    )(*exchange.inputs)


_SEM = pl.BlockSpec(memory_space=pltpu.SEMAPHORE)
_DATAFLOW = pltpu.SideEffectType.DATAFLOW_SIDE_EFFECTING


def _start_exchange(exchange, name, after=()):
    n_in, n_out, n_sem = len(exchange.inputs), len(exchange.out_shapes), len(exchange.sem_shapes)
    sources = [i for i in range(n_in) if i not in exchange.aliases]
    aliases = {i: n_sem + k for k, i in enumerate(sources)}
    aliases.update({i: n_sem + len(sources) + o for i, o in exchange.aliases.items()})

    def body(*refs):
        in_refs = refs[:n_in]
        outs = refs[n_in + len(after):]
        sems = outs[:n_sem]
        out_refs = outs[n_sem + len(sources):n_sem + len(sources) + n_out]
        exchange.make(in_refs, out_refs, sems)[0]()
        refs[-1][...] = jnp.zeros_like(refs[-1])

    outs = pl.pallas_call(
        body, name=name, in_specs=[HBM] * (n_in + len(after)),
        out_specs=[_SEM] * n_sem + [HBM] * (len(sources) + n_out) + [pl.BlockSpec(memory_space=pltpu.VMEM)],
        out_shape=(exchange.sem_shapes + [pltpu.HBM(exchange.inputs[i].shape, exchange.inputs[i].dtype) for i in sources]
                   + [pltpu.HBM(s.shape, s.dtype) for s in exchange.out_shapes]
                   + [jax.ShapeDtypeStruct((SUBLANES, LANES), F32)]),
        input_output_aliases=aliases, compiler_params=_params(has_side_effects=_DATAFLOW),
    )(*exchange.inputs, *after)
    return outs[:n_sem], outs[n_sem:n_sem + len(sources)], outs[n_sem + len(sources):-1], outs[-1]


def _finish_exchange(exchange, name, sems, sources, landing, after):
    n_src, n_out, n_sem = len(sources), len(landing), len(sems)
    n_in = len(exchange.inputs)
    source_at = [i for i in range(n_in) if i not in exchange.aliases]

    def body(*refs):
        src_refs, out_refs = refs[:n_src], refs[n_src:n_src + n_out]
        sem_refs = refs[n_src + n_out:n_src + n_out + n_sem]
        in_refs = [None] * n_in
        for k, i in enumerate(source_at):
            in_refs[i] = src_refs[k]
        for i, o in exchange.aliases.items():
            in_refs[i] = out_refs[o]
        exchange.make(in_refs, out_refs, sem_refs)[1]()

    arrays = list(sources) + list(landing)
    outs = pl.pallas_call(
        body, name=name, in_specs=[HBM] * len(arrays) + [_SEM] * n_sem + [HBM] * len(after),
        out_specs=[HBM] * len(arrays), out_shape=[pltpu.HBM(a.shape, a.dtype) for a in arrays],
        input_output_aliases={i: i for i in range(len(arrays))}, compiler_params=_params(has_side_effects=_DATAFLOW),
    )(*arrays, *sems, *after)
    return outs[:n_src], outs[n_src:]


N_GATHERED = 2
FIRST_COPIES = 1 + N_OTHER_CHIPS
_GATHERED_SHAPES = ((D_MODEL, IN_COLS), (D_MODEL, D_MODEL))


def _first_sem(layer, a, k):
    return (layer * N_GATHERED + a) * FIRST_COPIES + k


def _gather_copy(window_of, full_ref, blk, send_sem, recv_sem, to, src=None):
    window = window_of(full_ref, blk)
    return pltpu.make_async_remote_copy(
        src_ref=window if src is None else src, dst_ref=window, send_sem=send_sem, recv_sem=recv_sem,
        device_id=to, device_id_type=MESH)


def _first_copies(layer, shard_refs, full_refs, send_sems, recv_sems, local_sems):
    x, y, c = _mesh_position()
    me = _block_id(x, y, c)
    own, remote = [], []
    for a in range(N_GATHERED):
        shard = shard_refs[a].at[layer]
        own.append(pltpu.make_async_copy(
            shard, _BLOCK_OF[a](full_refs[a], me), local_sems.at[layer * N_GATHERED + a]))
        targets = [(x, y, 1 - c)] + [(*chip, c) for chip in _other_chips(x, y)]
        remote += [_gather_copy(_BLOCK_OF[a], full_refs[a], me, send_sems.at[_first_sem(layer, a, k)],
                                recv_sems.at[_first_sem(layer, a, k)], to, src=shard)
                   for k, to in enumerate(targets)]
    return own, remote


def _gather_weights_start(first_layer, win_shards, wout_shards, after):
    n_layers = win_shards.shape[0]
    n_first = n_layers * N_GATHERED * FIRST_COPIES
    sem_shapes = [pltpu.SemaphoreType.DMA((n_first,)), pltpu.SemaphoreType.DMA((n_first,)),
                  pltpu.SemaphoreType.DMA((n_layers * N_GATHERED,))]
    shards = [win_shards, wout_shards]

    def body(win_sh, wout_sh, *rest):
        send_sems, recv_sems, local_sems, win_thru, wout_thru, *landing = rest[len(after):]
        for layer in range(n_layers):
            own, remote = _first_copies(layer, (win_sh, wout_sh), landing[N_GATHERED * layer:N_GATHERED * (layer + 1)],
                                        send_sems, recv_sems, local_sems)
            for cp in own + remote:
                cp.start()

    outs = pl.pallas_call(
        body, name=f"all_gather_weights_start_{first_layer}", in_specs=[HBM] * (2 + len(after)),
        out_specs=[_SEM] * 3 + [HBM] * (2 + n_layers * N_GATHERED),
        out_shape=(sem_shapes + [pltpu.HBM(s.shape, s.dtype) for s in shards]
                   + [pltpu.HBM(s, BF16) for _ in range(n_layers) for s in _GATHERED_SHAPES]),
        input_output_aliases={0: 3, 1: 4}, compiler_params=_params(has_side_effects=_DATAFLOW),
    )(*shards, *after)
    landing = outs[5:]
    return outs[:3], outs[3:5], [landing[N_GATHERED * l:N_GATHERED * (l + 1)] for l in range(n_layers)]


def _passed_on_copies(full_refs, send_sems, recv_sems, core_of_block):
    x, y, c = _mesh_position()
    return [_gather_copy(_BLOCK_OF[a], full_refs[a], _block_id(*chip, core_of_block),
                         send_sems.at[a * N_OTHER_CHIPS + j], recv_sems.at[a * N_OTHER_CHIPS + j], (x, y, 1 - c))
            for a in range(N_GATHERED) for j, chip in enumerate(_other_chips(x, y))]


def _gather_weights_pass_on(layer, index, first_recv_sems, landing, after):
    n = N_GATHERED * N_OTHER_CHIPS

    def body(win_ref, wout_ref, first_recv, *rest):
        send_sems, recv_sems = rest[len(after):len(after) + 2]
        x, y, c = _mesh_position()
        full_refs = (win_ref, wout_ref)
        passed = _passed_on_copies(full_refs, send_sems, recv_sems, c)
        for a in range(N_GATHERED):
            for j, chip in enumerate(_other_chips(x, y)):
                sem = _first_sem(index, a, 1 + j)
                _gather_copy(_BLOCK_OF[a], full_refs[a], _block_id(*chip, c), first_recv.at[sem], first_recv.at[sem],
                             (x, y, c)).wait_recv()
                passed[a * N_OTHER_CHIPS + j].start()

    outs = pl.pallas_call(
        body, name=f"all_gather_weights_pass_on_{layer}", in_specs=[HBM] * N_GATHERED + [_SEM] + [HBM] * len(after),
        out_specs=[_SEM] * 2 + [HBM] * N_GATHERED,
        out_shape=[pltpu.SemaphoreType.DMA((n,)), pltpu.SemaphoreType.DMA((n,))]
        + [pltpu.HBM(a.shape, a.dtype) for a in landing],
        input_output_aliases={a: 2 + a for a in range(N_GATHERED)},
        compiler_params=_params(has_side_effects=_DATAFLOW),
    )(*landing, first_recv_sems, *after)
    return outs[:2], outs[2:]


def _gather_weights_finish(layer, index, first_sems, passed_sems, shards, landing):
    def body(win_ref, wout_ref, first_send, first_recv, local_sems, passed_send, passed_recv, win_sh, wout_sh, *thru):
        x, y, c = _mesh_position()
        full_refs = (win_ref, wout_ref)
        own, remote = _first_copies(index, (win_sh, wout_sh), full_refs, first_send, first_recv, local_sems)
        for a in range(N_GATHERED):
            sem = _first_sem(index, a, 0)
            _gather_copy(_BLOCK_OF[a], full_refs[a], _block_id(x, y, 1 - c), first_recv.at[sem], first_recv.at[sem],
                         (x, y, c)).wait_recv()
        for cp in _passed_on_copies(full_refs, passed_send, passed_recv, 1 - c):
            cp.wait_recv()
        for cp in remote + _passed_on_copies(full_refs, passed_send, passed_recv, c):
            cp.wait_send()
        for cp in own:
            cp.wait()

    return pl.pallas_call(
        body, name=f"all_gather_weights_finish_{layer}", in_specs=[HBM] * N_GATHERED + [_SEM] * 5 + [HBM] * 2,
        out_specs=[HBM] * N_GATHERED, out_shape=[pltpu.HBM(a.shape, a.dtype) for a in landing],
        input_output_aliases={a: a for a in range(N_GATHERED)}, compiler_params=_params(has_side_effects=_DATAFLOW),
    )(*landing, *first_sems, *passed_sems, *shards)


ALL_KINDS = (0, 1, 2)


def _sibling_exchange(grads, kinds):
    n_arr = len(grads)

    def make(in_refs, out_refs, sems):
        send_sems, recv_sems = sems
        x, y, c = _mesh_position()
        copies = [pltpu.make_async_remote_copy(
            src_ref=_BLOCK_OF[kinds[a]](in_refs[a], 2 * q + (1 - c)), dst_ref=out_refs[a].at[q],
            send_sem=send_sems.at[a * N_CHIP + q], recv_sem=recv_sems.at[a * N_CHIP + q],
            device_id=(x, y, 1 - c), device_id_type=MESH)
            for a in range(n_arr) for q in range(N_CHIP)]

        def start():
            for cp in copies:
                cp.start()

        def finish():
            for cp in copies:
                cp.wait_recv()
            for cp in copies:
                cp.wait_send()

        return start, finish

    return _Exchange(
        grads, [jax.ShapeDtypeStruct((N_CHIP,) + _BLOCK_SHAPES[k], BF16) for k in kinds], {},
        [pltpu.SemaphoreType.DMA((n_arr * N_CHIP,)), pltpu.SemaphoreType.DMA((n_arr * N_CHIP,))], make)


def _chips_exchange(layer, partials, received, kinds):
    n_arr = len(partials)

    def make(in_refs, out_refs, sems):
        send_sems, recv_sems = sems
        x, y, c = _mesh_position()
        copies = [pltpu.make_async_remote_copy(
            src_ref=in_refs[a].at[2 * qx + qy, layer], dst_ref=out_refs[a].at[j, layer],
            send_sem=send_sems.at[a * N_OTHER_CHIPS + j], recv_sem=recv_sems.at[a * N_OTHER_CHIPS + j],
            device_id=(qx, qy, c), device_id_type=MESH)
            for a in range(n_arr) for j, (qx, qy) in enumerate(_other_chips(x, y))]

        def start():
            for cp in copies:
                cp.start()

        def finish():
            for cp in copies:
                cp.wait_recv()
            for cp in copies:
                cp.wait_send()

        return start, finish

    inputs = list(partials)
    aliases = {}
    if received is not None:
        inputs += list(received)
        aliases = {n_arr + a: a for a in range(n_arr)}
    return _Exchange(
        inputs, [jax.ShapeDtypeStruct((N_OTHER_CHIPS, DEPTH) + _BLOCK_SHAPES[k], BF16) for k in kinds], aliases,
        [pltpu.SemaphoreType.DMA((n_arr * N_OTHER_CHIPS,)), pltpu.SemaphoreType.DMA((n_arr * N_OTHER_CHIPS,))], make)


def _all_gather_exchange(v):
    def make(in_refs, out_refs, sems):
        send_sems, recv_sems, local_sem = sems
        x, y, c = _mesh_position()
        me = _block_id(x, y, c)
        own = pltpu.make_async_copy(in_refs[0], out_refs[0].at[me], local_sem.at[0])
        sends, arrivals = [], []
        for k in range(1, N_DEV):
            px, py, pc = x ^ ((k >> 2) & 1), y ^ ((k >> 1) & 1), c ^ (k & 1)
            sends.append(pltpu.make_async_remote_copy(
                src_ref=in_refs[0], dst_ref=out_refs[0].at[me], send_sem=send_sems.at[k - 1],
                recv_sem=recv_sems.at[k - 1], device_id=(px, py, pc), device_id_type=MESH))
            arrivals.append(pltpu.make_async_remote_copy(
                src_ref=in_refs[0], dst_ref=out_refs[0].at[_block_id(px, py, pc)], send_sem=send_sems.at[k - 1],
                recv_sem=recv_sems.at[k - 1], device_id=(x, y, c), device_id_type=MESH))

        def start():
            for cp in [own] + sends:
                cp.start()

        def finish():
            for cp in arrivals:
                cp.wait_recv()
            for cp in sends:
                cp.wait_send()
            own.wait()

        return start, finish

    return _Exchange(
        [v], [jax.ShapeDtypeStruct((N_DEV,) + v.shape, v.dtype)], {},
        [pltpu.SemaphoreType.DMA((N_DEV - 1,)), pltpu.SemaphoreType.DMA((N_DEV - 1,)),
         pltpu.SemaphoreType.DMA((1,))], make)


def _host(exchange, args, in_specs, out_shape, out_specs, scratch):
    n_own = (len(args), len(out_shape), len(scratch))
    if exchange is None:
        return {}, lambda refs: (refs, None)
    n_ex = (len(exchange.inputs), len(exchange.out_shapes), len(exchange.sem_shapes))
    aliases = {n_own[0] + i: n_own[1] + o for i, o in exchange.aliases.items()}
    args += exchange.inputs
    in_specs += [HBM] * n_ex[0]
    out_shape += exchange.out_shapes
    out_specs += [HBM] * n_ex[1]
    scratch += exchange.sem_shapes

    def split(refs):
        own, theirs, at = [], [], 0
        for mine, ex in zip(n_own, n_ex):
            own += refs[at:at + mine]
            theirs.append(refs[at + mine:at + mine + ex])
            at += mine + ex
        return own, exchange.make(*theirs)

    return aliases, split


def _all_gather_small(v, name, after=()):
    vmem = pl.BlockSpec(memory_space=pltpu.VMEM)

    def body(v_ref, *rest):
        out_ref, send_sems, recv_sems = rest[len(after):]
        x, y, c = _mesh_position()
        me = _block_id(x, y, c)
        out_ref[me] = v_ref[...]
        sends = []
        for k in range(1, N_DEV):
            px, py, pc = x ^ ((k >> 2) & 1), y ^ ((k >> 1) & 1), c ^ (k & 1)
            send = pltpu.make_async_remote_copy(
                src_ref=v_ref, dst_ref=out_ref.at[me], send_sem=send_sems.at[k - 1], recv_sem=recv_sems.at[k - 1],
                device_id=(px, py, pc), device_id_type=MESH)
            send.start()
            sends.append((send, _block_id(px, py, pc)))
        for k, (send, peer) in enumerate(sends):
            pltpu.make_async_remote_copy(
                src_ref=v_ref, dst_ref=out_ref.at[peer], send_sem=send_sems.at[k], recv_sem=recv_sems.at[k],
                device_id=(x, y, c), device_id_type=MESH).wait_recv()
        for send, _ in sends:
            send.wait_send()

    return pl.pallas_call(
        body, name=name, in_specs=[vmem] + [HBM] * len(after), out_specs=vmem,
        out_shape=jax.ShapeDtypeStruct((N_DEV,) + v.shape, v.dtype),
        scratch_shapes=[pltpu.SemaphoreType.DMA((N_DEV - 1,)), pltpu.SemaphoreType.DMA((N_DEV - 1,))],
        compiler_params=_params(),
    )(v, *after)


def _forward_layer(layer, x, vec, cps, wpool, win, wout, target=None):
    t_len = x.shape[0]
    n_tiles = t_len // ROW_TILE
    row = lambda cols: pl.BlockSpec((ROW_TILE, cols), lambda i: (i, 0))
    col_t = pl.BlockSpec((D_MODEL, ROW_TILE), lambda i: (0, i))
    widths = (D_MODEL, 2 * CONV_W, 3 * CONV_W, 4 * POOL_W)
    head = target is not None

    def body(x_ref, vec_ref, cps_ref, wpool_ref, win_ref, wout_ref, *rest):
        target_ref = rest[0] if head else None
        xo_ref, y_ref, ht_ref, ycatt_ref, uc_ref, fa_ref, fp_ref = rest[head:head + 7]
        loss_ref = rest[head + 7] if head else None
        zc_ref, pc_ref = rest[-2:]
        i = pl.program_id(0)

        @pl.when(i == 0)
        def _():
            zc_ref[...] = jnp.zeros_like(zc_ref)
            pc_ref[...] = jnp.zeros_like(pc_ref)
            if head:
                loss_ref[...] = jnp.zeros_like(loss_ref)

        x_t = x_ref[...]
        shift, scale, gate = vec_ref[0:1, :], vec_ref[1:2, :], vec_ref[2:3, :]
        g_pre, g_post = vec_ref[3:4, :], vec_ref[4:5, :]
        w0, w1, w2, ps = cps_ref[0:1, :], cps_ref[1:2, :], cps_ref[2:3, :], cps_ref[3:4, :]
        rx = lax.rsqrt(jnp.mean(x_t * x_t, axis=-1, keepdims=True) + NORM_EPS)
        h = (x_t * rx) * g_pre * (1.0 + scale) + shift
        ht_ref[...] = h.astype(BF16)
        proj = _dot(h.astype(BF16), win_ref[...])
        u_a, b_a, c_a, g_a, u_p, g_p = _split_proj(proj)
        uc_ref[...] = jnp.concatenate([u_a, c_a], axis=1).astype(BF16)

        z = c_a * u_a
        zcat = jnp.concatenate([zc_ref[...], z], axis=0)
        zc_ref[...] = z[ROW_TILE - CONV_HALO:]
        conv = (w0 * _rows_from_before(zcat, 2)[CONV_HALO:] + w1 * _rows_from_before(zcat, 1)[CONV_HALO:] + w2 * z)
        sig_a = _sigmoid(g_a)
        silu_a = g_a * sig_a
        b_conv = b_a * conv
        y_a = b_conv * silu_a
        fa_ref[...] = jnp.concatenate(
            [silu_a * conv, silu_a * b_a, b_conv * (sig_a + silu_a * (1.0 - sig_a))], axis=1).astype(BF16)

        pcat = jnp.concatenate([pc_ref[...], u_p], axis=0)
        pc_ref[...] = u_p[ROW_TILE - POOL_HALO:]
        counts = _window_counts(i * ROW_TILE, ROW_TILE)
        pooled, mixed = [], []
        for g, w in enumerate(POOL_WINDOWS):
            cols = slice(g * GROUP_D, (g + 1) * GROUP_D)
            s = pcat[:, cols]
            step = 1
            while step < w:
                s = s + _rows_from_before(s, step)
                step *= 2
            pooled_g = (s[POOL_HALO:] * (1.0 / jnp.minimum(counts, float(w))) - u_p[:, cols]).astype(BF16)
            pooled.append(pooled_g)
            mixed.append(_dot(pooled_g, wpool_ref[g]))
        mixed = jnp.concatenate(mixed, axis=1)
        sig_p = _sigmoid(g_p)
        silu_p = g_p * sig_p
        mixed_ps = mixed * ps
        y_p = mixed_ps * silu_p
        fp_ref[...] = jnp.concatenate(
            [(ps * silu_p).astype(BF16), (mixed_ps * (sig_p + silu_p * (1.0 - sig_p))).astype(BF16),
             (silu_p * mixed).astype(BF16)] + pooled, axis=1)

        ycat = jnp.concatenate([y_a, y_p], axis=1)
        ycatt_ref[...] = ycat.astype(BF16)
        y_b = _dot(ycat.astype(BF16), wout_ref[...]).astype(BF16)
        y_ref[...] = y_b
        y_t = y_b.astype(F32)
        ry = lax.rsqrt(jnp.mean(y_t * y_t, axis=-1, keepdims=True) + NORM_EPS)
        x_next = x_t + gate * (y_t * ry * g_post)
        if head:
            err = x_next - target_ref[...]
            xo_ref[...] = err * (1.0 / D_MODEL)
            loss_ref[...] += jnp.sum(err * err) * (0.5 / D_MODEL)
        else:
            xo_ref[...] = x_next

    tile = (SUBLANES, LANES)
    return pl.pallas_call(
        body, name=f"forward_layer_{layer}", grid=(n_tiles,),
        in_specs=[row(D_MODEL), _layer_spec(vec.shape, layer), _layer_spec(cps.shape, layer),
                  _layer_spec(wpool.shape, layer), _whole_spec(win.shape), _whole_spec(wout.shape)]
        + [row(D_MODEL)] * head,
        out_specs=[row(D_MODEL), row(D_MODEL), row(D_MODEL), row(D_MODEL)] + [row(w) for w in widths[1:]]
        + [_whole_spec(tile)] * head,
        out_shape=[jax.ShapeDtypeStruct((t_len, D_MODEL), F32), jax.ShapeDtypeStruct((t_len, D_MODEL), BF16),
                   jax.ShapeDtypeStruct((t_len, D_MODEL), BF16), jax.ShapeDtypeStruct((t_len, D_MODEL), BF16)]
        + [jax.ShapeDtypeStruct((t_len, w), BF16) for w in widths[1:]] + [jax.ShapeDtypeStruct(tile, F32)] * head,
        scratch_shapes=[pltpu.VMEM((CONV_HALO, CONV_W), F32), pltpu.VMEM((POOL_HALO, POOL_W), F32)],
        compiler_params=_params(dimension_semantics=("arbitrary",)),
    )(x, vec, cps, wpool, win, wout, *([target] * head))


def _backward_layer(layer, dxo, y, x, uc, fa, fp, vec, cps, wpool, wout, win):
    t_len = dxo.shape[0]
    n_tiles = t_len // BWD_TILE
    halo_per_tile = BWD_TILE // POOL_HALO
    rev = lambda cols: pl.BlockSpec((BWD_TILE, cols), lambda i: (n_tiles - 1 - i, 0))
    halo_spec = pl.BlockSpec(
        (POOL_HALO, 2 * CONV_W), lambda i: (jnp.maximum((n_tiles - 1 - i) * halo_per_tile - 1, 0), 0))
    gwpool_shape = (len(POOL_WINDOWS), GROUP_D, GROUP_D)

    def body(dxo_ref, y_ref, x_ref, uc_ref, uch_ref, fa_ref, fp_ref, vec_ref, cps_ref, wpool_ref, wout_ref, win_ref,
             dx_ref, dproj_ref, dy_ref, gwpool_ref, dcps_ref, dvec_ref, gwpool_acc, dcc_ref, qc_ref):
        i = pl.program_id(0)
        tile = n_tiles - 1 - i

        @pl.when(i == 0)
        def _():
            gwpool_acc[...] = jnp.zeros_like(gwpool_acc)
            dcps_ref[...] = jnp.zeros_like(dcps_ref)
            dvec_ref[...] = jnp.zeros_like(dvec_ref)
            dcc_ref[...] = jnp.zeros_like(dcc_ref)
            qc_ref[...] = jnp.zeros_like(qc_ref)

        shift, scale, gate = vec_ref[0:1, :], vec_ref[1:2, :], vec_ref[2:3, :]
        g_pre, g_post = vec_ref[3:4, :], vec_ref[4:5, :]
        w0, w1, w2 = cps_ref[0:1, :], cps_ref[1:2, :], cps_ref[2:3, :]

        dxo_t = dxo_ref[...]
        y_t = y_ref[...].astype(F32)
        ry = lax.rsqrt(jnp.mean(y_t * y_t, axis=-1, keepdims=True) + NORM_EPS)
        yh = y_t * ry
        dvec_ref[2:3, :] += jnp.sum(dxo_t * yh, axis=0, keepdims=True)
        dyh = dxo_t * (gate * g_post)
        dy_b = (ry * (dyh - yh * jnp.mean(dyh * yh, axis=-1, keepdims=True))).astype(BF16)
        dy_ref[...] = dy_b
        dycat = _dot_nt(dy_b, wout_ref[...])
        dy_a, dy_p = dycat[:, :CONV_W], dycat[:, CONV_W:]

        fa_t = fa_ref[...].astype(F32)
        db_a = dy_a * fa_t[:, :CONV_W]
        dconv = dy_a * fa_t[:, CONV_W:2 * CONV_W]
        dg_a = dy_a * fa_t[:, 2 * CONV_W:]
        uc_t = uc_ref[...].astype(F32)
        u_a, c_a = uc_t[:, :CONV_W], uc_t[:, CONV_W:]
        halo = jnp.where(tile > 0, uch_ref[...].astype(F32), 0.0)[POOL_HALO - CONV_HALO:]
        z = c_a * u_a
        zcat = jnp.concatenate([halo[:, CONV_W:] * halo[:, :CONV_W], z], axis=0)
        z1 = _rows_from_before(zcat, 1)[CONV_HALO:]
        z2 = _rows_from_before(zcat, 2)[CONV_HALO:]
        dccat = jnp.concatenate([dconv, dcc_ref[...]], axis=0)
        dc1 = _rows_from_after(dccat, 1)[:BWD_TILE]
        dc2 = _rows_from_after(dccat, 2)[:BWD_TILE]
        dz = w2 * dconv + w1 * dc1 + w0 * dc2
        dcc_ref[...] = dconv[:CONV_HALO]
        dcps_ref[0:1, :] += jnp.sum(dconv * z2, axis=0, keepdims=True)
        dcps_ref[1:2, :] += jnp.sum(dconv * z1, axis=0, keepdims=True)
        dcps_ref[2:3, :] += jnp.sum(dconv * z, axis=0, keepdims=True)
        du_a = dz * c_a
        dc_a = dz * u_a

        dmixed = (dy_p * fp_ref[:, :POOL_W].astype(F32)).astype(BF16)
        dg_p = dy_p * fp_ref[:, POOL_W:2 * POOL_W].astype(F32)
        dcps_ref[3:4, :] += jnp.sum(dy_p * fp_ref[:, 2 * POOL_W:3 * POOL_W].astype(F32), axis=0, keepdims=True)
        counts = _window_counts(tile * BWD_TILE, BWD_TILE)
        du_p, q_head = [], []
        for g, w in enumerate(POOL_WINDOWS):
            cols = slice(g * GROUP_D, (g + 1) * GROUP_D)
            dm_g = dmixed[:, cols]
            dpooled_g = _dot_nt(dm_g, wpool_ref[g])
            gwpool_acc[g] += _dot_tn(fp_ref[:, 3 * POOL_W + g * GROUP_D:3 * POOL_W + (g + 1) * GROUP_D], dm_g)
            q_g = dpooled_g * (1.0 / jnp.minimum(counts, float(w)))
            q_head.append(q_g[:POOL_HALO])
            s = jnp.concatenate([q_g, qc_ref[:, cols]], axis=0)
            step = 1
            while step < w:
                s = s + _rows_from_after(s, step)
                step *= 2
            du_p.append(s[:BWD_TILE] - dpooled_g)
        qc_ref[...] = jnp.concatenate(q_head, axis=1)
        dproj_b = jnp.concatenate([du_a, db_a, dc_a, dg_a] + du_p + [dg_p], axis=1).astype(BF16)
        dproj_ref[...] = dproj_b

        x_t = x_ref[...]
        rx = lax.rsqrt(jnp.mean(x_t * x_t, axis=-1, keepdims=True) + NORM_EPS)
        xn = x_t * rx
        mod_scale = 1.0 + scale
        dh = _dot_nt(dproj_b, win_ref[...])
        dvec_ref[0:1, :] += jnp.sum(dh, axis=0, keepdims=True)
        dvec_ref[1:2, :] += jnp.sum(dh * xn, axis=0, keepdims=True)
        dxn = dh * (g_pre * mod_scale)
        dx_ref[...] = dxo_t + rx * (dxn - xn * jnp.mean(dxn * xn, axis=-1, keepdims=True))

        @pl.when(i == n_tiles - 1)
        def _():
            gwpool_ref[...] = gwpool_acc[...].astype(BF16)
            sum_dh_xn, sum_dxo_yh = dvec_ref[1:2, :], dvec_ref[2:3, :]
            dvec_ref[1:2, :] = sum_dh_xn * g_pre
            dvec_ref[3:4, :] = sum_dh_xn * mod_scale
            dvec_ref[2:3, :] = sum_dxo_yh * g_post
            dvec_ref[4:5, :] = sum_dxo_yh * gate

    return pl.pallas_call(
        body, name=f"backward_layer_{layer}", grid=(n_tiles,),
        in_specs=[rev(D_MODEL), rev(D_MODEL), rev(D_MODEL), rev(2 * CONV_W), halo_spec, rev(3 * CONV_W),
                  rev(4 * POOL_W), _layer_spec(vec.shape, layer), _layer_spec(cps.shape, layer),
                  _layer_spec(wpool.shape, layer), _whole_spec(wout.shape), _whole_spec(win.shape)],
        out_specs=[rev(D_MODEL), rev(IN_COLS), rev(D_MODEL), _whole_spec(gwpool_shape),
                   _whole_spec((SUBLANES, CONV_W)), _whole_spec((SUBLANES, D_MODEL))],
        out_shape=[jax.ShapeDtypeStruct((t_len, D_MODEL), F32), jax.ShapeDtypeStruct((t_len, IN_COLS), BF16),
                   jax.ShapeDtypeStruct((t_len, D_MODEL), BF16), jax.ShapeDtypeStruct(gwpool_shape, BF16),
                   jax.ShapeDtypeStruct((SUBLANES, CONV_W), F32), jax.ShapeDtypeStruct((SUBLANES, D_MODEL), F32)],
        scratch_shapes=[pltpu.VMEM(gwpool_shape, F32), pltpu.VMEM((CONV_HALO, CONV_W), F32),
                        pltpu.VMEM((POOL_HALO, POOL_W), F32)],
        compiler_params=_params(dimension_semantics=("arbitrary",)),
    )(dxo, y, x, uc, uc, fa, fp, vec, cps, wpool, wout, win)


def _weight_grads(layer, h_t, dproj, ycat_t, dy, exchange, after=()):
    t_len = dy.shape[0]
    n_in, n_out = IN_COLS // GWIN_COLS, D_MODEL // GWOUT_COLS
    args = [h_t, dproj, ycat_t, dy, *after]
    in_specs = [_whole_spec(h_t.shape),
                pl.BlockSpec((t_len, GWIN_COLS), lambda s: (0, jnp.minimum(s, n_in - 1))),
                _whole_spec(ycat_t.shape),
                pl.BlockSpec((t_len, GWOUT_COLS), lambda s: (0, jnp.maximum(s - n_in, 0)))] + [HBM] * len(after)
    out_shape = [jax.ShapeDtypeStruct((D_MODEL, IN_COLS), BF16), jax.ShapeDtypeStruct((D_MODEL, D_MODEL), BF16)]
    out_specs = [pl.BlockSpec((D_MODEL, GWIN_COLS), lambda s: (0, jnp.minimum(s, n_in - 1))),
                 pl.BlockSpec((D_MODEL, GWOUT_COLS), lambda s: (0, jnp.maximum(s - n_in, 0)))]
    scratch = []
    aliases, split = _host(exchange, args, in_specs, out_shape, out_specs, scratch)

    def body(*refs):
        (ht_ref, dproj_ref, ycatt_ref, dy_ref, *_, gwin_ref, gwout_ref), hosted = split(refs)
        s = pl.program_id(0)
        if hosted is not None:
            pl.when(s == 0)(hosted[0])

        @pl.when(s < n_in)
        def _():
            gwin_ref[...] = _dot_tn(ht_ref[...], dproj_ref[...]).astype(BF16)

        @pl.when(s >= n_in)
        def _():
            gwout_ref[...] = _dot_tn(ycatt_ref[...], dy_ref[...]).astype(BF16)

        if hosted is not None:
            pl.when(s == n_in + n_out - 1)(hosted[1])

    return pl.pallas_call(
        body, name=f"weight_grads_{layer}", grid=(n_in + n_out,), in_specs=in_specs, out_specs=out_specs,
        out_shape=out_shape, scratch_shapes=scratch, input_output_aliases=aliases,
        compiler_params=_params(dimension_semantics=("arbitrary",)),
    )(*args)


def _add_sibling_blocks(name, layer, grads, received, core, partials, kinds):
    n_arr = len(grads)

    def body(core_ref, *refs):
        mine, theirs, outs = refs[:n_arr], refs[n_arr:2 * n_arr], refs[-n_arr:]
        for a in range(n_arr):
            outs[a][...] = (mine[a][...].astype(F32) + theirs[a][...].astype(F32)).astype(BF16)

    own_of_kind = [
        pl.BlockSpec((D_MODEL, W_IN_SHARD), lambda q, core_ref: (0, 2 * q + core_ref[0])),
        pl.BlockSpec((W_OUT_SHARD, D_MODEL), lambda q, core_ref: (2 * q + core_ref[0], 0)),
        pl.BlockSpec((POOL_SHARD, GROUP_D), lambda q, core_ref: (2 * q + core_ref[0], 0)),
    ]
    shapes = [_BLOCK_SHAPES[k] for k in kinds]
    recv_specs = [pl.BlockSpec((None,) + s, lambda q, core_ref: (q, 0, 0)) for s in shapes]
    out_specs = [pl.BlockSpec((None, None) + s, lambda q, core_ref: (q, layer, 0, 0)) for s in shapes]
    args = [core, *grads, *received]
    in_specs = [own_of_kind[k] for k in kinds] + recv_specs
    aliases = {}
    if partials is not None:
        aliases = {len(args) + a: a for a in range(n_arr)}
        args += list(partials)
        in_specs += [HBM] * n_arr
    return pl.pallas_call(
        body, name=name,
        grid_spec=pltpu.PrefetchScalarGridSpec(
            num_scalar_prefetch=1, grid=(N_CHIP,), in_specs=in_specs, out_specs=out_specs),
        out_shape=[jax.ShapeDtypeStruct((N_CHIP, DEPTH) + s, BF16) for s in shapes],
        input_output_aliases=aliases,
        compiler_params=_params(dimension_semantics=("arbitrary",)),
    )(*args)


def _modulation_columns(c_all, w_ada):
    def body(c_ref, w_ref, cact_ref, out_ref):
        c_t = c_ref[...]
        c_act = c_t * _sigmoid(c_t)
        cact_ref[...] = c_act
        out_ref[...] = jnp.dot(c_act, w_ref[...], preferred_element_type=F32, precision=lax.Precision.HIGHEST)

    return pl.pallas_call(
        body, name="modulation_columns", grid=(DEPTH,),
        in_specs=[pl.BlockSpec((N_DEV, D_MODEL), lambda l: (0, 0)),
                  pl.BlockSpec((None, D_MODEL, W_IN_SHARD), lambda l: (l, 0, 0))],
        out_specs=[pl.BlockSpec((N_DEV, D_MODEL), lambda l: (0, 0)),
                   pl.BlockSpec((N_DEV, W_IN_SHARD), lambda l: (0, l))],
        out_shape=[jax.ShapeDtypeStruct((N_DEV, D_MODEL), F32),
                   jax.ShapeDtypeStruct((N_DEV, DEPTH * W_IN_SHARD), F32)],
        compiler_params=_params(dimension_semantics=("arbitrary",)),
    )(c_all, w_ada)


def _adamw(w, g, m, v):
    m_new = ADAM_B1 * m + (1.0 - ADAM_B1) * g
    v_new = ADAM_B2 * v + (1.0 - ADAM_B2) * (g * g)
    m_hat = m_new / (1.0 - ADAM_B1 ** ADAM_STEP)
    v_hat = v_new / (1.0 - ADAM_B2 ** ADAM_STEP)
    delta = -ADAM_LR * (m_hat / (jnp.sqrt(v_hat) + ADAM_EPS) + ADAM_WD * w)
    return delta, m_new, v_new


def _adamw_w_ada(w, m, v, c_act_t, dmod_cols):
    def body(w_ref, m_ref, v_ref, ct_ref, dm_ref, g_ref, d_ref, mo_ref, vo_ref):
        g = ct_ref[:, 0:1] * dm_ref[0:1, :]
        for b in range(1, N_DEV):
            g = g + ct_ref[:, b:b + 1] * dm_ref[b:b + 1, :]
        g_ref[...] = g
        d_ref[...], mo_ref[...], vo_ref[...] = _adamw(w_ref[...], g, m_ref[...], v_ref[...])

    big = pl.BlockSpec((None, D_MODEL, W_IN_SHARD), lambda l: (l, 0, 0))
    return pl.pallas_call(
        body, name="adamw_w_ada", grid=(DEPTH,),
        in_specs=[big, big, big, pl.BlockSpec((D_MODEL, N_DEV), lambda l: (0, 0)),
                  pl.BlockSpec((None, N_DEV, W_IN_SHARD), lambda l: (l, 0, 0))],
        out_specs=[big] * 4, out_shape=[jax.ShapeDtypeStruct(w.shape, F32)] * 4,
        compiler_params=_params(dimension_semantics=("arbitrary",)),
    )(w, m, v, c_act_t, dmod_cols)


def _sum_chip_partials(own_ref, recv_ref):
    g = own_ref[...].astype(F32)
    for j in range(N_OTHER_CHIPS):
        g = g + recv_ref[j].astype(F32)
    return g


def _partial_specs(row_tile, cols, first_layer=0):
    own = pl.BlockSpec((None, None, row_tile, cols), lambda l, r, chip_ref: (chip_ref[0], first_layer + l, r, 0))
    recv = pl.BlockSpec((N_OTHER_CHIPS, None, row_tile, cols), lambda l, r, chip_ref: (0, first_layer + l, r, 0))
    return own, recv


def _adamw_reduced(name, w, m, v, partial, received, chip, row_tile, layers, continued):
    depth, rows, cols = w.shape
    first, stop = layers

    def body(chip_ref, w_ref, m_ref, v_ref, own_ref, recv_ref, *rest):
        g_ref, d_ref, mo_ref, vo_ref = rest[-4:]
        g = _sum_chip_partials(own_ref, recv_ref)
        g_ref[...] = g
        d_ref[...], mo_ref[...], vo_ref[...] = _adamw(w_ref[...], g, m_ref[...], v_ref[...])

    blk = pl.BlockSpec((None, row_tile, cols), lambda l, r, chip_ref: (first + l, r, 0))
    args = [chip, w, m, v, partial, received]
    in_specs = [blk, blk, blk, *_partial_specs(row_tile, cols, first)]
    aliases = {}
    if continued is not None:
        aliases = {len(args) + k: k for k in range(4)}
        args += list(continued)
        in_specs += [HBM] * 4
    return pl.pallas_call(
        body, name=name,
        grid_spec=pltpu.PrefetchScalarGridSpec(
            num_scalar_prefetch=1, grid=(stop - first, rows // row_tile), in_specs=in_specs, out_specs=[blk] * 4),
        out_shape=[jax.ShapeDtypeStruct(w.shape, F32)] * 4, input_output_aliases=aliases,
        compiler_params=_params(dimension_semantics=("arbitrary", "arbitrary")),
    )(*args)


def _reduce_w_pool(partial, received, chip):
    def body(chip_ref, own_ref, recv_ref, g_ref):
        g_ref[...] = _sum_chip_partials(own_ref, recv_ref)

    return pl.pallas_call(
        body, name="reduce_w_pool",
        grid_spec=pltpu.PrefetchScalarGridSpec(
            num_scalar_prefetch=1, grid=(DEPTH, 1), in_specs=list(_partial_specs(POOL_SHARD, GROUP_D)),
            out_specs=pl.BlockSpec((POOL_SHARD, GROUP_D), lambda l, r, chip_ref: (l, 0))),
        out_shape=jax.ShapeDtypeStruct((DEPTH * POOL_SHARD, GROUP_D), F32),
        compiler_params=_params(dimension_semantics=("arbitrary", "arbitrary")),
    )(chip, partial, received)


def _adamw_small(params):
    n = len(params)

    def body(*refs):
        ins, outs = refs[:4 * n], refs[4 * n:]
        for p in range(n):
            w_ref, g_ref, m_ref, v_ref = ins[4 * p:4 * p + 4]
            d_ref, mo_ref, vo_ref = outs[3 * p:3 * p + 3]
            d_ref[...], mo_ref[...], vo_ref[...] = _adamw(w_ref[...], g_ref[...], m_ref[...], v_ref[...])

    vmem = pl.BlockSpec(memory_space=pltpu.VMEM)
    flat = [a for group in params for a in group]
    out_shape = [jax.ShapeDtypeStruct(group[0].shape, F32) for group in params for _ in range(3)]
    outs = pl.pallas_call(
        body, name="adamw_small", in_specs=[vmem] * len(flat), out_specs=[vmem] * len(out_shape),
        out_shape=out_shape, compiler_params=_params(),
    )(*flat)
    return [tuple(outs[3 * p:3 * p + 3]) for p in range(n)]


def _sum_sources(slabs):
    def body(s_ref, o_ref):
        acc = s_ref[0]
        for b in range(1, N_DEV):
            acc = acc + s_ref[b]
        o_ref[...] = acc

    vmem = pl.BlockSpec(memory_space=pltpu.VMEM)
    return pl.pallas_call(
        body, name="sum_small_grads", in_specs=[vmem], out_specs=vmem,
        out_shape=jax.ShapeDtypeStruct(slabs.shape[1:], F32), compiler_params=_params(),
    )(slabs)


def _to_bf16(a, name, layers=None):
    first, stop = layers or (0, a.shape[0])

    def body(a_ref, o_ref):
        o_ref[...] = a_ref[...].astype(BF16)

    block = (None,) + a.shape[1:]
    return pl.pallas_call(
        body, name=name, grid=(stop - first,), in_specs=[pl.BlockSpec(block, lambda l: (first + l, 0, 0))],
        out_specs=pl.BlockSpec(block, lambda l: (l, 0, 0)),
        out_shape=jax.ShapeDtypeStruct((stop - first,) + a.shape[1:], BF16),
        compiler_params=_params(dimension_semantics=("arbitrary",)),
    )(a)


def kernel(x, c, w_ada, b_ada, g_pre, w_in, w_conv, w_pool, pool_scale, w_out, g_post, loss_target, m_w_ada, m_b_ada, m_g_pre, m_w_in, m_w_conv, m_w_pool, m_pool_scale, m_w_out, m_g_post, v_w_ada, v_b_ada, v_g_pre, v_w_in, v_w_conv, v_w_pool, v_pool_scale, v_w_out, v_g_post):
    mx, my, mc = _mesh_position()
    me = _block_id(mx, my, mc)
    chip = (2 * mx + my).astype(jnp.int32).reshape(1)
    core = mc.astype(jnp.int32).reshape(1)
    x0 = x[0]
    target = loss_target[0]
    conv_shard = w_conv.shape[-1]

    own_small = jnp.concatenate([c, w_conv.reshape(1, DEPTH * 3 * conv_shard)], axis=1)
    all_small = _all_gather_small(own_small, "all_gather_c_w_conv")[:, 0, :]
    gathers = [_gather_weights_start(
        0, _to_bf16(w_in, "cast_w_in_0", (0, 1)), _to_bf16(w_out, "cast_w_out_0", (0, 1)), [all_small])]
    c_all = all_small[:, :D_MODEL]
    w_conv_full = all_small[:, D_MODEL:].reshape(N_DEV, DEPTH, 3, conv_shard).transpose(1, 2, 0, 3).reshape(
        DEPTH, 3, CONV_W)
    cps = jnp.concatenate([w_conv_full, pool_scale[:, None], jnp.zeros((DEPTH, 4, CONV_W), F32)], axis=1)

    c_act, pieces = _modulation_columns(c_all, w_ada)
    upper = (1, DEPTH)
    upper_shards = [_to_bf16(w_in, "cast_w_in_1", upper), _to_bf16(w_out, "cast_w_out_1", upper)]
    wpool_b = _to_bf16(w_pool.reshape(DEPTH, POOL_ROWS, GROUP_D), "cast_w_pool").reshape(w_pool.shape)
    mod_all = _all_gather_small(
        pieces, "all_gather_modulation", [gathers[0][1][0], *upper_shards, wpool_b])
    mod_mine = lax.dynamic_index_in_dim(mod_all, me, axis=1, keepdims=False)
    mod = mod_mine.reshape(N_DEV, DEPTH, W_IN_SHARD).transpose(1, 0, 2).reshape(DEPTH, 3 * D_MODEL) + b_ada
    zeros_d = jnp.zeros((DEPTH, 3, D_MODEL), F32)
    vec = jnp.concatenate([mod.reshape(DEPTH, 3, D_MODEL), g_pre[:, None], g_post[:, None], zeros_d], axis=1)

    gathers.append(_gather_weights_start(1, *upper_shards, [mod_all]))
    gathers = [(first_sems, shards, landing[k], k) for first_sems, shards, landing in gathers
               for k in range(len(landing))]

    xs, kept, wins, wouts = [x0], [], [], []
    for l in range(DEPTH):
        first_sems, shards, zones, index = gathers[l]
        passed_sems, zones = _gather_weights_pass_on(
            l, index, first_sems[1], zones, [vec, cps, wpool_b, gathers[-1][1][0]] if l == 0 else [xs[-1]])
        win, wout = _gather_weights_finish(l, index, first_sems, passed_sems, shards, zones)
        x_next, *for_backward = _forward_layer(
            l, xs[-1], vec, cps, wpool_b, win, wout, target if l == DEPTH - 1 else None)
        xs.append(x_next)
        kept.append(for_backward[:6])
        wins.append(win)
        wouts.append(wout)
    dx, loss_tile = xs[DEPTH], for_backward[6]

    slab_rows = [None] * DEPTH
    partials = received = None
    in_flight = []

    def scatter(layer, grads, from_sibling, after):
        nonlocal partials, received
        partials = _add_sibling_blocks(
            f"grad_add_sibling_{layer}", layer, grads, from_sibling, core, partials, ALL_KINDS)
        chips = _chips_exchange(layer, partials, received, ALL_KINDS)
        sems, partials, received, token = _start_exchange(chips, f"grad_chips_start_{layer}", after)
        in_flight.append((chips, sems, layer))
        return token

    grads_above = None
    for l in reversed(range(DEPTH)):
        y, h_t, ycat_t, uc, fa, fp = kept[l]
        dx, dproj, dy, gwpool, dcps, dvec = _backward_layer(
            l, dx, y, xs[l], uc, fa, fp, vec, cps, wpool_b, wouts[l], wins[l])
        slab_rows[l] = jnp.concatenate(
            [dvec[0], dvec[1], dvec[2], dvec[3], dvec[4], dcps[3], dcps[0], dcps[1], dcps[2],
             loss_tile[0] if l == 0 else jnp.zeros((LANES,), F32)])
        after = []
        if l == 0:
            gather_small = _all_gather_exchange(jnp.stack(slab_rows))
            sems_s, slab, slabs, token = _start_exchange(gather_small, "all_gather_small_grads_start")
            after = [token]
        hosted = _sibling_exchange(grads_above, ALL_KINDS) if grads_above is not None else None
        gwin, gwout, *from_sibling = _weight_grads(l, h_t, dproj, ycat_t, dy, hosted, after)
        if l == 0:
            _, (slabs,) = _finish_exchange(
                gather_small, "all_gather_small_grads_finish", sems_s, slab, slabs, [gwin])
        if grads_above is not None:
            scatter(l + 1, grads_above, from_sibling, [])
        grads_above = [gwin, gwout, gwpool.reshape(POOL_ROWS, GROUP_D)]
        if l <= 1:
            from_sibling = _run_exchange(_sibling_exchange(grads_above, ALL_KINDS), f"grad_exchange_sibling_{l}")
            token = scatter(l, grads_above, from_sibling, [slabs] if l == 0 else [])
            grads_above = None
    grad_x = dx[None]
    chips_0, sems_0, _ = in_flight.pop()

    total = _sum_sources(slabs)
    loss = total[0, SLAB_COLS]
    o = 3 * D_MODEL
    g_b_ada = total[:, :o]
    g_g_pre = total[:, o:o + D_MODEL]
    g_g_post = total[:, o + D_MODEL:o + 2 * D_MODEL]
    g_pool_scale = total[:, o + 2 * D_MODEL:o + 2 * D_MODEL + POOL_W]
    g_conv_full = total[:, o + 2 * D_MODEL + POOL_W:SLAB_COLS].reshape(DEPTH, 3, CONV_W)
    g_w_conv = lax.dynamic_slice_in_dim(g_conv_full, me * conv_shard, conv_shard, axis=2)

    after = [token]
    for chips, sems, l in in_flight:
        partials, received = _finish_exchange(chips, f"grad_chips_finish_{l}", sems, partials, received, after)
        after = []
    upper = (1, DEPTH)
    w_in_upper = _adamw_reduced(
        "adamw_w_in_upper", w_in, m_w_in, v_w_in, partials[0], received[0], chip, ROW_TILE, upper, None)
    w_out_upper = _adamw_reduced(
        "adamw_w_out_upper", w_out, m_w_out, v_w_out, partials[1], received[1], chip, W_OUT_SHARD, upper, None)
    dmod_all = slabs[:, :, :o].reshape(N_DEV, DEPTH, N_DEV, W_IN_SHARD)
    dmod_cols = lax.dynamic_index_in_dim(dmod_all, me, axis=2, keepdims=False).transpose(1, 0, 2) + token[0, 0]
    g_w_ada, d_w_ada, nm_w_ada, nv_w_ada = _adamw_w_ada(w_ada, m_w_ada, v_w_ada, c_act.T, dmod_cols)

    partials, received = _finish_exchange(
        chips_0, "grad_chips_finish_0", sems_0, partials, received, [nv_w_ada, w_in_upper[3], w_out_upper[3]])
    gather_pool = _all_gather_exchange(_reduce_w_pool(partials[2], received[2], chip))
    sems_p, pool_rows, pool_landing, token_p = _start_exchange(gather_pool, "all_gather_grad_w_pool_start")
    g_w_in, d_w_in, nm_w_in, nv_w_in = _adamw_reduced(
        "adamw_w_in_0", w_in, m_w_in, v_w_in, partials[0], received[0], chip, ROW_TILE, (0, 1), w_in_upper)
    g_w_out, d_w_out, nm_w_out, nv_w_out = _adamw_reduced(
        "adamw_w_out_0", w_out, m_w_out, v_w_out, partials[1], received[1], chip, W_OUT_SHARD, (0, 1), w_out_upper)
    _, (g_pool_all,) = _finish_exchange(
        gather_pool, "all_gather_grad_w_pool_finish", sems_p, pool_rows, pool_landing, [nv_w_in, nv_w_out])
    g_w_pool = g_pool_all.reshape(N_DEV, DEPTH, POOL_SHARD, GROUP_D).transpose(1, 0, 2, 3).reshape(w_pool.shape)

    flat2 = lambda a: a.reshape(-1, a.shape[-1])
    small = _adamw_small([
        (b_ada, g_b_ada, m_b_ada, v_b_ada),
        (g_pre, g_g_pre, m_g_pre, v_g_pre),
        (flat2(w_conv), flat2(g_w_conv), flat2(m_w_conv), flat2(v_w_conv)),
        (flat2(w_pool), flat2(g_w_pool), flat2(m_w_pool), flat2(v_w_pool)),
        (pool_scale, g_pool_scale, m_pool_scale, v_pool_scale),
        (g_post, g_g_post, m_g_post, v_g_post),
    ])
    (d_b_ada, nm_b_ada, nv_b_ada), (d_g_pre, nm_g_pre, nv_g_pre), conv_upd, pool_upd, \
        (d_ps, nm_ps, nv_ps), (d_g_post, nm_g_post, nv_g_post) = small
    d_w_conv, nm_w_conv, nv_w_conv = (a.reshape(w_conv.shape) for a in conv_upd)
    d_w_pool, nm_w_pool, nv_w_pool = (a.reshape(w_pool.shape) for a in pool_upd)

    return (loss, grad_x,
            g_w_ada, g_b_ada, g_g_pre, g_w_in, g_w_conv, g_w_pool, g_pool_scale, g_w_out, g_g_post,
            d_w_ada, d_b_ada, d_g_pre, d_w_in, d_w_conv, d_w_pool, d_ps, d_w_out, d_g_post,
            nm_w_ada, nm_b_ada, nm_g_pre, nm_w_in, nm_w_conv, nm_w_pool, nm_ps, nm_w_out, nm_g_post,
            nv_w_ada, nv_b_ada, nv_g_pre, nv_w_in, nv_w_conv, nv_w_pool, nv_ps, nv_w_out, nv_g_post)
```

```python
import jax
import jax.numpy as jnp
from jax import lax
from jax.experimental import pallas as pl
from jax.experimental.pallas import tpu as pltpu

F32 = jnp.float32
BF16 = jnp.bfloat16

D_MODEL = 1024
DEPTH = 4
CONV_W = 512
POOL_W = 512
POOL_WINDOWS = (2, 4, 8, 16)
GROUP_D = 128
IN_COLS = 4 * CONV_W + 2 * POOL_W
NORM_EPS = 1e-6

ADAM_LR = 0.001
ADAM_B1 = 0.9
ADAM_B2 = 0.999
ADAM_EPS = 1e-08
ADAM_WD = 0.01
ADAM_STEP = 10

N_DEV = 8
N_CHIP = 4
N_OTHER_CHIPS = N_CHIP - 1
MESH = pl.DeviceIdType.MESH
W_IN_SHARD = IN_COLS // N_DEV
W_OUT_SHARD = D_MODEL // N_DEV
POOL_ROWS = len(POOL_WINDOWS) * GROUP_D
POOL_SHARD = POOL_ROWS // N_DEV

SUBLANES = 8
LANES = 128
VMEM_LIMIT_BYTES = 56 * 1024 * 1024
ROW_TILE = 512
BWD_TILE = 256
GWIN_COLS = 768
GWOUT_COLS = 512
POOL_HALO = 16
CONV_HALO = SUBLANES

SLAB_COLS = 3 * D_MODEL + D_MODEL + D_MODEL + POOL_W + 3 * CONV_W

HBM = pl.BlockSpec(memory_space=pl.ANY)


def _params(**kw):
    return pltpu.CompilerParams(vmem_limit_bytes=VMEM_LIMIT_BYTES, **kw)


def _sigmoid(v):
    return 1.0 / (1.0 + jnp.exp(-v))


def _dot(a, b):
    return jnp.dot(a, b, preferred_element_type=F32)


def _dot_tn(a, b):
    return lax.dot_general(a, b, (((0,), (0,)), ((), ())), preferred_element_type=F32)


def _dot_nt(a, b):
    return lax.dot_general(a, b, (((1,), (1,)), ((), ())), preferred_element_type=F32)


def _rows_from_before(v, k):
    return pltpu.roll(v, k, 0)


def _rows_from_after(v, k):
    return pltpu.roll(v, v.shape[0] - k, 0)


def _window_counts(t0, rows):
    return (lax.broadcasted_iota(jnp.int32, (rows, 1), 0) + (t0 + 1)).astype(F32)


def _split_proj(p32):
    cw = CONV_W
    return (p32[:, 0 * cw:1 * cw], p32[:, 1 * cw:2 * cw], p32[:, 2 * cw:3 * cw], p32[:, 3 * cw:4 * cw],
            p32[:, 4 * cw:4 * cw + POOL_W], p32[:, 4 * cw + POOL_W:])


def _layer_spec(shape, layer):
    nd = len(shape)
    return pl.BlockSpec((None,) + tuple(shape[1:]), lambda i, _l=layer, _n=nd: (_l,) + (0,) * (_n - 1))


def _whole_spec(shape):
    return pl.BlockSpec(tuple(shape), lambda i, _n=len(shape): (0,) * _n, pipeline_mode=pl.Buffered(1))


def _mesh_position():
    return lax.axis_index("x"), lax.axis_index("y"), lax.axis_index("c")


def _block_id(x, y, c):
    return 4 * x + 2 * y + c


def _other_chips(x, y):
    return [(x ^ 1, y), (x, y ^ 1), (x ^ 1, y ^ 1)]


def _col_block(ref, blk):
    return ref.at[:, pl.ds(pl.multiple_of(blk * W_IN_SHARD, LANES), W_IN_SHARD)]


def _row_block(rows):
    def block(ref, blk):
        return ref.at[pl.ds(pl.multiple_of(blk * rows, rows), rows), :]
    return block


_BLOCK_OF = (_col_block, _row_block(W_OUT_SHARD), _row_block(POOL_SHARD))
_BLOCK_SHAPES = ((D_MODEL, W_IN_SHARD), (W_OUT_SHARD, D_MODEL), (POOL_SHARD, GROUP_D))


class _Exchange:
    def __init__(self, inputs, out_shapes, aliases, sem_shapes, make):
        self.inputs, self.out_shapes, self.aliases, self.sem_shapes, self.make = (
            list(inputs), list(out_shapes), dict(aliases), list(sem_shapes), make)


def _run_exchange(exchange, name):
    n_in, n_out = len(exchange.inputs), len(exchange.out_shapes)

    def body(*refs):
        start, finish = exchange.make(refs[:n_in], refs[n_in:n_in + n_out], refs[n_in + n_out:])
        start()
        finish()

    return pl.pallas_call(
        body, name=name, in_specs=[HBM] * n_in, out_specs=[HBM] * n_out, out_shape=exchange.out_shapes,
        scratch_shapes=exchange.sem_shapes, input_output_aliases=exchange.aliases, compiler_params=_params(),
    )(*exchange.inputs)


_SEM = pl.BlockSpec(memory_space=pltpu.SEMAPHORE)
_DATAFLOW = pltpu.SideEffectType.DATAFLOW_SIDE_EFFECTING


def _start_exchange(exchange, name, after=()):
    n_in, n_out, n_sem = len(exchange.inputs), len(exchange.out_shapes), len(exchange.sem_shapes)
    sources = [i for i in range(n_in) if i not in exchange.aliases]
    aliases = {i: n_sem + k for k, i in enumerate(sources)}
    aliases.update({i: n_sem + len(sources) + o for i, o in exchange.aliases.items()})

    def body(*refs):
        in_refs = refs[:n_in]
        outs = refs[n_in + len(after):]
        sems = outs[:n_sem]
        out_refs = outs[n_sem + len(sources):n_sem + len(sources) + n_out]
        exchange.make(in_refs, out_refs, sems)[0]()
        refs[-1][...] = jnp.zeros_like(refs[-1])

    outs = pl.pallas_call(
        body, name=name, in_specs=[HBM] * (n_in + len(after)),
        out_specs=[_SEM] * n_sem + [HBM] * (len(sources) + n_out) + [pl.BlockSpec(memory_space=pltpu.VMEM)],
        out_shape=(exchange.sem_shapes + [pltpu.HBM(exchange.inputs[i].shape, exchange.inputs[i].dtype) for i in sources]
                   + [pltpu.HBM(s.shape, s.dtype) for s in exchange.out_shapes]
                   + [jax.ShapeDtypeStruct((SUBLANES, LANES), F32)]),
        input_output_aliases=aliases, compiler_params=_params(has_side_effects=_DATAFLOW),
    )(*exchange.inputs, *after)
    return outs[:n_sem], outs[n_sem:n_sem + len(sources)], outs[n_sem + len(sources):-1], outs[-1]


def _finish_exchange(exchange, name, sems, sources, landing, after):
    n_src, n_out, n_sem = len(sources), len(landing), len(sems)
    n_in = len(exchange.inputs)
    source_at = [i for i in range(n_in) if i not in exchange.aliases]

    def body(*refs):
        src_refs, out_refs = refs[:n_src], refs[n_src:n_src + n_out]
        sem_refs = refs[n_src + n_out:n_src + n_out + n_sem]
        in_refs = [None] * n_in
        for k, i in enumerate(source_at):
            in_refs[i] = src_refs[k]
        for i, o in exchange.aliases.items():
            in_refs[i] = out_refs[o]
        exchange.make(in_refs, out_refs, sem_refs)[1]()

    arrays = list(sources) + list(landing)
    outs = pl.pallas_call(
        body, name=name, in_specs=[HBM] * len(arrays) + [_SEM] * n_sem + [HBM] * len(after),
        out_specs=[HBM] * len(arrays), out_shape=[pltpu.HBM(a.shape, a.dtype) for a in arrays],
        input_output_aliases={i: i for i in range(len(arrays))}, compiler_params=_params(has_side_effects=_DATAFLOW),
    )(*arrays, *sems, *after)
    return outs[:n_src], outs[n_src:]


N_GATHERED = 2
FIRST_COPIES = 1 + N_OTHER_CHIPS
_GATHERED_SHAPES = ((D_MODEL, IN_COLS), (D_MODEL, D_MODEL))


def _first_sem(layer, a, k):
    return (layer * N_GATHERED + a) * FIRST_COPIES + k


def _gather_copy(window_of, full_ref, blk, send_sem, recv_sem, to, src=None):
    window = window_of(full_ref, blk)
    return pltpu.make_async_remote_copy(
        src_ref=window if src is None else src, dst_ref=window, send_sem=send_sem, recv_sem=recv_sem,
        device_id=to, device_id_type=MESH)


def _first_copies(layer, shard_refs, full_refs, send_sems, recv_sems, local_sems):
    x, y, c = _mesh_position()
    me = _block_id(x, y, c)
    own, remote = [], []
    for a in range(N_GATHERED):
        shard = shard_refs[a].at[layer]
        own.append(pltpu.make_async_copy(
            shard, _BLOCK_OF[a](full_refs[a], me), local_sems.at[layer * N_GATHERED + a]))
        targets = [(x, y, 1 - c)] + [(*chip, c) for chip in _other_chips(x, y)]
        remote += [_gather_copy(_BLOCK_OF[a], full_refs[a], me, send_sems.at[_first_sem(layer, a, k)],
                                recv_sems.at[_first_sem(layer, a, k)], to, src=shard)
                   for k, to in enumerate(targets)]
    return own, remote


def _gather_weights_start(first_layer, win_shards, wout_shards, after):
    n_layers = win_shards.shape[0]
    n_first = n_layers * N_GATHERED * FIRST_COPIES
    sem_shapes = [pltpu.SemaphoreType.DMA((n_first,)), pltpu.SemaphoreType.DMA((n_first,)),
                  pltpu.SemaphoreType.DMA((n_layers * N_GATHERED,))]
    shards = [win_shards, wout_shards]

    def body(win_sh, wout_sh, *rest):
        send_sems, recv_sems, local_sems, win_thru, wout_thru, *landing = rest[len(after):]
        for layer in range(n_layers):
            own, remote = _first_copies(layer, (win_sh, wout_sh), landing[N_GATHERED * layer:N_GATHERED * (layer + 1)],
                                        send_sems, recv_sems, local_sems)
            for cp in own + remote:
                cp.start()

    outs = pl.pallas_call(
        body, name=f"all_gather_weights_start_{first_layer}", in_specs=[HBM] * (2 + len(after)),
        out_specs=[_SEM] * 3 + [HBM] * (2 + n_layers * N_GATHERED),
        out_shape=(sem_shapes + [pltpu.HBM(s.shape, s.dtype) for s in shards]
                   + [pltpu.HBM(s, BF16) for _ in range(n_layers) for s in _GATHERED_SHAPES]),
        input_output_aliases={0: 3, 1: 4}, compiler_params=_params(has_side_effects=_DATAFLOW),
    )(*shards, *after)
    landing = outs[5:]
    return outs[:3], outs[3:5], [landing[N_GATHERED * l:N_GATHERED * (l + 1)] for l in range(n_layers)]


def _passed_on_copies(full_refs, send_sems, recv_sems, core_of_block):
    x, y, c = _mesh_position()
    return [_gather_copy(_BLOCK_OF[a], full_refs[a], _block_id(*chip, core_of_block),
                         send_sems.at[a * N_OTHER_CHIPS + j], recv_sems.at[a * N_OTHER_CHIPS + j], (x, y, 1 - c))
            for a in range(N_GATHERED) for j, chip in enumerate(_other_chips(x, y))]


def _gather_weights_pass_on(layer, index, first_recv_sems, landing, after):
    n = N_GATHERED * N_OTHER_CHIPS

    def body(win_ref, wout_ref, first_recv, *rest):
        send_sems, recv_sems = rest[len(after):len(after) + 2]
        x, y, c = _mesh_position()
        full_refs = (win_ref, wout_ref)
        passed = _passed_on_copies(full_refs, send_sems, recv_sems, c)
        for a in range(N_GATHERED):
            for j, chip in enumerate(_other_chips(x, y)):
                sem = _first_sem(index, a, 1 + j)
                _gather_copy(_BLOCK_OF[a], full_refs[a], _block_id(*chip, c), first_recv.at[sem], first_recv.at[sem],
                             (x, y, c)).wait_recv()
                passed[a * N_OTHER_CHIPS + j].start()

    outs = pl.pallas_call(
        body, name=f"all_gather_weights_pass_on_{layer}", in_specs=[HBM] * N_GATHERED + [_SEM] + [HBM] * len(after),
        out_specs=[_SEM] * 2 + [HBM] * N_GATHERED,
        out_shape=[pltpu.SemaphoreType.DMA((n,)), pltpu.SemaphoreType.DMA((n,))]
        + [pltpu.HBM(a.shape, a.dtype) for a in landing],
        input_output_aliases={a: 2 + a for a in range(N_GATHERED)},
        compiler_params=_params(has_side_effects=_DATAFLOW),
    )(*landing, first_recv_sems, *after)
    return outs[:2], outs[2:]


def _gather_weights_finish(layer, index, first_sems, passed_sems, shards, landing):
    def body(win_ref, wout_ref, first_send, first_recv, local_sems, passed_send, passed_recv, win_sh, wout_sh, *thru):
        x, y, c = _mesh_position()
        full_refs = (win_ref, wout_ref)
        own, remote = _first_copies(index, (win_sh, wout_sh), full_refs, first_send, first_recv, local_sems)
        for a in range(N_GATHERED):
            sem = _first_sem(index, a, 0)
            _gather_copy(_BLOCK_OF[a], full_refs[a], _block_id(x, y, 1 - c), first_recv.at[sem], first_recv.at[sem],
                         (x, y, c)).wait_recv()
        for cp in _passed_on_copies(full_refs, passed_send, passed_recv, 1 - c):
            cp.wait_recv()
        for cp in remote + _passed_on_copies(full_refs, passed_send, passed_recv, c):
            cp.wait_send()
        for cp in own:
            cp.wait()

    return pl.pallas_call(
        body, name=f"all_gather_weights_finish_{layer}", in_specs=[HBM] * N_GATHERED + [_SEM] * 5 + [HBM] * 2,
        out_specs=[HBM] * N_GATHERED, out_shape=[pltpu.HBM(a.shape, a.dtype) for a in landing],
        input_output_aliases={a: a for a in range(N_GATHERED)}, compiler_params=_params(has_side_effects=_DATAFLOW),
    )(*landing, *first_sems, *passed_sems, *shards)


ALL_KINDS = (0, 1, 2)


def _sibling_exchange(grads, kinds):
    n_arr = len(grads)

    def make(in_refs, out_refs, sems):
        send_sems, recv_sems = sems
        x, y, c = _mesh_position()
        copies = [pltpu.make_async_remote_copy(
            src_ref=_BLOCK_OF[kinds[a]](in_refs[a], 2 * q + (1 - c)), dst_ref=out_refs[a].at[q],
            send_sem=send_sems.at[a * N_CHIP + q], recv_sem=recv_sems.at[a * N_CHIP + q],
            device_id=(x, y, 1 - c), device_id_type=MESH)
            for a in range(n_arr) for q in range(N_CHIP)]

        def start():
            for cp in copies:
                cp.start()

        def finish():
            for cp in copies:
                cp.wait_recv()
            for cp in copies:
                cp.wait_send()

        return start, finish

    return _Exchange(
        grads, [jax.ShapeDtypeStruct((N_CHIP,) + _BLOCK_SHAPES[k], BF16) for k in kinds], {},
        [pltpu.SemaphoreType.DMA((n_arr * N_CHIP,)), pltpu.SemaphoreType.DMA((n_arr * N_CHIP,))], make)


def _chips_exchange(layer, partials, received, kinds):
    n_arr = len(partials)

    def make(in_refs, out_refs, sems):
        send_sems, recv_sems = sems
        x, y, c = _mesh_position()
        copies = [pltpu.make_async_remote_copy(
            src_ref=in_refs[a].at[2 * qx + qy, layer], dst_ref=out_refs[a].at[j, layer],
            send_sem=send_sems.at[a * N_OTHER_CHIPS + j], recv_sem=recv_sems.at[a * N_OTHER_CHIPS + j],
            device_id=(qx, qy, c), device_id_type=MESH)
            for a in range(n_arr) for j, (qx, qy) in enumerate(_other_chips(x, y))]

        def start():
            for cp in copies:
                cp.start()

        def finish():
            for cp in copies:
                cp.wait_recv()
            for cp in copies:
                cp.wait_send()

        return start, finish

    inputs = list(partials)
    aliases = {}
    if received is not None:
        inputs += list(received)
        aliases = {n_arr + a: a for a in range(n_arr)}
    return _Exchange(
        inputs, [jax.ShapeDtypeStruct((N_OTHER_CHIPS, DEPTH) + _BLOCK_SHAPES[k], BF16) for k in kinds], aliases,
        [pltpu.SemaphoreType.DMA((n_arr * N_OTHER_CHIPS,)), pltpu.SemaphoreType.DMA((n_arr * N_OTHER_CHIPS,))], make)


def _all_gather_exchange(v):
    def make(in_refs, out_refs, sems):
        send_sems, recv_sems, local_sem = sems
        x, y, c = _mesh_position()
        me = _block_id(x, y, c)
        own = pltpu.make_async_copy(in_refs[0], out_refs[0].at[me], local_sem.at[0])
        sends, arrivals = [], []
        for k in range(1, N_DEV):
            px, py, pc = x ^ ((k >> 2) & 1), y ^ ((k >> 1) & 1), c ^ (k & 1)
            sends.append(pltpu.make_async_remote_copy(
                src_ref=in_refs[0], dst_ref=out_refs[0].at[me], send_sem=send_sems.at[k - 1],
                recv_sem=recv_sems.at[k - 1], device_id=(px, py, pc), device_id_type=MESH))
            arrivals.append(pltpu.make_async_remote_copy(
                src_ref=in_refs[0], dst_ref=out_refs[0].at[_block_id(px, py, pc)], send_sem=send_sems.at[k - 1],
                recv_sem=recv_sems.at[k - 1], device_id=(x, y, c), device_id_type=MESH))

        def start():
            for cp in [own] + sends:
                cp.start()

        def finish():
            for cp in arrivals:
                cp.wait_recv()
            for cp in sends:
                cp.wait_send()
            own.wait()

        return start, finish

    return _Exchange(
        [v], [jax.ShapeDtypeStruct((N_DEV,) + v.shape, v.dtype)], {},
        [pltpu.SemaphoreType.DMA((N_DEV - 1,)), pltpu.SemaphoreType.DMA((N_DEV - 1,)),
         pltpu.SemaphoreType.DMA((1,))], make)


def _host(exchange, args, in_specs, out_shape, out_specs, scratch):
    n_own = (len(args), len(out_shape), len(scratch))
    if exchange is None:
        return {}, lambda refs: (refs, None)
    n_ex = (len(exchange.inputs), len(exchange.out_shapes), len(exchange.sem_shapes))
    aliases = {n_own[0] + i: n_own[1] + o for i, o in exchange.aliases.items()}
    args += exchange.inputs
    in_specs += [HBM] * n_ex[0]
    out_shape += exchange.out_shapes
    out_specs += [HBM] * n_ex[1]
    scratch += exchange.sem_shapes

    def split(refs):
        own, theirs, at = [], [], 0
        for mine, ex in zip(n_own, n_ex):
            own += refs[at:at + mine]
            theirs.append(refs[at + mine:at + mine + ex])
            at += mine + ex
        return own, exchange.make(*theirs)

    return aliases, split


def _all_gather_small(v, name, after=()):
    vmem = pl.BlockSpec(memory_space=pltpu.VMEM)

    def body(v_ref, *rest):
        out_ref, send_sems, recv_sems = rest[len(after):]
        x, y, c = _mesh_position()
        me = _block_id(x, y, c)
        out_ref[me] = v_ref[...]
        sends = []
        for k in range(1, N_DEV):
            px, py, pc = x ^ ((k >> 2) & 1), y ^ ((k >> 1) & 1), c ^ (k & 1)
            send = pltpu.make_async_remote_copy(
                src_ref=v_ref, dst_ref=out_ref.at[me], send_sem=send_sems.at[k - 1], recv_sem=recv_sems.at[k - 1],
                device_id=(px, py, pc), device_id_type=MESH)
            send.start()
            sends.append((send, _block_id(px, py, pc)))
        for k, (send, peer) in enumerate(sends):
            pltpu.make_async_remote_copy(
                src_ref=v_ref, dst_ref=out_ref.at[peer], send_sem=send_sems.at[k], recv_sem=recv_sems.at[k],
                device_id=(x, y, c), device_id_type=MESH).wait_recv()
        for send, _ in sends:
            send.wait_send()

    return pl.pallas_call(
        body, name=name, in_specs=[vmem] + [HBM] * len(after), out_specs=vmem,
        out_shape=jax.ShapeDtypeStruct((N_DEV,) + v.shape, v.dtype),
        scratch_shapes=[pltpu.SemaphoreType.DMA((N_DEV - 1,)), pltpu.SemaphoreType.DMA((N_DEV - 1,))],
        compiler_params=_params(),
    )(v, *after)


def _forward_layer(layer, x, vec, cps, wpool, win, wout, target=None):
    t_len = x.shape[0]
    n_tiles = t_len // ROW_TILE
    row = lambda cols: pl.BlockSpec((ROW_TILE, cols), lambda i: (i, 0))
    widths = (D_MODEL, 2 * CONV_W, 3 * CONV_W, 4 * POOL_W)
    head = target is not None

    def body(x_ref, vec_ref, cps_ref, wpool_ref, win_ref, wout_ref, *rest):
        target_ref = rest[0] if head else None
        xo_ref, y_ref, h_ref, ycat_ref, uc_ref, fa_ref, fp_ref = rest[head:head + 7]
        loss_ref = rest[head + 7] if head else None
        zc_ref, pc_ref = rest[-2:]
        i = pl.program_id(0)

        @pl.when(i == 0)
        def _():
            zc_ref[...] = jnp.zeros_like(zc_ref)
            pc_ref[...] = jnp.zeros_like(pc_ref)
            if head:
                loss_ref[...] = jnp.zeros_like(loss_ref)

        x_t = x_ref[...]
        shift, scale, gate = vec_ref[0:1, :], vec_ref[1:2, :], vec_ref[2:3, :]
        g_pre, g_post = vec_ref[3:4, :], vec_ref[4:5, :]
        w0, w1, w2, ps = cps_ref[0:1, :], cps_ref[1:2, :], cps_ref[2:3, :], cps_ref[3:4, :]
        rx = lax.rsqrt(jnp.mean(x_t * x_t, axis=-1, keepdims=True) + NORM_EPS)
        h = (x_t * rx) * g_pre * (1.0 + scale) + shift
        h_ref[...] = h.astype(BF16)
        proj = _dot(h.astype(BF16), win_ref[...])
        u_a, b_a, c_a, g_a, u_p, g_p = _split_proj(proj)
        uc_ref[...] = jnp.concatenate([u_a, c_a], axis=1).astype(BF16)

        z = c_a * u_a
        zcat = jnp.concatenate([zc_ref[...], z], axis=0)
        zc_ref[...] = z[ROW_TILE - CONV_HALO:]
        conv = (w0 * _rows_from_before(zcat, 2)[CONV_HALO:] + w1 * _rows_from_before(zcat, 1)[CONV_HALO:] + w2 * z)
        sig_a = _sigmoid(g_a)
        silu_a = g_a * sig_a
        b_conv = b_a * conv
        y_a = b_conv * silu_a
        fa_ref[...] = jnp.concatenate(
            [silu_a * conv, silu_a * b_a, b_conv * (sig_a + silu_a * (1.0 - sig_a))], axis=1).astype(BF16)

        pcat = jnp.concatenate([pc_ref[...], u_p], axis=0)
        pc_ref[...] = u_p[ROW_TILE - POOL_HALO:]
        counts = _window_counts(i * ROW_TILE, ROW_TILE)
        pooled, mixed = [], []
        for g, w in enumerate(POOL_WINDOWS):
            cols = slice(g * GROUP_D, (g + 1) * GROUP_D)
            s = pcat[:, cols]
            step = 1
            while step < w:
                s = s + _rows_from_before(s, step)
                step *= 2
            pooled_g = (s[POOL_HALO:] * (1.0 / jnp.minimum(counts, float(w))) - u_p[:, cols]).astype(BF16)
            pooled.append(pooled_g)
            mixed.append(_dot(pooled_g, wpool_ref[g]))
        mixed = jnp.concatenate(mixed, axis=1)
        sig_p = _sigmoid(g_p)
        silu_p = g_p * sig_p
        mixed_ps = mixed * ps
        y_p = mixed_ps * silu_p
        fp_ref[...] = jnp.concatenate(
            [(ps * silu_p).astype(BF16), (mixed_ps * (sig_p + silu_p * (1.0 - sig_p))).astype(BF16),
             (silu_p * mixed).astype(BF16)] + pooled, axis=1)

        ycat = jnp.concatenate([y_a, y_p], axis=1)
        ycat_ref[...] = ycat.astype(BF16)
        y_b = _dot(ycat.astype(BF16), wout_ref[...]).astype(BF16)
        y_ref[...] = y_b
        y_t = y_b.astype(F32)
        ry = lax.rsqrt(jnp.mean(y_t * y_t, axis=-1, keepdims=True) + NORM_EPS)
        x_next = x_t + gate * (y_t * ry * g_post)
        if head:
            err = x_next - target_ref[...]
            xo_ref[...] = err * (1.0 / D_MODEL)
            loss_ref[...] += jnp.sum(err * err) * (0.5 / D_MODEL)
        else:
            xo_ref[...] = x_next

    tile = (SUBLANES, LANES)
    return pl.pallas_call(
        body, name=f"forward_layer_{layer}", grid=(n_tiles,),
        in_specs=[row(D_MODEL), _layer_spec(vec.shape, layer), _layer_spec(cps.shape, layer),
                  _layer_spec(wpool.shape, layer), _whole_spec(win.shape), _whole_spec(wout.shape)]
        + [row(D_MODEL)] * head,
        out_specs=[row(D_MODEL), row(D_MODEL), row(D_MODEL), row(D_MODEL)] + [row(w) for w in widths[1:]]
        + [_whole_spec(tile)] * head,
        out_shape=[jax.ShapeDtypeStruct((t_len, D_MODEL), F32), jax.ShapeDtypeStruct((t_len, D_MODEL), BF16),
                   jax.ShapeDtypeStruct((t_len, D_MODEL), BF16), jax.ShapeDtypeStruct((t_len, D_MODEL), BF16)]
        + [jax.ShapeDtypeStruct((t_len, w), BF16) for w in widths[1:]] + [jax.ShapeDtypeStruct(tile, F32)] * head,
        scratch_shapes=[pltpu.VMEM((CONV_HALO, CONV_W), F32), pltpu.VMEM((POOL_HALO, POOL_W), F32)],
        compiler_params=_params(dimension_semantics=("arbitrary",)),
    )(x, vec, cps, wpool, win, wout, *([target] * head))


def _backward_layer(layer, dxo, y, x, uc, fa, fp, vec, cps, wpool, wout, win):
    t_len = dxo.shape[0]
    n_tiles = t_len // BWD_TILE
    halo_per_tile = BWD_TILE // POOL_HALO
    rev = lambda cols: pl.BlockSpec((BWD_TILE, cols), lambda i: (n_tiles - 1 - i, 0))
    halo_spec = pl.BlockSpec(
        (POOL_HALO, 2 * CONV_W), lambda i: (jnp.maximum((n_tiles - 1 - i) * halo_per_tile - 1, 0), 0))
    gwpool_shape = (len(POOL_WINDOWS), GROUP_D, GROUP_D)

    def body(dxo_ref, y_ref, x_ref, uc_ref, uch_ref, fa_ref, fp_ref, vec_ref, cps_ref, wpool_ref, wout_ref, win_ref,
             dx_ref, dproj_ref, dy_ref, gwpool_ref, dcps_ref, dvec_ref, gwpool_acc, dcc_ref, qc_ref):
        i = pl.program_id(0)
        tile = n_tiles - 1 - i

        @pl.when(i == 0)
        def _():
            gwpool_acc[...] = jnp.zeros_like(gwpool_acc)
            dcps_ref[...] = jnp.zeros_like(dcps_ref)
            dvec_ref[...] = jnp.zeros_like(dvec_ref)
            dcc_ref[...] = jnp.zeros_like(dcc_ref)
            qc_ref[...] = jnp.zeros_like(qc_ref)

        shift, scale, gate = vec_ref[0:1, :], vec_ref[1:2, :], vec_ref[2:3, :]
        g_pre, g_post = vec_ref[3:4, :], vec_ref[4:5, :]
        w0, w1, w2 = cps_ref[0:1, :], cps_ref[1:2, :], cps_ref[2:3, :]

        dxo_t = dxo_ref[...]
        y_t = y_ref[...].astype(F32)
        ry = lax.rsqrt(jnp.mean(y_t * y_t, axis=-1, keepdims=True) + NORM_EPS)
        yh = y_t * ry
        dvec_ref[2:3, :] += jnp.sum(dxo_t * yh, axis=0, keepdims=True)
        dyh = dxo_t * (gate * g_post)
        dy_b = (ry * (dyh - yh * jnp.mean(dyh * yh, axis=-1, keepdims=True))).astype(BF16)
        dy_ref[...] = dy_b
        dycat = _dot_nt(dy_b, wout_ref[...])
        dy_a, dy_p = dycat[:, :CONV_W], dycat[:, CONV_W:]

        fa_t = fa_ref[...].astype(F32)
        db_a = dy_a * fa_t[:, :CONV_W]
        dconv = dy_a * fa_t[:, CONV_W:2 * CONV_W]
        dg_a = dy_a * fa_t[:, 2 * CONV_W:]
        uc_t = uc_ref[...].astype(F32)
        u_a, c_a = uc_t[:, :CONV_W], uc_t[:, CONV_W:]
        halo = jnp.where(tile > 0, uch_ref[...].astype(F32), 0.0)[POOL_HALO - CONV_HALO:]
        z = c_a * u_a
        zcat = jnp.concatenate([halo[:, CONV_W:] * halo[:, :CONV_W], z], axis=0)
        z1 = _rows_from_before(zcat, 1)[CONV_HALO:]
        z2 = _rows_from_before(zcat, 2)[CONV_HALO:]
        dccat = jnp.concatenate([dconv, dcc_ref[...]], axis=0)
        dc1 = _rows_from_after(dccat, 1)[:BWD_TILE]
        dc2 = _rows_from_after(dccat, 2)[:BWD_TILE]
        dz = w2 * dconv + w1 * dc1 + w0 * dc2
        dcc_ref[...] = dconv[:CONV_HALO]
        dcps_ref[0:1, :] += jnp.sum(dconv * z2, axis=0, keepdims=True)
        dcps_ref[1:2, :] += jnp.sum(dconv * z1, axis=0, keepdims=True)
        dcps_ref[2:3, :] += jnp.sum(dconv * z, axis=0, keepdims=True)
        du_a = dz * c_a
        dc_a = dz * u_a

        dmixed = (dy_p * fp_ref[:, :POOL_W].astype(F32)).astype(BF16)
        dg_p = dy_p * fp_ref[:, POOL_W:2 * POOL_W].astype(F32)
        dcps_ref[3:4, :] += jnp.sum(dy_p * fp_ref[:, 2 * POOL_W:3 * POOL_W].astype(F32), axis=0, keepdims=True)
        counts = _window_counts(tile * BWD_TILE, BWD_TILE)
        du_p, q_head = [], []
        for g, w in enumerate(POOL_WINDOWS):
            cols = slice(g * GROUP_D, (g + 1) * GROUP_D)
            dm_g = dmixed[:, cols]
            dpooled_g = _dot_nt(dm_g, wpool_ref[g])
            gwpool_acc[g] += _dot_tn(fp_ref[:, 3 * POOL_W + g * GROUP_D:3 * POOL_W + (g + 1) * GROUP_D], dm_g)
            q_g = dpooled_g * (1.0 / jnp.minimum(counts, float(w)))
            q_head.append(q_g[:POOL_HALO])
            s = jnp.concatenate([q_g, qc_ref[:, cols]], axis=0)
            step = 1
            while step < w:
                s = s + _rows_from_after(s, step)
                step *= 2
            du_p.append(s[:BWD_TILE] - dpooled_g)
        qc_ref[...] = jnp.concatenate(q_head, axis=1)
        dproj_b = jnp.concatenate([du_a, db_a, dc_a, dg_a] + du_p + [dg_p], axis=1).astype(BF16)
        dproj_ref[...] = dproj_b

        x_t = x_ref[...]
        rx = lax.rsqrt(jnp.mean(x_t * x_t, axis=-1, keepdims=True) + NORM_EPS)
        xn = x_t * rx
        mod_scale = 1.0 + scale
        dh = _dot_nt(dproj_b, win_ref[...])
        dvec_ref[0:1, :] += jnp.sum(dh, axis=0, keepdims=True)
        dvec_ref[1:2, :] += jnp.sum(dh * xn, axis=0, keepdims=True)
        dxn = dh * (g_pre * mod_scale)
        dx_ref[...] = dxo_t + rx * (dxn - xn * jnp.mean(dxn * xn, axis=-1, keepdims=True))

        @pl.when(i == n_tiles - 1)
        def _():
            gwpool_ref[...] = gwpool_acc[...].astype(BF16)
            sum_dh_xn, sum_dxo_yh = dvec_ref[1:2, :], dvec_ref[2:3, :]
            dvec_ref[1:2, :] = sum_dh_xn * g_pre
            dvec_ref[3:4, :] = sum_dh_xn * mod_scale
            dvec_ref[2:3, :] = sum_dxo_yh * g_post
            dvec_ref[4:5, :] = sum_dxo_yh * gate

    return pl.pallas_call(
        body, name=f"backward_layer_{layer}", grid=(n_tiles,),
        in_specs=[rev(D_MODEL), rev(D_MODEL), rev(D_MODEL), rev(2 * CONV_W), halo_spec, rev(3 * CONV_W),
                  rev(4 * POOL_W), _layer_spec(vec.shape, layer), _layer_spec(cps.shape, layer),
                  _layer_spec(wpool.shape, layer), _whole_spec(wout.shape), _whole_spec(win.shape)],
        out_specs=[rev(D_MODEL), rev(IN_COLS), rev(D_MODEL), _whole_spec(gwpool_shape),
                   _whole_spec((SUBLANES, CONV_W)), _whole_spec((SUBLANES, D_MODEL))],
        out_shape=[jax.ShapeDtypeStruct((t_len, D_MODEL), F32), jax.ShapeDtypeStruct((t_len, IN_COLS), BF16),
                   jax.ShapeDtypeStruct((t_len, D_MODEL), BF16), jax.ShapeDtypeStruct(gwpool_shape, BF16),
                   jax.ShapeDtypeStruct((SUBLANES, CONV_W), F32), jax.ShapeDtypeStruct((SUBLANES, D_MODEL), F32)],
        scratch_shapes=[pltpu.VMEM(gwpool_shape, F32), pltpu.VMEM((CONV_HALO, CONV_W), F32),
                        pltpu.VMEM((POOL_HALO, POOL_W), F32)],
        compiler_params=_params(dimension_semantics=("arbitrary",)),
    )(dxo, y, x, uc, uc, fa, fp, vec, cps, wpool, wout, win)


def _weight_grads(layer, h, dproj, ycat, dy, exchange, after=()):
    t_len = dy.shape[0]
    n_in, n_out = IN_COLS // GWIN_COLS, D_MODEL // GWOUT_COLS
    args = [h, dproj, ycat, dy, *after]
    in_specs = [_whole_spec(h.shape),
                pl.BlockSpec((t_len, GWIN_COLS), lambda s: (0, jnp.minimum(s, n_in - 1))),
                _whole_spec(ycat.shape),
                pl.BlockSpec((t_len, GWOUT_COLS), lambda s: (0, jnp.maximum(s - n_in, 0)))] + [HBM] * len(after)
    out_shape = [jax.ShapeDtypeStruct((D_MODEL, IN_COLS), BF16), jax.ShapeDtypeStruct((D_MODEL, D_MODEL), BF16)]
    out_specs = [pl.BlockSpec((D_MODEL, GWIN_COLS), lambda s: (0, jnp.minimum(s, n_in - 1))),
                 pl.BlockSpec((D_MODEL, GWOUT_COLS), lambda s: (0, jnp.maximum(s - n_in, 0)))]
    scratch = []
    aliases, split = _host(exchange, args, in_specs, out_shape, out_specs, scratch)

    def body(*refs):
        (h_ref, dproj_ref, ycat_ref, dy_ref, *_, gwin_ref, gwout_ref), hosted = split(refs)
        s = pl.program_id(0)
        if hosted is not None:
            pl.when(s == 0)(hosted[0])

        @pl.when(s < n_in)
        def _():
            gwin_ref[...] = _dot_tn(h_ref[...], dproj_ref[...]).astype(BF16)

        @pl.when(s >= n_in)
        def _():
            gwout_ref[...] = _dot_tn(ycat_ref[...], dy_ref[...]).astype(BF16)

        if hosted is not None:
            pl.when(s == n_in + n_out - 1)(hosted[1])

    return pl.pallas_call(
        body, name=f"weight_grads_{layer}", grid=(n_in + n_out,), in_specs=in_specs, out_specs=out_specs,
        out_shape=out_shape, scratch_shapes=scratch, input_output_aliases=aliases,
        compiler_params=_params(dimension_semantics=("arbitrary",)),
    )(*args)


def _add_sibling_blocks(name, layer, grads, received, core, partials, kinds):
    n_arr = len(grads)

    def body(core_ref, *refs):
        mine, theirs, outs = refs[:n_arr], refs[n_arr:2 * n_arr], refs[-n_arr:]
        for a in range(n_arr):
            outs[a][...] = (mine[a][...].astype(F32) + theirs[a][...].astype(F32)).astype(BF16)

    own_of_kind = [
        pl.BlockSpec((D_MODEL, W_IN_SHARD), lambda q, core_ref: (0, 2 * q + core_ref[0])),
        pl.BlockSpec((W_OUT_SHARD, D_MODEL), lambda q, core_ref: (2 * q + core_ref[0], 0)),
        pl.BlockSpec((POOL_SHARD, GROUP_D), lambda q, core_ref: (2 * q + core_ref[0], 0)),
    ]
    shapes = [_BLOCK_SHAPES[k] for k in kinds]
    recv_specs = [pl.BlockSpec((None,) + s, lambda q, core_ref: (q, 0, 0)) for s in shapes]
    out_specs = [pl.BlockSpec((None, None) + s, lambda q, core_ref: (q, layer, 0, 0)) for s in shapes]
    args = [core, *grads, *received]
    in_specs = [own_of_kind[k] for k in kinds] + recv_specs
    aliases = {}
    if partials is not None:
        aliases = {len(args) + a: a for a in range(n_arr)}
        args += list(partials)
        in_specs += [HBM] * n_arr
    return pl.pallas_call(
        body, name=name,
        grid_spec=pltpu.PrefetchScalarGridSpec(
            num_scalar_prefetch=1, grid=(N_CHIP,), in_specs=in_specs, out_specs=out_specs),
        out_shape=[jax.ShapeDtypeStruct((N_CHIP, DEPTH) + s, BF16) for s in shapes],
        input_output_aliases=aliases,
        compiler_params=_params(dimension_semantics=("arbitrary",)),
    )(*args)


def _modulation_columns(c_all, w_ada):
    def body(c_ref, w_ref, cact_ref, out_ref):
        c_t = c_ref[...]
        c_act = c_t * _sigmoid(c_t)
        cact_ref[...] = c_act
        out_ref[...] = jnp.dot(c_act, w_ref[...], preferred_element_type=F32, precision=lax.Precision.HIGHEST)

    return pl.pallas_call(
        body, name="modulation_columns", grid=(DEPTH,),
        in_specs=[pl.BlockSpec((N_DEV, D_MODEL), lambda l: (0, 0)),
                  pl.BlockSpec((None, D_MODEL, W_IN_SHARD), lambda l: (l, 0, 0))],
        out_specs=[pl.BlockSpec((N_DEV, D_MODEL), lambda l: (0, 0)),
                   pl.BlockSpec((N_DEV, W_IN_SHARD), lambda l: (0, l))],
        out_shape=[jax.ShapeDtypeStruct((N_DEV, D_MODEL), F32),
                   jax.ShapeDtypeStruct((N_DEV, DEPTH * W_IN_SHARD), F32)],
        compiler_params=_params(dimension_semantics=("arbitrary",)),
    )(c_all, w_ada)


def _adamw(w, g, m, v):
    m_new = ADAM_B1 * m + (1.0 - ADAM_B1) * g
    v_new = ADAM_B2 * v + (1.0 - ADAM_B2) * (g * g)
    m_hat = m_new / (1.0 - ADAM_B1 ** ADAM_STEP)
    v_hat = v_new / (1.0 - ADAM_B2 ** ADAM_STEP)
    delta = -ADAM_LR * (m_hat / (jnp.sqrt(v_hat) + ADAM_EPS) + ADAM_WD * w)
    return delta, m_new, v_new


def _adamw_w_ada(w, m, v, c_act_t, dmod_cols):
    def body(w_ref, m_ref, v_ref, ct_ref, dm_ref, g_ref, d_ref, mo_ref, vo_ref):
        g = ct_ref[:, 0:1] * dm_ref[0:1, :]
        for b in range(1, N_DEV):
            g = g + ct_ref[:, b:b + 1] * dm_ref[b:b + 1, :]
        g_ref[...] = g
        d_ref[...], mo_ref[...], vo_ref[...] = _adamw(w_ref[...], g, m_ref[...], v_ref[...])

    big = pl.BlockSpec((None, D_MODEL, W_IN_SHARD), lambda l: (l, 0, 0))
    return pl.pallas_call(
        body, name="adamw_w_ada", grid=(DEPTH,),
        in_specs=[big, big, big, pl.BlockSpec((D_MODEL, N_DEV), lambda l: (0, 0)),
                  pl.BlockSpec((None, N_DEV, W_IN_SHARD), lambda l: (l, 0, 0))],
        out_specs=[big] * 4, out_shape=[jax.ShapeDtypeStruct(w.shape, F32)] * 4,
        compiler_params=_params(dimension_semantics=("arbitrary",)),
    )(w, m, v, c_act_t, dmod_cols)


def _sum_chip_partials(own_ref, recv_ref):
    g = own_ref[...].astype(F32)
    for j in range(N_OTHER_CHIPS):
        g = g + recv_ref[j].astype(F32)
    return g


def _partial_specs(row_tile, cols, first_layer=0):
    own = pl.BlockSpec((None, None, row_tile, cols), lambda l, r, chip_ref: (chip_ref[0], first_layer + l, r, 0))
    recv = pl.BlockSpec((N_OTHER_CHIPS, None, row_tile, cols), lambda l, r, chip_ref: (0, first_layer + l, r, 0))
    return own, recv


def _adamw_reduced(name, w, m, v, partial, received, chip, row_tile, layers, continued):
    depth, rows, cols = w.shape
    first, stop = layers

    def body(chip_ref, w_ref, m_ref, v_ref, own_ref, recv_ref, *rest):
        g_ref, d_ref, mo_ref, vo_ref = rest[-4:]
        g = _sum_chip_partials(own_ref, recv_ref)
        g_ref[...] = g
        d_ref[...], mo_ref[...], vo_ref[...] = _adamw(w_ref[...], g, m_ref[...], v_ref[...])

    blk = pl.BlockSpec((None, row_tile, cols), lambda l, r, chip_ref: (first + l, r, 0))
    args = [chip, w, m, v, partial, received]
    in_specs = [blk, blk, blk, *_partial_specs(row_tile, cols, first)]
    aliases = {}
    if continued is not None:
        aliases = {len(args) + k: k for k in range(4)}
        args += list(continued)
        in_specs += [HBM] * 4
    return pl.pallas_call(
        body, name=name,
        grid_spec=pltpu.PrefetchScalarGridSpec(
            num_scalar_prefetch=1, grid=(stop - first, rows // row_tile), in_specs=in_specs, out_specs=[blk] * 4),
        out_shape=[jax.ShapeDtypeStruct(w.shape, F32)] * 4, input_output_aliases=aliases,
        compiler_params=_params(dimension_semantics=("arbitrary", "arbitrary")),
    )(*args)


def _reduce_w_pool(partial, received, chip):
    def body(chip_ref, own_ref, recv_ref, g_ref):
        g_ref[...] = _sum_chip_partials(own_ref, recv_ref)

    return pl.pallas_call(
        body, name="reduce_w_pool",
        grid_spec=pltpu.PrefetchScalarGridSpec(
            num_scalar_prefetch=1, grid=(DEPTH, 1), in_specs=list(_partial_specs(POOL_SHARD, GROUP_D)),
            out_specs=pl.BlockSpec((POOL_SHARD, GROUP_D), lambda l, r, chip_ref: (l, 0))),
        out_shape=jax.ShapeDtypeStruct((DEPTH * POOL_SHARD, GROUP_D), F32),
        compiler_params=_params(dimension_semantics=("arbitrary", "arbitrary")),
    )(chip, partial, received)


def _adamw_small(params):
    n = len(params)

    def body(*refs):
        ins, outs = refs[:4 * n], refs[4 * n:]
        for p in range(n):
            w_ref, g_ref, m_ref, v_ref = ins[4 * p:4 * p + 4]
            d_ref, mo_ref, vo_ref = outs[3 * p:3 * p + 3]
            d_ref[...], mo_ref[...], vo_ref[...] = _adamw(w_ref[...], g_ref[...], m_ref[...], v_ref[...])

    vmem = pl.BlockSpec(memory_space=pltpu.VMEM)
    flat = [a for group in params for a in group]
    out_shape = [jax.ShapeDtypeStruct(group[0].shape, F32) for group in params for _ in range(3)]
    outs = pl.pallas_call(
        body, name="adamw_small", in_specs=[vmem] * len(flat), out_specs=[vmem] * len(out_shape),
        out_shape=out_shape, compiler_params=_params(),
    )(*flat)
    return [tuple(outs[3 * p:3 * p + 3]) for p in range(n)]


def _sum_sources(slabs):
    def body(s_ref, o_ref):
        acc = s_ref[0]
        for b in range(1, N_DEV):
            acc = acc + s_ref[b]
        o_ref[...] = acc

    vmem = pl.BlockSpec(memory_space=pltpu.VMEM)
    return pl.pallas_call(
        body, name="sum_small_grads", in_specs=[vmem], out_specs=vmem,
        out_shape=jax.ShapeDtypeStruct(slabs.shape[1:], F32), compiler_params=_params(),
    )(slabs)


def _to_bf16(a, name, layers=None):
    first, stop = layers or (0, a.shape[0])

    def body(a_ref, o_ref):
        o_ref[...] = a_ref[...].astype(BF16)

    block = (None,) + a.shape[1:]
    return pl.pallas_call(
        body, name=name, grid=(stop - first,), in_specs=[pl.BlockSpec(block, lambda l: (first + l, 0, 0))],
        out_specs=pl.BlockSpec(block, lambda l: (l, 0, 0)),
        out_shape=jax.ShapeDtypeStruct((stop - first,) + a.shape[1:], BF16),
        compiler_params=_params(dimension_semantics=("arbitrary",)),
    )(a)


def kernel(x, c, w_ada, b_ada, g_pre, w_in, w_conv, w_pool, pool_scale, w_out, g_post, loss_target, m_w_ada, m_b_ada, m_g_pre, m_w_in, m_w_conv, m_w_pool, m_pool_scale, m_w_out, m_g_post, v_w_ada, v_b_ada, v_g_pre, v_w_in, v_w_conv, v_w_pool, v_pool_scale, v_w_out, v_g_post):
    mx, my, mc = _mesh_position()
    me = _block_id(mx, my, mc)
    chip = (2 * mx + my).astype(jnp.int32).reshape(1)
    core = mc.astype(jnp.int32).reshape(1)
    x0 = x[0]
    target = loss_target[0]
    conv_shard = w_conv.shape[-1]

    own_small = jnp.concatenate([c, w_conv.reshape(1, DEPTH * 3 * conv_shard)], axis=1)
    first_shards = [_to_bf16(w_in, "cast_w_in_0", (0, 1)), _to_bf16(w_out, "cast_w_out_0", (0, 1))]
    all_small = _all_gather_small(own_small, "all_gather_c_w_conv", first_shards)[:, 0, :]
    gathers = [_gather_weights_start(0, *first_shards, [all_small])]
    c_all = all_small[:, :D_MODEL]
    w_conv_full = all_small[:, D_MODEL:].reshape(N_DEV, DEPTH, 3, conv_shard).transpose(1, 2, 0, 3).reshape(
        DEPTH, 3, CONV_W)
    cps = jnp.concatenate([w_conv_full, pool_scale[:, None], jnp.zeros((DEPTH, 4, CONV_W), F32)], axis=1)

    c_act, pieces = _modulation_columns(c_all, w_ada)
    upper = (1, DEPTH)
    upper_shards = [_to_bf16(w_in, "cast_w_in_1", upper), _to_bf16(w_out, "cast_w_out_1", upper)]
    wpool_b = _to_bf16(w_pool.reshape(DEPTH, POOL_ROWS, GROUP_D), "cast_w_pool").reshape(w_pool.shape)
    mod_all = _all_gather_small(
        pieces, "all_gather_modulation", [gathers[0][1][0], *upper_shards, wpool_b])
    mod_mine = lax.dynamic_index_in_dim(mod_all, me, axis=1, keepdims=False)
    mod = mod_mine.reshape(N_DEV, DEPTH, W_IN_SHARD).transpose(1, 0, 2).reshape(DEPTH, 3 * D_MODEL) + b_ada
    zeros_d = jnp.zeros((DEPTH, 3, D_MODEL), F32)
    vec = jnp.concatenate([mod.reshape(DEPTH, 3, D_MODEL), g_pre[:, None], g_post[:, None], zeros_d], axis=1)

    gathers.append(_gather_weights_start(1, *upper_shards, [mod_all]))
    gathers = [(first_sems, shards, landing[k], k) for first_sems, shards, landing in gathers
               for k in range(len(landing))]

    xs, kept, wins, wouts = [x0], [], [], []
    for l in range(DEPTH):
        first_sems, shards, zones, index = gathers[l]
        passed_sems, zones = _gather_weights_pass_on(
            l, index, first_sems[1], zones, [vec, cps, wpool_b, gathers[-1][1][0]] if l == 0 else [xs[-1]])
        win, wout = _gather_weights_finish(l, index, first_sems, passed_sems, shards, zones)
        x_next, *for_backward = _forward_layer(
            l, xs[-1], vec, cps, wpool_b, win, wout, target if l == DEPTH - 1 else None)
        xs.append(x_next)
        kept.append(for_backward[:6])
        wins.append(win)
        wouts.append(wout)
    dx, loss_tile = xs[DEPTH], for_backward[6]

    slab_rows = [None] * DEPTH
    partials = received = None
    in_flight = []

    def scatter(layer, grads, from_sibling, after):
        nonlocal partials, received
        partials = _add_sibling_blocks(
            f"grad_add_sibling_{layer}", layer, grads, from_sibling, core, partials, ALL_KINDS)
        chips = _chips_exchange(layer, partials, received, ALL_KINDS)
        sems, partials, received, token = _start_exchange(chips, f"grad_chips_start_{layer}", after)
        in_flight.append((chips, sems, layer))
        return token

    grads_above = None
    for l in reversed(range(DEPTH)):
        y, h, ycat, uc, fa, fp = kept[l]
        dx, dproj, dy, gwpool, dcps, dvec = _backward_layer(
            l, dx, y, xs[l], uc, fa, fp, vec, cps, wpool_b, wouts[l], wins[l])
        slab_rows[l] = jnp.concatenate(
            [dvec[0], dvec[1], dvec[2], dvec[3], dvec[4], dcps[3], dcps[0], dcps[1], dcps[2],
             loss_tile[0] if l == 0 else jnp.zeros((LANES,), F32)])
        after = []
        if l == 0:
            gather_small = _all_gather_exchange(jnp.stack(slab_rows))
            sems_s, slab, slabs, token = _start_exchange(gather_small, "all_gather_small_grads_start")
            after = [token]
        hosted = _sibling_exchange(grads_above, ALL_KINDS) if grads_above is not None else None
        gwin, gwout, *from_sibling = _weight_grads(l, h, dproj, ycat, dy, hosted, after)
        if l == 0:
            _, (slabs,) = _finish_exchange(
                gather_small, "all_gather_small_grads_finish", sems_s, slab, slabs, [gwin])
        if grads_above is not None:
            scatter(l + 1, grads_above, from_sibling, [])
        grads_above = [gwin, gwout, gwpool.reshape(POOL_ROWS, GROUP_D)]
        if l <= 1:
            from_sibling = _run_exchange(_sibling_exchange(grads_above, ALL_KINDS), f"grad_exchange_sibling_{l}")
            token = scatter(l, grads_above, from_sibling, [slabs] if l == 0 else [])
            grads_above = None
    grad_x = dx[None]
    chips_0, sems_0, _ = in_flight.pop()

    total = _sum_sources(slabs)
    loss = total[0, SLAB_COLS]
    o = 3 * D_MODEL
    g_b_ada = total[:, :o]
    g_g_pre = total[:, o:o + D_MODEL]
    g_g_post = total[:, o + D_MODEL:o + 2 * D_MODEL]
    g_pool_scale = total[:, o + 2 * D_MODEL:o + 2 * D_MODEL + POOL_W]
    g_conv_full = total[:, o + 2 * D_MODEL + POOL_W:SLAB_COLS].reshape(DEPTH, 3, CONV_W)
    g_w_conv = lax.dynamic_slice_in_dim(g_conv_full, me * conv_shard, conv_shard, axis=2)

    after = [token]
    for chips, sems, l in in_flight:
        partials, received = _finish_exchange(chips, f"grad_chips_finish_{l}", sems, partials, received, after)
        after = []
    upper = (1, DEPTH)
    w_in_upper = _adamw_reduced(
        "adamw_w_in_upper", w_in, m_w_in, v_w_in, partials[0], received[0], chip, ROW_TILE, upper, None)
    w_out_upper = _adamw_reduced(
        "adamw_w_out_upper", w_out, m_w_out, v_w_out, partials[1], received[1], chip, W_OUT_SHARD, upper, None)
    dmod_all = slabs[:, :, :o].reshape(N_DEV, DEPTH, N_DEV, W_IN_SHARD)
    dmod_cols = lax.dynamic_index_in_dim(dmod_all, me, axis=2, keepdims=False).transpose(1, 0, 2) + token[0, 0]
    g_w_ada, d_w_ada, nm_w_ada, nv_w_ada = _adamw_w_ada(w_ada, m_w_ada, v_w_ada, c_act.T, dmod_cols)

    partials, received = _finish_exchange(
        chips_0, "grad_chips_finish_0", sems_0, partials, received, [nv_w_ada, w_in_upper[3], w_out_upper[3]])
    gather_pool = _all_gather_exchange(_reduce_w_pool(partials[2], received[2], chip))
    sems_p, pool_rows, pool_landing, token_p = _start_exchange(gather_pool, "all_gather_grad_w_pool_start")
    g_w_in, d_w_in, nm_w_in, nv_w_in = _adamw_reduced(
        "adamw_w_in_0", w_in, m_w_in, v_w_in, partials[0], received[0], chip, ROW_TILE, (0, 1), w_in_upper)
    g_w_out, d_w_out, nm_w_out, nv_w_out = _adamw_reduced(
        "adamw_w_out_0", w_out, m_w_out, v_w_out, partials[1], received[1], chip, W_OUT_SHARD, (0, 1), w_out_upper)
    _, (g_pool_all,) = _finish_exchange(
        gather_pool, "all_gather_grad_w_pool_finish", sems_p, pool_rows, pool_landing, [nv_w_in, nv_w_out])
    g_w_pool = g_pool_all.reshape(N_DEV, DEPTH, POOL_SHARD, GROUP_D).transpose(1, 0, 2, 3).reshape(w_pool.shape)

    flat2 = lambda a: a.reshape(-1, a.shape[-1])
    small = _adamw_small([
        (b_ada, g_b_ada, m_b_ada, v_b_ada),
        (g_pre, g_g_pre, m_g_pre, v_g_pre),
        (flat2(w_conv), flat2(g_w_conv), flat2(m_w_conv), flat2(v_w_conv)),
        (flat2(w_pool), flat2(g_w_pool), flat2(m_w_pool), flat2(v_w_pool)),
        (pool_scale, g_pool_scale, m_pool_scale, v_pool_scale),
        (g_post, g_g_post, m_g_post, v_g_post),
    ])
    (d_b_ada, nm_b_ada, nv_b_ada), (d_g_pre, nm_g_pre, nv_g_pre), conv_upd, pool_upd, \
        (d_ps, nm_ps, nv_ps), (d_g_post, nm_g_post, nv_g_post) = small
    d_w_conv, nm_w_conv, nv_w_conv = (a.reshape(w_conv.shape) for a in conv_upd)
    d_w_pool, nm_w_pool, nv_w_pool = (a.reshape(w_pool.shape) for a in pool_upd)

    return (loss, grad_x,
            g_w_ada, g_b_ada, g_g_pre, g_w_in, g_w_conv, g_w_pool, g_pool_scale, g_w_out, g_g_post,
            d_w_ada, d_b_ada, d_g_pre, d_w_in, d_w_conv, d_w_pool, d_ps, d_w_out, d_g_post,
            nm_w_ada, nm_b_ada, nm_g_pre, nm_w_in, nm_w_conv, nm_w_pool, nm_ps, nm_w_out, nm_g_post,
            nv_w_ada, nv_b_ada, nv_g_pre, nv_w_in, nv_w_conv, nv_w_pool, nv_ps, nv_w_out, nv_g_post)
```

```python
from functools import partial

import jax
import jax.numpy as jnp
from jax import lax
from jax.experimental import pallas as pl
from jax.experimental.pallas import tpu as pltpu

F32 = jnp.float32
BF16 = jnp.bfloat16

D_MODEL = 1024
DEPTH = 4
CONV_W = 512
POOL_W = 512
POOL_WINDOWS = (2, 4, 8, 16)
GROUP_D = 128
IN_COLS = 4 * CONV_W + 2 * POOL_W
NORM_EPS = 1e-6

ADAM_LR = 0.001
ADAM_B1 = 0.9
ADAM_B2 = 0.999
ADAM_EPS = 1e-08
ADAM_WD = 0.01
ADAM_STEP = 10

N_DEV = 8
N_CHIP = 4
N_OTHER_CHIPS = N_CHIP - 1
MESH = pl.DeviceIdType.MESH
W_IN_SHARD = IN_COLS // N_DEV
W_OUT_SHARD = D_MODEL // N_DEV
POOL_ROWS = len(POOL_WINDOWS) * GROUP_D
POOL_SHARD = POOL_ROWS // N_DEV

SUBLANES = 8
LANES = 128
VMEM_LIMIT_BYTES = 56 * 1024 * 1024
ROW_TILE = 512
BWD_TILE = 256
GWIN_COLS = 768
GWOUT_COLS = 512
POOL_HALO = 16
CONV_HALO = SUBLANES

SLAB_COLS = 3 * D_MODEL + D_MODEL + D_MODEL + POOL_W + 3 * CONV_W

HBM = pl.BlockSpec(memory_space=pl.ANY)


def _params(**kw):
    return pltpu.CompilerParams(vmem_limit_bytes=VMEM_LIMIT_BYTES, **kw)


def _sigmoid(v):
    return 1.0 / (1.0 + jnp.exp(-v))


def _dot(a, b):
    return jnp.dot(a, b, preferred_element_type=F32)


def _dot_tn(a, b):
    return lax.dot_general(a, b, (((0,), (0,)), ((), ())), preferred_element_type=F32)


def _dot_nt(a, b):
    return lax.dot_general(a, b, (((1,), (1,)), ((), ())), preferred_element_type=F32)


def _rows_from_before(v, k):
    return pltpu.roll(v, k, 0)


def _rows_from_after(v, k):
    return pltpu.roll(v, v.shape[0] - k, 0)


def _window_counts(t0, rows):
    return (lax.broadcasted_iota(jnp.int32, (rows, 1), 0) + (t0 + 1)).astype(F32)


def _split_proj(p32):
    cw = CONV_W
    return (p32[:, 0 * cw:1 * cw], p32[:, 1 * cw:2 * cw], p32[:, 2 * cw:3 * cw], p32[:, 3 * cw:4 * cw],
            p32[:, 4 * cw:4 * cw + POOL_W], p32[:, 4 * cw + POOL_W:])


def _layer_spec(shape, layer):
    nd = len(shape)
    return pl.BlockSpec((None,) + tuple(shape[1:]), lambda i, _l=layer, _n=nd: (_l,) + (0,) * (_n - 1))


def _whole_spec(shape):
    return pl.BlockSpec(tuple(shape), lambda i, _n=len(shape): (0,) * _n, pipeline_mode=pl.Buffered(1))


def _mesh_position():
    return lax.axis_index("x"), lax.axis_index("y"), lax.axis_index("c")


def _block_id(x, y, c):
    return 4 * x + 2 * y + c


def _other_chips(x, y):
    return [(x ^ 1, y), (x, y ^ 1), (x ^ 1, y ^ 1)]


def _col_block(ref, blk):
    return ref.at[:, pl.ds(pl.multiple_of(blk * W_IN_SHARD, LANES), W_IN_SHARD)]


def _row_block(rows):
    def block(ref, blk):
        return ref.at[pl.ds(pl.multiple_of(blk * rows, rows), rows), :]
    return block


_BLOCK_OF = (_col_block, _row_block(W_OUT_SHARD), _row_block(POOL_SHARD))
_BLOCK_SHAPES = ((D_MODEL, W_IN_SHARD), (W_OUT_SHARD, D_MODEL), (POOL_SHARD, GROUP_D))


class _Exchange:
    def __init__(self, inputs, out_shapes, aliases, sem_shapes, make):
        self.inputs, self.out_shapes, self.aliases, self.sem_shapes, self.make = (
            list(inputs), list(out_shapes), dict(aliases), list(sem_shapes), make)


def _run_exchange(exchange, name):
    n_in, n_out = len(exchange.inputs), len(exchange.out_shapes)

    def body(*refs):
        start, finish = exchange.make(refs[:n_in], refs[n_in:n_in + n_out], refs[n_in + n_out:])
        start()
        finish()

    return pl.pallas_call(
        body, name=name, in_specs=[HBM] * n_in, out_specs=[HBM] * n_out, out_shape=exchange.out_shapes,
        scratch_shapes=exchange.sem_shapes, input_output_aliases=exchange.aliases, compiler_params=_params(),
    )(*exchange.inputs)


_SEM = pl.BlockSpec(memory_space=pltpu.SEMAPHORE)
_DATAFLOW = pltpu.SideEffectType.DATAFLOW_SIDE_EFFECTING


def _start_exchange(exchange, name, after=()):
    n_in, n_out, n_sem = len(exchange.inputs), len(exchange.out_shapes), len(exchange.sem_shapes)
    sources = [i for i in range(n_in) if i not in exchange.aliases]
    aliases = {i: n_sem + k for k, i in enumerate(sources)}
    aliases.update({i: n_sem + len(sources) + o for i, o in exchange.aliases.items()})

    def body(*refs):
        in_refs = refs[:n_in]
        outs = refs[n_in + len(after):]
        sems = outs[:n_sem]
        out_refs = outs[n_sem + len(sources):n_sem + len(sources) + n_out]
        exchange.make(in_refs, out_refs, sems)[0]()
        refs[-1][...] = jnp.zeros_like(refs[-1])

    outs = pl.pallas_call(
        body, name=name, in_specs=[HBM] * (n_in + len(after)),
        out_specs=[_SEM] * n_sem + [HBM] * (len(sources) + n_out) + [pl.BlockSpec(memory_space=pltpu.VMEM)],
        out_shape=(exchange.sem_shapes + [pltpu.HBM(exchange.inputs[i].shape, exchange.inputs[i].dtype) for i in sources]
                   + [pltpu.HBM(s.shape, s.dtype) for s in exchange.out_shapes]
                   + [jax.ShapeDtypeStruct((SUBLANES, LANES), F32)]),
        input_output_aliases=aliases, compiler_params=_params(has_side_effects=_DATAFLOW),
    )(*exchange.inputs, *after)
    return outs[:n_sem], outs[n_sem:n_sem + len(sources)], outs[n_sem + len(sources):-1], outs[-1]


def _finish_exchange(exchange, name, sems, sources, landing, after):
    n_src, n_out, n_sem = len(sources), len(landing), len(sems)
    n_in = len(exchange.inputs)
    source_at = [i for i in range(n_in) if i not in exchange.aliases]

    def body(*refs):
        src_refs, out_refs = refs[:n_src], refs[n_src:n_src + n_out]
        sem_refs = refs[n_src + n_out:n_src + n_out + n_sem]
        in_refs = [None] * n_in
        for k, i in enumerate(source_at):
            in_refs[i] = src_refs[k]
        for i, o in exchange.aliases.items():
            in_refs[i] = out_refs[o]
        exchange.make(in_refs, out_refs, sem_refs)[1]()

    arrays = list(sources) + list(landing)
    outs = pl.pallas_call(
        body, name=name, in_specs=[HBM] * len(arrays) + [_SEM] * n_sem + [HBM] * len(after),
        out_specs=[HBM] * len(arrays), out_shape=[pltpu.HBM(a.shape, a.dtype) for a in arrays],
        input_output_aliases={i: i for i in range(len(arrays))}, compiler_params=_params(has_side_effects=_DATAFLOW),
    )(*arrays, *sems, *after)
    return outs[:n_src], outs[n_src:]


N_GATHERED = 2
FIRST_COPIES = 1 + N_OTHER_CHIPS
_GATHERED_SHAPES = ((D_MODEL, IN_COLS), (D_MODEL, D_MODEL))
ALL_OTHER_CHIPS, NEIGHBOUR_CHIPS, DIAGONAL_CHIP = (0, 1, 2), (0, 1), (2,)


def _first_sem(layer, a, k):
    return (layer * N_GATHERED + a) * FIRST_COPIES + k


def _first_arrival(layer, a, j):
    return _first_sem(layer, a, 1 + j)


def _gather_copy(window_of, full_ref, blk, send_sem, recv_sem, to, src=None):
    window = window_of(full_ref, blk)
    return pltpu.make_async_remote_copy(
        src_ref=window if src is None else src, dst_ref=window, send_sem=send_sem, recv_sem=recv_sem,
        device_id=to, device_id_type=MESH)


def _first_copies(layer, shard_refs, full_refs, send_sems, recv_sems, local_sems, relayed):
    x, y, c = _mesh_position()
    me = _block_id(x, y, c)
    chips = [_other_chips(x, y)[j] for j in (NEIGHBOUR_CHIPS if relayed else ALL_OTHER_CHIPS)]
    own, remote = [], []
    for a in range(N_GATHERED):
        shard = shard_refs[a].at[layer]
        own.append(pltpu.make_async_copy(
            shard, _BLOCK_OF[a](full_refs[a], me), local_sems.at[layer * N_GATHERED + a]))
        targets = [(x, y, 1 - c)] + [(*chip, c) for chip in chips]
        remote += [_gather_copy(_BLOCK_OF[a], full_refs[a], me, send_sems.at[_first_sem(layer, a, k)],
                                recv_sems.at[_first_sem(layer, a, k)], to, src=shard)
                   for k, to in enumerate(targets)]
    return own, remote


def _gather_weights_start(first_layer, win_shards, wout_shards, after, relayed=False):
    n_layers = win_shards.shape[0]
    n_first = n_layers * N_GATHERED * FIRST_COPIES
    sem_shapes = [pltpu.SemaphoreType.DMA((n_first,)), pltpu.SemaphoreType.DMA((n_first,)),
                  pltpu.SemaphoreType.DMA((n_layers * N_GATHERED,))]
    shards = [win_shards, wout_shards]

    def body(win_sh, wout_sh, *rest):
        send_sems, recv_sems, local_sems, win_thru, wout_thru, *landing = rest[len(after):]
        for layer in range(n_layers):
            own, remote = _first_copies(layer, (win_sh, wout_sh), landing[N_GATHERED * layer:N_GATHERED * (layer + 1)],
                                        send_sems, recv_sems, local_sems, relayed)
            for cp in own + remote:
                cp.start()

    outs = pl.pallas_call(
        body, name=f"all_gather_weights_start_{first_layer}", in_specs=[HBM] * (2 + len(after)),
        out_specs=[_SEM] * 3 + [HBM] * (2 + n_layers * N_GATHERED),
        out_shape=(sem_shapes + [pltpu.HBM(s.shape, s.dtype) for s in shards]
                   + [pltpu.HBM(s, BF16) for _ in range(n_layers) for s in _GATHERED_SHAPES]),
        input_output_aliases={0: 3, 1: 4}, compiler_params=_params(has_side_effects=_DATAFLOW),
    )(*shards, *after)
    landing = outs[5:]
    return outs[:3], outs[3:5], [landing[N_GATHERED * l:N_GATHERED * (l + 1)] for l in range(n_layers)]


def _passed_on_copies(full_refs, send_sems, recv_sems, core_of_block, chips):
    x, y, c = _mesh_position()
    return [_gather_copy(_BLOCK_OF[a], full_refs[a], _block_id(*_other_chips(x, y)[j], core_of_block),
                         send_sems.at[a * N_OTHER_CHIPS + j], recv_sems.at[a * N_OTHER_CHIPS + j], (x, y, 1 - c))
            for a in range(N_GATHERED) for j in chips]


def _relayed_copies(full_refs, send_sems, recv_sems):
    x, y, c = _mesh_position()
    source, to = (x ^ (1 - c), y ^ c), (x ^ c, y ^ (1 - c))
    return [_gather_copy(_BLOCK_OF[a], full_refs[a], _block_id(*source, c), send_sems.at[a], recv_sems.at[a], (*to, c))
            for a in range(N_GATHERED)]


def _gather_weights_pass_on(name, chips, arrival_sems, arrival_sem_of, landing, after, relay=False):
    n = N_GATHERED * N_OTHER_CHIPS
    sem_shapes = [pltpu.SemaphoreType.DMA((n,))] * 2 + [pltpu.SemaphoreType.DMA((N_GATHERED,))] * (2 if relay else 0)

    def body(win_ref, wout_ref, arrivals, *rest):
        sems = rest[len(after):len(after) + len(sem_shapes)]
        x, y, c = _mesh_position()
        full_refs = (win_ref, wout_ref)
        passed = _passed_on_copies(full_refs, sems[0], sems[1], c, chips)
        relayed = _relayed_copies(full_refs, sems[2], sems[3]) if relay else []
        for a in range(N_GATHERED):
            for j in chips:
                sem = arrivals.at[arrival_sem_of(a, j)]
                _gather_copy(_BLOCK_OF[a], full_refs[a], _block_id(*_other_chips(x, y)[j], c), sem, sem,
                             (x, y, c)).wait_recv()
            for cp in relayed[a:a + 1] + passed[a * len(chips):(a + 1) * len(chips)]:
                cp.start()

    outs = pl.pallas_call(
        body, name=name, in_specs=[HBM] * N_GATHERED + [_SEM] + [HBM] * len(after),
        out_specs=[_SEM] * len(sem_shapes) + [HBM] * N_GATHERED,
        out_shape=sem_shapes + [pltpu.HBM(a.shape, a.dtype) for a in landing],
        input_output_aliases={a: len(sem_shapes) + a for a in range(N_GATHERED)},
        compiler_params=_params(has_side_effects=_DATAFLOW),
    )(*landing, arrival_sems, *after)
    return outs[:len(sem_shapes)], outs[len(sem_shapes):]


def _gather_weights_finish(layer, index, first_sems, passed, shards, landing, relay_send_sems=None):
    relayed = relay_send_sems is not None
    passed_sems = [sem for (send, recv), _ in passed for sem in (send, recv)] + ([relay_send_sems] if relayed else [])

    def body(win_ref, wout_ref, first_send, first_recv, local_sems, *rest):
        sems, (win_sh, wout_sh) = rest[:len(passed_sems)], rest[len(passed_sems):len(passed_sems) + 2]
        x, y, c = _mesh_position()
        full_refs = (win_ref, wout_ref)
        own, sent = _first_copies(index, (win_sh, wout_sh), full_refs, first_send, first_recv, local_sems, relayed)
        for a in range(N_GATHERED):
            sem = _first_sem(index, a, 0)
            _gather_copy(_BLOCK_OF[a], full_refs[a], _block_id(x, y, 1 - c), first_recv.at[sem], first_recv.at[sem],
                         (x, y, c)).wait_recv()
        for p, (_, chips) in enumerate(passed):
            for cp in _passed_on_copies(full_refs, sems[2 * p], sems[2 * p + 1], 1 - c, chips):
                cp.wait_recv()
            sent += _passed_on_copies(full_refs, sems[2 * p], sems[2 * p + 1], c, chips)
        if relayed:
            sent += _relayed_copies(full_refs, sems[-1], sems[-1])
        for cp in sent:
            cp.wait_send()
        for cp in own:
            cp.wait()

    return pl.pallas_call(
        body, name=f"all_gather_weights_finish_{layer}",
        in_specs=[HBM] * N_GATHERED + [_SEM] * (3 + len(passed_sems)) + [HBM] * 2,
        out_specs=[HBM] * N_GATHERED, out_shape=[pltpu.HBM(a.shape, a.dtype) for a in landing],
        input_output_aliases={a: a for a in range(N_GATHERED)}, compiler_params=_params(has_side_effects=_DATAFLOW),
    )(*landing, *first_sems, *passed_sems, *shards)


ALL_KINDS = (0, 1, 2)


def _sibling_exchange(grads, kinds):
    n_arr = len(grads)

    def make(in_refs, out_refs, sems):
        send_sems, recv_sems = sems
        x, y, c = _mesh_position()
        copies = [pltpu.make_async_remote_copy(
            src_ref=_BLOCK_OF[kinds[a]](in_refs[a], 2 * q + (1 - c)), dst_ref=out_refs[a].at[q],
            send_sem=send_sems.at[a * N_CHIP + q], recv_sem=recv_sems.at[a * N_CHIP + q],
            device_id=(x, y, 1 - c), device_id_type=MESH)
            for a in range(n_arr) for q in range(N_CHIP)]

        def start():
            for cp in copies:
                cp.start()

        def finish():
            for cp in copies:
                cp.wait_recv()
            for cp in copies:
                cp.wait_send()

        return start, finish

    return _Exchange(
        grads, [jax.ShapeDtypeStruct((N_CHIP,) + _BLOCK_SHAPES[k], BF16) for k in kinds], {},
        [pltpu.SemaphoreType.DMA((n_arr * N_CHIP,)), pltpu.SemaphoreType.DMA((n_arr * N_CHIP,))], make)


def _chips_exchange(layer, partials, received, kinds):
    n_arr = len(partials)

    def make(in_refs, out_refs, sems):
        send_sems, recv_sems = sems
        x, y, c = _mesh_position()
        copies = [pltpu.make_async_remote_copy(
            src_ref=in_refs[a].at[2 * qx + qy, layer], dst_ref=out_refs[a].at[j, layer],
            send_sem=send_sems.at[a * N_OTHER_CHIPS + j], recv_sem=recv_sems.at[a * N_OTHER_CHIPS + j],
            device_id=(qx, qy, c), device_id_type=MESH)
            for a in range(n_arr) for j, (qx, qy) in enumerate(_other_chips(x, y))]

        def start():
            for cp in copies:
                cp.start()

        def finish():
            for cp in copies:
                cp.wait_recv()
            for cp in copies:
                cp.wait_send()

        return start, finish

    inputs = list(partials)
    aliases = {}
    if received is not None:
        inputs += list(received)
        aliases = {n_arr + a: a for a in range(n_arr)}
    return _Exchange(
        inputs, [jax.ShapeDtypeStruct((N_OTHER_CHIPS, DEPTH) + _BLOCK_SHAPES[k], BF16) for k in kinds], aliases,
        [pltpu.SemaphoreType.DMA((n_arr * N_OTHER_CHIPS,)), pltpu.SemaphoreType.DMA((n_arr * N_OTHER_CHIPS,))], make)


def _all_gather_exchange(v):
    def make(in_refs, out_refs, sems):
        send_sems, recv_sems, local_sem = sems
        x, y, c = _mesh_position()
        me = _block_id(x, y, c)
        own = pltpu.make_async_copy(in_refs[0], out_refs[0].at[me], local_sem.at[0])
        sends, arrivals = [], []
        for k in range(1, N_DEV):
            px, py, pc = x ^ ((k >> 2) & 1), y ^ ((k >> 1) & 1), c ^ (k & 1)
            sends.append(pltpu.make_async_remote_copy(
                src_ref=in_refs[0], dst_ref=out_refs[0].at[me], send_sem=send_sems.at[k - 1],
                recv_sem=recv_sems.at[k - 1], device_id=(px, py, pc), device_id_type=MESH))
            arrivals.append(pltpu.make_async_remote_copy(
                src_ref=in_refs[0], dst_ref=out_refs[0].at[_block_id(px, py, pc)], send_sem=send_sems.at[k - 1],
                recv_sem=recv_sems.at[k - 1], device_id=(x, y, c), device_id_type=MESH))

        def start():
            for cp in [own] + sends:
                cp.start()

        def finish():
            for cp in arrivals:
                cp.wait_recv()
            for cp in sends:
                cp.wait_send()
            own.wait()

        return start, finish

    return _Exchange(
        [v], [jax.ShapeDtypeStruct((N_DEV,) + v.shape, v.dtype)], {},
        [pltpu.SemaphoreType.DMA((N_DEV - 1,)), pltpu.SemaphoreType.DMA((N_DEV - 1,)),
         pltpu.SemaphoreType.DMA((1,))], make)


def _host(exchange, args, in_specs, out_shape, out_specs, scratch):
    n_own = (len(args), len(out_shape), len(scratch))
    if exchange is None:
        return {}, lambda refs: (refs, None)
    n_ex = (len(exchange.inputs), len(exchange.out_shapes), len(exchange.sem_shapes))
    aliases = {n_own[0] + i: n_own[1] + o for i, o in exchange.aliases.items()}
    args += exchange.inputs
    in_specs += [HBM] * n_ex[0]
    out_shape += exchange.out_shapes
    out_specs += [HBM] * n_ex[1]
    scratch += exchange.sem_shapes

    def split(refs):
        own, theirs, at = [], [], 0
        for mine, ex in zip(n_own, n_ex):
            own += refs[at:at + mine]
            theirs.append(refs[at + mine:at + mine + ex])
            at += mine + ex
        return own, exchange.make(*theirs)

    return aliases, split


def _all_gather_small(v, name, after=()):
    vmem = pl.BlockSpec(memory_space=pltpu.VMEM)

    def body(v_ref, *rest):
        out_ref, send_sems, recv_sems = rest[len(after):]
        x, y, c = _mesh_position()
        me = _block_id(x, y, c)
        out_ref[me] = v_ref[...]
        sends = []
        for k in range(1, N_DEV):
            px, py, pc = x ^ ((k >> 2) & 1), y ^ ((k >> 1) & 1), c ^ (k & 1)
            send = pltpu.make_async_remote_copy(
                src_ref=v_ref, dst_ref=out_ref.at[me], send_sem=send_sems.at[k - 1], recv_sem=recv_sems.at[k - 1],
                device_id=(px, py, pc), device_id_type=MESH)
            send.start()
            sends.append((send, _block_id(px, py, pc)))
        for k, (send, peer) in enumerate(sends):
            pltpu.make_async_remote_copy(
                src_ref=v_ref, dst_ref=out_ref.at[peer], send_sem=send_sems.at[k], recv_sem=recv_sems.at[k],
                device_id=(x, y, c), device_id_type=MESH).wait_recv()
        for send, _ in sends:
            send.wait_send()

    return pl.pallas_call(
        body, name=name, in_specs=[vmem] + [HBM] * len(after), out_specs=vmem,
        out_shape=jax.ShapeDtypeStruct((N_DEV,) + v.shape, v.dtype),
        scratch_shapes=[pltpu.SemaphoreType.DMA((N_DEV - 1,)), pltpu.SemaphoreType.DMA((N_DEV - 1,))],
        compiler_params=_params(),
    )(v, *after)


def _forward_layer(layer, x, vec, cps, wpool, win, wout, target=None):
    t_len = x.shape[0]
    n_tiles = t_len // ROW_TILE
    row = lambda cols: pl.BlockSpec((ROW_TILE, cols), lambda i: (i, 0))
    widths = (D_MODEL, 2 * CONV_W, 3 * CONV_W, 4 * POOL_W)
    head = target is not None

    def body(x_ref, vec_ref, cps_ref, wpool_ref, win_ref, wout_ref, *rest):
        target_ref = rest[0] if head else None
        xo_ref, y_ref, h_ref, ycat_ref, uc_ref, fa_ref, fp_ref = rest[head:head + 7]
        loss_ref = rest[head + 7] if head else None
        zc_ref, pc_ref = rest[-2:]
        i = pl.program_id(0)

        @pl.when(i == 0)
        def _():
            zc_ref[...] = jnp.zeros_like(zc_ref)
            pc_ref[...] = jnp.zeros_like(pc_ref)
            if head:
                loss_ref[...] = jnp.zeros_like(loss_ref)

        x_t = x_ref[...]
        shift, scale, gate = vec_ref[0:1, :], vec_ref[1:2, :], vec_ref[2:3, :]
        g_pre, g_post = vec_ref[3:4, :], vec_ref[4:5, :]
        w0, w1, w2, ps = cps_ref[0:1, :], cps_ref[1:2, :], cps_ref[2:3, :], cps_ref[3:4, :]
        rx = lax.rsqrt(jnp.mean(x_t * x_t, axis=-1, keepdims=True) + NORM_EPS)
        h = (x_t * rx) * g_pre * (1.0 + scale) + shift
        h_ref[...] = h.astype(BF16)
        proj = _dot(h.astype(BF16), win_ref[...])
        u_a, b_a, c_a, g_a, u_p, g_p = _split_proj(proj)
        uc_ref[...] = jnp.concatenate([u_a, c_a], axis=1).astype(BF16)

        z = c_a * u_a
        zcat = jnp.concatenate([zc_ref[...], z], axis=0)
        zc_ref[...] = z[ROW_TILE - CONV_HALO:]
        conv = (w0 * _rows_from_before(zcat, 2)[CONV_HALO:] + w1 * _rows_from_before(zcat, 1)[CONV_HALO:] + w2 * z)
        sig_a = _sigmoid(g_a)
        silu_a = g_a * sig_a
        b_conv = b_a * conv
        y_a = b_conv * silu_a
        fa_ref[...] = jnp.concatenate(
            [silu_a * conv, silu_a * b_a, b_conv * (sig_a + silu_a * (1.0 - sig_a))], axis=1).astype(BF16)

        pcat = jnp.concatenate([pc_ref[...], u_p], axis=0)
        pc_ref[...] = u_p[ROW_TILE - POOL_HALO:]
        counts = _window_counts(i * ROW_TILE, ROW_TILE)
        pooled, mixed = [], []
        for g, w in enumerate(POOL_WINDOWS):
            cols = slice(g * GROUP_D, (g + 1) * GROUP_D)
            s = pcat[:, cols]
            step = 1
            while step < w:
                s = s + _rows_from_before(s, step)
                step *= 2
            pooled_g = (s[POOL_HALO:] * (1.0 / jnp.minimum(counts, float(w))) - u_p[:, cols]).astype(BF16)
            pooled.append(pooled_g)
            mixed.append(_dot(pooled_g, wpool_ref[g]))
        mixed = jnp.concatenate(mixed, axis=1)
        sig_p = _sigmoid(g_p)
        silu_p = g_p * sig_p
        mixed_ps = mixed * ps
        y_p = mixed_ps * silu_p
        fp_ref[...] = jnp.concatenate(
            [(ps * silu_p).astype(BF16), (mixed_ps * (sig_p + silu_p * (1.0 - sig_p))).astype(BF16),
             (silu_p * mixed).astype(BF16)] + pooled, axis=1)

        ycat = jnp.concatenate([y_a, y_p], axis=1)
        ycat_ref[...] = ycat.astype(BF16)
        y_b = _dot(ycat.astype(BF16), wout_ref[...]).astype(BF16)
        y_ref[...] = y_b
        y_t = y_b.astype(F32)
        ry = lax.rsqrt(jnp.mean(y_t * y_t, axis=-1, keepdims=True) + NORM_EPS)
        x_next = x_t + gate * (y_t * ry * g_post)
        if head:
            err = x_next - target_ref[...]
            xo_ref[...] = err * (1.0 / D_MODEL)
            loss_ref[...] += jnp.sum(err * err) * (0.5 / D_MODEL)
        else:
            xo_ref[...] = x_next

    tile = (SUBLANES, LANES)
    return pl.pallas_call(
        body, name=f"forward_layer_{layer}", grid=(n_tiles,),
        in_specs=[row(D_MODEL), _layer_spec(vec.shape, layer), _layer_spec(cps.shape, layer),
                  _layer_spec(wpool.shape, layer), _whole_spec(win.shape), _whole_spec(wout.shape)]
        + [row(D_MODEL)] * head,
        out_specs=[row(D_MODEL), row(D_MODEL), row(D_MODEL), row(D_MODEL)] + [row(w) for w in widths[1:]]
        + [_whole_spec(tile)] * head,
        out_shape=[jax.ShapeDtypeStruct((t_len, D_MODEL), F32), jax.ShapeDtypeStruct((t_len, D_MODEL), BF16),
                   jax.ShapeDtypeStruct((t_len, D_MODEL), BF16), jax.ShapeDtypeStruct((t_len, D_MODEL), BF16)]
        + [jax.ShapeDtypeStruct((t_len, w), BF16) for w in widths[1:]] + [jax.ShapeDtypeStruct(tile, F32)] * head,
        scratch_shapes=[pltpu.VMEM((CONV_HALO, CONV_W), F32), pltpu.VMEM((POOL_HALO, POOL_W), F32)],
        compiler_params=_params(dimension_semantics=("arbitrary",)),
    )(x, vec, cps, wpool, win, wout, *([target] * head))


def _backward_layer(layer, dxo, y, x, uc, fa, fp, vec, cps, wpool, wout, win):
    t_len = dxo.shape[0]
    n_tiles = t_len // BWD_TILE
    halo_per_tile = BWD_TILE // POOL_HALO
    rev = lambda cols: pl.BlockSpec((BWD_TILE, cols), lambda i: (n_tiles - 1 - i, 0))
    halo_spec = pl.BlockSpec(
        (POOL_HALO, 2 * CONV_W), lambda i: (jnp.maximum((n_tiles - 1 - i) * halo_per_tile - 1, 0), 0))
    gwpool_shape = (len(POOL_WINDOWS), GROUP_D, GROUP_D)

    def body(dxo_ref, y_ref, x_ref, uc_ref, uch_ref, fa_ref, fp_ref, vec_ref, cps_ref, wpool_ref, wout_ref, win_ref,
             dx_ref, dproj_ref, dy_ref, gwpool_ref, dcps_ref, dvec_ref, gwpool_acc, dcc_ref, qc_ref):
        i = pl.program_id(0)
        tile = n_tiles - 1 - i

        @pl.when(i == 0)
        def _():
            gwpool_acc[...] = jnp.zeros_like(gwpool_acc)
            dcps_ref[...] = jnp.zeros_like(dcps_ref)
            dvec_ref[...] = jnp.zeros_like(dvec_ref)
            dcc_ref[...] = jnp.zeros_like(dcc_ref)
            qc_ref[...] = jnp.zeros_like(qc_ref)

        shift, scale, gate = vec_ref[0:1, :], vec_ref[1:2, :], vec_ref[2:3, :]
        g_pre, g_post = vec_ref[3:4, :], vec_ref[4:5, :]
        w0, w1, w2 = cps_ref[0:1, :], cps_ref[1:2, :], cps_ref[2:3, :]

        dxo_t = dxo_ref[...]
        y_t = y_ref[...].astype(F32)
        ry = lax.rsqrt(jnp.mean(y_t * y_t, axis=-1, keepdims=True) + NORM_EPS)
        yh = y_t * ry
        dvec_ref[2:3, :] += jnp.sum(dxo_t * yh, axis=0, keepdims=True)
        dyh = dxo_t * (gate * g_post)
        dy_b = (ry * (dyh - yh * jnp.mean(dyh * yh, axis=-1, keepdims=True))).astype(BF16)
        dy_ref[...] = dy_b
        dycat = _dot_nt(dy_b, wout_ref[...])
        dy_a, dy_p = dycat[:, :CONV_W], dycat[:, CONV_W:]

        fa_t = fa_ref[...].astype(F32)
        db_a = dy_a * fa_t[:, :CONV_W]
        dconv = dy_a * fa_t[:, CONV_W:2 * CONV_W]
        dg_a = dy_a * fa_t[:, 2 * CONV_W:]
        uc_t = uc_ref[...].astype(F32)
        u_a, c_a = uc_t[:, :CONV_W], uc_t[:, CONV_W:]
        halo = jnp.where(tile > 0, uch_ref[...].astype(F32), 0.0)[POOL_HALO - CONV_HALO:]
        z = c_a * u_a
        zcat = jnp.concatenate([halo[:, CONV_W:] * halo[:, :CONV_W], z], axis=0)
        z1 = _rows_from_before(zcat, 1)[CONV_HALO:]
        z2 = _rows_from_before(zcat, 2)[CONV_HALO:]
        dccat = jnp.concatenate([dconv, dcc_ref[...]], axis=0)
        dc1 = _rows_from_after(dccat, 1)[:BWD_TILE]
        dc2 = _rows_from_after(dccat, 2)[:BWD_TILE]
        dz = w2 * dconv + w1 * dc1 + w0 * dc2
        dcc_ref[...] = dconv[:CONV_HALO]
        dcps_ref[0:1, :] += jnp.sum(dconv * z2, axis=0, keepdims=True)
        dcps_ref[1:2, :] += jnp.sum(dconv * z1, axis=0, keepdims=True)
        dcps_ref[2:3, :] += jnp.sum(dconv * z, axis=0, keepdims=True)
        du_a = dz * c_a
        dc_a = dz * u_a

        dmixed = (dy_p * fp_ref[:, :POOL_W].astype(F32)).astype(BF16)
        dg_p = dy_p * fp_ref[:, POOL_W:2 * POOL_W].astype(F32)
        dcps_ref[3:4, :] += jnp.sum(dy_p * fp_ref[:, 2 * POOL_W:3 * POOL_W].astype(F32), axis=0, keepdims=True)
        counts = _window_counts(tile * BWD_TILE, BWD_TILE)
        du_p, q_head = [], []
        for g, w in enumerate(POOL_WINDOWS):
            cols = slice(g * GROUP_D, (g + 1) * GROUP_D)
            dm_g = dmixed[:, cols]
            dpooled_g = _dot_nt(dm_g, wpool_ref[g])
            gwpool_acc[g] += _dot_tn(fp_ref[:, 3 * POOL_W + g * GROUP_D:3 * POOL_W + (g + 1) * GROUP_D], dm_g)
            q_g = dpooled_g * (1.0 / jnp.minimum(counts, float(w)))
            q_head.append(q_g[:POOL_HALO])
            s = jnp.concatenate([q_g, qc_ref[:, cols]], axis=0)
            step = 1
            while step < w:
                s = s + _rows_from_after(s, step)
                step *= 2
            du_p.append(s[:BWD_TILE] - dpooled_g)
        qc_ref[...] = jnp.concatenate(q_head, axis=1)
        dproj_b = jnp.concatenate([du_a, db_a, dc_a, dg_a] + du_p + [dg_p], axis=1).astype(BF16)
        dproj_ref[...] = dproj_b

        x_t = x_ref[...]
        rx = lax.rsqrt(jnp.mean(x_t * x_t, axis=-1, keepdims=True) + NORM_EPS)
        xn = x_t * rx
        mod_scale = 1.0 + scale
        dh = _dot_nt(dproj_b, win_ref[...])
        dvec_ref[0:1, :] += jnp.sum(dh, axis=0, keepdims=True)
        dvec_ref[1:2, :] += jnp.sum(dh * xn, axis=0, keepdims=True)
        dxn = dh * (g_pre * mod_scale)
        dx_ref[...] = dxo_t + rx * (dxn - xn * jnp.mean(dxn * xn, axis=-1, keepdims=True))

        @pl.when(i == n_tiles - 1)
        def _():
            gwpool_ref[...] = gwpool_acc[...].astype(BF16)
            sum_dh_xn, sum_dxo_yh = dvec_ref[1:2, :], dvec_ref[2:3, :]
            dvec_ref[1:2, :] = sum_dh_xn * g_pre
            dvec_ref[3:4, :] = sum_dh_xn * mod_scale
            dvec_ref[2:3, :] = sum_dxo_yh * g_post
            dvec_ref[4:5, :] = sum_dxo_yh * gate

    return pl.pallas_call(
        body, name=f"backward_layer_{layer}", grid=(n_tiles,),
        in_specs=[rev(D_MODEL), rev(D_MODEL), rev(D_MODEL), rev(2 * CONV_W), halo_spec, rev(3 * CONV_W),
                  rev(4 * POOL_W), _layer_spec(vec.shape, layer), _layer_spec(cps.shape, layer),
                  _layer_spec(wpool.shape, layer), _whole_spec(wout.shape), _whole_spec(win.shape)],
        out_specs=[rev(D_MODEL), rev(IN_COLS), rev(D_MODEL), _whole_spec(gwpool_shape),
                   _whole_spec((SUBLANES, CONV_W)), _whole_spec((SUBLANES, D_MODEL))],
        out_shape=[jax.ShapeDtypeStruct((t_len, D_MODEL), F32), jax.ShapeDtypeStruct((t_len, IN_COLS), BF16),
                   jax.ShapeDtypeStruct((t_len, D_MODEL), BF16), jax.ShapeDtypeStruct(gwpool_shape, BF16),
                   jax.ShapeDtypeStruct((SUBLANES, CONV_W), F32), jax.ShapeDtypeStruct((SUBLANES, D_MODEL), F32)],
        scratch_shapes=[pltpu.VMEM(gwpool_shape, F32), pltpu.VMEM((CONV_HALO, CONV_W), F32),
                        pltpu.VMEM((POOL_HALO, POOL_W), F32)],
        compiler_params=_params(dimension_semantics=("arbitrary",)),
    )(dxo, y, x, uc, uc, fa, fp, vec, cps, wpool, wout, win)


def _weight_grads(layer, h, dproj, ycat, dy, exchange, after=()):
    t_len = dy.shape[0]
    n_in, n_out = IN_COLS // GWIN_COLS, D_MODEL // GWOUT_COLS
    args = [h, dproj, ycat, dy, *after]
    in_specs = [_whole_spec(h.shape),
                pl.BlockSpec((t_len, GWIN_COLS), lambda s: (0, jnp.minimum(s, n_in - 1))),
                _whole_spec(ycat.shape),
                pl.BlockSpec((t_len, GWOUT_COLS), lambda s: (0, jnp.maximum(s - n_in, 0)))] + [HBM] * len(after)
    out_shape = [jax.ShapeDtypeStruct((D_MODEL, IN_COLS), BF16), jax.ShapeDtypeStruct((D_MODEL, D_MODEL), BF16)]
    out_specs = [pl.BlockSpec((D_MODEL, GWIN_COLS), lambda s: (0, jnp.minimum(s, n_in - 1))),
                 pl.BlockSpec((D_MODEL, GWOUT_COLS), lambda s: (0, jnp.maximum(s - n_in, 0)))]
    scratch = []
    aliases, split = _host(exchange, args, in_specs, out_shape, out_specs, scratch)

    def body(*refs):
        (h_ref, dproj_ref, ycat_ref, dy_ref, *_, gwin_ref, gwout_ref), hosted = split(refs)
        s = pl.program_id(0)
        if hosted is not None:
            pl.when(s == 0)(hosted[0])

        @pl.when(s < n_in)
        def _():
            gwin_ref[...] = _dot_tn(h_ref[...], dproj_ref[...]).astype(BF16)

        @pl.when(s >= n_in)
        def _():
            gwout_ref[...] = _dot_tn(ycat_ref[...], dy_ref[...]).astype(BF16)

        if hosted is not None:
            pl.when(s == n_in + n_out - 1)(hosted[1])

    return pl.pallas_call(
        body, name=f"weight_grads_{layer}", grid=(n_in + n_out,), in_specs=in_specs, out_specs=out_specs,
        out_shape=out_shape, scratch_shapes=scratch, input_output_aliases=aliases,
        compiler_params=_params(dimension_semantics=("arbitrary",)),
    )(*args)


def _add_sibling_blocks(name, layer, grads, received, core, partials, kinds):
    n_arr = len(grads)

    def body(core_ref, *refs):
        mine, theirs, outs = refs[:n_arr], refs[n_arr:2 * n_arr], refs[-n_arr:]
        for a in range(n_arr):
            outs[a][...] = (mine[a][...].astype(F32) + theirs[a][...].astype(F32)).astype(BF16)

    own_of_kind = [
        pl.BlockSpec((D_MODEL, W_IN_SHARD), lambda q, core_ref: (0, 2 * q + core_ref[0])),
        pl.BlockSpec((W_OUT_SHARD, D_MODEL), lambda q, core_ref: (2 * q + core_ref[0], 0)),
        pl.BlockSpec((POOL_SHARD, GROUP_D), lambda q, core_ref: (2 * q + core_ref[0], 0)),
    ]
    shapes = [_BLOCK_SHAPES[k] for k in kinds]
    recv_specs = [pl.BlockSpec((None,) + s, lambda q, core_ref: (q, 0, 0)) for s in shapes]
    out_specs = [pl.BlockSpec((None, None) + s, lambda q, core_ref: (q, layer, 0, 0)) for s in shapes]
    args = [core, *grads, *received]
    in_specs = [own_of_kind[k] for k in kinds] + recv_specs
    aliases = {}
    if partials is not None:
        aliases = {len(args) + a: a for a in range(n_arr)}
        args += list(partials)
        in_specs += [HBM] * n_arr
    return pl.pallas_call(
        body, name=name,
        grid_spec=pltpu.PrefetchScalarGridSpec(
            num_scalar_prefetch=1, grid=(N_CHIP,), in_specs=in_specs, out_specs=out_specs),
        out_shape=[jax.ShapeDtypeStruct((N_CHIP, DEPTH) + s, BF16) for s in shapes],
        input_output_aliases=aliases,
        compiler_params=_params(dimension_semantics=("arbitrary",)),
    )(*args)


def _modulation_columns(c_all, w_ada):
    def body(c_ref, w_ref, cact_ref, out_ref):
        c_t = c_ref[...]
        c_act = c_t * _sigmoid(c_t)
        cact_ref[...] = c_act
        out_ref[...] = jnp.dot(c_act, w_ref[...], preferred_element_type=F32, precision=lax.Precision.HIGHEST)

    return pl.pallas_call(
        body, name="modulation_columns", grid=(DEPTH,),
        in_specs=[pl.BlockSpec((N_DEV, D_MODEL), lambda l: (0, 0)),
                  pl.BlockSpec((None, D_MODEL, W_IN_SHARD), lambda l: (l, 0, 0))],
        out_specs=[pl.BlockSpec((N_DEV, D_MODEL), lambda l: (0, 0)),
                   pl.BlockSpec((N_DEV, W_IN_SHARD), lambda l: (0, l))],
        out_shape=[jax.ShapeDtypeStruct((N_DEV, D_MODEL), F32),
                   jax.ShapeDtypeStruct((N_DEV, DEPTH * W_IN_SHARD), F32)],
        compiler_params=_params(dimension_semantics=("arbitrary",)),
    )(c_all, w_ada)


def _adamw(w, g, m, v):
    m_new = ADAM_B1 * m + (1.0 - ADAM_B1) * g
    v_new = ADAM_B2 * v + (1.0 - ADAM_B2) * (g * g)
    m_hat = m_new / (1.0 - ADAM_B1 ** ADAM_STEP)
    v_hat = v_new / (1.0 - ADAM_B2 ** ADAM_STEP)
    delta = -ADAM_LR * (m_hat / (jnp.sqrt(v_hat) + ADAM_EPS) + ADAM_WD * w)
    return delta, m_new, v_new


def _adamw_w_ada(w, m, v, c_act_t, dmod_cols):
    def body(w_ref, m_ref, v_ref, ct_ref, dm_ref, g_ref, d_ref, mo_ref, vo_ref):
        g = ct_ref[:, 0:1] * dm_ref[0:1, :]
        for b in range(1, N_DEV):
            g = g + ct_ref[:, b:b + 1] * dm_ref[b:b + 1, :]
        g_ref[...] = g
        d_ref[...], mo_ref[...], vo_ref[...] = _adamw(w_ref[...], g, m_ref[...], v_ref[...])

    big = pl.BlockSpec((None, D_MODEL, W_IN_SHARD), lambda l: (l, 0, 0))
    return pl.pallas_call(
        body, name="adamw_w_ada", grid=(DEPTH,),
        in_specs=[big, big, big, pl.BlockSpec((D_MODEL, N_DEV), lambda l: (0, 0)),
                  pl.BlockSpec((None, N_DEV, W_IN_SHARD), lambda l: (l, 0, 0))],
        out_specs=[big] * 4, out_shape=[jax.ShapeDtypeStruct(w.shape, F32)] * 4,
        compiler_params=_params(dimension_semantics=("arbitrary",)),
    )(w, m, v, c_act_t, dmod_cols)


def _sum_chip_partials(own_ref, recv_ref):
    g = own_ref[...].astype(F32)
    for j in range(N_OTHER_CHIPS):
        g = g + recv_ref[j].astype(F32)
    return g


def _partial_specs(row_tile, cols, first_layer=0):
    own = pl.BlockSpec((None, None, row_tile, cols), lambda l, r, chip_ref: (chip_ref[0], first_layer + l, r, 0))
    recv = pl.BlockSpec((N_OTHER_CHIPS, None, row_tile, cols), lambda l, r, chip_ref: (0, first_layer + l, r, 0))
    return own, recv


def _adamw_reduced(name, w, m, v, partial, received, chip, row_tile, layers, continued):
    depth, rows, cols = w.shape
    first, stop = layers

    def body(chip_ref, w_ref, m_ref, v_ref, own_ref, recv_ref, *rest):
        g_ref, d_ref, mo_ref, vo_ref = rest[-4:]
        g = _sum_chip_partials(own_ref, recv_ref)
        g_ref[...] = g
        d_ref[...], mo_ref[...], vo_ref[...] = _adamw(w_ref[...], g, m_ref[...], v_ref[...])

    blk = pl.BlockSpec((None, row_tile, cols), lambda l, r, chip_ref: (first + l, r, 0))
    args = [chip, w, m, v, partial, received]
    in_specs = [blk, blk, blk, *_partial_specs(row_tile, cols, first)]
    aliases = {}
    if continued is not None:
        aliases = {len(args) + k: k for k in range(4)}
        args += list(continued)
        in_specs += [HBM] * 4
    return pl.pallas_call(
        body, name=name,
        grid_spec=pltpu.PrefetchScalarGridSpec(
            num_scalar_prefetch=1, grid=(stop - first, rows // row_tile), in_specs=in_specs, out_specs=[blk] * 4),
        out_shape=[jax.ShapeDtypeStruct(w.shape, F32)] * 4, input_output_aliases=aliases,
        compiler_params=_params(dimension_semantics=("arbitrary", "arbitrary")),
    )(*args)


def _reduce_w_pool(partial, received, chip):
    def body(chip_ref, own_ref, recv_ref, g_ref):
        g_ref[...] = _sum_chip_partials(own_ref, recv_ref)

    return pl.pallas_call(
        body, name="reduce_w_pool",
        grid_spec=pltpu.PrefetchScalarGridSpec(
            num_scalar_prefetch=1, grid=(DEPTH, 1), in_specs=list(_partial_specs(POOL_SHARD, GROUP_D)),
            out_specs=pl.BlockSpec((POOL_SHARD, GROUP_D), lambda l, r, chip_ref: (l, 0))),
        out_shape=jax.ShapeDtypeStruct((DEPTH * POOL_SHARD, GROUP_D), F32),
        compiler_params=_params(dimension_semantics=("arbitrary", "arbitrary")),
    )(chip, partial, received)


def _adamw_small(params):
    n = len(params)

    def body(*refs):
        ins, outs = refs[:4 * n], refs[4 * n:]
        for p in range(n):
            w_ref, g_ref, m_ref, v_ref = ins[4 * p:4 * p + 4]
            d_ref, mo_ref, vo_ref = outs[3 * p:3 * p + 3]
            d_ref[...], mo_ref[...], vo_ref[...] = _adamw(w_ref[...], g_ref[...], m_ref[...], v_ref[...])

    vmem = pl.BlockSpec(memory_space=pltpu.VMEM)
    flat = [a for group in params for a in group]
    out_shape = [jax.ShapeDtypeStruct(group[0].shape, F32) for group in params for _ in range(3)]
    outs = pl.pallas_call(
        body, name="adamw_small", in_specs=[vmem] * len(flat), out_specs=[vmem] * len(out_shape),
        out_shape=out_shape, compiler_params=_params(),
    )(*flat)
    return [tuple(outs[3 * p:3 * p + 3]) for p in range(n)]


def _sum_sources(slabs):
    def body(s_ref, o_ref):
        acc = s_ref[0]
        for b in range(1, N_DEV):
            acc = acc + s_ref[b]
        o_ref[...] = acc

    vmem = pl.BlockSpec(memory_space=pltpu.VMEM)
    return pl.pallas_call(
        body, name="sum_small_grads", in_specs=[vmem], out_specs=vmem,
        out_shape=jax.ShapeDtypeStruct(slabs.shape[1:], F32), compiler_params=_params(),
    )(slabs)


def _to_bf16(a, name, layers=None):
    first, stop = layers or (0, a.shape[0])

    def body(a_ref, o_ref):
        o_ref[...] = a_ref[...].astype(BF16)

    block = (None,) + a.shape[1:]
    return pl.pallas_call(
        body, name=name, grid=(stop - first,), in_specs=[pl.BlockSpec(block, lambda l: (first + l, 0, 0))],
        out_specs=pl.BlockSpec(block, lambda l: (l, 0, 0)),
        out_shape=jax.ShapeDtypeStruct((stop - first,) + a.shape[1:], BF16),
        compiler_params=_params(dimension_semantics=("arbitrary",)),
    )(a)


def kernel(x, c, w_ada, b_ada, g_pre, w_in, w_conv, w_pool, pool_scale, w_out, g_post, loss_target, m_w_ada, m_b_ada, m_g_pre, m_w_in, m_w_conv, m_w_pool, m_pool_scale, m_w_out, m_g_post, v_w_ada, v_b_ada, v_g_pre, v_w_in, v_w_conv, v_w_pool, v_pool_scale, v_w_out, v_g_post):
    mx, my, mc = _mesh_position()
    me = _block_id(mx, my, mc)
    chip = (2 * mx + my).astype(jnp.int32).reshape(1)
    core = mc.astype(jnp.int32).reshape(1)
    x0 = x[0]
    target = loss_target[0]
    conv_shard = w_conv.shape[-1]

    own_small = jnp.concatenate([c, w_conv.reshape(1, DEPTH * 3 * conv_shard)], axis=1)
    first_shards = [_to_bf16(w_in, "cast_w_in_0", (0, 1)), _to_bf16(w_out, "cast_w_out_0", (0, 1))]
    all_small = _all_gather_small(own_small, "all_gather_c_w_conv", first_shards)[:, 0, :]
    gathers = [_gather_weights_start(0, *first_shards, [all_small], relayed=True)]
    c_all = all_small[:, :D_MODEL]
    w_conv_full = all_small[:, D_MODEL:].reshape(N_DEV, DEPTH, 3, conv_shard).transpose(1, 2, 0, 3).reshape(
        DEPTH, 3, CONV_W)
    cps = jnp.concatenate([w_conv_full, pool_scale[:, None], jnp.zeros((DEPTH, 4, CONV_W), F32)], axis=1)

    c_act, pieces = _modulation_columns(c_all, w_ada)
    upper = (1, DEPTH)
    upper_shards = [_to_bf16(w_in, "cast_w_in_1", upper), _to_bf16(w_out, "cast_w_out_1", upper)]
    wpool_b = _to_bf16(w_pool.reshape(DEPTH, POOL_ROWS, GROUP_D), "cast_w_pool").reshape(w_pool.shape)
    first_sems_0, _, (zones_0,) = gathers[0]
    neighbour_sems, zones_0 = _gather_weights_pass_on(
        "all_gather_weights_relay_0", NEIGHBOUR_CHIPS, first_sems_0[1], partial(_first_arrival, 0), zones_0,
        [pieces, *upper_shards, wpool_b], relay=True)
    mod_all = _all_gather_small(pieces, "all_gather_modulation", [zones_0[0]])
    mod_mine = lax.dynamic_index_in_dim(mod_all, me, axis=1, keepdims=False)
    mod = mod_mine.reshape(N_DEV, DEPTH, W_IN_SHARD).transpose(1, 0, 2).reshape(DEPTH, 3 * D_MODEL) + b_ada
    zeros_d = jnp.zeros((DEPTH, 3, D_MODEL), F32)
    vec = jnp.concatenate([mod.reshape(DEPTH, 3, D_MODEL), g_pre[:, None], g_post[:, None], zeros_d], axis=1)

    gathers.append(_gather_weights_start(1, *upper_shards, [mod_all]))
    gathers = [(first_sems, shards, landing[k], k) for first_sems, shards, landing in gathers
               for k in range(len(landing))]

    xs, kept, wins, wouts = [x0], [], [], []
    for l in range(DEPTH):
        first_sems, shards, zones, index = gathers[l]
        if l == 0:
            diagonal_sems, zones = _gather_weights_pass_on(
                "all_gather_weights_pass_on_0", DIAGONAL_CHIP, neighbour_sems[3], lambda a, j: a, zones_0,
                [vec, cps, gathers[-1][1][0]])
            passed = [(neighbour_sems[:2], NEIGHBOUR_CHIPS), (diagonal_sems, DIAGONAL_CHIP)]
            win, wout = _gather_weights_finish(l, index, first_sems, passed, shards, zones, neighbour_sems[2])
        else:
            passed_sems, zones = _gather_weights_pass_on(
                f"all_gather_weights_pass_on_{l}", ALL_OTHER_CHIPS, first_sems[1], partial(_first_arrival, index),
                zones, [xs[-1]])
            win, wout = _gather_weights_finish(l, index, first_sems, [(passed_sems, ALL_OTHER_CHIPS)], shards, zones)
        x_next, *for_backward = _forward_layer(
            l, xs[-1], vec, cps, wpool_b, win, wout, target if l == DEPTH - 1 else None)
        xs.append(x_next)
        kept.append(for_backward[:6])
        wins.append(win)
        wouts.append(wout)
    dx, loss_tile = xs[DEPTH], for_backward[6]

    slab_rows = [None] * DEPTH
    partials = received = None
    in_flight = []

    def scatter(layer, grads, from_sibling, after):
        nonlocal partials, received
        partials = _add_sibling_blocks(
            f"grad_add_sibling_{layer}", layer, grads, from_sibling, core, partials, ALL_KINDS)
        chips = _chips_exchange(layer, partials, received, ALL_KINDS)
        sems, partials, received, token = _start_exchange(chips, f"grad_chips_start_{layer}", after)
        in_flight.append((chips, sems, layer))
        return token

    grads_above = None
    for l in reversed(range(DEPTH)):
        y, h, ycat, uc, fa, fp = kept[l]
        dx, dproj, dy, gwpool, dcps, dvec = _backward_layer(
            l, dx, y, xs[l], uc, fa, fp, vec, cps, wpool_b, wouts[l], wins[l])
        slab_rows[l] = jnp.concatenate(
            [dvec[0], dvec[1], dvec[2], dvec[3], dvec[4], dcps[3], dcps[0], dcps[1], dcps[2],
             loss_tile[0] if l == 0 else jnp.zeros((LANES,), F32)])
        after = []
        if l == 0:
            gather_small = _all_gather_exchange(jnp.stack(slab_rows))
            sems_s, slab, slabs, token = _start_exchange(gather_small, "all_gather_small_grads_start")
            after = [token]
        hosted = _sibling_exchange(grads_above, ALL_KINDS) if grads_above is not None else None
        gwin, gwout, *from_sibling = _weight_grads(l, h, dproj, ycat, dy, hosted, after)
        if l == 0:
            _, (slabs,) = _finish_exchange(
                gather_small, "all_gather_small_grads_finish", sems_s, slab, slabs, [gwin])
        if grads_above is not None:
            scatter(l + 1, grads_above, from_sibling, [])
        grads_above = [gwin, gwout, gwpool.reshape(POOL_ROWS, GROUP_D)]
        if l <= 1:
            from_sibling = _run_exchange(_sibling_exchange(grads_above, ALL_KINDS), f"grad_exchange_sibling_{l}")
            token = scatter(l, grads_above, from_sibling, [slabs] if l == 0 else [])
            grads_above = None
    grad_x = dx[None]
    chips_0, sems_0, _ = in_flight.pop()

    total = _sum_sources(slabs)
    loss = total[0, SLAB_COLS]
    o = 3 * D_MODEL
    g_b_ada = total[:, :o]
    g_g_pre = total[:, o:o + D_MODEL]
    g_g_post = total[:, o + D_MODEL:o + 2 * D_MODEL]
    g_pool_scale = total[:, o + 2 * D_MODEL:o + 2 * D_MODEL + POOL_W]
    g_conv_full = total[:, o + 2 * D_MODEL + POOL_W:SLAB_COLS].reshape(DEPTH, 3, CONV_W)
    g_w_conv = lax.dynamic_slice_in_dim(g_conv_full, me * conv_shard, conv_shard, axis=2)

    after = [token]
    for chips, sems, l in in_flight:
        partials, received = _finish_exchange(chips, f"grad_chips_finish_{l}", sems, partials, received, after)
        after = []
    upper = (1, DEPTH)
    w_in_upper = _adamw_reduced(
        "adamw_w_in_upper", w_in, m_w_in, v_w_in, partials[0], received[0], chip, ROW_TILE, upper, None)
    w_out_upper = _adamw_reduced(
        "adamw_w_out_upper", w_out, m_w_out, v_w_out, partials[1], received[1], chip, W_OUT_SHARD, upper, None)
    dmod_all = slabs[:, :, :o].reshape(N_DEV, DEPTH, N_DEV, W_IN_SHARD)
    dmod_cols = lax.dynamic_index_in_dim(dmod_all, me, axis=2, keepdims=False).transpose(1, 0, 2) + token[0, 0]
    g_w_ada, d_w_ada, nm_w_ada, nv_w_ada = _adamw_w_ada(w_ada, m_w_ada, v_w_ada, c_act.T, dmod_cols)

    partials, received = _finish_exchange(
        chips_0, "grad_chips_finish_0", sems_0, partials, received, [nv_w_ada, w_in_upper[3], w_out_upper[3]])
    gather_pool = _all_gather_exchange(_reduce_w_pool(partials[2], received[2], chip))
    sems_p, pool_rows, pool_landing, token_p = _start_exchange(gather_pool, "all_gather_grad_w_pool_start")
    g_w_in, d_w_in, nm_w_in, nv_w_in = _adamw_reduced(
        "adamw_w_in_0", w_in, m_w_in, v_w_in, partials[0], received[0], chip, ROW_TILE, (0, 1), w_in_upper)
    g_w_out, d_w_out, nm_w_out, nv_w_out = _adamw_reduced(
        "adamw_w_out_0", w_out, m_w_out, v_w_out, partials[1], received[1], chip, W_OUT_SHARD, (0, 1), w_out_upper)
    _, (g_pool_all,) = _finish_exchange(
        gather_pool, "all_gather_grad_w_pool_finish", sems_p, pool_rows, pool_landing, [nv_w_in, nv_w_out])
    g_w_pool = g_pool_all.reshape(N_DEV, DEPTH, POOL_SHARD, GROUP_D).transpose(1, 0, 2, 3).reshape(w_pool.shape)

    flat2 = lambda a: a.reshape(-1, a.shape[-1])
    small = _adamw_small([
        (b_ada, g_b_ada, m_b_ada, v_b_ada),
        (g_pre, g_g_pre, m_g_pre, v_g_pre),
        (flat2(w_conv), flat2(g_w_conv), flat2(m_w_conv), flat2(v_w_conv)),
        (flat2(w_pool), flat2(g_w_pool), flat2(m_w_pool), flat2(v_w_pool)),
        (pool_scale, g_pool_scale, m_pool_scale, v_pool_scale),
        (g_post, g_g_post, m_g_post, v_g_post),
    ])
    (d_b_ada, nm_b_ada, nv_b_ada), (d_g_pre, nm_g_pre, nv_g_pre), conv_upd, pool_upd, \
        (d_ps, nm_ps, nv_ps), (d_g_post, nm_g_post, nv_g_post) = small
    d_w_conv, nm_w_conv, nv_w_conv = (a.reshape(w_conv.shape) for a in conv_upd)
    d_w_pool, nm_w_pool, nv_w_pool = (a.reshape(w_pool.shape) for a in pool_upd)

    return (loss, grad_x,
            g_w_ada, g_b_ada, g_g_pre, g_w_in, g_w_conv, g_w_pool, g_pool_scale, g_w_out, g_g_post,
            d_w_ada, d_b_ada, d_g_pre, d_w_in, d_w_conv, d_w_pool, d_ps, d_w_out, d_g_post,
            nm_w_ada, nm_b_ada, nm_g_pre, nm_w_in, nm_w_conv, nm_w_pool, nm_ps, nm_w_out, nm_g_post,
            nv_w_ada, nv_b_ada, nv_g_pre, nv_w_in, nv_w_conv, nv_w_pool, nv_ps, nv_w_out, nv_g_post)
```

```python
from functools import partial

import jax
import jax.numpy as jnp
from jax import lax
from jax.experimental import pallas as pl
from jax.experimental.pallas import tpu as pltpu

F32 = jnp.float32
BF16 = jnp.bfloat16

D_MODEL = 1024
DEPTH = 4
CONV_W = 512
POOL_W = 512
POOL_WINDOWS = (2, 4, 8, 16)
GROUP_D = 128
IN_COLS = 4 * CONV_W + 2 * POOL_W
NORM_EPS = 1e-6

ADAM_LR = 0.001
ADAM_B1 = 0.9
ADAM_B2 = 0.999
ADAM_EPS = 1e-08
ADAM_WD = 0.01
ADAM_STEP = 10

N_DEV = 8
N_CHIP = 4
N_OTHER_CHIPS = N_CHIP - 1
MESH = pl.DeviceIdType.MESH
W_IN_SHARD = IN_COLS // N_DEV
W_OUT_SHARD = D_MODEL // N_DEV
POOL_ROWS = len(POOL_WINDOWS) * GROUP_D
POOL_SHARD = POOL_ROWS // N_DEV

SUBLANES = 8
LANES = 128
VMEM_LIMIT_BYTES = 56 * 1024 * 1024
ROW_TILE = 512
BWD_TILE = 256
GWIN_COLS = 768
GWOUT_COLS = 512
POOL_HALO = 16
CONV_HALO = SUBLANES

SLAB_COLS = 3 * D_MODEL + D_MODEL + D_MODEL + POOL_W + 3 * CONV_W

HBM = pl.BlockSpec(memory_space=pl.ANY)


def _params(**kw):
    return pltpu.CompilerParams(vmem_limit_bytes=VMEM_LIMIT_BYTES, **kw)


def _sigmoid(v):
    return 1.0 / (1.0 + jnp.exp(-v))


def _dot(a, b):
    return jnp.dot(a, b, preferred_element_type=F32)


def _dot_tn(a, b):
    return lax.dot_general(a, b, (((0,), (0,)), ((), ())), preferred_element_type=F32)


def _dot_nt(a, b):
    return lax.dot_general(a, b, (((1,), (1,)), ((), ())), preferred_element_type=F32)


def _rows_from_before(v, k):
    return pltpu.roll(v, k, 0)


def _rows_from_after(v, k):
    return pltpu.roll(v, v.shape[0] - k, 0)


def _window_counts(t0, rows):
    return (lax.broadcasted_iota(jnp.int32, (rows, 1), 0) + (t0 + 1)).astype(F32)


def _split_proj(p32):
    cw = CONV_W
    return (p32[:, 0 * cw:1 * cw], p32[:, 1 * cw:2 * cw], p32[:, 2 * cw:3 * cw], p32[:, 3 * cw:4 * cw],
            p32[:, 4 * cw:4 * cw + POOL_W], p32[:, 4 * cw + POOL_W:])


def _layer_spec(shape, layer):
    nd = len(shape)
    return pl.BlockSpec((None,) + tuple(shape[1:]), lambda i, _l=layer, _n=nd: (_l,) + (0,) * (_n - 1))


def _whole_spec(shape):
    return pl.BlockSpec(tuple(shape), lambda i, _n=len(shape): (0,) * _n, pipeline_mode=pl.Buffered(1))


def _mesh_position():
    return lax.axis_index("x"), lax.axis_index("y"), lax.axis_index("c")


def _block_id(x, y, c):
    return 4 * x + 2 * y + c


def _other_chips(x, y):
    return [(x ^ 1, y), (x, y ^ 1), (x ^ 1, y ^ 1)]


def _col_block(ref, blk):
    return ref.at[:, pl.ds(pl.multiple_of(blk * W_IN_SHARD, LANES), W_IN_SHARD)]


def _row_block(rows):
    def block(ref, blk):
        return ref.at[pl.ds(pl.multiple_of(blk * rows, rows), rows), :]
    return block


_BLOCK_OF = (_col_block, _row_block(W_OUT_SHARD), _row_block(POOL_SHARD))
_BLOCK_SHAPES = ((D_MODEL, W_IN_SHARD), (W_OUT_SHARD, D_MODEL), (POOL_SHARD, GROUP_D))


class _Exchange:
    def __init__(self, inputs, out_shapes, aliases, sem_shapes, make):
        self.inputs, self.out_shapes, self.aliases, self.sem_shapes, self.make = (
            list(inputs), list(out_shapes), dict(aliases), list(sem_shapes), make)


def _run_exchange(exchange, name):
    n_in, n_out = len(exchange.inputs), len(exchange.out_shapes)

    def body(*refs):
        start, finish = exchange.make(refs[:n_in], refs[n_in:n_in + n_out], refs[n_in + n_out:])
        start()
        finish()

    return pl.pallas_call(
        body, name=name, in_specs=[HBM] * n_in, out_specs=[HBM] * n_out, out_shape=exchange.out_shapes,
        scratch_shapes=exchange.sem_shapes, input_output_aliases=exchange.aliases, compiler_params=_params(),
    )(*exchange.inputs)


_SEM = pl.BlockSpec(memory_space=pltpu.SEMAPHORE)
_DATAFLOW = pltpu.SideEffectType.DATAFLOW_SIDE_EFFECTING


def _start_exchange(exchange, name, after=()):
    n_in, n_out, n_sem = len(exchange.inputs), len(exchange.out_shapes), len(exchange.sem_shapes)
    sources = [i for i in range(n_in) if i not in exchange.aliases]
    aliases = {i: n_sem + k for k, i in enumerate(sources)}
    aliases.update({i: n_sem + len(sources) + o for i, o in exchange.aliases.items()})

    def body(*refs):
        in_refs = refs[:n_in]
        outs = refs[n_in + len(after):]
        sems = outs[:n_sem]
        out_refs = outs[n_sem + len(sources):n_sem + len(sources) + n_out]
        exchange.make(in_refs, out_refs, sems)[0]()
        refs[-1][...] = jnp.zeros_like(refs[-1])

    outs = pl.pallas_call(
        body, name=name, in_specs=[HBM] * (n_in + len(after)),
        out_specs=[_SEM] * n_sem + [HBM] * (len(sources) + n_out) + [pl.BlockSpec(memory_space=pltpu.VMEM)],
        out_shape=(exchange.sem_shapes + [pltpu.HBM(exchange.inputs[i].shape, exchange.inputs[i].dtype) for i in sources]
                   + [pltpu.HBM(s.shape, s.dtype) for s in exchange.out_shapes]
                   + [jax.ShapeDtypeStruct((SUBLANES, LANES), F32)]),
        input_output_aliases=aliases, compiler_params=_params(has_side_effects=_DATAFLOW),
    )(*exchange.inputs, *after)
    return outs[:n_sem], outs[n_sem:n_sem + len(sources)], outs[n_sem + len(sources):-1], outs[-1]


def _finish_exchange(exchange, name, sems, sources, landing, after):
    n_src, n_out, n_sem = len(sources), len(landing), len(sems)
    n_in = len(exchange.inputs)
    source_at = [i for i in range(n_in) if i not in exchange.aliases]

    def body(*refs):
        src_refs, out_refs = refs[:n_src], refs[n_src:n_src + n_out]
        sem_refs = refs[n_src + n_out:n_src + n_out + n_sem]
        in_refs = [None] * n_in
        for k, i in enumerate(source_at):
            in_refs[i] = src_refs[k]
        for i, o in exchange.aliases.items():
            in_refs[i] = out_refs[o]
        exchange.make(in_refs, out_refs, sem_refs)[1]()

    arrays = list(sources) + list(landing)
    outs = pl.pallas_call(
        body, name=name, in_specs=[HBM] * len(arrays) + [_SEM] * n_sem + [HBM] * len(after),
        out_specs=[HBM] * len(arrays), out_shape=[pltpu.HBM(a.shape, a.dtype) for a in arrays],
        input_output_aliases={i: i for i in range(len(arrays))}, compiler_params=_params(has_side_effects=_DATAFLOW),
    )(*arrays, *sems, *after)
    return outs[:n_src], outs[n_src:]


N_GATHERED = 2
FIRST_COPIES = 1 + N_OTHER_CHIPS
_GATHERED_SHAPES = ((D_MODEL, IN_COLS), (D_MODEL, D_MODEL))
ALL_OTHER_CHIPS, NEIGHBOUR_CHIPS, DIAGONAL_CHIP = (0, 1, 2), (0, 1), (2,)


def _first_sem(layer, a, k):
    return (layer * N_GATHERED + a) * FIRST_COPIES + k


def _first_arrival(layer, a, j):
    return _first_sem(layer, a, 1 + j)


def _gather_copy(window_of, full_ref, blk, send_sem, recv_sem, to, src=None):
    window = window_of(full_ref, blk)
    return pltpu.make_async_remote_copy(
        src_ref=window if src is None else src, dst_ref=window, send_sem=send_sem, recv_sem=recv_sem,
        device_id=to, device_id_type=MESH)


def _first_copies(layer, shard_refs, full_refs, send_sems, recv_sems, local_sems, relayed):
    x, y, c = _mesh_position()
    me = _block_id(x, y, c)
    chips = [_other_chips(x, y)[j] for j in (NEIGHBOUR_CHIPS if relayed else ALL_OTHER_CHIPS)]
    own, remote = [], []
    for a in range(N_GATHERED):
        shard = shard_refs[a].at[layer]
        own.append(pltpu.make_async_copy(
            shard, _BLOCK_OF[a](full_refs[a], me), local_sems.at[layer * N_GATHERED + a]))
        targets = [(x, y, 1 - c)] + [(*chip, c) for chip in chips]
        remote += [_gather_copy(_BLOCK_OF[a], full_refs[a], me, send_sems.at[_first_sem(layer, a, k)],
                                recv_sems.at[_first_sem(layer, a, k)], to, src=shard)
                   for k, to in enumerate(targets)]
    return own, remote


def _gather_weights_start(first_layer, win_shards, wout_shards, after, relayed=False):
    n_layers = win_shards.shape[0]
    n_first = n_layers * N_GATHERED * FIRST_COPIES
    sem_shapes = [pltpu.SemaphoreType.DMA((n_first,)), pltpu.SemaphoreType.DMA((n_first,)),
                  pltpu.SemaphoreType.DMA((n_layers * N_GATHERED,))]
    shards = [win_shards, wout_shards]

    def body(win_sh, wout_sh, *rest):
        send_sems, recv_sems, local_sems, win_thru, wout_thru, *landing = rest[len(after):]
        for layer in range(n_layers):
            own, remote = _first_copies(layer, (win_sh, wout_sh), landing[N_GATHERED * layer:N_GATHERED * (layer + 1)],
                                        send_sems, recv_sems, local_sems, relayed)
            for cp in own + remote:
                cp.start()

    outs = pl.pallas_call(
        body, name=f"all_gather_weights_start_{first_layer}", in_specs=[HBM] * (2 + len(after)),
        out_specs=[_SEM] * 3 + [HBM] * (2 + n_layers * N_GATHERED),
        out_shape=(sem_shapes + [pltpu.HBM(s.shape, s.dtype) for s in shards]
                   + [pltpu.HBM(s, BF16) for _ in range(n_layers) for s in _GATHERED_SHAPES]),
        input_output_aliases={0: 3, 1: 4}, compiler_params=_params(has_side_effects=_DATAFLOW),
    )(*shards, *after)
    landing = outs[5:]
    return outs[:3], outs[3:5], [landing[N_GATHERED * l:N_GATHERED * (l + 1)] for l in range(n_layers)]


def _passed_on_copies(full_refs, send_sems, recv_sems, core_of_block, chips):
    x, y, c = _mesh_position()
    return [_gather_copy(_BLOCK_OF[a], full_refs[a], _block_id(*_other_chips(x, y)[j], core_of_block),
                         send_sems.at[a * N_OTHER_CHIPS + j], recv_sems.at[a * N_OTHER_CHIPS + j], (x, y, 1 - c))
            for a in range(N_GATHERED) for j in chips]


def _relayed_copies(full_refs, send_sems, recv_sems):
    x, y, c = _mesh_position()
    source, to = (x ^ (1 - c), y ^ c), (x ^ c, y ^ (1 - c))
    return [_gather_copy(_BLOCK_OF[a], full_refs[a], _block_id(*source, c), send_sems.at[a], recv_sems.at[a], (*to, c))
            for a in range(N_GATHERED)]


def _gather_weights_pass_on(name, chips, arrival_sems, arrival_sem_of, landing, after, relay=False):
    n = N_GATHERED * N_OTHER_CHIPS
    sem_shapes = [pltpu.SemaphoreType.DMA((n,))] * 2 + [pltpu.SemaphoreType.DMA((N_GATHERED,))] * (2 if relay else 0)

    def body(win_ref, wout_ref, arrivals, *rest):
        sems = rest[len(after):len(after) + len(sem_shapes)]
        x, y, c = _mesh_position()
        full_refs = (win_ref, wout_ref)
        passed = _passed_on_copies(full_refs, sems[0], sems[1], c, chips)
        relayed = _relayed_copies(full_refs, sems[2], sems[3]) if relay else []
        for a in range(N_GATHERED):
            for j in chips:
                sem = arrivals.at[arrival_sem_of(a, j)]
                _gather_copy(_BLOCK_OF[a], full_refs[a], _block_id(*_other_chips(x, y)[j], c), sem, sem,
                             (x, y, c)).wait_recv()
            for cp in relayed[a:a + 1] + passed[a * len(chips):(a + 1) * len(chips)]:
                cp.start()

    outs = pl.pallas_call(
        body, name=name, in_specs=[HBM] * N_GATHERED + [_SEM] + [HBM] * len(after),
        out_specs=[_SEM] * len(sem_shapes) + [HBM] * N_GATHERED,
        out_shape=sem_shapes + [pltpu.HBM(a.shape, a.dtype) for a in landing],
        input_output_aliases={a: len(sem_shapes) + a for a in range(N_GATHERED)},
        compiler_params=_params(has_side_effects=_DATAFLOW),
    )(*landing, arrival_sems, *after)
    return outs[:len(sem_shapes)], outs[len(sem_shapes):]


def _gather_weights_finish(layer, index, first_sems, passed, shards, landing, relay_send_sems=None):
    relayed = relay_send_sems is not None
    passed_sems = [sem for (send, recv), _ in passed for sem in (send, recv)] + ([relay_send_sems] if relayed else [])

    def body(win_ref, wout_ref, first_send, first_recv, local_sems, *rest):
        sems, (win_sh, wout_sh) = rest[:len(passed_sems)], rest[len(passed_sems):len(passed_sems) + 2]
        x, y, c = _mesh_position()
        full_refs = (win_ref, wout_ref)
        own, sent = _first_copies(index, (win_sh, wout_sh), full_refs, first_send, first_recv, local_sems, relayed)
        for a in range(N_GATHERED):
            sem = _first_sem(index, a, 0)
            _gather_copy(_BLOCK_OF[a], full_refs[a], _block_id(x, y, 1 - c), first_recv.at[sem], first_recv.at[sem],
                         (x, y, c)).wait_recv()
        for p, (_, chips) in enumerate(passed):
            for cp in _passed_on_copies(full_refs, sems[2 * p], sems[2 * p + 1], 1 - c, chips):
                cp.wait_recv()
            sent += _passed_on_copies(full_refs, sems[2 * p], sems[2 * p + 1], c, chips)
        if relayed:
            sent += _relayed_copies(full_refs, sems[-1], sems[-1])
        for cp in sent:
            cp.wait_send()
        for cp in own:
            cp.wait()

    return pl.pallas_call(
        body, name=f"all_gather_weights_finish_{layer}",
        in_specs=[HBM] * N_GATHERED + [_SEM] * (3 + len(passed_sems)) + [HBM] * 2,
        out_specs=[HBM] * N_GATHERED, out_shape=[pltpu.HBM(a.shape, a.dtype) for a in landing],
        input_output_aliases={a: a for a in range(N_GATHERED)}, compiler_params=_params(has_side_effects=_DATAFLOW),
    )(*landing, *first_sems, *passed_sems, *shards)


ALL_KINDS = (0, 1, 2)


def _sibling_exchange(grads, kinds):
    n_arr = len(grads)

    def make(in_refs, out_refs, sems):
        send_sems, recv_sems = sems
        x, y, c = _mesh_position()
        copies = [pltpu.make_async_remote_copy(
            src_ref=_BLOCK_OF[kinds[a]](in_refs[a], 2 * q + (1 - c)), dst_ref=out_refs[a].at[q],
            send_sem=send_sems.at[a * N_CHIP + q], recv_sem=recv_sems.at[a * N_CHIP + q],
            device_id=(x, y, 1 - c), device_id_type=MESH)
            for a in range(n_arr) for q in range(N_CHIP)]

        def start():
            for cp in copies:
                cp.start()

        def finish():
            for cp in copies:
                cp.wait_recv()
            for cp in copies:
                cp.wait_send()

        return start, finish

    return _Exchange(
        grads, [jax.ShapeDtypeStruct((N_CHIP,) + _BLOCK_SHAPES[k], BF16) for k in kinds], {},
        [pltpu.SemaphoreType.DMA((n_arr * N_CHIP,)), pltpu.SemaphoreType.DMA((n_arr * N_CHIP,))], make)


def _chips_exchange(layer, partials, received, kinds):
    n_arr = len(partials)

    def make(in_refs, out_refs, sems):
        send_sems, recv_sems = sems
        x, y, c = _mesh_position()
        copies = [pltpu.make_async_remote_copy(
            src_ref=in_refs[a].at[2 * qx + qy, layer], dst_ref=out_refs[a].at[j, layer],
            send_sem=send_sems.at[a * N_OTHER_CHIPS + j], recv_sem=recv_sems.at[a * N_OTHER_CHIPS + j],
            device_id=(qx, qy, c), device_id_type=MESH)
            for a in range(n_arr) for j, (qx, qy) in enumerate(_other_chips(x, y))]

        def start():
            for cp in copies:
                cp.start()

        def finish():
            for cp in copies:
                cp.wait_recv()
            for cp in copies:
                cp.wait_send()

        return start, finish

    inputs = list(partials)
    aliases = {}
    if received is not None:
        inputs += list(received)
        aliases = {n_arr + a: a for a in range(n_arr)}
    return _Exchange(
        inputs, [jax.ShapeDtypeStruct((N_OTHER_CHIPS, DEPTH) + _BLOCK_SHAPES[k], BF16) for k in kinds], aliases,
        [pltpu.SemaphoreType.DMA((n_arr * N_OTHER_CHIPS,)), pltpu.SemaphoreType.DMA((n_arr * N_OTHER_CHIPS,))], make)


def _all_gather_exchange(v):
    def make(in_refs, out_refs, sems):
        send_sems, recv_sems, local_sem = sems
        x, y, c = _mesh_position()
        me = _block_id(x, y, c)
        own = pltpu.make_async_copy(in_refs[0], out_refs[0].at[me], local_sem.at[0])
        sends, arrivals = [], []
        for k in range(1, N_DEV):
            px, py, pc = x ^ ((k >> 2) & 1), y ^ ((k >> 1) & 1), c ^ (k & 1)
            sends.append(pltpu.make_async_remote_copy(
                src_ref=in_refs[0], dst_ref=out_refs[0].at[me], send_sem=send_sems.at[k - 1],
                recv_sem=recv_sems.at[k - 1], device_id=(px, py, pc), device_id_type=MESH))
            arrivals.append(pltpu.make_async_remote_copy(
                src_ref=in_refs[0], dst_ref=out_refs[0].at[_block_id(px, py, pc)], send_sem=send_sems.at[k - 1],
                recv_sem=recv_sems.at[k - 1], device_id=(x, y, c), device_id_type=MESH))

        def start():
            for cp in [own] + sends:
                cp.start()

        def finish():
            for cp in arrivals:
                cp.wait_recv()
            for cp in sends:
                cp.wait_send()
            own.wait()

        return start, finish

    return _Exchange(
        [v], [jax.ShapeDtypeStruct((N_DEV,) + v.shape, v.dtype)], {},
        [pltpu.SemaphoreType.DMA((N_DEV - 1,)), pltpu.SemaphoreType.DMA((N_DEV - 1,)),
         pltpu.SemaphoreType.DMA((1,))], make)


def _host(exchange, args, in_specs, out_shape, out_specs, scratch):
    n_own = (len(args), len(out_shape), len(scratch))
    if exchange is None:
        return {}, lambda refs: (refs, None)
    n_ex = (len(exchange.inputs), len(exchange.out_shapes), len(exchange.sem_shapes))
    aliases = {n_own[0] + i: n_own[1] + o for i, o in exchange.aliases.items()}
    args += exchange.inputs
    in_specs += [HBM] * n_ex[0]
    out_shape += exchange.out_shapes
    out_specs += [HBM] * n_ex[1]
    scratch += exchange.sem_shapes

    def split(refs):
        own, theirs, at = [], [], 0
        for mine, ex in zip(n_own, n_ex):
            own += refs[at:at + mine]
            theirs.append(refs[at + mine:at + mine + ex])
            at += mine + ex
        return own, exchange.make(*theirs)

    return aliases, split


def _all_gather_small(v, name, after=()):
    vmem = pl.BlockSpec(memory_space=pltpu.VMEM)

    def body(v_ref, *rest):
        out_ref, send_sems, recv_sems = rest[len(after):]
        x, y, c = _mesh_position()
        me = _block_id(x, y, c)
        out_ref[me] = v_ref[...]
        sends = []
        for k in range(1, N_DEV):
            px, py, pc = x ^ ((k >> 2) & 1), y ^ ((k >> 1) & 1), c ^ (k & 1)
            send = pltpu.make_async_remote_copy(
                src_ref=v_ref, dst_ref=out_ref.at[me], send_sem=send_sems.at[k - 1], recv_sem=recv_sems.at[k - 1],
                device_id=(px, py, pc), device_id_type=MESH)
            send.start()
            sends.append((send, _block_id(px, py, pc)))
        for k, (send, peer) in enumerate(sends):
            pltpu.make_async_remote_copy(
                src_ref=v_ref, dst_ref=out_ref.at[peer], send_sem=send_sems.at[k], recv_sem=recv_sems.at[k],
                device_id=(x, y, c), device_id_type=MESH).wait_recv()
        for send, _ in sends:
            send.wait_send()

    return pl.pallas_call(
        body, name=name, in_specs=[vmem] + [HBM] * len(after), out_specs=vmem,
        out_shape=jax.ShapeDtypeStruct((N_DEV,) + v.shape, v.dtype),
        scratch_shapes=[pltpu.SemaphoreType.DMA((N_DEV - 1,)), pltpu.SemaphoreType.DMA((N_DEV - 1,))],
        compiler_params=_params(),
    )(v, *after)


def _forward_layer(layer, x, vec, cps, wpool, win, wout, target=None):
    t_len = x.shape[0]
    n_tiles = t_len // ROW_TILE
    row = lambda cols: pl.BlockSpec((ROW_TILE, cols), lambda i: (i, 0))
    widths = (D_MODEL, 2 * CONV_W, 3 * CONV_W, 4 * POOL_W)
    head = target is not None

    def body(x_ref, vec_ref, cps_ref, wpool_ref, win_ref, wout_ref, *rest):
        target_ref = rest[0] if head else None
        xo_ref, y_ref, h_ref, ycat_ref, uc_ref, fa_ref, fp_ref = rest[head:head + 7]
        loss_ref = rest[head + 7] if head else None
        zc_ref, pc_ref = rest[-2:]
        i = pl.program_id(0)

        @pl.when(i == 0)
        def _():
            zc_ref[...] = jnp.zeros_like(zc_ref)
            pc_ref[...] = jnp.zeros_like(pc_ref)
            if head:
                loss_ref[...] = jnp.zeros_like(loss_ref)

        x_t = x_ref[...]
        shift, scale, gate = vec_ref[0:1, :], vec_ref[1:2, :], vec_ref[2:3, :]
        g_pre, g_post = vec_ref[3:4, :], vec_ref[4:5, :]
        w0, w1, w2, ps = cps_ref[0:1, :], cps_ref[1:2, :], cps_ref[2:3, :], cps_ref[3:4, :]
        rx = lax.rsqrt(jnp.mean(x_t * x_t, axis=-1, keepdims=True) + NORM_EPS)
        h = (x_t * rx) * g_pre * (1.0 + scale) + shift
        h_ref[...] = h.astype(BF16)
        proj = _dot(h.astype(BF16), win_ref[...])
        u_a, b_a, c_a, g_a, u_p, g_p = _split_proj(proj)
        uc_ref[...] = jnp.concatenate([u_a, c_a], axis=1).astype(BF16)

        z = c_a * u_a
        zcat = jnp.concatenate([zc_ref[...], z], axis=0)
        zc_ref[...] = z[ROW_TILE - CONV_HALO:]
        conv = (w0 * _rows_from_before(zcat, 2)[CONV_HALO:] + w1 * _rows_from_before(zcat, 1)[CONV_HALO:] + w2 * z)
        sig_a = _sigmoid(g_a)
        silu_a = g_a * sig_a
        b_conv = b_a * conv
        y_a = b_conv * silu_a
        fa_ref[...] = jnp.concatenate(
            [silu_a * conv, silu_a * b_a, b_conv * (sig_a + silu_a * (1.0 - sig_a))], axis=1).astype(BF16)

        pcat = jnp.concatenate([pc_ref[...], u_p], axis=0)
        pc_ref[...] = u_p[ROW_TILE - POOL_HALO:]
        counts = _window_counts(i * ROW_TILE, ROW_TILE)
        pooled, mixed = [], []
        for g, w in enumerate(POOL_WINDOWS):
            cols = slice(g * GROUP_D, (g + 1) * GROUP_D)
            s = pcat[:, cols]
            step = 1
            while step < w:
                s = s + _rows_from_before(s, step)
                step *= 2
            pooled_g = (s[POOL_HALO:] * (1.0 / jnp.minimum(counts, float(w))) - u_p[:, cols]).astype(BF16)
            pooled.append(pooled_g)
            mixed.append(_dot(pooled_g, wpool_ref[g]))
        mixed = jnp.concatenate(mixed, axis=1)
        sig_p = _sigmoid(g_p)
        silu_p = g_p * sig_p
        mixed_ps = mixed * ps
        y_p = mixed_ps * silu_p
        fp_ref[...] = jnp.concatenate(
            [(ps * silu_p).astype(BF16), (mixed_ps * (sig_p + silu_p * (1.0 - sig_p))).astype(BF16),
             (silu_p * mixed).astype(BF16)] + pooled, axis=1)

        ycat = jnp.concatenate([y_a, y_p], axis=1)
        ycat_ref[...] = ycat.astype(BF16)
        y_b = _dot(ycat.astype(BF16), wout_ref[...]).astype(BF16)
        y_ref[...] = y_b
        y_t = y_b.astype(F32)
        ry = lax.rsqrt(jnp.mean(y_t * y_t, axis=-1, keepdims=True) + NORM_EPS)
        x_next = x_t + gate * (y_t * ry * g_post)
        if head:
            err = x_next - target_ref[...]
            xo_ref[...] = err * (1.0 / D_MODEL)
            loss_ref[...] += jnp.sum(err * err) * (0.5 / D_MODEL)
        else:
            xo_ref[...] = x_next

    tile = (SUBLANES, LANES)
    return pl.pallas_call(
        body, name=f"forward_layer_{layer}", grid=(n_tiles,),
        in_specs=[row(D_MODEL), _layer_spec(vec.shape, layer), _layer_spec(cps.shape, layer),
                  _layer_spec(wpool.shape, layer), _whole_spec(win.shape), _whole_spec(wout.shape)]
        + [row(D_MODEL)] * head,
        out_specs=[row(D_MODEL), row(D_MODEL), row(D_MODEL), row(D_MODEL)] + [row(w) for w in widths[1:]]
        + [_whole_spec(tile)] * head,
        out_shape=[jax.ShapeDtypeStruct((t_len, D_MODEL), F32), jax.ShapeDtypeStruct((t_len, D_MODEL), BF16),
                   jax.ShapeDtypeStruct((t_len, D_MODEL), BF16), jax.ShapeDtypeStruct((t_len, D_MODEL), BF16)]
        + [jax.ShapeDtypeStruct((t_len, w), BF16) for w in widths[1:]] + [jax.ShapeDtypeStruct(tile, F32)] * head,
        scratch_shapes=[pltpu.VMEM((CONV_HALO, CONV_W), F32), pltpu.VMEM((POOL_HALO, POOL_W), F32)],
        compiler_params=_params(dimension_semantics=("arbitrary",)),
    )(x, vec, cps, wpool, win, wout, *([target] * head))


def _backward_layer(layer, dxo, y, x, uc, fa, fp, vec, cps, wpool, wout, win):
    t_len = dxo.shape[0]
    n_tiles = t_len // BWD_TILE
    halo_per_tile = BWD_TILE // POOL_HALO
    rev = lambda cols: pl.BlockSpec((BWD_TILE, cols), lambda i: (n_tiles - 1 - i, 0))
    halo_spec = pl.BlockSpec(
        (POOL_HALO, 2 * CONV_W), lambda i: (jnp.maximum((n_tiles - 1 - i) * halo_per_tile - 1, 0), 0))
    gwpool_shape = (len(POOL_WINDOWS), GROUP_D, GROUP_D)

    def body(dxo_ref, y_ref, x_ref, uc_ref, uch_ref, fa_ref, fp_ref, vec_ref, cps_ref, wpool_ref, wout_ref, win_ref,
             dx_ref, dproj_ref, dy_ref, gwpool_ref, dcps_ref, dvec_ref, gwpool_acc, dcc_ref, qc_ref):
        i = pl.program_id(0)
        tile = n_tiles - 1 - i

        @pl.when(i == 0)
        def _():
            gwpool_acc[...] = jnp.zeros_like(gwpool_acc)
            dcps_ref[...] = jnp.zeros_like(dcps_ref)
            dvec_ref[...] = jnp.zeros_like(dvec_ref)
            dcc_ref[...] = jnp.zeros_like(dcc_ref)
            qc_ref[...] = jnp.zeros_like(qc_ref)

        shift, scale, gate = vec_ref[0:1, :], vec_ref[1:2, :], vec_ref[2:3, :]
        g_pre, g_post = vec_ref[3:4, :], vec_ref[4:5, :]
        w0, w1, w2 = cps_ref[0:1, :], cps_ref[1:2, :], cps_ref[2:3, :]

        dxo_t = dxo_ref[...]
        y_t = y_ref[...].astype(F32)
        ry = lax.rsqrt(jnp.mean(y_t * y_t, axis=-1, keepdims=True) + NORM_EPS)
        yh = y_t * ry
        dvec_ref[2:3, :] += jnp.sum(dxo_t * yh, axis=0, keepdims=True)
        dyh = dxo_t * (gate * g_post)
        dy_b = (ry * (dyh - yh * jnp.mean(dyh * yh, axis=-1, keepdims=True))).astype(BF16)
        dy_ref[...] = dy_b
        dycat = _dot_nt(dy_b, wout_ref[...])
        dy_a, dy_p = dycat[:, :CONV_W], dycat[:, CONV_W:]

        fa_t = fa_ref[...].astype(F32)
        db_a = dy_a * fa_t[:, :CONV_W]
        dconv = dy_a * fa_t[:, CONV_W:2 * CONV_W]
        dg_a = dy_a * fa_t[:, 2 * CONV_W:]
        uc_t = uc_ref[...].astype(F32)
        u_a, c_a = uc_t[:, :CONV_W], uc_t[:, CONV_W:]
        halo = jnp.where(tile > 0, uch_ref[...].astype(F32), 0.0)[POOL_HALO - CONV_HALO:]
        z = c_a * u_a
        zcat = jnp.concatenate([halo[:, CONV_W:] * halo[:, :CONV_W], z], axis=0)
        z1 = _rows_from_before(zcat, 1)[CONV_HALO:]
        z2 = _rows_from_before(zcat, 2)[CONV_HALO:]
        dccat = jnp.concatenate([dconv, dcc_ref[...]], axis=0)
        dc1 = _rows_from_after(dccat, 1)[:BWD_TILE]
        dc2 = _rows_from_after(dccat, 2)[:BWD_TILE]
        dz = w2 * dconv + w1 * dc1 + w0 * dc2
        dcc_ref[...] = dconv[:CONV_HALO]
        dcps_ref[0:1, :] += jnp.sum(dconv * z2, axis=0, keepdims=True)
        dcps_ref[1:2, :] += jnp.sum(dconv * z1, axis=0, keepdims=True)
        dcps_ref[2:3, :] += jnp.sum(dconv * z, axis=0, keepdims=True)
        du_a = dz * c_a
        dc_a = dz * u_a

        dmixed = (dy_p * fp_ref[:, :POOL_W].astype(F32)).astype(BF16)
        dg_p = dy_p * fp_ref[:, POOL_W:2 * POOL_W].astype(F32)
        dcps_ref[3:4, :] += jnp.sum(dy_p * fp_ref[:, 2 * POOL_W:3 * POOL_W].astype(F32), axis=0, keepdims=True)
        counts = _window_counts(tile * BWD_TILE, BWD_TILE)
        du_p, q_head = [], []
        for g, w in enumerate(POOL_WINDOWS):
            cols = slice(g * GROUP_D, (g + 1) * GROUP_D)
            dm_g = dmixed[:, cols]
            dpooled_g = _dot_nt(dm_g, wpool_ref[g])
            gwpool_acc[g] += _dot_tn(fp_ref[:, 3 * POOL_W + g * GROUP_D:3 * POOL_W + (g + 1) * GROUP_D], dm_g)
            q_g = dpooled_g * (1.0 / jnp.minimum(counts, float(w)))
            q_head.append(q_g[:POOL_HALO])
            s = jnp.concatenate([q_g, qc_ref[:, cols]], axis=0)
            step = 1
            while step < w:
                s = s + _rows_from_after(s, step)
                step *= 2
            du_p.append(s[:BWD_TILE] - dpooled_g)
        qc_ref[...] = jnp.concatenate(q_head, axis=1)
        dproj_b = jnp.concatenate([du_a, db_a, dc_a, dg_a] + du_p + [dg_p], axis=1).astype(BF16)
        dproj_ref[...] = dproj_b

        x_t = x_ref[...]
        rx = lax.rsqrt(jnp.mean(x_t * x_t, axis=-1, keepdims=True) + NORM_EPS)
        xn = x_t * rx
        mod_scale = 1.0 + scale
        dh = _dot_nt(dproj_b, win_ref[...])
        dvec_ref[0:1, :] += jnp.sum(dh, axis=0, keepdims=True)
        dvec_ref[1:2, :] += jnp.sum(dh * xn, axis=0, keepdims=True)
        dxn = dh * (g_pre * mod_scale)
        dx_ref[...] = dxo_t + rx * (dxn - xn * jnp.mean(dxn * xn, axis=-1, keepdims=True))

        @pl.when(i == n_tiles - 1)
        def _():
            gwpool_ref[...] = gwpool_acc[...].astype(BF16)
            sum_dh_xn, sum_dxo_yh = dvec_ref[1:2, :], dvec_ref[2:3, :]
            dvec_ref[1:2, :] = sum_dh_xn * g_pre
            dvec_ref[3:4, :] = sum_dh_xn * mod_scale
            dvec_ref[2:3, :] = sum_dxo_yh * g_post
            dvec_ref[4:5, :] = sum_dxo_yh * gate

    return pl.pallas_call(
        body, name=f"backward_layer_{layer}", grid=(n_tiles,),
        in_specs=[rev(D_MODEL), rev(D_MODEL), rev(D_MODEL), rev(2 * CONV_W), halo_spec, rev(3 * CONV_W),
                  rev(4 * POOL_W), _layer_spec(vec.shape, layer), _layer_spec(cps.shape, layer),
                  _layer_spec(wpool.shape, layer), _whole_spec(wout.shape), _whole_spec(win.shape)],
        out_specs=[rev(D_MODEL), rev(IN_COLS), rev(D_MODEL), _whole_spec(gwpool_shape),
                   _whole_spec((SUBLANES, CONV_W)), _whole_spec((SUBLANES, D_MODEL))],
        out_shape=[jax.ShapeDtypeStruct((t_len, D_MODEL), F32), jax.ShapeDtypeStruct((t_len, IN_COLS), BF16),
                   jax.ShapeDtypeStruct((t_len, D_MODEL), BF16), jax.ShapeDtypeStruct(gwpool_shape, BF16),
                   jax.ShapeDtypeStruct((SUBLANES, CONV_W), F32), jax.ShapeDtypeStruct((SUBLANES, D_MODEL), F32)],
        scratch_shapes=[pltpu.VMEM(gwpool_shape, F32), pltpu.VMEM((CONV_HALO, CONV_W), F32),
                        pltpu.VMEM((POOL_HALO, POOL_W), F32)],
        compiler_params=_params(dimension_semantics=("arbitrary",)),
    )(dxo, y, x, uc, uc, fa, fp, vec, cps, wpool, wout, win)


def _weight_grads(layer, h, dproj, ycat, dy, exchange, after=()):
    t_len = dy.shape[0]
    n_in, n_out = IN_COLS // GWIN_COLS, D_MODEL // GWOUT_COLS
    args = [h, dproj, ycat, dy, *after]
    in_specs = [_whole_spec(h.shape),
                pl.BlockSpec((t_len, GWIN_COLS), lambda s: (0, jnp.minimum(s, n_in - 1))),
                _whole_spec(ycat.shape),
                pl.BlockSpec((t_len, GWOUT_COLS), lambda s: (0, jnp.maximum(s - n_in, 0)))] + [HBM] * len(after)
    out_shape = [jax.ShapeDtypeStruct((D_MODEL, IN_COLS), BF16), jax.ShapeDtypeStruct((D_MODEL, D_MODEL), BF16)]
    out_specs = [pl.BlockSpec((D_MODEL, GWIN_COLS), lambda s: (0, jnp.minimum(s, n_in - 1))),
                 pl.BlockSpec((D_MODEL, GWOUT_COLS), lambda s: (0, jnp.maximum(s - n_in, 0)))]
    scratch = []
    aliases, split = _host(exchange, args, in_specs, out_shape, out_specs, scratch)

    def body(*refs):
        (h_ref, dproj_ref, ycat_ref, dy_ref, *_, gwin_ref, gwout_ref), hosted = split(refs)
        s = pl.program_id(0)
        if hosted is not None:
            pl.when(s == 0)(hosted[0])

        @pl.when(s < n_in)
        def _():
            gwin_ref[...] = _dot_tn(h_ref[...], dproj_ref[...]).astype(BF16)

        @pl.when(s >= n_in)
        def _():
            gwout_ref[...] = _dot_tn(ycat_ref[...], dy_ref[...]).astype(BF16)

        if hosted is not None:
            pl.when(s == n_in + n_out - 1)(hosted[1])

    return pl.pallas_call(
        body, name=f"weight_grads_{layer}", grid=(n_in + n_out,), in_specs=in_specs, out_specs=out_specs,
        out_shape=out_shape, scratch_shapes=scratch, input_output_aliases=aliases,
        compiler_params=_params(dimension_semantics=("arbitrary",)),
    )(*args)


def _add_sibling_blocks(name, layer, grads, received, core, partials, kinds):
    n_arr = len(grads)

    def body(core_ref, *refs):
        mine, theirs, outs = refs[:n_arr], refs[n_arr:2 * n_arr], refs[-n_arr:]
        for a in range(n_arr):
            outs[a][...] = (mine[a][...].astype(F32) + theirs[a][...].astype(F32)).astype(BF16)

    own_of_kind = [
        pl.BlockSpec((D_MODEL, W_IN_SHARD), lambda q, core_ref: (0, 2 * q + core_ref[0])),
        pl.BlockSpec((W_OUT_SHARD, D_MODEL), lambda q, core_ref: (2 * q + core_ref[0], 0)),
        pl.BlockSpec((POOL_SHARD, GROUP_D), lambda q, core_ref: (2 * q + core_ref[0], 0)),
    ]
    shapes = [_BLOCK_SHAPES[k] for k in kinds]
    recv_specs = [pl.BlockSpec((None,) + s, lambda q, core_ref: (q, 0, 0)) for s in shapes]
    out_specs = [pl.BlockSpec((None, None) + s, lambda q, core_ref: (q, layer, 0, 0)) for s in shapes]
    args = [core, *grads, *received]
    in_specs = [own_of_kind[k] for k in kinds] + recv_specs
    aliases = {}
    if partials is not None:
        aliases = {len(args) + a: a for a in range(n_arr)}
        args += list(partials)
        in_specs += [HBM] * n_arr
    return pl.pallas_call(
        body, name=name,
        grid_spec=pltpu.PrefetchScalarGridSpec(
            num_scalar_prefetch=1, grid=(N_CHIP,), in_specs=in_specs, out_specs=out_specs),
        out_shape=[jax.ShapeDtypeStruct((N_CHIP, DEPTH) + s, BF16) for s in shapes],
        input_output_aliases=aliases,
        compiler_params=_params(dimension_semantics=("arbitrary",)),
    )(*args)


def _modulation_columns(c_all, w_ada):
    def body(c_ref, w_ref, cact_ref, out_ref):
        c_t = c_ref[...]
        c_act = c_t * _sigmoid(c_t)
        cact_ref[...] = c_act
        out_ref[...] = jnp.dot(c_act, w_ref[...], preferred_element_type=F32, precision=lax.Precision.HIGHEST)

    return pl.pallas_call(
        body, name="modulation_columns", grid=(DEPTH,),
        in_specs=[pl.BlockSpec((N_DEV, D_MODEL), lambda l: (0, 0)),
                  pl.BlockSpec((None, D_MODEL, W_IN_SHARD), lambda l: (l, 0, 0))],
        out_specs=[pl.BlockSpec((N_DEV, D_MODEL), lambda l: (0, 0)),
                   pl.BlockSpec((N_DEV, W_IN_SHARD), lambda l: (0, l))],
        out_shape=[jax.ShapeDtypeStruct((N_DEV, D_MODEL), F32),
                   jax.ShapeDtypeStruct((N_DEV, DEPTH * W_IN_SHARD), F32)],
        compiler_params=_params(dimension_semantics=("arbitrary",)),
    )(c_all, w_ada)


def _adamw(w, g, m, v):
    m_new = ADAM_B1 * m + (1.0 - ADAM_B1) * g
    v_new = ADAM_B2 * v + (1.0 - ADAM_B2) * (g * g)
    m_hat = m_new / (1.0 - ADAM_B1 ** ADAM_STEP)
    v_hat = v_new / (1.0 - ADAM_B2 ** ADAM_STEP)
    delta = -ADAM_LR * (m_hat / (jnp.sqrt(v_hat) + ADAM_EPS) + ADAM_WD * w)
    return delta, m_new, v_new


def _adamw_w_ada(w, m, v, c_act_t, dmod_cols):
    def body(w_ref, m_ref, v_ref, ct_ref, dm_ref, g_ref, d_ref, mo_ref, vo_ref):
        g = ct_ref[:, 0:1] * dm_ref[0:1, :]
        for b in range(1, N_DEV):
            g = g + ct_ref[:, b:b + 1] * dm_ref[b:b + 1, :]
        g_ref[...] = g
        d_ref[...], mo_ref[...], vo_ref[...] = _adamw(w_ref[...], g, m_ref[...], v_ref[...])

    big = pl.BlockSpec((None, D_MODEL, W_IN_SHARD), lambda l: (l, 0, 0))
    return pl.pallas_call(
        body, name="adamw_w_ada", grid=(DEPTH,),
        in_specs=[big, big, big, pl.BlockSpec((D_MODEL, N_DEV), lambda l: (0, 0)),
                  pl.BlockSpec((None, N_DEV, W_IN_SHARD), lambda l: (l, 0, 0))],
        out_specs=[big] * 4, out_shape=[jax.ShapeDtypeStruct(w.shape, F32)] * 4,
        compiler_params=_params(dimension_semantics=("arbitrary",)),
    )(w, m, v, c_act_t, dmod_cols)


def _sum_chip_partials(own_ref, recv_ref):
    g = own_ref[...].astype(F32)
    for j in range(N_OTHER_CHIPS):
        g = g + recv_ref[j].astype(F32)
    return g


def _partial_specs(row_tile, cols, first_layer=0):
    own = pl.BlockSpec((None, None, row_tile, cols), lambda l, r, chip_ref: (chip_ref[0], first_layer + l, r, 0))
    recv = pl.BlockSpec((N_OTHER_CHIPS, None, row_tile, cols), lambda l, r, chip_ref: (0, first_layer + l, r, 0))
    return own, recv


def _adamw_reduced(name, w, m, v, partial, received, chip, row_tile, layers, continued):
    depth, rows, cols = w.shape
    first, stop = layers

    def body(chip_ref, w_ref, m_ref, v_ref, own_ref, recv_ref, *rest):
        g_ref, d_ref, mo_ref, vo_ref = rest[-4:]
        g = _sum_chip_partials(own_ref, recv_ref)
        g_ref[...] = g
        d_ref[...], mo_ref[...], vo_ref[...] = _adamw(w_ref[...], g, m_ref[...], v_ref[...])

    blk = pl.BlockSpec((None, row_tile, cols), lambda l, r, chip_ref: (first + l, r, 0))
    args = [chip, w, m, v, partial, received]
    in_specs = [blk, blk, blk, *_partial_specs(row_tile, cols, first)]
    aliases = {}
    if continued is not None:
        aliases = {len(args) + k: k for k in range(4)}
        args += list(continued)
        in_specs += [HBM] * 4
    return pl.pallas_call(
        body, name=name,
        grid_spec=pltpu.PrefetchScalarGridSpec(
            num_scalar_prefetch=1, grid=(stop - first, rows // row_tile), in_specs=in_specs, out_specs=[blk] * 4),
        out_shape=[jax.ShapeDtypeStruct(w.shape, F32)] * 4, input_output_aliases=aliases,
        compiler_params=_params(dimension_semantics=("arbitrary", "arbitrary")),
    )(*args)


def _reduce_w_pool(partial, received, chip):
    def body(chip_ref, own_ref, recv_ref, g_ref):
        g_ref[...] = _sum_chip_partials(own_ref, recv_ref)

    return pl.pallas_call(
        body, name="reduce_w_pool",
        grid_spec=pltpu.PrefetchScalarGridSpec(
            num_scalar_prefetch=1, grid=(DEPTH, 1), in_specs=list(_partial_specs(POOL_SHARD, GROUP_D)),
            out_specs=pl.BlockSpec((POOL_SHARD, GROUP_D), lambda l, r, chip_ref: (l, 0))),
        out_shape=jax.ShapeDtypeStruct((DEPTH * POOL_SHARD, GROUP_D), F32),
        compiler_params=_params(dimension_semantics=("arbitrary", "arbitrary")),
    )(chip, partial, received)


def _adamw_small(params):
    n = len(params)

    def body(*refs):
        ins, outs = refs[:4 * n], refs[4 * n:]
        for p in range(n):
            w_ref, g_ref, m_ref, v_ref = ins[4 * p:4 * p + 4]
            d_ref, mo_ref, vo_ref = outs[3 * p:3 * p + 3]
            d_ref[...], mo_ref[...], vo_ref[...] = _adamw(w_ref[...], g_ref[...], m_ref[...], v_ref[...])

    vmem = pl.BlockSpec(memory_space=pltpu.VMEM)
    flat = [a for group in params for a in group]
    out_shape = [jax.ShapeDtypeStruct(group[0].shape, F32) for group in params for _ in range(3)]
    outs = pl.pallas_call(
        body, name="adamw_small", in_specs=[vmem] * len(flat), out_specs=[vmem] * len(out_shape),
        out_shape=out_shape, compiler_params=_params(),
    )(*flat)
    return [tuple(outs[3 * p:3 * p + 3]) for p in range(n)]


def _sum_sources(slabs):
    def body(s_ref, o_ref):
        acc = s_ref[0]
        for b in range(1, N_DEV):
            acc = acc + s_ref[b]
        o_ref[...] = acc

    vmem = pl.BlockSpec(memory_space=pltpu.VMEM)
    return pl.pallas_call(
        body, name="sum_small_grads", in_specs=[vmem], out_specs=vmem,
        out_shape=jax.ShapeDtypeStruct(slabs.shape[1:], F32), compiler_params=_params(),
    )(slabs)


def _to_bf16(a, name, layers=None):
    first, stop = layers or (0, a.shape[0])

    def body(a_ref, o_ref):
        o_ref[...] = a_ref[...].astype(BF16)

    block = (None,) + a.shape[1:]
    return pl.pallas_call(
        body, name=name, grid=(stop - first,), in_specs=[pl.BlockSpec(block, lambda l: (first + l, 0, 0))],
        out_specs=pl.BlockSpec(block, lambda l: (l, 0, 0)),
        out_shape=jax.ShapeDtypeStruct((stop - first,) + a.shape[1:], BF16),
        compiler_params=_params(dimension_semantics=("arbitrary",)),
    )(a)


def kernel(x, c, w_ada, b_ada, g_pre, w_in, w_conv, w_pool, pool_scale, w_out, g_post, loss_target, m_w_ada, m_b_ada, m_g_pre, m_w_in, m_w_conv, m_w_pool, m_pool_scale, m_w_out, m_g_post, v_w_ada, v_b_ada, v_g_pre, v_w_in, v_w_conv, v_w_pool, v_pool_scale, v_w_out, v_g_post):
    mx, my, mc = _mesh_position()
    me = _block_id(mx, my, mc)
    chip = (2 * mx + my).astype(jnp.int32).reshape(1)
    core = mc.astype(jnp.int32).reshape(1)
    x0 = x[0]
    target = loss_target[0]
    conv_shard = w_conv.shape[-1]

    own_small = jnp.concatenate([c, w_conv.reshape(1, DEPTH * 3 * conv_shard)], axis=1)
    first_shards = [_to_bf16(w_in, "cast_w_in_0", (0, 1)), _to_bf16(w_out, "cast_w_out_0", (0, 1))]
    all_small = _all_gather_small(own_small, "all_gather_c_w_conv", first_shards)[:, 0, :]
    gathers = [_gather_weights_start(0, *first_shards, [all_small], relayed=True)]
    c_all = all_small[:, :D_MODEL]
    w_conv_full = all_small[:, D_MODEL:].reshape(N_DEV, DEPTH, 3, conv_shard).transpose(1, 2, 0, 3).reshape(
        DEPTH, 3, CONV_W)
    cps = jnp.concatenate([w_conv_full, pool_scale[:, None], jnp.zeros((DEPTH, 4, CONV_W), F32)], axis=1)

    c_act, pieces = _modulation_columns(c_all, w_ada)
    upper = (1, DEPTH)
    upper_shards = [_to_bf16(w_in, "cast_w_in_1", upper), _to_bf16(w_out, "cast_w_out_1", upper)]
    wpool_b = _to_bf16(w_pool.reshape(DEPTH, POOL_ROWS, GROUP_D), "cast_w_pool").reshape(w_pool.shape)
    first_sems_0, shards_0, (zones_0,) = gathers[0]
    mod_all = _all_gather_small(
        pieces, "all_gather_modulation", [shards_0[0], *upper_shards, wpool_b])
    neighbour_sems, zones_0 = _gather_weights_pass_on(
        "all_gather_weights_relay_0", NEIGHBOUR_CHIPS, first_sems_0[1], partial(_first_arrival, 0), zones_0,
        [mod_all], relay=True)
    mod_mine = lax.dynamic_index_in_dim(mod_all, me, axis=1, keepdims=False)
    mod = mod_mine.reshape(N_DEV, DEPTH, W_IN_SHARD).transpose(1, 0, 2).reshape(DEPTH, 3 * D_MODEL) + b_ada
    zeros_d = jnp.zeros((DEPTH, 3, D_MODEL), F32)
    vec = jnp.concatenate([mod.reshape(DEPTH, 3, D_MODEL), g_pre[:, None], g_post[:, None], zeros_d], axis=1)

    gathers.append(_gather_weights_start(1, *upper_shards, [zones_0[0]]))
    gathers = [(first_sems, shards, landing[k], k) for first_sems, shards, landing in gathers
               for k in range(len(landing))]

    xs, kept, wins, wouts = [x0], [], [], []
    for l in range(DEPTH):
        first_sems, shards, zones, index = gathers[l]
        if l == 0:
            diagonal_sems, zones = _gather_weights_pass_on(
                "all_gather_weights_pass_on_0", DIAGONAL_CHIP, neighbour_sems[3], lambda a, j: a, zones_0,
                [vec, cps, gathers[-1][1][0]])
            passed = [(neighbour_sems[:2], NEIGHBOUR_CHIPS), (diagonal_sems, DIAGONAL_CHIP)]
            win, wout = _gather_weights_finish(l, index, first_sems, passed, shards, zones, neighbour_sems[2])
        else:
            passed_sems, zones = _gather_weights_pass_on(
                f"all_gather_weights_pass_on_{l}", ALL_OTHER_CHIPS, first_sems[1], partial(_first_arrival, index),
                zones, [xs[-1]])
            win, wout = _gather_weights_finish(l, index, first_sems, [(passed_sems, ALL_OTHER_CHIPS)], shards, zones)
        x_next, *for_backward = _forward_layer(
            l, xs[-1], vec, cps, wpool_b, win, wout, target if l == DEPTH - 1 else None)
        xs.append(x_next)
        kept.append(for_backward[:6])
        wins.append(win)
        wouts.append(wout)
    dx, loss_tile = xs[DEPTH], for_backward[6]

    slab_rows = [None] * DEPTH
    partials = received = None
    in_flight = []

    def scatter(layer, grads, from_sibling, after):
        nonlocal partials, received
        partials = _add_sibling_blocks(
            f"grad_add_sibling_{layer}", layer, grads, from_sibling, core, partials, ALL_KINDS)
        chips = _chips_exchange(layer, partials, received, ALL_KINDS)
        sems, partials, received, token = _start_exchange(chips, f"grad_chips_start_{layer}", after)
        in_flight.append((chips, sems, layer))
        return token

    grads_above = None
    for l in reversed(range(DEPTH)):
        y, h, ycat, uc, fa, fp = kept[l]
        dx, dproj, dy, gwpool, dcps, dvec = _backward_layer(
            l, dx, y, xs[l], uc, fa, fp, vec, cps, wpool_b, wouts[l], wins[l])
        slab_rows[l] = jnp.concatenate(
            [dvec[0], dvec[1], dvec[2], dvec[3], dvec[4], dcps[3], dcps[0], dcps[1], dcps[2],
             loss_tile[0] if l == 0 else jnp.zeros((LANES,), F32)])
        after = []
        if l == 0:
            gather_small = _all_gather_exchange(jnp.stack(slab_rows))
            sems_s, slab, slabs, token = _start_exchange(gather_small, "all_gather_small_grads_start")
            after = [token]
        hosted = _sibling_exchange(grads_above, ALL_KINDS) if grads_above is not None else None
        gwin, gwout, *from_sibling = _weight_grads(l, h, dproj, ycat, dy, hosted, after)
        if l == 0:
            _, (slabs,) = _finish_exchange(
                gather_small, "all_gather_small_grads_finish", sems_s, slab, slabs, [gwin])
        if grads_above is not None:
            scatter(l + 1, grads_above, from_sibling, [])
        grads_above = [gwin, gwout, gwpool.reshape(POOL_ROWS, GROUP_D)]
        if l <= 1:
            from_sibling = _run_exchange(_sibling_exchange(grads_above, ALL_KINDS), f"grad_exchange_sibling_{l}")
            token = scatter(l, grads_above, from_sibling, [slabs] if l == 0 else [])
            grads_above = None
    grad_x = dx[None]
    chips_0, sems_0, _ = in_flight.pop()

    total = _sum_sources(slabs)
    loss = total[0, SLAB_COLS]
    o = 3 * D_MODEL
    g_b_ada = total[:, :o]
    g_g_pre = total[:, o:o + D_MODEL]
    g_g_post = total[:, o + D_MODEL:o + 2 * D_MODEL]
    g_pool_scale = total[:, o + 2 * D_MODEL:o + 2 * D_MODEL + POOL_W]
    g_conv_full = total[:, o + 2 * D_MODEL + POOL_W:SLAB_COLS].reshape(DEPTH, 3, CONV_W)
    g_w_conv = lax.dynamic_slice_in_dim(g_conv_full, me * conv_shard, conv_shard, axis=2)

    after = [token]
    for chips, sems, l in in_flight:
        partials, received = _finish_exchange(chips, f"grad_chips_finish_{l}", sems, partials, received, after)
        after = []
    upper = (1, DEPTH)
    w_in_upper = _adamw_reduced(
        "adamw_w_in_upper", w_in, m_w_in, v_w_in, partials[0], received[0], chip, ROW_TILE, upper, None)
    w_out_upper = _adamw_reduced(
        "adamw_w_out_upper", w_out, m_w_out, v_w_out, partials[1], received[1], chip, W_OUT_SHARD, upper, None)
    dmod_all = slabs[:, :, :o].reshape(N_DEV, DEPTH, N_DEV, W_IN_SHARD)
    dmod_cols = lax.dynamic_index_in_dim(dmod_all, me, axis=2, keepdims=False).transpose(1, 0, 2) + token[0, 0]
    g_w_ada, d_w_ada, nm_w_ada, nv_w_ada = _adamw_w_ada(w_ada, m_w_ada, v_w_ada, c_act.T, dmod_cols)

    partials, received = _finish_exchange(
        chips_0, "grad_chips_finish_0", sems_0, partials, received, [nv_w_ada, w_in_upper[3], w_out_upper[3]])
    gather_pool = _all_gather_exchange(_reduce_w_pool(partials[2], received[2], chip))
    sems_p, pool_rows, pool_landing, token_p = _start_exchange(gather_pool, "all_gather_grad_w_pool_start")
    g_w_in, d_w_in, nm_w_in, nv_w_in = _adamw_reduced(
        "adamw_w_in_0", w_in, m_w_in, v_w_in, partials[0], received[0], chip, ROW_TILE, (0, 1), w_in_upper)
    g_w_out, d_w_out, nm_w_out, nv_w_out = _adamw_reduced(
        "adamw_w_out_0", w_out, m_w_out, v_w_out, partials[1], received[1], chip, W_OUT_SHARD, (0, 1), w_out_upper)
    _, (g_pool_all,) = _finish_exchange(
        gather_pool, "all_gather_grad_w_pool_finish", sems_p, pool_rows, pool_landing, [nv_w_in, nv_w_out])
    g_w_pool = g_pool_all.reshape(N_DEV, DEPTH, POOL_SHARD, GROUP_D).transpose(1, 0, 2, 3).reshape(w_pool.shape)

    flat2 = lambda a: a.reshape(-1, a.shape[-1])
    small = _adamw_small([
        (b_ada, g_b_ada, m_b_ada, v_b_ada),
        (g_pre, g_g_pre, m_g_pre, v_g_pre),
        (flat2(w_conv), flat2(g_w_conv), flat2(m_w_conv), flat2(v_w_conv)),
        (flat2(w_pool), flat2(g_w_pool), flat2(m_w_pool), flat2(v_w_pool)),
        (pool_scale, g_pool_scale, m_pool_scale, v_pool_scale),
        (g_post, g_g_post, m_g_post, v_g_post),
    ])
    (d_b_ada, nm_b_ada, nv_b_ada), (d_g_pre, nm_g_pre, nv_g_pre), conv_upd, pool_upd, \
        (d_ps, nm_ps, nv_ps), (d_g_post, nm_g_post, nv_g_post) = small
    d_w_conv, nm_w_conv, nv_w_conv = (a.reshape(w_conv.shape) for a in conv_upd)
    d_w_pool, nm_w_pool, nv_w_pool = (a.reshape(w_pool.shape) for a in pool_upd)

    return (loss, grad_x,
            g_w_ada, g_b_ada, g_g_pre, g_w_in, g_w_conv, g_w_pool, g_pool_scale, g_w_out, g_g_post,
            d_w_ada, d_b_ada, d_g_pre, d_w_in, d_w_conv, d_w_pool, d_ps, d_w_out, d_g_post,
            nm_w_ada, nm_b_ada, nm_g_pre, nm_w_in, nm_w_conv, nm_w_pool, nm_ps, nm_w_out, nm_g_post,
            nv_w_ada, nv_b_ada, nv_g_pre, nv_w_in, nv_w_conv, nv_w_pool, nv_ps, nv_w_out, nv_g_post)
```

```python
from functools import partial

import jax
import jax.numpy as jnp
from jax import lax
from jax.experimental import pallas as pl
from jax.experimental.pallas import tpu as pltpu

F32 = jnp.float32
BF16 = jnp.bfloat16

D_MODEL = 1024
DEPTH = 4
CONV_W = 512
POOL_W = 512
POOL_WINDOWS = (2, 4, 8, 16)
GROUP_D = 128
IN_COLS = 4 * CONV_W + 2 * POOL_W
NORM_EPS = 1e-6

ADAM_LR = 0.001
ADAM_B1 = 0.9
ADAM_B2 = 0.999
ADAM_EPS = 1e-08
ADAM_WD = 0.01
ADAM_STEP = 10

N_DEV = 8
N_CHIP = 4
N_OTHER_CHIPS = N_CHIP - 1
MESH = pl.DeviceIdType.MESH
W_IN_SHARD = IN_COLS // N_DEV
W_OUT_SHARD = D_MODEL // N_DEV
POOL_ROWS = len(POOL_WINDOWS) * GROUP_D
POOL_SHARD = POOL_ROWS // N_DEV

SUBLANES = 8
LANES = 128
VMEM_LIMIT_BYTES = 56 * 1024 * 1024
ROW_TILE = 512
BWD_TILE = 256
GWIN_COLS = 768
GWOUT_COLS = 512
POOL_HALO = 16
CONV_HALO = SUBLANES

SLAB_COLS = 3 * D_MODEL + D_MODEL + D_MODEL + POOL_W + 3 * CONV_W

HBM = pl.BlockSpec(memory_space=pl.ANY)


def _params(**kw):
    return pltpu.CompilerParams(vmem_limit_bytes=VMEM_LIMIT_BYTES, **kw)


def _sigmoid(v):
    return 1.0 / (1.0 + jnp.exp(-v))


def _dot(a, b):
    return jnp.dot(a, b, preferred_element_type=F32)


def _dot_tn(a, b):
    return lax.dot_general(a, b, (((0,), (0,)), ((), ())), preferred_element_type=F32)


def _dot_nt(a, b):
    return lax.dot_general(a, b, (((1,), (1,)), ((), ())), preferred_element_type=F32)


def _rows_from_before(v, k):
    return pltpu.roll(v, k, 0)


def _rows_from_after(v, k):
    return pltpu.roll(v, v.shape[0] - k, 0)


def _window_counts(t0, rows):
    return (lax.broadcasted_iota(jnp.int32, (rows, 1), 0) + (t0 + 1)).astype(F32)


def _split_proj(p32):
    cw = CONV_W
    return (p32[:, 0 * cw:1 * cw], p32[:, 1 * cw:2 * cw], p32[:, 2 * cw:3 * cw], p32[:, 3 * cw:4 * cw],
            p32[:, 4 * cw:4 * cw + POOL_W], p32[:, 4 * cw + POOL_W:])


def _layer_spec(shape, layer):
    nd = len(shape)
    return pl.BlockSpec((None,) + tuple(shape[1:]), lambda i, _l=layer, _n=nd: (_l,) + (0,) * (_n - 1))


def _whole_spec(shape):
    return pl.BlockSpec(tuple(shape), lambda i, _n=len(shape): (0,) * _n, pipeline_mode=pl.Buffered(1))


def _mesh_position():
    return lax.axis_index("x"), lax.axis_index("y"), lax.axis_index("c")


def _block_id(x, y, c):
    return 4 * x + 2 * y + c


def _other_chips(x, y):
    return [(x ^ 1, y), (x, y ^ 1), (x ^ 1, y ^ 1)]


def _col_block(ref, blk):
    return ref.at[:, pl.ds(pl.multiple_of(blk * W_IN_SHARD, LANES), W_IN_SHARD)]


def _row_block(rows):
    def block(ref, blk):
        return ref.at[pl.ds(pl.multiple_of(blk * rows, rows), rows), :]
    return block


_BLOCK_OF = (_col_block, _row_block(W_OUT_SHARD), _row_block(POOL_SHARD))
_BLOCK_SHAPES = ((D_MODEL, W_IN_SHARD), (W_OUT_SHARD, D_MODEL), (POOL_SHARD, GROUP_D))


class _Exchange:
    def __init__(self, inputs, out_shapes, aliases, sem_shapes, make):
        self.inputs, self.out_shapes, self.aliases, self.sem_shapes, self.make = (
            list(inputs), list(out_shapes), dict(aliases), list(sem_shapes), make)


def _run_exchange(exchange, name):
    n_in, n_out = len(exchange.inputs), len(exchange.out_shapes)

    def body(*refs):
        start, finish = exchange.make(refs[:n_in], refs[n_in:n_in + n_out], refs[n_in + n_out:])
        start()
        finish()

    return pl.pallas_call(
        body, name=name, in_specs=[HBM] * n_in, out_specs=[HBM] * n_out, out_shape=exchange.out_shapes,
        scratch_shapes=exchange.sem_shapes, input_output_aliases=exchange.aliases, compiler_params=_params(),
    )(*exchange.inputs)


_SEM = pl.BlockSpec(memory_space=pltpu.SEMAPHORE)
_DATAFLOW = pltpu.SideEffectType.DATAFLOW_SIDE_EFFECTING


def _start_exchange(exchange, name, after=()):
    n_in, n_out, n_sem = len(exchange.inputs), len(exchange.out_shapes), len(exchange.sem_shapes)
    sources = [i for i in range(n_in) if i not in exchange.aliases]
    aliases = {i: n_sem + k for k, i in enumerate(sources)}
    aliases.update({i: n_sem + len(sources) + o for i, o in exchange.aliases.items()})

    def body(*refs):
        in_refs = refs[:n_in]
        outs = refs[n_in + len(after):]
        sems = outs[:n_sem]
        out_refs = outs[n_sem + len(sources):n_sem + len(sources) + n_out]
        exchange.make(in_refs, out_refs, sems)[0]()
        refs[-1][...] = jnp.zeros_like(refs[-1])

    outs = pl.pallas_call(
        body, name=name, in_specs=[HBM] * (n_in + len(after)),
        out_specs=[_SEM] * n_sem + [HBM] * (len(sources) + n_out) + [pl.BlockSpec(memory_space=pltpu.VMEM)],
        out_shape=(exchange.sem_shapes + [pltpu.HBM(exchange.inputs[i].shape, exchange.inputs[i].dtype) for i in sources]
                   + [pltpu.HBM(s.shape, s.dtype) for s in exchange.out_shapes]
                   + [jax.ShapeDtypeStruct((SUBLANES, LANES), F32)]),
        input_output_aliases=aliases, compiler_params=_params(has_side_effects=_DATAFLOW),
    )(*exchange.inputs, *after)
    return outs[:n_sem], outs[n_sem:n_sem + len(sources)], outs[n_sem + len(sources):-1], outs[-1]


def _finish_exchange(exchange, name, sems, sources, landing, after):
    n_src, n_out, n_sem = len(sources), len(landing), len(sems)
    n_in = len(exchange.inputs)
    source_at = [i for i in range(n_in) if i not in exchange.aliases]

    def body(*refs):
        src_refs, out_refs = refs[:n_src], refs[n_src:n_src + n_out]
        sem_refs = refs[n_src + n_out:n_src + n_out + n_sem]
        in_refs = [None] * n_in
        for k, i in enumerate(source_at):
            in_refs[i] = src_refs[k]
        for i, o in exchange.aliases.items():
            in_refs[i] = out_refs[o]
        exchange.make(in_refs, out_refs, sem_refs)[1]()

    arrays = list(sources) + list(landing)
    outs = pl.pallas_call(
        body, name=name, in_specs=[HBM] * len(arrays) + [_SEM] * n_sem + [HBM] * len(after),
        out_specs=[HBM] * len(arrays), out_shape=[pltpu.HBM(a.shape, a.dtype) for a in arrays],
        input_output_aliases={i: i for i in range(len(arrays))}, compiler_params=_params(has_side_effects=_DATAFLOW),
    )(*arrays, *sems, *after)
    return outs[:n_src], outs[n_src:]


N_GATHERED = 2
FIRST_COPIES = 1 + N_OTHER_CHIPS
_GATHERED_SHAPES = ((D_MODEL, IN_COLS), (D_MODEL, D_MODEL))
ALL_OTHER_CHIPS, NEIGHBOUR_CHIPS, DIAGONAL_CHIP = (0, 1, 2), (0, 1), (2,)


def _first_sem(layer, a, k):
    return (layer * N_GATHERED + a) * FIRST_COPIES + k


def _first_arrival(layer, a, j):
    return _first_sem(layer, a, 1 + j)


def _gather_copy(window_of, full_ref, blk, send_sem, recv_sem, to, src=None):
    window = window_of(full_ref, blk)
    return pltpu.make_async_remote_copy(
        src_ref=window if src is None else src, dst_ref=window, send_sem=send_sem, recv_sem=recv_sem,
        device_id=to, device_id_type=MESH)


def _first_copies(layer, shard_refs, full_refs, send_sems, recv_sems, local_sems, relayed):
    x, y, c = _mesh_position()
    me = _block_id(x, y, c)
    chips = [_other_chips(x, y)[j] for j in (NEIGHBOUR_CHIPS if relayed else ALL_OTHER_CHIPS)]
    own, remote = [], []
    for a in range(N_GATHERED):
        shard = shard_refs[a].at[layer]
        own.append(pltpu.make_async_copy(
            shard, _BLOCK_OF[a](full_refs[a], me), local_sems.at[layer * N_GATHERED + a]))
        targets = [(x, y, 1 - c)] + [(*chip, c) for chip in chips]
        remote += [_gather_copy(_BLOCK_OF[a], full_refs[a], me, send_sems.at[_first_sem(layer, a, k)],
                                recv_sems.at[_first_sem(layer, a, k)], to, src=shard)
                   for k, to in enumerate(targets)]
    return own, remote


def _gather_weights_start(first_layer, win_shards, wout_shards, after, relayed=False):
    n_layers = win_shards.shape[0]
    n_first = n_layers * N_GATHERED * FIRST_COPIES
    sem_shapes = [pltpu.SemaphoreType.DMA((n_first,)), pltpu.SemaphoreType.DMA((n_first,)),
                  pltpu.SemaphoreType.DMA((n_layers * N_GATHERED,))]
    shards = [win_shards, wout_shards]

    def body(win_sh, wout_sh, *rest):
        send_sems, recv_sems, local_sems, win_thru, wout_thru, *landing = rest[len(after):]
        for layer in range(n_layers):
            own, remote = _first_copies(layer, (win_sh, wout_sh), landing[N_GATHERED * layer:N_GATHERED * (layer + 1)],
                                        send_sems, recv_sems, local_sems, relayed)
            for cp in own + remote:
                cp.start()

    outs = pl.pallas_call(
        body, name=f"all_gather_weights_start_{first_layer}", in_specs=[HBM] * (2 + len(after)),
        out_specs=[_SEM] * 3 + [HBM] * (2 + n_layers * N_GATHERED),
        out_shape=(sem_shapes + [pltpu.HBM(s.shape, s.dtype) for s in shards]
                   + [pltpu.HBM(s, BF16) for _ in range(n_layers) for s in _GATHERED_SHAPES]),
        input_output_aliases={0: 3, 1: 4}, compiler_params=_params(has_side_effects=_DATAFLOW),
    )(*shards, *after)
    landing = outs[5:]
    return outs[:3], outs[3:5], [landing[N_GATHERED * l:N_GATHERED * (l + 1)] for l in range(n_layers)]


def _passed_on_copies(full_refs, send_sems, recv_sems, core_of_block, chips):
    x, y, c = _mesh_position()
    return [_gather_copy(_BLOCK_OF[a], full_refs[a], _block_id(*_other_chips(x, y)[j], core_of_block),
                         send_sems.at[a * N_OTHER_CHIPS + j], recv_sems.at[a * N_OTHER_CHIPS + j], (x, y, 1 - c))
            for a in range(N_GATHERED) for j in chips]


def _relayed_copies(full_refs, send_sems, recv_sems):
    x, y, c = _mesh_position()
    source, to = (x ^ (1 - c), y ^ c), (x ^ c, y ^ (1 - c))
    return [_gather_copy(_BLOCK_OF[a], full_refs[a], _block_id(*source, c), send_sems.at[a], recv_sems.at[a], (*to, c))
            for a in range(N_GATHERED)]


def _gather_weights_pass_on(name, chips, arrival_sems, arrival_sem_of, landing, after, relay=False):
    n = N_GATHERED * N_OTHER_CHIPS
    sem_shapes = [pltpu.SemaphoreType.DMA((n,))] * 2 + [pltpu.SemaphoreType.DMA((N_GATHERED,))] * (2 if relay else 0)

    def body(win_ref, wout_ref, arrivals, *rest):
        sems = rest[len(after):len(after) + len(sem_shapes)]
        x, y, c = _mesh_position()
        full_refs = (win_ref, wout_ref)
        passed = _passed_on_copies(full_refs, sems[0], sems[1], c, chips)
        relayed = _relayed_copies(full_refs, sems[2], sems[3]) if relay else []
        for a in range(N_GATHERED):
            for j in chips:
                sem = arrivals.at[arrival_sem_of(a, j)]
                _gather_copy(_BLOCK_OF[a], full_refs[a], _block_id(*_other_chips(x, y)[j], c), sem, sem,
                             (x, y, c)).wait_recv()
            for cp in relayed[a:a + 1] + passed[a * len(chips):(a + 1) * len(chips)]:
                cp.start()

    outs = pl.pallas_call(
        body, name=name, in_specs=[HBM] * N_GATHERED + [_SEM] + [HBM] * len(after),
        out_specs=[_SEM] * len(sem_shapes) + [HBM] * N_GATHERED,
        out_shape=sem_shapes + [pltpu.HBM(a.shape, a.dtype) for a in landing],
        input_output_aliases={a: len(sem_shapes) + a for a in range(N_GATHERED)},
        compiler_params=_params(has_side_effects=_DATAFLOW),
    )(*landing, arrival_sems, *after)
    return outs[:len(sem_shapes)], outs[len(sem_shapes):]


def _gather_weights_finish(layer, index, first_sems, passed, shards, landing, relay_send_sems=None):
    relayed = relay_send_sems is not None
    passed_sems = [sem for (send, recv), _ in passed for sem in (send, recv)] + ([relay_send_sems] if relayed else [])

    def body(win_ref, wout_ref, first_send, first_recv, local_sems, *rest):
        sems, (win_sh, wout_sh) = rest[:len(passed_sems)], rest[len(passed_sems):len(passed_sems) + 2]
        x, y, c = _mesh_position()
        full_refs = (win_ref, wout_ref)
        own, sent = _first_copies(index, (win_sh, wout_sh), full_refs, first_send, first_recv, local_sems, relayed)
        for a in range(N_GATHERED):
            sem = _first_sem(index, a, 0)
            _gather_copy(_BLOCK_OF[a], full_refs[a], _block_id(x, y, 1 - c), first_recv.at[sem], first_recv.at[sem],
                         (x, y, c)).wait_recv()
        for p, (_, chips) in enumerate(passed):
            for cp in _passed_on_copies(full_refs, sems[2 * p], sems[2 * p + 1], 1 - c, chips):
                cp.wait_recv()
            sent += _passed_on_copies(full_refs, sems[2 * p], sems[2 * p + 1], c, chips)
        if relayed:
            sent += _relayed_copies(full_refs, sems[-1], sems[-1])
        for cp in sent:
            cp.wait_send()
        for cp in own:
            cp.wait()

    return pl.pallas_call(
        body, name=f"all_gather_weights_finish_{layer}",
        in_specs=[HBM] * N_GATHERED + [_SEM] * (3 + len(passed_sems)) + [HBM] * 2,
        out_specs=[HBM] * N_GATHERED, out_shape=[pltpu.HBM(a.shape, a.dtype) for a in landing],
        input_output_aliases={a: a for a in range(N_GATHERED)}, compiler_params=_params(has_side_effects=_DATAFLOW),
    )(*landing, *first_sems, *passed_sems, *shards)


ALL_KINDS = (0, 1, 2)


def _sibling_exchange(grads, kinds):
    n_arr = len(grads)

    def make(in_refs, out_refs, sems):
        send_sems, recv_sems = sems
        x, y, c = _mesh_position()
        copies = [pltpu.make_async_remote_copy(
            src_ref=_BLOCK_OF[kinds[a]](in_refs[a], 2 * q + (1 - c)), dst_ref=out_refs[a].at[q],
            send_sem=send_sems.at[a * N_CHIP + q], recv_sem=recv_sems.at[a * N_CHIP + q],
            device_id=(x, y, 1 - c), device_id_type=MESH)
            for a in range(n_arr) for q in range(N_CHIP)]

        def start():
            for cp in copies:
                cp.start()

        def finish():
            for cp in copies:
                cp.wait_recv()
            for cp in copies:
                cp.wait_send()

        return start, finish

    return _Exchange(
        grads, [jax.ShapeDtypeStruct((N_CHIP,) + _BLOCK_SHAPES[k], BF16) for k in kinds], {},
        [pltpu.SemaphoreType.DMA((n_arr * N_CHIP,)), pltpu.SemaphoreType.DMA((n_arr * N_CHIP,))], make)


def _chips_exchange(layer, partials, received, kinds):
    n_arr = len(partials)

    def make(in_refs, out_refs, sems):
        send_sems, recv_sems = sems
        x, y, c = _mesh_position()
        copies = [pltpu.make_async_remote_copy(
            src_ref=in_refs[a].at[2 * qx + qy, layer], dst_ref=out_refs[a].at[j, layer],
            send_sem=send_sems.at[a * N_OTHER_CHIPS + j], recv_sem=recv_sems.at[a * N_OTHER_CHIPS + j],
            device_id=(qx, qy, c), device_id_type=MESH)
            for a in range(n_arr) for j, (qx, qy) in enumerate(_other_chips(x, y))]

        def start():
            for cp in copies:
                cp.start()

        def finish():
            for cp in copies:
                cp.wait_recv()
            for cp in copies:
                cp.wait_send()

        return start, finish

    inputs = list(partials)
    aliases = {}
    if received is not None:
        inputs += list(received)
        aliases = {n_arr + a: a for a in range(n_arr)}
    return _Exchange(
        inputs, [jax.ShapeDtypeStruct((N_OTHER_CHIPS, DEPTH) + _BLOCK_SHAPES[k], BF16) for k in kinds], aliases,
        [pltpu.SemaphoreType.DMA((n_arr * N_OTHER_CHIPS,)), pltpu.SemaphoreType.DMA((n_arr * N_OTHER_CHIPS,))], make)


def _all_gather_exchange(v):
    def make(in_refs, out_refs, sems):
        send_sems, recv_sems, local_sem = sems
        x, y, c = _mesh_position()
        me = _block_id(x, y, c)
        own = pltpu.make_async_copy(in_refs[0], out_refs[0].at[me], local_sem.at[0])
        sends, arrivals = [], []
        for k in range(1, N_DEV):
            px, py, pc = x ^ ((k >> 2) & 1), y ^ ((k >> 1) & 1), c ^ (k & 1)
            sends.append(pltpu.make_async_remote_copy(
                src_ref=in_refs[0], dst_ref=out_refs[0].at[me], send_sem=send_sems.at[k - 1],
                recv_sem=recv_sems.at[k - 1], device_id=(px, py, pc), device_id_type=MESH))
            arrivals.append(pltpu.make_async_remote_copy(
                src_ref=in_refs[0], dst_ref=out_refs[0].at[_block_id(px, py, pc)], send_sem=send_sems.at[k - 1],
                recv_sem=recv_sems.at[k - 1], device_id=(x, y, c), device_id_type=MESH))

        def start():
            for cp in [own] + sends:
                cp.start()

        def finish():
            for cp in arrivals:
                cp.wait_recv()
            for cp in sends:
                cp.wait_send()
            own.wait()

        return start, finish

    return _Exchange(
        [v], [jax.ShapeDtypeStruct((N_DEV,) + v.shape, v.dtype)], {},
        [pltpu.SemaphoreType.DMA((N_DEV - 1,)), pltpu.SemaphoreType.DMA((N_DEV - 1,)),
         pltpu.SemaphoreType.DMA((1,))], make)


def _host(exchange, args, in_specs, out_shape, out_specs, scratch):
    n_own = (len(args), len(out_shape), len(scratch))
    if exchange is None:
        return {}, lambda refs: (refs, None)
    n_ex = (len(exchange.inputs), len(exchange.out_shapes), len(exchange.sem_shapes))
    aliases = {n_own[0] + i: n_own[1] + o for i, o in exchange.aliases.items()}
    args += exchange.inputs
    in_specs += [HBM] * n_ex[0]
    out_shape += exchange.out_shapes
    out_specs += [HBM] * n_ex[1]
    scratch += exchange.sem_shapes

    def split(refs):
        own, theirs, at = [], [], 0
        for mine, ex in zip(n_own, n_ex):
            own += refs[at:at + mine]
            theirs.append(refs[at + mine:at + mine + ex])
            at += mine + ex
        return own, exchange.make(*theirs)

    return aliases, split


def _all_gather_small(v, name, after=()):
    vmem = pl.BlockSpec(memory_space=pltpu.VMEM)

    def body(v_ref, *rest):
        out_ref, send_sems, recv_sems = rest[len(after):]
        x, y, c = _mesh_position()
        me = _block_id(x, y, c)
        out_ref[me] = v_ref[...]
        sends = []
        for k in range(1, N_DEV):
            px, py, pc = x ^ ((k >> 2) & 1), y ^ ((k >> 1) & 1), c ^ (k & 1)
            send = pltpu.make_async_remote_copy(
                src_ref=v_ref, dst_ref=out_ref.at[me], send_sem=send_sems.at[k - 1], recv_sem=recv_sems.at[k - 1],
                device_id=(px, py, pc), device_id_type=MESH)
            send.start()
            sends.append((send, _block_id(px, py, pc)))
        for k, (send, peer) in enumerate(sends):
            pltpu.make_async_remote_copy(
                src_ref=v_ref, dst_ref=out_ref.at[peer], send_sem=send_sems.at[k], recv_sem=recv_sems.at[k],
                device_id=(x, y, c), device_id_type=MESH).wait_recv()
        for send, _ in sends:
            send.wait_send()

    return pl.pallas_call(
        body, name=name, in_specs=[vmem] + [HBM] * len(after), out_specs=vmem,
        out_shape=jax.ShapeDtypeStruct((N_DEV,) + v.shape, v.dtype),
        scratch_shapes=[pltpu.SemaphoreType.DMA((N_DEV - 1,)), pltpu.SemaphoreType.DMA((N_DEV - 1,))],
        compiler_params=_params(),
    )(v, *after)


def _forward_layer(layer, x, vec, cps, wpool, win, wout, target=None):
    t_len = x.shape[0]
    n_tiles = t_len // ROW_TILE
    row = lambda cols: pl.BlockSpec((ROW_TILE, cols), lambda i: (i, 0))
    widths = (D_MODEL, 2 * CONV_W, 3 * CONV_W, 4 * POOL_W)
    head = target is not None

    def body(x_ref, vec_ref, cps_ref, wpool_ref, win_ref, wout_ref, *rest):
        target_ref = rest[0] if head else None
        xo_ref, y_ref, h_ref, ycat_ref, uc_ref, fa_ref, fp_ref = rest[head:head + 7]
        loss_ref = rest[head + 7] if head else None
        zc_ref, pc_ref = rest[-2:]
        i = pl.program_id(0)

        @pl.when(i == 0)
        def _():
            zc_ref[...] = jnp.zeros_like(zc_ref)
            pc_ref[...] = jnp.zeros_like(pc_ref)
            if head:
                loss_ref[...] = jnp.zeros_like(loss_ref)

        x_t = x_ref[...]
        shift, scale, gate = vec_ref[0:1, :], vec_ref[1:2, :], vec_ref[2:3, :]
        g_pre, g_post = vec_ref[3:4, :], vec_ref[4:5, :]
        w0, w1, w2, ps = cps_ref[0:1, :], cps_ref[1:2, :], cps_ref[2:3, :], cps_ref[3:4, :]
        rx = lax.rsqrt(jnp.mean(x_t * x_t, axis=-1, keepdims=True) + NORM_EPS)
        h = (x_t * rx) * g_pre * (1.0 + scale) + shift
        h_ref[...] = h.astype(BF16)
        proj = _dot(h.astype(BF16), win_ref[...])
        u_a, b_a, c_a, g_a, u_p, g_p = _split_proj(proj)
        uc_ref[...] = jnp.concatenate([u_a, c_a], axis=1).astype(BF16)

        z = c_a * u_a
        zcat = jnp.concatenate([zc_ref[...], z], axis=0)
        zc_ref[...] = z[ROW_TILE - CONV_HALO:]
        conv = (w0 * _rows_from_before(zcat, 2)[CONV_HALO:] + w1 * _rows_from_before(zcat, 1)[CONV_HALO:] + w2 * z)
        sig_a = _sigmoid(g_a)
        silu_a = g_a * sig_a
        b_conv = b_a * conv
        y_a = b_conv * silu_a
        fa_ref[...] = jnp.concatenate(
            [silu_a * conv, silu_a * b_a, b_conv * (sig_a + silu_a * (1.0 - sig_a))], axis=1).astype(BF16)

        pcat = jnp.concatenate([pc_ref[...], u_p], axis=0)
        pc_ref[...] = u_p[ROW_TILE - POOL_HALO:]
        counts = _window_counts(i * ROW_TILE, ROW_TILE)
        pooled, mixed = [], []
        for g, w in enumerate(POOL_WINDOWS):
            cols = slice(g * GROUP_D, (g + 1) * GROUP_D)
            s = pcat[:, cols]
            step = 1
            while step < w:
                s = s + _rows_from_before(s, step)
                step *= 2
            pooled_g = (s[POOL_HALO:] * (1.0 / jnp.minimum(counts, float(w))) - u_p[:, cols]).astype(BF16)
            pooled.append(pooled_g)
            mixed.append(_dot(pooled_g, wpool_ref[g]))
        mixed = jnp.concatenate(mixed, axis=1)
        sig_p = _sigmoid(g_p)
        silu_p = g_p * sig_p
        mixed_ps = mixed * ps
        y_p = mixed_ps * silu_p
        fp_ref[...] = jnp.concatenate(
            [(ps * silu_p).astype(BF16), (mixed_ps * (sig_p + silu_p * (1.0 - sig_p))).astype(BF16),
             (silu_p * mixed).astype(BF16)] + pooled, axis=1)

        ycat = jnp.concatenate([y_a, y_p], axis=1)
        ycat_ref[...] = ycat.astype(BF16)
        y_b = _dot(ycat.astype(BF16), wout_ref[...]).astype(BF16)
        y_ref[...] = y_b
        y_t = y_b.astype(F32)
        ry = lax.rsqrt(jnp.mean(y_t * y_t, axis=-1, keepdims=True) + NORM_EPS)
        x_next = x_t + gate * (y_t * ry * g_post)
        if head:
            err = x_next - target_ref[...]
            xo_ref[...] = err * (1.0 / D_MODEL)
            loss_ref[...] += jnp.sum(err * err) * (0.5 / D_MODEL)
        else:
            xo_ref[...] = x_next

    tile = (SUBLANES, LANES)
    return pl.pallas_call(
        body, name=f"forward_layer_{layer}", grid=(n_tiles,),
        in_specs=[row(D_MODEL), _layer_spec(vec.shape, layer), _layer_spec(cps.shape, layer),
                  _layer_spec(wpool.shape, layer), _whole_spec(win.shape), _whole_spec(wout.shape)]
        + [row(D_MODEL)] * head,
        out_specs=[row(D_MODEL), row(D_MODEL), row(D_MODEL), row(D_MODEL)] + [row(w) for w in widths[1:]]
        + [_whole_spec(tile)] * head,
        out_shape=[jax.ShapeDtypeStruct((t_len, D_MODEL), F32), jax.ShapeDtypeStruct((t_len, D_MODEL), BF16),
                   jax.ShapeDtypeStruct((t_len, D_MODEL), BF16), jax.ShapeDtypeStruct((t_len, D_MODEL), BF16)]
        + [jax.ShapeDtypeStruct((t_len, w), BF16) for w in widths[1:]] + [jax.ShapeDtypeStruct(tile, F32)] * head,
        scratch_shapes=[pltpu.VMEM((CONV_HALO, CONV_W), F32), pltpu.VMEM((POOL_HALO, POOL_W), F32)],
        compiler_params=_params(dimension_semantics=("arbitrary",)),
    )(x, vec, cps, wpool, win, wout, *([target] * head))


def _backward_layer(layer, dxo, y, x, uc, fa, fp, vec, cps, wpool, wout, win):
    t_len = dxo.shape[0]
    n_tiles = t_len // BWD_TILE
    halo_per_tile = BWD_TILE // POOL_HALO
    rev = lambda cols: pl.BlockSpec((BWD_TILE, cols), lambda i: (n_tiles - 1 - i, 0))
    halo_spec = pl.BlockSpec(
        (POOL_HALO, 2 * CONV_W), lambda i: (jnp.maximum((n_tiles - 1 - i) * halo_per_tile - 1, 0), 0))
    gwpool_shape = (len(POOL_WINDOWS), GROUP_D, GROUP_D)

    def body(dxo_ref, y_ref, x_ref, uc_ref, uch_ref, fa_ref, fp_ref, vec_ref, cps_ref, wpool_ref, wout_ref, win_ref,
             dx_ref, dproj_ref, dy_ref, gwpool_ref, dcps_ref, dvec_ref, gwpool_acc, dcc_ref, qc_ref):
        i = pl.program_id(0)
        tile = n_tiles - 1 - i

        @pl.when(i == 0)
        def _():
            gwpool_acc[...] = jnp.zeros_like(gwpool_acc)
            dcps_ref[...] = jnp.zeros_like(dcps_ref)
            dvec_ref[...] = jnp.zeros_like(dvec_ref)
            dcc_ref[...] = jnp.zeros_like(dcc_ref)
            qc_ref[...] = jnp.zeros_like(qc_ref)

        shift, scale, gate = vec_ref[0:1, :], vec_ref[1:2, :], vec_ref[2:3, :]
        g_pre, g_post = vec_ref[3:4, :], vec_ref[4:5, :]
        w0, w1, w2 = cps_ref[0:1, :], cps_ref[1:2, :], cps_ref[2:3, :]

        dxo_t = dxo_ref[...]
        y_t = y_ref[...].astype(F32)
        ry = lax.rsqrt(jnp.mean(y_t * y_t, axis=-1, keepdims=True) + NORM_EPS)
        yh = y_t * ry
        dvec_ref[2:3, :] += jnp.sum(dxo_t * yh, axis=0, keepdims=True)
        dyh = dxo_t * (gate * g_post)
        dy_b = (ry * (dyh - yh * jnp.mean(dyh * yh, axis=-1, keepdims=True))).astype(BF16)
        dy_ref[...] = dy_b
        dycat = _dot_nt(dy_b, wout_ref[...])
        dy_a, dy_p = dycat[:, :CONV_W], dycat[:, CONV_W:]

        fa_t = fa_ref[...].astype(F32)
        db_a = dy_a * fa_t[:, :CONV_W]
        dconv = dy_a * fa_t[:, CONV_W:2 * CONV_W]
        dg_a = dy_a * fa_t[:, 2 * CONV_W:]
        uc_t = uc_ref[...].astype(F32)
        u_a, c_a = uc_t[:, :CONV_W], uc_t[:, CONV_W:]
        halo = jnp.where(tile > 0, uch_ref[...].astype(F32), 0.0)[POOL_HALO - CONV_HALO:]
        z = c_a * u_a
        zcat = jnp.concatenate([halo[:, CONV_W:] * halo[:, :CONV_W], z], axis=0)
        z1 = _rows_from_before(zcat, 1)[CONV_HALO:]
        z2 = _rows_from_before(zcat, 2)[CONV_HALO:]
        dccat = jnp.concatenate([dconv, dcc_ref[...]], axis=0)
        dc1 = _rows_from_after(dccat, 1)[:BWD_TILE]
        dc2 = _rows_from_after(dccat, 2)[:BWD_TILE]
        dz = w2 * dconv + w1 * dc1 + w0 * dc2
        dcc_ref[...] = dconv[:CONV_HALO]
        dcps_ref[0:1, :] += jnp.sum(dconv * z2, axis=0, keepdims=True)
        dcps_ref[1:2, :] += jnp.sum(dconv * z1, axis=0, keepdims=True)
        dcps_ref[2:3, :] += jnp.sum(dconv * z, axis=0, keepdims=True)
        du_a = dz * c_a
        dc_a = dz * u_a

        dmixed = (dy_p * fp_ref[:, :POOL_W].astype(F32)).astype(BF16)
        dg_p = dy_p * fp_ref[:, POOL_W:2 * POOL_W].astype(F32)
        dcps_ref[3:4, :] += jnp.sum(dy_p * fp_ref[:, 2 * POOL_W:3 * POOL_W].astype(F32), axis=0, keepdims=True)
        counts = _window_counts(tile * BWD_TILE, BWD_TILE)
        du_p, q_head = [], []
        for g, w in enumerate(POOL_WINDOWS):
            cols = slice(g * GROUP_D, (g + 1) * GROUP_D)
            dm_g = dmixed[:, cols]
            dpooled_g = _dot_nt(dm_g, wpool_ref[g])
            gwpool_acc[g] += _dot_tn(fp_ref[:, 3 * POOL_W + g * GROUP_D:3 * POOL_W + (g + 1) * GROUP_D], dm_g)
            q_g = dpooled_g * (1.0 / jnp.minimum(counts, float(w)))
            q_head.append(q_g[:POOL_HALO])
            s = jnp.concatenate([q_g, qc_ref[:, cols]], axis=0)
            step = 1
            while step < w:
                s = s + _rows_from_after(s, step)
                step *= 2
            du_p.append(s[:BWD_TILE] - dpooled_g)
        qc_ref[...] = jnp.concatenate(q_head, axis=1)
        dproj_b = jnp.concatenate([du_a, db_a, dc_a, dg_a] + du_p + [dg_p], axis=1).astype(BF16)
        dproj_ref[...] = dproj_b

        x_t = x_ref[...]
        rx = lax.rsqrt(jnp.mean(x_t * x_t, axis=-1, keepdims=True) + NORM_EPS)
        xn = x_t * rx
        mod_scale = 1.0 + scale
        dh = _dot_nt(dproj_b, win_ref[...])
        dvec_ref[0:1, :] += jnp.sum(dh, axis=0, keepdims=True)
        dvec_ref[1:2, :] += jnp.sum(dh * xn, axis=0, keepdims=True)
        dxn = dh * (g_pre * mod_scale)
        dx_ref[...] = dxo_t + rx * (dxn - xn * jnp.mean(dxn * xn, axis=-1, keepdims=True))

        @pl.when(i == n_tiles - 1)
        def _():
            gwpool_ref[...] = gwpool_acc[...].astype(BF16)
            sum_dh_xn, sum_dxo_yh = dvec_ref[1:2, :], dvec_ref[2:3, :]
            dvec_ref[1:2, :] = sum_dh_xn * g_pre
            dvec_ref[3:4, :] = sum_dh_xn * mod_scale
            dvec_ref[2:3, :] = sum_dxo_yh * g_post
            dvec_ref[4:5, :] = sum_dxo_yh * gate

    return pl.pallas_call(
        body, name=f"backward_layer_{layer}", grid=(n_tiles,),
        in_specs=[rev(D_MODEL), rev(D_MODEL), rev(D_MODEL), rev(2 * CONV_W), halo_spec, rev(3 * CONV_W),
                  rev(4 * POOL_W), _layer_spec(vec.shape, layer), _layer_spec(cps.shape, layer),
                  _layer_spec(wpool.shape, layer), _whole_spec(wout.shape), _whole_spec(win.shape)],
        out_specs=[rev(D_MODEL), rev(IN_COLS), rev(D_MODEL), _whole_spec(gwpool_shape),
                   _whole_spec((SUBLANES, CONV_W)), _whole_spec((SUBLANES, D_MODEL))],
        out_shape=[jax.ShapeDtypeStruct((t_len, D_MODEL), F32), jax.ShapeDtypeStruct((t_len, IN_COLS), BF16),
                   jax.ShapeDtypeStruct((t_len, D_MODEL), BF16), jax.ShapeDtypeStruct(gwpool_shape, BF16),
                   jax.ShapeDtypeStruct((SUBLANES, CONV_W), F32), jax.ShapeDtypeStruct((SUBLANES, D_MODEL), F32)],
        scratch_shapes=[pltpu.VMEM(gwpool_shape, F32), pltpu.VMEM((CONV_HALO, CONV_W), F32),
                        pltpu.VMEM((POOL_HALO, POOL_W), F32)],
        compiler_params=_params(dimension_semantics=("arbitrary",)),
    )(dxo, y, x, uc, uc, fa, fp, vec, cps, wpool, wout, win)


def _weight_grads(name, products, exchange, after=()):
    steps = [b.shape[1] // band for _, b, band in products]
    first = [sum(steps[:i]) for i in range(len(products))]
    args, in_specs, out_shape, out_specs, scratch = [], [], [], [], []
    for (a, b, band), n, s0 in zip(products, steps, first):
        band_of_step = lambda s, n=n, s0=s0: (0, jnp.clip(s - s0, 0, n - 1))
        args += [a, b]
        in_specs += [_whole_spec(a.shape), pl.BlockSpec((b.shape[0], band), band_of_step)]
        out_shape.append(jax.ShapeDtypeStruct((a.shape[1], b.shape[1]), BF16))
        out_specs.append(pl.BlockSpec((a.shape[1], band), band_of_step))
    args += after
    in_specs += [HBM] * len(after)
    aliases, split = _host(exchange, args, in_specs, out_shape, out_specs, scratch)

    def body(*refs):
        own, hosted = split(refs)
        in_refs, out_refs = own[:2 * len(products)], own[-len(products):]
        s = pl.program_id(0)
        if hosted is not None:
            pl.when(s == 0)(hosted[0])
        for i, (n, s0) in enumerate(zip(steps, first)):
            @pl.when((s >= s0) & (s < s0 + n))
            def _(i=i):
                out_refs[i][...] = _dot_tn(in_refs[2 * i][...], in_refs[2 * i + 1][...]).astype(BF16)

        if hosted is not None:
            pl.when(s == sum(steps) - 1)(hosted[1])

    return pl.pallas_call(
        body, name=name, grid=(sum(steps),), in_specs=in_specs, out_specs=out_specs,
        out_shape=out_shape, scratch_shapes=scratch, input_output_aliases=aliases,
        compiler_params=_params(dimension_semantics=("arbitrary",)),
    )(*args)


def _add_sibling_blocks(name, layer, grads, received, core, partials, kinds):
    n_arr = len(grads)

    def body(core_ref, *refs):
        mine, theirs, outs = refs[:n_arr], refs[n_arr:2 * n_arr], refs[-n_arr:]
        for a in range(n_arr):
            outs[a][...] = (mine[a][...].astype(F32) + theirs[a][...].astype(F32)).astype(BF16)

    own_of_kind = [
        pl.BlockSpec((D_MODEL, W_IN_SHARD), lambda q, core_ref: (0, 2 * q + core_ref[0])),
        pl.BlockSpec((W_OUT_SHARD, D_MODEL), lambda q, core_ref: (2 * q + core_ref[0], 0)),
        pl.BlockSpec((POOL_SHARD, GROUP_D), lambda q, core_ref: (2 * q + core_ref[0], 0)),
    ]
    shapes = [_BLOCK_SHAPES[k] for k in kinds]
    recv_specs = [pl.BlockSpec((None,) + s, lambda q, core_ref: (q, 0, 0)) for s in shapes]
    out_specs = [pl.BlockSpec((None, None) + s, lambda q, core_ref: (q, layer, 0, 0)) for s in shapes]
    args = [core, *grads, *received]
    in_specs = [own_of_kind[k] for k in kinds] + recv_specs
    aliases = {}
    if partials is not None:
        aliases = {len(args) + a: a for a in range(n_arr)}
        args += list(partials)
        in_specs += [HBM] * n_arr
    return pl.pallas_call(
        body, name=name,
        grid_spec=pltpu.PrefetchScalarGridSpec(
            num_scalar_prefetch=1, grid=(N_CHIP,), in_specs=in_specs, out_specs=out_specs),
        out_shape=[jax.ShapeDtypeStruct((N_CHIP, DEPTH) + s, BF16) for s in shapes],
        input_output_aliases=aliases,
        compiler_params=_params(dimension_semantics=("arbitrary",)),
    )(*args)


def _modulation_columns(c_all, w_ada):
    def body(c_ref, w_ref, cact_ref, out_ref):
        c_t = c_ref[...]
        c_act = c_t * _sigmoid(c_t)
        cact_ref[...] = c_act
        out_ref[...] = jnp.dot(c_act, w_ref[...], preferred_element_type=F32, precision=lax.Precision.HIGHEST)

    return pl.pallas_call(
        body, name="modulation_columns", grid=(DEPTH,),
        in_specs=[pl.BlockSpec((N_DEV, D_MODEL), lambda l: (0, 0)),
                  pl.BlockSpec((None, D_MODEL, W_IN_SHARD), lambda l: (l, 0, 0))],
        out_specs=[pl.BlockSpec((N_DEV, D_MODEL), lambda l: (0, 0)),
                   pl.BlockSpec((N_DEV, W_IN_SHARD), lambda l: (0, l))],
        out_shape=[jax.ShapeDtypeStruct((N_DEV, D_MODEL), F32),
                   jax.ShapeDtypeStruct((N_DEV, DEPTH * W_IN_SHARD), F32)],
        compiler_params=_params(dimension_semantics=("arbitrary",)),
    )(c_all, w_ada)


def _adamw(w, g, m, v):
    m_new = ADAM_B1 * m + (1.0 - ADAM_B1) * g
    v_new = ADAM_B2 * v + (1.0 - ADAM_B2) * (g * g)
    m_hat = m_new / (1.0 - ADAM_B1 ** ADAM_STEP)
    v_hat = v_new / (1.0 - ADAM_B2 ** ADAM_STEP)
    delta = -ADAM_LR * (m_hat / (jnp.sqrt(v_hat) + ADAM_EPS) + ADAM_WD * w)
    return delta, m_new, v_new


def _adamw_w_ada(w, m, v, c_act_t, dmod_cols):
    def body(w_ref, m_ref, v_ref, ct_ref, dm_ref, g_ref, d_ref, mo_ref, vo_ref):
        g = ct_ref[:, 0:1] * dm_ref[0:1, :]
        for b in range(1, N_DEV):
            g = g + ct_ref[:, b:b + 1] * dm_ref[b:b + 1, :]
        g_ref[...] = g
        d_ref[...], mo_ref[...], vo_ref[...] = _adamw(w_ref[...], g, m_ref[...], v_ref[...])

    big = pl.BlockSpec((None, D_MODEL, W_IN_SHARD), lambda l: (l, 0, 0))
    return pl.pallas_call(
        body, name="adamw_w_ada", grid=(DEPTH,),
        in_specs=[big, big, big, pl.BlockSpec((D_MODEL, N_DEV), lambda l: (0, 0)),
                  pl.BlockSpec((None, N_DEV, W_IN_SHARD), lambda l: (l, 0, 0))],
        out_specs=[big] * 4, out_shape=[jax.ShapeDtypeStruct(w.shape, F32)] * 4,
        compiler_params=_params(dimension_semantics=("arbitrary",)),
    )(w, m, v, c_act_t, dmod_cols)


def _sum_chip_partials(own_ref, recv_ref):
    g = own_ref[...].astype(F32)
    for j in range(N_OTHER_CHIPS):
        g = g + recv_ref[j].astype(F32)
    return g


def _partial_specs(row_tile, cols, first_layer=0):
    own = pl.BlockSpec((None, None, row_tile, cols), lambda l, r, chip_ref: (chip_ref[0], first_layer + l, r, 0))
    recv = pl.BlockSpec((N_OTHER_CHIPS, None, row_tile, cols), lambda l, r, chip_ref: (0, first_layer + l, r, 0))
    return own, recv


def _adamw_reduced(name, w, m, v, partial, received, chip, row_tile, layers, continued):
    depth, rows, cols = w.shape
    first, stop = layers

    def body(chip_ref, w_ref, m_ref, v_ref, own_ref, recv_ref, *rest):
        g_ref, d_ref, mo_ref, vo_ref = rest[-4:]
        g = _sum_chip_partials(own_ref, recv_ref)
        g_ref[...] = g
        d_ref[...], mo_ref[...], vo_ref[...] = _adamw(w_ref[...], g, m_ref[...], v_ref[...])

    blk = pl.BlockSpec((None, row_tile, cols), lambda l, r, chip_ref: (first + l, r, 0))
    args = [chip, w, m, v, partial, received]
    in_specs = [blk, blk, blk, *_partial_specs(row_tile, cols, first)]
    aliases = {}
    if continued is not None:
        aliases = {len(args) + k: k for k in range(4)}
        args += list(continued)
        in_specs += [HBM] * 4
    return pl.pallas_call(
        body, name=name,
        grid_spec=pltpu.PrefetchScalarGridSpec(
            num_scalar_prefetch=1, grid=(stop - first, rows // row_tile), in_specs=in_specs, out_specs=[blk] * 4),
        out_shape=[jax.ShapeDtypeStruct(w.shape, F32)] * 4, input_output_aliases=aliases,
        compiler_params=_params(dimension_semantics=("arbitrary", "arbitrary")),
    )(*args)


def _reduce_w_pool(partial, received, chip):
    def body(chip_ref, own_ref, recv_ref, g_ref):
        g_ref[...] = _sum_chip_partials(own_ref, recv_ref)

    return pl.pallas_call(
        body, name="reduce_w_pool",
        grid_spec=pltpu.PrefetchScalarGridSpec(
            num_scalar_prefetch=1, grid=(DEPTH, 1), in_specs=list(_partial_specs(POOL_SHARD, GROUP_D)),
            out_specs=pl.BlockSpec((POOL_SHARD, GROUP_D), lambda l, r, chip_ref: (l, 0))),
        out_shape=jax.ShapeDtypeStruct((DEPTH * POOL_SHARD, GROUP_D), F32),
        compiler_params=_params(dimension_semantics=("arbitrary", "arbitrary")),
    )(chip, partial, received)


def _adamw_small(params):
    n = len(params)

    def body(*refs):
        ins, outs = refs[:4 * n], refs[4 * n:]
        for p in range(n):
            w_ref, g_ref, m_ref, v_ref = ins[4 * p:4 * p + 4]
            d_ref, mo_ref, vo_ref = outs[3 * p:3 * p + 3]
            d_ref[...], mo_ref[...], vo_ref[...] = _adamw(w_ref[...], g_ref[...], m_ref[...], v_ref[...])

    vmem = pl.BlockSpec(memory_space=pltpu.VMEM)
    flat = [a for group in params for a in group]
    out_shape = [jax.ShapeDtypeStruct(group[0].shape, F32) for group in params for _ in range(3)]
    outs = pl.pallas_call(
        body, name="adamw_small", in_specs=[vmem] * len(flat), out_specs=[vmem] * len(out_shape),
        out_shape=out_shape, compiler_params=_params(),
    )(*flat)
    return [tuple(outs[3 * p:3 * p + 3]) for p in range(n)]


def _sum_sources(slabs):
    def body(s_ref, o_ref):
        acc = s_ref[0]
        for b in range(1, N_DEV):
            acc = acc + s_ref[b]
        o_ref[...] = acc

    vmem = pl.BlockSpec(memory_space=pltpu.VMEM)
    return pl.pallas_call(
        body, name="sum_small_grads", in_specs=[vmem], out_specs=vmem,
        out_shape=jax.ShapeDtypeStruct(slabs.shape[1:], F32), compiler_params=_params(),
    )(slabs)


def _to_bf16(a, name, layers=None):
    first, stop = layers or (0, a.shape[0])

    def body(a_ref, o_ref):
        o_ref[...] = a_ref[...].astype(BF16)

    block = (None,) + a.shape[1:]
    return pl.pallas_call(
        body, name=name, grid=(stop - first,), in_specs=[pl.BlockSpec(block, lambda l: (first + l, 0, 0))],
        out_specs=pl.BlockSpec(block, lambda l: (l, 0, 0)),
        out_shape=jax.ShapeDtypeStruct((stop - first,) + a.shape[1:], BF16),
        compiler_params=_params(dimension_semantics=("arbitrary",)),
    )(a)


def kernel(x, c, w_ada, b_ada, g_pre, w_in, w_conv, w_pool, pool_scale, w_out, g_post, loss_target, m_w_ada, m_b_ada, m_g_pre, m_w_in, m_w_conv, m_w_pool, m_pool_scale, m_w_out, m_g_post, v_w_ada, v_b_ada, v_g_pre, v_w_in, v_w_conv, v_w_pool, v_pool_scale, v_w_out, v_g_post):
    mx, my, mc = _mesh_position()
    me = _block_id(mx, my, mc)
    chip = (2 * mx + my).astype(jnp.int32).reshape(1)
    core = mc.astype(jnp.int32).reshape(1)
    x0 = x[0]
    target = loss_target[0]
    conv_shard = w_conv.shape[-1]

    own_small = jnp.concatenate([c, w_conv.reshape(1, DEPTH * 3 * conv_shard)], axis=1)
    first_shards = [_to_bf16(w_in, "cast_w_in_0", (0, 1)), _to_bf16(w_out, "cast_w_out_0", (0, 1))]
    all_small = _all_gather_small(own_small, "all_gather_c_w_conv", first_shards)[:, 0, :]
    gathers = [_gather_weights_start(0, *first_shards, [all_small], relayed=True)]
    c_all = all_small[:, :D_MODEL]
    w_conv_full = all_small[:, D_MODEL:].reshape(N_DEV, DEPTH, 3, conv_shard).transpose(1, 2, 0, 3).reshape(
        DEPTH, 3, CONV_W)
    cps = jnp.concatenate([w_conv_full, pool_scale[:, None], jnp.zeros((DEPTH, 4, CONV_W), F32)], axis=1)

    c_act, pieces = _modulation_columns(c_all, w_ada)
    upper = (1, DEPTH)
    upper_shards = [_to_bf16(w_in, "cast_w_in_1", upper), _to_bf16(w_out, "cast_w_out_1", upper)]
    wpool_b = _to_bf16(w_pool.reshape(DEPTH, POOL_ROWS, GROUP_D), "cast_w_pool").reshape(w_pool.shape)
    first_sems_0, _, (zones_0,) = gathers[0]
    neighbour_sems, zones_0 = _gather_weights_pass_on(
        "all_gather_weights_relay_0", NEIGHBOUR_CHIPS, first_sems_0[1], partial(_first_arrival, 0), zones_0,
        [pieces, *upper_shards, wpool_b], relay=True)
    mod_all = _all_gather_small(pieces, "all_gather_modulation", [zones_0[0]])
    mod_mine = lax.dynamic_index_in_dim(mod_all, me, axis=1, keepdims=False)
    mod = mod_mine.reshape(N_DEV, DEPTH, W_IN_SHARD).transpose(1, 0, 2).reshape(DEPTH, 3 * D_MODEL) + b_ada
    zeros_d = jnp.zeros((DEPTH, 3, D_MODEL), F32)
    vec = jnp.concatenate([mod.reshape(DEPTH, 3, D_MODEL), g_pre[:, None], g_post[:, None], zeros_d], axis=1)

    gathers.append(_gather_weights_start(1, *upper_shards, [mod_all]))
    gathers = [(first_sems, shards, landing[k], k) for first_sems, shards, landing in gathers
               for k in range(len(landing))]

    xs, kept, wins, wouts = [x0], [], [], []
    for l in range(DEPTH):
        first_sems, shards, zones, index = gathers[l]
        if l == 0:
            diagonal_sems, zones = _gather_weights_pass_on(
                "all_gather_weights_pass_on_0", DIAGONAL_CHIP, neighbour_sems[3], lambda a, j: a, zones_0,
                [vec, cps, gathers[-1][1][0]])
            passed = [(neighbour_sems[:2], NEIGHBOUR_CHIPS), (diagonal_sems, DIAGONAL_CHIP)]
            win, wout = _gather_weights_finish(l, index, first_sems, passed, shards, zones, neighbour_sems[2])
        else:
            passed_sems, zones = _gather_weights_pass_on(
                f"all_gather_weights_pass_on_{l}", ALL_OTHER_CHIPS, first_sems[1], partial(_first_arrival, index),
                zones, [xs[-1]])
            win, wout = _gather_weights_finish(l, index, first_sems, [(passed_sems, ALL_OTHER_CHIPS)], shards, zones)
        x_next, *for_backward = _forward_layer(
            l, xs[-1], vec, cps, wpool_b, win, wout, target if l == DEPTH - 1 else None)
        xs.append(x_next)
        kept.append(for_backward[:6])
        wins.append(win)
        wouts.append(wout)
    dx, loss_tile = xs[DEPTH], for_backward[6]

    slab_rows = [None] * DEPTH
    partials = received = None
    in_flight = []

    def scatter(layer, grads, from_sibling, after):
        nonlocal partials, received
        partials = _add_sibling_blocks(
            f"grad_add_sibling_{layer}", layer, grads, from_sibling, core, partials, ALL_KINDS)
        chips = _chips_exchange(layer, partials, received, ALL_KINDS)
        sems, partials, received, token = _start_exchange(chips, f"grad_chips_start_{layer}", after)
        in_flight.append((chips, sems, layer))
        return token

    grads_above = None
    for l in reversed(range(DEPTH)):
        y, h, ycat, uc, fa, fp = kept[l]
        dx, dproj, dy, gwpool, dcps, dvec = _backward_layer(
            l, dx, y, xs[l], uc, fa, fp, vec, cps, wpool_b, wouts[l], wins[l])
        slab_rows[l] = jnp.concatenate(
            [dvec[0], dvec[1], dvec[2], dvec[3], dvec[4], dcps[3], dcps[0], dcps[1], dcps[2],
             loss_tile[0] if l == 0 else jnp.zeros((LANES,), F32)])
        after = []
        if l == 0:
            gather_small = _all_gather_exchange(jnp.stack(slab_rows))
            sems_s, slab, slabs, token = _start_exchange(gather_small, "all_gather_small_grads_start")
            after = [token]
        hosted = _sibling_exchange(grads_above, ALL_KINDS) if grads_above is not None else None
        for_w_in, for_w_out = (h, dproj, GWIN_COLS), (ycat, dy, GWOUT_COLS)
        if l > 1:
            gwin, gwout, *from_sibling = _weight_grads(f"weight_grads_{l}", [for_w_in, for_w_out], hosted, after)
        else:
            gwin, *from_sibling = _weight_grads(f"weight_grads_in_{l}", [for_w_in], hosted, after)
            gwout, *sibling_w_in = _weight_grads(
                f"weight_grads_out_{l}", [for_w_out], _sibling_exchange([gwin], ALL_KINDS[:1]))
        if l == 0:
            _, (slabs,) = _finish_exchange(
                gather_small, "all_gather_small_grads_finish", sems_s, slab, slabs, [gwin])
        if grads_above is not None:
            scatter(l + 1, grads_above, from_sibling, [])
        grads_above = [gwin, gwout, gwpool.reshape(POOL_ROWS, GROUP_D)]
        if l <= 1:
            sibling_rest = _run_exchange(
                _sibling_exchange(grads_above[1:], ALL_KINDS[1:]), f"grad_exchange_sibling_{l}")
            token = scatter(l, grads_above, [*sibling_w_in, *sibling_rest], [slabs] if l == 0 else [])
            grads_above = None
    grad_x = dx[None]
    chips_0, sems_0, _ = in_flight.pop()

    total = _sum_sources(slabs)
    loss = total[0, SLAB_COLS]
    o = 3 * D_MODEL
    g_b_ada = total[:, :o]
    g_g_pre = total[:, o:o + D_MODEL]
    g_g_post = total[:, o + D_MODEL:o + 2 * D_MODEL]
    g_pool_scale = total[:, o + 2 * D_MODEL:o + 2 * D_MODEL + POOL_W]
    g_conv_full = total[:, o + 2 * D_MODEL + POOL_W:SLAB_COLS].reshape(DEPTH, 3, CONV_W)
    g_w_conv = lax.dynamic_slice_in_dim(g_conv_full, me * conv_shard, conv_shard, axis=2)

    after = [token]
    for chips, sems, l in in_flight:
        partials, received = _finish_exchange(chips, f"grad_chips_finish_{l}", sems, partials, received, after)
        after = []
    upper = (1, DEPTH)
    w_in_upper = _adamw_reduced(
        "adamw_w_in_upper", w_in, m_w_in, v_w_in, partials[0], received[0], chip, ROW_TILE, upper, None)
    w_out_upper = _adamw_reduced(
        "adamw_w_out_upper", w_out, m_w_out, v_w_out, partials[1], received[1], chip, W_OUT_SHARD, upper, None)
    dmod_all = slabs[:, :, :o].reshape(N_DEV, DEPTH, N_DEV, W_IN_SHARD)
    dmod_cols = lax.dynamic_index_in_dim(dmod_all, me, axis=2, keepdims=False).transpose(1, 0, 2) + token[0, 0]
    g_w_ada, d_w_ada, nm_w_ada, nv_w_ada = _adamw_w_ada(w_ada, m_w_ada, v_w_ada, c_act.T, dmod_cols)

    partials, received = _finish_exchange(
        chips_0, "grad_chips_finish_0", sems_0, partials, received, [nv_w_ada, w_in_upper[3], w_out_upper[3]])
    gather_pool = _all_gather_exchange(_reduce_w_pool(partials[2], received[2], chip))
    sems_p, pool_rows, pool_landing, token_p = _start_exchange(gather_pool, "all_gather_grad_w_pool_start")
    g_w_in, d_w_in, nm_w_in, nv_w_in = _adamw_reduced(
        "adamw_w_in_0", w_in, m_w_in, v_w_in, partials[0], received[0], chip, ROW_TILE, (0, 1), w_in_upper)
    g_w_out, d_w_out, nm_w_out, nv_w_out = _adamw_reduced(
        "adamw_w_out_0", w_out, m_w_out, v_w_out, partials[1], received[1], chip, W_OUT_SHARD, (0, 1), w_out_upper)
    _, (g_pool_all,) = _finish_exchange(
        gather_pool, "all_gather_grad_w_pool_finish", sems_p, pool_rows, pool_landing, [nv_w_in, nv_w_out])
    g_w_pool = g_pool_all.reshape(N_DEV, DEPTH, POOL_SHARD, GROUP_D).transpose(1, 0, 2, 3).reshape(w_pool.shape)

    flat2 = lambda a: a.reshape(-1, a.shape[-1])
    small = _adamw_small([
        (b_ada, g_b_ada, m_b_ada, v_b_ada),
        (g_pre, g_g_pre, m_g_pre, v_g_pre),
        (flat2(w_conv), flat2(g_w_conv), flat2(m_w_conv), flat2(v_w_conv)),
        (flat2(w_pool), flat2(g_w_pool), flat2(m_w_pool), flat2(v_w_pool)),
        (pool_scale, g_pool_scale, m_pool_scale, v_pool_scale),
        (g_post, g_g_post, m_g_post, v_g_post),
    ])
    (d_b_ada, nm_b_ada, nv_b_ada), (d_g_pre, nm_g_pre, nv_g_pre), conv_upd, pool_upd, \
        (d_ps, nm_ps, nv_ps), (d_g_post, nm_g_post, nv_g_post) = small
    d_w_conv, nm_w_conv, nv_w_conv = (a.reshape(w_conv.shape) for a in conv_upd)
    d_w_pool, nm_w_pool, nv_w_pool = (a.reshape(w_pool.shape) for a in pool_upd)

    return (loss, grad_x,
            g_w_ada, g_b_ada, g_g_pre, g_w_in, g_w_conv, g_w_pool, g_pool_scale, g_w_out, g_g_post,
            d_w_ada, d_b_ada, d_g_pre, d_w_in, d_w_conv, d_w_pool, d_ps, d_w_out, d_g_post,
            nm_w_ada, nm_b_ada, nm_g_pre, nm_w_in, nm_w_conv, nm_w_pool, nm_ps, nm_w_out, nm_g_post,
            nv_w_ada, nv_b_ada, nv_g_pre, nv_w_in, nv_w_conv, nv_w_pool, nv_ps, nv_w_out, nv_g_post)
```

```python
from functools import partial

import jax
import jax.numpy as jnp
from jax import lax
from jax.experimental import pallas as pl
from jax.experimental.pallas import tpu as pltpu

F32 = jnp.float32
BF16 = jnp.bfloat16

D_MODEL = 1024
DEPTH = 4
CONV_W = 512
POOL_W = 512
POOL_WINDOWS = (2, 4, 8, 16)
GROUP_D = 128
IN_COLS = 4 * CONV_W + 2 * POOL_W
NORM_EPS = 1e-6

ADAM_LR = 0.001
ADAM_B1 = 0.9
ADAM_B2 = 0.999
ADAM_EPS = 1e-08
ADAM_WD = 0.01
ADAM_STEP = 10

N_DEV = 8
N_CHIP = 4
N_OTHER_CHIPS = N_CHIP - 1
MESH = pl.DeviceIdType.MESH
W_IN_SHARD = IN_COLS // N_DEV
W_OUT_SHARD = D_MODEL // N_DEV
POOL_ROWS = len(POOL_WINDOWS) * GROUP_D
POOL_SHARD = POOL_ROWS // N_DEV

SUBLANES = 8
LANES = 128
VMEM_LIMIT_BYTES = 56 * 1024 * 1024
ROW_TILE = 512
BWD_TILE = 256
GWIN_COLS = 768
GWOUT_COLS = 512
POOL_HALO = 16
CONV_HALO = SUBLANES

SLAB_COLS = 3 * D_MODEL + D_MODEL + D_MODEL + POOL_W + 3 * CONV_W

HBM = pl.BlockSpec(memory_space=pl.ANY)


def _params(**kw):
    return pltpu.CompilerParams(vmem_limit_bytes=VMEM_LIMIT_BYTES, **kw)


def _sigmoid(v):
    return 1.0 / (1.0 + jnp.exp(-v))


def _dot(a, b):
    return jnp.dot(a, b, preferred_element_type=F32)


def _dot_tn(a, b):
    return lax.dot_general(a, b, (((0,), (0,)), ((), ())), preferred_element_type=F32)


def _dot_nt(a, b):
    return lax.dot_general(a, b, (((1,), (1,)), ((), ())), preferred_element_type=F32)


def _rows_from_before(v, k):
    return pltpu.roll(v, k, 0)


def _rows_from_after(v, k):
    return pltpu.roll(v, v.shape[0] - k, 0)


def _window_counts(t0, rows):
    return (lax.broadcasted_iota(jnp.int32, (rows, 1), 0) + (t0 + 1)).astype(F32)


def _split_proj(p32):
    cw = CONV_W
    return (p32[:, 0 * cw:1 * cw], p32[:, 1 * cw:2 * cw], p32[:, 2 * cw:3 * cw], p32[:, 3 * cw:4 * cw],
            p32[:, 4 * cw:4 * cw + POOL_W], p32[:, 4 * cw + POOL_W:])


def _layer_spec(shape, layer):
    nd = len(shape)
    return pl.BlockSpec((None,) + tuple(shape[1:]), lambda i, _l=layer, _n=nd: (_l,) + (0,) * (_n - 1))


def _whole_spec(shape):
    return pl.BlockSpec(tuple(shape), lambda i, _n=len(shape): (0,) * _n, pipeline_mode=pl.Buffered(1))


def _mesh_position():
    return lax.axis_index("x"), lax.axis_index("y"), lax.axis_index("c")


def _block_id(x, y, c):
    return 4 * x + 2 * y + c


def _other_chips(x, y):
    return [(x ^ 1, y), (x, y ^ 1), (x ^ 1, y ^ 1)]


def _col_block(ref, blk):
    return ref.at[:, pl.ds(pl.multiple_of(blk * W_IN_SHARD, LANES), W_IN_SHARD)]


def _row_block(rows):
    def block(ref, blk):
        return ref.at[pl.ds(pl.multiple_of(blk * rows, rows), rows), :]
    return block


_BLOCK_OF = (_col_block, _row_block(W_OUT_SHARD), _row_block(POOL_SHARD))
_BLOCK_SHAPES = ((D_MODEL, W_IN_SHARD), (W_OUT_SHARD, D_MODEL), (POOL_SHARD, GROUP_D))


class _Exchange:
    def __init__(self, inputs, out_shapes, aliases, sem_shapes, make):
        self.inputs, self.out_shapes, self.aliases, self.sem_shapes, self.make = (
            list(inputs), list(out_shapes), dict(aliases), list(sem_shapes), make)


def _run_exchange(exchange, name):
    n_in, n_out = len(exchange.inputs), len(exchange.out_shapes)

    def body(*refs):
        start, finish = exchange.make(refs[:n_in], refs[n_in:n_in + n_out], refs[n_in + n_out:])
        start()
        finish()

    return pl.pallas_call(
        body, name=name, in_specs=[HBM] * n_in, out_specs=[HBM] * n_out, out_shape=exchange.out_shapes,
        scratch_shapes=exchange.sem_shapes, input_output_aliases=exchange.aliases, compiler_params=_params(),
    )(*exchange.inputs)


_SEM = pl.BlockSpec(memory_space=pltpu.SEMAPHORE)
_DATAFLOW = pltpu.SideEffectType.DATAFLOW_SIDE_EFFECTING


def _start_exchange(exchange, name, after=()):
    n_in, n_out, n_sem = len(exchange.inputs), len(exchange.out_shapes), len(exchange.sem_shapes)
    sources = [i for i in range(n_in) if i not in exchange.aliases]
    aliases = {i: n_sem + k for k, i in enumerate(sources)}
    aliases.update({i: n_sem + len(sources) + o for i, o in exchange.aliases.items()})

    def body(*refs):
        in_refs = refs[:n_in]
        outs = refs[n_in + len(after):]
        sems = outs[:n_sem]
        out_refs = outs[n_sem + len(sources):n_sem + len(sources) + n_out]
        exchange.make(in_refs, out_refs, sems)[0]()
        refs[-1][...] = jnp.zeros_like(refs[-1])

    outs = pl.pallas_call(
        body, name=name, in_specs=[HBM] * (n_in + len(after)),
        out_specs=[_SEM] * n_sem + [HBM] * (len(sources) + n_out) + [pl.BlockSpec(memory_space=pltpu.VMEM)],
        out_shape=(exchange.sem_shapes + [pltpu.HBM(exchange.inputs[i].shape, exchange.inputs[i].dtype) for i in sources]
                   + [pltpu.HBM(s.shape, s.dtype) for s in exchange.out_shapes]
                   + [jax.ShapeDtypeStruct((SUBLANES, LANES), F32)]),
        input_output_aliases=aliases, compiler_params=_params(has_side_effects=_DATAFLOW),
    )(*exchange.inputs, *after)
    return outs[:n_sem], outs[n_sem:n_sem + len(sources)], outs[n_sem + len(sources):-1], outs[-1]


def _finish_exchange(exchange, name, sems, sources, landing, after):
    n_src, n_out, n_sem = len(sources), len(landing), len(sems)
    n_in = len(exchange.inputs)
    source_at = [i for i in range(n_in) if i not in exchange.aliases]

    def body(*refs):
        src_refs, out_refs = refs[:n_src], refs[n_src:n_src + n_out]
        sem_refs = refs[n_src + n_out:n_src + n_out + n_sem]
        in_refs = [None] * n_in
        for k, i in enumerate(source_at):
            in_refs[i] = src_refs[k]
        for i, o in exchange.aliases.items():
            in_refs[i] = out_refs[o]
        exchange.make(in_refs, out_refs, sem_refs)[1]()

    arrays = list(sources) + list(landing)
    outs = pl.pallas_call(
        body, name=name, in_specs=[HBM] * len(arrays) + [_SEM] * n_sem + [HBM] * len(after),
        out_specs=[HBM] * len(arrays), out_shape=[pltpu.HBM(a.shape, a.dtype) for a in arrays],
        input_output_aliases={i: i for i in range(len(arrays))}, compiler_params=_params(has_side_effects=_DATAFLOW),
    )(*arrays, *sems, *after)
    return outs[:n_src], outs[n_src:]


N_GATHERED = 2
FIRST_COPIES = 1 + N_OTHER_CHIPS
_GATHERED_SHAPES = ((D_MODEL, IN_COLS), (D_MODEL, D_MODEL))
ALL_OTHER_CHIPS, NEIGHBOUR_CHIPS, DIAGONAL_CHIP = (0, 1, 2), (0, 1), (2,)


def _first_sem(layer, a, k):
    return (layer * N_GATHERED + a) * FIRST_COPIES + k


def _first_arrival(layer, a, j):
    return _first_sem(layer, a, 1 + j)


def _gather_copy(window_of, full_ref, blk, send_sem, recv_sem, to, src=None):
    window = window_of(full_ref, blk)
    return pltpu.make_async_remote_copy(
        src_ref=window if src is None else src, dst_ref=window, send_sem=send_sem, recv_sem=recv_sem,
        device_id=to, device_id_type=MESH)


def _first_copies(layer, shard_refs, full_refs, send_sems, recv_sems, local_sems, relayed):
    x, y, c = _mesh_position()
    me = _block_id(x, y, c)
    chips = [_other_chips(x, y)[j] for j in (NEIGHBOUR_CHIPS if relayed else ALL_OTHER_CHIPS)]
    own, remote = [], []
    for a in range(N_GATHERED):
        shard = shard_refs[a].at[layer]
        own.append(pltpu.make_async_copy(
            shard, _BLOCK_OF[a](full_refs[a], me), local_sems.at[layer * N_GATHERED + a]))
        targets = [(x, y, 1 - c)] + [(*chip, c) for chip in chips]
        remote += [_gather_copy(_BLOCK_OF[a], full_refs[a], me, send_sems.at[_first_sem(layer, a, k)],
                                recv_sems.at[_first_sem(layer, a, k)], to, src=shard)
                   for k, to in enumerate(targets)]
    return own, remote


def _gather_weights_start(first_layer, win_shards, wout_shards, after, relayed=False):
    n_layers = win_shards.shape[0]
    n_first = n_layers * N_GATHERED * FIRST_COPIES
    sem_shapes = [pltpu.SemaphoreType.DMA((n_first,)), pltpu.SemaphoreType.DMA((n_first,)),
                  pltpu.SemaphoreType.DMA((n_layers * N_GATHERED,))]
    shards = [win_shards, wout_shards]

    def body(win_sh, wout_sh, *rest):
        send_sems, recv_sems, local_sems, win_thru, wout_thru, *landing = rest[len(after):]
        for layer in range(n_layers):
            own, remote = _first_copies(layer, (win_sh, wout_sh), landing[N_GATHERED * layer:N_GATHERED * (layer + 1)],
                                        send_sems, recv_sems, local_sems, relayed)
            for cp in own + remote:
                cp.start()

    outs = pl.pallas_call(
        body, name=f"all_gather_weights_start_{first_layer}", in_specs=[HBM] * (2 + len(after)),
        out_specs=[_SEM] * 3 + [HBM] * (2 + n_layers * N_GATHERED),
        out_shape=(sem_shapes + [pltpu.HBM(s.shape, s.dtype) for s in shards]
                   + [pltpu.HBM(s, BF16) for _ in range(n_layers) for s in _GATHERED_SHAPES]),
        input_output_aliases={0: 3, 1: 4}, compiler_params=_params(has_side_effects=_DATAFLOW),
    )(*shards, *after)
    landing = outs[5:]
    return outs[:3], outs[3:5], [landing[N_GATHERED * l:N_GATHERED * (l + 1)] for l in range(n_layers)]


def _passed_on_copies(full_refs, send_sems, recv_sems, core_of_block, chips):
    x, y, c = _mesh_position()
    return [_gather_copy(_BLOCK_OF[a], full_refs[a], _block_id(*_other_chips(x, y)[j], core_of_block),
                         send_sems.at[a * N_OTHER_CHIPS + j], recv_sems.at[a * N_OTHER_CHIPS + j], (x, y, 1 - c))
            for a in range(N_GATHERED) for j in chips]


def _relayed_copies(full_refs, send_sems, recv_sems):
    x, y, c = _mesh_position()
    source, to = (x ^ (1 - c), y ^ c), (x ^ c, y ^ (1 - c))
    return [_gather_copy(_BLOCK_OF[a], full_refs[a], _block_id(*source, c), send_sems.at[a], recv_sems.at[a], (*to, c))
            for a in range(N_GATHERED)]


def _gather_weights_pass_on(name, chips, arrival_sems, arrival_sem_of, landing, after, relay=False):
    n = N_GATHERED * N_OTHER_CHIPS
    sem_shapes = [pltpu.SemaphoreType.DMA((n,))] * 2 + [pltpu.SemaphoreType.DMA((N_GATHERED,))] * (2 if relay else 0)

    def body(win_ref, wout_ref, arrivals, *rest):
        sems = rest[len(after):len(after) + len(sem_shapes)]
        x, y, c = _mesh_position()
        full_refs = (win_ref, wout_ref)
        passed = _passed_on_copies(full_refs, sems[0], sems[1], c, chips)
        relayed = _relayed_copies(full_refs, sems[2], sems[3]) if relay else []
        for a in range(N_GATHERED):
            for j in chips:
                sem = arrivals.at[arrival_sem_of(a, j)]
                _gather_copy(_BLOCK_OF[a], full_refs[a], _block_id(*_other_chips(x, y)[j], c), sem, sem,
                             (x, y, c)).wait_recv()
            for cp in relayed[a:a + 1] + passed[a * len(chips):(a + 1) * len(chips)]:
                cp.start()

    outs = pl.pallas_call(
        body, name=name, in_specs=[HBM] * N_GATHERED + [_SEM] + [HBM] * len(after),
        out_specs=[_SEM] * len(sem_shapes) + [HBM] * N_GATHERED,
        out_shape=sem_shapes + [pltpu.HBM(a.shape, a.dtype) for a in landing],
        input_output_aliases={a: len(sem_shapes) + a for a in range(N_GATHERED)},
        compiler_params=_params(has_side_effects=_DATAFLOW),
    )(*landing, arrival_sems, *after)
    return outs[:len(sem_shapes)], outs[len(sem_shapes):]


def _gather_weights_finish(layer, index, first_sems, passed, shards, landing, relay_send_sems=None):
    relayed = relay_send_sems is not None
    passed_sems = [sem for (send, recv), _ in passed for sem in (send, recv)] + ([relay_send_sems] if relayed else [])

    def body(win_ref, wout_ref, first_send, first_recv, local_sems, *rest):
        sems, (win_sh, wout_sh) = rest[:len(passed_sems)], rest[len(passed_sems):len(passed_sems) + 2]
        x, y, c = _mesh_position()
        full_refs = (win_ref, wout_ref)
        own, sent = _first_copies(index, (win_sh, wout_sh), full_refs, first_send, first_recv, local_sems, relayed)
        for a in range(N_GATHERED):
            sem = _first_sem(index, a, 0)
            _gather_copy(_BLOCK_OF[a], full_refs[a], _block_id(x, y, 1 - c), first_recv.at[sem], first_recv.at[sem],
                         (x, y, c)).wait_recv()
        for p, (_, chips) in enumerate(passed):
            for cp in _passed_on_copies(full_refs, sems[2 * p], sems[2 * p + 1], 1 - c, chips):
                cp.wait_recv()
            sent += _passed_on_copies(full_refs, sems[2 * p], sems[2 * p + 1], c, chips)
        if relayed:
            sent += _relayed_copies(full_refs, sems[-1], sems[-1])
        for cp in sent:
            cp.wait_send()
        for cp in own:
            cp.wait()

    return pl.pallas_call(
        body, name=f"all_gather_weights_finish_{layer}",
        in_specs=[HBM] * N_GATHERED + [_SEM] * (3 + len(passed_sems)) + [HBM] * 2,
        out_specs=[HBM] * N_GATHERED, out_shape=[pltpu.HBM(a.shape, a.dtype) for a in landing],
        input_output_aliases={a: a for a in range(N_GATHERED)}, compiler_params=_params(has_side_effects=_DATAFLOW),
    )(*landing, *first_sems, *passed_sems, *shards)


ALL_KINDS = (0, 1, 2)
_GRAD_BLOCK_OF = (lambda ref, blk: ref.at[blk],) + _BLOCK_OF[1:]


def _sibling_exchange(grads, kinds):
    n_arr = len(grads)

    def make(in_refs, out_refs, sems):
        send_sems, recv_sems = sems
        x, y, c = _mesh_position()
        copies = [pltpu.make_async_remote_copy(
            src_ref=_GRAD_BLOCK_OF[kinds[a]](in_refs[a], 2 * q + (1 - c)), dst_ref=out_refs[a].at[q],
            send_sem=send_sems.at[a * N_CHIP + q], recv_sem=recv_sems.at[a * N_CHIP + q],
            device_id=(x, y, 1 - c), device_id_type=MESH)
            for a in range(n_arr) for q in range(N_CHIP)]

        def start():
            for cp in copies:
                cp.start()

        def finish():
            for cp in copies:
                cp.wait_recv()
            for cp in copies:
                cp.wait_send()

        return start, finish

    return _Exchange(
        grads, [jax.ShapeDtypeStruct((N_CHIP,) + _BLOCK_SHAPES[k], BF16) for k in kinds], {},
        [pltpu.SemaphoreType.DMA((n_arr * N_CHIP,)), pltpu.SemaphoreType.DMA((n_arr * N_CHIP,))], make)


def _chips_exchange(layer, partials, received, kinds):
    n_arr = len(partials)

    def make(in_refs, out_refs, sems):
        send_sems, recv_sems = sems
        x, y, c = _mesh_position()
        copies = [pltpu.make_async_remote_copy(
            src_ref=in_refs[a].at[2 * qx + qy, layer], dst_ref=out_refs[a].at[j, layer],
            send_sem=send_sems.at[a * N_OTHER_CHIPS + j], recv_sem=recv_sems.at[a * N_OTHER_CHIPS + j],
            device_id=(qx, qy, c), device_id_type=MESH)
            for a in range(n_arr) for j, (qx, qy) in enumerate(_other_chips(x, y))]

        def start():
            for cp in copies:
                cp.start()

        def finish():
            for cp in copies:
                cp.wait_recv()
            for cp in copies:
                cp.wait_send()

        return start, finish

    inputs = list(partials)
    aliases = {}
    if received is not None:
        inputs += list(received)
        aliases = {n_arr + a: a for a in range(n_arr)}
    return _Exchange(
        inputs, [jax.ShapeDtypeStruct((N_OTHER_CHIPS, DEPTH) + _BLOCK_SHAPES[k], BF16) for k in kinds], aliases,
        [pltpu.SemaphoreType.DMA((n_arr * N_OTHER_CHIPS,)), pltpu.SemaphoreType.DMA((n_arr * N_OTHER_CHIPS,))], make)


def _all_gather_exchange(v):
    def make(in_refs, out_refs, sems):
        send_sems, recv_sems, local_sem = sems
        x, y, c = _mesh_position()
        me = _block_id(x, y, c)
        own = pltpu.make_async_copy(in_refs[0], out_refs[0].at[me], local_sem.at[0])
        sends, arrivals = [], []
        for k in range(1, N_DEV):
            px, py, pc = x ^ ((k >> 2) & 1), y ^ ((k >> 1) & 1), c ^ (k & 1)
            sends.append(pltpu.make_async_remote_copy(
                src_ref=in_refs[0], dst_ref=out_refs[0].at[me], send_sem=send_sems.at[k - 1],
                recv_sem=recv_sems.at[k - 1], device_id=(px, py, pc), device_id_type=MESH))
            arrivals.append(pltpu.make_async_remote_copy(
                src_ref=in_refs[0], dst_ref=out_refs[0].at[_block_id(px, py, pc)], send_sem=send_sems.at[k - 1],
                recv_sem=recv_sems.at[k - 1], device_id=(x, y, c), device_id_type=MESH))

        def start():
            for cp in [own] + sends:
                cp.start()

        def finish():
            for cp in arrivals:
                cp.wait_recv()
            for cp in sends:
                cp.wait_send()
            own.wait()

        return start, finish

    return _Exchange(
        [v], [jax.ShapeDtypeStruct((N_DEV,) + v.shape, v.dtype)], {},
        [pltpu.SemaphoreType.DMA((N_DEV - 1,)), pltpu.SemaphoreType.DMA((N_DEV - 1,)),
         pltpu.SemaphoreType.DMA((1,))], make)


def _host(exchange, args, in_specs, out_shape, out_specs, scratch):
    n_own = (len(args), len(out_shape), len(scratch))
    if exchange is None:
        return {}, lambda refs: (refs, None)
    n_ex = (len(exchange.inputs), len(exchange.out_shapes), len(exchange.sem_shapes))
    aliases = {n_own[0] + i: n_own[1] + o for i, o in exchange.aliases.items()}
    args += exchange.inputs
    in_specs += [HBM] * n_ex[0]
    out_shape += exchange.out_shapes
    out_specs += [HBM] * n_ex[1]
    scratch += exchange.sem_shapes

    def split(refs):
        own, theirs, at = [], [], 0
        for mine, ex in zip(n_own, n_ex):
            own += refs[at:at + mine]
            theirs.append(refs[at + mine:at + mine + ex])
            at += mine + ex
        return own, exchange.make(*theirs)

    return aliases, split


def _all_gather_small(v, name, after=()):
    vmem = pl.BlockSpec(memory_space=pltpu.VMEM)

    def body(v_ref, *rest):
        out_ref, send_sems, recv_sems = rest[len(after):]
        x, y, c = _mesh_position()
        me = _block_id(x, y, c)
        out_ref[me] = v_ref[...]
        sends = []
        for k in range(1, N_DEV):
            px, py, pc = x ^ ((k >> 2) & 1), y ^ ((k >> 1) & 1), c ^ (k & 1)
            send = pltpu.make_async_remote_copy(
                src_ref=v_ref, dst_ref=out_ref.at[me], send_sem=send_sems.at[k - 1], recv_sem=recv_sems.at[k - 1],
                device_id=(px, py, pc), device_id_type=MESH)
            send.start()
            sends.append((send, _block_id(px, py, pc)))
        for k, (send, peer) in enumerate(sends):
            pltpu.make_async_remote_copy(
                src_ref=v_ref, dst_ref=out_ref.at[peer], send_sem=send_sems.at[k], recv_sem=recv_sems.at[k],
                device_id=(x, y, c), device_id_type=MESH).wait_recv()
        for send, _ in sends:
            send.wait_send()

    return pl.pallas_call(
        body, name=name, in_specs=[vmem] + [HBM] * len(after), out_specs=vmem,
        out_shape=jax.ShapeDtypeStruct((N_DEV,) + v.shape, v.dtype),
        scratch_shapes=[pltpu.SemaphoreType.DMA((N_DEV - 1,)), pltpu.SemaphoreType.DMA((N_DEV - 1,))],
        compiler_params=_params(),
    )(v, *after)


def _forward_layer(layer, x, vec, cps, wpool, win, wout, target=None):
    t_len = x.shape[0]
    n_tiles = t_len // ROW_TILE
    row = lambda cols: pl.BlockSpec((ROW_TILE, cols), lambda i: (i, 0))
    widths = (D_MODEL, 2 * CONV_W, 3 * CONV_W, 4 * POOL_W)
    head = target is not None

    def body(x_ref, vec_ref, cps_ref, wpool_ref, win_ref, wout_ref, *rest):
        target_ref = rest[0] if head else None
        xo_ref, y_ref, h_ref, ycat_ref, uc_ref, fa_ref, fp_ref = rest[head:head + 7]
        loss_ref = rest[head + 7] if head else None
        zc_ref, pc_ref = rest[-2:]
        i = pl.program_id(0)

        @pl.when(i == 0)
        def _():
            zc_ref[...] = jnp.zeros_like(zc_ref)
            pc_ref[...] = jnp.zeros_like(pc_ref)
            if head:
                loss_ref[...] = jnp.zeros_like(loss_ref)

        x_t = x_ref[...]
        shift, scale, gate = vec_ref[0:1, :], vec_ref[1:2, :], vec_ref[2:3, :]
        g_pre, g_post = vec_ref[3:4, :], vec_ref[4:5, :]
        w0, w1, w2, ps = cps_ref[0:1, :], cps_ref[1:2, :], cps_ref[2:3, :], cps_ref[3:4, :]
        rx = lax.rsqrt(jnp.mean(x_t * x_t, axis=-1, keepdims=True) + NORM_EPS)
        h = (x_t * rx) * g_pre * (1.0 + scale) + shift
        h_ref[...] = h.astype(BF16)
        proj = _dot(h.astype(BF16), win_ref[...])
        u_a, b_a, c_a, g_a, u_p, g_p = _split_proj(proj)
        uc_ref[...] = jnp.concatenate([u_a, c_a], axis=1).astype(BF16)

        z = c_a * u_a
        zcat = jnp.concatenate([zc_ref[...], z], axis=0)
        zc_ref[...] = z[ROW_TILE - CONV_HALO:]
        conv = (w0 * _rows_from_before(zcat, 2)[CONV_HALO:] + w1 * _rows_from_before(zcat, 1)[CONV_HALO:] + w2 * z)
        sig_a = _sigmoid(g_a)
        silu_a = g_a * sig_a
        b_conv = b_a * conv
        y_a = b_conv * silu_a
        fa_ref[...] = jnp.concatenate(
            [silu_a * conv, silu_a * b_a, b_conv * (sig_a + silu_a * (1.0 - sig_a))], axis=1).astype(BF16)

        pcat = jnp.concatenate([pc_ref[...], u_p], axis=0)
        pc_ref[...] = u_p[ROW_TILE - POOL_HALO:]
        counts = _window_counts(i * ROW_TILE, ROW_TILE)
        pooled, mixed = [], []
        for g, w in enumerate(POOL_WINDOWS):
            cols = slice(g * GROUP_D, (g + 1) * GROUP_D)
            s = pcat[:, cols]
            step = 1
            while step < w:
                s = s + _rows_from_before(s, step)
                step *= 2
            pooled_g = (s[POOL_HALO:] * (1.0 / jnp.minimum(counts, float(w))) - u_p[:, cols]).astype(BF16)
            pooled.append(pooled_g)
            mixed.append(_dot(pooled_g, wpool_ref[g]))
        mixed = jnp.concatenate(mixed, axis=1)
        sig_p = _sigmoid(g_p)
        silu_p = g_p * sig_p
        mixed_ps = mixed * ps
        y_p = mixed_ps * silu_p
        fp_ref[...] = jnp.concatenate(
            [(ps * silu_p).astype(BF16), (mixed_ps * (sig_p + silu_p * (1.0 - sig_p))).astype(BF16),
             (silu_p * mixed).astype(BF16)] + pooled, axis=1)

        ycat = jnp.concatenate([y_a, y_p], axis=1)
        ycat_ref[...] = ycat.astype(BF16)
        y_b = _dot(ycat.astype(BF16), wout_ref[...]).astype(BF16)
        y_ref[...] = y_b
        y_t = y_b.astype(F32)
        ry = lax.rsqrt(jnp.mean(y_t * y_t, axis=-1, keepdims=True) + NORM_EPS)
        x_next = x_t + gate * (y_t * ry * g_post)
        if head:
            err = x_next - target_ref[...]
            xo_ref[...] = err * (1.0 / D_MODEL)
            loss_ref[...] += jnp.sum(err * err) * (0.5 / D_MODEL)
        else:
            xo_ref[...] = x_next

    tile = (SUBLANES, LANES)
    return pl.pallas_call(
        body, name=f"forward_layer_{layer}", grid=(n_tiles,),
        in_specs=[row(D_MODEL), _layer_spec(vec.shape, layer), _layer_spec(cps.shape, layer),
                  _layer_spec(wpool.shape, layer), _whole_spec(win.shape), _whole_spec(wout.shape)]
        + [row(D_MODEL)] * head,
        out_specs=[row(D_MODEL), row(D_MODEL), row(D_MODEL), row(D_MODEL)] + [row(w) for w in widths[1:]]
        + [_whole_spec(tile)] * head,
        out_shape=[jax.ShapeDtypeStruct((t_len, D_MODEL), F32), jax.ShapeDtypeStruct((t_len, D_MODEL), BF16),
                   jax.ShapeDtypeStruct((t_len, D_MODEL), BF16), jax.ShapeDtypeStruct((t_len, D_MODEL), BF16)]
        + [jax.ShapeDtypeStruct((t_len, w), BF16) for w in widths[1:]] + [jax.ShapeDtypeStruct(tile, F32)] * head,
        scratch_shapes=[pltpu.VMEM((CONV_HALO, CONV_W), F32), pltpu.VMEM((POOL_HALO, POOL_W), F32)],
        compiler_params=_params(dimension_semantics=("arbitrary",)),
    )(x, vec, cps, wpool, win, wout, *([target] * head))


def _backward_layer(layer, dxo, y, x, uc, fa, fp, vec, cps, wpool, wout, win):
    t_len = dxo.shape[0]
    n_tiles = t_len // BWD_TILE
    halo_per_tile = BWD_TILE // POOL_HALO
    rev = lambda cols: pl.BlockSpec((BWD_TILE, cols), lambda i: (n_tiles - 1 - i, 0))
    halo_spec = pl.BlockSpec(
        (POOL_HALO, 2 * CONV_W), lambda i: (jnp.maximum((n_tiles - 1 - i) * halo_per_tile - 1, 0), 0))
    gwpool_shape = (len(POOL_WINDOWS), GROUP_D, GROUP_D)

    def body(dxo_ref, y_ref, x_ref, uc_ref, uch_ref, fa_ref, fp_ref, vec_ref, cps_ref, wpool_ref, wout_ref, win_ref,
             dx_ref, dproj_ref, dy_ref, gwpool_ref, dcps_ref, dvec_ref, gwpool_acc, dcc_ref, qc_ref):
        i = pl.program_id(0)
        tile = n_tiles - 1 - i

        @pl.when(i == 0)
        def _():
            gwpool_acc[...] = jnp.zeros_like(gwpool_acc)
            dcps_ref[...] = jnp.zeros_like(dcps_ref)
            dvec_ref[...] = jnp.zeros_like(dvec_ref)
            dcc_ref[...] = jnp.zeros_like(dcc_ref)
            qc_ref[...] = jnp.zeros_like(qc_ref)

        shift, scale, gate = vec_ref[0:1, :], vec_ref[1:2, :], vec_ref[2:3, :]
        g_pre, g_post = vec_ref[3:4, :], vec_ref[4:5, :]
        w0, w1, w2 = cps_ref[0:1, :], cps_ref[1:2, :], cps_ref[2:3, :]

        dxo_t = dxo_ref[...]
        y_t = y_ref[...].astype(F32)
        ry = lax.rsqrt(jnp.mean(y_t * y_t, axis=-1, keepdims=True) + NORM_EPS)
        yh = y_t * ry
        dvec_ref[2:3, :] += jnp.sum(dxo_t * yh, axis=0, keepdims=True)
        dyh = dxo_t * (gate * g_post)
        dy_b = (ry * (dyh - yh * jnp.mean(dyh * yh, axis=-1, keepdims=True))).astype(BF16)
        dy_ref[...] = dy_b
        dycat = _dot_nt(dy_b, wout_ref[...])
        dy_a, dy_p = dycat[:, :CONV_W], dycat[:, CONV_W:]

        fa_t = fa_ref[...].astype(F32)
        db_a = dy_a * fa_t[:, :CONV_W]
        dconv = dy_a * fa_t[:, CONV_W:2 * CONV_W]
        dg_a = dy_a * fa_t[:, 2 * CONV_W:]
        uc_t = uc_ref[...].astype(F32)
        u_a, c_a = uc_t[:, :CONV_W], uc_t[:, CONV_W:]
        halo = jnp.where(tile > 0, uch_ref[...].astype(F32), 0.0)[POOL_HALO - CONV_HALO:]
        z = c_a * u_a
        zcat = jnp.concatenate([halo[:, CONV_W:] * halo[:, :CONV_W], z], axis=0)
        z1 = _rows_from_before(zcat, 1)[CONV_HALO:]
        z2 = _rows_from_before(zcat, 2)[CONV_HALO:]
        dccat = jnp.concatenate([dconv, dcc_ref[...]], axis=0)
        dc1 = _rows_from_after(dccat, 1)[:BWD_TILE]
        dc2 = _rows_from_after(dccat, 2)[:BWD_TILE]
        dz = w2 * dconv + w1 * dc1 + w0 * dc2
        dcc_ref[...] = dconv[:CONV_HALO]
        dcps_ref[0:1, :] += jnp.sum(dconv * z2, axis=0, keepdims=True)
        dcps_ref[1:2, :] += jnp.sum(dconv * z1, axis=0, keepdims=True)
        dcps_ref[2:3, :] += jnp.sum(dconv * z, axis=0, keepdims=True)
        du_a = dz * c_a
        dc_a = dz * u_a

        dmixed = (dy_p * fp_ref[:, :POOL_W].astype(F32)).astype(BF16)
        dg_p = dy_p * fp_ref[:, POOL_W:2 * POOL_W].astype(F32)
        dcps_ref[3:4, :] += jnp.sum(dy_p * fp_ref[:, 2 * POOL_W:3 * POOL_W].astype(F32), axis=0, keepdims=True)
        counts = _window_counts(tile * BWD_TILE, BWD_TILE)
        du_p, q_head = [], []
        for g, w in enumerate(POOL_WINDOWS):
            cols = slice(g * GROUP_D, (g + 1) * GROUP_D)
            dm_g = dmixed[:, cols]
            dpooled_g = _dot_nt(dm_g, wpool_ref[g])
            gwpool_acc[g] += _dot_tn(fp_ref[:, 3 * POOL_W + g * GROUP_D:3 * POOL_W + (g + 1) * GROUP_D], dm_g)
            q_g = dpooled_g * (1.0 / jnp.minimum(counts, float(w)))
            q_head.append(q_g[:POOL_HALO])
            s = jnp.concatenate([q_g, qc_ref[:, cols]], axis=0)
            step = 1
            while step < w:
                s = s + _rows_from_after(s, step)
                step *= 2
            du_p.append(s[:BWD_TILE] - dpooled_g)
        qc_ref[...] = jnp.concatenate(q_head, axis=1)
        dproj_b = jnp.concatenate([du_a, db_a, dc_a, dg_a] + du_p + [dg_p], axis=1).astype(BF16)
        dproj_ref[...] = dproj_b

        x_t = x_ref[...]
        rx = lax.rsqrt(jnp.mean(x_t * x_t, axis=-1, keepdims=True) + NORM_EPS)
        xn = x_t * rx
        mod_scale = 1.0 + scale
        dh = _dot_nt(dproj_b, win_ref[...])
        dvec_ref[0:1, :] += jnp.sum(dh, axis=0, keepdims=True)
        dvec_ref[1:2, :] += jnp.sum(dh * xn, axis=0, keepdims=True)
        dxn = dh * (g_pre * mod_scale)
        dx_ref[...] = dxo_t + rx * (dxn - xn * jnp.mean(dxn * xn, axis=-1, keepdims=True))

        @pl.when(i == n_tiles - 1)
        def _():
            gwpool_ref[...] = gwpool_acc[...].astype(BF16)
            sum_dh_xn, sum_dxo_yh = dvec_ref[1:2, :], dvec_ref[2:3, :]
            dvec_ref[1:2, :] = sum_dh_xn * g_pre
            dvec_ref[3:4, :] = sum_dh_xn * mod_scale
            dvec_ref[2:3, :] = sum_dxo_yh * g_post
            dvec_ref[4:5, :] = sum_dxo_yh * gate

    return pl.pallas_call(
        body, name=f"backward_layer_{layer}", grid=(n_tiles,),
        in_specs=[rev(D_MODEL), rev(D_MODEL), rev(D_MODEL), rev(2 * CONV_W), halo_spec, rev(3 * CONV_W),
                  rev(4 * POOL_W), _layer_spec(vec.shape, layer), _layer_spec(cps.shape, layer),
                  _layer_spec(wpool.shape, layer), _whole_spec(wout.shape), _whole_spec(win.shape)],
        out_specs=[rev(D_MODEL), rev(IN_COLS), rev(D_MODEL), _whole_spec(gwpool_shape),
                   _whole_spec((SUBLANES, CONV_W)), _whole_spec((SUBLANES, D_MODEL))],
        out_shape=[jax.ShapeDtypeStruct((t_len, D_MODEL), F32), jax.ShapeDtypeStruct((t_len, IN_COLS), BF16),
                   jax.ShapeDtypeStruct((t_len, D_MODEL), BF16), jax.ShapeDtypeStruct(gwpool_shape, BF16),
                   jax.ShapeDtypeStruct((SUBLANES, CONV_W), F32), jax.ShapeDtypeStruct((SUBLANES, D_MODEL), F32)],
        scratch_shapes=[pltpu.VMEM(gwpool_shape, F32), pltpu.VMEM((CONV_HALO, CONV_W), F32),
                        pltpu.VMEM((POOL_HALO, POOL_W), F32)],
        compiler_params=_params(dimension_semantics=("arbitrary",)),
    )(dxo, y, x, uc, uc, fa, fp, vec, cps, wpool, wout, win)


def _weight_grads(name, products, exchange, after=()):
    steps = [b.shape[1] // band for _, b, band, _ in products]
    first = [sum(steps[:i]) for i in range(len(products))]
    args, in_specs, out_shape, out_specs, scratch = [], [], [], [], []
    for (a, b, band, block), n, s0 in zip(products, steps, first):
        band_of_step = lambda s, n=n, s0=s0: jnp.clip(s - s0, 0, n - 1)
        args += [a, b]
        in_specs += [_whole_spec(a.shape), pl.BlockSpec((b.shape[0], band), lambda s, at=band_of_step: (0, at(s)))]
        if block is None:
            out_shape.append(jax.ShapeDtypeStruct((a.shape[1], b.shape[1]), BF16))
            out_specs.append(pl.BlockSpec((a.shape[1], band), lambda s, at=band_of_step: (0, at(s))))
        else:
            out_shape.append(jax.ShapeDtypeStruct((b.shape[1] // block, a.shape[1], block), BF16))
            out_specs.append(pl.BlockSpec((band // block, a.shape[1], block), lambda s, at=band_of_step: (at(s), 0, 0)))
    args += after
    in_specs += [HBM] * len(after)
    aliases, split = _host(exchange, args, in_specs, out_shape, out_specs, scratch)

    def body(*refs):
        own, hosted = split(refs)
        in_refs, out_refs = own[:2 * len(products)], own[-len(products):]
        s = pl.program_id(0)
        if hosted is not None:
            pl.when(s == 0)(hosted[0])
        for i, (n, s0) in enumerate(zip(steps, first)):
            @pl.when((s >= s0) & (s < s0 + n))
            def _(i=i):
                band, block = products[i][2:]
                product = _dot_tn(in_refs[2 * i][...], in_refs[2 * i + 1][...]).astype(BF16)
                if block is None:
                    out_refs[i][...] = product
                else:
                    for k in range(band // block):
                        out_refs[i][k] = product[:, k * block:(k + 1) * block]

        if hosted is not None:
            pl.when(s == sum(steps) - 1)(hosted[1])

    return pl.pallas_call(
        body, name=name, grid=(sum(steps),), in_specs=in_specs, out_specs=out_specs,
        out_shape=out_shape, scratch_shapes=scratch, input_output_aliases=aliases,
        compiler_params=_params(dimension_semantics=("arbitrary",)),
    )(*args)


def _add_sibling_blocks(name, layer, grads, received, core, partials, kinds):
    n_arr = len(grads)

    def body(core_ref, *refs):
        mine, theirs, outs = refs[:n_arr], refs[n_arr:2 * n_arr], refs[-n_arr:]
        for a in range(n_arr):
            outs[a][...] = (mine[a][...].astype(F32) + theirs[a][...].astype(F32)).astype(BF16)

    own_of_kind = [
        pl.BlockSpec((None, D_MODEL, W_IN_SHARD), lambda q, core_ref: (2 * q + core_ref[0], 0, 0)),
        pl.BlockSpec((W_OUT_SHARD, D_MODEL), lambda q, core_ref: (2 * q + core_ref[0], 0)),
        pl.BlockSpec((POOL_SHARD, GROUP_D), lambda q, core_ref: (2 * q + core_ref[0], 0)),
    ]
    shapes = [_BLOCK_SHAPES[k] for k in kinds]
    recv_specs = [pl.BlockSpec((None,) + s, lambda q, core_ref: (q, 0, 0)) for s in shapes]
    out_specs = [pl.BlockSpec((None, None) + s, lambda q, core_ref: (q, layer, 0, 0)) for s in shapes]
    args = [core, *grads, *received]
    in_specs = [own_of_kind[k] for k in kinds] + recv_specs
    aliases = {}
    if partials is not None:
        aliases = {len(args) + a: a for a in range(n_arr)}
        args += list(partials)
        in_specs += [HBM] * n_arr
    return pl.pallas_call(
        body, name=name,
        grid_spec=pltpu.PrefetchScalarGridSpec(
            num_scalar_prefetch=1, grid=(N_CHIP,), in_specs=in_specs, out_specs=out_specs),
        out_shape=[jax.ShapeDtypeStruct((N_CHIP, DEPTH) + s, BF16) for s in shapes],
        input_output_aliases=aliases,
        compiler_params=_params(dimension_semantics=("arbitrary",)),
    )(*args)


def _modulation_columns(c_all, w_ada):
    def body(c_ref, w_ref, cact_ref, out_ref):
        c_t = c_ref[...]
        c_act = c_t * _sigmoid(c_t)
        cact_ref[...] = c_act
        out_ref[...] = jnp.dot(c_act, w_ref[...], preferred_element_type=F32, precision=lax.Precision.HIGHEST)

    return pl.pallas_call(
        body, name="modulation_columns", grid=(DEPTH,),
        in_specs=[pl.BlockSpec((N_DEV, D_MODEL), lambda l: (0, 0)),
                  pl.BlockSpec((None, D_MODEL, W_IN_SHARD), lambda l: (l, 0, 0))],
        out_specs=[pl.BlockSpec((N_DEV, D_MODEL), lambda l: (0, 0)),
                   pl.BlockSpec((N_DEV, W_IN_SHARD), lambda l: (0, l))],
        out_shape=[jax.ShapeDtypeStruct((N_DEV, D_MODEL), F32),
                   jax.ShapeDtypeStruct((N_DEV, DEPTH * W_IN_SHARD), F32)],
        compiler_params=_params(dimension_semantics=("arbitrary",)),
    )(c_all, w_ada)


def _adamw(w, g, m, v):
    m_new = ADAM_B1 * m + (1.0 - ADAM_B1) * g
    v_new = ADAM_B2 * v + (1.0 - ADAM_B2) * (g * g)
    m_hat = m_new / (1.0 - ADAM_B1 ** ADAM_STEP)
    v_hat = v_new / (1.0 - ADAM_B2 ** ADAM_STEP)
    delta = -ADAM_LR * (m_hat / (jnp.sqrt(v_hat) + ADAM_EPS) + ADAM_WD * w)
    return delta, m_new, v_new


def _adamw_w_ada(w, m, v, c_act_t, dmod_cols):
    def body(w_ref, m_ref, v_ref, ct_ref, dm_ref, g_ref, d_ref, mo_ref, vo_ref):
        g = ct_ref[:, 0:1] * dm_ref[0:1, :]
        for b in range(1, N_DEV):
            g = g + ct_ref[:, b:b + 1] * dm_ref[b:b + 1, :]
        g_ref[...] = g
        d_ref[...], mo_ref[...], vo_ref[...] = _adamw(w_ref[...], g, m_ref[...], v_ref[...])

    big = pl.BlockSpec((None, D_MODEL, W_IN_SHARD), lambda l: (l, 0, 0))
    return pl.pallas_call(
        body, name="adamw_w_ada", grid=(DEPTH,),
        in_specs=[big, big, big, pl.BlockSpec((D_MODEL, N_DEV), lambda l: (0, 0)),
                  pl.BlockSpec((None, N_DEV, W_IN_SHARD), lambda l: (l, 0, 0))],
        out_specs=[big] * 4, out_shape=[jax.ShapeDtypeStruct(w.shape, F32)] * 4,
        compiler_params=_params(dimension_semantics=("arbitrary",)),
    )(w, m, v, c_act_t, dmod_cols)


def _sum_chip_partials(own_ref, recv_ref):
    g = own_ref[...].astype(F32)
    for j in range(N_OTHER_CHIPS):
        g = g + recv_ref[j].astype(F32)
    return g


def _partial_specs(row_tile, cols, first_layer=0):
    own = pl.BlockSpec((None, None, row_tile, cols), lambda l, r, chip_ref: (chip_ref[0], first_layer + l, r, 0))
    recv = pl.BlockSpec((N_OTHER_CHIPS, None, row_tile, cols), lambda l, r, chip_ref: (0, first_layer + l, r, 0))
    return own, recv


def _adamw_reduced(name, w, m, v, partial, received, chip, row_tile, layers, continued):
    depth, rows, cols = w.shape
    first, stop = layers

    def body(chip_ref, w_ref, m_ref, v_ref, own_ref, recv_ref, *rest):
        g_ref, d_ref, mo_ref, vo_ref = rest[-4:]
        g = _sum_chip_partials(own_ref, recv_ref)
        g_ref[...] = g
        d_ref[...], mo_ref[...], vo_ref[...] = _adamw(w_ref[...], g, m_ref[...], v_ref[...])

    blk = pl.BlockSpec((None, row_tile, cols), lambda l, r, chip_ref: (first + l, r, 0))
    args = [chip, w, m, v, partial, received]
    in_specs = [blk, blk, blk, *_partial_specs(row_tile, cols, first)]
    aliases = {}
    if continued is not None:
        aliases = {len(args) + k: k for k in range(4)}
        args += list(continued)
        in_specs += [HBM] * 4
    return pl.pallas_call(
        body, name=name,
        grid_spec=pltpu.PrefetchScalarGridSpec(
            num_scalar_prefetch=1, grid=(stop - first, rows // row_tile), in_specs=in_specs, out_specs=[blk] * 4),
        out_shape=[jax.ShapeDtypeStruct(w.shape, F32)] * 4, input_output_aliases=aliases,
        compiler_params=_params(dimension_semantics=("arbitrary", "arbitrary")),
    )(*args)


def _reduce_w_pool(partial, received, chip):
    def body(chip_ref, own_ref, recv_ref, g_ref):
        g_ref[...] = _sum_chip_partials(own_ref, recv_ref)

    return pl.pallas_call(
        body, name="reduce_w_pool",
        grid_spec=pltpu.PrefetchScalarGridSpec(
            num_scalar_prefetch=1, grid=(DEPTH, 1), in_specs=list(_partial_specs(POOL_SHARD, GROUP_D)),
            out_specs=pl.BlockSpec((POOL_SHARD, GROUP_D), lambda l, r, chip_ref: (l, 0))),
        out_shape=jax.ShapeDtypeStruct((DEPTH * POOL_SHARD, GROUP_D), F32),
        compiler_params=_params(dimension_semantics=("arbitrary", "arbitrary")),
    )(chip, partial, received)


def _adamw_small(params):
    n = len(params)

    def body(*refs):
        ins, outs = refs[:4 * n], refs[4 * n:]
        for p in range(n):
            w_ref, g_ref, m_ref, v_ref = ins[4 * p:4 * p + 4]
            d_ref, mo_ref, vo_ref = outs[3 * p:3 * p + 3]
            d_ref[...], mo_ref[...], vo_ref[...] = _adamw(w_ref[...], g_ref[...], m_ref[...], v_ref[...])

    vmem = pl.BlockSpec(memory_space=pltpu.VMEM)
    flat = [a for group in params for a in group]
    out_shape = [jax.ShapeDtypeStruct(group[0].shape, F32) for group in params for _ in range(3)]
    outs = pl.pallas_call(
        body, name="adamw_small", in_specs=[vmem] * len(flat), out_specs=[vmem] * len(out_shape),
        out_shape=out_shape, compiler_params=_params(),
    )(*flat)
    return [tuple(outs[3 * p:3 * p + 3]) for p in range(n)]


def _sum_sources(slabs):
    def body(s_ref, o_ref):
        acc = s_ref[0]
        for b in range(1, N_DEV):
            acc = acc + s_ref[b]
        o_ref[...] = acc

    vmem = pl.BlockSpec(memory_space=pltpu.VMEM)
    return pl.pallas_call(
        body, name="sum_small_grads", in_specs=[vmem], out_specs=vmem,
        out_shape=jax.ShapeDtypeStruct(slabs.shape[1:], F32), compiler_params=_params(),
    )(slabs)


def _to_bf16(a, name, layers=None):
    first, stop = layers or (0, a.shape[0])

    def body(a_ref, o_ref):
        o_ref[...] = a_ref[...].astype(BF16)

    block = (None,) + a.shape[1:]
    return pl.pallas_call(
        body, name=name, grid=(stop - first,), in_specs=[pl.BlockSpec(block, lambda l: (first + l, 0, 0))],
        out_specs=pl.BlockSpec(block, lambda l: (l, 0, 0)),
        out_shape=jax.ShapeDtypeStruct((stop - first,) + a.shape[1:], BF16),
        compiler_params=_params(dimension_semantics=("arbitrary",)),
    )(a)


def kernel(x, c, w_ada, b_ada, g_pre, w_in, w_conv, w_pool, pool_scale, w_out, g_post, loss_target, m_w_ada, m_b_ada, m_g_pre, m_w_in, m_w_conv, m_w_pool, m_pool_scale, m_w_out, m_g_post, v_w_ada, v_b_ada, v_g_pre, v_w_in, v_w_conv, v_w_pool, v_pool_scale, v_w_out, v_g_post):
    mx, my, mc = _mesh_position()
    me = _block_id(mx, my, mc)
    chip = (2 * mx + my).astype(jnp.int32).reshape(1)
    core = mc.astype(jnp.int32).reshape(1)
    x0 = x[0]
    target = loss_target[0]
    conv_shard = w_conv.shape[-1]

    own_small = jnp.concatenate([c, w_conv.reshape(1, DEPTH * 3 * conv_shard)], axis=1)
    first_shards = [_to_bf16(w_in, "cast_w_in_0", (0, 1)), _to_bf16(w_out, "cast_w_out_0", (0, 1))]
    all_small = _all_gather_small(own_small, "all_gather_c_w_conv", first_shards)[:, 0, :]
    gathers = [_gather_weights_start(0, *first_shards, [all_small], relayed=True)]
    c_all = all_small[:, :D_MODEL]
    w_conv_full = all_small[:, D_MODEL:].reshape(N_DEV, DEPTH, 3, conv_shard).transpose(1, 2, 0, 3).reshape(
        DEPTH, 3, CONV_W)
    cps = jnp.concatenate([w_conv_full, pool_scale[:, None], jnp.zeros((DEPTH, 4, CONV_W), F32)], axis=1)

    c_act, pieces = _modulation_columns(c_all, w_ada)
    upper = (1, DEPTH)
    upper_shards = [_to_bf16(w_in, "cast_w_in_1", upper), _to_bf16(w_out, "cast_w_out_1", upper)]
    wpool_b = _to_bf16(w_pool.reshape(DEPTH, POOL_ROWS, GROUP_D), "cast_w_pool").reshape(w_pool.shape)
    first_sems_0, _, (zones_0,) = gathers[0]
    neighbour_sems, zones_0 = _gather_weights_pass_on(
        "all_gather_weights_relay_0", NEIGHBOUR_CHIPS, first_sems_0[1], partial(_first_arrival, 0), zones_0,
        [pieces, *upper_shards, wpool_b], relay=True)
    mod_all = _all_gather_small(pieces, "all_gather_modulation", [zones_0[0]])
    mod_mine = lax.dynamic_index_in_dim(mod_all, me, axis=1, keepdims=False)
    mod = mod_mine.reshape(N_DEV, DEPTH, W_IN_SHARD).transpose(1, 0, 2).reshape(DEPTH, 3 * D_MODEL) + b_ada
    zeros_d = jnp.zeros((DEPTH, 3, D_MODEL), F32)
    vec = jnp.concatenate([mod.reshape(DEPTH, 3, D_MODEL), g_pre[:, None], g_post[:, None], zeros_d], axis=1)

    gathers.append(_gather_weights_start(1, *upper_shards, [mod_all]))
    gathers = [(first_sems, shards, landing[k], k) for first_sems, shards, landing in gathers
               for k in range(len(landing))]

    xs, kept, wins, wouts = [x0], [], [], []
    for l in range(DEPTH):
        first_sems, shards, zones, index = gathers[l]
        if l == 0:
            diagonal_sems, zones = _gather_weights_pass_on(
                "all_gather_weights_pass_on_0", DIAGONAL_CHIP, neighbour_sems[3], lambda a, j: a, zones_0,
                [vec, cps, gathers[-1][1][0]])
            passed = [(neighbour_sems[:2], NEIGHBOUR_CHIPS), (diagonal_sems, DIAGONAL_CHIP)]
            win, wout = _gather_weights_finish(l, index, first_sems, passed, shards, zones, neighbour_sems[2])
        else:
            passed_sems, zones = _gather_weights_pass_on(
                f"all_gather_weights_pass_on_{l}", ALL_OTHER_CHIPS, first_sems[1], partial(_first_arrival, index),
                zones, [xs[-1]])
            win, wout = _gather_weights_finish(l, index, first_sems, [(passed_sems, ALL_OTHER_CHIPS)], shards, zones)
        x_next, *for_backward = _forward_layer(
            l, xs[-1], vec, cps, wpool_b, win, wout, target if l == DEPTH - 1 else None)
        xs.append(x_next)
        kept.append(for_backward[:6])
        wins.append(win)
        wouts.append(wout)
    dx, loss_tile = xs[DEPTH], for_backward[6]

    slab_rows = [None] * DEPTH
    partials = received = None
    in_flight = []

    def scatter(layer, grads, from_sibling, after):
        nonlocal partials, received
        partials = _add_sibling_blocks(
            f"grad_add_sibling_{layer}", layer, grads, from_sibling, core, partials, ALL_KINDS)
        chips = _chips_exchange(layer, partials, received, ALL_KINDS)
        sems, partials, received, token = _start_exchange(chips, f"grad_chips_start_{layer}", after)
        in_flight.append((chips, sems, layer))
        return token

    grads_above = None
    for l in reversed(range(DEPTH)):
        y, h, ycat, uc, fa, fp = kept[l]
        dx, dproj, dy, gwpool, dcps, dvec = _backward_layer(
            l, dx, y, xs[l], uc, fa, fp, vec, cps, wpool_b, wouts[l], wins[l])
        slab_rows[l] = jnp.concatenate(
            [dvec[0], dvec[1], dvec[2], dvec[3], dvec[4], dcps[3], dcps[0], dcps[1], dcps[2],
             loss_tile[0] if l == 0 else jnp.zeros((LANES,), F32)])
        after = []
        if l == 0:
            gather_small = _all_gather_exchange(jnp.stack(slab_rows))
            sems_s, slab, slabs, token = _start_exchange(gather_small, "all_gather_small_grads_start")
            after = [token]
        hosted = _sibling_exchange(grads_above, ALL_KINDS) if grads_above is not None else None
        for_w_in, for_w_out = (h, dproj, GWIN_COLS, W_IN_SHARD), (ycat, dy, GWOUT_COLS, None)
        if l > 1:
            gwin, gwout, *from_sibling = _weight_grads(f"weight_grads_{l}", [for_w_in, for_w_out], hosted, after)
        else:
            gwin, *from_sibling = _weight_grads(f"weight_grads_in_{l}", [for_w_in], hosted, after)
            gwout, *sibling_w_in = _weight_grads(
                f"weight_grads_out_{l}", [for_w_out], _sibling_exchange([gwin], ALL_KINDS[:1]))
        if l == 0:
            _, (slabs,) = _finish_exchange(
                gather_small, "all_gather_small_grads_finish", sems_s, slab, slabs, [gwin])
        if grads_above is not None:
            scatter(l + 1, grads_above, from_sibling, [])
        grads_above = [gwin, gwout, gwpool.reshape(POOL_ROWS, GROUP_D)]
        if l <= 1:
            sibling_rest = _run_exchange(
                _sibling_exchange(grads_above[1:], ALL_KINDS[1:]), f"grad_exchange_sibling_{l}")
            token = scatter(l, grads_above, [*sibling_w_in, *sibling_rest], [slabs] if l == 0 else [])
            grads_above = None
    grad_x = dx[None]
    chips_0, sems_0, _ = in_flight.pop()

    total = _sum_sources(slabs)
    loss = total[0, SLAB_COLS]
    o = 3 * D_MODEL
    g_b_ada = total[:, :o]
    g_g_pre = total[:, o:o + D_MODEL]
    g_g_post = total[:, o + D_MODEL:o + 2 * D_MODEL]
    g_pool_scale = total[:, o + 2 * D_MODEL:o + 2 * D_MODEL + POOL_W]
    g_conv_full = total[:, o + 2 * D_MODEL + POOL_W:SLAB_COLS].reshape(DEPTH, 3, CONV_W)
    g_w_conv = lax.dynamic_slice_in_dim(g_conv_full, me * conv_shard, conv_shard, axis=2)

    after = [token]
    for chips, sems, l in in_flight:
        partials, received = _finish_exchange(chips, f"grad_chips_finish_{l}", sems, partials, received, after)
        after = []
    upper = (1, DEPTH)
    w_in_upper = _adamw_reduced(
        "adamw_w_in_upper", w_in, m_w_in, v_w_in, partials[0], received[0], chip, ROW_TILE, upper, None)
    w_out_upper = _adamw_reduced(
        "adamw_w_out_upper", w_out, m_w_out, v_w_out, partials[1], received[1], chip, W_OUT_SHARD, upper, None)
    dmod_all = slabs[:, :, :o].reshape(N_DEV, DEPTH, N_DEV, W_IN_SHARD)
    dmod_cols = lax.dynamic_index_in_dim(dmod_all, me, axis=2, keepdims=False).transpose(1, 0, 2) + token[0, 0]
    g_w_ada, d_w_ada, nm_w_ada, nv_w_ada = _adamw_w_ada(w_ada, m_w_ada, v_w_ada, c_act.T, dmod_cols)

    partials, received = _finish_exchange(
        chips_0, "grad_chips_finish_0", sems_0, partials, received, [nv_w_ada, w_in_upper[3], w_out_upper[3]])
    gather_pool = _all_gather_exchange(_reduce_w_pool(partials[2], received[2], chip))
    sems_p, pool_rows, pool_landing, token_p = _start_exchange(gather_pool, "all_gather_grad_w_pool_start")
    g_w_in, d_w_in, nm_w_in, nv_w_in = _adamw_reduced(
        "adamw_w_in_0", w_in, m_w_in, v_w_in, partials[0], received[0], chip, ROW_TILE, (0, 1), w_in_upper)
    g_w_out, d_w_out, nm_w_out, nv_w_out = _adamw_reduced(
        "adamw_w_out_0", w_out, m_w_out, v_w_out, partials[1], received[1], chip, W_OUT_SHARD, (0, 1), w_out_upper)
    _, (g_pool_all,) = _finish_exchange(
        gather_pool, "all_gather_grad_w_pool_finish", sems_p, pool_rows, pool_landing, [nv_w_in, nv_w_out])
    g_w_pool = g_pool_all.reshape(N_DEV, DEPTH, POOL_SHARD, GROUP_D).transpose(1, 0, 2, 3).reshape(w_pool.shape)

    flat2 = lambda a: a.reshape(-1, a.shape[-1])
    small = _adamw_small([
        (b_ada, g_b_ada, m_b_ada, v_b_ada),
        (g_pre, g_g_pre, m_g_pre, v_g_pre),
        (flat2(w_conv), flat2(g_w_conv), flat2(m_w_conv), flat2(v_w_conv)),
        (flat2(w_pool), flat2(g_w_pool), flat2(m_w_pool), flat2(v_w_pool)),
        (pool_scale, g_pool_scale, m_pool_scale, v_pool_scale),
        (g_post, g_g_post, m_g_post, v_g_post),
    ])
    (d_b_ada, nm_b_ada, nv_b_ada), (d_g_pre, nm_g_pre, nv_g_pre), conv_upd, pool_upd, \
        (d_ps, nm_ps, nv_ps), (d_g_post, nm_g_post, nv_g_post) = small
    d_w_conv, nm_w_conv, nv_w_conv = (a.reshape(w_conv.shape) for a in conv_upd)
    d_w_pool, nm_w_pool, nv_w_pool = (a.reshape(w_pool.shape) for a in pool_upd)

    return (loss, grad_x,
            g_w_ada, g_b_ada, g_g_pre, g_w_in, g_w_conv, g_w_pool, g_pool_scale, g_w_out, g_g_post,
            d_w_ada, d_b_ada, d_g_pre, d_w_in, d_w_conv, d_w_pool, d_ps, d_w_out, d_g_post,
            nm_w_ada, nm_b_ada, nm_g_pre, nm_w_in, nm_w_conv, nm_w_pool, nm_ps, nm_w_out, nm_g_post,
            nv_w_ada, nv_b_ada, nv_g_pre, nv_w_in, nv_w_conv, nv_w_pool, nv_ps, nv_w_out, nv_g_post)
```

```python
from functools import partial

import jax
import jax.numpy as jnp
from jax import lax
from jax.experimental import pallas as pl
from jax.experimental.pallas import tpu as pltpu

F32 = jnp.float32
BF16 = jnp.bfloat16

D_MODEL = 1024
DEPTH = 4
CONV_W = 512
POOL_W = 512
POOL_WINDOWS = (2, 4, 8, 16)
GROUP_D = 128
IN_COLS = 4 * CONV_W + 2 * POOL_W
NORM_EPS = 1e-6

ADAM_LR = 0.001
ADAM_B1 = 0.9
ADAM_B2 = 0.999
ADAM_EPS = 1e-08
ADAM_WD = 0.01
ADAM_STEP = 10

N_DEV = 8
N_CHIP = 4
N_OTHER_CHIPS = N_CHIP - 1
MESH = pl.DeviceIdType.MESH
W_IN_SHARD = IN_COLS // N_DEV
W_OUT_SHARD = D_MODEL // N_DEV
POOL_ROWS = len(POOL_WINDOWS) * GROUP_D
POOL_SHARD = POOL_ROWS // N_DEV

SUBLANES = 8
LANES = 128
VMEM_LIMIT_BYTES = 56 * 1024 * 1024
ROW_TILE = 512
BWD_TILE = 256
GWIN_COLS = 768
GWOUT_COLS = 512
POOL_HALO = 16
CONV_HALO = SUBLANES

SLAB_COLS = 3 * D_MODEL + D_MODEL + D_MODEL + POOL_W + 3 * CONV_W

HBM = pl.BlockSpec(memory_space=pl.ANY)


def _params(**kw):
    return pltpu.CompilerParams(vmem_limit_bytes=VMEM_LIMIT_BYTES, **kw)


def _sigmoid(v):
    return 1.0 / (1.0 + jnp.exp(-v))


def _dot(a, b):
    return jnp.dot(a, b, preferred_element_type=F32)


def _dot_tn(a, b):
    return lax.dot_general(a, b, (((0,), (0,)), ((), ())), preferred_element_type=F32)


def _dot_nt(a, b):
    return lax.dot_general(a, b, (((1,), (1,)), ((), ())), preferred_element_type=F32)


def _rows_from_before(v, k):
    return pltpu.roll(v, k, 0)


def _rows_from_after(v, k):
    return pltpu.roll(v, v.shape[0] - k, 0)


def _window_counts(t0, rows):
    return (lax.broadcasted_iota(jnp.int32, (rows, 1), 0) + (t0 + 1)).astype(F32)


def _split_proj(p32):
    cw = CONV_W
    return (p32[:, 0 * cw:1 * cw], p32[:, 1 * cw:2 * cw], p32[:, 2 * cw:3 * cw], p32[:, 3 * cw:4 * cw],
            p32[:, 4 * cw:4 * cw + POOL_W], p32[:, 4 * cw + POOL_W:])


def _layer_spec(shape, layer):
    nd = len(shape)
    return pl.BlockSpec((None,) + tuple(shape[1:]), lambda i, _l=layer, _n=nd: (_l,) + (0,) * (_n - 1))


def _whole_spec(shape):
    return pl.BlockSpec(tuple(shape), lambda i, _n=len(shape): (0,) * _n, pipeline_mode=pl.Buffered(1))


def _mesh_position():
    return lax.axis_index("x"), lax.axis_index("y"), lax.axis_index("c")


def _block_id(x, y, c):
    return 4 * x + 2 * y + c


def _other_chips(x, y):
    return [(x ^ 1, y), (x, y ^ 1), (x ^ 1, y ^ 1)]


def _col_block(ref, blk):
    return ref.at[:, pl.ds(pl.multiple_of(blk * W_IN_SHARD, LANES), W_IN_SHARD)]


def _row_block(rows):
    def block(ref, blk):
        return ref.at[pl.ds(pl.multiple_of(blk * rows, rows), rows), :]
    return block


_BLOCK_OF = (_col_block, _row_block(W_OUT_SHARD), _row_block(POOL_SHARD))
_BLOCK_SHAPES = ((D_MODEL, W_IN_SHARD), (W_OUT_SHARD, D_MODEL), (POOL_SHARD, GROUP_D))


class _Exchange:
    def __init__(self, inputs, out_shapes, aliases, sem_shapes, make):
        self.inputs, self.out_shapes, self.aliases, self.sem_shapes, self.make = (
            list(inputs), list(out_shapes), dict(aliases), list(sem_shapes), make)


def _run_exchange(exchange, name):
    n_in, n_out = len(exchange.inputs), len(exchange.out_shapes)

    def body(*refs):
        start, finish = exchange.make(refs[:n_in], refs[n_in:n_in + n_out], refs[n_in + n_out:])
        start()
        finish()

    return pl.pallas_call(
        body, name=name, in_specs=[HBM] * n_in, out_specs=[HBM] * n_out, out_shape=exchange.out_shapes,
        scratch_shapes=exchange.sem_shapes, input_output_aliases=exchange.aliases, compiler_params=_params(),
    )(*exchange.inputs)


_SEM = pl.BlockSpec(memory_space=pltpu.SEMAPHORE)
_DATAFLOW = pltpu.SideEffectType.DATAFLOW_SIDE_EFFECTING


def _start_exchange(exchange, name, after=()):
    n_in, n_out, n_sem = len(exchange.inputs), len(exchange.out_shapes), len(exchange.sem_shapes)
    sources = [i for i in range(n_in) if i not in exchange.aliases]
    aliases = {i: n_sem + k for k, i in enumerate(sources)}
    aliases.update({i: n_sem + len(sources) + o for i, o in exchange.aliases.items()})

    def body(*refs):
        in_refs = refs[:n_in]
        outs = refs[n_in + len(after):]
        sems = outs[:n_sem]
        out_refs = outs[n_sem + len(sources):n_sem + len(sources) + n_out]
        exchange.make(in_refs, out_refs, sems)[0]()
        refs[-1][...] = jnp.zeros_like(refs[-1])

    outs = pl.pallas_call(
        body, name=name, in_specs=[HBM] * (n_in + len(after)),
        out_specs=[_SEM] * n_sem + [HBM] * (len(sources) + n_out) + [pl.BlockSpec(memory_space=pltpu.VMEM)],
        out_shape=(exchange.sem_shapes + [pltpu.HBM(exchange.inputs[i].shape, exchange.inputs[i].dtype) for i in sources]
                   + [pltpu.HBM(s.shape, s.dtype) for s in exchange.out_shapes]
                   + [jax.ShapeDtypeStruct((SUBLANES, LANES), F32)]),
        input_output_aliases=aliases, compiler_params=_params(has_side_effects=_DATAFLOW),
    )(*exchange.inputs, *after)
    return outs[:n_sem], outs[n_sem:n_sem + len(sources)], outs[n_sem + len(sources):-1], outs[-1]


def _finish_exchange(exchange, name, sems, sources, landing, after):
    n_src, n_out, n_sem = len(sources), len(landing), len(sems)
    n_in = len(exchange.inputs)
    source_at = [i for i in range(n_in) if i not in exchange.aliases]

    def body(*refs):
        src_refs, out_refs = refs[:n_src], refs[n_src:n_src + n_out]
        sem_refs = refs[n_src + n_out:n_src + n_out + n_sem]
        in_refs = [None] * n_in
        for k, i in enumerate(source_at):
            in_refs[i] = src_refs[k]
        for i, o in exchange.aliases.items():
            in_refs[i] = out_refs[o]
        exchange.make(in_refs, out_refs, sem_refs)[1]()

    arrays = list(sources) + list(landing)
    outs = pl.pallas_call(
        body, name=name, in_specs=[HBM] * len(arrays) + [_SEM] * n_sem + [HBM] * len(after),
        out_specs=[HBM] * len(arrays), out_shape=[pltpu.HBM(a.shape, a.dtype) for a in arrays],
        input_output_aliases={i: i for i in range(len(arrays))}, compiler_params=_params(has_side_effects=_DATAFLOW),
    )(*arrays, *sems, *after)
    return outs[:n_src], outs[n_src:]


N_GATHERED = 2
FIRST_COPIES = 1 + N_OTHER_CHIPS
_GATHERED_SHAPES = ((D_MODEL, IN_COLS), (D_MODEL, D_MODEL))
ALL_OTHER_CHIPS, NEIGHBOUR_CHIPS, DIAGONAL_CHIP = (0, 1, 2), (0, 1), (2,)


def _first_sem(layer, a, k):
    return (layer * N_GATHERED + a) * FIRST_COPIES + k


def _first_arrival(layer, a, j):
    return _first_sem(layer, a, 1 + j)


def _gather_copy(window_of, full_ref, blk, send_sem, recv_sem, to, src=None):
    window = window_of(full_ref, blk)
    return pltpu.make_async_remote_copy(
        src_ref=window if src is None else src, dst_ref=window, send_sem=send_sem, recv_sem=recv_sem,
        device_id=to, device_id_type=MESH)


def _first_copies(layer, shard_refs, full_refs, send_sems, recv_sems, local_sems, relayed):
    x, y, c = _mesh_position()
    me = _block_id(x, y, c)
    chips = [_other_chips(x, y)[j] for j in (NEIGHBOUR_CHIPS if relayed else ALL_OTHER_CHIPS)]
    own, remote = [], []
    for a in range(N_GATHERED):
        shard = shard_refs[a].at[layer]
        own.append(pltpu.make_async_copy(
            shard, _BLOCK_OF[a](full_refs[a], me), local_sems.at[layer * N_GATHERED + a]))
        targets = [(x, y, 1 - c)] + [(*chip, c) for chip in chips]
        remote += [_gather_copy(_BLOCK_OF[a], full_refs[a], me, send_sems.at[_first_sem(layer, a, k)],
                                recv_sems.at[_first_sem(layer, a, k)], to, src=shard)
                   for k, to in enumerate(targets)]
    return own, remote


def _gather_weights_start(first_layer, win_shards, wout_shards, after, relayed=False):
    n_layers = win_shards.shape[0]
    n_first = n_layers * N_GATHERED * FIRST_COPIES
    sem_shapes = [pltpu.SemaphoreType.DMA((n_first,)), pltpu.SemaphoreType.DMA((n_first,)),
                  pltpu.SemaphoreType.DMA((n_layers * N_GATHERED,))]
    shards = [win_shards, wout_shards]

    def body(win_sh, wout_sh, *rest):
        send_sems, recv_sems, local_sems, win_thru, wout_thru, *landing = rest[len(after):]
        for layer in range(n_layers):
            own, remote = _first_copies(layer, (win_sh, wout_sh), landing[N_GATHERED * layer:N_GATHERED * (layer + 1)],
                                        send_sems, recv_sems, local_sems, relayed)
            for cp in own + remote:
                cp.start()

    outs = pl.pallas_call(
        body, name=f"all_gather_weights_start_{first_layer}", in_specs=[HBM] * (2 + len(after)),
        out_specs=[_SEM] * 3 + [HBM] * (2 + n_layers * N_GATHERED),
        out_shape=(sem_shapes + [pltpu.HBM(s.shape, s.dtype) for s in shards]
                   + [pltpu.HBM(s, BF16) for _ in range(n_layers) for s in _GATHERED_SHAPES]),
        input_output_aliases={0: 3, 1: 4}, compiler_params=_params(has_side_effects=_DATAFLOW),
    )(*shards, *after)
    landing = outs[5:]
    return outs[:3], outs[3:5], [landing[N_GATHERED * l:N_GATHERED * (l + 1)] for l in range(n_layers)]


def _passed_on_copies(full_refs, send_sems, recv_sems, core_of_block, chips):
    x, y, c = _mesh_position()
    return [_gather_copy(_BLOCK_OF[a], full_refs[a], _block_id(*_other_chips(x, y)[j], core_of_block),
                         send_sems.at[a * N_OTHER_CHIPS + j], recv_sems.at[a * N_OTHER_CHIPS + j], (x, y, 1 - c))
            for a in range(N_GATHERED) for j in chips]


def _relayed_copies(full_refs, send_sems, recv_sems):
    x, y, c = _mesh_position()
    source, to = (x ^ (1 - c), y ^ c), (x ^ c, y ^ (1 - c))
    return [_gather_copy(_BLOCK_OF[a], full_refs[a], _block_id(*source, c), send_sems.at[a], recv_sems.at[a], (*to, c))
            for a in range(N_GATHERED)]


def _gather_weights_pass_on(name, chips, arrival_sems, arrival_sem_of, landing, after, relay=False):
    n = N_GATHERED * N_OTHER_CHIPS
    sem_shapes = [pltpu.SemaphoreType.DMA((n,))] * 2 + [pltpu.SemaphoreType.DMA((N_GATHERED,))] * (2 if relay else 0)

    def body(win_ref, wout_ref, arrivals, *rest):
        sems = rest[len(after):len(after) + len(sem_shapes)]
        x, y, c = _mesh_position()
        full_refs = (win_ref, wout_ref)
        passed = _passed_on_copies(full_refs, sems[0], sems[1], c, chips)
        relayed = _relayed_copies(full_refs, sems[2], sems[3]) if relay else []
        for a in range(N_GATHERED):
            for j in chips:
                sem = arrivals.at[arrival_sem_of(a, j)]
                _gather_copy(_BLOCK_OF[a], full_refs[a], _block_id(*_other_chips(x, y)[j], c), sem, sem,
                             (x, y, c)).wait_recv()
            for cp in relayed[a:a + 1] + passed[a * len(chips):(a + 1) * len(chips)]:
                cp.start()

    outs = pl.pallas_call(
        body, name=name, in_specs=[HBM] * N_GATHERED + [_SEM] + [HBM] * len(after),
        out_specs=[_SEM] * len(sem_shapes) + [HBM] * N_GATHERED,
        out_shape=sem_shapes + [pltpu.HBM(a.shape, a.dtype) for a in landing],
        input_output_aliases={a: len(sem_shapes) + a for a in range(N_GATHERED)},
        compiler_params=_params(has_side_effects=_DATAFLOW),
    )(*landing, arrival_sems, *after)
    return outs[:len(sem_shapes)], outs[len(sem_shapes):]


def _gather_weights_finish(layer, index, first_sems, passed, shards, landing, relay_send_sems=None):
    relayed = relay_send_sems is not None
    passed_sems = [sem for (send, recv), _ in passed for sem in (send, recv)] + ([relay_send_sems] if relayed else [])

    def body(win_ref, wout_ref, first_send, first_recv, local_sems, *rest):
        sems, (win_sh, wout_sh) = rest[:len(passed_sems)], rest[len(passed_sems):len(passed_sems) + 2]
        x, y, c = _mesh_position()
        full_refs = (win_ref, wout_ref)
        own, sent = _first_copies(index, (win_sh, wout_sh), full_refs, first_send, first_recv, local_sems, relayed)
        for a in range(N_GATHERED):
            sem = _first_sem(index, a, 0)
            _gather_copy(_BLOCK_OF[a], full_refs[a], _block_id(x, y, 1 - c), first_recv.at[sem], first_recv.at[sem],
                         (x, y, c)).wait_recv()
        for p, (_, chips) in enumerate(passed):
            for cp in _passed_on_copies(full_refs, sems[2 * p], sems[2 * p + 1], 1 - c, chips):
                cp.wait_recv()
            sent += _passed_on_copies(full_refs, sems[2 * p], sems[2 * p + 1], c, chips)
        if relayed:
            sent += _relayed_copies(full_refs, sems[-1], sems[-1])
        for cp in sent:
            cp.wait_send()
        for cp in own:
            cp.wait()

    return pl.pallas_call(
        body, name=f"all_gather_weights_finish_{layer}",
        in_specs=[HBM] * N_GATHERED + [_SEM] * (3 + len(passed_sems)) + [HBM] * 2,
        out_specs=[HBM] * N_GATHERED, out_shape=[pltpu.HBM(a.shape, a.dtype) for a in landing],
        input_output_aliases={a: a for a in range(N_GATHERED)}, compiler_params=_params(has_side_effects=_DATAFLOW),
    )(*landing, *first_sems, *passed_sems, *shards)


ALL_KINDS = (0, 1, 2)


def _sibling_exchange(grads, kinds):
    n_arr = len(grads)

    def make(in_refs, out_refs, sems):
        send_sems, recv_sems = sems
        x, y, c = _mesh_position()
        copies = [pltpu.make_async_remote_copy(
            src_ref=_BLOCK_OF[kinds[a]](in_refs[a], 2 * q + (1 - c)), dst_ref=out_refs[a].at[q],
            send_sem=send_sems.at[a * N_CHIP + q], recv_sem=recv_sems.at[a * N_CHIP + q],
            device_id=(x, y, 1 - c), device_id_type=MESH)
            for a in range(n_arr) for q in range(N_CHIP)]

        def start():
            for cp in copies:
                cp.start()

        def finish():
            for cp in copies:
                cp.wait_recv()
            for cp in copies:
                cp.wait_send()

        return start, finish

    return _Exchange(
        grads, [jax.ShapeDtypeStruct((N_CHIP,) + _BLOCK_SHAPES[k], BF16) for k in kinds], {},
        [pltpu.SemaphoreType.DMA((n_arr * N_CHIP,)), pltpu.SemaphoreType.DMA((n_arr * N_CHIP,))], make)


def _chips_exchange(layer, partials, received, kinds):
    n_arr = len(partials)

    def make(in_refs, out_refs, sems):
        send_sems, recv_sems = sems
        x, y, c = _mesh_position()
        copies = [pltpu.make_async_remote_copy(
            src_ref=in_refs[a].at[2 * qx + qy, layer], dst_ref=out_refs[a].at[j, layer],
            send_sem=send_sems.at[a * N_OTHER_CHIPS + j], recv_sem=recv_sems.at[a * N_OTHER_CHIPS + j],
            device_id=(qx, qy, c), device_id_type=MESH)
            for a in range(n_arr) for j, (qx, qy) in enumerate(_other_chips(x, y))]

        def start():
            for cp in copies:
                cp.start()

        def finish():
            for cp in copies:
                cp.wait_recv()
            for cp in copies:
                cp.wait_send()

        return start, finish

    inputs = list(partials)
    aliases = {}
    if received is not None:
        inputs += list(received)
        aliases = {n_arr + a: a for a in range(n_arr)}
    return _Exchange(
        inputs, [jax.ShapeDtypeStruct((N_OTHER_CHIPS, DEPTH) + _BLOCK_SHAPES[k], BF16) for k in kinds], aliases,
        [pltpu.SemaphoreType.DMA((n_arr * N_OTHER_CHIPS,)), pltpu.SemaphoreType.DMA((n_arr * N_OTHER_CHIPS,))], make)


def _all_gather_exchange(v):
    def make(in_refs, out_refs, sems):
        send_sems, recv_sems, local_sem = sems
        x, y, c = _mesh_position()
        me = _block_id(x, y, c)
        own = pltpu.make_async_copy(in_refs[0], out_refs[0].at[me], local_sem.at[0])
        sends, arrivals = [], []
        for k in range(1, N_DEV):
            px, py, pc = x ^ ((k >> 2) & 1), y ^ ((k >> 1) & 1), c ^ (k & 1)
            sends.append(pltpu.make_async_remote_copy(
                src_ref=in_refs[0], dst_ref=out_refs[0].at[me], send_sem=send_sems.at[k - 1],
                recv_sem=recv_sems.at[k - 1], device_id=(px, py, pc), device_id_type=MESH))
            arrivals.append(pltpu.make_async_remote_copy(
                src_ref=in_refs[0], dst_ref=out_refs[0].at[_block_id(px, py, pc)], send_sem=send_sems.at[k - 1],
                recv_sem=recv_sems.at[k - 1], device_id=(x, y, c), device_id_type=MESH))

        def start():
            for cp in [own] + sends:
                cp.start()

        def finish():
            for cp in arrivals:
                cp.wait_recv()
            for cp in sends:
                cp.wait_send()
            own.wait()

        return start, finish

    return _Exchange(
        [v], [jax.ShapeDtypeStruct((N_DEV,) + v.shape, v.dtype)], {},
        [pltpu.SemaphoreType.DMA((N_DEV - 1,)), pltpu.SemaphoreType.DMA((N_DEV - 1,)),
         pltpu.SemaphoreType.DMA((1,))], make)


def _host(exchange, args, in_specs, out_shape, out_specs, scratch):
    n_own = (len(args), len(out_shape), len(scratch))
    if exchange is None:
        return {}, lambda refs: (refs, None)
    n_ex = (len(exchange.inputs), len(exchange.out_shapes), len(exchange.sem_shapes))
    aliases = {n_own[0] + i: n_own[1] + o for i, o in exchange.aliases.items()}
    args += exchange.inputs
    in_specs += [HBM] * n_ex[0]
    out_shape += exchange.out_shapes
    out_specs += [HBM] * n_ex[1]
    scratch += exchange.sem_shapes

    def split(refs):
        own, theirs, at = [], [], 0
        for mine, ex in zip(n_own, n_ex):
            own += refs[at:at + mine]
            theirs.append(refs[at + mine:at + mine + ex])
            at += mine + ex
        return own, exchange.make(*theirs)

    return aliases, split


def _all_gather_small(v, name, after=()):
    vmem = pl.BlockSpec(memory_space=pltpu.VMEM)

    def body(v_ref, *rest):
        out_ref, send_sems, recv_sems = rest[len(after):]
        x, y, c = _mesh_position()
        me = _block_id(x, y, c)
        out_ref[me] = v_ref[...]
        sends = []
        for k in range(1, N_DEV):
            px, py, pc = x ^ ((k >> 2) & 1), y ^ ((k >> 1) & 1), c ^ (k & 1)
            send = pltpu.make_async_remote_copy(
                src_ref=v_ref, dst_ref=out_ref.at[me], send_sem=send_sems.at[k - 1], recv_sem=recv_sems.at[k - 1],
                device_id=(px, py, pc), device_id_type=MESH)
            send.start()
            sends.append((send, _block_id(px, py, pc)))
        for k, (send, peer) in enumerate(sends):
            pltpu.make_async_remote_copy(
                src_ref=v_ref, dst_ref=out_ref.at[peer], send_sem=send_sems.at[k], recv_sem=recv_sems.at[k],
                device_id=(x, y, c), device_id_type=MESH).wait_recv()
        for send, _ in sends:
            send.wait_send()

    return pl.pallas_call(
        body, name=name, in_specs=[vmem] + [HBM] * len(after), out_specs=vmem,
        out_shape=jax.ShapeDtypeStruct((N_DEV,) + v.shape, v.dtype),
        scratch_shapes=[pltpu.SemaphoreType.DMA((N_DEV - 1,)), pltpu.SemaphoreType.DMA((N_DEV - 1,))],
        compiler_params=_params(),
    )(v, *after)


def _forward_layer(layer, x, vec, cps, wpool, win, wout, target=None):
    t_len = x.shape[0]
    n_tiles = t_len // ROW_TILE
    row = lambda cols: pl.BlockSpec((ROW_TILE, cols), lambda i: (i, 0))
    widths = (D_MODEL, 2 * CONV_W, 3 * CONV_W, 4 * POOL_W)
    head = target is not None

    def body(x_ref, vec_ref, cps_ref, wpool_ref, win_ref, wout_ref, *rest):
        target_ref = rest[0] if head else None
        xo_ref, y_ref, h_ref, ycat_ref, uc_ref, fa_ref, fp_ref = rest[head:head + 7]
        loss_ref = rest[head + 7] if head else None
        zc_ref, pc_ref = rest[-2:]
        i = pl.program_id(0)

        @pl.when(i == 0)
        def _():
            zc_ref[...] = jnp.zeros_like(zc_ref)
            pc_ref[...] = jnp.zeros_like(pc_ref)
            if head:
                loss_ref[...] = jnp.zeros_like(loss_ref)

        x_t = x_ref[...]
        shift, scale, gate = vec_ref[0:1, :], vec_ref[1:2, :], vec_ref[2:3, :]
        g_pre, g_post = vec_ref[3:4, :], vec_ref[4:5, :]
        w0, w1, w2, ps = cps_ref[0:1, :], cps_ref[1:2, :], cps_ref[2:3, :], cps_ref[3:4, :]
        rx = lax.rsqrt(jnp.mean(x_t * x_t, axis=-1, keepdims=True) + NORM_EPS)
        h = (x_t * rx) * g_pre * (1.0 + scale) + shift
        h_ref[...] = h.astype(BF16)
        proj = _dot(h.astype(BF16), win_ref[...])
        u_a, b_a, c_a, g_a, u_p, g_p = _split_proj(proj)
        uc_ref[...] = jnp.concatenate([u_a, c_a], axis=1).astype(BF16)

        z = c_a * u_a
        zcat = jnp.concatenate([zc_ref[...], z], axis=0)
        zc_ref[...] = z[ROW_TILE - CONV_HALO:]
        conv = (w0 * _rows_from_before(zcat, 2)[CONV_HALO:] + w1 * _rows_from_before(zcat, 1)[CONV_HALO:] + w2 * z)
        sig_a = _sigmoid(g_a)
        silu_a = g_a * sig_a
        b_conv = b_a * conv
        y_a = b_conv * silu_a
        fa_ref[...] = jnp.concatenate(
            [silu_a * conv, silu_a * b_a, b_conv * (sig_a + silu_a * (1.0 - sig_a))], axis=1).astype(BF16)

        pcat = jnp.concatenate([pc_ref[...], u_p], axis=0)
        pc_ref[...] = u_p[ROW_TILE - POOL_HALO:]
        counts = _window_counts(i * ROW_TILE, ROW_TILE)
        pooled, mixed = [], []
        for g, w in enumerate(POOL_WINDOWS):
            cols = slice(g * GROUP_D, (g + 1) * GROUP_D)
            s = pcat[:, cols]
            step = 1
            while step < w:
                s = s + _rows_from_before(s, step)
                step *= 2
            pooled_g = (s[POOL_HALO:] * (1.0 / jnp.minimum(counts, float(w))) - u_p[:, cols]).astype(BF16)
            pooled.append(pooled_g)
            mixed.append(_dot(pooled_g, wpool_ref[g]))
        mixed = jnp.concatenate(mixed, axis=1)
        sig_p = _sigmoid(g_p)
        silu_p = g_p * sig_p
        mixed_ps = mixed * ps
        y_p = mixed_ps * silu_p
        fp_ref[...] = jnp.concatenate(
            [(ps * silu_p).astype(BF16), (mixed_ps * (sig_p + silu_p * (1.0 - sig_p))).astype(BF16),
             (silu_p * mixed).astype(BF16)] + pooled, axis=1)

        ycat = jnp.concatenate([y_a, y_p], axis=1)
        ycat_ref[...] = ycat.astype(BF16)
        y_b = _dot(ycat.astype(BF16), wout_ref[...]).astype(BF16)
        y_ref[...] = y_b
        y_t = y_b.astype(F32)
        ry = lax.rsqrt(jnp.mean(y_t * y_t, axis=-1, keepdims=True) + NORM_EPS)
        x_next = x_t + gate * (y_t * ry * g_post)
        if head:
            err = x_next - target_ref[...]
            xo_ref[...] = err * (1.0 / D_MODEL)
            loss_ref[...] += jnp.sum(err * err) * (0.5 / D_MODEL)
        else:
            xo_ref[...] = x_next

    tile = (SUBLANES, LANES)
    return pl.pallas_call(
        body, name=f"forward_layer_{layer}", grid=(n_tiles,),
        in_specs=[row(D_MODEL), _layer_spec(vec.shape, layer), _layer_spec(cps.shape, layer),
                  _layer_spec(wpool.shape, layer), _whole_spec(win.shape), _whole_spec(wout.shape)]
        + [row(D_MODEL)] * head,
        out_specs=[row(D_MODEL), row(D_MODEL), row(D_MODEL), row(D_MODEL)] + [row(w) for w in widths[1:]]
        + [_whole_spec(tile)] * head,
        out_shape=[jax.ShapeDtypeStruct((t_len, D_MODEL), F32), jax.ShapeDtypeStruct((t_len, D_MODEL), BF16),
                   jax.ShapeDtypeStruct((t_len, D_MODEL), BF16), jax.ShapeDtypeStruct((t_len, D_MODEL), BF16)]
        + [jax.ShapeDtypeStruct((t_len, w), BF16) for w in widths[1:]] + [jax.ShapeDtypeStruct(tile, F32)] * head,
        scratch_shapes=[pltpu.VMEM((CONV_HALO, CONV_W), F32), pltpu.VMEM((POOL_HALO, POOL_W), F32)],
        compiler_params=_params(dimension_semantics=("arbitrary",)),
    )(x, vec, cps, wpool, win, wout, *([target] * head))


def _backward_layer(layer, dxo, y, x, uc, fa, fp, vec, cps, wpool, wout, win):
    t_len = dxo.shape[0]
    n_tiles = t_len // BWD_TILE
    halo_per_tile = BWD_TILE // POOL_HALO
    rev = lambda cols: pl.BlockSpec((BWD_TILE, cols), lambda i: (n_tiles - 1 - i, 0))
    halo_spec = pl.BlockSpec(
        (POOL_HALO, 2 * CONV_W), lambda i: (jnp.maximum((n_tiles - 1 - i) * halo_per_tile - 1, 0), 0))
    gwpool_shape = (len(POOL_WINDOWS), GROUP_D, GROUP_D)

    def body(dxo_ref, y_ref, x_ref, uc_ref, uch_ref, fa_ref, fp_ref, vec_ref, cps_ref, wpool_ref, wout_ref, win_ref,
             dx_ref, dproj_ref, dy_ref, gwpool_ref, dcps_ref, dvec_ref, gwpool_acc, dcc_ref, qc_ref):
        i = pl.program_id(0)
        tile = n_tiles - 1 - i

        @pl.when(i == 0)
        def _():
            gwpool_acc[...] = jnp.zeros_like(gwpool_acc)
            dcps_ref[...] = jnp.zeros_like(dcps_ref)
            dvec_ref[...] = jnp.zeros_like(dvec_ref)
            dcc_ref[...] = jnp.zeros_like(dcc_ref)
            qc_ref[...] = jnp.zeros_like(qc_ref)

        shift, scale, gate = vec_ref[0:1, :], vec_ref[1:2, :], vec_ref[2:3, :]
        g_pre, g_post = vec_ref[3:4, :], vec_ref[4:5, :]
        w0, w1, w2 = cps_ref[0:1, :], cps_ref[1:2, :], cps_ref[2:3, :]

        dxo_t = dxo_ref[...]
        y_t = y_ref[...].astype(F32)
        ry = lax.rsqrt(jnp.mean(y_t * y_t, axis=-1, keepdims=True) + NORM_EPS)
        yh = y_t * ry
        dvec_ref[2:3, :] += jnp.sum(dxo_t * yh, axis=0, keepdims=True)
        dyh = dxo_t * (gate * g_post)
        dy_b = (ry * (dyh - yh * jnp.mean(dyh * yh, axis=-1, keepdims=True))).astype(BF16)
        dy_ref[...] = dy_b
        dycat = _dot_nt(dy_b, wout_ref[...])
        dy_a, dy_p = dycat[:, :CONV_W], dycat[:, CONV_W:]

        fa_t = fa_ref[...].astype(F32)
        db_a = dy_a * fa_t[:, :CONV_W]
        dconv = dy_a * fa_t[:, CONV_W:2 * CONV_W]
        dg_a = dy_a * fa_t[:, 2 * CONV_W:]
        uc_t = uc_ref[...].astype(F32)
        u_a, c_a = uc_t[:, :CONV_W], uc_t[:, CONV_W:]
        halo = jnp.where(tile > 0, uch_ref[...].astype(F32), 0.0)[POOL_HALO - CONV_HALO:]
        z = c_a * u_a
        zcat = jnp.concatenate([halo[:, CONV_W:] * halo[:, :CONV_W], z], axis=0)
        z1 = _rows_from_before(zcat, 1)[CONV_HALO:]
        z2 = _rows_from_before(zcat, 2)[CONV_HALO:]
        dccat = jnp.concatenate([dconv, dcc_ref[...]], axis=0)
        dc1 = _rows_from_after(dccat, 1)[:BWD_TILE]
        dc2 = _rows_from_after(dccat, 2)[:BWD_TILE]
        dz = w2 * dconv + w1 * dc1 + w0 * dc2
        dcc_ref[...] = dconv[:CONV_HALO]
        dcps_ref[0:1, :] += jnp.sum(dconv * z2, axis=0, keepdims=True)
        dcps_ref[1:2, :] += jnp.sum(dconv * z1, axis=0, keepdims=True)
        dcps_ref[2:3, :] += jnp.sum(dconv * z, axis=0, keepdims=True)
        du_a = dz * c_a
        dc_a = dz * u_a

        dmixed = (dy_p * fp_ref[:, :POOL_W].astype(F32)).astype(BF16)
        dg_p = dy_p * fp_ref[:, POOL_W:2 * POOL_W].astype(F32)
        dcps_ref[3:4, :] += jnp.sum(dy_p * fp_ref[:, 2 * POOL_W:3 * POOL_W].astype(F32), axis=0, keepdims=True)
        counts = _window_counts(tile * BWD_TILE, BWD_TILE)
        du_p, q_head = [], []
        for g, w in enumerate(POOL_WINDOWS):
            cols = slice(g * GROUP_D, (g + 1) * GROUP_D)
            dm_g = dmixed[:, cols]
            dpooled_g = _dot_nt(dm_g, wpool_ref[g])
            gwpool_acc[g] += _dot_tn(fp_ref[:, 3 * POOL_W + g * GROUP_D:3 * POOL_W + (g + 1) * GROUP_D], dm_g)
            q_g = dpooled_g * (1.0 / jnp.minimum(counts, float(w)))
            q_head.append(q_g[:POOL_HALO])
            s = jnp.concatenate([q_g, qc_ref[:, cols]], axis=0)
            step = 1
            while step < w:
                s = s + _rows_from_after(s, step)
                step *= 2
            du_p.append(s[:BWD_TILE] - dpooled_g)
        qc_ref[...] = jnp.concatenate(q_head, axis=1)
        dproj_b = jnp.concatenate([du_a, db_a, dc_a, dg_a] + du_p + [dg_p], axis=1).astype(BF16)
        dproj_ref[...] = dproj_b

        x_t = x_ref[...]
        rx = lax.rsqrt(jnp.mean(x_t * x_t, axis=-1, keepdims=True) + NORM_EPS)
        xn = x_t * rx
        mod_scale = 1.0 + scale
        dh = _dot_nt(dproj_b, win_ref[...])
        dvec_ref[0:1, :] += jnp.sum(dh, axis=0, keepdims=True)
        dvec_ref[1:2, :] += jnp.sum(dh * xn, axis=0, keepdims=True)
        dxn = dh * (g_pre * mod_scale)
        dx_ref[...] = dxo_t + rx * (dxn - xn * jnp.mean(dxn * xn, axis=-1, keepdims=True))

        @pl.when(i == n_tiles - 1)
        def _():
            gwpool_ref[...] = gwpool_acc[...].astype(BF16)
            sum_dh_xn, sum_dxo_yh = dvec_ref[1:2, :], dvec_ref[2:3, :]
            dvec_ref[1:2, :] = sum_dh_xn * g_pre
            dvec_ref[3:4, :] = sum_dh_xn * mod_scale
            dvec_ref[2:3, :] = sum_dxo_yh * g_post
            dvec_ref[4:5, :] = sum_dxo_yh * gate

    return pl.pallas_call(
        body, name=f"backward_layer_{layer}", grid=(n_tiles,),
        in_specs=[rev(D_MODEL), rev(D_MODEL), rev(D_MODEL), rev(2 * CONV_W), halo_spec, rev(3 * CONV_W),
                  rev(4 * POOL_W), _layer_spec(vec.shape, layer), _layer_spec(cps.shape, layer),
                  _layer_spec(wpool.shape, layer), _whole_spec(wout.shape), _whole_spec(win.shape)],
        out_specs=[rev(D_MODEL), rev(IN_COLS), rev(D_MODEL), _whole_spec(gwpool_shape),
                   _whole_spec((SUBLANES, CONV_W)), _whole_spec((SUBLANES, D_MODEL))],
        out_shape=[jax.ShapeDtypeStruct((t_len, D_MODEL), F32), jax.ShapeDtypeStruct((t_len, IN_COLS), BF16),
                   jax.ShapeDtypeStruct((t_len, D_MODEL), BF16), jax.ShapeDtypeStruct(gwpool_shape, BF16),
                   jax.ShapeDtypeStruct((SUBLANES, CONV_W), F32), jax.ShapeDtypeStruct((SUBLANES, D_MODEL), F32)],
        scratch_shapes=[pltpu.VMEM(gwpool_shape, F32), pltpu.VMEM((CONV_HALO, CONV_W), F32),
                        pltpu.VMEM((POOL_HALO, POOL_W), F32)],
        compiler_params=_params(dimension_semantics=("arbitrary",)),
    )(dxo, y, x, uc, uc, fa, fp, vec, cps, wpool, wout, win)


def _weight_grads(layer, h, dproj, ycat, dy, exchange, after=()):
    t_len = dy.shape[0]
    n_in, n_out = IN_COLS // GWIN_COLS, D_MODEL // GWOUT_COLS
    args = [h, dproj, ycat, dy, *after]
    in_specs = [_whole_spec(h.shape),
                pl.BlockSpec((t_len, GWIN_COLS), lambda s: (0, jnp.minimum(s, n_in - 1))),
                _whole_spec(ycat.shape),
                pl.BlockSpec((t_len, GWOUT_COLS), lambda s: (0, jnp.maximum(s - n_in, 0)))] + [HBM] * len(after)
    out_shape = [jax.ShapeDtypeStruct((D_MODEL, IN_COLS), BF16), jax.ShapeDtypeStruct((D_MODEL, D_MODEL), BF16)]
    out_specs = [pl.BlockSpec((D_MODEL, GWIN_COLS), lambda s: (0, jnp.minimum(s, n_in - 1))),
                 pl.BlockSpec((D_MODEL, GWOUT_COLS), lambda s: (0, jnp.maximum(s - n_in, 0)))]
    scratch = []
    aliases, split = _host(exchange, args, in_specs, out_shape, out_specs, scratch)

    def body(*refs):
        (h_ref, dproj_ref, ycat_ref, dy_ref, *_, gwin_ref, gwout_ref), hosted = split(refs)
        s = pl.program_id(0)
        if hosted is not None:
            pl.when(s == 0)(hosted[0])

        @pl.when(s < n_in)
        def _():
            gwin_ref[...] = _dot_tn(h_ref[...], dproj_ref[...]).astype(BF16)

        @pl.when(s >= n_in)
        def _():
            gwout_ref[...] = _dot_tn(ycat_ref[...], dy_ref[...]).astype(BF16)

        if hosted is not None:
            pl.when(s == n_in + n_out - 1)(hosted[1])

    return pl.pallas_call(
        body, name=f"weight_grads_{layer}", grid=(n_in + n_out,), in_specs=in_specs, out_specs=out_specs,
        out_shape=out_shape, scratch_shapes=scratch, input_output_aliases=aliases,
        compiler_params=_params(dimension_semantics=("arbitrary",)),
    )(*args)


def _add_sibling_blocks(name, layer, grads, received, core, partials, kinds):
    n_arr = len(grads)

    def body(core_ref, *refs):
        mine, theirs, outs = refs[:n_arr], refs[n_arr:2 * n_arr], refs[-n_arr:]
        for a in range(n_arr):
            outs[a][...] = (mine[a][...].astype(F32) + theirs[a][...].astype(F32)).astype(BF16)

    own_of_kind = [
        pl.BlockSpec((D_MODEL, W_IN_SHARD), lambda q, core_ref: (0, 2 * q + core_ref[0])),
        pl.BlockSpec((W_OUT_SHARD, D_MODEL), lambda q, core_ref: (2 * q + core_ref[0], 0)),
        pl.BlockSpec((POOL_SHARD, GROUP_D), lambda q, core_ref: (2 * q + core_ref[0], 0)),
    ]
    shapes = [_BLOCK_SHAPES[k] for k in kinds]
    recv_specs = [pl.BlockSpec((None,) + s, lambda q, core_ref: (q, 0, 0)) for s in shapes]
    out_specs = [pl.BlockSpec((None, None) + s, lambda q, core_ref: (q, layer, 0, 0)) for s in shapes]
    args = [core, *grads, *received]
    in_specs = [own_of_kind[k] for k in kinds] + recv_specs
    aliases = {}
    if partials is not None:
        aliases = {len(args) + a: a for a in range(n_arr)}
        args += list(partials)
        in_specs += [HBM] * n_arr
    return pl.pallas_call(
        body, name=name,
        grid_spec=pltpu.PrefetchScalarGridSpec(
            num_scalar_prefetch=1, grid=(N_CHIP,), in_specs=in_specs, out_specs=out_specs),
        out_shape=[jax.ShapeDtypeStruct((N_CHIP, DEPTH) + s, BF16) for s in shapes],
        input_output_aliases=aliases,
        compiler_params=_params(dimension_semantics=("arbitrary",)),
    )(*args)


def _modulation_columns(c_all, w_ada):
    def body(c_ref, w_ref, cact_ref, out_ref):
        c_t = c_ref[...]
        c_act = c_t * _sigmoid(c_t)
        cact_ref[...] = c_act
        out_ref[...] = jnp.dot(c_act, w_ref[...], preferred_element_type=F32, precision=lax.Precision.HIGHEST)

    return pl.pallas_call(
        body, name="modulation_columns", grid=(DEPTH,),
        in_specs=[pl.BlockSpec((N_DEV, D_MODEL), lambda l: (0, 0)),
                  pl.BlockSpec((None, D_MODEL, W_IN_SHARD), lambda l: (l, 0, 0))],
        out_specs=[pl.BlockSpec((N_DEV, D_MODEL), lambda l: (0, 0)),
                   pl.BlockSpec((N_DEV, W_IN_SHARD), lambda l: (0, l))],
        out_shape=[jax.ShapeDtypeStruct((N_DEV, D_MODEL), F32),
                   jax.ShapeDtypeStruct((N_DEV, DEPTH * W_IN_SHARD), F32)],
        compiler_params=_params(dimension_semantics=("arbitrary",)),
    )(c_all, w_ada)


def _adamw(w, g, m, v):
    m_new = ADAM_B1 * m + (1.0 - ADAM_B1) * g
    v_new = ADAM_B2 * v + (1.0 - ADAM_B2) * (g * g)
    m_hat = m_new / (1.0 - ADAM_B1 ** ADAM_STEP)
    v_hat = v_new / (1.0 - ADAM_B2 ** ADAM_STEP)
    delta = -ADAM_LR * (m_hat / (jnp.sqrt(v_hat) + ADAM_EPS) + ADAM_WD * w)
    return delta, m_new, v_new


def _adamw_w_ada(w, m, v, c_act_t, dmod_cols):
    def body(w_ref, m_ref, v_ref, ct_ref, dm_ref, g_ref, d_ref, mo_ref, vo_ref):
        g = ct_ref[:, 0:1] * dm_ref[0:1, :]
        for b in range(1, N_DEV):
            g = g + ct_ref[:, b:b + 1] * dm_ref[b:b + 1, :]
        g_ref[...] = g
        d_ref[...], mo_ref[...], vo_ref[...] = _adamw(w_ref[...], g, m_ref[...], v_ref[...])

    big = pl.BlockSpec((None, D_MODEL, W_IN_SHARD), lambda l: (l, 0, 0))
    return pl.pallas_call(
        body, name="adamw_w_ada", grid=(DEPTH,),
        in_specs=[big, big, big, pl.BlockSpec((D_MODEL, N_DEV), lambda l: (0, 0)),
                  pl.BlockSpec((None, N_DEV, W_IN_SHARD), lambda l: (l, 0, 0))],
        out_specs=[big] * 4, out_shape=[jax.ShapeDtypeStruct(w.shape, F32)] * 4,
        compiler_params=_params(dimension_semantics=("arbitrary",)),
    )(w, m, v, c_act_t, dmod_cols)


def _sum_chip_partials(own_ref, recv_ref):
    g = own_ref[...].astype(F32)
    for j in range(N_OTHER_CHIPS):
        g = g + recv_ref[j].astype(F32)
    return g


def _partial_specs(row_tile, cols, first_layer=0):
    own = pl.BlockSpec((None, None, row_tile, cols), lambda l, r, chip_ref: (chip_ref[0], first_layer + l, r, 0))
    recv = pl.BlockSpec((N_OTHER_CHIPS, None, row_tile, cols), lambda l, r, chip_ref: (0, first_layer + l, r, 0))
    return own, recv


def _adamw_reduced(name, w, m, v, partial, received, chip, row_tile, layers, continued):
    depth, rows, cols = w.shape
    first, stop = layers

    def body(chip_ref, w_ref, m_ref, v_ref, own_ref, recv_ref, *rest):
        g_ref, d_ref, mo_ref, vo_ref = rest[-4:]
        g = _sum_chip_partials(own_ref, recv_ref)
        g_ref[...] = g
        d_ref[...], mo_ref[...], vo_ref[...] = _adamw(w_ref[...], g, m_ref[...], v_ref[...])

    blk = pl.BlockSpec((None, row_tile, cols), lambda l, r, chip_ref: (first + l, r, 0))
    args = [chip, w, m, v, partial, received]
    in_specs = [blk, blk, blk, *_partial_specs(row_tile, cols, first)]
    aliases = {}
    if continued is not None:
        aliases = {len(args) + k: k for k in range(4)}
        args += list(continued)
        in_specs += [HBM] * 4
    return pl.pallas_call(
        body, name=name,
        grid_spec=pltpu.PrefetchScalarGridSpec(
            num_scalar_prefetch=1, grid=(stop - first, rows // row_tile), in_specs=in_specs, out_specs=[blk] * 4),
        out_shape=[jax.ShapeDtypeStruct(w.shape, F32)] * 4, input_output_aliases=aliases,
        compiler_params=_params(dimension_semantics=("arbitrary", "arbitrary")),
    )(*args)


def _reduce_w_pool(partial, received, chip):
    def body(chip_ref, own_ref, recv_ref, g_ref):
        g_ref[...] = _sum_chip_partials(own_ref, recv_ref)

    return pl.pallas_call(
        body, name="reduce_w_pool",
        grid_spec=pltpu.PrefetchScalarGridSpec(
            num_scalar_prefetch=1, grid=(DEPTH, 1), in_specs=list(_partial_specs(POOL_SHARD, GROUP_D)),
            out_specs=pl.BlockSpec((POOL_SHARD, GROUP_D), lambda l, r, chip_ref: (l, 0))),
        out_shape=jax.ShapeDtypeStruct((DEPTH * POOL_SHARD, GROUP_D), F32),
        compiler_params=_params(dimension_semantics=("arbitrary", "arbitrary")),
    )(chip, partial, received)


def _adamw_small(params):
    n = len(params)

    def body(*refs):
        ins, outs = refs[:4 * n], refs[4 * n:]
        for p in range(n):
            w_ref, g_ref, m_ref, v_ref = ins[4 * p:4 * p + 4]
            d_ref, mo_ref, vo_ref = outs[3 * p:3 * p + 3]
            d_ref[...], mo_ref[...], vo_ref[...] = _adamw(w_ref[...], g_ref[...], m_ref[...], v_ref[...])

    vmem = pl.BlockSpec(memory_space=pltpu.VMEM)
    flat = [a for group in params for a in group]
    out_shape = [jax.ShapeDtypeStruct(group[0].shape, F32) for group in params for _ in range(3)]
    outs = pl.pallas_call(
        body, name="adamw_small", in_specs=[vmem] * len(flat), out_specs=[vmem] * len(out_shape),
        out_shape=out_shape, compiler_params=_params(),
    )(*flat)
    return [tuple(outs[3 * p:3 * p + 3]) for p in range(n)]


def _sum_sources(slabs):
    def body(s_ref, o_ref):
        acc = s_ref[0]
        for b in range(1, N_DEV):
            acc = acc + s_ref[b]
        o_ref[...] = acc

    vmem = pl.BlockSpec(memory_space=pltpu.VMEM)
    return pl.pallas_call(
        body, name="sum_small_grads", in_specs=[vmem], out_specs=vmem,
        out_shape=jax.ShapeDtypeStruct(slabs.shape[1:], F32), compiler_params=_params(),
    )(slabs)


def _to_bf16(a, name, layers=None):
    first, stop = layers or (0, a.shape[0])

    def body(a_ref, o_ref):
        o_ref[...] = a_ref[...].astype(BF16)

    block = (None,) + a.shape[1:]
    return pl.pallas_call(
        body, name=name, grid=(stop - first,), in_specs=[pl.BlockSpec(block, lambda l: (first + l, 0, 0))],
        out_specs=pl.BlockSpec(block, lambda l: (l, 0, 0)),
        out_shape=jax.ShapeDtypeStruct((stop - first,) + a.shape[1:], BF16),
        compiler_params=_params(dimension_semantics=("arbitrary",)),
    )(a)


def kernel(x, c, w_ada, b_ada, g_pre, w_in, w_conv, w_pool, pool_scale, w_out, g_post, loss_target, m_w_ada, m_b_ada, m_g_pre, m_w_in, m_w_conv, m_w_pool, m_pool_scale, m_w_out, m_g_post, v_w_ada, v_b_ada, v_g_pre, v_w_in, v_w_conv, v_w_pool, v_pool_scale, v_w_out, v_g_post):
    mx, my, mc = _mesh_position()
    me = _block_id(mx, my, mc)
    chip = (2 * mx + my).astype(jnp.int32).reshape(1)
    core = mc.astype(jnp.int32).reshape(1)
    x0 = x[0]
    target = loss_target[0]
    conv_shard = w_conv.shape[-1]

    own_small = jnp.concatenate([c, w_conv.reshape(1, DEPTH * 3 * conv_shard)], axis=1)
    first_shards = [_to_bf16(w_in, "cast_w_in_0", (0, 1)), _to_bf16(w_out, "cast_w_out_0", (0, 1))]
    all_small = _all_gather_small(own_small, "all_gather_c_w_conv", first_shards)[:, 0, :]
    gathers = [_gather_weights_start(0, *first_shards, [all_small], relayed=True)]
    c_all = all_small[:, :D_MODEL]
    w_conv_full = all_small[:, D_MODEL:].reshape(N_DEV, DEPTH, 3, conv_shard).transpose(1, 2, 0, 3).reshape(
        DEPTH, 3, CONV_W)
    cps = jnp.concatenate([w_conv_full, pool_scale[:, None], jnp.zeros((DEPTH, 4, CONV_W), F32)], axis=1)

    c_act, pieces = _modulation_columns(c_all, w_ada)
    upper = (1, DEPTH)
    upper_shards = [_to_bf16(w_in, "cast_w_in_1", upper), _to_bf16(w_out, "cast_w_out_1", upper)]
    wpool_b = _to_bf16(w_pool.reshape(DEPTH, POOL_ROWS, GROUP_D), "cast_w_pool").reshape(w_pool.shape)
    first_sems_0, _, (zones_0,) = gathers[0]
    neighbour_sems, zones_0 = _gather_weights_pass_on(
        "all_gather_weights_relay_0", NEIGHBOUR_CHIPS, first_sems_0[1], partial(_first_arrival, 0), zones_0,
        [pieces, *upper_shards, wpool_b], relay=True)
    mod_all = _all_gather_small(pieces, "all_gather_modulation", [zones_0[0]])
    mod_mine = lax.dynamic_index_in_dim(mod_all, me, axis=1, keepdims=False)
    mod = mod_mine.reshape(N_DEV, DEPTH, W_IN_SHARD).transpose(1, 0, 2).reshape(DEPTH, 3 * D_MODEL) + b_ada
    zeros_d = jnp.zeros((DEPTH, 3, D_MODEL), F32)
    vec = jnp.concatenate([mod.reshape(DEPTH, 3, D_MODEL), g_pre[:, None], g_post[:, None], zeros_d], axis=1)

    gathers.append(_gather_weights_start(1, *upper_shards, [mod_all]))
    gathers = [(first_sems, shards, landing[k], k) for first_sems, shards, landing in gathers
               for k in range(len(landing))]

    xs, kept, wins, wouts = [x0], [], [], []
    for l in range(DEPTH):
        first_sems, shards, zones, index = gathers[l]
        if l == 0:
            diagonal_sems, zones = _gather_weights_pass_on(
                "all_gather_weights_pass_on_0", DIAGONAL_CHIP, neighbour_sems[3], lambda a, j: a, zones_0,
                [vec, cps, gathers[-1][1][0]])
            passed = [(neighbour_sems[:2], NEIGHBOUR_CHIPS), (diagonal_sems, DIAGONAL_CHIP)]
            win, wout = _gather_weights_finish(l, index, first_sems, passed, shards, zones, neighbour_sems[2])
        else:
            passed_sems, zones = _gather_weights_pass_on(
                f"all_gather_weights_pass_on_{l}", ALL_OTHER_CHIPS, first_sems[1], partial(_first_arrival, index),
                zones, [xs[-1]])
            win, wout = _gather_weights_finish(l, index, first_sems, [(passed_sems, ALL_OTHER_CHIPS)], shards, zones)
        x_next, *for_backward = _forward_layer(
            l, xs[-1], vec, cps, wpool_b, win, wout, target if l == DEPTH - 1 else None)
        xs.append(x_next)
        kept.append(for_backward[:6])
        wins.append(win)
        wouts.append(wout)
    dx, loss_tile = xs[DEPTH], for_backward[6]

    slab_rows = [None] * DEPTH
    partials = received = None
    in_flight = []

    def scatter(layer, grads, from_sibling, after):
        nonlocal partials, received
        partials = _add_sibling_blocks(
            f"grad_add_sibling_{layer}", layer, grads, from_sibling, core, partials, ALL_KINDS)
        chips = _chips_exchange(layer, partials, received, ALL_KINDS)
        sems, partials, received, token = _start_exchange(chips, f"grad_chips_start_{layer}", after)
        in_flight.append((chips, sems, layer))
        return token

    grads_above = None
    for l in reversed(range(DEPTH)):
        y, h, ycat, uc, fa, fp = kept[l]
        dx, dproj, dy, gwpool, dcps, dvec = _backward_layer(
            l, dx, y, xs[l], uc, fa, fp, vec, cps, wpool_b, wouts[l], wins[l])
        slab_rows[l] = jnp.concatenate(
            [dvec[0], dvec[1], dvec[2], dvec[3], dvec[4], dcps[3], dcps[0], dcps[1], dcps[2],
             loss_tile[0] if l == 0 else jnp.zeros((LANES,), F32)])
        after = []
        if l == 0:
            gather_small = _all_gather_exchange(jnp.stack(slab_rows))
            sems_s, slab, slabs, token = _start_exchange(gather_small, "all_gather_small_grads_start")
            after = [token]
        hosted = _sibling_exchange(grads_above, ALL_KINDS) if grads_above is not None else None
        gwin, gwout, *from_sibling = _weight_grads(l, h, dproj, ycat, dy, hosted, after)
        if l == 0:
            _, (slabs,) = _finish_exchange(
                gather_small, "all_gather_small_grads_finish", sems_s, slab, slabs, [gwin])
        if grads_above is not None:
            scatter(l + 1, grads_above, from_sibling, [])
        grads_above = [gwin, gwout, gwpool.reshape(POOL_ROWS, GROUP_D)]
    from_sibling = _run_exchange(_sibling_exchange(grads_above, ALL_KINDS), "grad_exchange_sibling_0")
    token = scatter(0, grads_above, from_sibling, [slabs])
    grad_x = dx[None]
    chips_0, sems_0, _ = in_flight.pop()

    total = _sum_sources(slabs)
    loss = total[0, SLAB_COLS]
    o = 3 * D_MODEL
    g_b_ada = total[:, :o]
    g_g_pre = total[:, o:o + D_MODEL]
    g_g_post = total[:, o + D_MODEL:o + 2 * D_MODEL]
    g_pool_scale = total[:, o + 2 * D_MODEL:o + 2 * D_MODEL + POOL_W]
    g_conv_full = total[:, o + 2 * D_MODEL + POOL_W:SLAB_COLS].reshape(DEPTH, 3, CONV_W)
    g_w_conv = lax.dynamic_slice_in_dim(g_conv_full, me * conv_shard, conv_shard, axis=2)

    dmod_all = slabs[:, :, :o].reshape(N_DEV, DEPTH, N_DEV, W_IN_SHARD)
    dmod_cols = lax.dynamic_index_in_dim(dmod_all, me, axis=2, keepdims=False).transpose(1, 0, 2) + token[0, 0]
    g_w_ada, d_w_ada, nm_w_ada, nv_w_ada = _adamw_w_ada(w_ada, m_w_ada, v_w_ada, c_act.T, dmod_cols)
    after = [nv_w_ada]
    for chips, sems, l in in_flight:
        partials, received = _finish_exchange(chips, f"grad_chips_finish_{l}", sems, partials, received, after)
        after = []
    upper = (1, DEPTH)
    w_in_upper = _adamw_reduced(
        "adamw_w_in_upper", w_in, m_w_in, v_w_in, partials[0], received[0], chip, ROW_TILE, upper, None)
    w_out_upper = _adamw_reduced(
        "adamw_w_out_upper", w_out, m_w_out, v_w_out, partials[1], received[1], chip, W_OUT_SHARD, upper, None)

    partials, received = _finish_exchange(
        chips_0, "grad_chips_finish_0", sems_0, partials, received, [w_in_upper[3], w_out_upper[3]])
    gather_pool = _all_gather_exchange(_reduce_w_pool(partials[2], received[2], chip))
    sems_p, pool_rows, pool_landing, token_p = _start_exchange(gather_pool, "all_gather_grad_w_pool_start")
    g_w_in, d_w_in, nm_w_in, nv_w_in = _adamw_reduced(
        "adamw_w_in_0", w_in, m_w_in, v_w_in, partials[0], received[0], chip, ROW_TILE, (0, 1), w_in_upper)
    g_w_out, d_w_out, nm_w_out, nv_w_out = _adamw_reduced(
        "adamw_w_out_0", w_out, m_w_out, v_w_out, partials[1], received[1], chip, W_OUT_SHARD, (0, 1), w_out_upper)
    _, (g_pool_all,) = _finish_exchange(
        gather_pool, "all_gather_grad_w_pool_finish", sems_p, pool_rows, pool_landing, [nv_w_in, nv_w_out])
    g_w_pool = g_pool_all.reshape(N_DEV, DEPTH, POOL_SHARD, GROUP_D).transpose(1, 0, 2, 3).reshape(w_pool.shape)

    flat2 = lambda a: a.reshape(-1, a.shape[-1])
    small = _adamw_small([
        (b_ada, g_b_ada, m_b_ada, v_b_ada),
        (g_pre, g_g_pre, m_g_pre, v_g_pre),
        (flat2(w_conv), flat2(g_w_conv), flat2(m_w_conv), flat2(v_w_conv)),
        (flat2(w_pool), flat2(g_w_pool), flat2(m_w_pool), flat2(v_w_pool)),
        (pool_scale, g_pool_scale, m_pool_scale, v_pool_scale),
        (g_post, g_g_post, m_g_post, v_g_post),
    ])
    (d_b_ada, nm_b_ada, nv_b_ada), (d_g_pre, nm_g_pre, nv_g_pre), conv_upd, pool_upd, \
        (d_ps, nm_ps, nv_ps), (d_g_post, nm_g_post, nv_g_post) = small
    d_w_conv, nm_w_conv, nv_w_conv = (a.reshape(w_conv.shape) for a in conv_upd)
    d_w_pool, nm_w_pool, nv_w_pool = (a.reshape(w_pool.shape) for a in pool_upd)

    return (loss, grad_x,
            g_w_ada, g_b_ada, g_g_pre, g_w_in, g_w_conv, g_w_pool, g_pool_scale, g_w_out, g_g_post,
            d_w_ada, d_b_ada, d_g_pre, d_w_in, d_w_conv, d_w_pool, d_ps, d_w_out, d_g_post,
            nm_w_ada, nm_b_ada, nm_g_pre, nm_w_in, nm_w_conv, nm_w_pool, nm_ps, nm_w_out, nm_g_post,
            nv_w_ada, nv_b_ada, nv_g_pre, nv_w_in, nv_w_conv, nv_w_pool, nv_ps, nv_w_out, nv_g_post)
```

```python
from functools import partial

import jax
import jax.numpy as jnp
from jax import lax
from jax.experimental import pallas as pl
from jax.experimental.pallas import tpu as pltpu

F32 = jnp.float32
BF16 = jnp.bfloat16

D_MODEL = 1024
DEPTH = 4
CONV_W = 512
POOL_W = 512
POOL_WINDOWS = (2, 4, 8, 16)
GROUP_D = 128
IN_COLS = 4 * CONV_W + 2 * POOL_W
NORM_EPS = 1e-6

ADAM_LR = 0.001
ADAM_B1 = 0.9
ADAM_B2 = 0.999
ADAM_EPS = 1e-08
ADAM_WD = 0.01
ADAM_STEP = 10

N_DEV = 8
N_CHIP = 4
N_OTHER_CHIPS = N_CHIP - 1
MESH = pl.DeviceIdType.MESH
W_IN_SHARD = IN_COLS // N_DEV
W_OUT_SHARD = D_MODEL // N_DEV
POOL_ROWS = len(POOL_WINDOWS) * GROUP_D
POOL_SHARD = POOL_ROWS // N_DEV

SUBLANES = 8
LANES = 128
VMEM_LIMIT_BYTES = 56 * 1024 * 1024
ROW_TILE = 512
BWD_TILE = 256
GWIN_COLS = 768
GWOUT_COLS = 512
POOL_HALO = 16
CONV_HALO = SUBLANES

SLAB_COLS = 3 * D_MODEL + D_MODEL + D_MODEL + POOL_W + 3 * CONV_W

HBM = pl.BlockSpec(memory_space=pl.ANY)


def _params(**kw):
    return pltpu.CompilerParams(vmem_limit_bytes=VMEM_LIMIT_BYTES, **kw)


def _sigmoid(v):
    return 1.0 / (1.0 + jnp.exp(-v))


def _dot(a, b):
    return jnp.dot(a, b, preferred_element_type=F32)


def _dot_tn(a, b):
    return lax.dot_general(a, b, (((0,), (0,)), ((), ())), preferred_element_type=F32)


def _dot_nt(a, b):
    return lax.dot_general(a, b, (((1,), (1,)), ((), ())), preferred_element_type=F32)


def _rows_from_before(v, k):
    return pltpu.roll(v, k, 0)


def _rows_from_after(v, k):
    return pltpu.roll(v, v.shape[0] - k, 0)


def _window_counts(t0, rows):
    return (lax.broadcasted_iota(jnp.int32, (rows, 1), 0) + (t0 + 1)).astype(F32)


def _split_proj(p32):
    cw = CONV_W
    return (p32[:, 0 * cw:1 * cw], p32[:, 1 * cw:2 * cw], p32[:, 2 * cw:3 * cw], p32[:, 3 * cw:4 * cw],
            p32[:, 4 * cw:4 * cw + POOL_W], p32[:, 4 * cw + POOL_W:])


def _layer_spec(shape, layer):
    nd = len(shape)
    return pl.BlockSpec((None,) + tuple(shape[1:]), lambda i, _l=layer, _n=nd: (_l,) + (0,) * (_n - 1))


def _whole_spec(shape):
    return pl.BlockSpec(tuple(shape), lambda i, _n=len(shape): (0,) * _n, pipeline_mode=pl.Buffered(1))


def _mesh_position():
    return lax.axis_index("x"), lax.axis_index("y"), lax.axis_index("c")


def _block_id(x, y, c):
    return 4 * x + 2 * y + c


def _other_chips(x, y):
    return [(x ^ 1, y), (x, y ^ 1), (x ^ 1, y ^ 1)]


def _col_block(ref, blk):
    return ref.at[:, pl.ds(pl.multiple_of(blk * W_IN_SHARD, LANES), W_IN_SHARD)]


def _row_block(rows):
    def block(ref, blk):
        return ref.at[pl.ds(pl.multiple_of(blk * rows, rows), rows), :]
    return block


_BLOCK_OF = (_col_block, _row_block(W_OUT_SHARD), _row_block(POOL_SHARD))
_BLOCK_SHAPES = ((D_MODEL, W_IN_SHARD), (W_OUT_SHARD, D_MODEL), (POOL_SHARD, GROUP_D))


class _Exchange:
    def __init__(self, inputs, out_shapes, aliases, sem_shapes, make):
        self.inputs, self.out_shapes, self.aliases, self.sem_shapes, self.make = (
            list(inputs), list(out_shapes), dict(aliases), list(sem_shapes), make)


def _run_exchange(exchange, name):
    n_in, n_out = len(exchange.inputs), len(exchange.out_shapes)

    def body(*refs):
        start, finish = exchange.make(refs[:n_in], refs[n_in:n_in + n_out], refs[n_in + n_out:])
        start()
        finish()

    return pl.pallas_call(
        body, name=name, in_specs=[HBM] * n_in, out_specs=[HBM] * n_out, out_shape=exchange.out_shapes,
        scratch_shapes=exchange.sem_shapes, input_output_aliases=exchange.aliases, compiler_params=_params(),
    )(*exchange.inputs)


_SEM = pl.BlockSpec(memory_space=pltpu.SEMAPHORE)
_DATAFLOW = pltpu.SideEffectType.DATAFLOW_SIDE_EFFECTING


def _start_exchange(exchange, name, after=()):
    n_in, n_out, n_sem = len(exchange.inputs), len(exchange.out_shapes), len(exchange.sem_shapes)
    sources = [i for i in range(n_in) if i not in exchange.aliases]
    aliases = {i: n_sem + k for k, i in enumerate(sources)}
    aliases.update({i: n_sem + len(sources) + o for i, o in exchange.aliases.items()})

    def body(*refs):
        in_refs = refs[:n_in]
        outs = refs[n_in + len(after):]
        sems = outs[:n_sem]
        out_refs = outs[n_sem + len(sources):n_sem + len(sources) + n_out]
        exchange.make(in_refs, out_refs, sems)[0]()
        refs[-1][...] = jnp.zeros_like(refs[-1])

    outs = pl.pallas_call(
        body, name=name, in_specs=[HBM] * (n_in + len(after)),
        out_specs=[_SEM] * n_sem + [HBM] * (len(sources) + n_out) + [pl.BlockSpec(memory_space=pltpu.VMEM)],
        out_shape=(exchange.sem_shapes + [pltpu.HBM(exchange.inputs[i].shape, exchange.inputs[i].dtype) for i in sources]
                   + [pltpu.HBM(s.shape, s.dtype) for s in exchange.out_shapes]
                   + [jax.ShapeDtypeStruct((SUBLANES, LANES), F32)]),
        input_output_aliases=aliases, compiler_params=_params(has_side_effects=_DATAFLOW),
    )(*exchange.inputs, *after)
    return outs[:n_sem], outs[n_sem:n_sem + len(sources)], outs[n_sem + len(sources):-1], outs[-1]


def _finish_exchange(exchange, name, sems, sources, landing, after):
    n_src, n_out, n_sem = len(sources), len(landing), len(sems)
    n_in = len(exchange.inputs)
    source_at = [i for i in range(n_in) if i not in exchange.aliases]

    def body(*refs):
        src_refs, out_refs = refs[:n_src], refs[n_src:n_src + n_out]
        sem_refs = refs[n_src + n_out:n_src + n_out + n_sem]
        in_refs = [None] * n_in
        for k, i in enumerate(source_at):
            in_refs[i] = src_refs[k]
        for i, o in exchange.aliases.items():
            in_refs[i] = out_refs[o]
        exchange.make(in_refs, out_refs, sem_refs)[1]()

    arrays = list(sources) + list(landing)
    outs = pl.pallas_call(
        body, name=name, in_specs=[HBM] * len(arrays) + [_SEM] * n_sem + [HBM] * len(after),
        out_specs=[HBM] * len(arrays), out_shape=[pltpu.HBM(a.shape, a.dtype) for a in arrays],
        input_output_aliases={i: i for i in range(len(arrays))}, compiler_params=_params(has_side_effects=_DATAFLOW),
    )(*arrays, *sems, *after)
    return outs[:n_src], outs[n_src:]


N_GATHERED = 2
FIRST_COPIES = 1 + N_OTHER_CHIPS
_GATHERED_SHAPES = ((D_MODEL, IN_COLS), (D_MODEL, D_MODEL))
ALL_OTHER_CHIPS, NEIGHBOUR_CHIPS, DIAGONAL_CHIP = (0, 1, 2), (0, 1), (2,)


def _first_sem(layer, a, k):
    return (layer * N_GATHERED + a) * FIRST_COPIES + k


def _first_arrival(layer, a, j):
    return _first_sem(layer, a, 1 + j)


def _gather_copy(window_of, full_ref, blk, send_sem, recv_sem, to, src=None):
    window = window_of(full_ref, blk)
    return pltpu.make_async_remote_copy(
        src_ref=window if src is None else src, dst_ref=window, send_sem=send_sem, recv_sem=recv_sem,
        device_id=to, device_id_type=MESH)


def _first_copies(layer, shard_refs, full_refs, send_sems, recv_sems, local_sems, relayed):
    x, y, c = _mesh_position()
    me = _block_id(x, y, c)
    chips = [_other_chips(x, y)[j] for j in (NEIGHBOUR_CHIPS if relayed else ALL_OTHER_CHIPS)]
    own, remote = [], []
    for a in range(N_GATHERED):
        shard = shard_refs[a].at[layer]
        own.append(pltpu.make_async_copy(
            shard, _BLOCK_OF[a](full_refs[a], me), local_sems.at[layer * N_GATHERED + a]))
        targets = [(x, y, 1 - c)] + [(*chip, c) for chip in chips]
        remote += [_gather_copy(_BLOCK_OF[a], full_refs[a], me, send_sems.at[_first_sem(layer, a, k)],
                                recv_sems.at[_first_sem(layer, a, k)], to, src=shard)
                   for k, to in enumerate(targets)]
    return own, remote


def _gather_weights_start(first_layer, win_shards, wout_shards, after, relayed=False):
    n_layers = win_shards.shape[0]
    n_first = n_layers * N_GATHERED * FIRST_COPIES
    sem_shapes = [pltpu.SemaphoreType.DMA((n_first,)), pltpu.SemaphoreType.DMA((n_first,)),
                  pltpu.SemaphoreType.DMA((n_layers * N_GATHERED,))]
    shards = [win_shards, wout_shards]

    def body(win_sh, wout_sh, *rest):
        send_sems, recv_sems, local_sems, win_thru, wout_thru, *landing = rest[len(after):]
        for layer in range(n_layers):
            own, remote = _first_copies(layer, (win_sh, wout_sh), landing[N_GATHERED * layer:N_GATHERED * (layer + 1)],
                                        send_sems, recv_sems, local_sems, relayed)
            for cp in own + remote:
                cp.start()

    outs = pl.pallas_call(
        body, name=f"all_gather_weights_start_{first_layer}", in_specs=[HBM] * (2 + len(after)),
        out_specs=[_SEM] * 3 + [HBM] * (2 + n_layers * N_GATHERED),
        out_shape=(sem_shapes + [pltpu.HBM(s.shape, s.dtype) for s in shards]
                   + [pltpu.HBM(s, BF16) for _ in range(n_layers) for s in _GATHERED_SHAPES]),
        input_output_aliases={0: 3, 1: 4}, compiler_params=_params(has_side_effects=_DATAFLOW),
    )(*shards, *after)
    landing = outs[5:]
    return outs[:3], outs[3:5], [landing[N_GATHERED * l:N_GATHERED * (l + 1)] for l in range(n_layers)]


def _passed_on_copies(full_refs, send_sems, recv_sems, core_of_block, chips):
    x, y, c = _mesh_position()
    return [_gather_copy(_BLOCK_OF[a], full_refs[a], _block_id(*_other_chips(x, y)[j], core_of_block),
                         send_sems.at[a * N_OTHER_CHIPS + j], recv_sems.at[a * N_OTHER_CHIPS + j], (x, y, 1 - c))
            for a in range(N_GATHERED) for j in chips]


def _relayed_copies(full_refs, send_sems, recv_sems):
    x, y, c = _mesh_position()
    source, to = (x ^ (1 - c), y ^ c), (x ^ c, y ^ (1 - c))
    return [_gather_copy(_BLOCK_OF[a], full_refs[a], _block_id(*source, c), send_sems.at[a], recv_sems.at[a], (*to, c))
            for a in range(N_GATHERED)]


def _gather_weights_pass_on(name, chips, arrival_sems, arrival_sem_of, landing, after, relay=False):
    n = N_GATHERED * N_OTHER_CHIPS
    sem_shapes = [pltpu.SemaphoreType.DMA((n,))] * 2 + [pltpu.SemaphoreType.DMA((N_GATHERED,))] * (2 if relay else 0)

    def body(win_ref, wout_ref, arrivals, *rest):
        sems = rest[len(after):len(after) + len(sem_shapes)]
        x, y, c = _mesh_position()
        full_refs = (win_ref, wout_ref)
        passed = _passed_on_copies(full_refs, sems[0], sems[1], c, chips)
        relayed = _relayed_copies(full_refs, sems[2], sems[3]) if relay else []
        for a in range(N_GATHERED):
            for j in chips:
                sem = arrivals.at[arrival_sem_of(a, j)]
                _gather_copy(_BLOCK_OF[a], full_refs[a], _block_id(*_other_chips(x, y)[j], c), sem, sem,
                             (x, y, c)).wait_recv()
            for cp in relayed[a:a + 1] + passed[a * len(chips):(a + 1) * len(chips)]:
                cp.start()

    outs = pl.pallas_call(
        body, name=name, in_specs=[HBM] * N_GATHERED + [_SEM] + [HBM] * len(after),
        out_specs=[_SEM] * len(sem_shapes) + [HBM] * N_GATHERED,
        out_shape=sem_shapes + [pltpu.HBM(a.shape, a.dtype) for a in landing],
        input_output_aliases={a: len(sem_shapes) + a for a in range(N_GATHERED)},
        compiler_params=_params(has_side_effects=_DATAFLOW),
    )(*landing, arrival_sems, *after)
    return outs[:len(sem_shapes)], outs[len(sem_shapes):]


def _gather_weights_finish(layer, index, first_sems, passed, shards, landing, relay_send_sems=None):
    relayed = relay_send_sems is not None
    passed_sems = [sem for (send, recv), _ in passed for sem in (send, recv)] + ([relay_send_sems] if relayed else [])

    def body(win_ref, wout_ref, first_send, first_recv, local_sems, *rest):
        sems, (win_sh, wout_sh) = rest[:len(passed_sems)], rest[len(passed_sems):len(passed_sems) + 2]
        x, y, c = _mesh_position()
        full_refs = (win_ref, wout_ref)
        own, sent = _first_copies(index, (win_sh, wout_sh), full_refs, first_send, first_recv, local_sems, relayed)
        for a in range(N_GATHERED):
            sem = _first_sem(index, a, 0)
            _gather_copy(_BLOCK_OF[a], full_refs[a], _block_id(x, y, 1 - c), first_recv.at[sem], first_recv.at[sem],
                         (x, y, c)).wait_recv()
        for p, (_, chips) in enumerate(passed):
            for cp in _passed_on_copies(full_refs, sems[2 * p], sems[2 * p + 1], 1 - c, chips):
                cp.wait_recv()
            sent += _passed_on_copies(full_refs, sems[2 * p], sems[2 * p + 1], c, chips)
        if relayed:
            sent += _relayed_copies(full_refs, sems[-1], sems[-1])
        for cp in sent:
            cp.wait_send()
        for cp in own:
            cp.wait()

    return pl.pallas_call(
        body, name=f"all_gather_weights_finish_{layer}",
        in_specs=[HBM] * N_GATHERED + [_SEM] * (3 + len(passed_sems)) + [HBM] * 2,
        out_specs=[HBM] * N_GATHERED, out_shape=[pltpu.HBM(a.shape, a.dtype) for a in landing],
        input_output_aliases={a: a for a in range(N_GATHERED)}, compiler_params=_params(has_side_effects=_DATAFLOW),
    )(*landing, *first_sems, *passed_sems, *shards)


ALL_KINDS = (0, 1, 2)


def _sibling_exchange(grads, kinds):
    n_arr = len(grads)

    def make(in_refs, out_refs, sems):
        send_sems, recv_sems = sems
        x, y, c = _mesh_position()
        copies = [pltpu.make_async_remote_copy(
            src_ref=_BLOCK_OF[kinds[a]](in_refs[a], 2 * q + (1 - c)), dst_ref=out_refs[a].at[q],
            send_sem=send_sems.at[a * N_CHIP + q], recv_sem=recv_sems.at[a * N_CHIP + q],
            device_id=(x, y, 1 - c), device_id_type=MESH)
            for a in range(n_arr) for q in range(N_CHIP)]

        def start():
            for cp in copies:
                cp.start()

        def finish():
            for cp in copies:
                cp.wait_recv()
            for cp in copies:
                cp.wait_send()

        return start, finish

    return _Exchange(
        grads, [jax.ShapeDtypeStruct((N_CHIP,) + _BLOCK_SHAPES[k], BF16) for k in kinds], {},
        [pltpu.SemaphoreType.DMA((n_arr * N_CHIP,)), pltpu.SemaphoreType.DMA((n_arr * N_CHIP,))], make)


def _chips_exchange(layer, partials, received, kinds):
    n_arr = len(partials)

    def make(in_refs, out_refs, sems):
        send_sems, recv_sems = sems
        x, y, c = _mesh_position()
        copies = [pltpu.make_async_remote_copy(
            src_ref=in_refs[a].at[2 * qx + qy, layer], dst_ref=out_refs[a].at[j, layer],
            send_sem=send_sems.at[a * N_OTHER_CHIPS + j], recv_sem=recv_sems.at[a * N_OTHER_CHIPS + j],
            device_id=(qx, qy, c), device_id_type=MESH)
            for a in range(n_arr) for j, (qx, qy) in enumerate(_other_chips(x, y))]

        def start():
            for cp in copies:
                cp.start()

        def finish():
            for cp in copies:
                cp.wait_recv()
            for cp in copies:
                cp.wait_send()

        return start, finish

    inputs = list(partials)
    aliases = {}
    if received is not None:
        inputs += list(received)
        aliases = {n_arr + a: a for a in range(n_arr)}
    return _Exchange(
        inputs, [jax.ShapeDtypeStruct((N_OTHER_CHIPS, DEPTH) + _BLOCK_SHAPES[k], BF16) for k in kinds], aliases,
        [pltpu.SemaphoreType.DMA((n_arr * N_OTHER_CHIPS,)), pltpu.SemaphoreType.DMA((n_arr * N_OTHER_CHIPS,))], make)


def _all_gather_exchange(v, axis=0):
    def make(in_refs, out_refs, sems):
        send_sems, recv_sems, local_sem = sems
        x, y, c = _mesh_position()
        me = _block_id(x, y, c)
        block = lambda blk: out_refs[0].at[(slice(None),) * axis + (blk,)]
        own = pltpu.make_async_copy(in_refs[0], block(me), local_sem.at[0])
        sends, arrivals = [], []
        for k in range(1, N_DEV):
            px, py, pc = x ^ ((k >> 2) & 1), y ^ ((k >> 1) & 1), c ^ (k & 1)
            sends.append(pltpu.make_async_remote_copy(
                src_ref=in_refs[0], dst_ref=block(me), send_sem=send_sems.at[k - 1],
                recv_sem=recv_sems.at[k - 1], device_id=(px, py, pc), device_id_type=MESH))
            arrivals.append(pltpu.make_async_remote_copy(
                src_ref=in_refs[0], dst_ref=block(_block_id(px, py, pc)), send_sem=send_sems.at[k - 1],
                recv_sem=recv_sems.at[k - 1], device_id=(x, y, c), device_id_type=MESH))

        def start():
            for cp in [own] + sends:
                cp.start()

        def finish():
            for cp in arrivals:
                cp.wait_recv()
            for cp in sends:
                cp.wait_send()
            own.wait()

        return start, finish

    return _Exchange(
        [v], [jax.ShapeDtypeStruct(v.shape[:axis] + (N_DEV,) + v.shape[axis:], v.dtype)], {},
        [pltpu.SemaphoreType.DMA((N_DEV - 1,)), pltpu.SemaphoreType.DMA((N_DEV - 1,)),
         pltpu.SemaphoreType.DMA((1,))], make)


def _host(exchange, args, in_specs, out_shape, out_specs, scratch):
    n_own = (len(args), len(out_shape), len(scratch))
    if exchange is None:
        return {}, lambda refs: (refs, None)
    n_ex = (len(exchange.inputs), len(exchange.out_shapes), len(exchange.sem_shapes))
    aliases = {n_own[0] + i: n_own[1] + o for i, o in exchange.aliases.items()}
    args += exchange.inputs
    in_specs += [HBM] * n_ex[0]
    out_shape += exchange.out_shapes
    out_specs += [HBM] * n_ex[1]
    scratch += exchange.sem_shapes

    def split(refs):
        own, theirs, at = [], [], 0
        for mine, ex in zip(n_own, n_ex):
            own += refs[at:at + mine]
            theirs.append(refs[at + mine:at + mine + ex])
            at += mine + ex
        return own, exchange.make(*theirs)

    return aliases, split


def _all_gather_small(v, name, after=()):
    vmem = pl.BlockSpec(memory_space=pltpu.VMEM)

    def body(v_ref, *rest):
        out_ref, send_sems, recv_sems = rest[len(after):]
        x, y, c = _mesh_position()
        me = _block_id(x, y, c)
        out_ref[me] = v_ref[...]
        sends = []
        for k in range(1, N_DEV):
            px, py, pc = x ^ ((k >> 2) & 1), y ^ ((k >> 1) & 1), c ^ (k & 1)
            send = pltpu.make_async_remote_copy(
                src_ref=v_ref, dst_ref=out_ref.at[me], send_sem=send_sems.at[k - 1], recv_sem=recv_sems.at[k - 1],
                device_id=(px, py, pc), device_id_type=MESH)
            send.start()
            sends.append((send, _block_id(px, py, pc)))
        for k, (send, peer) in enumerate(sends):
            pltpu.make_async_remote_copy(
                src_ref=v_ref, dst_ref=out_ref.at[peer], send_sem=send_sems.at[k], recv_sem=recv_sems.at[k],
                device_id=(x, y, c), device_id_type=MESH).wait_recv()
        for send, _ in sends:
            send.wait_send()

    return pl.pallas_call(
        body, name=name, in_specs=[vmem] + [HBM] * len(after), out_specs=vmem,
        out_shape=jax.ShapeDtypeStruct((N_DEV,) + v.shape, v.dtype),
        scratch_shapes=[pltpu.SemaphoreType.DMA((N_DEV - 1,)), pltpu.SemaphoreType.DMA((N_DEV - 1,))],
        compiler_params=_params(),
    )(v, *after)


def _forward_layer(layer, x, vec, cps, wpool, win, wout, target=None):
    t_len = x.shape[0]
    n_tiles = t_len // ROW_TILE
    row = lambda cols: pl.BlockSpec((ROW_TILE, cols), lambda i: (i, 0))
    widths = (D_MODEL, 2 * CONV_W, 3 * CONV_W, 4 * POOL_W)
    head = target is not None

    def body(x_ref, vec_ref, cps_ref, wpool_ref, win_ref, wout_ref, *rest):
        target_ref = rest[0] if head else None
        xo_ref, y_ref, h_ref, ycat_ref, uc_ref, fa_ref, fp_ref = rest[head:head + 7]
        loss_ref = rest[head + 7] if head else None
        zc_ref, pc_ref = rest[-2:]
        i = pl.program_id(0)

        @pl.when(i == 0)
        def _():
            zc_ref[...] = jnp.zeros_like(zc_ref)
            pc_ref[...] = jnp.zeros_like(pc_ref)
            if head:
                loss_ref[...] = jnp.zeros_like(loss_ref)

        x_t = x_ref[...]
        shift, scale, gate = vec_ref[0:1, :], vec_ref[1:2, :], vec_ref[2:3, :]
        g_pre, g_post = vec_ref[3:4, :], vec_ref[4:5, :]
        w0, w1, w2, ps = cps_ref[0:1, :], cps_ref[1:2, :], cps_ref[2:3, :], cps_ref[3:4, :]
        rx = lax.rsqrt(jnp.mean(x_t * x_t, axis=-1, keepdims=True) + NORM_EPS)
        h = (x_t * rx) * g_pre * (1.0 + scale) + shift
        h_ref[...] = h.astype(BF16)
        proj = _dot(h.astype(BF16), win_ref[...])
        u_a, b_a, c_a, g_a, u_p, g_p = _split_proj(proj)
        uc_ref[...] = jnp.concatenate([u_a, c_a], axis=1).astype(BF16)

        z = c_a * u_a
        zcat = jnp.concatenate([zc_ref[...], z], axis=0)
        zc_ref[...] = z[ROW_TILE - CONV_HALO:]
        conv = (w0 * _rows_from_before(zcat, 2)[CONV_HALO:] + w1 * _rows_from_before(zcat, 1)[CONV_HALO:] + w2 * z)
        sig_a = _sigmoid(g_a)
        silu_a = g_a * sig_a
        b_conv = b_a * conv
        y_a = b_conv * silu_a
        fa_ref[...] = jnp.concatenate(
            [silu_a * conv, silu_a * b_a, b_conv * (sig_a + silu_a * (1.0 - sig_a))], axis=1).astype(BF16)

        pcat = jnp.concatenate([pc_ref[...], u_p], axis=0)
        pc_ref[...] = u_p[ROW_TILE - POOL_HALO:]
        counts = _window_counts(i * ROW_TILE, ROW_TILE)
        pooled, mixed = [], []
        for g, w in enumerate(POOL_WINDOWS):
            cols = slice(g * GROUP_D, (g + 1) * GROUP_D)
            s = pcat[:, cols]
            step = 1
            while step < w:
                s = s + _rows_from_before(s, step)
                step *= 2
            pooled_g = (s[POOL_HALO:] * (1.0 / jnp.minimum(counts, float(w))) - u_p[:, cols]).astype(BF16)
            pooled.append(pooled_g)
            mixed.append(_dot(pooled_g, wpool_ref[g]))
        mixed = jnp.concatenate(mixed, axis=1)
        sig_p = _sigmoid(g_p)
        silu_p = g_p * sig_p
        mixed_ps = mixed * ps
        y_p = mixed_ps * silu_p
        fp_ref[...] = jnp.concatenate(
            [(ps * silu_p).astype(BF16), (mixed_ps * (sig_p + silu_p * (1.0 - sig_p))).astype(BF16),
             (silu_p * mixed).astype(BF16)] + pooled, axis=1)

        ycat = jnp.concatenate([y_a, y_p], axis=1)
        ycat_ref[...] = ycat.astype(BF16)
        y_b = _dot(ycat.astype(BF16), wout_ref[...]).astype(BF16)
        y_ref[...] = y_b
        y_t = y_b.astype(F32)
        ry = lax.rsqrt(jnp.mean(y_t * y_t, axis=-1, keepdims=True) + NORM_EPS)
        x_next = x_t + gate * (y_t * ry * g_post)
        if head:
            err = x_next - target_ref[...]
            xo_ref[...] = err * (1.0 / D_MODEL)
            loss_ref[...] += jnp.sum(err * err) * (0.5 / D_MODEL)
        else:
            xo_ref[...] = x_next

    tile = (SUBLANES, LANES)
    return pl.pallas_call(
        body, name=f"forward_layer_{layer}", grid=(n_tiles,),
        in_specs=[row(D_MODEL), _layer_spec(vec.shape, layer), _layer_spec(cps.shape, layer),
                  _layer_spec(wpool.shape, layer), _whole_spec(win.shape), _whole_spec(wout.shape)]
        + [row(D_MODEL)] * head,
        out_specs=[row(D_MODEL), row(D_MODEL), row(D_MODEL), row(D_MODEL)] + [row(w) for w in widths[1:]]
        + [_whole_spec(tile)] * head,
        out_shape=[jax.ShapeDtypeStruct((t_len, D_MODEL), F32), jax.ShapeDtypeStruct((t_len, D_MODEL), BF16),
                   jax.ShapeDtypeStruct((t_len, D_MODEL), BF16), jax.ShapeDtypeStruct((t_len, D_MODEL), BF16)]
        + [jax.ShapeDtypeStruct((t_len, w), BF16) for w in widths[1:]] + [jax.ShapeDtypeStruct(tile, F32)] * head,
        scratch_shapes=[pltpu.VMEM((CONV_HALO, CONV_W), F32), pltpu.VMEM((POOL_HALO, POOL_W), F32)],
        compiler_params=_params(dimension_semantics=("arbitrary",)),
    )(x, vec, cps, wpool, win, wout, *([target] * head))


def _backward_layer(layer, dxo, y, x, uc, fa, fp, vec, cps, wpool, wout, win):
    t_len = dxo.shape[0]
    n_tiles = t_len // BWD_TILE
    halo_per_tile = BWD_TILE // POOL_HALO
    rev = lambda cols: pl.BlockSpec((BWD_TILE, cols), lambda i: (n_tiles - 1 - i, 0))
    halo_spec = pl.BlockSpec(
        (POOL_HALO, 2 * CONV_W), lambda i: (jnp.maximum((n_tiles - 1 - i) * halo_per_tile - 1, 0), 0))
    gwpool_shape = (len(POOL_WINDOWS), GROUP_D, GROUP_D)

    def body(dxo_ref, y_ref, x_ref, uc_ref, uch_ref, fa_ref, fp_ref, vec_ref, cps_ref, wpool_ref, wout_ref, win_ref,
             dx_ref, dproj_ref, dy_ref, gwpool_ref, dcps_ref, dvec_ref, gwpool_acc, dcc_ref, qc_ref):
        i = pl.program_id(0)
        tile = n_tiles - 1 - i

        @pl.when(i == 0)
        def _():
            gwpool_acc[...] = jnp.zeros_like(gwpool_acc)
            dcps_ref[...] = jnp.zeros_like(dcps_ref)
            dvec_ref[...] = jnp.zeros_like(dvec_ref)
            dcc_ref[...] = jnp.zeros_like(dcc_ref)
            qc_ref[...] = jnp.zeros_like(qc_ref)

        shift, scale, gate = vec_ref[0:1, :], vec_ref[1:2, :], vec_ref[2:3, :]
        g_pre, g_post = vec_ref[3:4, :], vec_ref[4:5, :]
        w0, w1, w2 = cps_ref[0:1, :], cps_ref[1:2, :], cps_ref[2:3, :]

        dxo_t = dxo_ref[...]
        y_t = y_ref[...].astype(F32)
        ry = lax.rsqrt(jnp.mean(y_t * y_t, axis=-1, keepdims=True) + NORM_EPS)
        yh = y_t * ry
        dvec_ref[2:3, :] += jnp.sum(dxo_t * yh, axis=0, keepdims=True)
        dyh = dxo_t * (gate * g_post)
        dy_b = (ry * (dyh - yh * jnp.mean(dyh * yh, axis=-1, keepdims=True))).astype(BF16)
        dy_ref[...] = dy_b
        dycat = _dot_nt(dy_b, wout_ref[...])
        dy_a, dy_p = dycat[:, :CONV_W], dycat[:, CONV_W:]

        fa_t = fa_ref[...].astype(F32)
        db_a = dy_a * fa_t[:, :CONV_W]
        dconv = dy_a * fa_t[:, CONV_W:2 * CONV_W]
        dg_a = dy_a * fa_t[:, 2 * CONV_W:]
        uc_t = uc_ref[...].astype(F32)
        u_a, c_a = uc_t[:, :CONV_W], uc_t[:, CONV_W:]
        halo = jnp.where(tile > 0, uch_ref[...].astype(F32), 0.0)[POOL_HALO - CONV_HALO:]
        z = c_a * u_a
        zcat = jnp.concatenate([halo[:, CONV_W:] * halo[:, :CONV_W], z], axis=0)
        z1 = _rows_from_before(zcat, 1)[CONV_HALO:]
        z2 = _rows_from_before(zcat, 2)[CONV_HALO:]
        dccat = jnp.concatenate([dconv, dcc_ref[...]], axis=0)
        dc1 = _rows_from_after(dccat, 1)[:BWD_TILE]
        dc2 = _rows_from_after(dccat, 2)[:BWD_TILE]
        dz = w2 * dconv + w1 * dc1 + w0 * dc2
        dcc_ref[...] = dconv[:CONV_HALO]
        dcps_ref[0:1, :] += jnp.sum(dconv * z2, axis=0, keepdims=True)
        dcps_ref[1:2, :] += jnp.sum(dconv * z1, axis=0, keepdims=True)
        dcps_ref[2:3, :] += jnp.sum(dconv * z, axis=0, keepdims=True)
        du_a = dz * c_a
        dc_a = dz * u_a

        dmixed = (dy_p * fp_ref[:, :POOL_W].astype(F32)).astype(BF16)
        dg_p = dy_p * fp_ref[:, POOL_W:2 * POOL_W].astype(F32)
        dcps_ref[3:4, :] += jnp.sum(dy_p * fp_ref[:, 2 * POOL_W:3 * POOL_W].astype(F32), axis=0, keepdims=True)
        counts = _window_counts(tile * BWD_TILE, BWD_TILE)
        du_p, q_head = [], []
        for g, w in enumerate(POOL_WINDOWS):
            cols = slice(g * GROUP_D, (g + 1) * GROUP_D)
            dm_g = dmixed[:, cols]
            dpooled_g = _dot_nt(dm_g, wpool_ref[g])
            gwpool_acc[g] += _dot_tn(fp_ref[:, 3 * POOL_W + g * GROUP_D:3 * POOL_W + (g + 1) * GROUP_D], dm_g)
            q_g = dpooled_g * (1.0 / jnp.minimum(counts, float(w)))
            q_head.append(q_g[:POOL_HALO])
            s = jnp.concatenate([q_g, qc_ref[:, cols]], axis=0)
            step = 1
            while step < w:
                s = s + _rows_from_after(s, step)
                step *= 2
            du_p.append(s[:BWD_TILE] - dpooled_g)
        qc_ref[...] = jnp.concatenate(q_head, axis=1)
        dproj_b = jnp.concatenate([du_a, db_a, dc_a, dg_a] + du_p + [dg_p], axis=1).astype(BF16)
        dproj_ref[...] = dproj_b

        x_t = x_ref[...]
        rx = lax.rsqrt(jnp.mean(x_t * x_t, axis=-1, keepdims=True) + NORM_EPS)
        xn = x_t * rx
        mod_scale = 1.0 + scale
        dh = _dot_nt(dproj_b, win_ref[...])
        dvec_ref[0:1, :] += jnp.sum(dh, axis=0, keepdims=True)
        dvec_ref[1:2, :] += jnp.sum(dh * xn, axis=0, keepdims=True)
        dxn = dh * (g_pre * mod_scale)
        dx_ref[...] = dxo_t + rx * (dxn - xn * jnp.mean(dxn * xn, axis=-1, keepdims=True))

        @pl.when(i == n_tiles - 1)
        def _():
            gwpool_ref[...] = gwpool_acc[...].astype(BF16)
            sum_dh_xn, sum_dxo_yh = dvec_ref[1:2, :], dvec_ref[2:3, :]
            dvec_ref[1:2, :] = sum_dh_xn * g_pre
            dvec_ref[3:4, :] = sum_dh_xn * mod_scale
            dvec_ref[2:3, :] = sum_dxo_yh * g_post
            dvec_ref[4:5, :] = sum_dxo_yh * gate

    return pl.pallas_call(
        body, name=f"backward_layer_{layer}", grid=(n_tiles,),
        in_specs=[rev(D_MODEL), rev(D_MODEL), rev(D_MODEL), rev(2 * CONV_W), halo_spec, rev(3 * CONV_W),
                  rev(4 * POOL_W), _layer_spec(vec.shape, layer), _layer_spec(cps.shape, layer),
                  _layer_spec(wpool.shape, layer), _whole_spec(wout.shape), _whole_spec(win.shape)],
        out_specs=[rev(D_MODEL), rev(IN_COLS), rev(D_MODEL), _whole_spec(gwpool_shape),
                   _whole_spec((SUBLANES, CONV_W)), _whole_spec((SUBLANES, D_MODEL))],
        out_shape=[jax.ShapeDtypeStruct((t_len, D_MODEL), F32), jax.ShapeDtypeStruct((t_len, IN_COLS), BF16),
                   jax.ShapeDtypeStruct((t_len, D_MODEL), BF16), jax.ShapeDtypeStruct(gwpool_shape, BF16),
                   jax.ShapeDtypeStruct((SUBLANES, CONV_W), F32), jax.ShapeDtypeStruct((SUBLANES, D_MODEL), F32)],
        scratch_shapes=[pltpu.VMEM(gwpool_shape, F32), pltpu.VMEM((CONV_HALO, CONV_W), F32),
                        pltpu.VMEM((POOL_HALO, POOL_W), F32)],
        compiler_params=_params(dimension_semantics=("arbitrary",)),
    )(dxo, y, x, uc, uc, fa, fp, vec, cps, wpool, wout, win)


def _weight_grads(layer, h, dproj, ycat, dy, exchange, after=()):
    t_len = dy.shape[0]
    n_in, n_out = IN_COLS // GWIN_COLS, D_MODEL // GWOUT_COLS
    args = [h, dproj, ycat, dy, *after]
    in_specs = [_whole_spec(h.shape),
                pl.BlockSpec((t_len, GWIN_COLS), lambda s: (0, jnp.minimum(s, n_in - 1))),
                _whole_spec(ycat.shape),
                pl.BlockSpec((t_len, GWOUT_COLS), lambda s: (0, jnp.maximum(s - n_in, 0)))] + [HBM] * len(after)
    out_shape = [jax.ShapeDtypeStruct((D_MODEL, IN_COLS), BF16), jax.ShapeDtypeStruct((D_MODEL, D_MODEL), BF16)]
    out_specs = [pl.BlockSpec((D_MODEL, GWIN_COLS), lambda s: (0, jnp.minimum(s, n_in - 1))),
                 pl.BlockSpec((D_MODEL, GWOUT_COLS), lambda s: (0, jnp.maximum(s - n_in, 0)))]
    scratch = []
    aliases, split = _host(exchange, args, in_specs, out_shape, out_specs, scratch)

    def body(*refs):
        (h_ref, dproj_ref, ycat_ref, dy_ref, *_, gwin_ref, gwout_ref), hosted = split(refs)
        s = pl.program_id(0)
        if hosted is not None:
            pl.when(s == 0)(hosted[0])

        @pl.when(s < n_in)
        def _():
            gwin_ref[...] = _dot_tn(h_ref[...], dproj_ref[...]).astype(BF16)

        @pl.when(s >= n_in)
        def _():
            gwout_ref[...] = _dot_tn(ycat_ref[...], dy_ref[...]).astype(BF16)

        if hosted is not None:
            pl.when(s == n_in + n_out - 1)(hosted[1])

    return pl.pallas_call(
        body, name=f"weight_grads_{layer}", grid=(n_in + n_out,), in_specs=in_specs, out_specs=out_specs,
        out_shape=out_shape, scratch_shapes=scratch, input_output_aliases=aliases,
        compiler_params=_params(dimension_semantics=("arbitrary",)),
    )(*args)


def _add_sibling_blocks(name, layer, grads, received, core, partials, kinds):
    n_arr = len(grads)

    def body(core_ref, *refs):
        mine, theirs, outs = refs[:n_arr], refs[n_arr:2 * n_arr], refs[-n_arr:]
        for a in range(n_arr):
            outs[a][...] = (mine[a][...].astype(F32) + theirs[a][...].astype(F32)).astype(BF16)

    own_of_kind = [
        pl.BlockSpec((D_MODEL, W_IN_SHARD), lambda q, core_ref: (0, 2 * q + core_ref[0])),
        pl.BlockSpec((W_OUT_SHARD, D_MODEL), lambda q, core_ref: (2 * q + core_ref[0], 0)),
        pl.BlockSpec((POOL_SHARD, GROUP_D), lambda q, core_ref: (2 * q + core_ref[0], 0)),
    ]
    shapes = [_BLOCK_SHAPES[k] for k in kinds]
    recv_specs = [pl.BlockSpec((None,) + s, lambda q, core_ref: (q, 0, 0)) for s in shapes]
    out_specs = [pl.BlockSpec((None, None) + s, lambda q, core_ref: (q, layer, 0, 0)) for s in shapes]
    args = [core, *grads, *received]
    in_specs = [own_of_kind[k] for k in kinds] + recv_specs
    aliases = {}
    if partials is not None:
        aliases = {len(args) + a: a for a in range(n_arr)}
        args += list(partials)
        in_specs += [HBM] * n_arr
    return pl.pallas_call(
        body, name=name,
        grid_spec=pltpu.PrefetchScalarGridSpec(
            num_scalar_prefetch=1, grid=(N_CHIP,), in_specs=in_specs, out_specs=out_specs),
        out_shape=[jax.ShapeDtypeStruct((N_CHIP, DEPTH) + s, BF16) for s in shapes],
        input_output_aliases=aliases,
        compiler_params=_params(dimension_semantics=("arbitrary",)),
    )(*args)


def _modulation_columns(c_all, w_ada):
    def body(c_ref, w_ref, cact_ref, out_ref):
        c_t = c_ref[...]
        c_act = c_t * _sigmoid(c_t)
        cact_ref[...] = c_act
        out_ref[...] = jnp.dot(c_act, w_ref[...], preferred_element_type=F32, precision=lax.Precision.HIGHEST)

    return pl.pallas_call(
        body, name="modulation_columns", grid=(DEPTH,),
        in_specs=[pl.BlockSpec((N_DEV, D_MODEL), lambda l: (0, 0)),
                  pl.BlockSpec((None, D_MODEL, W_IN_SHARD), lambda l: (l, 0, 0))],
        out_specs=[pl.BlockSpec((N_DEV, D_MODEL), lambda l: (0, 0)),
                   pl.BlockSpec((N_DEV, W_IN_SHARD), lambda l: (0, l))],
        out_shape=[jax.ShapeDtypeStruct((N_DEV, D_MODEL), F32),
                   jax.ShapeDtypeStruct((N_DEV, DEPTH * W_IN_SHARD), F32)],
        compiler_params=_params(dimension_semantics=("arbitrary",)),
    )(c_all, w_ada)


def _adamw(w, g, m, v):
    m_new = ADAM_B1 * m + (1.0 - ADAM_B1) * g
    v_new = ADAM_B2 * v + (1.0 - ADAM_B2) * (g * g)
    m_hat = m_new / (1.0 - ADAM_B1 ** ADAM_STEP)
    v_hat = v_new / (1.0 - ADAM_B2 ** ADAM_STEP)
    delta = -ADAM_LR * (m_hat / (jnp.sqrt(v_hat) + ADAM_EPS) + ADAM_WD * w)
    return delta, m_new, v_new


def _adamw_w_ada(w, m, v, c_act_t, dmod_cols):
    def body(w_ref, m_ref, v_ref, ct_ref, dm_ref, g_ref, d_ref, mo_ref, vo_ref):
        g = ct_ref[:, 0:1] * dm_ref[0:1, :]
        for b in range(1, N_DEV):
            g = g + ct_ref[:, b:b + 1] * dm_ref[b:b + 1, :]
        g_ref[...] = g
        d_ref[...], mo_ref[...], vo_ref[...] = _adamw(w_ref[...], g, m_ref[...], v_ref[...])

    big = pl.BlockSpec((None, D_MODEL, W_IN_SHARD), lambda l: (l, 0, 0))
    return pl.pallas_call(
        body, name="adamw_w_ada", grid=(DEPTH,),
        in_specs=[big, big, big, pl.BlockSpec((D_MODEL, N_DEV), lambda l: (0, 0)),
                  pl.BlockSpec((None, N_DEV, W_IN_SHARD), lambda l: (l, 0, 0))],
        out_specs=[big] * 4, out_shape=[jax.ShapeDtypeStruct(w.shape, F32)] * 4,
        compiler_params=_params(dimension_semantics=("arbitrary",)),
    )(w, m, v, c_act_t, dmod_cols)


def _sum_chip_partials(own_ref, recv_ref):
    g = own_ref[...].astype(F32)
    for j in range(N_OTHER_CHIPS):
        g = g + recv_ref[j].astype(F32)
    return g


def _partial_specs(row_tile, cols, first_layer=0):
    own = pl.BlockSpec((None, None, row_tile, cols), lambda l, r, chip_ref: (chip_ref[0], first_layer + l, r, 0))
    recv = pl.BlockSpec((N_OTHER_CHIPS, None, row_tile, cols), lambda l, r, chip_ref: (0, first_layer + l, r, 0))
    return own, recv


def _adamw_reduced(name, w, m, v, partial, received, chip, row_tile, layers, continued):
    depth, rows, cols = w.shape
    first, stop = layers

    def body(chip_ref, w_ref, m_ref, v_ref, own_ref, recv_ref, *rest):
        g_ref, d_ref, mo_ref, vo_ref = rest[-4:]
        g = _sum_chip_partials(own_ref, recv_ref)
        g_ref[...] = g
        d_ref[...], mo_ref[...], vo_ref[...] = _adamw(w_ref[...], g, m_ref[...], v_ref[...])

    blk = pl.BlockSpec((None, row_tile, cols), lambda l, r, chip_ref: (first + l, r, 0))
    args = [chip, w, m, v, partial, received]
    in_specs = [blk, blk, blk, *_partial_specs(row_tile, cols, first)]
    aliases = {}
    if continued is not None:
        aliases = {len(args) + k: k for k in range(4)}
        args += list(continued)
        in_specs += [HBM] * 4
    return pl.pallas_call(
        body, name=name,
        grid_spec=pltpu.PrefetchScalarGridSpec(
            num_scalar_prefetch=1, grid=(stop - first, rows // row_tile), in_specs=in_specs, out_specs=[blk] * 4),
        out_shape=[jax.ShapeDtypeStruct(w.shape, F32)] * 4, input_output_aliases=aliases,
        compiler_params=_params(dimension_semantics=("arbitrary", "arbitrary")),
    )(*args)


def _reduce_w_pool(partial, received, chip):
    def body(chip_ref, own_ref, recv_ref, g_ref):
        g_ref[...] = _sum_chip_partials(own_ref, recv_ref)

    return pl.pallas_call(
        body, name="reduce_w_pool",
        grid_spec=pltpu.PrefetchScalarGridSpec(
            num_scalar_prefetch=1, grid=(DEPTH, 1), in_specs=list(_partial_specs(POOL_SHARD, GROUP_D)),
            out_specs=pl.BlockSpec((None, POOL_SHARD, GROUP_D), lambda l, r, chip_ref: (l, 0, 0))),
        out_shape=jax.ShapeDtypeStruct((DEPTH, POOL_SHARD, GROUP_D), F32),
        compiler_params=_params(dimension_semantics=("arbitrary", "arbitrary")),
    )(chip, partial, received)


def _adamw_small(params):
    n = len(params)

    def body(*refs):
        ins, outs = refs[:4 * n], refs[4 * n:]
        for p in range(n):
            w_ref, g_ref, m_ref, v_ref = ins[4 * p:4 * p + 4]
            d_ref, mo_ref, vo_ref = outs[3 * p:3 * p + 3]
            d_ref[...], mo_ref[...], vo_ref[...] = _adamw(w_ref[...], g_ref[...], m_ref[...], v_ref[...])

    vmem = pl.BlockSpec(memory_space=pltpu.VMEM)
    flat = [a for group in params for a in group]
    out_shape = [jax.ShapeDtypeStruct(group[0].shape, F32) for group in params for _ in range(3)]
    outs = pl.pallas_call(
        body, name="adamw_small", in_specs=[vmem] * len(flat), out_specs=[vmem] * len(out_shape),
        out_shape=out_shape, compiler_params=_params(),
    )(*flat)
    return [tuple(outs[3 * p:3 * p + 3]) for p in range(n)]


def _sum_sources(slabs):
    def body(s_ref, o_ref):
        acc = s_ref[0]
        for b in range(1, N_DEV):
            acc = acc + s_ref[b]
        o_ref[...] = acc

    vmem = pl.BlockSpec(memory_space=pltpu.VMEM)
    return pl.pallas_call(
        body, name="sum_small_grads", in_specs=[vmem], out_specs=vmem,
        out_shape=jax.ShapeDtypeStruct(slabs.shape[1:], F32), compiler_params=_params(),
    )(slabs)


def _to_bf16(a, name, layers=None):
    first, stop = layers or (0, a.shape[0])

    def body(a_ref, o_ref):
        o_ref[...] = a_ref[...].astype(BF16)

    block = (None,) + a.shape[1:]
    return pl.pallas_call(
        body, name=name, grid=(stop - first,), in_specs=[pl.BlockSpec(block, lambda l: (first + l, 0, 0))],
        out_specs=pl.BlockSpec(block, lambda l: (l, 0, 0)),
        out_shape=jax.ShapeDtypeStruct((stop - first,) + a.shape[1:], BF16),
        compiler_params=_params(dimension_semantics=("arbitrary",)),
    )(a)


def kernel(x, c, w_ada, b_ada, g_pre, w_in, w_conv, w_pool, pool_scale, w_out, g_post, loss_target, m_w_ada, m_b_ada, m_g_pre, m_w_in, m_w_conv, m_w_pool, m_pool_scale, m_w_out, m_g_post, v_w_ada, v_b_ada, v_g_pre, v_w_in, v_w_conv, v_w_pool, v_pool_scale, v_w_out, v_g_post):
    mx, my, mc = _mesh_position()
    me = _block_id(mx, my, mc)
    chip = (2 * mx + my).astype(jnp.int32).reshape(1)
    core = mc.astype(jnp.int32).reshape(1)
    x0 = x[0]
    target = loss_target[0]
    conv_shard = w_conv.shape[-1]

    own_small = jnp.concatenate([c, w_conv.reshape(1, DEPTH * 3 * conv_shard)], axis=1)
    first_shards = [_to_bf16(w_in, "cast_w_in_0", (0, 1)), _to_bf16(w_out, "cast_w_out_0", (0, 1))]
    all_small = _all_gather_small(own_small, "all_gather_c_w_conv", first_shards)[:, 0, :]
    gathers = [_gather_weights_start(0, *first_shards, [all_small], relayed=True)]
    c_all = all_small[:, :D_MODEL]
    w_conv_full = all_small[:, D_MODEL:].reshape(N_DEV, DEPTH, 3, conv_shard).transpose(1, 2, 0, 3).reshape(
        DEPTH, 3, CONV_W)
    cps = jnp.concatenate([w_conv_full, pool_scale[:, None], jnp.zeros((DEPTH, 4, CONV_W), F32)], axis=1)

    c_act, pieces = _modulation_columns(c_all, w_ada)
    upper = (1, DEPTH)
    upper_shards = [_to_bf16(w_in, "cast_w_in_1", upper), _to_bf16(w_out, "cast_w_out_1", upper)]
    wpool_b = _to_bf16(w_pool.reshape(DEPTH, POOL_ROWS, GROUP_D), "cast_w_pool").reshape(w_pool.shape)
    first_sems_0, _, (zones_0,) = gathers[0]
    neighbour_sems, zones_0 = _gather_weights_pass_on(
        "all_gather_weights_relay_0", NEIGHBOUR_CHIPS, first_sems_0[1], partial(_first_arrival, 0), zones_0,
        [pieces, *upper_shards, wpool_b], relay=True)
    mod_all = _all_gather_small(pieces, "all_gather_modulation", [zones_0[0]])
    mod_mine = lax.dynamic_index_in_dim(mod_all, me, axis=1, keepdims=False)
    mod = mod_mine.reshape(N_DEV, DEPTH, W_IN_SHARD).transpose(1, 0, 2).reshape(DEPTH, 3 * D_MODEL) + b_ada
    zeros_d = jnp.zeros((DEPTH, 3, D_MODEL), F32)
    vec = jnp.concatenate([mod.reshape(DEPTH, 3, D_MODEL), g_pre[:, None], g_post[:, None], zeros_d], axis=1)

    gathers.append(_gather_weights_start(1, *upper_shards, [mod_all]))
    gathers = [(first_sems, shards, landing[k], k) for first_sems, shards, landing in gathers
               for k in range(len(landing))]

    xs, kept, wins, wouts = [x0], [], [], []
    for l in range(DEPTH):
        first_sems, shards, zones, index = gathers[l]
        if l == 0:
            diagonal_sems, zones = _gather_weights_pass_on(
                "all_gather_weights_pass_on_0", DIAGONAL_CHIP, neighbour_sems[3], lambda a, j: a, zones_0,
                [vec, cps, gathers[-1][1][0]])
            passed = [(neighbour_sems[:2], NEIGHBOUR_CHIPS), (diagonal_sems, DIAGONAL_CHIP)]
            win, wout = _gather_weights_finish(l, index, first_sems, passed, shards, zones, neighbour_sems[2])
        else:
            passed_sems, zones = _gather_weights_pass_on(
                f"all_gather_weights_pass_on_{l}", ALL_OTHER_CHIPS, first_sems[1], partial(_first_arrival, index),
                zones, [xs[-1]])
            win, wout = _gather_weights_finish(l, index, first_sems, [(passed_sems, ALL_OTHER_CHIPS)], shards, zones)
        x_next, *for_backward = _forward_layer(
            l, xs[-1], vec, cps, wpool_b, win, wout, target if l == DEPTH - 1 else None)
        xs.append(x_next)
        kept.append(for_backward[:6])
        wins.append(win)
        wouts.append(wout)
    dx, loss_tile = xs[DEPTH], for_backward[6]

    slab_rows = [None] * DEPTH
    partials = received = None
    in_flight = []

    def scatter(layer, grads, from_sibling, after):
        nonlocal partials, received
        partials = _add_sibling_blocks(
            f"grad_add_sibling_{layer}", layer, grads, from_sibling, core, partials, ALL_KINDS)
        chips = _chips_exchange(layer, partials, received, ALL_KINDS)
        sems, partials, received, token = _start_exchange(chips, f"grad_chips_start_{layer}", after)
        in_flight.append((chips, sems, layer))
        return token

    grads_above = None
    for l in reversed(range(DEPTH)):
        y, h, ycat, uc, fa, fp = kept[l]
        dx, dproj, dy, gwpool, dcps, dvec = _backward_layer(
            l, dx, y, xs[l], uc, fa, fp, vec, cps, wpool_b, wouts[l], wins[l])
        slab_rows[l] = jnp.concatenate(
            [dvec[0], dvec[1], dvec[2], dvec[3], dvec[4], dcps[3], dcps[0], dcps[1], dcps[2],
             loss_tile[0] if l == 0 else jnp.zeros((LANES,), F32)])
        after = []
        if l == 0:
            gather_small = _all_gather_exchange(jnp.stack(slab_rows))
            sems_s, slab, slabs, token = _start_exchange(gather_small, "all_gather_small_grads_start")
            after = [token]
        hosted = _sibling_exchange(grads_above, ALL_KINDS) if grads_above is not None else None
        gwin, gwout, *from_sibling = _weight_grads(l, h, dproj, ycat, dy, hosted, after)
        if l == 0:
            _, (slabs,) = _finish_exchange(
                gather_small, "all_gather_small_grads_finish", sems_s, slab, slabs, [gwin])
        if grads_above is not None:
            scatter(l + 1, grads_above, from_sibling, [])
        grads_above = [gwin, gwout, gwpool.reshape(POOL_ROWS, GROUP_D)]
        if l <= 1:
            from_sibling = _run_exchange(_sibling_exchange(grads_above, ALL_KINDS), f"grad_exchange_sibling_{l}")
            token = scatter(l, grads_above, from_sibling, [slabs] if l == 0 else [])
            grads_above = None
    grad_x = dx[None]
    chips_0, sems_0, _ = in_flight.pop()

    total = _sum_sources(slabs)
    loss = total[0, SLAB_COLS]
    o = 3 * D_MODEL
    g_b_ada = total[:, :o]
    g_g_pre = total[:, o:o + D_MODEL]
    g_g_post = total[:, o + D_MODEL:o + 2 * D_MODEL]
    g_pool_scale = total[:, o + 2 * D_MODEL:o + 2 * D_MODEL + POOL_W]
    g_conv_full = total[:, o + 2 * D_MODEL + POOL_W:SLAB_COLS].reshape(DEPTH, 3, CONV_W)
    g_w_conv = lax.dynamic_slice_in_dim(g_conv_full, me * conv_shard, conv_shard, axis=2)

    after = [token]
    for chips, sems, l in in_flight:
        partials, received = _finish_exchange(chips, f"grad_chips_finish_{l}", sems, partials, received, after)
        after = []
    upper = (1, DEPTH)
    w_in_upper = _adamw_reduced(
        "adamw_w_in_upper", w_in, m_w_in, v_w_in, partials[0], received[0], chip, ROW_TILE, upper, None)
    w_out_upper = _adamw_reduced(
        "adamw_w_out_upper", w_out, m_w_out, v_w_out, partials[1], received[1], chip, W_OUT_SHARD, upper, None)
    dmod_all = slabs[:, :, :o].reshape(N_DEV, DEPTH, N_DEV, W_IN_SHARD)
    dmod_cols = lax.dynamic_index_in_dim(dmod_all, me, axis=2, keepdims=False).transpose(1, 0, 2) + token[0, 0]
    g_w_ada, d_w_ada, nm_w_ada, nv_w_ada = _adamw_w_ada(w_ada, m_w_ada, v_w_ada, c_act.T, dmod_cols)

    partials, received = _finish_exchange(
        chips_0, "grad_chips_finish_0", sems_0, partials, received, [nv_w_ada, w_in_upper[3], w_out_upper[3]])
    gather_pool = _all_gather_exchange(_reduce_w_pool(partials[2], received[2], chip), axis=1)
    sems_p, pool_rows, pool_landing, token_p = _start_exchange(gather_pool, "all_gather_grad_w_pool_start")
    g_w_in, d_w_in, nm_w_in, nv_w_in = _adamw_reduced(
        "adamw_w_in_0", w_in, m_w_in, v_w_in, partials[0], received[0], chip, ROW_TILE, (0, 1), w_in_upper)
    g_w_out, d_w_out, nm_w_out, nv_w_out = _adamw_reduced(
        "adamw_w_out_0", w_out, m_w_out, v_w_out, partials[1], received[1], chip, W_OUT_SHARD, (0, 1), w_out_upper)
    _, (g_pool_all,) = _finish_exchange(
        gather_pool, "all_gather_grad_w_pool_finish", sems_p, pool_rows, pool_landing, [nv_w_in, nv_w_out])
    g_w_pool = g_pool_all.reshape(w_pool.shape)

    small = _adamw_small([
        (b_ada, g_b_ada, m_b_ada, v_b_ada),
        (g_pre, g_g_pre, m_g_pre, v_g_pre),
        (w_conv, g_w_conv, m_w_conv, v_w_conv),
        (w_pool, g_w_pool, m_w_pool, v_w_pool),
        (pool_scale, g_pool_scale, m_pool_scale, v_pool_scale),
        (g_post, g_g_post, m_g_post, v_g_post),
    ])
    (d_b_ada, nm_b_ada, nv_b_ada), (d_g_pre, nm_g_pre, nv_g_pre), (d_w_conv, nm_w_conv, nv_w_conv), \
        (d_w_pool, nm_w_pool, nv_w_pool), (d_ps, nm_ps, nv_ps), (d_g_post, nm_g_post, nv_g_post) = small

    return (loss, grad_x,
            g_w_ada, g_b_ada, g_g_pre, g_w_in, g_w_conv, g_w_pool, g_pool_scale, g_w_out, g_g_post,
            d_w_ada, d_b_ada, d_g_pre, d_w_in, d_w_conv, d_w_pool, d_ps, d_w_out, d_g_post,
            nm_w_ada, nm_b_ada, nm_g_pre, nm_w_in, nm_w_conv, nm_w_pool, nm_ps, nm_w_out, nm_g_post,
            nv_w_ada, nv_b_ada, nv_g_pre, nv_w_in, nv_w_conv, nv_w_pool, nv_ps, nv_w_out, nv_g_post)
```

```python
from functools import partial

import jax
import jax.numpy as jnp
from jax import lax
from jax.experimental import pallas as pl
from jax.experimental.pallas import tpu as pltpu

F32 = jnp.float32
BF16 = jnp.bfloat16

D_MODEL = 1024
DEPTH = 4
CONV_W = 512
POOL_W = 512
POOL_WINDOWS = (2, 4, 8, 16)
GROUP_D = 128
IN_COLS = 4 * CONV_W + 2 * POOL_W
NORM_EPS = 1e-6

ADAM_LR = 0.001
ADAM_B1 = 0.9
ADAM_B2 = 0.999
ADAM_EPS = 1e-08
ADAM_WD = 0.01
ADAM_STEP = 10

N_DEV = 8
N_CHIP = 4
N_OTHER_CHIPS = N_CHIP - 1
MESH = pl.DeviceIdType.MESH
W_IN_SHARD = IN_COLS // N_DEV
W_OUT_SHARD = D_MODEL // N_DEV
POOL_ROWS = len(POOL_WINDOWS) * GROUP_D
POOL_SHARD = POOL_ROWS // N_DEV

SUBLANES = 8
LANES = 128
VMEM_LIMIT_BYTES = 56 * 1024 * 1024
ROW_TILE = 512
BWD_TILE = 256
GWIN_COLS = 768
GWOUT_COLS = 512
POOL_HALO = 16
CONV_HALO = SUBLANES

SLAB_COLS = 3 * D_MODEL + D_MODEL + D_MODEL + POOL_W + 3 * CONV_W

HBM = pl.BlockSpec(memory_space=pl.ANY)


def _params(**kw):
    return pltpu.CompilerParams(vmem_limit_bytes=VMEM_LIMIT_BYTES, **kw)


def _sigmoid(v):
    return 1.0 / (1.0 + jnp.exp(-v))


def _dot(a, b):
    return jnp.dot(a, b, preferred_element_type=F32)


def _dot_tn(a, b):
    return lax.dot_general(a, b, (((0,), (0,)), ((), ())), preferred_element_type=F32)


def _dot_nt(a, b):
    return lax.dot_general(a, b, (((1,), (1,)), ((), ())), preferred_element_type=F32)


def _rows_from_before(v, k):
    return pltpu.roll(v, k, 0)


def _rows_from_after(v, k):
    return pltpu.roll(v, v.shape[0] - k, 0)


def _window_counts(t0, rows):
    return (lax.broadcasted_iota(jnp.int32, (rows, 1), 0) + (t0 + 1)).astype(F32)


def _split_proj(p32):
    cw = CONV_W
    return (p32[:, 0 * cw:1 * cw], p32[:, 1 * cw:2 * cw], p32[:, 2 * cw:3 * cw], p32[:, 3 * cw:4 * cw],
            p32[:, 4 * cw:4 * cw + POOL_W], p32[:, 4 * cw + POOL_W:])


def _layer_spec(shape, layer):
    nd = len(shape)
    return pl.BlockSpec((None,) + tuple(shape[1:]), lambda i, _l=layer, _n=nd: (_l,) + (0,) * (_n - 1))


def _whole_spec(shape):
    return pl.BlockSpec(tuple(shape), lambda i, _n=len(shape): (0,) * _n, pipeline_mode=pl.Buffered(1))


def _mesh_position():
    return lax.axis_index("x"), lax.axis_index("y"), lax.axis_index("c")


def _block_id(x, y, c):
    return 4 * x + 2 * y + c


def _other_chips(x, y):
    return [(x ^ 1, y), (x, y ^ 1), (x ^ 1, y ^ 1)]


def _col_block(ref, blk):
    return ref.at[:, pl.ds(pl.multiple_of(blk * W_IN_SHARD, LANES), W_IN_SHARD)]


def _row_block(rows):
    def block(ref, blk):
        return ref.at[pl.ds(pl.multiple_of(blk * rows, rows), rows), :]
    return block


_BLOCK_OF = (_col_block, _row_block(W_OUT_SHARD), _row_block(POOL_SHARD))
_BLOCK_SHAPES = ((D_MODEL, W_IN_SHARD), (W_OUT_SHARD, D_MODEL), (POOL_SHARD, GROUP_D))


class _Exchange:
    def __init__(self, inputs, out_shapes, aliases, sem_shapes, make):
        self.inputs, self.out_shapes, self.aliases, self.sem_shapes, self.make = (
            list(inputs), list(out_shapes), dict(aliases), list(sem_shapes), make)


def _run_exchange(exchange, name):
    n_in, n_out = len(exchange.inputs), len(exchange.out_shapes)

    def body(*refs):
        start, finish = exchange.make(refs[:n_in], refs[n_in:n_in + n_out], refs[n_in + n_out:])
        start()
        finish()

    return pl.pallas_call(
        body, name=name, in_specs=[HBM] * n_in, out_specs=[HBM] * n_out, out_shape=exchange.out_shapes,
        scratch_shapes=exchange.sem_shapes, input_output_aliases=exchange.aliases, compiler_params=_params(),
    )(*exchange.inputs)


_SEM = pl.BlockSpec(memory_space=pltpu.SEMAPHORE)
_DATAFLOW = pltpu.SideEffectType.DATAFLOW_SIDE_EFFECTING


def _start_exchange(exchange, name, after=()):
    n_in, n_out, n_sem = len(exchange.inputs), len(exchange.out_shapes), len(exchange.sem_shapes)
    sources = [i for i in range(n_in) if i not in exchange.aliases]
    aliases = {i: n_sem + k for k, i in enumerate(sources)}
    aliases.update({i: n_sem + len(sources) + o for i, o in exchange.aliases.items()})

    def body(*refs):
        in_refs = refs[:n_in]
        outs = refs[n_in + len(after):]
        sems = outs[:n_sem]
        out_refs = outs[n_sem + len(sources):n_sem + len(sources) + n_out]
        exchange.make(in_refs, out_refs, sems)[0]()
        refs[-1][...] = jnp.zeros_like(refs[-1])

    outs = pl.pallas_call(
        body, name=name, in_specs=[HBM] * (n_in + len(after)),
        out_specs=[_SEM] * n_sem + [HBM] * (len(sources) + n_out) + [pl.BlockSpec(memory_space=pltpu.VMEM)],
        out_shape=(exchange.sem_shapes + [pltpu.HBM(exchange.inputs[i].shape, exchange.inputs[i].dtype) for i in sources]
                   + [pltpu.HBM(s.shape, s.dtype) for s in exchange.out_shapes]
                   + [jax.ShapeDtypeStruct((SUBLANES, LANES), F32)]),
        input_output_aliases=aliases, compiler_params=_params(has_side_effects=_DATAFLOW),
    )(*exchange.inputs, *after)
    return outs[:n_sem], outs[n_sem:n_sem + len(sources)], outs[n_sem + len(sources):-1], outs[-1]


def _finish_exchange(exchange, name, sems, sources, landing, after):
    n_src, n_out, n_sem = len(sources), len(landing), len(sems)
    n_in = len(exchange.inputs)
    source_at = [i for i in range(n_in) if i not in exchange.aliases]

    def body(*refs):
        src_refs, out_refs = refs[:n_src], refs[n_src:n_src + n_out]
        sem_refs = refs[n_src + n_out:n_src + n_out + n_sem]
        in_refs = [None] * n_in
        for k, i in enumerate(source_at):
            in_refs[i] = src_refs[k]
        for i, o in exchange.aliases.items():
            in_refs[i] = out_refs[o]
        exchange.make(in_refs, out_refs, sem_refs)[1]()

    arrays = list(sources) + list(landing)
    outs = pl.pallas_call(
        body, name=name, in_specs=[HBM] * len(arrays) + [_SEM] * n_sem + [HBM] * len(after),
        out_specs=[HBM] * len(arrays), out_shape=[pltpu.HBM(a.shape, a.dtype) for a in arrays],
        input_output_aliases={i: i for i in range(len(arrays))}, compiler_params=_params(has_side_effects=_DATAFLOW),
    )(*arrays, *sems, *after)
    return outs[:n_src], outs[n_src:]


N_GATHERED = 2
FIRST_COPIES = 1 + N_OTHER_CHIPS
_GATHERED_SHAPES = ((D_MODEL, IN_COLS), (D_MODEL, D_MODEL))
ALL_OTHER_CHIPS, NEIGHBOUR_CHIPS, DIAGONAL_CHIP = (0, 1, 2), (0, 1), (2,)


def _first_sem(layer, a, k):
    return (layer * N_GATHERED + a) * FIRST_COPIES + k


def _first_arrival(layer, a, j):
    return _first_sem(layer, a, 1 + j)


def _gather_copy(window_of, full_ref, blk, send_sem, recv_sem, to, src=None):
    window = window_of(full_ref, blk)
    return pltpu.make_async_remote_copy(
        src_ref=window if src is None else src, dst_ref=window, send_sem=send_sem, recv_sem=recv_sem,
        device_id=to, device_id_type=MESH)


def _first_copies(layer, shard_refs, full_refs, send_sems, recv_sems, local_sems, relayed):
    x, y, c = _mesh_position()
    me = _block_id(x, y, c)
    chips = [_other_chips(x, y)[j] for j in (NEIGHBOUR_CHIPS if relayed else ALL_OTHER_CHIPS)]
    own, remote = [], []
    for a in range(N_GATHERED):
        shard = shard_refs[a].at[layer]
        own.append(pltpu.make_async_copy(
            shard, _BLOCK_OF[a](full_refs[a], me), local_sems.at[layer * N_GATHERED + a]))
        targets = [(x, y, 1 - c)] + [(*chip, c) for chip in chips]
        remote += [_gather_copy(_BLOCK_OF[a], full_refs[a], me, send_sems.at[_first_sem(layer, a, k)],
                                recv_sems.at[_first_sem(layer, a, k)], to, src=shard)
                   for k, to in enumerate(targets)]
    return own, remote


def _gather_weights_start(first_layer, win_shards, wout_shards, after, relayed=False):
    n_layers = win_shards.shape[0]
    n_first = n_layers * N_GATHERED * FIRST_COPIES
    sem_shapes = [pltpu.SemaphoreType.DMA((n_first,)), pltpu.SemaphoreType.DMA((n_first,)),
                  pltpu.SemaphoreType.DMA((n_layers * N_GATHERED,))]
    shards = [win_shards, wout_shards]

    def body(win_sh, wout_sh, *rest):
        send_sems, recv_sems, local_sems, win_thru, wout_thru, *landing = rest[len(after):]
        for layer in range(n_layers):
            own, remote = _first_copies(layer, (win_sh, wout_sh), landing[N_GATHERED * layer:N_GATHERED * (layer + 1)],
                                        send_sems, recv_sems, local_sems, relayed)
            for cp in own + remote:
                cp.start()

    outs = pl.pallas_call(
        body, name=f"all_gather_weights_start_{first_layer}", in_specs=[HBM] * (2 + len(after)),
        out_specs=[_SEM] * 3 + [HBM] * (2 + n_layers * N_GATHERED),
        out_shape=(sem_shapes + [pltpu.HBM(s.shape, s.dtype) for s in shards]
                   + [pltpu.HBM(s, BF16) for _ in range(n_layers) for s in _GATHERED_SHAPES]),
        input_output_aliases={0: 3, 1: 4}, compiler_params=_params(has_side_effects=_DATAFLOW),
    )(*shards, *after)
    landing = outs[5:]
    return outs[:3], outs[3:5], [landing[N_GATHERED * l:N_GATHERED * (l + 1)] for l in range(n_layers)]


def _passed_on_copies(full_refs, send_sems, recv_sems, core_of_block, chips):
    x, y, c = _mesh_position()
    return [_gather_copy(_BLOCK_OF[a], full_refs[a], _block_id(*_other_chips(x, y)[j], core_of_block),
                         send_sems.at[a * N_OTHER_CHIPS + j], recv_sems.at[a * N_OTHER_CHIPS + j], (x, y, 1 - c))
            for a in range(N_GATHERED) for j in chips]


def _relayed_copies(full_refs, send_sems, recv_sems):
    x, y, c = _mesh_position()
    source, to = (x ^ (1 - c), y ^ c), (x ^ c, y ^ (1 - c))
    return [_gather_copy(_BLOCK_OF[a], full_refs[a], _block_id(*source, c), send_sems.at[a], recv_sems.at[a], (*to, c))
            for a in range(N_GATHERED)]


def _gather_weights_pass_on(name, chips, arrival_sems, arrival_sem_of, landing, after, relay=False):
    n = N_GATHERED * N_OTHER_CHIPS
    sem_shapes = [pltpu.SemaphoreType.DMA((n,))] * 2 + [pltpu.SemaphoreType.DMA((N_GATHERED,))] * (2 if relay else 0)

    def body(win_ref, wout_ref, arrivals, *rest):
        sems = rest[len(after):len(after) + len(sem_shapes)]
        x, y, c = _mesh_position()
        full_refs = (win_ref, wout_ref)
        passed = _passed_on_copies(full_refs, sems[0], sems[1], c, chips)
        relayed = _relayed_copies(full_refs, sems[2], sems[3]) if relay else []
        for a in range(N_GATHERED):
            for j in chips:
                sem = arrivals.at[arrival_sem_of(a, j)]
                _gather_copy(_BLOCK_OF[a], full_refs[a], _block_id(*_other_chips(x, y)[j], c), sem, sem,
                             (x, y, c)).wait_recv()
            for cp in relayed[a:a + 1] + passed[a * len(chips):(a + 1) * len(chips)]:
                cp.start()

    outs = pl.pallas_call(
        body, name=name, in_specs=[HBM] * N_GATHERED + [_SEM] + [HBM] * len(after),
        out_specs=[_SEM] * len(sem_shapes) + [HBM] * N_GATHERED,
        out_shape=sem_shapes + [pltpu.HBM(a.shape, a.dtype) for a in landing],
        input_output_aliases={a: len(sem_shapes) + a for a in range(N_GATHERED)},
        compiler_params=_params(has_side_effects=_DATAFLOW),
    )(*landing, arrival_sems, *after)
    return outs[:len(sem_shapes)], outs[len(sem_shapes):]


def _gather_weights_finish(layer, index, first_sems, passed, shards, landing, relay_send_sems=None):
    relayed = relay_send_sems is not None
    passed_sems = [sem for (send, recv), _ in passed for sem in (send, recv)] + ([relay_send_sems] if relayed else [])

    def body(win_ref, wout_ref, first_send, first_recv, local_sems, *rest):
        sems, (win_sh, wout_sh) = rest[:len(passed_sems)], rest[len(passed_sems):len(passed_sems) + 2]
        x, y, c = _mesh_position()
        full_refs = (win_ref, wout_ref)
        own, sent = _first_copies(index, (win_sh, wout_sh), full_refs, first_send, first_recv, local_sems, relayed)
        for a in range(N_GATHERED):
            sem = _first_sem(index, a, 0)
            _gather_copy(_BLOCK_OF[a], full_refs[a], _block_id(x, y, 1 - c), first_recv.at[sem], first_recv.at[sem],
                         (x, y, c)).wait_recv()
        for p, (_, chips) in enumerate(passed):
            for cp in _passed_on_copies(full_refs, sems[2 * p], sems[2 * p + 1], 1 - c, chips):
                cp.wait_recv()
            sent += _passed_on_copies(full_refs, sems[2 * p], sems[2 * p + 1], c, chips)
        if relayed:
            sent += _relayed_copies(full_refs, sems[-1], sems[-1])
        for cp in sent:
            cp.wait_send()
        for cp in own:
            cp.wait()

    return pl.pallas_call(
        body, name=f"all_gather_weights_finish_{layer}",
        in_specs=[HBM] * N_GATHERED + [_SEM] * (3 + len(passed_sems)) + [HBM] * 2,
        out_specs=[HBM] * N_GATHERED, out_shape=[pltpu.HBM(a.shape, a.dtype) for a in landing],
        input_output_aliases={a: a for a in range(N_GATHERED)}, compiler_params=_params(has_side_effects=_DATAFLOW),
    )(*landing, *first_sems, *passed_sems, *shards)


ALL_KINDS = (0, 1, 2)


def _sibling_exchange(grads, kinds):
    n_arr = len(grads)

    def make(in_refs, out_refs, sems):
        send_sems, recv_sems = sems
        x, y, c = _mesh_position()
        copies = [pltpu.make_async_remote_copy(
            src_ref=_BLOCK_OF[kinds[a]](in_refs[a], 2 * q + (1 - c)), dst_ref=out_refs[a].at[q],
            send_sem=send_sems.at[a * N_CHIP + q], recv_sem=recv_sems.at[a * N_CHIP + q],
            device_id=(x, y, 1 - c), device_id_type=MESH)
            for a in range(n_arr) for q in range(N_CHIP)]

        def start():
            for cp in copies:
                cp.start()

        def finish():
            for cp in copies:
                cp.wait_recv()
            for cp in copies:
                cp.wait_send()

        return start, finish

    return _Exchange(
        grads, [jax.ShapeDtypeStruct((N_CHIP,) + _BLOCK_SHAPES[k], BF16) for k in kinds], {},
        [pltpu.SemaphoreType.DMA((n_arr * N_CHIP,)), pltpu.SemaphoreType.DMA((n_arr * N_CHIP,))], make)


def _chips_exchange(layer, partials, received, kinds):
    n_arr = len(partials)

    def make(in_refs, out_refs, sems):
        send_sems, recv_sems = sems
        x, y, c = _mesh_position()
        copies = [pltpu.make_async_remote_copy(
            src_ref=in_refs[a].at[2 * qx + qy, layer], dst_ref=out_refs[a].at[j, layer],
            send_sem=send_sems.at[a * N_OTHER_CHIPS + j], recv_sem=recv_sems.at[a * N_OTHER_CHIPS + j],
            device_id=(qx, qy, c), device_id_type=MESH)
            for a in range(n_arr) for j, (qx, qy) in enumerate(_other_chips(x, y))]

        def start():
            for cp in copies:
                cp.start()

        def finish():
            for cp in copies:
                cp.wait_recv()
            for cp in copies:
                cp.wait_send()

        return start, finish

    inputs = list(partials)
    aliases = {}
    if received is not None:
        inputs += list(received)
        aliases = {n_arr + a: a for a in range(n_arr)}
    return _Exchange(
        inputs, [jax.ShapeDtypeStruct((N_OTHER_CHIPS, DEPTH) + _BLOCK_SHAPES[k], BF16) for k in kinds], aliases,
        [pltpu.SemaphoreType.DMA((n_arr * N_OTHER_CHIPS,)), pltpu.SemaphoreType.DMA((n_arr * N_OTHER_CHIPS,))], make)


def _all_gather_exchange(v, axis=0):
    def make(in_refs, out_refs, sems):
        send_sems, recv_sems, local_sem = sems
        x, y, c = _mesh_position()
        me = _block_id(x, y, c)
        block = lambda blk: out_refs[0].at[(slice(None),) * axis + (blk,)]
        own = pltpu.make_async_copy(in_refs[0], block(me), local_sem.at[0])
        sends, arrivals = [], []
        for k in range(1, N_DEV):
            px, py, pc = x ^ ((k >> 2) & 1), y ^ ((k >> 1) & 1), c ^ (k & 1)
            sends.append(pltpu.make_async_remote_copy(
                src_ref=in_refs[0], dst_ref=block(me), send_sem=send_sems.at[k - 1],
                recv_sem=recv_sems.at[k - 1], device_id=(px, py, pc), device_id_type=MESH))
            arrivals.append(pltpu.make_async_remote_copy(
                src_ref=in_refs[0], dst_ref=block(_block_id(px, py, pc)), send_sem=send_sems.at[k - 1],
                recv_sem=recv_sems.at[k - 1], device_id=(x, y, c), device_id_type=MESH))

        def start():
            for cp in [own] + sends:
                cp.start()

        def finish():
            for cp in arrivals:
                cp.wait_recv()
            for cp in sends:
                cp.wait_send()
            own.wait()

        return start, finish

    return _Exchange(
        [v], [jax.ShapeDtypeStruct(v.shape[:axis] + (N_DEV,) + v.shape[axis:], v.dtype)], {},
        [pltpu.SemaphoreType.DMA((N_DEV - 1,)), pltpu.SemaphoreType.DMA((N_DEV - 1,)),
         pltpu.SemaphoreType.DMA((1,))], make)


def _host(exchange, args, in_specs, out_shape, out_specs, scratch):
    n_own = (len(args), len(out_shape), len(scratch))
    if exchange is None:
        return {}, lambda refs: (refs, None)
    n_ex = (len(exchange.inputs), len(exchange.out_shapes), len(exchange.sem_shapes))
    aliases = {n_own[0] + i: n_own[1] + o for i, o in exchange.aliases.items()}
    args += exchange.inputs
    in_specs += [HBM] * n_ex[0]
    out_shape += exchange.out_shapes
    out_specs += [HBM] * n_ex[1]
    scratch += exchange.sem_shapes

    def split(refs):
        own, theirs, at = [], [], 0
        for mine, ex in zip(n_own, n_ex):
            own += refs[at:at + mine]
            theirs.append(refs[at + mine:at + mine + ex])
            at += mine + ex
        return own, exchange.make(*theirs)

    return aliases, split


def _all_gather_small(v, name, after=()):
    vmem = pl.BlockSpec(memory_space=pltpu.VMEM)

    def body(v_ref, *rest):
        out_ref, send_sems, recv_sems = rest[len(after):]
        x, y, c = _mesh_position()
        me = _block_id(x, y, c)
        out_ref[me] = v_ref[...]
        sends = []
        for k in range(1, N_DEV):
            px, py, pc = x ^ ((k >> 2) & 1), y ^ ((k >> 1) & 1), c ^ (k & 1)
            send = pltpu.make_async_remote_copy(
                src_ref=v_ref, dst_ref=out_ref.at[me], send_sem=send_sems.at[k - 1], recv_sem=recv_sems.at[k - 1],
                device_id=(px, py, pc), device_id_type=MESH)
            send.start()
            sends.append((send, _block_id(px, py, pc)))
        for k, (send, peer) in enumerate(sends):
            pltpu.make_async_remote_copy(
                src_ref=v_ref, dst_ref=out_ref.at[peer], send_sem=send_sems.at[k], recv_sem=recv_sems.at[k],
                device_id=(x, y, c), device_id_type=MESH).wait_recv()
        for send, _ in sends:
            send.wait_send()

    return pl.pallas_call(
        body, name=name, in_specs=[vmem] + [HBM] * len(after), out_specs=vmem,
        out_shape=jax.ShapeDtypeStruct((N_DEV,) + v.shape, v.dtype),
        scratch_shapes=[pltpu.SemaphoreType.DMA((N_DEV - 1,)), pltpu.SemaphoreType.DMA((N_DEV - 1,))],
        compiler_params=_params(),
    )(v, *after)


def _forward_layer(layer, x, vec, cps, wpool, win, wout, target=None):
    t_len = x.shape[0]
    n_tiles = t_len // ROW_TILE
    row = lambda cols: pl.BlockSpec((ROW_TILE, cols), lambda i: (i, 0))
    widths = (D_MODEL, 2 * CONV_W, 3 * CONV_W, 4 * POOL_W)
    head = target is not None

    def body(x_ref, vec_ref, cps_ref, wpool_ref, win_ref, wout_ref, *rest):
        target_ref = rest[0] if head else None
        xo_ref, y_ref, h_ref, ycat_ref, uc_ref, fa_ref, fp_ref = rest[head:head + 7]
        loss_ref = rest[head + 7] if head else None
        zc_ref, pc_ref = rest[-2:]
        i = pl.program_id(0)

        @pl.when(i == 0)
        def _():
            zc_ref[...] = jnp.zeros_like(zc_ref)
            pc_ref[...] = jnp.zeros_like(pc_ref)
            if head:
                loss_ref[...] = jnp.zeros_like(loss_ref)

        x_t = x_ref[...]
        shift, scale, gate = vec_ref[0:1, :], vec_ref[1:2, :], vec_ref[2:3, :]
        g_pre, g_post = vec_ref[3:4, :], vec_ref[4:5, :]
        w0, w1, w2, ps = cps_ref[0:1, :], cps_ref[1:2, :], cps_ref[2:3, :], cps_ref[3:4, :]
        rx = lax.rsqrt(jnp.mean(x_t * x_t, axis=-1, keepdims=True) + NORM_EPS)
        h = (x_t * rx) * g_pre * (1.0 + scale) + shift
        h_ref[...] = h.astype(BF16)
        proj = _dot(h.astype(BF16), win_ref[...])
        u_a, b_a, c_a, g_a, u_p, g_p = _split_proj(proj)
        uc_ref[...] = jnp.concatenate([u_a, c_a], axis=1).astype(BF16)

        z = c_a * u_a
        zcat = jnp.concatenate([zc_ref[...], z], axis=0)
        zc_ref[...] = z[ROW_TILE - CONV_HALO:]
        conv = (w0 * _rows_from_before(zcat, 2)[CONV_HALO:] + w1 * _rows_from_before(zcat, 1)[CONV_HALO:] + w2 * z)
        sig_a = _sigmoid(g_a)
        silu_a = g_a * sig_a
        b_conv = b_a * conv
        y_a = b_conv * silu_a
        fa_ref[...] = jnp.concatenate(
            [silu_a * conv, silu_a * b_a, b_conv * (sig_a + silu_a * (1.0 - sig_a))], axis=1).astype(BF16)

        pcat = jnp.concatenate([pc_ref[...], u_p], axis=0)
        pc_ref[...] = u_p[ROW_TILE - POOL_HALO:]
        counts = _window_counts(i * ROW_TILE, ROW_TILE)
        pooled, mixed = [], []
        for g, w in enumerate(POOL_WINDOWS):
            cols = slice(g * GROUP_D, (g + 1) * GROUP_D)
            s = pcat[:, cols]
            step = 1
            while step < w:
                s = s + _rows_from_before(s, step)
                step *= 2
            pooled_g = (s[POOL_HALO:] * (1.0 / jnp.minimum(counts, float(w))) - u_p[:, cols]).astype(BF16)
            pooled.append(pooled_g)
            mixed.append(_dot(pooled_g, wpool_ref[g]))
        mixed = jnp.concatenate(mixed, axis=1)
        sig_p = _sigmoid(g_p)
        silu_p = g_p * sig_p
        mixed_ps = mixed * ps
        y_p = mixed_ps * silu_p
        fp_ref[...] = jnp.concatenate(
            [(ps * silu_p).astype(BF16), (mixed_ps * (sig_p + silu_p * (1.0 - sig_p))).astype(BF16),
             (silu_p * mixed).astype(BF16)] + pooled, axis=1)

        ycat = jnp.concatenate([y_a, y_p], axis=1)
        ycat_ref[...] = ycat.astype(BF16)
        y_b = _dot(ycat.astype(BF16), wout_ref[...]).astype(BF16)
        y_ref[...] = y_b
        y_t = y_b.astype(F32)
        ry = lax.rsqrt(jnp.mean(y_t * y_t, axis=-1, keepdims=True) + NORM_EPS)
        x_next = x_t + gate * (y_t * ry * g_post)
        if head:
            err = x_next - target_ref[...]
            xo_ref[...] = err * (1.0 / D_MODEL)
            loss_ref[...] += jnp.sum(err * err) * (0.5 / D_MODEL)
        else:
            xo_ref[...] = x_next

    tile = (SUBLANES, LANES)
    return pl.pallas_call(
        body, name=f"forward_layer_{layer}", grid=(n_tiles,),
        in_specs=[row(D_MODEL), _layer_spec(vec.shape, layer), _layer_spec(cps.shape, layer),
                  _layer_spec(wpool.shape, layer), _whole_spec(win.shape), _whole_spec(wout.shape)]
        + [row(D_MODEL)] * head,
        out_specs=[row(D_MODEL), row(D_MODEL), row(D_MODEL), row(D_MODEL)] + [row(w) for w in widths[1:]]
        + [_whole_spec(tile)] * head,
        out_shape=[jax.ShapeDtypeStruct((t_len, D_MODEL), F32), jax.ShapeDtypeStruct((t_len, D_MODEL), BF16),
                   jax.ShapeDtypeStruct((t_len, D_MODEL), BF16), jax.ShapeDtypeStruct((t_len, D_MODEL), BF16)]
        + [jax.ShapeDtypeStruct((t_len, w), BF16) for w in widths[1:]] + [jax.ShapeDtypeStruct(tile, F32)] * head,
        scratch_shapes=[pltpu.VMEM((CONV_HALO, CONV_W), F32), pltpu.VMEM((POOL_HALO, POOL_W), F32)],
        compiler_params=_params(dimension_semantics=("arbitrary",)),
    )(x, vec, cps, wpool, win, wout, *([target] * head))


def _backward_layer(layer, dxo, y, x, uc, fa, fp, vec, cps, wpool, wout, win):
    t_len = dxo.shape[0]
    n_tiles = t_len // BWD_TILE
    halo_per_tile = BWD_TILE // POOL_HALO
    rev = lambda cols: pl.BlockSpec((BWD_TILE, cols), lambda i: (n_tiles - 1 - i, 0))
    halo_spec = pl.BlockSpec(
        (POOL_HALO, 2 * CONV_W), lambda i: (jnp.maximum((n_tiles - 1 - i) * halo_per_tile - 1, 0), 0))
    gwpool_shape = (len(POOL_WINDOWS), GROUP_D, GROUP_D)

    def body(dxo_ref, y_ref, x_ref, uc_ref, uch_ref, fa_ref, fp_ref, vec_ref, cps_ref, wpool_ref, wout_ref, win_ref,
             dx_ref, dproj_ref, dy_ref, gwpool_ref, dcps_ref, dvec_ref, gwpool_acc, dcc_ref, qc_ref):
        i = pl.program_id(0)
        tile = n_tiles - 1 - i

        @pl.when(i == 0)
        def _():
            gwpool_acc[...] = jnp.zeros_like(gwpool_acc)
            dcps_ref[...] = jnp.zeros_like(dcps_ref)
            dvec_ref[...] = jnp.zeros_like(dvec_ref)
            dcc_ref[...] = jnp.zeros_like(dcc_ref)
            qc_ref[...] = jnp.zeros_like(qc_ref)

        shift, scale, gate = vec_ref[0:1, :], vec_ref[1:2, :], vec_ref[2:3, :]
        g_pre, g_post = vec_ref[3:4, :], vec_ref[4:5, :]
        w0, w1, w2 = cps_ref[0:1, :], cps_ref[1:2, :], cps_ref[2:3, :]

        dxo_t = dxo_ref[...]
        y_t = y_ref[...].astype(F32)
        ry = lax.rsqrt(jnp.mean(y_t * y_t, axis=-1, keepdims=True) + NORM_EPS)
        yh = y_t * ry
        dvec_ref[2:3, :] += jnp.sum(dxo_t * yh, axis=0, keepdims=True)
        dyh = dxo_t * (gate * g_post)
        dy_b = (ry * (dyh - yh * jnp.mean(dyh * yh, axis=-1, keepdims=True))).astype(BF16)
        dy_ref[...] = dy_b
        dycat = _dot_nt(dy_b, wout_ref[...])
        dy_a, dy_p = dycat[:, :CONV_W], dycat[:, CONV_W:]

        fa_t = fa_ref[...].astype(F32)
        db_a = dy_a * fa_t[:, :CONV_W]
        dconv = dy_a * fa_t[:, CONV_W:2 * CONV_W]
        dg_a = dy_a * fa_t[:, 2 * CONV_W:]
        uc_t = uc_ref[...].astype(F32)
        u_a, c_a = uc_t[:, :CONV_W], uc_t[:, CONV_W:]
        halo = jnp.where(tile > 0, uch_ref[...].astype(F32), 0.0)[POOL_HALO - CONV_HALO:]
        z = c_a * u_a
        zcat = jnp.concatenate([halo[:, CONV_W:] * halo[:, :CONV_W], z], axis=0)
        z1 = _rows_from_before(zcat, 1)[CONV_HALO:]
        z2 = _rows_from_before(zcat, 2)[CONV_HALO:]
        dccat = jnp.concatenate([dconv, dcc_ref[...]], axis=0)
        dc1 = _rows_from_after(dccat, 1)[:BWD_TILE]
        dc2 = _rows_from_after(dccat, 2)[:BWD_TILE]
        dz = w2 * dconv + w1 * dc1 + w0 * dc2
        dcc_ref[...] = dconv[:CONV_HALO]
        dcps_ref[0:1, :] += jnp.sum(dconv * z2, axis=0, keepdims=True)
        dcps_ref[1:2, :] += jnp.sum(dconv * z1, axis=0, keepdims=True)
        dcps_ref[2:3, :] += jnp.sum(dconv * z, axis=0, keepdims=True)
        du_a = dz * c_a
        dc_a = dz * u_a

        dmixed = (dy_p * fp_ref[:, :POOL_W].astype(F32)).astype(BF16)
        dg_p = dy_p * fp_ref[:, POOL_W:2 * POOL_W].astype(F32)
        dcps_ref[3:4, :] += jnp.sum(dy_p * fp_ref[:, 2 * POOL_W:3 * POOL_W].astype(F32), axis=0, keepdims=True)
        counts = _window_counts(tile * BWD_TILE, BWD_TILE)
        du_p, q_head = [], []
        for g, w in enumerate(POOL_WINDOWS):
            cols = slice(g * GROUP_D, (g + 1) * GROUP_D)
            dm_g = dmixed[:, cols]
            dpooled_g = _dot_nt(dm_g, wpool_ref[g])
            gwpool_acc[g] += _dot_tn(fp_ref[:, 3 * POOL_W + g * GROUP_D:3 * POOL_W + (g + 1) * GROUP_D], dm_g)
            q_g = dpooled_g * (1.0 / jnp.minimum(counts, float(w)))
            q_head.append(q_g[:POOL_HALO])
            s = jnp.concatenate([q_g, qc_ref[:, cols]], axis=0)
            step = 1
            while step < w:
                s = s + _rows_from_after(s, step)
                step *= 2
            du_p.append(s[:BWD_TILE] - dpooled_g)
        qc_ref[...] = jnp.concatenate(q_head, axis=1)
        dproj_b = jnp.concatenate([du_a, db_a, dc_a, dg_a] + du_p + [dg_p], axis=1).astype(BF16)
        dproj_ref[...] = dproj_b

        x_t = x_ref[...]
        rx = lax.rsqrt(jnp.mean(x_t * x_t, axis=-1, keepdims=True) + NORM_EPS)
        xn = x_t * rx
        mod_scale = 1.0 + scale
        dh = _dot_nt(dproj_b, win_ref[...])
        dvec_ref[0:1, :] += jnp.sum(dh, axis=0, keepdims=True)
        dvec_ref[1:2, :] += jnp.sum(dh * xn, axis=0, keepdims=True)
        dxn = dh * (g_pre * mod_scale)
        dx_ref[...] = dxo_t + rx * (dxn - xn * jnp.mean(dxn * xn, axis=-1, keepdims=True))

        @pl.when(i == n_tiles - 1)
        def _():
            gwpool_ref[...] = gwpool_acc[...].astype(BF16)
            sum_dh_xn, sum_dxo_yh = dvec_ref[1:2, :], dvec_ref[2:3, :]
            dvec_ref[1:2, :] = sum_dh_xn * g_pre
            dvec_ref[3:4, :] = sum_dh_xn * mod_scale
            dvec_ref[2:3, :] = sum_dxo_yh * g_post
            dvec_ref[4:5, :] = sum_dxo_yh * gate

    return pl.pallas_call(
        body, name=f"backward_layer_{layer}", grid=(n_tiles,),
        in_specs=[rev(D_MODEL), rev(D_MODEL), rev(D_MODEL), rev(2 * CONV_W), halo_spec, rev(3 * CONV_W),
                  rev(4 * POOL_W), _layer_spec(vec.shape, layer), _layer_spec(cps.shape, layer),
                  _layer_spec(wpool.shape, layer), _whole_spec(wout.shape), _whole_spec(win.shape)],
        out_specs=[rev(D_MODEL), rev(IN_COLS), rev(D_MODEL), _whole_spec(gwpool_shape),
                   _whole_spec((SUBLANES, CONV_W)), _whole_spec((SUBLANES, D_MODEL))],
        out_shape=[jax.ShapeDtypeStruct((t_len, D_MODEL), F32), jax.ShapeDtypeStruct((t_len, IN_COLS), BF16),
                   jax.ShapeDtypeStruct((t_len, D_MODEL), BF16), jax.ShapeDtypeStruct(gwpool_shape, BF16),
                   jax.ShapeDtypeStruct((SUBLANES, CONV_W), F32), jax.ShapeDtypeStruct((SUBLANES, D_MODEL), F32)],
        scratch_shapes=[pltpu.VMEM(gwpool_shape, F32), pltpu.VMEM((CONV_HALO, CONV_W), F32),
                        pltpu.VMEM((POOL_HALO, POOL_W), F32)],
        compiler_params=_params(dimension_semantics=("arbitrary",)),
    )(dxo, y, x, uc, uc, fa, fp, vec, cps, wpool, wout, win)


def _weight_grads(layer, h, dproj, ycat, dy, exchange, after=()):
    t_len = dy.shape[0]
    n_in, n_out = IN_COLS // GWIN_COLS, D_MODEL // GWOUT_COLS
    args = [h, dproj, ycat, dy, *after]
    in_specs = [_whole_spec(h.shape),
                pl.BlockSpec((t_len, GWIN_COLS), lambda s: (0, jnp.minimum(s, n_in - 1))),
                _whole_spec(ycat.shape),
                pl.BlockSpec((t_len, GWOUT_COLS), lambda s: (0, jnp.maximum(s - n_in, 0)))] + [HBM] * len(after)
    out_shape = [jax.ShapeDtypeStruct((D_MODEL, IN_COLS), BF16), jax.ShapeDtypeStruct((D_MODEL, D_MODEL), BF16)]
    out_specs = [pl.BlockSpec((D_MODEL, GWIN_COLS), lambda s: (0, jnp.minimum(s, n_in - 1))),
                 pl.BlockSpec((D_MODEL, GWOUT_COLS), lambda s: (0, jnp.maximum(s - n_in, 0)))]
    scratch = []
    aliases, split = _host(exchange, args, in_specs, out_shape, out_specs, scratch)

    def body(*refs):
        (h_ref, dproj_ref, ycat_ref, dy_ref, *_, gwin_ref, gwout_ref), hosted = split(refs)
        s = pl.program_id(0)
        if hosted is not None:
            pl.when(s == 0)(hosted[0])

        @pl.when(s < n_in)
        def _():
            gwin_ref[...] = _dot_tn(h_ref[...], dproj_ref[...]).astype(BF16)

        @pl.when(s >= n_in)
        def _():
            gwout_ref[...] = _dot_tn(ycat_ref[...], dy_ref[...]).astype(BF16)

        if hosted is not None:
            pl.when(s == n_in + n_out - 1)(hosted[1])

    return pl.pallas_call(
        body, name=f"weight_grads_{layer}", grid=(n_in + n_out,), in_specs=in_specs, out_specs=out_specs,
        out_shape=out_shape, scratch_shapes=scratch, input_output_aliases=aliases,
        compiler_params=_params(dimension_semantics=("arbitrary",)),
    )(*args)


def _add_sibling_blocks(name, layer, grads, received, core, partials, kinds):
    n_arr = len(grads)

    def body(core_ref, *refs):
        mine, theirs, outs = refs[:n_arr], refs[n_arr:2 * n_arr], refs[-n_arr:]
        for a in range(n_arr):
            outs[a][...] = (mine[a][...].astype(F32) + theirs[a][...].astype(F32)).astype(BF16)

    own_of_kind = [
        pl.BlockSpec((D_MODEL, W_IN_SHARD), lambda q, core_ref: (0, 2 * q + core_ref[0])),
        pl.BlockSpec((W_OUT_SHARD, D_MODEL), lambda q, core_ref: (2 * q + core_ref[0], 0)),
        pl.BlockSpec((POOL_SHARD, GROUP_D), lambda q, core_ref: (2 * q + core_ref[0], 0)),
    ]
    shapes = [_BLOCK_SHAPES[k] for k in kinds]
    recv_specs = [pl.BlockSpec((None,) + s, lambda q, core_ref: (q, 0, 0)) for s in shapes]
    out_specs = [pl.BlockSpec((None, None) + s, lambda q, core_ref: (q, layer, 0, 0)) for s in shapes]
    args = [core, *grads, *received]
    in_specs = [own_of_kind[k] for k in kinds] + recv_specs
    aliases = {}
    if partials is not None:
        aliases = {len(args) + a: a for a in range(n_arr)}
        args += list(partials)
        in_specs += [HBM] * n_arr
    return pl.pallas_call(
        body, name=name,
        grid_spec=pltpu.PrefetchScalarGridSpec(
            num_scalar_prefetch=1, grid=(N_CHIP,), in_specs=in_specs, out_specs=out_specs),
        out_shape=[jax.ShapeDtypeStruct((N_CHIP, DEPTH) + s, BF16) for s in shapes],
        input_output_aliases=aliases,
        compiler_params=_params(dimension_semantics=("arbitrary",)),
    )(*args)


def _modulation_columns(c_all, w_ada):
    def body(c_ref, w_ref, cact_ref, out_ref):
        c_t = c_ref[...]
        c_act = c_t * _sigmoid(c_t)
        cact_ref[...] = c_act
        out_ref[...] = jnp.dot(c_act, w_ref[...], preferred_element_type=F32, precision=lax.Precision.HIGHEST)

    return pl.pallas_call(
        body, name="modulation_columns", grid=(DEPTH,),
        in_specs=[pl.BlockSpec((N_DEV, D_MODEL), lambda l: (0, 0)),
                  pl.BlockSpec((None, D_MODEL, W_IN_SHARD), lambda l: (l, 0, 0))],
        out_specs=[pl.BlockSpec((N_DEV, D_MODEL), lambda l: (0, 0)),
                   pl.BlockSpec((N_DEV, W_IN_SHARD), lambda l: (0, l))],
        out_shape=[jax.ShapeDtypeStruct((N_DEV, D_MODEL), F32),
                   jax.ShapeDtypeStruct((N_DEV, DEPTH * W_IN_SHARD), F32)],
        compiler_params=_params(dimension_semantics=("arbitrary",)),
    )(c_all, w_ada)


def _adamw(w, g, m, v):
    m_new = ADAM_B1 * m + (1.0 - ADAM_B1) * g
    v_new = ADAM_B2 * v + (1.0 - ADAM_B2) * (g * g)
    m_hat = m_new / (1.0 - ADAM_B1 ** ADAM_STEP)
    v_hat = v_new / (1.0 - ADAM_B2 ** ADAM_STEP)
    delta = -ADAM_LR * (m_hat / (jnp.sqrt(v_hat) + ADAM_EPS) + ADAM_WD * w)
    return delta, m_new, v_new


def _adamw_w_ada(w, m, v, c_act_t, dmod_cols):
    def body(w_ref, m_ref, v_ref, ct_ref, dm_ref, g_ref, d_ref, mo_ref, vo_ref):
        g = ct_ref[:, 0:1] * dm_ref[0:1, :]
        for b in range(1, N_DEV):
            g = g + ct_ref[:, b:b + 1] * dm_ref[b:b + 1, :]
        g_ref[...] = g
        d_ref[...], mo_ref[...], vo_ref[...] = _adamw(w_ref[...], g, m_ref[...], v_ref[...])

    big = pl.BlockSpec((None, D_MODEL, W_IN_SHARD), lambda l: (l, 0, 0))
    return pl.pallas_call(
        body, name="adamw_w_ada", grid=(DEPTH,),
        in_specs=[big, big, big, pl.BlockSpec((D_MODEL, N_DEV), lambda l: (0, 0)),
                  pl.BlockSpec((None, N_DEV, W_IN_SHARD), lambda l: (l, 0, 0))],
        out_specs=[big] * 4, out_shape=[jax.ShapeDtypeStruct(w.shape, F32)] * 4,
        compiler_params=_params(dimension_semantics=("arbitrary",)),
    )(w, m, v, c_act_t, dmod_cols)


def _sum_chip_partials(own_ref, recv_ref):
    g = own_ref[...].astype(F32)
    for j in range(N_OTHER_CHIPS):
        g = g + recv_ref[j].astype(F32)
    return g


def _partial_specs(row_tile, cols, first_layer=0):
    own = pl.BlockSpec((None, None, row_tile, cols), lambda l, r, chip_ref: (chip_ref[0], first_layer + l, r, 0))
    recv = pl.BlockSpec((N_OTHER_CHIPS, None, row_tile, cols), lambda l, r, chip_ref: (0, first_layer + l, r, 0))
    return own, recv


def _adamw_reduced(name, w, m, v, partial, received, chip, row_tile, layers, continued):
    depth, rows, cols = w.shape
    first, stop = layers

    def body(chip_ref, w_ref, m_ref, v_ref, own_ref, recv_ref, *rest):
        g_ref, d_ref, mo_ref, vo_ref = rest[-4:]
        g = _sum_chip_partials(own_ref, recv_ref)
        g_ref[...] = g
        d_ref[...], mo_ref[...], vo_ref[...] = _adamw(w_ref[...], g, m_ref[...], v_ref[...])

    blk = pl.BlockSpec((None, row_tile, cols), lambda l, r, chip_ref: (first + l, r, 0))
    args = [chip, w, m, v, partial, received]
    in_specs = [blk, blk, blk, *_partial_specs(row_tile, cols, first)]
    aliases = {}
    if continued is not None:
        aliases = {len(args) + k: k for k in range(4)}
        args += list(continued)
        in_specs += [HBM] * 4
    return pl.pallas_call(
        body, name=name,
        grid_spec=pltpu.PrefetchScalarGridSpec(
            num_scalar_prefetch=1, grid=(stop - first, rows // row_tile), in_specs=in_specs, out_specs=[blk] * 4),
        out_shape=[jax.ShapeDtypeStruct(w.shape, F32)] * 4, input_output_aliases=aliases,
        compiler_params=_params(dimension_semantics=("arbitrary", "arbitrary")),
    )(*args)


def _reduce_w_pool(partial, received, chip):
    def body(chip_ref, own_ref, recv_ref, g_ref):
        g_ref[...] = _sum_chip_partials(own_ref, recv_ref)

    return pl.pallas_call(
        body, name="reduce_w_pool",
        grid_spec=pltpu.PrefetchScalarGridSpec(
            num_scalar_prefetch=1, grid=(DEPTH, 1), in_specs=list(_partial_specs(POOL_SHARD, GROUP_D)),
            out_specs=pl.BlockSpec((None, POOL_SHARD, GROUP_D), lambda l, r, chip_ref: (l, 0, 0))),
        out_shape=jax.ShapeDtypeStruct((DEPTH, POOL_SHARD, GROUP_D), F32),
        compiler_params=_params(dimension_semantics=("arbitrary", "arbitrary")),
    )(chip, partial, received)


def _adamw_small(name, params):
    n = len(params)

    def body(*refs):
        ins, outs = refs[:4 * n], refs[4 * n:]
        for p in range(n):
            w_ref, g_ref, m_ref, v_ref = ins[4 * p:4 * p + 4]
            d_ref, mo_ref, vo_ref = outs[3 * p:3 * p + 3]
            d_ref[...], mo_ref[...], vo_ref[...] = _adamw(w_ref[...], g_ref[...], m_ref[...], v_ref[...])

    vmem = pl.BlockSpec(memory_space=pltpu.VMEM)
    flat = [a for group in params for a in group]
    out_shape = [jax.ShapeDtypeStruct(group[0].shape, F32) for group in params for _ in range(3)]
    outs = pl.pallas_call(
        body, name=name, in_specs=[vmem] * len(flat), out_specs=[vmem] * len(out_shape),
        out_shape=out_shape, compiler_params=_params(),
    )(*flat)
    return [tuple(outs[3 * p:3 * p + 3]) for p in range(n)]


def _sum_sources(slabs):
    def body(s_ref, o_ref):
        acc = s_ref[0]
        for b in range(1, N_DEV):
            acc = acc + s_ref[b]
        o_ref[...] = acc

    vmem = pl.BlockSpec(memory_space=pltpu.VMEM)
    return pl.pallas_call(
        body, name="sum_small_grads", in_specs=[vmem], out_specs=vmem,
        out_shape=jax.ShapeDtypeStruct(slabs.shape[1:], F32), compiler_params=_params(),
    )(slabs)


def _to_bf16(a, name, layers=None):
    first, stop = layers or (0, a.shape[0])

    def body(a_ref, o_ref):
        o_ref[...] = a_ref[...].astype(BF16)

    block = (None,) + a.shape[1:]
    return pl.pallas_call(
        body, name=name, grid=(stop - first,), in_specs=[pl.BlockSpec(block, lambda l: (first + l, 0, 0))],
        out_specs=pl.BlockSpec(block, lambda l: (l, 0, 0)),
        out_shape=jax.ShapeDtypeStruct((stop - first,) + a.shape[1:], BF16),
        compiler_params=_params(dimension_semantics=("arbitrary",)),
    )(a)


def kernel(x, c, w_ada, b_ada, g_pre, w_in, w_conv, w_pool, pool_scale, w_out, g_post, loss_target, m_w_ada, m_b_ada, m_g_pre, m_w_in, m_w_conv, m_w_pool, m_pool_scale, m_w_out, m_g_post, v_w_ada, v_b_ada, v_g_pre, v_w_in, v_w_conv, v_w_pool, v_pool_scale, v_w_out, v_g_post):
    mx, my, mc = _mesh_position()
    me = _block_id(mx, my, mc)
    chip = (2 * mx + my).astype(jnp.int32).reshape(1)
    core = mc.astype(jnp.int32).reshape(1)
    x0 = x[0]
    target = loss_target[0]
    conv_shard = w_conv.shape[-1]

    own_small = jnp.concatenate([c, w_conv.reshape(1, DEPTH * 3 * conv_shard)], axis=1)
    first_shards = [_to_bf16(w_in, "cast_w_in_0", (0, 1)), _to_bf16(w_out, "cast_w_out_0", (0, 1))]
    all_small = _all_gather_small(own_small, "all_gather_c_w_conv", first_shards)[:, 0, :]
    gathers = [_gather_weights_start(0, *first_shards, [all_small], relayed=True)]
    c_all = all_small[:, :D_MODEL]
    w_conv_full = all_small[:, D_MODEL:].reshape(N_DEV, DEPTH, 3, conv_shard).transpose(1, 2, 0, 3).reshape(
        DEPTH, 3, CONV_W)
    cps = jnp.concatenate([w_conv_full, pool_scale[:, None], jnp.zeros((DEPTH, 4, CONV_W), F32)], axis=1)

    c_act, pieces = _modulation_columns(c_all, w_ada)
    upper = (1, DEPTH)
    upper_shards = [_to_bf16(w_in, "cast_w_in_1", upper), _to_bf16(w_out, "cast_w_out_1", upper)]
    wpool_b = _to_bf16(w_pool.reshape(DEPTH, POOL_ROWS, GROUP_D), "cast_w_pool").reshape(w_pool.shape)
    first_sems_0, _, (zones_0,) = gathers[0]
    neighbour_sems, zones_0 = _gather_weights_pass_on(
        "all_gather_weights_relay_0", NEIGHBOUR_CHIPS, first_sems_0[1], partial(_first_arrival, 0), zones_0,
        [pieces, *upper_shards, wpool_b], relay=True)
    mod_all = _all_gather_small(pieces, "all_gather_modulation", [zones_0[0]])
    mod_mine = lax.dynamic_index_in_dim(mod_all, me, axis=1, keepdims=False)
    mod = mod_mine.reshape(N_DEV, DEPTH, W_IN_SHARD).transpose(1, 0, 2).reshape(DEPTH, 3 * D_MODEL) + b_ada
    zeros_d = jnp.zeros((DEPTH, 3, D_MODEL), F32)
    vec = jnp.concatenate([mod.reshape(DEPTH, 3, D_MODEL), g_pre[:, None], g_post[:, None], zeros_d], axis=1)

    gathers.append(_gather_weights_start(1, *upper_shards, [mod_all]))
    gathers = [(first_sems, shards, landing[k], k) for first_sems, shards, landing in gathers
               for k in range(len(landing))]

    xs, kept, wins, wouts = [x0], [], [], []
    for l in range(DEPTH):
        first_sems, shards, zones, index = gathers[l]
        if l == 0:
            diagonal_sems, zones = _gather_weights_pass_on(
                "all_gather_weights_pass_on_0", DIAGONAL_CHIP, neighbour_sems[3], lambda a, j: a, zones_0,
                [vec, cps, gathers[-1][1][0]])
            passed = [(neighbour_sems[:2], NEIGHBOUR_CHIPS), (diagonal_sems, DIAGONAL_CHIP)]
            win, wout = _gather_weights_finish(l, index, first_sems, passed, shards, zones, neighbour_sems[2])
        else:
            passed_sems, zones = _gather_weights_pass_on(
                f"all_gather_weights_pass_on_{l}", ALL_OTHER_CHIPS, first_sems[1], partial(_first_arrival, index),
                zones, [xs[-1]])
            win, wout = _gather_weights_finish(l, index, first_sems, [(passed_sems, ALL_OTHER_CHIPS)], shards, zones)
        x_next, *for_backward = _forward_layer(
            l, xs[-1], vec, cps, wpool_b, win, wout, target if l == DEPTH - 1 else None)
        xs.append(x_next)
        kept.append(for_backward[:6])
        wins.append(win)
        wouts.append(wout)
    dx, loss_tile = xs[DEPTH], for_backward[6]

    slab_rows = [None] * DEPTH
    partials = received = None
    in_flight = []

    def scatter(layer, grads, from_sibling, after):
        nonlocal partials, received
        partials = _add_sibling_blocks(
            f"grad_add_sibling_{layer}", layer, grads, from_sibling, core, partials, ALL_KINDS)
        chips = _chips_exchange(layer, partials, received, ALL_KINDS)
        sems, partials, received, token = _start_exchange(chips, f"grad_chips_start_{layer}", after)
        in_flight.append((chips, sems, layer))
        return token

    grads_above = None
    for l in reversed(range(DEPTH)):
        y, h, ycat, uc, fa, fp = kept[l]
        dx, dproj, dy, gwpool, dcps, dvec = _backward_layer(
            l, dx, y, xs[l], uc, fa, fp, vec, cps, wpool_b, wouts[l], wins[l])
        slab_rows[l] = jnp.concatenate(
            [dvec[0], dvec[1], dvec[2], dvec[3], dvec[4], dcps[3], dcps[0], dcps[1], dcps[2],
             loss_tile[0] if l == 0 else jnp.zeros((LANES,), F32)])
        after = []
        if l == 0:
            gather_small = _all_gather_exchange(jnp.stack(slab_rows))
            sems_s, slab, slabs, token = _start_exchange(gather_small, "all_gather_small_grads_start")
            after = [token]
        hosted = _sibling_exchange(grads_above, ALL_KINDS) if grads_above is not None else None
        gwin, gwout, *from_sibling = _weight_grads(l, h, dproj, ycat, dy, hosted, after)
        if l == 0:
            _, (slabs,) = _finish_exchange(
                gather_small, "all_gather_small_grads_finish", sems_s, slab, slabs, [gwin])
        if grads_above is not None:
            scatter(l + 1, grads_above, from_sibling, [])
        grads_above = [gwin, gwout, gwpool.reshape(POOL_ROWS, GROUP_D)]
        if l <= 1:
            from_sibling = _run_exchange(_sibling_exchange(grads_above, ALL_KINDS), f"grad_exchange_sibling_{l}")
            token = scatter(l, grads_above, from_sibling, [slabs] if l == 0 else [])
            grads_above = None
    grad_x = dx[None]
    chips_0, sems_0, _ = in_flight.pop()

    total = _sum_sources(slabs)
    loss = total[0, SLAB_COLS]
    o = 3 * D_MODEL
    g_b_ada = total[:, :o]
    g_g_pre = total[:, o:o + D_MODEL]
    g_g_post = total[:, o + D_MODEL:o + 2 * D_MODEL]
    g_pool_scale = total[:, o + 2 * D_MODEL:o + 2 * D_MODEL + POOL_W]
    g_conv_full = total[:, o + 2 * D_MODEL + POOL_W:SLAB_COLS].reshape(DEPTH, 3, CONV_W)
    g_w_conv = lax.dynamic_slice_in_dim(g_conv_full, me * conv_shard, conv_shard, axis=2)

    after = [token]
    for chips, sems, l in in_flight:
        partials, received = _finish_exchange(chips, f"grad_chips_finish_{l}", sems, partials, received, after)
        after = []
    upper = (1, DEPTH)
    w_in_upper = _adamw_reduced(
        "adamw_w_in_upper", w_in, m_w_in, v_w_in, partials[0], received[0], chip, ROW_TILE, upper, None)
    w_out_upper = _adamw_reduced(
        "adamw_w_out_upper", w_out, m_w_out, v_w_out, partials[1], received[1], chip, W_OUT_SHARD, upper, None)
    dmod_all = slabs[:, :, :o].reshape(N_DEV, DEPTH, N_DEV, W_IN_SHARD)
    dmod_cols = lax.dynamic_index_in_dim(dmod_all, me, axis=2, keepdims=False).transpose(1, 0, 2) + token[0, 0]
    g_w_ada, d_w_ada, nm_w_ada, nv_w_ada = _adamw_w_ada(w_ada, m_w_ada, v_w_ada, c_act.T, dmod_cols)

    partials, received = _finish_exchange(
        chips_0, "grad_chips_finish_0", sems_0, partials, received, [nv_w_ada, w_in_upper[3], w_out_upper[3]])
    gather_pool = _all_gather_exchange(_reduce_w_pool(partials[2], received[2], chip), axis=1)
    sems_p, pool_rows, pool_landing, token_p = _start_exchange(gather_pool, "all_gather_grad_w_pool_start")
    g_w_in, d_w_in, nm_w_in, nv_w_in = _adamw_reduced(
        "adamw_w_in_0", w_in, m_w_in, v_w_in, partials[0], received[0], chip, ROW_TILE, (0, 1), w_in_upper)
    g_w_out, d_w_out, nm_w_out, nv_w_out = _adamw_reduced(
        "adamw_w_out_0", w_out, m_w_out, v_w_out, partials[1], received[1], chip, W_OUT_SHARD, (0, 1), w_out_upper)
    small = _adamw_small("adamw_small", [
        (b_ada, g_b_ada, m_b_ada, v_b_ada),
        (g_pre, g_g_pre, m_g_pre, v_g_pre),
        (w_conv, g_w_conv, m_w_conv, v_w_conv),
        (pool_scale, g_pool_scale, m_pool_scale, v_pool_scale),
        (g_post, g_g_post, m_g_post, v_g_post),
    ])
    (d_b_ada, nm_b_ada, nv_b_ada), (d_g_pre, nm_g_pre, nv_g_pre), (d_w_conv, nm_w_conv, nv_w_conv), \
        (d_ps, nm_ps, nv_ps), (d_g_post, nm_g_post, nv_g_post) = small
    _, (g_pool_all,) = _finish_exchange(
        gather_pool, "all_gather_grad_w_pool_finish", sems_p, pool_rows, pool_landing, [nv_w_in, nv_w_out, nv_g_post])
    g_w_pool = g_pool_all.reshape(w_pool.shape)
    ((d_w_pool, nm_w_pool, nv_w_pool),) = _adamw_small("adamw_w_pool", [(w_pool, g_w_pool, m_w_pool, v_w_pool)])

    return (loss, grad_x,
            g_w_ada, g_b_ada, g_g_pre, g_w_in, g_w_conv, g_w_pool, g_pool_scale, g_w_out, g_g_post,
            d_w_ada, d_b_ada, d_g_pre, d_w_in, d_w_conv, d_w_pool, d_ps, d_w_out, d_g_post,
            nm_w_ada, nm_b_ada, nm_g_pre, nm_w_in, nm_w_conv, nm_w_pool, nm_ps, nm_w_out, nm_g_post,
            nv_w_ada, nv_b_ada, nv_g_pre, nv_w_in, nv_w_conv, nv_w_pool, nv_ps, nv_w_out, nv_g_post)
```

```python
from functools import partial

import jax
import jax.numpy as jnp
from jax import lax
from jax.experimental import pallas as pl
from jax.experimental.pallas import tpu as pltpu

F32 = jnp.float32
BF16 = jnp.bfloat16

D_MODEL = 1024
DEPTH = 4
CONV_W = 512
POOL_W = 512
POOL_WINDOWS = (2, 4, 8, 16)
GROUP_D = 128
IN_COLS = 4 * CONV_W + 2 * POOL_W
NORM_EPS = 1e-6

ADAM_LR = 0.001
ADAM_B1 = 0.9
ADAM_B2 = 0.999
ADAM_EPS = 1e-08
ADAM_WD = 0.01
ADAM_STEP = 10

N_DEV = 8
N_CHIP = 4
N_OTHER_CHIPS = N_CHIP - 1
MESH = pl.DeviceIdType.MESH
W_IN_SHARD = IN_COLS // N_DEV
W_OUT_SHARD = D_MODEL // N_DEV
POOL_ROWS = len(POOL_WINDOWS) * GROUP_D
POOL_SHARD = POOL_ROWS // N_DEV

SUBLANES = 8
LANES = 128
VMEM_LIMIT_BYTES = 56 * 1024 * 1024
ROW_TILE = 512
BWD_TILE = 256
GWIN_COLS = 768
GWOUT_COLS = 512
POOL_HALO = 16
CONV_HALO = SUBLANES

SLAB_COLS = 3 * D_MODEL + D_MODEL + D_MODEL + POOL_W + 3 * CONV_W

HBM = pl.BlockSpec(memory_space=pl.ANY)


def _params(**kw):
    return pltpu.CompilerParams(vmem_limit_bytes=VMEM_LIMIT_BYTES, **kw)


def _sigmoid(v):
    return 1.0 / (1.0 + jnp.exp(-v))


def _dot(a, b):
    return jnp.dot(a, b, preferred_element_type=F32)


def _dot_tn(a, b):
    return lax.dot_general(a, b, (((0,), (0,)), ((), ())), preferred_element_type=F32)


def _dot_nt(a, b):
    return lax.dot_general(a, b, (((1,), (1,)), ((), ())), preferred_element_type=F32)


def _rows_from_before(v, k):
    return pltpu.roll(v, k, 0)


def _rows_from_after(v, k):
    return pltpu.roll(v, v.shape[0] - k, 0)


def _window_counts(t0, rows):
    return (lax.broadcasted_iota(jnp.int32, (rows, 1), 0) + (t0 + 1)).astype(F32)


def _split_proj(p32):
    cw = CONV_W
    return (p32[:, 0 * cw:1 * cw], p32[:, 1 * cw:2 * cw], p32[:, 2 * cw:3 * cw], p32[:, 3 * cw:4 * cw],
            p32[:, 4 * cw:4 * cw + POOL_W], p32[:, 4 * cw + POOL_W:])


def _layer_spec(shape, layer):
    nd = len(shape)
    return pl.BlockSpec((None,) + tuple(shape[1:]), lambda i, _l=layer, _n=nd: (_l,) + (0,) * (_n - 1))


def _whole_spec(shape):
    return pl.BlockSpec(tuple(shape), lambda i, _n=len(shape): (0,) * _n, pipeline_mode=pl.Buffered(1))


def _mesh_position():
    return lax.axis_index("x"), lax.axis_index("y"), lax.axis_index("c")


def _block_id(x, y, c):
    return 4 * x + 2 * y + c


def _other_chips(x, y):
    return [(x ^ 1, y), (x, y ^ 1), (x ^ 1, y ^ 1)]


def _col_block(ref, blk):
    return ref.at[:, pl.ds(pl.multiple_of(blk * W_IN_SHARD, LANES), W_IN_SHARD)]


def _row_block(rows):
    def block(ref, blk):
        return ref.at[pl.ds(pl.multiple_of(blk * rows, rows), rows), :]
    return block


_BLOCK_OF = (_col_block, _row_block(W_OUT_SHARD), _row_block(POOL_SHARD))
_BLOCK_SHAPES = ((D_MODEL, W_IN_SHARD), (W_OUT_SHARD, D_MODEL), (POOL_SHARD, GROUP_D))


class _Exchange:
    def __init__(self, inputs, out_shapes, aliases, sem_shapes, make):
        self.inputs, self.out_shapes, self.aliases, self.sem_shapes, self.make = (
            list(inputs), list(out_shapes), dict(aliases), list(sem_shapes), make)


def _run_exchange(exchange, name):
    n_in, n_out = len(exchange.inputs), len(exchange.out_shapes)

    def body(*refs):
        start, finish = exchange.make(refs[:n_in], refs[n_in:n_in + n_out], refs[n_in + n_out:])
        start()
        finish()

    return pl.pallas_call(
        body, name=name, in_specs=[HBM] * n_in, out_specs=[HBM] * n_out, out_shape=exchange.out_shapes,
        scratch_shapes=exchange.sem_shapes, input_output_aliases=exchange.aliases, compiler_params=_params(),
    )(*exchange.inputs)


_SEM = pl.BlockSpec(memory_space=pltpu.SEMAPHORE)
_DATAFLOW = pltpu.SideEffectType.DATAFLOW_SIDE_EFFECTING


def _start_exchange(exchange, name, after=()):
    n_in, n_out, n_sem = len(exchange.inputs), len(exchange.out_shapes), len(exchange.sem_shapes)
    sources = [i for i in range(n_in) if i not in exchange.aliases]
    aliases = {i: n_sem + k for k, i in enumerate(sources)}
    aliases.update({i: n_sem + len(sources) + o for i, o in exchange.aliases.items()})

    def body(*refs):
        in_refs = refs[:n_in]
        outs = refs[n_in + len(after):]
        sems = outs[:n_sem]
        out_refs = outs[n_sem + len(sources):n_sem + len(sources) + n_out]
        exchange.make(in_refs, out_refs, sems)[0]()
        refs[-1][...] = jnp.zeros_like(refs[-1])

    outs = pl.pallas_call(
        body, name=name, in_specs=[HBM] * (n_in + len(after)),
        out_specs=[_SEM] * n_sem + [HBM] * (len(sources) + n_out) + [pl.BlockSpec(memory_space=pltpu.VMEM)],
        out_shape=(exchange.sem_shapes + [pltpu.HBM(exchange.inputs[i].shape, exchange.inputs[i].dtype) for i in sources]
                   + [pltpu.HBM(s.shape, s.dtype) for s in exchange.out_shapes]
                   + [jax.ShapeDtypeStruct((SUBLANES, LANES), F32)]),
        input_output_aliases=aliases, compiler_params=_params(has_side_effects=_DATAFLOW),
    )(*exchange.inputs, *after)
    return outs[:n_sem], outs[n_sem:n_sem + len(sources)], outs[n_sem + len(sources):-1], outs[-1]


def _finish_exchange(exchange, name, sems, sources, landing, after):
    n_src, n_out, n_sem = len(sources), len(landing), len(sems)
    n_in = len(exchange.inputs)
    source_at = [i for i in range(n_in) if i not in exchange.aliases]

    def body(*refs):
        src_refs, out_refs = refs[:n_src], refs[n_src:n_src + n_out]
        sem_refs = refs[n_src + n_out:n_src + n_out + n_sem]
        in_refs = [None] * n_in
        for k, i in enumerate(source_at):
            in_refs[i] = src_refs[k]
        for i, o in exchange.aliases.items():
            in_refs[i] = out_refs[o]
        exchange.make(in_refs, out_refs, sem_refs)[1]()

    arrays = list(sources) + list(landing)
    outs = pl.pallas_call(
        body, name=name, in_specs=[HBM] * len(arrays) + [_SEM] * n_sem + [HBM] * len(after),
        out_specs=[HBM] * len(arrays), out_shape=[pltpu.HBM(a.shape, a.dtype) for a in arrays],
        input_output_aliases={i: i for i in range(len(arrays))}, compiler_params=_params(has_side_effects=_DATAFLOW),
    )(*arrays, *sems, *after)
    return outs[:n_src], outs[n_src:]


N_GATHERED = 2
FIRST_COPIES = 1 + N_OTHER_CHIPS
_GATHERED_SHAPES = ((D_MODEL, IN_COLS), (D_MODEL, D_MODEL))
ALL_OTHER_CHIPS, NEIGHBOUR_CHIPS, DIAGONAL_CHIP = (0, 1, 2), (0, 1), (2,)


def _first_sem(layer, a, k):
    return (layer * N_GATHERED + a) * FIRST_COPIES + k


def _first_arrival(layer, a, j):
    return _first_sem(layer, a, 1 + j)


def _gather_copy(window_of, full_ref, blk, send_sem, recv_sem, to, src=None):
    window = window_of(full_ref, blk)
    return pltpu.make_async_remote_copy(
        src_ref=window if src is None else src, dst_ref=window, send_sem=send_sem, recv_sem=recv_sem,
        device_id=to, device_id_type=MESH)


def _first_copies(layer, shard_refs, full_refs, send_sems, recv_sems, local_sems, relayed):
    x, y, c = _mesh_position()
    me = _block_id(x, y, c)
    chips = [_other_chips(x, y)[j] for j in (NEIGHBOUR_CHIPS if relayed else ALL_OTHER_CHIPS)]
    own, remote = [], []
    for a in range(N_GATHERED):
        shard = shard_refs[a].at[layer]
        own.append(pltpu.make_async_copy(
            shard, _BLOCK_OF[a](full_refs[a], me), local_sems.at[layer * N_GATHERED + a]))
        targets = [(x, y, 1 - c)] + [(*chip, c) for chip in chips]
        remote += [_gather_copy(_BLOCK_OF[a], full_refs[a], me, send_sems.at[_first_sem(layer, a, k)],
                                recv_sems.at[_first_sem(layer, a, k)], to, src=shard)
                   for k, to in enumerate(targets)]
    return own, remote


def _gather_weights_start(first_layer, win_shards, wout_shards, after, relayed=False):
    n_layers = win_shards.shape[0]
    n_first = n_layers * N_GATHERED * FIRST_COPIES
    sem_shapes = [pltpu.SemaphoreType.DMA((n_first,)), pltpu.SemaphoreType.DMA((n_first,)),
                  pltpu.SemaphoreType.DMA((n_layers * N_GATHERED,))]
    shards = [win_shards, wout_shards]

    def body(win_sh, wout_sh, *rest):
        send_sems, recv_sems, local_sems, win_thru, wout_thru, *landing = rest[len(after):]
        for layer in range(n_layers):
            own, remote = _first_copies(layer, (win_sh, wout_sh), landing[N_GATHERED * layer:N_GATHERED * (layer + 1)],
                                        send_sems, recv_sems, local_sems, relayed)
            for cp in own + remote:
                cp.start()

    outs = pl.pallas_call(
        body, name=f"all_gather_weights_start_{first_layer}", in_specs=[HBM] * (2 + len(after)),
        out_specs=[_SEM] * 3 + [HBM] * (2 + n_layers * N_GATHERED),
        out_shape=(sem_shapes + [pltpu.HBM(s.shape, s.dtype) for s in shards]
                   + [pltpu.HBM(s, BF16) for _ in range(n_layers) for s in _GATHERED_SHAPES]),
        input_output_aliases={0: 3, 1: 4}, compiler_params=_params(has_side_effects=_DATAFLOW),
    )(*shards, *after)
    landing = outs[5:]
    return outs[:3], outs[3:5], [landing[N_GATHERED * l:N_GATHERED * (l + 1)] for l in range(n_layers)]


def _passed_on_copies(full_refs, send_sems, recv_sems, core_of_block, chips):
    x, y, c = _mesh_position()
    return [_gather_copy(_BLOCK_OF[a], full_refs[a], _block_id(*_other_chips(x, y)[j], core_of_block),
                         send_sems.at[a * N_OTHER_CHIPS + j], recv_sems.at[a * N_OTHER_CHIPS + j], (x, y, 1 - c))
            for a in range(N_GATHERED) for j in chips]


def _relayed_copies(full_refs, send_sems, recv_sems):
    x, y, c = _mesh_position()
    source, to = (x ^ (1 - c), y ^ c), (x ^ c, y ^ (1 - c))
    return [_gather_copy(_BLOCK_OF[a], full_refs[a], _block_id(*source, c), send_sems.at[a], recv_sems.at[a], (*to, c))
            for a in range(N_GATHERED)]


def _gather_weights_pass_on(name, chips, arrival_sems, arrival_sem_of, landing, after, relay=False):
    n = N_GATHERED * N_OTHER_CHIPS
    sem_shapes = [pltpu.SemaphoreType.DMA((n,))] * 2 + [pltpu.SemaphoreType.DMA((N_GATHERED,))] * (2 if relay else 0)

    def body(win_ref, wout_ref, arrivals, *rest):
        sems = rest[len(after):len(after) + len(sem_shapes)]
        x, y, c = _mesh_position()
        full_refs = (win_ref, wout_ref)
        passed = _passed_on_copies(full_refs, sems[0], sems[1], c, chips)
        relayed = _relayed_copies(full_refs, sems[2], sems[3]) if relay else []
        for a in range(N_GATHERED):
            for j in chips:
                sem = arrivals.at[arrival_sem_of(a, j)]
                _gather_copy(_BLOCK_OF[a], full_refs[a], _block_id(*_other_chips(x, y)[j], c), sem, sem,
                             (x, y, c)).wait_recv()
            for cp in relayed[a:a + 1] + passed[a * len(chips):(a + 1) * len(chips)]:
                cp.start()

    outs = pl.pallas_call(
        body, name=name, in_specs=[HBM] * N_GATHERED + [_SEM] + [HBM] * len(after),
        out_specs=[_SEM] * len(sem_shapes) + [HBM] * N_GATHERED,
        out_shape=sem_shapes + [pltpu.HBM(a.shape, a.dtype) for a in landing],
        input_output_aliases={a: len(sem_shapes) + a for a in range(N_GATHERED)},
        compiler_params=_params(has_side_effects=_DATAFLOW),
    )(*landing, arrival_sems, *after)
    return outs[:len(sem_shapes)], outs[len(sem_shapes):]


def _gather_weights_finish(layer, index, first_sems, passed, shards, landing, relay_send_sems=None):
    relayed = relay_send_sems is not None
    passed_sems = [sem for (send, recv), _ in passed for sem in (send, recv)] + ([relay_send_sems] if relayed else [])

    def body(win_ref, wout_ref, first_send, first_recv, local_sems, *rest):
        sems, (win_sh, wout_sh) = rest[:len(passed_sems)], rest[len(passed_sems):len(passed_sems) + 2]
        x, y, c = _mesh_position()
        full_refs = (win_ref, wout_ref)
        own, sent = _first_copies(index, (win_sh, wout_sh), full_refs, first_send, first_recv, local_sems, relayed)
        for a in range(N_GATHERED):
            sem = _first_sem(index, a, 0)
            _gather_copy(_BLOCK_OF[a], full_refs[a], _block_id(x, y, 1 - c), first_recv.at[sem], first_recv.at[sem],
                         (x, y, c)).wait_recv()
        for p, (_, chips) in enumerate(passed):
            for cp in _passed_on_copies(full_refs, sems[2 * p], sems[2 * p + 1], 1 - c, chips):
                cp.wait_recv()
            sent += _passed_on_copies(full_refs, sems[2 * p], sems[2 * p + 1], c, chips)
        if relayed:
            sent += _relayed_copies(full_refs, sems[-1], sems[-1])
        for cp in sent:
            cp.wait_send()
        for cp in own:
            cp.wait()

    return pl.pallas_call(
        body, name=f"all_gather_weights_finish_{layer}",
        in_specs=[HBM] * N_GATHERED + [_SEM] * (3 + len(passed_sems)) + [HBM] * 2,
        out_specs=[HBM] * N_GATHERED, out_shape=[pltpu.HBM(a.shape, a.dtype) for a in landing],
        input_output_aliases={a: a for a in range(N_GATHERED)}, compiler_params=_params(has_side_effects=_DATAFLOW),
    )(*landing, *first_sems, *passed_sems, *shards)


ALL_KINDS = (0, 1, 2)


def _sibling_exchange(grads, kinds):
    n_arr = len(grads)

    def make(in_refs, out_refs, sems):
        send_sems, recv_sems = sems
        x, y, c = _mesh_position()
        copies = [pltpu.make_async_remote_copy(
            src_ref=_BLOCK_OF[kinds[a]](in_refs[a], 2 * q + (1 - c)), dst_ref=out_refs[a].at[q],
            send_sem=send_sems.at[a * N_CHIP + q], recv_sem=recv_sems.at[a * N_CHIP + q],
            device_id=(x, y, 1 - c), device_id_type=MESH)
            for a in range(n_arr) for q in range(N_CHIP)]

        def start():
            for cp in copies:
                cp.start()

        def finish():
            for cp in copies:
                cp.wait_recv()
            for cp in copies:
                cp.wait_send()

        return start, finish

    return _Exchange(
        grads, [jax.ShapeDtypeStruct((N_CHIP,) + _BLOCK_SHAPES[k], BF16) for k in kinds], {},
        [pltpu.SemaphoreType.DMA((n_arr * N_CHIP,)), pltpu.SemaphoreType.DMA((n_arr * N_CHIP,))], make)


def _chips_exchange(layer, partials, received, kinds):
    n_arr = len(partials)

    def make(in_refs, out_refs, sems):
        send_sems, recv_sems = sems
        x, y, c = _mesh_position()
        copies = [pltpu.make_async_remote_copy(
            src_ref=in_refs[a].at[2 * qx + qy, layer], dst_ref=out_refs[a].at[j, layer],
            send_sem=send_sems.at[a * N_OTHER_CHIPS + j], recv_sem=recv_sems.at[a * N_OTHER_CHIPS + j],
            device_id=(qx, qy, c), device_id_type=MESH)
            for a in range(n_arr) for j, (qx, qy) in enumerate(_other_chips(x, y))]

        def start():
            for cp in copies:
                cp.start()

        def finish():
            for cp in copies:
                cp.wait_recv()
            for cp in copies:
                cp.wait_send()

        return start, finish

    inputs = list(partials)
    aliases = {}
    if received is not None:
        inputs += list(received)
        aliases = {n_arr + a: a for a in range(n_arr)}
    return _Exchange(
        inputs, [jax.ShapeDtypeStruct((N_OTHER_CHIPS, DEPTH) + _BLOCK_SHAPES[k], BF16) for k in kinds], aliases,
        [pltpu.SemaphoreType.DMA((n_arr * N_OTHER_CHIPS,)), pltpu.SemaphoreType.DMA((n_arr * N_OTHER_CHIPS,))], make)


def _all_gather_exchange(v, axis=0):
    def make(in_refs, out_refs, sems):
        send_sems, recv_sems, local_sem = sems
        x, y, c = _mesh_position()
        me = _block_id(x, y, c)
        block = lambda blk: out_refs[0].at[(slice(None),) * axis + (blk,)]
        own = pltpu.make_async_copy(in_refs[0], block(me), local_sem.at[0])
        sends, arrivals = [], []
        for k in range(1, N_DEV):
            px, py, pc = x ^ ((k >> 2) & 1), y ^ ((k >> 1) & 1), c ^ (k & 1)
            sends.append(pltpu.make_async_remote_copy(
                src_ref=in_refs[0], dst_ref=block(me), send_sem=send_sems.at[k - 1],
                recv_sem=recv_sems.at[k - 1], device_id=(px, py, pc), device_id_type=MESH))
            arrivals.append(pltpu.make_async_remote_copy(
                src_ref=in_refs[0], dst_ref=block(_block_id(px, py, pc)), send_sem=send_sems.at[k - 1],
                recv_sem=recv_sems.at[k - 1], device_id=(x, y, c), device_id_type=MESH))

        def start():
            for cp in [own] + sends:
                cp.start()

        def finish():
            for cp in arrivals:
                cp.wait_recv()
            for cp in sends:
                cp.wait_send()
            own.wait()

        return start, finish

    return _Exchange(
        [v], [jax.ShapeDtypeStruct(v.shape[:axis] + (N_DEV,) + v.shape[axis:], v.dtype)], {},
        [pltpu.SemaphoreType.DMA((N_DEV - 1,)), pltpu.SemaphoreType.DMA((N_DEV - 1,)),
         pltpu.SemaphoreType.DMA((1,))], make)


def _host(exchange, args, in_specs, out_shape, out_specs, scratch):
    n_own = (len(args), len(out_shape), len(scratch))
    if exchange is None:
        return {}, lambda refs: (refs, None)
    n_ex = (len(exchange.inputs), len(exchange.out_shapes), len(exchange.sem_shapes))
    aliases = {n_own[0] + i: n_own[1] + o for i, o in exchange.aliases.items()}
    args += exchange.inputs
    in_specs += [HBM] * n_ex[0]
    out_shape += exchange.out_shapes
    out_specs += [HBM] * n_ex[1]
    scratch += exchange.sem_shapes

    def split(refs):
        own, theirs, at = [], [], 0
        for mine, ex in zip(n_own, n_ex):
            own += refs[at:at + mine]
            theirs.append(refs[at + mine:at + mine + ex])
            at += mine + ex
        return own, exchange.make(*theirs)

    return aliases, split


def _all_gather_small(v, name, after=()):
    vmem = pl.BlockSpec(memory_space=pltpu.VMEM)

    def body(v_ref, *rest):
        out_ref, send_sems, recv_sems = rest[len(after):]
        x, y, c = _mesh_position()
        me = _block_id(x, y, c)
        out_ref[me] = v_ref[...]
        sends = []
        for k in range(1, N_DEV):
            px, py, pc = x ^ ((k >> 2) & 1), y ^ ((k >> 1) & 1), c ^ (k & 1)
            send = pltpu.make_async_remote_copy(
                src_ref=v_ref, dst_ref=out_ref.at[me], send_sem=send_sems.at[k - 1], recv_sem=recv_sems.at[k - 1],
                device_id=(px, py, pc), device_id_type=MESH)
            send.start()
            sends.append((send, _block_id(px, py, pc)))
        for k, (send, peer) in enumerate(sends):
            pltpu.make_async_remote_copy(
                src_ref=v_ref, dst_ref=out_ref.at[peer], send_sem=send_sems.at[k], recv_sem=recv_sems.at[k],
                device_id=(x, y, c), device_id_type=MESH).wait_recv()
        for send, _ in sends:
            send.wait_send()

    return pl.pallas_call(
        body, name=name, in_specs=[vmem] + [HBM] * len(after), out_specs=vmem,
        out_shape=jax.ShapeDtypeStruct((N_DEV,) + v.shape, v.dtype),
        scratch_shapes=[pltpu.SemaphoreType.DMA((N_DEV - 1,)), pltpu.SemaphoreType.DMA((N_DEV - 1,))],
        compiler_params=_params(),
    )(v, *after)


def _forward_layer(layer, x, vec, cps, wpool, win, wout, target=None):
    t_len = x.shape[0]
    n_tiles = t_len // ROW_TILE
    row = lambda cols: pl.BlockSpec((ROW_TILE, cols), lambda i: (i, 0))
    widths = (D_MODEL, 2 * CONV_W, 3 * CONV_W, 4 * POOL_W)
    head = target is not None

    def body(x_ref, vec_ref, cps_ref, wpool_ref, win_ref, wout_ref, *rest):
        target_ref = rest[0] if head else None
        xo_ref, y_ref, h_ref, ycat_ref, uc_ref, fa_ref, fp_ref = rest[head:head + 7]
        loss_ref = rest[head + 7] if head else None
        zc_ref, pc_ref = rest[-2:]
        i = pl.program_id(0)

        @pl.when(i == 0)
        def _():
            zc_ref[...] = jnp.zeros_like(zc_ref)
            pc_ref[...] = jnp.zeros_like(pc_ref)
            if head:
                loss_ref[...] = jnp.zeros_like(loss_ref)

        x_t = x_ref[...]
        shift, scale, gate = vec_ref[0:1, :], vec_ref[1:2, :], vec_ref[2:3, :]
        g_pre, g_post = vec_ref[3:4, :], vec_ref[4:5, :]
        w0, w1, w2, ps = cps_ref[0:1, :], cps_ref[1:2, :], cps_ref[2:3, :], cps_ref[3:4, :]
        rx = lax.rsqrt(jnp.mean(x_t * x_t, axis=-1, keepdims=True) + NORM_EPS)
        h = (x_t * rx) * g_pre * (1.0 + scale) + shift
        h_ref[...] = h.astype(BF16)
        proj = _dot(h.astype(BF16), win_ref[...])
        u_a, b_a, c_a, g_a, u_p, g_p = _split_proj(proj)
        uc_ref[...] = jnp.concatenate([u_a, c_a], axis=1).astype(BF16)

        z = c_a * u_a
        zcat = jnp.concatenate([zc_ref[...], z], axis=0)
        zc_ref[...] = z[ROW_TILE - CONV_HALO:]
        conv = (w0 * _rows_from_before(zcat, 2)[CONV_HALO:] + w1 * _rows_from_before(zcat, 1)[CONV_HALO:] + w2 * z)
        sig_a = _sigmoid(g_a)
        silu_a = g_a * sig_a
        b_conv = b_a * conv
        y_a = b_conv * silu_a
        fa_ref[...] = jnp.concatenate(
            [silu_a * conv, silu_a * b_a, b_conv * (sig_a + silu_a * (1.0 - sig_a))], axis=1).astype(BF16)

        pcat = jnp.concatenate([pc_ref[...], u_p], axis=0)
        pc_ref[...] = u_p[ROW_TILE - POOL_HALO:]
        counts = _window_counts(i * ROW_TILE, ROW_TILE)
        pooled, mixed = [], []
        for g, w in enumerate(POOL_WINDOWS):
            cols = slice(g * GROUP_D, (g + 1) * GROUP_D)
            s = pcat[:, cols]
            step = 1
            while step < w:
                s = s + _rows_from_before(s, step)
                step *= 2
            pooled_g = (s[POOL_HALO:] * (1.0 / jnp.minimum(counts, float(w))) - u_p[:, cols]).astype(BF16)
            pooled.append(pooled_g)
            mixed.append(_dot(pooled_g, wpool_ref[g]))
        mixed = jnp.concatenate(mixed, axis=1)
        sig_p = _sigmoid(g_p)
        silu_p = g_p * sig_p
        mixed_ps = mixed * ps
        y_p = mixed_ps * silu_p
        fp_ref[...] = jnp.concatenate(
            [(ps * silu_p).astype(BF16), (mixed_ps * (sig_p + silu_p * (1.0 - sig_p))).astype(BF16),
             (silu_p * mixed).astype(BF16)] + pooled, axis=1)

        ycat = jnp.concatenate([y_a, y_p], axis=1)
        ycat_ref[...] = ycat.astype(BF16)
        y_b = _dot(ycat.astype(BF16), wout_ref[...]).astype(BF16)
        y_ref[...] = y_b
        y_t = y_b.astype(F32)
        ry = lax.rsqrt(jnp.mean(y_t * y_t, axis=-1, keepdims=True) + NORM_EPS)
        x_next = x_t + gate * (y_t * ry * g_post)
        if head:
            err = x_next - target_ref[...]
            xo_ref[...] = err * (1.0 / D_MODEL)
            loss_ref[...] += jnp.sum(err * err) * (0.5 / D_MODEL)
        else:
            xo_ref[...] = x_next

    tile = (SUBLANES, LANES)
    return pl.pallas_call(
        body, name=f"forward_layer_{layer}", grid=(n_tiles,),
        in_specs=[row(D_MODEL), _layer_spec(vec.shape, layer), _layer_spec(cps.shape, layer),
                  _layer_spec(wpool.shape, layer), _whole_spec(win.shape), _whole_spec(wout.shape)]
        + [row(D_MODEL)] * head,
        out_specs=[row(D_MODEL), row(D_MODEL), row(D_MODEL), row(D_MODEL)] + [row(w) for w in widths[1:]]
        + [_whole_spec(tile)] * head,
        out_shape=[jax.ShapeDtypeStruct((t_len, D_MODEL), F32), jax.ShapeDtypeStruct((t_len, D_MODEL), BF16),
                   jax.ShapeDtypeStruct((t_len, D_MODEL), BF16), jax.ShapeDtypeStruct((t_len, D_MODEL), BF16)]
        + [jax.ShapeDtypeStruct((t_len, w), BF16) for w in widths[1:]] + [jax.ShapeDtypeStruct(tile, F32)] * head,
        scratch_shapes=[pltpu.VMEM((CONV_HALO, CONV_W), F32), pltpu.VMEM((POOL_HALO, POOL_W), F32)],
        compiler_params=_params(dimension_semantics=("arbitrary",)),
    )(x, vec, cps, wpool, win, wout, *([target] * head))


def _backward_layer(layer, dxo, y, x, uc, fa, fp, vec, cps, wpool, wout, win, after):
    t_len = dxo.shape[0]
    n_tiles = t_len // BWD_TILE
    halo_per_tile = BWD_TILE // POOL_HALO
    rev = lambda cols: pl.BlockSpec((BWD_TILE, cols), lambda i: (n_tiles - 1 - i, 0))
    halo_spec = pl.BlockSpec(
        (POOL_HALO, 2 * CONV_W), lambda i: (jnp.maximum((n_tiles - 1 - i) * halo_per_tile - 1, 0), 0))
    gwpool_shape = (len(POOL_WINDOWS), GROUP_D, GROUP_D)

    def body(dxo_ref, y_ref, x_ref, uc_ref, uch_ref, fa_ref, fp_ref, vec_ref, cps_ref, wpool_ref, wout_ref, win_ref,
             *rest):
        dx_ref, dproj_ref, dy_ref, gwpool_ref, dcps_ref, dvec_ref, gwpool_acc, dcc_ref, qc_ref = rest[len(after):]
        i = pl.program_id(0)
        tile = n_tiles - 1 - i

        @pl.when(i == 0)
        def _():
            gwpool_acc[...] = jnp.zeros_like(gwpool_acc)
            dcps_ref[...] = jnp.zeros_like(dcps_ref)
            dvec_ref[...] = jnp.zeros_like(dvec_ref)
            dcc_ref[...] = jnp.zeros_like(dcc_ref)
            qc_ref[...] = jnp.zeros_like(qc_ref)

        shift, scale, gate = vec_ref[0:1, :], vec_ref[1:2, :], vec_ref[2:3, :]
        g_pre, g_post = vec_ref[3:4, :], vec_ref[4:5, :]
        w0, w1, w2 = cps_ref[0:1, :], cps_ref[1:2, :], cps_ref[2:3, :]

        dxo_t = dxo_ref[...]
        y_t = y_ref[...].astype(F32)
        ry = lax.rsqrt(jnp.mean(y_t * y_t, axis=-1, keepdims=True) + NORM_EPS)
        yh = y_t * ry
        dvec_ref[2:3, :] += jnp.sum(dxo_t * yh, axis=0, keepdims=True)
        dyh = dxo_t * (gate * g_post)
        dy_b = (ry * (dyh - yh * jnp.mean(dyh * yh, axis=-1, keepdims=True))).astype(BF16)
        dy_ref[...] = dy_b
        dycat = _dot_nt(dy_b, wout_ref[...])
        dy_a, dy_p = dycat[:, :CONV_W], dycat[:, CONV_W:]

        fa_t = fa_ref[...].astype(F32)
        db_a = dy_a * fa_t[:, :CONV_W]
        dconv = dy_a * fa_t[:, CONV_W:2 * CONV_W]
        dg_a = dy_a * fa_t[:, 2 * CONV_W:]
        uc_t = uc_ref[...].astype(F32)
        u_a, c_a = uc_t[:, :CONV_W], uc_t[:, CONV_W:]
        halo = jnp.where(tile > 0, uch_ref[...].astype(F32), 0.0)[POOL_HALO - CONV_HALO:]
        z = c_a * u_a
        zcat = jnp.concatenate([halo[:, CONV_W:] * halo[:, :CONV_W], z], axis=0)
        z1 = _rows_from_before(zcat, 1)[CONV_HALO:]
        z2 = _rows_from_before(zcat, 2)[CONV_HALO:]
        dccat = jnp.concatenate([dconv, dcc_ref[...]], axis=0)
        dc1 = _rows_from_after(dccat, 1)[:BWD_TILE]
        dc2 = _rows_from_after(dccat, 2)[:BWD_TILE]
        dz = w2 * dconv + w1 * dc1 + w0 * dc2
        dcc_ref[...] = dconv[:CONV_HALO]
        dcps_ref[0:1, :] += jnp.sum(dconv * z2, axis=0, keepdims=True)
        dcps_ref[1:2, :] += jnp.sum(dconv * z1, axis=0, keepdims=True)
        dcps_ref[2:3, :] += jnp.sum(dconv * z, axis=0, keepdims=True)
        du_a = dz * c_a
        dc_a = dz * u_a

        dmixed = (dy_p * fp_ref[:, :POOL_W].astype(F32)).astype(BF16)
        dg_p = dy_p * fp_ref[:, POOL_W:2 * POOL_W].astype(F32)
        dcps_ref[3:4, :] += jnp.sum(dy_p * fp_ref[:, 2 * POOL_W:3 * POOL_W].astype(F32), axis=0, keepdims=True)
        counts = _window_counts(tile * BWD_TILE, BWD_TILE)
        du_p, q_head = [], []
        for g, w in enumerate(POOL_WINDOWS):
            cols = slice(g * GROUP_D, (g + 1) * GROUP_D)
            dm_g = dmixed[:, cols]
            dpooled_g = _dot_nt(dm_g, wpool_ref[g])
            gwpool_acc[g] += _dot_tn(fp_ref[:, 3 * POOL_W + g * GROUP_D:3 * POOL_W + (g + 1) * GROUP_D], dm_g)
            q_g = dpooled_g * (1.0 / jnp.minimum(counts, float(w)))
            q_head.append(q_g[:POOL_HALO])
            s = jnp.concatenate([q_g, qc_ref[:, cols]], axis=0)
            step = 1
            while step < w:
                s = s + _rows_from_after(s, step)
                step *= 2
            du_p.append(s[:BWD_TILE] - dpooled_g)
        qc_ref[...] = jnp.concatenate(q_head, axis=1)
        dproj_b = jnp.concatenate([du_a, db_a, dc_a, dg_a] + du_p + [dg_p], axis=1).astype(BF16)
        dproj_ref[...] = dproj_b

        x_t = x_ref[...]
        rx = lax.rsqrt(jnp.mean(x_t * x_t, axis=-1, keepdims=True) + NORM_EPS)
        xn = x_t * rx
        mod_scale = 1.0 + scale
        dh = _dot_nt(dproj_b, win_ref[...])
        dvec_ref[0:1, :] += jnp.sum(dh, axis=0, keepdims=True)
        dvec_ref[1:2, :] += jnp.sum(dh * xn, axis=0, keepdims=True)
        dxn = dh * (g_pre * mod_scale)
        dx_ref[...] = dxo_t + rx * (dxn - xn * jnp.mean(dxn * xn, axis=-1, keepdims=True))

        @pl.when(i == n_tiles - 1)
        def _():
            gwpool_ref[...] = gwpool_acc[...].astype(BF16)
            sum_dh_xn, sum_dxo_yh = dvec_ref[1:2, :], dvec_ref[2:3, :]
            dvec_ref[1:2, :] = sum_dh_xn * g_pre
            dvec_ref[3:4, :] = sum_dh_xn * mod_scale
            dvec_ref[2:3, :] = sum_dxo_yh * g_post
            dvec_ref[4:5, :] = sum_dxo_yh * gate

    return pl.pallas_call(
        body, name=f"backward_layer_{layer}", grid=(n_tiles,),
        in_specs=[rev(D_MODEL), rev(D_MODEL), rev(D_MODEL), rev(2 * CONV_W), halo_spec, rev(3 * CONV_W),
                  rev(4 * POOL_W), _layer_spec(vec.shape, layer), _layer_spec(cps.shape, layer),
                  _layer_spec(wpool.shape, layer), _whole_spec(wout.shape), _whole_spec(win.shape)]
        + [HBM] * len(after),
        out_specs=[rev(D_MODEL), rev(IN_COLS), rev(D_MODEL), _whole_spec(gwpool_shape),
                   _whole_spec((SUBLANES, CONV_W)), _whole_spec((SUBLANES, D_MODEL))],
        out_shape=[jax.ShapeDtypeStruct((t_len, D_MODEL), F32), jax.ShapeDtypeStruct((t_len, IN_COLS), BF16),
                   jax.ShapeDtypeStruct((t_len, D_MODEL), BF16), jax.ShapeDtypeStruct(gwpool_shape, BF16),
                   jax.ShapeDtypeStruct((SUBLANES, CONV_W), F32), jax.ShapeDtypeStruct((SUBLANES, D_MODEL), F32)],
        scratch_shapes=[pltpu.VMEM(gwpool_shape, F32), pltpu.VMEM((CONV_HALO, CONV_W), F32),
                        pltpu.VMEM((POOL_HALO, POOL_W), F32)],
        compiler_params=_params(dimension_semantics=("arbitrary",)),
    )(dxo, y, x, uc, uc, fa, fp, vec, cps, wpool, wout, win, *after)


def _weight_grads(layer, h, dproj, ycat, dy, exchange, after=()):
    t_len = dy.shape[0]
    n_in, n_out = IN_COLS // GWIN_COLS, D_MODEL // GWOUT_COLS
    args = [h, dproj, ycat, dy, *after]
    in_specs = [_whole_spec(h.shape),
                pl.BlockSpec((t_len, GWIN_COLS), lambda s: (0, jnp.minimum(s, n_in - 1))),
                _whole_spec(ycat.shape),
                pl.BlockSpec((t_len, GWOUT_COLS), lambda s: (0, jnp.maximum(s - n_in, 0)))] + [HBM] * len(after)
    out_shape = [jax.ShapeDtypeStruct((D_MODEL, IN_COLS), BF16), jax.ShapeDtypeStruct((D_MODEL, D_MODEL), BF16)]
    out_specs = [pl.BlockSpec((D_MODEL, GWIN_COLS), lambda s: (0, jnp.minimum(s, n_in - 1))),
                 pl.BlockSpec((D_MODEL, GWOUT_COLS), lambda s: (0, jnp.maximum(s - n_in, 0)))]
    scratch = []
    aliases, split = _host(exchange, args, in_specs, out_shape, out_specs, scratch)

    def body(*refs):
        (h_ref, dproj_ref, ycat_ref, dy_ref, *_, gwin_ref, gwout_ref), hosted = split(refs)
        s = pl.program_id(0)
        if hosted is not None:
            pl.when(s == 0)(hosted[0])

        @pl.when(s < n_in)
        def _():
            gwin_ref[...] = _dot_tn(h_ref[...], dproj_ref[...]).astype(BF16)

        @pl.when(s >= n_in)
        def _():
            gwout_ref[...] = _dot_tn(ycat_ref[...], dy_ref[...]).astype(BF16)

        if hosted is not None:
            pl.when(s == n_in + n_out - 1)(hosted[1])

    return pl.pallas_call(
        body, name=f"weight_grads_{layer}", grid=(n_in + n_out,), in_specs=in_specs, out_specs=out_specs,
        out_shape=out_shape, scratch_shapes=scratch, input_output_aliases=aliases,
        compiler_params=_params(dimension_semantics=("arbitrary",)),
    )(*args)


def _add_sibling_blocks(name, layer, grads, received, core, partials, kinds):
    n_arr = len(grads)

    def body(core_ref, *refs):
        mine, theirs, outs = refs[:n_arr], refs[n_arr:2 * n_arr], refs[-n_arr:]
        for a in range(n_arr):
            outs[a][...] = (mine[a][...].astype(F32) + theirs[a][...].astype(F32)).astype(BF16)

    own_of_kind = [
        pl.BlockSpec((D_MODEL, W_IN_SHARD), lambda q, core_ref: (0, 2 * q + core_ref[0])),
        pl.BlockSpec((W_OUT_SHARD, D_MODEL), lambda q, core_ref: (2 * q + core_ref[0], 0)),
        pl.BlockSpec((POOL_SHARD, GROUP_D), lambda q, core_ref: (2 * q + core_ref[0], 0)),
    ]
    shapes = [_BLOCK_SHAPES[k] for k in kinds]
    recv_specs = [pl.BlockSpec((None,) + s, lambda q, core_ref: (q, 0, 0)) for s in shapes]
    out_specs = [pl.BlockSpec((None, None) + s, lambda q, core_ref: (q, layer, 0, 0)) for s in shapes]
    args = [core, *grads, *received]
    in_specs = [own_of_kind[k] for k in kinds] + recv_specs
    aliases = {}
    if partials is not None:
        aliases = {len(args) + a: a for a in range(n_arr)}
        args += list(partials)
        in_specs += [HBM] * n_arr
    return pl.pallas_call(
        body, name=name,
        grid_spec=pltpu.PrefetchScalarGridSpec(
            num_scalar_prefetch=1, grid=(N_CHIP,), in_specs=in_specs, out_specs=out_specs),
        out_shape=[jax.ShapeDtypeStruct((N_CHIP, DEPTH) + s, BF16) for s in shapes],
        input_output_aliases=aliases,
        compiler_params=_params(dimension_semantics=("arbitrary",)),
    )(*args)


def _modulation_columns(c_all, w_ada):
    def body(c_ref, w_ref, cact_ref, out_ref):
        c_t = c_ref[...]
        c_act = c_t * _sigmoid(c_t)
        cact_ref[...] = c_act
        out_ref[...] = jnp.dot(c_act, w_ref[...], preferred_element_type=F32, precision=lax.Precision.HIGHEST)

    return pl.pallas_call(
        body, name="modulation_columns", grid=(DEPTH,),
        in_specs=[pl.BlockSpec((N_DEV, D_MODEL), lambda l: (0, 0)),
                  pl.BlockSpec((None, D_MODEL, W_IN_SHARD), lambda l: (l, 0, 0))],
        out_specs=[pl.BlockSpec((N_DEV, D_MODEL), lambda l: (0, 0)),
                   pl.BlockSpec((N_DEV, W_IN_SHARD), lambda l: (0, l))],
        out_shape=[jax.ShapeDtypeStruct((N_DEV, D_MODEL), F32),
                   jax.ShapeDtypeStruct((N_DEV, DEPTH * W_IN_SHARD), F32)],
        compiler_params=_params(dimension_semantics=("arbitrary",)),
    )(c_all, w_ada)


def _adamw(w, g, m, v):
    m_new = ADAM_B1 * m + (1.0 - ADAM_B1) * g
    v_new = ADAM_B2 * v + (1.0 - ADAM_B2) * (g * g)
    m_hat = m_new / (1.0 - ADAM_B1 ** ADAM_STEP)
    v_hat = v_new / (1.0 - ADAM_B2 ** ADAM_STEP)
    delta = -ADAM_LR * (m_hat / (jnp.sqrt(v_hat) + ADAM_EPS) + ADAM_WD * w)
    return delta, m_new, v_new


def _adamw_w_ada(w, m, v, c_act_t, dmod_cols):
    def body(w_ref, m_ref, v_ref, ct_ref, dm_ref, g_ref, d_ref, mo_ref, vo_ref):
        g = ct_ref[:, 0:1] * dm_ref[0:1, :]
        for b in range(1, N_DEV):
            g = g + ct_ref[:, b:b + 1] * dm_ref[b:b + 1, :]
        g_ref[...] = g
        d_ref[...], mo_ref[...], vo_ref[...] = _adamw(w_ref[...], g, m_ref[...], v_ref[...])

    big = pl.BlockSpec((None, D_MODEL, W_IN_SHARD), lambda l: (l, 0, 0))
    return pl.pallas_call(
        body, name="adamw_w_ada", grid=(DEPTH,),
        in_specs=[big, big, big, pl.BlockSpec((D_MODEL, N_DEV), lambda l: (0, 0)),
                  pl.BlockSpec((None, N_DEV, W_IN_SHARD), lambda l: (l, 0, 0))],
        out_specs=[big] * 4, out_shape=[jax.ShapeDtypeStruct(w.shape, F32)] * 4,
        compiler_params=_params(dimension_semantics=("arbitrary",)),
    )(w, m, v, c_act_t, dmod_cols)


def _sum_chip_partials(own_ref, recv_ref):
    g = own_ref[...].astype(F32)
    for j in range(N_OTHER_CHIPS):
        g = g + recv_ref[j].astype(F32)
    return g


def _partial_specs(row_tile, cols, first_layer=0):
    own = pl.BlockSpec((None, None, row_tile, cols), lambda l, r, chip_ref: (chip_ref[0], first_layer + l, r, 0))
    recv = pl.BlockSpec((N_OTHER_CHIPS, None, row_tile, cols), lambda l, r, chip_ref: (0, first_layer + l, r, 0))
    return own, recv


def _adamw_reduced(name, w, m, v, partial, received, chip, row_tile, layers, continued):
    depth, rows, cols = w.shape
    first, stop = layers

    def body(chip_ref, w_ref, m_ref, v_ref, own_ref, recv_ref, *rest):
        g_ref, d_ref, mo_ref, vo_ref = rest[-4:]
        g = _sum_chip_partials(own_ref, recv_ref)
        g_ref[...] = g
        d_ref[...], mo_ref[...], vo_ref[...] = _adamw(w_ref[...], g, m_ref[...], v_ref[...])

    blk = pl.BlockSpec((None, row_tile, cols), lambda l, r, chip_ref: (first + l, r, 0))
    args = [chip, w, m, v, partial, received]
    in_specs = [blk, blk, blk, *_partial_specs(row_tile, cols, first)]
    aliases = {}
    if continued is not None:
        aliases = {len(args) + k: k for k in range(4)}
        args += list(continued)
        in_specs += [HBM] * 4
    return pl.pallas_call(
        body, name=name,
        grid_spec=pltpu.PrefetchScalarGridSpec(
            num_scalar_prefetch=1, grid=(stop - first, rows // row_tile), in_specs=in_specs, out_specs=[blk] * 4),
        out_shape=[jax.ShapeDtypeStruct(w.shape, F32)] * 4, input_output_aliases=aliases,
        compiler_params=_params(dimension_semantics=("arbitrary", "arbitrary")),
    )(*args)


def _reduce_w_pool(partial, received, chip):
    def body(chip_ref, own_ref, recv_ref, g_ref):
        g_ref[...] = _sum_chip_partials(own_ref, recv_ref)

    return pl.pallas_call(
        body, name="reduce_w_pool",
        grid_spec=pltpu.PrefetchScalarGridSpec(
            num_scalar_prefetch=1, grid=(DEPTH, 1), in_specs=list(_partial_specs(POOL_SHARD, GROUP_D)),
            out_specs=pl.BlockSpec((None, POOL_SHARD, GROUP_D), lambda l, r, chip_ref: (l, 0, 0))),
        out_shape=jax.ShapeDtypeStruct((DEPTH, POOL_SHARD, GROUP_D), F32),
        compiler_params=_params(dimension_semantics=("arbitrary", "arbitrary")),
    )(chip, partial, received)


def _adamw_small(name, params):
    n = len(params)

    def body(*refs):
        ins, outs = refs[:4 * n], refs[4 * n:]
        for p in range(n):
            w_ref, g_ref, m_ref, v_ref = ins[4 * p:4 * p + 4]
            d_ref, mo_ref, vo_ref = outs[3 * p:3 * p + 3]
            d_ref[...], mo_ref[...], vo_ref[...] = _adamw(w_ref[...], g_ref[...], m_ref[...], v_ref[...])

    vmem = pl.BlockSpec(memory_space=pltpu.VMEM)
    flat = [a for group in params for a in group]
    out_shape = [jax.ShapeDtypeStruct(group[0].shape, F32) for group in params for _ in range(3)]
    outs = pl.pallas_call(
        body, name=name, in_specs=[vmem] * len(flat), out_specs=[vmem] * len(out_shape),
        out_shape=out_shape, compiler_params=_params(),
    )(*flat)
    return [tuple(outs[3 * p:3 * p + 3]) for p in range(n)]


def _sum_sources(slabs):
    def body(s_ref, o_ref):
        acc = s_ref[0]
        for b in range(1, N_DEV):
            acc = acc + s_ref[b]
        o_ref[...] = acc

    vmem = pl.BlockSpec(memory_space=pltpu.VMEM)
    return pl.pallas_call(
        body, name="sum_small_grads", in_specs=[vmem], out_specs=vmem,
        out_shape=jax.ShapeDtypeStruct(slabs.shape[1:], F32), compiler_params=_params(),
    )(slabs)


def _to_bf16(a, name, layers=None):
    first, stop = layers or (0, a.shape[0])

    def body(a_ref, o_ref):
        o_ref[...] = a_ref[...].astype(BF16)

    block = (None,) + a.shape[1:]
    return pl.pallas_call(
        body, name=name, grid=(stop - first,), in_specs=[pl.BlockSpec(block, lambda l: (first + l, 0, 0))],
        out_specs=pl.BlockSpec(block, lambda l: (l, 0, 0)),
        out_shape=jax.ShapeDtypeStruct((stop - first,) + a.shape[1:], BF16),
        compiler_params=_params(dimension_semantics=("arbitrary",)),
    )(a)


def kernel(x, c, w_ada, b_ada, g_pre, w_in, w_conv, w_pool, pool_scale, w_out, g_post, loss_target, m_w_ada, m_b_ada, m_g_pre, m_w_in, m_w_conv, m_w_pool, m_pool_scale, m_w_out, m_g_post, v_w_ada, v_b_ada, v_g_pre, v_w_in, v_w_conv, v_w_pool, v_pool_scale, v_w_out, v_g_post):
    mx, my, mc = _mesh_position()
    me = _block_id(mx, my, mc)
    chip = (2 * mx + my).astype(jnp.int32).reshape(1)
    core = mc.astype(jnp.int32).reshape(1)
    x0 = x[0]
    target = loss_target[0]
    conv_shard = w_conv.shape[-1]

    own_small = jnp.concatenate([c, w_conv.reshape(1, DEPTH * 3 * conv_shard)], axis=1)
    first_shards = [_to_bf16(w_in, "cast_w_in_0", (0, 1)), _to_bf16(w_out, "cast_w_out_0", (0, 1))]
    all_small = _all_gather_small(own_small, "all_gather_c_w_conv", first_shards)[:, 0, :]
    gathers = [_gather_weights_start(0, *first_shards, [all_small], relayed=True)]
    c_all = all_small[:, :D_MODEL]
    w_conv_full = all_small[:, D_MODEL:].reshape(N_DEV, DEPTH, 3, conv_shard).transpose(1, 2, 0, 3).reshape(
        DEPTH, 3, CONV_W)
    cps = jnp.concatenate([w_conv_full, pool_scale[:, None], jnp.zeros((DEPTH, 4, CONV_W), F32)], axis=1)

    c_act, pieces = _modulation_columns(c_all, w_ada)
    upper = (1, DEPTH)
    upper_shards = [_to_bf16(w_in, "cast_w_in_1", upper), _to_bf16(w_out, "cast_w_out_1", upper)]
    wpool_b = _to_bf16(w_pool.reshape(DEPTH, POOL_ROWS, GROUP_D), "cast_w_pool").reshape(w_pool.shape)
    first_sems_0, _, (zones_0,) = gathers[0]
    neighbour_sems, zones_0 = _gather_weights_pass_on(
        "all_gather_weights_relay_0", NEIGHBOUR_CHIPS, first_sems_0[1], partial(_first_arrival, 0), zones_0,
        [pieces, *upper_shards, wpool_b], relay=True)
    mod_all = _all_gather_small(pieces, "all_gather_modulation", [zones_0[0]])
    mod_mine = lax.dynamic_index_in_dim(mod_all, me, axis=1, keepdims=False)
    mod = mod_mine.reshape(N_DEV, DEPTH, W_IN_SHARD).transpose(1, 0, 2).reshape(DEPTH, 3 * D_MODEL) + b_ada
    zeros_d = jnp.zeros((DEPTH, 3, D_MODEL), F32)
    vec = jnp.concatenate([mod.reshape(DEPTH, 3, D_MODEL), g_pre[:, None], g_post[:, None], zeros_d], axis=1)

    gathers.append(_gather_weights_start(1, *upper_shards, [mod_all]))
    gathers = [(first_sems, shards, landing[k], k) for first_sems, shards, landing in gathers
               for k in range(len(landing))]

    xs, kept, wins, wouts = [x0], [], [], []
    for l in range(DEPTH):
        first_sems, shards, zones, index = gathers[l]
        if l == 0:
            diagonal_sems, zones = _gather_weights_pass_on(
                "all_gather_weights_pass_on_0", DIAGONAL_CHIP, neighbour_sems[3], lambda a, j: a, zones_0,
                [vec, cps, gathers[-1][1][0]])
            passed = [(neighbour_sems[:2], NEIGHBOUR_CHIPS), (diagonal_sems, DIAGONAL_CHIP)]
            win, wout = _gather_weights_finish(l, index, first_sems, passed, shards, zones, neighbour_sems[2])
        else:
            passed_sems, zones = _gather_weights_pass_on(
                f"all_gather_weights_pass_on_{l}", ALL_OTHER_CHIPS, first_sems[1], partial(_first_arrival, index),
                zones, [xs[-1]])
            win, wout = _gather_weights_finish(l, index, first_sems, [(passed_sems, ALL_OTHER_CHIPS)], shards, zones)
        x_next, *for_backward = _forward_layer(
            l, xs[-1], vec, cps, wpool_b, win, wout, target if l == DEPTH - 1 else None)
        xs.append(x_next)
        kept.append(for_backward[:6])
        wins.append(win)
        wouts.append(wout)
    dx, loss_tile = xs[DEPTH], for_backward[6]

    slab_rows = [None] * DEPTH
    partials = received = None
    in_flight = []

    def scatter(layer, grads, from_sibling, after):
        nonlocal partials, received
        partials = _add_sibling_blocks(
            f"grad_add_sibling_{layer}", layer, grads, from_sibling, core, partials, ALL_KINDS)
        chips = _chips_exchange(layer, partials, received, ALL_KINDS)
        sems, partials, received, token = _start_exchange(chips, f"grad_chips_start_{layer}", after)
        in_flight.append((chips, sems, layer))
        return token

    grads_above = None
    issued = []
    for l in reversed(range(DEPTH)):
        y, h, ycat, uc, fa, fp = kept[l]
        dx, dproj, dy, gwpool, dcps, dvec = _backward_layer(
            l, dx, y, xs[l], uc, fa, fp, vec, cps, wpool_b, wouts[l], wins[l], issued)
        slab_rows[l] = jnp.concatenate(
            [dvec[0], dvec[1], dvec[2], dvec[3], dvec[4], dcps[3], dcps[0], dcps[1], dcps[2],
             loss_tile[0] if l == 0 else jnp.zeros((LANES,), F32)])
        after = []
        if l == 0:
            gather_small = _all_gather_exchange(jnp.stack(slab_rows))
            sems_s, slab, slabs, token = _start_exchange(gather_small, "all_gather_small_grads_start")
            after = [token]
        hosted = _sibling_exchange(grads_above, ALL_KINDS) if grads_above is not None else None
        gwin, gwout, *from_sibling = _weight_grads(l, h, dproj, ycat, dy, hosted, after)
        if l == 0:
            _, (slabs,) = _finish_exchange(
                gather_small, "all_gather_small_grads_finish", sems_s, slab, slabs, [gwin])
        issued = [gwin]
        if grads_above is not None:
            issued = [scatter(l + 1, grads_above, from_sibling, [])]
        grads_above = [gwin, gwout, gwpool.reshape(POOL_ROWS, GROUP_D)]
        if l <= 1:
            from_sibling = _run_exchange(_sibling_exchange(grads_above, ALL_KINDS), f"grad_exchange_sibling_{l}")
            token = scatter(l, grads_above, from_sibling, [slabs] if l == 0 else [])
            issued = [token]
            grads_above = None
    grad_x = dx[None]
    chips_0, sems_0, _ = in_flight.pop()

    total = _sum_sources(slabs)
    loss = total[0, SLAB_COLS]
    o = 3 * D_MODEL
    g_b_ada = total[:, :o]
    g_g_pre = total[:, o:o + D_MODEL]
    g_g_post = total[:, o + D_MODEL:o + 2 * D_MODEL]
    g_pool_scale = total[:, o + 2 * D_MODEL:o + 2 * D_MODEL + POOL_W]
    g_conv_full = total[:, o + 2 * D_MODEL + POOL_W:SLAB_COLS].reshape(DEPTH, 3, CONV_W)
    g_w_conv = lax.dynamic_slice_in_dim(g_conv_full, me * conv_shard, conv_shard, axis=2)

    after = [token]
    for chips, sems, l in in_flight:
        partials, received = _finish_exchange(chips, f"grad_chips_finish_{l}", sems, partials, received, after)
        after = []
    upper = (1, DEPTH)
    w_in_upper = _adamw_reduced(
        "adamw_w_in_upper", w_in, m_w_in, v_w_in, partials[0], received[0], chip, ROW_TILE, upper, None)
    w_out_upper = _adamw_reduced(
        "adamw_w_out_upper", w_out, m_w_out, v_w_out, partials[1], received[1], chip, W_OUT_SHARD, upper, None)
    dmod_all = slabs[:, :, :o].reshape(N_DEV, DEPTH, N_DEV, W_IN_SHARD)
    dmod_cols = lax.dynamic_index_in_dim(dmod_all, me, axis=2, keepdims=False).transpose(1, 0, 2) + token[0, 0]
    g_w_ada, d_w_ada, nm_w_ada, nv_w_ada = _adamw_w_ada(w_ada, m_w_ada, v_w_ada, c_act.T, dmod_cols)

    partials, received = _finish_exchange(
        chips_0, "grad_chips_finish_0", sems_0, partials, received, [nv_w_ada, w_in_upper[3], w_out_upper[3]])
    gather_pool = _all_gather_exchange(_reduce_w_pool(partials[2], received[2], chip), axis=1)
    sems_p, pool_rows, pool_landing, token_p = _start_exchange(gather_pool, "all_gather_grad_w_pool_start")
    g_w_in, d_w_in, nm_w_in, nv_w_in = _adamw_reduced(
        "adamw_w_in_0", w_in, m_w_in, v_w_in, partials[0], received[0], chip, ROW_TILE, (0, 1), w_in_upper)
    g_w_out, d_w_out, nm_w_out, nv_w_out = _adamw_reduced(
        "adamw_w_out_0", w_out, m_w_out, v_w_out, partials[1], received[1], chip, W_OUT_SHARD, (0, 1), w_out_upper)
    small = _adamw_small("adamw_small", [
        (b_ada, g_b_ada, m_b_ada, v_b_ada),
        (g_pre, g_g_pre, m_g_pre, v_g_pre),
        (w_conv, g_w_conv, m_w_conv, v_w_conv),
        (pool_scale, g_pool_scale, m_pool_scale, v_pool_scale),
        (g_post, g_g_post, m_g_post, v_g_post),
    ])
    (d_b_ada, nm_b_ada, nv_b_ada), (d_g_pre, nm_g_pre, nv_g_pre), (d_w_conv, nm_w_conv, nv_w_conv), \
        (d_ps, nm_ps, nv_ps), (d_g_post, nm_g_post, nv_g_post) = small
    _, (g_pool_all,) = _finish_exchange(
        gather_pool, "all_gather_grad_w_pool_finish", sems_p, pool_rows, pool_landing, [nv_w_in, nv_w_out, nv_g_post])
    g_w_pool = g_pool_all.reshape(w_pool.shape)
    ((d_w_pool, nm_w_pool, nv_w_pool),) = _adamw_small("adamw_w_pool", [(w_pool, g_w_pool, m_w_pool, v_w_pool)])

    return (loss, grad_x,
            g_w_ada, g_b_ada, g_g_pre, g_w_in, g_w_conv, g_w_pool, g_pool_scale, g_w_out, g_g_post,
            d_w_ada, d_b_ada, d_g_pre, d_w_in, d_w_conv, d_w_pool, d_ps, d_w_out, d_g_post,
            nm_w_ada, nm_b_ada, nm_g_pre, nm_w_in, nm_w_conv, nm_w_pool, nm_ps, nm_w_out, nm_g_post,
            nv_w_ada, nv_b_ada, nv_g_pre, nv_w_in, nv_w_conv, nv_w_pool, nv_ps, nv_w_out, nv_g_post)
```

```python
from functools import partial

import jax
import jax.numpy as jnp
from jax import lax
from jax.experimental import pallas as pl
from jax.experimental.pallas import tpu as pltpu

F32 = jnp.float32
BF16 = jnp.bfloat16

D_MODEL = 1024
DEPTH = 4
CONV_W = 512
POOL_W = 512
POOL_WINDOWS = (2, 4, 8, 16)
GROUP_D = 128
IN_COLS = 4 * CONV_W + 2 * POOL_W
NORM_EPS = 1e-6

ADAM_LR = 0.001
ADAM_B1 = 0.9
ADAM_B2 = 0.999
ADAM_EPS = 1e-08
ADAM_WD = 0.01
ADAM_STEP = 10

N_DEV = 8
N_CHIP = 4
N_OTHER_CHIPS = N_CHIP - 1
MESH = pl.DeviceIdType.MESH
W_IN_SHARD = IN_COLS // N_DEV
W_OUT_SHARD = D_MODEL // N_DEV
POOL_ROWS = len(POOL_WINDOWS) * GROUP_D
POOL_SHARD = POOL_ROWS // N_DEV

SUBLANES = 8
LANES = 128
VMEM_LIMIT_BYTES = 56 * 1024 * 1024
ROW_TILE = 512
BWD_TILE = 256
GWIN_COLS = 768
GWOUT_COLS = 512
POOL_HALO = 16
CONV_HALO = SUBLANES

SLAB_COLS = 3 * D_MODEL + D_MODEL + D_MODEL + POOL_W + 3 * CONV_W

HBM = pl.BlockSpec(memory_space=pl.ANY)


def _params(**kw):
    return pltpu.CompilerParams(vmem_limit_bytes=VMEM_LIMIT_BYTES, **kw)


def _sigmoid(v):
    return 1.0 / (1.0 + jnp.exp(-v))


def _dot(a, b):
    return jnp.dot(a, b, preferred_element_type=F32)


def _dot_tn(a, b):
    return lax.dot_general(a, b, (((0,), (0,)), ((), ())), preferred_element_type=F32)


def _dot_nt(a, b):
    return lax.dot_general(a, b, (((1,), (1,)), ((), ())), preferred_element_type=F32)


def _rows_from_before(v, k):
    return pltpu.roll(v, k, 0)


def _rows_from_after(v, k):
    return pltpu.roll(v, v.shape[0] - k, 0)


def _window_counts(t0, rows):
    return (lax.broadcasted_iota(jnp.int32, (rows, 1), 0) + (t0 + 1)).astype(F32)


def _split_proj(p32):
    cw = CONV_W
    return (p32[:, 0 * cw:1 * cw], p32[:, 1 * cw:2 * cw], p32[:, 2 * cw:3 * cw], p32[:, 3 * cw:4 * cw],
            p32[:, 4 * cw:4 * cw + POOL_W], p32[:, 4 * cw + POOL_W:])


def _layer_spec(shape, layer):
    nd = len(shape)
    return pl.BlockSpec((None,) + tuple(shape[1:]), lambda i, _l=layer, _n=nd: (_l,) + (0,) * (_n - 1))


def _whole_spec(shape):
    return pl.BlockSpec(tuple(shape), lambda i, _n=len(shape): (0,) * _n, pipeline_mode=pl.Buffered(1))


def _mesh_position():
    return lax.axis_index("x"), lax.axis_index("y"), lax.axis_index("c")


def _block_id(x, y, c):
    return 4 * x + 2 * y + c


def _other_chips(x, y):
    return [(x ^ 1, y), (x, y ^ 1), (x ^ 1, y ^ 1)]


def _col_block(ref, blk):
    return ref.at[:, pl.ds(pl.multiple_of(blk * W_IN_SHARD, LANES), W_IN_SHARD)]


def _row_block(rows):
    def block(ref, blk):
        return ref.at[pl.ds(pl.multiple_of(blk * rows, rows), rows), :]
    return block


_BLOCK_OF = (_col_block, _row_block(W_OUT_SHARD), _row_block(POOL_SHARD))
_BLOCK_SHAPES = ((D_MODEL, W_IN_SHARD), (W_OUT_SHARD, D_MODEL), (POOL_SHARD, GROUP_D))


class _Exchange:
    def __init__(self, inputs, out_shapes, aliases, sem_shapes, make):
        self.inputs, self.out_shapes, self.aliases, self.sem_shapes, self.make = (
            list(inputs), list(out_shapes), dict(aliases), list(sem_shapes), make)


def _run_exchange(exchange, name):
    n_in, n_out = len(exchange.inputs), len(exchange.out_shapes)

    def body(*refs):
        start, finish = exchange.make(refs[:n_in], refs[n_in:n_in + n_out], refs[n_in + n_out:])
        start()
        finish()

    return pl.pallas_call(
        body, name=name, in_specs=[HBM] * n_in, out_specs=[HBM] * n_out, out_shape=exchange.out_shapes,
        scratch_shapes=exchange.sem_shapes, input_output_aliases=exchange.aliases, compiler_params=_params(),
    )(*exchange.inputs)


_SEM = pl.BlockSpec(memory_space=pltpu.SEMAPHORE)
_DATAFLOW = pltpu.SideEffectType.DATAFLOW_SIDE_EFFECTING


def _start_exchange(exchange, name, after=()):
    n_in, n_out, n_sem = len(exchange.inputs), len(exchange.out_shapes), len(exchange.sem_shapes)
    sources = [i for i in range(n_in) if i not in exchange.aliases]
    aliases = {i: n_sem + k for k, i in enumerate(sources)}
    aliases.update({i: n_sem + len(sources) + o for i, o in exchange.aliases.items()})

    def body(*refs):
        in_refs = refs[:n_in]
        outs = refs[n_in + len(after):]
        sems = outs[:n_sem]
        out_refs = outs[n_sem + len(sources):n_sem + len(sources) + n_out]
        exchange.make(in_refs, out_refs, sems)[0]()
        refs[-1][...] = jnp.zeros_like(refs[-1])

    outs = pl.pallas_call(
        body, name=name, in_specs=[HBM] * (n_in + len(after)),
        out_specs=[_SEM] * n_sem + [HBM] * (len(sources) + n_out) + [pl.BlockSpec(memory_space=pltpu.VMEM)],
        out_shape=(exchange.sem_shapes + [pltpu.HBM(exchange.inputs[i].shape, exchange.inputs[i].dtype) for i in sources]
                   + [pltpu.HBM(s.shape, s.dtype) for s in exchange.out_shapes]
                   + [jax.ShapeDtypeStruct((SUBLANES, LANES), F32)]),
        input_output_aliases=aliases, compiler_params=_params(has_side_effects=_DATAFLOW),
    )(*exchange.inputs, *after)
    return outs[:n_sem], outs[n_sem:n_sem + len(sources)], outs[n_sem + len(sources):-1], outs[-1]


def _finish_exchange(exchange, name, sems, sources, landing, after):
    n_src, n_out, n_sem = len(sources), len(landing), len(sems)
    n_in = len(exchange.inputs)
    source_at = [i for i in range(n_in) if i not in exchange.aliases]

    def body(*refs):
        src_refs, out_refs = refs[:n_src], refs[n_src:n_src + n_out]
        sem_refs = refs[n_src + n_out:n_src + n_out + n_sem]
        in_refs = [None] * n_in
        for k, i in enumerate(source_at):
            in_refs[i] = src_refs[k]
        for i, o in exchange.aliases.items():
            in_refs[i] = out_refs[o]
        exchange.make(in_refs, out_refs, sem_refs)[1]()

    arrays = list(sources) + list(landing)
    outs = pl.pallas_call(
        body, name=name, in_specs=[HBM] * len(arrays) + [_SEM] * n_sem + [HBM] * len(after),
        out_specs=[HBM] * len(arrays), out_shape=[pltpu.HBM(a.shape, a.dtype) for a in arrays],
        input_output_aliases={i: i for i in range(len(arrays))}, compiler_params=_params(has_side_effects=_DATAFLOW),
    )(*arrays, *sems, *after)
    return outs[:n_src], outs[n_src:]


N_GATHERED = 2
FIRST_COPIES = 1 + N_OTHER_CHIPS
_GATHERED_SHAPES = ((D_MODEL, IN_COLS), (D_MODEL, D_MODEL))
ALL_OTHER_CHIPS, NEIGHBOUR_CHIPS, DIAGONAL_CHIP = (0, 1, 2), (0, 1), (2,)


def _first_sem(layer, a, k):
    return (layer * N_GATHERED + a) * FIRST_COPIES + k


def _first_arrival(layer, a, j):
    return _first_sem(layer, a, 1 + j)


def _gather_copy(window_of, full_ref, blk, send_sem, recv_sem, to, src=None):
    window = window_of(full_ref, blk)
    return pltpu.make_async_remote_copy(
        src_ref=window if src is None else src, dst_ref=window, send_sem=send_sem, recv_sem=recv_sem,
        device_id=to, device_id_type=MESH)


def _first_copies(layer, shard_refs, full_refs, send_sems, recv_sems, local_sems, relayed):
    x, y, c = _mesh_position()
    me = _block_id(x, y, c)
    chips = [_other_chips(x, y)[j] for j in (NEIGHBOUR_CHIPS if relayed else ALL_OTHER_CHIPS)]
    own, remote = [], []
    for a in range(N_GATHERED):
        shard = shard_refs[a].at[layer]
        own.append(pltpu.make_async_copy(
            shard, _BLOCK_OF[a](full_refs[a], me), local_sems.at[layer * N_GATHERED + a]))
        targets = [(x, y, 1 - c)] + [(*chip, c) for chip in chips]
        remote += [_gather_copy(_BLOCK_OF[a], full_refs[a], me, send_sems.at[_first_sem(layer, a, k)],
                                recv_sems.at[_first_sem(layer, a, k)], to, src=shard)
                   for k, to in enumerate(targets)]
    return own, remote


def _gather_weights_start(first_layer, win_shards, wout_shards, after, relayed=False):
    n_layers = win_shards.shape[0]
    n_first = n_layers * N_GATHERED * FIRST_COPIES
    sem_shapes = [pltpu.SemaphoreType.DMA((n_first,)), pltpu.SemaphoreType.DMA((n_first,)),
                  pltpu.SemaphoreType.DMA((n_layers * N_GATHERED,))]
    shards = [win_shards, wout_shards]

    def body(win_sh, wout_sh, *rest):
        send_sems, recv_sems, local_sems, win_thru, wout_thru, *landing = rest[len(after):]
        for layer in range(n_layers):
            own, remote = _first_copies(layer, (win_sh, wout_sh), landing[N_GATHERED * layer:N_GATHERED * (layer + 1)],
                                        send_sems, recv_sems, local_sems, relayed)
            for cp in own + remote:
                cp.start()

    outs = pl.pallas_call(
        body, name=f"all_gather_weights_start_{first_layer}", in_specs=[HBM] * (2 + len(after)),
        out_specs=[_SEM] * 3 + [HBM] * (2 + n_layers * N_GATHERED),
        out_shape=(sem_shapes + [pltpu.HBM(s.shape, s.dtype) for s in shards]
                   + [pltpu.HBM(s, BF16) for _ in range(n_layers) for s in _GATHERED_SHAPES]),
        input_output_aliases={0: 3, 1: 4}, compiler_params=_params(has_side_effects=_DATAFLOW),
    )(*shards, *after)
    landing = outs[5:]
    return outs[:3], outs[3:5], [landing[N_GATHERED * l:N_GATHERED * (l + 1)] for l in range(n_layers)]


def _passed_on_copies(full_refs, send_sems, recv_sems, core_of_block, chips):
    x, y, c = _mesh_position()
    return [_gather_copy(_BLOCK_OF[a], full_refs[a], _block_id(*_other_chips(x, y)[j], core_of_block),
                         send_sems.at[a * N_OTHER_CHIPS + j], recv_sems.at[a * N_OTHER_CHIPS + j], (x, y, 1 - c))
            for a in range(N_GATHERED) for j in chips]


def _relayed_copies(full_refs, send_sems, recv_sems):
    x, y, c = _mesh_position()
    source, to = (x ^ (1 - c), y ^ c), (x ^ c, y ^ (1 - c))
    return [_gather_copy(_BLOCK_OF[a], full_refs[a], _block_id(*source, c), send_sems.at[a], recv_sems.at[a], (*to, c))
            for a in range(N_GATHERED)]


def _gather_weights_pass_on(name, chips, arrival_sems, arrival_sem_of, landing, after, relay=False):
    n = N_GATHERED * N_OTHER_CHIPS
    sem_shapes = [pltpu.SemaphoreType.DMA((n,))] * 2 + [pltpu.SemaphoreType.DMA((N_GATHERED,))] * (2 if relay else 0)

    def body(win_ref, wout_ref, arrivals, *rest):
        sems = rest[len(after):len(after) + len(sem_shapes)]
        x, y, c = _mesh_position()
        full_refs = (win_ref, wout_ref)
        passed = _passed_on_copies(full_refs, sems[0], sems[1], c, chips)
        relayed = _relayed_copies(full_refs, sems[2], sems[3]) if relay else []
        for a in range(N_GATHERED):
            for j in chips:
                sem = arrivals.at[arrival_sem_of(a, j)]
                _gather_copy(_BLOCK_OF[a], full_refs[a], _block_id(*_other_chips(x, y)[j], c), sem, sem,
                             (x, y, c)).wait_recv()
            for cp in relayed[a:a + 1] + passed[a * len(chips):(a + 1) * len(chips)]:
                cp.start()

    outs = pl.pallas_call(
        body, name=name, in_specs=[HBM] * N_GATHERED + [_SEM] + [HBM] * len(after),
        out_specs=[_SEM] * len(sem_shapes) + [HBM] * N_GATHERED,
        out_shape=sem_shapes + [pltpu.HBM(a.shape, a.dtype) for a in landing],
        input_output_aliases={a: len(sem_shapes) + a for a in range(N_GATHERED)},
        compiler_params=_params(has_side_effects=_DATAFLOW),
    )(*landing, arrival_sems, *after)
    return outs[:len(sem_shapes)], outs[len(sem_shapes):]


def _gather_weights_finish(layer, index, first_sems, passed, shards, landing, relay_send_sems=None):
    relayed = relay_send_sems is not None
    passed_sems = [sem for (send, recv), _ in passed for sem in (send, recv)] + ([relay_send_sems] if relayed else [])

    def body(win_ref, wout_ref, first_send, first_recv, local_sems, *rest):
        sems, (win_sh, wout_sh) = rest[:len(passed_sems)], rest[len(passed_sems):len(passed_sems) + 2]
        x, y, c = _mesh_position()
        full_refs = (win_ref, wout_ref)
        own, sent = _first_copies(index, (win_sh, wout_sh), full_refs, first_send, first_recv, local_sems, relayed)
        for a in range(N_GATHERED):
            sem = _first_sem(index, a, 0)
            _gather_copy(_BLOCK_OF[a], full_refs[a], _block_id(x, y, 1 - c), first_recv.at[sem], first_recv.at[sem],
                         (x, y, c)).wait_recv()
        for p, (_, chips) in enumerate(passed):
            for cp in _passed_on_copies(full_refs, sems[2 * p], sems[2 * p + 1], 1 - c, chips):
                cp.wait_recv()
            sent += _passed_on_copies(full_refs, sems[2 * p], sems[2 * p + 1], c, chips)
        if relayed:
            sent += _relayed_copies(full_refs, sems[-1], sems[-1])
        for cp in sent:
            cp.wait_send()
        for cp in own:
            cp.wait()

    return pl.pallas_call(
        body, name=f"all_gather_weights_finish_{layer}",
        in_specs=[HBM] * N_GATHERED + [_SEM] * (3 + len(passed_sems)) + [HBM] * 2,
        out_specs=[HBM] * N_GATHERED, out_shape=[pltpu.HBM(a.shape, a.dtype) for a in landing],
        input_output_aliases={a: a for a in range(N_GATHERED)}, compiler_params=_params(has_side_effects=_DATAFLOW),
    )(*landing, *first_sems, *passed_sems, *shards)


ALL_KINDS = (0, 1, 2)


def _sibling_exchange(grads, kinds):
    n_arr = len(grads)

    def make(in_refs, out_refs, sems):
        send_sems, recv_sems = sems
        x, y, c = _mesh_position()
        copies = [pltpu.make_async_remote_copy(
            src_ref=_BLOCK_OF[kinds[a]](in_refs[a], 2 * q + (1 - c)), dst_ref=out_refs[a].at[q],
            send_sem=send_sems.at[a * N_CHIP + q], recv_sem=recv_sems.at[a * N_CHIP + q],
            device_id=(x, y, 1 - c), device_id_type=MESH)
            for a in range(n_arr) for q in range(N_CHIP)]

        def start():
            for cp in copies:
                cp.start()

        def finish():
            for cp in copies:
                cp.wait_recv()
            for cp in copies:
                cp.wait_send()

        return start, finish

    return _Exchange(
        grads, [jax.ShapeDtypeStruct((N_CHIP,) + _BLOCK_SHAPES[k], BF16) for k in kinds], {},
        [pltpu.SemaphoreType.DMA((n_arr * N_CHIP,)), pltpu.SemaphoreType.DMA((n_arr * N_CHIP,))], make)


def _chips_exchange(layer, partials, received, kinds):
    n_arr = len(partials)

    def make(in_refs, out_refs, sems):
        send_sems, recv_sems = sems
        x, y, c = _mesh_position()
        copies = [pltpu.make_async_remote_copy(
            src_ref=in_refs[a].at[2 * qx + qy, layer], dst_ref=out_refs[a].at[j, layer],
            send_sem=send_sems.at[a * N_OTHER_CHIPS + j], recv_sem=recv_sems.at[a * N_OTHER_CHIPS + j],
            device_id=(qx, qy, c), device_id_type=MESH)
            for a in range(n_arr) for j, (qx, qy) in enumerate(_other_chips(x, y))]

        def start():
            for cp in copies:
                cp.start()

        def finish():
            for cp in copies:
                cp.wait_recv()
            for cp in copies:
                cp.wait_send()

        return start, finish

    inputs = list(partials)
    aliases = {}
    if received is not None:
        inputs += list(received)
        aliases = {n_arr + a: a for a in range(n_arr)}
    return _Exchange(
        inputs, [jax.ShapeDtypeStruct((N_OTHER_CHIPS, DEPTH) + _BLOCK_SHAPES[k], BF16) for k in kinds], aliases,
        [pltpu.SemaphoreType.DMA((n_arr * N_OTHER_CHIPS,)), pltpu.SemaphoreType.DMA((n_arr * N_OTHER_CHIPS,))], make)


def _all_gather_exchange(v, axis=0):
    def make(in_refs, out_refs, sems):
        send_sems, recv_sems, local_sem = sems
        x, y, c = _mesh_position()
        me = _block_id(x, y, c)
        block = lambda blk: out_refs[0].at[(slice(None),) * axis + (blk,)]
        own = pltpu.make_async_copy(in_refs[0], block(me), local_sem.at[0])
        sends, arrivals = [], []
        for k in range(1, N_DEV):
            px, py, pc = x ^ ((k >> 2) & 1), y ^ ((k >> 1) & 1), c ^ (k & 1)
            sends.append(pltpu.make_async_remote_copy(
                src_ref=in_refs[0], dst_ref=block(me), send_sem=send_sems.at[k - 1],
                recv_sem=recv_sems.at[k - 1], device_id=(px, py, pc), device_id_type=MESH))
            arrivals.append(pltpu.make_async_remote_copy(
                src_ref=in_refs[0], dst_ref=block(_block_id(px, py, pc)), send_sem=send_sems.at[k - 1],
                recv_sem=recv_sems.at[k - 1], device_id=(x, y, c), device_id_type=MESH))

        def start():
            for cp in [own] + sends:
                cp.start()

        def finish():
            for cp in arrivals:
                cp.wait_recv()
            for cp in sends:
                cp.wait_send()
            own.wait()

        return start, finish

    return _Exchange(
        [v], [jax.ShapeDtypeStruct(v.shape[:axis] + (N_DEV,) + v.shape[axis:], v.dtype)], {},
        [pltpu.SemaphoreType.DMA((N_DEV - 1,)), pltpu.SemaphoreType.DMA((N_DEV - 1,)),
         pltpu.SemaphoreType.DMA((1,))], make)


def _host(exchange, args, in_specs, out_shape, out_specs, scratch):
    n_own = (len(args), len(out_shape), len(scratch))
    if exchange is None:
        return {}, lambda refs: (refs, None)
    n_ex = (len(exchange.inputs), len(exchange.out_shapes), len(exchange.sem_shapes))
    aliases = {n_own[0] + i: n_own[1] + o for i, o in exchange.aliases.items()}
    args += exchange.inputs
    in_specs += [HBM] * n_ex[0]
    out_shape += exchange.out_shapes
    out_specs += [HBM] * n_ex[1]
    scratch += exchange.sem_shapes

    def split(refs):
        own, theirs, at = [], [], 0
        for mine, ex in zip(n_own, n_ex):
            own += refs[at:at + mine]
            theirs.append(refs[at + mine:at + mine + ex])
            at += mine + ex
        return own, exchange.make(*theirs)

    return aliases, split


def _all_gather_small(v, name, after=()):
    vmem = pl.BlockSpec(memory_space=pltpu.VMEM)

    def body(v_ref, *rest):
        out_ref, send_sems, recv_sems = rest[len(after):]
        x, y, c = _mesh_position()
        me = _block_id(x, y, c)
        out_ref[me] = v_ref[...]
        sends = []
        for k in range(1, N_DEV):
            px, py, pc = x ^ ((k >> 2) & 1), y ^ ((k >> 1) & 1), c ^ (k & 1)
            send = pltpu.make_async_remote_copy(
                src_ref=v_ref, dst_ref=out_ref.at[me], send_sem=send_sems.at[k - 1], recv_sem=recv_sems.at[k - 1],
                device_id=(px, py, pc), device_id_type=MESH)
            send.start()
            sends.append((send, _block_id(px, py, pc)))
        for k, (send, peer) in enumerate(sends):
            pltpu.make_async_remote_copy(
                src_ref=v_ref, dst_ref=out_ref.at[peer], send_sem=send_sems.at[k], recv_sem=recv_sems.at[k],
                device_id=(x, y, c), device_id_type=MESH).wait_recv()
        for send, _ in sends:
            send.wait_send()

    return pl.pallas_call(
        body, name=name, in_specs=[vmem] + [HBM] * len(after), out_specs=vmem,
        out_shape=jax.ShapeDtypeStruct((N_DEV,) + v.shape, v.dtype),
        scratch_shapes=[pltpu.SemaphoreType.DMA((N_DEV - 1,)), pltpu.SemaphoreType.DMA((N_DEV - 1,))],
        compiler_params=_params(),
    )(v, *after)


def _forward_layer(layer, x, vec, cps, wpool, win, wout, target=None):
    t_len = x.shape[0]
    n_tiles = t_len // ROW_TILE
    row = lambda cols: pl.BlockSpec((ROW_TILE, cols), lambda i: (i, 0))
    widths = (D_MODEL, 2 * CONV_W, 3 * CONV_W, 4 * POOL_W)
    head = target is not None

    def body(x_ref, vec_ref, cps_ref, wpool_ref, win_ref, wout_ref, *rest):
        target_ref = rest[0] if head else None
        xo_ref, y_ref, h_ref, ycat_ref, uc_ref, fa_ref, fp_ref = rest[head:head + 7]
        loss_ref = rest[head + 7] if head else None
        zc_ref, pc_ref = rest[-2:]
        i = pl.program_id(0)

        @pl.when(i == 0)
        def _():
            zc_ref[...] = jnp.zeros_like(zc_ref)
            pc_ref[...] = jnp.zeros_like(pc_ref)
            if head:
                loss_ref[...] = jnp.zeros_like(loss_ref)

        x_t = x_ref[...]
        shift, scale, gate = vec_ref[0:1, :], vec_ref[1:2, :], vec_ref[2:3, :]
        g_pre, g_post = vec_ref[3:4, :], vec_ref[4:5, :]
        w0, w1, w2, ps = cps_ref[0:1, :], cps_ref[1:2, :], cps_ref[2:3, :], cps_ref[3:4, :]
        rx = lax.rsqrt(jnp.mean(x_t * x_t, axis=-1, keepdims=True) + NORM_EPS)
        h = (x_t * rx) * g_pre * (1.0 + scale) + shift
        h_ref[...] = h.astype(BF16)
        proj = _dot(h.astype(BF16), win_ref[...])
        u_a, b_a, c_a, g_a, u_p, g_p = _split_proj(proj)
        uc_ref[...] = jnp.concatenate([u_a, c_a], axis=1).astype(BF16)

        z = c_a * u_a
        zcat = jnp.concatenate([zc_ref[...], z], axis=0)
        zc_ref[...] = z[ROW_TILE - CONV_HALO:]
        conv = (w0 * _rows_from_before(zcat, 2)[CONV_HALO:] + w1 * _rows_from_before(zcat, 1)[CONV_HALO:] + w2 * z)
        sig_a = _sigmoid(g_a)
        silu_a = g_a * sig_a
        b_conv = b_a * conv
        y_a = b_conv * silu_a
        fa_ref[...] = jnp.concatenate(
            [silu_a * conv, silu_a * b_a, b_conv * (sig_a + silu_a * (1.0 - sig_a))], axis=1).astype(BF16)

        pcat = jnp.concatenate([pc_ref[...], u_p], axis=0)
        pc_ref[...] = u_p[ROW_TILE - POOL_HALO:]
        counts = _window_counts(i * ROW_TILE, ROW_TILE)
        pooled, mixed = [], []
        for g, w in enumerate(POOL_WINDOWS):
            cols = slice(g * GROUP_D, (g + 1) * GROUP_D)
            s = pcat[:, cols]
            step = 1
            while step < w:
                s = s + _rows_from_before(s, step)
                step *= 2
            pooled_g = (s[POOL_HALO:] * (1.0 / jnp.minimum(counts, float(w))) - u_p[:, cols]).astype(BF16)
            pooled.append(pooled_g)
            mixed.append(_dot(pooled_g, wpool_ref[g]))
        mixed = jnp.concatenate(mixed, axis=1)
        sig_p = _sigmoid(g_p)
        silu_p = g_p * sig_p
        mixed_ps = mixed * ps
        y_p = mixed_ps * silu_p
        fp_ref[...] = jnp.concatenate(
            [(ps * silu_p).astype(BF16), (mixed_ps * (sig_p + silu_p * (1.0 - sig_p))).astype(BF16),
             (silu_p * mixed).astype(BF16)] + pooled, axis=1)

        ycat = jnp.concatenate([y_a, y_p], axis=1)
        ycat_ref[...] = ycat.astype(BF16)
        y_b = _dot(ycat.astype(BF16), wout_ref[...]).astype(BF16)
        y_ref[...] = y_b
        y_t = y_b.astype(F32)
        ry = lax.rsqrt(jnp.mean(y_t * y_t, axis=-1, keepdims=True) + NORM_EPS)
        x_next = x_t + gate * (y_t * ry * g_post)
        if head:
            err = x_next - target_ref[...]
            xo_ref[...] = err * (1.0 / D_MODEL)
            loss_ref[...] += jnp.sum(err * err) * (0.5 / D_MODEL)
        else:
            xo_ref[...] = x_next

    tile = (SUBLANES, LANES)
    return pl.pallas_call(
        body, name=f"forward_layer_{layer}", grid=(n_tiles,),
        in_specs=[row(D_MODEL), _layer_spec(vec.shape, layer), _layer_spec(cps.shape, layer),
                  _layer_spec(wpool.shape, layer), _whole_spec(win.shape), _whole_spec(wout.shape)]
        + [row(D_MODEL)] * head,
        out_specs=[row(D_MODEL), row(D_MODEL), row(D_MODEL), row(D_MODEL)] + [row(w) for w in widths[1:]]
        + [_whole_spec(tile)] * head,
        out_shape=[jax.ShapeDtypeStruct((t_len, D_MODEL), F32), jax.ShapeDtypeStruct((t_len, D_MODEL), BF16),
                   jax.ShapeDtypeStruct((t_len, D_MODEL), BF16), jax.ShapeDtypeStruct((t_len, D_MODEL), BF16)]
        + [jax.ShapeDtypeStruct((t_len, w), BF16) for w in widths[1:]] + [jax.ShapeDtypeStruct(tile, F32)] * head,
        scratch_shapes=[pltpu.VMEM((CONV_HALO, CONV_W), F32), pltpu.VMEM((POOL_HALO, POOL_W), F32)],
        compiler_params=_params(dimension_semantics=("arbitrary",)),
    )(x, vec, cps, wpool, win, wout, *([target] * head))


def _backward_layer(layer, dxo, y, x, uc, fa, fp, vec, cps, wpool, wout, win, after):
    t_len = dxo.shape[0]
    n_tiles = t_len // BWD_TILE
    halo_per_tile = BWD_TILE // POOL_HALO
    rev = lambda cols: pl.BlockSpec((BWD_TILE, cols), lambda i: (n_tiles - 1 - i, 0))
    halo_spec = pl.BlockSpec(
        (POOL_HALO, 2 * CONV_W), lambda i: (jnp.maximum((n_tiles - 1 - i) * halo_per_tile - 1, 0), 0))
    gwpool_shape = (len(POOL_WINDOWS), GROUP_D, GROUP_D)

    def body(dxo_ref, y_ref, x_ref, uc_ref, uch_ref, fa_ref, fp_ref, vec_ref, cps_ref, wpool_ref, wout_ref, win_ref,
             *rest):
        dx_ref, dproj_ref, dy_ref, gwpool_ref, dcps_ref, dvec_ref, gwpool_acc, dcc_ref, qc_ref = rest[len(after):]
        i = pl.program_id(0)
        tile = n_tiles - 1 - i

        @pl.when(i == 0)
        def _():
            gwpool_acc[...] = jnp.zeros_like(gwpool_acc)
            dcps_ref[...] = jnp.zeros_like(dcps_ref)
            dvec_ref[...] = jnp.zeros_like(dvec_ref)
            dcc_ref[...] = jnp.zeros_like(dcc_ref)
            qc_ref[...] = jnp.zeros_like(qc_ref)

        shift, scale, gate = vec_ref[0:1, :], vec_ref[1:2, :], vec_ref[2:3, :]
        g_pre, g_post = vec_ref[3:4, :], vec_ref[4:5, :]
        w0, w1, w2 = cps_ref[0:1, :], cps_ref[1:2, :], cps_ref[2:3, :]

        dxo_t = dxo_ref[...]
        y_t = y_ref[...].astype(F32)
        ry = lax.rsqrt(jnp.mean(y_t * y_t, axis=-1, keepdims=True) + NORM_EPS)
        yh = y_t * ry
        dvec_ref[2:3, :] += jnp.sum(dxo_t * yh, axis=0, keepdims=True)
        dyh = dxo_t * (gate * g_post)
        dy_b = (ry * (dyh - yh * jnp.mean(dyh * yh, axis=-1, keepdims=True))).astype(BF16)
        dy_ref[...] = dy_b
        dycat = _dot_nt(dy_b, wout_ref[...])
        dy_a, dy_p = dycat[:, :CONV_W], dycat[:, CONV_W:]

        fa_t = fa_ref[...].astype(F32)
        db_a = dy_a * fa_t[:, :CONV_W]
        dconv = dy_a * fa_t[:, CONV_W:2 * CONV_W]
        dg_a = dy_a * fa_t[:, 2 * CONV_W:]
        uc_t = uc_ref[...].astype(F32)
        u_a, c_a = uc_t[:, :CONV_W], uc_t[:, CONV_W:]
        halo = jnp.where(tile > 0, uch_ref[...].astype(F32), 0.0)[POOL_HALO - CONV_HALO:]
        z = c_a * u_a
        zcat = jnp.concatenate([halo[:, CONV_W:] * halo[:, :CONV_W], z], axis=0)
        z1 = _rows_from_before(zcat, 1)[CONV_HALO:]
        z2 = _rows_from_before(zcat, 2)[CONV_HALO:]
        dccat = jnp.concatenate([dconv, dcc_ref[...]], axis=0)
        dc1 = _rows_from_after(dccat, 1)[:BWD_TILE]
        dc2 = _rows_from_after(dccat, 2)[:BWD_TILE]
        dz = w2 * dconv + w1 * dc1 + w0 * dc2
        dcc_ref[...] = dconv[:CONV_HALO]
        dcps_ref[0:1, :] += jnp.sum(dconv * z2, axis=0, keepdims=True)
        dcps_ref[1:2, :] += jnp.sum(dconv * z1, axis=0, keepdims=True)
        dcps_ref[2:3, :] += jnp.sum(dconv * z, axis=0, keepdims=True)
        du_a = dz * c_a
        dc_a = dz * u_a

        dmixed = (dy_p * fp_ref[:, :POOL_W].astype(F32)).astype(BF16)
        dg_p = dy_p * fp_ref[:, POOL_W:2 * POOL_W].astype(F32)
        dcps_ref[3:4, :] += jnp.sum(dy_p * fp_ref[:, 2 * POOL_W:3 * POOL_W].astype(F32), axis=0, keepdims=True)
        counts = _window_counts(tile * BWD_TILE, BWD_TILE)
        du_p, q_head = [], []
        for g, w in enumerate(POOL_WINDOWS):
            cols = slice(g * GROUP_D, (g + 1) * GROUP_D)
            dm_g = dmixed[:, cols]
            dpooled_g = _dot_nt(dm_g, wpool_ref[g])
            gwpool_acc[g] += _dot_tn(fp_ref[:, 3 * POOL_W + g * GROUP_D:3 * POOL_W + (g + 1) * GROUP_D], dm_g)
            q_g = dpooled_g * (1.0 / jnp.minimum(counts, float(w)))
            q_head.append(q_g[:POOL_HALO])
            s = jnp.concatenate([q_g, qc_ref[:, cols]], axis=0)
            step = 1
            while step < w:
                s = s + _rows_from_after(s, step)
                step *= 2
            du_p.append(s[:BWD_TILE] - dpooled_g)
        qc_ref[...] = jnp.concatenate(q_head, axis=1)
        dproj_b = jnp.concatenate([du_a, db_a, dc_a, dg_a] + du_p + [dg_p], axis=1).astype(BF16)
        dproj_ref[...] = dproj_b

        x_t = x_ref[...]
        rx = lax.rsqrt(jnp.mean(x_t * x_t, axis=-1, keepdims=True) + NORM_EPS)
        xn = x_t * rx
        mod_scale = 1.0 + scale
        dh = _dot_nt(dproj_b, win_ref[...])
        dvec_ref[0:1, :] += jnp.sum(dh, axis=0, keepdims=True)
        dvec_ref[1:2, :] += jnp.sum(dh * xn, axis=0, keepdims=True)
        dxn = dh * (g_pre * mod_scale)
        dx_ref[...] = dxo_t + rx * (dxn - xn * jnp.mean(dxn * xn, axis=-1, keepdims=True))

        @pl.when(i == n_tiles - 1)
        def _():
            gwpool_ref[...] = gwpool_acc[...].astype(BF16)
            sum_dh_xn, sum_dxo_yh = dvec_ref[1:2, :], dvec_ref[2:3, :]
            dvec_ref[1:2, :] = sum_dh_xn * g_pre
            dvec_ref[3:4, :] = sum_dh_xn * mod_scale
            dvec_ref[2:3, :] = sum_dxo_yh * g_post
            dvec_ref[4:5, :] = sum_dxo_yh * gate

    return pl.pallas_call(
        body, name=f"backward_layer_{layer}", grid=(n_tiles,),
        in_specs=[rev(D_MODEL), rev(D_MODEL), rev(D_MODEL), rev(2 * CONV_W), halo_spec, rev(3 * CONV_W),
                  rev(4 * POOL_W), _layer_spec(vec.shape, layer), _layer_spec(cps.shape, layer),
                  _layer_spec(wpool.shape, layer), _whole_spec(wout.shape), _whole_spec(win.shape)]
        + [HBM] * len(after),
        out_specs=[rev(D_MODEL), rev(IN_COLS), rev(D_MODEL), _whole_spec(gwpool_shape),
                   _whole_spec((SUBLANES, CONV_W)), _whole_spec((SUBLANES, D_MODEL))],
        out_shape=[jax.ShapeDtypeStruct((t_len, D_MODEL), F32), jax.ShapeDtypeStruct((t_len, IN_COLS), BF16),
                   jax.ShapeDtypeStruct((t_len, D_MODEL), BF16), jax.ShapeDtypeStruct(gwpool_shape, BF16),
                   jax.ShapeDtypeStruct((SUBLANES, CONV_W), F32), jax.ShapeDtypeStruct((SUBLANES, D_MODEL), F32)],
        scratch_shapes=[pltpu.VMEM(gwpool_shape, F32), pltpu.VMEM((CONV_HALO, CONV_W), F32),
                        pltpu.VMEM((POOL_HALO, POOL_W), F32)],
        compiler_params=_params(dimension_semantics=("arbitrary",)),
    )(dxo, y, x, uc, uc, fa, fp, vec, cps, wpool, wout, win, *after)


def _weight_grads(layer, h, dproj, ycat, dy, exchange, after=()):
    t_len = dy.shape[0]
    n_in, n_out = IN_COLS // GWIN_COLS, D_MODEL // GWOUT_COLS
    args = [h, dproj, ycat, dy, *after]
    in_specs = [_whole_spec(h.shape),
                pl.BlockSpec((t_len, GWIN_COLS), lambda s: (0, jnp.minimum(s, n_in - 1))),
                _whole_spec(ycat.shape),
                pl.BlockSpec((t_len, GWOUT_COLS), lambda s: (0, jnp.maximum(s - n_in, 0)))] + [HBM] * len(after)
    out_shape = [jax.ShapeDtypeStruct((D_MODEL, IN_COLS), BF16), jax.ShapeDtypeStruct((D_MODEL, D_MODEL), BF16)]
    out_specs = [pl.BlockSpec((D_MODEL, GWIN_COLS), lambda s: (0, jnp.minimum(s, n_in - 1))),
                 pl.BlockSpec((D_MODEL, GWOUT_COLS), lambda s: (0, jnp.maximum(s - n_in, 0)))]
    scratch = []
    aliases, split = _host(exchange, args, in_specs, out_shape, out_specs, scratch)

    def body(*refs):
        (h_ref, dproj_ref, ycat_ref, dy_ref, *_, gwin_ref, gwout_ref), hosted = split(refs)
        s = pl.program_id(0)
        if hosted is not None:
            pl.when(s == 0)(hosted[0])

        @pl.when(s < n_in)
        def _():
            gwin_ref[...] = _dot_tn(h_ref[...], dproj_ref[...]).astype(BF16)

        @pl.when(s >= n_in)
        def _():
            gwout_ref[...] = _dot_tn(ycat_ref[...], dy_ref[...]).astype(BF16)

        if hosted is not None:
            pl.when(s == n_in + n_out - 1)(hosted[1])

    return pl.pallas_call(
        body, name=f"weight_grads_{layer}", grid=(n_in + n_out,), in_specs=in_specs, out_specs=out_specs,
        out_shape=out_shape, scratch_shapes=scratch, input_output_aliases=aliases,
        compiler_params=_params(dimension_semantics=("arbitrary",)),
    )(*args)


def _add_sibling_blocks(name, layer, grads, received, core, partials, kinds):
    n_arr = len(grads)

    def body(core_ref, *refs):
        mine, theirs, outs = refs[:n_arr], refs[n_arr:2 * n_arr], refs[-n_arr:]
        for a in range(n_arr):
            outs[a][...] = (mine[a][...].astype(F32) + theirs[a][...].astype(F32)).astype(BF16)

    own_of_kind = [
        pl.BlockSpec((D_MODEL, W_IN_SHARD), lambda q, core_ref: (0, 2 * q + core_ref[0])),
        pl.BlockSpec((W_OUT_SHARD, D_MODEL), lambda q, core_ref: (2 * q + core_ref[0], 0)),
        pl.BlockSpec((POOL_SHARD, GROUP_D), lambda q, core_ref: (2 * q + core_ref[0], 0)),
    ]
    shapes = [_BLOCK_SHAPES[k] for k in kinds]
    recv_specs = [pl.BlockSpec((None,) + s, lambda q, core_ref: (q, 0, 0)) for s in shapes]
    out_specs = [pl.BlockSpec((None, None) + s, lambda q, core_ref: (q, layer, 0, 0)) for s in shapes]
    args = [core, *grads, *received]
    in_specs = [own_of_kind[k] for k in kinds] + recv_specs
    aliases = {}
    if partials is not None:
        aliases = {len(args) + a: a for a in range(n_arr)}
        args += list(partials)
        in_specs += [HBM] * n_arr
    return pl.pallas_call(
        body, name=name,
        grid_spec=pltpu.PrefetchScalarGridSpec(
            num_scalar_prefetch=1, grid=(N_CHIP,), in_specs=in_specs, out_specs=out_specs),
        out_shape=[jax.ShapeDtypeStruct((N_CHIP, DEPTH) + s, BF16) for s in shapes],
        input_output_aliases=aliases,
        compiler_params=_params(dimension_semantics=("arbitrary",)),
    )(*args)


def _modulation_columns(c_all, w_ada):
    def body(c_ref, w_ref, cact_ref, out_ref):
        c_t = c_ref[...]
        c_act = c_t * _sigmoid(c_t)
        cact_ref[...] = c_act
        out_ref[...] = jnp.dot(c_act, w_ref[...], preferred_element_type=F32, precision=lax.Precision.HIGHEST)

    return pl.pallas_call(
        body, name="modulation_columns", grid=(DEPTH,),
        in_specs=[pl.BlockSpec((N_DEV, D_MODEL), lambda l: (0, 0)),
                  pl.BlockSpec((None, D_MODEL, W_IN_SHARD), lambda l: (l, 0, 0))],
        out_specs=[pl.BlockSpec((N_DEV, D_MODEL), lambda l: (0, 0)),
                   pl.BlockSpec((N_DEV, W_IN_SHARD), lambda l: (0, l))],
        out_shape=[jax.ShapeDtypeStruct((N_DEV, D_MODEL), F32),
                   jax.ShapeDtypeStruct((N_DEV, DEPTH * W_IN_SHARD), F32)],
        compiler_params=_params(dimension_semantics=("arbitrary",)),
    )(c_all, w_ada)


def _adamw(w, g, m, v):
    m_new = ADAM_B1 * m + (1.0 - ADAM_B1) * g
    v_new = ADAM_B2 * v + (1.0 - ADAM_B2) * (g * g)
    m_hat = m_new / (1.0 - ADAM_B1 ** ADAM_STEP)
    v_hat = v_new / (1.0 - ADAM_B2 ** ADAM_STEP)
    delta = -ADAM_LR * (m_hat / (jnp.sqrt(v_hat) + ADAM_EPS) + ADAM_WD * w)
    return delta, m_new, v_new


def _adamw_w_ada(w, m, v, c_act_t, dmod_cols):
    def body(w_ref, m_ref, v_ref, ct_ref, dm_ref, g_ref, d_ref, mo_ref, vo_ref):
        g = ct_ref[:, 0:1] * dm_ref[0:1, :]
        for b in range(1, N_DEV):
            g = g + ct_ref[:, b:b + 1] * dm_ref[b:b + 1, :]
        g_ref[...] = g
        d_ref[...], mo_ref[...], vo_ref[...] = _adamw(w_ref[...], g, m_ref[...], v_ref[...])

    big = pl.BlockSpec((None, D_MODEL, W_IN_SHARD), lambda l: (l, 0, 0))
    return pl.pallas_call(
        body, name="adamw_w_ada", grid=(DEPTH,),
        in_specs=[big, big, big, pl.BlockSpec((D_MODEL, N_DEV), lambda l: (0, 0)),
                  pl.BlockSpec((None, N_DEV, W_IN_SHARD), lambda l: (l, 0, 0))],
        out_specs=[big] * 4, out_shape=[jax.ShapeDtypeStruct(w.shape, F32)] * 4,
        compiler_params=_params(dimension_semantics=("arbitrary",)),
    )(w, m, v, c_act_t, dmod_cols)


def _sum_chip_partials(own_ref, recv_ref):
    g = own_ref[...].astype(F32)
    for j in range(N_OTHER_CHIPS):
        g = g + recv_ref[j].astype(F32)
    return g


def _partial_specs(row_tile, cols, first_layer=0):
    own = pl.BlockSpec((None, None, row_tile, cols), lambda l, r, chip_ref: (chip_ref[0], first_layer + l, r, 0))
    recv = pl.BlockSpec((N_OTHER_CHIPS, None, row_tile, cols), lambda l, r, chip_ref: (0, first_layer + l, r, 0))
    return own, recv


def _adamw_reduced(name, w, m, v, partial, received, chip, row_tile, layers, continued):
    depth, rows, cols = w.shape
    first, stop = layers

    def body(chip_ref, w_ref, m_ref, v_ref, own_ref, recv_ref, *rest):
        g_ref, d_ref, mo_ref, vo_ref = rest[-4:]
        g = _sum_chip_partials(own_ref, recv_ref)
        g_ref[...] = g
        d_ref[...], mo_ref[...], vo_ref[...] = _adamw(w_ref[...], g, m_ref[...], v_ref[...])

    blk = pl.BlockSpec((None, row_tile, cols), lambda l, r, chip_ref: (first + l, r, 0))
    args = [chip, w, m, v, partial, received]
    in_specs = [blk, blk, blk, *_partial_specs(row_tile, cols, first)]
    aliases = {}
    if continued is not None:
        aliases = {len(args) + k: k for k in range(4)}
        args += list(continued)
        in_specs += [HBM] * 4
    return pl.pallas_call(
        body, name=name,
        grid_spec=pltpu.PrefetchScalarGridSpec(
            num_scalar_prefetch=1, grid=(stop - first, rows // row_tile), in_specs=in_specs, out_specs=[blk] * 4),
        out_shape=[jax.ShapeDtypeStruct(w.shape, F32)] * 4, input_output_aliases=aliases,
        compiler_params=_params(dimension_semantics=("arbitrary", "arbitrary")),
    )(*args)


def _reduce_w_pool(partial, received, chip):
    def body(chip_ref, own_ref, recv_ref, g_ref):
        g_ref[...] = _sum_chip_partials(own_ref, recv_ref)

    return pl.pallas_call(
        body, name="reduce_w_pool",
        grid_spec=pltpu.PrefetchScalarGridSpec(
            num_scalar_prefetch=1, grid=(DEPTH, 1), in_specs=list(_partial_specs(POOL_SHARD, GROUP_D)),
            out_specs=pl.BlockSpec((None, POOL_SHARD, GROUP_D), lambda l, r, chip_ref: (l, 0, 0))),
        out_shape=jax.ShapeDtypeStruct((DEPTH, POOL_SHARD, GROUP_D), F32),
        compiler_params=_params(dimension_semantics=("arbitrary", "arbitrary")),
    )(chip, partial, received)


def _adamw_small(name, params):
    n = len(params)

    def body(*refs):
        ins, outs = refs[:4 * n], refs[4 * n:]
        for p in range(n):
            w_ref, g_ref, m_ref, v_ref = ins[4 * p:4 * p + 4]
            d_ref, mo_ref, vo_ref = outs[3 * p:3 * p + 3]
            d_ref[...], mo_ref[...], vo_ref[...] = _adamw(w_ref[...], g_ref[...], m_ref[...], v_ref[...])

    vmem = pl.BlockSpec(memory_space=pltpu.VMEM)
    flat = [a for group in params for a in group]
    out_shape = [jax.ShapeDtypeStruct(group[0].shape, F32) for group in params for _ in range(3)]
    outs = pl.pallas_call(
        body, name=name, in_specs=[vmem] * len(flat), out_specs=[vmem] * len(out_shape),
        out_shape=out_shape, compiler_params=_params(),
    )(*flat)
    return [tuple(outs[3 * p:3 * p + 3]) for p in range(n)]


def _sum_sources(slabs, after):
    def body(s_ref, *rest):
        acc = s_ref[0]
        for b in range(1, N_DEV):
            acc = acc + s_ref[b]
        rest[-1][...] = acc

    vmem = pl.BlockSpec(memory_space=pltpu.VMEM)
    return pl.pallas_call(
        body, name="sum_small_grads", in_specs=[vmem] + [HBM] * len(after), out_specs=vmem,
        out_shape=jax.ShapeDtypeStruct(slabs.shape[1:], F32), compiler_params=_params(),
    )(slabs, *after)


def _to_bf16(a, name, layers=None):
    first, stop = layers or (0, a.shape[0])

    def body(a_ref, o_ref):
        o_ref[...] = a_ref[...].astype(BF16)

    block = (None,) + a.shape[1:]
    return pl.pallas_call(
        body, name=name, grid=(stop - first,), in_specs=[pl.BlockSpec(block, lambda l: (first + l, 0, 0))],
        out_specs=pl.BlockSpec(block, lambda l: (l, 0, 0)),
        out_shape=jax.ShapeDtypeStruct((stop - first,) + a.shape[1:], BF16),
        compiler_params=_params(dimension_semantics=("arbitrary",)),
    )(a)


def kernel(x, c, w_ada, b_ada, g_pre, w_in, w_conv, w_pool, pool_scale, w_out, g_post, loss_target, m_w_ada, m_b_ada, m_g_pre, m_w_in, m_w_conv, m_w_pool, m_pool_scale, m_w_out, m_g_post, v_w_ada, v_b_ada, v_g_pre, v_w_in, v_w_conv, v_w_pool, v_pool_scale, v_w_out, v_g_post):
    mx, my, mc = _mesh_position()
    me = _block_id(mx, my, mc)
    chip = (2 * mx + my).astype(jnp.int32).reshape(1)
    core = mc.astype(jnp.int32).reshape(1)
    x0 = x[0]
    target = loss_target[0]
    conv_shard = w_conv.shape[-1]

    own_small = jnp.concatenate([c, w_conv.reshape(1, DEPTH * 3 * conv_shard)], axis=1)
    first_shards = [_to_bf16(w_in, "cast_w_in_0", (0, 1)), _to_bf16(w_out, "cast_w_out_0", (0, 1))]
    all_small = _all_gather_small(own_small, "all_gather_c_w_conv", first_shards)[:, 0, :]
    gathers = [_gather_weights_start(0, *first_shards, [all_small], relayed=True)]
    c_all = all_small[:, :D_MODEL]
    w_conv_full = all_small[:, D_MODEL:].reshape(N_DEV, DEPTH, 3, conv_shard).transpose(1, 2, 0, 3).reshape(
        DEPTH, 3, CONV_W)
    cps = jnp.concatenate([w_conv_full, pool_scale[:, None], jnp.zeros((DEPTH, 4, CONV_W), F32)], axis=1)

    c_act, pieces = _modulation_columns(c_all, w_ada)
    upper = (1, DEPTH)
    upper_shards = [_to_bf16(w_in, "cast_w_in_1", upper), _to_bf16(w_out, "cast_w_out_1", upper)]
    wpool_b = _to_bf16(w_pool.reshape(DEPTH, POOL_ROWS, GROUP_D), "cast_w_pool").reshape(w_pool.shape)
    first_sems_0, _, (zones_0,) = gathers[0]
    neighbour_sems, zones_0 = _gather_weights_pass_on(
        "all_gather_weights_relay_0", NEIGHBOUR_CHIPS, first_sems_0[1], partial(_first_arrival, 0), zones_0,
        [pieces, *upper_shards, wpool_b], relay=True)
    mod_all = _all_gather_small(pieces, "all_gather_modulation", [zones_0[0]])
    mod_mine = lax.dynamic_index_in_dim(mod_all, me, axis=1, keepdims=False)
    mod = mod_mine.reshape(N_DEV, DEPTH, W_IN_SHARD).transpose(1, 0, 2).reshape(DEPTH, 3 * D_MODEL) + b_ada
    zeros_d = jnp.zeros((DEPTH, 3, D_MODEL), F32)
    vec = jnp.concatenate([mod.reshape(DEPTH, 3, D_MODEL), g_pre[:, None], g_post[:, None], zeros_d], axis=1)

    gathers.append(_gather_weights_start(1, *upper_shards, [mod_all]))
    gathers = [(first_sems, shards, landing[k], k) for first_sems, shards, landing in gathers
               for k in range(len(landing))]

    xs, kept, wins, wouts = [x0], [], [], []
    for l in range(DEPTH):
        first_sems, shards, zones, index = gathers[l]
        if l == 0:
            diagonal_sems, zones = _gather_weights_pass_on(
                "all_gather_weights_pass_on_0", DIAGONAL_CHIP, neighbour_sems[3], lambda a, j: a, zones_0,
                [vec, cps, gathers[-1][1][0]])
            passed = [(neighbour_sems[:2], NEIGHBOUR_CHIPS), (diagonal_sems, DIAGONAL_CHIP)]
            win, wout = _gather_weights_finish(l, index, first_sems, passed, shards, zones, neighbour_sems[2])
        else:
            passed_sems, zones = _gather_weights_pass_on(
                f"all_gather_weights_pass_on_{l}", ALL_OTHER_CHIPS, first_sems[1], partial(_first_arrival, index),
                zones, [xs[-1]])
            win, wout = _gather_weights_finish(l, index, first_sems, [(passed_sems, ALL_OTHER_CHIPS)], shards, zones)
        x_next, *for_backward = _forward_layer(
            l, xs[-1], vec, cps, wpool_b, win, wout, target if l == DEPTH - 1 else None)
        xs.append(x_next)
        kept.append(for_backward[:6])
        wins.append(win)
        wouts.append(wout)
    dx, loss_tile = xs[DEPTH], for_backward[6]

    slab_rows = [None] * DEPTH
    partials = received = None
    in_flight = []

    def scatter(layer, grads, from_sibling, after):
        nonlocal partials, received
        partials = _add_sibling_blocks(
            f"grad_add_sibling_{layer}", layer, grads, from_sibling, core, partials, ALL_KINDS)
        chips = _chips_exchange(layer, partials, received, ALL_KINDS)
        sems, partials, received, token = _start_exchange(chips, f"grad_chips_start_{layer}", after)
        in_flight.append((chips, sems, layer))
        return token

    grads_above = None
    issued = []
    for l in reversed(range(DEPTH)):
        y, h, ycat, uc, fa, fp = kept[l]
        dx, dproj, dy, gwpool, dcps, dvec = _backward_layer(
            l, dx, y, xs[l], uc, fa, fp, vec, cps, wpool_b, wouts[l], wins[l], issued)
        slab_rows[l] = jnp.concatenate(
            [dvec[0], dvec[1], dvec[2], dvec[3], dvec[4], dcps[3], dcps[0], dcps[1], dcps[2],
             loss_tile[0] if l == 0 else jnp.zeros((LANES,), F32)])
        after = []
        if l == 0:
            gather_small = _all_gather_exchange(jnp.stack(slab_rows))
            sems_s, slab, slabs, token = _start_exchange(gather_small, "all_gather_small_grads_start")
            after = [token]
        hosted = _sibling_exchange(grads_above, ALL_KINDS) if grads_above is not None else None
        gwin, gwout, *from_sibling = _weight_grads(l, h, dproj, ycat, dy, hosted, after)
        if l == 0:
            _, (slabs,) = _finish_exchange(
                gather_small, "all_gather_small_grads_finish", sems_s, slab, slabs, [gwin])
        issued = [gwin]
        if grads_above is not None:
            issued = [scatter(l + 1, grads_above, from_sibling, [])]
        grads_above = [gwin, gwout, gwpool.reshape(POOL_ROWS, GROUP_D)]
        if l <= 1:
            from_sibling = _run_exchange(_sibling_exchange(grads_above, ALL_KINDS), f"grad_exchange_sibling_{l}")
            token = scatter(l, grads_above, from_sibling, [slabs] if l == 0 else [])
            issued = [token]
            grads_above = None
    grad_x = dx[None]
    chips_0, sems_0, _ = in_flight.pop()

    o = 3 * D_MODEL

    after = [token]
    for chips, sems, l in in_flight:
        partials, received = _finish_exchange(chips, f"grad_chips_finish_{l}", sems, partials, received, after)
        after = []
    upper = (1, DEPTH)
    w_in_upper = _adamw_reduced(
        "adamw_w_in_upper", w_in, m_w_in, v_w_in, partials[0], received[0], chip, ROW_TILE, upper, None)
    w_out_upper = _adamw_reduced(
        "adamw_w_out_upper", w_out, m_w_out, v_w_out, partials[1], received[1], chip, W_OUT_SHARD, upper, None)
    dmod_all = slabs[:, :, :o].reshape(N_DEV, DEPTH, N_DEV, W_IN_SHARD)
    dmod_cols = lax.dynamic_index_in_dim(dmod_all, me, axis=2, keepdims=False).transpose(1, 0, 2) + token[0, 0]
    g_w_ada, d_w_ada, nm_w_ada, nv_w_ada = _adamw_w_ada(w_ada, m_w_ada, v_w_ada, c_act.T, dmod_cols)

    partials, received = _finish_exchange(
        chips_0, "grad_chips_finish_0", sems_0, partials, received, [nv_w_ada, w_in_upper[3], w_out_upper[3]])
    gather_pool = _all_gather_exchange(_reduce_w_pool(partials[2], received[2], chip), axis=1)
    sems_p, pool_rows, pool_landing, token_p = _start_exchange(gather_pool, "all_gather_grad_w_pool_start")
    g_w_in, d_w_in, nm_w_in, nv_w_in = _adamw_reduced(
        "adamw_w_in_0", w_in, m_w_in, v_w_in, partials[0], received[0], chip, ROW_TILE, (0, 1), w_in_upper)
    g_w_out, d_w_out, nm_w_out, nv_w_out = _adamw_reduced(
        "adamw_w_out_0", w_out, m_w_out, v_w_out, partials[1], received[1], chip, W_OUT_SHARD, (0, 1), w_out_upper)
    total = _sum_sources(slabs, [nv_w_in, nv_w_out])
    loss = total[0, SLAB_COLS]
    g_b_ada = total[:, :o]
    g_g_pre = total[:, o:o + D_MODEL]
    g_g_post = total[:, o + D_MODEL:o + 2 * D_MODEL]
    g_pool_scale = total[:, o + 2 * D_MODEL:o + 2 * D_MODEL + POOL_W]
    g_conv_full = total[:, o + 2 * D_MODEL + POOL_W:SLAB_COLS].reshape(DEPTH, 3, CONV_W)
    g_w_conv = lax.dynamic_slice_in_dim(g_conv_full, me * conv_shard, conv_shard, axis=2)
    small = _adamw_small("adamw_small", [
        (b_ada, g_b_ada, m_b_ada, v_b_ada),
        (g_pre, g_g_pre, m_g_pre, v_g_pre),
        (w_conv, g_w_conv, m_w_conv, v_w_conv),
        (pool_scale, g_pool_scale, m_pool_scale, v_pool_scale),
        (g_post, g_g_post, m_g_post, v_g_post),
    ])
    (d_b_ada, nm_b_ada, nv_b_ada), (d_g_pre, nm_g_pre, nv_g_pre), (d_w_conv, nm_w_conv, nv_w_conv), \
        (d_ps, nm_ps, nv_ps), (d_g_post, nm_g_post, nv_g_post) = small
    _, (g_pool_all,) = _finish_exchange(
        gather_pool, "all_gather_grad_w_pool_finish", sems_p, pool_rows, pool_landing, [nv_w_in, nv_w_out, nv_g_post])
    g_w_pool = g_pool_all.reshape(w_pool.shape)
    ((d_w_pool, nm_w_pool, nv_w_pool),) = _adamw_small("adamw_w_pool", [(w_pool, g_w_pool, m_w_pool, v_w_pool)])

    return (loss, grad_x,
            g_w_ada, g_b_ada, g_g_pre, g_w_in, g_w_conv, g_w_pool, g_pool_scale, g_w_out, g_g_post,
            d_w_ada, d_b_ada, d_g_pre, d_w_in, d_w_conv, d_w_pool, d_ps, d_w_out, d_g_post,
            nm_w_ada, nm_b_ada, nm_g_pre, nm_w_in, nm_w_conv, nm_w_pool, nm_ps, nm_w_out, nm_g_post,
            nv_w_ada, nv_b_ada, nv_g_pre, nv_w_in, nv_w_conv, nv_w_pool, nv_ps, nv_w_out, nv_g_post)
```

```python
from functools import partial

import jax
import jax.numpy as jnp
from jax import lax
from jax.experimental import pallas as pl
from jax.experimental.pallas import tpu as pltpu

F32 = jnp.float32
BF16 = jnp.bfloat16

D_MODEL = 1024
DEPTH = 4
CONV_W = 512
POOL_W = 512
POOL_WINDOWS = (2, 4, 8, 16)
GROUP_D = 128
IN_COLS = 4 * CONV_W + 2 * POOL_W
NORM_EPS = 1e-6

ADAM_LR = 0.001
ADAM_B1 = 0.9
ADAM_B2 = 0.999
ADAM_EPS = 1e-08
ADAM_WD = 0.01
ADAM_STEP = 10

N_DEV = 8
N_CHIP = 4
N_OTHER_CHIPS = N_CHIP - 1
MESH = pl.DeviceIdType.MESH
W_IN_SHARD = IN_COLS // N_DEV
W_OUT_SHARD = D_MODEL // N_DEV
POOL_ROWS = len(POOL_WINDOWS) * GROUP_D
POOL_SHARD = POOL_ROWS // N_DEV

SUBLANES = 8
LANES = 128
VMEM_LIMIT_BYTES = 56 * 1024 * 1024
ROW_TILE = 512
BWD_TILE = 256
GWIN_COLS = 768
GWOUT_COLS = 512
POOL_HALO = 16
CONV_HALO = SUBLANES

SLAB_COLS = 3 * D_MODEL + D_MODEL + D_MODEL + POOL_W + 3 * CONV_W

HBM = pl.BlockSpec(memory_space=pl.ANY)


def _params(**kw):
    return pltpu.CompilerParams(vmem_limit_bytes=VMEM_LIMIT_BYTES, **kw)


def _sigmoid(v):
    return 1.0 / (1.0 + jnp.exp(-v))


def _dot(a, b):
    return jnp.dot(a, b, preferred_element_type=F32)


def _dot_tn(a, b):
    return lax.dot_general(a, b, (((0,), (0,)), ((), ())), preferred_element_type=F32)


def _dot_nt(a, b):
    return lax.dot_general(a, b, (((1,), (1,)), ((), ())), preferred_element_type=F32)


def _rows_from_before(v, k):
    return pltpu.roll(v, k, 0)


def _rows_from_after(v, k):
    return pltpu.roll(v, v.shape[0] - k, 0)


def _window_counts(t0, rows):
    return (lax.broadcasted_iota(jnp.int32, (rows, 1), 0) + (t0 + 1)).astype(F32)


def _split_proj(p32):
    cw = CONV_W
    return (p32[:, 0 * cw:1 * cw], p32[:, 1 * cw:2 * cw], p32[:, 2 * cw:3 * cw], p32[:, 3 * cw:4 * cw],
            p32[:, 4 * cw:4 * cw + POOL_W], p32[:, 4 * cw + POOL_W:])


def _layer_spec(shape, layer):
    nd = len(shape)
    return pl.BlockSpec((None,) + tuple(shape[1:]), lambda i, _l=layer, _n=nd: (_l,) + (0,) * (_n - 1))


def _whole_spec(shape):
    return pl.BlockSpec(tuple(shape), lambda i, _n=len(shape): (0,) * _n, pipeline_mode=pl.Buffered(1))


def _mesh_position():
    return lax.axis_index("x"), lax.axis_index("y"), lax.axis_index("c")


def _block_id(x, y, c):
    return 4 * x + 2 * y + c


def _other_chips(x, y):
    return [(x ^ 1, y), (x, y ^ 1), (x ^ 1, y ^ 1)]


def _col_block(ref, blk):
    return ref.at[:, pl.ds(pl.multiple_of(blk * W_IN_SHARD, LANES), W_IN_SHARD)]


def _row_block(rows):
    def block(ref, blk):
        return ref.at[pl.ds(pl.multiple_of(blk * rows, rows), rows), :]
    return block


_BLOCK_OF = (_col_block, _row_block(W_OUT_SHARD), _row_block(POOL_SHARD))
_BLOCK_SHAPES = ((D_MODEL, W_IN_SHARD), (W_OUT_SHARD, D_MODEL), (POOL_SHARD, GROUP_D))


class _Exchange:
    def __init__(self, inputs, out_shapes, aliases, sem_shapes, make):
        self.inputs, self.out_shapes, self.aliases, self.sem_shapes, self.make = (
            list(inputs), list(out_shapes), dict(aliases), list(sem_shapes), make)


def _run_exchange(exchange, name):
    n_in, n_out = len(exchange.inputs), len(exchange.out_shapes)

    def body(*refs):
        start, finish = exchange.make(refs[:n_in], refs[n_in:n_in + n_out], refs[n_in + n_out:])
        start()
        finish()

    return pl.pallas_call(
        body, name=name, in_specs=[HBM] * n_in, out_specs=[HBM] * n_out, out_shape=exchange.out_shapes,
        scratch_shapes=exchange.sem_shapes, input_output_aliases=exchange.aliases, compiler_params=_params(),
    )(*exchange.inputs)


_SEM = pl.BlockSpec(memory_space=pltpu.SEMAPHORE)
_DATAFLOW = pltpu.SideEffectType.DATAFLOW_SIDE_EFFECTING


def _start_exchange(exchange, name, after=()):
    n_in, n_out, n_sem = len(exchange.inputs), len(exchange.out_shapes), len(exchange.sem_shapes)
    sources = [i for i in range(n_in) if i not in exchange.aliases]
    aliases = {i: n_sem + k for k, i in enumerate(sources)}
    aliases.update({i: n_sem + len(sources) + o for i, o in exchange.aliases.items()})

    def body(*refs):
        in_refs = refs[:n_in]
        outs = refs[n_in + len(after):]
        sems = outs[:n_sem]
        out_refs = outs[n_sem + len(sources):n_sem + len(sources) + n_out]
        exchange.make(in_refs, out_refs, sems)[0]()
        refs[-1][...] = jnp.zeros_like(refs[-1])

    outs = pl.pallas_call(
        body, name=name, in_specs=[HBM] * (n_in + len(after)),
        out_specs=[_SEM] * n_sem + [HBM] * (len(sources) + n_out) + [pl.BlockSpec(memory_space=pltpu.VMEM)],
        out_shape=(exchange.sem_shapes + [pltpu.HBM(exchange.inputs[i].shape, exchange.inputs[i].dtype) for i in sources]
                   + [pltpu.HBM(s.shape, s.dtype) for s in exchange.out_shapes]
                   + [jax.ShapeDtypeStruct((SUBLANES, LANES), F32)]),
        input_output_aliases=aliases, compiler_params=_params(has_side_effects=_DATAFLOW),
    )(*exchange.inputs, *after)
    return outs[:n_sem], outs[n_sem:n_sem + len(sources)], outs[n_sem + len(sources):-1], outs[-1]


def _finish_exchange(exchange, name, sems, sources, landing, after):
    n_src, n_out, n_sem = len(sources), len(landing), len(sems)
    n_in = len(exchange.inputs)
    source_at = [i for i in range(n_in) if i not in exchange.aliases]

    def body(*refs):
        src_refs, out_refs = refs[:n_src], refs[n_src:n_src + n_out]
        sem_refs = refs[n_src + n_out:n_src + n_out + n_sem]
        in_refs = [None] * n_in
        for k, i in enumerate(source_at):
            in_refs[i] = src_refs[k]
        for i, o in exchange.aliases.items():
            in_refs[i] = out_refs[o]
        exchange.make(in_refs, out_refs, sem_refs)[1]()

    arrays = list(sources) + list(landing)
    outs = pl.pallas_call(
        body, name=name, in_specs=[HBM] * len(arrays) + [_SEM] * n_sem + [HBM] * len(after),
        out_specs=[HBM] * len(arrays), out_shape=[pltpu.HBM(a.shape, a.dtype) for a in arrays],
        input_output_aliases={i: i for i in range(len(arrays))}, compiler_params=_params(has_side_effects=_DATAFLOW),
    )(*arrays, *sems, *after)
    return outs[:n_src], outs[n_src:]


N_GATHERED = 2
FIRST_COPIES = 1 + N_OTHER_CHIPS
_GATHERED_SHAPES = ((D_MODEL, IN_COLS), (D_MODEL, D_MODEL))
ALL_OTHER_CHIPS, NEIGHBOUR_CHIPS, DIAGONAL_CHIP = (0, 1, 2), (0, 1), (2,)


def _first_sem(layer, a, k):
    return (layer * N_GATHERED + a) * FIRST_COPIES + k


def _first_arrival(layer, a, j):
    return _first_sem(layer, a, 1 + j)


def _gather_copy(window_of, full_ref, blk, send_sem, recv_sem, to, src=None):
    window = window_of(full_ref, blk)
    return pltpu.make_async_remote_copy(
        src_ref=window if src is None else src, dst_ref=window, send_sem=send_sem, recv_sem=recv_sem,
        device_id=to, device_id_type=MESH)


def _first_copies(layer, shard_refs, full_refs, send_sems, recv_sems, local_sems, relayed):
    x, y, c = _mesh_position()
    me = _block_id(x, y, c)
    chips = [_other_chips(x, y)[j] for j in (NEIGHBOUR_CHIPS if relayed else ALL_OTHER_CHIPS)]
    own, remote = [], []
    for a in range(N_GATHERED):
        shard = shard_refs[a].at[layer]
        own.append(pltpu.make_async_copy(
            shard, _BLOCK_OF[a](full_refs[a], me), local_sems.at[layer * N_GATHERED + a]))
        targets = [(x, y, 1 - c)] + [(*chip, c) for chip in chips]
        remote += [_gather_copy(_BLOCK_OF[a], full_refs[a], me, send_sems.at[_first_sem(layer, a, k)],
                                recv_sems.at[_first_sem(layer, a, k)], to, src=shard)
                   for k, to in enumerate(targets)]
    return own, remote


def _gather_weights_start(first_layer, win_shards, wout_shards, after, relayed=False):
    n_layers = win_shards.shape[0]
    n_first = n_layers * N_GATHERED * FIRST_COPIES
    sem_shapes = [pltpu.SemaphoreType.DMA((n_first,)), pltpu.SemaphoreType.DMA((n_first,)),
                  pltpu.SemaphoreType.DMA((n_layers * N_GATHERED,))]
    shards = [win_shards, wout_shards]

    def body(win_sh, wout_sh, *rest):
        send_sems, recv_sems, local_sems, win_thru, wout_thru, *landing = rest[len(after):]
        for layer in range(n_layers):
            own, remote = _first_copies(layer, (win_sh, wout_sh), landing[N_GATHERED * layer:N_GATHERED * (layer + 1)],
                                        send_sems, recv_sems, local_sems, relayed)
            for cp in own + remote:
                cp.start()

    outs = pl.pallas_call(
        body, name=f"all_gather_weights_start_{first_layer}", in_specs=[HBM] * (2 + len(after)),
        out_specs=[_SEM] * 3 + [HBM] * (2 + n_layers * N_GATHERED),
        out_shape=(sem_shapes + [pltpu.HBM(s.shape, s.dtype) for s in shards]
                   + [pltpu.HBM(s, BF16) for _ in range(n_layers) for s in _GATHERED_SHAPES]),
        input_output_aliases={0: 3, 1: 4}, compiler_params=_params(has_side_effects=_DATAFLOW),
    )(*shards, *after)
    landing = outs[5:]
    return outs[:3], outs[3:5], [landing[N_GATHERED * l:N_GATHERED * (l + 1)] for l in range(n_layers)]


def _passed_on_copies(full_refs, send_sems, recv_sems, core_of_block, chips):
    x, y, c = _mesh_position()
    return [_gather_copy(_BLOCK_OF[a], full_refs[a], _block_id(*_other_chips(x, y)[j], core_of_block),
                         send_sems.at[a * N_OTHER_CHIPS + j], recv_sems.at[a * N_OTHER_CHIPS + j], (x, y, 1 - c))
            for a in range(N_GATHERED) for j in chips]


def _relayed_copies(full_refs, send_sems, recv_sems):
    x, y, c = _mesh_position()
    source, to = (x ^ (1 - c), y ^ c), (x ^ c, y ^ (1 - c))
    return [_gather_copy(_BLOCK_OF[a], full_refs[a], _block_id(*source, c), send_sems.at[a], recv_sems.at[a], (*to, c))
            for a in range(N_GATHERED)]


def _gather_weights_pass_on(name, chips, arrival_sems, arrival_sem_of, landing, after, relay=False):
    n = N_GATHERED * N_OTHER_CHIPS
    sem_shapes = [pltpu.SemaphoreType.DMA((n,))] * 2 + [pltpu.SemaphoreType.DMA((N_GATHERED,))] * (2 if relay else 0)

    def body(win_ref, wout_ref, arrivals, *rest):
        sems = rest[len(after):len(after) + len(sem_shapes)]
        x, y, c = _mesh_position()
        full_refs = (win_ref, wout_ref)
        passed = _passed_on_copies(full_refs, sems[0], sems[1], c, chips)
        relayed = _relayed_copies(full_refs, sems[2], sems[3]) if relay else []
        for a in range(N_GATHERED):
            for j in chips:
                sem = arrivals.at[arrival_sem_of(a, j)]
                _gather_copy(_BLOCK_OF[a], full_refs[a], _block_id(*_other_chips(x, y)[j], c), sem, sem,
                             (x, y, c)).wait_recv()
            for cp in relayed[a:a + 1] + passed[a * len(chips):(a + 1) * len(chips)]:
                cp.start()

    outs = pl.pallas_call(
        body, name=name, in_specs=[HBM] * N_GATHERED + [_SEM] + [HBM] * len(after),
        out_specs=[_SEM] * len(sem_shapes) + [HBM] * N_GATHERED,
        out_shape=sem_shapes + [pltpu.HBM(a.shape, a.dtype) for a in landing],
        input_output_aliases={a: len(sem_shapes) + a for a in range(N_GATHERED)},
        compiler_params=_params(has_side_effects=_DATAFLOW),
    )(*landing, arrival_sems, *after)
    return outs[:len(sem_shapes)], outs[len(sem_shapes):]


def _gather_weights_finish(layer, index, first_sems, passed, shards, landing, relay_send_sems=None):
    relayed = relay_send_sems is not None
    passed_sems = [sem for (send, recv), _ in passed for sem in (send, recv)] + ([relay_send_sems] if relayed else [])

    def body(win_ref, wout_ref, first_send, first_recv, local_sems, *rest):
        sems, (win_sh, wout_sh) = rest[:len(passed_sems)], rest[len(passed_sems):len(passed_sems) + 2]
        x, y, c = _mesh_position()
        full_refs = (win_ref, wout_ref)
        own, sent = _first_copies(index, (win_sh, wout_sh), full_refs, first_send, first_recv, local_sems, relayed)
        for a in range(N_GATHERED):
            sem = _first_sem(index, a, 0)
            _gather_copy(_BLOCK_OF[a], full_refs[a], _block_id(x, y, 1 - c), first_recv.at[sem], first_recv.at[sem],
                         (x, y, c)).wait_recv()
        for p, (_, chips) in enumerate(passed):
            for cp in _passed_on_copies(full_refs, sems[2 * p], sems[2 * p + 1], 1 - c, chips):
                cp.wait_recv()
            sent += _passed_on_copies(full_refs, sems[2 * p], sems[2 * p + 1], c, chips)
        if relayed:
            sent += _relayed_copies(full_refs, sems[-1], sems[-1])
        for cp in sent:
            cp.wait_send()
        for cp in own:
            cp.wait()

    return pl.pallas_call(
        body, name=f"all_gather_weights_finish_{layer}",
        in_specs=[HBM] * N_GATHERED + [_SEM] * (3 + len(passed_sems)) + [HBM] * 2,
        out_specs=[HBM] * N_GATHERED, out_shape=[pltpu.HBM(a.shape, a.dtype) for a in landing],
        input_output_aliases={a: a for a in range(N_GATHERED)}, compiler_params=_params(has_side_effects=_DATAFLOW),
    )(*landing, *first_sems, *passed_sems, *shards)


ALL_KINDS = (0, 1, 2)


def _sibling_exchange(grads, kinds):
    n_arr = len(grads)

    def make(in_refs, out_refs, sems):
        send_sems, recv_sems = sems
        x, y, c = _mesh_position()
        copies = [pltpu.make_async_remote_copy(
            src_ref=_BLOCK_OF[kinds[a]](in_refs[a], 2 * q + (1 - c)), dst_ref=out_refs[a].at[q],
            send_sem=send_sems.at[a * N_CHIP + q], recv_sem=recv_sems.at[a * N_CHIP + q],
            device_id=(x, y, 1 - c), device_id_type=MESH)
            for a in range(n_arr) for q in range(N_CHIP)]

        def start():
            for cp in copies:
                cp.start()

        def finish():
            for cp in copies:
                cp.wait_recv()
            for cp in copies:
                cp.wait_send()

        return start, finish

    return _Exchange(
        grads, [jax.ShapeDtypeStruct((N_CHIP,) + _BLOCK_SHAPES[k], BF16) for k in kinds], {},
        [pltpu.SemaphoreType.DMA((n_arr * N_CHIP,)), pltpu.SemaphoreType.DMA((n_arr * N_CHIP,))], make)


def _chips_exchange(layer, partials, received, kinds):
    n_arr = len(partials)

    def make(in_refs, out_refs, sems):
        send_sems, recv_sems = sems
        x, y, c = _mesh_position()
        copies = [pltpu.make_async_remote_copy(
            src_ref=in_refs[a].at[2 * qx + qy, layer], dst_ref=out_refs[a].at[j, layer],
            send_sem=send_sems.at[a * N_OTHER_CHIPS + j], recv_sem=recv_sems.at[a * N_OTHER_CHIPS + j],
            device_id=(qx, qy, c), device_id_type=MESH)
            for a in range(n_arr) for j, (qx, qy) in enumerate(_other_chips(x, y))]

        def start():
            for cp in copies:
                cp.start()

        def finish():
            for cp in copies:
                cp.wait_recv()
            for cp in copies:
                cp.wait_send()

        return start, finish

    inputs = list(partials)
    aliases = {}
    if received is not None:
        inputs += list(received)
        aliases = {n_arr + a: a for a in range(n_arr)}
    return _Exchange(
        inputs, [jax.ShapeDtypeStruct((N_OTHER_CHIPS, DEPTH) + _BLOCK_SHAPES[k], BF16) for k in kinds], aliases,
        [pltpu.SemaphoreType.DMA((n_arr * N_OTHER_CHIPS,)), pltpu.SemaphoreType.DMA((n_arr * N_OTHER_CHIPS,))], make)


def _all_gather_exchange(v, axis=0):
    def make(in_refs, out_refs, sems):
        send_sems, recv_sems, local_sem = sems
        x, y, c = _mesh_position()
        me = _block_id(x, y, c)
        block = lambda blk: out_refs[0].at[(slice(None),) * axis + (blk,)]
        own = pltpu.make_async_copy(in_refs[0], block(me), local_sem.at[0])
        sends, arrivals = [], []
        for k in range(1, N_DEV):
            px, py, pc = x ^ ((k >> 2) & 1), y ^ ((k >> 1) & 1), c ^ (k & 1)
            sends.append(pltpu.make_async_remote_copy(
                src_ref=in_refs[0], dst_ref=block(me), send_sem=send_sems.at[k - 1],
                recv_sem=recv_sems.at[k - 1], device_id=(px, py, pc), device_id_type=MESH))
            arrivals.append(pltpu.make_async_remote_copy(
                src_ref=in_refs[0], dst_ref=block(_block_id(px, py, pc)), send_sem=send_sems.at[k - 1],
                recv_sem=recv_sems.at[k - 1], device_id=(x, y, c), device_id_type=MESH))

        def start():
            for cp in [own] + sends:
                cp.start()

        def finish():
            for cp in arrivals:
                cp.wait_recv()
            for cp in sends:
                cp.wait_send()
            own.wait()

        return start, finish

    return _Exchange(
        [v], [jax.ShapeDtypeStruct(v.shape[:axis] + (N_DEV,) + v.shape[axis:], v.dtype)], {},
        [pltpu.SemaphoreType.DMA((N_DEV - 1,)), pltpu.SemaphoreType.DMA((N_DEV - 1,)),
         pltpu.SemaphoreType.DMA((1,))], make)


def _host(exchange, args, in_specs, out_shape, out_specs, scratch):
    n_own = (len(args), len(out_shape), len(scratch))
    if exchange is None:
        return {}, lambda refs: (refs, None)
    n_ex = (len(exchange.inputs), len(exchange.out_shapes), len(exchange.sem_shapes))
    aliases = {n_own[0] + i: n_own[1] + o for i, o in exchange.aliases.items()}
    args += exchange.inputs
    in_specs += [HBM] * n_ex[0]
    out_shape += exchange.out_shapes
    out_specs += [HBM] * n_ex[1]
    scratch += exchange.sem_shapes

    def split(refs):
        own, theirs, at = [], [], 0
        for mine, ex in zip(n_own, n_ex):
            own += refs[at:at + mine]
            theirs.append(refs[at + mine:at + mine + ex])
            at += mine + ex
        return own, exchange.make(*theirs)

    return aliases, split


def _all_gather_small(v, name, after=()):
    vmem = pl.BlockSpec(memory_space=pltpu.VMEM)

    def body(v_ref, *rest):
        out_ref, send_sems, recv_sems = rest[len(after):]
        x, y, c = _mesh_position()
        me = _block_id(x, y, c)
        out_ref[me] = v_ref[...]
        sends = []
        for k in range(1, N_DEV):
            px, py, pc = x ^ ((k >> 2) & 1), y ^ ((k >> 1) & 1), c ^ (k & 1)
            send = pltpu.make_async_remote_copy(
                src_ref=v_ref, dst_ref=out_ref.at[me], send_sem=send_sems.at[k - 1], recv_sem=recv_sems.at[k - 1],
                device_id=(px, py, pc), device_id_type=MESH)
            send.start()
            sends.append((send, _block_id(px, py, pc)))
        for k, (send, peer) in enumerate(sends):
            pltpu.make_async_remote_copy(
                src_ref=v_ref, dst_ref=out_ref.at[peer], send_sem=send_sems.at[k], recv_sem=recv_sems.at[k],
                device_id=(x, y, c), device_id_type=MESH).wait_recv()
        for send, _ in sends:
            send.wait_send()

    return pl.pallas_call(
        body, name=name, in_specs=[vmem] + [HBM] * len(after), out_specs=vmem,
        out_shape=jax.ShapeDtypeStruct((N_DEV,) + v.shape, v.dtype),
        scratch_shapes=[pltpu.SemaphoreType.DMA((N_DEV - 1,)), pltpu.SemaphoreType.DMA((N_DEV - 1,))],
        compiler_params=_params(),
    )(v, *after)


def _forward_layer(layer, x, vec, cps, wpool, win, wout, target=None):
    t_len = x.shape[0]
    n_tiles = t_len // ROW_TILE
    row = lambda cols: pl.BlockSpec((ROW_TILE, cols), lambda i: (i, 0))
    widths = (D_MODEL, 2 * CONV_W, 3 * CONV_W, 4 * POOL_W)
    head = target is not None

    def body(x_ref, vec_ref, cps_ref, wpool_ref, win_ref, wout_ref, *rest):
        target_ref = rest[0] if head else None
        xo_ref, y_ref, h_ref, ycat_ref, uc_ref, fa_ref, fp_ref = rest[head:head + 7]
        loss_ref = rest[head + 7] if head else None
        zc_ref, pc_ref = rest[-2:]
        i = pl.program_id(0)

        @pl.when(i == 0)
        def _():
            zc_ref[...] = jnp.zeros_like(zc_ref)
            pc_ref[...] = jnp.zeros_like(pc_ref)
            if head:
                loss_ref[...] = jnp.zeros_like(loss_ref)

        x_t = x_ref[...]
        shift, scale, gate = vec_ref[0:1, :], vec_ref[1:2, :], vec_ref[2:3, :]
        g_pre, g_post = vec_ref[3:4, :], vec_ref[4:5, :]
        w0, w1, w2, ps = cps_ref[0:1, :], cps_ref[1:2, :], cps_ref[2:3, :], cps_ref[3:4, :]
        rx = lax.rsqrt(jnp.mean(x_t * x_t, axis=-1, keepdims=True) + NORM_EPS)
        h = (x_t * rx) * g_pre * (1.0 + scale) + shift
        h_ref[...] = h.astype(BF16)
        proj = _dot(h.astype(BF16), win_ref[...])
        u_a, b_a, c_a, g_a, u_p, g_p = _split_proj(proj)
        uc_ref[...] = jnp.concatenate([u_a, c_a], axis=1).astype(BF16)

        z = c_a * u_a
        zcat = jnp.concatenate([zc_ref[...], z], axis=0)
        zc_ref[...] = z[ROW_TILE - CONV_HALO:]
        conv = (w0 * _rows_from_before(zcat, 2)[CONV_HALO:] + w1 * _rows_from_before(zcat, 1)[CONV_HALO:] + w2 * z)
        sig_a = _sigmoid(g_a)
        silu_a = g_a * sig_a
        b_conv = b_a * conv
        y_a = b_conv * silu_a
        fa_ref[...] = jnp.concatenate(
            [silu_a * conv, silu_a * b_a, b_conv * (sig_a + silu_a * (1.0 - sig_a))], axis=1).astype(BF16)

        pcat = jnp.concatenate([pc_ref[...], u_p], axis=0)
        pc_ref[...] = u_p[ROW_TILE - POOL_HALO:]
        counts = _window_counts(i * ROW_TILE, ROW_TILE)
        pooled, mixed = [], []
        for g, w in enumerate(POOL_WINDOWS):
            cols = slice(g * GROUP_D, (g + 1) * GROUP_D)
            s = pcat[:, cols]
            step = 1
            while step < w:
                s = s + _rows_from_before(s, step)
                step *= 2
            pooled_g = (s[POOL_HALO:] * (1.0 / jnp.minimum(counts, float(w))) - u_p[:, cols]).astype(BF16)
            pooled.append(pooled_g)
            mixed.append(_dot(pooled_g, wpool_ref[g]))
        mixed = jnp.concatenate(mixed, axis=1)
        sig_p = _sigmoid(g_p)
        silu_p = g_p * sig_p
        mixed_ps = mixed * ps
        y_p = mixed_ps * silu_p
        fp_ref[...] = jnp.concatenate(
            [(ps * silu_p).astype(BF16), (mixed_ps * (sig_p + silu_p * (1.0 - sig_p))).astype(BF16),
             (silu_p * mixed).astype(BF16)] + pooled, axis=1)

        ycat = jnp.concatenate([y_a, y_p], axis=1)
        ycat_ref[...] = ycat.astype(BF16)
        y_b = _dot(ycat.astype(BF16), wout_ref[...]).astype(BF16)
        y_ref[...] = y_b
        y_t = y_b.astype(F32)
        ry = lax.rsqrt(jnp.mean(y_t * y_t, axis=-1, keepdims=True) + NORM_EPS)
        x_next = x_t + gate * (y_t * ry * g_post)
        if head:
            err = x_next - target_ref[...]
            xo_ref[...] = err * (1.0 / D_MODEL)
            loss_ref[...] += jnp.sum(err * err) * (0.5 / D_MODEL)
        else:
            xo_ref[...] = x_next

    tile = (SUBLANES, LANES)
    return pl.pallas_call(
        body, name=f"forward_layer_{layer}", grid=(n_tiles,),
        in_specs=[row(D_MODEL), _layer_spec(vec.shape, layer), _layer_spec(cps.shape, layer),
                  _layer_spec(wpool.shape, layer), _whole_spec(win.shape), _whole_spec(wout.shape)]
        + [row(D_MODEL)] * head,
        out_specs=[row(D_MODEL), row(D_MODEL), row(D_MODEL), row(D_MODEL)] + [row(w) for w in widths[1:]]
        + [_whole_spec(tile)] * head,
        out_shape=[jax.ShapeDtypeStruct((t_len, D_MODEL), F32), jax.ShapeDtypeStruct((t_len, D_MODEL), BF16),
                   jax.ShapeDtypeStruct((t_len, D_MODEL), BF16), jax.ShapeDtypeStruct((t_len, D_MODEL), BF16)]
        + [jax.ShapeDtypeStruct((t_len, w), BF16) for w in widths[1:]] + [jax.ShapeDtypeStruct(tile, F32)] * head,
        scratch_shapes=[pltpu.VMEM((CONV_HALO, CONV_W), F32), pltpu.VMEM((POOL_HALO, POOL_W), F32)],
        compiler_params=_params(dimension_semantics=("arbitrary",)),
    )(x, vec, cps, wpool, win, wout, *([target] * head))


def _backward_layer(layer, dxo, y, x, uc, fa, fp, vec, cps, wpool, wout, win, after):
    t_len = dxo.shape[0]
    n_tiles = t_len // BWD_TILE
    halo_per_tile = BWD_TILE // POOL_HALO
    rev = lambda cols: pl.BlockSpec((BWD_TILE, cols), lambda i: (n_tiles - 1 - i, 0))
    halo_spec = pl.BlockSpec(
        (POOL_HALO, 2 * CONV_W), lambda i: (jnp.maximum((n_tiles - 1 - i) * halo_per_tile - 1, 0), 0))
    gwpool_shape = (len(POOL_WINDOWS), GROUP_D, GROUP_D)

    def body(dxo_ref, y_ref, x_ref, uc_ref, uch_ref, fa_ref, fp_ref, vec_ref, cps_ref, wpool_ref, wout_ref, win_ref,
             *rest):
        dx_ref, dproj_ref, dy_ref, gwpool_ref, dcps_ref, dvec_ref, gwpool_acc, dcc_ref, qc_ref = rest[len(after):]
        i = pl.program_id(0)
        tile = n_tiles - 1 - i

        @pl.when(i == 0)
        def _():
            gwpool_acc[...] = jnp.zeros_like(gwpool_acc)
            dcps_ref[...] = jnp.zeros_like(dcps_ref)
            dvec_ref[...] = jnp.zeros_like(dvec_ref)
            dcc_ref[...] = jnp.zeros_like(dcc_ref)
            qc_ref[...] = jnp.zeros_like(qc_ref)

        shift, scale, gate = vec_ref[0:1, :], vec_ref[1:2, :], vec_ref[2:3, :]
        g_pre, g_post = vec_ref[3:4, :], vec_ref[4:5, :]
        w0, w1, w2 = cps_ref[0:1, :], cps_ref[1:2, :], cps_ref[2:3, :]

        dxo_t = dxo_ref[...]
        y_t = y_ref[...].astype(F32)
        ry = lax.rsqrt(jnp.mean(y_t * y_t, axis=-1, keepdims=True) + NORM_EPS)
        yh = y_t * ry
        dvec_ref[2:3, :] += jnp.sum(dxo_t * yh, axis=0, keepdims=True)
        dyh = dxo_t * (gate * g_post)
        dy_b = (ry * (dyh - yh * jnp.mean(dyh * yh, axis=-1, keepdims=True))).astype(BF16)
        dy_ref[...] = dy_b
        dycat = _dot_nt(dy_b, wout_ref[...])
        dy_a, dy_p = dycat[:, :CONV_W], dycat[:, CONV_W:]

        fa_t = fa_ref[...].astype(F32)
        db_a = dy_a * fa_t[:, :CONV_W]
        dconv = dy_a * fa_t[:, CONV_W:2 * CONV_W]
        dg_a = dy_a * fa_t[:, 2 * CONV_W:]
        uc_t = uc_ref[...].astype(F32)
        u_a, c_a = uc_t[:, :CONV_W], uc_t[:, CONV_W:]
        halo = jnp.where(tile > 0, uch_ref[...].astype(F32), 0.0)[POOL_HALO - CONV_HALO:]
        z = c_a * u_a
        zcat = jnp.concatenate([halo[:, CONV_W:] * halo[:, :CONV_W], z], axis=0)
        z1 = _rows_from_before(zcat, 1)[CONV_HALO:]
        z2 = _rows_from_before(zcat, 2)[CONV_HALO:]
        dccat = jnp.concatenate([dconv, dcc_ref[...]], axis=0)
        dc1 = _rows_from_after(dccat, 1)[:BWD_TILE]
        dc2 = _rows_from_after(dccat, 2)[:BWD_TILE]
        dz = w2 * dconv + w1 * dc1 + w0 * dc2
        dcc_ref[...] = dconv[:CONV_HALO]
        dcps_ref[0:1, :] += jnp.sum(dconv * z2, axis=0, keepdims=True)
        dcps_ref[1:2, :] += jnp.sum(dconv * z1, axis=0, keepdims=True)
        dcps_ref[2:3, :] += jnp.sum(dconv * z, axis=0, keepdims=True)
        du_a = dz * c_a
        dc_a = dz * u_a

        dmixed = (dy_p * fp_ref[:, :POOL_W].astype(F32)).astype(BF16)
        dg_p = dy_p * fp_ref[:, POOL_W:2 * POOL_W].astype(F32)
        dcps_ref[3:4, :] += jnp.sum(dy_p * fp_ref[:, 2 * POOL_W:3 * POOL_W].astype(F32), axis=0, keepdims=True)
        counts = _window_counts(tile * BWD_TILE, BWD_TILE)
        du_p, q_head = [], []
        for g, w in enumerate(POOL_WINDOWS):
            cols = slice(g * GROUP_D, (g + 1) * GROUP_D)
            dm_g = dmixed[:, cols]
            dpooled_g = _dot_nt(dm_g, wpool_ref[g])
            gwpool_acc[g] += _dot_tn(fp_ref[:, 3 * POOL_W + g * GROUP_D:3 * POOL_W + (g + 1) * GROUP_D], dm_g)
            q_g = dpooled_g * (1.0 / jnp.minimum(counts, float(w)))
            q_head.append(q_g[:POOL_HALO])
            s = jnp.concatenate([q_g, qc_ref[:, cols]], axis=0)
            step = 1
            while step < w:
                s = s + _rows_from_after(s, step)
                step *= 2
            du_p.append(s[:BWD_TILE] - dpooled_g)
        qc_ref[...] = jnp.concatenate(q_head, axis=1)
        dproj_b = jnp.concatenate([du_a, db_a, dc_a, dg_a] + du_p + [dg_p], axis=1).astype(BF16)
        dproj_ref[...] = dproj_b

        x_t = x_ref[...]
        rx = lax.rsqrt(jnp.mean(x_t * x_t, axis=-1, keepdims=True) + NORM_EPS)
        xn = x_t * rx
        mod_scale = 1.0 + scale
        dh = _dot_nt(dproj_b, win_ref[...])
        dvec_ref[0:1, :] += jnp.sum(dh, axis=0, keepdims=True)
        dvec_ref[1:2, :] += jnp.sum(dh * xn, axis=0, keepdims=True)
        dxn = dh * (g_pre * mod_scale)
        dx_ref[...] = dxo_t + rx * (dxn - xn * jnp.mean(dxn * xn, axis=-1, keepdims=True))

        @pl.when(i == n_tiles - 1)
        def _():
            gwpool_ref[...] = gwpool_acc[...].astype(BF16)
            sum_dh_xn, sum_dxo_yh = dvec_ref[1:2, :], dvec_ref[2:3, :]
            dvec_ref[1:2, :] = sum_dh_xn * g_pre
            dvec_ref[3:4, :] = sum_dh_xn * mod_scale
            dvec_ref[2:3, :] = sum_dxo_yh * g_post
            dvec_ref[4:5, :] = sum_dxo_yh * gate

    return pl.pallas_call(
        body, name=f"backward_layer_{layer}", grid=(n_tiles,),
        in_specs=[rev(D_MODEL), rev(D_MODEL), rev(D_MODEL), rev(2 * CONV_W), halo_spec, rev(3 * CONV_W),
                  rev(4 * POOL_W), _layer_spec(vec.shape, layer), _layer_spec(cps.shape, layer),
                  _layer_spec(wpool.shape, layer), _whole_spec(wout.shape), _whole_spec(win.shape)]
        + [HBM] * len(after),
        out_specs=[rev(D_MODEL), rev(IN_COLS), rev(D_MODEL), _whole_spec(gwpool_shape),
                   _whole_spec((SUBLANES, CONV_W)), _whole_spec((SUBLANES, D_MODEL))],
        out_shape=[jax.ShapeDtypeStruct((t_len, D_MODEL), F32), jax.ShapeDtypeStruct((t_len, IN_COLS), BF16),
                   jax.ShapeDtypeStruct((t_len, D_MODEL), BF16), jax.ShapeDtypeStruct(gwpool_shape, BF16),
                   jax.ShapeDtypeStruct((SUBLANES, CONV_W), F32), jax.ShapeDtypeStruct((SUBLANES, D_MODEL), F32)],
        scratch_shapes=[pltpu.VMEM(gwpool_shape, F32), pltpu.VMEM((CONV_HALO, CONV_W), F32),
                        pltpu.VMEM((POOL_HALO, POOL_W), F32)],
        compiler_params=_params(dimension_semantics=("arbitrary",)),
    )(dxo, y, x, uc, uc, fa, fp, vec, cps, wpool, wout, win, *after)


def _weight_grads(layer, h, dproj, ycat, dy, exchange, after=()):
    t_len = dy.shape[0]
    n_in, n_out = IN_COLS // GWIN_COLS, D_MODEL // GWOUT_COLS
    args = [h, dproj, ycat, dy, *after]
    in_specs = [_whole_spec(h.shape),
                pl.BlockSpec((t_len, GWIN_COLS), lambda s: (0, jnp.minimum(s, n_in - 1))),
                _whole_spec(ycat.shape),
                pl.BlockSpec((t_len, GWOUT_COLS), lambda s: (0, jnp.maximum(s - n_in, 0)))] + [HBM] * len(after)
    out_shape = [jax.ShapeDtypeStruct((D_MODEL, IN_COLS), BF16), jax.ShapeDtypeStruct((D_MODEL, D_MODEL), BF16)]
    out_specs = [pl.BlockSpec((D_MODEL, GWIN_COLS), lambda s: (0, jnp.minimum(s, n_in - 1))),
                 pl.BlockSpec((D_MODEL, GWOUT_COLS), lambda s: (0, jnp.maximum(s - n_in, 0)))]
    scratch = []
    aliases, split = _host(exchange, args, in_specs, out_shape, out_specs, scratch)

    def body(*refs):
        (h_ref, dproj_ref, ycat_ref, dy_ref, *_, gwin_ref, gwout_ref), hosted = split(refs)
        s = pl.program_id(0)
        if hosted is not None:
            pl.when(s == 0)(hosted[0])

        @pl.when(s < n_in)
        def _():
            gwin_ref[...] = _dot_tn(h_ref[...], dproj_ref[...]).astype(BF16)

        @pl.when(s >= n_in)
        def _():
            gwout_ref[...] = _dot_tn(ycat_ref[...], dy_ref[...]).astype(BF16)

        if hosted is not None:
            pl.when(s == n_in + n_out - 1)(hosted[1])

    return pl.pallas_call(
        body, name=f"weight_grads_{layer}", grid=(n_in + n_out,), in_specs=in_specs, out_specs=out_specs,
        out_shape=out_shape, scratch_shapes=scratch, input_output_aliases=aliases,
        compiler_params=_params(dimension_semantics=("arbitrary",)),
    )(*args)


def _add_sibling_blocks(name, layer, grads, received, core, partials, kinds):
    n_arr = len(grads)

    def body(core_ref, *refs):
        mine, theirs, outs = refs[:n_arr], refs[n_arr:2 * n_arr], refs[-n_arr:]
        for a in range(n_arr):
            outs[a][...] = (mine[a][...].astype(F32) + theirs[a][...].astype(F32)).astype(BF16)

    own_of_kind = [
        pl.BlockSpec((D_MODEL, W_IN_SHARD), lambda q, core_ref: (0, 2 * q + core_ref[0])),
        pl.BlockSpec((W_OUT_SHARD, D_MODEL), lambda q, core_ref: (2 * q + core_ref[0], 0)),
        pl.BlockSpec((POOL_SHARD, GROUP_D), lambda q, core_ref: (2 * q + core_ref[0], 0)),
    ]
    shapes = [_BLOCK_SHAPES[k] for k in kinds]
    recv_specs = [pl.BlockSpec((None,) + s, lambda q, core_ref: (q, 0, 0)) for s in shapes]
    out_specs = [pl.BlockSpec((None, None) + s, lambda q, core_ref: (q, layer, 0, 0)) for s in shapes]
    args = [core, *grads, *received]
    in_specs = [own_of_kind[k] for k in kinds] + recv_specs
    aliases = {}
    if partials is not None:
        aliases = {len(args) + a: a for a in range(n_arr)}
        args += list(partials)
        in_specs += [HBM] * n_arr
    return pl.pallas_call(
        body, name=name,
        grid_spec=pltpu.PrefetchScalarGridSpec(
            num_scalar_prefetch=1, grid=(N_CHIP,), in_specs=in_specs, out_specs=out_specs),
        out_shape=[jax.ShapeDtypeStruct((N_CHIP, DEPTH) + s, BF16) for s in shapes],
        input_output_aliases=aliases,
        compiler_params=_params(dimension_semantics=("arbitrary",)),
    )(*args)


def _modulation_columns(c_all, w_ada):
    def body(c_ref, w_ref, cact_ref, out_ref):
        c_t = c_ref[...]
        c_act = c_t * _sigmoid(c_t)
        cact_ref[...] = c_act
        out_ref[...] = jnp.dot(c_act, w_ref[...], preferred_element_type=F32, precision=lax.Precision.HIGHEST)

    return pl.pallas_call(
        body, name="modulation_columns", grid=(DEPTH,),
        in_specs=[pl.BlockSpec((N_DEV, D_MODEL), lambda l: (0, 0)),
                  pl.BlockSpec((None, D_MODEL, W_IN_SHARD), lambda l: (l, 0, 0))],
        out_specs=[pl.BlockSpec((N_DEV, D_MODEL), lambda l: (0, 0)),
                   pl.BlockSpec((N_DEV, W_IN_SHARD), lambda l: (0, l))],
        out_shape=[jax.ShapeDtypeStruct((N_DEV, D_MODEL), F32),
                   jax.ShapeDtypeStruct((N_DEV, DEPTH * W_IN_SHARD), F32)],
        compiler_params=_params(dimension_semantics=("arbitrary",)),
    )(c_all, w_ada)


def _adamw(w, g, m, v):
    m_new = ADAM_B1 * m + (1.0 - ADAM_B1) * g
    v_new = ADAM_B2 * v + (1.0 - ADAM_B2) * (g * g)
    m_hat = m_new / (1.0 - ADAM_B1 ** ADAM_STEP)
    v_hat = v_new / (1.0 - ADAM_B2 ** ADAM_STEP)
    delta = -ADAM_LR * (m_hat / (jnp.sqrt(v_hat) + ADAM_EPS) + ADAM_WD * w)
    return delta, m_new, v_new


def _adamw_w_ada(w, m, v, c_act_t, dmod_cols):
    def body(w_ref, m_ref, v_ref, ct_ref, dm_ref, g_ref, d_ref, mo_ref, vo_ref):
        g = ct_ref[:, 0:1] * dm_ref[0:1, :]
        for b in range(1, N_DEV):
            g = g + ct_ref[:, b:b + 1] * dm_ref[b:b + 1, :]
        g_ref[...] = g
        d_ref[...], mo_ref[...], vo_ref[...] = _adamw(w_ref[...], g, m_ref[...], v_ref[...])

    big = pl.BlockSpec((None, D_MODEL, W_IN_SHARD), lambda l: (l, 0, 0))
    return pl.pallas_call(
        body, name="adamw_w_ada", grid=(DEPTH,),
        in_specs=[big, big, big, pl.BlockSpec((D_MODEL, N_DEV), lambda l: (0, 0)),
                  pl.BlockSpec((None, N_DEV, W_IN_SHARD), lambda l: (l, 0, 0))],
        out_specs=[big] * 4, out_shape=[jax.ShapeDtypeStruct(w.shape, F32)] * 4,
        compiler_params=_params(dimension_semantics=("arbitrary",)),
    )(w, m, v, c_act_t, dmod_cols)


def _sum_chip_partials(own_ref, recv_ref):
    g = own_ref[...].astype(F32)
    for j in range(N_OTHER_CHIPS):
        g = g + recv_ref[j].astype(F32)
    return g


def _partial_specs(row_tile, cols, first_layer=0):
    own = pl.BlockSpec((None, None, row_tile, cols), lambda l, r, chip_ref: (chip_ref[0], first_layer + l, r, 0))
    recv = pl.BlockSpec((N_OTHER_CHIPS, None, row_tile, cols), lambda l, r, chip_ref: (0, first_layer + l, r, 0))
    return own, recv


def _adamw_reduced(name, w, m, v, partial, received, chip, row_tile, layers, continued):
    depth, rows, cols = w.shape
    first, stop = layers

    def body(chip_ref, w_ref, m_ref, v_ref, own_ref, recv_ref, *rest):
        g_ref, d_ref, mo_ref, vo_ref = rest[-4:]
        g = _sum_chip_partials(own_ref, recv_ref)
        g_ref[...] = g
        d_ref[...], mo_ref[...], vo_ref[...] = _adamw(w_ref[...], g, m_ref[...], v_ref[...])

    blk = pl.BlockSpec((None, row_tile, cols), lambda l, r, chip_ref: (first + l, r, 0))
    args = [chip, w, m, v, partial, received]
    in_specs = [blk, blk, blk, *_partial_specs(row_tile, cols, first)]
    aliases = {}
    if continued is not None:
        aliases = {len(args) + k: k for k in range(4)}
        args += list(continued)
        in_specs += [HBM] * 4
    return pl.pallas_call(
        body, name=name,
        grid_spec=pltpu.PrefetchScalarGridSpec(
            num_scalar_prefetch=1, grid=(stop - first, rows // row_tile), in_specs=in_specs, out_specs=[blk] * 4),
        out_shape=[jax.ShapeDtypeStruct(w.shape, F32)] * 4, input_output_aliases=aliases,
        compiler_params=_params(dimension_semantics=("arbitrary", "arbitrary")),
    )(*args)


def _reduce_w_pool(partial, received, chip):
    def body(chip_ref, own_ref, recv_ref, g_ref):
        g_ref[...] = _sum_chip_partials(own_ref, recv_ref)

    return pl.pallas_call(
        body, name="reduce_w_pool",
        grid_spec=pltpu.PrefetchScalarGridSpec(
            num_scalar_prefetch=1, grid=(DEPTH, 1), in_specs=list(_partial_specs(POOL_SHARD, GROUP_D)),
            out_specs=pl.BlockSpec((None, POOL_SHARD, GROUP_D), lambda l, r, chip_ref: (l, 0, 0))),
        out_shape=jax.ShapeDtypeStruct((DEPTH, POOL_SHARD, GROUP_D), F32),
        compiler_params=_params(dimension_semantics=("arbitrary", "arbitrary")),
    )(chip, partial, received)


def _adamw_small(name, params):
    n = len(params)

    def body(*refs):
        ins, outs = refs[:4 * n], refs[4 * n:]
        for p in range(n):
            w_ref, g_ref, m_ref, v_ref = ins[4 * p:4 * p + 4]
            d_ref, mo_ref, vo_ref = outs[3 * p:3 * p + 3]
            d_ref[...], mo_ref[...], vo_ref[...] = _adamw(w_ref[...], g_ref[...], m_ref[...], v_ref[...])

    vmem = pl.BlockSpec(memory_space=pltpu.VMEM)
    flat = [a for group in params for a in group]
    out_shape = [jax.ShapeDtypeStruct(group[0].shape, F32) for group in params for _ in range(3)]
    outs = pl.pallas_call(
        body, name=name, in_specs=[vmem] * len(flat), out_specs=[vmem] * len(out_shape),
        out_shape=out_shape, compiler_params=_params(),
    )(*flat)
    return [tuple(outs[3 * p:3 * p + 3]) for p in range(n)]


def _sum_sources(slabs, after):
    def body(s_ref, *rest):
        acc = s_ref[0]
        for b in range(1, N_DEV):
            acc = acc + s_ref[b]
        rest[-1][...] = acc

    vmem = pl.BlockSpec(memory_space=pltpu.VMEM)
    return pl.pallas_call(
        body, name="sum_small_grads", in_specs=[vmem] + [HBM] * len(after), out_specs=vmem,
        out_shape=jax.ShapeDtypeStruct(slabs.shape[1:], F32), compiler_params=_params(),
    )(slabs, *after)


def _to_bf16(a, name, layers=None):
    first, stop = layers or (0, a.shape[0])

    def body(a_ref, o_ref):
        o_ref[...] = a_ref[...].astype(BF16)

    block = (None,) + a.shape[1:]
    return pl.pallas_call(
        body, name=name, grid=(stop - first,), in_specs=[pl.BlockSpec(block, lambda l: (first + l, 0, 0))],
        out_specs=pl.BlockSpec(block, lambda l: (l, 0, 0)),
        out_shape=jax.ShapeDtypeStruct((stop - first,) + a.shape[1:], BF16),
        compiler_params=_params(dimension_semantics=("arbitrary",)),
    )(a)


def kernel(x, c, w_ada, b_ada, g_pre, w_in, w_conv, w_pool, pool_scale, w_out, g_post, loss_target, m_w_ada, m_b_ada, m_g_pre, m_w_in, m_w_conv, m_w_pool, m_pool_scale, m_w_out, m_g_post, v_w_ada, v_b_ada, v_g_pre, v_w_in, v_w_conv, v_w_pool, v_pool_scale, v_w_out, v_g_post):
    mx, my, mc = _mesh_position()
    me = _block_id(mx, my, mc)
    chip = (2 * mx + my).astype(jnp.int32).reshape(1)
    core = mc.astype(jnp.int32).reshape(1)
    x0 = x[0]
    target = loss_target[0]
    conv_shard = w_conv.shape[-1]

    own_small = jnp.concatenate([c, w_conv.reshape(1, DEPTH * 3 * conv_shard)], axis=1)
    first_shards = [_to_bf16(w_in, "cast_w_in_0", (0, 1)), _to_bf16(w_out, "cast_w_out_0", (0, 1))]
    all_small = _all_gather_small(own_small, "all_gather_c_w_conv", first_shards)[:, 0, :]
    gathers = [_gather_weights_start(0, *first_shards, [all_small], relayed=True)]
    c_all = all_small[:, :D_MODEL]
    w_conv_full = all_small[:, D_MODEL:].reshape(N_DEV, DEPTH, 3, conv_shard).transpose(1, 2, 0, 3).reshape(
        DEPTH, 3, CONV_W)
    cps = jnp.concatenate([w_conv_full, pool_scale[:, None], jnp.zeros((DEPTH, 4, CONV_W), F32)], axis=1)

    c_act, pieces = _modulation_columns(c_all, w_ada)
    upper = (1, DEPTH)
    upper_shards = [_to_bf16(w_in, "cast_w_in_1", upper), _to_bf16(w_out, "cast_w_out_1", upper)]
    wpool_b = _to_bf16(w_pool.reshape(DEPTH, POOL_ROWS, GROUP_D), "cast_w_pool").reshape(w_pool.shape)
    first_sems_0, _, (zones_0,) = gathers[0]
    neighbour_sems, zones_0 = _gather_weights_pass_on(
        "all_gather_weights_relay_0", NEIGHBOUR_CHIPS, first_sems_0[1], partial(_first_arrival, 0), zones_0,
        [pieces, *upper_shards, wpool_b], relay=True)
    mod_all = _all_gather_small(pieces, "all_gather_modulation", [zones_0[0]])
    mod_mine = lax.dynamic_index_in_dim(mod_all, me, axis=1, keepdims=False)
    mod = mod_mine.reshape(N_DEV, DEPTH, W_IN_SHARD).transpose(1, 0, 2).reshape(DEPTH, 3 * D_MODEL) + b_ada
    zeros_d = jnp.zeros((DEPTH, 3, D_MODEL), F32)
    vec = jnp.concatenate([mod.reshape(DEPTH, 3, D_MODEL), g_pre[:, None], g_post[:, None], zeros_d], axis=1)

    gathers.append(_gather_weights_start(1, *upper_shards, [mod_all]))
    gathers = [(first_sems, shards, landing[k], k) for first_sems, shards, landing in gathers
               for k in range(len(landing))]

    xs, kept, wins, wouts = [x0], [], [], []
    for l in range(DEPTH):
        first_sems, shards, zones, index = gathers[l]
        if l == 0:
            diagonal_sems, zones = _gather_weights_pass_on(
                "all_gather_weights_pass_on_0", DIAGONAL_CHIP, neighbour_sems[3], lambda a, j: a, zones_0,
                [vec, cps, gathers[-1][1][0]])
            passed = [(neighbour_sems[:2], NEIGHBOUR_CHIPS), (diagonal_sems, DIAGONAL_CHIP)]
            win, wout = _gather_weights_finish(l, index, first_sems, passed, shards, zones, neighbour_sems[2])
        else:
            passed_sems, zones = _gather_weights_pass_on(
                f"all_gather_weights_pass_on_{l}", ALL_OTHER_CHIPS, first_sems[1], partial(_first_arrival, index),
                zones, [xs[-1]])
            win, wout = _gather_weights_finish(l, index, first_sems, [(passed_sems, ALL_OTHER_CHIPS)], shards, zones)
        x_next, *for_backward = _forward_layer(
            l, xs[-1], vec, cps, wpool_b, win, wout, target if l == DEPTH - 1 else None)
        xs.append(x_next)
        kept.append(for_backward[:6])
        wins.append(win)
        wouts.append(wout)
    dx, loss_tile = xs[DEPTH], for_backward[6]

    slab_rows = [None] * DEPTH
    partials = received = None
    in_flight = []

    def scatter(layer, grads, from_sibling, after):
        nonlocal partials, received
        partials = _add_sibling_blocks(
            f"grad_add_sibling_{layer}", layer, grads, from_sibling, core, partials, ALL_KINDS)
        chips = _chips_exchange(layer, partials, received, ALL_KINDS)
        sems, partials, received, token = _start_exchange(chips, f"grad_chips_start_{layer}", after)
        in_flight.append((chips, sems, layer))
        return token

    grads_above = None
    issued = []
    for l in reversed(range(DEPTH)):
        y, h, ycat, uc, fa, fp = kept[l]
        dx, dproj, dy, gwpool, dcps, dvec = _backward_layer(
            l, dx, y, xs[l], uc, fa, fp, vec, cps, wpool_b, wouts[l], wins[l], issued)
        slab_rows[l] = jnp.concatenate(
            [dvec[0], dvec[1], dvec[2], dvec[3], dvec[4], dcps[3], dcps[0], dcps[1], dcps[2],
             loss_tile[0] if l == 0 else jnp.zeros((LANES,), F32)])
        after = []
        if l == 0:
            gather_small = _all_gather_exchange(jnp.stack(slab_rows))
            sems_s, slab, slabs, token = _start_exchange(gather_small, "all_gather_small_grads_start")
            after = [token]
        hosted = _sibling_exchange(grads_above, ALL_KINDS) if grads_above is not None else None
        gwin, gwout, *from_sibling = _weight_grads(l, h, dproj, ycat, dy, hosted, after)
        if l == 0:
            _, (slabs,) = _finish_exchange(
                gather_small, "all_gather_small_grads_finish", sems_s, slab, slabs, [gwin])
        issued = [gwin]
        if grads_above is not None:
            issued = [scatter(l + 1, grads_above, from_sibling, [])]
        grads_above = [gwin, gwout, gwpool.reshape(POOL_ROWS, GROUP_D)]
        if l <= 1:
            from_sibling = _run_exchange(_sibling_exchange(grads_above, ALL_KINDS), f"grad_exchange_sibling_{l}")
            token = scatter(l, grads_above, from_sibling, [slabs] if l == 0 else [])
            issued = [token]
            grads_above = None
    grad_x = dx[None]
    chips_0, sems_0, _ = in_flight.pop()

    o = 3 * D_MODEL

    after = [token]
    for chips, sems, l in in_flight:
        partials, received = _finish_exchange(chips, f"grad_chips_finish_{l}", sems, partials, received, after)
        after = []
    upper = (1, DEPTH)
    w_in_upper = _adamw_reduced(
        "adamw_w_in_upper", w_in, m_w_in, v_w_in, partials[0], received[0], chip, ROW_TILE, upper, None)
    w_out_upper = _adamw_reduced(
        "adamw_w_out_upper", w_out, m_w_out, v_w_out, partials[1], received[1], chip, W_OUT_SHARD, upper, None)
    dmod_all = slabs[:, :, :o].reshape(N_DEV, DEPTH, N_DEV, W_IN_SHARD)
    dmod_cols = lax.dynamic_index_in_dim(dmod_all, me, axis=2, keepdims=False).transpose(1, 0, 2) + token[0, 0]
    g_w_ada, d_w_ada, nm_w_ada, nv_w_ada = _adamw_w_ada(w_ada, m_w_ada, v_w_ada, c_act.T, dmod_cols)

    partials, received = _finish_exchange(
        chips_0, "grad_chips_finish_0", sems_0, partials, received, [nv_w_ada, w_in_upper[3], w_out_upper[3]])
    gather_pool = _all_gather_exchange(_reduce_w_pool(partials[2], received[2], chip), axis=1)
    sems_p, pool_rows, pool_landing, token_p = _start_exchange(gather_pool, "all_gather_grad_w_pool_start")
    g_w_in, d_w_in, nm_w_in, nv_w_in = _adamw_reduced(
        "adamw_w_in_0", w_in, m_w_in, v_w_in, partials[0], received[0], chip, ROW_TILE, (0, 1), w_in_upper)
    g_w_out, d_w_out, nm_w_out, nv_w_out = _adamw_reduced(
        "adamw_w_out_0", w_out, m_w_out, v_w_out, partials[1], received[1], chip, W_OUT_SHARD, (0, 1), w_out_upper)
    total = _sum_sources(slabs, [token_p, nv_w_in, nv_w_out])
    loss = total[0, SLAB_COLS]
    g_b_ada = total[:, :o]
    g_g_pre = total[:, o:o + D_MODEL]
    g_g_post = total[:, o + D_MODEL:o + 2 * D_MODEL]
    g_pool_scale = total[:, o + 2 * D_MODEL:o + 2 * D_MODEL + POOL_W]
    g_conv_full = total[:, o + 2 * D_MODEL + POOL_W:SLAB_COLS].reshape(DEPTH, 3, CONV_W)
    g_w_conv = lax.dynamic_slice_in_dim(g_conv_full, me * conv_shard, conv_shard, axis=2)
    small = _adamw_small("adamw_small", [
        (b_ada, g_b_ada, m_b_ada, v_b_ada),
        (g_pre, g_g_pre, m_g_pre, v_g_pre),
        (w_conv, g_w_conv, m_w_conv, v_w_conv),
        (pool_scale, g_pool_scale, m_pool_scale, v_pool_scale),
        (g_post, g_g_post, m_g_post, v_g_post),
    ])
    (d_b_ada, nm_b_ada, nv_b_ada), (d_g_pre, nm_g_pre, nv_g_pre), (d_w_conv, nm_w_conv, nv_w_conv), \
        (d_ps, nm_ps, nv_ps), (d_g_post, nm_g_post, nv_g_post) = small
    _, (g_pool_all,) = _finish_exchange(
        gather_pool, "all_gather_grad_w_pool_finish", sems_p, pool_rows, pool_landing, [nv_w_in, nv_w_out, nv_g_post])
    g_w_pool = g_pool_all.reshape(w_pool.shape)
    ((d_w_pool, nm_w_pool, nv_w_pool),) = _adamw_small("adamw_w_pool", [(w_pool, g_w_pool, m_w_pool, v_w_pool)])

    return (loss, grad_x,
            g_w_ada, g_b_ada, g_g_pre, g_w_in, g_w_conv, g_w_pool, g_pool_scale, g_w_out, g_g_post,
            d_w_ada, d_b_ada, d_g_pre, d_w_in, d_w_conv, d_w_pool, d_ps, d_w_out, d_g_post,
            nm_w_ada, nm_b_ada, nm_g_pre, nm_w_in, nm_w_conv, nm_w_pool, nm_ps, nm_w_out, nm_g_post,
            nv_w_ada, nv_b_ada, nv_g_pre, nv_w_in, nv_w_conv, nv_w_pool, nv_ps, nv_w_out, nv_g_post)
```

```python
from functools import partial

import jax
import jax.numpy as jnp
from jax import lax
from jax.experimental import pallas as pl
from jax.experimental.pallas import tpu as pltpu

F32 = jnp.float32
BF16 = jnp.bfloat16

D_MODEL = 1024
DEPTH = 4
CONV_W = 512
POOL_W = 512
POOL_WINDOWS = (2, 4, 8, 16)
GROUP_D = 128
IN_COLS = 4 * CONV_W + 2 * POOL_W
NORM_EPS = 1e-6

ADAM_LR = 0.001
ADAM_B1 = 0.9
ADAM_B2 = 0.999
ADAM_EPS = 1e-08
ADAM_WD = 0.01
ADAM_STEP = 10

N_DEV = 8
N_CHIP = 4
N_OTHER_CHIPS = N_CHIP - 1
MESH = pl.DeviceIdType.MESH
W_IN_SHARD = IN_COLS // N_DEV
W_OUT_SHARD = D_MODEL // N_DEV
POOL_ROWS = len(POOL_WINDOWS) * GROUP_D
POOL_SHARD = POOL_ROWS // N_DEV

SUBLANES = 8
LANES = 128
VMEM_LIMIT_BYTES = 56 * 1024 * 1024
ROW_TILE = 512
BWD_TILE = 256
GWIN_COLS = 768
GWOUT_COLS = 512
POOL_HALO = 16
CONV_HALO = SUBLANES

SLAB_COLS = 3 * D_MODEL + D_MODEL + D_MODEL + POOL_W + 3 * CONV_W

HBM = pl.BlockSpec(memory_space=pl.ANY)


def _params(**kw):
    return pltpu.CompilerParams(vmem_limit_bytes=VMEM_LIMIT_BYTES, **kw)


def _sigmoid(v):
    return 1.0 / (1.0 + jnp.exp(-v))


def _dot(a, b):
    return jnp.dot(a, b, preferred_element_type=F32)


def _dot_tn(a, b):
    return lax.dot_general(a, b, (((0,), (0,)), ((), ())), preferred_element_type=F32)


def _dot_nt(a, b):
    return lax.dot_general(a, b, (((1,), (1,)), ((), ())), preferred_element_type=F32)


def _rows_from_before(v, k):
    return pltpu.roll(v, k, 0)


def _rows_from_after(v, k):
    return pltpu.roll(v, v.shape[0] - k, 0)


def _window_counts(t0, rows):
    return (lax.broadcasted_iota(jnp.int32, (rows, 1), 0) + (t0 + 1)).astype(F32)


def _split_proj(p32):
    cw = CONV_W
    return (p32[:, 0 * cw:1 * cw], p32[:, 1 * cw:2 * cw], p32[:, 2 * cw:3 * cw], p32[:, 3 * cw:4 * cw],
            p32[:, 4 * cw:4 * cw + POOL_W], p32[:, 4 * cw + POOL_W:])


def _layer_spec(shape, layer):
    nd = len(shape)
    return pl.BlockSpec((None,) + tuple(shape[1:]), lambda i, _l=layer, _n=nd: (_l,) + (0,) * (_n - 1))


def _whole_spec(shape):
    return pl.BlockSpec(tuple(shape), lambda i, _n=len(shape): (0,) * _n, pipeline_mode=pl.Buffered(1))


def _mesh_position():
    return lax.axis_index("x"), lax.axis_index("y"), lax.axis_index("c")


def _block_id(x, y, c):
    return 4 * x + 2 * y + c


def _other_chips(x, y):
    return [(x ^ 1, y), (x, y ^ 1), (x ^ 1, y ^ 1)]


def _col_block(ref, blk):
    return ref.at[:, pl.ds(pl.multiple_of(blk * W_IN_SHARD, LANES), W_IN_SHARD)]


def _row_block(rows):
    def block(ref, blk):
        return ref.at[pl.ds(pl.multiple_of(blk * rows, rows), rows), :]
    return block


_BLOCK_OF = (_col_block, _row_block(W_OUT_SHARD), _row_block(POOL_SHARD))
_BLOCK_SHAPES = ((D_MODEL, W_IN_SHARD), (W_OUT_SHARD, D_MODEL), (POOL_SHARD, GROUP_D))


class _Exchange:
    def __init__(self, inputs, out_shapes, aliases, sem_shapes, make):
        self.inputs, self.out_shapes, self.aliases, self.sem_shapes, self.make = (
            list(inputs), list(out_shapes), dict(aliases), list(sem_shapes), make)


def _run_exchange(exchange, name):
    n_in, n_out = len(exchange.inputs), len(exchange.out_shapes)

    def body(*refs):
        start, finish = exchange.make(refs[:n_in], refs[n_in:n_in + n_out], refs[n_in + n_out:])
        start()
        finish()

    return pl.pallas_call(
        body, name=name, in_specs=[HBM] * n_in, out_specs=[HBM] * n_out, out_shape=exchange.out_shapes,
        scratch_shapes=exchange.sem_shapes, input_output_aliases=exchange.aliases, compiler_params=_params(),
    )(*exchange.inputs)


_SEM = pl.BlockSpec(memory_space=pltpu.SEMAPHORE)
_DATAFLOW = pltpu.SideEffectType.DATAFLOW_SIDE_EFFECTING


def _start_exchange(exchange, name, after=()):
    n_in, n_out, n_sem = len(exchange.inputs), len(exchange.out_shapes), len(exchange.sem_shapes)
    sources = [i for i in range(n_in) if i not in exchange.aliases]
    aliases = {i: n_sem + k for k, i in enumerate(sources)}
    aliases.update({i: n_sem + len(sources) + o for i, o in exchange.aliases.items()})

    def body(*refs):
        in_refs = refs[:n_in]
        outs = refs[n_in + len(after):]
        sems = outs[:n_sem]
        out_refs = outs[n_sem + len(sources):n_sem + len(sources) + n_out]
        exchange.make(in_refs, out_refs, sems)[0]()
        refs[-1][...] = jnp.zeros_like(refs[-1])

    outs = pl.pallas_call(
        body, name=name, in_specs=[HBM] * (n_in + len(after)),
        out_specs=[_SEM] * n_sem + [HBM] * (len(sources) + n_out) + [pl.BlockSpec(memory_space=pltpu.VMEM)],
        out_shape=(exchange.sem_shapes + [pltpu.HBM(exchange.inputs[i].shape, exchange.inputs[i].dtype) for i in sources]
                   + [pltpu.HBM(s.shape, s.dtype) for s in exchange.out_shapes]
                   + [jax.ShapeDtypeStruct((SUBLANES, LANES), F32)]),
        input_output_aliases=aliases, compiler_params=_params(has_side_effects=_DATAFLOW),
    )(*exchange.inputs, *after)
    return outs[:n_sem], outs[n_sem:n_sem + len(sources)], outs[n_sem + len(sources):-1], outs[-1]


def _finish_exchange(exchange, name, sems, sources, landing, after):
    n_src, n_out, n_sem = len(sources), len(landing), len(sems)
    n_in = len(exchange.inputs)
    source_at = [i for i in range(n_in) if i not in exchange.aliases]

    def body(*refs):
        src_refs, out_refs = refs[:n_src], refs[n_src:n_src + n_out]
        sem_refs = refs[n_src + n_out:n_src + n_out + n_sem]
        in_refs = [None] * n_in
        for k, i in enumerate(source_at):
            in_refs[i] = src_refs[k]
        for i, o in exchange.aliases.items():
            in_refs[i] = out_refs[o]
        exchange.make(in_refs, out_refs, sem_refs)[1]()

    arrays = list(sources) + list(landing)
    outs = pl.pallas_call(
        body, name=name, in_specs=[HBM] * len(arrays) + [_SEM] * n_sem + [HBM] * len(after),
        out_specs=[HBM] * len(arrays), out_shape=[pltpu.HBM(a.shape, a.dtype) for a in arrays],
        input_output_aliases={i: i for i in range(len(arrays))}, compiler_params=_params(has_side_effects=_DATAFLOW),
    )(*arrays, *sems, *after)
    return outs[:n_src], outs[n_src:]


N_GATHERED = 2
FIRST_COPIES = 1 + N_OTHER_CHIPS
_GATHERED_SHAPES = ((D_MODEL, IN_COLS), (D_MODEL, D_MODEL))
ALL_OTHER_CHIPS, NEIGHBOUR_CHIPS, DIAGONAL_CHIP = (0, 1, 2), (0, 1), (2,)


def _first_sem(layer, a, k):
    return (layer * N_GATHERED + a) * FIRST_COPIES + k


def _first_arrival(layer, a, j):
    return _first_sem(layer, a, 1 + j)


def _gather_copy(window_of, full_ref, blk, send_sem, recv_sem, to, src=None):
    window = window_of(full_ref, blk)
    return pltpu.make_async_remote_copy(
        src_ref=window if src is None else src, dst_ref=window, send_sem=send_sem, recv_sem=recv_sem,
        device_id=to, device_id_type=MESH)


def _first_copies(layer, shard_refs, full_refs, send_sems, recv_sems, local_sems, relayed):
    x, y, c = _mesh_position()
    me = _block_id(x, y, c)
    chips = [_other_chips(x, y)[j] for j in (NEIGHBOUR_CHIPS if relayed else ALL_OTHER_CHIPS)]
    own, remote = [], []
    for a in range(N_GATHERED):
        shard = shard_refs[a].at[layer]
        own.append(pltpu.make_async_copy(
            shard, _BLOCK_OF[a](full_refs[a], me), local_sems.at[layer * N_GATHERED + a]))
        targets = [(x, y, 1 - c)] + [(*chip, c) for chip in chips]
        remote += [_gather_copy(_BLOCK_OF[a], full_refs[a], me, send_sems.at[_first_sem(layer, a, k)],
                                recv_sems.at[_first_sem(layer, a, k)], to, src=shard)
                   for k, to in enumerate(targets)]
    return own, remote


def _gather_weights_start(first_layer, win_shards, wout_shards, after, relayed=False):
    n_layers = win_shards.shape[0]
    n_first = n_layers * N_GATHERED * FIRST_COPIES
    sem_shapes = [pltpu.SemaphoreType.DMA((n_first,)), pltpu.SemaphoreType.DMA((n_first,)),
                  pltpu.SemaphoreType.DMA((n_layers * N_GATHERED,))]
    shards = [win_shards, wout_shards]

    def body(win_sh, wout_sh, *rest):
        send_sems, recv_sems, local_sems, win_thru, wout_thru, *landing = rest[len(after):]
        for layer in range(n_layers):
            own, remote = _first_copies(layer, (win_sh, wout_sh), landing[N_GATHERED * layer:N_GATHERED * (layer + 1)],
                                        send_sems, recv_sems, local_sems, relayed)
            for cp in own + remote:
                cp.start()

    outs = pl.pallas_call(
        body, name=f"all_gather_weights_start_{first_layer}", in_specs=[HBM] * (2 + len(after)),
        out_specs=[_SEM] * 3 + [HBM] * (2 + n_layers * N_GATHERED),
        out_shape=(sem_shapes + [pltpu.HBM(s.shape, s.dtype) for s in shards]
                   + [pltpu.HBM(s, BF16) for _ in range(n_layers) for s in _GATHERED_SHAPES]),
        input_output_aliases={0: 3, 1: 4}, compiler_params=_params(has_side_effects=_DATAFLOW),
    )(*shards, *after)
    landing = outs[5:]
    return outs[:3], outs[3:5], [landing[N_GATHERED * l:N_GATHERED * (l + 1)] for l in range(n_layers)]


def _passed_on_copies(full_refs, send_sems, recv_sems, core_of_block, chips):
    x, y, c = _mesh_position()
    return [_gather_copy(_BLOCK_OF[a], full_refs[a], _block_id(*_other_chips(x, y)[j], core_of_block),
                         send_sems.at[a * N_OTHER_CHIPS + j], recv_sems.at[a * N_OTHER_CHIPS + j], (x, y, 1 - c))
            for a in range(N_GATHERED) for j in chips]


def _relayed_copies(full_refs, send_sems, recv_sems):
    x, y, c = _mesh_position()
    source, to = (x ^ (1 - c), y ^ c), (x ^ c, y ^ (1 - c))
    return [_gather_copy(_BLOCK_OF[a], full_refs[a], _block_id(*source, c), send_sems.at[a], recv_sems.at[a], (*to, c))
            for a in range(N_GATHERED)]


def _gather_weights_pass_on(name, chips, arrival_sems, arrival_sem_of, landing, after, relay=False):
    n = N_GATHERED * N_OTHER_CHIPS
    sem_shapes = [pltpu.SemaphoreType.DMA((n,))] * 2 + [pltpu.SemaphoreType.DMA((N_GATHERED,))] * (2 if relay else 0)

    def body(win_ref, wout_ref, arrivals, *rest):
        sems = rest[len(after):len(after) + len(sem_shapes)]
        x, y, c = _mesh_position()
        full_refs = (win_ref, wout_ref)
        passed = _passed_on_copies(full_refs, sems[0], sems[1], c, chips)
        relayed = _relayed_copies(full_refs, sems[2], sems[3]) if relay else []
        for a in range(N_GATHERED):
            for j in chips:
                sem = arrivals.at[arrival_sem_of(a, j)]
                _gather_copy(_BLOCK_OF[a], full_refs[a], _block_id(*_other_chips(x, y)[j], c), sem, sem,
                             (x, y, c)).wait_recv()
            for cp in relayed[a:a + 1] + passed[a * len(chips):(a + 1) * len(chips)]:
                cp.start()

    outs = pl.pallas_call(
        body, name=name, in_specs=[HBM] * N_GATHERED + [_SEM] + [HBM] * len(after),
        out_specs=[_SEM] * len(sem_shapes) + [HBM] * N_GATHERED,
        out_shape=sem_shapes + [pltpu.HBM(a.shape, a.dtype) for a in landing],
        input_output_aliases={a: len(sem_shapes) + a for a in range(N_GATHERED)},
        compiler_params=_params(has_side_effects=_DATAFLOW),
    )(*landing, arrival_sems, *after)
    return outs[:len(sem_shapes)], outs[len(sem_shapes):]


def _gather_weights_finish(layer, index, first_sems, passed, shards, landing, relay_send_sems=None):
    relayed = relay_send_sems is not None
    passed_sems = [sem for (send, recv), _ in passed for sem in (send, recv)] + ([relay_send_sems] if relayed else [])

    def body(win_ref, wout_ref, first_send, first_recv, local_sems, *rest):
        sems, (win_sh, wout_sh) = rest[:len(passed_sems)], rest[len(passed_sems):len(passed_sems) + 2]
        x, y, c = _mesh_position()
        full_refs = (win_ref, wout_ref)
        own, sent = _first_copies(index, (win_sh, wout_sh), full_refs, first_send, first_recv, local_sems, relayed)
        for a in range(N_GATHERED):
            sem = _first_sem(index, a, 0)
            _gather_copy(_BLOCK_OF[a], full_refs[a], _block_id(x, y, 1 - c), first_recv.at[sem], first_recv.at[sem],
                         (x, y, c)).wait_recv()
        for p, (_, chips) in enumerate(passed):
            for cp in _passed_on_copies(full_refs, sems[2 * p], sems[2 * p + 1], 1 - c, chips):
                cp.wait_recv()
            sent += _passed_on_copies(full_refs, sems[2 * p], sems[2 * p + 1], c, chips)
        if relayed:
            sent += _relayed_copies(full_refs, sems[-1], sems[-1])
        for cp in sent:
            cp.wait_send()
        for cp in own:
            cp.wait()

    return pl.pallas_call(
        body, name=f"all_gather_weights_finish_{layer}",
        in_specs=[HBM] * N_GATHERED + [_SEM] * (3 + len(passed_sems)) + [HBM] * 2,
        out_specs=[HBM] * N_GATHERED, out_shape=[pltpu.HBM(a.shape, a.dtype) for a in landing],
        input_output_aliases={a: a for a in range(N_GATHERED)}, compiler_params=_params(has_side_effects=_DATAFLOW),
    )(*landing, *first_sems, *passed_sems, *shards)


ALL_KINDS = (0, 1, 2)


def _sibling_exchange(grads, kinds):
    n_arr = len(grads)

    def make(in_refs, out_refs, sems):
        send_sems, recv_sems = sems
        x, y, c = _mesh_position()
        copies = [pltpu.make_async_remote_copy(
            src_ref=_BLOCK_OF[kinds[a]](in_refs[a], 2 * q + (1 - c)), dst_ref=out_refs[a].at[q],
            send_sem=send_sems.at[a * N_CHIP + q], recv_sem=recv_sems.at[a * N_CHIP + q],
            device_id=(x, y, 1 - c), device_id_type=MESH)
            for a in range(n_arr) for q in range(N_CHIP)]

        def start():
            for cp in copies:
                cp.start()

        def finish():
            for cp in copies:
                cp.wait_recv()
            for cp in copies:
                cp.wait_send()

        return start, finish

    return _Exchange(
        grads, [jax.ShapeDtypeStruct((N_CHIP,) + _BLOCK_SHAPES[k], BF16) for k in kinds], {},
        [pltpu.SemaphoreType.DMA((n_arr * N_CHIP,)), pltpu.SemaphoreType.DMA((n_arr * N_CHIP,))], make)


def _chips_exchange(layer, partials, received, kinds):
    n_arr = len(partials)

    def make(in_refs, out_refs, sems):
        send_sems, recv_sems = sems
        x, y, c = _mesh_position()
        copies = [pltpu.make_async_remote_copy(
            src_ref=in_refs[a].at[2 * qx + qy, layer], dst_ref=out_refs[a].at[j, layer],
            send_sem=send_sems.at[a * N_OTHER_CHIPS + j], recv_sem=recv_sems.at[a * N_OTHER_CHIPS + j],
            device_id=(qx, qy, c), device_id_type=MESH)
            for a in range(n_arr) for j, (qx, qy) in enumerate(_other_chips(x, y))]

        def start():
            for cp in copies:
                cp.start()

        def finish():
            for cp in copies:
                cp.wait_recv()
            for cp in copies:
                cp.wait_send()

        return start, finish

    inputs = list(partials)
    aliases = {}
    if received is not None:
        inputs += list(received)
        aliases = {n_arr + a: a for a in range(n_arr)}
    return _Exchange(
        inputs, [jax.ShapeDtypeStruct((N_OTHER_CHIPS, DEPTH) + _BLOCK_SHAPES[k], BF16) for k in kinds], aliases,
        [pltpu.SemaphoreType.DMA((n_arr * N_OTHER_CHIPS,)), pltpu.SemaphoreType.DMA((n_arr * N_OTHER_CHIPS,))], make)


def _all_gather_exchange(v, axis=0):
    def make(in_refs, out_refs, sems):
        send_sems, recv_sems, local_sem = sems
        x, y, c = _mesh_position()
        me = _block_id(x, y, c)
        block = lambda blk: out_refs[0].at[(slice(None),) * axis + (blk,)]
        own = pltpu.make_async_copy(in_refs[0], block(me), local_sem.at[0])
        sends, arrivals = [], []
        for k in range(1, N_DEV):
            px, py, pc = x ^ ((k >> 2) & 1), y ^ ((k >> 1) & 1), c ^ (k & 1)
            sends.append(pltpu.make_async_remote_copy(
                src_ref=in_refs[0], dst_ref=block(me), send_sem=send_sems.at[k - 1],
                recv_sem=recv_sems.at[k - 1], device_id=(px, py, pc), device_id_type=MESH))
            arrivals.append(pltpu.make_async_remote_copy(
                src_ref=in_refs[0], dst_ref=block(_block_id(px, py, pc)), send_sem=send_sems.at[k - 1],
                recv_sem=recv_sems.at[k - 1], device_id=(x, y, c), device_id_type=MESH))

        def start():
            for cp in [own] + sends:
                cp.start()

        def finish():
            for cp in arrivals:
                cp.wait_recv()
            for cp in sends:
                cp.wait_send()
            own.wait()

        return start, finish

    return _Exchange(
        [v], [jax.ShapeDtypeStruct(v.shape[:axis] + (N_DEV,) + v.shape[axis:], v.dtype)], {},
        [pltpu.SemaphoreType.DMA((N_DEV - 1,)), pltpu.SemaphoreType.DMA((N_DEV - 1,)),
         pltpu.SemaphoreType.DMA((1,))], make)


def _host(exchange, args, in_specs, out_shape, out_specs, scratch):
    n_own = (len(args), len(out_shape), len(scratch))
    if exchange is None:
        return {}, lambda refs: (refs, None)
    n_ex = (len(exchange.inputs), len(exchange.out_shapes), len(exchange.sem_shapes))
    aliases = {n_own[0] + i: n_own[1] + o for i, o in exchange.aliases.items()}
    args += exchange.inputs
    in_specs += [HBM] * n_ex[0]
    out_shape += exchange.out_shapes
    out_specs += [HBM] * n_ex[1]
    scratch += exchange.sem_shapes

    def split(refs):
        own, theirs, at = [], [], 0
        for mine, ex in zip(n_own, n_ex):
            own += refs[at:at + mine]
            theirs.append(refs[at + mine:at + mine + ex])
            at += mine + ex
        return own, exchange.make(*theirs)

    return aliases, split


def _all_gather_small(v, name, after=()):
    vmem = pl.BlockSpec(memory_space=pltpu.VMEM)

    def body(v_ref, *rest):
        out_ref, send_sems, recv_sems = rest[len(after):]
        x, y, c = _mesh_position()
        me = _block_id(x, y, c)
        out_ref[me] = v_ref[...]
        sends = []
        for k in range(1, N_DEV):
            px, py, pc = x ^ ((k >> 2) & 1), y ^ ((k >> 1) & 1), c ^ (k & 1)
            send = pltpu.make_async_remote_copy(
                src_ref=v_ref, dst_ref=out_ref.at[me], send_sem=send_sems.at[k - 1], recv_sem=recv_sems.at[k - 1],
                device_id=(px, py, pc), device_id_type=MESH)
            send.start()
            sends.append((send, _block_id(px, py, pc)))
        for k, (send, peer) in enumerate(sends):
            pltpu.make_async_remote_copy(
                src_ref=v_ref, dst_ref=out_ref.at[peer], send_sem=send_sems.at[k], recv_sem=recv_sems.at[k],
                device_id=(x, y, c), device_id_type=MESH).wait_recv()
        for send, _ in sends:
            send.wait_send()

    return pl.pallas_call(
        body, name=name, in_specs=[vmem] + [HBM] * len(after), out_specs=vmem,
        out_shape=jax.ShapeDtypeStruct((N_DEV,) + v.shape, v.dtype),
        scratch_shapes=[pltpu.SemaphoreType.DMA((N_DEV - 1,)), pltpu.SemaphoreType.DMA((N_DEV - 1,))],
        compiler_params=_params(),
    )(v, *after)


def _forward_layer(layer, x, vec, cps, wpool, win, wout, target=None):
    t_len = x.shape[0]
    n_tiles = t_len // ROW_TILE
    row = lambda cols: pl.BlockSpec((ROW_TILE, cols), lambda i: (i, 0))
    widths = (D_MODEL, 2 * CONV_W, 3 * CONV_W, 4 * POOL_W)
    head = target is not None

    def body(x_ref, vec_ref, cps_ref, wpool_ref, win_ref, wout_ref, *rest):
        target_ref = rest[0] if head else None
        xo_ref, y_ref, h_ref, ycat_ref, uc_ref, fa_ref, fp_ref = rest[head:head + 7]
        loss_ref = rest[head + 7] if head else None
        zc_ref, pc_ref = rest[-2:]
        i = pl.program_id(0)

        @pl.when(i == 0)
        def _():
            zc_ref[...] = jnp.zeros_like(zc_ref)
            pc_ref[...] = jnp.zeros_like(pc_ref)
            if head:
                loss_ref[...] = jnp.zeros_like(loss_ref)

        x_t = x_ref[...]
        shift, scale, gate = vec_ref[0:1, :], vec_ref[1:2, :], vec_ref[2:3, :]
        g_pre, g_post = vec_ref[3:4, :], vec_ref[4:5, :]
        w0, w1, w2, ps = cps_ref[0:1, :], cps_ref[1:2, :], cps_ref[2:3, :], cps_ref[3:4, :]
        rx = lax.rsqrt(jnp.mean(x_t * x_t, axis=-1, keepdims=True) + NORM_EPS)
        h = (x_t * rx) * g_pre * (1.0 + scale) + shift
        h_ref[...] = h.astype(BF16)
        proj = _dot(h.astype(BF16), win_ref[...])
        u_a, b_a, c_a, g_a, u_p, g_p = _split_proj(proj)
        uc_ref[...] = jnp.concatenate([u_a, c_a], axis=1).astype(BF16)

        z = c_a * u_a
        zcat = jnp.concatenate([zc_ref[...], z], axis=0)
        zc_ref[...] = z[ROW_TILE - CONV_HALO:]
        conv = (w0 * _rows_from_before(zcat, 2)[CONV_HALO:] + w1 * _rows_from_before(zcat, 1)[CONV_HALO:] + w2 * z)
        sig_a = _sigmoid(g_a)
        silu_a = g_a * sig_a
        b_conv = b_a * conv
        y_a = b_conv * silu_a
        fa_ref[...] = jnp.concatenate(
            [silu_a * conv, silu_a * b_a, b_conv * (sig_a + silu_a * (1.0 - sig_a))], axis=1).astype(BF16)

        pcat = jnp.concatenate([pc_ref[...], u_p], axis=0)
        pc_ref[...] = u_p[ROW_TILE - POOL_HALO:]
        counts = _window_counts(i * ROW_TILE, ROW_TILE)
        pooled, mixed = [], []
        for g, w in enumerate(POOL_WINDOWS):
            cols = slice(g * GROUP_D, (g + 1) * GROUP_D)
            s = pcat[:, cols]
            step = 1
            while step < w:
                s = s + _rows_from_before(s, step)
                step *= 2
            pooled_g = (s[POOL_HALO:] * (1.0 / jnp.minimum(counts, float(w))) - u_p[:, cols]).astype(BF16)
            pooled.append(pooled_g)
            mixed.append(_dot(pooled_g, wpool_ref[g]))
        mixed = jnp.concatenate(mixed, axis=1)
        sig_p = _sigmoid(g_p)
        silu_p = g_p * sig_p
        mixed_ps = mixed * ps
        y_p = mixed_ps * silu_p
        fp_ref[...] = jnp.concatenate(
            [(ps * silu_p).astype(BF16), (mixed_ps * (sig_p + silu_p * (1.0 - sig_p))).astype(BF16),
             (silu_p * mixed).astype(BF16)] + pooled, axis=1)

        ycat = jnp.concatenate([y_a, y_p], axis=1)
        ycat_ref[...] = ycat.astype(BF16)
        y_b = _dot(ycat.astype(BF16), wout_ref[...]).astype(BF16)
        y_ref[...] = y_b
        y_t = y_b.astype(F32)
        ry = lax.rsqrt(jnp.mean(y_t * y_t, axis=-1, keepdims=True) + NORM_EPS)
        x_next = x_t + gate * (y_t * ry * g_post)
        if head:
            err = x_next - target_ref[...]
            xo_ref[...] = err * (1.0 / D_MODEL)
            loss_ref[...] += jnp.sum(err * err) * (0.5 / D_MODEL)
        else:
            xo_ref[...] = x_next

    tile = (SUBLANES, LANES)
    return pl.pallas_call(
        body, name=f"forward_layer_{layer}", grid=(n_tiles,),
        in_specs=[row(D_MODEL), _layer_spec(vec.shape, layer), _layer_spec(cps.shape, layer),
                  _layer_spec(wpool.shape, layer), _whole_spec(win.shape), _whole_spec(wout.shape)]
        + [row(D_MODEL)] * head,
        out_specs=[row(D_MODEL), row(D_MODEL), row(D_MODEL), row(D_MODEL)] + [row(w) for w in widths[1:]]
        + [_whole_spec(tile)] * head,
        out_shape=[jax.ShapeDtypeStruct((t_len, D_MODEL), F32), jax.ShapeDtypeStruct((t_len, D_MODEL), BF16),
                   jax.ShapeDtypeStruct((t_len, D_MODEL), BF16), jax.ShapeDtypeStruct((t_len, D_MODEL), BF16)]
        + [jax.ShapeDtypeStruct((t_len, w), BF16) for w in widths[1:]] + [jax.ShapeDtypeStruct(tile, F32)] * head,
        scratch_shapes=[pltpu.VMEM((CONV_HALO, CONV_W), F32), pltpu.VMEM((POOL_HALO, POOL_W), F32)],
        compiler_params=_params(dimension_semantics=("arbitrary",)),
    )(x, vec, cps, wpool, win, wout, *([target] * head))


def _backward_layer(layer, dxo, y, x, uc, fa, fp, vec, cps, wpool, wout, win, after):
    t_len = dxo.shape[0]
    n_tiles = t_len // BWD_TILE
    halo_per_tile = BWD_TILE // POOL_HALO
    rev = lambda cols: pl.BlockSpec((BWD_TILE, cols), lambda i: (n_tiles - 1 - i, 0))
    halo_spec = pl.BlockSpec(
        (POOL_HALO, 2 * CONV_W), lambda i: (jnp.maximum((n_tiles - 1 - i) * halo_per_tile - 1, 0), 0))
    gwpool_shape = (len(POOL_WINDOWS), GROUP_D, GROUP_D)

    def body(dxo_ref, y_ref, x_ref, uc_ref, uch_ref, fa_ref, fp_ref, vec_ref, cps_ref, wpool_ref, wout_ref, win_ref,
             *rest):
        dx_ref, dproj_ref, dy_ref, gwpool_ref, dcps_ref, dvec_ref, gwpool_acc, dcc_ref, qc_ref = rest[len(after):]
        i = pl.program_id(0)
        tile = n_tiles - 1 - i

        @pl.when(i == 0)
        def _():
            gwpool_acc[...] = jnp.zeros_like(gwpool_acc)
            dcps_ref[...] = jnp.zeros_like(dcps_ref)
            dvec_ref[...] = jnp.zeros_like(dvec_ref)
            dcc_ref[...] = jnp.zeros_like(dcc_ref)
            qc_ref[...] = jnp.zeros_like(qc_ref)

        shift, scale, gate = vec_ref[0:1, :], vec_ref[1:2, :], vec_ref[2:3, :]
        g_pre, g_post = vec_ref[3:4, :], vec_ref[4:5, :]
        w0, w1, w2 = cps_ref[0:1, :], cps_ref[1:2, :], cps_ref[2:3, :]

        dxo_t = dxo_ref[...]
        y_t = y_ref[...].astype(F32)
        ry = lax.rsqrt(jnp.mean(y_t * y_t, axis=-1, keepdims=True) + NORM_EPS)
        yh = y_t * ry
        dvec_ref[2:3, :] += jnp.sum(dxo_t * yh, axis=0, keepdims=True)
        dyh = dxo_t * (gate * g_post)
        dy_b = (ry * (dyh - yh * jnp.mean(dyh * yh, axis=-1, keepdims=True))).astype(BF16)
        dy_ref[...] = dy_b
        dycat = _dot_nt(dy_b, wout_ref[...])
        dy_a, dy_p = dycat[:, :CONV_W], dycat[:, CONV_W:]

        fa_t = fa_ref[...].astype(F32)
        db_a = dy_a * fa_t[:, :CONV_W]
        dconv = dy_a * fa_t[:, CONV_W:2 * CONV_W]
        dg_a = dy_a * fa_t[:, 2 * CONV_W:]
        uc_t = uc_ref[...].astype(F32)
        u_a, c_a = uc_t[:, :CONV_W], uc_t[:, CONV_W:]
        halo = jnp.where(tile > 0, uch_ref[...].astype(F32), 0.0)[POOL_HALO - CONV_HALO:]
        z = c_a * u_a
        zcat = jnp.concatenate([halo[:, CONV_W:] * halo[:, :CONV_W], z], axis=0)
        z1 = _rows_from_before(zcat, 1)[CONV_HALO:]
        z2 = _rows_from_before(zcat, 2)[CONV_HALO:]
        dccat = jnp.concatenate([dconv, dcc_ref[...]], axis=0)
        dc1 = _rows_from_after(dccat, 1)[:BWD_TILE]
        dc2 = _rows_from_after(dccat, 2)[:BWD_TILE]
        dz = w2 * dconv + w1 * dc1 + w0 * dc2
        dcc_ref[...] = dconv[:CONV_HALO]
        dcps_ref[0:1, :] += jnp.sum(dconv * z2, axis=0, keepdims=True)
        dcps_ref[1:2, :] += jnp.sum(dconv * z1, axis=0, keepdims=True)
        dcps_ref[2:3, :] += jnp.sum(dconv * z, axis=0, keepdims=True)
        du_a = dz * c_a
        dc_a = dz * u_a

        dmixed = (dy_p * fp_ref[:, :POOL_W].astype(F32)).astype(BF16)
        dg_p = dy_p * fp_ref[:, POOL_W:2 * POOL_W].astype(F32)
        dcps_ref[3:4, :] += jnp.sum(dy_p * fp_ref[:, 2 * POOL_W:3 * POOL_W].astype(F32), axis=0, keepdims=True)
        counts = _window_counts(tile * BWD_TILE, BWD_TILE)
        du_p, q_head = [], []
        for g, w in enumerate(POOL_WINDOWS):
            cols = slice(g * GROUP_D, (g + 1) * GROUP_D)
            dm_g = dmixed[:, cols]
            dpooled_g = _dot_nt(dm_g, wpool_ref[g])
            gwpool_acc[g] += _dot_tn(fp_ref[:, 3 * POOL_W + g * GROUP_D:3 * POOL_W + (g + 1) * GROUP_D], dm_g)
            q_g = dpooled_g * (1.0 / jnp.minimum(counts, float(w)))
            q_head.append(q_g[:POOL_HALO])
            s = jnp.concatenate([q_g, qc_ref[:, cols]], axis=0)
            step = 1
            while step < w:
                s = s + _rows_from_after(s, step)
                step *= 2
            du_p.append(s[:BWD_TILE] - dpooled_g)
        qc_ref[...] = jnp.concatenate(q_head, axis=1)
        dproj_b = jnp.concatenate([du_a, db_a, dc_a, dg_a] + du_p + [dg_p], axis=1).astype(BF16)
        dproj_ref[...] = dproj_b

        x_t = x_ref[...]
        rx = lax.rsqrt(jnp.mean(x_t * x_t, axis=-1, keepdims=True) + NORM_EPS)
        xn = x_t * rx
        mod_scale = 1.0 + scale
        dh = _dot_nt(dproj_b, win_ref[...])
        dvec_ref[0:1, :] += jnp.sum(dh, axis=0, keepdims=True)
        dvec_ref[1:2, :] += jnp.sum(dh * xn, axis=0, keepdims=True)
        dxn = dh * (g_pre * mod_scale)
        dx_ref[...] = dxo_t + rx * (dxn - xn * jnp.mean(dxn * xn, axis=-1, keepdims=True))

        @pl.when(i == n_tiles - 1)
        def _():
            gwpool_ref[...] = gwpool_acc[...].astype(BF16)
            sum_dh_xn, sum_dxo_yh = dvec_ref[1:2, :], dvec_ref[2:3, :]
            dvec_ref[1:2, :] = sum_dh_xn * g_pre
            dvec_ref[3:4, :] = sum_dh_xn * mod_scale
            dvec_ref[2:3, :] = sum_dxo_yh * g_post
            dvec_ref[4:5, :] = sum_dxo_yh * gate

    return pl.pallas_call(
        body, name=f"backward_layer_{layer}", grid=(n_tiles,),
        in_specs=[rev(D_MODEL), rev(D_MODEL), rev(D_MODEL), rev(2 * CONV_W), halo_spec, rev(3 * CONV_W),
                  rev(4 * POOL_W), _layer_spec(vec.shape, layer), _layer_spec(cps.shape, layer),
                  _layer_spec(wpool.shape, layer), _whole_spec(wout.shape), _whole_spec(win.shape)]
        + [HBM] * len(after),
        out_specs=[rev(D_MODEL), rev(IN_COLS), rev(D_MODEL), _whole_spec(gwpool_shape),
                   _whole_spec((SUBLANES, CONV_W)), _whole_spec((SUBLANES, D_MODEL))],
        out_shape=[jax.ShapeDtypeStruct((t_len, D_MODEL), F32), jax.ShapeDtypeStruct((t_len, IN_COLS), BF16),
                   jax.ShapeDtypeStruct((t_len, D_MODEL), BF16), jax.ShapeDtypeStruct(gwpool_shape, BF16),
                   jax.ShapeDtypeStruct((SUBLANES, CONV_W), F32), jax.ShapeDtypeStruct((SUBLANES, D_MODEL), F32)],
        scratch_shapes=[pltpu.VMEM(gwpool_shape, F32), pltpu.VMEM((CONV_HALO, CONV_W), F32),
                        pltpu.VMEM((POOL_HALO, POOL_W), F32)],
        compiler_params=_params(dimension_semantics=("arbitrary",)),
    )(dxo, y, x, uc, uc, fa, fp, vec, cps, wpool, wout, win, *after)


def _weight_grads(layer, h, dproj, ycat, dy, exchange, after=()):
    t_len = dy.shape[0]
    n_in, n_out = IN_COLS // GWIN_COLS, D_MODEL // GWOUT_COLS
    args = [h, dproj, ycat, dy, *after]
    in_specs = [_whole_spec(h.shape),
                pl.BlockSpec((t_len, GWIN_COLS), lambda s: (0, jnp.minimum(s, n_in - 1))),
                _whole_spec(ycat.shape),
                pl.BlockSpec((t_len, GWOUT_COLS), lambda s: (0, jnp.maximum(s - n_in, 0)))] + [HBM] * len(after)
    out_shape = [jax.ShapeDtypeStruct((D_MODEL, IN_COLS), BF16), jax.ShapeDtypeStruct((D_MODEL, D_MODEL), BF16)]
    out_specs = [pl.BlockSpec((D_MODEL, GWIN_COLS), lambda s: (0, jnp.minimum(s, n_in - 1))),
                 pl.BlockSpec((D_MODEL, GWOUT_COLS), lambda s: (0, jnp.maximum(s - n_in, 0)))]
    scratch = []
    aliases, split = _host(exchange, args, in_specs, out_shape, out_specs, scratch)

    def body(*refs):
        (h_ref, dproj_ref, ycat_ref, dy_ref, *_, gwin_ref, gwout_ref), hosted = split(refs)
        s = pl.program_id(0)
        if hosted is not None:
            pl.when(s == 0)(hosted[0])

        @pl.when(s < n_in)
        def _():
            gwin_ref[...] = _dot_tn(h_ref[...], dproj_ref[...]).astype(BF16)

        @pl.when(s >= n_in)
        def _():
            gwout_ref[...] = _dot_tn(ycat_ref[...], dy_ref[...]).astype(BF16)

        if hosted is not None:
            pl.when(s == n_in + n_out - 1)(hosted[1])

    return pl.pallas_call(
        body, name=f"weight_grads_{layer}", grid=(n_in + n_out,), in_specs=in_specs, out_specs=out_specs,
        out_shape=out_shape, scratch_shapes=scratch, input_output_aliases=aliases,
        compiler_params=_params(dimension_semantics=("arbitrary",)),
    )(*args)


def _add_sibling_blocks(name, layer, grads, received, core, partials, kinds):
    n_arr = len(grads)

    def body(core_ref, *refs):
        mine, theirs, outs = refs[:n_arr], refs[n_arr:2 * n_arr], refs[-n_arr:]
        for a in range(n_arr):
            outs[a][...] = (mine[a][...].astype(F32) + theirs[a][...].astype(F32)).astype(BF16)

    own_of_kind = [
        pl.BlockSpec((D_MODEL, W_IN_SHARD), lambda q, core_ref: (0, 2 * q + core_ref[0])),
        pl.BlockSpec((W_OUT_SHARD, D_MODEL), lambda q, core_ref: (2 * q + core_ref[0], 0)),
        pl.BlockSpec((POOL_SHARD, GROUP_D), lambda q, core_ref: (2 * q + core_ref[0], 0)),
    ]
    shapes = [_BLOCK_SHAPES[k] for k in kinds]
    recv_specs = [pl.BlockSpec((None,) + s, lambda q, core_ref: (q, 0, 0)) for s in shapes]
    out_specs = [pl.BlockSpec((None, None) + s, lambda q, core_ref: (q, layer, 0, 0)) for s in shapes]
    args = [core, *grads, *received]
    in_specs = [own_of_kind[k] for k in kinds] + recv_specs
    aliases = {}
    if partials is not None:
        aliases = {len(args) + a: a for a in range(n_arr)}
        args += list(partials)
        in_specs += [HBM] * n_arr
    return pl.pallas_call(
        body, name=name,
        grid_spec=pltpu.PrefetchScalarGridSpec(
            num_scalar_prefetch=1, grid=(N_CHIP,), in_specs=in_specs, out_specs=out_specs),
        out_shape=[jax.ShapeDtypeStruct((N_CHIP, DEPTH) + s, BF16) for s in shapes],
        input_output_aliases=aliases,
        compiler_params=_params(dimension_semantics=("arbitrary",)),
    )(*args)


def _modulation_columns(c_all, w_ada):
    def body(c_ref, w_ref, cact_ref, out_ref):
        c_t = c_ref[...]
        c_act = c_t * _sigmoid(c_t)
        cact_ref[...] = c_act
        out_ref[...] = jnp.dot(c_act, w_ref[...], preferred_element_type=F32, precision=lax.Precision.HIGHEST)

    return pl.pallas_call(
        body, name="modulation_columns", grid=(DEPTH,),
        in_specs=[pl.BlockSpec((N_DEV, D_MODEL), lambda l: (0, 0)),
                  pl.BlockSpec((None, D_MODEL, W_IN_SHARD), lambda l: (l, 0, 0))],
        out_specs=[pl.BlockSpec((N_DEV, D_MODEL), lambda l: (0, 0)),
                   pl.BlockSpec((N_DEV, W_IN_SHARD), lambda l: (0, l))],
        out_shape=[jax.ShapeDtypeStruct((N_DEV, D_MODEL), F32),
                   jax.ShapeDtypeStruct((N_DEV, DEPTH * W_IN_SHARD), F32)],
        compiler_params=_params(dimension_semantics=("arbitrary",)),
    )(c_all, w_ada)


def _adamw(w, g, m, v):
    m_new = ADAM_B1 * m + (1.0 - ADAM_B1) * g
    v_new = ADAM_B2 * v + (1.0 - ADAM_B2) * (g * g)
    m_hat = m_new / (1.0 - ADAM_B1 ** ADAM_STEP)
    v_hat = v_new / (1.0 - ADAM_B2 ** ADAM_STEP)
    delta = -ADAM_LR * (m_hat / (jnp.sqrt(v_hat) + ADAM_EPS) + ADAM_WD * w)
    return delta, m_new, v_new


def _adamw_w_ada(w, m, v, c_act_t, dmod_cols):
    def body(w_ref, m_ref, v_ref, ct_ref, dm_ref, g_ref, d_ref, mo_ref, vo_ref):
        g = ct_ref[:, 0:1] * dm_ref[0:1, :]
        for b in range(1, N_DEV):
            g = g + ct_ref[:, b:b + 1] * dm_ref[b:b + 1, :]
        g_ref[...] = g
        d_ref[...], mo_ref[...], vo_ref[...] = _adamw(w_ref[...], g, m_ref[...], v_ref[...])

    big = pl.BlockSpec((None, D_MODEL, W_IN_SHARD), lambda l: (l, 0, 0))
    return pl.pallas_call(
        body, name="adamw_w_ada", grid=(DEPTH,),
        in_specs=[big, big, big, pl.BlockSpec((D_MODEL, N_DEV), lambda l: (0, 0)),
                  pl.BlockSpec((None, N_DEV, W_IN_SHARD), lambda l: (l, 0, 0))],
        out_specs=[big] * 4, out_shape=[jax.ShapeDtypeStruct(w.shape, F32)] * 4,
        compiler_params=_params(dimension_semantics=("arbitrary",)),
    )(w, m, v, c_act_t, dmod_cols)


def _sum_chip_partials(own_ref, recv_ref):
    g = own_ref[...].astype(F32)
    for j in range(N_OTHER_CHIPS):
        g = g + recv_ref[j].astype(F32)
    return g


def _partial_specs(row_tile, cols, first_layer=0):
    own = pl.BlockSpec((None, None, row_tile, cols), lambda l, r, chip_ref: (chip_ref[0], first_layer + l, r, 0))
    recv = pl.BlockSpec((N_OTHER_CHIPS, None, row_tile, cols), lambda l, r, chip_ref: (0, first_layer + l, r, 0))
    return own, recv


def _adamw_reduced(name, w, m, v, partial, received, chip, row_tile, layers, continued, after=()):
    depth, rows, cols = w.shape
    first, stop = layers

    def body(chip_ref, w_ref, m_ref, v_ref, own_ref, recv_ref, *rest):
        g_ref, d_ref, mo_ref, vo_ref = rest[-4:]
        g = _sum_chip_partials(own_ref, recv_ref)
        g_ref[...] = g
        d_ref[...], mo_ref[...], vo_ref[...] = _adamw(w_ref[...], g, m_ref[...], v_ref[...])

    blk = pl.BlockSpec((None, row_tile, cols), lambda l, r, chip_ref: (first + l, r, 0))
    args = [chip, w, m, v, partial, received]
    in_specs = [blk, blk, blk, *_partial_specs(row_tile, cols, first)]
    aliases = {}
    if continued is not None:
        aliases = {len(args) + k: k for k in range(4)}
        args += list(continued)
        in_specs += [HBM] * 4
    args += after
    in_specs += [HBM] * len(after)
    return pl.pallas_call(
        body, name=name,
        grid_spec=pltpu.PrefetchScalarGridSpec(
            num_scalar_prefetch=1, grid=(stop - first, rows // row_tile), in_specs=in_specs, out_specs=[blk] * 4),
        out_shape=[jax.ShapeDtypeStruct(w.shape, F32)] * 4, input_output_aliases=aliases,
        compiler_params=_params(dimension_semantics=("arbitrary", "arbitrary")),
    )(*args)


def _reduce_w_pool(partial, received, chip):
    def body(chip_ref, own_ref, recv_ref, g_ref):
        g_ref[...] = _sum_chip_partials(own_ref, recv_ref)

    return pl.pallas_call(
        body, name="reduce_w_pool",
        grid_spec=pltpu.PrefetchScalarGridSpec(
            num_scalar_prefetch=1, grid=(DEPTH, 1), in_specs=list(_partial_specs(POOL_SHARD, GROUP_D)),
            out_specs=pl.BlockSpec((None, POOL_SHARD, GROUP_D), lambda l, r, chip_ref: (l, 0, 0))),
        out_shape=jax.ShapeDtypeStruct((DEPTH, POOL_SHARD, GROUP_D), F32),
        compiler_params=_params(dimension_semantics=("arbitrary", "arbitrary")),
    )(chip, partial, received)


def _adamw_small(name, params):
    n = len(params)

    def body(*refs):
        ins, outs = refs[:4 * n], refs[4 * n:]
        for p in range(n):
            w_ref, g_ref, m_ref, v_ref = ins[4 * p:4 * p + 4]
            d_ref, mo_ref, vo_ref = outs[3 * p:3 * p + 3]
            d_ref[...], mo_ref[...], vo_ref[...] = _adamw(w_ref[...], g_ref[...], m_ref[...], v_ref[...])

    vmem = pl.BlockSpec(memory_space=pltpu.VMEM)
    flat = [a for group in params for a in group]
    out_shape = [jax.ShapeDtypeStruct(group[0].shape, F32) for group in params for _ in range(3)]
    outs = pl.pallas_call(
        body, name=name, in_specs=[vmem] * len(flat), out_specs=[vmem] * len(out_shape),
        out_shape=out_shape, compiler_params=_params(),
    )(*flat)
    return [tuple(outs[3 * p:3 * p + 3]) for p in range(n)]


def _sum_sources(slabs, after):
    def body(s_ref, *rest):
        acc = s_ref[0]
        for b in range(1, N_DEV):
            acc = acc + s_ref[b]
        rest[-1][...] = acc

    vmem = pl.BlockSpec(memory_space=pltpu.VMEM)
    return pl.pallas_call(
        body, name="sum_small_grads", in_specs=[vmem] + [HBM] * len(after), out_specs=vmem,
        out_shape=jax.ShapeDtypeStruct(slabs.shape[1:], F32), compiler_params=_params(),
    )(slabs, *after)


def _to_bf16(a, name, layers=None):
    first, stop = layers or (0, a.shape[0])

    def body(a_ref, o_ref):
        o_ref[...] = a_ref[...].astype(BF16)

    block = (None,) + a.shape[1:]
    return pl.pallas_call(
        body, name=name, grid=(stop - first,), in_specs=[pl.BlockSpec(block, lambda l: (first + l, 0, 0))],
        out_specs=pl.BlockSpec(block, lambda l: (l, 0, 0)),
        out_shape=jax.ShapeDtypeStruct((stop - first,) + a.shape[1:], BF16),
        compiler_params=_params(dimension_semantics=("arbitrary",)),
    )(a)


def kernel(x, c, w_ada, b_ada, g_pre, w_in, w_conv, w_pool, pool_scale, w_out, g_post, loss_target, m_w_ada, m_b_ada, m_g_pre, m_w_in, m_w_conv, m_w_pool, m_pool_scale, m_w_out, m_g_post, v_w_ada, v_b_ada, v_g_pre, v_w_in, v_w_conv, v_w_pool, v_pool_scale, v_w_out, v_g_post):
    mx, my, mc = _mesh_position()
    me = _block_id(mx, my, mc)
    chip = (2 * mx + my).astype(jnp.int32).reshape(1)
    core = mc.astype(jnp.int32).reshape(1)
    x0 = x[0]
    target = loss_target[0]
    conv_shard = w_conv.shape[-1]

    own_small = jnp.concatenate([c, w_conv.reshape(1, DEPTH * 3 * conv_shard)], axis=1)
    first_shards = [_to_bf16(w_in, "cast_w_in_0", (0, 1)), _to_bf16(w_out, "cast_w_out_0", (0, 1))]
    all_small = _all_gather_small(own_small, "all_gather_c_w_conv", first_shards)[:, 0, :]
    gathers = [_gather_weights_start(0, *first_shards, [all_small], relayed=True)]
    c_all = all_small[:, :D_MODEL]
    w_conv_full = all_small[:, D_MODEL:].reshape(N_DEV, DEPTH, 3, conv_shard).transpose(1, 2, 0, 3).reshape(
        DEPTH, 3, CONV_W)
    cps = jnp.concatenate([w_conv_full, pool_scale[:, None], jnp.zeros((DEPTH, 4, CONV_W), F32)], axis=1)

    c_act, pieces = _modulation_columns(c_all, w_ada)
    upper = (1, DEPTH)
    upper_shards = [_to_bf16(w_in, "cast_w_in_1", upper), _to_bf16(w_out, "cast_w_out_1", upper)]
    wpool_b = _to_bf16(w_pool.reshape(DEPTH, POOL_ROWS, GROUP_D), "cast_w_pool").reshape(w_pool.shape)
    first_sems_0, _, (zones_0,) = gathers[0]
    neighbour_sems, zones_0 = _gather_weights_pass_on(
        "all_gather_weights_relay_0", NEIGHBOUR_CHIPS, first_sems_0[1], partial(_first_arrival, 0), zones_0,
        [pieces, *upper_shards, wpool_b], relay=True)
    mod_all = _all_gather_small(pieces, "all_gather_modulation", [zones_0[0]])
    mod_mine = lax.dynamic_index_in_dim(mod_all, me, axis=1, keepdims=False)
    mod = mod_mine.reshape(N_DEV, DEPTH, W_IN_SHARD).transpose(1, 0, 2).reshape(DEPTH, 3 * D_MODEL) + b_ada
    zeros_d = jnp.zeros((DEPTH, 3, D_MODEL), F32)
    vec = jnp.concatenate([mod.reshape(DEPTH, 3, D_MODEL), g_pre[:, None], g_post[:, None], zeros_d], axis=1)

    gathers.append(_gather_weights_start(1, *upper_shards, [mod_all]))
    gathers = [(first_sems, shards, landing[k], k) for first_sems, shards, landing in gathers
               for k in range(len(landing))]

    xs, kept, wins, wouts = [x0], [], [], []
    for l in range(DEPTH):
        first_sems, shards, zones, index = gathers[l]
        if l == 0:
            diagonal_sems, zones = _gather_weights_pass_on(
                "all_gather_weights_pass_on_0", DIAGONAL_CHIP, neighbour_sems[3], lambda a, j: a, zones_0,
                [vec, cps, gathers[-1][1][0]])
            passed = [(neighbour_sems[:2], NEIGHBOUR_CHIPS), (diagonal_sems, DIAGONAL_CHIP)]
            win, wout = _gather_weights_finish(l, index, first_sems, passed, shards, zones, neighbour_sems[2])
        else:
            passed_sems, zones = _gather_weights_pass_on(
                f"all_gather_weights_pass_on_{l}", ALL_OTHER_CHIPS, first_sems[1], partial(_first_arrival, index),
                zones, [xs[-1]])
            win, wout = _gather_weights_finish(l, index, first_sems, [(passed_sems, ALL_OTHER_CHIPS)], shards, zones)
        x_next, *for_backward = _forward_layer(
            l, xs[-1], vec, cps, wpool_b, win, wout, target if l == DEPTH - 1 else None)
        xs.append(x_next)
        kept.append(for_backward[:6])
        wins.append(win)
        wouts.append(wout)
    dx, loss_tile = xs[DEPTH], for_backward[6]

    slab_rows = [None] * DEPTH
    partials = received = None
    in_flight = []

    def scatter(layer, grads, from_sibling, after):
        nonlocal partials, received
        partials = _add_sibling_blocks(
            f"grad_add_sibling_{layer}", layer, grads, from_sibling, core, partials, ALL_KINDS)
        chips = _chips_exchange(layer, partials, received, ALL_KINDS)
        sems, partials, received, token = _start_exchange(chips, f"grad_chips_start_{layer}", after)
        in_flight.append((chips, sems, layer))
        return token

    grads_above = None
    issued = []
    for l in reversed(range(DEPTH)):
        y, h, ycat, uc, fa, fp = kept[l]
        dx, dproj, dy, gwpool, dcps, dvec = _backward_layer(
            l, dx, y, xs[l], uc, fa, fp, vec, cps, wpool_b, wouts[l], wins[l], issued)
        slab_rows[l] = jnp.concatenate(
            [dvec[0], dvec[1], dvec[2], dvec[3], dvec[4], dcps[3], dcps[0], dcps[1], dcps[2],
             loss_tile[0] if l == 0 else jnp.zeros((LANES,), F32)])
        after = []
        if l == 0:
            gather_small = _all_gather_exchange(jnp.stack(slab_rows))
            sems_s, slab, slabs, token = _start_exchange(gather_small, "all_gather_small_grads_start")
            after = [token]
        hosted = _sibling_exchange(grads_above, ALL_KINDS) if grads_above is not None else None
        gwin, gwout, *from_sibling = _weight_grads(l, h, dproj, ycat, dy, hosted, after)
        if l == 0:
            _, (slabs,) = _finish_exchange(
                gather_small, "all_gather_small_grads_finish", sems_s, slab, slabs, [gwin])
        issued = [gwin]
        if grads_above is not None:
            issued = [scatter(l + 1, grads_above, from_sibling, [])]
        grads_above = [gwin, gwout, gwpool.reshape(POOL_ROWS, GROUP_D)]
        if l <= 1:
            from_sibling = _run_exchange(_sibling_exchange(grads_above, ALL_KINDS), f"grad_exchange_sibling_{l}")
            token = scatter(l, grads_above, from_sibling, [slabs] if l == 0 else [])
            issued = [token]
            grads_above = None
    grad_x = dx[None]
    chips_0, sems_0, _ = in_flight.pop()

    o = 3 * D_MODEL

    after = [token]
    for chips, sems, l in in_flight:
        partials, received = _finish_exchange(chips, f"grad_chips_finish_{l}", sems, partials, received, after)
        after = []
    upper = (1, DEPTH)
    w_in_upper = _adamw_reduced(
        "adamw_w_in_upper", w_in, m_w_in, v_w_in, partials[0], received[0], chip, ROW_TILE, upper, None)
    w_out_upper = _adamw_reduced(
        "adamw_w_out_upper", w_out, m_w_out, v_w_out, partials[1], received[1], chip, W_OUT_SHARD, upper, None)
    dmod_all = slabs[:, :, :o].reshape(N_DEV, DEPTH, N_DEV, W_IN_SHARD)
    dmod_cols = lax.dynamic_index_in_dim(dmod_all, me, axis=2, keepdims=False).transpose(1, 0, 2) + token[0, 0]
    g_w_ada, d_w_ada, nm_w_ada, nv_w_ada = _adamw_w_ada(w_ada, m_w_ada, v_w_ada, c_act.T, dmod_cols)

    partials, received = _finish_exchange(
        chips_0, "grad_chips_finish_0", sems_0, partials, received, [nv_w_ada, w_in_upper[3], w_out_upper[3]])
    gather_pool = _all_gather_exchange(_reduce_w_pool(partials[2], received[2], chip), axis=1)
    sems_p, pool_rows, pool_landing, token_p = _start_exchange(gather_pool, "all_gather_grad_w_pool_start")
    g_w_in, d_w_in, nm_w_in, nv_w_in = _adamw_reduced(
        "adamw_w_in_0", w_in, m_w_in, v_w_in, partials[0], received[0], chip, ROW_TILE, (0, 1), w_in_upper, [token_p])
    g_w_out, d_w_out, nm_w_out, nv_w_out = _adamw_reduced(
        "adamw_w_out_0", w_out, m_w_out, v_w_out, partials[1], received[1], chip, W_OUT_SHARD, (0, 1), w_out_upper,
        [token_p])
    total = _sum_sources(slabs, [token_p, nv_w_in, nv_w_out])
    loss = total[0, SLAB_COLS]
    g_b_ada = total[:, :o]
    g_g_pre = total[:, o:o + D_MODEL]
    g_g_post = total[:, o + D_MODEL:o + 2 * D_MODEL]
    g_pool_scale = total[:, o + 2 * D_MODEL:o + 2 * D_MODEL + POOL_W]
    g_conv_full = total[:, o + 2 * D_MODEL + POOL_W:SLAB_COLS].reshape(DEPTH, 3, CONV_W)
    g_w_conv = lax.dynamic_slice_in_dim(g_conv_full, me * conv_shard, conv_shard, axis=2)
    small = _adamw_small("adamw_small", [
        (b_ada, g_b_ada, m_b_ada, v_b_ada),
        (g_pre, g_g_pre, m_g_pre, v_g_pre),
        (w_conv, g_w_conv, m_w_conv, v_w_conv),
        (pool_scale, g_pool_scale, m_pool_scale, v_pool_scale),
        (g_post, g_g_post, m_g_post, v_g_post),
    ])
    (d_b_ada, nm_b_ada, nv_b_ada), (d_g_pre, nm_g_pre, nv_g_pre), (d_w_conv, nm_w_conv, nv_w_conv), \
        (d_ps, nm_ps, nv_ps), (d_g_post, nm_g_post, nv_g_post) = small
    _, (g_pool_all,) = _finish_exchange(
        gather_pool, "all_gather_grad_w_pool_finish", sems_p, pool_rows, pool_landing, [nv_w_in, nv_w_out, nv_g_post])
    g_w_pool = g_pool_all.reshape(w_pool.shape)
    ((d_w_pool, nm_w_pool, nv_w_pool),) = _adamw_small("adamw_w_pool", [(w_pool, g_w_pool, m_w_pool, v_w_pool)])

    return (loss, grad_x,
            g_w_ada, g_b_ada, g_g_pre, g_w_in, g_w_conv, g_w_pool, g_pool_scale, g_w_out, g_g_post,
            d_w_ada, d_b_ada, d_g_pre, d_w_in, d_w_conv, d_w_pool, d_ps, d_w_out, d_g_post,
            nm_w_ada, nm_b_ada, nm_g_pre, nm_w_in, nm_w_conv, nm_w_pool, nm_ps, nm_w_out, nm_g_post,
            nv_w_ada, nv_b_ada, nv_g_pre, nv_w_in, nv_w_conv, nv_w_pool, nv_ps, nv_w_out, nv_g_post)
```

```python
from functools import partial

import jax
import jax.numpy as jnp
from jax import lax
from jax.experimental import pallas as pl
from jax.experimental.pallas import tpu as pltpu

F32 = jnp.float32
BF16 = jnp.bfloat16

D_MODEL = 1024
DEPTH = 4
CONV_W = 512
POOL_W = 512
POOL_WINDOWS = (2, 4, 8, 16)
GROUP_D = 128
IN_COLS = 4 * CONV_W + 2 * POOL_W
NORM_EPS = 1e-6

ADAM_LR = 0.001
ADAM_B1 = 0.9
ADAM_B2 = 0.999
ADAM_EPS = 1e-08
ADAM_WD = 0.01
ADAM_STEP = 10

N_DEV = 8
N_CHIP = 4
N_OTHER_CHIPS = N_CHIP - 1
MESH = pl.DeviceIdType.MESH
W_IN_SHARD = IN_COLS // N_DEV
W_OUT_SHARD = D_MODEL // N_DEV
POOL_ROWS = len(POOL_WINDOWS) * GROUP_D
POOL_SHARD = POOL_ROWS // N_DEV

SUBLANES = 8
LANES = 128
VMEM_LIMIT_BYTES = 56 * 1024 * 1024
ROW_TILE = 512
BWD_TILE = 256
GWIN_COLS = 768
GWOUT_COLS = 512
POOL_HALO = 16
CONV_HALO = SUBLANES

SLAB_COLS = 3 * D_MODEL + D_MODEL + D_MODEL + POOL_W + 3 * CONV_W

HBM = pl.BlockSpec(memory_space=pl.ANY)


def _params(**kw):
    return pltpu.CompilerParams(vmem_limit_bytes=VMEM_LIMIT_BYTES, **kw)


def _sigmoid(v):
    return 1.0 / (1.0 + jnp.exp(-v))


def _dot(a, b):
    return jnp.dot(a, b, preferred_element_type=F32)


def _dot_tn(a, b):
    return lax.dot_general(a, b, (((0,), (0,)), ((), ())), preferred_element_type=F32)


def _dot_nt(a, b):
    return lax.dot_general(a, b, (((1,), (1,)), ((), ())), preferred_element_type=F32)


def _rows_from_before(v, k):
    return pltpu.roll(v, k, 0)


def _rows_from_after(v, k):
    return pltpu.roll(v, v.shape[0] - k, 0)


def _window_counts(t0, rows):
    return (lax.broadcasted_iota(jnp.int32, (rows, 1), 0) + (t0 + 1)).astype(F32)


def _split_proj(p32):
    cw = CONV_W
    return (p32[:, 0 * cw:1 * cw], p32[:, 1 * cw:2 * cw], p32[:, 2 * cw:3 * cw], p32[:, 3 * cw:4 * cw],
            p32[:, 4 * cw:4 * cw + POOL_W], p32[:, 4 * cw + POOL_W:])


def _layer_spec(shape, layer):
    nd = len(shape)
    return pl.BlockSpec((None,) + tuple(shape[1:]), lambda i, _l=layer, _n=nd: (_l,) + (0,) * (_n - 1))


def _whole_spec(shape):
    return pl.BlockSpec(tuple(shape), lambda i, _n=len(shape): (0,) * _n, pipeline_mode=pl.Buffered(1))


def _mesh_position():
    return lax.axis_index("x"), lax.axis_index("y"), lax.axis_index("c")


def _block_id(x, y, c):
    return 4 * x + 2 * y + c


def _other_chips(x, y):
    return [(x ^ 1, y), (x, y ^ 1), (x ^ 1, y ^ 1)]


def _col_block(ref, blk):
    return ref.at[:, pl.ds(pl.multiple_of(blk * W_IN_SHARD, LANES), W_IN_SHARD)]


def _row_block(rows):
    def block(ref, blk):
        return ref.at[pl.ds(pl.multiple_of(blk * rows, rows), rows), :]
    return block


_BLOCK_OF = (_col_block, _row_block(W_OUT_SHARD), _row_block(POOL_SHARD))
_BLOCK_SHAPES = ((D_MODEL, W_IN_SHARD), (W_OUT_SHARD, D_MODEL), (POOL_SHARD, GROUP_D))


class _Exchange:
    def __init__(self, inputs, out_shapes, aliases, sem_shapes, make):
        self.inputs, self.out_shapes, self.aliases, self.sem_shapes, self.make = (
            list(inputs), list(out_shapes), dict(aliases), list(sem_shapes), make)


def _run_exchange(exchange, name):
    n_in, n_out = len(exchange.inputs), len(exchange.out_shapes)

    def body(*refs):
        start, finish = exchange.make(refs[:n_in], refs[n_in:n_in + n_out], refs[n_in + n_out:])
        start()
        finish()

    return pl.pallas_call(
        body, name=name, in_specs=[HBM] * n_in, out_specs=[HBM] * n_out, out_shape=exchange.out_shapes,
        scratch_shapes=exchange.sem_shapes, input_output_aliases=exchange.aliases, compiler_params=_params(),
    )(*exchange.inputs)


_SEM = pl.BlockSpec(memory_space=pltpu.SEMAPHORE)
_DATAFLOW = pltpu.SideEffectType.DATAFLOW_SIDE_EFFECTING


def _start_exchange(exchange, name, after=()):
    n_in, n_out, n_sem = len(exchange.inputs), len(exchange.out_shapes), len(exchange.sem_shapes)
    sources = [i for i in range(n_in) if i not in exchange.aliases]
    aliases = {i: n_sem + k for k, i in enumerate(sources)}
    aliases.update({i: n_sem + len(sources) + o for i, o in exchange.aliases.items()})

    def body(*refs):
        in_refs = refs[:n_in]
        outs = refs[n_in + len(after):]
        sems = outs[:n_sem]
        out_refs = outs[n_sem + len(sources):n_sem + len(sources) + n_out]
        exchange.make(in_refs, out_refs, sems)[0]()
        refs[-1][...] = jnp.zeros_like(refs[-1])

    outs = pl.pallas_call(
        body, name=name, in_specs=[HBM] * (n_in + len(after)),
        out_specs=[_SEM] * n_sem + [HBM] * (len(sources) + n_out) + [pl.BlockSpec(memory_space=pltpu.VMEM)],
        out_shape=(exchange.sem_shapes + [pltpu.HBM(exchange.inputs[i].shape, exchange.inputs[i].dtype) for i in sources]
                   + [pltpu.HBM(s.shape, s.dtype) for s in exchange.out_shapes]
                   + [jax.ShapeDtypeStruct((SUBLANES, LANES), F32)]),
        input_output_aliases=aliases, compiler_params=_params(has_side_effects=_DATAFLOW),
    )(*exchange.inputs, *after)
    return outs[:n_sem], outs[n_sem:n_sem + len(sources)], outs[n_sem + len(sources):-1], outs[-1]


def _finish_exchange(exchange, name, sems, sources, landing, after):
    n_src, n_out, n_sem = len(sources), len(landing), len(sems)
    n_in = len(exchange.inputs)
    source_at = [i for i in range(n_in) if i not in exchange.aliases]

    def body(*refs):
        src_refs, out_refs = refs[:n_src], refs[n_src:n_src + n_out]
        sem_refs = refs[n_src + n_out:n_src + n_out + n_sem]
        in_refs = [None] * n_in
        for k, i in enumerate(source_at):
            in_refs[i] = src_refs[k]
        for i, o in exchange.aliases.items():
            in_refs[i] = out_refs[o]
        exchange.make(in_refs, out_refs, sem_refs)[1]()

    arrays = list(sources) + list(landing)
    outs = pl.pallas_call(
        body, name=name, in_specs=[HBM] * len(arrays) + [_SEM] * n_sem + [HBM] * len(after),
        out_specs=[HBM] * len(arrays), out_shape=[pltpu.HBM(a.shape, a.dtype) for a in arrays],
        input_output_aliases={i: i for i in range(len(arrays))}, compiler_params=_params(has_side_effects=_DATAFLOW),
    )(*arrays, *sems, *after)
    return outs[:n_src], outs[n_src:]


N_GATHERED = 2
FIRST_COPIES = 1 + N_OTHER_CHIPS
_GATHERED_SHAPES = ((D_MODEL, IN_COLS), (D_MODEL, D_MODEL))
ALL_OTHER_CHIPS, NEIGHBOUR_CHIPS, DIAGONAL_CHIP = (0, 1, 2), (0, 1), (2,)


def _first_sem(layer, a, k):
    return (layer * N_GATHERED + a) * FIRST_COPIES + k


def _first_arrival(layer, a, j):
    return _first_sem(layer, a, 1 + j)


def _gather_copy(window_of, full_ref, blk, send_sem, recv_sem, to, src=None):
    window = window_of(full_ref, blk)
    return pltpu.make_async_remote_copy(
        src_ref=window if src is None else src, dst_ref=window, send_sem=send_sem, recv_sem=recv_sem,
        device_id=to, device_id_type=MESH)


def _first_copies(layer, shard_refs, full_refs, send_sems, recv_sems, local_sems, relayed):
    x, y, c = _mesh_position()
    me = _block_id(x, y, c)
    chips = [_other_chips(x, y)[j] for j in (NEIGHBOUR_CHIPS if relayed else ALL_OTHER_CHIPS)]
    own, remote = [], []
    for a in range(N_GATHERED):
        shard = shard_refs[a].at[layer]
        own.append(pltpu.make_async_copy(
            shard, _BLOCK_OF[a](full_refs[a], me), local_sems.at[layer * N_GATHERED + a]))
        targets = [(x, y, 1 - c)] + [(*chip, c) for chip in chips]
        remote += [_gather_copy(_BLOCK_OF[a], full_refs[a], me, send_sems.at[_first_sem(layer, a, k)],
                                recv_sems.at[_first_sem(layer, a, k)], to, src=shard)
                   for k, to in enumerate(targets)]
    return own, remote


def _gather_weights_start(first_layer, win_shards, wout_shards, after, relayed=False):
    n_layers = win_shards.shape[0]
    n_first = n_layers * N_GATHERED * FIRST_COPIES
    sem_shapes = [pltpu.SemaphoreType.DMA((n_first,)), pltpu.SemaphoreType.DMA((n_first,)),
                  pltpu.SemaphoreType.DMA((n_layers * N_GATHERED,))]
    shards = [win_shards, wout_shards]

    def body(win_sh, wout_sh, *rest):
        send_sems, recv_sems, local_sems, win_thru, wout_thru, *landing = rest[len(after):]
        for layer in range(n_layers):
            own, remote = _first_copies(layer, (win_sh, wout_sh), landing[N_GATHERED * layer:N_GATHERED * (layer + 1)],
                                        send_sems, recv_sems, local_sems, relayed)
            for cp in own + remote:
                cp.start()

    outs = pl.pallas_call(
        body, name=f"all_gather_weights_start_{first_layer}", in_specs=[HBM] * (2 + len(after)),
        out_specs=[_SEM] * 3 + [HBM] * (2 + n_layers * N_GATHERED),
        out_shape=(sem_shapes + [pltpu.HBM(s.shape, s.dtype) for s in shards]
                   + [pltpu.HBM(s, BF16) for _ in range(n_layers) for s in _GATHERED_SHAPES]),
        input_output_aliases={0: 3, 1: 4}, compiler_params=_params(has_side_effects=_DATAFLOW),
    )(*shards, *after)
    landing = outs[5:]
    return outs[:3], outs[3:5], [landing[N_GATHERED * l:N_GATHERED * (l + 1)] for l in range(n_layers)]


def _passed_on_copies(full_refs, send_sems, recv_sems, core_of_block, chips):
    x, y, c = _mesh_position()
    return [_gather_copy(_BLOCK_OF[a], full_refs[a], _block_id(*_other_chips(x, y)[j], core_of_block),
                         send_sems.at[a * N_OTHER_CHIPS + j], recv_sems.at[a * N_OTHER_CHIPS + j], (x, y, 1 - c))
            for a in range(N_GATHERED) for j in chips]


def _relayed_copies(full_refs, send_sems, recv_sems):
    x, y, c = _mesh_position()
    source, to = (x ^ (1 - c), y ^ c), (x ^ c, y ^ (1 - c))
    return [_gather_copy(_BLOCK_OF[a], full_refs[a], _block_id(*source, c), send_sems.at[a], recv_sems.at[a], (*to, c))
            for a in range(N_GATHERED)]


def _gather_weights_pass_on(name, chips, arrival_sems, arrival_sem_of, landing, after, relay=False):
    n = N_GATHERED * N_OTHER_CHIPS
    sem_shapes = [pltpu.SemaphoreType.DMA((n,))] * 2 + [pltpu.SemaphoreType.DMA((N_GATHERED,))] * (2 if relay else 0)

    def body(win_ref, wout_ref, arrivals, *rest):
        sems = rest[len(after):len(after) + len(sem_shapes)]
        x, y, c = _mesh_position()
        full_refs = (win_ref, wout_ref)
        passed = _passed_on_copies(full_refs, sems[0], sems[1], c, chips)
        relayed = _relayed_copies(full_refs, sems[2], sems[3]) if relay else []
        for a in range(N_GATHERED):
            for j in chips:
                sem = arrivals.at[arrival_sem_of(a, j)]
                _gather_copy(_BLOCK_OF[a], full_refs[a], _block_id(*_other_chips(x, y)[j], c), sem, sem,
                             (x, y, c)).wait_recv()
            for cp in relayed[a:a + 1] + passed[a * len(chips):(a + 1) * len(chips)]:
                cp.start()

    outs = pl.pallas_call(
        body, name=name, in_specs=[HBM] * N_GATHERED + [_SEM] + [HBM] * len(after),
        out_specs=[_SEM] * len(sem_shapes) + [HBM] * N_GATHERED,
        out_shape=sem_shapes + [pltpu.HBM(a.shape, a.dtype) for a in landing],
        input_output_aliases={a: len(sem_shapes) + a for a in range(N_GATHERED)},
        compiler_params=_params(has_side_effects=_DATAFLOW),
    )(*landing, arrival_sems, *after)
    return outs[:len(sem_shapes)], outs[len(sem_shapes):]


def _gather_weights_finish(layer, index, first_sems, passed, shards, landing, relay_send_sems=None):
    relayed = relay_send_sems is not None
    passed_sems = [sem for (send, recv), _ in passed for sem in (send, recv)] + ([relay_send_sems] if relayed else [])

    def body(win_ref, wout_ref, first_send, first_recv, local_sems, *rest):
        sems, (win_sh, wout_sh) = rest[:len(passed_sems)], rest[len(passed_sems):len(passed_sems) + 2]
        x, y, c = _mesh_position()
        full_refs = (win_ref, wout_ref)
        own, sent = _first_copies(index, (win_sh, wout_sh), full_refs, first_send, first_recv, local_sems, relayed)
        for a in range(N_GATHERED):
            sem = _first_sem(index, a, 0)
            _gather_copy(_BLOCK_OF[a], full_refs[a], _block_id(x, y, 1 - c), first_recv.at[sem], first_recv.at[sem],
                         (x, y, c)).wait_recv()
        for p, (_, chips) in enumerate(passed):
            for cp in _passed_on_copies(full_refs, sems[2 * p], sems[2 * p + 1], 1 - c, chips):
                cp.wait_recv()
            sent += _passed_on_copies(full_refs, sems[2 * p], sems[2 * p + 1], c, chips)
        if relayed:
            sent += _relayed_copies(full_refs, sems[-1], sems[-1])
        for cp in sent:
            cp.wait_send()
        for cp in own:
            cp.wait()

    return pl.pallas_call(
        body, name=f"all_gather_weights_finish_{layer}",
        in_specs=[HBM] * N_GATHERED + [_SEM] * (3 + len(passed_sems)) + [HBM] * 2,
        out_specs=[HBM] * N_GATHERED, out_shape=[pltpu.HBM(a.shape, a.dtype) for a in landing],
        input_output_aliases={a: a for a in range(N_GATHERED)}, compiler_params=_params(has_side_effects=_DATAFLOW),
    )(*landing, *first_sems, *passed_sems, *shards)


ALL_KINDS = (0, 1, 2)


def _sibling_exchange(grads, kinds):
    n_arr = len(grads)

    def make(in_refs, out_refs, sems):
        send_sems, recv_sems = sems
        x, y, c = _mesh_position()
        copies = [pltpu.make_async_remote_copy(
            src_ref=_BLOCK_OF[kinds[a]](in_refs[a], 2 * q + (1 - c)), dst_ref=out_refs[a].at[q],
            send_sem=send_sems.at[a * N_CHIP + q], recv_sem=recv_sems.at[a * N_CHIP + q],
            device_id=(x, y, 1 - c), device_id_type=MESH)
            for a in range(n_arr) for q in range(N_CHIP)]

        def start():
            for cp in copies:
                cp.start()

        def finish():
            for cp in copies:
                cp.wait_recv()
            for cp in copies:
                cp.wait_send()

        return start, finish

    return _Exchange(
        grads, [jax.ShapeDtypeStruct((N_CHIP,) + _BLOCK_SHAPES[k], BF16) for k in kinds], {},
        [pltpu.SemaphoreType.DMA((n_arr * N_CHIP,)), pltpu.SemaphoreType.DMA((n_arr * N_CHIP,))], make)


def _chips_exchange(layer, partials, received, kinds):
    n_arr = len(partials)

    def make(in_refs, out_refs, sems):
        send_sems, recv_sems = sems
        x, y, c = _mesh_position()
        copies = [pltpu.make_async_remote_copy(
            src_ref=in_refs[a].at[2 * qx + qy, layer], dst_ref=out_refs[a].at[j, layer],
            send_sem=send_sems.at[a * N_OTHER_CHIPS + j], recv_sem=recv_sems.at[a * N_OTHER_CHIPS + j],
            device_id=(qx, qy, c), device_id_type=MESH)
            for a in range(n_arr) for j, (qx, qy) in enumerate(_other_chips(x, y))]

        def start():
            for cp in copies:
                cp.start()

        def finish():
            for cp in copies:
                cp.wait_recv()
            for cp in copies:
                cp.wait_send()

        return start, finish

    inputs = list(partials)
    aliases = {}
    if received is not None:
        inputs += list(received)
        aliases = {n_arr + a: a for a in range(n_arr)}
    return _Exchange(
        inputs, [jax.ShapeDtypeStruct((N_OTHER_CHIPS, DEPTH) + _BLOCK_SHAPES[k], BF16) for k in kinds], aliases,
        [pltpu.SemaphoreType.DMA((n_arr * N_OTHER_CHIPS,)), pltpu.SemaphoreType.DMA((n_arr * N_OTHER_CHIPS,))], make)


def _all_gather_exchange(v, axis=0):
    def make(in_refs, out_refs, sems):
        send_sems, recv_sems, local_sem = sems
        x, y, c = _mesh_position()
        me = _block_id(x, y, c)
        block = lambda blk: out_refs[0].at[(slice(None),) * axis + (blk,)]
        own = pltpu.make_async_copy(in_refs[0], block(me), local_sem.at[0])
        sends, arrivals = [], []
        for k in range(1, N_DEV):
            px, py, pc = x ^ ((k >> 2) & 1), y ^ ((k >> 1) & 1), c ^ (k & 1)
            sends.append(pltpu.make_async_remote_copy(
                src_ref=in_refs[0], dst_ref=block(me), send_sem=send_sems.at[k - 1],
                recv_sem=recv_sems.at[k - 1], device_id=(px, py, pc), device_id_type=MESH))
            arrivals.append(pltpu.make_async_remote_copy(
                src_ref=in_refs[0], dst_ref=block(_block_id(px, py, pc)), send_sem=send_sems.at[k - 1],
                recv_sem=recv_sems.at[k - 1], device_id=(x, y, c), device_id_type=MESH))

        def start():
            for cp in [own] + sends:
                cp.start()

        def finish():
            for cp in arrivals:
                cp.wait_recv()
            for cp in sends:
                cp.wait_send()
            own.wait()

        return start, finish

    return _Exchange(
        [v], [jax.ShapeDtypeStruct(v.shape[:axis] + (N_DEV,) + v.shape[axis:], v.dtype)], {},
        [pltpu.SemaphoreType.DMA((N_DEV - 1,)), pltpu.SemaphoreType.DMA((N_DEV - 1,)),
         pltpu.SemaphoreType.DMA((1,))], make)


def _host(exchange, args, in_specs, out_shape, out_specs, scratch):
    n_own = (len(args), len(out_shape), len(scratch))
    if exchange is None:
        return {}, lambda refs: (refs, None)
    n_ex = (len(exchange.inputs), len(exchange.out_shapes), len(exchange.sem_shapes))
    aliases = {n_own[0] + i: n_own[1] + o for i, o in exchange.aliases.items()}
    args += exchange.inputs
    in_specs += [HBM] * n_ex[0]
    out_shape += exchange.out_shapes
    out_specs += [HBM] * n_ex[1]
    scratch += exchange.sem_shapes

    def split(refs):
        own, theirs, at = [], [], 0
        for mine, ex in zip(n_own, n_ex):
            own += refs[at:at + mine]
            theirs.append(refs[at + mine:at + mine + ex])
            at += mine + ex
        return own, exchange.make(*theirs)

    return aliases, split


def _all_gather_small(v, name, after=()):
    vmem = pl.BlockSpec(memory_space=pltpu.VMEM)

    def body(v_ref, *rest):
        out_ref, send_sems, recv_sems = rest[len(after):]
        x, y, c = _mesh_position()
        me = _block_id(x, y, c)
        out_ref[me] = v_ref[...]
        sends = []
        for k in range(1, N_DEV):
            px, py, pc = x ^ ((k >> 2) & 1), y ^ ((k >> 1) & 1), c ^ (k & 1)
            send = pltpu.make_async_remote_copy(
                src_ref=v_ref, dst_ref=out_ref.at[me], send_sem=send_sems.at[k - 1], recv_sem=recv_sems.at[k - 1],
                device_id=(px, py, pc), device_id_type=MESH)
            send.start()
            sends.append((send, _block_id(px, py, pc)))
        for k, (send, peer) in enumerate(sends):
            pltpu.make_async_remote_copy(
                src_ref=v_ref, dst_ref=out_ref.at[peer], send_sem=send_sems.at[k], recv_sem=recv_sems.at[k],
                device_id=(x, y, c), device_id_type=MESH).wait_recv()
        for send, _ in sends:
            send.wait_send()

    return pl.pallas_call(
        body, name=name, in_specs=[vmem] + [HBM] * len(after), out_specs=vmem,
        out_shape=jax.ShapeDtypeStruct((N_DEV,) + v.shape, v.dtype),
        scratch_shapes=[pltpu.SemaphoreType.DMA((N_DEV - 1,)), pltpu.SemaphoreType.DMA((N_DEV - 1,))],
        compiler_params=_params(),
    )(v, *after)


def _forward_layer(layer, x, vec, cps, wpool, win, wout, target=None):
    t_len = x.shape[0]
    n_tiles = t_len // ROW_TILE
    row = lambda cols: pl.BlockSpec((ROW_TILE, cols), lambda i: (i, 0))
    widths = (D_MODEL, 2 * CONV_W, 3 * CONV_W, 4 * POOL_W)
    head = target is not None

    def body(x_ref, vec_ref, cps_ref, wpool_ref, win_ref, wout_ref, *rest):
        target_ref = rest[0] if head else None
        xo_ref, y_ref, h_ref, ycat_ref, uc_ref, fa_ref, fp_ref = rest[head:head + 7]
        loss_ref = rest[head + 7] if head else None
        zc_ref, pc_ref = rest[-2:]
        i = pl.program_id(0)

        @pl.when(i == 0)
        def _():
            zc_ref[...] = jnp.zeros_like(zc_ref)
            pc_ref[...] = jnp.zeros_like(pc_ref)
            if head:
                loss_ref[...] = jnp.zeros_like(loss_ref)

        x_t = x_ref[...]
        shift, scale, gate = vec_ref[0:1, :], vec_ref[1:2, :], vec_ref[2:3, :]
        g_pre, g_post = vec_ref[3:4, :], vec_ref[4:5, :]
        w0, w1, w2, ps = cps_ref[0:1, :], cps_ref[1:2, :], cps_ref[2:3, :], cps_ref[3:4, :]
        rx = lax.rsqrt(jnp.mean(x_t * x_t, axis=-1, keepdims=True) + NORM_EPS)
        h = (x_t * rx) * g_pre * (1.0 + scale) + shift
        h_ref[...] = h.astype(BF16)
        proj = _dot(h.astype(BF16), win_ref[...])
        u_a, b_a, c_a, g_a, u_p, g_p = _split_proj(proj)
        uc_ref[...] = jnp.concatenate([u_a, c_a], axis=1).astype(BF16)

        z = c_a * u_a
        zcat = jnp.concatenate([zc_ref[...], z], axis=0)
        zc_ref[...] = z[ROW_TILE - CONV_HALO:]
        conv = (w0 * _rows_from_before(zcat, 2)[CONV_HALO:] + w1 * _rows_from_before(zcat, 1)[CONV_HALO:] + w2 * z)
        sig_a = _sigmoid(g_a)
        silu_a = g_a * sig_a
        b_conv = b_a * conv
        y_a = b_conv * silu_a
        fa_ref[...] = jnp.concatenate(
            [silu_a * conv, silu_a * b_a, b_conv * (sig_a + silu_a * (1.0 - sig_a))], axis=1).astype(BF16)

        pcat = jnp.concatenate([pc_ref[...], u_p], axis=0)
        pc_ref[...] = u_p[ROW_TILE - POOL_HALO:]
        counts = _window_counts(i * ROW_TILE, ROW_TILE)
        pooled, mixed = [], []
        for g, w in enumerate(POOL_WINDOWS):
            cols = slice(g * GROUP_D, (g + 1) * GROUP_D)
            s = pcat[:, cols]
            step = 1
            while step < w:
                s = s + _rows_from_before(s, step)
                step *= 2
            pooled_g = (s[POOL_HALO:] * (1.0 / jnp.minimum(counts, float(w))) - u_p[:, cols]).astype(BF16)
            pooled.append(pooled_g)
            mixed.append(_dot(pooled_g, wpool_ref[g]))
        mixed = jnp.concatenate(mixed, axis=1)
        sig_p = _sigmoid(g_p)
        silu_p = g_p * sig_p
        mixed_ps = mixed * ps
        y_p = mixed_ps * silu_p
        fp_ref[...] = jnp.concatenate(
            [(ps * silu_p).astype(BF16), (mixed_ps * (sig_p + silu_p * (1.0 - sig_p))).astype(BF16),
             (silu_p * mixed).astype(BF16)] + pooled, axis=1)

        ycat = jnp.concatenate([y_a, y_p], axis=1)
        ycat_ref[...] = ycat.astype(BF16)
        y_b = _dot(ycat.astype(BF16), wout_ref[...]).astype(BF16)
        y_ref[...] = y_b
        y_t = y_b.astype(F32)
        ry = lax.rsqrt(jnp.mean(y_t * y_t, axis=-1, keepdims=True) + NORM_EPS)
        x_next = x_t + gate * (y_t * ry * g_post)
        if head:
            err = x_next - target_ref[...]
            xo_ref[...] = err * (1.0 / D_MODEL)
            loss_ref[...] += jnp.sum(err * err) * (0.5 / D_MODEL)
        else:
            xo_ref[...] = x_next

    tile = (SUBLANES, LANES)
    return pl.pallas_call(
        body, name=f"forward_layer_{layer}", grid=(n_tiles,),
        in_specs=[row(D_MODEL), _layer_spec(vec.shape, layer), _layer_spec(cps.shape, layer),
                  _layer_spec(wpool.shape, layer), _whole_spec(win.shape), _whole_spec(wout.shape)]
        + [row(D_MODEL)] * head,
        out_specs=[row(D_MODEL), row(D_MODEL), row(D_MODEL), row(D_MODEL)] + [row(w) for w in widths[1:]]
        + [_whole_spec(tile)] * head,
        out_shape=[jax.ShapeDtypeStruct((t_len, D_MODEL), F32), jax.ShapeDtypeStruct((t_len, D_MODEL), BF16),
                   jax.ShapeDtypeStruct((t_len, D_MODEL), BF16), jax.ShapeDtypeStruct((t_len, D_MODEL), BF16)]
        + [jax.ShapeDtypeStruct((t_len, w), BF16) for w in widths[1:]] + [jax.ShapeDtypeStruct(tile, F32)] * head,
        scratch_shapes=[pltpu.VMEM((CONV_HALO, CONV_W), F32), pltpu.VMEM((POOL_HALO, POOL_W), F32)],
        compiler_params=_params(dimension_semantics=("arbitrary",)),
    )(x, vec, cps, wpool, win, wout, *([target] * head))


def _backward_layer(layer, dxo, y, x, uc, fa, fp, vec, cps, wpool, wout, win, after):
    t_len = dxo.shape[0]
    n_tiles = t_len // BWD_TILE
    halo_per_tile = BWD_TILE // POOL_HALO
    rev = lambda cols: pl.BlockSpec((BWD_TILE, cols), lambda i: (n_tiles - 1 - i, 0))
    halo_spec = pl.BlockSpec(
        (POOL_HALO, 2 * CONV_W), lambda i: (jnp.maximum((n_tiles - 1 - i) * halo_per_tile - 1, 0), 0))
    gwpool_shape = (len(POOL_WINDOWS), GROUP_D, GROUP_D)

    def body(dxo_ref, y_ref, x_ref, uc_ref, uch_ref, fa_ref, fp_ref, vec_ref, cps_ref, wpool_ref, wout_ref, win_ref,
             *rest):
        dx_ref, dproj_ref, dy_ref, gwpool_ref, dcps_ref, dvec_ref, gwpool_acc, dcc_ref, qc_ref = rest[len(after):]
        i = pl.program_id(0)
        tile = n_tiles - 1 - i

        @pl.when(i == 0)
        def _():
            gwpool_acc[...] = jnp.zeros_like(gwpool_acc)
            dcps_ref[...] = jnp.zeros_like(dcps_ref)
            dvec_ref[...] = jnp.zeros_like(dvec_ref)
            dcc_ref[...] = jnp.zeros_like(dcc_ref)
            qc_ref[...] = jnp.zeros_like(qc_ref)

        shift, scale, gate = vec_ref[0:1, :], vec_ref[1:2, :], vec_ref[2:3, :]
        g_pre, g_post = vec_ref[3:4, :], vec_ref[4:5, :]
        w0, w1, w2 = cps_ref[0:1, :], cps_ref[1:2, :], cps_ref[2:3, :]

        dxo_t = dxo_ref[...]
        y_t = y_ref[...].astype(F32)
        ry = lax.rsqrt(jnp.mean(y_t * y_t, axis=-1, keepdims=True) + NORM_EPS)
        yh = y_t * ry
        dvec_ref[2:3, :] += jnp.sum(dxo_t * yh, axis=0, keepdims=True)
        dyh = dxo_t * (gate * g_post)
        dy_b = (ry * (dyh - yh * jnp.mean(dyh * yh, axis=-1, keepdims=True))).astype(BF16)
        dy_ref[...] = dy_b
        dycat = _dot_nt(dy_b, wout_ref[...])
        dy_a, dy_p = dycat[:, :CONV_W], dycat[:, CONV_W:]

        fa_t = fa_ref[...].astype(F32)
        db_a = dy_a * fa_t[:, :CONV_W]
        dconv = dy_a * fa_t[:, CONV_W:2 * CONV_W]
        dg_a = dy_a * fa_t[:, 2 * CONV_W:]
        uc_t = uc_ref[...].astype(F32)
        u_a, c_a = uc_t[:, :CONV_W], uc_t[:, CONV_W:]
        halo = jnp.where(tile > 0, uch_ref[...].astype(F32), 0.0)[POOL_HALO - CONV_HALO:]
        z = c_a * u_a
        zcat = jnp.concatenate([halo[:, CONV_W:] * halo[:, :CONV_W], z], axis=0)
        z1 = _rows_from_before(zcat, 1)[CONV_HALO:]
        z2 = _rows_from_before(zcat, 2)[CONV_HALO:]
        dccat = jnp.concatenate([dconv, dcc_ref[...]], axis=0)
        dc1 = _rows_from_after(dccat, 1)[:BWD_TILE]
        dc2 = _rows_from_after(dccat, 2)[:BWD_TILE]
        dz = w2 * dconv + w1 * dc1 + w0 * dc2
        dcc_ref[...] = dconv[:CONV_HALO]
        dcps_ref[0:1, :] += jnp.sum(dconv * z2, axis=0, keepdims=True)
        dcps_ref[1:2, :] += jnp.sum(dconv * z1, axis=0, keepdims=True)
        dcps_ref[2:3, :] += jnp.sum(dconv * z, axis=0, keepdims=True)
        du_a = dz * c_a
        dc_a = dz * u_a

        dmixed = (dy_p * fp_ref[:, :POOL_W].astype(F32)).astype(BF16)
        dg_p = dy_p * fp_ref[:, POOL_W:2 * POOL_W].astype(F32)
        dcps_ref[3:4, :] += jnp.sum(dy_p * fp_ref[:, 2 * POOL_W:3 * POOL_W].astype(F32), axis=0, keepdims=True)
        counts = _window_counts(tile * BWD_TILE, BWD_TILE)
        du_p, q_head = [], []
        for g, w in enumerate(POOL_WINDOWS):
            cols = slice(g * GROUP_D, (g + 1) * GROUP_D)
            dm_g = dmixed[:, cols]
            dpooled_g = _dot_nt(dm_g, wpool_ref[g])
            gwpool_acc[g] += _dot_tn(fp_ref[:, 3 * POOL_W + g * GROUP_D:3 * POOL_W + (g + 1) * GROUP_D], dm_g)
            q_g = dpooled_g * (1.0 / jnp.minimum(counts, float(w)))
            q_head.append(q_g[:POOL_HALO])
            s = jnp.concatenate([q_g, qc_ref[:, cols]], axis=0)
            step = 1
            while step < w:
                s = s + _rows_from_after(s, step)
                step *= 2
            du_p.append(s[:BWD_TILE] - dpooled_g)
        qc_ref[...] = jnp.concatenate(q_head, axis=1)
        dproj_b = jnp.concatenate([du_a, db_a, dc_a, dg_a] + du_p + [dg_p], axis=1).astype(BF16)
        dproj_ref[...] = dproj_b

        x_t = x_ref[...]
        rx = lax.rsqrt(jnp.mean(x_t * x_t, axis=-1, keepdims=True) + NORM_EPS)
        xn = x_t * rx
        mod_scale = 1.0 + scale
        dh = _dot_nt(dproj_b, win_ref[...])
        dvec_ref[0:1, :] += jnp.sum(dh, axis=0, keepdims=True)
        dvec_ref[1:2, :] += jnp.sum(dh * xn, axis=0, keepdims=True)
        dxn = dh * (g_pre * mod_scale)
        dx_ref[...] = dxo_t + rx * (dxn - xn * jnp.mean(dxn * xn, axis=-1, keepdims=True))

        @pl.when(i == n_tiles - 1)
        def _():
            gwpool_ref[...] = gwpool_acc[...].astype(BF16)
            sum_dh_xn, sum_dxo_yh = dvec_ref[1:2, :], dvec_ref[2:3, :]
            dvec_ref[1:2, :] = sum_dh_xn * g_pre
            dvec_ref[3:4, :] = sum_dh_xn * mod_scale
            dvec_ref[2:3, :] = sum_dxo_yh * g_post
            dvec_ref[4:5, :] = sum_dxo_yh * gate

    return pl.pallas_call(
        body, name=f"backward_layer_{layer}", grid=(n_tiles,),
        in_specs=[rev(D_MODEL), rev(D_MODEL), rev(D_MODEL), rev(2 * CONV_W), halo_spec, rev(3 * CONV_W),
                  rev(4 * POOL_W), _layer_spec(vec.shape, layer), _layer_spec(cps.shape, layer),
                  _layer_spec(wpool.shape, layer), _whole_spec(wout.shape), _whole_spec(win.shape)]
        + [HBM] * len(after),
        out_specs=[rev(D_MODEL), rev(IN_COLS), rev(D_MODEL), _whole_spec(gwpool_shape),
                   _whole_spec((SUBLANES, CONV_W)), _whole_spec((SUBLANES, D_MODEL))],
        out_shape=[jax.ShapeDtypeStruct((t_len, D_MODEL), F32), jax.ShapeDtypeStruct((t_len, IN_COLS), BF16),
                   jax.ShapeDtypeStruct((t_len, D_MODEL), BF16), jax.ShapeDtypeStruct(gwpool_shape, BF16),
                   jax.ShapeDtypeStruct((SUBLANES, CONV_W), F32), jax.ShapeDtypeStruct((SUBLANES, D_MODEL), F32)],
        scratch_shapes=[pltpu.VMEM(gwpool_shape, F32), pltpu.VMEM((CONV_HALO, CONV_W), F32),
                        pltpu.VMEM((POOL_HALO, POOL_W), F32)],
        compiler_params=_params(dimension_semantics=("arbitrary",)),
    )(dxo, y, x, uc, uc, fa, fp, vec, cps, wpool, wout, win, *after)


def _weight_grads(layer, h, dproj, ycat, dy, exchange, after=()):
    t_len = dy.shape[0]
    n_in, n_out = IN_COLS // GWIN_COLS, D_MODEL // GWOUT_COLS
    args = [h, dproj, ycat, dy, *after]
    in_specs = [_whole_spec(h.shape),
                pl.BlockSpec((t_len, GWIN_COLS), lambda s: (0, jnp.minimum(s, n_in - 1))),
                _whole_spec(ycat.shape),
                pl.BlockSpec((t_len, GWOUT_COLS), lambda s: (0, jnp.maximum(s - n_in, 0)))] + [HBM] * len(after)
    out_shape = [jax.ShapeDtypeStruct((D_MODEL, IN_COLS), BF16), jax.ShapeDtypeStruct((D_MODEL, D_MODEL), BF16)]
    out_specs = [pl.BlockSpec((D_MODEL, GWIN_COLS), lambda s: (0, jnp.minimum(s, n_in - 1))),
                 pl.BlockSpec((D_MODEL, GWOUT_COLS), lambda s: (0, jnp.maximum(s - n_in, 0)))]
    scratch = []
    aliases, split = _host(exchange, args, in_specs, out_shape, out_specs, scratch)

    def body(*refs):
        (h_ref, dproj_ref, ycat_ref, dy_ref, *_, gwin_ref, gwout_ref), hosted = split(refs)
        s = pl.program_id(0)
        if hosted is not None:
            pl.when(s == 0)(hosted[0])

        @pl.when(s < n_in)
        def _():
            gwin_ref[...] = _dot_tn(h_ref[...], dproj_ref[...]).astype(BF16)

        @pl.when(s >= n_in)
        def _():
            gwout_ref[...] = _dot_tn(ycat_ref[...], dy_ref[...]).astype(BF16)

        if hosted is not None:
            pl.when(s == n_in + n_out - 1)(hosted[1])

    return pl.pallas_call(
        body, name=f"weight_grads_{layer}", grid=(n_in + n_out,), in_specs=in_specs, out_specs=out_specs,
        out_shape=out_shape, scratch_shapes=scratch, input_output_aliases=aliases,
        compiler_params=_params(dimension_semantics=("arbitrary",)),
    )(*args)


def _add_sibling_blocks(name, layer, grads, received, core, partials, kinds):
    n_arr = len(grads)

    def body(core_ref, *refs):
        mine, theirs, outs = refs[:n_arr], refs[n_arr:2 * n_arr], refs[-n_arr:]
        for a in range(n_arr):
            outs[a][...] = (mine[a][...].astype(F32) + theirs[a][...].astype(F32)).astype(BF16)

    own_of_kind = [
        pl.BlockSpec((D_MODEL, W_IN_SHARD), lambda q, core_ref: (0, 2 * q + core_ref[0])),
        pl.BlockSpec((W_OUT_SHARD, D_MODEL), lambda q, core_ref: (2 * q + core_ref[0], 0)),
        pl.BlockSpec((POOL_SHARD, GROUP_D), lambda q, core_ref: (2 * q + core_ref[0], 0)),
    ]
    shapes = [_BLOCK_SHAPES[k] for k in kinds]
    recv_specs = [pl.BlockSpec((None,) + s, lambda q, core_ref: (q, 0, 0)) for s in shapes]
    out_specs = [pl.BlockSpec((None, None) + s, lambda q, core_ref: (q, layer, 0, 0)) for s in shapes]
    args = [core, *grads, *received]
    in_specs = [own_of_kind[k] for k in kinds] + recv_specs
    aliases = {}
    if partials is not None:
        aliases = {len(args) + a: a for a in range(n_arr)}
        args += list(partials)
        in_specs += [HBM] * n_arr
    return pl.pallas_call(
        body, name=name,
        grid_spec=pltpu.PrefetchScalarGridSpec(
            num_scalar_prefetch=1, grid=(N_CHIP,), in_specs=in_specs, out_specs=out_specs),
        out_shape=[jax.ShapeDtypeStruct((N_CHIP, DEPTH) + s, BF16) for s in shapes],
        input_output_aliases=aliases,
        compiler_params=_params(dimension_semantics=("arbitrary",)),
    )(*args)


def _modulation_columns(c_all, w_ada):
    def body(c_ref, w_ref, cact_ref, out_ref):
        c_t = c_ref[...]
        c_act = c_t * _sigmoid(c_t)
        cact_ref[...] = c_act
        out_ref[...] = jnp.dot(c_act, w_ref[...], preferred_element_type=F32, precision=lax.Precision.HIGHEST)

    return pl.pallas_call(
        body, name="modulation_columns", grid=(DEPTH,),
        in_specs=[pl.BlockSpec((N_DEV, D_MODEL), lambda l: (0, 0)),
                  pl.BlockSpec((None, D_MODEL, W_IN_SHARD), lambda l: (l, 0, 0))],
        out_specs=[pl.BlockSpec((N_DEV, D_MODEL), lambda l: (0, 0)),
                   pl.BlockSpec((N_DEV, W_IN_SHARD), lambda l: (0, l))],
        out_shape=[jax.ShapeDtypeStruct((N_DEV, D_MODEL), F32),
                   jax.ShapeDtypeStruct((N_DEV, DEPTH * W_IN_SHARD), F32)],
        compiler_params=_params(dimension_semantics=("arbitrary",)),
    )(c_all, w_ada)


def _adamw(w, g, m, v):
    m_new = ADAM_B1 * m + (1.0 - ADAM_B1) * g
    v_new = ADAM_B2 * v + (1.0 - ADAM_B2) * (g * g)
    m_hat = m_new / (1.0 - ADAM_B1 ** ADAM_STEP)
    v_hat = v_new / (1.0 - ADAM_B2 ** ADAM_STEP)
    delta = -ADAM_LR * (m_hat / (jnp.sqrt(v_hat) + ADAM_EPS) + ADAM_WD * w)
    return delta, m_new, v_new


def _adamw_w_ada(w, m, v, c_act_t, dmod_cols):
    def body(w_ref, m_ref, v_ref, ct_ref, dm_ref, g_ref, d_ref, mo_ref, vo_ref):
        g = ct_ref[:, 0:1] * dm_ref[0:1, :]
        for b in range(1, N_DEV):
            g = g + ct_ref[:, b:b + 1] * dm_ref[b:b + 1, :]
        g_ref[...] = g
        d_ref[...], mo_ref[...], vo_ref[...] = _adamw(w_ref[...], g, m_ref[...], v_ref[...])

    big = pl.BlockSpec((None, D_MODEL, W_IN_SHARD), lambda l: (l, 0, 0))
    return pl.pallas_call(
        body, name="adamw_w_ada", grid=(DEPTH,),
        in_specs=[big, big, big, pl.BlockSpec((D_MODEL, N_DEV), lambda l: (0, 0)),
                  pl.BlockSpec((None, N_DEV, W_IN_SHARD), lambda l: (l, 0, 0))],
        out_specs=[big] * 4, out_shape=[jax.ShapeDtypeStruct(w.shape, F32)] * 4,
        compiler_params=_params(dimension_semantics=("arbitrary",)),
    )(w, m, v, c_act_t, dmod_cols)


def _sum_chip_partials(own_ref, recv_ref):
    g = own_ref[...].astype(F32)
    for j in range(N_OTHER_CHIPS):
        g = g + recv_ref[j].astype(F32)
    return g


def _partial_specs(row_tile, cols, first_layer=0):
    own = pl.BlockSpec((None, None, row_tile, cols), lambda l, r, chip_ref: (chip_ref[0], first_layer + l, r, 0))
    recv = pl.BlockSpec((N_OTHER_CHIPS, None, row_tile, cols), lambda l, r, chip_ref: (0, first_layer + l, r, 0))
    return own, recv


def _adamw_reduced(name, w, m, v, partial, received, chip, row_tile, layers, continued, after=()):
    depth, rows, cols = w.shape
    first, stop = layers

    def body(chip_ref, w_ref, m_ref, v_ref, own_ref, recv_ref, *rest):
        g_ref, d_ref, mo_ref, vo_ref = rest[-4:]
        g = _sum_chip_partials(own_ref, recv_ref)
        g_ref[...] = g
        d_ref[...], mo_ref[...], vo_ref[...] = _adamw(w_ref[...], g, m_ref[...], v_ref[...])

    blk = pl.BlockSpec((None, row_tile, cols), lambda l, r, chip_ref: (first + l, r, 0))
    args = [chip, w, m, v, partial, received]
    in_specs = [blk, blk, blk, *_partial_specs(row_tile, cols, first)]
    aliases = {}
    if continued is not None:
        aliases = {len(args) + k: k for k in range(4)}
        args += list(continued)
        in_specs += [HBM] * 4
    args += after
    in_specs += [HBM] * len(after)
    return pl.pallas_call(
        body, name=name,
        grid_spec=pltpu.PrefetchScalarGridSpec(
            num_scalar_prefetch=1, grid=(stop - first, rows // row_tile), in_specs=in_specs, out_specs=[blk] * 4),
        out_shape=[jax.ShapeDtypeStruct(w.shape, F32)] * 4, input_output_aliases=aliases,
        compiler_params=_params(dimension_semantics=("arbitrary", "arbitrary")),
    )(*args)


def _reduce_w_pool(partial, received, chip):
    def body(chip_ref, own_ref, recv_ref, g_ref):
        g_ref[...] = _sum_chip_partials(own_ref, recv_ref)

    return pl.pallas_call(
        body, name="reduce_w_pool",
        grid_spec=pltpu.PrefetchScalarGridSpec(
            num_scalar_prefetch=1, grid=(DEPTH, 1), in_specs=list(_partial_specs(POOL_SHARD, GROUP_D)),
            out_specs=pl.BlockSpec((None, POOL_SHARD, GROUP_D), lambda l, r, chip_ref: (l, 0, 0))),
        out_shape=jax.ShapeDtypeStruct((DEPTH, POOL_SHARD, GROUP_D), F32),
        compiler_params=_params(dimension_semantics=("arbitrary", "arbitrary")),
    )(chip, partial, received)


def _adamw_small(name, params, copy_gradients=False):
    n, n_out = len(params), 4 if copy_gradients else 3

    def body(*refs):
        ins, outs = refs[:4 * n], refs[4 * n:]
        for p in range(n):
            w_ref, g_ref, m_ref, v_ref = ins[4 * p:4 * p + 4]
            d_ref, mo_ref, vo_ref = outs[n_out * p:n_out * p + 3]
            d_ref[...], mo_ref[...], vo_ref[...] = _adamw(w_ref[...], g_ref[...], m_ref[...], v_ref[...])
            if copy_gradients:
                outs[n_out * p + 3][...] = g_ref[...]

    vmem = pl.BlockSpec(memory_space=pltpu.VMEM)
    flat = [a for group in params for a in group]
    out_shape = [jax.ShapeDtypeStruct(group[0].shape, F32) for group in params for _ in range(n_out)]
    outs = pl.pallas_call(
        body, name=name, in_specs=[vmem] * len(flat), out_specs=[vmem] * len(out_shape),
        out_shape=out_shape, compiler_params=_params(),
    )(*flat)
    return [tuple(outs[n_out * p:n_out * p + n_out]) for p in range(n)]


def _sum_sources(slabs, columns, after):
    def body(s_ref, *rest):
        outs = rest[len(after):]
        acc = s_ref[0]
        for b in range(1, N_DEV):
            acc = acc + s_ref[b]
        for (start, stop), o_ref in zip(columns, outs):
            o_ref[...] = acc[:, start:stop]
        outs[-1][...] = acc[0:1, SLAB_COLS:SLAB_COLS + 1]

    vmem = pl.BlockSpec(memory_space=pltpu.VMEM)
    out_shape = [jax.ShapeDtypeStruct((slabs.shape[1], stop - start), F32) for start, stop in columns]
    out_shape.append(jax.ShapeDtypeStruct((1, 1), F32))
    return pl.pallas_call(
        body, name="sum_small_grads", in_specs=[vmem] + [HBM] * len(after), out_specs=[vmem] * len(out_shape),
        out_shape=out_shape, compiler_params=_params(),
    )(slabs, *after)


def _to_bf16(a, name, layers=None):
    first, stop = layers or (0, a.shape[0])

    def body(a_ref, o_ref):
        o_ref[...] = a_ref[...].astype(BF16)

    block = (None,) + a.shape[1:]
    return pl.pallas_call(
        body, name=name, grid=(stop - first,), in_specs=[pl.BlockSpec(block, lambda l: (first + l, 0, 0))],
        out_specs=pl.BlockSpec(block, lambda l: (l, 0, 0)),
        out_shape=jax.ShapeDtypeStruct((stop - first,) + a.shape[1:], BF16),
        compiler_params=_params(dimension_semantics=("arbitrary",)),
    )(a)


def kernel(x, c, w_ada, b_ada, g_pre, w_in, w_conv, w_pool, pool_scale, w_out, g_post, loss_target, m_w_ada, m_b_ada, m_g_pre, m_w_in, m_w_conv, m_w_pool, m_pool_scale, m_w_out, m_g_post, v_w_ada, v_b_ada, v_g_pre, v_w_in, v_w_conv, v_w_pool, v_pool_scale, v_w_out, v_g_post):
    mx, my, mc = _mesh_position()
    me = _block_id(mx, my, mc)
    chip = (2 * mx + my).astype(jnp.int32).reshape(1)
    core = mc.astype(jnp.int32).reshape(1)
    x0 = x[0]
    target = loss_target[0]
    conv_shard = w_conv.shape[-1]

    own_small = jnp.concatenate([c, w_conv.reshape(1, DEPTH * 3 * conv_shard)], axis=1)
    first_shards = [_to_bf16(w_in, "cast_w_in_0", (0, 1)), _to_bf16(w_out, "cast_w_out_0", (0, 1))]
    all_small = _all_gather_small(own_small, "all_gather_c_w_conv", first_shards)[:, 0, :]
    gathers = [_gather_weights_start(0, *first_shards, [all_small], relayed=True)]
    c_all = all_small[:, :D_MODEL]
    w_conv_full = all_small[:, D_MODEL:].reshape(N_DEV, DEPTH, 3, conv_shard).transpose(1, 2, 0, 3).reshape(
        DEPTH, 3, CONV_W)
    cps = jnp.concatenate([w_conv_full, pool_scale[:, None], jnp.zeros((DEPTH, 4, CONV_W), F32)], axis=1)

    c_act, pieces = _modulation_columns(c_all, w_ada)
    upper = (1, DEPTH)
    upper_shards = [_to_bf16(w_in, "cast_w_in_1", upper), _to_bf16(w_out, "cast_w_out_1", upper)]
    wpool_b = _to_bf16(w_pool.reshape(DEPTH, POOL_ROWS, GROUP_D), "cast_w_pool").reshape(w_pool.shape)
    first_sems_0, _, (zones_0,) = gathers[0]
    neighbour_sems, zones_0 = _gather_weights_pass_on(
        "all_gather_weights_relay_0", NEIGHBOUR_CHIPS, first_sems_0[1], partial(_first_arrival, 0), zones_0,
        [pieces, *upper_shards, wpool_b], relay=True)
    mod_all = _all_gather_small(pieces, "all_gather_modulation", [zones_0[0]])
    mod_mine = lax.dynamic_index_in_dim(mod_all, me, axis=1, keepdims=False)
    mod = mod_mine.reshape(N_DEV, DEPTH, W_IN_SHARD).transpose(1, 0, 2).reshape(DEPTH, 3 * D_MODEL) + b_ada
    zeros_d = jnp.zeros((DEPTH, 3, D_MODEL), F32)
    vec = jnp.concatenate([mod.reshape(DEPTH, 3, D_MODEL), g_pre[:, None], g_post[:, None], zeros_d], axis=1)

    gathers.append(_gather_weights_start(1, *upper_shards, [mod_all]))
    gathers = [(first_sems, shards, landing[k], k) for first_sems, shards, landing in gathers
               for k in range(len(landing))]

    xs, kept, wins, wouts = [x0], [], [], []
    for l in range(DEPTH):
        first_sems, shards, zones, index = gathers[l]
        if l == 0:
            diagonal_sems, zones = _gather_weights_pass_on(
                "all_gather_weights_pass_on_0", DIAGONAL_CHIP, neighbour_sems[3], lambda a, j: a, zones_0,
                [vec, cps, gathers[-1][1][0]])
            passed = [(neighbour_sems[:2], NEIGHBOUR_CHIPS), (diagonal_sems, DIAGONAL_CHIP)]
            win, wout = _gather_weights_finish(l, index, first_sems, passed, shards, zones, neighbour_sems[2])
        else:
            passed_sems, zones = _gather_weights_pass_on(
                f"all_gather_weights_pass_on_{l}", ALL_OTHER_CHIPS, first_sems[1], partial(_first_arrival, index),
                zones, [xs[-1]])
            win, wout = _gather_weights_finish(l, index, first_sems, [(passed_sems, ALL_OTHER_CHIPS)], shards, zones)
        x_next, *for_backward = _forward_layer(
            l, xs[-1], vec, cps, wpool_b, win, wout, target if l == DEPTH - 1 else None)
        xs.append(x_next)
        kept.append(for_backward[:6])
        wins.append(win)
        wouts.append(wout)
    dx, loss_tile = xs[DEPTH], for_backward[6]

    slab_rows = [None] * DEPTH
    partials = received = None
    in_flight = []

    def scatter(layer, grads, from_sibling, after):
        nonlocal partials, received
        partials = _add_sibling_blocks(
            f"grad_add_sibling_{layer}", layer, grads, from_sibling, core, partials, ALL_KINDS)
        chips = _chips_exchange(layer, partials, received, ALL_KINDS)
        sems, partials, received, token = _start_exchange(chips, f"grad_chips_start_{layer}", after)
        in_flight.append((chips, sems, layer))
        return token

    grads_above = None
    issued = []
    for l in reversed(range(DEPTH)):
        y, h, ycat, uc, fa, fp = kept[l]
        dx, dproj, dy, gwpool, dcps, dvec = _backward_layer(
            l, dx, y, xs[l], uc, fa, fp, vec, cps, wpool_b, wouts[l], wins[l], issued)
        slab_rows[l] = jnp.concatenate(
            [dvec[0], dvec[1], dvec[2], dvec[3], dvec[4], dcps[3], dcps[0], dcps[1], dcps[2],
             loss_tile[0] if l == 0 else jnp.zeros((LANES,), F32)])
        after = []
        if l == 0:
            gather_small = _all_gather_exchange(jnp.stack(slab_rows))
            sems_s, slab, slabs, token = _start_exchange(gather_small, "all_gather_small_grads_start")
            after = [token]
        hosted = _sibling_exchange(grads_above, ALL_KINDS) if grads_above is not None else None
        gwin, gwout, *from_sibling = _weight_grads(l, h, dproj, ycat, dy, hosted, after)
        if l == 0:
            _, (slabs,) = _finish_exchange(
                gather_small, "all_gather_small_grads_finish", sems_s, slab, slabs, [gwin])
        issued = [gwin]
        if grads_above is not None:
            issued = [scatter(l + 1, grads_above, from_sibling, [])]
        grads_above = [gwin, gwout, gwpool.reshape(POOL_ROWS, GROUP_D)]
        if l <= 1:
            from_sibling = _run_exchange(_sibling_exchange(grads_above, ALL_KINDS), f"grad_exchange_sibling_{l}")
            token = scatter(l, grads_above, from_sibling, [slabs] if l == 0 else [])
            issued = [token]
            grads_above = None
    grad_x = dx[None]
    chips_0, sems_0, _ = in_flight.pop()

    o = 3 * D_MODEL

    after = [token]
    for chips, sems, l in in_flight:
        partials, received = _finish_exchange(chips, f"grad_chips_finish_{l}", sems, partials, received, after)
        after = []
    upper = (1, DEPTH)
    w_in_upper = _adamw_reduced(
        "adamw_w_in_upper", w_in, m_w_in, v_w_in, partials[0], received[0], chip, ROW_TILE, upper, None)
    w_out_upper = _adamw_reduced(
        "adamw_w_out_upper", w_out, m_w_out, v_w_out, partials[1], received[1], chip, W_OUT_SHARD, upper, None)
    dmod_all = slabs[:, :, :o].reshape(N_DEV, DEPTH, N_DEV, W_IN_SHARD)
    dmod_cols = lax.dynamic_index_in_dim(dmod_all, me, axis=2, keepdims=False).transpose(1, 0, 2) + token[0, 0]
    g_w_ada, d_w_ada, nm_w_ada, nv_w_ada = _adamw_w_ada(w_ada, m_w_ada, v_w_ada, c_act.T, dmod_cols)

    partials, received = _finish_exchange(
        chips_0, "grad_chips_finish_0", sems_0, partials, received, [nv_w_ada, w_in_upper[3], w_out_upper[3]])
    gather_pool = _all_gather_exchange(_reduce_w_pool(partials[2], received[2], chip), axis=1)
    sems_p, pool_rows, pool_landing, token_p = _start_exchange(gather_pool, "all_gather_grad_w_pool_start")
    g_w_in, d_w_in, nm_w_in, nv_w_in = _adamw_reduced(
        "adamw_w_in_0", w_in, m_w_in, v_w_in, partials[0], received[0], chip, ROW_TILE, (0, 1), w_in_upper, [token_p])
    g_w_out, d_w_out, nm_w_out, nv_w_out = _adamw_reduced(
        "adamw_w_out_0", w_out, m_w_out, v_w_out, partials[1], received[1], chip, W_OUT_SHARD, (0, 1), w_out_upper,
        [token_p])
    ends = [0, o, o + D_MODEL, o + 2 * D_MODEL, o + 2 * D_MODEL + POOL_W, SLAB_COLS]
    g_b_ada, g_g_pre, g_g_post, g_pool_scale, g_conv, loss = _sum_sources(
        slabs, list(zip(ends[:-1], ends[1:])), [token_p, nv_w_in, nv_w_out])
    loss = loss.reshape(())
    g_w_conv = lax.dynamic_slice_in_dim(g_conv.reshape(DEPTH, 3, CONV_W), me * conv_shard, conv_shard, axis=2)
    taps_first = lambda a: a.transpose(1, 0, 2)
    small = _adamw_small("adamw_small", [
        (b_ada, g_b_ada, m_b_ada, v_b_ada),
        (g_pre, g_g_pre, m_g_pre, v_g_pre),
        tuple(taps_first(a) for a in (w_conv, g_w_conv, m_w_conv, v_w_conv)),
        (pool_scale, g_pool_scale, m_pool_scale, v_pool_scale),
        (g_post, g_g_post, m_g_post, v_g_post),
    ])
    (d_b_ada, nm_b_ada, nv_b_ada), (d_g_pre, nm_g_pre, nv_g_pre), conv_steps, \
        (d_ps, nm_ps, nv_ps), (d_g_post, nm_g_post, nv_g_post) = small
    d_w_conv, nm_w_conv, nv_w_conv = (taps_first(a) for a in conv_steps)
    _, (g_pool_all,) = _finish_exchange(
        gather_pool, "all_gather_grad_w_pool_finish", sems_p, pool_rows, pool_landing, [nv_w_in, nv_w_out, nv_g_post])
    ((d_w_pool, nm_w_pool, nv_w_pool, g_w_pool),) = _adamw_small(
        "adamw_w_pool", [(w_pool, g_pool_all.reshape(w_pool.shape), m_w_pool, v_w_pool)], copy_gradients=True)

    return (loss, grad_x,
            g_w_ada, g_b_ada, g_g_pre, g_w_in, g_w_conv, g_w_pool, g_pool_scale, g_w_out, g_g_post,
            d_w_ada, d_b_ada, d_g_pre, d_w_in, d_w_conv, d_w_pool, d_ps, d_w_out, d_g_post,
            nm_w_ada, nm_b_ada, nm_g_pre, nm_w_in, nm_w_conv, nm_w_pool, nm_ps, nm_w_out, nm_g_post,
            nv_w_ada, nv_b_ada, nv_g_pre, nv_w_in, nv_w_conv, nv_w_pool, nv_ps, nv_w_out, nv_g_post)
```

```python
from functools import partial

import jax
import jax.numpy as jnp
from jax import lax
from jax.experimental import pallas as pl
from jax.experimental.pallas import tpu as pltpu

F32 = jnp.float32
BF16 = jnp.bfloat16

D_MODEL = 1024
DEPTH = 4
CONV_W = 512
POOL_W = 512
POOL_WINDOWS = (2, 4, 8, 16)
GROUP_D = 128
IN_COLS = 4 * CONV_W + 2 * POOL_W
NORM_EPS = 1e-6

ADAM_LR = 0.001
ADAM_B1 = 0.9
ADAM_B2 = 0.999
ADAM_EPS = 1e-08
ADAM_WD = 0.01
ADAM_STEP = 10

N_DEV = 8
N_CHIP = 4
N_OTHER_CHIPS = N_CHIP - 1
MESH = pl.DeviceIdType.MESH
W_IN_SHARD = IN_COLS // N_DEV
W_OUT_SHARD = D_MODEL // N_DEV
POOL_ROWS = len(POOL_WINDOWS) * GROUP_D
POOL_SHARD = POOL_ROWS // N_DEV

SUBLANES = 8
LANES = 128
VMEM_LIMIT_BYTES = 56 * 1024 * 1024
ROW_TILE = 512
BWD_TILE = 256
GWIN_COLS = 768
GWOUT_COLS = 512
POOL_HALO = 16
CONV_HALO = SUBLANES

SLAB_COLS = 3 * D_MODEL + D_MODEL + D_MODEL + POOL_W + 3 * CONV_W

HBM = pl.BlockSpec(memory_space=pl.ANY)


def _params(**kw):
    return pltpu.CompilerParams(vmem_limit_bytes=VMEM_LIMIT_BYTES, **kw)


def _sigmoid(v):
    return 1.0 / (1.0 + jnp.exp(-v))


def _dot(a, b):
    return jnp.dot(a, b, preferred_element_type=F32)


def _dot_tn(a, b):
    return lax.dot_general(a, b, (((0,), (0,)), ((), ())), preferred_element_type=F32)


def _dot_nt(a, b):
    return lax.dot_general(a, b, (((1,), (1,)), ((), ())), preferred_element_type=F32)


def _rows_from_before(v, k):
    return pltpu.roll(v, k, 0)


def _rows_from_after(v, k):
    return pltpu.roll(v, v.shape[0] - k, 0)


def _window_counts(t0, rows):
    return (lax.broadcasted_iota(jnp.int32, (rows, 1), 0) + (t0 + 1)).astype(F32)


def _split_proj(p32):
    cw = CONV_W
    return (p32[:, 0 * cw:1 * cw], p32[:, 1 * cw:2 * cw], p32[:, 2 * cw:3 * cw], p32[:, 3 * cw:4 * cw],
            p32[:, 4 * cw:4 * cw + POOL_W], p32[:, 4 * cw + POOL_W:])


def _layer_spec(shape, layer):
    nd = len(shape)
    return pl.BlockSpec((None,) + tuple(shape[1:]), lambda i, _l=layer, _n=nd: (_l,) + (0,) * (_n - 1))


def _whole_spec(shape):
    return pl.BlockSpec(tuple(shape), lambda i, _n=len(shape): (0,) * _n, pipeline_mode=pl.Buffered(1))


def _mesh_position():
    return lax.axis_index("x"), lax.axis_index("y"), lax.axis_index("c")


def _block_id(x, y, c):
    return 4 * x + 2 * y + c


def _other_chips(x, y):
    return [(x ^ 1, y), (x, y ^ 1), (x ^ 1, y ^ 1)]


def _col_block(ref, blk):
    return ref.at[:, pl.ds(pl.multiple_of(blk * W_IN_SHARD, LANES), W_IN_SHARD)]


def _row_block(rows):
    def block(ref, blk):
        return ref.at[pl.ds(pl.multiple_of(blk * rows, rows), rows), :]
    return block


_BLOCK_OF = (_col_block, _row_block(W_OUT_SHARD), _row_block(POOL_SHARD))
_BLOCK_SHAPES = ((D_MODEL, W_IN_SHARD), (W_OUT_SHARD, D_MODEL), (POOL_SHARD, GROUP_D))


class _Exchange:
    def __init__(self, inputs, out_shapes, aliases, sem_shapes, make):
        self.inputs, self.out_shapes, self.aliases, self.sem_shapes, self.make = (
            list(inputs), list(out_shapes), dict(aliases), list(sem_shapes), make)


def _run_exchange(exchange, name):
    n_in, n_out = len(exchange.inputs), len(exchange.out_shapes)

    def body(*refs):
        start, finish = exchange.make(refs[:n_in], refs[n_in:n_in + n_out], refs[n_in + n_out:])
        start()
        finish()

    return pl.pallas_call(
        body, name=name, in_specs=[HBM] * n_in, out_specs=[HBM] * n_out, out_shape=exchange.out_shapes,
        scratch_shapes=exchange.sem_shapes, input_output_aliases=exchange.aliases, compiler_params=_params(),
    )(*exchange.inputs)


_SEM = pl.BlockSpec(memory_space=pltpu.SEMAPHORE)
_DATAFLOW = pltpu.SideEffectType.DATAFLOW_SIDE_EFFECTING


def _start_exchange(exchange, name, after=()):
    n_in, n_out, n_sem = len(exchange.inputs), len(exchange.out_shapes), len(exchange.sem_shapes)
    sources = [i for i in range(n_in) if i not in exchange.aliases]
    aliases = {i: n_sem + k for k, i in enumerate(sources)}
    aliases.update({i: n_sem + len(sources) + o for i, o in exchange.aliases.items()})

    def body(*refs):
        in_refs = refs[:n_in]
        outs = refs[n_in + len(after):]
        sems = outs[:n_sem]
        out_refs = outs[n_sem + len(sources):n_sem + len(sources) + n_out]
        exchange.make(in_refs, out_refs, sems)[0]()
        refs[-1][...] = jnp.zeros_like(refs[-1])

    outs = pl.pallas_call(
        body, name=name, in_specs=[HBM] * (n_in + len(after)),
        out_specs=[_SEM] * n_sem + [HBM] * (len(sources) + n_out) + [pl.BlockSpec(memory_space=pltpu.VMEM)],
        out_shape=(exchange.sem_shapes + [pltpu.HBM(exchange.inputs[i].shape, exchange.inputs[i].dtype) for i in sources]
                   + [pltpu.HBM(s.shape, s.dtype) for s in exchange.out_shapes]
                   + [jax.ShapeDtypeStruct((SUBLANES, LANES), F32)]),
        input_output_aliases=aliases, compiler_params=_params(has_side_effects=_DATAFLOW),
    )(*exchange.inputs, *after)
    return outs[:n_sem], outs[n_sem:n_sem + len(sources)], outs[n_sem + len(sources):-1], outs[-1]


def _finish_exchange(exchange, name, sems, sources, landing, after):
    n_src, n_out, n_sem = len(sources), len(landing), len(sems)
    n_in = len(exchange.inputs)
    source_at = [i for i in range(n_in) if i not in exchange.aliases]

    def body(*refs):
        src_refs, out_refs = refs[:n_src], refs[n_src:n_src + n_out]
        sem_refs = refs[n_src + n_out:n_src + n_out + n_sem]
        in_refs = [None] * n_in
        for k, i in enumerate(source_at):
            in_refs[i] = src_refs[k]
        for i, o in exchange.aliases.items():
            in_refs[i] = out_refs[o]
        exchange.make(in_refs, out_refs, sem_refs)[1]()

    arrays = list(sources) + list(landing)
    outs = pl.pallas_call(
        body, name=name, in_specs=[HBM] * len(arrays) + [_SEM] * n_sem + [HBM] * len(after),
        out_specs=[HBM] * len(arrays), out_shape=[pltpu.HBM(a.shape, a.dtype) for a in arrays],
        input_output_aliases={i: i for i in range(len(arrays))}, compiler_params=_params(has_side_effects=_DATAFLOW),
    )(*arrays, *sems, *after)
    return outs[:n_src], outs[n_src:]


N_GATHERED = 2
FIRST_COPIES = 1 + N_OTHER_CHIPS
_GATHERED_SHAPES = ((D_MODEL, IN_COLS), (D_MODEL, D_MODEL))
ALL_OTHER_CHIPS, NEIGHBOUR_CHIPS, DIAGONAL_CHIP = (0, 1, 2), (0, 1), (2,)


def _first_sem(layer, a, k):
    return (layer * N_GATHERED + a) * FIRST_COPIES + k


def _first_arrival(layer, a, j):
    return _first_sem(layer, a, 1 + j)


def _gather_copy(window_of, full_ref, blk, send_sem, recv_sem, to, src=None):
    window = window_of(full_ref, blk)
    return pltpu.make_async_remote_copy(
        src_ref=window if src is None else src, dst_ref=window, send_sem=send_sem, recv_sem=recv_sem,
        device_id=to, device_id_type=MESH)


def _first_copies(layer, shard_refs, full_refs, send_sems, recv_sems, local_sems, relayed):
    x, y, c = _mesh_position()
    me = _block_id(x, y, c)
    chips = [_other_chips(x, y)[j] for j in (NEIGHBOUR_CHIPS if relayed else ALL_OTHER_CHIPS)]
    own, remote = [], []
    for a in range(N_GATHERED):
        shard = shard_refs[a].at[layer]
        own.append(pltpu.make_async_copy(
            shard, _BLOCK_OF[a](full_refs[a], me), local_sems.at[layer * N_GATHERED + a]))
        targets = [(x, y, 1 - c)] + [(*chip, c) for chip in chips]
        remote += [_gather_copy(_BLOCK_OF[a], full_refs[a], me, send_sems.at[_first_sem(layer, a, k)],
                                recv_sems.at[_first_sem(layer, a, k)], to, src=shard)
                   for k, to in enumerate(targets)]
    return own, remote


def _gather_weights_start(first_layer, win_shards, wout_shards, after, relayed=False):
    n_layers = win_shards.shape[0]
    n_first = n_layers * N_GATHERED * FIRST_COPIES
    sem_shapes = [pltpu.SemaphoreType.DMA((n_first,)), pltpu.SemaphoreType.DMA((n_first,)),
                  pltpu.SemaphoreType.DMA((n_layers * N_GATHERED,))]
    shards = [win_shards, wout_shards]

    def body(win_sh, wout_sh, *rest):
        send_sems, recv_sems, local_sems, win_thru, wout_thru, *landing = rest[len(after):]
        for layer in range(n_layers):
            own, remote = _first_copies(layer, (win_sh, wout_sh), landing[N_GATHERED * layer:N_GATHERED * (layer + 1)],
                                        send_sems, recv_sems, local_sems, relayed)
            for cp in own + remote:
                cp.start()

    outs = pl.pallas_call(
        body, name=f"all_gather_weights_start_{first_layer}", in_specs=[HBM] * (2 + len(after)),
        out_specs=[_SEM] * 3 + [HBM] * (2 + n_layers * N_GATHERED),
        out_shape=(sem_shapes + [pltpu.HBM(s.shape, s.dtype) for s in shards]
                   + [pltpu.HBM(s, BF16) for _ in range(n_layers) for s in _GATHERED_SHAPES]),
        input_output_aliases={0: 3, 1: 4}, compiler_params=_params(has_side_effects=_DATAFLOW),
    )(*shards, *after)
    landing = outs[5:]
    return outs[:3], outs[3:5], [landing[N_GATHERED * l:N_GATHERED * (l + 1)] for l in range(n_layers)]


def _passed_on_copies(full_refs, send_sems, recv_sems, core_of_block, chips):
    x, y, c = _mesh_position()
    return [_gather_copy(_BLOCK_OF[a], full_refs[a], _block_id(*_other_chips(x, y)[j], core_of_block),
                         send_sems.at[a * N_OTHER_CHIPS + j], recv_sems.at[a * N_OTHER_CHIPS + j], (x, y, 1 - c))
            for a in range(N_GATHERED) for j in chips]


def _relayed_copies(full_refs, send_sems, recv_sems):
    x, y, c = _mesh_position()
    source, to = (x ^ (1 - c), y ^ c), (x ^ c, y ^ (1 - c))
    return [_gather_copy(_BLOCK_OF[a], full_refs[a], _block_id(*source, c), send_sems.at[a], recv_sems.at[a], (*to, c))
            for a in range(N_GATHERED)]


def _gather_weights_pass_on(name, chips, arrival_sems, arrival_sem_of, landing, after, relay=False):
    n = N_GATHERED * N_OTHER_CHIPS
    sem_shapes = [pltpu.SemaphoreType.DMA((n,))] * 2 + [pltpu.SemaphoreType.DMA((N_GATHERED,))] * (2 if relay else 0)

    def body(win_ref, wout_ref, arrivals, *rest):
        sems = rest[len(after):len(after) + len(sem_shapes)]
        x, y, c = _mesh_position()
        full_refs = (win_ref, wout_ref)
        passed = _passed_on_copies(full_refs, sems[0], sems[1], c, chips)
        relayed = _relayed_copies(full_refs, sems[2], sems[3]) if relay else []
        for a in range(N_GATHERED):
            for j in chips:
                sem = arrivals.at[arrival_sem_of(a, j)]
                _gather_copy(_BLOCK_OF[a], full_refs[a], _block_id(*_other_chips(x, y)[j], c), sem, sem,
                             (x, y, c)).wait_recv()
            for cp in relayed[a:a + 1] + passed[a * len(chips):(a + 1) * len(chips)]:
                cp.start()

    outs = pl.pallas_call(
        body, name=name, in_specs=[HBM] * N_GATHERED + [_SEM] + [HBM] * len(after),
        out_specs=[_SEM] * len(sem_shapes) + [HBM] * N_GATHERED,
        out_shape=sem_shapes + [pltpu.HBM(a.shape, a.dtype) for a in landing],
        input_output_aliases={a: len(sem_shapes) + a for a in range(N_GATHERED)},
        compiler_params=_params(has_side_effects=_DATAFLOW),
    )(*landing, arrival_sems, *after)
    return outs[:len(sem_shapes)], outs[len(sem_shapes):]


def _gather_weights_finish(layer, index, first_sems, passed, shards, landing, relay_send_sems=None):
    relayed = relay_send_sems is not None
    passed_sems = [sem for (send, recv), _ in passed for sem in (send, recv)] + ([relay_send_sems] if relayed else [])

    def body(win_ref, wout_ref, first_send, first_recv, local_sems, *rest):
        sems, (win_sh, wout_sh) = rest[:len(passed_sems)], rest[len(passed_sems):len(passed_sems) + 2]
        x, y, c = _mesh_position()
        full_refs = (win_ref, wout_ref)
        own, sent = _first_copies(index, (win_sh, wout_sh), full_refs, first_send, first_recv, local_sems, relayed)
        for a in range(N_GATHERED):
            sem = _first_sem(index, a, 0)
            _gather_copy(_BLOCK_OF[a], full_refs[a], _block_id(x, y, 1 - c), first_recv.at[sem], first_recv.at[sem],
                         (x, y, c)).wait_recv()
        for p, (_, chips) in enumerate(passed):
            for cp in _passed_on_copies(full_refs, sems[2 * p], sems[2 * p + 1], 1 - c, chips):
                cp.wait_recv()
            sent += _passed_on_copies(full_refs, sems[2 * p], sems[2 * p + 1], c, chips)
        if relayed:
            sent += _relayed_copies(full_refs, sems[-1], sems[-1])
        for cp in sent:
            cp.wait_send()
        for cp in own:
            cp.wait()

    return pl.pallas_call(
        body, name=f"all_gather_weights_finish_{layer}",
        in_specs=[HBM] * N_GATHERED + [_SEM] * (3 + len(passed_sems)) + [HBM] * 2,
        out_specs=[HBM] * N_GATHERED, out_shape=[pltpu.HBM(a.shape, a.dtype) for a in landing],
        input_output_aliases={a: a for a in range(N_GATHERED)}, compiler_params=_params(has_side_effects=_DATAFLOW),
    )(*landing, *first_sems, *passed_sems, *shards)


ALL_KINDS = (0, 1, 2)


def _sibling_exchange(grads, kinds):
    n_arr = len(grads)

    def make(in_refs, out_refs, sems):
        send_sems, recv_sems = sems
        x, y, c = _mesh_position()
        copies = [pltpu.make_async_remote_copy(
            src_ref=_BLOCK_OF[kinds[a]](in_refs[a], 2 * q + (1 - c)), dst_ref=out_refs[a].at[q],
            send_sem=send_sems.at[a * N_CHIP + q], recv_sem=recv_sems.at[a * N_CHIP + q],
            device_id=(x, y, 1 - c), device_id_type=MESH)
            for a in range(n_arr) for q in range(N_CHIP)]

        def start():
            for cp in copies:
                cp.start()

        def finish():
            for cp in copies:
                cp.wait_recv()
            for cp in copies:
                cp.wait_send()

        return start, finish

    return _Exchange(
        grads, [jax.ShapeDtypeStruct((N_CHIP,) + _BLOCK_SHAPES[k], BF16) for k in kinds], {},
        [pltpu.SemaphoreType.DMA((n_arr * N_CHIP,)), pltpu.SemaphoreType.DMA((n_arr * N_CHIP,))], make)


def _chips_exchange(layer, partials, received, kinds):
    n_arr = len(partials)

    def make(in_refs, out_refs, sems):
        send_sems, recv_sems = sems
        x, y, c = _mesh_position()
        copies = [pltpu.make_async_remote_copy(
            src_ref=in_refs[a].at[2 * qx + qy, layer], dst_ref=out_refs[a].at[j, layer],
            send_sem=send_sems.at[a * N_OTHER_CHIPS + j], recv_sem=recv_sems.at[a * N_OTHER_CHIPS + j],
            device_id=(qx, qy, c), device_id_type=MESH)
            for a in range(n_arr) for j, (qx, qy) in enumerate(_other_chips(x, y))]

        def start():
            for cp in copies:
                cp.start()

        def finish():
            for cp in copies:
                cp.wait_recv()
            for cp in copies:
                cp.wait_send()

        return start, finish

    inputs = list(partials)
    aliases = {}
    if received is not None:
        inputs += list(received)
        aliases = {n_arr + a: a for a in range(n_arr)}
    return _Exchange(
        inputs, [jax.ShapeDtypeStruct((N_OTHER_CHIPS, DEPTH) + _BLOCK_SHAPES[k], BF16) for k in kinds], aliases,
        [pltpu.SemaphoreType.DMA((n_arr * N_OTHER_CHIPS,)), pltpu.SemaphoreType.DMA((n_arr * N_OTHER_CHIPS,))], make)


def _all_gather_exchange(v, axis=0):
    def make(in_refs, out_refs, sems):
        send_sems, recv_sems, local_sem = sems
        x, y, c = _mesh_position()
        me = _block_id(x, y, c)
        block = lambda blk: out_refs[0].at[(slice(None),) * axis + (blk,)]
        own = pltpu.make_async_copy(in_refs[0], block(me), local_sem.at[0])
        sends, arrivals = [], []
        for k in range(1, N_DEV):
            px, py, pc = x ^ ((k >> 2) & 1), y ^ ((k >> 1) & 1), c ^ (k & 1)
            sends.append(pltpu.make_async_remote_copy(
                src_ref=in_refs[0], dst_ref=block(me), send_sem=send_sems.at[k - 1],
                recv_sem=recv_sems.at[k - 1], device_id=(px, py, pc), device_id_type=MESH))
            arrivals.append(pltpu.make_async_remote_copy(
                src_ref=in_refs[0], dst_ref=block(_block_id(px, py, pc)), send_sem=send_sems.at[k - 1],
                recv_sem=recv_sems.at[k - 1], device_id=(x, y, c), device_id_type=MESH))

        def start():
            for cp in [own] + sends:
                cp.start()

        def finish():
            for cp in arrivals:
                cp.wait_recv()
            for cp in sends:
                cp.wait_send()
            own.wait()

        return start, finish

    return _Exchange(
        [v], [jax.ShapeDtypeStruct(v.shape[:axis] + (N_DEV,) + v.shape[axis:], v.dtype)], {},
        [pltpu.SemaphoreType.DMA((N_DEV - 1,)), pltpu.SemaphoreType.DMA((N_DEV - 1,)),
         pltpu.SemaphoreType.DMA((1,))], make)


def _host(exchange, args, in_specs, out_shape, out_specs, scratch):
    n_own = (len(args), len(out_shape), len(scratch))
    if exchange is None:
        return {}, lambda refs: (refs, None)
    n_ex = (len(exchange.inputs), len(exchange.out_shapes), len(exchange.sem_shapes))
    aliases = {n_own[0] + i: n_own[1] + o for i, o in exchange.aliases.items()}
    args += exchange.inputs
    in_specs += [HBM] * n_ex[0]
    out_shape += exchange.out_shapes
    out_specs += [HBM] * n_ex[1]
    scratch += exchange.sem_shapes

    def split(refs):
        own, theirs, at = [], [], 0
        for mine, ex in zip(n_own, n_ex):
            own += refs[at:at + mine]
            theirs.append(refs[at + mine:at + mine + ex])
            at += mine + ex
        return own, exchange.make(*theirs)

    return aliases, split


def _all_gather_small(v, name, after=()):
    vmem = pl.BlockSpec(memory_space=pltpu.VMEM)

    def body(v_ref, *rest):
        out_ref, send_sems, recv_sems = rest[len(after):]
        x, y, c = _mesh_position()
        me = _block_id(x, y, c)
        out_ref[me] = v_ref[...]
        sends = []
        for k in range(1, N_DEV):
            px, py, pc = x ^ ((k >> 2) & 1), y ^ ((k >> 1) & 1), c ^ (k & 1)
            send = pltpu.make_async_remote_copy(
                src_ref=v_ref, dst_ref=out_ref.at[me], send_sem=send_sems.at[k - 1], recv_sem=recv_sems.at[k - 1],
                device_id=(px, py, pc), device_id_type=MESH)
            send.start()
            sends.append((send, _block_id(px, py, pc)))
        for k, (send, peer) in enumerate(sends):
            pltpu.make_async_remote_copy(
                src_ref=v_ref, dst_ref=out_ref.at[peer], send_sem=send_sems.at[k], recv_sem=recv_sems.at[k],
                device_id=(x, y, c), device_id_type=MESH).wait_recv()
        for send, _ in sends:
            send.wait_send()

    return pl.pallas_call(
        body, name=name, in_specs=[vmem] + [HBM] * len(after), out_specs=vmem,
        out_shape=jax.ShapeDtypeStruct((N_DEV,) + v.shape, v.dtype),
        scratch_shapes=[pltpu.SemaphoreType.DMA((N_DEV - 1,)), pltpu.SemaphoreType.DMA((N_DEV - 1,))],
        compiler_params=_params(),
    )(v, *after)


def _forward_layer(layer, x, vec, cps, wpool, win, wout, target=None):
    t_len = x.shape[0]
    n_tiles = t_len // ROW_TILE
    row = lambda cols: pl.BlockSpec((ROW_TILE, cols), lambda i: (i, 0))
    widths = (D_MODEL, 2 * CONV_W, 3 * CONV_W, 4 * POOL_W)
    head = target is not None

    def body(x_ref, vec_ref, cps_ref, wpool_ref, win_ref, wout_ref, *rest):
        target_ref = rest[0] if head else None
        xo_ref, y_ref, h_ref, ycat_ref, uc_ref, fa_ref, fp_ref = rest[head:head + 7]
        loss_ref = rest[head + 7] if head else None
        zc_ref, pc_ref = rest[-2:]
        i = pl.program_id(0)

        @pl.when(i == 0)
        def _():
            zc_ref[...] = jnp.zeros_like(zc_ref)
            pc_ref[...] = jnp.zeros_like(pc_ref)
            if head:
                loss_ref[...] = jnp.zeros_like(loss_ref)

        x_t = x_ref[...]
        shift, scale, gate = vec_ref[0:1, :], vec_ref[1:2, :], vec_ref[2:3, :]
        g_pre, g_post = vec_ref[3:4, :], vec_ref[4:5, :]
        w0, w1, w2, ps = cps_ref[0:1, :], cps_ref[1:2, :], cps_ref[2:3, :], cps_ref[3:4, :]
        rx = lax.rsqrt(jnp.mean(x_t * x_t, axis=-1, keepdims=True) + NORM_EPS)
        h = (x_t * rx) * g_pre * (1.0 + scale) + shift
        h_ref[...] = h.astype(BF16)
        proj = _dot(h.astype(BF16), win_ref[...])
        u_a, b_a, c_a, g_a, u_p, g_p = _split_proj(proj)
        uc_ref[...] = jnp.concatenate([u_a, c_a], axis=1).astype(BF16)

        z = c_a * u_a
        zcat = jnp.concatenate([zc_ref[...], z], axis=0)
        zc_ref[...] = z[ROW_TILE - CONV_HALO:]
        conv = (w0 * _rows_from_before(zcat, 2)[CONV_HALO:] + w1 * _rows_from_before(zcat, 1)[CONV_HALO:] + w2 * z)
        sig_a = _sigmoid(g_a)
        silu_a = g_a * sig_a
        b_conv = b_a * conv
        y_a = b_conv * silu_a
        fa_ref[...] = jnp.concatenate(
            [silu_a * conv, silu_a * b_a, b_conv * (sig_a + silu_a * (1.0 - sig_a))], axis=1).astype(BF16)

        pcat = jnp.concatenate([pc_ref[...], u_p], axis=0)
        pc_ref[...] = u_p[ROW_TILE - POOL_HALO:]
        counts = _window_counts(i * ROW_TILE, ROW_TILE)
        pooled, mixed = [], []
        for g, w in enumerate(POOL_WINDOWS):
            cols = slice(g * GROUP_D, (g + 1) * GROUP_D)
            s = pcat[:, cols]
            step = 1
            while step < w:
                s = s + _rows_from_before(s, step)
                step *= 2
            pooled_g = (s[POOL_HALO:] * (1.0 / jnp.minimum(counts, float(w))) - u_p[:, cols]).astype(BF16)
            pooled.append(pooled_g)
            mixed.append(_dot(pooled_g, wpool_ref[g]))
        mixed = jnp.concatenate(mixed, axis=1)
        sig_p = _sigmoid(g_p)
        silu_p = g_p * sig_p
        mixed_ps = mixed * ps
        y_p = mixed_ps * silu_p
        fp_ref[...] = jnp.concatenate(
            [(ps * silu_p).astype(BF16), (mixed_ps * (sig_p + silu_p * (1.0 - sig_p))).astype(BF16),
             (silu_p * mixed).astype(BF16)] + pooled, axis=1)

        ycat = jnp.concatenate([y_a, y_p], axis=1)
        ycat_ref[...] = ycat.astype(BF16)
        y_b = _dot(ycat.astype(BF16), wout_ref[...]).astype(BF16)
        y_ref[...] = y_b
        y_t = y_b.astype(F32)
        ry = lax.rsqrt(jnp.mean(y_t * y_t, axis=-1, keepdims=True) + NORM_EPS)
        x_next = x_t + gate * (y_t * ry * g_post)
        if head:
            err = x_next - target_ref[...]
            xo_ref[...] = err * (1.0 / D_MODEL)
            loss_ref[...] += jnp.sum(err * err) * (0.5 / D_MODEL)
        else:
            xo_ref[...] = x_next

    tile = (SUBLANES, LANES)
    return pl.pallas_call(
        body, name=f"forward_layer_{layer}", grid=(n_tiles,),
        in_specs=[row(D_MODEL), _layer_spec(vec.shape, layer), _layer_spec(cps.shape, layer),
                  _layer_spec(wpool.shape, layer), _whole_spec(win.shape), _whole_spec(wout.shape)]
        + [row(D_MODEL)] * head,
        out_specs=[row(D_MODEL), row(D_MODEL), row(D_MODEL), row(D_MODEL)] + [row(w) for w in widths[1:]]
        + [_whole_spec(tile)] * head,
        out_shape=[jax.ShapeDtypeStruct((t_len, D_MODEL), F32), jax.ShapeDtypeStruct((t_len, D_MODEL), BF16),
                   jax.ShapeDtypeStruct((t_len, D_MODEL), BF16), jax.ShapeDtypeStruct((t_len, D_MODEL), BF16)]
        + [jax.ShapeDtypeStruct((t_len, w), BF16) for w in widths[1:]] + [jax.ShapeDtypeStruct(tile, F32)] * head,
        scratch_shapes=[pltpu.VMEM((CONV_HALO, CONV_W), F32), pltpu.VMEM((POOL_HALO, POOL_W), F32)],
        compiler_params=_params(dimension_semantics=("arbitrary",)),
    )(x, vec, cps, wpool, win, wout, *([target] * head))


def _backward_layer(layer, dxo, y, x, uc, fa, fp, vec, cps, wpool, wout, win, after):
    t_len = dxo.shape[0]
    n_tiles = t_len // BWD_TILE
    halo_per_tile = BWD_TILE // POOL_HALO
    rev = lambda cols: pl.BlockSpec((BWD_TILE, cols), lambda i: (n_tiles - 1 - i, 0))
    halo_spec = pl.BlockSpec(
        (POOL_HALO, 2 * CONV_W), lambda i: (jnp.maximum((n_tiles - 1 - i) * halo_per_tile - 1, 0), 0))
    gwpool_shape = (len(POOL_WINDOWS), GROUP_D, GROUP_D)

    def body(dxo_ref, y_ref, x_ref, uc_ref, uch_ref, fa_ref, fp_ref, vec_ref, cps_ref, wpool_ref, wout_ref, win_ref,
             *rest):
        dx_ref, dproj_ref, dy_ref, gwpool_ref, dcps_ref, dvec_ref, gwpool_acc, dcc_ref, qc_ref = rest[len(after):]
        i = pl.program_id(0)
        tile = n_tiles - 1 - i

        @pl.when(i == 0)
        def _():
            gwpool_acc[...] = jnp.zeros_like(gwpool_acc)
            dcps_ref[...] = jnp.zeros_like(dcps_ref)
            dvec_ref[...] = jnp.zeros_like(dvec_ref)
            dcc_ref[...] = jnp.zeros_like(dcc_ref)
            qc_ref[...] = jnp.zeros_like(qc_ref)

        shift, scale, gate = vec_ref[0:1, :], vec_ref[1:2, :], vec_ref[2:3, :]
        g_pre, g_post = vec_ref[3:4, :], vec_ref[4:5, :]
        w0, w1, w2 = cps_ref[0:1, :], cps_ref[1:2, :], cps_ref[2:3, :]

        dxo_t = dxo_ref[...]
        y_t = y_ref[...].astype(F32)
        ry = lax.rsqrt(jnp.mean(y_t * y_t, axis=-1, keepdims=True) + NORM_EPS)
        yh = y_t * ry
        dvec_ref[2:3, :] += jnp.sum(dxo_t * yh, axis=0, keepdims=True)
        dyh = dxo_t * (gate * g_post)
        dy_b = (ry * (dyh - yh * jnp.mean(dyh * yh, axis=-1, keepdims=True))).astype(BF16)
        dy_ref[...] = dy_b
        dycat = _dot_nt(dy_b, wout_ref[...])
        dy_a, dy_p = dycat[:, :CONV_W], dycat[:, CONV_W:]

        fa_t = fa_ref[...].astype(F32)
        db_a = dy_a * fa_t[:, :CONV_W]
        dconv = dy_a * fa_t[:, CONV_W:2 * CONV_W]
        dg_a = dy_a * fa_t[:, 2 * CONV_W:]
        uc_t = uc_ref[...].astype(F32)
        u_a, c_a = uc_t[:, :CONV_W], uc_t[:, CONV_W:]
        halo = jnp.where(tile > 0, uch_ref[...].astype(F32), 0.0)[POOL_HALO - CONV_HALO:]
        z = c_a * u_a
        zcat = jnp.concatenate([halo[:, CONV_W:] * halo[:, :CONV_W], z], axis=0)
        z1 = _rows_from_before(zcat, 1)[CONV_HALO:]
        z2 = _rows_from_before(zcat, 2)[CONV_HALO:]
        dccat = jnp.concatenate([dconv, dcc_ref[...]], axis=0)
        dc1 = _rows_from_after(dccat, 1)[:BWD_TILE]
        dc2 = _rows_from_after(dccat, 2)[:BWD_TILE]
        dz = w2 * dconv + w1 * dc1 + w0 * dc2
        dcc_ref[...] = dconv[:CONV_HALO]
        dcps_ref[0:1, :] += jnp.sum(dconv * z2, axis=0, keepdims=True)
        dcps_ref[1:2, :] += jnp.sum(dconv * z1, axis=0, keepdims=True)
        dcps_ref[2:3, :] += jnp.sum(dconv * z, axis=0, keepdims=True)
        du_a = dz * c_a
        dc_a = dz * u_a

        dmixed = (dy_p * fp_ref[:, :POOL_W].astype(F32)).astype(BF16)
        dg_p = dy_p * fp_ref[:, POOL_W:2 * POOL_W].astype(F32)
        dcps_ref[3:4, :] += jnp.sum(dy_p * fp_ref[:, 2 * POOL_W:3 * POOL_W].astype(F32), axis=0, keepdims=True)
        counts = _window_counts(tile * BWD_TILE, BWD_TILE)
        du_p, q_head = [], []
        for g, w in enumerate(POOL_WINDOWS):
            cols = slice(g * GROUP_D, (g + 1) * GROUP_D)
            dm_g = dmixed[:, cols]
            dpooled_g = _dot_nt(dm_g, wpool_ref[g])
            gwpool_acc[g] += _dot_tn(fp_ref[:, 3 * POOL_W + g * GROUP_D:3 * POOL_W + (g + 1) * GROUP_D], dm_g)
            q_g = dpooled_g * (1.0 / jnp.minimum(counts, float(w)))
            q_head.append(q_g[:POOL_HALO])
            s = jnp.concatenate([q_g, qc_ref[:, cols]], axis=0)
            step = 1
            while step < w:
                s = s + _rows_from_after(s, step)
                step *= 2
            du_p.append(s[:BWD_TILE] - dpooled_g)
        qc_ref[...] = jnp.concatenate(q_head, axis=1)
        dproj_b = jnp.concatenate([du_a, db_a, dc_a, dg_a] + du_p + [dg_p], axis=1).astype(BF16)
        dproj_ref[...] = dproj_b

        x_t = x_ref[...]
        rx = lax.rsqrt(jnp.mean(x_t * x_t, axis=-1, keepdims=True) + NORM_EPS)
        xn = x_t * rx
        mod_scale = 1.0 + scale
        dh = _dot_nt(dproj_b, win_ref[...])
        dvec_ref[0:1, :] += jnp.sum(dh, axis=0, keepdims=True)
        dvec_ref[1:2, :] += jnp.sum(dh * xn, axis=0, keepdims=True)
        dxn = dh * (g_pre * mod_scale)
        dx_ref[...] = dxo_t + rx * (dxn - xn * jnp.mean(dxn * xn, axis=-1, keepdims=True))

        @pl.when(i == n_tiles - 1)
        def _():
            gwpool_ref[...] = gwpool_acc[...].astype(BF16)
            sum_dh_xn, sum_dxo_yh = dvec_ref[1:2, :], dvec_ref[2:3, :]
            dvec_ref[1:2, :] = sum_dh_xn * g_pre
            dvec_ref[3:4, :] = sum_dh_xn * mod_scale
            dvec_ref[2:3, :] = sum_dxo_yh * g_post
            dvec_ref[4:5, :] = sum_dxo_yh * gate

    return pl.pallas_call(
        body, name=f"backward_layer_{layer}", grid=(n_tiles,),
        in_specs=[rev(D_MODEL), rev(D_MODEL), rev(D_MODEL), rev(2 * CONV_W), halo_spec, rev(3 * CONV_W),
                  rev(4 * POOL_W), _layer_spec(vec.shape, layer), _layer_spec(cps.shape, layer),
                  _layer_spec(wpool.shape, layer), _whole_spec(wout.shape), _whole_spec(win.shape)]
        + [HBM] * len(after),
        out_specs=[rev(D_MODEL), rev(IN_COLS), rev(D_MODEL), _whole_spec(gwpool_shape),
                   _whole_spec((SUBLANES, CONV_W)), _whole_spec((SUBLANES, D_MODEL))],
        out_shape=[jax.ShapeDtypeStruct((t_len, D_MODEL), F32), jax.ShapeDtypeStruct((t_len, IN_COLS), BF16),
                   jax.ShapeDtypeStruct((t_len, D_MODEL), BF16), jax.ShapeDtypeStruct(gwpool_shape, BF16),
                   jax.ShapeDtypeStruct((SUBLANES, CONV_W), F32), jax.ShapeDtypeStruct((SUBLANES, D_MODEL), F32)],
        scratch_shapes=[pltpu.VMEM(gwpool_shape, F32), pltpu.VMEM((CONV_HALO, CONV_W), F32),
                        pltpu.VMEM((POOL_HALO, POOL_W), F32)],
        compiler_params=_params(dimension_semantics=("arbitrary",)),
    )(dxo, y, x, uc, uc, fa, fp, vec, cps, wpool, wout, win, *after)


def _weight_grads(layer, h, dproj, ycat, dy, exchange, after=()):
    t_len = dy.shape[0]
    n_in, n_out = IN_COLS // GWIN_COLS, D_MODEL // GWOUT_COLS
    args = [h, dproj, ycat, dy, *after]
    in_specs = [_whole_spec(h.shape),
                pl.BlockSpec((t_len, GWIN_COLS), lambda s: (0, jnp.minimum(s, n_in - 1))),
                _whole_spec(ycat.shape),
                pl.BlockSpec((t_len, GWOUT_COLS), lambda s: (0, jnp.maximum(s - n_in, 0)))] + [HBM] * len(after)
    out_shape = [jax.ShapeDtypeStruct((D_MODEL, IN_COLS), BF16), jax.ShapeDtypeStruct((D_MODEL, D_MODEL), BF16)]
    out_specs = [pl.BlockSpec((D_MODEL, GWIN_COLS), lambda s: (0, jnp.minimum(s, n_in - 1))),
                 pl.BlockSpec((D_MODEL, GWOUT_COLS), lambda s: (0, jnp.maximum(s - n_in, 0)))]
    scratch = []
    aliases, split = _host(exchange, args, in_specs, out_shape, out_specs, scratch)

    def body(*refs):
        (h_ref, dproj_ref, ycat_ref, dy_ref, *_, gwin_ref, gwout_ref), hosted = split(refs)
        s = pl.program_id(0)
        if hosted is not None:
            pl.when(s == 0)(hosted[0])

        @pl.when(s < n_in)
        def _():
            gwin_ref[...] = _dot_tn(h_ref[...], dproj_ref[...]).astype(BF16)

        @pl.when(s >= n_in)
        def _():
            gwout_ref[...] = _dot_tn(ycat_ref[...], dy_ref[...]).astype(BF16)

        if hosted is not None:
            pl.when(s == n_in + n_out - 1)(hosted[1])

    return pl.pallas_call(
        body, name=f"weight_grads_{layer}", grid=(n_in + n_out,), in_specs=in_specs, out_specs=out_specs,
        out_shape=out_shape, scratch_shapes=scratch, input_output_aliases=aliases,
        compiler_params=_params(dimension_semantics=("arbitrary",)),
    )(*args)


def _add_sibling_blocks(name, layer, grads, received, core, partials, kinds):
    n_arr = len(grads)

    def body(core_ref, *refs):
        mine, theirs, outs = refs[:n_arr], refs[n_arr:2 * n_arr], refs[-n_arr:]
        for a in range(n_arr):
            outs[a][...] = (mine[a][...].astype(F32) + theirs[a][...].astype(F32)).astype(BF16)

    own_of_kind = [
        pl.BlockSpec((D_MODEL, W_IN_SHARD), lambda q, core_ref: (0, 2 * q + core_ref[0])),
        pl.BlockSpec((W_OUT_SHARD, D_MODEL), lambda q, core_ref: (2 * q + core_ref[0], 0)),
        pl.BlockSpec((POOL_SHARD, GROUP_D), lambda q, core_ref: (2 * q + core_ref[0], 0)),
    ]
    shapes = [_BLOCK_SHAPES[k] for k in kinds]
    recv_specs = [pl.BlockSpec((None,) + s, lambda q, core_ref: (q, 0, 0)) for s in shapes]
    out_specs = [pl.BlockSpec((None, None) + s, lambda q, core_ref: (q, layer, 0, 0)) for s in shapes]
    args = [core, *grads, *received]
    in_specs = [own_of_kind[k] for k in kinds] + recv_specs
    aliases = {}
    if partials is not None:
        aliases = {len(args) + a: a for a in range(n_arr)}
        args += list(partials)
        in_specs += [HBM] * n_arr
    return pl.pallas_call(
        body, name=name,
        grid_spec=pltpu.PrefetchScalarGridSpec(
            num_scalar_prefetch=1, grid=(N_CHIP,), in_specs=in_specs, out_specs=out_specs),
        out_shape=[jax.ShapeDtypeStruct((N_CHIP, DEPTH) + s, BF16) for s in shapes],
        input_output_aliases=aliases,
        compiler_params=_params(dimension_semantics=("arbitrary",)),
    )(*args)


def _modulation_columns(c_all, w_ada):
    def body(c_ref, w_ref, cact_ref, out_ref):
        c_t = c_ref[...]
        c_act = c_t * _sigmoid(c_t)
        cact_ref[...] = c_act
        out_ref[...] = jnp.dot(c_act, w_ref[...], preferred_element_type=F32, precision=lax.Precision.HIGHEST)

    return pl.pallas_call(
        body, name="modulation_columns", grid=(DEPTH,),
        in_specs=[pl.BlockSpec((N_DEV, D_MODEL), lambda l: (0, 0)),
                  pl.BlockSpec((None, D_MODEL, W_IN_SHARD), lambda l: (l, 0, 0))],
        out_specs=[pl.BlockSpec((N_DEV, D_MODEL), lambda l: (0, 0)),
                   pl.BlockSpec((N_DEV, W_IN_SHARD), lambda l: (0, l))],
        out_shape=[jax.ShapeDtypeStruct((N_DEV, D_MODEL), F32),
                   jax.ShapeDtypeStruct((N_DEV, DEPTH * W_IN_SHARD), F32)],
        compiler_params=_params(dimension_semantics=("arbitrary",)),
    )(c_all, w_ada)


def _adamw(w, g, m, v):
    m_new = ADAM_B1 * m + (1.0 - ADAM_B1) * g
    v_new = ADAM_B2 * v + (1.0 - ADAM_B2) * (g * g)
    m_hat = m_new / (1.0 - ADAM_B1 ** ADAM_STEP)
    v_hat = v_new / (1.0 - ADAM_B2 ** ADAM_STEP)
    delta = -ADAM_LR * (m_hat / (jnp.sqrt(v_hat) + ADAM_EPS) + ADAM_WD * w)
    return delta, m_new, v_new


def _adamw_w_ada(w, m, v, c_act_t, dmod_cols):
    def body(w_ref, m_ref, v_ref, ct_ref, dm_ref, g_ref, d_ref, mo_ref, vo_ref):
        g = ct_ref[:, 0:1] * dm_ref[0:1, :]
        for b in range(1, N_DEV):
            g = g + ct_ref[:, b:b + 1] * dm_ref[b:b + 1, :]
        g_ref[...] = g
        d_ref[...], mo_ref[...], vo_ref[...] = _adamw(w_ref[...], g, m_ref[...], v_ref[...])

    big = pl.BlockSpec((None, D_MODEL, W_IN_SHARD), lambda l: (l, 0, 0))
    return pl.pallas_call(
        body, name="adamw_w_ada", grid=(DEPTH,),
        in_specs=[big, big, big, pl.BlockSpec((D_MODEL, N_DEV), lambda l: (0, 0)),
                  pl.BlockSpec((None, N_DEV, W_IN_SHARD), lambda l: (l, 0, 0))],
        out_specs=[big] * 4, out_shape=[jax.ShapeDtypeStruct(w.shape, F32)] * 4,
        compiler_params=_params(dimension_semantics=("arbitrary",)),
    )(w, m, v, c_act_t, dmod_cols)


def _sum_chip_partials(own_ref, recv_ref):
    g = own_ref[...].astype(F32)
    for j in range(N_OTHER_CHIPS):
        g = g + recv_ref[j].astype(F32)
    return g


def _partial_specs(row_tile, cols, first_layer=0):
    own = pl.BlockSpec((None, None, row_tile, cols), lambda l, r, chip_ref: (chip_ref[0], first_layer + l, r, 0))
    recv = pl.BlockSpec((N_OTHER_CHIPS, None, row_tile, cols), lambda l, r, chip_ref: (0, first_layer + l, r, 0))
    return own, recv


def _adamw_reduced(name, w, m, v, partial, received, chip, row_tile, layers, continued, after=()):
    depth, rows, cols = w.shape
    first, stop = layers

    def body(chip_ref, w_ref, m_ref, v_ref, own_ref, recv_ref, *rest):
        g_ref, d_ref, mo_ref, vo_ref = rest[-4:]
        g = _sum_chip_partials(own_ref, recv_ref)
        g_ref[...] = g
        d_ref[...], mo_ref[...], vo_ref[...] = _adamw(w_ref[...], g, m_ref[...], v_ref[...])

    blk = pl.BlockSpec((None, row_tile, cols), lambda l, r, chip_ref: (first + l, r, 0))
    args = [chip, w, m, v, partial, received]
    in_specs = [blk, blk, blk, *_partial_specs(row_tile, cols, first)]
    aliases = {}
    if continued is not None:
        aliases = {len(args) + k: k for k in range(4)}
        args += list(continued)
        in_specs += [HBM] * 4
    args += after
    in_specs += [HBM] * len(after)
    return pl.pallas_call(
        body, name=name,
        grid_spec=pltpu.PrefetchScalarGridSpec(
            num_scalar_prefetch=1, grid=(stop - first, rows // row_tile), in_specs=in_specs, out_specs=[blk] * 4),
        out_shape=[jax.ShapeDtypeStruct(w.shape, F32)] * 4, input_output_aliases=aliases,
        compiler_params=_params(dimension_semantics=("arbitrary", "arbitrary")),
    )(*args)


def _reduce_w_pool(partial, received, chip):
    def body(chip_ref, own_ref, recv_ref, g_ref):
        g_ref[...] = _sum_chip_partials(own_ref, recv_ref)

    return pl.pallas_call(
        body, name="reduce_w_pool",
        grid_spec=pltpu.PrefetchScalarGridSpec(
            num_scalar_prefetch=1, grid=(DEPTH, 1), in_specs=list(_partial_specs(POOL_SHARD, GROUP_D)),
            out_specs=pl.BlockSpec((None, POOL_SHARD, GROUP_D), lambda l, r, chip_ref: (l, 0, 0))),
        out_shape=jax.ShapeDtypeStruct((DEPTH, POOL_SHARD, GROUP_D), F32),
        compiler_params=_params(dimension_semantics=("arbitrary", "arbitrary")),
    )(chip, partial, received)


def _adamw_small(name, params, by_layer=False):
    n = len(params)

    def body(*refs):
        ins, outs = refs[:4 * n], refs[4 * n:]
        for p in range(n):
            w_ref, g_ref, m_ref, v_ref = ins[4 * p:4 * p + 4]
            d_ref, mo_ref, vo_ref, go_ref = outs[4 * p:4 * p + 4]
            d_ref[...], mo_ref[...], vo_ref[...] = _adamw(w_ref[...], g_ref[...], m_ref[...], v_ref[...])
            go_ref[...] = g_ref[...]

    def spec(a):
        if not by_layer:
            return pl.BlockSpec(memory_space=pltpu.VMEM)
        return pl.BlockSpec((None,) + a.shape[1:], lambda l: (l,) + (0,) * (a.ndim - 1))

    flat = [a for group in params for a in group]
    outs = pl.pallas_call(
        body, name=name, grid=(params[0][0].shape[0],) if by_layer else (), in_specs=[spec(a) for a in flat],
        out_specs=[spec(a) for a in flat], out_shape=[jax.ShapeDtypeStruct(a.shape, F32) for a in flat],
        compiler_params=_params(),
    )(*flat)
    return [tuple(outs[4 * p:4 * p + 4]) for p in range(n)]


def _sum_sources(slabs, columns, after):
    def body(s_ref, *rest):
        outs = rest[len(after):]
        acc = s_ref[0]
        for b in range(1, N_DEV):
            acc = acc + s_ref[b]
        for (start, stop), o_ref in zip(columns, outs):
            o_ref[...] = acc[:, start:stop]
        outs[-1][...] = acc[0:1, SLAB_COLS:SLAB_COLS + 1]

    vmem = pl.BlockSpec(memory_space=pltpu.VMEM)
    out_shape = [jax.ShapeDtypeStruct((slabs.shape[1], stop - start), F32) for start, stop in columns]
    out_shape.append(jax.ShapeDtypeStruct((1, 1), F32))
    return pl.pallas_call(
        body, name="sum_small_grads", in_specs=[vmem] + [HBM] * len(after), out_specs=[vmem] * len(out_shape),
        out_shape=out_shape, compiler_params=_params(),
    )(slabs, *after)


def _to_bf16(a, name, layers=None):
    first, stop = layers or (0, a.shape[0])

    def body(a_ref, o_ref):
        o_ref[...] = a_ref[...].astype(BF16)

    block = (None,) + a.shape[1:]
    return pl.pallas_call(
        body, name=name, grid=(stop - first,), in_specs=[pl.BlockSpec(block, lambda l: (first + l, 0, 0))],
        out_specs=pl.BlockSpec(block, lambda l: (l, 0, 0)),
        out_shape=jax.ShapeDtypeStruct((stop - first,) + a.shape[1:], BF16),
        compiler_params=_params(dimension_semantics=("arbitrary",)),
    )(a)


def kernel(x, c, w_ada, b_ada, g_pre, w_in, w_conv, w_pool, pool_scale, w_out, g_post, loss_target, m_w_ada, m_b_ada, m_g_pre, m_w_in, m_w_conv, m_w_pool, m_pool_scale, m_w_out, m_g_post, v_w_ada, v_b_ada, v_g_pre, v_w_in, v_w_conv, v_w_pool, v_pool_scale, v_w_out, v_g_post):
    mx, my, mc = _mesh_position()
    me = _block_id(mx, my, mc)
    chip = (2 * mx + my).astype(jnp.int32).reshape(1)
    core = mc.astype(jnp.int32).reshape(1)
    x0 = x[0]
    target = loss_target[0]
    conv_shard = w_conv.shape[-1]

    own_small = jnp.concatenate([c, w_conv.reshape(1, DEPTH * 3 * conv_shard)], axis=1)
    first_shards = [_to_bf16(w_in, "cast_w_in_0", (0, 1)), _to_bf16(w_out, "cast_w_out_0", (0, 1))]
    all_small = _all_gather_small(own_small, "all_gather_c_w_conv", first_shards)[:, 0, :]
    gathers = [_gather_weights_start(0, *first_shards, [all_small], relayed=True)]
    c_all = all_small[:, :D_MODEL]
    w_conv_full = all_small[:, D_MODEL:].reshape(N_DEV, DEPTH, 3, conv_shard).transpose(1, 2, 0, 3).reshape(
        DEPTH, 3, CONV_W)
    cps = jnp.concatenate([w_conv_full, pool_scale[:, None], jnp.zeros((DEPTH, 4, CONV_W), F32)], axis=1)

    c_act, pieces = _modulation_columns(c_all, w_ada)
    upper = (1, DEPTH)
    upper_shards = [_to_bf16(w_in, "cast_w_in_1", upper), _to_bf16(w_out, "cast_w_out_1", upper)]
    wpool_b = _to_bf16(w_pool.reshape(DEPTH, POOL_ROWS, GROUP_D), "cast_w_pool").reshape(w_pool.shape)
    first_sems_0, _, (zones_0,) = gathers[0]
    neighbour_sems, zones_0 = _gather_weights_pass_on(
        "all_gather_weights_relay_0", NEIGHBOUR_CHIPS, first_sems_0[1], partial(_first_arrival, 0), zones_0,
        [pieces, *upper_shards, wpool_b], relay=True)
    mod_all = _all_gather_small(pieces, "all_gather_modulation", [zones_0[0]])
    mod_mine = lax.dynamic_index_in_dim(mod_all, me, axis=1, keepdims=False)
    mod = mod_mine.reshape(N_DEV, DEPTH, W_IN_SHARD).transpose(1, 0, 2).reshape(DEPTH, 3 * D_MODEL) + b_ada
    zeros_d = jnp.zeros((DEPTH, 3, D_MODEL), F32)
    vec = jnp.concatenate([mod.reshape(DEPTH, 3, D_MODEL), g_pre[:, None], g_post[:, None], zeros_d], axis=1)

    gathers.append(_gather_weights_start(1, *upper_shards, [mod_all]))
    gathers = [(first_sems, shards, landing[k], k) for first_sems, shards, landing in gathers
               for k in range(len(landing))]

    xs, kept, wins, wouts = [x0], [], [], []
    for l in range(DEPTH):
        first_sems, shards, zones, index = gathers[l]
        if l == 0:
            diagonal_sems, zones = _gather_weights_pass_on(
                "all_gather_weights_pass_on_0", DIAGONAL_CHIP, neighbour_sems[3], lambda a, j: a, zones_0,
                [vec, cps, gathers[-1][1][0]])
            passed = [(neighbour_sems[:2], NEIGHBOUR_CHIPS), (diagonal_sems, DIAGONAL_CHIP)]
            win, wout = _gather_weights_finish(l, index, first_sems, passed, shards, zones, neighbour_sems[2])
        else:
            passed_sems, zones = _gather_weights_pass_on(
                f"all_gather_weights_pass_on_{l}", ALL_OTHER_CHIPS, first_sems[1], partial(_first_arrival, index),
                zones, [xs[-1]])
            win, wout = _gather_weights_finish(l, index, first_sems, [(passed_sems, ALL_OTHER_CHIPS)], shards, zones)
        x_next, *for_backward = _forward_layer(
            l, xs[-1], vec, cps, wpool_b, win, wout, target if l == DEPTH - 1 else None)
        xs.append(x_next)
        kept.append(for_backward[:6])
        wins.append(win)
        wouts.append(wout)
    dx, loss_tile = xs[DEPTH], for_backward[6]

    slab_rows = [None] * DEPTH
    partials = received = None
    in_flight = []

    def scatter(layer, grads, from_sibling, after):
        nonlocal partials, received
        partials = _add_sibling_blocks(
            f"grad_add_sibling_{layer}", layer, grads, from_sibling, core, partials, ALL_KINDS)
        chips = _chips_exchange(layer, partials, received, ALL_KINDS)
        sems, partials, received, token = _start_exchange(chips, f"grad_chips_start_{layer}", after)
        in_flight.append((chips, sems, layer))
        return token

    grads_above = None
    issued = []
    for l in reversed(range(DEPTH)):
        y, h, ycat, uc, fa, fp = kept[l]
        dx, dproj, dy, gwpool, dcps, dvec = _backward_layer(
            l, dx, y, xs[l], uc, fa, fp, vec, cps, wpool_b, wouts[l], wins[l], issued)
        slab_rows[l] = jnp.concatenate(
            [dvec[0], dvec[1], dvec[2], dvec[3], dvec[4], dcps[3], dcps[0], dcps[1], dcps[2],
             loss_tile[0] if l == 0 else jnp.zeros((LANES,), F32)])
        after = []
        if l == 0:
            gather_small = _all_gather_exchange(jnp.stack(slab_rows))
            sems_s, slab, slabs, token = _start_exchange(gather_small, "all_gather_small_grads_start")
            after = [token]
        hosted = _sibling_exchange(grads_above, ALL_KINDS) if grads_above is not None else None
        gwin, gwout, *from_sibling = _weight_grads(l, h, dproj, ycat, dy, hosted, after)
        if l == 0:
            _, (slabs,) = _finish_exchange(
                gather_small, "all_gather_small_grads_finish", sems_s, slab, slabs, [gwin])
        issued = [gwin]
        if grads_above is not None:
            issued = [scatter(l + 1, grads_above, from_sibling, [])]
        grads_above = [gwin, gwout, gwpool.reshape(POOL_ROWS, GROUP_D)]
        if l <= 1:
            from_sibling = _run_exchange(_sibling_exchange(grads_above, ALL_KINDS), f"grad_exchange_sibling_{l}")
            token = scatter(l, grads_above, from_sibling, [slabs] if l == 0 else [])
            issued = [token]
            grads_above = None
    grad_x = dx[None]
    chips_0, sems_0, _ = in_flight.pop()

    o = 3 * D_MODEL

    after = [token]
    for chips, sems, l in in_flight:
        partials, received = _finish_exchange(chips, f"grad_chips_finish_{l}", sems, partials, received, after)
        after = []
    upper = (1, DEPTH)
    w_in_upper = _adamw_reduced(
        "adamw_w_in_upper", w_in, m_w_in, v_w_in, partials[0], received[0], chip, ROW_TILE, upper, None)
    w_out_upper = _adamw_reduced(
        "adamw_w_out_upper", w_out, m_w_out, v_w_out, partials[1], received[1], chip, W_OUT_SHARD, upper, None)
    dmod_all = slabs[:, :, :o].reshape(N_DEV, DEPTH, N_DEV, W_IN_SHARD)
    dmod_cols = lax.dynamic_index_in_dim(dmod_all, me, axis=2, keepdims=False).transpose(1, 0, 2) + token[0, 0]
    g_w_ada, d_w_ada, nm_w_ada, nv_w_ada = _adamw_w_ada(w_ada, m_w_ada, v_w_ada, c_act.T, dmod_cols)

    partials, received = _finish_exchange(
        chips_0, "grad_chips_finish_0", sems_0, partials, received, [nv_w_ada, w_in_upper[3], w_out_upper[3]])
    gather_pool = _all_gather_exchange(_reduce_w_pool(partials[2], received[2], chip), axis=1)
    sems_p, pool_rows, pool_landing, token_p = _start_exchange(gather_pool, "all_gather_grad_w_pool_start")
    g_w_in, d_w_in, nm_w_in, nv_w_in = _adamw_reduced(
        "adamw_w_in_0", w_in, m_w_in, v_w_in, partials[0], received[0], chip, ROW_TILE, (0, 1), w_in_upper, [token_p])
    g_w_out, d_w_out, nm_w_out, nv_w_out = _adamw_reduced(
        "adamw_w_out_0", w_out, m_w_out, v_w_out, partials[1], received[1], chip, W_OUT_SHARD, (0, 1), w_out_upper,
        [token_p])
    ends = [0, o, o + D_MODEL, o + 2 * D_MODEL, o + 2 * D_MODEL + POOL_W, SLAB_COLS]
    g_b_ada, g_g_pre, g_g_post, g_pool_scale, g_conv, loss = _sum_sources(
        slabs, list(zip(ends[:-1], ends[1:])), [token_p, nv_w_in, nv_w_out])
    loss = loss.reshape(())
    g_w_conv = lax.dynamic_slice_in_dim(g_conv.reshape(DEPTH, 3, CONV_W), me * conv_shard, conv_shard, axis=2)
    taps_first = lambda a: a.transpose(1, 0, 2)
    small = _adamw_small("adamw_small", [
        (b_ada, g_b_ada, m_b_ada, v_b_ada),
        (g_pre, g_g_pre, m_g_pre, v_g_pre),
        tuple(taps_first(a) for a in (w_conv, g_w_conv, m_w_conv, v_w_conv)),
        (pool_scale, g_pool_scale, m_pool_scale, v_pool_scale),
        (g_post, g_g_post, m_g_post, v_g_post),
    ])
    (d_b_ada, nm_b_ada, nv_b_ada, g_b_ada), (d_g_pre, nm_g_pre, nv_g_pre, g_g_pre), conv_steps, \
        (d_ps, nm_ps, nv_ps, g_pool_scale), (d_g_post, nm_g_post, nv_g_post, g_g_post) = small
    d_w_conv, nm_w_conv, nv_w_conv, g_w_conv = (taps_first(a) for a in conv_steps)
    _, (g_pool_all,) = _finish_exchange(
        gather_pool, "all_gather_grad_w_pool_finish", sems_p, pool_rows, pool_landing, [nv_w_in, nv_w_out, nv_g_post])
    ((d_w_pool, nm_w_pool, nv_w_pool, g_w_pool),) = _adamw_small(
        "adamw_w_pool", [(w_pool, g_pool_all.reshape(w_pool.shape), m_w_pool, v_w_pool)], by_layer=True)

    return (loss, grad_x,
            g_w_ada, g_b_ada, g_g_pre, g_w_in, g_w_conv, g_w_pool, g_pool_scale, g_w_out, g_g_post,
            d_w_ada, d_b_ada, d_g_pre, d_w_in, d_w_conv, d_w_pool, d_ps, d_w_out, d_g_post,
            nm_w_ada, nm_b_ada, nm_g_pre, nm_w_in, nm_w_conv, nm_w_pool, nm_ps, nm_w_out, nm_g_post,
            nv_w_ada, nv_b_ada, nv_g_pre, nv_w_in, nv_w_conv, nv_w_pool, nv_ps, nv_w_out, nv_g_post)
```

```python
from functools import partial

import jax
import jax.numpy as jnp
from jax import lax
from jax.experimental import pallas as pl
from jax.experimental.pallas import tpu as pltpu

F32 = jnp.float32
BF16 = jnp.bfloat16

D_MODEL = 1024
DEPTH = 4
CONV_W = 512
POOL_W = 512
POOL_WINDOWS = (2, 4, 8, 16)
GROUP_D = 128
IN_COLS = 4 * CONV_W + 2 * POOL_W
NORM_EPS = 1e-6

ADAM_LR = 0.001
ADAM_B1 = 0.9
ADAM_B2 = 0.999
ADAM_EPS = 1e-08
ADAM_WD = 0.01
ADAM_STEP = 10

N_DEV = 8
N_CHIP = 4
N_OTHER_CHIPS = N_CHIP - 1
MESH = pl.DeviceIdType.MESH
W_IN_SHARD = IN_COLS // N_DEV
W_OUT_SHARD = D_MODEL // N_DEV
POOL_ROWS = len(POOL_WINDOWS) * GROUP_D
POOL_SHARD = POOL_ROWS // N_DEV

SUBLANES = 8
LANES = 128
VMEM_LIMIT_BYTES = 56 * 1024 * 1024
ROW_TILE = 512
BWD_TILE = 256
GWIN_COLS = 768
GWOUT_COLS = 512
POOL_HALO = 16
CONV_HALO = SUBLANES

SLAB_COLS = 3 * D_MODEL + D_MODEL + D_MODEL + POOL_W + 3 * CONV_W

HBM = pl.BlockSpec(memory_space=pl.ANY)


def _params(**kw):
    return pltpu.CompilerParams(vmem_limit_bytes=VMEM_LIMIT_BYTES, **kw)


def _sigmoid(v):
    return 1.0 / (1.0 + jnp.exp(-v))


def _dot(a, b):
    return jnp.dot(a, b, preferred_element_type=F32)


def _dot_tn(a, b):
    return lax.dot_general(a, b, (((0,), (0,)), ((), ())), preferred_element_type=F32)


def _dot_nt(a, b):
    return lax.dot_general(a, b, (((1,), (1,)), ((), ())), preferred_element_type=F32)


def _rows_from_before(v, k):
    return pltpu.roll(v, k, 0)


def _rows_from_after(v, k):
    return pltpu.roll(v, v.shape[0] - k, 0)


def _window_counts(t0, rows):
    return (lax.broadcasted_iota(jnp.int32, (rows, 1), 0) + (t0 + 1)).astype(F32)


def _split_proj(p32):
    cw = CONV_W
    return (p32[:, 0 * cw:1 * cw], p32[:, 1 * cw:2 * cw], p32[:, 2 * cw:3 * cw], p32[:, 3 * cw:4 * cw],
            p32[:, 4 * cw:4 * cw + POOL_W], p32[:, 4 * cw + POOL_W:])


def _layer_spec(shape, layer):
    nd = len(shape)
    return pl.BlockSpec((None,) + tuple(shape[1:]), lambda i, _l=layer, _n=nd: (_l,) + (0,) * (_n - 1))


def _whole_spec(shape):
    return pl.BlockSpec(tuple(shape), lambda i, _n=len(shape): (0,) * _n, pipeline_mode=pl.Buffered(1))


def _mesh_position():
    return lax.axis_index("x"), lax.axis_index("y"), lax.axis_index("c")


def _block_id(x, y, c):
    return 4 * x + 2 * y + c


def _other_chips(x, y):
    return [(x ^ 1, y), (x, y ^ 1), (x ^ 1, y ^ 1)]


def _col_block(ref, blk):
    return ref.at[:, pl.ds(pl.multiple_of(blk * W_IN_SHARD, LANES), W_IN_SHARD)]


def _row_block(rows):
    def block(ref, blk):
        return ref.at[pl.ds(pl.multiple_of(blk * rows, rows), rows), :]
    return block


_BLOCK_OF = (_col_block, _row_block(W_OUT_SHARD), _row_block(POOL_SHARD))
_BLOCK_SHAPES = ((D_MODEL, W_IN_SHARD), (W_OUT_SHARD, D_MODEL), (POOL_SHARD, GROUP_D))


class _Exchange:
    def __init__(self, inputs, out_shapes, aliases, sem_shapes, make):
        self.inputs, self.out_shapes, self.aliases, self.sem_shapes, self.make = (
            list(inputs), list(out_shapes), dict(aliases), list(sem_shapes), make)


def _run_exchange(exchange, name):
    n_in, n_out = len(exchange.inputs), len(exchange.out_shapes)

    def body(*refs):
        start, finish = exchange.make(refs[:n_in], refs[n_in:n_in + n_out], refs[n_in + n_out:])
        start()
        finish()

    return pl.pallas_call(
        body, name=name, in_specs=[HBM] * n_in, out_specs=[HBM] * n_out, out_shape=exchange.out_shapes,
        scratch_shapes=exchange.sem_shapes, input_output_aliases=exchange.aliases, compiler_params=_params(),
    )(*exchange.inputs)


_SEM = pl.BlockSpec(memory_space=pltpu.SEMAPHORE)
_DATAFLOW = pltpu.SideEffectType.DATAFLOW_SIDE_EFFECTING


def _start_exchange(exchange, name, after=()):
    n_in, n_out, n_sem = len(exchange.inputs), len(exchange.out_shapes), len(exchange.sem_shapes)
    sources = [i for i in range(n_in) if i not in exchange.aliases]
    aliases = {i: n_sem + k for k, i in enumerate(sources)}
    aliases.update({i: n_sem + len(sources) + o for i, o in exchange.aliases.items()})

    def body(*refs):
        in_refs = refs[:n_in]
        outs = refs[n_in + len(after):]
        sems = outs[:n_sem]
        out_refs = outs[n_sem + len(sources):n_sem + len(sources) + n_out]
        exchange.make(in_refs, out_refs, sems)[0]()
        refs[-1][...] = jnp.zeros_like(refs[-1])

    outs = pl.pallas_call(
        body, name=name, in_specs=[HBM] * (n_in + len(after)),
        out_specs=[_SEM] * n_sem + [HBM] * (len(sources) + n_out) + [pl.BlockSpec(memory_space=pltpu.VMEM)],
        out_shape=(exchange.sem_shapes + [pltpu.HBM(exchange.inputs[i].shape, exchange.inputs[i].dtype) for i in sources]
                   + [pltpu.HBM(s.shape, s.dtype) for s in exchange.out_shapes]
                   + [jax.ShapeDtypeStruct((SUBLANES, LANES), F32)]),
        input_output_aliases=aliases, compiler_params=_params(has_side_effects=_DATAFLOW),
    )(*exchange.inputs, *after)
    return outs[:n_sem], outs[n_sem:n_sem + len(sources)], outs[n_sem + len(sources):-1], outs[-1]


def _finish_exchange(exchange, name, sems, sources, landing, after):
    n_src, n_out, n_sem = len(sources), len(landing), len(sems)
    n_in = len(exchange.inputs)
    source_at = [i for i in range(n_in) if i not in exchange.aliases]

    def body(*refs):
        src_refs, out_refs = refs[:n_src], refs[n_src:n_src + n_out]
        sem_refs = refs[n_src + n_out:n_src + n_out + n_sem]
        in_refs = [None] * n_in
        for k, i in enumerate(source_at):
            in_refs[i] = src_refs[k]
        for i, o in exchange.aliases.items():
            in_refs[i] = out_refs[o]
        exchange.make(in_refs, out_refs, sem_refs)[1]()

    arrays = list(sources) + list(landing)
    outs = pl.pallas_call(
        body, name=name, in_specs=[HBM] * len(arrays) + [_SEM] * n_sem + [HBM] * len(after),
        out_specs=[HBM] * len(arrays), out_shape=[pltpu.HBM(a.shape, a.dtype) for a in arrays],
        input_output_aliases={i: i for i in range(len(arrays))}, compiler_params=_params(has_side_effects=_DATAFLOW),
    )(*arrays, *sems, *after)
    return outs[:n_src], outs[n_src:]


N_GATHERED = 2
FIRST_COPIES = 1 + N_OTHER_CHIPS
_GATHERED_SHAPES = ((D_MODEL, IN_COLS), (D_MODEL, D_MODEL))
ALL_OTHER_CHIPS, NEIGHBOUR_CHIPS, DIAGONAL_CHIP = (0, 1, 2), (0, 1), (2,)


def _first_sem(layer, a, k):
    return (layer * N_GATHERED + a) * FIRST_COPIES + k


def _first_arrival(layer, a, j):
    return _first_sem(layer, a, 1 + j)


def _gather_copy(window_of, full_ref, blk, send_sem, recv_sem, to, src=None):
    window = window_of(full_ref, blk)
    return pltpu.make_async_remote_copy(
        src_ref=window if src is None else src, dst_ref=window, send_sem=send_sem, recv_sem=recv_sem,
        device_id=to, device_id_type=MESH)


def _first_copies(layer, shard_refs, full_refs, send_sems, recv_sems, local_sems, relayed):
    x, y, c = _mesh_position()
    me = _block_id(x, y, c)
    chips = [_other_chips(x, y)[j] for j in (NEIGHBOUR_CHIPS if relayed else ALL_OTHER_CHIPS)]
    own, remote = [], []
    for a in range(N_GATHERED):
        shard = shard_refs[a].at[layer]
        own.append(pltpu.make_async_copy(
            shard, _BLOCK_OF[a](full_refs[a], me), local_sems.at[layer * N_GATHERED + a]))
        targets = [(x, y, 1 - c)] + [(*chip, c) for chip in chips]
        remote += [_gather_copy(_BLOCK_OF[a], full_refs[a], me, send_sems.at[_first_sem(layer, a, k)],
                                recv_sems.at[_first_sem(layer, a, k)], to, src=shard)
                   for k, to in enumerate(targets)]
    return own, remote


def _gather_weights_start(first_layer, win_shards, wout_shards, after, relayed=False):
    n_layers = win_shards.shape[0]
    n_first = n_layers * N_GATHERED * FIRST_COPIES
    sem_shapes = [pltpu.SemaphoreType.DMA((n_first,)), pltpu.SemaphoreType.DMA((n_first,)),
                  pltpu.SemaphoreType.DMA((n_layers * N_GATHERED,))]
    shards = [win_shards, wout_shards]

    def body(win_sh, wout_sh, *rest):
        send_sems, recv_sems, local_sems, win_thru, wout_thru, *landing = rest[len(after):]
        for layer in range(n_layers):
            own, remote = _first_copies(layer, (win_sh, wout_sh), landing[N_GATHERED * layer:N_GATHERED * (layer + 1)],
                                        send_sems, recv_sems, local_sems, relayed)
            for cp in own + remote:
                cp.start()

    outs = pl.pallas_call(
        body, name=f"all_gather_weights_start_{first_layer}", in_specs=[HBM] * (2 + len(after)),
        out_specs=[_SEM] * 3 + [HBM] * (2 + n_layers * N_GATHERED),
        out_shape=(sem_shapes + [pltpu.HBM(s.shape, s.dtype) for s in shards]
                   + [pltpu.HBM(s, BF16) for _ in range(n_layers) for s in _GATHERED_SHAPES]),
        input_output_aliases={0: 3, 1: 4}, compiler_params=_params(has_side_effects=_DATAFLOW),
    )(*shards, *after)
    landing = outs[5:]
    return outs[:3], outs[3:5], [landing[N_GATHERED * l:N_GATHERED * (l + 1)] for l in range(n_layers)]


def _passed_on_copies(full_refs, send_sems, recv_sems, core_of_block, chips):
    x, y, c = _mesh_position()
    return [_gather_copy(_BLOCK_OF[a], full_refs[a], _block_id(*_other_chips(x, y)[j], core_of_block),
                         send_sems.at[a * N_OTHER_CHIPS + j], recv_sems.at[a * N_OTHER_CHIPS + j], (x, y, 1 - c))
            for a in range(N_GATHERED) for j in chips]


def _relayed_copies(full_refs, send_sems, recv_sems):
    x, y, c = _mesh_position()
    source, to = (x ^ (1 - c), y ^ c), (x ^ c, y ^ (1 - c))
    return [_gather_copy(_BLOCK_OF[a], full_refs[a], _block_id(*source, c), send_sems.at[a], recv_sems.at[a], (*to, c))
            for a in range(N_GATHERED)]


def _gather_weights_pass_on(name, chips, arrival_sems, arrival_sem_of, landing, after, relay=False):
    n = N_GATHERED * N_OTHER_CHIPS
    sem_shapes = [pltpu.SemaphoreType.DMA((n,))] * 2 + [pltpu.SemaphoreType.DMA((N_GATHERED,))] * (2 if relay else 0)

    def body(win_ref, wout_ref, arrivals, *rest):
        sems = rest[len(after):len(after) + len(sem_shapes)]
        x, y, c = _mesh_position()
        full_refs = (win_ref, wout_ref)
        passed = _passed_on_copies(full_refs, sems[0], sems[1], c, chips)
        relayed = _relayed_copies(full_refs, sems[2], sems[3]) if relay else []
        for a in range(N_GATHERED):
            for j in chips:
                sem = arrivals.at[arrival_sem_of(a, j)]
                _gather_copy(_BLOCK_OF[a], full_refs[a], _block_id(*_other_chips(x, y)[j], c), sem, sem,
                             (x, y, c)).wait_recv()
            for cp in relayed[a:a + 1] + passed[a * len(chips):(a + 1) * len(chips)]:
                cp.start()

    outs = pl.pallas_call(
        body, name=name, in_specs=[HBM] * N_GATHERED + [_SEM] + [HBM] * len(after),
        out_specs=[_SEM] * len(sem_shapes) + [HBM] * N_GATHERED,
        out_shape=sem_shapes + [pltpu.HBM(a.shape, a.dtype) for a in landing],
        input_output_aliases={a: len(sem_shapes) + a for a in range(N_GATHERED)},
        compiler_params=_params(has_side_effects=_DATAFLOW),
    )(*landing, arrival_sems, *after)
    return outs[:len(sem_shapes)], outs[len(sem_shapes):]


def _gather_weights_finish(layer, index, first_sems, passed, shards, landing, relay_send_sems=None):
    relayed = relay_send_sems is not None
    passed_sems = [sem for (send, recv), _ in passed for sem in (send, recv)] + ([relay_send_sems] if relayed else [])

    def body(win_ref, wout_ref, first_send, first_recv, local_sems, *rest):
        sems, (win_sh, wout_sh) = rest[:len(passed_sems)], rest[len(passed_sems):len(passed_sems) + 2]
        x, y, c = _mesh_position()
        full_refs = (win_ref, wout_ref)
        own, sent = _first_copies(index, (win_sh, wout_sh), full_refs, first_send, first_recv, local_sems, relayed)
        for a in range(N_GATHERED):
            sem = _first_sem(index, a, 0)
            _gather_copy(_BLOCK_OF[a], full_refs[a], _block_id(x, y, 1 - c), first_recv.at[sem], first_recv.at[sem],
                         (x, y, c)).wait_recv()
        for p, (_, chips) in enumerate(passed):
            for cp in _passed_on_copies(full_refs, sems[2 * p], sems[2 * p + 1], 1 - c, chips):
                cp.wait_recv()
            sent += _passed_on_copies(full_refs, sems[2 * p], sems[2 * p + 1], c, chips)
        if relayed:
            sent += _relayed_copies(full_refs, sems[-1], sems[-1])
        for cp in sent:
            cp.wait_send()
        for cp in own:
            cp.wait()

    return pl.pallas_call(
        body, name=f"all_gather_weights_finish_{layer}",
        in_specs=[HBM] * N_GATHERED + [_SEM] * (3 + len(passed_sems)) + [HBM] * 2,
        out_specs=[HBM] * N_GATHERED, out_shape=[pltpu.HBM(a.shape, a.dtype) for a in landing],
        input_output_aliases={a: a for a in range(N_GATHERED)}, compiler_params=_params(has_side_effects=_DATAFLOW),
    )(*landing, *first_sems, *passed_sems, *shards)


ALL_KINDS = (0, 1, 2)


def _sibling_exchange(grads, kinds):
    n_arr = len(grads)

    def make(in_refs, out_refs, sems):
        send_sems, recv_sems = sems
        x, y, c = _mesh_position()
        copies = [pltpu.make_async_remote_copy(
            src_ref=_BLOCK_OF[kinds[a]](in_refs[a], 2 * q + (1 - c)), dst_ref=out_refs[a].at[q],
            send_sem=send_sems.at[a * N_CHIP + q], recv_sem=recv_sems.at[a * N_CHIP + q],
            device_id=(x, y, 1 - c), device_id_type=MESH)
            for a in range(n_arr) for q in range(N_CHIP)]

        def start():
            for cp in copies:
                cp.start()

        def finish():
            for cp in copies:
                cp.wait_recv()
            for cp in copies:
                cp.wait_send()

        return start, finish

    return _Exchange(
        grads, [jax.ShapeDtypeStruct((N_CHIP,) + _BLOCK_SHAPES[k], BF16) for k in kinds], {},
        [pltpu.SemaphoreType.DMA((n_arr * N_CHIP,)), pltpu.SemaphoreType.DMA((n_arr * N_CHIP,))], make)


def _chips_exchange(layer, partials, received, kinds):
    n_arr = len(partials)

    def make(in_refs, out_refs, sems):
        send_sems, recv_sems = sems
        x, y, c = _mesh_position()
        copies = [pltpu.make_async_remote_copy(
            src_ref=in_refs[a].at[2 * qx + qy, layer], dst_ref=out_refs[a].at[j, layer],
            send_sem=send_sems.at[a * N_OTHER_CHIPS + j], recv_sem=recv_sems.at[a * N_OTHER_CHIPS + j],
            device_id=(qx, qy, c), device_id_type=MESH)
            for a in range(n_arr) for j, (qx, qy) in enumerate(_other_chips(x, y))]

        def start():
            for cp in copies:
                cp.start()

        def finish():
            for cp in copies:
                cp.wait_recv()
            for cp in copies:
                cp.wait_send()

        return start, finish

    inputs = list(partials)
    aliases = {}
    if received is not None:
        inputs += list(received)
        aliases = {n_arr + a: a for a in range(n_arr)}
    return _Exchange(
        inputs, [jax.ShapeDtypeStruct((N_OTHER_CHIPS, DEPTH) + _BLOCK_SHAPES[k], BF16) for k in kinds], aliases,
        [pltpu.SemaphoreType.DMA((n_arr * N_OTHER_CHIPS,)), pltpu.SemaphoreType.DMA((n_arr * N_OTHER_CHIPS,))], make)


def _all_gather_exchange(v, axis=0):
    def make(in_refs, out_refs, sems):
        send_sems, recv_sems, local_sem = sems
        x, y, c = _mesh_position()
        me = _block_id(x, y, c)
        block = lambda blk: out_refs[0].at[(slice(None),) * axis + (blk,)]
        own = pltpu.make_async_copy(in_refs[0], block(me), local_sem.at[0])
        sends, arrivals = [], []
        for k in range(1, N_DEV):
            px, py, pc = x ^ ((k >> 2) & 1), y ^ ((k >> 1) & 1), c ^ (k & 1)
            sends.append(pltpu.make_async_remote_copy(
                src_ref=in_refs[0], dst_ref=block(me), send_sem=send_sems.at[k - 1],
                recv_sem=recv_sems.at[k - 1], device_id=(px, py, pc), device_id_type=MESH))
            arrivals.append(pltpu.make_async_remote_copy(
                src_ref=in_refs[0], dst_ref=block(_block_id(px, py, pc)), send_sem=send_sems.at[k - 1],
                recv_sem=recv_sems.at[k - 1], device_id=(x, y, c), device_id_type=MESH))

        def start():
            for cp in [own] + sends:
                cp.start()

        def finish():
            for cp in arrivals:
                cp.wait_recv()
            for cp in sends:
                cp.wait_send()
            own.wait()

        return start, finish

    return _Exchange(
        [v], [jax.ShapeDtypeStruct(v.shape[:axis] + (N_DEV,) + v.shape[axis:], v.dtype)], {},
        [pltpu.SemaphoreType.DMA((N_DEV - 1,)), pltpu.SemaphoreType.DMA((N_DEV - 1,)),
         pltpu.SemaphoreType.DMA((1,))], make)


def _host(exchange, args, in_specs, out_shape, out_specs, scratch):
    n_own = (len(args), len(out_shape), len(scratch))
    if exchange is None:
        return {}, lambda refs: (refs, None)
    n_ex = (len(exchange.inputs), len(exchange.out_shapes), len(exchange.sem_shapes))
    aliases = {n_own[0] + i: n_own[1] + o for i, o in exchange.aliases.items()}
    args += exchange.inputs
    in_specs += [HBM] * n_ex[0]
    out_shape += exchange.out_shapes
    out_specs += [HBM] * n_ex[1]
    scratch += exchange.sem_shapes

    def split(refs):
        own, theirs, at = [], [], 0
        for mine, ex in zip(n_own, n_ex):
            own += refs[at:at + mine]
            theirs.append(refs[at + mine:at + mine + ex])
            at += mine + ex
        return own, exchange.make(*theirs)

    return aliases, split


def _all_gather_small(v, name, after=()):
    vmem = pl.BlockSpec(memory_space=pltpu.VMEM)

    def body(v_ref, *rest):
        out_ref, send_sems, recv_sems = rest[len(after):]
        x, y, c = _mesh_position()
        me = _block_id(x, y, c)
        out_ref[me] = v_ref[...]
        sends = []
        for k in range(1, N_DEV):
            px, py, pc = x ^ ((k >> 2) & 1), y ^ ((k >> 1) & 1), c ^ (k & 1)
            send = pltpu.make_async_remote_copy(
                src_ref=v_ref, dst_ref=out_ref.at[me], send_sem=send_sems.at[k - 1], recv_sem=recv_sems.at[k - 1],
                device_id=(px, py, pc), device_id_type=MESH)
            send.start()
            sends.append((send, _block_id(px, py, pc)))
        for k, (send, peer) in enumerate(sends):
            pltpu.make_async_remote_copy(
                src_ref=v_ref, dst_ref=out_ref.at[peer], send_sem=send_sems.at[k], recv_sem=recv_sems.at[k],
                device_id=(x, y, c), device_id_type=MESH).wait_recv()
        for send, _ in sends:
            send.wait_send()

    return pl.pallas_call(
        body, name=name, in_specs=[vmem] + [HBM] * len(after), out_specs=vmem,
        out_shape=jax.ShapeDtypeStruct((N_DEV,) + v.shape, v.dtype),
        scratch_shapes=[pltpu.SemaphoreType.DMA((N_DEV - 1,)), pltpu.SemaphoreType.DMA((N_DEV - 1,))],
        compiler_params=_params(),
    )(v, *after)


def _forward_layer(layer, x, vec, cps, wpool, win, wout, target=None):
    t_len = x.shape[0]
    n_tiles = t_len // ROW_TILE
    row = lambda cols: pl.BlockSpec((ROW_TILE, cols), lambda i: (i, 0))
    widths = (D_MODEL, 2 * CONV_W, 3 * CONV_W, 4 * POOL_W)
    head = target is not None

    def body(x_ref, vec_ref, cps_ref, wpool_ref, win_ref, wout_ref, *rest):
        target_ref = rest[0] if head else None
        xo_ref, y_ref, h_ref, ycat_ref, uc_ref, fa_ref, fp_ref = rest[head:head + 7]
        loss_ref = rest[head + 7] if head else None
        zc_ref, pc_ref = rest[-2:]
        i = pl.program_id(0)

        @pl.when(i == 0)
        def _():
            zc_ref[...] = jnp.zeros_like(zc_ref)
            pc_ref[...] = jnp.zeros_like(pc_ref)
            if head:
                loss_ref[...] = jnp.zeros_like(loss_ref)

        x_t = x_ref[...]
        shift, scale, gate = vec_ref[0:1, :], vec_ref[1:2, :], vec_ref[2:3, :]
        g_pre, g_post = vec_ref[3:4, :], vec_ref[4:5, :]
        w0, w1, w2, ps = cps_ref[0:1, :], cps_ref[1:2, :], cps_ref[2:3, :], cps_ref[3:4, :]
        rx = lax.rsqrt(jnp.mean(x_t * x_t, axis=-1, keepdims=True) + NORM_EPS)
        h = (x_t * rx) * g_pre * (1.0 + scale) + shift
        h_ref[...] = h.astype(BF16)
        proj = _dot(h.astype(BF16), win_ref[...])
        u_a, b_a, c_a, g_a, u_p, g_p = _split_proj(proj)
        uc_ref[...] = jnp.concatenate([u_a, c_a], axis=1).astype(BF16)

        z = c_a * u_a
        zcat = jnp.concatenate([zc_ref[...], z], axis=0)
        zc_ref[...] = z[ROW_TILE - CONV_HALO:]
        conv = (w0 * _rows_from_before(zcat, 2)[CONV_HALO:] + w1 * _rows_from_before(zcat, 1)[CONV_HALO:] + w2 * z)
        sig_a = _sigmoid(g_a)
        silu_a = g_a * sig_a
        b_conv = b_a * conv
        y_a = b_conv * silu_a
        fa_ref[...] = jnp.concatenate(
            [silu_a * conv, silu_a * b_a, b_conv * (sig_a + silu_a * (1.0 - sig_a))], axis=1).astype(BF16)

        pcat = jnp.concatenate([pc_ref[...], u_p], axis=0)
        pc_ref[...] = u_p[ROW_TILE - POOL_HALO:]
        counts = _window_counts(i * ROW_TILE, ROW_TILE)
        pooled, mixed = [], []
        for g, w in enumerate(POOL_WINDOWS):
            cols = slice(g * GROUP_D, (g + 1) * GROUP_D)
            s = pcat[:, cols]
            step = 1
            while step < w:
                s = s + _rows_from_before(s, step)
                step *= 2
            pooled_g = (s[POOL_HALO:] * (1.0 / jnp.minimum(counts, float(w))) - u_p[:, cols]).astype(BF16)
            pooled.append(pooled_g)
            mixed.append(_dot(pooled_g, wpool_ref[g]))
        mixed = jnp.concatenate(mixed, axis=1)
        sig_p = _sigmoid(g_p)
        silu_p = g_p * sig_p
        mixed_ps = mixed * ps
        y_p = mixed_ps * silu_p
        fp_ref[...] = jnp.concatenate(
            [(ps * silu_p).astype(BF16), (mixed_ps * (sig_p + silu_p * (1.0 - sig_p))).astype(BF16),
             (silu_p * mixed).astype(BF16)] + pooled, axis=1)

        ycat = jnp.concatenate([y_a, y_p], axis=1)
        ycat_ref[...] = ycat.astype(BF16)
        y_b = _dot(ycat.astype(BF16), wout_ref[...]).astype(BF16)
        y_ref[...] = y_b
        y_t = y_b.astype(F32)
        ry = lax.rsqrt(jnp.mean(y_t * y_t, axis=-1, keepdims=True) + NORM_EPS)
        x_next = x_t + gate * (y_t * ry * g_post)
        if head:
            err = x_next - target_ref[...]
            xo_ref[...] = err * (1.0 / D_MODEL)
            loss_ref[...] += jnp.sum(err * err) * (0.5 / D_MODEL)
        else:
            xo_ref[...] = x_next

    tile = (SUBLANES, LANES)
    return pl.pallas_call(
        body, name=f"forward_layer_{layer}", grid=(n_tiles,),
        in_specs=[row(D_MODEL), _layer_spec(vec.shape, layer), _layer_spec(cps.shape, layer),
                  _layer_spec(wpool.shape, layer), _whole_spec(win.shape), _whole_spec(wout.shape)]
        + [row(D_MODEL)] * head,
        out_specs=[row(D_MODEL), row(D_MODEL), row(D_MODEL), row(D_MODEL)] + [row(w) for w in widths[1:]]
        + [_whole_spec(tile)] * head,
        out_shape=[jax.ShapeDtypeStruct((t_len, D_MODEL), F32), jax.ShapeDtypeStruct((t_len, D_MODEL), BF16),
                   jax.ShapeDtypeStruct((t_len, D_MODEL), BF16), jax.ShapeDtypeStruct((t_len, D_MODEL), BF16)]
        + [jax.ShapeDtypeStruct((t_len, w), BF16) for w in widths[1:]] + [jax.ShapeDtypeStruct(tile, F32)] * head,
        scratch_shapes=[pltpu.VMEM((CONV_HALO, CONV_W), F32), pltpu.VMEM((POOL_HALO, POOL_W), F32)],
        compiler_params=_params(dimension_semantics=("arbitrary",)),
    )(x, vec, cps, wpool, win, wout, *([target] * head))


def _backward_layer(layer, dxo, y, x, uc, fa, fp, vec, cps, wpool, wout, win, after):
    t_len = dxo.shape[0]
    n_tiles = t_len // BWD_TILE
    halo_per_tile = BWD_TILE // POOL_HALO
    rev = lambda cols: pl.BlockSpec((BWD_TILE, cols), lambda i: (n_tiles - 1 - i, 0))
    halo_spec = pl.BlockSpec(
        (POOL_HALO, 2 * CONV_W), lambda i: (jnp.maximum((n_tiles - 1 - i) * halo_per_tile - 1, 0), 0))
    gwpool_shape = (len(POOL_WINDOWS), GROUP_D, GROUP_D)

    def body(dxo_ref, y_ref, x_ref, uc_ref, uch_ref, fa_ref, fp_ref, vec_ref, cps_ref, wpool_ref, wout_ref, win_ref,
             *rest):
        dx_ref, dproj_ref, dy_ref, gwpool_ref, dcps_ref, dvec_ref, gwpool_acc, dcc_ref, qc_ref = rest[len(after):]
        i = pl.program_id(0)
        tile = n_tiles - 1 - i

        @pl.when(i == 0)
        def _():
            gwpool_acc[...] = jnp.zeros_like(gwpool_acc)
            dcps_ref[...] = jnp.zeros_like(dcps_ref)
            dvec_ref[...] = jnp.zeros_like(dvec_ref)
            dcc_ref[...] = jnp.zeros_like(dcc_ref)
            qc_ref[...] = jnp.zeros_like(qc_ref)

        shift, scale, gate = vec_ref[0:1, :], vec_ref[1:2, :], vec_ref[2:3, :]
        g_pre, g_post = vec_ref[3:4, :], vec_ref[4:5, :]
        w0, w1, w2 = cps_ref[0:1, :], cps_ref[1:2, :], cps_ref[2:3, :]

        dxo_t = dxo_ref[...]
        y_t = y_ref[...].astype(F32)
        ry = lax.rsqrt(jnp.mean(y_t * y_t, axis=-1, keepdims=True) + NORM_EPS)
        yh = y_t * ry
        dvec_ref[2:3, :] += jnp.sum(dxo_t * yh, axis=0, keepdims=True)
        dyh = dxo_t * (gate * g_post)
        dy_b = (ry * (dyh - yh * jnp.mean(dyh * yh, axis=-1, keepdims=True))).astype(BF16)
        dy_ref[...] = dy_b
        dycat = _dot_nt(dy_b, wout_ref[...])
        dy_a, dy_p = dycat[:, :CONV_W], dycat[:, CONV_W:]

        fa_t = fa_ref[...].astype(F32)
        db_a = dy_a * fa_t[:, :CONV_W]
        dconv = dy_a * fa_t[:, CONV_W:2 * CONV_W]
        dg_a = dy_a * fa_t[:, 2 * CONV_W:]
        uc_t = uc_ref[...].astype(F32)
        u_a, c_a = uc_t[:, :CONV_W], uc_t[:, CONV_W:]
        halo = jnp.where(tile > 0, uch_ref[...].astype(F32), 0.0)[POOL_HALO - CONV_HALO:]
        z = c_a * u_a
        zcat = jnp.concatenate([halo[:, CONV_W:] * halo[:, :CONV_W], z], axis=0)
        z1 = _rows_from_before(zcat, 1)[CONV_HALO:]
        z2 = _rows_from_before(zcat, 2)[CONV_HALO:]
        dccat = jnp.concatenate([dconv, dcc_ref[...]], axis=0)
        dc1 = _rows_from_after(dccat, 1)[:BWD_TILE]
        dc2 = _rows_from_after(dccat, 2)[:BWD_TILE]
        dz = w2 * dconv + w1 * dc1 + w0 * dc2
        dcc_ref[...] = dconv[:CONV_HALO]
        dcps_ref[0:1, :] += jnp.sum(dconv * z2, axis=0, keepdims=True)
        dcps_ref[1:2, :] += jnp.sum(dconv * z1, axis=0, keepdims=True)
        dcps_ref[2:3, :] += jnp.sum(dconv * z, axis=0, keepdims=True)
        du_a = dz * c_a
        dc_a = dz * u_a

        dmixed = (dy_p * fp_ref[:, :POOL_W].astype(F32)).astype(BF16)
        dg_p = dy_p * fp_ref[:, POOL_W:2 * POOL_W].astype(F32)
        dcps_ref[3:4, :] += jnp.sum(dy_p * fp_ref[:, 2 * POOL_W:3 * POOL_W].astype(F32), axis=0, keepdims=True)
        counts = _window_counts(tile * BWD_TILE, BWD_TILE)
        du_p, q_head = [], []
        for g, w in enumerate(POOL_WINDOWS):
            cols = slice(g * GROUP_D, (g + 1) * GROUP_D)
            dm_g = dmixed[:, cols]
            dpooled_g = _dot_nt(dm_g, wpool_ref[g])
            gwpool_acc[g] += _dot_tn(fp_ref[:, 3 * POOL_W + g * GROUP_D:3 * POOL_W + (g + 1) * GROUP_D], dm_g)
            q_g = dpooled_g * (1.0 / jnp.minimum(counts, float(w)))
            q_head.append(q_g[:POOL_HALO])
            s = jnp.concatenate([q_g, qc_ref[:, cols]], axis=0)
            step = 1
            while step < w:
                s = s + _rows_from_after(s, step)
                step *= 2
            du_p.append(s[:BWD_TILE] - dpooled_g)
        qc_ref[...] = jnp.concatenate(q_head, axis=1)
        dproj_b = jnp.concatenate([du_a, db_a, dc_a, dg_a] + du_p + [dg_p], axis=1).astype(BF16)
        dproj_ref[...] = dproj_b

        x_t = x_ref[...]
        rx = lax.rsqrt(jnp.mean(x_t * x_t, axis=-1, keepdims=True) + NORM_EPS)
        xn = x_t * rx
        mod_scale = 1.0 + scale
        dh = _dot_nt(dproj_b, win_ref[...])
        dvec_ref[0:1, :] += jnp.sum(dh, axis=0, keepdims=True)
        dvec_ref[1:2, :] += jnp.sum(dh * xn, axis=0, keepdims=True)
        dxn = dh * (g_pre * mod_scale)
        dx_ref[...] = dxo_t + rx * (dxn - xn * jnp.mean(dxn * xn, axis=-1, keepdims=True))

        @pl.when(i == n_tiles - 1)
        def _():
            gwpool_ref[...] = gwpool_acc[...].astype(BF16)
            sum_dh_xn, sum_dxo_yh = dvec_ref[1:2, :], dvec_ref[2:3, :]
            dvec_ref[1:2, :] = sum_dh_xn * g_pre
            dvec_ref[3:4, :] = sum_dh_xn * mod_scale
            dvec_ref[2:3, :] = sum_dxo_yh * g_post
            dvec_ref[4:5, :] = sum_dxo_yh * gate

    return pl.pallas_call(
        body, name=f"backward_layer_{layer}", grid=(n_tiles,),
        in_specs=[rev(D_MODEL), rev(D_MODEL), rev(D_MODEL), rev(2 * CONV_W), halo_spec, rev(3 * CONV_W),
                  rev(4 * POOL_W), _layer_spec(vec.shape, layer), _layer_spec(cps.shape, layer),
                  _layer_spec(wpool.shape, layer), _whole_spec(wout.shape), _whole_spec(win.shape)]
        + [HBM] * len(after),
        out_specs=[rev(D_MODEL), rev(IN_COLS), rev(D_MODEL), _whole_spec(gwpool_shape),
                   _whole_spec((SUBLANES, CONV_W)), _whole_spec((SUBLANES, D_MODEL))],
        out_shape=[jax.ShapeDtypeStruct((t_len, D_MODEL), F32), jax.ShapeDtypeStruct((t_len, IN_COLS), BF16),
                   jax.ShapeDtypeStruct((t_len, D_MODEL), BF16), jax.ShapeDtypeStruct(gwpool_shape, BF16),
                   jax.ShapeDtypeStruct((SUBLANES, CONV_W), F32), jax.ShapeDtypeStruct((SUBLANES, D_MODEL), F32)],
        scratch_shapes=[pltpu.VMEM(gwpool_shape, F32), pltpu.VMEM((CONV_HALO, CONV_W), F32),
                        pltpu.VMEM((POOL_HALO, POOL_W), F32)],
        compiler_params=_params(dimension_semantics=("arbitrary",)),
    )(dxo, y, x, uc, uc, fa, fp, vec, cps, wpool, wout, win, *after)


def _weight_grads(layer, h, dproj, ycat, dy, exchange, after=()):
    t_len = dy.shape[0]
    n_in, n_out = IN_COLS // GWIN_COLS, D_MODEL // GWOUT_COLS
    args = [h, dproj, ycat, dy, *after]
    in_specs = [_whole_spec(h.shape),
                pl.BlockSpec((t_len, GWIN_COLS), lambda s: (0, jnp.minimum(s, n_in - 1))),
                _whole_spec(ycat.shape),
                pl.BlockSpec((t_len, GWOUT_COLS), lambda s: (0, jnp.maximum(s - n_in, 0)))] + [HBM] * len(after)
    out_shape = [jax.ShapeDtypeStruct((D_MODEL, IN_COLS), BF16), jax.ShapeDtypeStruct((D_MODEL, D_MODEL), BF16)]
    out_specs = [pl.BlockSpec((D_MODEL, GWIN_COLS), lambda s: (0, jnp.minimum(s, n_in - 1))),
                 pl.BlockSpec((D_MODEL, GWOUT_COLS), lambda s: (0, jnp.maximum(s - n_in, 0)))]
    scratch = []
    aliases, split = _host(exchange, args, in_specs, out_shape, out_specs, scratch)

    def body(*refs):
        (h_ref, dproj_ref, ycat_ref, dy_ref, *_, gwin_ref, gwout_ref), hosted = split(refs)
        s = pl.program_id(0)
        if hosted is not None:
            pl.when(s == 0)(hosted[0])

        @pl.when(s < n_in)
        def _():
            gwin_ref[...] = _dot_tn(h_ref[...], dproj_ref[...]).astype(BF16)

        @pl.when(s >= n_in)
        def _():
            gwout_ref[...] = _dot_tn(ycat_ref[...], dy_ref[...]).astype(BF16)

        if hosted is not None:
            pl.when(s == n_in + n_out - 1)(hosted[1])

    return pl.pallas_call(
        body, name=f"weight_grads_{layer}", grid=(n_in + n_out,), in_specs=in_specs, out_specs=out_specs,
        out_shape=out_shape, scratch_shapes=scratch, input_output_aliases=aliases,
        compiler_params=_params(dimension_semantics=("arbitrary",)),
    )(*args)


def _add_sibling_blocks(name, layer, grads, received, core, partials, kinds):
    n_arr = len(grads)

    def body(core_ref, *refs):
        mine, theirs, outs = refs[:n_arr], refs[n_arr:2 * n_arr], refs[-n_arr:]
        for a in range(n_arr):
            outs[a][...] = (mine[a][...].astype(F32) + theirs[a][...].astype(F32)).astype(BF16)

    own_of_kind = [
        pl.BlockSpec((D_MODEL, W_IN_SHARD), lambda q, core_ref: (0, 2 * q + core_ref[0])),
        pl.BlockSpec((W_OUT_SHARD, D_MODEL), lambda q, core_ref: (2 * q + core_ref[0], 0)),
        pl.BlockSpec((POOL_SHARD, GROUP_D), lambda q, core_ref: (2 * q + core_ref[0], 0)),
    ]
    shapes = [_BLOCK_SHAPES[k] for k in kinds]
    recv_specs = [pl.BlockSpec((None,) + s, lambda q, core_ref: (q, 0, 0)) for s in shapes]
    out_specs = [pl.BlockSpec((None, None) + s, lambda q, core_ref: (q, layer, 0, 0)) for s in shapes]
    args = [core, *grads, *received]
    in_specs = [own_of_kind[k] for k in kinds] + recv_specs
    aliases = {}
    if partials is not None:
        aliases = {len(args) + a: a for a in range(n_arr)}
        args += list(partials)
        in_specs += [HBM] * n_arr
    return pl.pallas_call(
        body, name=name,
        grid_spec=pltpu.PrefetchScalarGridSpec(
            num_scalar_prefetch=1, grid=(N_CHIP,), in_specs=in_specs, out_specs=out_specs),
        out_shape=[jax.ShapeDtypeStruct((N_CHIP, DEPTH) + s, BF16) for s in shapes],
        input_output_aliases=aliases,
        compiler_params=_params(dimension_semantics=("arbitrary",)),
    )(*args)


def _modulation_columns(c_all, w_ada):
    def body(c_ref, w_ref, cact_ref, out_ref):
        c_t = c_ref[...]
        c_act = c_t * _sigmoid(c_t)
        cact_ref[...] = c_act
        out_ref[...] = jnp.dot(c_act, w_ref[...], preferred_element_type=F32, precision=lax.Precision.HIGHEST)

    return pl.pallas_call(
        body, name="modulation_columns", grid=(DEPTH,),
        in_specs=[pl.BlockSpec((N_DEV, D_MODEL), lambda l: (0, 0)),
                  pl.BlockSpec((None, D_MODEL, W_IN_SHARD), lambda l: (l, 0, 0))],
        out_specs=[pl.BlockSpec((N_DEV, D_MODEL), lambda l: (0, 0)),
                   pl.BlockSpec((N_DEV, W_IN_SHARD), lambda l: (0, l))],
        out_shape=[jax.ShapeDtypeStruct((N_DEV, D_MODEL), F32),
                   jax.ShapeDtypeStruct((N_DEV, DEPTH * W_IN_SHARD), F32)],
        compiler_params=_params(dimension_semantics=("arbitrary",)),
    )(c_all, w_ada)


def _adamw(w, g, m, v):
    m_new = ADAM_B1 * m + (1.0 - ADAM_B1) * g
    v_new = ADAM_B2 * v + (1.0 - ADAM_B2) * (g * g)
    m_hat = m_new / (1.0 - ADAM_B1 ** ADAM_STEP)
    v_hat = v_new / (1.0 - ADAM_B2 ** ADAM_STEP)
    delta = -ADAM_LR * (m_hat / (jnp.sqrt(v_hat) + ADAM_EPS) + ADAM_WD * w)
    return delta, m_new, v_new


def _adamw_w_ada(w, m, v, c_act_t, dmod_cols):
    def body(w_ref, m_ref, v_ref, ct_ref, dm_ref, g_ref, d_ref, mo_ref, vo_ref):
        g = ct_ref[:, 0:1] * dm_ref[0:1, :]
        for b in range(1, N_DEV):
            g = g + ct_ref[:, b:b + 1] * dm_ref[b:b + 1, :]
        g_ref[...] = g
        d_ref[...], mo_ref[...], vo_ref[...] = _adamw(w_ref[...], g, m_ref[...], v_ref[...])

    big = pl.BlockSpec((None, D_MODEL, W_IN_SHARD), lambda l: (l, 0, 0))
    return pl.pallas_call(
        body, name="adamw_w_ada", grid=(DEPTH,),
        in_specs=[big, big, big, pl.BlockSpec((D_MODEL, N_DEV), lambda l: (0, 0)),
                  pl.BlockSpec((None, N_DEV, W_IN_SHARD), lambda l: (l, 0, 0))],
        out_specs=[big] * 4, out_shape=[jax.ShapeDtypeStruct(w.shape, F32)] * 4,
        compiler_params=_params(dimension_semantics=("arbitrary",)),
    )(w, m, v, c_act_t, dmod_cols)


def _sum_chip_partials(own_ref, recv_ref):
    g = own_ref[...].astype(F32)
    for j in range(N_OTHER_CHIPS):
        g = g + recv_ref[j].astype(F32)
    return g


def _partial_specs(row_tile, cols, first_layer=0):
    own = pl.BlockSpec((None, None, row_tile, cols), lambda l, r, chip_ref: (chip_ref[0], first_layer + l, r, 0))
    recv = pl.BlockSpec((N_OTHER_CHIPS, None, row_tile, cols), lambda l, r, chip_ref: (0, first_layer + l, r, 0))
    return own, recv


def _adamw_reduced(name, w, m, v, partial, received, chip, row_tile, layers, continued, after=()):
    depth, rows, cols = w.shape
    first, stop = layers

    def body(chip_ref, w_ref, m_ref, v_ref, own_ref, recv_ref, *rest):
        g_ref, d_ref, mo_ref, vo_ref = rest[-4:]
        g = _sum_chip_partials(own_ref, recv_ref)
        g_ref[...] = g
        d_ref[...], mo_ref[...], vo_ref[...] = _adamw(w_ref[...], g, m_ref[...], v_ref[...])

    blk = pl.BlockSpec((None, row_tile, cols), lambda l, r, chip_ref: (first + l, r, 0))
    args = [chip, w, m, v, partial, received]
    in_specs = [blk, blk, blk, *_partial_specs(row_tile, cols, first)]
    aliases = {}
    if continued is not None:
        aliases = {len(args) + k: k for k in range(4)}
        args += list(continued)
        in_specs += [HBM] * 4
    args += after
    in_specs += [HBM] * len(after)
    return pl.pallas_call(
        body, name=name,
        grid_spec=pltpu.PrefetchScalarGridSpec(
            num_scalar_prefetch=1, grid=(stop - first, rows // row_tile), in_specs=in_specs, out_specs=[blk] * 4),
        out_shape=[jax.ShapeDtypeStruct(w.shape, F32)] * 4, input_output_aliases=aliases,
        compiler_params=_params(dimension_semantics=("arbitrary", "arbitrary")),
    )(*args)


def _reduce_w_pool(partial, received, chip):
    def body(chip_ref, own_ref, recv_ref, g_ref):
        g_ref[...] = _sum_chip_partials(own_ref, recv_ref)

    return pl.pallas_call(
        body, name="reduce_w_pool",
        grid_spec=pltpu.PrefetchScalarGridSpec(
            num_scalar_prefetch=1, grid=(DEPTH, 1), in_specs=list(_partial_specs(POOL_SHARD, GROUP_D)),
            out_specs=pl.BlockSpec((None, POOL_SHARD, GROUP_D), lambda l, r, chip_ref: (l, 0, 0))),
        out_shape=jax.ShapeDtypeStruct((DEPTH, POOL_SHARD, GROUP_D), F32),
        compiler_params=_params(dimension_semantics=("arbitrary", "arbitrary")),
    )(chip, partial, received)


def _adamw_small(name, params):
    n = len(params)

    def body(*refs):
        ins, outs = refs[:4 * n], refs[4 * n:]
        for p in range(n):
            w_ref, g_ref, m_ref, v_ref = ins[4 * p:4 * p + 4]
            d_ref, mo_ref, vo_ref, go_ref = outs[4 * p:4 * p + 4]
            d_ref[...], mo_ref[...], vo_ref[...] = _adamw(w_ref[...], g_ref[...], m_ref[...], v_ref[...])
            go_ref[...] = g_ref[...]

    vmem = pl.BlockSpec(memory_space=pltpu.VMEM)
    flat = [a for group in params for a in group]
    outs = pl.pallas_call(
        body, name=name, in_specs=[vmem] * len(flat), out_specs=[vmem] * len(flat),
        out_shape=[jax.ShapeDtypeStruct(a.shape, F32) for a in flat], compiler_params=_params(),
    )(*flat)
    return [tuple(outs[4 * p:4 * p + 4]) for p in range(n)]


def _sum_sources(slabs, columns, after):
    def body(s_ref, *rest):
        outs = rest[len(after):]
        acc = s_ref[0]
        for b in range(1, N_DEV):
            acc = acc + s_ref[b]
        for (start, stop), o_ref in zip(columns, outs):
            o_ref[...] = acc[:, start:stop]
        outs[-1][...] = acc[0:1, SLAB_COLS:SLAB_COLS + 1]

    vmem = pl.BlockSpec(memory_space=pltpu.VMEM)
    out_shape = [jax.ShapeDtypeStruct((slabs.shape[1], stop - start), F32) for start, stop in columns]
    out_shape.append(jax.ShapeDtypeStruct((1, 1), F32))
    return pl.pallas_call(
        body, name="sum_small_grads", in_specs=[vmem] + [HBM] * len(after), out_specs=[vmem] * len(out_shape),
        out_shape=out_shape, compiler_params=_params(),
    )(slabs, *after)


def _to_bf16(a, name, layers=None):
    first, stop = layers or (0, a.shape[0])

    def body(a_ref, o_ref):
        o_ref[...] = a_ref[...].astype(BF16)

    block = (None,) + a.shape[1:]
    return pl.pallas_call(
        body, name=name, grid=(stop - first,), in_specs=[pl.BlockSpec(block, lambda l: (first + l, 0, 0))],
        out_specs=pl.BlockSpec(block, lambda l: (l, 0, 0)),
        out_shape=jax.ShapeDtypeStruct((stop - first,) + a.shape[1:], BF16),
        compiler_params=_params(dimension_semantics=("arbitrary",)),
    )(a)


def kernel(x, c, w_ada, b_ada, g_pre, w_in, w_conv, w_pool, pool_scale, w_out, g_post, loss_target, m_w_ada, m_b_ada, m_g_pre, m_w_in, m_w_conv, m_w_pool, m_pool_scale, m_w_out, m_g_post, v_w_ada, v_b_ada, v_g_pre, v_w_in, v_w_conv, v_w_pool, v_pool_scale, v_w_out, v_g_post):
    mx, my, mc = _mesh_position()
    me = _block_id(mx, my, mc)
    chip = (2 * mx + my).astype(jnp.int32).reshape(1)
    core = mc.astype(jnp.int32).reshape(1)
    x0 = x[0]
    target = loss_target[0]
    conv_shard = w_conv.shape[-1]

    own_small = jnp.concatenate([c, w_conv.reshape(1, DEPTH * 3 * conv_shard)], axis=1)
    first_shards = [_to_bf16(w_in, "cast_w_in_0", (0, 1)), _to_bf16(w_out, "cast_w_out_0", (0, 1))]
    all_small = _all_gather_small(own_small, "all_gather_c_w_conv", first_shards)[:, 0, :]
    gathers = [_gather_weights_start(0, *first_shards, [all_small], relayed=True)]
    c_all = all_small[:, :D_MODEL]
    w_conv_full = all_small[:, D_MODEL:].reshape(N_DEV, DEPTH, 3, conv_shard).transpose(1, 2, 0, 3).reshape(
        DEPTH, 3, CONV_W)
    cps = jnp.concatenate([w_conv_full, pool_scale[:, None], jnp.zeros((DEPTH, 4, CONV_W), F32)], axis=1)

    c_act, pieces = _modulation_columns(c_all, w_ada)
    upper = (1, DEPTH)
    upper_shards = [_to_bf16(w_in, "cast_w_in_1", upper), _to_bf16(w_out, "cast_w_out_1", upper)]
    wpool_b = _to_bf16(w_pool.reshape(DEPTH, POOL_ROWS, GROUP_D), "cast_w_pool").reshape(w_pool.shape)
    first_sems_0, _, (zones_0,) = gathers[0]
    neighbour_sems, zones_0 = _gather_weights_pass_on(
        "all_gather_weights_relay_0", NEIGHBOUR_CHIPS, first_sems_0[1], partial(_first_arrival, 0), zones_0,
        [pieces, *upper_shards, wpool_b, cps], relay=True)
    mod_all = _all_gather_small(pieces, "all_gather_modulation", [zones_0[0]])
    mod_mine = lax.dynamic_index_in_dim(mod_all, me, axis=1, keepdims=False)
    mod = mod_mine.reshape(N_DEV, DEPTH, W_IN_SHARD).transpose(1, 0, 2).reshape(DEPTH, 3 * D_MODEL) + b_ada
    zeros_d = jnp.zeros((DEPTH, 3, D_MODEL), F32)
    vec = jnp.concatenate([mod.reshape(DEPTH, 3, D_MODEL), g_pre[:, None], g_post[:, None], zeros_d], axis=1)

    gathers.append(_gather_weights_start(1, *upper_shards, [mod_all]))
    gathers = [(first_sems, shards, landing[k], k) for first_sems, shards, landing in gathers
               for k in range(len(landing))]

    xs, kept, wins, wouts = [x0], [], [], []
    for l in range(DEPTH):
        first_sems, shards, zones, index = gathers[l]
        if l == 0:
            diagonal_sems, zones = _gather_weights_pass_on(
                "all_gather_weights_pass_on_0", DIAGONAL_CHIP, neighbour_sems[3], lambda a, j: a, zones_0,
                [vec, cps, gathers[-1][1][0]])
            passed = [(neighbour_sems[:2], NEIGHBOUR_CHIPS), (diagonal_sems, DIAGONAL_CHIP)]
            win, wout = _gather_weights_finish(l, index, first_sems, passed, shards, zones, neighbour_sems[2])
        else:
            passed_sems, zones = _gather_weights_pass_on(
                f"all_gather_weights_pass_on_{l}", ALL_OTHER_CHIPS, first_sems[1], partial(_first_arrival, index),
                zones, [xs[-1]])
            win, wout = _gather_weights_finish(l, index, first_sems, [(passed_sems, ALL_OTHER_CHIPS)], shards, zones)
        x_next, *for_backward = _forward_layer(
            l, xs[-1], vec, cps, wpool_b, win, wout, target if l == DEPTH - 1 else None)
        xs.append(x_next)
        kept.append(for_backward[:6])
        wins.append(win)
        wouts.append(wout)
    dx, loss_tile = xs[DEPTH], for_backward[6]

    slab_rows = [None] * DEPTH
    partials = received = None
    in_flight = []

    def scatter(layer, grads, from_sibling, after):
        nonlocal partials, received
        partials = _add_sibling_blocks(
            f"grad_add_sibling_{layer}", layer, grads, from_sibling, core, partials, ALL_KINDS)
        chips = _chips_exchange(layer, partials, received, ALL_KINDS)
        sems, partials, received, token = _start_exchange(chips, f"grad_chips_start_{layer}", after)
        in_flight.append((chips, sems, layer))
        return token

    grads_above = None
    issued = []
    for l in reversed(range(DEPTH)):
        y, h, ycat, uc, fa, fp = kept[l]
        dx, dproj, dy, gwpool, dcps, dvec = _backward_layer(
            l, dx, y, xs[l], uc, fa, fp, vec, cps, wpool_b, wouts[l], wins[l], issued)
        slab_rows[l] = jnp.concatenate(
            [dvec[0], dvec[1], dvec[2], dvec[3], dvec[4], dcps[3], dcps[0], dcps[1], dcps[2],
             loss_tile[0] if l == 0 else jnp.zeros((LANES,), F32)])
        after = []
        if l == 0:
            gather_small = _all_gather_exchange(jnp.stack(slab_rows))
            sems_s, slab, slabs, token = _start_exchange(gather_small, "all_gather_small_grads_start")
            after = [token]
        hosted = _sibling_exchange(grads_above, ALL_KINDS) if grads_above is not None else None
        gwin, gwout, *from_sibling = _weight_grads(l, h, dproj, ycat, dy, hosted, after)
        if l == 0:
            _, (slabs,) = _finish_exchange(
                gather_small, "all_gather_small_grads_finish", sems_s, slab, slabs, [gwin])
        issued = [gwin]
        if grads_above is not None:
            issued = [scatter(l + 1, grads_above, from_sibling, [])]
        grads_above = [gwin, gwout, gwpool.reshape(POOL_ROWS, GROUP_D)]
        if l <= 1:
            from_sibling = _run_exchange(_sibling_exchange(grads_above, ALL_KINDS), f"grad_exchange_sibling_{l}")
            token = scatter(l, grads_above, from_sibling, [slabs] if l == 0 else [])
            issued = [token]
            grads_above = None
    grad_x = dx[None]
    chips_0, sems_0, _ = in_flight.pop()

    o = 3 * D_MODEL

    after = [token]
    for chips, sems, l in in_flight:
        partials, received = _finish_exchange(chips, f"grad_chips_finish_{l}", sems, partials, received, after)
        after = []
    upper = (1, DEPTH)
    w_in_upper = _adamw_reduced(
        "adamw_w_in_upper", w_in, m_w_in, v_w_in, partials[0], received[0], chip, ROW_TILE, upper, None)
    w_out_upper = _adamw_reduced(
        "adamw_w_out_upper", w_out, m_w_out, v_w_out, partials[1], received[1], chip, W_OUT_SHARD, upper, None)
    dmod_all = slabs[:, :, :o].reshape(N_DEV, DEPTH, N_DEV, W_IN_SHARD)
    dmod_cols = lax.dynamic_index_in_dim(dmod_all, me, axis=2, keepdims=False).transpose(1, 0, 2) + token[0, 0]
    g_w_ada, d_w_ada, nm_w_ada, nv_w_ada = _adamw_w_ada(w_ada, m_w_ada, v_w_ada, c_act.T, dmod_cols)

    partials, received = _finish_exchange(
        chips_0, "grad_chips_finish_0", sems_0, partials, received, [nv_w_ada, w_in_upper[3], w_out_upper[3]])
    gather_pool = _all_gather_exchange(_reduce_w_pool(partials[2], received[2], chip), axis=1)
    sems_p, pool_rows, pool_landing, token_p = _start_exchange(gather_pool, "all_gather_grad_w_pool_start")
    g_w_in, d_w_in, nm_w_in, nv_w_in = _adamw_reduced(
        "adamw_w_in_0", w_in, m_w_in, v_w_in, partials[0], received[0], chip, ROW_TILE, (0, 1), w_in_upper, [token_p])
    g_w_out, d_w_out, nm_w_out, nv_w_out = _adamw_reduced(
        "adamw_w_out_0", w_out, m_w_out, v_w_out, partials[1], received[1], chip, W_OUT_SHARD, (0, 1), w_out_upper,
        [token_p])
    ends = [0, o, o + D_MODEL, o + 2 * D_MODEL, o + 2 * D_MODEL + POOL_W, SLAB_COLS]
    g_b_ada, g_g_pre, g_g_post, g_pool_scale, g_conv, loss = _sum_sources(
        slabs, list(zip(ends[:-1], ends[1:])), [token_p, nv_w_in, nv_w_out])
    loss = loss.reshape(())
    g_w_conv = lax.dynamic_slice_in_dim(g_conv.reshape(DEPTH, 3, CONV_W), me * conv_shard, conv_shard, axis=2)
    taps_first = lambda a: a.transpose(1, 0, 2)
    small = _adamw_small("adamw_small", [
        (b_ada, g_b_ada, m_b_ada, v_b_ada),
        (g_pre, g_g_pre, m_g_pre, v_g_pre),
        tuple(taps_first(a) for a in (w_conv, g_w_conv, m_w_conv, v_w_conv)),
        (pool_scale, g_pool_scale, m_pool_scale, v_pool_scale),
        (g_post, g_g_post, m_g_post, v_g_post),
    ])
    (d_b_ada, nm_b_ada, nv_b_ada, g_b_ada), (d_g_pre, nm_g_pre, nv_g_pre, g_g_pre), conv_steps, \
        (d_ps, nm_ps, nv_ps, g_pool_scale), (d_g_post, nm_g_post, nv_g_post, g_g_post) = small
    d_w_conv, nm_w_conv, nv_w_conv, g_w_conv = (taps_first(a) for a in conv_steps)
    _, (g_pool_all,) = _finish_exchange(
        gather_pool, "all_gather_grad_w_pool_finish", sems_p, pool_rows, pool_landing, [nv_w_in, nv_w_out, nv_g_post])
    ((d_w_pool, nm_w_pool, nv_w_pool, g_w_pool),) = _adamw_small(
        "adamw_w_pool", [(w_pool, g_pool_all.reshape(w_pool.shape), m_w_pool, v_w_pool)])

    return (loss, grad_x,
            g_w_ada, g_b_ada, g_g_pre, g_w_in, g_w_conv, g_w_pool, g_pool_scale, g_w_out, g_g_post,
            d_w_ada, d_b_ada, d_g_pre, d_w_in, d_w_conv, d_w_pool, d_ps, d_w_out, d_g_post,
            nm_w_ada, nm_b_ada, nm_g_pre, nm_w_in, nm_w_conv, nm_w_pool, nm_ps, nm_w_out, nm_g_post,
            nv_w_ada, nv_b_ada, nv_g_pre, nv_w_in, nv_w_conv, nv_w_pool, nv_ps, nv_w_out, nv_g_post)
```

```python
from functools import partial

import jax
import jax.numpy as jnp
from jax import lax
from jax.experimental import pallas as pl
from jax.experimental.pallas import tpu as pltpu

F32 = jnp.float32
BF16 = jnp.bfloat16

D_MODEL = 1024
DEPTH = 4
CONV_W = 512
POOL_W = 512
POOL_WINDOWS = (2, 4, 8, 16)
GROUP_D = 128
IN_COLS = 4 * CONV_W + 2 * POOL_W
NORM_EPS = 1e-6

ADAM_LR = 0.001
ADAM_B1 = 0.9
ADAM_B2 = 0.999
ADAM_EPS = 1e-08
ADAM_WD = 0.01
ADAM_STEP = 10

N_DEV = 8
N_CHIP = 4
N_OTHER_CHIPS = N_CHIP - 1
MESH = pl.DeviceIdType.MESH
W_IN_SHARD = IN_COLS // N_DEV
W_OUT_SHARD = D_MODEL // N_DEV
POOL_ROWS = len(POOL_WINDOWS) * GROUP_D
POOL_SHARD = POOL_ROWS // N_DEV

SUBLANES = 8
LANES = 128
VMEM_LIMIT_BYTES = 56 * 1024 * 1024
ROW_TILE = 512
BWD_TILE = 256
GWIN_COLS = 768
GWOUT_COLS = 512
POOL_HALO = 16
CONV_HALO = SUBLANES

SLAB_COLS = 3 * D_MODEL + D_MODEL + D_MODEL + POOL_W + 3 * CONV_W

HBM = pl.BlockSpec(memory_space=pl.ANY)


def _params(**kw):
    return pltpu.CompilerParams(vmem_limit_bytes=VMEM_LIMIT_BYTES, **kw)


def _sigmoid(v):
    return 1.0 / (1.0 + jnp.exp(-v))


def _dot(a, b):
    return jnp.dot(a, b, preferred_element_type=F32)


def _dot_tn(a, b):
    return lax.dot_general(a, b, (((0,), (0,)), ((), ())), preferred_element_type=F32)


def _dot_nt(a, b):
    return lax.dot_general(a, b, (((1,), (1,)), ((), ())), preferred_element_type=F32)


def _rows_from_before(v, k):
    return pltpu.roll(v, k, 0)


def _rows_from_after(v, k):
    return pltpu.roll(v, v.shape[0] - k, 0)


def _window_counts(t0, rows):
    return (lax.broadcasted_iota(jnp.int32, (rows, 1), 0) + (t0 + 1)).astype(F32)


def _split_proj(p32):
    cw = CONV_W
    return (p32[:, 0 * cw:1 * cw], p32[:, 1 * cw:2 * cw], p32[:, 2 * cw:3 * cw], p32[:, 3 * cw:4 * cw],
            p32[:, 4 * cw:4 * cw + POOL_W], p32[:, 4 * cw + POOL_W:])


def _layer_spec(shape, layer):
    nd = len(shape)
    return pl.BlockSpec((None,) + tuple(shape[1:]), lambda i, _l=layer, _n=nd: (_l,) + (0,) * (_n - 1))


def _whole_spec(shape):
    return pl.BlockSpec(tuple(shape), lambda i, _n=len(shape): (0,) * _n, pipeline_mode=pl.Buffered(1))


def _mesh_position():
    return lax.axis_index("x"), lax.axis_index("y"), lax.axis_index("c")


def _block_id(x, y, c):
    return 4 * x + 2 * y + c


def _other_chips(x, y):
    return [(x ^ 1, y), (x, y ^ 1), (x ^ 1, y ^ 1)]


def _col_block(ref, blk):
    return ref.at[:, pl.ds(pl.multiple_of(blk * W_IN_SHARD, LANES), W_IN_SHARD)]


def _row_block(rows):
    def block(ref, blk):
        return ref.at[pl.ds(pl.multiple_of(blk * rows, rows), rows), :]
    return block


_BLOCK_OF = (_col_block, _row_block(W_OUT_SHARD), _row_block(POOL_SHARD))
_BLOCK_SHAPES = ((D_MODEL, W_IN_SHARD), (W_OUT_SHARD, D_MODEL), (POOL_SHARD, GROUP_D))


class _Exchange:
    def __init__(self, inputs, out_shapes, aliases, sem_shapes, make):
        self.inputs, self.out_shapes, self.aliases, self.sem_shapes, self.make = (
            list(inputs), list(out_shapes), dict(aliases), list(sem_shapes), make)


def _run_exchange(exchange, name):
    n_in, n_out = len(exchange.inputs), len(exchange.out_shapes)

    def body(*refs):
        start, finish = exchange.make(refs[:n_in], refs[n_in:n_in + n_out], refs[n_in + n_out:])
        start()
        finish()

    return pl.pallas_call(
        body, name=name, in_specs=[HBM] * n_in, out_specs=[HBM] * n_out, out_shape=exchange.out_shapes,
        scratch_shapes=exchange.sem_shapes, input_output_aliases=exchange.aliases, compiler_params=_params(),
    )(*exchange.inputs)


_SEM = pl.BlockSpec(memory_space=pltpu.SEMAPHORE)
_DATAFLOW = pltpu.SideEffectType.DATAFLOW_SIDE_EFFECTING


def _start_exchange(exchange, name, after=()):
    n_in, n_out, n_sem = len(exchange.inputs), len(exchange.out_shapes), len(exchange.sem_shapes)
    sources = [i for i in range(n_in) if i not in exchange.aliases]
    aliases = {i: n_sem + k for k, i in enumerate(sources)}
    aliases.update({i: n_sem + len(sources) + o for i, o in exchange.aliases.items()})

    def body(*refs):
        in_refs = refs[:n_in]
        outs = refs[n_in + len(after):]
        sems = outs[:n_sem]
        out_refs = outs[n_sem + len(sources):n_sem + len(sources) + n_out]
        exchange.make(in_refs, out_refs, sems)[0]()
        refs[-1][...] = jnp.zeros_like(refs[-1])

    outs = pl.pallas_call(
        body, name=name, in_specs=[HBM] * (n_in + len(after)),
        out_specs=[_SEM] * n_sem + [HBM] * (len(sources) + n_out) + [pl.BlockSpec(memory_space=pltpu.VMEM)],
        out_shape=(exchange.sem_shapes + [pltpu.HBM(exchange.inputs[i].shape, exchange.inputs[i].dtype) for i in sources]
                   + [pltpu.HBM(s.shape, s.dtype) for s in exchange.out_shapes]
                   + [jax.ShapeDtypeStruct((SUBLANES, LANES), F32)]),
        input_output_aliases=aliases, compiler_params=_params(has_side_effects=_DATAFLOW),
    )(*exchange.inputs, *after)
    return outs[:n_sem], outs[n_sem:n_sem + len(sources)], outs[n_sem + len(sources):-1], outs[-1]


def _finish_exchange(exchange, name, sems, sources, landing, after):
    n_src, n_out, n_sem = len(sources), len(landing), len(sems)
    n_in = len(exchange.inputs)
    source_at = [i for i in range(n_in) if i not in exchange.aliases]

    def body(*refs):
        src_refs, out_refs = refs[:n_src], refs[n_src:n_src + n_out]
        sem_refs = refs[n_src + n_out:n_src + n_out + n_sem]
        in_refs = [None] * n_in
        for k, i in enumerate(source_at):
            in_refs[i] = src_refs[k]
        for i, o in exchange.aliases.items():
            in_refs[i] = out_refs[o]
        exchange.make(in_refs, out_refs, sem_refs)[1]()

    arrays = list(sources) + list(landing)
    outs = pl.pallas_call(
        body, name=name, in_specs=[HBM] * len(arrays) + [_SEM] * n_sem + [HBM] * len(after),
        out_specs=[HBM] * len(arrays), out_shape=[pltpu.HBM(a.shape, a.dtype) for a in arrays],
        input_output_aliases={i: i for i in range(len(arrays))}, compiler_params=_params(has_side_effects=_DATAFLOW),
    )(*arrays, *sems, *after)
    return outs[:n_src], outs[n_src:]


N_GATHERED = 2
FIRST_COPIES = 1 + N_OTHER_CHIPS
_GATHERED_SHAPES = ((D_MODEL, IN_COLS), (D_MODEL, D_MODEL))
ALL_OTHER_CHIPS, NEIGHBOUR_CHIPS, DIAGONAL_CHIP = (0, 1, 2), (0, 1), (2,)


def _first_sem(layer, a, k):
    return (layer * N_GATHERED + a) * FIRST_COPIES + k


def _first_arrival(layer, a, j):
    return _first_sem(layer, a, 1 + j)


def _gather_copy(window_of, full_ref, blk, send_sem, recv_sem, to, src=None):
    window = window_of(full_ref, blk)
    return pltpu.make_async_remote_copy(
        src_ref=window if src is None else src, dst_ref=window, send_sem=send_sem, recv_sem=recv_sem,
        device_id=to, device_id_type=MESH)


def _first_copies(layer, shard_refs, full_refs, send_sems, recv_sems, local_sems, relayed):
    x, y, c = _mesh_position()
    me = _block_id(x, y, c)
    chips = [_other_chips(x, y)[j] for j in (NEIGHBOUR_CHIPS if relayed else ALL_OTHER_CHIPS)]
    own, remote = [], []
    for a in range(N_GATHERED):
        shard = shard_refs[a].at[layer]
        own.append(pltpu.make_async_copy(
            shard, _BLOCK_OF[a](full_refs[a], me), local_sems.at[layer * N_GATHERED + a]))
        targets = [(x, y, 1 - c)] + [(*chip, c) for chip in chips]
        remote += [_gather_copy(_BLOCK_OF[a], full_refs[a], me, send_sems.at[_first_sem(layer, a, k)],
                                recv_sems.at[_first_sem(layer, a, k)], to, src=shard)
                   for k, to in enumerate(targets)]
    return own, remote


def _gather_weights_start(first_layer, win_shards, wout_shards, after, relayed=False):
    n_layers = win_shards.shape[0]
    n_first = n_layers * N_GATHERED * FIRST_COPIES
    sem_shapes = [pltpu.SemaphoreType.DMA((n_first,)), pltpu.SemaphoreType.DMA((n_first,)),
                  pltpu.SemaphoreType.DMA((n_layers * N_GATHERED,))]
    shards = [win_shards, wout_shards]

    def body(win_sh, wout_sh, *rest):
        send_sems, recv_sems, local_sems, win_thru, wout_thru, *landing = rest[len(after):]
        for layer in range(n_layers):
            own, remote = _first_copies(layer, (win_sh, wout_sh), landing[N_GATHERED * layer:N_GATHERED * (layer + 1)],
                                        send_sems, recv_sems, local_sems, relayed)
            for cp in own + remote:
                cp.start()

    outs = pl.pallas_call(
        body, name=f"all_gather_weights_start_{first_layer}", in_specs=[HBM] * (2 + len(after)),
        out_specs=[_SEM] * 3 + [HBM] * (2 + n_layers * N_GATHERED),
        out_shape=(sem_shapes + [pltpu.HBM(s.shape, s.dtype) for s in shards]
                   + [pltpu.HBM(s, BF16) for _ in range(n_layers) for s in _GATHERED_SHAPES]),
        input_output_aliases={0: 3, 1: 4}, compiler_params=_params(has_side_effects=_DATAFLOW),
    )(*shards, *after)
    landing = outs[5:]
    return outs[:3], outs[3:5], [landing[N_GATHERED * l:N_GATHERED * (l + 1)] for l in range(n_layers)]


def _passed_on_copies(full_refs, send_sems, recv_sems, core_of_block, chips):
    x, y, c = _mesh_position()
    return [_gather_copy(_BLOCK_OF[a], full_refs[a], _block_id(*_other_chips(x, y)[j], core_of_block),
                         send_sems.at[a * N_OTHER_CHIPS + j], recv_sems.at[a * N_OTHER_CHIPS + j], (x, y, 1 - c))
            for a in range(N_GATHERED) for j in chips]


def _relayed_copies(full_refs, send_sems, recv_sems):
    x, y, c = _mesh_position()
    source, to = (x ^ (1 - c), y ^ c), (x ^ c, y ^ (1 - c))
    return [_gather_copy(_BLOCK_OF[a], full_refs[a], _block_id(*source, c), send_sems.at[a], recv_sems.at[a], (*to, c))
            for a in range(N_GATHERED)]


def _gather_weights_pass_on(name, chips, arrival_sems, arrival_sem_of, landing, after, relay=False):
    n = N_GATHERED * N_OTHER_CHIPS
    sem_shapes = [pltpu.SemaphoreType.DMA((n,))] * 2 + [pltpu.SemaphoreType.DMA((N_GATHERED,))] * (2 if relay else 0)

    def body(win_ref, wout_ref, arrivals, *rest):
        sems = rest[len(after):len(after) + len(sem_shapes)]
        x, y, c = _mesh_position()
        full_refs = (win_ref, wout_ref)
        passed = _passed_on_copies(full_refs, sems[0], sems[1], c, chips)
        relayed = _relayed_copies(full_refs, sems[2], sems[3]) if relay else []
        for a in range(N_GATHERED):
            for j in chips:
                sem = arrivals.at[arrival_sem_of(a, j)]
                _gather_copy(_BLOCK_OF[a], full_refs[a], _block_id(*_other_chips(x, y)[j], c), sem, sem,
                             (x, y, c)).wait_recv()
            for cp in relayed[a:a + 1] + passed[a * len(chips):(a + 1) * len(chips)]:
                cp.start()

    outs = pl.pallas_call(
        body, name=name, in_specs=[HBM] * N_GATHERED + [_SEM] + [HBM] * len(after),
        out_specs=[_SEM] * len(sem_shapes) + [HBM] * N_GATHERED,
        out_shape=sem_shapes + [pltpu.HBM(a.shape, a.dtype) for a in landing],
        input_output_aliases={a: len(sem_shapes) + a for a in range(N_GATHERED)},
        compiler_params=_params(has_side_effects=_DATAFLOW),
    )(*landing, arrival_sems, *after)
    return outs[:len(sem_shapes)], outs[len(sem_shapes):]


def _gather_weights_finish(layer, index, first_sems, passed, shards, landing, relay_send_sems=None):
    relayed = relay_send_sems is not None
    passed_sems = [sem for (send, recv), _ in passed for sem in (send, recv)] + ([relay_send_sems] if relayed else [])

    def body(win_ref, wout_ref, first_send, first_recv, local_sems, *rest):
        sems, (win_sh, wout_sh) = rest[:len(passed_sems)], rest[len(passed_sems):len(passed_sems) + 2]
        x, y, c = _mesh_position()
        full_refs = (win_ref, wout_ref)
        own, sent = _first_copies(index, (win_sh, wout_sh), full_refs, first_send, first_recv, local_sems, relayed)
        for a in range(N_GATHERED):
            sem = _first_sem(index, a, 0)
            _gather_copy(_BLOCK_OF[a], full_refs[a], _block_id(x, y, 1 - c), first_recv.at[sem], first_recv.at[sem],
                         (x, y, c)).wait_recv()
        for p, (_, chips) in enumerate(passed):
            for cp in _passed_on_copies(full_refs, sems[2 * p], sems[2 * p + 1], 1 - c, chips):
                cp.wait_recv()
            sent += _passed_on_copies(full_refs, sems[2 * p], sems[2 * p + 1], c, chips)
        if relayed:
            sent += _relayed_copies(full_refs, sems[-1], sems[-1])
        for cp in sent:
            cp.wait_send()
        for cp in own:
            cp.wait()

    return pl.pallas_call(
        body, name=f"all_gather_weights_finish_{layer}",
        in_specs=[HBM] * N_GATHERED + [_SEM] * (3 + len(passed_sems)) + [HBM] * 2,
        out_specs=[HBM] * N_GATHERED, out_shape=[pltpu.HBM(a.shape, a.dtype) for a in landing],
        input_output_aliases={a: a for a in range(N_GATHERED)}, compiler_params=_params(has_side_effects=_DATAFLOW),
    )(*landing, *first_sems, *passed_sems, *shards)


ALL_KINDS = (0, 1, 2)


def _sibling_exchange(grads, kinds):
    n_arr = len(grads)

    def make(in_refs, out_refs, sems):
        send_sems, recv_sems = sems
        x, y, c = _mesh_position()
        copies = [pltpu.make_async_remote_copy(
            src_ref=_BLOCK_OF[kinds[a]](in_refs[a], 2 * q + (1 - c)), dst_ref=out_refs[a].at[q],
            send_sem=send_sems.at[a * N_CHIP + q], recv_sem=recv_sems.at[a * N_CHIP + q],
            device_id=(x, y, 1 - c), device_id_type=MESH)
            for a in range(n_arr) for q in range(N_CHIP)]

        def start():
            for cp in copies:
                cp.start()

        def finish():
            for cp in copies:
                cp.wait_recv()
            for cp in copies:
                cp.wait_send()

        return start, finish

    return _Exchange(
        grads, [jax.ShapeDtypeStruct((N_CHIP,) + _BLOCK_SHAPES[k], BF16) for k in kinds], {},
        [pltpu.SemaphoreType.DMA((n_arr * N_CHIP,)), pltpu.SemaphoreType.DMA((n_arr * N_CHIP,))], make)


def _chips_exchange(layer, partials, received, kinds):
    n_arr = len(partials)

    def make(in_refs, out_refs, sems):
        send_sems, recv_sems = sems
        x, y, c = _mesh_position()
        copies = [pltpu.make_async_remote_copy(
            src_ref=in_refs[a].at[2 * qx + qy, layer], dst_ref=out_refs[a].at[j, layer],
            send_sem=send_sems.at[a * N_OTHER_CHIPS + j], recv_sem=recv_sems.at[a * N_OTHER_CHIPS + j],
            device_id=(qx, qy, c), device_id_type=MESH)
            for a in range(n_arr) for j, (qx, qy) in enumerate(_other_chips(x, y))]

        def start():
            for cp in copies:
                cp.start()

        def finish():
            for cp in copies:
                cp.wait_recv()
            for cp in copies:
                cp.wait_send()

        return start, finish

    inputs = list(partials)
    aliases = {}
    if received is not None:
        inputs += list(received)
        aliases = {n_arr + a: a for a in range(n_arr)}
    return _Exchange(
        inputs, [jax.ShapeDtypeStruct((N_OTHER_CHIPS, DEPTH) + _BLOCK_SHAPES[k], BF16) for k in kinds], aliases,
        [pltpu.SemaphoreType.DMA((n_arr * N_OTHER_CHIPS,)), pltpu.SemaphoreType.DMA((n_arr * N_OTHER_CHIPS,))], make)


def _all_gather_exchange(v, axis=0):
    def make(in_refs, out_refs, sems):
        send_sems, recv_sems, local_sem = sems
        x, y, c = _mesh_position()
        me = _block_id(x, y, c)
        block = lambda blk: out_refs[0].at[(slice(None),) * axis + (blk,)]
        own = pltpu.make_async_copy(in_refs[0], block(me), local_sem.at[0])
        sends, arrivals = [], []
        for k in range(1, N_DEV):
            px, py, pc = x ^ ((k >> 2) & 1), y ^ ((k >> 1) & 1), c ^ (k & 1)
            sends.append(pltpu.make_async_remote_copy(
                src_ref=in_refs[0], dst_ref=block(me), send_sem=send_sems.at[k - 1],
                recv_sem=recv_sems.at[k - 1], device_id=(px, py, pc), device_id_type=MESH))
            arrivals.append(pltpu.make_async_remote_copy(
                src_ref=in_refs[0], dst_ref=block(_block_id(px, py, pc)), send_sem=send_sems.at[k - 1],
                recv_sem=recv_sems.at[k - 1], device_id=(x, y, c), device_id_type=MESH))

        def start():
            for cp in [own] + sends:
                cp.start()

        def finish():
            for cp in arrivals:
                cp.wait_recv()
            for cp in sends:
                cp.wait_send()
            own.wait()

        return start, finish

    return _Exchange(
        [v], [jax.ShapeDtypeStruct(v.shape[:axis] + (N_DEV,) + v.shape[axis:], v.dtype)], {},
        [pltpu.SemaphoreType.DMA((N_DEV - 1,)), pltpu.SemaphoreType.DMA((N_DEV - 1,)),
         pltpu.SemaphoreType.DMA((1,))], make)


def _host(exchange, args, in_specs, out_shape, out_specs, scratch):
    n_own = (len(args), len(out_shape), len(scratch))
    if exchange is None:
        return {}, lambda refs: (refs, None)
    n_ex = (len(exchange.inputs), len(exchange.out_shapes), len(exchange.sem_shapes))
    aliases = {n_own[0] + i: n_own[1] + o for i, o in exchange.aliases.items()}
    args += exchange.inputs
    in_specs += [HBM] * n_ex[0]
    out_shape += exchange.out_shapes
    out_specs += [HBM] * n_ex[1]
    scratch += exchange.sem_shapes

    def split(refs):
        own, theirs, at = [], [], 0
        for mine, ex in zip(n_own, n_ex):
            own += refs[at:at + mine]
            theirs.append(refs[at + mine:at + mine + ex])
            at += mine + ex
        return own, exchange.make(*theirs)

    return aliases, split


def _all_gather_small(v, name, after=()):
    vmem = pl.BlockSpec(memory_space=pltpu.VMEM)

    def body(v_ref, *rest):
        out_ref, send_sems, recv_sems = rest[len(after):]
        x, y, c = _mesh_position()
        me = _block_id(x, y, c)
        out_ref[me] = v_ref[...]
        sends = []
        for k in range(1, N_DEV):
            px, py, pc = x ^ ((k >> 2) & 1), y ^ ((k >> 1) & 1), c ^ (k & 1)
            send = pltpu.make_async_remote_copy(
                src_ref=v_ref, dst_ref=out_ref.at[me], send_sem=send_sems.at[k - 1], recv_sem=recv_sems.at[k - 1],
                device_id=(px, py, pc), device_id_type=MESH)
            send.start()
            sends.append((send, _block_id(px, py, pc)))
        for k, (send, peer) in enumerate(sends):
            pltpu.make_async_remote_copy(
                src_ref=v_ref, dst_ref=out_ref.at[peer], send_sem=send_sems.at[k], recv_sem=recv_sems.at[k],
                device_id=(x, y, c), device_id_type=MESH).wait_recv()
        for send, _ in sends:
            send.wait_send()

    return pl.pallas_call(
        body, name=name, in_specs=[vmem] + [HBM] * len(after), out_specs=vmem,
        out_shape=jax.ShapeDtypeStruct((N_DEV,) + v.shape, v.dtype),
        scratch_shapes=[pltpu.SemaphoreType.DMA((N_DEV - 1,)), pltpu.SemaphoreType.DMA((N_DEV - 1,))],
        compiler_params=_params(),
    )(v, *after)


def _forward_layer(layer, x, vec, cps, wpool, win, wout, target=None):
    t_len = x.shape[0]
    n_tiles = t_len // ROW_TILE
    row = lambda cols: pl.BlockSpec((ROW_TILE, cols), lambda i: (i, 0))
    widths = (D_MODEL, 2 * CONV_W, 3 * CONV_W, 4 * POOL_W)
    head = target is not None

    def body(x_ref, vec_ref, cps_ref, wpool_ref, win_ref, wout_ref, *rest):
        target_ref = rest[0] if head else None
        xo_ref, y_ref, h_ref, ycat_ref, uc_ref, fa_ref, fp_ref = rest[head:head + 7]
        loss_ref = rest[head + 7] if head else None
        zc_ref, pc_ref = rest[-2:]
        i = pl.program_id(0)

        @pl.when(i == 0)
        def _():
            zc_ref[...] = jnp.zeros_like(zc_ref)
            pc_ref[...] = jnp.zeros_like(pc_ref)
            if head:
                loss_ref[...] = jnp.zeros_like(loss_ref)

        x_t = x_ref[...]
        shift, scale, gate = vec_ref[0:1, :], vec_ref[1:2, :], vec_ref[2:3, :]
        g_pre, g_post = vec_ref[3:4, :], vec_ref[4:5, :]
        w0, w1, w2, ps = cps_ref[0:1, :], cps_ref[1:2, :], cps_ref[2:3, :], cps_ref[3:4, :]
        rx = lax.rsqrt(jnp.mean(x_t * x_t, axis=-1, keepdims=True) + NORM_EPS)
        h = (x_t * rx) * g_pre * (1.0 + scale) + shift
        h_ref[...] = h.astype(BF16)
        proj = _dot(h.astype(BF16), win_ref[...])
        u_a, b_a, c_a, g_a, u_p, g_p = _split_proj(proj)
        uc_ref[...] = jnp.concatenate([u_a, c_a], axis=1).astype(BF16)

        z = c_a * u_a
        zcat = jnp.concatenate([zc_ref[...], z], axis=0)
        zc_ref[...] = z[ROW_TILE - CONV_HALO:]
        conv = (w0 * _rows_from_before(zcat, 2)[CONV_HALO:] + w1 * _rows_from_before(zcat, 1)[CONV_HALO:] + w2 * z)
        sig_a = _sigmoid(g_a)
        silu_a = g_a * sig_a
        b_conv = b_a * conv
        y_a = b_conv * silu_a
        fa_ref[...] = jnp.concatenate(
            [silu_a * conv, silu_a * b_a, b_conv * (sig_a + silu_a * (1.0 - sig_a))], axis=1).astype(BF16)

        pcat = jnp.concatenate([pc_ref[...], u_p], axis=0)
        pc_ref[...] = u_p[ROW_TILE - POOL_HALO:]
        counts = _window_counts(i * ROW_TILE, ROW_TILE)
        pooled, mixed = [], []
        for g, w in enumerate(POOL_WINDOWS):
            cols = slice(g * GROUP_D, (g + 1) * GROUP_D)
            s = pcat[:, cols]
            step = 1
            while step < w:
                s = s + _rows_from_before(s, step)
                step *= 2
            pooled_g = (s[POOL_HALO:] * (1.0 / jnp.minimum(counts, float(w))) - u_p[:, cols]).astype(BF16)
            pooled.append(pooled_g)
            mixed.append(_dot(pooled_g, wpool_ref[g]))
        mixed = jnp.concatenate(mixed, axis=1)
        sig_p = _sigmoid(g_p)
        silu_p = g_p * sig_p
        mixed_ps = mixed * ps
        y_p = mixed_ps * silu_p
        fp_ref[...] = jnp.concatenate(
            [(ps * silu_p).astype(BF16), (mixed_ps * (sig_p + silu_p * (1.0 - sig_p))).astype(BF16),
             (silu_p * mixed).astype(BF16)] + pooled, axis=1)

        ycat = jnp.concatenate([y_a, y_p], axis=1)
        ycat_ref[...] = ycat.astype(BF16)
        y_b = _dot(ycat.astype(BF16), wout_ref[...]).astype(BF16)
        y_ref[...] = y_b
        y_t = y_b.astype(F32)
        ry = lax.rsqrt(jnp.mean(y_t * y_t, axis=-1, keepdims=True) + NORM_EPS)
        x_next = x_t + gate * (y_t * ry * g_post)
        if head:
            err = x_next - target_ref[...]
            xo_ref[...] = err * (1.0 / D_MODEL)
            loss_ref[...] += jnp.sum(err * err) * (0.5 / D_MODEL)
        else:
            xo_ref[...] = x_next

    tile = (SUBLANES, LANES)
    return pl.pallas_call(
        body, name=f"forward_layer_{layer}", grid=(n_tiles,),
        in_specs=[row(D_MODEL), _layer_spec(vec.shape, layer), _layer_spec(cps.shape, layer),
                  _layer_spec(wpool.shape, layer), _whole_spec(win.shape), _whole_spec(wout.shape)]
        + [row(D_MODEL)] * head,
        out_specs=[row(D_MODEL), row(D_MODEL), row(D_MODEL), row(D_MODEL)] + [row(w) for w in widths[1:]]
        + [_whole_spec(tile)] * head,
        out_shape=[jax.ShapeDtypeStruct((t_len, D_MODEL), F32), jax.ShapeDtypeStruct((t_len, D_MODEL), BF16),
                   jax.ShapeDtypeStruct((t_len, D_MODEL), BF16), jax.ShapeDtypeStruct((t_len, D_MODEL), BF16)]
        + [jax.ShapeDtypeStruct((t_len, w), BF16) for w in widths[1:]] + [jax.ShapeDtypeStruct(tile, F32)] * head,
        scratch_shapes=[pltpu.VMEM((CONV_HALO, CONV_W), F32), pltpu.VMEM((POOL_HALO, POOL_W), F32)],
        compiler_params=_params(dimension_semantics=("arbitrary",)),
    )(x, vec, cps, wpool, win, wout, *([target] * head))


def _backward_layer(layer, dxo, y, x, uc, fa, fp, vec, cps, wpool, wout, win, after):
    t_len = dxo.shape[0]
    n_tiles = t_len // BWD_TILE
    halo_per_tile = BWD_TILE // POOL_HALO
    rev = lambda cols: pl.BlockSpec((BWD_TILE, cols), lambda i: (n_tiles - 1 - i, 0))
    halo_spec = pl.BlockSpec(
        (POOL_HALO, 2 * CONV_W), lambda i: (jnp.maximum((n_tiles - 1 - i) * halo_per_tile - 1, 0), 0))
    gwpool_shape = (len(POOL_WINDOWS), GROUP_D, GROUP_D)

    def body(dxo_ref, y_ref, x_ref, uc_ref, uch_ref, fa_ref, fp_ref, vec_ref, cps_ref, wpool_ref, wout_ref, win_ref,
             *rest):
        dx_ref, dproj_ref, dy_ref, gwpool_ref, dcps_ref, dvec_ref, gwpool_acc, dcc_ref, qc_ref = rest[len(after):]
        i = pl.program_id(0)
        tile = n_tiles - 1 - i

        @pl.when(i == 0)
        def _():
            gwpool_acc[...] = jnp.zeros_like(gwpool_acc)
            dcps_ref[...] = jnp.zeros_like(dcps_ref)
            dvec_ref[...] = jnp.zeros_like(dvec_ref)
            dcc_ref[...] = jnp.zeros_like(dcc_ref)
            qc_ref[...] = jnp.zeros_like(qc_ref)

        shift, scale, gate = vec_ref[0:1, :], vec_ref[1:2, :], vec_ref[2:3, :]
        g_pre, g_post = vec_ref[3:4, :], vec_ref[4:5, :]
        w0, w1, w2 = cps_ref[0:1, :], cps_ref[1:2, :], cps_ref[2:3, :]

        dxo_t = dxo_ref[...]
        y_t = y_ref[...].astype(F32)
        ry = lax.rsqrt(jnp.mean(y_t * y_t, axis=-1, keepdims=True) + NORM_EPS)
        yh = y_t * ry
        dvec_ref[2:3, :] += jnp.sum(dxo_t * yh, axis=0, keepdims=True)
        dyh = dxo_t * (gate * g_post)
        dy_b = (ry * (dyh - yh * jnp.mean(dyh * yh, axis=-1, keepdims=True))).astype(BF16)
        dy_ref[...] = dy_b
        dycat = _dot_nt(dy_b, wout_ref[...])
        dy_a, dy_p = dycat[:, :CONV_W], dycat[:, CONV_W:]

        fa_t = fa_ref[...].astype(F32)
        db_a = dy_a * fa_t[:, :CONV_W]
        dconv = dy_a * fa_t[:, CONV_W:2 * CONV_W]
        dg_a = dy_a * fa_t[:, 2 * CONV_W:]
        uc_t = uc_ref[...].astype(F32)
        u_a, c_a = uc_t[:, :CONV_W], uc_t[:, CONV_W:]
        halo = jnp.where(tile > 0, uch_ref[...].astype(F32), 0.0)[POOL_HALO - CONV_HALO:]
        z = c_a * u_a
        zcat = jnp.concatenate([halo[:, CONV_W:] * halo[:, :CONV_W], z], axis=0)
        z1 = _rows_from_before(zcat, 1)[CONV_HALO:]
        z2 = _rows_from_before(zcat, 2)[CONV_HALO:]
        dccat = jnp.concatenate([dconv, dcc_ref[...]], axis=0)
        dc1 = _rows_from_after(dccat, 1)[:BWD_TILE]
        dc2 = _rows_from_after(dccat, 2)[:BWD_TILE]
        dz = w2 * dconv + w1 * dc1 + w0 * dc2
        dcc_ref[...] = dconv[:CONV_HALO]
        dcps_ref[0:1, :] += jnp.sum(dconv * z2, axis=0, keepdims=True)
        dcps_ref[1:2, :] += jnp.sum(dconv * z1, axis=0, keepdims=True)
        dcps_ref[2:3, :] += jnp.sum(dconv * z, axis=0, keepdims=True)
        du_a = dz * c_a
        dc_a = dz * u_a

        dmixed = (dy_p * fp_ref[:, :POOL_W].astype(F32)).astype(BF16)
        dg_p = dy_p * fp_ref[:, POOL_W:2 * POOL_W].astype(F32)
        dcps_ref[3:4, :] += jnp.sum(dy_p * fp_ref[:, 2 * POOL_W:3 * POOL_W].astype(F32), axis=0, keepdims=True)
        counts = _window_counts(tile * BWD_TILE, BWD_TILE)
        du_p, q_head = [], []
        for g, w in enumerate(POOL_WINDOWS):
            cols = slice(g * GROUP_D, (g + 1) * GROUP_D)
            dm_g = dmixed[:, cols]
            dpooled_g = _dot_nt(dm_g, wpool_ref[g])
            gwpool_acc[g] += _dot_tn(fp_ref[:, 3 * POOL_W + g * GROUP_D:3 * POOL_W + (g + 1) * GROUP_D], dm_g)
            q_g = dpooled_g * (1.0 / jnp.minimum(counts, float(w)))
            q_head.append(q_g[:POOL_HALO])
            s = jnp.concatenate([q_g, qc_ref[:, cols]], axis=0)
            step = 1
            while step < w:
                s = s + _rows_from_after(s, step)
                step *= 2
            du_p.append(s[:BWD_TILE] - dpooled_g)
        qc_ref[...] = jnp.concatenate(q_head, axis=1)
        dproj_b = jnp.concatenate([du_a, db_a, dc_a, dg_a] + du_p + [dg_p], axis=1).astype(BF16)
        dproj_ref[...] = dproj_b

        x_t = x_ref[...]
        rx = lax.rsqrt(jnp.mean(x_t * x_t, axis=-1, keepdims=True) + NORM_EPS)
        xn = x_t * rx
        mod_scale = 1.0 + scale
        dh = _dot_nt(dproj_b, win_ref[...])
        dvec_ref[0:1, :] += jnp.sum(dh, axis=0, keepdims=True)
        dvec_ref[1:2, :] += jnp.sum(dh * xn, axis=0, keepdims=True)
        dxn = dh * (g_pre * mod_scale)
        dx_ref[...] = dxo_t + rx * (dxn - xn * jnp.mean(dxn * xn, axis=-1, keepdims=True))

        @pl.when(i == n_tiles - 1)
        def _():
            gwpool_ref[...] = gwpool_acc[...].astype(BF16)
            sum_dh_xn, sum_dxo_yh = dvec_ref[1:2, :], dvec_ref[2:3, :]
            dvec_ref[1:2, :] = sum_dh_xn * g_pre
            dvec_ref[3:4, :] = sum_dh_xn * mod_scale
            dvec_ref[2:3, :] = sum_dxo_yh * g_post
            dvec_ref[4:5, :] = sum_dxo_yh * gate

    return pl.pallas_call(
        body, name=f"backward_layer_{layer}", grid=(n_tiles,),
        in_specs=[rev(D_MODEL), rev(D_MODEL), rev(D_MODEL), rev(2 * CONV_W), halo_spec, rev(3 * CONV_W),
                  rev(4 * POOL_W), _layer_spec(vec.shape, layer), _layer_spec(cps.shape, layer),
                  _layer_spec(wpool.shape, layer), _whole_spec(wout.shape), _whole_spec(win.shape)]
        + [HBM] * len(after),
        out_specs=[rev(D_MODEL), rev(IN_COLS), rev(D_MODEL), _whole_spec(gwpool_shape),
                   _whole_spec((SUBLANES, CONV_W)), _whole_spec((SUBLANES, D_MODEL))],
        out_shape=[jax.ShapeDtypeStruct((t_len, D_MODEL), F32), jax.ShapeDtypeStruct((t_len, IN_COLS), BF16),
                   jax.ShapeDtypeStruct((t_len, D_MODEL), BF16), jax.ShapeDtypeStruct(gwpool_shape, BF16),
                   jax.ShapeDtypeStruct((SUBLANES, CONV_W), F32), jax.ShapeDtypeStruct((SUBLANES, D_MODEL), F32)],
        scratch_shapes=[pltpu.VMEM(gwpool_shape, F32), pltpu.VMEM((CONV_HALO, CONV_W), F32),
                        pltpu.VMEM((POOL_HALO, POOL_W), F32)],
        compiler_params=_params(dimension_semantics=("arbitrary",)),
    )(dxo, y, x, uc, uc, fa, fp, vec, cps, wpool, wout, win, *after)


def _weight_grads(layer, h, dproj, ycat, dy, exchange, after=()):
    t_len = dy.shape[0]
    n_in, n_out = IN_COLS // GWIN_COLS, D_MODEL // GWOUT_COLS
    args = [h, dproj, ycat, dy, *after]
    in_specs = [_whole_spec(h.shape),
                pl.BlockSpec((t_len, GWIN_COLS), lambda s: (0, jnp.minimum(s, n_in - 1))),
                _whole_spec(ycat.shape),
                pl.BlockSpec((t_len, GWOUT_COLS), lambda s: (0, jnp.maximum(s - n_in, 0)))] + [HBM] * len(after)
    out_shape = [jax.ShapeDtypeStruct((D_MODEL, IN_COLS), BF16), jax.ShapeDtypeStruct((D_MODEL, D_MODEL), BF16)]
    out_specs = [pl.BlockSpec((D_MODEL, GWIN_COLS), lambda s: (0, jnp.minimum(s, n_in - 1))),
                 pl.BlockSpec((D_MODEL, GWOUT_COLS), lambda s: (0, jnp.maximum(s - n_in, 0)))]
    scratch = []
    aliases, split = _host(exchange, args, in_specs, out_shape, out_specs, scratch)

    def body(*refs):
        (h_ref, dproj_ref, ycat_ref, dy_ref, *_, gwin_ref, gwout_ref), hosted = split(refs)
        s = pl.program_id(0)
        if hosted is not None:
            pl.when(s == 0)(hosted[0])

        @pl.when(s < n_in)
        def _():
            gwin_ref[...] = _dot_tn(h_ref[...], dproj_ref[...]).astype(BF16)

        @pl.when(s >= n_in)
        def _():
            gwout_ref[...] = _dot_tn(ycat_ref[...], dy_ref[...]).astype(BF16)

        if hosted is not None:
            pl.when(s == n_in + n_out - 1)(hosted[1])

    return pl.pallas_call(
        body, name=f"weight_grads_{layer}", grid=(n_in + n_out,), in_specs=in_specs, out_specs=out_specs,
        out_shape=out_shape, scratch_shapes=scratch, input_output_aliases=aliases,
        compiler_params=_params(dimension_semantics=("arbitrary",)),
    )(*args)


def _add_sibling_blocks(name, layer, grads, received, core, partials, kinds):
    n_arr = len(grads)

    def body(core_ref, *refs):
        mine, theirs, outs = refs[:n_arr], refs[n_arr:2 * n_arr], refs[-n_arr:]
        for a in range(n_arr):
            outs[a][...] = (mine[a][...].astype(F32) + theirs[a][...].astype(F32)).astype(BF16)

    own_of_kind = [
        pl.BlockSpec((D_MODEL, W_IN_SHARD), lambda q, core_ref: (0, 2 * q + core_ref[0])),
        pl.BlockSpec((W_OUT_SHARD, D_MODEL), lambda q, core_ref: (2 * q + core_ref[0], 0)),
        pl.BlockSpec((POOL_SHARD, GROUP_D), lambda q, core_ref: (2 * q + core_ref[0], 0)),
    ]
    shapes = [_BLOCK_SHAPES[k] for k in kinds]
    recv_specs = [pl.BlockSpec((None,) + s, lambda q, core_ref: (q, 0, 0)) for s in shapes]
    out_specs = [pl.BlockSpec((None, None) + s, lambda q, core_ref: (q, layer, 0, 0)) for s in shapes]
    args = [core, *grads, *received]
    in_specs = [own_of_kind[k] for k in kinds] + recv_specs
    aliases = {}
    if partials is not None:
        aliases = {len(args) + a: a for a in range(n_arr)}
        args += list(partials)
        in_specs += [HBM] * n_arr
    return pl.pallas_call(
        body, name=name,
        grid_spec=pltpu.PrefetchScalarGridSpec(
            num_scalar_prefetch=1, grid=(N_CHIP,), in_specs=in_specs, out_specs=out_specs),
        out_shape=[jax.ShapeDtypeStruct((N_CHIP, DEPTH) + s, BF16) for s in shapes],
        input_output_aliases=aliases,
        compiler_params=_params(dimension_semantics=("arbitrary",)),
    )(*args)


def _modulation_columns(c_all, w_ada):
    def body(c_ref, w_ref, cact_ref, out_ref):
        c_t = c_ref[...]
        c_act = c_t * _sigmoid(c_t)
        cact_ref[...] = c_act
        out_ref[...] = jnp.dot(c_act, w_ref[...], preferred_element_type=F32, precision=lax.Precision.HIGHEST)

    return pl.pallas_call(
        body, name="modulation_columns", grid=(DEPTH,),
        in_specs=[pl.BlockSpec((N_DEV, D_MODEL), lambda l: (0, 0)),
                  pl.BlockSpec((None, D_MODEL, W_IN_SHARD), lambda l: (l, 0, 0))],
        out_specs=[pl.BlockSpec((N_DEV, D_MODEL), lambda l: (0, 0)),
                   pl.BlockSpec((N_DEV, W_IN_SHARD), lambda l: (0, l))],
        out_shape=[jax.ShapeDtypeStruct((N_DEV, D_MODEL), F32),
                   jax.ShapeDtypeStruct((N_DEV, DEPTH * W_IN_SHARD), F32)],
        compiler_params=_params(dimension_semantics=("arbitrary",)),
    )(c_all, w_ada)


def _adamw(w, g, m, v):
    m_new = ADAM_B1 * m + (1.0 - ADAM_B1) * g
    v_new = ADAM_B2 * v + (1.0 - ADAM_B2) * (g * g)
    m_hat = m_new / (1.0 - ADAM_B1 ** ADAM_STEP)
    v_hat = v_new / (1.0 - ADAM_B2 ** ADAM_STEP)
    delta = -ADAM_LR * (m_hat / (jnp.sqrt(v_hat) + ADAM_EPS) + ADAM_WD * w)
    return delta, m_new, v_new


def _adamw_w_ada(w, m, v, c_act_t, dmod_cols):
    def body(w_ref, m_ref, v_ref, ct_ref, dm_ref, g_ref, d_ref, mo_ref, vo_ref):
        g = ct_ref[:, 0:1] * dm_ref[0:1, :]
        for b in range(1, N_DEV):
            g = g + ct_ref[:, b:b + 1] * dm_ref[b:b + 1, :]
        g_ref[...] = g
        d_ref[...], mo_ref[...], vo_ref[...] = _adamw(w_ref[...], g, m_ref[...], v_ref[...])

    big = pl.BlockSpec((None, D_MODEL, W_IN_SHARD), lambda l: (l, 0, 0))
    return pl.pallas_call(
        body, name="adamw_w_ada", grid=(DEPTH,),
        in_specs=[big, big, big, pl.BlockSpec((D_MODEL, N_DEV), lambda l: (0, 0)),
                  pl.BlockSpec((None, N_DEV, W_IN_SHARD), lambda l: (l, 0, 0))],
        out_specs=[big] * 4, out_shape=[jax.ShapeDtypeStruct(w.shape, F32)] * 4,
        compiler_params=_params(dimension_semantics=("arbitrary",)),
    )(w, m, v, c_act_t, dmod_cols)


def _sum_chip_partials(own_ref, recv_ref):
    g = own_ref[...].astype(F32)
    for j in range(N_OTHER_CHIPS):
        g = g + recv_ref[j].astype(F32)
    return g


def _partial_specs(row_tile, cols, first_layer=0):
    own = pl.BlockSpec((None, None, row_tile, cols), lambda l, r, chip_ref: (chip_ref[0], first_layer + l, r, 0))
    recv = pl.BlockSpec((N_OTHER_CHIPS, None, row_tile, cols), lambda l, r, chip_ref: (0, first_layer + l, r, 0))
    return own, recv


def _adamw_reduced(name, w, m, v, partial, received, chip, row_tile, layers, continued, after=()):
    depth, rows, cols = w.shape
    first, stop = layers

    def body(chip_ref, w_ref, m_ref, v_ref, own_ref, recv_ref, *rest):
        g_ref, d_ref, mo_ref, vo_ref = rest[-4:]
        g = _sum_chip_partials(own_ref, recv_ref)
        g_ref[...] = g
        d_ref[...], mo_ref[...], vo_ref[...] = _adamw(w_ref[...], g, m_ref[...], v_ref[...])

    blk = pl.BlockSpec((None, row_tile, cols), lambda l, r, chip_ref: (first + l, r, 0))
    args = [chip, w, m, v, partial, received]
    in_specs = [blk, blk, blk, *_partial_specs(row_tile, cols, first)]
    aliases = {}
    if continued is not None:
        aliases = {len(args) + k: k for k in range(4)}
        args += list(continued)
        in_specs += [HBM] * 4
    args += after
    in_specs += [HBM] * len(after)
    return pl.pallas_call(
        body, name=name,
        grid_spec=pltpu.PrefetchScalarGridSpec(
            num_scalar_prefetch=1, grid=(stop - first, rows // row_tile), in_specs=in_specs, out_specs=[blk] * 4),
        out_shape=[jax.ShapeDtypeStruct(w.shape, F32)] * 4, input_output_aliases=aliases,
        compiler_params=_params(dimension_semantics=("arbitrary", "arbitrary")),
    )(*args)


def _reduce_w_pool(partial, received, chip):
    def body(chip_ref, own_ref, recv_ref, g_ref):
        g_ref[...] = _sum_chip_partials(own_ref, recv_ref)

    return pl.pallas_call(
        body, name="reduce_w_pool",
        grid_spec=pltpu.PrefetchScalarGridSpec(
            num_scalar_prefetch=1, grid=(DEPTH, 1), in_specs=list(_partial_specs(POOL_SHARD, GROUP_D)),
            out_specs=pl.BlockSpec((None, POOL_SHARD, GROUP_D), lambda l, r, chip_ref: (l, 0, 0))),
        out_shape=jax.ShapeDtypeStruct((DEPTH, POOL_SHARD, GROUP_D), F32),
        compiler_params=_params(dimension_semantics=("arbitrary", "arbitrary")),
    )(chip, partial, received)


def _adamw_small(name, params):
    n = len(params)

    def body(*refs):
        ins, outs = refs[:4 * n], refs[4 * n:]
        for p in range(n):
            w_ref, g_ref, m_ref, v_ref = ins[4 * p:4 * p + 4]
            d_ref, mo_ref, vo_ref, go_ref = outs[4 * p:4 * p + 4]
            d_ref[...], mo_ref[...], vo_ref[...] = _adamw(w_ref[...], g_ref[...], m_ref[...], v_ref[...])
            go_ref[...] = g_ref[...]

    vmem = pl.BlockSpec(memory_space=pltpu.VMEM)
    flat = [a for group in params for a in group]
    outs = pl.pallas_call(
        body, name=name, in_specs=[vmem] * len(flat), out_specs=[vmem] * len(flat),
        out_shape=[jax.ShapeDtypeStruct(a.shape, F32) for a in flat], compiler_params=_params(),
    )(*flat)
    return [tuple(outs[4 * p:4 * p + 4]) for p in range(n)]


def _sum_sources(slabs, columns, after):
    def body(s_ref, *rest):
        outs = rest[len(after):]
        acc = s_ref[0]
        for b in range(1, N_DEV):
            acc = acc + s_ref[b]
        for (start, stop), o_ref in zip(columns, outs):
            o_ref[...] = acc[:, start:stop]
        outs[-1][...] = acc[0:1, SLAB_COLS:SLAB_COLS + 1]

    vmem = pl.BlockSpec(memory_space=pltpu.VMEM)
    out_shape = [jax.ShapeDtypeStruct((slabs.shape[1], stop - start), F32) for start, stop in columns]
    out_shape.append(jax.ShapeDtypeStruct((1, 1), F32))
    return pl.pallas_call(
        body, name="sum_small_grads", in_specs=[vmem] + [HBM] * len(after), out_specs=[vmem] * len(out_shape),
        out_shape=out_shape, compiler_params=_params(),
    )(slabs, *after)


def _to_bf16(a, name, layers=None):
    first, stop = layers or (0, a.shape[0])

    def body(a_ref, o_ref):
        o_ref[...] = a_ref[...].astype(BF16)

    block = (None,) + a.shape[1:]
    return pl.pallas_call(
        body, name=name, grid=(stop - first,), in_specs=[pl.BlockSpec(block, lambda l: (first + l, 0, 0))],
        out_specs=pl.BlockSpec(block, lambda l: (l, 0, 0)),
        out_shape=jax.ShapeDtypeStruct((stop - first,) + a.shape[1:], BF16),
        compiler_params=_params(dimension_semantics=("arbitrary",)),
    )(a)


def kernel(x, c, w_ada, b_ada, g_pre, w_in, w_conv, w_pool, pool_scale, w_out, g_post, loss_target, m_w_ada, m_b_ada, m_g_pre, m_w_in, m_w_conv, m_w_pool, m_pool_scale, m_w_out, m_g_post, v_w_ada, v_b_ada, v_g_pre, v_w_in, v_w_conv, v_w_pool, v_pool_scale, v_w_out, v_g_post):
    mx, my, mc = _mesh_position()
    me = _block_id(mx, my, mc)
    chip = (2 * mx + my).astype(jnp.int32).reshape(1)
    core = mc.astype(jnp.int32).reshape(1)
    x0 = x[0]
    target = loss_target[0]
    conv_shard = w_conv.shape[-1]

    own_small = jnp.concatenate([c, w_conv.reshape(1, DEPTH * 3 * conv_shard)], axis=1)
    first_shards = [_to_bf16(w_in, "cast_w_in_0", (0, 1)), _to_bf16(w_out, "cast_w_out_0", (0, 1))]
    all_small = _all_gather_small(own_small, "all_gather_c_w_conv", first_shards)[:, 0, :]
    gathers = [_gather_weights_start(0, *first_shards, [all_small], relayed=True)]
    c_all = all_small[:, :D_MODEL]
    w_conv_full = all_small[:, D_MODEL:].reshape(N_DEV, DEPTH, 3, conv_shard).transpose(1, 2, 0, 3).reshape(
        DEPTH, 3, CONV_W)
    cps = jnp.concatenate([w_conv_full, pool_scale[:, None], jnp.zeros((DEPTH, 4, CONV_W), F32)], axis=1)

    c_act, pieces = _modulation_columns(c_all, w_ada)
    upper = (1, DEPTH)
    upper_shards = [_to_bf16(w_in, "cast_w_in_1", upper), _to_bf16(w_out, "cast_w_out_1", upper)]
    wpool_b = _to_bf16(w_pool.reshape(DEPTH, POOL_ROWS, GROUP_D), "cast_w_pool").reshape(w_pool.shape)
    first_sems_0, _, (zones_0,) = gathers[0]
    neighbour_sems, zones_0 = _gather_weights_pass_on(
        "all_gather_weights_relay_0", NEIGHBOUR_CHIPS, first_sems_0[1], partial(_first_arrival, 0), zones_0,
        [pieces, *upper_shards, wpool_b, cps], relay=True)
    gather_mod = _all_gather_exchange(pieces)
    sems_m, pieces_thru, mod_landing, token_m = _start_exchange(gather_mod, "all_gather_modulation_start", [zones_0[0]])
    diagonal_sems, zones_0 = _gather_weights_pass_on(
        "all_gather_weights_pass_on_0", DIAGONAL_CHIP, neighbour_sems[3], lambda a, j: a, zones_0, [token_m])
    _, (mod_all,) = _finish_exchange(
        gather_mod, "all_gather_modulation_finish", sems_m, pieces_thru, mod_landing, [zones_0[0]])
    mod_mine = lax.dynamic_index_in_dim(mod_all, me, axis=1, keepdims=False)
    mod = mod_mine.reshape(N_DEV, DEPTH, W_IN_SHARD).transpose(1, 0, 2).reshape(DEPTH, 3 * D_MODEL) + b_ada
    zeros_d = jnp.zeros((DEPTH, 3, D_MODEL), F32)
    vec = jnp.concatenate([mod.reshape(DEPTH, 3, D_MODEL), g_pre[:, None], g_post[:, None], zeros_d], axis=1)

    gathers.append(_gather_weights_start(1, *upper_shards, [mod_all]))
    gathers = [(first_sems, shards, landing[k], k) for first_sems, shards, landing in gathers
               for k in range(len(landing))]

    xs, kept, wins, wouts = [x0], [], [], []
    for l in range(DEPTH):
        first_sems, shards, zones, index = gathers[l]
        if l == 0:
            passed = [(neighbour_sems[:2], NEIGHBOUR_CHIPS), (diagonal_sems, DIAGONAL_CHIP)]
            win, wout = _gather_weights_finish(l, index, first_sems, passed, shards, zones_0, neighbour_sems[2])
        else:
            passed_sems, zones = _gather_weights_pass_on(
                f"all_gather_weights_pass_on_{l}", ALL_OTHER_CHIPS, first_sems[1], partial(_first_arrival, index),
                zones, [xs[-1]])
            win, wout = _gather_weights_finish(l, index, first_sems, [(passed_sems, ALL_OTHER_CHIPS)], shards, zones)
        x_next, *for_backward = _forward_layer(
            l, xs[-1], vec, cps, wpool_b, win, wout, target if l == DEPTH - 1 else None)
        xs.append(x_next)
        kept.append(for_backward[:6])
        wins.append(win)
        wouts.append(wout)
    dx, loss_tile = xs[DEPTH], for_backward[6]

    slab_rows = [None] * DEPTH
    partials = received = None
    in_flight = []

    def scatter(layer, grads, from_sibling, after):
        nonlocal partials, received
        partials = _add_sibling_blocks(
            f"grad_add_sibling_{layer}", layer, grads, from_sibling, core, partials, ALL_KINDS)
        chips = _chips_exchange(layer, partials, received, ALL_KINDS)
        sems, partials, received, token = _start_exchange(chips, f"grad_chips_start_{layer}", after)
        in_flight.append((chips, sems, layer))
        return token

    grads_above = None
    issued = []
    for l in reversed(range(DEPTH)):
        y, h, ycat, uc, fa, fp = kept[l]
        dx, dproj, dy, gwpool, dcps, dvec = _backward_layer(
            l, dx, y, xs[l], uc, fa, fp, vec, cps, wpool_b, wouts[l], wins[l], issued)
        slab_rows[l] = jnp.concatenate(
            [dvec[0], dvec[1], dvec[2], dvec[3], dvec[4], dcps[3], dcps[0], dcps[1], dcps[2],
             loss_tile[0] if l == 0 else jnp.zeros((LANES,), F32)])
        after = []
        if l == 0:
            gather_small = _all_gather_exchange(jnp.stack(slab_rows))
            sems_s, slab, slabs, token = _start_exchange(gather_small, "all_gather_small_grads_start")
            after = [token]
        hosted = _sibling_exchange(grads_above, ALL_KINDS) if grads_above is not None else None
        gwin, gwout, *from_sibling = _weight_grads(l, h, dproj, ycat, dy, hosted, after)
        if l == 0:
            _, (slabs,) = _finish_exchange(
                gather_small, "all_gather_small_grads_finish", sems_s, slab, slabs, [gwin])
        issued = [gwin]
        if grads_above is not None:
            issued = [scatter(l + 1, grads_above, from_sibling, [])]
        grads_above = [gwin, gwout, gwpool.reshape(POOL_ROWS, GROUP_D)]
        if l <= 1:
            from_sibling = _run_exchange(_sibling_exchange(grads_above, ALL_KINDS), f"grad_exchange_sibling_{l}")
            token = scatter(l, grads_above, from_sibling, [slabs] if l == 0 else [])
            issued = [token]
            grads_above = None
    grad_x = dx[None]
    chips_0, sems_0, _ = in_flight.pop()

    o = 3 * D_MODEL

    after = [token]
    for chips, sems, l in in_flight:
        partials, received = _finish_exchange(chips, f"grad_chips_finish_{l}", sems, partials, received, after)
        after = []
    upper = (1, DEPTH)
    w_in_upper = _adamw_reduced(
        "adamw_w_in_upper", w_in, m_w_in, v_w_in, partials[0], received[0], chip, ROW_TILE, upper, None)
    w_out_upper = _adamw_reduced(
        "adamw_w_out_upper", w_out, m_w_out, v_w_out, partials[1], received[1], chip, W_OUT_SHARD, upper, None)
    dmod_all = slabs[:, :, :o].reshape(N_DEV, DEPTH, N_DEV, W_IN_SHARD)
    dmod_cols = lax.dynamic_index_in_dim(dmod_all, me, axis=2, keepdims=False).transpose(1, 0, 2) + token[0, 0]
    g_w_ada, d_w_ada, nm_w_ada, nv_w_ada = _adamw_w_ada(w_ada, m_w_ada, v_w_ada, c_act.T, dmod_cols)

    partials, received = _finish_exchange(
        chips_0, "grad_chips_finish_0", sems_0, partials, received, [nv_w_ada, w_in_upper[3], w_out_upper[3]])
    gather_pool = _all_gather_exchange(_reduce_w_pool(partials[2], received[2], chip), axis=1)
    sems_p, pool_rows, pool_landing, token_p = _start_exchange(gather_pool, "all_gather_grad_w_pool_start")
    g_w_in, d_w_in, nm_w_in, nv_w_in = _adamw_reduced(
        "adamw_w_in_0", w_in, m_w_in, v_w_in, partials[0], received[0], chip, ROW_TILE, (0, 1), w_in_upper, [token_p])
    g_w_out, d_w_out, nm_w_out, nv_w_out = _adamw_reduced(
        "adamw_w_out_0", w_out, m_w_out, v_w_out, partials[1], received[1], chip, W_OUT_SHARD, (0, 1), w_out_upper,
        [token_p])
    ends = [0, o, o + D_MODEL, o + 2 * D_MODEL, o + 2 * D_MODEL + POOL_W, SLAB_COLS]
    g_b_ada, g_g_pre, g_g_post, g_pool_scale, g_conv, loss = _sum_sources(
        slabs, list(zip(ends[:-1], ends[1:])), [token_p, nv_w_in, nv_w_out])
    loss = loss.reshape(())
    g_w_conv = lax.dynamic_slice_in_dim(g_conv.reshape(DEPTH, 3, CONV_W), me * conv_shard, conv_shard, axis=2)
    taps_first = lambda a: a.transpose(1, 0, 2)
    small = _adamw_small("adamw_small", [
        (b_ada, g_b_ada, m_b_ada, v_b_ada),
        (g_pre, g_g_pre, m_g_pre, v_g_pre),
        tuple(taps_first(a) for a in (w_conv, g_w_conv, m_w_conv, v_w_conv)),
        (pool_scale, g_pool_scale, m_pool_scale, v_pool_scale),
        (g_post, g_g_post, m_g_post, v_g_post),
    ])
    (d_b_ada, nm_b_ada, nv_b_ada, g_b_ada), (d_g_pre, nm_g_pre, nv_g_pre, g_g_pre), conv_steps, \
        (d_ps, nm_ps, nv_ps, g_pool_scale), (d_g_post, nm_g_post, nv_g_post, g_g_post) = small
    d_w_conv, nm_w_conv, nv_w_conv, g_w_conv = (taps_first(a) for a in conv_steps)
    _, (g_pool_all,) = _finish_exchange(
        gather_pool, "all_gather_grad_w_pool_finish", sems_p, pool_rows, pool_landing, [nv_w_in, nv_w_out, nv_g_post])
    ((d_w_pool, nm_w_pool, nv_w_pool, g_w_pool),) = _adamw_small(
        "adamw_w_pool", [(w_pool, g_pool_all.reshape(w_pool.shape), m_w_pool, v_w_pool)])

    return (loss, grad_x,
            g_w_ada, g_b_ada, g_g_pre, g_w_in, g_w_conv, g_w_pool, g_pool_scale, g_w_out, g_g_post,
            d_w_ada, d_b_ada, d_g_pre, d_w_in, d_w_conv, d_w_pool, d_ps, d_w_out, d_g_post,
            nm_w_ada, nm_b_ada, nm_g_pre, nm_w_in, nm_w_conv, nm_w_pool, nm_ps, nm_w_out, nm_g_post,
            nv_w_ada, nv_b_ada, nv_g_pre, nv_w_in, nv_w_conv, nv_w_pool, nv_ps, nv_w_out, nv_g_post)
```

```python
from functools import partial

import jax
import jax.numpy as jnp
from jax import lax
from jax.experimental import pallas as pl
from jax.experimental.pallas import tpu as pltpu

F32 = jnp.float32
BF16 = jnp.bfloat16

D_MODEL = 1024
DEPTH = 4
CONV_W = 512
POOL_W = 512
POOL_WINDOWS = (2, 4, 8, 16)
GROUP_D = 128
IN_COLS = 4 * CONV_W + 2 * POOL_W
NORM_EPS = 1e-6

ADAM_LR = 0.001
ADAM_B1 = 0.9
ADAM_B2 = 0.999
ADAM_EPS = 1e-08
ADAM_WD = 0.01
ADAM_STEP = 10

N_DEV = 8
N_CHIP = 4
N_OTHER_CHIPS = N_CHIP - 1
MESH = pl.DeviceIdType.MESH
W_IN_SHARD = IN_COLS // N_DEV
W_OUT_SHARD = D_MODEL // N_DEV
POOL_ROWS = len(POOL_WINDOWS) * GROUP_D
POOL_SHARD = POOL_ROWS // N_DEV

SUBLANES = 8
LANES = 128
VMEM_LIMIT_BYTES = 56 * 1024 * 1024
ROW_TILE = 512
BWD_TILE = 256
GWIN_COLS = 768
GWOUT_COLS = 512
POOL_HALO = 16
CONV_HALO = SUBLANES

SLAB_COLS = 3 * D_MODEL + D_MODEL + D_MODEL + POOL_W + 3 * CONV_W

HBM = pl.BlockSpec(memory_space=pl.ANY)


def _params(**kw):
    return pltpu.CompilerParams(vmem_limit_bytes=VMEM_LIMIT_BYTES, **kw)


def _sigmoid(v):
    return 1.0 / (1.0 + jnp.exp(-v))


def _dot(a, b):
    return jnp.dot(a, b, preferred_element_type=F32)


def _dot_tn(a, b):
    return lax.dot_general(a, b, (((0,), (0,)), ((), ())), preferred_element_type=F32)


def _dot_nt(a, b):
    return lax.dot_general(a, b, (((1,), (1,)), ((), ())), preferred_element_type=F32)


def _rows_from_before(v, k):
    return pltpu.roll(v, k, 0)


def _rows_from_after(v, k):
    return pltpu.roll(v, v.shape[0] - k, 0)


def _window_counts(t0, rows):
    return (lax.broadcasted_iota(jnp.int32, (rows, 1), 0) + (t0 + 1)).astype(F32)


def _split_proj(p32):
    cw = CONV_W
    return (p32[:, 0 * cw:1 * cw], p32[:, 1 * cw:2 * cw], p32[:, 2 * cw:3 * cw], p32[:, 3 * cw:4 * cw],
            p32[:, 4 * cw:4 * cw + POOL_W], p32[:, 4 * cw + POOL_W:])


def _layer_spec(shape, layer):
    nd = len(shape)
    return pl.BlockSpec((None,) + tuple(shape[1:]), lambda i, _l=layer, _n=nd: (_l,) + (0,) * (_n - 1))


def _whole_spec(shape):
    return pl.BlockSpec(tuple(shape), lambda i, _n=len(shape): (0,) * _n, pipeline_mode=pl.Buffered(1))


def _mesh_position():
    return lax.axis_index("x"), lax.axis_index("y"), lax.axis_index("c")


def _block_id(x, y, c):
    return 4 * x + 2 * y + c


def _other_chips(x, y):
    return [(x ^ 1, y), (x, y ^ 1), (x ^ 1, y ^ 1)]


def _col_block(ref, blk):
    return ref.at[:, pl.ds(pl.multiple_of(blk * W_IN_SHARD, LANES), W_IN_SHARD)]


def _row_block(rows):
    def block(ref, blk):
        return ref.at[pl.ds(pl.multiple_of(blk * rows, rows), rows), :]
    return block


_BLOCK_OF = (_col_block, _row_block(W_OUT_SHARD), _row_block(POOL_SHARD))
_BLOCK_SHAPES = ((D_MODEL, W_IN_SHARD), (W_OUT_SHARD, D_MODEL), (POOL_SHARD, GROUP_D))


class _Exchange:
    def __init__(self, inputs, out_shapes, aliases, sem_shapes, make):
        self.inputs, self.out_shapes, self.aliases, self.sem_shapes, self.make = (
            list(inputs), list(out_shapes), dict(aliases), list(sem_shapes), make)


def _run_exchange(exchange, name):
    n_in, n_out = len(exchange.inputs), len(exchange.out_shapes)

    def body(*refs):
        start, finish = exchange.make(refs[:n_in], refs[n_in:n_in + n_out], refs[n_in + n_out:])
        start()
        finish()

    return pl.pallas_call(
        body, name=name, in_specs=[HBM] * n_in, out_specs=[HBM] * n_out, out_shape=exchange.out_shapes,
        scratch_shapes=exchange.sem_shapes, input_output_aliases=exchange.aliases, compiler_params=_params(),
    )(*exchange.inputs)


_SEM = pl.BlockSpec(memory_space=pltpu.SEMAPHORE)
_DATAFLOW = pltpu.SideEffectType.DATAFLOW_SIDE_EFFECTING


def _start_exchange(exchange, name, after=()):
    n_in, n_out, n_sem = len(exchange.inputs), len(exchange.out_shapes), len(exchange.sem_shapes)
    sources = [i for i in range(n_in) if i not in exchange.aliases]
    aliases = {i: n_sem + k for k, i in enumerate(sources)}
    aliases.update({i: n_sem + len(sources) + o for i, o in exchange.aliases.items()})

    def body(*refs):
        in_refs = refs[:n_in]
        outs = refs[n_in + len(after):]
        sems = outs[:n_sem]
        out_refs = outs[n_sem + len(sources):n_sem + len(sources) + n_out]
        exchange.make(in_refs, out_refs, sems)[0]()
        refs[-1][...] = jnp.zeros_like(refs[-1])

    outs = pl.pallas_call(
        body, name=name, in_specs=[HBM] * (n_in + len(after)),
        out_specs=[_SEM] * n_sem + [HBM] * (len(sources) + n_out) + [pl.BlockSpec(memory_space=pltpu.VMEM)],
        out_shape=(exchange.sem_shapes + [pltpu.HBM(exchange.inputs[i].shape, exchange.inputs[i].dtype) for i in sources]
                   + [pltpu.HBM(s.shape, s.dtype) for s in exchange.out_shapes]
                   + [jax.ShapeDtypeStruct((SUBLANES, LANES), F32)]),
        input_output_aliases=aliases, compiler_params=_params(has_side_effects=_DATAFLOW),
    )(*exchange.inputs, *after)
    return outs[:n_sem], outs[n_sem:n_sem + len(sources)], outs[n_sem + len(sources):-1], outs[-1]


def _finish_exchange(exchange, name, sems, sources, landing, after):
    n_src, n_out, n_sem = len(sources), len(landing), len(sems)
    n_in = len(exchange.inputs)
    source_at = [i for i in range(n_in) if i not in exchange.aliases]

    def body(*refs):
        src_refs, out_refs = refs[:n_src], refs[n_src:n_src + n_out]
        sem_refs = refs[n_src + n_out:n_src + n_out + n_sem]
        in_refs = [None] * n_in
        for k, i in enumerate(source_at):
            in_refs[i] = src_refs[k]
        for i, o in exchange.aliases.items():
            in_refs[i] = out_refs[o]
        exchange.make(in_refs, out_refs, sem_refs)[1]()

    arrays = list(sources) + list(landing)
    outs = pl.pallas_call(
        body, name=name, in_specs=[HBM] * len(arrays) + [_SEM] * n_sem + [HBM] * len(after),
        out_specs=[HBM] * len(arrays), out_shape=[pltpu.HBM(a.shape, a.dtype) for a in arrays],
        input_output_aliases={i: i for i in range(len(arrays))}, compiler_params=_params(has_side_effects=_DATAFLOW),
    )(*arrays, *sems, *after)
    return outs[:n_src], outs[n_src:]


N_GATHERED = 2
FIRST_COPIES = 1 + N_OTHER_CHIPS
_GATHERED_SHAPES = ((D_MODEL, IN_COLS), (D_MODEL, D_MODEL))
ALL_OTHER_CHIPS, NEIGHBOUR_CHIPS, DIAGONAL_CHIP = (0, 1, 2), (0, 1), (2,)


def _first_sem(layer, a, k):
    return (layer * N_GATHERED + a) * FIRST_COPIES + k


def _first_arrival(layer, a, j):
    return _first_sem(layer, a, 1 + j)


def _gather_copy(window_of, full_ref, blk, send_sem, recv_sem, to, src=None):
    window = window_of(full_ref, blk)
    return pltpu.make_async_remote_copy(
        src_ref=window if src is None else src, dst_ref=window, send_sem=send_sem, recv_sem=recv_sem,
        device_id=to, device_id_type=MESH)


def _first_copies(layer, shard_refs, full_refs, send_sems, recv_sems, local_sems, relayed):
    x, y, c = _mesh_position()
    me = _block_id(x, y, c)
    chips = [_other_chips(x, y)[j] for j in (NEIGHBOUR_CHIPS if relayed else ALL_OTHER_CHIPS)]
    own, remote = [], []
    for a in range(N_GATHERED):
        shard = shard_refs[a].at[layer]
        own.append(pltpu.make_async_copy(
            shard, _BLOCK_OF[a](full_refs[a], me), local_sems.at[layer * N_GATHERED + a]))
        targets = [(x, y, 1 - c)] + [(*chip, c) for chip in chips]
        remote += [_gather_copy(_BLOCK_OF[a], full_refs[a], me, send_sems.at[_first_sem(layer, a, k)],
                                recv_sems.at[_first_sem(layer, a, k)], to, src=shard)
                   for k, to in enumerate(targets)]
    return own, remote


def _gather_weights_start(first_layer, win_shards, wout_shards, after, relayed=False):
    n_layers = win_shards.shape[0]
    n_first = n_layers * N_GATHERED * FIRST_COPIES
    sem_shapes = [pltpu.SemaphoreType.DMA((n_first,)), pltpu.SemaphoreType.DMA((n_first,)),
                  pltpu.SemaphoreType.DMA((n_layers * N_GATHERED,))]
    shards = [win_shards, wout_shards]

    def body(win_sh, wout_sh, *rest):
        send_sems, recv_sems, local_sems, win_thru, wout_thru, *landing = rest[len(after):]
        for layer in range(n_layers):
            own, remote = _first_copies(layer, (win_sh, wout_sh), landing[N_GATHERED * layer:N_GATHERED * (layer + 1)],
                                        send_sems, recv_sems, local_sems, relayed)
            for cp in own + remote:
                cp.start()

    outs = pl.pallas_call(
        body, name=f"all_gather_weights_start_{first_layer}", in_specs=[HBM] * (2 + len(after)),
        out_specs=[_SEM] * 3 + [HBM] * (2 + n_layers * N_GATHERED),
        out_shape=(sem_shapes + [pltpu.HBM(s.shape, s.dtype) for s in shards]
                   + [pltpu.HBM(s, BF16) for _ in range(n_layers) for s in _GATHERED_SHAPES]),
        input_output_aliases={0: 3, 1: 4}, compiler_params=_params(has_side_effects=_DATAFLOW),
    )(*shards, *after)
    landing = outs[5:]
    return outs[:3], outs[3:5], [landing[N_GATHERED * l:N_GATHERED * (l + 1)] for l in range(n_layers)]


def _passed_on_copies(full_refs, send_sems, recv_sems, core_of_block, chips):
    x, y, c = _mesh_position()
    return [_gather_copy(_BLOCK_OF[a], full_refs[a], _block_id(*_other_chips(x, y)[j], core_of_block),
                         send_sems.at[a * N_OTHER_CHIPS + j], recv_sems.at[a * N_OTHER_CHIPS + j], (x, y, 1 - c))
            for a in range(N_GATHERED) for j in chips]


def _relayed_copies(full_refs, send_sems, recv_sems):
    x, y, c = _mesh_position()
    source, to = (x ^ (1 - c), y ^ c), (x ^ c, y ^ (1 - c))
    return [_gather_copy(_BLOCK_OF[a], full_refs[a], _block_id(*source, c), send_sems.at[a], recv_sems.at[a], (*to, c))
            for a in range(N_GATHERED)]


def _gather_weights_pass_on(name, chips, arrival_sems, arrival_sem_of, landing, after, relay=False):
    n = N_GATHERED * N_OTHER_CHIPS
    sem_shapes = [pltpu.SemaphoreType.DMA((n,))] * 2 + [pltpu.SemaphoreType.DMA((N_GATHERED,))] * (2 if relay else 0)

    def body(win_ref, wout_ref, arrivals, *rest):
        sems = rest[len(after):len(after) + len(sem_shapes)]
        x, y, c = _mesh_position()
        full_refs = (win_ref, wout_ref)
        passed = _passed_on_copies(full_refs, sems[0], sems[1], c, chips)
        relayed = _relayed_copies(full_refs, sems[2], sems[3]) if relay else []
        for a in range(N_GATHERED):
            for j in chips:
                sem = arrivals.at[arrival_sem_of(a, j)]
                _gather_copy(_BLOCK_OF[a], full_refs[a], _block_id(*_other_chips(x, y)[j], c), sem, sem,
                             (x, y, c)).wait_recv()
            for cp in relayed[a:a + 1] + passed[a * len(chips):(a + 1) * len(chips)]:
                cp.start()

    outs = pl.pallas_call(
        body, name=name, in_specs=[HBM] * N_GATHERED + [_SEM] + [HBM] * len(after),
        out_specs=[_SEM] * len(sem_shapes) + [HBM] * N_GATHERED,
        out_shape=sem_shapes + [pltpu.HBM(a.shape, a.dtype) for a in landing],
        input_output_aliases={a: len(sem_shapes) + a for a in range(N_GATHERED)},
        compiler_params=_params(has_side_effects=_DATAFLOW),
    )(*landing, arrival_sems, *after)
    return outs[:len(sem_shapes)], outs[len(sem_shapes):]


def _gather_weights_finish(layer, index, first_sems, passed, shards, landing, relay_send_sems=None, after=()):
    relayed = relay_send_sems is not None
    passed_sems = [sem for (send, recv), _ in passed for sem in (send, recv)] + ([relay_send_sems] if relayed else [])

    def body(win_ref, wout_ref, first_send, first_recv, local_sems, *rest):
        sems, (win_sh, wout_sh) = rest[:len(passed_sems)], rest[len(passed_sems):len(passed_sems) + 2]
        x, y, c = _mesh_position()
        full_refs = (win_ref, wout_ref)
        own, sent = _first_copies(index, (win_sh, wout_sh), full_refs, first_send, first_recv, local_sems, relayed)
        for a in range(N_GATHERED):
            sem = _first_sem(index, a, 0)
            _gather_copy(_BLOCK_OF[a], full_refs[a], _block_id(x, y, 1 - c), first_recv.at[sem], first_recv.at[sem],
                         (x, y, c)).wait_recv()
        for p, (_, chips) in enumerate(passed):
            for cp in _passed_on_copies(full_refs, sems[2 * p], sems[2 * p + 1], 1 - c, chips):
                cp.wait_recv()
            sent += _passed_on_copies(full_refs, sems[2 * p], sems[2 * p + 1], c, chips)
        if relayed:
            sent += _relayed_copies(full_refs, sems[-1], sems[-1])
        for cp in sent:
            cp.wait_send()
        for cp in own:
            cp.wait()

    return pl.pallas_call(
        body, name=f"all_gather_weights_finish_{layer}",
        in_specs=[HBM] * N_GATHERED + [_SEM] * (3 + len(passed_sems)) + [HBM] * (2 + len(after)),
        out_specs=[HBM] * N_GATHERED, out_shape=[pltpu.HBM(a.shape, a.dtype) for a in landing],
        input_output_aliases={a: a for a in range(N_GATHERED)}, compiler_params=_params(has_side_effects=_DATAFLOW),
    )(*landing, *first_sems, *passed_sems, *shards, *after)


ALL_KINDS = (0, 1, 2)


def _sibling_exchange(grads, kinds):
    n_arr = len(grads)

    def make(in_refs, out_refs, sems):
        send_sems, recv_sems = sems
        x, y, c = _mesh_position()
        copies = [pltpu.make_async_remote_copy(
            src_ref=_BLOCK_OF[kinds[a]](in_refs[a], 2 * q + (1 - c)), dst_ref=out_refs[a].at[q],
            send_sem=send_sems.at[a * N_CHIP + q], recv_sem=recv_sems.at[a * N_CHIP + q],
            device_id=(x, y, 1 - c), device_id_type=MESH)
            for a in range(n_arr) for q in range(N_CHIP)]

        def start():
            for cp in copies:
                cp.start()

        def finish():
            for cp in copies:
                cp.wait_recv()
            for cp in copies:
                cp.wait_send()

        return start, finish

    return _Exchange(
        grads, [jax.ShapeDtypeStruct((N_CHIP,) + _BLOCK_SHAPES[k], BF16) for k in kinds], {},
        [pltpu.SemaphoreType.DMA((n_arr * N_CHIP,)), pltpu.SemaphoreType.DMA((n_arr * N_CHIP,))], make)


def _chips_exchange(layer, partials, received, kinds):
    n_arr = len(partials)

    def make(in_refs, out_refs, sems):
        send_sems, recv_sems = sems
        x, y, c = _mesh_position()
        copies = [pltpu.make_async_remote_copy(
            src_ref=in_refs[a].at[2 * qx + qy, layer], dst_ref=out_refs[a].at[j, layer],
            send_sem=send_sems.at[a * N_OTHER_CHIPS + j], recv_sem=recv_sems.at[a * N_OTHER_CHIPS + j],
            device_id=(qx, qy, c), device_id_type=MESH)
            for a in range(n_arr) for j, (qx, qy) in enumerate(_other_chips(x, y))]

        def start():
            for cp in copies:
                cp.start()

        def finish():
            for cp in copies:
                cp.wait_recv()
            for cp in copies:
                cp.wait_send()

        return start, finish

    inputs = list(partials)
    aliases = {}
    if received is not None:
        inputs += list(received)
        aliases = {n_arr + a: a for a in range(n_arr)}
    return _Exchange(
        inputs, [jax.ShapeDtypeStruct((N_OTHER_CHIPS, DEPTH) + _BLOCK_SHAPES[k], BF16) for k in kinds], aliases,
        [pltpu.SemaphoreType.DMA((n_arr * N_OTHER_CHIPS,)), pltpu.SemaphoreType.DMA((n_arr * N_OTHER_CHIPS,))], make)


def _all_gather_exchange(v, axis=0):
    def make(in_refs, out_refs, sems):
        send_sems, recv_sems, local_sem = sems
        x, y, c = _mesh_position()
        me = _block_id(x, y, c)
        block = lambda blk: out_refs[0].at[(slice(None),) * axis + (blk,)]
        own = pltpu.make_async_copy(in_refs[0], block(me), local_sem.at[0])
        sends, arrivals = [], []
        for k in range(1, N_DEV):
            px, py, pc = x ^ ((k >> 2) & 1), y ^ ((k >> 1) & 1), c ^ (k & 1)
            sends.append(pltpu.make_async_remote_copy(
                src_ref=in_refs[0], dst_ref=block(me), send_sem=send_sems.at[k - 1],
                recv_sem=recv_sems.at[k - 1], device_id=(px, py, pc), device_id_type=MESH))
            arrivals.append(pltpu.make_async_remote_copy(
                src_ref=in_refs[0], dst_ref=block(_block_id(px, py, pc)), send_sem=send_sems.at[k - 1],
                recv_sem=recv_sems.at[k - 1], device_id=(x, y, c), device_id_type=MESH))

        def start():
            for cp in [own] + sends:
                cp.start()

        def finish():
            for cp in arrivals:
                cp.wait_recv()
            for cp in sends:
                cp.wait_send()
            own.wait()

        return start, finish

    return _Exchange(
        [v], [jax.ShapeDtypeStruct(v.shape[:axis] + (N_DEV,) + v.shape[axis:], v.dtype)], {},
        [pltpu.SemaphoreType.DMA((N_DEV - 1,)), pltpu.SemaphoreType.DMA((N_DEV - 1,)),
         pltpu.SemaphoreType.DMA((1,))], make)


def _host(exchange, args, in_specs, out_shape, out_specs, scratch):
    n_own = (len(args), len(out_shape), len(scratch))
    if exchange is None:
        return {}, lambda refs: (refs, None)
    n_ex = (len(exchange.inputs), len(exchange.out_shapes), len(exchange.sem_shapes))
    aliases = {n_own[0] + i: n_own[1] + o for i, o in exchange.aliases.items()}
    args += exchange.inputs
    in_specs += [HBM] * n_ex[0]
    out_shape += exchange.out_shapes
    out_specs += [HBM] * n_ex[1]
    scratch += exchange.sem_shapes

    def split(refs):
        own, theirs, at = [], [], 0
        for mine, ex in zip(n_own, n_ex):
            own += refs[at:at + mine]
            theirs.append(refs[at + mine:at + mine + ex])
            at += mine + ex
        return own, exchange.make(*theirs)

    return aliases, split


def _all_gather_small(v, name, after=()):
    vmem = pl.BlockSpec(memory_space=pltpu.VMEM)

    def body(v_ref, *rest):
        out_ref, send_sems, recv_sems = rest[len(after):]
        x, y, c = _mesh_position()
        me = _block_id(x, y, c)
        out_ref[me] = v_ref[...]
        sends = []
        for k in range(1, N_DEV):
            px, py, pc = x ^ ((k >> 2) & 1), y ^ ((k >> 1) & 1), c ^ (k & 1)
            send = pltpu.make_async_remote_copy(
                src_ref=v_ref, dst_ref=out_ref.at[me], send_sem=send_sems.at[k - 1], recv_sem=recv_sems.at[k - 1],
                device_id=(px, py, pc), device_id_type=MESH)
            send.start()
            sends.append((send, _block_id(px, py, pc)))
        for k, (send, peer) in enumerate(sends):
            pltpu.make_async_remote_copy(
                src_ref=v_ref, dst_ref=out_ref.at[peer], send_sem=send_sems.at[k], recv_sem=recv_sems.at[k],
                device_id=(x, y, c), device_id_type=MESH).wait_recv()
        for send, _ in sends:
            send.wait_send()

    return pl.pallas_call(
        body, name=name, in_specs=[vmem] + [HBM] * len(after), out_specs=vmem,
        out_shape=jax.ShapeDtypeStruct((N_DEV,) + v.shape, v.dtype),
        scratch_shapes=[pltpu.SemaphoreType.DMA((N_DEV - 1,)), pltpu.SemaphoreType.DMA((N_DEV - 1,))],
        compiler_params=_params(),
    )(v, *after)


def _forward_layer(layer, x, vec, cps, wpool, win, wout, target=None):
    t_len = x.shape[0]
    n_tiles = t_len // ROW_TILE
    row = lambda cols: pl.BlockSpec((ROW_TILE, cols), lambda i: (i, 0))
    widths = (D_MODEL, 2 * CONV_W, 3 * CONV_W, 4 * POOL_W)
    head = target is not None

    def body(x_ref, vec_ref, cps_ref, wpool_ref, win_ref, wout_ref, *rest):
        target_ref = rest[0] if head else None
        xo_ref, y_ref, h_ref, ycat_ref, uc_ref, fa_ref, fp_ref = rest[head:head + 7]
        loss_ref = rest[head + 7] if head else None
        zc_ref, pc_ref = rest[-2:]
        i = pl.program_id(0)

        @pl.when(i == 0)
        def _():
            zc_ref[...] = jnp.zeros_like(zc_ref)
            pc_ref[...] = jnp.zeros_like(pc_ref)
            if head:
                loss_ref[...] = jnp.zeros_like(loss_ref)

        x_t = x_ref[...]
        shift, scale, gate = vec_ref[0:1, :], vec_ref[1:2, :], vec_ref[2:3, :]
        g_pre, g_post = vec_ref[3:4, :], vec_ref[4:5, :]
        w0, w1, w2, ps = cps_ref[0:1, :], cps_ref[1:2, :], cps_ref[2:3, :], cps_ref[3:4, :]
        rx = lax.rsqrt(jnp.mean(x_t * x_t, axis=-1, keepdims=True) + NORM_EPS)
        h = (x_t * rx) * g_pre * (1.0 + scale) + shift
        h_ref[...] = h.astype(BF16)
        proj = _dot(h.astype(BF16), win_ref[...])
        u_a, b_a, c_a, g_a, u_p, g_p = _split_proj(proj)
        uc_ref[...] = jnp.concatenate([u_a, c_a], axis=1).astype(BF16)

        z = c_a * u_a
        zcat = jnp.concatenate([zc_ref[...], z], axis=0)
        zc_ref[...] = z[ROW_TILE - CONV_HALO:]
        conv = (w0 * _rows_from_before(zcat, 2)[CONV_HALO:] + w1 * _rows_from_before(zcat, 1)[CONV_HALO:] + w2 * z)
        sig_a = _sigmoid(g_a)
        silu_a = g_a * sig_a
        b_conv = b_a * conv
        y_a = b_conv * silu_a
        fa_ref[...] = jnp.concatenate(
            [silu_a * conv, silu_a * b_a, b_conv * (sig_a + silu_a * (1.0 - sig_a))], axis=1).astype(BF16)

        pcat = jnp.concatenate([pc_ref[...], u_p], axis=0)
        pc_ref[...] = u_p[ROW_TILE - POOL_HALO:]
        counts = _window_counts(i * ROW_TILE, ROW_TILE)
        pooled, mixed = [], []
        for g, w in enumerate(POOL_WINDOWS):
            cols = slice(g * GROUP_D, (g + 1) * GROUP_D)
            s = pcat[:, cols]
            step = 1
            while step < w:
                s = s + _rows_from_before(s, step)
                step *= 2
            pooled_g = (s[POOL_HALO:] * (1.0 / jnp.minimum(counts, float(w))) - u_p[:, cols]).astype(BF16)
            pooled.append(pooled_g)
            mixed.append(_dot(pooled_g, wpool_ref[g]))
        mixed = jnp.concatenate(mixed, axis=1)
        sig_p = _sigmoid(g_p)
        silu_p = g_p * sig_p
        mixed_ps = mixed * ps
        y_p = mixed_ps * silu_p
        fp_ref[...] = jnp.concatenate(
            [(ps * silu_p).astype(BF16), (mixed_ps * (sig_p + silu_p * (1.0 - sig_p))).astype(BF16),
             (silu_p * mixed).astype(BF16)] + pooled, axis=1)

        ycat = jnp.concatenate([y_a, y_p], axis=1)
        ycat_ref[...] = ycat.astype(BF16)
        y_b = _dot(ycat.astype(BF16), wout_ref[...]).astype(BF16)
        y_ref[...] = y_b
        y_t = y_b.astype(F32)
        ry = lax.rsqrt(jnp.mean(y_t * y_t, axis=-1, keepdims=True) + NORM_EPS)
        x_next = x_t + gate * (y_t * ry * g_post)
        if head:
            err = x_next - target_ref[...]
            xo_ref[...] = err * (1.0 / D_MODEL)
            loss_ref[...] += jnp.sum(err * err) * (0.5 / D_MODEL)
        else:
            xo_ref[...] = x_next

    tile = (SUBLANES, LANES)
    return pl.pallas_call(
        body, name=f"forward_layer_{layer}", grid=(n_tiles,),
        in_specs=[row(D_MODEL), _layer_spec(vec.shape, layer), _layer_spec(cps.shape, layer),
                  _layer_spec(wpool.shape, layer), _whole_spec(win.shape), _whole_spec(wout.shape)]
        + [row(D_MODEL)] * head,
        out_specs=[row(D_MODEL), row(D_MODEL), row(D_MODEL), row(D_MODEL)] + [row(w) for w in widths[1:]]
        + [_whole_spec(tile)] * head,
        out_shape=[jax.ShapeDtypeStruct((t_len, D_MODEL), F32), jax.ShapeDtypeStruct((t_len, D_MODEL), BF16),
                   jax.ShapeDtypeStruct((t_len, D_MODEL), BF16), jax.ShapeDtypeStruct((t_len, D_MODEL), BF16)]
        + [jax.ShapeDtypeStruct((t_len, w), BF16) for w in widths[1:]] + [jax.ShapeDtypeStruct(tile, F32)] * head,
        scratch_shapes=[pltpu.VMEM((CONV_HALO, CONV_W), F32), pltpu.VMEM((POOL_HALO, POOL_W), F32)],
        compiler_params=_params(dimension_semantics=("arbitrary",)),
    )(x, vec, cps, wpool, win, wout, *([target] * head))


def _backward_layer(layer, dxo, y, x, uc, fa, fp, vec, cps, wpool, wout, win, after):
    t_len = dxo.shape[0]
    n_tiles = t_len // BWD_TILE
    halo_per_tile = BWD_TILE // POOL_HALO
    rev = lambda cols: pl.BlockSpec((BWD_TILE, cols), lambda i: (n_tiles - 1 - i, 0))
    halo_spec = pl.BlockSpec(
        (POOL_HALO, 2 * CONV_W), lambda i: (jnp.maximum((n_tiles - 1 - i) * halo_per_tile - 1, 0), 0))
    gwpool_shape = (len(POOL_WINDOWS), GROUP_D, GROUP_D)

    def body(dxo_ref, y_ref, x_ref, uc_ref, uch_ref, fa_ref, fp_ref, vec_ref, cps_ref, wpool_ref, wout_ref, win_ref,
             *rest):
        dx_ref, dproj_ref, dy_ref, gwpool_ref, dcps_ref, dvec_ref, gwpool_acc, dcc_ref, qc_ref = rest[len(after):]
        i = pl.program_id(0)
        tile = n_tiles - 1 - i

        @pl.when(i == 0)
        def _():
            gwpool_acc[...] = jnp.zeros_like(gwpool_acc)
            dcps_ref[...] = jnp.zeros_like(dcps_ref)
            dvec_ref[...] = jnp.zeros_like(dvec_ref)
            dcc_ref[...] = jnp.zeros_like(dcc_ref)
            qc_ref[...] = jnp.zeros_like(qc_ref)

        shift, scale, gate = vec_ref[0:1, :], vec_ref[1:2, :], vec_ref[2:3, :]
        g_pre, g_post = vec_ref[3:4, :], vec_ref[4:5, :]
        w0, w1, w2 = cps_ref[0:1, :], cps_ref[1:2, :], cps_ref[2:3, :]

        dxo_t = dxo_ref[...]
        y_t = y_ref[...].astype(F32)
        ry = lax.rsqrt(jnp.mean(y_t * y_t, axis=-1, keepdims=True) + NORM_EPS)
        yh = y_t * ry
        dvec_ref[2:3, :] += jnp.sum(dxo_t * yh, axis=0, keepdims=True)
        dyh = dxo_t * (gate * g_post)
        dy_b = (ry * (dyh - yh * jnp.mean(dyh * yh, axis=-1, keepdims=True))).astype(BF16)
        dy_ref[...] = dy_b
        dycat = _dot_nt(dy_b, wout_ref[...])
        dy_a, dy_p = dycat[:, :CONV_W], dycat[:, CONV_W:]

        fa_t = fa_ref[...].astype(F32)
        db_a = dy_a * fa_t[:, :CONV_W]
        dconv = dy_a * fa_t[:, CONV_W:2 * CONV_W]
        dg_a = dy_a * fa_t[:, 2 * CONV_W:]
        uc_t = uc_ref[...].astype(F32)
        u_a, c_a = uc_t[:, :CONV_W], uc_t[:, CONV_W:]
        halo = jnp.where(tile > 0, uch_ref[...].astype(F32), 0.0)[POOL_HALO - CONV_HALO:]
        z = c_a * u_a
        zcat = jnp.concatenate([halo[:, CONV_W:] * halo[:, :CONV_W], z], axis=0)
        z1 = _rows_from_before(zcat, 1)[CONV_HALO:]
        z2 = _rows_from_before(zcat, 2)[CONV_HALO:]
        dccat = jnp.concatenate([dconv, dcc_ref[...]], axis=0)
        dc1 = _rows_from_after(dccat, 1)[:BWD_TILE]
        dc2 = _rows_from_after(dccat, 2)[:BWD_TILE]
        dz = w2 * dconv + w1 * dc1 + w0 * dc2
        dcc_ref[...] = dconv[:CONV_HALO]
        dcps_ref[0:1, :] += jnp.sum(dconv * z2, axis=0, keepdims=True)
        dcps_ref[1:2, :] += jnp.sum(dconv * z1, axis=0, keepdims=True)
        dcps_ref[2:3, :] += jnp.sum(dconv * z, axis=0, keepdims=True)
        du_a = dz * c_a
        dc_a = dz * u_a

        dmixed = (dy_p * fp_ref[:, :POOL_W].astype(F32)).astype(BF16)
        dg_p = dy_p * fp_ref[:, POOL_W:2 * POOL_W].astype(F32)
        dcps_ref[3:4, :] += jnp.sum(dy_p * fp_ref[:, 2 * POOL_W:3 * POOL_W].astype(F32), axis=0, keepdims=True)
        counts = _window_counts(tile * BWD_TILE, BWD_TILE)
        du_p, q_head = [], []
        for g, w in enumerate(POOL_WINDOWS):
            cols = slice(g * GROUP_D, (g + 1) * GROUP_D)
            dm_g = dmixed[:, cols]
            dpooled_g = _dot_nt(dm_g, wpool_ref[g])
            gwpool_acc[g] += _dot_tn(fp_ref[:, 3 * POOL_W + g * GROUP_D:3 * POOL_W + (g + 1) * GROUP_D], dm_g)
            q_g = dpooled_g * (1.0 / jnp.minimum(counts, float(w)))
            q_head.append(q_g[:POOL_HALO])
            s = jnp.concatenate([q_g, qc_ref[:, cols]], axis=0)
            step = 1
            while step < w:
                s = s + _rows_from_after(s, step)
                step *= 2
            du_p.append(s[:BWD_TILE] - dpooled_g)
        qc_ref[...] = jnp.concatenate(q_head, axis=1)
        dproj_b = jnp.concatenate([du_a, db_a, dc_a, dg_a] + du_p + [dg_p], axis=1).astype(BF16)
        dproj_ref[...] = dproj_b

        x_t = x_ref[...]
        rx = lax.rsqrt(jnp.mean(x_t * x_t, axis=-1, keepdims=True) + NORM_EPS)
        xn = x_t * rx
        mod_scale = 1.0 + scale
        dh = _dot_nt(dproj_b, win_ref[...])
        dvec_ref[0:1, :] += jnp.sum(dh, axis=0, keepdims=True)
        dvec_ref[1:2, :] += jnp.sum(dh * xn, axis=0, keepdims=True)
        dxn = dh * (g_pre * mod_scale)
        dx_ref[...] = dxo_t + rx * (dxn - xn * jnp.mean(dxn * xn, axis=-1, keepdims=True))

        @pl.when(i == n_tiles - 1)
        def _():
            gwpool_ref[...] = gwpool_acc[...].astype(BF16)
            sum_dh_xn, sum_dxo_yh = dvec_ref[1:2, :], dvec_ref[2:3, :]
            dvec_ref[1:2, :] = sum_dh_xn * g_pre
            dvec_ref[3:4, :] = sum_dh_xn * mod_scale
            dvec_ref[2:3, :] = sum_dxo_yh * g_post
            dvec_ref[4:5, :] = sum_dxo_yh * gate

    return pl.pallas_call(
        body, name=f"backward_layer_{layer}", grid=(n_tiles,),
        in_specs=[rev(D_MODEL), rev(D_MODEL), rev(D_MODEL), rev(2 * CONV_W), halo_spec, rev(3 * CONV_W),
                  rev(4 * POOL_W), _layer_spec(vec.shape, layer), _layer_spec(cps.shape, layer),
                  _layer_spec(wpool.shape, layer), _whole_spec(wout.shape), _whole_spec(win.shape)]
        + [HBM] * len(after),
        out_specs=[rev(D_MODEL), rev(IN_COLS), rev(D_MODEL), _whole_spec(gwpool_shape),
                   _whole_spec((SUBLANES, CONV_W)), _whole_spec((SUBLANES, D_MODEL))],
        out_shape=[jax.ShapeDtypeStruct((t_len, D_MODEL), F32), jax.ShapeDtypeStruct((t_len, IN_COLS), BF16),
                   jax.ShapeDtypeStruct((t_len, D_MODEL), BF16), jax.ShapeDtypeStruct(gwpool_shape, BF16),
                   jax.ShapeDtypeStruct((SUBLANES, CONV_W), F32), jax.ShapeDtypeStruct((SUBLANES, D_MODEL), F32)],
        scratch_shapes=[pltpu.VMEM(gwpool_shape, F32), pltpu.VMEM((CONV_HALO, CONV_W), F32),
                        pltpu.VMEM((POOL_HALO, POOL_W), F32)],
        compiler_params=_params(dimension_semantics=("arbitrary",)),
    )(dxo, y, x, uc, uc, fa, fp, vec, cps, wpool, wout, win, *after)


def _weight_grads(layer, h, dproj, ycat, dy, exchange, after=()):
    t_len = dy.shape[0]
    n_in, n_out = IN_COLS // GWIN_COLS, D_MODEL // GWOUT_COLS
    args = [h, dproj, ycat, dy, *after]
    in_specs = [_whole_spec(h.shape),
                pl.BlockSpec((t_len, GWIN_COLS), lambda s: (0, jnp.minimum(s, n_in - 1))),
                _whole_spec(ycat.shape),
                pl.BlockSpec((t_len, GWOUT_COLS), lambda s: (0, jnp.maximum(s - n_in, 0)))] + [HBM] * len(after)
    out_shape = [jax.ShapeDtypeStruct((D_MODEL, IN_COLS), BF16), jax.ShapeDtypeStruct((D_MODEL, D_MODEL), BF16)]
    out_specs = [pl.BlockSpec((D_MODEL, GWIN_COLS), lambda s: (0, jnp.minimum(s, n_in - 1))),
                 pl.BlockSpec((D_MODEL, GWOUT_COLS), lambda s: (0, jnp.maximum(s - n_in, 0)))]
    scratch = []
    aliases, split = _host(exchange, args, in_specs, out_shape, out_specs, scratch)

    def body(*refs):
        (h_ref, dproj_ref, ycat_ref, dy_ref, *_, gwin_ref, gwout_ref), hosted = split(refs)
        s = pl.program_id(0)
        if hosted is not None:
            pl.when(s == 0)(hosted[0])

        @pl.when(s < n_in)
        def _():
            gwin_ref[...] = _dot_tn(h_ref[...], dproj_ref[...]).astype(BF16)

        @pl.when(s >= n_in)
        def _():
            gwout_ref[...] = _dot_tn(ycat_ref[...], dy_ref[...]).astype(BF16)

        if hosted is not None:
            pl.when(s == n_in + n_out - 1)(hosted[1])

    return pl.pallas_call(
        body, name=f"weight_grads_{layer}", grid=(n_in + n_out,), in_specs=in_specs, out_specs=out_specs,
        out_shape=out_shape, scratch_shapes=scratch, input_output_aliases=aliases,
        compiler_params=_params(dimension_semantics=("arbitrary",)),
    )(*args)


def _add_sibling_blocks(name, layer, grads, received, core, partials, kinds):
    n_arr = len(grads)

    def body(core_ref, *refs):
        mine, theirs, outs = refs[:n_arr], refs[n_arr:2 * n_arr], refs[-n_arr:]
        for a in range(n_arr):
            outs[a][...] = (mine[a][...].astype(F32) + theirs[a][...].astype(F32)).astype(BF16)

    own_of_kind = [
        pl.BlockSpec((D_MODEL, W_IN_SHARD), lambda q, core_ref: (0, 2 * q + core_ref[0])),
        pl.BlockSpec((W_OUT_SHARD, D_MODEL), lambda q, core_ref: (2 * q + core_ref[0], 0)),
        pl.BlockSpec((POOL_SHARD, GROUP_D), lambda q, core_ref: (2 * q + core_ref[0], 0)),
    ]
    shapes = [_BLOCK_SHAPES[k] for k in kinds]
    recv_specs = [pl.BlockSpec((None,) + s, lambda q, core_ref: (q, 0, 0)) for s in shapes]
    out_specs = [pl.BlockSpec((None, None) + s, lambda q, core_ref: (q, layer, 0, 0)) for s in shapes]
    args = [core, *grads, *received]
    in_specs = [own_of_kind[k] for k in kinds] + recv_specs
    aliases = {}
    if partials is not None:
        aliases = {len(args) + a: a for a in range(n_arr)}
        args += list(partials)
        in_specs += [HBM] * n_arr
    return pl.pallas_call(
        body, name=name,
        grid_spec=pltpu.PrefetchScalarGridSpec(
            num_scalar_prefetch=1, grid=(N_CHIP,), in_specs=in_specs, out_specs=out_specs),
        out_shape=[jax.ShapeDtypeStruct((N_CHIP, DEPTH) + s, BF16) for s in shapes],
        input_output_aliases=aliases,
        compiler_params=_params(dimension_semantics=("arbitrary",)),
    )(*args)


def _modulation_columns(c_all, w_ada):
    def body(c_ref, w_ref, cact_ref, out_ref):
        c_t = c_ref[...]
        c_act = c_t * _sigmoid(c_t)
        cact_ref[...] = c_act
        out_ref[...] = jnp.dot(c_act, w_ref[...], preferred_element_type=F32, precision=lax.Precision.HIGHEST)

    return pl.pallas_call(
        body, name="modulation_columns", grid=(DEPTH,),
        in_specs=[pl.BlockSpec((N_DEV, D_MODEL), lambda l: (0, 0)),
                  pl.BlockSpec((None, D_MODEL, W_IN_SHARD), lambda l: (l, 0, 0))],
        out_specs=[pl.BlockSpec((N_DEV, D_MODEL), lambda l: (0, 0)),
                   pl.BlockSpec((N_DEV, W_IN_SHARD), lambda l: (0, l))],
        out_shape=[jax.ShapeDtypeStruct((N_DEV, D_MODEL), F32),
                   jax.ShapeDtypeStruct((N_DEV, DEPTH * W_IN_SHARD), F32)],
        compiler_params=_params(dimension_semantics=("arbitrary",)),
    )(c_all, w_ada)


def _adamw(w, g, m, v):
    m_new = ADAM_B1 * m + (1.0 - ADAM_B1) * g
    v_new = ADAM_B2 * v + (1.0 - ADAM_B2) * (g * g)
    m_hat = m_new / (1.0 - ADAM_B1 ** ADAM_STEP)
    v_hat = v_new / (1.0 - ADAM_B2 ** ADAM_STEP)
    delta = -ADAM_LR * (m_hat / (jnp.sqrt(v_hat) + ADAM_EPS) + ADAM_WD * w)
    return delta, m_new, v_new


def _adamw_w_ada(w, m, v, c_act_t, dmod_cols):
    def body(w_ref, m_ref, v_ref, ct_ref, dm_ref, g_ref, d_ref, mo_ref, vo_ref):
        g = ct_ref[:, 0:1] * dm_ref[0:1, :]
        for b in range(1, N_DEV):
            g = g + ct_ref[:, b:b + 1] * dm_ref[b:b + 1, :]
        g_ref[...] = g
        d_ref[...], mo_ref[...], vo_ref[...] = _adamw(w_ref[...], g, m_ref[...], v_ref[...])

    big = pl.BlockSpec((None, D_MODEL, W_IN_SHARD), lambda l: (l, 0, 0))
    return pl.pallas_call(
        body, name="adamw_w_ada", grid=(DEPTH,),
        in_specs=[big, big, big, pl.BlockSpec((D_MODEL, N_DEV), lambda l: (0, 0)),
                  pl.BlockSpec((None, N_DEV, W_IN_SHARD), lambda l: (l, 0, 0))],
        out_specs=[big] * 4, out_shape=[jax.ShapeDtypeStruct(w.shape, F32)] * 4,
        compiler_params=_params(dimension_semantics=("arbitrary",)),
    )(w, m, v, c_act_t, dmod_cols)


def _sum_chip_partials(own_ref, recv_ref):
    g = own_ref[...].astype(F32)
    for j in range(N_OTHER_CHIPS):
        g = g + recv_ref[j].astype(F32)
    return g


def _partial_specs(row_tile, cols, first_layer=0):
    own = pl.BlockSpec((None, None, row_tile, cols), lambda l, r, chip_ref: (chip_ref[0], first_layer + l, r, 0))
    recv = pl.BlockSpec((N_OTHER_CHIPS, None, row_tile, cols), lambda l, r, chip_ref: (0, first_layer + l, r, 0))
    return own, recv


def _adamw_reduced(name, w, m, v, partial, received, chip, row_tile, layers, continued, after=()):
    depth, rows, cols = w.shape
    first, stop = layers

    def body(chip_ref, w_ref, m_ref, v_ref, own_ref, recv_ref, *rest):
        g_ref, d_ref, mo_ref, vo_ref = rest[-4:]
        g = _sum_chip_partials(own_ref, recv_ref)
        g_ref[...] = g
        d_ref[...], mo_ref[...], vo_ref[...] = _adamw(w_ref[...], g, m_ref[...], v_ref[...])

    blk = pl.BlockSpec((None, row_tile, cols), lambda l, r, chip_ref: (first + l, r, 0))
    args = [chip, w, m, v, partial, received]
    in_specs = [blk, blk, blk, *_partial_specs(row_tile, cols, first)]
    aliases = {}
    if continued is not None:
        aliases = {len(args) + k: k for k in range(4)}
        args += list(continued)
        in_specs += [HBM] * 4
    args += after
    in_specs += [HBM] * len(after)
    return pl.pallas_call(
        body, name=name,
        grid_spec=pltpu.PrefetchScalarGridSpec(
            num_scalar_prefetch=1, grid=(stop - first, rows // row_tile), in_specs=in_specs, out_specs=[blk] * 4),
        out_shape=[jax.ShapeDtypeStruct(w.shape, F32)] * 4, input_output_aliases=aliases,
        compiler_params=_params(dimension_semantics=("arbitrary", "arbitrary")),
    )(*args)


def _reduce_w_pool(partial, received, chip):
    def body(chip_ref, own_ref, recv_ref, g_ref):
        g_ref[...] = _sum_chip_partials(own_ref, recv_ref)

    return pl.pallas_call(
        body, name="reduce_w_pool",
        grid_spec=pltpu.PrefetchScalarGridSpec(
            num_scalar_prefetch=1, grid=(DEPTH, 1), in_specs=list(_partial_specs(POOL_SHARD, GROUP_D)),
            out_specs=pl.BlockSpec((None, POOL_SHARD, GROUP_D), lambda l, r, chip_ref: (l, 0, 0))),
        out_shape=jax.ShapeDtypeStruct((DEPTH, POOL_SHARD, GROUP_D), F32),
        compiler_params=_params(dimension_semantics=("arbitrary", "arbitrary")),
    )(chip, partial, received)


def _adamw_small(name, params):
    n = len(params)

    def body(*refs):
        ins, outs = refs[:4 * n], refs[4 * n:]
        for p in range(n):
            w_ref, g_ref, m_ref, v_ref = ins[4 * p:4 * p + 4]
            d_ref, mo_ref, vo_ref, go_ref = outs[4 * p:4 * p + 4]
            d_ref[...], mo_ref[...], vo_ref[...] = _adamw(w_ref[...], g_ref[...], m_ref[...], v_ref[...])
            go_ref[...] = g_ref[...]

    vmem = pl.BlockSpec(memory_space=pltpu.VMEM)
    flat = [a for group in params for a in group]
    outs = pl.pallas_call(
        body, name=name, in_specs=[vmem] * len(flat), out_specs=[vmem] * len(flat),
        out_shape=[jax.ShapeDtypeStruct(a.shape, F32) for a in flat], compiler_params=_params(),
    )(*flat)
    return [tuple(outs[4 * p:4 * p + 4]) for p in range(n)]


def _sum_sources(slabs, columns, after):
    def body(s_ref, *rest):
        outs = rest[len(after):]
        acc = s_ref[0]
        for b in range(1, N_DEV):
            acc = acc + s_ref[b]
        for (start, stop), o_ref in zip(columns, outs):
            o_ref[...] = acc[:, start:stop]
        outs[-1][...] = acc[0:1, SLAB_COLS:SLAB_COLS + 1]

    vmem = pl.BlockSpec(memory_space=pltpu.VMEM)
    out_shape = [jax.ShapeDtypeStruct((slabs.shape[1], stop - start), F32) for start, stop in columns]
    out_shape.append(jax.ShapeDtypeStruct((1, 1), F32))
    return pl.pallas_call(
        body, name="sum_small_grads", in_specs=[vmem] + [HBM] * len(after), out_specs=[vmem] * len(out_shape),
        out_shape=out_shape, compiler_params=_params(),
    )(slabs, *after)


def _to_bf16(a, name, layers=None):
    first, stop = layers or (0, a.shape[0])

    def body(a_ref, o_ref):
        o_ref[...] = a_ref[...].astype(BF16)

    block = (None,) + a.shape[1:]
    return pl.pallas_call(
        body, name=name, grid=(stop - first,), in_specs=[pl.BlockSpec(block, lambda l: (first + l, 0, 0))],
        out_specs=pl.BlockSpec(block, lambda l: (l, 0, 0)),
        out_shape=jax.ShapeDtypeStruct((stop - first,) + a.shape[1:], BF16),
        compiler_params=_params(dimension_semantics=("arbitrary",)),
    )(a)


def kernel(x, c, w_ada, b_ada, g_pre, w_in, w_conv, w_pool, pool_scale, w_out, g_post, loss_target, m_w_ada, m_b_ada, m_g_pre, m_w_in, m_w_conv, m_w_pool, m_pool_scale, m_w_out, m_g_post, v_w_ada, v_b_ada, v_g_pre, v_w_in, v_w_conv, v_w_pool, v_pool_scale, v_w_out, v_g_post):
    mx, my, mc = _mesh_position()
    me = _block_id(mx, my, mc)
    chip = (2 * mx + my).astype(jnp.int32).reshape(1)
    core = mc.astype(jnp.int32).reshape(1)
    x0 = x[0]
    target = loss_target[0]
    conv_shard = w_conv.shape[-1]

    own_small = jnp.concatenate([c, w_conv.reshape(1, DEPTH * 3 * conv_shard)], axis=1)
    first_shards = [_to_bf16(w_in, "cast_w_in_0", (0, 1)), _to_bf16(w_out, "cast_w_out_0", (0, 1))]
    all_small = _all_gather_small(own_small, "all_gather_c_w_conv", first_shards)[:, 0, :]
    gathers = [_gather_weights_start(0, *first_shards, [all_small], relayed=True)]
    c_all = all_small[:, :D_MODEL]
    w_conv_full = all_small[:, D_MODEL:].reshape(N_DEV, DEPTH, 3, conv_shard).transpose(1, 2, 0, 3).reshape(
        DEPTH, 3, CONV_W)
    cps = jnp.concatenate([w_conv_full, pool_scale[:, None], jnp.zeros((DEPTH, 4, CONV_W), F32)], axis=1)

    c_act, pieces = _modulation_columns(c_all, w_ada)
    upper = (1, DEPTH)
    upper_shards = [_to_bf16(w_in, "cast_w_in_1", upper), _to_bf16(w_out, "cast_w_out_1", upper)]
    wpool_b = _to_bf16(w_pool.reshape(DEPTH, POOL_ROWS, GROUP_D), "cast_w_pool").reshape(w_pool.shape)
    first_sems_0, _, (zones_0,) = gathers[0]
    neighbour_sems, zones_0 = _gather_weights_pass_on(
        "all_gather_weights_relay_0", NEIGHBOUR_CHIPS, first_sems_0[1], partial(_first_arrival, 0), zones_0,
        [pieces, *upper_shards, wpool_b, cps], relay=True)
    gather_mod = _all_gather_exchange(pieces)
    sems_m, pieces_thru, mod_landing, token_m = _start_exchange(gather_mod, "all_gather_modulation_start", [zones_0[0]])
    diagonal_sems, zones_0 = _gather_weights_pass_on(
        "all_gather_weights_pass_on_0", DIAGONAL_CHIP, neighbour_sems[3], lambda a, j: a, zones_0, [token_m])
    _, (mod_all,) = _finish_exchange(
        gather_mod, "all_gather_modulation_finish", sems_m, pieces_thru, mod_landing, [zones_0[0]])
    mod_mine = lax.dynamic_index_in_dim(mod_all, me, axis=1, keepdims=False)
    mod = mod_mine.reshape(N_DEV, DEPTH, W_IN_SHARD).transpose(1, 0, 2).reshape(DEPTH, 3 * D_MODEL) + b_ada
    zeros_d = jnp.zeros((DEPTH, 3, D_MODEL), F32)
    vec = jnp.concatenate([mod.reshape(DEPTH, 3, D_MODEL), g_pre[:, None], g_post[:, None], zeros_d], axis=1)

    gathers.append(_gather_weights_start(1, *upper_shards, [mod_all]))
    gathers = [(first_sems, shards, landing[k], k) for first_sems, shards, landing in gathers
               for k in range(len(landing))]

    xs, kept, wins, wouts = [x0], [], [], []
    for l in range(DEPTH):
        first_sems, shards, zones, index = gathers[l]
        if l == 0:
            passed = [(neighbour_sems[:2], NEIGHBOUR_CHIPS), (diagonal_sems, DIAGONAL_CHIP)]
            win, wout = _gather_weights_finish(
                l, index, first_sems, passed, shards, zones_0, neighbour_sems[2], [gathers[-1][1][0]])
        else:
            passed_sems, zones = _gather_weights_pass_on(
                f"all_gather_weights_pass_on_{l}", ALL_OTHER_CHIPS, first_sems[1], partial(_first_arrival, index),
                zones, [xs[-1]])
            win, wout = _gather_weights_finish(l, index, first_sems, [(passed_sems, ALL_OTHER_CHIPS)], shards, zones)
        x_next, *for_backward = _forward_layer(
            l, xs[-1], vec, cps, wpool_b, win, wout, target if l == DEPTH - 1 else None)
        xs.append(x_next)
        kept.append(for_backward[:6])
        wins.append(win)
        wouts.append(wout)
    dx, loss_tile = xs[DEPTH], for_backward[6]

    slab_rows = [None] * DEPTH
    partials = received = None
    in_flight = []

    def scatter(layer, grads, from_sibling, after):
        nonlocal partials, received
        partials = _add_sibling_blocks(
            f"grad_add_sibling_{layer}", layer, grads, from_sibling, core, partials, ALL_KINDS)
        chips = _chips_exchange(layer, partials, received, ALL_KINDS)
        sems, partials, received, token = _start_exchange(chips, f"grad_chips_start_{layer}", after)
        in_flight.append((chips, sems, layer))
        return token

    grads_above = None
    issued = []
    for l in reversed(range(DEPTH)):
        y, h, ycat, uc, fa, fp = kept[l]
        dx, dproj, dy, gwpool, dcps, dvec = _backward_layer(
            l, dx, y, xs[l], uc, fa, fp, vec, cps, wpool_b, wouts[l], wins[l], issued)
        slab_rows[l] = jnp.concatenate(
            [dvec[0], dvec[1], dvec[2], dvec[3], dvec[4], dcps[3], dcps[0], dcps[1], dcps[2],
             loss_tile[0] if l == 0 else jnp.zeros((LANES,), F32)])
        after = []
        if l == 0:
            gather_small = _all_gather_exchange(jnp.stack(slab_rows))
            sems_s, slab, slabs, token = _start_exchange(gather_small, "all_gather_small_grads_start")
            after = [token]
        hosted = _sibling_exchange(grads_above, ALL_KINDS) if grads_above is not None else None
        gwin, gwout, *from_sibling = _weight_grads(l, h, dproj, ycat, dy, hosted, after)
        if l == 0:
            _, (slabs,) = _finish_exchange(
                gather_small, "all_gather_small_grads_finish", sems_s, slab, slabs, [gwin])
        issued = [gwin]
        if grads_above is not None:
            issued = [scatter(l + 1, grads_above, from_sibling, [])]
        grads_above = [gwin, gwout, gwpool.reshape(POOL_ROWS, GROUP_D)]
        if l <= 1:
            from_sibling = _run_exchange(_sibling_exchange(grads_above, ALL_KINDS), f"grad_exchange_sibling_{l}")
            token = scatter(l, grads_above, from_sibling, [slabs] if l == 0 else [])
            issued = [token]
            grads_above = None
    grad_x = dx[None]
    chips_0, sems_0, _ = in_flight.pop()

    o = 3 * D_MODEL

    after = [token]
    for chips, sems, l in in_flight:
        partials, received = _finish_exchange(chips, f"grad_chips_finish_{l}", sems, partials, received, after)
        after = []
    upper = (1, DEPTH)
    w_in_upper = _adamw_reduced(
        "adamw_w_in_upper", w_in, m_w_in, v_w_in, partials[0], received[0], chip, ROW_TILE, upper, None)
    w_out_upper = _adamw_reduced(
        "adamw_w_out_upper", w_out, m_w_out, v_w_out, partials[1], received[1], chip, W_OUT_SHARD, upper, None)
    dmod_all = slabs[:, :, :o].reshape(N_DEV, DEPTH, N_DEV, W_IN_SHARD)
    dmod_cols = lax.dynamic_index_in_dim(dmod_all, me, axis=2, keepdims=False).transpose(1, 0, 2) + token[0, 0]
    g_w_ada, d_w_ada, nm_w_ada, nv_w_ada = _adamw_w_ada(w_ada, m_w_ada, v_w_ada, c_act.T, dmod_cols)

    partials, received = _finish_exchange(
        chips_0, "grad_chips_finish_0", sems_0, partials, received, [nv_w_ada, w_in_upper[3], w_out_upper[3]])
    gather_pool = _all_gather_exchange(_reduce_w_pool(partials[2], received[2], chip), axis=1)
    sems_p, pool_rows, pool_landing, token_p = _start_exchange(gather_pool, "all_gather_grad_w_pool_start")
    g_w_in, d_w_in, nm_w_in, nv_w_in = _adamw_reduced(
        "adamw_w_in_0", w_in, m_w_in, v_w_in, partials[0], received[0], chip, ROW_TILE, (0, 1), w_in_upper, [token_p])
    g_w_out, d_w_out, nm_w_out, nv_w_out = _adamw_reduced(
        "adamw_w_out_0", w_out, m_w_out, v_w_out, partials[1], received[1], chip, W_OUT_SHARD, (0, 1), w_out_upper,
        [token_p])
    ends = [0, o, o + D_MODEL, o + 2 * D_MODEL, o + 2 * D_MODEL + POOL_W, SLAB_COLS]
    g_b_ada, g_g_pre, g_g_post, g_pool_scale, g_conv, loss = _sum_sources(
        slabs, list(zip(ends[:-1], ends[1:])), [token_p, nv_w_in, nv_w_out])
    loss = loss.reshape(())
    g_w_conv = lax.dynamic_slice_in_dim(g_conv.reshape(DEPTH, 3, CONV_W), me * conv_shard, conv_shard, axis=2)
    taps_first = lambda a: a.transpose(1, 0, 2)
    small = _adamw_small("adamw_small", [
        (b_ada, g_b_ada, m_b_ada, v_b_ada),
        (g_pre, g_g_pre, m_g_pre, v_g_pre),
        tuple(taps_first(a) for a in (w_conv, g_w_conv, m_w_conv, v_w_conv)),
        (pool_scale, g_pool_scale, m_pool_scale, v_pool_scale),
        (g_post, g_g_post, m_g_post, v_g_post),
    ])
    (d_b_ada, nm_b_ada, nv_b_ada, g_b_ada), (d_g_pre, nm_g_pre, nv_g_pre, g_g_pre), conv_steps, \
        (d_ps, nm_ps, nv_ps, g_pool_scale), (d_g_post, nm_g_post, nv_g_post, g_g_post) = small
    d_w_conv, nm_w_conv, nv_w_conv, g_w_conv = (taps_first(a) for a in conv_steps)
    _, (g_pool_all,) = _finish_exchange(
        gather_pool, "all_gather_grad_w_pool_finish", sems_p, pool_rows, pool_landing, [nv_w_in, nv_w_out, nv_g_post])
    ((d_w_pool, nm_w_pool, nv_w_pool, g_w_pool),) = _adamw_small(
        "adamw_w_pool", [(w_pool, g_pool_all.reshape(w_pool.shape), m_w_pool, v_w_pool)])

    return (loss, grad_x,
            g_w_ada, g_b_ada, g_g_pre, g_w_in, g_w_conv, g_w_pool, g_pool_scale, g_w_out, g_g_post,
            d_w_ada, d_b_ada, d_g_pre, d_w_in, d_w_conv, d_w_pool, d_ps, d_w_out, d_g_post,
            nm_w_ada, nm_b_ada, nm_g_pre, nm_w_in, nm_w_conv, nm_w_pool, nm_ps, nm_w_out, nm_g_post,
            nv_w_ada, nv_b_ada, nv_g_pre, nv_w_in, nv_w_conv, nv_w_pool, nv_ps, nv_w_out, nv_g_post)
```

```python
from functools import partial

import jax
import jax.numpy as jnp
from jax import lax
from jax.experimental import pallas as pl
from jax.experimental.pallas import tpu as pltpu

F32 = jnp.float32
BF16 = jnp.bfloat16

D_MODEL = 1024
DEPTH = 4
CONV_W = 512
POOL_W = 512
POOL_WINDOWS = (2, 4, 8, 16)
GROUP_D = 128
IN_COLS = 4 * CONV_W + 2 * POOL_W
NORM_EPS = 1e-6

ADAM_LR = 0.001
ADAM_B1 = 0.9
ADAM_B2 = 0.999
ADAM_EPS = 1e-08
ADAM_WD = 0.01
ADAM_STEP = 10

N_DEV = 8
N_CHIP = 4
N_OTHER_CHIPS = N_CHIP - 1
MESH = pl.DeviceIdType.MESH
W_IN_SHARD = IN_COLS // N_DEV
W_OUT_SHARD = D_MODEL // N_DEV
POOL_ROWS = len(POOL_WINDOWS) * GROUP_D
POOL_SHARD = POOL_ROWS // N_DEV

SUBLANES = 8
LANES = 128
VMEM_LIMIT_BYTES = 56 * 1024 * 1024
ROW_TILE = 512
BWD_TILE = 256
GWIN_COLS = 768
GWOUT_COLS = 512
POOL_HALO = 16
CONV_HALO = SUBLANES

SLAB_COLS = 3 * D_MODEL + D_MODEL + D_MODEL + POOL_W + 3 * CONV_W

HBM = pl.BlockSpec(memory_space=pl.ANY)


def _params(**kw):
    return pltpu.CompilerParams(vmem_limit_bytes=VMEM_LIMIT_BYTES, **kw)


def _sigmoid(v):
    return 1.0 / (1.0 + jnp.exp(-v))


def _dot(a, b):
    return jnp.dot(a, b, preferred_element_type=F32)


def _dot_tn(a, b):
    return lax.dot_general(a, b, (((0,), (0,)), ((), ())), preferred_element_type=F32)


def _dot_nt(a, b):
    return lax.dot_general(a, b, (((1,), (1,)), ((), ())), preferred_element_type=F32)


def _rows_from_before(v, k):
    return pltpu.roll(v, k, 0)


def _rows_from_after(v, k):
    return pltpu.roll(v, v.shape[0] - k, 0)


def _window_counts(t0, rows):
    return (lax.broadcasted_iota(jnp.int32, (rows, 1), 0) + (t0 + 1)).astype(F32)


def _split_proj(p32):
    cw = CONV_W
    return (p32[:, 0 * cw:1 * cw], p32[:, 1 * cw:2 * cw], p32[:, 2 * cw:3 * cw], p32[:, 3 * cw:4 * cw],
            p32[:, 4 * cw:4 * cw + POOL_W], p32[:, 4 * cw + POOL_W:])


def _layer_spec(shape, layer):
    nd = len(shape)
    return pl.BlockSpec((None,) + tuple(shape[1:]), lambda i, _l=layer, _n=nd: (_l,) + (0,) * (_n - 1))


def _whole_spec(shape):
    return pl.BlockSpec(tuple(shape), lambda i, _n=len(shape): (0,) * _n, pipeline_mode=pl.Buffered(1))


def _mesh_position():
    return lax.axis_index("x"), lax.axis_index("y"), lax.axis_index("c")


def _block_id(x, y, c):
    return 4 * x + 2 * y + c


def _other_chips(x, y):
    return [(x ^ 1, y), (x, y ^ 1), (x ^ 1, y ^ 1)]


def _col_block(ref, blk):
    return ref.at[:, pl.ds(pl.multiple_of(blk * W_IN_SHARD, LANES), W_IN_SHARD)]


def _row_block(rows):
    def block(ref, blk):
        return ref.at[pl.ds(pl.multiple_of(blk * rows, rows), rows), :]
    return block


_BLOCK_OF = (_col_block, _row_block(W_OUT_SHARD), _row_block(POOL_SHARD))
_BLOCK_SHAPES = ((D_MODEL, W_IN_SHARD), (W_OUT_SHARD, D_MODEL), (POOL_SHARD, GROUP_D))


class _Exchange:
    def __init__(self, inputs, out_shapes, aliases, sem_shapes, make):
        self.inputs, self.out_shapes, self.aliases, self.sem_shapes, self.make = (
            list(inputs), list(out_shapes), dict(aliases), list(sem_shapes), make)


def _run_exchange(exchange, name):
    n_in, n_out = len(exchange.inputs), len(exchange.out_shapes)

    def body(*refs):
        start, finish = exchange.make(refs[:n_in], refs[n_in:n_in + n_out], refs[n_in + n_out:])
        start()
        finish()

    return pl.pallas_call(
        body, name=name, in_specs=[HBM] * n_in, out_specs=[HBM] * n_out, out_shape=exchange.out_shapes,
        scratch_shapes=exchange.sem_shapes, input_output_aliases=exchange.aliases, compiler_params=_params(),
    )(*exchange.inputs)


_SEM = pl.BlockSpec(memory_space=pltpu.SEMAPHORE)
_DATAFLOW = pltpu.SideEffectType.DATAFLOW_SIDE_EFFECTING


def _start_exchange(exchange, name, after=()):
    n_in, n_out, n_sem = len(exchange.inputs), len(exchange.out_shapes), len(exchange.sem_shapes)
    sources = [i for i in range(n_in) if i not in exchange.aliases]
    aliases = {i: n_sem + k for k, i in enumerate(sources)}
    aliases.update({i: n_sem + len(sources) + o for i, o in exchange.aliases.items()})

    def body(*refs):
        in_refs = refs[:n_in]
        outs = refs[n_in + len(after):]
        sems = outs[:n_sem]
        out_refs = outs[n_sem + len(sources):n_sem + len(sources) + n_out]
        exchange.make(in_refs, out_refs, sems)[0]()
        refs[-1][...] = jnp.zeros_like(refs[-1])

    outs = pl.pallas_call(
        body, name=name, in_specs=[HBM] * (n_in + len(after)),
        out_specs=[_SEM] * n_sem + [HBM] * (len(sources) + n_out) + [pl.BlockSpec(memory_space=pltpu.VMEM)],
        out_shape=(exchange.sem_shapes + [pltpu.HBM(exchange.inputs[i].shape, exchange.inputs[i].dtype) for i in sources]
                   + [pltpu.HBM(s.shape, s.dtype) for s in exchange.out_shapes]
                   + [jax.ShapeDtypeStruct((SUBLANES, LANES), F32)]),
        input_output_aliases=aliases, compiler_params=_params(has_side_effects=_DATAFLOW),
    )(*exchange.inputs, *after)
    return outs[:n_sem], outs[n_sem:n_sem + len(sources)], outs[n_sem + len(sources):-1], outs[-1]


def _finish_exchange(exchange, name, sems, sources, landing, after):
    n_src, n_out, n_sem = len(sources), len(landing), len(sems)
    n_in = len(exchange.inputs)
    source_at = [i for i in range(n_in) if i not in exchange.aliases]

    def body(*refs):
        src_refs, out_refs = refs[:n_src], refs[n_src:n_src + n_out]
        sem_refs = refs[n_src + n_out:n_src + n_out + n_sem]
        in_refs = [None] * n_in
        for k, i in enumerate(source_at):
            in_refs[i] = src_refs[k]
        for i, o in exchange.aliases.items():
            in_refs[i] = out_refs[o]
        exchange.make(in_refs, out_refs, sem_refs)[1]()

    arrays = list(sources) + list(landing)
    outs = pl.pallas_call(
        body, name=name, in_specs=[HBM] * len(arrays) + [_SEM] * n_sem + [HBM] * len(after),
        out_specs=[HBM] * len(arrays), out_shape=[pltpu.HBM(a.shape, a.dtype) for a in arrays],
        input_output_aliases={i: i for i in range(len(arrays))}, compiler_params=_params(has_side_effects=_DATAFLOW),
    )(*arrays, *sems, *after)
    return outs[:n_src], outs[n_src:]


N_GATHERED = 2
FIRST_COPIES = 1 + N_OTHER_CHIPS
_GATHERED_SHAPES = ((D_MODEL, IN_COLS), (D_MODEL, D_MODEL))
ALL_OTHER_CHIPS, NEIGHBOUR_CHIPS, DIAGONAL_CHIP = (0, 1, 2), (0, 1), (2,)


def _first_sem(layer, a, k):
    return (layer * N_GATHERED + a) * FIRST_COPIES + k


def _first_arrival(layer, a, j):
    return _first_sem(layer, a, 1 + j)


def _gather_copy(window_of, full_ref, blk, send_sem, recv_sem, to, src=None):
    window = window_of(full_ref, blk)
    return pltpu.make_async_remote_copy(
        src_ref=window if src is None else src, dst_ref=window, send_sem=send_sem, recv_sem=recv_sem,
        device_id=to, device_id_type=MESH)


def _first_copies(layer, shard_refs, full_refs, send_sems, recv_sems, local_sems, relayed):
    x, y, c = _mesh_position()
    me = _block_id(x, y, c)
    chips = [_other_chips(x, y)[j] for j in (NEIGHBOUR_CHIPS if relayed else ALL_OTHER_CHIPS)]
    own, remote = [], []
    for a in range(N_GATHERED):
        shard = shard_refs[a].at[layer]
        own.append(pltpu.make_async_copy(
            shard, _BLOCK_OF[a](full_refs[a], me), local_sems.at[layer * N_GATHERED + a]))
        targets = [(x, y, 1 - c)] + [(*chip, c) for chip in chips]
        remote += [_gather_copy(_BLOCK_OF[a], full_refs[a], me, send_sems.at[_first_sem(layer, a, k)],
                                recv_sems.at[_first_sem(layer, a, k)], to, src=shard)
                   for k, to in enumerate(targets)]
    return own, remote


def _gather_weights_start(first_layer, win_shards, wout_shards, after, relayed=False):
    n_layers = win_shards.shape[0]
    n_first = n_layers * N_GATHERED * FIRST_COPIES
    sem_shapes = [pltpu.SemaphoreType.DMA((n_first,)), pltpu.SemaphoreType.DMA((n_first,)),
                  pltpu.SemaphoreType.DMA((n_layers * N_GATHERED,))]
    shards = [win_shards, wout_shards]

    def body(win_sh, wout_sh, *rest):
        send_sems, recv_sems, local_sems, win_thru, wout_thru, *landing = rest[len(after):]
        for layer in range(n_layers):
            own, remote = _first_copies(layer, (win_sh, wout_sh), landing[N_GATHERED * layer:N_GATHERED * (layer + 1)],
                                        send_sems, recv_sems, local_sems, relayed)
            for cp in own + remote:
                cp.start()

    outs = pl.pallas_call(
        body, name=f"all_gather_weights_start_{first_layer}", in_specs=[HBM] * (2 + len(after)),
        out_specs=[_SEM] * 3 + [HBM] * (2 + n_layers * N_GATHERED),
        out_shape=(sem_shapes + [pltpu.HBM(s.shape, s.dtype) for s in shards]
                   + [pltpu.HBM(s, BF16) for _ in range(n_layers) for s in _GATHERED_SHAPES]),
        input_output_aliases={0: 3, 1: 4}, compiler_params=_params(has_side_effects=_DATAFLOW),
    )(*shards, *after)
    landing = outs[5:]
    return outs[:3], outs[3:5], [landing[N_GATHERED * l:N_GATHERED * (l + 1)] for l in range(n_layers)]


def _passed_on_copies(full_refs, send_sems, recv_sems, core_of_block, chips):
    x, y, c = _mesh_position()
    return [_gather_copy(_BLOCK_OF[a], full_refs[a], _block_id(*_other_chips(x, y)[j], core_of_block),
                         send_sems.at[a * N_OTHER_CHIPS + j], recv_sems.at[a * N_OTHER_CHIPS + j], (x, y, 1 - c))
            for a in range(N_GATHERED) for j in chips]


def _relayed_copies(full_refs, send_sems, recv_sems):
    x, y, c = _mesh_position()
    source, to = (x ^ (1 - c), y ^ c), (x ^ c, y ^ (1 - c))
    return [_gather_copy(_BLOCK_OF[a], full_refs[a], _block_id(*source, c), send_sems.at[a], recv_sems.at[a], (*to, c))
            for a in range(N_GATHERED)]


def _gather_weights_pass_on(name, chips, arrival_sems, arrival_sem_of, landing, after, relay=False):
    n = N_GATHERED * N_OTHER_CHIPS
    sem_shapes = [pltpu.SemaphoreType.DMA((n,))] * 2 + [pltpu.SemaphoreType.DMA((N_GATHERED,))] * (2 if relay else 0)

    def body(win_ref, wout_ref, arrivals, *rest):
        sems = rest[len(after):len(after) + len(sem_shapes)]
        x, y, c = _mesh_position()
        full_refs = (win_ref, wout_ref)
        passed = _passed_on_copies(full_refs, sems[0], sems[1], c, chips)
        relayed = _relayed_copies(full_refs, sems[2], sems[3]) if relay else []
        for a in range(N_GATHERED):
            for j in chips:
                sem = arrivals.at[arrival_sem_of(a, j)]
                _gather_copy(_BLOCK_OF[a], full_refs[a], _block_id(*_other_chips(x, y)[j], c), sem, sem,
                             (x, y, c)).wait_recv()
            for cp in relayed[a:a + 1] + passed[a * len(chips):(a + 1) * len(chips)]:
                cp.start()

    outs = pl.pallas_call(
        body, name=name, in_specs=[HBM] * N_GATHERED + [_SEM] + [HBM] * len(after),
        out_specs=[_SEM] * len(sem_shapes) + [HBM] * N_GATHERED,
        out_shape=sem_shapes + [pltpu.HBM(a.shape, a.dtype) for a in landing],
        input_output_aliases={a: len(sem_shapes) + a for a in range(N_GATHERED)},
        compiler_params=_params(has_side_effects=_DATAFLOW),
    )(*landing, arrival_sems, *after)
    return outs[:len(sem_shapes)], outs[len(sem_shapes):]


def _gather_weights_finish(layer, index, first_sems, passed, shards, landing, relay_send_sems=None, after=()):
    relayed = relay_send_sems is not None
    passed_sems = [sem for (send, recv), _ in passed for sem in (send, recv)] + ([relay_send_sems] if relayed else [])

    def body(win_ref, wout_ref, first_send, first_recv, local_sems, *rest):
        sems, (win_sh, wout_sh) = rest[:len(passed_sems)], rest[len(passed_sems):len(passed_sems) + 2]
        x, y, c = _mesh_position()
        full_refs = (win_ref, wout_ref)
        own, sent = _first_copies(index, (win_sh, wout_sh), full_refs, first_send, first_recv, local_sems, relayed)
        for a in range(N_GATHERED):
            sem = _first_sem(index, a, 0)
            _gather_copy(_BLOCK_OF[a], full_refs[a], _block_id(x, y, 1 - c), first_recv.at[sem], first_recv.at[sem],
                         (x, y, c)).wait_recv()
        for p, (_, chips) in enumerate(passed):
            for cp in _passed_on_copies(full_refs, sems[2 * p], sems[2 * p + 1], 1 - c, chips):
                cp.wait_recv()
            sent += _passed_on_copies(full_refs, sems[2 * p], sems[2 * p + 1], c, chips)
        if relayed:
            sent += _relayed_copies(full_refs, sems[-1], sems[-1])
        for cp in sent:
            cp.wait_send()
        for cp in own:
            cp.wait()

    return pl.pallas_call(
        body, name=f"all_gather_weights_finish_{layer}",
        in_specs=[HBM] * N_GATHERED + [_SEM] * (3 + len(passed_sems)) + [HBM] * (2 + len(after)),
        out_specs=[HBM] * N_GATHERED, out_shape=[pltpu.HBM(a.shape, a.dtype) for a in landing],
        input_output_aliases={a: a for a in range(N_GATHERED)}, compiler_params=_params(has_side_effects=_DATAFLOW),
    )(*landing, *first_sems, *passed_sems, *shards, *after)


ALL_KINDS = (0, 1, 2)


def _sibling_exchange(grads, kinds):
    n_arr = len(grads)

    def make(in_refs, out_refs, sems):
        send_sems, recv_sems = sems
        x, y, c = _mesh_position()
        copies = [pltpu.make_async_remote_copy(
            src_ref=_BLOCK_OF[kinds[a]](in_refs[a], 2 * q + (1 - c)), dst_ref=out_refs[a].at[q],
            send_sem=send_sems.at[a * N_CHIP + q], recv_sem=recv_sems.at[a * N_CHIP + q],
            device_id=(x, y, 1 - c), device_id_type=MESH)
            for a in range(n_arr) for q in range(N_CHIP)]

        def start():
            for cp in copies:
                cp.start()

        def finish():
            for cp in copies:
                cp.wait_recv()
            for cp in copies:
                cp.wait_send()

        return start, finish

    return _Exchange(
        grads, [jax.ShapeDtypeStruct((N_CHIP,) + _BLOCK_SHAPES[k], BF16) for k in kinds], {},
        [pltpu.SemaphoreType.DMA((n_arr * N_CHIP,)), pltpu.SemaphoreType.DMA((n_arr * N_CHIP,))], make)


def _chips_exchange(layer, partials, received, kinds):
    n_arr = len(partials)

    def make(in_refs, out_refs, sems):
        send_sems, recv_sems = sems
        x, y, c = _mesh_position()
        copies = [pltpu.make_async_remote_copy(
            src_ref=in_refs[a].at[2 * qx + qy, layer], dst_ref=out_refs[a].at[j, layer],
            send_sem=send_sems.at[a * N_OTHER_CHIPS + j], recv_sem=recv_sems.at[a * N_OTHER_CHIPS + j],
            device_id=(qx, qy, c), device_id_type=MESH)
            for a in range(n_arr) for j, (qx, qy) in enumerate(_other_chips(x, y))]

        def start():
            for cp in copies:
                cp.start()

        def finish():
            for cp in copies:
                cp.wait_recv()
            for cp in copies:
                cp.wait_send()

        return start, finish

    inputs = list(partials)
    aliases = {}
    if received is not None:
        inputs += list(received)
        aliases = {n_arr + a: a for a in range(n_arr)}
    return _Exchange(
        inputs, [jax.ShapeDtypeStruct((N_OTHER_CHIPS, DEPTH) + _BLOCK_SHAPES[k], BF16) for k in kinds], aliases,
        [pltpu.SemaphoreType.DMA((n_arr * N_OTHER_CHIPS,)), pltpu.SemaphoreType.DMA((n_arr * N_OTHER_CHIPS,))], make)


def _all_gather_exchange(v, axis=0):
    def make(in_refs, out_refs, sems):
        send_sems, recv_sems, local_sem = sems
        x, y, c = _mesh_position()
        me = _block_id(x, y, c)
        block = lambda blk: out_refs[0].at[(slice(None),) * axis + (blk,)]
        own = pltpu.make_async_copy(in_refs[0], block(me), local_sem.at[0])
        sends, arrivals = [], []
        for k in range(1, N_DEV):
            px, py, pc = x ^ ((k >> 2) & 1), y ^ ((k >> 1) & 1), c ^ (k & 1)
            sends.append(pltpu.make_async_remote_copy(
                src_ref=in_refs[0], dst_ref=block(me), send_sem=send_sems.at[k - 1],
                recv_sem=recv_sems.at[k - 1], device_id=(px, py, pc), device_id_type=MESH))
            arrivals.append(pltpu.make_async_remote_copy(
                src_ref=in_refs[0], dst_ref=block(_block_id(px, py, pc)), send_sem=send_sems.at[k - 1],
                recv_sem=recv_sems.at[k - 1], device_id=(x, y, c), device_id_type=MESH))

        def start():
            for cp in [own] + sends:
                cp.start()

        def finish():
            for cp in arrivals:
                cp.wait_recv()
            for cp in sends:
                cp.wait_send()
            own.wait()

        return start, finish

    return _Exchange(
        [v], [jax.ShapeDtypeStruct(v.shape[:axis] + (N_DEV,) + v.shape[axis:], v.dtype)], {},
        [pltpu.SemaphoreType.DMA((N_DEV - 1,)), pltpu.SemaphoreType.DMA((N_DEV - 1,)),
         pltpu.SemaphoreType.DMA((1,))], make)


def _host(exchange, args, in_specs, out_shape, out_specs, scratch):
    n_own = (len(args), len(out_shape), len(scratch))
    if exchange is None:
        return {}, lambda refs: (refs, None)
    n_ex = (len(exchange.inputs), len(exchange.out_shapes), len(exchange.sem_shapes))
    aliases = {n_own[0] + i: n_own[1] + o for i, o in exchange.aliases.items()}
    args += exchange.inputs
    in_specs += [HBM] * n_ex[0]
    out_shape += exchange.out_shapes
    out_specs += [HBM] * n_ex[1]
    scratch += exchange.sem_shapes

    def split(refs):
        own, theirs, at = [], [], 0
        for mine, ex in zip(n_own, n_ex):
            own += refs[at:at + mine]
            theirs.append(refs[at + mine:at + mine + ex])
            at += mine + ex
        return own, exchange.make(*theirs)

    return aliases, split


def _all_gather_small(v, name, after=()):
    vmem = pl.BlockSpec(memory_space=pltpu.VMEM)

    def body(v_ref, *rest):
        out_ref, send_sems, recv_sems = rest[len(after):]
        x, y, c = _mesh_position()
        me = _block_id(x, y, c)
        out_ref[me] = v_ref[...]
        sends = []
        for k in range(1, N_DEV):
            px, py, pc = x ^ ((k >> 2) & 1), y ^ ((k >> 1) & 1), c ^ (k & 1)
            send = pltpu.make_async_remote_copy(
                src_ref=v_ref, dst_ref=out_ref.at[me], send_sem=send_sems.at[k - 1], recv_sem=recv_sems.at[k - 1],
                device_id=(px, py, pc), device_id_type=MESH)
            send.start()
            sends.append((send, _block_id(px, py, pc)))
        for k, (send, peer) in enumerate(sends):
            pltpu.make_async_remote_copy(
                src_ref=v_ref, dst_ref=out_ref.at[peer], send_sem=send_sems.at[k], recv_sem=recv_sems.at[k],
                device_id=(x, y, c), device_id_type=MESH).wait_recv()
        for send, _ in sends:
            send.wait_send()

    return pl.pallas_call(
        body, name=name, in_specs=[vmem] + [HBM] * len(after), out_specs=vmem,
        out_shape=jax.ShapeDtypeStruct((N_DEV,) + v.shape, v.dtype),
        scratch_shapes=[pltpu.SemaphoreType.DMA((N_DEV - 1,)), pltpu.SemaphoreType.DMA((N_DEV - 1,))],
        compiler_params=_params(),
    )(v, *after)


def _forward_layer(layer, x, vec, cps, wpool, win, wout, target=None):
    t_len = x.shape[0]
    n_tiles = t_len // ROW_TILE
    row = lambda cols: pl.BlockSpec((ROW_TILE, cols), lambda i: (i, 0))
    widths = (D_MODEL, 2 * CONV_W, 3 * CONV_W, 4 * POOL_W)
    head = target is not None

    def body(x_ref, vec_ref, cps_ref, wpool_ref, win_ref, wout_ref, *rest):
        target_ref = rest[0] if head else None
        xo_ref, y_ref, h_ref, ycat_ref, uc_ref, fa_ref, fp_ref = rest[head:head + 7]
        loss_ref = rest[head + 7] if head else None
        zc_ref, pc_ref = rest[-2:]
        i = pl.program_id(0)

        @pl.when(i == 0)
        def _():
            zc_ref[...] = jnp.zeros_like(zc_ref)
            pc_ref[...] = jnp.zeros_like(pc_ref)
            if head:
                loss_ref[...] = jnp.zeros_like(loss_ref)

        x_t = x_ref[...]
        shift, scale, gate = vec_ref[0:1, :], vec_ref[1:2, :], vec_ref[2:3, :]
        g_pre, g_post = vec_ref[3:4, :], vec_ref[4:5, :]
        w0, w1, w2, ps = cps_ref[0:1, :], cps_ref[1:2, :], cps_ref[2:3, :], cps_ref[3:4, :]
        rx = lax.rsqrt(jnp.mean(x_t * x_t, axis=-1, keepdims=True) + NORM_EPS)
        h = (x_t * rx) * g_pre * (1.0 + scale) + shift
        h_ref[...] = h.astype(BF16)
        proj = _dot(h.astype(BF16), win_ref[...])
        u_a, b_a, c_a, g_a, u_p, g_p = _split_proj(proj)
        uc_ref[...] = jnp.concatenate([u_a, c_a], axis=1).astype(BF16)

        z = c_a * u_a
        zcat = jnp.concatenate([zc_ref[...], z], axis=0)
        zc_ref[...] = z[ROW_TILE - CONV_HALO:]
        conv = (w0 * _rows_from_before(zcat, 2)[CONV_HALO:] + w1 * _rows_from_before(zcat, 1)[CONV_HALO:] + w2 * z)
        sig_a = _sigmoid(g_a)
        silu_a = g_a * sig_a
        b_conv = b_a * conv
        y_a = b_conv * silu_a
        fa_ref[...] = jnp.concatenate(
            [silu_a * conv, silu_a * b_a, b_conv * (sig_a + silu_a * (1.0 - sig_a))], axis=1).astype(BF16)

        pcat = jnp.concatenate([pc_ref[...], u_p], axis=0)
        pc_ref[...] = u_p[ROW_TILE - POOL_HALO:]
        counts = _window_counts(i * ROW_TILE, ROW_TILE)
        pooled, mixed = [], []
        for g, w in enumerate(POOL_WINDOWS):
            cols = slice(g * GROUP_D, (g + 1) * GROUP_D)
            s = pcat[:, cols]
            step = 1
            while step < w:
                s = s + _rows_from_before(s, step)
                step *= 2
            pooled_g = (s[POOL_HALO:] * (1.0 / jnp.minimum(counts, float(w))) - u_p[:, cols]).astype(BF16)
            pooled.append(pooled_g)
            mixed.append(_dot(pooled_g, wpool_ref[g]))
        mixed = jnp.concatenate(mixed, axis=1)
        sig_p = _sigmoid(g_p)
        silu_p = g_p * sig_p
        mixed_ps = mixed * ps
        y_p = mixed_ps * silu_p
        fp_ref[...] = jnp.concatenate(
            [(ps * silu_p).astype(BF16), (mixed_ps * (sig_p + silu_p * (1.0 - sig_p))).astype(BF16),
             (silu_p * mixed).astype(BF16)] + pooled, axis=1)

        ycat = jnp.concatenate([y_a, y_p], axis=1)
        ycat_ref[...] = ycat.astype(BF16)
        y_b = _dot(ycat.astype(BF16), wout_ref[...]).astype(BF16)
        y_ref[...] = y_b
        y_t = y_b.astype(F32)
        ry = lax.rsqrt(jnp.mean(y_t * y_t, axis=-1, keepdims=True) + NORM_EPS)
        x_next = x_t + gate * (y_t * ry * g_post)
        if head:
            err = x_next - target_ref[...]
            xo_ref[...] = err * (1.0 / D_MODEL)
            loss_ref[...] += jnp.sum(err * err) * (0.5 / D_MODEL)
        else:
            xo_ref[...] = x_next

    tile = (SUBLANES, LANES)
    return pl.pallas_call(
        body, name=f"forward_layer_{layer}", grid=(n_tiles,),
        in_specs=[row(D_MODEL), _layer_spec(vec.shape, layer), _layer_spec(cps.shape, layer),
                  _layer_spec(wpool.shape, layer), _whole_spec(win.shape), _whole_spec(wout.shape)]
        + [row(D_MODEL)] * head,
        out_specs=[row(D_MODEL), row(D_MODEL), row(D_MODEL), row(D_MODEL)] + [row(w) for w in widths[1:]]
        + [_whole_spec(tile)] * head,
        out_shape=[jax.ShapeDtypeStruct((t_len, D_MODEL), F32), jax.ShapeDtypeStruct((t_len, D_MODEL), BF16),
                   jax.ShapeDtypeStruct((t_len, D_MODEL), BF16), jax.ShapeDtypeStruct((t_len, D_MODEL), BF16)]
        + [jax.ShapeDtypeStruct((t_len, w), BF16) for w in widths[1:]] + [jax.ShapeDtypeStruct(tile, F32)] * head,
        scratch_shapes=[pltpu.VMEM((CONV_HALO, CONV_W), F32), pltpu.VMEM((POOL_HALO, POOL_W), F32)],
        compiler_params=_params(dimension_semantics=("arbitrary",)),
    )(x, vec, cps, wpool, win, wout, *([target] * head))


def _backward_layer(layer, dxo, y, x, uc, fa, fp, vec, cps, wpool, wout, win, after):
    t_len = dxo.shape[0]
    n_tiles = t_len // BWD_TILE
    halo_per_tile = BWD_TILE // POOL_HALO
    rev = lambda cols: pl.BlockSpec((BWD_TILE, cols), lambda i: (n_tiles - 1 - i, 0))
    halo_spec = pl.BlockSpec(
        (POOL_HALO, 2 * CONV_W), lambda i: (jnp.maximum((n_tiles - 1 - i) * halo_per_tile - 1, 0), 0))
    gwpool_shape = (len(POOL_WINDOWS), GROUP_D, GROUP_D)

    def body(dxo_ref, y_ref, x_ref, uc_ref, uch_ref, fa_ref, fp_ref, vec_ref, cps_ref, wpool_ref, wout_ref, win_ref,
             *rest):
        dx_ref, dproj_ref, dy_ref, gwpool_ref, dcps_ref, dvec_ref, gwpool_acc, dcc_ref, qc_ref = rest[len(after):]
        i = pl.program_id(0)
        tile = n_tiles - 1 - i

        @pl.when(i == 0)
        def _():
            gwpool_acc[...] = jnp.zeros_like(gwpool_acc)
            dcps_ref[...] = jnp.zeros_like(dcps_ref)
            dvec_ref[...] = jnp.zeros_like(dvec_ref)
            dcc_ref[...] = jnp.zeros_like(dcc_ref)
            qc_ref[...] = jnp.zeros_like(qc_ref)

        shift, scale, gate = vec_ref[0:1, :], vec_ref[1:2, :], vec_ref[2:3, :]
        g_pre, g_post = vec_ref[3:4, :], vec_ref[4:5, :]
        w0, w1, w2 = cps_ref[0:1, :], cps_ref[1:2, :], cps_ref[2:3, :]

        dxo_t = dxo_ref[...]
        y_t = y_ref[...].astype(F32)
        ry = lax.rsqrt(jnp.mean(y_t * y_t, axis=-1, keepdims=True) + NORM_EPS)
        yh = y_t * ry
        dvec_ref[2:3, :] += jnp.sum(dxo_t * yh, axis=0, keepdims=True)
        dyh = dxo_t * (gate * g_post)
        dy_b = (ry * (dyh - yh * jnp.mean(dyh * yh, axis=-1, keepdims=True))).astype(BF16)
        dy_ref[...] = dy_b
        dycat = _dot_nt(dy_b, wout_ref[...])
        dy_a, dy_p = dycat[:, :CONV_W], dycat[:, CONV_W:]

        fa_t = fa_ref[...].astype(F32)
        db_a = dy_a * fa_t[:, :CONV_W]
        dconv = dy_a * fa_t[:, CONV_W:2 * CONV_W]
        dg_a = dy_a * fa_t[:, 2 * CONV_W:]
        uc_t = uc_ref[...].astype(F32)
        u_a, c_a = uc_t[:, :CONV_W], uc_t[:, CONV_W:]
        halo = jnp.where(tile > 0, uch_ref[...].astype(F32), 0.0)[POOL_HALO - CONV_HALO:]
        z = c_a * u_a
        zcat = jnp.concatenate([halo[:, CONV_W:] * halo[:, :CONV_W], z], axis=0)
        z1 = _rows_from_before(zcat, 1)[CONV_HALO:]
        z2 = _rows_from_before(zcat, 2)[CONV_HALO:]
        dccat = jnp.concatenate([dconv, dcc_ref[...]], axis=0)
        dc1 = _rows_from_after(dccat, 1)[:BWD_TILE]
        dc2 = _rows_from_after(dccat, 2)[:BWD_TILE]
        dz = w2 * dconv + w1 * dc1 + w0 * dc2
        dcc_ref[...] = dconv[:CONV_HALO]
        dcps_ref[0:1, :] += jnp.sum(dconv * z2, axis=0, keepdims=True)
        dcps_ref[1:2, :] += jnp.sum(dconv * z1, axis=0, keepdims=True)
        dcps_ref[2:3, :] += jnp.sum(dconv * z, axis=0, keepdims=True)
        du_a = dz * c_a
        dc_a = dz * u_a

        dmixed = (dy_p * fp_ref[:, :POOL_W].astype(F32)).astype(BF16)
        dg_p = dy_p * fp_ref[:, POOL_W:2 * POOL_W].astype(F32)
        dcps_ref[3:4, :] += jnp.sum(dy_p * fp_ref[:, 2 * POOL_W:3 * POOL_W].astype(F32), axis=0, keepdims=True)
        counts = _window_counts(tile * BWD_TILE, BWD_TILE)
        du_p, q_head = [], []
        for g, w in enumerate(POOL_WINDOWS):
            cols = slice(g * GROUP_D, (g + 1) * GROUP_D)
            dm_g = dmixed[:, cols]
            dpooled_g = _dot_nt(dm_g, wpool_ref[g])
            gwpool_acc[g] += _dot_tn(fp_ref[:, 3 * POOL_W + g * GROUP_D:3 * POOL_W + (g + 1) * GROUP_D], dm_g)
            q_g = dpooled_g * (1.0 / jnp.minimum(counts, float(w)))
            q_head.append(q_g[:POOL_HALO])
            s = jnp.concatenate([q_g, qc_ref[:, cols]], axis=0)
            step = 1
            while step < w:
                s = s + _rows_from_after(s, step)
                step *= 2
            du_p.append(s[:BWD_TILE] - dpooled_g)
        qc_ref[...] = jnp.concatenate(q_head, axis=1)
        dproj_b = jnp.concatenate([du_a, db_a, dc_a, dg_a] + du_p + [dg_p], axis=1).astype(BF16)
        dproj_ref[...] = dproj_b

        x_t = x_ref[...]
        rx = lax.rsqrt(jnp.mean(x_t * x_t, axis=-1, keepdims=True) + NORM_EPS)
        xn = x_t * rx
        mod_scale = 1.0 + scale
        dh = _dot_nt(dproj_b, win_ref[...])
        dvec_ref[0:1, :] += jnp.sum(dh, axis=0, keepdims=True)
        dvec_ref[1:2, :] += jnp.sum(dh * xn, axis=0, keepdims=True)
        dxn = dh * (g_pre * mod_scale)
        dx_ref[...] = dxo_t + rx * (dxn - xn * jnp.mean(dxn * xn, axis=-1, keepdims=True))

        @pl.when(i == n_tiles - 1)
        def _():
            gwpool_ref[...] = gwpool_acc[...].astype(BF16)
            sum_dh_xn, sum_dxo_yh = dvec_ref[1:2, :], dvec_ref[2:3, :]
            dvec_ref[1:2, :] = sum_dh_xn * g_pre
            dvec_ref[3:4, :] = sum_dh_xn * mod_scale
            dvec_ref[2:3, :] = sum_dxo_yh * g_post
            dvec_ref[4:5, :] = sum_dxo_yh * gate

    return pl.pallas_call(
        body, name=f"backward_layer_{layer}", grid=(n_tiles,),
        in_specs=[rev(D_MODEL), rev(D_MODEL), rev(D_MODEL), rev(2 * CONV_W), halo_spec, rev(3 * CONV_W),
                  rev(4 * POOL_W), _layer_spec(vec.shape, layer), _layer_spec(cps.shape, layer),
                  _layer_spec(wpool.shape, layer), _whole_spec(wout.shape), _whole_spec(win.shape)]
        + [HBM] * len(after),
        out_specs=[rev(D_MODEL), rev(IN_COLS), rev(D_MODEL), _whole_spec(gwpool_shape),
                   _whole_spec((SUBLANES, CONV_W)), _whole_spec((SUBLANES, D_MODEL))],
        out_shape=[jax.ShapeDtypeStruct((t_len, D_MODEL), F32), jax.ShapeDtypeStruct((t_len, IN_COLS), BF16),
                   jax.ShapeDtypeStruct((t_len, D_MODEL), BF16), jax.ShapeDtypeStruct(gwpool_shape, BF16),
                   jax.ShapeDtypeStruct((SUBLANES, CONV_W), F32), jax.ShapeDtypeStruct((SUBLANES, D_MODEL), F32)],
        scratch_shapes=[pltpu.VMEM(gwpool_shape, F32), pltpu.VMEM((CONV_HALO, CONV_W), F32),
                        pltpu.VMEM((POOL_HALO, POOL_W), F32)],
        compiler_params=_params(dimension_semantics=("arbitrary",)),
    )(dxo, y, x, uc, uc, fa, fp, vec, cps, wpool, wout, win, *after)


def _weight_grads(layer, h, dproj, ycat, dy, exchange, after=()):
    t_len = dy.shape[0]
    n_in, n_out = IN_COLS // GWIN_COLS, D_MODEL // GWOUT_COLS
    args = [h, dproj, ycat, dy, *after]
    in_specs = [_whole_spec(h.shape),
                pl.BlockSpec((t_len, GWIN_COLS), lambda s: (0, jnp.minimum(s, n_in - 1))),
                _whole_spec(ycat.shape),
                pl.BlockSpec((t_len, GWOUT_COLS), lambda s: (0, jnp.maximum(s - n_in, 0)))] + [HBM] * len(after)
    out_shape = [jax.ShapeDtypeStruct((D_MODEL, IN_COLS), BF16), jax.ShapeDtypeStruct((D_MODEL, D_MODEL), BF16)]
    out_specs = [pl.BlockSpec((D_MODEL, GWIN_COLS), lambda s: (0, jnp.minimum(s, n_in - 1))),
                 pl.BlockSpec((D_MODEL, GWOUT_COLS), lambda s: (0, jnp.maximum(s - n_in, 0)))]
    scratch = []
    aliases, split = _host(exchange, args, in_specs, out_shape, out_specs, scratch)

    def body(*refs):
        (h_ref, dproj_ref, ycat_ref, dy_ref, *_, gwin_ref, gwout_ref), hosted = split(refs)
        s = pl.program_id(0)
        if hosted is not None:
            pl.when(s == 0)(hosted[0])

        @pl.when(s < n_in)
        def _():
            gwin_ref[...] = _dot_tn(h_ref[...], dproj_ref[...]).astype(BF16)

        @pl.when(s >= n_in)
        def _():
            gwout_ref[...] = _dot_tn(ycat_ref[...], dy_ref[...]).astype(BF16)

        if hosted is not None:
            pl.when(s == n_in + n_out - 1)(hosted[1])

    return pl.pallas_call(
        body, name=f"weight_grads_{layer}", grid=(n_in + n_out,), in_specs=in_specs, out_specs=out_specs,
        out_shape=out_shape, scratch_shapes=scratch, input_output_aliases=aliases,
        compiler_params=_params(dimension_semantics=("arbitrary",)),
    )(*args)


def _add_sibling_blocks(name, layer, grads, received, core, partials, kinds):
    n_arr = len(grads)

    def body(core_ref, *refs):
        mine, theirs, outs = refs[:n_arr], refs[n_arr:2 * n_arr], refs[-n_arr:]
        for a in range(n_arr):
            outs[a][...] = (mine[a][...].astype(F32) + theirs[a][...].astype(F32)).astype(BF16)

    own_of_kind = [
        pl.BlockSpec((D_MODEL, W_IN_SHARD), lambda q, core_ref: (0, 2 * q + core_ref[0])),
        pl.BlockSpec((W_OUT_SHARD, D_MODEL), lambda q, core_ref: (2 * q + core_ref[0], 0)),
        pl.BlockSpec((POOL_SHARD, GROUP_D), lambda q, core_ref: (2 * q + core_ref[0], 0)),
    ]
    shapes = [_BLOCK_SHAPES[k] for k in kinds]
    recv_specs = [pl.BlockSpec((None,) + s, lambda q, core_ref: (q, 0, 0)) for s in shapes]
    out_specs = [pl.BlockSpec((None, None) + s, lambda q, core_ref: (q, layer, 0, 0)) for s in shapes]
    args = [core, *grads, *received]
    in_specs = [own_of_kind[k] for k in kinds] + recv_specs
    aliases = {}
    if partials is not None:
        aliases = {len(args) + a: a for a in range(n_arr)}
        args += list(partials)
        in_specs += [HBM] * n_arr
    return pl.pallas_call(
        body, name=name,
        grid_spec=pltpu.PrefetchScalarGridSpec(
            num_scalar_prefetch=1, grid=(N_CHIP,), in_specs=in_specs, out_specs=out_specs),
        out_shape=[jax.ShapeDtypeStruct((N_CHIP, DEPTH) + s, BF16) for s in shapes],
        input_output_aliases=aliases,
        compiler_params=_params(dimension_semantics=("arbitrary",)),
    )(*args)


def _modulation_columns(c_all, w_ada):
    def body(c_ref, w_ref, cact_ref, out_ref):
        c_t = c_ref[...]
        c_act = c_t * _sigmoid(c_t)
        cact_ref[...] = c_act
        out_ref[...] = jnp.dot(c_act, w_ref[...], preferred_element_type=F32, precision=lax.Precision.HIGHEST)

    return pl.pallas_call(
        body, name="modulation_columns", grid=(DEPTH,),
        in_specs=[pl.BlockSpec((N_DEV, D_MODEL), lambda l: (0, 0)),
                  pl.BlockSpec((None, D_MODEL, W_IN_SHARD), lambda l: (l, 0, 0))],
        out_specs=[pl.BlockSpec((N_DEV, D_MODEL), lambda l: (0, 0)),
                   pl.BlockSpec((N_DEV, W_IN_SHARD), lambda l: (0, l))],
        out_shape=[jax.ShapeDtypeStruct((N_DEV, D_MODEL), F32),
                   jax.ShapeDtypeStruct((N_DEV, DEPTH * W_IN_SHARD), F32)],
        compiler_params=_params(dimension_semantics=("arbitrary",)),
    )(c_all, w_ada)


def _adamw(w, g, m, v):
    m_new = ADAM_B1 * m + (1.0 - ADAM_B1) * g
    v_new = ADAM_B2 * v + (1.0 - ADAM_B2) * (g * g)
    m_hat = m_new / (1.0 - ADAM_B1 ** ADAM_STEP)
    v_hat = v_new / (1.0 - ADAM_B2 ** ADAM_STEP)
    delta = -ADAM_LR * (m_hat / (jnp.sqrt(v_hat) + ADAM_EPS) + ADAM_WD * w)
    return delta, m_new, v_new


def _adamw_w_ada(w, m, v, c_act_t, slabs, block, after):
    def body(block_ref, w_ref, m_ref, v_ref, ct_ref, dm_ref, *rest):
        g_ref, d_ref, mo_ref, vo_ref = rest[len(after):]
        layer = pl.ds(pl.program_id(0), 1)
        g = ct_ref[:, 0:1] * dm_ref[0, layer, :]
        for b in range(1, N_DEV):
            g = g + ct_ref[:, b:b + 1] * dm_ref[b, layer, :]
        g_ref[...] = g
        d_ref[...], mo_ref[...], vo_ref[...] = _adamw(w_ref[...], g, m_ref[...], v_ref[...])

    big = pl.BlockSpec((None, D_MODEL, W_IN_SHARD), lambda l, block_ref: (l, 0, 0))
    return pl.pallas_call(
        body, name="adamw_w_ada",
        grid_spec=pltpu.PrefetchScalarGridSpec(
            num_scalar_prefetch=1, grid=(DEPTH,),
            in_specs=[big, big, big, pl.BlockSpec((D_MODEL, N_DEV), lambda l, block_ref: (0, 0)),
                      pl.BlockSpec((N_DEV, DEPTH, W_IN_SHARD), lambda l, block_ref: (0, 0, block_ref[0]))]
            + [HBM] * len(after),
            out_specs=[big] * 4),
        out_shape=[jax.ShapeDtypeStruct(w.shape, F32)] * 4,
        compiler_params=_params(dimension_semantics=("arbitrary",)),
    )(block, w, m, v, c_act_t, slabs, *after)


def _sum_chip_partials(own_ref, recv_ref):
    g = own_ref[...].astype(F32)
    for j in range(N_OTHER_CHIPS):
        g = g + recv_ref[j].astype(F32)
    return g


def _partial_specs(row_tile, cols, first_layer=0):
    own = pl.BlockSpec((None, None, row_tile, cols), lambda l, r, chip_ref: (chip_ref[0], first_layer + l, r, 0))
    recv = pl.BlockSpec((N_OTHER_CHIPS, None, row_tile, cols), lambda l, r, chip_ref: (0, first_layer + l, r, 0))
    return own, recv


def _adamw_reduced(name, w, m, v, partial, received, chip, row_tile, layers, continued, after=()):
    depth, rows, cols = w.shape
    first, stop = layers

    def body(chip_ref, w_ref, m_ref, v_ref, own_ref, recv_ref, *rest):
        g_ref, d_ref, mo_ref, vo_ref = rest[-4:]
        g = _sum_chip_partials(own_ref, recv_ref)
        g_ref[...] = g
        d_ref[...], mo_ref[...], vo_ref[...] = _adamw(w_ref[...], g, m_ref[...], v_ref[...])

    blk = pl.BlockSpec((None, row_tile, cols), lambda l, r, chip_ref: (first + l, r, 0))
    args = [chip, w, m, v, partial, received]
    in_specs = [blk, blk, blk, *_partial_specs(row_tile, cols, first)]
    aliases = {}
    if continued is not None:
        aliases = {len(args) + k: k for k in range(4)}
        args += list(continued)
        in_specs += [HBM] * 4
    args += after
    in_specs += [HBM] * len(after)
    return pl.pallas_call(
        body, name=name,
        grid_spec=pltpu.PrefetchScalarGridSpec(
            num_scalar_prefetch=1, grid=(stop - first, rows // row_tile), in_specs=in_specs, out_specs=[blk] * 4),
        out_shape=[jax.ShapeDtypeStruct(w.shape, F32)] * 4, input_output_aliases=aliases,
        compiler_params=_params(dimension_semantics=("arbitrary", "arbitrary")),
    )(*args)


def _reduce_w_pool(partial, received, chip):
    def body(chip_ref, own_ref, recv_ref, g_ref):
        g_ref[...] = _sum_chip_partials(own_ref, recv_ref)

    return pl.pallas_call(
        body, name="reduce_w_pool",
        grid_spec=pltpu.PrefetchScalarGridSpec(
            num_scalar_prefetch=1, grid=(DEPTH, 1), in_specs=list(_partial_specs(POOL_SHARD, GROUP_D)),
            out_specs=pl.BlockSpec((None, POOL_SHARD, GROUP_D), lambda l, r, chip_ref: (l, 0, 0))),
        out_shape=jax.ShapeDtypeStruct((DEPTH, POOL_SHARD, GROUP_D), F32),
        compiler_params=_params(dimension_semantics=("arbitrary", "arbitrary")),
    )(chip, partial, received)


def _adamw_small(name, params):
    n = len(params)

    def body(*refs):
        ins, outs = refs[:4 * n], refs[4 * n:]
        for p in range(n):
            w_ref, g_ref, m_ref, v_ref = ins[4 * p:4 * p + 4]
            d_ref, mo_ref, vo_ref, go_ref = outs[4 * p:4 * p + 4]
            d_ref[...], mo_ref[...], vo_ref[...] = _adamw(w_ref[...], g_ref[...], m_ref[...], v_ref[...])
            go_ref[...] = g_ref[...]

    vmem = pl.BlockSpec(memory_space=pltpu.VMEM)
    flat = [a for group in params for a in group]
    outs = pl.pallas_call(
        body, name=name, in_specs=[vmem] * len(flat), out_specs=[vmem] * len(flat),
        out_shape=[jax.ShapeDtypeStruct(a.shape, F32) for a in flat], compiler_params=_params(),
    )(*flat)
    return [tuple(outs[4 * p:4 * p + 4]) for p in range(n)]


def _sum_sources(slabs, columns, after):
    def body(s_ref, *rest):
        outs = rest[len(after):]
        acc = s_ref[0]
        for b in range(1, N_DEV):
            acc = acc + s_ref[b]
        for (start, stop), o_ref in zip(columns, outs):
            o_ref[...] = acc[:, start:stop]
        outs[-1][...] = acc[0:1, SLAB_COLS:SLAB_COLS + 1]

    vmem = pl.BlockSpec(memory_space=pltpu.VMEM)
    out_shape = [jax.ShapeDtypeStruct((slabs.shape[1], stop - start), F32) for start, stop in columns]
    out_shape.append(jax.ShapeDtypeStruct((1, 1), F32))
    return pl.pallas_call(
        body, name="sum_small_grads", in_specs=[vmem] + [HBM] * len(after), out_specs=[vmem] * len(out_shape),
        out_shape=out_shape, compiler_params=_params(),
    )(slabs, *after)


def _to_bf16(a, name, layers=None):
    first, stop = layers or (0, a.shape[0])

    def body(a_ref, o_ref):
        o_ref[...] = a_ref[...].astype(BF16)

    block = (None,) + a.shape[1:]
    return pl.pallas_call(
        body, name=name, grid=(stop - first,), in_specs=[pl.BlockSpec(block, lambda l: (first + l, 0, 0))],
        out_specs=pl.BlockSpec(block, lambda l: (l, 0, 0)),
        out_shape=jax.ShapeDtypeStruct((stop - first,) + a.shape[1:], BF16),
        compiler_params=_params(dimension_semantics=("arbitrary",)),
    )(a)


def kernel(x, c, w_ada, b_ada, g_pre, w_in, w_conv, w_pool, pool_scale, w_out, g_post, loss_target, m_w_ada, m_b_ada, m_g_pre, m_w_in, m_w_conv, m_w_pool, m_pool_scale, m_w_out, m_g_post, v_w_ada, v_b_ada, v_g_pre, v_w_in, v_w_conv, v_w_pool, v_pool_scale, v_w_out, v_g_post):
    mx, my, mc = _mesh_position()
    me = _block_id(mx, my, mc)
    chip = (2 * mx + my).astype(jnp.int32).reshape(1)
    core = mc.astype(jnp.int32).reshape(1)
    x0 = x[0]
    target = loss_target[0]
    conv_shard = w_conv.shape[-1]

    own_small = jnp.concatenate([c, w_conv.reshape(1, DEPTH * 3 * conv_shard)], axis=1)
    first_shards = [_to_bf16(w_in, "cast_w_in_0", (0, 1)), _to_bf16(w_out, "cast_w_out_0", (0, 1))]
    all_small = _all_gather_small(own_small, "all_gather_c_w_conv", first_shards)[:, 0, :]
    gathers = [_gather_weights_start(0, *first_shards, [all_small], relayed=True)]
    c_all = all_small[:, :D_MODEL]
    w_conv_full = all_small[:, D_MODEL:].reshape(N_DEV, DEPTH, 3, conv_shard).transpose(1, 2, 0, 3).reshape(
        DEPTH, 3, CONV_W)
    cps = jnp.concatenate([w_conv_full, pool_scale[:, None], jnp.zeros((DEPTH, 4, CONV_W), F32)], axis=1)

    c_act, pieces = _modulation_columns(c_all, w_ada)
    c_act_t = c_act.T
    upper = (1, DEPTH)
    upper_shards = [_to_bf16(w_in, "cast_w_in_1", upper), _to_bf16(w_out, "cast_w_out_1", upper)]
    wpool_b = _to_bf16(w_pool.reshape(DEPTH, POOL_ROWS, GROUP_D), "cast_w_pool").reshape(w_pool.shape)
    first_sems_0, _, (zones_0,) = gathers[0]
    neighbour_sems, zones_0 = _gather_weights_pass_on(
        "all_gather_weights_relay_0", NEIGHBOUR_CHIPS, first_sems_0[1], partial(_first_arrival, 0), zones_0,
        [pieces, *upper_shards, wpool_b, cps, c_act_t], relay=True)
    gather_mod = _all_gather_exchange(pieces)
    sems_m, pieces_thru, mod_landing, token_m = _start_exchange(gather_mod, "all_gather_modulation_start", [zones_0[0]])
    diagonal_sems, zones_0 = _gather_weights_pass_on(
        "all_gather_weights_pass_on_0", DIAGONAL_CHIP, neighbour_sems[3], lambda a, j: a, zones_0, [token_m])
    _, (mod_all,) = _finish_exchange(
        gather_mod, "all_gather_modulation_finish", sems_m, pieces_thru, mod_landing, [zones_0[0]])
    mod_mine = lax.dynamic_index_in_dim(mod_all, me, axis=1, keepdims=False)
    mod = mod_mine.reshape(N_DEV, DEPTH, W_IN_SHARD).transpose(1, 0, 2).reshape(DEPTH, 3 * D_MODEL) + b_ada
    zeros_d = jnp.zeros((DEPTH, 3, D_MODEL), F32)
    vec = jnp.concatenate([mod.reshape(DEPTH, 3, D_MODEL), g_pre[:, None], g_post[:, None], zeros_d], axis=1)

    gathers.append(_gather_weights_start(1, *upper_shards, [mod_all]))
    gathers = [(first_sems, shards, landing[k], k) for first_sems, shards, landing in gathers
               for k in range(len(landing))]

    xs, kept, wins, wouts = [x0], [], [], []
    for l in range(DEPTH):
        first_sems, shards, zones, index = gathers[l]
        if l == 0:
            passed = [(neighbour_sems[:2], NEIGHBOUR_CHIPS), (diagonal_sems, DIAGONAL_CHIP)]
            win, wout = _gather_weights_finish(
                l, index, first_sems, passed, shards, zones_0, neighbour_sems[2], [gathers[-1][1][0]])
        else:
            passed_sems, zones = _gather_weights_pass_on(
                f"all_gather_weights_pass_on_{l}", ALL_OTHER_CHIPS, first_sems[1], partial(_first_arrival, index),
                zones, [xs[-1]])
            win, wout = _gather_weights_finish(l, index, first_sems, [(passed_sems, ALL_OTHER_CHIPS)], shards, zones)
        x_next, *for_backward = _forward_layer(
            l, xs[-1], vec, cps, wpool_b, win, wout, target if l == DEPTH - 1 else None)
        xs.append(x_next)
        kept.append(for_backward[:6])
        wins.append(win)
        wouts.append(wout)
    dx, loss_tile = xs[DEPTH], for_backward[6]

    slab_rows = [None] * DEPTH
    partials = received = None
    in_flight = []

    def scatter(layer, grads, from_sibling, after):
        nonlocal partials, received
        partials = _add_sibling_blocks(
            f"grad_add_sibling_{layer}", layer, grads, from_sibling, core, partials, ALL_KINDS)
        chips = _chips_exchange(layer, partials, received, ALL_KINDS)
        sems, partials, received, token = _start_exchange(chips, f"grad_chips_start_{layer}", after)
        in_flight.append((chips, sems, layer))
        return token

    grads_above = None
    issued = []
    for l in reversed(range(DEPTH)):
        y, h, ycat, uc, fa, fp = kept[l]
        dx, dproj, dy, gwpool, dcps, dvec = _backward_layer(
            l, dx, y, xs[l], uc, fa, fp, vec, cps, wpool_b, wouts[l], wins[l], issued)
        slab_rows[l] = jnp.concatenate(
            [dvec[0], dvec[1], dvec[2], dvec[3], dvec[4], dcps[3], dcps[0], dcps[1], dcps[2],
             loss_tile[0] if l == 0 else jnp.zeros((LANES,), F32)])
        after = []
        if l == 0:
            gather_small = _all_gather_exchange(jnp.stack(slab_rows))
            sems_s, slab, slabs, token = _start_exchange(gather_small, "all_gather_small_grads_start")
            after = [token]
        hosted = _sibling_exchange(grads_above, ALL_KINDS) if grads_above is not None else None
        gwin, gwout, *from_sibling = _weight_grads(l, h, dproj, ycat, dy, hosted, after)
        if l == 0:
            _, (slabs,) = _finish_exchange(
                gather_small, "all_gather_small_grads_finish", sems_s, slab, slabs, [gwin])
        issued = [gwin]
        if grads_above is not None:
            issued = [scatter(l + 1, grads_above, from_sibling, [])]
        grads_above = [gwin, gwout, gwpool.reshape(POOL_ROWS, GROUP_D)]
        if l <= 1:
            from_sibling = _run_exchange(_sibling_exchange(grads_above, ALL_KINDS), f"grad_exchange_sibling_{l}")
            token = scatter(l, grads_above, from_sibling, [slabs] if l == 0 else [])
            issued = [token]
            grads_above = None
    grad_x = dx[None]
    chips_0, sems_0, _ = in_flight.pop()

    o = 3 * D_MODEL

    after = [token]
    for chips, sems, l in in_flight:
        partials, received = _finish_exchange(chips, f"grad_chips_finish_{l}", sems, partials, received, after)
        after = []
    upper = (1, DEPTH)
    w_in_upper = _adamw_reduced(
        "adamw_w_in_upper", w_in, m_w_in, v_w_in, partials[0], received[0], chip, ROW_TILE, upper, None)
    w_out_upper = _adamw_reduced(
        "adamw_w_out_upper", w_out, m_w_out, v_w_out, partials[1], received[1], chip, W_OUT_SHARD, upper, None)
    g_w_ada, d_w_ada, nm_w_ada, nv_w_ada = _adamw_w_ada(
        w_ada, m_w_ada, v_w_ada, c_act_t, slabs, me.astype(jnp.int32).reshape(1), [token])

    partials, received = _finish_exchange(
        chips_0, "grad_chips_finish_0", sems_0, partials, received, [nv_w_ada, w_in_upper[3], w_out_upper[3]])
    gather_pool = _all_gather_exchange(_reduce_w_pool(partials[2], received[2], chip), axis=1)
    sems_p, pool_rows, pool_landing, token_p = _start_exchange(gather_pool, "all_gather_grad_w_pool_start")
    g_w_in, d_w_in, nm_w_in, nv_w_in = _adamw_reduced(
        "adamw_w_in_0", w_in, m_w_in, v_w_in, partials[0], received[0], chip, ROW_TILE, (0, 1), w_in_upper, [token_p])
    g_w_out, d_w_out, nm_w_out, nv_w_out = _adamw_reduced(
        "adamw_w_out_0", w_out, m_w_out, v_w_out, partials[1], received[1], chip, W_OUT_SHARD, (0, 1), w_out_upper,
        [token_p])
    ends = [0, o, o + D_MODEL, o + 2 * D_MODEL, o + 2 * D_MODEL + POOL_W, SLAB_COLS]
    g_b_ada, g_g_pre, g_g_post, g_pool_scale, g_conv, loss = _sum_sources(
        slabs, list(zip(ends[:-1], ends[1:])), [token_p, nv_w_in, nv_w_out])
    loss = loss.reshape(())
    g_w_conv = lax.dynamic_slice_in_dim(g_conv.reshape(DEPTH, 3, CONV_W), me * conv_shard, conv_shard, axis=2)
    taps_first = lambda a: a.transpose(1, 0, 2)
    small = _adamw_small("adamw_small", [
        (b_ada, g_b_ada, m_b_ada, v_b_ada),
        (g_pre, g_g_pre, m_g_pre, v_g_pre),
        tuple(taps_first(a) for a in (w_conv, g_w_conv, m_w_conv, v_w_conv)),
        (pool_scale, g_pool_scale, m_pool_scale, v_pool_scale),
        (g_post, g_g_post, m_g_post, v_g_post),
    ])
    (d_b_ada, nm_b_ada, nv_b_ada, g_b_ada), (d_g_pre, nm_g_pre, nv_g_pre, g_g_pre), conv_steps, \
        (d_ps, nm_ps, nv_ps, g_pool_scale), (d_g_post, nm_g_post, nv_g_post, g_g_post) = small
    d_w_conv, nm_w_conv, nv_w_conv, g_w_conv = (taps_first(a) for a in conv_steps)
    _, (g_pool_all,) = _finish_exchange(
        gather_pool, "all_gather_grad_w_pool_finish", sems_p, pool_rows, pool_landing, [nv_w_in, nv_w_out, nv_g_post])
    ((d_w_pool, nm_w_pool, nv_w_pool, g_w_pool),) = _adamw_small(
        "adamw_w_pool", [(w_pool, g_pool_all.reshape(w_pool.shape), m_w_pool, v_w_pool)])

    return (loss, grad_x,
            g_w_ada, g_b_ada, g_g_pre, g_w_in, g_w_conv, g_w_pool, g_pool_scale, g_w_out, g_g_post,
            d_w_ada, d_b_ada, d_g_pre, d_w_in, d_w_conv, d_w_pool, d_ps, d_w_out, d_g_post,
            nm_w_ada, nm_b_ada, nm_g_pre, nm_w_in, nm_w_conv, nm_w_pool, nm_ps, nm_w_out, nm_g_post,
            nv_w_ada, nv_b_ada, nv_g_pre, nv_w_in, nv_w_conv, nv_w_pool, nv_ps, nv_w_out, nv_g_post)
```

```python
from functools import partial

import jax
import jax.numpy as jnp
from jax import lax
from jax.experimental import pallas as pl
from jax.experimental.pallas import tpu as pltpu

F32 = jnp.float32
BF16 = jnp.bfloat16

D_MODEL = 1024
DEPTH = 4
CONV_W = 512
POOL_W = 512
POOL_WINDOWS = (2, 4, 8, 16)
GROUP_D = 128
IN_COLS = 4 * CONV_W + 2 * POOL_W
NORM_EPS = 1e-6

ADAM_LR = 0.001
ADAM_B1 = 0.9
ADAM_B2 = 0.999
ADAM_EPS = 1e-08
ADAM_WD = 0.01
ADAM_STEP = 10

N_DEV = 8
N_CHIP = 4
N_OTHER_CHIPS = N_CHIP - 1
MESH = pl.DeviceIdType.MESH
W_IN_SHARD = IN_COLS // N_DEV
W_OUT_SHARD = D_MODEL // N_DEV
POOL_ROWS = len(POOL_WINDOWS) * GROUP_D
POOL_SHARD = POOL_ROWS // N_DEV

SUBLANES = 8
LANES = 128
VMEM_LIMIT_BYTES = 56 * 1024 * 1024
ROW_TILE = 512
BWD_TILE = 256
GWIN_COLS = 768
GWOUT_COLS = 512
POOL_HALO = 16
CONV_HALO = SUBLANES

SLAB_COLS = 3 * D_MODEL + D_MODEL + D_MODEL + POOL_W + 3 * CONV_W

HBM = pl.BlockSpec(memory_space=pl.ANY)


def _params(**kw):
    return pltpu.CompilerParams(vmem_limit_bytes=VMEM_LIMIT_BYTES, **kw)


def _sigmoid(v):
    return 1.0 / (1.0 + jnp.exp(-v))


def _dot(a, b):
    return jnp.dot(a, b, preferred_element_type=F32)


def _dot_tn(a, b):
    return lax.dot_general(a, b, (((0,), (0,)), ((), ())), preferred_element_type=F32)


def _dot_nt(a, b):
    return lax.dot_general(a, b, (((1,), (1,)), ((), ())), preferred_element_type=F32)


def _rows_from_before(v, k):
    return pltpu.roll(v, k, 0)


def _rows_from_after(v, k):
    return pltpu.roll(v, v.shape[0] - k, 0)


def _window_counts(t0, rows):
    return (lax.broadcasted_iota(jnp.int32, (rows, 1), 0) + (t0 + 1)).astype(F32)


def _split_proj(p32):
    cw = CONV_W
    return (p32[:, 0 * cw:1 * cw], p32[:, 1 * cw:2 * cw], p32[:, 2 * cw:3 * cw], p32[:, 3 * cw:4 * cw],
            p32[:, 4 * cw:4 * cw + POOL_W], p32[:, 4 * cw + POOL_W:])


def _layer_spec(shape, layer):
    nd = len(shape)
    return pl.BlockSpec((None,) + tuple(shape[1:]), lambda i, _l=layer, _n=nd: (_l,) + (0,) * (_n - 1))


def _whole_spec(shape):
    return pl.BlockSpec(tuple(shape), lambda i, _n=len(shape): (0,) * _n, pipeline_mode=pl.Buffered(1))


def _mesh_position():
    return lax.axis_index("x"), lax.axis_index("y"), lax.axis_index("c")


def _block_id(x, y, c):
    return 4 * x + 2 * y + c


def _other_chips(x, y):
    return [(x ^ 1, y), (x, y ^ 1), (x ^ 1, y ^ 1)]


def _col_block(ref, blk):
    return ref.at[:, pl.ds(pl.multiple_of(blk * W_IN_SHARD, LANES), W_IN_SHARD)]


def _row_block(rows):
    def block(ref, blk):
        return ref.at[pl.ds(pl.multiple_of(blk * rows, rows), rows), :]
    return block


_BLOCK_OF = (_col_block, _row_block(W_OUT_SHARD), _row_block(POOL_SHARD))
_BLOCK_SHAPES = ((D_MODEL, W_IN_SHARD), (W_OUT_SHARD, D_MODEL), (POOL_SHARD, GROUP_D))


class _Exchange:
    def __init__(self, inputs, out_shapes, aliases, sem_shapes, make):
        self.inputs, self.out_shapes, self.aliases, self.sem_shapes, self.make = (
            list(inputs), list(out_shapes), dict(aliases), list(sem_shapes), make)


def _run_exchange(exchange, name):
    n_in, n_out = len(exchange.inputs), len(exchange.out_shapes)

    def body(*refs):
        start, finish = exchange.make(refs[:n_in], refs[n_in:n_in + n_out], refs[n_in + n_out:])
        start()
        finish()

    return pl.pallas_call(
        body, name=name, in_specs=[HBM] * n_in, out_specs=[HBM] * n_out, out_shape=exchange.out_shapes,
        scratch_shapes=exchange.sem_shapes, input_output_aliases=exchange.aliases, compiler_params=_params(),
    )(*exchange.inputs)


_SEM = pl.BlockSpec(memory_space=pltpu.SEMAPHORE)
_DATAFLOW = pltpu.SideEffectType.DATAFLOW_SIDE_EFFECTING


def _start_exchange(exchange, name, after=()):
    n_in, n_out, n_sem = len(exchange.inputs), len(exchange.out_shapes), len(exchange.sem_shapes)
    sources = [i for i in range(n_in) if i not in exchange.aliases]
    aliases = {i: n_sem + k for k, i in enumerate(sources)}
    aliases.update({i: n_sem + len(sources) + o for i, o in exchange.aliases.items()})

    def body(*refs):
        in_refs = refs[:n_in]
        outs = refs[n_in + len(after):]
        sems = outs[:n_sem]
        out_refs = outs[n_sem + len(sources):n_sem + len(sources) + n_out]
        exchange.make(in_refs, out_refs, sems)[0]()
        refs[-1][...] = jnp.zeros_like(refs[-1])

    outs = pl.pallas_call(
        body, name=name, in_specs=[HBM] * (n_in + len(after)),
        out_specs=[_SEM] * n_sem + [HBM] * (len(sources) + n_out) + [pl.BlockSpec(memory_space=pltpu.VMEM)],
        out_shape=(exchange.sem_shapes + [pltpu.HBM(exchange.inputs[i].shape, exchange.inputs[i].dtype) for i in sources]
                   + [pltpu.HBM(s.shape, s.dtype) for s in exchange.out_shapes]
                   + [jax.ShapeDtypeStruct((SUBLANES, LANES), F32)]),
        input_output_aliases=aliases, compiler_params=_params(has_side_effects=_DATAFLOW),
    )(*exchange.inputs, *after)
    return outs[:n_sem], outs[n_sem:n_sem + len(sources)], outs[n_sem + len(sources):-1], outs[-1]


def _finish_exchange(exchange, name, sems, sources, landing, after):
    n_src, n_out, n_sem = len(sources), len(landing), len(sems)
    n_in = len(exchange.inputs)
    source_at = [i for i in range(n_in) if i not in exchange.aliases]

    def body(*refs):
        src_refs, out_refs = refs[:n_src], refs[n_src:n_src + n_out]
        sem_refs = refs[n_src + n_out:n_src + n_out + n_sem]
        in_refs = [None] * n_in
        for k, i in enumerate(source_at):
            in_refs[i] = src_refs[k]
        for i, o in exchange.aliases.items():
            in_refs[i] = out_refs[o]
        exchange.make(in_refs, out_refs, sem_refs)[1]()

    arrays = list(sources) + list(landing)
    outs = pl.pallas_call(
        body, name=name, in_specs=[HBM] * len(arrays) + [_SEM] * n_sem + [HBM] * len(after),
        out_specs=[HBM] * len(arrays), out_shape=[pltpu.HBM(a.shape, a.dtype) for a in arrays],
        input_output_aliases={i: i for i in range(len(arrays))}, compiler_params=_params(has_side_effects=_DATAFLOW),
    )(*arrays, *sems, *after)
    return outs[:n_src], outs[n_src:]


N_GATHERED = 2
FIRST_COPIES = 1 + N_OTHER_CHIPS
_GATHERED_SHAPES = ((D_MODEL, IN_COLS), (D_MODEL, D_MODEL))
ALL_OTHER_CHIPS, NEIGHBOUR_CHIPS, DIAGONAL_CHIP = (0, 1, 2), (0, 1), (2,)


def _first_sem(layer, a, k):
    return (layer * N_GATHERED + a) * FIRST_COPIES + k


def _first_arrival(layer, a, j):
    return _first_sem(layer, a, 1 + j)


def _gather_copy(window_of, full_ref, blk, send_sem, recv_sem, to, src=None):
    window = window_of(full_ref, blk)
    return pltpu.make_async_remote_copy(
        src_ref=window if src is None else src, dst_ref=window, send_sem=send_sem, recv_sem=recv_sem,
        device_id=to, device_id_type=MESH)


def _first_copies(layer, shard_refs, full_refs, send_sems, recv_sems, local_sems, relayed):
    x, y, c = _mesh_position()
    me = _block_id(x, y, c)
    chips = [_other_chips(x, y)[j] for j in (NEIGHBOUR_CHIPS if relayed else ALL_OTHER_CHIPS)]
    own, remote = [], []
    for a in range(N_GATHERED):
        shard = shard_refs[a].at[layer]
        own.append(pltpu.make_async_copy(
            shard, _BLOCK_OF[a](full_refs[a], me), local_sems.at[layer * N_GATHERED + a]))
        targets = [(x, y, 1 - c)] + [(*chip, c) for chip in chips]
        remote += [_gather_copy(_BLOCK_OF[a], full_refs[a], me, send_sems.at[_first_sem(layer, a, k)],
                                recv_sems.at[_first_sem(layer, a, k)], to, src=shard)
                   for k, to in enumerate(targets)]
    return own, remote


def _gather_weights_start(first_layer, win_shards, wout_shards, after, relayed=False):
    n_layers = win_shards.shape[0]
    n_first = n_layers * N_GATHERED * FIRST_COPIES
    sem_shapes = [pltpu.SemaphoreType.DMA((n_first,)), pltpu.SemaphoreType.DMA((n_first,)),
                  pltpu.SemaphoreType.DMA((n_layers * N_GATHERED,))]
    shards = [win_shards, wout_shards]

    def body(win_sh, wout_sh, *rest):
        send_sems, recv_sems, local_sems, win_thru, wout_thru, *landing = rest[len(after):]
        for layer in range(n_layers):
            own, remote = _first_copies(layer, (win_sh, wout_sh), landing[N_GATHERED * layer:N_GATHERED * (layer + 1)],
                                        send_sems, recv_sems, local_sems, relayed)
            for cp in own + remote:
                cp.start()

    outs = pl.pallas_call(
        body, name=f"all_gather_weights_start_{first_layer}", in_specs=[HBM] * (2 + len(after)),
        out_specs=[_SEM] * 3 + [HBM] * (2 + n_layers * N_GATHERED),
        out_shape=(sem_shapes + [pltpu.HBM(s.shape, s.dtype) for s in shards]
                   + [pltpu.HBM(s, BF16) for _ in range(n_layers) for s in _GATHERED_SHAPES]),
        input_output_aliases={0: 3, 1: 4}, compiler_params=_params(has_side_effects=_DATAFLOW),
    )(*shards, *after)
    landing = outs[5:]
    return outs[:3], outs[3:5], [landing[N_GATHERED * l:N_GATHERED * (l + 1)] for l in range(n_layers)]


def _passed_on_copies(full_refs, send_sems, recv_sems, core_of_block, chips):
    x, y, c = _mesh_position()
    return [_gather_copy(_BLOCK_OF[a], full_refs[a], _block_id(*_other_chips(x, y)[j], core_of_block),
                         send_sems.at[a * N_OTHER_CHIPS + j], recv_sems.at[a * N_OTHER_CHIPS + j], (x, y, 1 - c))
            for a in range(N_GATHERED) for j in chips]


def _relayed_copies(full_refs, send_sems, recv_sems):
    x, y, c = _mesh_position()
    source, to = (x ^ (1 - c), y ^ c), (x ^ c, y ^ (1 - c))
    return [_gather_copy(_BLOCK_OF[a], full_refs[a], _block_id(*source, c), send_sems.at[a], recv_sems.at[a], (*to, c))
            for a in range(N_GATHERED)]


def _gather_weights_pass_on(name, chips, arrival_sems, arrival_sem_of, landing, after, relay=False):
    n = N_GATHERED * N_OTHER_CHIPS
    sem_shapes = [pltpu.SemaphoreType.DMA((n,))] * 2 + [pltpu.SemaphoreType.DMA((N_GATHERED,))] * (2 if relay else 0)

    def body(win_ref, wout_ref, arrivals, *rest):
        sems = rest[len(after):len(after) + len(sem_shapes)]
        x, y, c = _mesh_position()
        full_refs = (win_ref, wout_ref)
        passed = _passed_on_copies(full_refs, sems[0], sems[1], c, chips)
        relayed = _relayed_copies(full_refs, sems[2], sems[3]) if relay else []
        for a in range(N_GATHERED):
            for j in chips:
                sem = arrivals.at[arrival_sem_of(a, j)]
                _gather_copy(_BLOCK_OF[a], full_refs[a], _block_id(*_other_chips(x, y)[j], c), sem, sem,
                             (x, y, c)).wait_recv()
            for cp in relayed[a:a + 1] + passed[a * len(chips):(a + 1) * len(chips)]:
                cp.start()

    outs = pl.pallas_call(
        body, name=name, in_specs=[HBM] * N_GATHERED + [_SEM] + [HBM] * len(after),
        out_specs=[_SEM] * len(sem_shapes) + [HBM] * N_GATHERED,
        out_shape=sem_shapes + [pltpu.HBM(a.shape, a.dtype) for a in landing],
        input_output_aliases={a: len(sem_shapes) + a for a in range(N_GATHERED)},
        compiler_params=_params(has_side_effects=_DATAFLOW),
    )(*landing, arrival_sems, *after)
    return outs[:len(sem_shapes)], outs[len(sem_shapes):]


def _gather_weights_finish(layer, index, first_sems, passed, shards, landing, relay_send_sems=None, after=()):
    relayed = relay_send_sems is not None
    passed_sems = [sem for (send, recv), _ in passed for sem in (send, recv)] + ([relay_send_sems] if relayed else [])

    def body(win_ref, wout_ref, first_send, first_recv, local_sems, *rest):
        sems, (win_sh, wout_sh) = rest[:len(passed_sems)], rest[len(passed_sems):len(passed_sems) + 2]
        x, y, c = _mesh_position()
        full_refs = (win_ref, wout_ref)
        own, sent = _first_copies(index, (win_sh, wout_sh), full_refs, first_send, first_recv, local_sems, relayed)
        for a in range(N_GATHERED):
            sem = _first_sem(index, a, 0)
            _gather_copy(_BLOCK_OF[a], full_refs[a], _block_id(x, y, 1 - c), first_recv.at[sem], first_recv.at[sem],
                         (x, y, c)).wait_recv()
        for p, (_, chips) in enumerate(passed):
            for cp in _passed_on_copies(full_refs, sems[2 * p], sems[2 * p + 1], 1 - c, chips):
                cp.wait_recv()
            sent += _passed_on_copies(full_refs, sems[2 * p], sems[2 * p + 1], c, chips)
        if relayed:
            sent += _relayed_copies(full_refs, sems[-1], sems[-1])
        for cp in sent:
            cp.wait_send()
        for cp in own:
            cp.wait()

    return pl.pallas_call(
        body, name=f"all_gather_weights_finish_{layer}",
        in_specs=[HBM] * N_GATHERED + [_SEM] * (3 + len(passed_sems)) + [HBM] * (2 + len(after)),
        out_specs=[HBM] * N_GATHERED, out_shape=[pltpu.HBM(a.shape, a.dtype) for a in landing],
        input_output_aliases={a: a for a in range(N_GATHERED)}, compiler_params=_params(has_side_effects=_DATAFLOW),
    )(*landing, *first_sems, *passed_sems, *shards, *after)


ALL_KINDS = (0, 1, 2)


def _sibling_exchange(grads, kinds):
    n_arr = len(grads)

    def make(in_refs, out_refs, sems):
        send_sems, recv_sems = sems
        x, y, c = _mesh_position()
        copies = [pltpu.make_async_remote_copy(
            src_ref=_BLOCK_OF[kinds[a]](in_refs[a], 2 * q + (1 - c)), dst_ref=out_refs[a].at[q],
            send_sem=send_sems.at[a * N_CHIP + q], recv_sem=recv_sems.at[a * N_CHIP + q],
            device_id=(x, y, 1 - c), device_id_type=MESH)
            for a in range(n_arr) for q in range(N_CHIP)]

        def start():
            for cp in copies:
                cp.start()

        def finish():
            for cp in copies:
                cp.wait_recv()
            for cp in copies:
                cp.wait_send()

        return start, finish

    return _Exchange(
        grads, [jax.ShapeDtypeStruct((N_CHIP,) + _BLOCK_SHAPES[k], BF16) for k in kinds], {},
        [pltpu.SemaphoreType.DMA((n_arr * N_CHIP,)), pltpu.SemaphoreType.DMA((n_arr * N_CHIP,))], make)


def _chips_exchange(layer, partials, received, kinds):
    n_arr = len(partials)

    def make(in_refs, out_refs, sems):
        send_sems, recv_sems = sems
        x, y, c = _mesh_position()
        copies = [pltpu.make_async_remote_copy(
            src_ref=in_refs[a].at[2 * qx + qy, layer], dst_ref=out_refs[a].at[j, layer],
            send_sem=send_sems.at[a * N_OTHER_CHIPS + j], recv_sem=recv_sems.at[a * N_OTHER_CHIPS + j],
            device_id=(qx, qy, c), device_id_type=MESH)
            for a in range(n_arr) for j, (qx, qy) in enumerate(_other_chips(x, y))]

        def start():
            for cp in copies:
                cp.start()

        def finish():
            for cp in copies:
                cp.wait_recv()
            for cp in copies:
                cp.wait_send()

        return start, finish

    inputs = list(partials)
    aliases = {}
    if received is not None:
        inputs += list(received)
        aliases = {n_arr + a: a for a in range(n_arr)}
    return _Exchange(
        inputs, [jax.ShapeDtypeStruct((N_OTHER_CHIPS, DEPTH) + _BLOCK_SHAPES[k], BF16) for k in kinds], aliases,
        [pltpu.SemaphoreType.DMA((n_arr * N_OTHER_CHIPS,)), pltpu.SemaphoreType.DMA((n_arr * N_OTHER_CHIPS,))], make)


def _all_gather_exchange(v, axis=0):
    def make(in_refs, out_refs, sems):
        send_sems, recv_sems, local_sem = sems
        x, y, c = _mesh_position()
        me = _block_id(x, y, c)
        block = lambda blk: out_refs[0].at[(slice(None),) * axis + (blk,)]
        own = pltpu.make_async_copy(in_refs[0], block(me), local_sem.at[0])
        sends, arrivals = [], []
        for k in range(1, N_DEV):
            px, py, pc = x ^ ((k >> 2) & 1), y ^ ((k >> 1) & 1), c ^ (k & 1)
            sends.append(pltpu.make_async_remote_copy(
                src_ref=in_refs[0], dst_ref=block(me), send_sem=send_sems.at[k - 1],
                recv_sem=recv_sems.at[k - 1], device_id=(px, py, pc), device_id_type=MESH))
            arrivals.append(pltpu.make_async_remote_copy(
                src_ref=in_refs[0], dst_ref=block(_block_id(px, py, pc)), send_sem=send_sems.at[k - 1],
                recv_sem=recv_sems.at[k - 1], device_id=(x, y, c), device_id_type=MESH))

        def start():
            for cp in [own] + sends:
                cp.start()

        def finish():
            for cp in arrivals:
                cp.wait_recv()
            for cp in sends:
                cp.wait_send()
            own.wait()

        return start, finish

    return _Exchange(
        [v], [jax.ShapeDtypeStruct(v.shape[:axis] + (N_DEV,) + v.shape[axis:], v.dtype)], {},
        [pltpu.SemaphoreType.DMA((N_DEV - 1,)), pltpu.SemaphoreType.DMA((N_DEV - 1,)),
         pltpu.SemaphoreType.DMA((1,))], make)


def _host(exchange, args, in_specs, out_shape, out_specs, scratch):
    n_own = (len(args), len(out_shape), len(scratch))
    if exchange is None:
        return {}, lambda refs: (refs, None)
    n_ex = (len(exchange.inputs), len(exchange.out_shapes), len(exchange.sem_shapes))
    aliases = {n_own[0] + i: n_own[1] + o for i, o in exchange.aliases.items()}
    args += exchange.inputs
    in_specs += [HBM] * n_ex[0]
    out_shape += exchange.out_shapes
    out_specs += [HBM] * n_ex[1]
    scratch += exchange.sem_shapes

    def split(refs):
        own, theirs, at = [], [], 0
        for mine, ex in zip(n_own, n_ex):
            own += refs[at:at + mine]
            theirs.append(refs[at + mine:at + mine + ex])
            at += mine + ex
        return own, exchange.make(*theirs)

    return aliases, split


def _all_gather_small(v, name, after=()):
    vmem = pl.BlockSpec(memory_space=pltpu.VMEM)

    def body(v_ref, *rest):
        out_ref, send_sems, recv_sems = rest[len(after):]
        x, y, c = _mesh_position()
        me = _block_id(x, y, c)
        out_ref[me] = v_ref[...]
        sends = []
        for k in range(1, N_DEV):
            px, py, pc = x ^ ((k >> 2) & 1), y ^ ((k >> 1) & 1), c ^ (k & 1)
            send = pltpu.make_async_remote_copy(
                src_ref=v_ref, dst_ref=out_ref.at[me], send_sem=send_sems.at[k - 1], recv_sem=recv_sems.at[k - 1],
                device_id=(px, py, pc), device_id_type=MESH)
            send.start()
            sends.append((send, _block_id(px, py, pc)))
        for k, (send, peer) in enumerate(sends):
            pltpu.make_async_remote_copy(
                src_ref=v_ref, dst_ref=out_ref.at[peer], send_sem=send_sems.at[k], recv_sem=recv_sems.at[k],
                device_id=(x, y, c), device_id_type=MESH).wait_recv()
        for send, _ in sends:
            send.wait_send()

    return pl.pallas_call(
        body, name=name, in_specs=[vmem] + [HBM] * len(after), out_specs=vmem,
        out_shape=jax.ShapeDtypeStruct((N_DEV,) + v.shape, v.dtype),
        scratch_shapes=[pltpu.SemaphoreType.DMA((N_DEV - 1,)), pltpu.SemaphoreType.DMA((N_DEV - 1,))],
        compiler_params=_params(),
    )(v, *after)


def _forward_layer(layer, x, vec, cps, wpool, win, wout, target=None):
    t_len = x.shape[0]
    n_tiles = t_len // ROW_TILE
    row = lambda cols: pl.BlockSpec((ROW_TILE, cols), lambda i: (i, 0))
    widths = (D_MODEL, 2 * CONV_W, 3 * CONV_W, 4 * POOL_W)
    head = target is not None

    def body(x_ref, vec_ref, cps_ref, wpool_ref, win_ref, wout_ref, *rest):
        target_ref = rest[0] if head else None
        xo_ref, y_ref, h_ref, ycat_ref, uc_ref, fa_ref, fp_ref = rest[head:head + 7]
        loss_ref = rest[head + 7] if head else None
        zc_ref, pc_ref = rest[-2:]
        i = pl.program_id(0)

        @pl.when(i == 0)
        def _():
            zc_ref[...] = jnp.zeros_like(zc_ref)
            pc_ref[...] = jnp.zeros_like(pc_ref)
            if head:
                loss_ref[...] = jnp.zeros_like(loss_ref)

        x_t = x_ref[...]
        shift, scale, gate = vec_ref[0:1, :], vec_ref[1:2, :], vec_ref[2:3, :]
        g_pre, g_post = vec_ref[3:4, :], vec_ref[4:5, :]
        w0, w1, w2, ps = cps_ref[0:1, :], cps_ref[1:2, :], cps_ref[2:3, :], cps_ref[3:4, :]
        rx = lax.rsqrt(jnp.mean(x_t * x_t, axis=-1, keepdims=True) + NORM_EPS)
        h = (x_t * rx) * g_pre * (1.0 + scale) + shift
        h_ref[...] = h.astype(BF16)
        proj = _dot(h.astype(BF16), win_ref[...])
        u_a, b_a, c_a, g_a, u_p, g_p = _split_proj(proj)
        uc_ref[...] = jnp.concatenate([u_a, c_a], axis=1).astype(BF16)

        z = c_a * u_a
        zcat = jnp.concatenate([zc_ref[...], z], axis=0)
        zc_ref[...] = z[ROW_TILE - CONV_HALO:]
        conv = (w0 * _rows_from_before(zcat, 2)[CONV_HALO:] + w1 * _rows_from_before(zcat, 1)[CONV_HALO:] + w2 * z)
        sig_a = _sigmoid(g_a)
        silu_a = g_a * sig_a
        b_conv = b_a * conv
        y_a = b_conv * silu_a
        fa_ref[...] = jnp.concatenate(
            [silu_a * conv, silu_a * b_a, b_conv * (sig_a + silu_a * (1.0 - sig_a))], axis=1).astype(BF16)

        pcat = jnp.concatenate([pc_ref[...], u_p], axis=0)
        pc_ref[...] = u_p[ROW_TILE - POOL_HALO:]
        counts = _window_counts(i * ROW_TILE, ROW_TILE)
        pooled, mixed = [], []
        for g, w in enumerate(POOL_WINDOWS):
            cols = slice(g * GROUP_D, (g + 1) * GROUP_D)
            s = pcat[:, cols]
            step = 1
            while step < w:
                s = s + _rows_from_before(s, step)
                step *= 2
            pooled_g = (s[POOL_HALO:] * (1.0 / jnp.minimum(counts, float(w))) - u_p[:, cols]).astype(BF16)
            pooled.append(pooled_g)
            mixed.append(_dot(pooled_g, wpool_ref[g]))
        mixed = jnp.concatenate(mixed, axis=1)
        sig_p = _sigmoid(g_p)
        silu_p = g_p * sig_p
        mixed_ps = mixed * ps
        y_p = mixed_ps * silu_p
        fp_ref[...] = jnp.concatenate(
            [(ps * silu_p).astype(BF16), (mixed_ps * (sig_p + silu_p * (1.0 - sig_p))).astype(BF16),
             (silu_p * mixed).astype(BF16)] + pooled, axis=1)

        ycat = jnp.concatenate([y_a, y_p], axis=1)
        ycat_ref[...] = ycat.astype(BF16)
        y_b = _dot(ycat.astype(BF16), wout_ref[...]).astype(BF16)
        y_ref[...] = y_b
        y_t = y_b.astype(F32)
        ry = lax.rsqrt(jnp.mean(y_t * y_t, axis=-1, keepdims=True) + NORM_EPS)
        x_next = x_t + gate * (y_t * ry * g_post)
        if head:
            err = x_next - target_ref[...]
            xo_ref[...] = err * (1.0 / D_MODEL)
            loss_ref[...] += jnp.sum(err * err) * (0.5 / D_MODEL)
        else:
            xo_ref[...] = x_next

    tile = (SUBLANES, LANES)
    return pl.pallas_call(
        body, name=f"forward_layer_{layer}", grid=(n_tiles,),
        in_specs=[row(D_MODEL), _layer_spec(vec.shape, layer), _layer_spec(cps.shape, layer),
                  _layer_spec(wpool.shape, layer), _whole_spec(win.shape), _whole_spec(wout.shape)]
        + [row(D_MODEL)] * head,
        out_specs=[row(D_MODEL), row(D_MODEL), row(D_MODEL), row(D_MODEL)] + [row(w) for w in widths[1:]]
        + [_whole_spec(tile)] * head,
        out_shape=[jax.ShapeDtypeStruct((t_len, D_MODEL), F32), jax.ShapeDtypeStruct((t_len, D_MODEL), BF16),
                   jax.ShapeDtypeStruct((t_len, D_MODEL), BF16), jax.ShapeDtypeStruct((t_len, D_MODEL), BF16)]
        + [jax.ShapeDtypeStruct((t_len, w), BF16) for w in widths[1:]] + [jax.ShapeDtypeStruct(tile, F32)] * head,
        scratch_shapes=[pltpu.VMEM((CONV_HALO, CONV_W), F32), pltpu.VMEM((POOL_HALO, POOL_W), F32)],
        compiler_params=_params(dimension_semantics=("arbitrary",)),
    )(x, vec, cps, wpool, win, wout, *([target] * head))


def _backward_layer(layer, dxo, y, x, uc, fa, fp, vec, cps, wpool, wout, win, after):
    t_len = dxo.shape[0]
    n_tiles = t_len // BWD_TILE
    halo_per_tile = BWD_TILE // POOL_HALO
    rev = lambda cols: pl.BlockSpec((BWD_TILE, cols), lambda i: (n_tiles - 1 - i, 0))
    halo_spec = pl.BlockSpec(
        (POOL_HALO, 2 * CONV_W), lambda i: (jnp.maximum((n_tiles - 1 - i) * halo_per_tile - 1, 0), 0))
    gwpool_shape = (len(POOL_WINDOWS), GROUP_D, GROUP_D)

    def body(dxo_ref, y_ref, x_ref, uc_ref, uch_ref, fa_ref, fp_ref, vec_ref, cps_ref, wpool_ref, wout_ref, win_ref,
             *rest):
        dx_ref, dproj_ref, dy_ref, gwpool_ref, dcps_ref, dvec_ref, gwpool_acc, dcc_ref, qc_ref = rest[len(after):]
        i = pl.program_id(0)
        tile = n_tiles - 1 - i

        @pl.when(i == 0)
        def _():
            gwpool_acc[...] = jnp.zeros_like(gwpool_acc)
            dcps_ref[...] = jnp.zeros_like(dcps_ref)
            dvec_ref[...] = jnp.zeros_like(dvec_ref)
            dcc_ref[...] = jnp.zeros_like(dcc_ref)
            qc_ref[...] = jnp.zeros_like(qc_ref)

        shift, scale, gate = vec_ref[0:1, :], vec_ref[1:2, :], vec_ref[2:3, :]
        g_pre, g_post = vec_ref[3:4, :], vec_ref[4:5, :]
        w0, w1, w2 = cps_ref[0:1, :], cps_ref[1:2, :], cps_ref[2:3, :]

        dxo_t = dxo_ref[...]
        y_t = y_ref[...].astype(F32)
        ry = lax.rsqrt(jnp.mean(y_t * y_t, axis=-1, keepdims=True) + NORM_EPS)
        yh = y_t * ry
        dvec_ref[2:3, :] += jnp.sum(dxo_t * yh, axis=0, keepdims=True)
        dyh = dxo_t * (gate * g_post)
        dy_b = (ry * (dyh - yh * jnp.mean(dyh * yh, axis=-1, keepdims=True))).astype(BF16)
        dy_ref[...] = dy_b
        dycat = _dot_nt(dy_b, wout_ref[...])
        dy_a, dy_p = dycat[:, :CONV_W], dycat[:, CONV_W:]

        fa_t = fa_ref[...].astype(F32)
        db_a = dy_a * fa_t[:, :CONV_W]
        dconv = dy_a * fa_t[:, CONV_W:2 * CONV_W]
        dg_a = dy_a * fa_t[:, 2 * CONV_W:]
        uc_t = uc_ref[...].astype(F32)
        u_a, c_a = uc_t[:, :CONV_W], uc_t[:, CONV_W:]
        halo = jnp.where(tile > 0, uch_ref[...].astype(F32), 0.0)[POOL_HALO - CONV_HALO:]
        z = c_a * u_a
        zcat = jnp.concatenate([halo[:, CONV_W:] * halo[:, :CONV_W], z], axis=0)
        z1 = _rows_from_before(zcat, 1)[CONV_HALO:]
        z2 = _rows_from_before(zcat, 2)[CONV_HALO:]
        dccat = jnp.concatenate([dconv, dcc_ref[...]], axis=0)
        dc1 = _rows_from_after(dccat, 1)[:BWD_TILE]
        dc2 = _rows_from_after(dccat, 2)[:BWD_TILE]
        dz = w2 * dconv + w1 * dc1 + w0 * dc2
        dcc_ref[...] = dconv[:CONV_HALO]
        dcps_ref[0:1, :] += jnp.sum(dconv * z2, axis=0, keepdims=True)
        dcps_ref[1:2, :] += jnp.sum(dconv * z1, axis=0, keepdims=True)
        dcps_ref[2:3, :] += jnp.sum(dconv * z, axis=0, keepdims=True)
        du_a = dz * c_a
        dc_a = dz * u_a

        dmixed = (dy_p * fp_ref[:, :POOL_W].astype(F32)).astype(BF16)
        dg_p = dy_p * fp_ref[:, POOL_W:2 * POOL_W].astype(F32)
        dcps_ref[3:4, :] += jnp.sum(dy_p * fp_ref[:, 2 * POOL_W:3 * POOL_W].astype(F32), axis=0, keepdims=True)
        counts = _window_counts(tile * BWD_TILE, BWD_TILE)
        du_p, q_head = [], []
        for g, w in enumerate(POOL_WINDOWS):
            cols = slice(g * GROUP_D, (g + 1) * GROUP_D)
            dm_g = dmixed[:, cols]
            dpooled_g = _dot_nt(dm_g, wpool_ref[g])
            gwpool_acc[g] += _dot_tn(fp_ref[:, 3 * POOL_W + g * GROUP_D:3 * POOL_W + (g + 1) * GROUP_D], dm_g)
            q_g = dpooled_g * (1.0 / jnp.minimum(counts, float(w)))
            q_head.append(q_g[:POOL_HALO])
            s = jnp.concatenate([q_g, qc_ref[:, cols]], axis=0)
            step = 1
            while step < w:
                s = s + _rows_from_after(s, step)
                step *= 2
            du_p.append(s[:BWD_TILE] - dpooled_g)
        qc_ref[...] = jnp.concatenate(q_head, axis=1)
        dproj_b = jnp.concatenate([du_a, db_a, dc_a, dg_a] + du_p + [dg_p], axis=1).astype(BF16)
        dproj_ref[...] = dproj_b

        x_t = x_ref[...]
        rx = lax.rsqrt(jnp.mean(x_t * x_t, axis=-1, keepdims=True) + NORM_EPS)
        xn = x_t * rx
        mod_scale = 1.0 + scale
        dh = _dot_nt(dproj_b, win_ref[...])
        dvec_ref[0:1, :] += jnp.sum(dh, axis=0, keepdims=True)
        dvec_ref[1:2, :] += jnp.sum(dh * xn, axis=0, keepdims=True)
        dxn = dh * (g_pre * mod_scale)
        dx_ref[...] = dxo_t + rx * (dxn - xn * jnp.mean(dxn * xn, axis=-1, keepdims=True))

        @pl.when(i == n_tiles - 1)
        def _():
            gwpool_ref[...] = gwpool_acc[...].astype(BF16)
            sum_dh_xn, sum_dxo_yh = dvec_ref[1:2, :], dvec_ref[2:3, :]
            dvec_ref[1:2, :] = sum_dh_xn * g_pre
            dvec_ref[3:4, :] = sum_dh_xn * mod_scale
            dvec_ref[2:3, :] = sum_dxo_yh * g_post
            dvec_ref[4:5, :] = sum_dxo_yh * gate

    return pl.pallas_call(
        body, name=f"backward_layer_{layer}", grid=(n_tiles,),
        in_specs=[rev(D_MODEL), rev(D_MODEL), rev(D_MODEL), rev(2 * CONV_W), halo_spec, rev(3 * CONV_W),
                  rev(4 * POOL_W), _layer_spec(vec.shape, layer), _layer_spec(cps.shape, layer),
                  _layer_spec(wpool.shape, layer), _whole_spec(wout.shape), _whole_spec(win.shape)]
        + [HBM] * len(after),
        out_specs=[rev(D_MODEL), rev(IN_COLS), rev(D_MODEL), _whole_spec(gwpool_shape),
                   _whole_spec((SUBLANES, CONV_W)), _whole_spec((SUBLANES, D_MODEL))],
        out_shape=[jax.ShapeDtypeStruct((t_len, D_MODEL), F32), jax.ShapeDtypeStruct((t_len, IN_COLS), BF16),
                   jax.ShapeDtypeStruct((t_len, D_MODEL), BF16), jax.ShapeDtypeStruct(gwpool_shape, BF16),
                   jax.ShapeDtypeStruct((SUBLANES, CONV_W), F32), jax.ShapeDtypeStruct((SUBLANES, D_MODEL), F32)],
        scratch_shapes=[pltpu.VMEM(gwpool_shape, F32), pltpu.VMEM((CONV_HALO, CONV_W), F32),
                        pltpu.VMEM((POOL_HALO, POOL_W), F32)],
        compiler_params=_params(dimension_semantics=("arbitrary",)),
    )(dxo, y, x, uc, uc, fa, fp, vec, cps, wpool, wout, win, *after)


def _weight_grads(layer, h, dproj, ycat, dy, exchange, after=()):
    t_len = dy.shape[0]
    n_in, n_out = IN_COLS // GWIN_COLS, D_MODEL // GWOUT_COLS
    args = [h, dproj, ycat, dy, *after]
    in_specs = [_whole_spec(h.shape),
                pl.BlockSpec((t_len, GWIN_COLS), lambda s: (0, jnp.minimum(s, n_in - 1))),
                _whole_spec(ycat.shape),
                pl.BlockSpec((t_len, GWOUT_COLS), lambda s: (0, jnp.maximum(s - n_in, 0)))] + [HBM] * len(after)
    out_shape = [jax.ShapeDtypeStruct((D_MODEL, IN_COLS), BF16), jax.ShapeDtypeStruct((D_MODEL, D_MODEL), BF16)]
    out_specs = [pl.BlockSpec((D_MODEL, GWIN_COLS), lambda s: (0, jnp.minimum(s, n_in - 1))),
                 pl.BlockSpec((D_MODEL, GWOUT_COLS), lambda s: (0, jnp.maximum(s - n_in, 0)))]
    scratch = []
    aliases, split = _host(exchange, args, in_specs, out_shape, out_specs, scratch)

    def body(*refs):
        (h_ref, dproj_ref, ycat_ref, dy_ref, *_, gwin_ref, gwout_ref), hosted = split(refs)
        s = pl.program_id(0)
        if hosted is not None:
            pl.when(s == 0)(hosted[0])

        @pl.when(s < n_in)
        def _():
            gwin_ref[...] = _dot_tn(h_ref[...], dproj_ref[...]).astype(BF16)

        @pl.when(s >= n_in)
        def _():
            gwout_ref[...] = _dot_tn(ycat_ref[...], dy_ref[...]).astype(BF16)

        if hosted is not None:
            pl.when(s == n_in + n_out - 1)(hosted[1])

    return pl.pallas_call(
        body, name=f"weight_grads_{layer}", grid=(n_in + n_out,), in_specs=in_specs, out_specs=out_specs,
        out_shape=out_shape, scratch_shapes=scratch, input_output_aliases=aliases,
        compiler_params=_params(dimension_semantics=("arbitrary",)),
    )(*args)


def _add_sibling_blocks(name, layer, grads, received, core, partials, kinds):
    n_arr = len(grads)

    def body(core_ref, *refs):
        mine, theirs, outs = refs[:n_arr], refs[n_arr:2 * n_arr], refs[-n_arr:]
        for a in range(n_arr):
            outs[a][...] = (mine[a][...].astype(F32) + theirs[a][...].astype(F32)).astype(BF16)

    own_of_kind = [
        pl.BlockSpec((D_MODEL, W_IN_SHARD), lambda q, core_ref: (0, 2 * q + core_ref[0])),
        pl.BlockSpec((W_OUT_SHARD, D_MODEL), lambda q, core_ref: (2 * q + core_ref[0], 0)),
        pl.BlockSpec((POOL_SHARD, GROUP_D), lambda q, core_ref: (2 * q + core_ref[0], 0)),
    ]
    shapes = [_BLOCK_SHAPES[k] for k in kinds]
    recv_specs = [pl.BlockSpec((None,) + s, lambda q, core_ref: (q, 0, 0)) for s in shapes]
    out_specs = [pl.BlockSpec((None, None) + s, lambda q, core_ref: (q, layer, 0, 0)) for s in shapes]
    args = [core, *grads, *received]
    in_specs = [own_of_kind[k] for k in kinds] + recv_specs
    aliases = {}
    if partials is not None:
        aliases = {len(args) + a: a for a in range(n_arr)}
        args += list(partials)
        in_specs += [HBM] * n_arr
    return pl.pallas_call(
        body, name=name,
        grid_spec=pltpu.PrefetchScalarGridSpec(
            num_scalar_prefetch=1, grid=(N_CHIP,), in_specs=in_specs, out_specs=out_specs),
        out_shape=[jax.ShapeDtypeStruct((N_CHIP, DEPTH) + s, BF16) for s in shapes],
        input_output_aliases=aliases,
        compiler_params=_params(dimension_semantics=("arbitrary",)),
    )(*args)


def _modulation_columns(c_all, w_ada, after):
    def body(c_ref, w_ref, *rest):
        cact_ref, out_ref = rest[len(after):]
        c_t = c_ref[...]
        c_act = c_t * _sigmoid(c_t)
        cact_ref[...] = c_act
        out_ref[...] = jnp.dot(c_act, w_ref[...], preferred_element_type=F32, precision=lax.Precision.HIGHEST)

    return pl.pallas_call(
        body, name="modulation_columns", grid=(DEPTH,),
        in_specs=[pl.BlockSpec((N_DEV, D_MODEL), lambda l: (0, 0)),
                  pl.BlockSpec((None, D_MODEL, W_IN_SHARD), lambda l: (l, 0, 0))] + [HBM] * len(after),
        out_specs=[pl.BlockSpec((N_DEV, D_MODEL), lambda l: (0, 0)),
                   pl.BlockSpec((N_DEV, W_IN_SHARD), lambda l: (0, l))],
        out_shape=[jax.ShapeDtypeStruct((N_DEV, D_MODEL), F32),
                   jax.ShapeDtypeStruct((N_DEV, DEPTH * W_IN_SHARD), F32)],
        compiler_params=_params(dimension_semantics=("arbitrary",)),
    )(c_all, w_ada, *after)


def _adamw(w, g, m, v):
    m_new = ADAM_B1 * m + (1.0 - ADAM_B1) * g
    v_new = ADAM_B2 * v + (1.0 - ADAM_B2) * (g * g)
    m_hat = m_new / (1.0 - ADAM_B1 ** ADAM_STEP)
    v_hat = v_new / (1.0 - ADAM_B2 ** ADAM_STEP)
    delta = -ADAM_LR * (m_hat / (jnp.sqrt(v_hat) + ADAM_EPS) + ADAM_WD * w)
    return delta, m_new, v_new


def _adamw_w_ada(w, m, v, c_act_t, slabs, block, after):
    def body(block_ref, w_ref, m_ref, v_ref, ct_ref, dm_ref, *rest):
        g_ref, d_ref, mo_ref, vo_ref = rest[len(after):]
        layer = pl.ds(pl.program_id(0), 1)
        g = ct_ref[:, 0:1] * dm_ref[0, layer, :]
        for b in range(1, N_DEV):
            g = g + ct_ref[:, b:b + 1] * dm_ref[b, layer, :]
        g_ref[...] = g
        d_ref[...], mo_ref[...], vo_ref[...] = _adamw(w_ref[...], g, m_ref[...], v_ref[...])

    big = pl.BlockSpec((None, D_MODEL, W_IN_SHARD), lambda l, block_ref: (l, 0, 0))
    return pl.pallas_call(
        body, name="adamw_w_ada",
        grid_spec=pltpu.PrefetchScalarGridSpec(
            num_scalar_prefetch=1, grid=(DEPTH,),
            in_specs=[big, big, big, pl.BlockSpec((D_MODEL, N_DEV), lambda l, block_ref: (0, 0)),
                      pl.BlockSpec((N_DEV, DEPTH, W_IN_SHARD), lambda l, block_ref: (0, 0, block_ref[0]))]
            + [HBM] * len(after),
            out_specs=[big] * 4),
        out_shape=[jax.ShapeDtypeStruct(w.shape, F32)] * 4,
        compiler_params=_params(dimension_semantics=("arbitrary",)),
    )(block, w, m, v, c_act_t, slabs, *after)


def _sum_chip_partials(own_ref, recv_ref):
    g = own_ref[...].astype(F32)
    for j in range(N_OTHER_CHIPS):
        g = g + recv_ref[j].astype(F32)
    return g


def _partial_specs(row_tile, cols, first_layer=0):
    own = pl.BlockSpec((None, None, row_tile, cols), lambda l, r, chip_ref: (chip_ref[0], first_layer + l, r, 0))
    recv = pl.BlockSpec((N_OTHER_CHIPS, None, row_tile, cols), lambda l, r, chip_ref: (0, first_layer + l, r, 0))
    return own, recv


def _adamw_reduced(name, w, m, v, partial, received, chip, row_tile, layers, continued, after=()):
    depth, rows, cols = w.shape
    first, stop = layers

    def body(chip_ref, w_ref, m_ref, v_ref, own_ref, recv_ref, *rest):
        g_ref, d_ref, mo_ref, vo_ref = rest[-4:]
        g = _sum_chip_partials(own_ref, recv_ref)
        g_ref[...] = g
        d_ref[...], mo_ref[...], vo_ref[...] = _adamw(w_ref[...], g, m_ref[...], v_ref[...])

    blk = pl.BlockSpec((None, row_tile, cols), lambda l, r, chip_ref: (first + l, r, 0))
    args = [chip, w, m, v, partial, received]
    in_specs = [blk, blk, blk, *_partial_specs(row_tile, cols, first)]
    aliases = {}
    if continued is not None:
        aliases = {len(args) + k: k for k in range(4)}
        args += list(continued)
        in_specs += [HBM] * 4
    args += after
    in_specs += [HBM] * len(after)
    return pl.pallas_call(
        body, name=name,
        grid_spec=pltpu.PrefetchScalarGridSpec(
            num_scalar_prefetch=1, grid=(stop - first, rows // row_tile), in_specs=in_specs, out_specs=[blk] * 4),
        out_shape=[jax.ShapeDtypeStruct(w.shape, F32)] * 4, input_output_aliases=aliases,
        compiler_params=_params(dimension_semantics=("arbitrary", "arbitrary")),
    )(*args)


def _reduce_w_pool(partial, received, chip):
    def body(chip_ref, own_ref, recv_ref, g_ref):
        g_ref[...] = _sum_chip_partials(own_ref, recv_ref)

    return pl.pallas_call(
        body, name="reduce_w_pool",
        grid_spec=pltpu.PrefetchScalarGridSpec(
            num_scalar_prefetch=1, grid=(DEPTH, 1), in_specs=list(_partial_specs(POOL_SHARD, GROUP_D)),
            out_specs=pl.BlockSpec((None, POOL_SHARD, GROUP_D), lambda l, r, chip_ref: (l, 0, 0))),
        out_shape=jax.ShapeDtypeStruct((DEPTH, POOL_SHARD, GROUP_D), F32),
        compiler_params=_params(dimension_semantics=("arbitrary", "arbitrary")),
    )(chip, partial, received)


def _adamw_small(name, params):
    n = len(params)

    def body(*refs):
        ins, outs = refs[:4 * n], refs[4 * n:]
        for p in range(n):
            w_ref, g_ref, m_ref, v_ref = ins[4 * p:4 * p + 4]
            d_ref, mo_ref, vo_ref, go_ref = outs[4 * p:4 * p + 4]
            d_ref[...], mo_ref[...], vo_ref[...] = _adamw(w_ref[...], g_ref[...], m_ref[...], v_ref[...])
            go_ref[...] = g_ref[...]

    vmem = pl.BlockSpec(memory_space=pltpu.VMEM)
    flat = [a for group in params for a in group]
    outs = pl.pallas_call(
        body, name=name, in_specs=[vmem] * len(flat), out_specs=[vmem] * len(flat),
        out_shape=[jax.ShapeDtypeStruct(a.shape, F32) for a in flat], compiler_params=_params(),
    )(*flat)
    return [tuple(outs[4 * p:4 * p + 4]) for p in range(n)]


def _sum_sources(slabs, columns, after):
    def body(s_ref, *rest):
        outs = rest[len(after):]
        acc = s_ref[0]
        for b in range(1, N_DEV):
            acc = acc + s_ref[b]
        for (start, stop), o_ref in zip(columns, outs):
            o_ref[...] = acc[:, start:stop]
        outs[-1][...] = acc[0:1, SLAB_COLS:SLAB_COLS + 1]

    vmem = pl.BlockSpec(memory_space=pltpu.VMEM)
    out_shape = [jax.ShapeDtypeStruct((slabs.shape[1], stop - start), F32) for start, stop in columns]
    out_shape.append(jax.ShapeDtypeStruct((1, 1), F32))
    return pl.pallas_call(
        body, name="sum_small_grads", in_specs=[vmem] + [HBM] * len(after), out_specs=[vmem] * len(out_shape),
        out_shape=out_shape, compiler_params=_params(),
    )(slabs, *after)


def _to_bf16(a, name, layers=None):
    first, stop = layers or (0, a.shape[0])

    def body(a_ref, o_ref):
        o_ref[...] = a_ref[...].astype(BF16)

    block = (None,) + a.shape[1:]
    return pl.pallas_call(
        body, name=name, grid=(stop - first,), in_specs=[pl.BlockSpec(block, lambda l: (first + l, 0, 0))],
        out_specs=pl.BlockSpec(block, lambda l: (l, 0, 0)),
        out_shape=jax.ShapeDtypeStruct((stop - first,) + a.shape[1:], BF16),
        compiler_params=_params(dimension_semantics=("arbitrary",)),
    )(a)


def kernel(x, c, w_ada, b_ada, g_pre, w_in, w_conv, w_pool, pool_scale, w_out, g_post, loss_target, m_w_ada, m_b_ada, m_g_pre, m_w_in, m_w_conv, m_w_pool, m_pool_scale, m_w_out, m_g_post, v_w_ada, v_b_ada, v_g_pre, v_w_in, v_w_conv, v_w_pool, v_pool_scale, v_w_out, v_g_post):
    mx, my, mc = _mesh_position()
    me = _block_id(mx, my, mc)
    chip = (2 * mx + my).astype(jnp.int32).reshape(1)
    core = mc.astype(jnp.int32).reshape(1)
    x0 = x[0]
    target = loss_target[0]
    conv_shard = w_conv.shape[-1]

    own_small = jnp.concatenate([c, w_conv.reshape(1, DEPTH * 3 * conv_shard)], axis=1)
    first_shards = [_to_bf16(w_in, "cast_w_in_0", (0, 1)), _to_bf16(w_out, "cast_w_out_0", (0, 1))]
    all_small = _all_gather_small(own_small, "all_gather_c_w_conv", first_shards)[:, 0, :]
    gathers = [_gather_weights_start(0, *first_shards, [all_small], relayed=True)]
    c_all = all_small[:, :D_MODEL]
    w_conv_full = all_small[:, D_MODEL:].reshape(N_DEV, DEPTH, 3, conv_shard).transpose(1, 2, 0, 3).reshape(
        DEPTH, 3, CONV_W)
    cps = jnp.concatenate([w_conv_full, pool_scale[:, None], jnp.zeros((DEPTH, 4, CONV_W), F32)], axis=1)

    c_act, pieces = _modulation_columns(c_all, w_ada, [gathers[0][1][0]])
    c_act_t = c_act.T
    upper = (1, DEPTH)
    upper_shards = [_to_bf16(w_in, "cast_w_in_1", upper), _to_bf16(w_out, "cast_w_out_1", upper)]
    wpool_b = _to_bf16(w_pool.reshape(DEPTH, POOL_ROWS, GROUP_D), "cast_w_pool").reshape(w_pool.shape)
    first_sems_0, _, (zones_0,) = gathers[0]
    neighbour_sems, zones_0 = _gather_weights_pass_on(
        "all_gather_weights_relay_0", NEIGHBOUR_CHIPS, first_sems_0[1], partial(_first_arrival, 0), zones_0,
        [pieces, *upper_shards, wpool_b, cps, c_act_t], relay=True)
    gather_mod = _all_gather_exchange(pieces)
    sems_m, pieces_thru, mod_landing, token_m = _start_exchange(gather_mod, "all_gather_modulation_start", [zones_0[0]])
    diagonal_sems, zones_0 = _gather_weights_pass_on(
        "all_gather_weights_pass_on_0", DIAGONAL_CHIP, neighbour_sems[3], lambda a, j: a, zones_0, [token_m])
    _, (mod_all,) = _finish_exchange(
        gather_mod, "all_gather_modulation_finish", sems_m, pieces_thru, mod_landing, [zones_0[0]])
    mod_mine = lax.dynamic_index_in_dim(mod_all, me, axis=1, keepdims=False)
    mod = mod_mine.reshape(N_DEV, DEPTH, W_IN_SHARD).transpose(1, 0, 2).reshape(DEPTH, 3 * D_MODEL) + b_ada
    zeros_d = jnp.zeros((DEPTH, 3, D_MODEL), F32)
    vec = jnp.concatenate([mod.reshape(DEPTH, 3, D_MODEL), g_pre[:, None], g_post[:, None], zeros_d], axis=1)

    gathers.append(_gather_weights_start(1, *upper_shards, [mod_all]))
    gathers = [(first_sems, shards, landing[k], k) for first_sems, shards, landing in gathers
               for k in range(len(landing))]

    xs, kept, wins, wouts = [x0], [], [], []
    for l in range(DEPTH):
        first_sems, shards, zones, index = gathers[l]
        if l == 0:
            passed = [(neighbour_sems[:2], NEIGHBOUR_CHIPS), (diagonal_sems, DIAGONAL_CHIP)]
            win, wout = _gather_weights_finish(
                l, index, first_sems, passed, shards, zones_0, neighbour_sems[2], [gathers[-1][1][0]])
        else:
            passed_sems, zones = _gather_weights_pass_on(
                f"all_gather_weights_pass_on_{l}", ALL_OTHER_CHIPS, first_sems[1], partial(_first_arrival, index),
                zones, [xs[-1]])
            win, wout = _gather_weights_finish(l, index, first_sems, [(passed_sems, ALL_OTHER_CHIPS)], shards, zones)
        x_next, *for_backward = _forward_layer(
            l, xs[-1], vec, cps, wpool_b, win, wout, target if l == DEPTH - 1 else None)
        xs.append(x_next)
        kept.append(for_backward[:6])
        wins.append(win)
        wouts.append(wout)
    dx, loss_tile = xs[DEPTH], for_backward[6]

    slab_rows = [None] * DEPTH
    partials = received = None
    in_flight = []

    def scatter(layer, grads, from_sibling, after):
        nonlocal partials, received
        partials = _add_sibling_blocks(
            f"grad_add_sibling_{layer}", layer, grads, from_sibling, core, partials, ALL_KINDS)
        chips = _chips_exchange(layer, partials, received, ALL_KINDS)
        sems, partials, received, token = _start_exchange(chips, f"grad_chips_start_{layer}", after)
        in_flight.append((chips, sems, layer))
        return token

    grads_above = None
    issued = []
    for l in reversed(range(DEPTH)):
        y, h, ycat, uc, fa, fp = kept[l]
        dx, dproj, dy, gwpool, dcps, dvec = _backward_layer(
            l, dx, y, xs[l], uc, fa, fp, vec, cps, wpool_b, wouts[l], wins[l], issued)
        slab_rows[l] = jnp.concatenate(
            [dvec[0], dvec[1], dvec[2], dvec[3], dvec[4], dcps[3], dcps[0], dcps[1], dcps[2],
             loss_tile[0] if l == 0 else jnp.zeros((LANES,), F32)])
        after = []
        if l == 0:
            gather_small = _all_gather_exchange(jnp.stack(slab_rows))
            sems_s, slab, slabs, token = _start_exchange(gather_small, "all_gather_small_grads_start")
            after = [token]
        hosted = _sibling_exchange(grads_above, ALL_KINDS) if grads_above is not None else None
        gwin, gwout, *from_sibling = _weight_grads(l, h, dproj, ycat, dy, hosted, after)
        if l == 0:
            _, (slabs,) = _finish_exchange(
                gather_small, "all_gather_small_grads_finish", sems_s, slab, slabs, [gwin])
        issued = [gwin]
        if grads_above is not None:
            issued = [scatter(l + 1, grads_above, from_sibling, [])]
        grads_above = [gwin, gwout, gwpool.reshape(POOL_ROWS, GROUP_D)]
        if l <= 1:
            from_sibling = _run_exchange(_sibling_exchange(grads_above, ALL_KINDS), f"grad_exchange_sibling_{l}")
            token = scatter(l, grads_above, from_sibling, [slabs] if l == 0 else [])
            issued = [token]
            grads_above = None
    grad_x = dx[None]
    chips_0, sems_0, _ = in_flight.pop()

    o = 3 * D_MODEL

    after = [token]
    for chips, sems, l in in_flight:
        partials, received = _finish_exchange(chips, f"grad_chips_finish_{l}", sems, partials, received, after)
        after = []
    upper = (1, DEPTH)
    w_in_upper = _adamw_reduced(
        "adamw_w_in_upper", w_in, m_w_in, v_w_in, partials[0], received[0], chip, ROW_TILE, upper, None)
    w_out_upper = _adamw_reduced(
        "adamw_w_out_upper", w_out, m_w_out, v_w_out, partials[1], received[1], chip, W_OUT_SHARD, upper, None)
    g_w_ada, d_w_ada, nm_w_ada, nv_w_ada = _adamw_w_ada(
        w_ada, m_w_ada, v_w_ada, c_act_t, slabs, me.astype(jnp.int32).reshape(1), [token])

    partials, received = _finish_exchange(
        chips_0, "grad_chips_finish_0", sems_0, partials, received, [nv_w_ada, w_in_upper[3], w_out_upper[3]])
    gather_pool = _all_gather_exchange(_reduce_w_pool(partials[2], received[2], chip), axis=1)
    sems_p, pool_rows, pool_landing, token_p = _start_exchange(gather_pool, "all_gather_grad_w_pool_start")
    g_w_in, d_w_in, nm_w_in, nv_w_in = _adamw_reduced(
        "adamw_w_in_0", w_in, m_w_in, v_w_in, partials[0], received[0], chip, ROW_TILE, (0, 1), w_in_upper, [token_p])
    g_w_out, d_w_out, nm_w_out, nv_w_out = _adamw_reduced(
        "adamw_w_out_0", w_out, m_w_out, v_w_out, partials[1], received[1], chip, W_OUT_SHARD, (0, 1), w_out_upper,
        [token_p])
    ends = [0, o, o + D_MODEL, o + 2 * D_MODEL, o + 2 * D_MODEL + POOL_W, SLAB_COLS]
    g_b_ada, g_g_pre, g_g_post, g_pool_scale, g_conv, loss = _sum_sources(
        slabs, list(zip(ends[:-1], ends[1:])), [token_p, nv_w_in, nv_w_out])
    loss = loss.reshape(())
    g_w_conv = lax.dynamic_slice_in_dim(g_conv.reshape(DEPTH, 3, CONV_W), me * conv_shard, conv_shard, axis=2)
    taps_first = lambda a: a.transpose(1, 0, 2)
    small = _adamw_small("adamw_small", [
        (b_ada, g_b_ada, m_b_ada, v_b_ada),
        (g_pre, g_g_pre, m_g_pre, v_g_pre),
        tuple(taps_first(a) for a in (w_conv, g_w_conv, m_w_conv, v_w_conv)),
        (pool_scale, g_pool_scale, m_pool_scale, v_pool_scale),
        (g_post, g_g_post, m_g_post, v_g_post),
    ])
    (d_b_ada, nm_b_ada, nv_b_ada, g_b_ada), (d_g_pre, nm_g_pre, nv_g_pre, g_g_pre), conv_steps, \
        (d_ps, nm_ps, nv_ps, g_pool_scale), (d_g_post, nm_g_post, nv_g_post, g_g_post) = small
    d_w_conv, nm_w_conv, nv_w_conv, g_w_conv = (taps_first(a) for a in conv_steps)
    _, (g_pool_all,) = _finish_exchange(
        gather_pool, "all_gather_grad_w_pool_finish", sems_p, pool_rows, pool_landing, [nv_w_in, nv_w_out, nv_g_post])
    ((d_w_pool, nm_w_pool, nv_w_pool, g_w_pool),) = _adamw_small(
        "adamw_w_pool", [(w_pool, g_pool_all.reshape(w_pool.shape), m_w_pool, v_w_pool)])

    return (loss, grad_x,
            g_w_ada, g_b_ada, g_g_pre, g_w_in, g_w_conv, g_w_pool, g_pool_scale, g_w_out, g_g_post,
            d_w_ada, d_b_ada, d_g_pre, d_w_in, d_w_conv, d_w_pool, d_ps, d_w_out, d_g_post,
            nm_w_ada, nm_b_ada, nm_g_pre, nm_w_in, nm_w_conv, nm_w_pool, nm_ps, nm_w_out, nm_g_post,
            nv_w_ada, nv_b_ada, nv_g_pre, nv_w_in, nv_w_conv, nv_w_pool, nv_ps, nv_w_out, nv_g_post)
```

```python
from functools import partial

import jax
import jax.numpy as jnp
from jax import lax
from jax.experimental import pallas as pl
from jax.experimental.pallas import tpu as pltpu

F32 = jnp.float32
BF16 = jnp.bfloat16

D_MODEL = 1024
DEPTH = 4
CONV_W = 512
POOL_W = 512
POOL_WINDOWS = (2, 4, 8, 16)
GROUP_D = 128
IN_COLS = 4 * CONV_W + 2 * POOL_W
NORM_EPS = 1e-6

ADAM_LR = 0.001
ADAM_B1 = 0.9
ADAM_B2 = 0.999
ADAM_EPS = 1e-08
ADAM_WD = 0.01
ADAM_STEP = 10

N_DEV = 8
N_CHIP = 4
N_OTHER_CHIPS = N_CHIP - 1
MESH = pl.DeviceIdType.MESH
W_IN_SHARD = IN_COLS // N_DEV
W_OUT_SHARD = D_MODEL // N_DEV
POOL_ROWS = len(POOL_WINDOWS) * GROUP_D
POOL_SHARD = POOL_ROWS // N_DEV

SUBLANES = 8
LANES = 128
VMEM_LIMIT_BYTES = 56 * 1024 * 1024
ROW_TILE = 512
BWD_TILE = 256
GWIN_COLS = 768
GWOUT_COLS = 512
POOL_HALO = 16
CONV_HALO = SUBLANES

SLAB_COLS = 3 * D_MODEL + D_MODEL + D_MODEL + POOL_W + 3 * CONV_W

HBM = pl.BlockSpec(memory_space=pl.ANY)


def _params(**kw):
    return pltpu.CompilerParams(vmem_limit_bytes=VMEM_LIMIT_BYTES, **kw)


def _sigmoid(v):
    return 1.0 / (1.0 + jnp.exp(-v))


def _dot(a, b):
    return jnp.dot(a, b, preferred_element_type=F32)


def _dot_tn(a, b):
    return lax.dot_general(a, b, (((0,), (0,)), ((), ())), preferred_element_type=F32)


def _dot_nt(a, b):
    return lax.dot_general(a, b, (((1,), (1,)), ((), ())), preferred_element_type=F32)


def _rows_from_before(v, k):
    return pltpu.roll(v, k, 0)


def _rows_from_after(v, k):
    return pltpu.roll(v, v.shape[0] - k, 0)


def _window_counts(t0, rows):
    return (lax.broadcasted_iota(jnp.int32, (rows, 1), 0) + (t0 + 1)).astype(F32)


def _split_proj(p32):
    cw = CONV_W
    return (p32[:, 0 * cw:1 * cw], p32[:, 1 * cw:2 * cw], p32[:, 2 * cw:3 * cw], p32[:, 3 * cw:4 * cw],
            p32[:, 4 * cw:4 * cw + POOL_W], p32[:, 4 * cw + POOL_W:])


def _layer_spec(shape, layer):
    nd = len(shape)
    return pl.BlockSpec((None,) + tuple(shape[1:]), lambda i, _l=layer, _n=nd: (_l,) + (0,) * (_n - 1))


def _whole_spec(shape):
    return pl.BlockSpec(tuple(shape), lambda i, _n=len(shape): (0,) * _n, pipeline_mode=pl.Buffered(1))


def _mesh_position():
    return lax.axis_index("x"), lax.axis_index("y"), lax.axis_index("c")


def _block_id(x, y, c):
    return 4 * x + 2 * y + c


def _other_chips(x, y):
    return [(x ^ 1, y), (x, y ^ 1), (x ^ 1, y ^ 1)]


def _col_block(ref, blk):
    return ref.at[:, pl.ds(pl.multiple_of(blk * W_IN_SHARD, LANES), W_IN_SHARD)]


def _row_block(rows):
    def block(ref, blk):
        return ref.at[pl.ds(pl.multiple_of(blk * rows, rows), rows), :]
    return block


_BLOCK_OF = (_col_block, _row_block(W_OUT_SHARD), _row_block(POOL_SHARD))
_BLOCK_SHAPES = ((D_MODEL, W_IN_SHARD), (W_OUT_SHARD, D_MODEL), (POOL_SHARD, GROUP_D))


class _Exchange:
    def __init__(self, inputs, out_shapes, aliases, sem_shapes, make):
        self.inputs, self.out_shapes, self.aliases, self.sem_shapes, self.make = (
            list(inputs), list(out_shapes), dict(aliases), list(sem_shapes), make)


def _run_exchange(exchange, name):
    n_in, n_out = len(exchange.inputs), len(exchange.out_shapes)

    def body(*refs):
        start, finish = exchange.make(refs[:n_in], refs[n_in:n_in + n_out], refs[n_in + n_out:])
        start()
        finish()

    return pl.pallas_call(
        body, name=name, in_specs=[HBM] * n_in, out_specs=[HBM] * n_out, out_shape=exchange.out_shapes,
        scratch_shapes=exchange.sem_shapes, input_output_aliases=exchange.aliases, compiler_params=_params(),
    )(*exchange.inputs)


_SEM = pl.BlockSpec(memory_space=pltpu.SEMAPHORE)
_DATAFLOW = pltpu.SideEffectType.DATAFLOW_SIDE_EFFECTING


def _start_exchange(exchange, name, after=()):
    n_in, n_out, n_sem = len(exchange.inputs), len(exchange.out_shapes), len(exchange.sem_shapes)
    sources = [i for i in range(n_in) if i not in exchange.aliases]
    aliases = {i: n_sem + k for k, i in enumerate(sources)}
    aliases.update({i: n_sem + len(sources) + o for i, o in exchange.aliases.items()})

    def body(*refs):
        in_refs = refs[:n_in]
        outs = refs[n_in + len(after):]
        sems = outs[:n_sem]
        out_refs = outs[n_sem + len(sources):n_sem + len(sources) + n_out]
        exchange.make(in_refs, out_refs, sems)[0]()
        refs[-1][...] = jnp.zeros_like(refs[-1])

    outs = pl.pallas_call(
        body, name=name, in_specs=[HBM] * (n_in + len(after)),
        out_specs=[_SEM] * n_sem + [HBM] * (len(sources) + n_out) + [pl.BlockSpec(memory_space=pltpu.VMEM)],
        out_shape=(exchange.sem_shapes + [pltpu.HBM(exchange.inputs[i].shape, exchange.inputs[i].dtype) for i in sources]
                   + [pltpu.HBM(s.shape, s.dtype) for s in exchange.out_shapes]
                   + [jax.ShapeDtypeStruct((SUBLANES, LANES), F32)]),
        input_output_aliases=aliases, compiler_params=_params(has_side_effects=_DATAFLOW),
    )(*exchange.inputs, *after)
    return outs[:n_sem], outs[n_sem:n_sem + len(sources)], outs[n_sem + len(sources):-1], outs[-1]


def _finish_exchange(exchange, name, sems, sources, landing, after):
    n_src, n_out, n_sem = len(sources), len(landing), len(sems)
    n_in = len(exchange.inputs)
    source_at = [i for i in range(n_in) if i not in exchange.aliases]

    def body(*refs):
        src_refs, out_refs = refs[:n_src], refs[n_src:n_src + n_out]
        sem_refs = refs[n_src + n_out:n_src + n_out + n_sem]
        in_refs = [None] * n_in
        for k, i in enumerate(source_at):
            in_refs[i] = src_refs[k]
        for i, o in exchange.aliases.items():
            in_refs[i] = out_refs[o]
        exchange.make(in_refs, out_refs, sem_refs)[1]()

    arrays = list(sources) + list(landing)
    outs = pl.pallas_call(
        body, name=name, in_specs=[HBM] * len(arrays) + [_SEM] * n_sem + [HBM] * len(after),
        out_specs=[HBM] * len(arrays), out_shape=[pltpu.HBM(a.shape, a.dtype) for a in arrays],
        input_output_aliases={i: i for i in range(len(arrays))}, compiler_params=_params(has_side_effects=_DATAFLOW),
    )(*arrays, *sems, *after)
    return outs[:n_src], outs[n_src:]


N_GATHERED = 2
FIRST_COPIES = 1 + N_OTHER_CHIPS
_GATHERED_SHAPES = ((D_MODEL, IN_COLS), (D_MODEL, D_MODEL))
ALL_OTHER_CHIPS, NEIGHBOUR_CHIPS, DIAGONAL_CHIP = (0, 1, 2), (0, 1), (2,)


def _first_sem(layer, a, k):
    return (layer * N_GATHERED + a) * FIRST_COPIES + k


def _first_arrival(layer, a, j):
    return _first_sem(layer, a, 1 + j)


def _gather_copy(window_of, full_ref, blk, send_sem, recv_sem, to, src=None):
    window = window_of(full_ref, blk)
    return pltpu.make_async_remote_copy(
        src_ref=window if src is None else src, dst_ref=window, send_sem=send_sem, recv_sem=recv_sem,
        device_id=to, device_id_type=MESH)


def _first_copies(layer, shard_refs, full_refs, send_sems, recv_sems, local_sems, relayed):
    x, y, c = _mesh_position()
    me = _block_id(x, y, c)
    chips = [_other_chips(x, y)[j] for j in (NEIGHBOUR_CHIPS if relayed else ALL_OTHER_CHIPS)]
    own, remote = [], []
    for a in range(N_GATHERED):
        shard = shard_refs[a].at[layer]
        own.append(pltpu.make_async_copy(
            shard, _BLOCK_OF[a](full_refs[a], me), local_sems.at[layer * N_GATHERED + a]))
        targets = [(x, y, 1 - c)] + [(*chip, c) for chip in chips]
        remote += [_gather_copy(_BLOCK_OF[a], full_refs[a], me, send_sems.at[_first_sem(layer, a, k)],
                                recv_sems.at[_first_sem(layer, a, k)], to, src=shard)
                   for k, to in enumerate(targets)]
    return own, remote


def _gather_weights_start(first_layer, win_shards, wout_shards, after, relayed=False):
    n_layers = win_shards.shape[0]
    n_first = n_layers * N_GATHERED * FIRST_COPIES
    sem_shapes = [pltpu.SemaphoreType.DMA((n_first,)), pltpu.SemaphoreType.DMA((n_first,)),
                  pltpu.SemaphoreType.DMA((n_layers * N_GATHERED,))]
    shards = [win_shards, wout_shards]

    def body(win_sh, wout_sh, *rest):
        send_sems, recv_sems, local_sems, win_thru, wout_thru, *landing = rest[len(after):]
        for layer in range(n_layers):
            own, remote = _first_copies(layer, (win_sh, wout_sh), landing[N_GATHERED * layer:N_GATHERED * (layer + 1)],
                                        send_sems, recv_sems, local_sems, relayed)
            for cp in own + remote:
                cp.start()

    outs = pl.pallas_call(
        body, name=f"all_gather_weights_start_{first_layer}", in_specs=[HBM] * (2 + len(after)),
        out_specs=[_SEM] * 3 + [HBM] * (2 + n_layers * N_GATHERED),
        out_shape=(sem_shapes + [pltpu.HBM(s.shape, s.dtype) for s in shards]
                   + [pltpu.HBM(s, BF16) for _ in range(n_layers) for s in _GATHERED_SHAPES]),
        input_output_aliases={0: 3, 1: 4}, compiler_params=_params(has_side_effects=_DATAFLOW),
    )(*shards, *after)
    landing = outs[5:]
    return outs[:3], outs[3:5], [landing[N_GATHERED * l:N_GATHERED * (l + 1)] for l in range(n_layers)]


def _passed_on_copies(full_refs, send_sems, recv_sems, core_of_block, chips):
    x, y, c = _mesh_position()
    return [_gather_copy(_BLOCK_OF[a], full_refs[a], _block_id(*_other_chips(x, y)[j], core_of_block),
                         send_sems.at[a * N_OTHER_CHIPS + j], recv_sems.at[a * N_OTHER_CHIPS + j], (x, y, 1 - c))
            for a in range(N_GATHERED) for j in chips]


def _relayed_copies(full_refs, send_sems, recv_sems):
    x, y, c = _mesh_position()
    source, to = (x ^ (1 - c), y ^ c), (x ^ c, y ^ (1 - c))
    return [_gather_copy(_BLOCK_OF[a], full_refs[a], _block_id(*source, c), send_sems.at[a], recv_sems.at[a], (*to, c))
            for a in range(N_GATHERED)]


def _gather_weights_pass_on(name, chips, arrival_sems, arrival_sem_of, landing, after, relay=False):
    n = N_GATHERED * N_OTHER_CHIPS
    sem_shapes = [pltpu.SemaphoreType.DMA((n,))] * 2 + [pltpu.SemaphoreType.DMA((N_GATHERED,))] * (2 if relay else 0)

    def body(win_ref, wout_ref, arrivals, *rest):
        sems = rest[len(after):len(after) + len(sem_shapes)]
        x, y, c = _mesh_position()
        full_refs = (win_ref, wout_ref)
        passed = _passed_on_copies(full_refs, sems[0], sems[1], c, chips)
        relayed = _relayed_copies(full_refs, sems[2], sems[3]) if relay else []
        for a in range(N_GATHERED):
            for j in chips:
                sem = arrivals.at[arrival_sem_of(a, j)]
                _gather_copy(_BLOCK_OF[a], full_refs[a], _block_id(*_other_chips(x, y)[j], c), sem, sem,
                             (x, y, c)).wait_recv()
            for cp in relayed[a:a + 1] + passed[a * len(chips):(a + 1) * len(chips)]:
                cp.start()

    outs = pl.pallas_call(
        body, name=name, in_specs=[HBM] * N_GATHERED + [_SEM] + [HBM] * len(after),
        out_specs=[_SEM] * len(sem_shapes) + [HBM] * N_GATHERED,
        out_shape=sem_shapes + [pltpu.HBM(a.shape, a.dtype) for a in landing],
        input_output_aliases={a: len(sem_shapes) + a for a in range(N_GATHERED)},
        compiler_params=_params(has_side_effects=_DATAFLOW),
    )(*landing, arrival_sems, *after)
    return outs[:len(sem_shapes)], outs[len(sem_shapes):]


def _gather_weights_finish(layer, index, first_sems, passed, shards, landing, relay_send_sems=None, after=()):
    relayed = relay_send_sems is not None
    passed_sems = [sem for (send, recv), _ in passed for sem in (send, recv)] + ([relay_send_sems] if relayed else [])

    def body(win_ref, wout_ref, first_send, first_recv, local_sems, *rest):
        sems, (win_sh, wout_sh) = rest[:len(passed_sems)], rest[len(passed_sems):len(passed_sems) + 2]
        x, y, c = _mesh_position()
        full_refs = (win_ref, wout_ref)
        own, sent = _first_copies(index, (win_sh, wout_sh), full_refs, first_send, first_recv, local_sems, relayed)
        for a in range(N_GATHERED):
            sem = _first_sem(index, a, 0)
            _gather_copy(_BLOCK_OF[a], full_refs[a], _block_id(x, y, 1 - c), first_recv.at[sem], first_recv.at[sem],
                         (x, y, c)).wait_recv()
        for p, (_, chips) in enumerate(passed):
            for cp in _passed_on_copies(full_refs, sems[2 * p], sems[2 * p + 1], 1 - c, chips):
                cp.wait_recv()
            sent += _passed_on_copies(full_refs, sems[2 * p], sems[2 * p + 1], c, chips)
        if relayed:
            sent += _relayed_copies(full_refs, sems[-1], sems[-1])
        for cp in sent:
            cp.wait_send()
        for cp in own:
            cp.wait()

    return pl.pallas_call(
        body, name=f"all_gather_weights_finish_{layer}",
        in_specs=[HBM] * N_GATHERED + [_SEM] * (3 + len(passed_sems)) + [HBM] * (2 + len(after)),
        out_specs=[HBM] * N_GATHERED, out_shape=[pltpu.HBM(a.shape, a.dtype) for a in landing],
        input_output_aliases={a: a for a in range(N_GATHERED)}, compiler_params=_params(has_side_effects=_DATAFLOW),
    )(*landing, *first_sems, *passed_sems, *shards, *after)


ALL_KINDS = (0, 1, 2)


def _sibling_exchange(grads, kinds):
    n_arr = len(grads)

    def make(in_refs, out_refs, sems):
        send_sems, recv_sems = sems
        x, y, c = _mesh_position()
        copies = [pltpu.make_async_remote_copy(
            src_ref=_BLOCK_OF[kinds[a]](in_refs[a], 2 * q + (1 - c)), dst_ref=out_refs[a].at[q],
            send_sem=send_sems.at[a * N_CHIP + q], recv_sem=recv_sems.at[a * N_CHIP + q],
            device_id=(x, y, 1 - c), device_id_type=MESH)
            for a in range(n_arr) for q in range(N_CHIP)]

        def start():
            for cp in copies:
                cp.start()

        def finish():
            for cp in copies:
                cp.wait_recv()
            for cp in copies:
                cp.wait_send()

        return start, finish

    return _Exchange(
        grads, [jax.ShapeDtypeStruct((N_CHIP,) + _BLOCK_SHAPES[k], BF16) for k in kinds], {},
        [pltpu.SemaphoreType.DMA((n_arr * N_CHIP,)), pltpu.SemaphoreType.DMA((n_arr * N_CHIP,))], make)


def _chips_exchange(layer, partials, received, kinds):
    n_arr = len(partials)

    def make(in_refs, out_refs, sems):
        send_sems, recv_sems = sems
        x, y, c = _mesh_position()
        copies = [pltpu.make_async_remote_copy(
            src_ref=in_refs[a].at[2 * qx + qy, layer], dst_ref=out_refs[a].at[j, layer],
            send_sem=send_sems.at[a * N_OTHER_CHIPS + j], recv_sem=recv_sems.at[a * N_OTHER_CHIPS + j],
            device_id=(qx, qy, c), device_id_type=MESH)
            for a in range(n_arr) for j, (qx, qy) in enumerate(_other_chips(x, y))]

        def start():
            for cp in copies:
                cp.start()

        def finish():
            for cp in copies:
                cp.wait_recv()
            for cp in copies:
                cp.wait_send()

        return start, finish

    inputs = list(partials)
    aliases = {}
    if received is not None:
        inputs += list(received)
        aliases = {n_arr + a: a for a in range(n_arr)}
    return _Exchange(
        inputs, [jax.ShapeDtypeStruct((N_OTHER_CHIPS, DEPTH) + _BLOCK_SHAPES[k], BF16) for k in kinds], aliases,
        [pltpu.SemaphoreType.DMA((n_arr * N_OTHER_CHIPS,)), pltpu.SemaphoreType.DMA((n_arr * N_OTHER_CHIPS,))], make)


def _all_gather_exchange(v, axis=0):
    def make(in_refs, out_refs, sems):
        send_sems, recv_sems, local_sem = sems
        x, y, c = _mesh_position()
        me = _block_id(x, y, c)
        block = lambda blk: out_refs[0].at[(slice(None),) * axis + (blk,)]
        own = pltpu.make_async_copy(in_refs[0], block(me), local_sem.at[0])
        sends, arrivals = [], []
        for k in range(1, N_DEV):
            px, py, pc = x ^ ((k >> 2) & 1), y ^ ((k >> 1) & 1), c ^ (k & 1)
            sends.append(pltpu.make_async_remote_copy(
                src_ref=in_refs[0], dst_ref=block(me), send_sem=send_sems.at[k - 1],
                recv_sem=recv_sems.at[k - 1], device_id=(px, py, pc), device_id_type=MESH))
            arrivals.append(pltpu.make_async_remote_copy(
                src_ref=in_refs[0], dst_ref=block(_block_id(px, py, pc)), send_sem=send_sems.at[k - 1],
                recv_sem=recv_sems.at[k - 1], device_id=(x, y, c), device_id_type=MESH))

        def start():
            for cp in [own] + sends:
                cp.start()

        def finish():
            for cp in arrivals:
                cp.wait_recv()
            for cp in sends:
                cp.wait_send()
            own.wait()

        return start, finish

    return _Exchange(
        [v], [jax.ShapeDtypeStruct(v.shape[:axis] + (N_DEV,) + v.shape[axis:], v.dtype)], {},
        [pltpu.SemaphoreType.DMA((N_DEV - 1,)), pltpu.SemaphoreType.DMA((N_DEV - 1,)),
         pltpu.SemaphoreType.DMA((1,))], make)


def _host(exchange, args, in_specs, out_shape, out_specs, scratch):
    n_own = (len(args), len(out_shape), len(scratch))
    if exchange is None:
        return {}, lambda refs: (refs, None)
    n_ex = (len(exchange.inputs), len(exchange.out_shapes), len(exchange.sem_shapes))
    aliases = {n_own[0] + i: n_own[1] + o for i, o in exchange.aliases.items()}
    args += exchange.inputs
    in_specs += [HBM] * n_ex[0]
    out_shape += exchange.out_shapes
    out_specs += [HBM] * n_ex[1]
    scratch += exchange.sem_shapes

    def split(refs):
        own, theirs, at = [], [], 0
        for mine, ex in zip(n_own, n_ex):
            own += refs[at:at + mine]
            theirs.append(refs[at + mine:at + mine + ex])
            at += mine + ex
        return own, exchange.make(*theirs)

    return aliases, split


def _all_gather_small(v, name, after=()):
    vmem = pl.BlockSpec(memory_space=pltpu.VMEM)

    def body(v_ref, *rest):
        out_ref, send_sems, recv_sems = rest[len(after):]
        x, y, c = _mesh_position()
        me = _block_id(x, y, c)
        out_ref[me] = v_ref[...]
        sends = []
        for k in range(1, N_DEV):
            px, py, pc = x ^ ((k >> 2) & 1), y ^ ((k >> 1) & 1), c ^ (k & 1)
            send = pltpu.make_async_remote_copy(
                src_ref=v_ref, dst_ref=out_ref.at[me], send_sem=send_sems.at[k - 1], recv_sem=recv_sems.at[k - 1],
                device_id=(px, py, pc), device_id_type=MESH)
            send.start()
            sends.append((send, _block_id(px, py, pc)))
        for k, (send, peer) in enumerate(sends):
            pltpu.make_async_remote_copy(
                src_ref=v_ref, dst_ref=out_ref.at[peer], send_sem=send_sems.at[k], recv_sem=recv_sems.at[k],
                device_id=(x, y, c), device_id_type=MESH).wait_recv()
        for send, _ in sends:
            send.wait_send()

    return pl.pallas_call(
        body, name=name, in_specs=[vmem] + [HBM] * len(after), out_specs=vmem,
        out_shape=jax.ShapeDtypeStruct((N_DEV,) + v.shape, v.dtype),
        scratch_shapes=[pltpu.SemaphoreType.DMA((N_DEV - 1,)), pltpu.SemaphoreType.DMA((N_DEV - 1,))],
        compiler_params=_params(),
    )(v, *after)


def _forward_layer(layer, x, vec, cps, wpool, win, wout, target=None):
    t_len = x.shape[0]
    n_tiles = t_len // ROW_TILE
    row = lambda cols: pl.BlockSpec((ROW_TILE, cols), lambda i: (i, 0))
    widths = (D_MODEL, 2 * CONV_W, 3 * CONV_W, 4 * POOL_W)
    head = target is not None

    def body(x_ref, vec_ref, cps_ref, wpool_ref, win_ref, wout_ref, *rest):
        target_ref = rest[0] if head else None
        xo_ref, y_ref, h_ref, ycat_ref, uc_ref, fa_ref, fp_ref = rest[head:head + 7]
        loss_ref = rest[head + 7] if head else None
        zc_ref, pc_ref = rest[-2:]
        i = pl.program_id(0)

        @pl.when(i == 0)
        def _():
            zc_ref[...] = jnp.zeros_like(zc_ref)
            pc_ref[...] = jnp.zeros_like(pc_ref)
            if head:
                loss_ref[...] = jnp.zeros_like(loss_ref)

        x_t = x_ref[...]
        shift, scale, gate = vec_ref[0:1, :], vec_ref[1:2, :], vec_ref[2:3, :]
        g_pre, g_post = vec_ref[3:4, :], vec_ref[4:5, :]
        w0, w1, w2, ps = cps_ref[0:1, :], cps_ref[1:2, :], cps_ref[2:3, :], cps_ref[3:4, :]
        rx = lax.rsqrt(jnp.mean(x_t * x_t, axis=-1, keepdims=True) + NORM_EPS)
        h = (x_t * rx) * g_pre * (1.0 + scale) + shift
        h_ref[...] = h.astype(BF16)
        proj = _dot(h.astype(BF16), win_ref[...])
        u_a, b_a, c_a, g_a, u_p, g_p = _split_proj(proj)
        uc_ref[...] = jnp.concatenate([u_a, c_a], axis=1).astype(BF16)

        z = c_a * u_a
        zcat = jnp.concatenate([zc_ref[...], z], axis=0)
        zc_ref[...] = z[ROW_TILE - CONV_HALO:]
        conv = (w0 * _rows_from_before(zcat, 2)[CONV_HALO:] + w1 * _rows_from_before(zcat, 1)[CONV_HALO:] + w2 * z)
        sig_a = _sigmoid(g_a)
        silu_a = g_a * sig_a
        b_conv = b_a * conv
        y_a = b_conv * silu_a
        fa_ref[...] = jnp.concatenate(
            [silu_a * conv, silu_a * b_a, b_conv * (sig_a + silu_a * (1.0 - sig_a))], axis=1).astype(BF16)

        pcat = jnp.concatenate([pc_ref[...], u_p], axis=0)
        pc_ref[...] = u_p[ROW_TILE - POOL_HALO:]
        counts = _window_counts(i * ROW_TILE, ROW_TILE)
        pooled, mixed = [], []
        for g, w in enumerate(POOL_WINDOWS):
            cols = slice(g * GROUP_D, (g + 1) * GROUP_D)
            s = pcat[:, cols]
            step = 1
            while step < w:
                s = s + _rows_from_before(s, step)
                step *= 2
            pooled_g = (s[POOL_HALO:] * (1.0 / jnp.minimum(counts, float(w))) - u_p[:, cols]).astype(BF16)
            pooled.append(pooled_g)
            mixed.append(_dot(pooled_g, wpool_ref[g]))
        mixed = jnp.concatenate(mixed, axis=1)
        sig_p = _sigmoid(g_p)
        silu_p = g_p * sig_p
        mixed_ps = mixed * ps
        y_p = mixed_ps * silu_p
        fp_ref[...] = jnp.concatenate(
            [(ps * silu_p).astype(BF16), (mixed_ps * (sig_p + silu_p * (1.0 - sig_p))).astype(BF16),
             (silu_p * mixed).astype(BF16)] + pooled, axis=1)

        ycat = jnp.concatenate([y_a, y_p], axis=1)
        ycat_ref[...] = ycat.astype(BF16)
        y_b = _dot(ycat.astype(BF16), wout_ref[...]).astype(BF16)
        y_ref[...] = y_b
        y_t = y_b.astype(F32)
        ry = lax.rsqrt(jnp.mean(y_t * y_t, axis=-1, keepdims=True) + NORM_EPS)
        x_next = x_t + gate * (y_t * ry * g_post)
        if head:
            err = x_next - target_ref[...]
            xo_ref[...] = err * (1.0 / D_MODEL)
            loss_ref[...] += jnp.sum(err * err) * (0.5 / D_MODEL)
        else:
            xo_ref[...] = x_next

    tile = (SUBLANES, LANES)
    return pl.pallas_call(
        body, name=f"forward_layer_{layer}", grid=(n_tiles,),
        in_specs=[row(D_MODEL), _layer_spec(vec.shape, layer), _layer_spec(cps.shape, layer),
                  _layer_spec(wpool.shape, layer), _whole_spec(win.shape), _whole_spec(wout.shape)]
        + [row(D_MODEL)] * head,
        out_specs=[row(D_MODEL), row(D_MODEL), row(D_MODEL), row(D_MODEL)] + [row(w) for w in widths[1:]]
        + [_whole_spec(tile)] * head,
        out_shape=[jax.ShapeDtypeStruct((t_len, D_MODEL), F32), jax.ShapeDtypeStruct((t_len, D_MODEL), BF16),
                   jax.ShapeDtypeStruct((t_len, D_MODEL), BF16), jax.ShapeDtypeStruct((t_len, D_MODEL), BF16)]
        + [jax.ShapeDtypeStruct((t_len, w), BF16) for w in widths[1:]] + [jax.ShapeDtypeStruct(tile, F32)] * head,
        scratch_shapes=[pltpu.VMEM((CONV_HALO, CONV_W), F32), pltpu.VMEM((POOL_HALO, POOL_W), F32)],
        compiler_params=_params(dimension_semantics=("arbitrary",)),
    )(x, vec, cps, wpool, win, wout, *([target] * head))


def _backward_layer(layer, dxo, y, x, uc, fa, fp, vec, cps, wpool, wout, win, after):
    t_len = dxo.shape[0]
    n_tiles = t_len // BWD_TILE
    halo_per_tile = BWD_TILE // POOL_HALO
    rev = lambda cols: pl.BlockSpec((BWD_TILE, cols), lambda i: (n_tiles - 1 - i, 0))
    halo_spec = pl.BlockSpec(
        (POOL_HALO, 2 * CONV_W), lambda i: (jnp.maximum((n_tiles - 1 - i) * halo_per_tile - 1, 0), 0))
    gwpool_shape = (len(POOL_WINDOWS), GROUP_D, GROUP_D)

    def body(dxo_ref, y_ref, x_ref, uc_ref, uch_ref, fa_ref, fp_ref, vec_ref, cps_ref, wpool_ref, wout_ref, win_ref,
             *rest):
        dx_ref, dproj_ref, dy_ref, gwpool_ref, dcps_ref, dvec_ref, gwpool_acc, dcc_ref, qc_ref = rest[len(after):]
        i = pl.program_id(0)
        tile = n_tiles - 1 - i

        @pl.when(i == 0)
        def _():
            gwpool_acc[...] = jnp.zeros_like(gwpool_acc)
            dcps_ref[...] = jnp.zeros_like(dcps_ref)
            dvec_ref[...] = jnp.zeros_like(dvec_ref)
            dcc_ref[...] = jnp.zeros_like(dcc_ref)
            qc_ref[...] = jnp.zeros_like(qc_ref)

        shift, scale, gate = vec_ref[0:1, :], vec_ref[1:2, :], vec_ref[2:3, :]
        g_pre, g_post = vec_ref[3:4, :], vec_ref[4:5, :]
        w0, w1, w2 = cps_ref[0:1, :], cps_ref[1:2, :], cps_ref[2:3, :]

        dxo_t = dxo_ref[...]
        y_t = y_ref[...].astype(F32)
        ry = lax.rsqrt(jnp.mean(y_t * y_t, axis=-1, keepdims=True) + NORM_EPS)
        yh = y_t * ry
        dvec_ref[2:3, :] += jnp.sum(dxo_t * yh, axis=0, keepdims=True)
        dyh = dxo_t * (gate * g_post)
        dy_b = (ry * (dyh - yh * jnp.mean(dyh * yh, axis=-1, keepdims=True))).astype(BF16)
        dy_ref[...] = dy_b
        dycat = _dot_nt(dy_b, wout_ref[...])
        dy_a, dy_p = dycat[:, :CONV_W], dycat[:, CONV_W:]

        fa_t = fa_ref[...].astype(F32)
        db_a = dy_a * fa_t[:, :CONV_W]
        dconv = dy_a * fa_t[:, CONV_W:2 * CONV_W]
        dg_a = dy_a * fa_t[:, 2 * CONV_W:]
        uc_t = uc_ref[...].astype(F32)
        u_a, c_a = uc_t[:, :CONV_W], uc_t[:, CONV_W:]
        halo = jnp.where(tile > 0, uch_ref[...].astype(F32), 0.0)[POOL_HALO - CONV_HALO:]
        z = c_a * u_a
        zcat = jnp.concatenate([halo[:, CONV_W:] * halo[:, :CONV_W], z], axis=0)
        z1 = _rows_from_before(zcat, 1)[CONV_HALO:]
        z2 = _rows_from_before(zcat, 2)[CONV_HALO:]
        dccat = jnp.concatenate([dconv, dcc_ref[...]], axis=0)
        dc1 = _rows_from_after(dccat, 1)[:BWD_TILE]
        dc2 = _rows_from_after(dccat, 2)[:BWD_TILE]
        dz = w2 * dconv + w1 * dc1 + w0 * dc2
        dcc_ref[...] = dconv[:CONV_HALO]
        dcps_ref[0:1, :] += jnp.sum(dconv * z2, axis=0, keepdims=True)
        dcps_ref[1:2, :] += jnp.sum(dconv * z1, axis=0, keepdims=True)
        dcps_ref[2:3, :] += jnp.sum(dconv * z, axis=0, keepdims=True)
        du_a = dz * c_a
        dc_a = dz * u_a

        dmixed = (dy_p * fp_ref[:, :POOL_W].astype(F32)).astype(BF16)
        dg_p = dy_p * fp_ref[:, POOL_W:2 * POOL_W].astype(F32)
        dcps_ref[3:4, :] += jnp.sum(dy_p * fp_ref[:, 2 * POOL_W:3 * POOL_W].astype(F32), axis=0, keepdims=True)
        counts = _window_counts(tile * BWD_TILE, BWD_TILE)
        du_p, q_head = [], []
        for g, w in enumerate(POOL_WINDOWS):
            cols = slice(g * GROUP_D, (g + 1) * GROUP_D)
            dm_g = dmixed[:, cols]
            dpooled_g = _dot_nt(dm_g, wpool_ref[g])
            gwpool_acc[g] += _dot_tn(fp_ref[:, 3 * POOL_W + g * GROUP_D:3 * POOL_W + (g + 1) * GROUP_D], dm_g)
            q_g = dpooled_g * (1.0 / jnp.minimum(counts, float(w)))
            q_head.append(q_g[:POOL_HALO])
            s = jnp.concatenate([q_g, qc_ref[:, cols]], axis=0)
            step = 1
            while step < w:
                s = s + _rows_from_after(s, step)
                step *= 2
            du_p.append(s[:BWD_TILE] - dpooled_g)
        qc_ref[...] = jnp.concatenate(q_head, axis=1)
        dproj_b = jnp.concatenate([du_a, db_a, dc_a, dg_a] + du_p + [dg_p], axis=1).astype(BF16)
        dproj_ref[...] = dproj_b

        x_t = x_ref[...]
        rx = lax.rsqrt(jnp.mean(x_t * x_t, axis=-1, keepdims=True) + NORM_EPS)
        xn = x_t * rx
        mod_scale = 1.0 + scale
        dh = _dot_nt(dproj_b, win_ref[...])
        dvec_ref[0:1, :] += jnp.sum(dh, axis=0, keepdims=True)
        dvec_ref[1:2, :] += jnp.sum(dh * xn, axis=0, keepdims=True)
        dxn = dh * (g_pre * mod_scale)
        dx_ref[...] = dxo_t + rx * (dxn - xn * jnp.mean(dxn * xn, axis=-1, keepdims=True))

        @pl.when(i == n_tiles - 1)
        def _():
            gwpool_ref[...] = gwpool_acc[...].astype(BF16)
            sum_dh_xn, sum_dxo_yh = dvec_ref[1:2, :], dvec_ref[2:3, :]
            dvec_ref[1:2, :] = sum_dh_xn * g_pre
            dvec_ref[3:4, :] = sum_dh_xn * mod_scale
            dvec_ref[2:3, :] = sum_dxo_yh * g_post
            dvec_ref[4:5, :] = sum_dxo_yh * gate

    return pl.pallas_call(
        body, name=f"backward_layer_{layer}", grid=(n_tiles,),
        in_specs=[rev(D_MODEL), rev(D_MODEL), rev(D_MODEL), rev(2 * CONV_W), halo_spec, rev(3 * CONV_W),
                  rev(4 * POOL_W), _layer_spec(vec.shape, layer), _layer_spec(cps.shape, layer),
                  _layer_spec(wpool.shape, layer), _whole_spec(wout.shape), _whole_spec(win.shape)]
        + [HBM] * len(after),
        out_specs=[rev(D_MODEL), rev(IN_COLS), rev(D_MODEL), _whole_spec(gwpool_shape),
                   _whole_spec((SUBLANES, CONV_W)), _whole_spec((SUBLANES, D_MODEL))],
        out_shape=[jax.ShapeDtypeStruct((t_len, D_MODEL), F32), jax.ShapeDtypeStruct((t_len, IN_COLS), BF16),
                   jax.ShapeDtypeStruct((t_len, D_MODEL), BF16), jax.ShapeDtypeStruct(gwpool_shape, BF16),
                   jax.ShapeDtypeStruct((SUBLANES, CONV_W), F32), jax.ShapeDtypeStruct((SUBLANES, D_MODEL), F32)],
        scratch_shapes=[pltpu.VMEM(gwpool_shape, F32), pltpu.VMEM((CONV_HALO, CONV_W), F32),
                        pltpu.VMEM((POOL_HALO, POOL_W), F32)],
        compiler_params=_params(dimension_semantics=("arbitrary",)),
    )(dxo, y, x, uc, uc, fa, fp, vec, cps, wpool, wout, win, *after)


def _weight_grads(layer, h, dproj, ycat, dy, exchange, after=()):
    t_len = dy.shape[0]
    n_in, n_out = IN_COLS // GWIN_COLS, D_MODEL // GWOUT_COLS
    args = [h, dproj, ycat, dy, *after]
    in_specs = [_whole_spec(h.shape),
                pl.BlockSpec((t_len, GWIN_COLS), lambda s: (0, jnp.minimum(s, n_in - 1))),
                _whole_spec(ycat.shape),
                pl.BlockSpec((t_len, GWOUT_COLS), lambda s: (0, jnp.maximum(s - n_in, 0)))] + [HBM] * len(after)
    out_shape = [jax.ShapeDtypeStruct((D_MODEL, IN_COLS), BF16), jax.ShapeDtypeStruct((D_MODEL, D_MODEL), BF16)]
    out_specs = [pl.BlockSpec((D_MODEL, GWIN_COLS), lambda s: (0, jnp.minimum(s, n_in - 1))),
                 pl.BlockSpec((D_MODEL, GWOUT_COLS), lambda s: (0, jnp.maximum(s - n_in, 0)))]
    scratch = []
    aliases, split = _host(exchange, args, in_specs, out_shape, out_specs, scratch)

    def body(*refs):
        (h_ref, dproj_ref, ycat_ref, dy_ref, *_, gwin_ref, gwout_ref), hosted = split(refs)
        s = pl.program_id(0)
        if hosted is not None:
            pl.when(s == 0)(hosted[0])

        @pl.when(s < n_in)
        def _():
            gwin_ref[...] = _dot_tn(h_ref[...], dproj_ref[...]).astype(BF16)

        @pl.when(s >= n_in)
        def _():
            gwout_ref[...] = _dot_tn(ycat_ref[...], dy_ref[...]).astype(BF16)

        if hosted is not None:
            pl.when(s == n_in + n_out - 1)(hosted[1])

    return pl.pallas_call(
        body, name=f"weight_grads_{layer}", grid=(n_in + n_out,), in_specs=in_specs, out_specs=out_specs,
        out_shape=out_shape, scratch_shapes=scratch, input_output_aliases=aliases,
        compiler_params=_params(dimension_semantics=("arbitrary",)),
    )(*args)


def _add_sibling_blocks(name, layer, grads, received, core, partials, kinds):
    n_arr = len(grads)

    def body(core_ref, *refs):
        mine, theirs, outs = refs[:n_arr], refs[n_arr:2 * n_arr], refs[-n_arr:]
        for a in range(n_arr):
            outs[a][...] = (mine[a][...].astype(F32) + theirs[a][...].astype(F32)).astype(BF16)

    own_of_kind = [
        pl.BlockSpec((D_MODEL, W_IN_SHARD), lambda q, core_ref: (0, 2 * q + core_ref[0])),
        pl.BlockSpec((W_OUT_SHARD, D_MODEL), lambda q, core_ref: (2 * q + core_ref[0], 0)),
        pl.BlockSpec((POOL_SHARD, GROUP_D), lambda q, core_ref: (2 * q + core_ref[0], 0)),
    ]
    shapes = [_BLOCK_SHAPES[k] for k in kinds]
    recv_specs = [pl.BlockSpec((None,) + s, lambda q, core_ref: (q, 0, 0)) for s in shapes]
    out_specs = [pl.BlockSpec((None, None) + s, lambda q, core_ref: (q, layer, 0, 0)) for s in shapes]
    args = [core, *grads, *received]
    in_specs = [own_of_kind[k] for k in kinds] + recv_specs
    aliases = {}
    if partials is not None:
        aliases = {len(args) + a: a for a in range(n_arr)}
        args += list(partials)
        in_specs += [HBM] * n_arr
    return pl.pallas_call(
        body, name=name,
        grid_spec=pltpu.PrefetchScalarGridSpec(
            num_scalar_prefetch=1, grid=(N_CHIP,), in_specs=in_specs, out_specs=out_specs),
        out_shape=[jax.ShapeDtypeStruct((N_CHIP, DEPTH) + s, BF16) for s in shapes],
        input_output_aliases=aliases,
        compiler_params=_params(dimension_semantics=("arbitrary",)),
    )(*args)


def _modulation_columns(c_all, w_ada):
    def body(c_ref, w_ref, cact_ref, out_ref):
        c_t = c_ref[...]
        c_act = c_t * _sigmoid(c_t)
        cact_ref[...] = c_act
        out_ref[...] = jnp.dot(c_act, w_ref[...], preferred_element_type=F32, precision=lax.Precision.HIGHEST)

    return pl.pallas_call(
        body, name="modulation_columns", grid=(DEPTH,),
        in_specs=[pl.BlockSpec((N_DEV, D_MODEL), lambda l: (0, 0)),
                  pl.BlockSpec((None, D_MODEL, W_IN_SHARD), lambda l: (l, 0, 0))],
        out_specs=[pl.BlockSpec((N_DEV, D_MODEL), lambda l: (0, 0)),
                   pl.BlockSpec((N_DEV, W_IN_SHARD), lambda l: (0, l))],
        out_shape=[jax.ShapeDtypeStruct((N_DEV, D_MODEL), F32),
                   jax.ShapeDtypeStruct((N_DEV, DEPTH * W_IN_SHARD), F32)],
        compiler_params=_params(dimension_semantics=("arbitrary",)),
    )(c_all, w_ada)


def _adamw(w, g, m, v):
    m_new = ADAM_B1 * m + (1.0 - ADAM_B1) * g
    v_new = ADAM_B2 * v + (1.0 - ADAM_B2) * (g * g)
    m_hat = m_new / (1.0 - ADAM_B1 ** ADAM_STEP)
    v_hat = v_new / (1.0 - ADAM_B2 ** ADAM_STEP)
    delta = -ADAM_LR * (m_hat / (jnp.sqrt(v_hat) + ADAM_EPS) + ADAM_WD * w)
    return delta, m_new, v_new


def _adamw_w_ada(w, m, v, c_act_t, dmod_cols):
    def body(w_ref, m_ref, v_ref, ct_ref, dm_ref, g_ref, d_ref, mo_ref, vo_ref):
        g = ct_ref[:, 0:1] * dm_ref[0:1, :]
        for b in range(1, N_DEV):
            g = g + ct_ref[:, b:b + 1] * dm_ref[b:b + 1, :]
        g_ref[...] = g
        d_ref[...], mo_ref[...], vo_ref[...] = _adamw(w_ref[...], g, m_ref[...], v_ref[...])

    big = pl.BlockSpec((None, D_MODEL, W_IN_SHARD), lambda l: (l, 0, 0))
    return pl.pallas_call(
        body, name="adamw_w_ada", grid=(DEPTH,),
        in_specs=[big, big, big, pl.BlockSpec((D_MODEL, N_DEV), lambda l: (0, 0)),
                  pl.BlockSpec((None, N_DEV, W_IN_SHARD), lambda l: (l, 0, 0))],
        out_specs=[big] * 4, out_shape=[jax.ShapeDtypeStruct(w.shape, F32)] * 4,
        compiler_params=_params(dimension_semantics=("arbitrary",)),
    )(w, m, v, c_act_t, dmod_cols)


def _sum_chip_partials(own_ref, recv_ref):
    g = own_ref[...].astype(F32)
    for j in range(N_OTHER_CHIPS):
        g = g + recv_ref[j].astype(F32)
    return g


def _partial_specs(row_tile, cols, first_layer=0):
    own = pl.BlockSpec((None, None, row_tile, cols), lambda l, r, chip_ref: (chip_ref[0], first_layer + l, r, 0))
    recv = pl.BlockSpec((N_OTHER_CHIPS, None, row_tile, cols), lambda l, r, chip_ref: (0, first_layer + l, r, 0))
    return own, recv


def _adamw_reduced(name, w, m, v, partial, received, chip, row_tile, layers, continued, after=()):
    depth, rows, cols = w.shape
    first, stop = layers

    def body(chip_ref, w_ref, m_ref, v_ref, own_ref, recv_ref, *rest):
        g_ref, d_ref, mo_ref, vo_ref = rest[-4:]
        g = _sum_chip_partials(own_ref, recv_ref)
        g_ref[...] = g
        d_ref[...], mo_ref[...], vo_ref[...] = _adamw(w_ref[...], g, m_ref[...], v_ref[...])

    blk = pl.BlockSpec((None, row_tile, cols), lambda l, r, chip_ref: (first + l, r, 0))
    args = [chip, w, m, v, partial, received]
    in_specs = [blk, blk, blk, *_partial_specs(row_tile, cols, first)]
    aliases = {}
    if continued is not None:
        aliases = {len(args) + k: k for k in range(4)}
        args += list(continued)
        in_specs += [HBM] * 4
    args += after
    in_specs += [HBM] * len(after)
    return pl.pallas_call(
        body, name=name,
        grid_spec=pltpu.PrefetchScalarGridSpec(
            num_scalar_prefetch=1, grid=(stop - first, rows // row_tile), in_specs=in_specs, out_specs=[blk] * 4),
        out_shape=[jax.ShapeDtypeStruct(w.shape, F32)] * 4, input_output_aliases=aliases,
        compiler_params=_params(dimension_semantics=("arbitrary", "arbitrary")),
    )(*args)


def _reduce_w_pool(partial, received, chip):
    def body(chip_ref, own_ref, recv_ref, g_ref):
        g_ref[...] = _sum_chip_partials(own_ref, recv_ref)

    return pl.pallas_call(
        body, name="reduce_w_pool",
        grid_spec=pltpu.PrefetchScalarGridSpec(
            num_scalar_prefetch=1, grid=(DEPTH, 1), in_specs=list(_partial_specs(POOL_SHARD, GROUP_D)),
            out_specs=pl.BlockSpec((None, POOL_SHARD, GROUP_D), lambda l, r, chip_ref: (l, 0, 0))),
        out_shape=jax.ShapeDtypeStruct((DEPTH, POOL_SHARD, GROUP_D), F32),
        compiler_params=_params(dimension_semantics=("arbitrary", "arbitrary")),
    )(chip, partial, received)


def _adamw_small(name, params):
    n = len(params)

    def body(*refs):
        ins, outs = refs[:4 * n], refs[4 * n:]
        for p in range(n):
            w_ref, g_ref, m_ref, v_ref = ins[4 * p:4 * p + 4]
            d_ref, mo_ref, vo_ref, go_ref = outs[4 * p:4 * p + 4]
            d_ref[...], mo_ref[...], vo_ref[...] = _adamw(w_ref[...], g_ref[...], m_ref[...], v_ref[...])
            go_ref[...] = g_ref[...]

    vmem = pl.BlockSpec(memory_space=pltpu.VMEM)
    flat = [a for group in params for a in group]
    outs = pl.pallas_call(
        body, name=name, in_specs=[vmem] * len(flat), out_specs=[vmem] * len(flat),
        out_shape=[jax.ShapeDtypeStruct(a.shape, F32) for a in flat], compiler_params=_params(),
    )(*flat)
    return [tuple(outs[4 * p:4 * p + 4]) for p in range(n)]


def _sum_sources(slabs, columns, after):
    def body(s_ref, *rest):
        outs = rest[len(after):]
        acc = s_ref[0]
        for b in range(1, N_DEV):
            acc = acc + s_ref[b]
        for (start, stop), o_ref in zip(columns, outs):
            o_ref[...] = acc[:, start:stop]
        outs[-1][...] = acc[0:1, SLAB_COLS:SLAB_COLS + 1]

    vmem = pl.BlockSpec(memory_space=pltpu.VMEM)
    out_shape = [jax.ShapeDtypeStruct((slabs.shape[1], stop - start), F32) for start, stop in columns]
    out_shape.append(jax.ShapeDtypeStruct((1, 1), F32))
    return pl.pallas_call(
        body, name="sum_small_grads", in_specs=[vmem] + [HBM] * len(after), out_specs=[vmem] * len(out_shape),
        out_shape=out_shape, compiler_params=_params(),
    )(slabs, *after)


def _to_bf16(a, name, layers=None):
    first, stop = layers or (0, a.shape[0])

    def body(a_ref, o_ref):
        o_ref[...] = a_ref[...].astype(BF16)

    block = (None,) + a.shape[1:]
    return pl.pallas_call(
        body, name=name, grid=(stop - first,), in_specs=[pl.BlockSpec(block, lambda l: (first + l, 0, 0))],
        out_specs=pl.BlockSpec(block, lambda l: (l, 0, 0)),
        out_shape=jax.ShapeDtypeStruct((stop - first,) + a.shape[1:], BF16),
        compiler_params=_params(dimension_semantics=("arbitrary",)),
    )(a)


def kernel(x, c, w_ada, b_ada, g_pre, w_in, w_conv, w_pool, pool_scale, w_out, g_post, loss_target, m_w_ada, m_b_ada, m_g_pre, m_w_in, m_w_conv, m_w_pool, m_pool_scale, m_w_out, m_g_post, v_w_ada, v_b_ada, v_g_pre, v_w_in, v_w_conv, v_w_pool, v_pool_scale, v_w_out, v_g_post):
    mx, my, mc = _mesh_position()
    me = _block_id(mx, my, mc)
    chip = (2 * mx + my).astype(jnp.int32).reshape(1)
    core = mc.astype(jnp.int32).reshape(1)
    x0 = x[0]
    target = loss_target[0]
    conv_shard = w_conv.shape[-1]

    own_small = jnp.concatenate([c, w_conv.reshape(1, DEPTH * 3 * conv_shard)], axis=1)
    first_shards = [_to_bf16(w_in, "cast_w_in_0", (0, 1)), _to_bf16(w_out, "cast_w_out_0", (0, 1))]
    all_small = _all_gather_small(own_small, "all_gather_c_w_conv", first_shards)[:, 0, :]
    gathers = [_gather_weights_start(0, *first_shards, [all_small], relayed=True)]
    c_all = all_small[:, :D_MODEL]
    w_conv_full = all_small[:, D_MODEL:].reshape(N_DEV, DEPTH, 3, conv_shard).transpose(1, 2, 0, 3).reshape(
        DEPTH, 3, CONV_W)
    cps = jnp.concatenate([w_conv_full, pool_scale[:, None], jnp.zeros((DEPTH, 4, CONV_W), F32)], axis=1)

    c_act, pieces = _modulation_columns(c_all, w_ada)
    upper = (1, DEPTH)
    upper_shards = [_to_bf16(w_in, "cast_w_in_1", upper), _to_bf16(w_out, "cast_w_out_1", upper)]
    wpool_b = _to_bf16(w_pool.reshape(DEPTH, POOL_ROWS, GROUP_D), "cast_w_pool").reshape(w_pool.shape)
    first_sems_0, _, (zones_0,) = gathers[0]
    neighbour_sems, zones_0 = _gather_weights_pass_on(
        "all_gather_weights_relay_0", NEIGHBOUR_CHIPS, first_sems_0[1], partial(_first_arrival, 0), zones_0,
        [pieces, *upper_shards, wpool_b, cps], relay=True)
    gather_mod = _all_gather_exchange(pieces)
    sems_m, pieces_thru, mod_landing, token_m = _start_exchange(gather_mod, "all_gather_modulation_start", [zones_0[0]])
    diagonal_sems, zones_0 = _gather_weights_pass_on(
        "all_gather_weights_pass_on_0", DIAGONAL_CHIP, neighbour_sems[3], lambda a, j: a, zones_0, [token_m])
    _, (mod_all,) = _finish_exchange(
        gather_mod, "all_gather_modulation_finish", sems_m, pieces_thru, mod_landing, [zones_0[0]])
    mod_mine = lax.dynamic_index_in_dim(mod_all, me, axis=1, keepdims=False)
    mod = mod_mine.reshape(N_DEV, DEPTH, W_IN_SHARD).transpose(1, 0, 2).reshape(DEPTH, 3 * D_MODEL) + b_ada
    zeros_d = jnp.zeros((DEPTH, 3, D_MODEL), F32)
    vec = jnp.concatenate([mod.reshape(DEPTH, 3, D_MODEL), g_pre[:, None], g_post[:, None], zeros_d], axis=1)

    gathers.append(_gather_weights_start(1, *upper_shards, [mod_all]))
    gathers = [(first_sems, shards, landing[k], k) for first_sems, shards, landing in gathers
               for k in range(len(landing))]

    xs, kept, wins, wouts = [x0], [], [], []
    for l in range(DEPTH):
        first_sems, shards, zones, index = gathers[l]
        if l == 0:
            passed = [(neighbour_sems[:2], NEIGHBOUR_CHIPS), (diagonal_sems, DIAGONAL_CHIP)]
            win, wout = _gather_weights_finish(
                l, index, first_sems, passed, shards, zones_0, neighbour_sems[2], [gathers[-1][1][0]])
        else:
            passed_sems, zones = _gather_weights_pass_on(
                f"all_gather_weights_pass_on_{l}", ALL_OTHER_CHIPS, first_sems[1], partial(_first_arrival, index),
                zones, [xs[-1]])
            win, wout = _gather_weights_finish(l, index, first_sems, [(passed_sems, ALL_OTHER_CHIPS)], shards, zones)
        x_next, *for_backward = _forward_layer(
            l, xs[-1], vec, cps, wpool_b, win, wout, target if l == DEPTH - 1 else None)
        xs.append(x_next)
        kept.append(for_backward[:6])
        wins.append(win)
        wouts.append(wout)
    dx, loss_tile = xs[DEPTH], for_backward[6]

    slab_rows = [None] * DEPTH
    partials = received = None
    in_flight = []

    def scatter(layer, grads, from_sibling, after):
        nonlocal partials, received
        partials = _add_sibling_blocks(
            f"grad_add_sibling_{layer}", layer, grads, from_sibling, core, partials, ALL_KINDS)
        chips = _chips_exchange(layer, partials, received, ALL_KINDS)
        sems, partials, received, token = _start_exchange(chips, f"grad_chips_start_{layer}", after)
        in_flight.append((chips, sems, layer))
        return token

    grads_above = None
    issued = []
    for l in reversed(range(DEPTH)):
        y, h, ycat, uc, fa, fp = kept[l]
        dx, dproj, dy, gwpool, dcps, dvec = _backward_layer(
            l, dx, y, xs[l], uc, fa, fp, vec, cps, wpool_b, wouts[l], wins[l], issued)
        slab_rows[l] = jnp.concatenate(
            [dvec[0], dvec[1], dvec[2], dvec[3], dvec[4], dcps[3], dcps[0], dcps[1], dcps[2],
             loss_tile[0] if l == 0 else jnp.zeros((LANES,), F32)])
        hosted = _sibling_exchange(grads_above, ALL_KINDS) if grads_above is not None else None
        if l == 0:
            hosted = _all_gather_exchange(jnp.stack(slab_rows))
        gwin, gwout, *from_sibling = _weight_grads(l, h, dproj, ycat, dy, hosted)
        if l == 0:
            (slabs,) = from_sibling
        issued = [gwin]
        if grads_above is not None:
            issued = [scatter(l + 1, grads_above, from_sibling, [])]
        grads_above = [gwin, gwout, gwpool.reshape(POOL_ROWS, GROUP_D)]
        if l <= 1:
            from_sibling = _run_exchange(_sibling_exchange(grads_above, ALL_KINDS), f"grad_exchange_sibling_{l}")
            token = scatter(l, grads_above, from_sibling, [slabs] if l == 0 else [])
            issued = [token]
            grads_above = None
    grad_x = dx[None]
    chips_0, sems_0, _ = in_flight.pop()

    o = 3 * D_MODEL

    after = [token]
    for chips, sems, l in in_flight:
        partials, received = _finish_exchange(chips, f"grad_chips_finish_{l}", sems, partials, received, after)
        after = []
    upper = (1, DEPTH)
    w_in_upper = _adamw_reduced(
        "adamw_w_in_upper", w_in, m_w_in, v_w_in, partials[0], received[0], chip, ROW_TILE, upper, None)
    w_out_upper = _adamw_reduced(
        "adamw_w_out_upper", w_out, m_w_out, v_w_out, partials[1], received[1], chip, W_OUT_SHARD, upper, None)
    dmod_all = slabs[:, :, :o].reshape(N_DEV, DEPTH, N_DEV, W_IN_SHARD)
    dmod_cols = lax.dynamic_index_in_dim(dmod_all, me, axis=2, keepdims=False).transpose(1, 0, 2) + token[0, 0]
    g_w_ada, d_w_ada, nm_w_ada, nv_w_ada = _adamw_w_ada(w_ada, m_w_ada, v_w_ada, c_act.T, dmod_cols)

    partials, received = _finish_exchange(
        chips_0, "grad_chips_finish_0", sems_0, partials, received, [nv_w_ada, w_in_upper[3], w_out_upper[3]])
    gather_pool = _all_gather_exchange(_reduce_w_pool(partials[2], received[2], chip), axis=1)
    sems_p, pool_rows, pool_landing, token_p = _start_exchange(gather_pool, "all_gather_grad_w_pool_start")
    g_w_in, d_w_in, nm_w_in, nv_w_in = _adamw_reduced(
        "adamw_w_in_0", w_in, m_w_in, v_w_in, partials[0], received[0], chip, ROW_TILE, (0, 1), w_in_upper, [token_p])
    g_w_out, d_w_out, nm_w_out, nv_w_out = _adamw_reduced(
        "adamw_w_out_0", w_out, m_w_out, v_w_out, partials[1], received[1], chip, W_OUT_SHARD, (0, 1), w_out_upper,
        [token_p])
    ends = [0, o, o + D_MODEL, o + 2 * D_MODEL, o + 2 * D_MODEL + POOL_W, SLAB_COLS]
    g_b_ada, g_g_pre, g_g_post, g_pool_scale, g_conv, loss = _sum_sources(
        slabs, list(zip(ends[:-1], ends[1:])), [token_p, nv_w_in, nv_w_out])
    loss = loss.reshape(())
    g_w_conv = lax.dynamic_slice_in_dim(g_conv.reshape(DEPTH, 3, CONV_W), me * conv_shard, conv_shard, axis=2)
    taps_first = lambda a: a.transpose(1, 0, 2)
    small = _adamw_small("adamw_small", [
        (b_ada, g_b_ada, m_b_ada, v_b_ada),
        (g_pre, g_g_pre, m_g_pre, v_g_pre),
        tuple(taps_first(a) for a in (w_conv, g_w_conv, m_w_conv, v_w_conv)),
        (pool_scale, g_pool_scale, m_pool_scale, v_pool_scale),
        (g_post, g_g_post, m_g_post, v_g_post),
    ])
    (d_b_ada, nm_b_ada, nv_b_ada, g_b_ada), (d_g_pre, nm_g_pre, nv_g_pre, g_g_pre), conv_steps, \
        (d_ps, nm_ps, nv_ps, g_pool_scale), (d_g_post, nm_g_post, nv_g_post, g_g_post) = small
    d_w_conv, nm_w_conv, nv_w_conv, g_w_conv = (taps_first(a) for a in conv_steps)
    _, (g_pool_all,) = _finish_exchange(
        gather_pool, "all_gather_grad_w_pool_finish", sems_p, pool_rows, pool_landing, [nv_w_in, nv_w_out, nv_g_post])
    ((d_w_pool, nm_w_pool, nv_w_pool, g_w_pool),) = _adamw_small(
        "adamw_w_pool", [(w_pool, g_pool_all.reshape(w_pool.shape), m_w_pool, v_w_pool)])

    return (loss, grad_x,
            g_w_ada, g_b_ada, g_g_pre, g_w_in, g_w_conv, g_w_pool, g_pool_scale, g_w_out, g_g_post,
            d_w_ada, d_b_ada, d_g_pre, d_w_in, d_w_conv, d_w_pool, d_ps, d_w_out, d_g_post,
            nm_w_ada, nm_b_ada, nm_g_pre, nm_w_in, nm_w_conv, nm_w_pool, nm_ps, nm_w_out, nm_g_post,
            nv_w_ada, nv_b_ada, nv_g_pre, nv_w_in, nv_w_conv, nv_w_pool, nv_ps, nv_w_out, nv_g_post)
```

```python
from functools import partial

import jax
import jax.numpy as jnp
from jax import lax
from jax.experimental import pallas as pl
from jax.experimental.pallas import tpu as pltpu

F32 = jnp.float32
BF16 = jnp.bfloat16

D_MODEL = 1024
DEPTH = 4
CONV_W = 512
POOL_W = 512
POOL_WINDOWS = (2, 4, 8, 16)
GROUP_D = 128
IN_COLS = 4 * CONV_W + 2 * POOL_W
NORM_EPS = 1e-6

ADAM_LR = 0.001
ADAM_B1 = 0.9
ADAM_B2 = 0.999
ADAM_EPS = 1e-08
ADAM_WD = 0.01
ADAM_STEP = 10

N_DEV = 8
N_CHIP = 4
N_OTHER_CHIPS = N_CHIP - 1
MESH = pl.DeviceIdType.MESH
W_IN_SHARD = IN_COLS // N_DEV
W_OUT_SHARD = D_MODEL // N_DEV
POOL_ROWS = len(POOL_WINDOWS) * GROUP_D
POOL_SHARD = POOL_ROWS // N_DEV

SUBLANES = 8
LANES = 128
VMEM_LIMIT_BYTES = 56 * 1024 * 1024
ROW_TILE = 512
BWD_TILE = 256
RING_SLOTS = 3
GWIN_COLS = 768
GWOUT_COLS = 512
POOL_HALO = 16
CONV_HALO = SUBLANES

SLAB_COLS = 3 * D_MODEL + D_MODEL + D_MODEL + POOL_W + 3 * CONV_W

HBM = pl.BlockSpec(memory_space=pl.ANY)


def _params(**kw):
    return pltpu.CompilerParams(vmem_limit_bytes=VMEM_LIMIT_BYTES, **kw)


def _sigmoid(v):
    return 1.0 / (1.0 + jnp.exp(-v))


def _dot(a, b):
    return jnp.dot(a, b, preferred_element_type=F32)


def _dot_tn(a, b):
    return lax.dot_general(a, b, (((0,), (0,)), ((), ())), preferred_element_type=F32)


def _dot_nt(a, b):
    return lax.dot_general(a, b, (((1,), (1,)), ((), ())), preferred_element_type=F32)


def _rows_from_before(v, k):
    return pltpu.roll(v, k, 0)


def _rows_from_after(v, k):
    return pltpu.roll(v, v.shape[0] - k, 0)


def _window_counts(t0, rows):
    return (lax.broadcasted_iota(jnp.int32, (rows, 1), 0) + (t0 + 1)).astype(F32)


def _split_proj(p32):
    cw = CONV_W
    return (p32[:, 0 * cw:1 * cw], p32[:, 1 * cw:2 * cw], p32[:, 2 * cw:3 * cw], p32[:, 3 * cw:4 * cw],
            p32[:, 4 * cw:4 * cw + POOL_W], p32[:, 4 * cw + POOL_W:])


def _layer_spec(shape, layer):
    nd = len(shape)
    return pl.BlockSpec((None,) + tuple(shape[1:]), lambda i, _l=layer, _n=nd: (_l,) + (0,) * (_n - 1))


def _whole_spec(shape):
    return pl.BlockSpec(tuple(shape), lambda i, _n=len(shape): (0,) * _n, pipeline_mode=pl.Buffered(1))


def _mesh_position():
    return lax.axis_index("x"), lax.axis_index("y"), lax.axis_index("c")


def _block_id(x, y, c):
    return 4 * x + 2 * y + c


def _other_chips(x, y):
    return [(x ^ 1, y), (x, y ^ 1), (x ^ 1, y ^ 1)]


def _col_block(ref, blk):
    return ref.at[:, pl.ds(pl.multiple_of(blk * W_IN_SHARD, LANES), W_IN_SHARD)]


def _row_block(rows):
    def block(ref, blk):
        return ref.at[pl.ds(pl.multiple_of(blk * rows, rows), rows), :]
    return block


_BLOCK_OF = (_col_block, _row_block(W_OUT_SHARD), _row_block(POOL_SHARD))
_BLOCK_SHAPES = ((D_MODEL, W_IN_SHARD), (W_OUT_SHARD, D_MODEL), (POOL_SHARD, GROUP_D))


class _Exchange:
    def __init__(self, inputs, out_shapes, aliases, sem_shapes, make):
        self.inputs, self.out_shapes, self.aliases, self.sem_shapes, self.make = (
            list(inputs), list(out_shapes), dict(aliases), list(sem_shapes), make)


def _run_exchange(exchange, name):
    n_in, n_out = len(exchange.inputs), len(exchange.out_shapes)

    def body(*refs):
        start, finish = exchange.make(refs[:n_in], refs[n_in:n_in + n_out], refs[n_in + n_out:])
        start()
        finish()

    return pl.pallas_call(
        body, name=name, in_specs=[HBM] * n_in, out_specs=[HBM] * n_out, out_shape=exchange.out_shapes,
        scratch_shapes=exchange.sem_shapes, input_output_aliases=exchange.aliases, compiler_params=_params(),
    )(*exchange.inputs)


_SEM = pl.BlockSpec(memory_space=pltpu.SEMAPHORE)
_DATAFLOW = pltpu.SideEffectType.DATAFLOW_SIDE_EFFECTING


def _start_exchange(exchange, name, after=()):
    n_in, n_out, n_sem = len(exchange.inputs), len(exchange.out_shapes), len(exchange.sem_shapes)
    sources = [i for i in range(n_in) if i not in exchange.aliases]
    aliases = {i: n_sem + k for k, i in enumerate(sources)}
    aliases.update({i: n_sem + len(sources) + o for i, o in exchange.aliases.items()})

    def body(*refs):
        in_refs = refs[:n_in]
        outs = refs[n_in + len(after):]
        sems = outs[:n_sem]
        out_refs = outs[n_sem + len(sources):n_sem + len(sources) + n_out]
        exchange.make(in_refs, out_refs, sems)[0]()
        refs[-1][...] = jnp.zeros_like(refs[-1])

    outs = pl.pallas_call(
        body, name=name, in_specs=[HBM] * (n_in + len(after)),
        out_specs=[_SEM] * n_sem + [HBM] * (len(sources) + n_out) + [pl.BlockSpec(memory_space=pltpu.VMEM)],
        out_shape=(exchange.sem_shapes + [pltpu.HBM(exchange.inputs[i].shape, exchange.inputs[i].dtype) for i in sources]
                   + [pltpu.HBM(s.shape, s.dtype) for s in exchange.out_shapes]
                   + [jax.ShapeDtypeStruct((SUBLANES, LANES), F32)]),
        input_output_aliases=aliases, compiler_params=_params(has_side_effects=_DATAFLOW),
    )(*exchange.inputs, *after)
    return outs[:n_sem], outs[n_sem:n_sem + len(sources)], outs[n_sem + len(sources):-1], outs[-1]


def _finish_exchange(exchange, name, sems, sources, landing, after):
    n_src, n_out, n_sem = len(sources), len(landing), len(sems)
    n_in = len(exchange.inputs)
    source_at = [i for i in range(n_in) if i not in exchange.aliases]

    def body(*refs):
        src_refs, out_refs = refs[:n_src], refs[n_src:n_src + n_out]
        sem_refs = refs[n_src + n_out:n_src + n_out + n_sem]
        in_refs = [None] * n_in
        for k, i in enumerate(source_at):
            in_refs[i] = src_refs[k]
        for i, o in exchange.aliases.items():
            in_refs[i] = out_refs[o]
        exchange.make(in_refs, out_refs, sem_refs)[1]()

    arrays = list(sources) + list(landing)
    outs = pl.pallas_call(
        body, name=name, in_specs=[HBM] * len(arrays) + [_SEM] * n_sem + [HBM] * len(after),
        out_specs=[HBM] * len(arrays), out_shape=[pltpu.HBM(a.shape, a.dtype) for a in arrays],
        input_output_aliases={i: i for i in range(len(arrays))}, compiler_params=_params(has_side_effects=_DATAFLOW),
    )(*arrays, *sems, *after)
    return outs[:n_src], outs[n_src:]


N_GATHERED = 2
FIRST_COPIES = 1 + N_OTHER_CHIPS
_GATHERED_SHAPES = ((D_MODEL, IN_COLS), (D_MODEL, D_MODEL))
ALL_OTHER_CHIPS, NEIGHBOUR_CHIPS, DIAGONAL_CHIP = (0, 1, 2), (0, 1), (2,)


def _first_sem(layer, a, k):
    return (layer * N_GATHERED + a) * FIRST_COPIES + k


def _first_arrival(layer, a, j):
    return _first_sem(layer, a, 1 + j)


def _gather_copy(window_of, full_ref, blk, send_sem, recv_sem, to, src=None):
    window = window_of(full_ref, blk)
    return pltpu.make_async_remote_copy(
        src_ref=window if src is None else src, dst_ref=window, send_sem=send_sem, recv_sem=recv_sem,
        device_id=to, device_id_type=MESH)


def _first_copies(layer, shard_refs, full_refs, send_sems, recv_sems, local_sems, relayed):
    x, y, c = _mesh_position()
    me = _block_id(x, y, c)
    chips = [_other_chips(x, y)[j] for j in (NEIGHBOUR_CHIPS if relayed else ALL_OTHER_CHIPS)]
    own, remote = [], []
    for a in range(N_GATHERED):
        shard = shard_refs[a].at[layer]
        own.append(pltpu.make_async_copy(
            shard, _BLOCK_OF[a](full_refs[a], me), local_sems.at[layer * N_GATHERED + a]))
        targets = [(x, y, 1 - c)] + [(*chip, c) for chip in chips]
        remote += [_gather_copy(_BLOCK_OF[a], full_refs[a], me, send_sems.at[_first_sem(layer, a, k)],
                                recv_sems.at[_first_sem(layer, a, k)], to, src=shard)
                   for k, to in enumerate(targets)]
    return own, remote


def _gather_weights_start(first_layer, win_shards, wout_shards, after, relayed=False):
    n_layers = win_shards.shape[0]
    n_first = n_layers * N_GATHERED * FIRST_COPIES
    sem_shapes = [pltpu.SemaphoreType.DMA((n_first,)), pltpu.SemaphoreType.DMA((n_first,)),
                  pltpu.SemaphoreType.DMA((n_layers * N_GATHERED,))]
    shards = [win_shards, wout_shards]

    def body(win_sh, wout_sh, *rest):
        send_sems, recv_sems, local_sems, win_thru, wout_thru, *landing = rest[len(after):]
        for layer in range(n_layers):
            own, remote = _first_copies(layer, (win_sh, wout_sh), landing[N_GATHERED * layer:N_GATHERED * (layer + 1)],
                                        send_sems, recv_sems, local_sems, relayed)
            for cp in own + remote:
                cp.start()

    outs = pl.pallas_call(
        body, name=f"all_gather_weights_start_{first_layer}", in_specs=[HBM] * (2 + len(after)),
        out_specs=[_SEM] * 3 + [HBM] * (2 + n_layers * N_GATHERED),
        out_shape=(sem_shapes + [pltpu.HBM(s.shape, s.dtype) for s in shards]
                   + [pltpu.HBM(s, BF16) for _ in range(n_layers) for s in _GATHERED_SHAPES]),
        input_output_aliases={0: 3, 1: 4}, compiler_params=_params(has_side_effects=_DATAFLOW),
    )(*shards, *after)
    landing = outs[5:]
    return outs[:3], outs[3:5], [landing[N_GATHERED * l:N_GATHERED * (l + 1)] for l in range(n_layers)]


def _passed_on_copies(full_refs, send_sems, recv_sems, core_of_block, chips):
    x, y, c = _mesh_position()
    return [_gather_copy(_BLOCK_OF[a], full_refs[a], _block_id(*_other_chips(x, y)[j], core_of_block),
                         send_sems.at[a * N_OTHER_CHIPS + j], recv_sems.at[a * N_OTHER_CHIPS + j], (x, y, 1 - c))
            for a in range(N_GATHERED) for j in chips]


def _relayed_copies(full_refs, send_sems, recv_sems):
    x, y, c = _mesh_position()
    source, to = (x ^ (1 - c), y ^ c), (x ^ c, y ^ (1 - c))
    return [_gather_copy(_BLOCK_OF[a], full_refs[a], _block_id(*source, c), send_sems.at[a], recv_sems.at[a], (*to, c))
            for a in range(N_GATHERED)]


def _gather_weights_pass_on(name, chips, arrival_sems, arrival_sem_of, landing, after, relay=False):
    n = N_GATHERED * N_OTHER_CHIPS
    sem_shapes = [pltpu.SemaphoreType.DMA((n,))] * 2 + [pltpu.SemaphoreType.DMA((N_GATHERED,))] * (2 if relay else 0)

    def body(win_ref, wout_ref, arrivals, *rest):
        sems = rest[len(after):len(after) + len(sem_shapes)]
        x, y, c = _mesh_position()
        full_refs = (win_ref, wout_ref)
        passed = _passed_on_copies(full_refs, sems[0], sems[1], c, chips)
        relayed = _relayed_copies(full_refs, sems[2], sems[3]) if relay else []
        for a in range(N_GATHERED):
            for j in chips:
                sem = arrivals.at[arrival_sem_of(a, j)]
                _gather_copy(_BLOCK_OF[a], full_refs[a], _block_id(*_other_chips(x, y)[j], c), sem, sem,
                             (x, y, c)).wait_recv()
            for cp in relayed[a:a + 1] + passed[a * len(chips):(a + 1) * len(chips)]:
                cp.start()

    outs = pl.pallas_call(
        body, name=name, in_specs=[HBM] * N_GATHERED + [_SEM] + [HBM] * len(after),
        out_specs=[_SEM] * len(sem_shapes) + [HBM] * N_GATHERED,
        out_shape=sem_shapes + [pltpu.HBM(a.shape, a.dtype) for a in landing],
        input_output_aliases={a: len(sem_shapes) + a for a in range(N_GATHERED)},
        compiler_params=_params(has_side_effects=_DATAFLOW),
    )(*landing, arrival_sems, *after)
    return outs[:len(sem_shapes)], outs[len(sem_shapes):]


def _gather_weights_finish(layer, index, first_sems, passed, shards, landing, relay_send_sems=None, after=()):
    relayed = relay_send_sems is not None
    passed_sems = [sem for (send, recv), _ in passed for sem in (send, recv)] + ([relay_send_sems] if relayed else [])

    def body(win_ref, wout_ref, first_send, first_recv, local_sems, *rest):
        sems, (win_sh, wout_sh) = rest[:len(passed_sems)], rest[len(passed_sems):len(passed_sems) + 2]
        x, y, c = _mesh_position()
        full_refs = (win_ref, wout_ref)
        own, sent = _first_copies(index, (win_sh, wout_sh), full_refs, first_send, first_recv, local_sems, relayed)
        for a in range(N_GATHERED):
            sem = _first_sem(index, a, 0)
            _gather_copy(_BLOCK_OF[a], full_refs[a], _block_id(x, y, 1 - c), first_recv.at[sem], first_recv.at[sem],
                         (x, y, c)).wait_recv()
        for p, (_, chips) in enumerate(passed):
            for cp in _passed_on_copies(full_refs, sems[2 * p], sems[2 * p + 1], 1 - c, chips):
                cp.wait_recv()
            sent += _passed_on_copies(full_refs, sems[2 * p], sems[2 * p + 1], c, chips)
        if relayed:
            sent += _relayed_copies(full_refs, sems[-1], sems[-1])
        for cp in sent:
            cp.wait_send()
        for cp in own:
            cp.wait()

    return pl.pallas_call(
        body, name=f"all_gather_weights_finish_{layer}",
        in_specs=[HBM] * N_GATHERED + [_SEM] * (3 + len(passed_sems)) + [HBM] * (2 + len(after)),
        out_specs=[HBM] * N_GATHERED, out_shape=[pltpu.HBM(a.shape, a.dtype) for a in landing],
        input_output_aliases={a: a for a in range(N_GATHERED)}, compiler_params=_params(has_side_effects=_DATAFLOW),
    )(*landing, *first_sems, *passed_sems, *shards, *after)


ALL_KINDS = (0, 1, 2)


def _sibling_exchange(grads, kinds):
    n_arr = len(grads)

    def make(in_refs, out_refs, sems):
        send_sems, recv_sems = sems
        x, y, c = _mesh_position()
        copies = [pltpu.make_async_remote_copy(
            src_ref=_BLOCK_OF[kinds[a]](in_refs[a], 2 * q + (1 - c)), dst_ref=out_refs[a].at[q],
            send_sem=send_sems.at[a * N_CHIP + q], recv_sem=recv_sems.at[a * N_CHIP + q],
            device_id=(x, y, 1 - c), device_id_type=MESH)
            for a in range(n_arr) for q in range(N_CHIP)]

        def start():
            for cp in copies:
                cp.start()

        def finish():
            for cp in copies:
                cp.wait_recv()
            for cp in copies:
                cp.wait_send()

        return start, finish

    return _Exchange(
        grads, [jax.ShapeDtypeStruct((N_CHIP,) + _BLOCK_SHAPES[k], BF16) for k in kinds], {},
        [pltpu.SemaphoreType.DMA((n_arr * N_CHIP,)), pltpu.SemaphoreType.DMA((n_arr * N_CHIP,))], make)


def _chips_exchange(layer, partials, received, kinds):
    n_arr = len(partials)

    def make(in_refs, out_refs, sems):
        send_sems, recv_sems = sems
        x, y, c = _mesh_position()
        copies = [pltpu.make_async_remote_copy(
            src_ref=in_refs[a].at[2 * qx + qy, layer], dst_ref=out_refs[a].at[j, layer],
            send_sem=send_sems.at[a * N_OTHER_CHIPS + j], recv_sem=recv_sems.at[a * N_OTHER_CHIPS + j],
            device_id=(qx, qy, c), device_id_type=MESH)
            for a in range(n_arr) for j, (qx, qy) in enumerate(_other_chips(x, y))]

        def start():
            for cp in copies:
                cp.start()

        def finish():
            for cp in copies:
                cp.wait_recv()
            for cp in copies:
                cp.wait_send()

        return start, finish

    inputs = list(partials)
    aliases = {}
    if received is not None:
        inputs += list(received)
        aliases = {n_arr + a: a for a in range(n_arr)}
    return _Exchange(
        inputs, [jax.ShapeDtypeStruct((N_OTHER_CHIPS, DEPTH) + _BLOCK_SHAPES[k], BF16) for k in kinds], aliases,
        [pltpu.SemaphoreType.DMA((n_arr * N_OTHER_CHIPS,)), pltpu.SemaphoreType.DMA((n_arr * N_OTHER_CHIPS,))], make)


def _all_gather_exchange(v, axis=0):
    def make(in_refs, out_refs, sems):
        send_sems, recv_sems, local_sem = sems
        x, y, c = _mesh_position()
        me = _block_id(x, y, c)
        block = lambda blk: out_refs[0].at[(slice(None),) * axis + (blk,)]
        own = pltpu.make_async_copy(in_refs[0], block(me), local_sem.at[0])
        sends, arrivals = [], []
        for k in range(1, N_DEV):
            px, py, pc = x ^ ((k >> 2) & 1), y ^ ((k >> 1) & 1), c ^ (k & 1)
            sends.append(pltpu.make_async_remote_copy(
                src_ref=in_refs[0], dst_ref=block(me), send_sem=send_sems.at[k - 1],
                recv_sem=recv_sems.at[k - 1], device_id=(px, py, pc), device_id_type=MESH))
            arrivals.append(pltpu.make_async_remote_copy(
                src_ref=in_refs[0], dst_ref=block(_block_id(px, py, pc)), send_sem=send_sems.at[k - 1],
                recv_sem=recv_sems.at[k - 1], device_id=(x, y, c), device_id_type=MESH))

        def start():
            for cp in [own] + sends:
                cp.start()

        def finish():
            for cp in arrivals:
                cp.wait_recv()
            for cp in sends:
                cp.wait_send()
            own.wait()

        return start, finish

    return _Exchange(
        [v], [jax.ShapeDtypeStruct(v.shape[:axis] + (N_DEV,) + v.shape[axis:], v.dtype)], {},
        [pltpu.SemaphoreType.DMA((N_DEV - 1,)), pltpu.SemaphoreType.DMA((N_DEV - 1,)),
         pltpu.SemaphoreType.DMA((1,))], make)


def _host(exchange, args, in_specs, out_shape, out_specs, scratch):
    n_own = (len(args), len(out_shape), len(scratch))
    if exchange is None:
        return {}, lambda refs: (refs, None)
    n_ex = (len(exchange.inputs), len(exchange.out_shapes), len(exchange.sem_shapes))
    aliases = {n_own[0] + i: n_own[1] + o for i, o in exchange.aliases.items()}
    args += exchange.inputs
    in_specs += [HBM] * n_ex[0]
    out_shape += exchange.out_shapes
    out_specs += [HBM] * n_ex[1]
    scratch += exchange.sem_shapes

    def split(refs):
        own, theirs, at = [], [], 0
        for mine, ex in zip(n_own, n_ex):
            own += refs[at:at + mine]
            theirs.append(refs[at + mine:at + mine + ex])
            at += mine + ex
        return own, exchange.make(*theirs)

    return aliases, split


def _all_gather_small(v, name, after=()):
    vmem = pl.BlockSpec(memory_space=pltpu.VMEM)

    def body(v_ref, *rest):
        out_ref, send_sems, recv_sems = rest[len(after):]
        x, y, c = _mesh_position()
        me = _block_id(x, y, c)
        out_ref[me] = v_ref[...]
        sends = []
        for k in range(1, N_DEV):
            px, py, pc = x ^ ((k >> 2) & 1), y ^ ((k >> 1) & 1), c ^ (k & 1)
            send = pltpu.make_async_remote_copy(
                src_ref=v_ref, dst_ref=out_ref.at[me], send_sem=send_sems.at[k - 1], recv_sem=recv_sems.at[k - 1],
                device_id=(px, py, pc), device_id_type=MESH)
            send.start()
            sends.append((send, _block_id(px, py, pc)))
        for k, (send, peer) in enumerate(sends):
            pltpu.make_async_remote_copy(
                src_ref=v_ref, dst_ref=out_ref.at[peer], send_sem=send_sems.at[k], recv_sem=recv_sems.at[k],
                device_id=(x, y, c), device_id_type=MESH).wait_recv()
        for send, _ in sends:
            send.wait_send()

    return pl.pallas_call(
        body, name=name, in_specs=[vmem] + [HBM] * len(after), out_specs=vmem,
        out_shape=jax.ShapeDtypeStruct((N_DEV,) + v.shape, v.dtype),
        scratch_shapes=[pltpu.SemaphoreType.DMA((N_DEV - 1,)), pltpu.SemaphoreType.DMA((N_DEV - 1,))],
        compiler_params=_params(),
    )(v, *after)


def _forward_layer(layer, x, vec, cps, wpool, win, wout, target=None):
    t_len = x.shape[0]
    n_tiles = t_len // ROW_TILE
    row = lambda cols: pl.BlockSpec((ROW_TILE, cols), lambda i: (i, 0))
    widths = (D_MODEL, 2 * CONV_W, 3 * CONV_W, 4 * POOL_W)
    head = target is not None

    def body(x_ref, vec_ref, cps_ref, wpool_ref, win_ref, wout_ref, *rest):
        target_ref = rest[0] if head else None
        xo_ref, y_ref, h_ref, ycat_ref, uc_ref, fa_ref, fp_ref = rest[head:head + 7]
        loss_ref = rest[head + 7] if head else None
        zc_ref, pc_ref = rest[-2:]
        i = pl.program_id(0)

        @pl.when(i == 0)
        def _():
            zc_ref[...] = jnp.zeros_like(zc_ref)
            pc_ref[...] = jnp.zeros_like(pc_ref)
            if head:
                loss_ref[...] = jnp.zeros_like(loss_ref)

        x_t = x_ref[...]
        shift, scale, gate = vec_ref[0:1, :], vec_ref[1:2, :], vec_ref[2:3, :]
        g_pre, g_post = vec_ref[3:4, :], vec_ref[4:5, :]
        w0, w1, w2, ps = cps_ref[0:1, :], cps_ref[1:2, :], cps_ref[2:3, :], cps_ref[3:4, :]
        rx = lax.rsqrt(jnp.mean(x_t * x_t, axis=-1, keepdims=True) + NORM_EPS)
        h = (x_t * rx) * g_pre * (1.0 + scale) + shift
        h_ref[...] = h.astype(BF16)
        proj = _dot(h.astype(BF16), win_ref[...])
        u_a, b_a, c_a, g_a, u_p, g_p = _split_proj(proj)
        uc_ref[...] = jnp.concatenate([u_a, c_a], axis=1).astype(BF16)

        z = c_a * u_a
        zcat = jnp.concatenate([zc_ref[...], z], axis=0)
        zc_ref[...] = z[ROW_TILE - CONV_HALO:]
        conv = (w0 * _rows_from_before(zcat, 2)[CONV_HALO:] + w1 * _rows_from_before(zcat, 1)[CONV_HALO:] + w2 * z)
        sig_a = _sigmoid(g_a)
        silu_a = g_a * sig_a
        b_conv = b_a * conv
        y_a = b_conv * silu_a
        fa_ref[...] = jnp.concatenate(
            [silu_a * conv, silu_a * b_a, b_conv * (sig_a + silu_a * (1.0 - sig_a))], axis=1).astype(BF16)

        pcat = jnp.concatenate([pc_ref[...], u_p], axis=0)
        pc_ref[...] = u_p[ROW_TILE - POOL_HALO:]
        counts = _window_counts(i * ROW_TILE, ROW_TILE)
        pooled, mixed = [], []
        for g, w in enumerate(POOL_WINDOWS):
            cols = slice(g * GROUP_D, (g + 1) * GROUP_D)
            s = pcat[:, cols]
            step = 1
            while step < w:
                s = s + _rows_from_before(s, step)
                step *= 2
            pooled_g = (s[POOL_HALO:] * (1.0 / jnp.minimum(counts, float(w))) - u_p[:, cols]).astype(BF16)
            pooled.append(pooled_g)
            mixed.append(_dot(pooled_g, wpool_ref[g]))
        mixed = jnp.concatenate(mixed, axis=1)
        sig_p = _sigmoid(g_p)
        silu_p = g_p * sig_p
        mixed_ps = mixed * ps
        y_p = mixed_ps * silu_p
        fp_ref[...] = jnp.concatenate(
            [(ps * silu_p).astype(BF16), (mixed_ps * (sig_p + silu_p * (1.0 - sig_p))).astype(BF16),
             (silu_p * mixed).astype(BF16)] + pooled, axis=1)

        ycat = jnp.concatenate([y_a, y_p], axis=1)
        ycat_ref[...] = ycat.astype(BF16)
        y_b = _dot(ycat.astype(BF16), wout_ref[...]).astype(BF16)
        y_ref[...] = y_b
        y_t = y_b.astype(F32)
        ry = lax.rsqrt(jnp.mean(y_t * y_t, axis=-1, keepdims=True) + NORM_EPS)
        x_next = x_t + gate * (y_t * ry * g_post)
        if head:
            err = x_next - target_ref[...]
            xo_ref[...] = err * (1.0 / D_MODEL)
            loss_ref[...] += jnp.sum(err * err) * (0.5 / D_MODEL)
        else:
            xo_ref[...] = x_next

    tile = (SUBLANES, LANES)
    return pl.pallas_call(
        body, name=f"forward_layer_{layer}", grid=(n_tiles,),
        in_specs=[row(D_MODEL), _layer_spec(vec.shape, layer), _layer_spec(cps.shape, layer),
                  _layer_spec(wpool.shape, layer), _whole_spec(win.shape), _whole_spec(wout.shape)]
        + [row(D_MODEL)] * head,
        out_specs=[row(D_MODEL), row(D_MODEL), row(D_MODEL), row(D_MODEL)] + [row(w) for w in widths[1:]]
        + [_whole_spec(tile)] * head,
        out_shape=[jax.ShapeDtypeStruct((t_len, D_MODEL), F32), jax.ShapeDtypeStruct((t_len, D_MODEL), BF16),
                   jax.ShapeDtypeStruct((t_len, D_MODEL), BF16), jax.ShapeDtypeStruct((t_len, D_MODEL), BF16)]
        + [jax.ShapeDtypeStruct((t_len, w), BF16) for w in widths[1:]] + [jax.ShapeDtypeStruct(tile, F32)] * head,
        scratch_shapes=[pltpu.VMEM((CONV_HALO, CONV_W), F32), pltpu.VMEM((POOL_HALO, POOL_W), F32)],
        compiler_params=_params(dimension_semantics=("arbitrary",)),
    )(x, vec, cps, wpool, win, wout, *([target] * head))


def _backward_layer(layer, dxo, y, x, uc, fa, fp, vec, cps, wpool, wout, win, after):
    t_len = dxo.shape[0]
    n_tiles = t_len // BWD_TILE
    halo_per_tile = BWD_TILE // POOL_HALO
    rev = lambda cols: pl.BlockSpec((BWD_TILE, cols), lambda i: (n_tiles - 1 - i, 0))
    halo_spec = pl.BlockSpec(
        (POOL_HALO, 2 * CONV_W), lambda i: (jnp.maximum((n_tiles - 1 - i) * halo_per_tile - 1, 0), 0))
    gwpool_shape = (len(POOL_WINDOWS), GROUP_D, GROUP_D)

    ringed = (dxo, y, x)

    def body(dxo_hbm, y_hbm, x_hbm, uc_ref, uch_ref, fa_ref, fp_ref, vec_ref, cps_ref, wpool_ref, wout_ref, win_ref,
             *rest):
        (dx_ref, dproj_ref, dy_ref, gwpool_ref, dcps_ref, dvec_ref, gwpool_acc, dcc_ref, qc_ref,
         dxo_ring, y_ring, x_ring, ring_sems) = rest[len(after):]
        i = pl.program_id(0)
        tile = n_tiles - 1 - i

        def fetches(step):
            rows = pl.ds(pl.multiple_of((n_tiles - 1 - step) * BWD_TILE, BWD_TILE), BWD_TILE)
            slot = step % RING_SLOTS
            return [pltpu.make_async_copy(src.at[rows, :], ring.at[slot], ring_sems.at[k * RING_SLOTS + slot])
                    for k, (src, ring) in enumerate(((dxo_hbm, dxo_ring), (y_hbm, y_ring), (x_hbm, x_ring)))]

        @pl.when(i == 0)
        def _():
            for cp in fetches(0) + fetches(1):
                cp.start()

        @pl.when(i + 2 < n_tiles)
        def _():
            for cp in fetches(i + 2):
                cp.start()

        for cp in fetches(i):
            cp.wait()
        slot = i % RING_SLOTS
        dxo_ref, y_ref, x_ref = dxo_ring.at[slot], y_ring.at[slot], x_ring.at[slot]

        @pl.when(i == 0)
        def _():
            gwpool_acc[...] = jnp.zeros_like(gwpool_acc)
            dcps_ref[...] = jnp.zeros_like(dcps_ref)
            dvec_ref[...] = jnp.zeros_like(dvec_ref)
            dcc_ref[...] = jnp.zeros_like(dcc_ref)
            qc_ref[...] = jnp.zeros_like(qc_ref)

        shift, scale, gate = vec_ref[0:1, :], vec_ref[1:2, :], vec_ref[2:3, :]
        g_pre, g_post = vec_ref[3:4, :], vec_ref[4:5, :]
        w0, w1, w2 = cps_ref[0:1, :], cps_ref[1:2, :], cps_ref[2:3, :]

        dxo_t = dxo_ref[...]
        y_t = y_ref[...].astype(F32)
        ry = lax.rsqrt(jnp.mean(y_t * y_t, axis=-1, keepdims=True) + NORM_EPS)
        yh = y_t * ry
        dvec_ref[2:3, :] += jnp.sum(dxo_t * yh, axis=0, keepdims=True)
        dyh = dxo_t * (gate * g_post)
        dy_b = (ry * (dyh - yh * jnp.mean(dyh * yh, axis=-1, keepdims=True))).astype(BF16)
        dy_ref[...] = dy_b
        dycat = _dot_nt(dy_b, wout_ref[...])
        dy_a, dy_p = dycat[:, :CONV_W], dycat[:, CONV_W:]

        fa_t = fa_ref[...].astype(F32)
        db_a = dy_a * fa_t[:, :CONV_W]
        dconv = dy_a * fa_t[:, CONV_W:2 * CONV_W]
        dg_a = dy_a * fa_t[:, 2 * CONV_W:]
        uc_t = uc_ref[...].astype(F32)
        u_a, c_a = uc_t[:, :CONV_W], uc_t[:, CONV_W:]
        halo = jnp.where(tile > 0, uch_ref[...].astype(F32), 0.0)[POOL_HALO - CONV_HALO:]
        z = c_a * u_a
        zcat = jnp.concatenate([halo[:, CONV_W:] * halo[:, :CONV_W], z], axis=0)
        z1 = _rows_from_before(zcat, 1)[CONV_HALO:]
        z2 = _rows_from_before(zcat, 2)[CONV_HALO:]
        dccat = jnp.concatenate([dconv, dcc_ref[...]], axis=0)
        dc1 = _rows_from_after(dccat, 1)[:BWD_TILE]
        dc2 = _rows_from_after(dccat, 2)[:BWD_TILE]
        dz = w2 * dconv + w1 * dc1 + w0 * dc2
        dcc_ref[...] = dconv[:CONV_HALO]
        dcps_ref[0:1, :] += jnp.sum(dconv * z2, axis=0, keepdims=True)
        dcps_ref[1:2, :] += jnp.sum(dconv * z1, axis=0, keepdims=True)
        dcps_ref[2:3, :] += jnp.sum(dconv * z, axis=0, keepdims=True)
        du_a = dz * c_a
        dc_a = dz * u_a

        dmixed = (dy_p * fp_ref[:, :POOL_W].astype(F32)).astype(BF16)
        dg_p = dy_p * fp_ref[:, POOL_W:2 * POOL_W].astype(F32)
        dcps_ref[3:4, :] += jnp.sum(dy_p * fp_ref[:, 2 * POOL_W:3 * POOL_W].astype(F32), axis=0, keepdims=True)
        counts = _window_counts(tile * BWD_TILE, BWD_TILE)
        du_p, q_head = [], []
        for g, w in enumerate(POOL_WINDOWS):
            cols = slice(g * GROUP_D, (g + 1) * GROUP_D)
            dm_g = dmixed[:, cols]
            dpooled_g = _dot_nt(dm_g, wpool_ref[g])
            gwpool_acc[g] += _dot_tn(fp_ref[:, 3 * POOL_W + g * GROUP_D:3 * POOL_W + (g + 1) * GROUP_D], dm_g)
            q_g = dpooled_g * (1.0 / jnp.minimum(counts, float(w)))
            q_head.append(q_g[:POOL_HALO])
            s = jnp.concatenate([q_g, qc_ref[:, cols]], axis=0)
            step = 1
            while step < w:
                s = s + _rows_from_after(s, step)
                step *= 2
            du_p.append(s[:BWD_TILE] - dpooled_g)
        qc_ref[...] = jnp.concatenate(q_head, axis=1)
        dproj_b = jnp.concatenate([du_a, db_a, dc_a, dg_a] + du_p + [dg_p], axis=1).astype(BF16)
        dproj_ref[...] = dproj_b

        x_t = x_ref[...]
        rx = lax.rsqrt(jnp.mean(x_t * x_t, axis=-1, keepdims=True) + NORM_EPS)
        xn = x_t * rx
        mod_scale = 1.0 + scale
        dh = _dot_nt(dproj_b, win_ref[...])
        dvec_ref[0:1, :] += jnp.sum(dh, axis=0, keepdims=True)
        dvec_ref[1:2, :] += jnp.sum(dh * xn, axis=0, keepdims=True)
        dxn = dh * (g_pre * mod_scale)
        dx_ref[...] = dxo_t + rx * (dxn - xn * jnp.mean(dxn * xn, axis=-1, keepdims=True))

        @pl.when(i == n_tiles - 1)
        def _():
            gwpool_ref[...] = gwpool_acc[...].astype(BF16)
            sum_dh_xn, sum_dxo_yh = dvec_ref[1:2, :], dvec_ref[2:3, :]
            dvec_ref[1:2, :] = sum_dh_xn * g_pre
            dvec_ref[3:4, :] = sum_dh_xn * mod_scale
            dvec_ref[2:3, :] = sum_dxo_yh * g_post
            dvec_ref[4:5, :] = sum_dxo_yh * gate

    return pl.pallas_call(
        body, name=f"backward_layer_{layer}", grid=(n_tiles,),
        in_specs=[HBM, HBM, HBM, rev(2 * CONV_W), halo_spec, rev(3 * CONV_W),
                  rev(4 * POOL_W), _layer_spec(vec.shape, layer), _layer_spec(cps.shape, layer),
                  _layer_spec(wpool.shape, layer), _whole_spec(wout.shape), _whole_spec(win.shape)]
        + [HBM] * len(after),
        out_specs=[rev(D_MODEL), rev(IN_COLS), rev(D_MODEL), _whole_spec(gwpool_shape),
                   _whole_spec((SUBLANES, CONV_W)), _whole_spec((SUBLANES, D_MODEL))],
        out_shape=[jax.ShapeDtypeStruct((t_len, D_MODEL), F32), jax.ShapeDtypeStruct((t_len, IN_COLS), BF16),
                   jax.ShapeDtypeStruct((t_len, D_MODEL), BF16), jax.ShapeDtypeStruct(gwpool_shape, BF16),
                   jax.ShapeDtypeStruct((SUBLANES, CONV_W), F32), jax.ShapeDtypeStruct((SUBLANES, D_MODEL), F32)],
        scratch_shapes=[pltpu.VMEM(gwpool_shape, F32), pltpu.VMEM((CONV_HALO, CONV_W), F32),
                        pltpu.VMEM((POOL_HALO, POOL_W), F32)]
        + [pltpu.VMEM((RING_SLOTS, BWD_TILE, D_MODEL), a.dtype) for a in ringed]
        + [pltpu.SemaphoreType.DMA((len(ringed) * RING_SLOTS,))],
        compiler_params=_params(dimension_semantics=("arbitrary",)),
    )(dxo, y, x, uc, uc, fa, fp, vec, cps, wpool, wout, win, *after)


def _weight_grads(layer, h, dproj, ycat, dy, exchange, after=()):
    t_len = dy.shape[0]
    n_in, n_out = IN_COLS // GWIN_COLS, D_MODEL // GWOUT_COLS
    args = [h, dproj, ycat, dy, *after]
    in_specs = [_whole_spec(h.shape),
                pl.BlockSpec((t_len, GWIN_COLS), lambda s: (0, jnp.minimum(s, n_in - 1))),
                _whole_spec(ycat.shape),
                pl.BlockSpec((t_len, GWOUT_COLS), lambda s: (0, jnp.maximum(s - n_in, 0)))] + [HBM] * len(after)
    out_shape = [jax.ShapeDtypeStruct((D_MODEL, IN_COLS), BF16), jax.ShapeDtypeStruct((D_MODEL, D_MODEL), BF16)]
    out_specs = [pl.BlockSpec((D_MODEL, GWIN_COLS), lambda s: (0, jnp.minimum(s, n_in - 1))),
                 pl.BlockSpec((D_MODEL, GWOUT_COLS), lambda s: (0, jnp.maximum(s - n_in, 0)))]
    scratch = []
    aliases, split = _host(exchange, args, in_specs, out_shape, out_specs, scratch)

    def body(*refs):
        (h_ref, dproj_ref, ycat_ref, dy_ref, *_, gwin_ref, gwout_ref), hosted = split(refs)
        s = pl.program_id(0)
        if hosted is not None:
            pl.when(s == 0)(hosted[0])

        @pl.when(s < n_in)
        def _():
            gwin_ref[...] = _dot_tn(h_ref[...], dproj_ref[...]).astype(BF16)

        @pl.when(s >= n_in)
        def _():
            gwout_ref[...] = _dot_tn(ycat_ref[...], dy_ref[...]).astype(BF16)

        if hosted is not None:
            pl.when(s == n_in + n_out - 1)(hosted[1])

    return pl.pallas_call(
        body, name=f"weight_grads_{layer}", grid=(n_in + n_out,), in_specs=in_specs, out_specs=out_specs,
        out_shape=out_shape, scratch_shapes=scratch, input_output_aliases=aliases,
        compiler_params=_params(dimension_semantics=("arbitrary",)),
    )(*args)


def _add_sibling_blocks(name, layer, grads, received, core, partials, kinds):
    n_arr = len(grads)

    def body(core_ref, *refs):
        mine, theirs, outs = refs[:n_arr], refs[n_arr:2 * n_arr], refs[-n_arr:]
        for a in range(n_arr):
            outs[a][...] = (mine[a][...].astype(F32) + theirs[a][...].astype(F32)).astype(BF16)

    own_of_kind = [
        pl.BlockSpec((D_MODEL, W_IN_SHARD), lambda q, core_ref: (0, 2 * q + core_ref[0])),
        pl.BlockSpec((W_OUT_SHARD, D_MODEL), lambda q, core_ref: (2 * q + core_ref[0], 0)),
        pl.BlockSpec((POOL_SHARD, GROUP_D), lambda q, core_ref: (2 * q + core_ref[0], 0)),
    ]
    shapes = [_BLOCK_SHAPES[k] for k in kinds]
    recv_specs = [pl.BlockSpec((None,) + s, lambda q, core_ref: (q, 0, 0)) for s in shapes]
    out_specs = [pl.BlockSpec((None, None) + s, lambda q, core_ref: (q, layer, 0, 0)) for s in shapes]
    args = [core, *grads, *received]
    in_specs = [own_of_kind[k] for k in kinds] + recv_specs
    aliases = {}
    if partials is not None:
        aliases = {len(args) + a: a for a in range(n_arr)}
        args += list(partials)
        in_specs += [HBM] * n_arr
    return pl.pallas_call(
        body, name=name,
        grid_spec=pltpu.PrefetchScalarGridSpec(
            num_scalar_prefetch=1, grid=(N_CHIP,), in_specs=in_specs, out_specs=out_specs),
        out_shape=[jax.ShapeDtypeStruct((N_CHIP, DEPTH) + s, BF16) for s in shapes],
        input_output_aliases=aliases,
        compiler_params=_params(dimension_semantics=("arbitrary",)),
    )(*args)


def _modulation_columns(c_all, w_ada):
    def body(c_ref, w_ref, cact_ref, out_ref):
        c_t = c_ref[...]
        c_act = c_t * _sigmoid(c_t)
        cact_ref[...] = c_act
        out_ref[...] = jnp.dot(c_act, w_ref[...], preferred_element_type=F32, precision=lax.Precision.HIGHEST)

    return pl.pallas_call(
        body, name="modulation_columns", grid=(DEPTH,),
        in_specs=[pl.BlockSpec((N_DEV, D_MODEL), lambda l: (0, 0)),
                  pl.BlockSpec((None, D_MODEL, W_IN_SHARD), lambda l: (l, 0, 0))],
        out_specs=[pl.BlockSpec((N_DEV, D_MODEL), lambda l: (0, 0)),
                   pl.BlockSpec((N_DEV, W_IN_SHARD), lambda l: (0, l))],
        out_shape=[jax.ShapeDtypeStruct((N_DEV, D_MODEL), F32),
                   jax.ShapeDtypeStruct((N_DEV, DEPTH * W_IN_SHARD), F32)],
        compiler_params=_params(dimension_semantics=("arbitrary",)),
    )(c_all, w_ada)


def _adamw(w, g, m, v):
    m_new = ADAM_B1 * m + (1.0 - ADAM_B1) * g
    v_new = ADAM_B2 * v + (1.0 - ADAM_B2) * (g * g)
    m_hat = m_new / (1.0 - ADAM_B1 ** ADAM_STEP)
    v_hat = v_new / (1.0 - ADAM_B2 ** ADAM_STEP)
    delta = -ADAM_LR * (m_hat / (jnp.sqrt(v_hat) + ADAM_EPS) + ADAM_WD * w)
    return delta, m_new, v_new


def _adamw_w_ada(w, m, v, c_act_t, dmod_cols):
    def body(w_ref, m_ref, v_ref, ct_ref, dm_ref, g_ref, d_ref, mo_ref, vo_ref):
        g = ct_ref[:, 0:1] * dm_ref[0:1, :]
        for b in range(1, N_DEV):
            g = g + ct_ref[:, b:b + 1] * dm_ref[b:b + 1, :]
        g_ref[...] = g
        d_ref[...], mo_ref[...], vo_ref[...] = _adamw(w_ref[...], g, m_ref[...], v_ref[...])

    big = pl.BlockSpec((None, D_MODEL, W_IN_SHARD), lambda l: (l, 0, 0))
    return pl.pallas_call(
        body, name="adamw_w_ada", grid=(DEPTH,),
        in_specs=[big, big, big, pl.BlockSpec((D_MODEL, N_DEV), lambda l: (0, 0)),
                  pl.BlockSpec((None, N_DEV, W_IN_SHARD), lambda l: (l, 0, 0))],
        out_specs=[big] * 4, out_shape=[jax.ShapeDtypeStruct(w.shape, F32)] * 4,
        compiler_params=_params(dimension_semantics=("arbitrary",)),
    )(w, m, v, c_act_t, dmod_cols)


def _sum_chip_partials(own_ref, recv_ref):
    g = own_ref[...].astype(F32)
    for j in range(N_OTHER_CHIPS):
        g = g + recv_ref[j].astype(F32)
    return g


def _partial_specs(row_tile, cols, first_layer=0):
    own = pl.BlockSpec((None, None, row_tile, cols), lambda l, r, chip_ref: (chip_ref[0], first_layer + l, r, 0))
    recv = pl.BlockSpec((N_OTHER_CHIPS, None, row_tile, cols), lambda l, r, chip_ref: (0, first_layer + l, r, 0))
    return own, recv


def _adamw_reduced(name, w, m, v, partial, received, chip, row_tile, layers, continued, after=()):
    depth, rows, cols = w.shape
    first, stop = layers

    def body(chip_ref, w_ref, m_ref, v_ref, own_ref, recv_ref, *rest):
        g_ref, d_ref, mo_ref, vo_ref = rest[-4:]
        g = _sum_chip_partials(own_ref, recv_ref)
        g_ref[...] = g
        d_ref[...], mo_ref[...], vo_ref[...] = _adamw(w_ref[...], g, m_ref[...], v_ref[...])

    blk = pl.BlockSpec((None, row_tile, cols), lambda l, r, chip_ref: (first + l, r, 0))
    args = [chip, w, m, v, partial, received]
    in_specs = [blk, blk, blk, *_partial_specs(row_tile, cols, first)]
    aliases = {}
    if continued is not None:
        aliases = {len(args) + k: k for k in range(4)}
        args += list(continued)
        in_specs += [HBM] * 4
    args += after
    in_specs += [HBM] * len(after)
    return pl.pallas_call(
        body, name=name,
        grid_spec=pltpu.PrefetchScalarGridSpec(
            num_scalar_prefetch=1, grid=(stop - first, rows // row_tile), in_specs=in_specs, out_specs=[blk] * 4),
        out_shape=[jax.ShapeDtypeStruct(w.shape, F32)] * 4, input_output_aliases=aliases,
        compiler_params=_params(dimension_semantics=("arbitrary", "arbitrary")),
    )(*args)


def _reduce_w_pool(partial, received, chip):
    def body(chip_ref, own_ref, recv_ref, g_ref):
        g_ref[...] = _sum_chip_partials(own_ref, recv_ref)

    return pl.pallas_call(
        body, name="reduce_w_pool",
        grid_spec=pltpu.PrefetchScalarGridSpec(
            num_scalar_prefetch=1, grid=(DEPTH, 1), in_specs=list(_partial_specs(POOL_SHARD, GROUP_D)),
            out_specs=pl.BlockSpec((None, POOL_SHARD, GROUP_D), lambda l, r, chip_ref: (l, 0, 0))),
        out_shape=jax.ShapeDtypeStruct((DEPTH, POOL_SHARD, GROUP_D), F32),
        compiler_params=_params(dimension_semantics=("arbitrary", "arbitrary")),
    )(chip, partial, received)


def _adamw_small(name, params):
    n = len(params)

    def body(*refs):
        ins, outs = refs[:4 * n], refs[4 * n:]
        for p in range(n):
            w_ref, g_ref, m_ref, v_ref = ins[4 * p:4 * p + 4]
            d_ref, mo_ref, vo_ref, go_ref = outs[4 * p:4 * p + 4]
            d_ref[...], mo_ref[...], vo_ref[...] = _adamw(w_ref[...], g_ref[...], m_ref[...], v_ref[...])
            go_ref[...] = g_ref[...]

    vmem = pl.BlockSpec(memory_space=pltpu.VMEM)
    flat = [a for group in params for a in group]
    outs = pl.pallas_call(
        body, name=name, in_specs=[vmem] * len(flat), out_specs=[vmem] * len(flat),
        out_shape=[jax.ShapeDtypeStruct(a.shape, F32) for a in flat], compiler_params=_params(),
    )(*flat)
    return [tuple(outs[4 * p:4 * p + 4]) for p in range(n)]


def _sum_sources(slabs, columns, after):
    def body(s_ref, *rest):
        outs = rest[len(after):]
        acc = s_ref[0]
        for b in range(1, N_DEV):
            acc = acc + s_ref[b]
        for (start, stop), o_ref in zip(columns, outs):
            o_ref[...] = acc[:, start:stop]
        outs[-1][...] = acc[0:1, SLAB_COLS:SLAB_COLS + 1]

    vmem = pl.BlockSpec(memory_space=pltpu.VMEM)
    out_shape = [jax.ShapeDtypeStruct((slabs.shape[1], stop - start), F32) for start, stop in columns]
    out_shape.append(jax.ShapeDtypeStruct((1, 1), F32))
    return pl.pallas_call(
        body, name="sum_small_grads", in_specs=[vmem] + [HBM] * len(after), out_specs=[vmem] * len(out_shape),
        out_shape=out_shape, compiler_params=_params(),
    )(slabs, *after)


def _to_bf16(a, name, layers=None):
    first, stop = layers or (0, a.shape[0])

    def body(a_ref, o_ref):
        o_ref[...] = a_ref[...].astype(BF16)

    block = (None,) + a.shape[1:]
    return pl.pallas_call(
        body, name=name, grid=(stop - first,), in_specs=[pl.BlockSpec(block, lambda l: (first + l, 0, 0))],
        out_specs=pl.BlockSpec(block, lambda l: (l, 0, 0)),
        out_shape=jax.ShapeDtypeStruct((stop - first,) + a.shape[1:], BF16),
        compiler_params=_params(dimension_semantics=("arbitrary",)),
    )(a)


def kernel(x, c, w_ada, b_ada, g_pre, w_in, w_conv, w_pool, pool_scale, w_out, g_post, loss_target, m_w_ada, m_b_ada, m_g_pre, m_w_in, m_w_conv, m_w_pool, m_pool_scale, m_w_out, m_g_post, v_w_ada, v_b_ada, v_g_pre, v_w_in, v_w_conv, v_w_pool, v_pool_scale, v_w_out, v_g_post):
    mx, my, mc = _mesh_position()
    me = _block_id(mx, my, mc)
    chip = (2 * mx + my).astype(jnp.int32).reshape(1)
    core = mc.astype(jnp.int32).reshape(1)
    x0 = x[0]
    target = loss_target[0]
    conv_shard = w_conv.shape[-1]

    own_small = jnp.concatenate([c, w_conv.reshape(1, DEPTH * 3 * conv_shard)], axis=1)
    first_shards = [_to_bf16(w_in, "cast_w_in_0", (0, 1)), _to_bf16(w_out, "cast_w_out_0", (0, 1))]
    all_small = _all_gather_small(own_small, "all_gather_c_w_conv", first_shards)[:, 0, :]
    gathers = [_gather_weights_start(0, *first_shards, [all_small], relayed=True)]
    c_all = all_small[:, :D_MODEL]
    w_conv_full = all_small[:, D_MODEL:].reshape(N_DEV, DEPTH, 3, conv_shard).transpose(1, 2, 0, 3).reshape(
        DEPTH, 3, CONV_W)
    cps = jnp.concatenate([w_conv_full, pool_scale[:, None], jnp.zeros((DEPTH, 4, CONV_W), F32)], axis=1)

    c_act, pieces = _modulation_columns(c_all, w_ada)
    upper = (1, DEPTH)
    upper_shards = [_to_bf16(w_in, "cast_w_in_1", upper), _to_bf16(w_out, "cast_w_out_1", upper)]
    wpool_b = _to_bf16(w_pool.reshape(DEPTH, POOL_ROWS, GROUP_D), "cast_w_pool").reshape(w_pool.shape)
    first_sems_0, _, (zones_0,) = gathers[0]
    neighbour_sems, zones_0 = _gather_weights_pass_on(
        "all_gather_weights_relay_0", NEIGHBOUR_CHIPS, first_sems_0[1], partial(_first_arrival, 0), zones_0,
        [pieces, *upper_shards, wpool_b, cps], relay=True)
    gather_mod = _all_gather_exchange(pieces)
    sems_m, pieces_thru, mod_landing, token_m = _start_exchange(gather_mod, "all_gather_modulation_start", [zones_0[0]])
    diagonal_sems, zones_0 = _gather_weights_pass_on(
        "all_gather_weights_pass_on_0", DIAGONAL_CHIP, neighbour_sems[3], lambda a, j: a, zones_0, [token_m])
    _, (mod_all,) = _finish_exchange(
        gather_mod, "all_gather_modulation_finish", sems_m, pieces_thru, mod_landing, [zones_0[0]])
    mod_mine = lax.dynamic_index_in_dim(mod_all, me, axis=1, keepdims=False)
    mod = mod_mine.reshape(N_DEV, DEPTH, W_IN_SHARD).transpose(1, 0, 2).reshape(DEPTH, 3 * D_MODEL) + b_ada
    zeros_d = jnp.zeros((DEPTH, 3, D_MODEL), F32)
    vec = jnp.concatenate([mod.reshape(DEPTH, 3, D_MODEL), g_pre[:, None], g_post[:, None], zeros_d], axis=1)

    gathers.append(_gather_weights_start(1, *upper_shards, [mod_all]))
    gathers = [(first_sems, shards, landing[k], k) for first_sems, shards, landing in gathers
               for k in range(len(landing))]

    xs, kept, wins, wouts = [x0], [], [], []
    for l in range(DEPTH):
        first_sems, shards, zones, index = gathers[l]
        if l == 0:
            passed = [(neighbour_sems[:2], NEIGHBOUR_CHIPS), (diagonal_sems, DIAGONAL_CHIP)]
            win, wout = _gather_weights_finish(
                l, index, first_sems, passed, shards, zones_0, neighbour_sems[2], [gathers[-1][1][0]])
        else:
            passed_sems, zones = _gather_weights_pass_on(
                f"all_gather_weights_pass_on_{l}", ALL_OTHER_CHIPS, first_sems[1], partial(_first_arrival, index),
                zones, [xs[-1]])
            win, wout = _gather_weights_finish(l, index, first_sems, [(passed_sems, ALL_OTHER_CHIPS)], shards, zones)
        x_next, *for_backward = _forward_layer(
            l, xs[-1], vec, cps, wpool_b, win, wout, target if l == DEPTH - 1 else None)
        xs.append(x_next)
        kept.append(for_backward[:6])
        wins.append(win)
        wouts.append(wout)
    dx, loss_tile = xs[DEPTH], for_backward[6]

    slab_rows = [None] * DEPTH
    partials = received = None
    in_flight = []

    def scatter(layer, grads, from_sibling, after):
        nonlocal partials, received
        partials = _add_sibling_blocks(
            f"grad_add_sibling_{layer}", layer, grads, from_sibling, core, partials, ALL_KINDS)
        chips = _chips_exchange(layer, partials, received, ALL_KINDS)
        sems, partials, received, token = _start_exchange(chips, f"grad_chips_start_{layer}", after)
        in_flight.append((chips, sems, layer))
        return token

    grads_above = None
    issued = []
    for l in reversed(range(DEPTH)):
        y, h, ycat, uc, fa, fp = kept[l]
        dx, dproj, dy, gwpool, dcps, dvec = _backward_layer(
            l, dx, y, xs[l], uc, fa, fp, vec, cps, wpool_b, wouts[l], wins[l], issued)
        slab_rows[l] = jnp.concatenate(
            [dvec[0], dvec[1], dvec[2], dvec[3], dvec[4], dcps[3], dcps[0], dcps[1], dcps[2],
             loss_tile[0] if l == 0 else jnp.zeros((LANES,), F32)])
        after = []
        if l == 0:
            gather_small = _all_gather_exchange(jnp.stack(slab_rows))
            sems_s, slab, slabs, token = _start_exchange(gather_small, "all_gather_small_grads_start")
            after = [token]
        hosted = _sibling_exchange(grads_above, ALL_KINDS) if grads_above is not None else None
        gwin, gwout, *from_sibling = _weight_grads(l, h, dproj, ycat, dy, hosted, after)
        if l == 0:
            _, (slabs,) = _finish_exchange(
                gather_small, "all_gather_small_grads_finish", sems_s, slab, slabs, [gwin])
        issued = [gwin]
        if grads_above is not None:
            issued = [scatter(l + 1, grads_above, from_sibling, [])]
        grads_above = [gwin, gwout, gwpool.reshape(POOL_ROWS, GROUP_D)]
        if l <= 1:
            from_sibling = _run_exchange(_sibling_exchange(grads_above, ALL_KINDS), f"grad_exchange_sibling_{l}")
            token = scatter(l, grads_above, from_sibling, [slabs] if l == 0 else [])
            issued = [token]
            grads_above = None
    grad_x = dx[None]
    chips_0, sems_0, _ = in_flight.pop()

    o = 3 * D_MODEL

    after = [token]
    for chips, sems, l in in_flight:
        partials, received = _finish_exchange(chips, f"grad_chips_finish_{l}", sems, partials, received, after)
        after = []
    upper = (1, DEPTH)
    w_in_upper = _adamw_reduced(
        "adamw_w_in_upper", w_in, m_w_in, v_w_in, partials[0], received[0], chip, ROW_TILE, upper, None)
    w_out_upper = _adamw_reduced(
        "adamw_w_out_upper", w_out, m_w_out, v_w_out, partials[1], received[1], chip, W_OUT_SHARD, upper, None)
    dmod_all = slabs[:, :, :o].reshape(N_DEV, DEPTH, N_DEV, W_IN_SHARD)
    dmod_cols = lax.dynamic_index_in_dim(dmod_all, me, axis=2, keepdims=False).transpose(1, 0, 2) + token[0, 0]
    g_w_ada, d_w_ada, nm_w_ada, nv_w_ada = _adamw_w_ada(w_ada, m_w_ada, v_w_ada, c_act.T, dmod_cols)

    partials, received = _finish_exchange(
        chips_0, "grad_chips_finish_0", sems_0, partials, received, [nv_w_ada, w_in_upper[3], w_out_upper[3]])
    gather_pool = _all_gather_exchange(_reduce_w_pool(partials[2], received[2], chip), axis=1)
    sems_p, pool_rows, pool_landing, token_p = _start_exchange(gather_pool, "all_gather_grad_w_pool_start")
    g_w_in, d_w_in, nm_w_in, nv_w_in = _adamw_reduced(
        "adamw_w_in_0", w_in, m_w_in, v_w_in, partials[0], received[0], chip, ROW_TILE, (0, 1), w_in_upper, [token_p])
    g_w_out, d_w_out, nm_w_out, nv_w_out = _adamw_reduced(
        "adamw_w_out_0", w_out, m_w_out, v_w_out, partials[1], received[1], chip, W_OUT_SHARD, (0, 1), w_out_upper,
        [token_p])
    ends = [0, o, o + D_MODEL, o + 2 * D_MODEL, o + 2 * D_MODEL + POOL_W, SLAB_COLS]
    g_b_ada, g_g_pre, g_g_post, g_pool_scale, g_conv, loss = _sum_sources(
        slabs, list(zip(ends[:-1], ends[1:])), [token_p, nv_w_in, nv_w_out])
    loss = loss.reshape(())
    g_w_conv = lax.dynamic_slice_in_dim(g_conv.reshape(DEPTH, 3, CONV_W), me * conv_shard, conv_shard, axis=2)
    taps_first = lambda a: a.transpose(1, 0, 2)
    small = _adamw_small("adamw_small", [
        (b_ada, g_b_ada, m_b_ada, v_b_ada),
        (g_pre, g_g_pre, m_g_pre, v_g_pre),
        tuple(taps_first(a) for a in (w_conv, g_w_conv, m_w_conv, v_w_conv)),
        (pool_scale, g_pool_scale, m_pool_scale, v_pool_scale),
        (g_post, g_g_post, m_g_post, v_g_post),
    ])
    (d_b_ada, nm_b_ada, nv_b_ada, g_b_ada), (d_g_pre, nm_g_pre, nv_g_pre, g_g_pre), conv_steps, \
        (d_ps, nm_ps, nv_ps, g_pool_scale), (d_g_post, nm_g_post, nv_g_post, g_g_post) = small
    d_w_conv, nm_w_conv, nv_w_conv, g_w_conv = (taps_first(a) for a in conv_steps)
    _, (g_pool_all,) = _finish_exchange(
        gather_pool, "all_gather_grad_w_pool_finish", sems_p, pool_rows, pool_landing, [nv_w_in, nv_w_out, nv_g_post])
    ((d_w_pool, nm_w_pool, nv_w_pool, g_w_pool),) = _adamw_small(
        "adamw_w_pool", [(w_pool, g_pool_all.reshape(w_pool.shape), m_w_pool, v_w_pool)])

    return (loss, grad_x,
            g_w_ada, g_b_ada, g_g_pre, g_w_in, g_w_conv, g_w_pool, g_pool_scale, g_w_out, g_g_post,
            d_w_ada, d_b_ada, d_g_pre, d_w_in, d_w_conv, d_w_pool, d_ps, d_w_out, d_g_post,
            nm_w_ada, nm_b_ada, nm_g_pre, nm_w_in, nm_w_conv, nm_w_pool, nm_ps, nm_w_out, nm_g_post,
            nv_w_ada, nv_b_ada, nv_g_pre, nv_w_in, nv_w_conv, nv_w_pool, nv_ps, nv_w_out, nv_g_post)
```

```python
from functools import partial

import jax
import jax.numpy as jnp
from jax import lax
from jax.experimental import pallas as pl
from jax.experimental.pallas import tpu as pltpu

F32 = jnp.float32
BF16 = jnp.bfloat16

D_MODEL = 1024
DEPTH = 4
CONV_W = 512
POOL_W = 512
POOL_WINDOWS = (2, 4, 8, 16)
GROUP_D = 128
IN_COLS = 4 * CONV_W + 2 * POOL_W
NORM_EPS = 1e-6

ADAM_LR = 0.001
ADAM_B1 = 0.9
ADAM_B2 = 0.999
ADAM_EPS = 1e-08
ADAM_WD = 0.01
ADAM_STEP = 10

N_DEV = 8
N_CHIP = 4
N_OTHER_CHIPS = N_CHIP - 1
MESH = pl.DeviceIdType.MESH
W_IN_SHARD = IN_COLS // N_DEV
W_OUT_SHARD = D_MODEL // N_DEV
POOL_ROWS = len(POOL_WINDOWS) * GROUP_D
POOL_SHARD = POOL_ROWS // N_DEV

SUBLANES = 8
LANES = 128
VMEM_LIMIT_BYTES = 56 * 1024 * 1024
ROW_TILE = 512
BWD_TILE = 256
GWIN_COLS = 768
GWOUT_COLS = 512
POOL_HALO = 16
CONV_HALO = SUBLANES

SLAB_COLS = 3 * D_MODEL + D_MODEL + D_MODEL + POOL_W + 3 * CONV_W

HBM = pl.BlockSpec(memory_space=pl.ANY)


def _params(**kw):
    return pltpu.CompilerParams(vmem_limit_bytes=VMEM_LIMIT_BYTES, **kw)


def _sigmoid(v):
    return 1.0 / (1.0 + jnp.exp(-v))


def _dot(a, b):
    return jnp.dot(a, b, preferred_element_type=F32)


def _dot_tn(a, b):
    return lax.dot_general(a, b, (((0,), (0,)), ((), ())), preferred_element_type=F32)


def _dot_nt(a, b):
    return lax.dot_general(a, b, (((1,), (1,)), ((), ())), preferred_element_type=F32)


def _rows_from_before(v, k):
    return pltpu.roll(v, k, 0)


def _rows_from_after(v, k):
    return pltpu.roll(v, v.shape[0] - k, 0)


def _window_counts(t0, rows):
    return (lax.broadcasted_iota(jnp.int32, (rows, 1), 0) + (t0 + 1)).astype(F32)


def _split_proj(p32):
    cw = CONV_W
    return (p32[:, 0 * cw:1 * cw], p32[:, 1 * cw:2 * cw], p32[:, 2 * cw:3 * cw], p32[:, 3 * cw:4 * cw],
            p32[:, 4 * cw:4 * cw + POOL_W], p32[:, 4 * cw + POOL_W:])


def _layer_spec(shape, layer):
    nd = len(shape)
    return pl.BlockSpec((None,) + tuple(shape[1:]), lambda i, _l=layer, _n=nd: (_l,) + (0,) * (_n - 1))


def _whole_spec(shape):
    return pl.BlockSpec(tuple(shape), lambda i, _n=len(shape): (0,) * _n, pipeline_mode=pl.Buffered(1))


def _mesh_position():
    return lax.axis_index("x"), lax.axis_index("y"), lax.axis_index("c")


def _block_id(x, y, c):
    return 4 * x + 2 * y + c


def _other_chips(x, y):
    return [(x ^ 1, y), (x, y ^ 1), (x ^ 1, y ^ 1)]


def _col_block(ref, blk):
    return ref.at[:, pl.ds(pl.multiple_of(blk * W_IN_SHARD, LANES), W_IN_SHARD)]


def _row_block(rows):
    def block(ref, blk):
        return ref.at[pl.ds(pl.multiple_of(blk * rows, rows), rows), :]
    return block


_BLOCK_OF = (_col_block, _row_block(W_OUT_SHARD), _row_block(POOL_SHARD))
_BLOCK_SHAPES = ((D_MODEL, W_IN_SHARD), (W_OUT_SHARD, D_MODEL), (POOL_SHARD, GROUP_D))


class _Exchange:
    def __init__(self, inputs, out_shapes, aliases, sem_shapes, make):
        self.inputs, self.out_shapes, self.aliases, self.sem_shapes, self.make = (
            list(inputs), list(out_shapes), dict(aliases), list(sem_shapes), make)


def _run_exchange(exchange, name):
    n_in, n_out = len(exchange.inputs), len(exchange.out_shapes)

    def body(*refs):
        start, finish = exchange.make(refs[:n_in], refs[n_in:n_in + n_out], refs[n_in + n_out:])
        start()
        finish()

    return pl.pallas_call(
        body, name=name, in_specs=[HBM] * n_in, out_specs=[HBM] * n_out, out_shape=exchange.out_shapes,
        scratch_shapes=exchange.sem_shapes, input_output_aliases=exchange.aliases, compiler_params=_params(),
    )(*exchange.inputs)


_SEM = pl.BlockSpec(memory_space=pltpu.SEMAPHORE)
_DATAFLOW = pltpu.SideEffectType.DATAFLOW_SIDE_EFFECTING


def _start_exchange(exchange, name, after=()):
    n_in, n_out, n_sem = len(exchange.inputs), len(exchange.out_shapes), len(exchange.sem_shapes)
    sources = [i for i in range(n_in) if i not in exchange.aliases]
    aliases = {i: n_sem + k for k, i in enumerate(sources)}
    aliases.update({i: n_sem + len(sources) + o for i, o in exchange.aliases.items()})

    def body(*refs):
        in_refs = refs[:n_in]
        outs = refs[n_in + len(after):]
        sems = outs[:n_sem]
        out_refs = outs[n_sem + len(sources):n_sem + len(sources) + n_out]
        exchange.make(in_refs, out_refs, sems)[0]()
        refs[-1][...] = jnp.zeros_like(refs[-1])

    outs = pl.pallas_call(
        body, name=name, in_specs=[HBM] * (n_in + len(after)),
        out_specs=[_SEM] * n_sem + [HBM] * (len(sources) + n_out) + [pl.BlockSpec(memory_space=pltpu.VMEM)],
        out_shape=(exchange.sem_shapes + [pltpu.HBM(exchange.inputs[i].shape, exchange.inputs[i].dtype) for i in sources]
                   + [pltpu.HBM(s.shape, s.dtype) for s in exchange.out_shapes]
                   + [jax.ShapeDtypeStruct((SUBLANES, LANES), F32)]),
        input_output_aliases=aliases, compiler_params=_params(has_side_effects=_DATAFLOW),
    )(*exchange.inputs, *after)
    return outs[:n_sem], outs[n_sem:n_sem + len(sources)], outs[n_sem + len(sources):-1], outs[-1]


def _finish_exchange(exchange, name, sems, sources, landing, after):
    n_src, n_out, n_sem = len(sources), len(landing), len(sems)
    n_in = len(exchange.inputs)
    source_at = [i for i in range(n_in) if i not in exchange.aliases]

    def body(*refs):
        src_refs, out_refs = refs[:n_src], refs[n_src:n_src + n_out]
        sem_refs = refs[n_src + n_out:n_src + n_out + n_sem]
        in_refs = [None] * n_in
        for k, i in enumerate(source_at):
            in_refs[i] = src_refs[k]
        for i, o in exchange.aliases.items():
            in_refs[i] = out_refs[o]
        exchange.make(in_refs, out_refs, sem_refs)[1]()

    arrays = list(sources) + list(landing)
    outs = pl.pallas_call(
        body, name=name, in_specs=[HBM] * len(arrays) + [_SEM] * n_sem + [HBM] * len(after),
        out_specs=[HBM] * len(arrays), out_shape=[pltpu.HBM(a.shape, a.dtype) for a in arrays],
        input_output_aliases={i: i for i in range(len(arrays))}, compiler_params=_params(has_side_effects=_DATAFLOW),
    )(*arrays, *sems, *after)
    return outs[:n_src], outs[n_src:]


N_GATHERED = 2
FIRST_COPIES = 1 + N_OTHER_CHIPS
_GATHERED_SHAPES = ((D_MODEL, IN_COLS), (D_MODEL, D_MODEL))
ALL_OTHER_CHIPS, NEIGHBOUR_CHIPS, DIAGONAL_CHIP = (0, 1, 2), (0, 1), (2,)


def _first_sem(layer, a, k):
    return (layer * N_GATHERED + a) * FIRST_COPIES + k


def _first_arrival(layer, a, j):
    return _first_sem(layer, a, 1 + j)


def _gather_copy(window_of, full_ref, blk, send_sem, recv_sem, to, src=None):
    window = window_of(full_ref, blk)
    return pltpu.make_async_remote_copy(
        src_ref=window if src is None else src, dst_ref=window, send_sem=send_sem, recv_sem=recv_sem,
        device_id=to, device_id_type=MESH)


def _first_copies(layer, shard_refs, full_refs, send_sems, recv_sems, local_sems, relayed):
    x, y, c = _mesh_position()
    me = _block_id(x, y, c)
    chips = [_other_chips(x, y)[j] for j in (NEIGHBOUR_CHIPS if relayed else ALL_OTHER_CHIPS)]
    own, remote = [], []
    for a in range(N_GATHERED):
        shard = shard_refs[a].at[layer]
        own.append(pltpu.make_async_copy(
            shard, _BLOCK_OF[a](full_refs[a], me), local_sems.at[layer * N_GATHERED + a]))
        targets = [(x, y, 1 - c)] + [(*chip, c) for chip in chips]
        remote += [_gather_copy(_BLOCK_OF[a], full_refs[a], me, send_sems.at[_first_sem(layer, a, k)],
                                recv_sems.at[_first_sem(layer, a, k)], to, src=shard)
                   for k, to in enumerate(targets)]
    return own, remote


def _gather_weights_start(first_layer, win_shards, wout_shards, after, relayed=False):
    n_layers = win_shards.shape[0]
    n_first = n_layers * N_GATHERED * FIRST_COPIES
    sem_shapes = [pltpu.SemaphoreType.DMA((n_first,)), pltpu.SemaphoreType.DMA((n_first,)),
                  pltpu.SemaphoreType.DMA((n_layers * N_GATHERED,))]
    shards = [win_shards, wout_shards]

    def body(win_sh, wout_sh, *rest):
        send_sems, recv_sems, local_sems, win_thru, wout_thru, *landing = rest[len(after):]
        for layer in range(n_layers):
            own, remote = _first_copies(layer, (win_sh, wout_sh), landing[N_GATHERED * layer:N_GATHERED * (layer + 1)],
                                        send_sems, recv_sems, local_sems, relayed)
            for cp in own + remote:
                cp.start()

    outs = pl.pallas_call(
        body, name=f"all_gather_weights_start_{first_layer}", in_specs=[HBM] * (2 + len(after)),
        out_specs=[_SEM] * 3 + [HBM] * (2 + n_layers * N_GATHERED),
        out_shape=(sem_shapes + [pltpu.HBM(s.shape, s.dtype) for s in shards]
                   + [pltpu.HBM(s, BF16) for _ in range(n_layers) for s in _GATHERED_SHAPES]),
        input_output_aliases={0: 3, 1: 4}, compiler_params=_params(has_side_effects=_DATAFLOW),
    )(*shards, *after)
    landing = outs[5:]
    return outs[:3], outs[3:5], [landing[N_GATHERED * l:N_GATHERED * (l + 1)] for l in range(n_layers)]


def _passed_on_copies(full_refs, send_sems, recv_sems, core_of_block, chips):
    x, y, c = _mesh_position()
    return [_gather_copy(_BLOCK_OF[a], full_refs[a], _block_id(*_other_chips(x, y)[j], core_of_block),
                         send_sems.at[a * N_OTHER_CHIPS + j], recv_sems.at[a * N_OTHER_CHIPS + j], (x, y, 1 - c))
            for a in range(N_GATHERED) for j in chips]


def _relayed_copies(full_refs, send_sems, recv_sems):
    x, y, c = _mesh_position()
    source, to = (x ^ (1 - c), y ^ c), (x ^ c, y ^ (1 - c))
    return [_gather_copy(_BLOCK_OF[a], full_refs[a], _block_id(*source, c), send_sems.at[a], recv_sems.at[a], (*to, c))
            for a in range(N_GATHERED)]


def _gather_weights_pass_on(name, chips, arrival_sems, arrival_sem_of, landing, after, relay=False):
    n = N_GATHERED * N_OTHER_CHIPS
    sem_shapes = [pltpu.SemaphoreType.DMA((n,))] * 2 + [pltpu.SemaphoreType.DMA((N_GATHERED,))] * (2 if relay else 0)

    def body(win_ref, wout_ref, arrivals, *rest):
        sems = rest[len(after):len(after) + len(sem_shapes)]
        x, y, c = _mesh_position()
        full_refs = (win_ref, wout_ref)
        passed = _passed_on_copies(full_refs, sems[0], sems[1], c, chips)
        relayed = _relayed_copies(full_refs, sems[2], sems[3]) if relay else []
        for a in range(N_GATHERED):
            for j in chips:
                sem = arrivals.at[arrival_sem_of(a, j)]
                _gather_copy(_BLOCK_OF[a], full_refs[a], _block_id(*_other_chips(x, y)[j], c), sem, sem,
                             (x, y, c)).wait_recv()
            for cp in relayed[a:a + 1] + passed[a * len(chips):(a + 1) * len(chips)]:
                cp.start()

    outs = pl.pallas_call(
        body, name=name, in_specs=[HBM] * N_GATHERED + [_SEM] + [HBM] * len(after),
        out_specs=[_SEM] * len(sem_shapes) + [HBM] * N_GATHERED,
        out_shape=sem_shapes + [pltpu.HBM(a.shape, a.dtype) for a in landing],
        input_output_aliases={a: len(sem_shapes) + a for a in range(N_GATHERED)},
        compiler_params=_params(has_side_effects=_DATAFLOW),
    )(*landing, arrival_sems, *after)
    return outs[:len(sem_shapes)], outs[len(sem_shapes):]


def _gather_weights_finish(layer, index, first_sems, passed, shards, landing, relay_send_sems=None, after=()):
    relayed = relay_send_sems is not None
    passed_sems = [sem for (send, recv), _ in passed for sem in (send, recv)] + ([relay_send_sems] if relayed else [])

    def body(win_ref, wout_ref, first_send, first_recv, local_sems, *rest):
        sems, (win_sh, wout_sh) = rest[:len(passed_sems)], rest[len(passed_sems):len(passed_sems) + 2]
        x, y, c = _mesh_position()
        full_refs = (win_ref, wout_ref)
        own, sent = _first_copies(index, (win_sh, wout_sh), full_refs, first_send, first_recv, local_sems, relayed)
        for a in range(N_GATHERED):
            sem = _first_sem(index, a, 0)
            _gather_copy(_BLOCK_OF[a], full_refs[a], _block_id(x, y, 1 - c), first_recv.at[sem], first_recv.at[sem],
                         (x, y, c)).wait_recv()
        for p, (_, chips) in enumerate(passed):
            for cp in _passed_on_copies(full_refs, sems[2 * p], sems[2 * p + 1], 1 - c, chips):
                cp.wait_recv()
            sent += _passed_on_copies(full_refs, sems[2 * p], sems[2 * p + 1], c, chips)
        if relayed:
            sent += _relayed_copies(full_refs, sems[-1], sems[-1])
        for cp in sent:
            cp.wait_send()
        for cp in own:
            cp.wait()

    return pl.pallas_call(
        body, name=f"all_gather_weights_finish_{layer}",
        in_specs=[HBM] * N_GATHERED + [_SEM] * (3 + len(passed_sems)) + [HBM] * (2 + len(after)),
        out_specs=[HBM] * N_GATHERED, out_shape=[pltpu.HBM(a.shape, a.dtype) for a in landing],
        input_output_aliases={a: a for a in range(N_GATHERED)}, compiler_params=_params(has_side_effects=_DATAFLOW),
    )(*landing, *first_sems, *passed_sems, *shards, *after)


ALL_KINDS = (0, 1, 2)


def _sibling_exchange(grads, kinds):
    n_arr = len(grads)

    def make(in_refs, out_refs, sems):
        send_sems, recv_sems = sems
        x, y, c = _mesh_position()
        copies = [pltpu.make_async_remote_copy(
            src_ref=_BLOCK_OF[kinds[a]](in_refs[a], 2 * q + (1 - c)), dst_ref=out_refs[a].at[q],
            send_sem=send_sems.at[a * N_CHIP + q], recv_sem=recv_sems.at[a * N_CHIP + q],
            device_id=(x, y, 1 - c), device_id_type=MESH)
            for a in range(n_arr) for q in range(N_CHIP)]

        def start():
            for cp in copies:
                cp.start()

        def finish():
            for cp in copies:
                cp.wait_recv()
            for cp in copies:
                cp.wait_send()

        return start, finish

    return _Exchange(
        grads, [jax.ShapeDtypeStruct((N_CHIP,) + _BLOCK_SHAPES[k], BF16) for k in kinds], {},
        [pltpu.SemaphoreType.DMA((n_arr * N_CHIP,)), pltpu.SemaphoreType.DMA((n_arr * N_CHIP,))], make)


def _chips_exchange(layer, partials, received, kinds):
    n_arr = len(partials)

    def make(in_refs, out_refs, sems):
        send_sems, recv_sems = sems
        x, y, c = _mesh_position()
        copies = [pltpu.make_async_remote_copy(
            src_ref=in_refs[a].at[2 * qx + qy, layer], dst_ref=out_refs[a].at[j, layer],
            send_sem=send_sems.at[a * N_OTHER_CHIPS + j], recv_sem=recv_sems.at[a * N_OTHER_CHIPS + j],
            device_id=(qx, qy, c), device_id_type=MESH)
            for a in range(n_arr) for j, (qx, qy) in enumerate(_other_chips(x, y))]

        def start():
            for cp in copies:
                cp.start()

        def finish():
            for cp in copies:
                cp.wait_recv()
            for cp in copies:
                cp.wait_send()

        return start, finish

    inputs = list(partials)
    aliases = {}
    if received is not None:
        inputs += list(received)
        aliases = {n_arr + a: a for a in range(n_arr)}
    return _Exchange(
        inputs, [jax.ShapeDtypeStruct((N_OTHER_CHIPS, DEPTH) + _BLOCK_SHAPES[k], BF16) for k in kinds], aliases,
        [pltpu.SemaphoreType.DMA((n_arr * N_OTHER_CHIPS,)), pltpu.SemaphoreType.DMA((n_arr * N_OTHER_CHIPS,))], make)


def _all_gather_exchange(v, axis=0):
    def make(in_refs, out_refs, sems):
        send_sems, recv_sems, local_sem = sems
        x, y, c = _mesh_position()
        me = _block_id(x, y, c)
        block = lambda blk: out_refs[0].at[(slice(None),) * axis + (blk,)]
        own = pltpu.make_async_copy(in_refs[0], block(me), local_sem.at[0])
        sends, arrivals = [], []
        for k in range(1, N_DEV):
            px, py, pc = x ^ ((k >> 2) & 1), y ^ ((k >> 1) & 1), c ^ (k & 1)
            sends.append(pltpu.make_async_remote_copy(
                src_ref=in_refs[0], dst_ref=block(me), send_sem=send_sems.at[k - 1],
                recv_sem=recv_sems.at[k - 1], device_id=(px, py, pc), device_id_type=MESH))
            arrivals.append(pltpu.make_async_remote_copy(
                src_ref=in_refs[0], dst_ref=block(_block_id(px, py, pc)), send_sem=send_sems.at[k - 1],
                recv_sem=recv_sems.at[k - 1], device_id=(x, y, c), device_id_type=MESH))

        def start():
            for cp in [own] + sends:
                cp.start()

        def finish():
            for cp in arrivals:
                cp.wait_recv()
            for cp in sends:
                cp.wait_send()
            own.wait()

        return start, finish

    return _Exchange(
        [v], [jax.ShapeDtypeStruct(v.shape[:axis] + (N_DEV,) + v.shape[axis:], v.dtype)], {},
        [pltpu.SemaphoreType.DMA((N_DEV - 1,)), pltpu.SemaphoreType.DMA((N_DEV - 1,)),
         pltpu.SemaphoreType.DMA((1,))], make)


def _host(exchange, args, in_specs, out_shape, out_specs, scratch):
    n_own = (len(args), len(out_shape), len(scratch))
    if exchange is None:
        return {}, lambda refs: (refs, None)
    n_ex = (len(exchange.inputs), len(exchange.out_shapes), len(exchange.sem_shapes))
    aliases = {n_own[0] + i: n_own[1] + o for i, o in exchange.aliases.items()}
    args += exchange.inputs
    in_specs += [HBM] * n_ex[0]
    out_shape += exchange.out_shapes
    out_specs += [HBM] * n_ex[1]
    scratch += exchange.sem_shapes

    def split(refs):
        own, theirs, at = [], [], 0
        for mine, ex in zip(n_own, n_ex):
            own += refs[at:at + mine]
            theirs.append(refs[at + mine:at + mine + ex])
            at += mine + ex
        return own, exchange.make(*theirs)

    return aliases, split


def _all_gather_small(v, name, after=()):
    vmem = pl.BlockSpec(memory_space=pltpu.VMEM)

    def body(v_ref, *rest):
        out_ref, send_sems, recv_sems = rest[len(after):]
        x, y, c = _mesh_position()
        me = _block_id(x, y, c)
        out_ref[me] = v_ref[...]
        sends = []
        for k in range(1, N_DEV):
            px, py, pc = x ^ ((k >> 2) & 1), y ^ ((k >> 1) & 1), c ^ (k & 1)
            send = pltpu.make_async_remote_copy(
                src_ref=v_ref, dst_ref=out_ref.at[me], send_sem=send_sems.at[k - 1], recv_sem=recv_sems.at[k - 1],
                device_id=(px, py, pc), device_id_type=MESH)
            send.start()
            sends.append((send, _block_id(px, py, pc)))
        for k, (send, peer) in enumerate(sends):
            pltpu.make_async_remote_copy(
                src_ref=v_ref, dst_ref=out_ref.at[peer], send_sem=send_sems.at[k], recv_sem=recv_sems.at[k],
                device_id=(x, y, c), device_id_type=MESH).wait_recv()
        for send, _ in sends:
            send.wait_send()

    return pl.pallas_call(
        body, name=name, in_specs=[vmem] + [HBM] * len(after), out_specs=vmem,
        out_shape=jax.ShapeDtypeStruct((N_DEV,) + v.shape, v.dtype),
        scratch_shapes=[pltpu.SemaphoreType.DMA((N_DEV - 1,)), pltpu.SemaphoreType.DMA((N_DEV - 1,))],
        compiler_params=_params(),
    )(v, *after)


def _forward_layer(layer, x, vec, cps, wpool, win, wout, target=None):
    t_len = x.shape[0]
    n_tiles = t_len // ROW_TILE
    row = lambda cols: pl.BlockSpec((ROW_TILE, cols), lambda i: (i, 0))
    widths = (D_MODEL, 2 * CONV_W, 3 * CONV_W, 4 * POOL_W)
    head = target is not None

    def body(x_ref, vec_ref, cps_ref, wpool_ref, win_ref, wout_ref, *rest):
        target_ref = rest[0] if head else None
        xo_ref, y_ref, h_ref, ycat_ref, uc_ref, fa_ref, fp_ref = rest[head:head + 7]
        loss_ref = rest[head + 7] if head else None
        zc_ref, pc_ref = rest[-2:]
        i = pl.program_id(0)

        @pl.when(i == 0)
        def _():
            zc_ref[...] = jnp.zeros_like(zc_ref)
            pc_ref[...] = jnp.zeros_like(pc_ref)
            if head:
                loss_ref[...] = jnp.zeros_like(loss_ref)

        x_t = x_ref[...]
        shift, scale, gate = vec_ref[0:1, :], vec_ref[1:2, :], vec_ref[2:3, :]
        g_pre, g_post = vec_ref[3:4, :], vec_ref[4:5, :]
        w0, w1, w2, ps = cps_ref[0:1, :], cps_ref[1:2, :], cps_ref[2:3, :], cps_ref[3:4, :]
        rx = lax.rsqrt(jnp.mean(x_t * x_t, axis=-1, keepdims=True) + NORM_EPS)
        h = (x_t * rx) * g_pre * (1.0 + scale) + shift
        h_ref[...] = h.astype(BF16)
        proj = _dot(h.astype(BF16), win_ref[...])
        u_a, b_a, c_a, g_a, u_p, g_p = _split_proj(proj)
        uc_ref[...] = jnp.concatenate([u_a, c_a], axis=1).astype(BF16)

        z = c_a * u_a
        zcat = jnp.concatenate([zc_ref[...], z], axis=0)
        zc_ref[...] = z[ROW_TILE - CONV_HALO:]
        conv = (w0 * _rows_from_before(zcat, 2)[CONV_HALO:] + w1 * _rows_from_before(zcat, 1)[CONV_HALO:] + w2 * z)
        sig_a = _sigmoid(g_a)
        silu_a = g_a * sig_a
        b_conv = b_a * conv
        y_a = b_conv * silu_a
        fa_ref[...] = jnp.concatenate(
            [silu_a * conv, silu_a * b_a, b_conv * (sig_a + silu_a * (1.0 - sig_a))], axis=1).astype(BF16)

        pcat = jnp.concatenate([pc_ref[...], u_p], axis=0)
        pc_ref[...] = u_p[ROW_TILE - POOL_HALO:]
        counts = _window_counts(i * ROW_TILE, ROW_TILE)
        pooled, mixed = [], []
        for g, w in enumerate(POOL_WINDOWS):
            cols = slice(g * GROUP_D, (g + 1) * GROUP_D)
            s = pcat[:, cols]
            step = 1
            while step < w:
                s = s + _rows_from_before(s, step)
                step *= 2
            pooled_g = (s[POOL_HALO:] * (1.0 / jnp.minimum(counts, float(w))) - u_p[:, cols]).astype(BF16)
            pooled.append(pooled_g)
            mixed.append(_dot(pooled_g, wpool_ref[g]))
        mixed = jnp.concatenate(mixed, axis=1)
        sig_p = _sigmoid(g_p)
        silu_p = g_p * sig_p
        mixed_ps = mixed * ps
        y_p = mixed_ps * silu_p
        fp_ref[...] = jnp.concatenate(
            [(ps * silu_p).astype(BF16), (mixed_ps * (sig_p + silu_p * (1.0 - sig_p))).astype(BF16),
             (silu_p * mixed).astype(BF16)] + pooled, axis=1)

        ycat = jnp.concatenate([y_a, y_p], axis=1)
        ycat_ref[...] = ycat.astype(BF16)
        y_b = _dot(ycat.astype(BF16), wout_ref[...]).astype(BF16)
        y_ref[...] = y_b
        y_t = y_b.astype(F32)
        ry = lax.rsqrt(jnp.mean(y_t * y_t, axis=-1, keepdims=True) + NORM_EPS)
        x_next = x_t + gate * (y_t * ry * g_post)
        if head:
            err = x_next - target_ref[...]
            xo_ref[...] = err * (1.0 / D_MODEL)
            loss_ref[...] += jnp.sum(err * err) * (0.5 / D_MODEL)
        else:
            xo_ref[...] = x_next

    tile = (SUBLANES, LANES)
    return pl.pallas_call(
        body, name=f"forward_layer_{layer}", grid=(n_tiles,),
        in_specs=[row(D_MODEL), _layer_spec(vec.shape, layer), _layer_spec(cps.shape, layer),
                  _layer_spec(wpool.shape, layer), _whole_spec(win.shape), _whole_spec(wout.shape)]
        + [row(D_MODEL)] * head,
        out_specs=[row(D_MODEL), row(D_MODEL), row(D_MODEL), row(D_MODEL)] + [row(w) for w in widths[1:]]
        + [_whole_spec(tile)] * head,
        out_shape=[jax.ShapeDtypeStruct((t_len, D_MODEL), F32), jax.ShapeDtypeStruct((t_len, D_MODEL), BF16),
                   jax.ShapeDtypeStruct((t_len, D_MODEL), BF16), jax.ShapeDtypeStruct((t_len, D_MODEL), BF16)]
        + [jax.ShapeDtypeStruct((t_len, w), BF16) for w in widths[1:]] + [jax.ShapeDtypeStruct(tile, F32)] * head,
        scratch_shapes=[pltpu.VMEM((CONV_HALO, CONV_W), F32), pltpu.VMEM((POOL_HALO, POOL_W), F32)],
        compiler_params=_params(dimension_semantics=("arbitrary",)),
    )(x, vec, cps, wpool, win, wout, *([target] * head))


def _backward_layer(layer, dxo, y, x, uc, fa, fp, vec, cps, wpool, wout, win, after):
    t_len = dxo.shape[0]
    n_tiles = t_len // BWD_TILE
    halo_per_tile = BWD_TILE // POOL_HALO
    rev = lambda cols: pl.BlockSpec((BWD_TILE, cols), lambda i: (n_tiles - 1 - i, 0))
    halo_spec = pl.BlockSpec(
        (POOL_HALO, 2 * CONV_W), lambda i: (jnp.maximum((n_tiles - 1 - i) * halo_per_tile - 1, 0), 0))
    gwpool_shape = (len(POOL_WINDOWS), GROUP_D, GROUP_D)

    def body(dxo_ref, y_ref, x_ref, uc_ref, uch_ref, fa_ref, fp_ref, vec_ref, cps_ref, wpool_ref, wout_ref, win_ref,
             *rest):
        dx_ref, dproj_ref, dy_ref, gwpool_ref, dcps_ref, dvec_ref, gwpool_acc, dcc_ref, qc_ref = rest[len(after):]
        i = pl.program_id(0)
        tile = n_tiles - 1 - i

        @pl.when(i == 0)
        def _():
            gwpool_acc[...] = jnp.zeros_like(gwpool_acc)
            dcps_ref[...] = jnp.zeros_like(dcps_ref)
            dvec_ref[...] = jnp.zeros_like(dvec_ref)
            dcc_ref[...] = jnp.zeros_like(dcc_ref)
            qc_ref[...] = jnp.zeros_like(qc_ref)

        shift, scale, gate = vec_ref[0:1, :], vec_ref[1:2, :], vec_ref[2:3, :]
        g_pre, g_post = vec_ref[3:4, :], vec_ref[4:5, :]
        w0, w1, w2 = cps_ref[0:1, :], cps_ref[1:2, :], cps_ref[2:3, :]

        dxo_t = dxo_ref[...]
        y_t = y_ref[...].astype(F32)
        ry = lax.rsqrt(jnp.mean(y_t * y_t, axis=-1, keepdims=True) + NORM_EPS)
        yh = y_t * ry
        dvec_ref[2:3, :] += jnp.sum(dxo_t * yh, axis=0, keepdims=True)
        dyh = dxo_t * (gate * g_post)
        dy_b = (ry * (dyh - yh * jnp.mean(dyh * yh, axis=-1, keepdims=True))).astype(BF16)
        dy_ref[...] = dy_b
        dycat = _dot_nt(dy_b, wout_ref[...])
        dy_a, dy_p = dycat[:, :CONV_W], dycat[:, CONV_W:]

        fa_t = fa_ref[...].astype(F32)
        db_a = dy_a * fa_t[:, :CONV_W]
        dconv = dy_a * fa_t[:, CONV_W:2 * CONV_W]
        dg_a = dy_a * fa_t[:, 2 * CONV_W:]
        uc_t = uc_ref[...].astype(F32)
        u_a, c_a = uc_t[:, :CONV_W], uc_t[:, CONV_W:]
        halo = jnp.where(tile > 0, uch_ref[...].astype(F32), 0.0)[POOL_HALO - CONV_HALO:]
        z = c_a * u_a
        zcat = jnp.concatenate([halo[:, CONV_W:] * halo[:, :CONV_W], z], axis=0)
        z1 = _rows_from_before(zcat, 1)[CONV_HALO:]
        z2 = _rows_from_before(zcat, 2)[CONV_HALO:]
        dccat = jnp.concatenate([dconv, dcc_ref[...]], axis=0)
        dc1 = _rows_from_after(dccat, 1)[:BWD_TILE]
        dc2 = _rows_from_after(dccat, 2)[:BWD_TILE]
        dz = w2 * dconv + w1 * dc1 + w0 * dc2
        dcc_ref[...] = dconv[:CONV_HALO]
        dcps_ref[0:1, :] += jnp.sum(dconv * z2, axis=0, keepdims=True)
        dcps_ref[1:2, :] += jnp.sum(dconv * z1, axis=0, keepdims=True)
        dcps_ref[2:3, :] += jnp.sum(dconv * z, axis=0, keepdims=True)
        du_a = dz * c_a
        dc_a = dz * u_a

        dmixed = (dy_p * fp_ref[:, :POOL_W].astype(F32)).astype(BF16)
        dg_p = dy_p * fp_ref[:, POOL_W:2 * POOL_W].astype(F32)
        dcps_ref[3:4, :] += jnp.sum(dy_p * fp_ref[:, 2 * POOL_W:3 * POOL_W].astype(F32), axis=0, keepdims=True)
        counts = _window_counts(tile * BWD_TILE, BWD_TILE)
        du_p, q_head = [], []
        for g, w in enumerate(POOL_WINDOWS):
            cols = slice(g * GROUP_D, (g + 1) * GROUP_D)
            dm_g = dmixed[:, cols]
            dpooled_g = _dot_nt(dm_g, wpool_ref[g])
            gwpool_acc[g] += _dot_tn(fp_ref[:, 3 * POOL_W + g * GROUP_D:3 * POOL_W + (g + 1) * GROUP_D], dm_g)
            q_g = dpooled_g * (1.0 / jnp.minimum(counts, float(w)))
            q_head.append(q_g[:POOL_HALO])
            s = jnp.concatenate([q_g, qc_ref[:, cols]], axis=0)
            step = 1
            while step < w:
                s = s + _rows_from_after(s, step)
                step *= 2
            du_p.append(s[:BWD_TILE] - dpooled_g)
        qc_ref[...] = jnp.concatenate(q_head, axis=1)
        dproj_b = jnp.concatenate([du_a, db_a, dc_a, dg_a] + du_p + [dg_p], axis=1).astype(BF16)
        dproj_ref[...] = dproj_b

        x_t = x_ref[...]
        rx = lax.rsqrt(jnp.mean(x_t * x_t, axis=-1, keepdims=True) + NORM_EPS)
        xn = x_t * rx
        mod_scale = 1.0 + scale
        dh = _dot_nt(dproj_b, win_ref[...])
        dvec_ref[0:1, :] += jnp.sum(dh, axis=0, keepdims=True)
        dvec_ref[1:2, :] += jnp.sum(dh * xn, axis=0, keepdims=True)
        dxn = dh * (g_pre * mod_scale)
        dx_ref[...] = dxo_t + rx * (dxn - xn * jnp.mean(dxn * xn, axis=-1, keepdims=True))

        @pl.when(i == n_tiles - 1)
        def _():
            gwpool_ref[...] = gwpool_acc[...].astype(BF16)
            sum_dh_xn, sum_dxo_yh = dvec_ref[1:2, :], dvec_ref[2:3, :]
            dvec_ref[1:2, :] = sum_dh_xn * g_pre
            dvec_ref[3:4, :] = sum_dh_xn * mod_scale
            dvec_ref[2:3, :] = sum_dxo_yh * g_post
            dvec_ref[4:5, :] = sum_dxo_yh * gate

    return pl.pallas_call(
        body, name=f"backward_layer_{layer}", grid=(n_tiles,),
        in_specs=[rev(D_MODEL), rev(D_MODEL), rev(D_MODEL), rev(2 * CONV_W), halo_spec, rev(3 * CONV_W),
                  rev(4 * POOL_W), _layer_spec(vec.shape, layer), _layer_spec(cps.shape, layer),
                  _layer_spec(wpool.shape, layer), _whole_spec(wout.shape), _whole_spec(win.shape)]
        + [HBM] * len(after),
        out_specs=[rev(D_MODEL), rev(IN_COLS), rev(D_MODEL), _whole_spec(gwpool_shape),
                   _whole_spec((SUBLANES, CONV_W)), _whole_spec((SUBLANES, D_MODEL))],
        out_shape=[jax.ShapeDtypeStruct((t_len, D_MODEL), F32), jax.ShapeDtypeStruct((t_len, IN_COLS), BF16),
                   jax.ShapeDtypeStruct((t_len, D_MODEL), BF16), jax.ShapeDtypeStruct(gwpool_shape, BF16),
                   jax.ShapeDtypeStruct((SUBLANES, CONV_W), F32), jax.ShapeDtypeStruct((SUBLANES, D_MODEL), F32)],
        scratch_shapes=[pltpu.VMEM(gwpool_shape, F32), pltpu.VMEM((CONV_HALO, CONV_W), F32),
                        pltpu.VMEM((POOL_HALO, POOL_W), F32)],
        compiler_params=_params(dimension_semantics=("arbitrary",)),
    )(dxo, y, x, uc, uc, fa, fp, vec, cps, wpool, wout, win, *after)


def _weight_grads(layer, h, dproj, ycat, dy, exchange, after=()):
    t_len = dy.shape[0]
    n_in, n_out = IN_COLS // GWIN_COLS, D_MODEL // GWOUT_COLS
    args = [h, dproj, ycat, dy, *after]
    in_specs = [_whole_spec(h.shape),
                pl.BlockSpec((t_len, GWIN_COLS), lambda s: (0, jnp.minimum(s, n_in - 1))),
                _whole_spec(ycat.shape),
                pl.BlockSpec((t_len, GWOUT_COLS), lambda s: (0, jnp.maximum(s - n_in, 0)))] + [HBM] * len(after)
    out_shape = [jax.ShapeDtypeStruct((D_MODEL, IN_COLS), BF16), jax.ShapeDtypeStruct((D_MODEL, D_MODEL), BF16)]
    out_specs = [pl.BlockSpec((D_MODEL, GWIN_COLS), lambda s: (0, jnp.minimum(s, n_in - 1))),
                 pl.BlockSpec((D_MODEL, GWOUT_COLS), lambda s: (0, jnp.maximum(s - n_in, 0)))]
    scratch = []
    aliases, split = _host(exchange, args, in_specs, out_shape, out_specs, scratch)

    def body(*refs):
        (h_ref, dproj_ref, ycat_ref, dy_ref, *_, gwin_ref, gwout_ref), hosted = split(refs)
        s = pl.program_id(0)
        if hosted is not None:
            pl.when(s == 0)(hosted[0])

        @pl.when(s < n_in)
        def _():
            gwin_ref[...] = _dot_tn(h_ref[...], dproj_ref[...]).astype(BF16)

        @pl.when(s >= n_in)
        def _():
            gwout_ref[...] = _dot_tn(ycat_ref[...], dy_ref[...]).astype(BF16)

        if hosted is not None:
            pl.when(s == n_in + n_out - 1)(hosted[1])

    return pl.pallas_call(
        body, name=f"weight_grads_{layer}", grid=(n_in + n_out,), in_specs=in_specs, out_specs=out_specs,
        out_shape=out_shape, scratch_shapes=scratch, input_output_aliases=aliases,
        compiler_params=_params(dimension_semantics=("arbitrary",)),
    )(*args)


def _add_sibling_blocks(name, layer, grads, received, core, partials, kinds):
    n_arr = len(grads)

    def body(core_ref, *refs):
        mine, theirs, outs = refs[:n_arr], refs[n_arr:2 * n_arr], refs[-n_arr:]
        for a in range(n_arr):
            outs[a][...] = (mine[a][...].astype(F32) + theirs[a][...].astype(F32)).astype(BF16)

    own_of_kind = [
        pl.BlockSpec((D_MODEL, W_IN_SHARD), lambda q, core_ref: (0, 2 * q + core_ref[0])),
        pl.BlockSpec((W_OUT_SHARD, D_MODEL), lambda q, core_ref: (2 * q + core_ref[0], 0)),
        pl.BlockSpec((POOL_SHARD, GROUP_D), lambda q, core_ref: (2 * q + core_ref[0], 0)),
    ]
    shapes = [_BLOCK_SHAPES[k] for k in kinds]
    recv_specs = [pl.BlockSpec((None,) + s, lambda q, core_ref: (q, 0, 0)) for s in shapes]
    out_specs = [pl.BlockSpec((None, None) + s, lambda q, core_ref: (q, layer, 0, 0)) for s in shapes]
    args = [core, *grads, *received]
    in_specs = [own_of_kind[k] for k in kinds] + recv_specs
    aliases = {}
    if partials is not None:
        aliases = {len(args) + a: a for a in range(n_arr)}
        args += list(partials)
        in_specs += [HBM] * n_arr
    return pl.pallas_call(
        body, name=name,
        grid_spec=pltpu.PrefetchScalarGridSpec(
            num_scalar_prefetch=1, grid=(N_CHIP,), in_specs=in_specs, out_specs=out_specs),
        out_shape=[jax.ShapeDtypeStruct((N_CHIP, DEPTH) + s, BF16) for s in shapes],
        input_output_aliases=aliases,
        compiler_params=_params(dimension_semantics=("arbitrary",)),
    )(*args)


def _modulation_columns(c_all, w_ada):
    def body(c_ref, w_ref, cact_ref, out_ref):
        c_t = c_ref[...]
        c_act = c_t * _sigmoid(c_t)
        cact_ref[...] = c_act
        out_ref[...] = jnp.dot(c_act, w_ref[...], preferred_element_type=F32, precision=lax.Precision.HIGHEST)

    return pl.pallas_call(
        body, name="modulation_columns", grid=(DEPTH,),
        in_specs=[pl.BlockSpec((N_DEV, D_MODEL), lambda l: (0, 0)),
                  pl.BlockSpec((None, D_MODEL, W_IN_SHARD), lambda l: (l, 0, 0))],
        out_specs=[pl.BlockSpec((N_DEV, D_MODEL), lambda l: (0, 0)),
                   pl.BlockSpec((N_DEV, W_IN_SHARD), lambda l: (0, l))],
        out_shape=[jax.ShapeDtypeStruct((N_DEV, D_MODEL), F32),
                   jax.ShapeDtypeStruct((N_DEV, DEPTH * W_IN_SHARD), F32)],
        compiler_params=_params(dimension_semantics=("arbitrary",)),
    )(c_all, w_ada)


def _adamw(w, g, m, v):
    m_new = ADAM_B1 * m + (1.0 - ADAM_B1) * g
    v_new = ADAM_B2 * v + (1.0 - ADAM_B2) * (g * g)
    m_hat = m_new / (1.0 - ADAM_B1 ** ADAM_STEP)
    v_hat = v_new / (1.0 - ADAM_B2 ** ADAM_STEP)
    delta = -ADAM_LR * (m_hat / (jnp.sqrt(v_hat) + ADAM_EPS) + ADAM_WD * w)
    return delta, m_new, v_new


def _adamw_w_ada(w, m, v, c_act_t, dmod_cols):
    def body(w_ref, m_ref, v_ref, ct_ref, dm_ref, g_ref, d_ref, mo_ref, vo_ref):
        g = ct_ref[:, 0:1] * dm_ref[0:1, :]
        for b in range(1, N_DEV):
            g = g + ct_ref[:, b:b + 1] * dm_ref[b:b + 1, :]
        g_ref[...] = g
        d_ref[...], mo_ref[...], vo_ref[...] = _adamw(w_ref[...], g, m_ref[...], v_ref[...])

    big = pl.BlockSpec((None, ROW_TILE, W_IN_SHARD), lambda l, r: (l, r, 0))
    return pl.pallas_call(
        body, name="adamw_w_ada", grid=(DEPTH, D_MODEL // ROW_TILE),
        in_specs=[big, big, big, pl.BlockSpec((ROW_TILE, N_DEV), lambda l, r: (r, 0)),
                  pl.BlockSpec((None, N_DEV, W_IN_SHARD), lambda l, r: (l, 0, 0))],
        out_specs=[big] * 4, out_shape=[jax.ShapeDtypeStruct(w.shape, F32)] * 4,
        compiler_params=_params(dimension_semantics=("arbitrary", "arbitrary")),
    )(w, m, v, c_act_t, dmod_cols)


def _sum_chip_partials(own_ref, recv_ref):
    g = own_ref[...].astype(F32)
    for j in range(N_OTHER_CHIPS):
        g = g + recv_ref[j].astype(F32)
    return g


def _partial_specs(row_tile, cols, first_layer=0):
    own = pl.BlockSpec((None, None, row_tile, cols), lambda l, r, chip_ref: (chip_ref[0], first_layer + l, r, 0))
    recv = pl.BlockSpec((N_OTHER_CHIPS, None, row_tile, cols), lambda l, r, chip_ref: (0, first_layer + l, r, 0))
    return own, recv


def _adamw_reduced(name, w, m, v, partial, received, chip, row_tile, layers, continued, after=()):
    depth, rows, cols = w.shape
    first, stop = layers

    def body(chip_ref, w_ref, m_ref, v_ref, own_ref, recv_ref, *rest):
        g_ref, d_ref, mo_ref, vo_ref = rest[-4:]
        g = _sum_chip_partials(own_ref, recv_ref)
        g_ref[...] = g
        d_ref[...], mo_ref[...], vo_ref[...] = _adamw(w_ref[...], g, m_ref[...], v_ref[...])

    blk = pl.BlockSpec((None, row_tile, cols), lambda l, r, chip_ref: (first + l, r, 0))
    args = [chip, w, m, v, partial, received]
    in_specs = [blk, blk, blk, *_partial_specs(row_tile, cols, first)]
    aliases = {}
    if continued is not None:
        aliases = {len(args) + k: k for k in range(4)}
        args += list(continued)
        in_specs += [HBM] * 4
    args += after
    in_specs += [HBM] * len(after)
    return pl.pallas_call(
        body, name=name,
        grid_spec=pltpu.PrefetchScalarGridSpec(
            num_scalar_prefetch=1, grid=(stop - first, rows // row_tile), in_specs=in_specs, out_specs=[blk] * 4),
        out_shape=[jax.ShapeDtypeStruct(w.shape, F32)] * 4, input_output_aliases=aliases,
        compiler_params=_params(dimension_semantics=("arbitrary", "arbitrary")),
    )(*args)


def _reduce_w_pool(partial, received, chip):
    def body(chip_ref, own_ref, recv_ref, g_ref):
        g_ref[...] = _sum_chip_partials(own_ref, recv_ref)

    return pl.pallas_call(
        body, name="reduce_w_pool",
        grid_spec=pltpu.PrefetchScalarGridSpec(
            num_scalar_prefetch=1, grid=(DEPTH, 1), in_specs=list(_partial_specs(POOL_SHARD, GROUP_D)),
            out_specs=pl.BlockSpec((None, POOL_SHARD, GROUP_D), lambda l, r, chip_ref: (l, 0, 0))),
        out_shape=jax.ShapeDtypeStruct((DEPTH, POOL_SHARD, GROUP_D), F32),
        compiler_params=_params(dimension_semantics=("arbitrary", "arbitrary")),
    )(chip, partial, received)


def _adamw_small(name, params):
    n = len(params)

    def body(*refs):
        ins, outs = refs[:4 * n], refs[4 * n:]
        for p in range(n):
            w_ref, g_ref, m_ref, v_ref = ins[4 * p:4 * p + 4]
            d_ref, mo_ref, vo_ref, go_ref = outs[4 * p:4 * p + 4]
            d_ref[...], mo_ref[...], vo_ref[...] = _adamw(w_ref[...], g_ref[...], m_ref[...], v_ref[...])
            go_ref[...] = g_ref[...]

    vmem = pl.BlockSpec(memory_space=pltpu.VMEM)
    flat = [a for group in params for a in group]
    outs = pl.pallas_call(
        body, name=name, in_specs=[vmem] * len(flat), out_specs=[vmem] * len(flat),
        out_shape=[jax.ShapeDtypeStruct(a.shape, F32) for a in flat], compiler_params=_params(),
    )(*flat)
    return [tuple(outs[4 * p:4 * p + 4]) for p in range(n)]


def _sum_sources(slabs, columns, after):
    def body(s_ref, *rest):
        outs = rest[len(after):]
        acc = s_ref[0]
        for b in range(1, N_DEV):
            acc = acc + s_ref[b]
        for (start, stop), o_ref in zip(columns, outs):
            o_ref[...] = acc[:, start:stop]
        outs[-1][...] = acc[0:1, SLAB_COLS:SLAB_COLS + 1]

    vmem = pl.BlockSpec(memory_space=pltpu.VMEM)
    out_shape = [jax.ShapeDtypeStruct((slabs.shape[1], stop - start), F32) for start, stop in columns]
    out_shape.append(jax.ShapeDtypeStruct((1, 1), F32))
    return pl.pallas_call(
        body, name="sum_small_grads", in_specs=[vmem] + [HBM] * len(after), out_specs=[vmem] * len(out_shape),
        out_shape=out_shape, compiler_params=_params(),
    )(slabs, *after)


def _to_bf16(a, name, layers=None):
    first, stop = layers or (0, a.shape[0])

    def body(a_ref, o_ref):
        o_ref[...] = a_ref[...].astype(BF16)

    block = (None,) + a.shape[1:]
    return pl.pallas_call(
        body, name=name, grid=(stop - first,), in_specs=[pl.BlockSpec(block, lambda l: (first + l, 0, 0))],
        out_specs=pl.BlockSpec(block, lambda l: (l, 0, 0)),
        out_shape=jax.ShapeDtypeStruct((stop - first,) + a.shape[1:], BF16),
        compiler_params=_params(dimension_semantics=("arbitrary",)),
    )(a)


def kernel(x, c, w_ada, b_ada, g_pre, w_in, w_conv, w_pool, pool_scale, w_out, g_post, loss_target, m_w_ada, m_b_ada, m_g_pre, m_w_in, m_w_conv, m_w_pool, m_pool_scale, m_w_out, m_g_post, v_w_ada, v_b_ada, v_g_pre, v_w_in, v_w_conv, v_w_pool, v_pool_scale, v_w_out, v_g_post):
    mx, my, mc = _mesh_position()
    me = _block_id(mx, my, mc)
    chip = (2 * mx + my).astype(jnp.int32).reshape(1)
    core = mc.astype(jnp.int32).reshape(1)
    x0 = x[0]
    target = loss_target[0]
    conv_shard = w_conv.shape[-1]

    own_small = jnp.concatenate([c, w_conv.reshape(1, DEPTH * 3 * conv_shard)], axis=1)
    first_shards = [_to_bf16(w_in, "cast_w_in_0", (0, 1)), _to_bf16(w_out, "cast_w_out_0", (0, 1))]
    all_small = _all_gather_small(own_small, "all_gather_c_w_conv", first_shards)[:, 0, :]
    gathers = [_gather_weights_start(0, *first_shards, [all_small], relayed=True)]
    c_all = all_small[:, :D_MODEL]
    w_conv_full = all_small[:, D_MODEL:].reshape(N_DEV, DEPTH, 3, conv_shard).transpose(1, 2, 0, 3).reshape(
        DEPTH, 3, CONV_W)
    cps = jnp.concatenate([w_conv_full, pool_scale[:, None], jnp.zeros((DEPTH, 4, CONV_W), F32)], axis=1)

    c_act, pieces = _modulation_columns(c_all, w_ada)
    upper = (1, DEPTH)
    upper_shards = [_to_bf16(w_in, "cast_w_in_1", upper), _to_bf16(w_out, "cast_w_out_1", upper)]
    wpool_b = _to_bf16(w_pool.reshape(DEPTH, POOL_ROWS, GROUP_D), "cast_w_pool").reshape(w_pool.shape)
    first_sems_0, _, (zones_0,) = gathers[0]
    neighbour_sems, zones_0 = _gather_weights_pass_on(
        "all_gather_weights_relay_0", NEIGHBOUR_CHIPS, first_sems_0[1], partial(_first_arrival, 0), zones_0,
        [pieces, *upper_shards, wpool_b, cps], relay=True)
    gather_mod = _all_gather_exchange(pieces)
    sems_m, pieces_thru, mod_landing, token_m = _start_exchange(gather_mod, "all_gather_modulation_start", [zones_0[0]])
    diagonal_sems, zones_0 = _gather_weights_pass_on(
        "all_gather_weights_pass_on_0", DIAGONAL_CHIP, neighbour_sems[3], lambda a, j: a, zones_0, [token_m])
    _, (mod_all,) = _finish_exchange(
        gather_mod, "all_gather_modulation_finish", sems_m, pieces_thru, mod_landing, [zones_0[0]])
    mod_mine = lax.dynamic_index_in_dim(mod_all, me, axis=1, keepdims=False)
    mod = mod_mine.reshape(N_DEV, DEPTH, W_IN_SHARD).transpose(1, 0, 2).reshape(DEPTH, 3 * D_MODEL) + b_ada
    zeros_d = jnp.zeros((DEPTH, 3, D_MODEL), F32)
    vec = jnp.concatenate([mod.reshape(DEPTH, 3, D_MODEL), g_pre[:, None], g_post[:, None], zeros_d], axis=1)

    gathers.append(_gather_weights_start(1, *upper_shards, [mod_all]))
    gathers = [(first_sems, shards, landing[k], k) for first_sems, shards, landing in gathers
               for k in range(len(landing))]

    xs, kept, wins, wouts = [x0], [], [], []
    for l in range(DEPTH):
        first_sems, shards, zones, index = gathers[l]
        if l == 0:
            passed = [(neighbour_sems[:2], NEIGHBOUR_CHIPS), (diagonal_sems, DIAGONAL_CHIP)]
            win, wout = _gather_weights_finish(
                l, index, first_sems, passed, shards, zones_0, neighbour_sems[2], [gathers[-1][1][0]])
        else:
            passed_sems, zones = _gather_weights_pass_on(
                f"all_gather_weights_pass_on_{l}", ALL_OTHER_CHIPS, first_sems[1], partial(_first_arrival, index),
                zones, [xs[-1]])
            win, wout = _gather_weights_finish(l, index, first_sems, [(passed_sems, ALL_OTHER_CHIPS)], shards, zones)
        x_next, *for_backward = _forward_layer(
            l, xs[-1], vec, cps, wpool_b, win, wout, target if l == DEPTH - 1 else None)
        xs.append(x_next)
        kept.append(for_backward[:6])
        wins.append(win)
        wouts.append(wout)
    dx, loss_tile = xs[DEPTH], for_backward[6]

    slab_rows = [None] * DEPTH
    partials = received = None
    in_flight = []

    def scatter(layer, grads, from_sibling, after):
        nonlocal partials, received
        partials = _add_sibling_blocks(
            f"grad_add_sibling_{layer}", layer, grads, from_sibling, core, partials, ALL_KINDS)
        chips = _chips_exchange(layer, partials, received, ALL_KINDS)
        sems, partials, received, token = _start_exchange(chips, f"grad_chips_start_{layer}", after)
        in_flight.append((chips, sems, layer))
        return token

    grads_above = None
    issued = []
    for l in reversed(range(DEPTH)):
        y, h, ycat, uc, fa, fp = kept[l]
        dx, dproj, dy, gwpool, dcps, dvec = _backward_layer(
            l, dx, y, xs[l], uc, fa, fp, vec, cps, wpool_b, wouts[l], wins[l], issued)
        slab_rows[l] = jnp.concatenate(
            [dvec[0], dvec[1], dvec[2], dvec[3], dvec[4], dcps[3], dcps[0], dcps[1], dcps[2],
             loss_tile[0] if l == 0 else jnp.zeros((LANES,), F32)])
        after = []
        if l == 0:
            gather_small = _all_gather_exchange(jnp.stack(slab_rows))
            sems_s, slab, slabs, token = _start_exchange(gather_small, "all_gather_small_grads_start")
            after = [token]
        hosted = _sibling_exchange(grads_above, ALL_KINDS) if grads_above is not None else None
        gwin, gwout, *from_sibling = _weight_grads(l, h, dproj, ycat, dy, hosted, after)
        if l == 0:
            _, (slabs,) = _finish_exchange(
                gather_small, "all_gather_small_grads_finish", sems_s, slab, slabs, [gwin])
        issued = [gwin]
        if grads_above is not None:
            issued = [scatter(l + 1, grads_above, from_sibling, [])]
        grads_above = [gwin, gwout, gwpool.reshape(POOL_ROWS, GROUP_D)]
        if l <= 1:
            from_sibling = _run_exchange(_sibling_exchange(grads_above, ALL_KINDS), f"grad_exchange_sibling_{l}")
            token = scatter(l, grads_above, from_sibling, [slabs] if l == 0 else [])
            issued = [token]
            grads_above = None
    grad_x = dx[None]
    chips_0, sems_0, _ = in_flight.pop()

    o = 3 * D_MODEL

    after = [token]
    for chips, sems, l in in_flight:
        partials, received = _finish_exchange(chips, f"grad_chips_finish_{l}", sems, partials, received, after)
        after = []
    upper = (1, DEPTH)
    w_in_upper = _adamw_reduced(
        "adamw_w_in_upper", w_in, m_w_in, v_w_in, partials[0], received[0], chip, ROW_TILE, upper, None)
    w_out_upper = _adamw_reduced(
        "adamw_w_out_upper", w_out, m_w_out, v_w_out, partials[1], received[1], chip, W_OUT_SHARD, upper, None)
    dmod_all = slabs[:, :, :o].reshape(N_DEV, DEPTH, N_DEV, W_IN_SHARD)
    dmod_cols = lax.dynamic_index_in_dim(dmod_all, me, axis=2, keepdims=False).transpose(1, 0, 2) + token[0, 0]
    g_w_ada, d_w_ada, nm_w_ada, nv_w_ada = _adamw_w_ada(w_ada, m_w_ada, v_w_ada, c_act.T, dmod_cols)

    partials, received = _finish_exchange(
        chips_0, "grad_chips_finish_0", sems_0, partials, received, [nv_w_ada, w_in_upper[3], w_out_upper[3]])
    gather_pool = _all_gather_exchange(_reduce_w_pool(partials[2], received[2], chip), axis=1)
    sems_p, pool_rows, pool_landing, token_p = _start_exchange(gather_pool, "all_gather_grad_w_pool_start")
    g_w_in, d_w_in, nm_w_in, nv_w_in = _adamw_reduced(
        "adamw_w_in_0", w_in, m_w_in, v_w_in, partials[0], received[0], chip, ROW_TILE, (0, 1), w_in_upper, [token_p])
    g_w_out, d_w_out, nm_w_out, nv_w_out = _adamw_reduced(
        "adamw_w_out_0", w_out, m_w_out, v_w_out, partials[1], received[1], chip, W_OUT_SHARD, (0, 1), w_out_upper,
        [token_p])
    ends = [0, o, o + D_MODEL, o + 2 * D_MODEL, o + 2 * D_MODEL + POOL_W, SLAB_COLS]
    g_b_ada, g_g_pre, g_g_post, g_pool_scale, g_conv, loss = _sum_sources(
        slabs, list(zip(ends[:-1], ends[1:])), [token_p, nv_w_in, nv_w_out])
    loss = loss.reshape(())
    g_w_conv = lax.dynamic_slice_in_dim(g_conv.reshape(DEPTH, 3, CONV_W), me * conv_shard, conv_shard, axis=2)
    taps_first = lambda a: a.transpose(1, 0, 2)
    small = _adamw_small("adamw_small", [
        (b_ada, g_b_ada, m_b_ada, v_b_ada),
        (g_pre, g_g_pre, m_g_pre, v_g_pre),
        tuple(taps_first(a) for a in (w_conv, g_w_conv, m_w_conv, v_w_conv)),
        (pool_scale, g_pool_scale, m_pool_scale, v_pool_scale),
        (g_post, g_g_post, m_g_post, v_g_post),
    ])
    (d_b_ada, nm_b_ada, nv_b_ada, g_b_ada), (d_g_pre, nm_g_pre, nv_g_pre, g_g_pre), conv_steps, \
        (d_ps, nm_ps, nv_ps, g_pool_scale), (d_g_post, nm_g_post, nv_g_post, g_g_post) = small
    d_w_conv, nm_w_conv, nv_w_conv, g_w_conv = (taps_first(a) for a in conv_steps)
    _, (g_pool_all,) = _finish_exchange(
        gather_pool, "all_gather_grad_w_pool_finish", sems_p, pool_rows, pool_landing, [nv_w_in, nv_w_out, nv_g_post])
    ((d_w_pool, nm_w_pool, nv_w_pool, g_w_pool),) = _adamw_small(
        "adamw_w_pool", [(w_pool, g_pool_all.reshape(w_pool.shape), m_w_pool, v_w_pool)])

    return (loss, grad_x,
            g_w_ada, g_b_ada, g_g_pre, g_w_in, g_w_conv, g_w_pool, g_pool_scale, g_w_out, g_g_post,
            d_w_ada, d_b_ada, d_g_pre, d_w_in, d_w_conv, d_w_pool, d_ps, d_w_out, d_g_post,
            nm_w_ada, nm_b_ada, nm_g_pre, nm_w_in, nm_w_conv, nm_w_pool, nm_ps, nm_w_out, nm_g_post,
            nv_w_ada, nv_b_ada, nv_g_pre, nv_w_in, nv_w_conv, nv_w_pool, nv_ps, nv_w_out, nv_g_post)
```
